```python
import jax, jax.numpy as jnp
from jax import lax
import numpy as np

D_MODEL = 2048
BATCH = 8
SEQ = 4096
DEPTH = 1

HEAD_DIM = 128
N_DELTA_HEADS = D_MODEL // (2 * HEAD_DIM)
N_ATTN_HEADS = D_MODEL // (2 * HEAD_DIM)
DELTA_WIDTH = N_DELTA_HEADS * HEAD_DIM
ATTN_WIDTH = N_ATTN_HEADS * HEAD_DIM
MIX_WIDTH = DELTA_WIDTH + ATTN_WIDTH
CONV_WIDTH = 4
CHUNK = 64
DILATED_PATTERNS = ((128, 1), (512, 4), (2048, 16))
ROPE_THETA = 10000.0
D_FF = ((8 * D_MODEL // 3 + 255) // 256) * 256
EPS = 1e-6

PROJ_SIZES = (DELTA_WIDTH, DELTA_WIDTH, DELTA_WIDTH, DELTA_WIDTH, N_DELTA_HEADS, N_DELTA_HEADS,
              ATTN_WIDTH, ATTN_WIDTH, ATTN_WIDTH)
PROJ_WIDTH = int(sum(PROJ_SIZES))
SPLIT_POINTS = tuple(int(s) for s in np.cumsum(PROJ_SIZES)[:-1])

kernel_name = "hybrid_deltanet_dilated_swa_layer"


def rms_norm(x, w):
    xf = x.astype(jnp.float32)
    y = xf * lax.rsqrt(jnp.mean(xf * xf, axis=-1, keepdims=True) + EPS)
    return (y * w.astype(jnp.float32)).astype(x.dtype)


def l2_norm(x):
    xf = x.astype(jnp.float32)
    return xf * lax.rsqrt(jnp.sum(xf * xf, axis=-1, keepdims=True) + EPS)


def rope(x, positions):
    half = x.shape[-1] // 2
    inv_freq = ROPE_THETA ** (-jnp.arange(half, dtype=jnp.float32) / half)
    ang = positions.astype(jnp.float32)[..., None] * inv_freq
    cos = jnp.cos(ang)[:, :, None, :]
    sin = jnp.sin(ang)[:, :, None, :]
    xf = x.astype(jnp.float32)
    x1, x2 = xf[..., :half], xf[..., half:]
    return jnp.concatenate([x1 * cos - x2 * sin, x2 * cos + x1 * sin], axis=-1).astype(x.dtype)


def causal_depthwise_conv_silu(x, w):
    k_len, chans = w.shape
    y = lax.conv_general_dilated(x, w[:, None, :].astype(x.dtype), window_strides=(1,),
                                 padding=[(k_len - 1, 0)],
                                 dimension_numbers=('NWC', 'WIO', 'NWC'),
                                 feature_group_count=chans)
    return jax.nn.silu(y)


def gated_delta_rule_chunked(q, k, v, g, beta):
    bsz, seq, heads, dk = q.shape
    dv = v.shape[-1]
    n = seq // CHUNK
    def to_chunks(t):
        return t.reshape(bsz, n, CHUNK, heads, t.shape[-1]).transpose(0, 3, 1, 2, 4)
    qc, kc, vc = to_chunks(q), to_chunks(k), to_chunks(v)
    gc = jnp.cumsum(g.reshape(bsz, n, CHUNK, heads).transpose(0, 3, 1, 2), axis=-1)
    bc = beta.reshape(bsz, n, CHUNK, heads).transpose(0, 3, 1, 2)
    idx = jnp.arange(CHUNK)
    causal = idx[:, None] >= idx[None, :]
    strict = idx[:, None] > idx[None, :]
    diff = gc[..., :, None] - gc[..., None, :]
    decay = jnp.where(causal, jnp.exp(jnp.where(causal, diff, 0.0)), 0.0)
    kk = jnp.einsum('bhncd,bhnmd->bhncm', kc, kc)
    lower = jnp.where(strict, bc[..., :, None] * kk * decay, 0.0)
    a_mat = lower + jnp.eye(CHUNK, dtype=jnp.float32)
    rhs = jnp.concatenate([vc * bc[..., None], kc * (bc * jnp.exp(gc))[..., None]], axis=-1)
    sol = lax.linalg.triangular_solve(a_mat, rhs, left_side=True, lower=True, unit_diagonal=True)
    u, w = sol[..., :dv], sol[..., dv:]
    qk = jnp.einsum('bhncd,bhnmd->bhncm', qc, kc) * decay
    q_dec = qc * jnp.exp(gc)[..., None]
    k_dec = kc * jnp.exp(gc[..., -1:] - gc)[..., None]
    chunk_decay = jnp.exp(gc[..., -1])

    def step(state, xs):
        u_c, w_c, qk_c, qd_c, kd_c, dec_c = xs
        v_new = u_c - jnp.einsum('bhcd,bhde->bhce', w_c, state)
        o_c = jnp.einsum('bhcd,bhde->bhce', qd_c, state) + jnp.einsum('bhcm,bhme->bhce', qk_c, v_new)
        state = state * dec_c[..., None, None] + jnp.einsum('bhcd,bhce->bhde', kd_c, v_new)
        return state, o_c

    xs = tuple(jnp.moveaxis(t, 2, 0) for t in (u, w, qk, q_dec, k_dec, chunk_decay))
    state0 = jnp.zeros((bsz, heads, dk, dv), jnp.float32)
    _, o = lax.scan(step, state0, xs)
    return o.transpose(1, 0, 3, 2, 4).reshape(bsz, seq, heads, dv)


def dilated_window_branch(q, k, v, window, dilation):
    bsz, seq, heads, hd = q.shape
    span = window // dilation
    blk = span
    unit = dilation * blk
    s_pad = -(-seq // unit) * unit
    nb = s_pad // unit
    def to_blocks(t):
        t = jnp.pad(t, [(0, 0), (0, s_pad - seq), (0, 0), (0, 0)])
        return t.reshape(bsz, nb, blk, dilation, heads, hd)
    def with_prev(t):
        prev = jnp.pad(t, [(0, 0), (1, 0), (0, 0), (0, 0), (0, 0), (0, 0)])[:, :-1]
        return jnp.concatenate([prev, t], axis=2)
    qb = to_blocks(q)
    kw = with_prev(to_blocks(k))
    vw = with_prev(to_blocks(v))
    s = jnp.einsum('bnqrhd,bnkrhd->bnrhqk', qb, kw,
                   preferred_element_type=jnp.float32) * (hd ** -0.5)
    qi = jnp.arange(blk)[:, None]
    ki = jnp.arange(2 * blk)[None, :]
    dist = qi + blk - ki
    band = (dist >= 0) & (dist <= span)
    mask = band[None] & ((jnp.arange(nb)[:, None, None] > 0) | (ki[None] >= blk))
    s = jnp.where(mask[None, :, None, None], s, -jnp.inf)
    m = jnp.max(s, axis=-1)
    p = jnp.exp(s - m[..., None])
    den = jnp.sum(p, axis=-1)
    o = jnp.einsum('bnrhqk,bnkrhd->bnqrhd', p, vw.astype(jnp.float32))
    m = m.transpose(0, 1, 4, 2, 3)
    den = den.transpose(0, 1, 4, 2, 3)
    o = o / den[..., None]
    o = o.reshape(bsz, s_pad, heads, hd)[:, :seq]
    m = m.reshape(bsz, s_pad, heads)[:, :seq]
    den = den.reshape(bsz, s_pad, heads)[:, :seq]
    return o, m, den


def dilated_mixture_attention(q, k, v):
    branches = [dilated_window_branch(q, k, v, w, d) for (w, d) in DILATED_PATTERNS]
    m_all = branches[0][1]
    for _, m_i, _ in branches[1:]:
        m_all = jnp.maximum(m_all, m_i)
    num = 0.0
    tot = 0.0
    for o_i, m_i, den_i in branches:
        wt = den_i * jnp.exp(m_i - m_all)
        num = num + wt[..., None] * o_i
        tot = tot + wt
    return num / tot[..., None]


def _fwd_setup_inputs(seed: int = 0) -> dict:
    key = jax.random.key(seed)
    ks = jax.random.split(key, 20)
    f32 = jnp.float32
    x = jax.random.normal(ks[0], (BATCH, SEQ, D_MODEL), f32)
    offset = jax.random.randint(ks[1], (BATCH, 1), 0, 1024, dtype=jnp.int32)
    positions = (offset + jnp.arange(SEQ, dtype=jnp.int32)[None, :]).astype(jnp.int32)
    def gain(k, n):
        return 1.0 + 0.02 * jax.random.normal(k, (DEPTH, n), f32)
    attn_norm_w = gain(ks[2], D_MODEL)
    w_in = jax.random.normal(ks[3], (DEPTH, D_MODEL, PROJ_WIDTH), f32) * D_MODEL ** -0.5
    conv_w = jax.random.normal(ks[4], (DEPTH, CONV_WIDTH, 3 * DELTA_WIDTH), f32) * CONV_WIDTH ** -0.5
    a_log = jnp.log(jax.random.uniform(ks[5], (DEPTH, N_DELTA_HEADS), f32, 1.0, 16.0))
    dt = jnp.exp(jax.random.uniform(ks[6], (DEPTH, N_DELTA_HEADS), f32, np.log(1e-3), np.log(1e-1)))
    dt_bias = dt + jnp.log(-jnp.expm1(-dt))
    delta_out_norm_w = gain(ks[7], HEAD_DIM)
    q_norm_w = gain(ks[8], HEAD_DIM)
    k_norm_w = gain(ks[9], HEAD_DIM)
    attn_out_norm_w = gain(ks[10], HEAD_DIM)
    w_out = jax.random.normal(ks[11], (DEPTH, MIX_WIDTH, D_MODEL), f32) * MIX_WIDTH ** -0.5
    ffn_norm_w = gain(ks[12], D_MODEL)
    w_gate_up = jax.random.normal(ks[13], (DEPTH, D_MODEL, 2 * D_FF), f32) * D_MODEL ** -0.5
    w_down = jax.random.normal(ks[14], (DEPTH, D_FF, D_MODEL), f32) * D_FF ** -0.5
    return {'x': x, 'positions': positions, 'attn_norm_w': attn_norm_w, 'w_in': w_in,
            'conv_w': conv_w, 'a_log': a_log, 'dt_bias': dt_bias,
            'delta_out_norm_w': delta_out_norm_w, 'q_norm_w': q_norm_w, 'k_norm_w': k_norm_w,
            'attn_out_norm_w': attn_out_norm_w, 'w_out': w_out, 'ffn_norm_w': ffn_norm_w,
            'w_gate_up': w_gate_up, 'w_down': w_down}


def _fwd_reference(x, positions, attn_norm_w, w_in, conv_w, a_log, dt_bias, delta_out_norm_w,
              q_norm_w, k_norm_w, attn_out_norm_w, w_out, ffn_norm_w, w_gate_up, w_down):
    bsz, seq, _ = x.shape
    for l in range(DEPTH):
        h = rms_norm(x, attn_norm_w[l])
        proj = h @ w_in[l].astype(h.dtype)
        qkv_raw, dz, db, da, aq, ak, av = jnp.split(
            proj, [3 * DELTA_WIDTH] + list(SPLIT_POINTS[3:]), axis=-1)
        qkv = causal_depthwise_conv_silu(qkv_raw, conv_w[l])
        dq, dk, dv = jnp.split(qkv, 3, axis=-1)
        dq = l2_norm(dq.reshape(bsz, seq, N_DELTA_HEADS, HEAD_DIM)) * (HEAD_DIM ** -0.5)
        dk = l2_norm(dk.reshape(bsz, seq, N_DELTA_HEADS, HEAD_DIM))
        dv = dv.reshape(bsz, seq, N_DELTA_HEADS, HEAD_DIM).astype(jnp.float32)
        beta = jax.nn.sigmoid(db.astype(jnp.float32))
        g = -jnp.exp(a_log[l].astype(jnp.float32)) * jax.nn.softplus(
            da.astype(jnp.float32) + dt_bias[l].astype(jnp.float32))
        o_a = gated_delta_rule_chunked(dq, dk, dv, g, beta)
        z = dz.reshape(bsz, seq, N_DELTA_HEADS, HEAD_DIM).astype(jnp.float32)
        o_a = (rms_norm(o_a, delta_out_norm_w[l]) * jax.nn.silu(z)).astype(x.dtype)
        aq = rope(rms_norm(aq.reshape(bsz, seq, N_ATTN_HEADS, HEAD_DIM), q_norm_w[l]), positions)
        ak = rope(rms_norm(ak.reshape(bsz, seq, N_ATTN_HEADS, HEAD_DIM), k_norm_w[l]), positions)
        av = av.reshape(bsz, seq, N_ATTN_HEADS, HEAD_DIM)
        o_b = dilated_mixture_attention(aq, ak, av)
        o_b = rms_norm(o_b, attn_out_norm_w[l]).astype(x.dtype)
        mixed = jnp.concatenate([o_a.reshape(bsz, seq, DELTA_WIDTH),
                                 o_b.reshape(bsz, seq, ATTN_WIDTH)], axis=-1)
        x = x + mixed @ w_out[l].astype(x.dtype)
        h = rms_norm(x, ffn_norm_w[l])
        gate, up = jnp.split(h @ w_gate_up[l].astype(h.dtype), 2, axis=-1)
        x = x + (jax.nn.silu(gate) * up) @ w_down[l].astype(x.dtype)
    return x


import jax as _jax
import jax.numpy as _jnp

TWIN_FORMAT = 'train_step'
FWD_PARAMS = ['x', 'positions', 'attn_norm_w', 'w_in', 'conv_w', 'a_log', 'dt_bias', 'delta_out_norm_w', 'q_norm_w', 'k_norm_w', 'attn_out_norm_w', 'w_out', 'ffn_norm_w', 'w_gate_up', 'w_down']
TWIN_WEIGHTS = ['attn_norm_w', 'w_in', 'conv_w', 'a_log', 'dt_bias', 'delta_out_norm_w', 'q_norm_w', 'k_norm_w', 'attn_out_norm_w', 'w_out', 'ffn_norm_w', 'w_gate_up', 'w_down']
TWIN_DIFF_INPUT = 'x'
TWIN_INPUTS = ['x', 'positions', 'attn_norm_w', 'w_in', 'conv_w', 'a_log', 'dt_bias', 'delta_out_norm_w', 'q_norm_w', 'k_norm_w', 'attn_out_norm_w', 'w_out', 'ffn_norm_w', 'w_gate_up', 'w_down', 'loss_target', 'm_attn_norm_w', 'm_w_in', 'm_conv_w', 'm_a_log', 'm_dt_bias', 'm_delta_out_norm_w', 'm_q_norm_w', 'm_k_norm_w', 'm_attn_out_norm_w', 'm_w_out', 'm_ffn_norm_w', 'm_w_gate_up', 'm_w_down', 'v_attn_norm_w', 'v_w_in', 'v_conv_w', 'v_a_log', 'v_dt_bias', 'v_delta_out_norm_w', 'v_q_norm_w', 'v_k_norm_w', 'v_attn_out_norm_w', 'v_w_out', 'v_ffn_norm_w', 'v_w_gate_up', 'v_w_down']
TWIN_OUTPUTS = ['loss', 'grad_x', 'grad_attn_norm_w', 'grad_w_in', 'grad_conv_w', 'grad_a_log', 'grad_dt_bias', 'grad_delta_out_norm_w', 'grad_q_norm_w', 'grad_k_norm_w', 'grad_attn_out_norm_w', 'grad_w_out', 'grad_ffn_norm_w', 'grad_w_gate_up', 'grad_w_down', 'delta_attn_norm_w', 'delta_w_in', 'delta_conv_w', 'delta_a_log', 'delta_dt_bias', 'delta_delta_out_norm_w', 'delta_q_norm_w', 'delta_k_norm_w', 'delta_attn_out_norm_w', 'delta_w_out', 'delta_ffn_norm_w', 'delta_w_gate_up', 'delta_w_down', 'new_m_attn_norm_w', 'new_m_w_in', 'new_m_conv_w', 'new_m_a_log', 'new_m_dt_bias', 'new_m_delta_out_norm_w', 'new_m_q_norm_w', 'new_m_k_norm_w', 'new_m_attn_out_norm_w', 'new_m_w_out', 'new_m_ffn_norm_w', 'new_m_w_gate_up', 'new_m_w_down', 'new_v_attn_norm_w', 'new_v_w_in', 'new_v_conv_w', 'new_v_a_log', 'new_v_dt_bias', 'new_v_delta_out_norm_w', 'new_v_q_norm_w', 'new_v_k_norm_w', 'new_v_attn_out_norm_w', 'new_v_w_out', 'new_v_ffn_norm_w', 'new_v_w_gate_up', 'new_v_w_down']
TWIN_LEAF_KINDS = {'loss': 'loss', 'grad_x': 'grad_x', 'grad_attn_norm_w': 'grad_w', 'grad_w_in': 'grad_w', 'grad_conv_w': 'grad_w', 'grad_a_log': 'grad_w', 'grad_dt_bias': 'grad_w', 'grad_delta_out_norm_w': 'grad_w', 'grad_q_norm_w': 'grad_w', 'grad_k_norm_w': 'grad_w', 'grad_attn_out_norm_w': 'grad_w', 'grad_w_out': 'grad_w', 'grad_ffn_norm_w': 'grad_w', 'grad_w_gate_up': 'grad_w', 'grad_w_down': 'grad_w', 'delta_attn_norm_w': 'delta_w', 'delta_w_in': 'delta_w', 'delta_conv_w': 'delta_w', 'delta_a_log': 'delta_w', 'delta_dt_bias': 'delta_w', 'delta_delta_out_norm_w': 'delta_w', 'delta_q_norm_w': 'delta_w', 'delta_k_norm_w': 'delta_w', 'delta_attn_out_norm_w': 'delta_w', 'delta_w_out': 'delta_w', 'delta_ffn_norm_w': 'delta_w', 'delta_w_gate_up': 'delta_w', 'delta_w_down': 'delta_w', 'new_m_attn_norm_w': 'new_m', 'new_m_w_in': 'new_m', 'new_m_conv_w': 'new_m', 'new_m_a_log': 'new_m', 'new_m_dt_bias': 'new_m', 'new_m_delta_out_norm_w': 'new_m', 'new_m_q_norm_w': 'new_m', 'new_m_k_norm_w': 'new_m', 'new_m_attn_out_norm_w': 'new_m', 'new_m_w_out': 'new_m', 'new_m_ffn_norm_w': 'new_m', 'new_m_w_gate_up': 'new_m', 'new_m_w_down': 'new_m', 'new_v_attn_norm_w': 'new_v', 'new_v_w_in': 'new_v', 'new_v_conv_w': 'new_v', 'new_v_a_log': 'new_v', 'new_v_dt_bias': 'new_v', 'new_v_delta_out_norm_w': 'new_v', 'new_v_q_norm_w': 'new_v', 'new_v_k_norm_w': 'new_v', 'new_v_attn_out_norm_w': 'new_v', 'new_v_w_out': 'new_v', 'new_v_ffn_norm_w': 'new_v', 'new_v_w_gate_up': 'new_v', 'new_v_w_down': 'new_v'}


def _forward(args):
    return _fwd_reference(*[args[k] for k in FWD_PARAMS])


def _output_shape():
    def fwd():
        inp = _fwd_setup_inputs(0)
        return _fwd_reference(*[inp[k] for k in FWD_PARAMS])
    out = _jax.eval_shape(fwd)
    return out.shape, out.dtype

N_MICROBATCH = 1
ADAM_LR = 0.001
ADAM_B1 = 0.9
ADAM_B2 = 0.999
ADAM_EPS = 1e-08
ADAM_WD = 0.01
ADAM_STEP = 10
PER_EXAMPLE_BATCH_AXIS = {'x': 0, 'positions': 0, 'loss_target': 0}
SHARED_INPUTS = []
_WEIGHT_DTYPES = {'attn_norm_w': _jnp.float32, 'w_in': _jnp.float32, 'conv_w': _jnp.float32, 'a_log': _jnp.float32, 'dt_bias': _jnp.float32, 'delta_out_norm_w': _jnp.float32, 'q_norm_w': _jnp.float32, 'k_norm_w': _jnp.float32, 'attn_out_norm_w': _jnp.float32, 'w_out': _jnp.float32, 'ffn_norm_w': _jnp.float32, 'w_gate_up': _jnp.float32, 'w_down': _jnp.float32}
MOMENT_SCALE = {'attn_norm_w': 3.230062e+00, 'w_in': 2.058765e-01, 'conv_w': 2.473152e-01, 'a_log': 2.476165e+01, 'dt_bias': 2.383860e+01, 'delta_out_norm_w': 4.369661e+01, 'q_norm_w': 4.873424e-01, 'k_norm_w': 5.392780e-01, 'attn_out_norm_w': 1.273486e+02, 'w_out': 3.776934e-01, 'ffn_norm_w': 1.241676e+01, 'w_gate_up': 1.115184e-01, 'w_down': 1.384109e-01}


def _to_microbatches(a, axis):
    t = _jnp.moveaxis(a, axis, 0)
    t = t.reshape((N_MICROBATCH, t.shape[0] // N_MICROBATCH) + t.shape[1:])
    return _jnp.moveaxis(t, 1, axis + 1)


def setup_inputs(seed: int = 0) -> dict:
    inp = _fwd_setup_inputs(seed)
    key = _jax.random.fold_in(_jax.random.key(seed), 7919)
    shape, _ = _output_shape()
    out = dict(inp)
    out["loss_target"] = _jax.random.normal(_jax.random.fold_in(key, 0), shape, _jnp.float32)
    for i, name in enumerate(TWIN_WEIGHTS):
        w = inp[name].astype(_jnp.float32)
        if MOMENT_SCALE is None:
            s = _jnp.sqrt(_jnp.mean(_jnp.square(w)) + 1e-30)
        else:
            s = MOMENT_SCALE[name]
        km, kv = _jax.random.split(_jax.random.fold_in(key, i + 1))
        out[name] = w
        out["m_" + name] = s * _jax.random.normal(km, w.shape, _jnp.float32)
        out["v_" + name] = (s * s) * _jax.random.uniform(kv, w.shape, _jnp.float32, 0.5, 1.5)
    if N_MICROBATCH > 1:
        for name, axis in PER_EXAMPLE_BATCH_AXIS.items():
            out[name] = _to_microbatches(out[name], axis)
    return {'x': out['x'], 'positions': out['positions'], 'attn_norm_w': out['attn_norm_w'], 'w_in': out['w_in'], 'conv_w': out['conv_w'], 'a_log': out['a_log'], 'dt_bias': out['dt_bias'], 'delta_out_norm_w': out['delta_out_norm_w'], 'q_norm_w': out['q_norm_w'], 'k_norm_w': out['k_norm_w'], 'attn_out_norm_w': out['attn_out_norm_w'], 'w_out': out['w_out'], 'ffn_norm_w': out['ffn_norm_w'], 'w_gate_up': out['w_gate_up'], 'w_down': out['w_down'], 'loss_target': out['loss_target'], 'm_attn_norm_w': out['m_attn_norm_w'], 'm_w_in': out['m_w_in'], 'm_conv_w': out['m_conv_w'], 'm_a_log': out['m_a_log'], 'm_dt_bias': out['m_dt_bias'], 'm_delta_out_norm_w': out['m_delta_out_norm_w'], 'm_q_norm_w': out['m_q_norm_w'], 'm_k_norm_w': out['m_k_norm_w'], 'm_attn_out_norm_w': out['m_attn_out_norm_w'], 'm_w_out': out['m_w_out'], 'm_ffn_norm_w': out['m_ffn_norm_w'], 'm_w_gate_up': out['m_w_gate_up'], 'm_w_down': out['m_w_down'], 'v_attn_norm_w': out['v_attn_norm_w'], 'v_w_in': out['v_w_in'], 'v_conv_w': out['v_conv_w'], 'v_a_log': out['v_a_log'], 'v_dt_bias': out['v_dt_bias'], 'v_delta_out_norm_w': out['v_delta_out_norm_w'], 'v_q_norm_w': out['v_q_norm_w'], 'v_k_norm_w': out['v_k_norm_w'], 'v_attn_out_norm_w': out['v_attn_out_norm_w'], 'v_w_out': out['v_w_out'], 'v_ffn_norm_w': out['v_ffn_norm_w'], 'v_w_gate_up': out['v_w_gate_up'], 'v_w_down': out['v_w_down']}


def _loss(weights, diff, rest, loss_target):
    with _jax.named_scope("forward"):
        args = {**rest, TWIN_DIFF_INPUT: diff, **{k: w.astype(_WEIGHT_DTYPES[k]) for k, w in weights.items()}}
        y = _forward(args)
    with _jax.named_scope("loss_head"):
        err = _jnp.square(y.astype(_jnp.float32) - loss_target)
        return 0.5 * _jnp.sum(_jnp.mean(err, axis=-1)) if err.ndim else 0.5 * err


def _adamw(w, g, m, v):
    m = ADAM_B1 * m + (1.0 - ADAM_B1) * g
    v = ADAM_B2 * v + (1.0 - ADAM_B2) * _jnp.square(g)
    m_hat = m / (1.0 - ADAM_B1 ** ADAM_STEP)
    v_hat = v / (1.0 - ADAM_B2 ** ADAM_STEP)
    delta = -ADAM_LR * (m_hat / (_jnp.sqrt(v_hat) + ADAM_EPS) + ADAM_WD * w)
    return delta, m, v


def reference(x, positions, attn_norm_w, w_in, conv_w, a_log, dt_bias, delta_out_norm_w, q_norm_w, k_norm_w, attn_out_norm_w, w_out, ffn_norm_w, w_gate_up, w_down, loss_target, m_attn_norm_w, m_w_in, m_conv_w, m_a_log, m_dt_bias, m_delta_out_norm_w, m_q_norm_w, m_k_norm_w, m_attn_out_norm_w, m_w_out, m_ffn_norm_w, m_w_gate_up, m_w_down, v_attn_norm_w, v_w_in, v_conv_w, v_a_log, v_dt_bias, v_delta_out_norm_w, v_q_norm_w, v_k_norm_w, v_attn_out_norm_w, v_w_out, v_ffn_norm_w, v_w_gate_up, v_w_down):
    given = dict(x=x, positions=positions, attn_norm_w=attn_norm_w, w_in=w_in, conv_w=conv_w, a_log=a_log, dt_bias=dt_bias, delta_out_norm_w=delta_out_norm_w, q_norm_w=q_norm_w, k_norm_w=k_norm_w, attn_out_norm_w=attn_out_norm_w, w_out=w_out, ffn_norm_w=ffn_norm_w, w_gate_up=w_gate_up, w_down=w_down, loss_target=loss_target, m_attn_norm_w=m_attn_norm_w, m_w_in=m_w_in, m_conv_w=m_conv_w, m_a_log=m_a_log, m_dt_bias=m_dt_bias, m_delta_out_norm_w=m_delta_out_norm_w, m_q_norm_w=m_q_norm_w, m_k_norm_w=m_k_norm_w, m_attn_out_norm_w=m_attn_out_norm_w, m_w_out=m_w_out, m_ffn_norm_w=m_ffn_norm_w, m_w_gate_up=m_w_gate_up, m_w_down=m_w_down, v_attn_norm_w=v_attn_norm_w, v_w_in=v_w_in, v_conv_w=v_conv_w, v_a_log=v_a_log, v_dt_bias=v_dt_bias, v_delta_out_norm_w=v_delta_out_norm_w, v_q_norm_w=v_q_norm_w, v_k_norm_w=v_k_norm_w, v_attn_out_norm_w=v_attn_out_norm_w, v_w_out=v_w_out, v_ffn_norm_w=v_ffn_norm_w, v_w_gate_up=v_w_gate_up, v_w_down=v_w_down)
    weights = {n: given[n] for n in TWIN_WEIGHTS}
    shared = {n: given[n] for n in SHARED_INPUTS}
    per_example = {n: given[n] for n in ['x', 'positions']}
    grad_fn = _jax.value_and_grad(_loss, argnums=(0, 1))

    def one_microbatch(ex, loss_target):
        ex = dict(ex)
        diff = ex.pop(TWIN_DIFF_INPUT)
        return grad_fn(weights, diff, {**shared, **ex}, loss_target)

    if N_MICROBATCH == 1:
        loss, (grad_w, grad_x) = one_microbatch(per_example, given["loss_target"])
    else:
        def body(carry, xs):
            loss_sum, grad_sum = carry
            l_k, (gw_k, gx_k) = one_microbatch(xs[0], xs[1])
            with _jax.named_scope("update"):
                return (loss_sum + l_k, _jax.tree.map(_jnp.add, grad_sum, gw_k)), gx_k

        init = (_jnp.zeros((), _jnp.float32), _jax.tree.map(_jnp.zeros_like, weights))
        (loss, grad_w), grad_x = _jax.lax.scan(body, init, (per_example, given["loss_target"]))
    with _jax.named_scope("update"):
        delta_w, new_m, new_v = {}, {}, {}
        for n in TWIN_WEIGHTS:
            delta_w[n], new_m[n], new_v[n] = _adamw(weights[n], grad_w[n], given["m_" + n], given["v_" + n])
    return (loss, grad_x, *[grad_w[n] for n in TWIN_WEIGHTS], *[delta_w[n] for n in TWIN_WEIGHTS],
            *[new_m[n] for n in TWIN_WEIGHTS], *[new_v[n] for n in TWIN_WEIGHTS])
```

```python
import functools

import numpy as np
import jax
import jax.numpy as jnp
from jax import lax
from jax.experimental import pallas as pl
from jax.experimental.pallas import tpu as pltpu

F32 = jnp.float32
BF16 = jnp.bfloat16

N_DEV = 8
N_HEADS = 8
HD = 128
GW = N_HEADS * HD
CHUNK = 64
PAIR = 2 * CHUNK
SPAN = 128
DILATIONS = (1, 4, 16)
ROPE_THETA = 10000.0
EPS = 1e-6
D_FF = 5632
ADAM_LR, ADAM_B1, ADAM_B2, ADAM_EPS, ADAM_WD, ADAM_STEP = 0.001, 0.9, 0.999, 1e-8, 0.01, 10
NEG = -1e30
VMEM_LIMIT = 56 * 1024 * 1024
ANY = pl.BlockSpec(memory_space=pl.ANY)
HI = lax.Precision.HIGHEST


def _params(n_grid, vmem=VMEM_LIMIT):
    return pltpu.CompilerParams(dimension_semantics=("arbitrary",) * n_grid, vmem_limit_bytes=vmem)


def _sds(shape, dtype):
    return jax.ShapeDtypeStruct(tuple(shape), dtype)


def _sigmoid(x):
    return 1.0 / (1.0 + jnp.exp(-x))


def _silu(x):
    return x * _sigmoid(x)


def _softplus(x):
    return jnp.maximum(x, 0.0) + jnp.log(1.0 + jnp.exp(-jnp.abs(x)))


def _dot(a, b, ca, cb, precision=None):
    return lax.dot_general(a, b, (((ca,), (cb,)), ((), ())), precision=precision,
                           preferred_element_type=F32)


def _b16(x):
    return x if x.dtype == BF16 else x.astype(BF16)


def _iota2(shape, axis):
    return lax.broadcasted_iota(jnp.int32, shape, axis)


def _mm(name, a, b, *, grid, a_spec, b_spec, o_spec, out_shape, ca, cb, nk, add=None, add_spec=None,
        vmem=VMEM_LIMIT):
    has_add = add is not None

    def body(*refs):
        a_ref, b_ref = refs[0], refs[1]
        e_ref = refs[2] if has_add else None
        o_ref = refs[3] if has_add else refs[2]
        part = _dot(_b16(a_ref[...]), _b16(b_ref[...]), ca, cb)
        if nk == 1:
            if has_add:
                part = part + e_ref[...]
            o_ref[...] = part.astype(o_ref.dtype)
            return
        acc = refs[-1]
        k = pl.program_id(2)

        @pl.when(k == 0)
        def _():
            acc[...] = part

        @pl.when(k > 0)
        def _():
            acc[...] += part

        @pl.when(k == nk - 1)
        def _():
            res = acc[...]
            if has_add:
                res = res + e_ref[...]
            o_ref[...] = res.astype(o_ref.dtype)

    in_specs = [a_spec, b_spec] + ([add_spec] if has_add else [])
    args = (a, b) + ((add,) if has_add else ())
    blk = [d for d in o_spec.block_shape if d is not None]
    scratch = [pltpu.VMEM(tuple(blk), F32)] if nk > 1 else []
    return pl.pallas_call(body, grid=grid, in_specs=in_specs, out_specs=o_spec, out_shape=out_shape,
                          scratch_shapes=scratch, name=name, compiler_params=_params(3, vmem))(*args)


def _rms_f(xv, wv):
    return xv * lax.rsqrt(jnp.mean(xv * xv, axis=-1, keepdims=True) + EPS) * wv


def _rms_fwd(name, x, w):
    t, d = x.shape
    tm = min(512, t)

    def body(x_ref, w_ref, o_ref):
        o_ref[...] = _rms_f(x_ref[...], w_ref[...]).astype(BF16)

    row = pl.BlockSpec((tm, d), lambda i: (i, 0))
    vec = pl.BlockSpec((1, d), lambda i: (0, 0))
    return pl.pallas_call(body, grid=(t // tm,), in_specs=[row, vec], out_specs=row,
                          out_shape=_sds((t, d), BF16), name=name, compiler_params=_params(1))(x, w)


def _rms_bwd(name, x, w, dh, res):
    t, d = x.shape
    tm = min(256, t)

    def body(x_ref, w_ref, dh_ref, res_ref, dx_ref, dw_ref):
        _, vjp = jax.vjp(_rms_f, x_ref[...], w_ref[...])
        dxv, dwv = vjp(dh_ref[...])
        dx_ref[...] = dxv + res_ref[...]

        @pl.when(pl.program_id(0) == 0)
        def _():
            dw_ref[...] = jnp.zeros_like(dw_ref)

        dw_ref[...] += dwv

    row = pl.BlockSpec((tm, d), lambda i: (i, 0))
    vec = pl.BlockSpec((1, d), lambda i: (0, 0))
    return pl.pallas_call(body, grid=(t // tm,), in_specs=[row, vec, row, row], out_specs=[row, vec],
                          out_shape=[_sds((t, d), F32), _sds((1, d), F32)], name=name,
                          compiler_params=_params(1))(x, w, dh, res)


def _conv_taps(xv, w_ref, rows):
    c = w_ref[3:4, :] * xv
    for s in (1, 2, 3):
        c = c + w_ref[3 - s:4 - s, :] * jnp.where(rows >= s, pltpu.roll(xv, s, 0), 0.0)
    return c


def _post_conv(c, l2, scale):
    y = _silu(c)
    if l2:
        y = y * lax.rsqrt(jnp.sum(y * y, axis=-1, keepdims=True) + EPS) * scale
    return y


def _conv_fwd(name, proj, conv_w8, group, l2, scale):
    t = proj.shape[0]

    def body(x_ref, w_ref, o_ref):
        rows = _iota2((t, HD), 0)
        o_ref[...] = _post_conv(_conv_taps(x_ref[...], w_ref, rows), l2, scale)

    return pl.pallas_call(
        body, grid=(N_HEADS,),
        in_specs=[pl.BlockSpec((t, HD), lambda h: (0, h + group * N_HEADS)),
                  pl.BlockSpec((8, HD), lambda h: (0, h + group * N_HEADS))],
        out_specs=pl.BlockSpec((t, HD), lambda h: (0, h)),
        out_shape=_sds((t, GW), F32), name=name, compiler_params=_params(1, VMEM_LIMIT))(proj, conv_w8)


def _conv_bwd(name, proj, conv_w8, dn, group, l2, scale):
    t = proj.shape[0]

    def body(x_ref, w_ref, dn_ref, dx_ref, dw_ref):
        rows = _iota2((t, HD), 0)
        xv = x_ref[...]
        c = _conv_taps(xv, w_ref, rows)
        _, vjp = jax.vjp(lambda cc: _post_conv(cc, l2, scale), c)
        (dc,) = vjp(dn_ref[...])
        dx = w_ref[3:4, :] * dc
        dw = jnp.zeros((8, HD), F32)
        rid = _iota2((8, HD), 0)
        dw = dw + jnp.where(rid == 3, jnp.sum(dc * xv, axis=0, keepdims=True), 0.0)
        for s in (1, 2, 3):
            dx = dx + w_ref[3 - s:4 - s, :] * jnp.where(rows < t - s, pltpu.roll(dc, t - s, 0), 0.0)
            xs = jnp.where(rows >= s, pltpu.roll(xv, s, 0), 0.0)
            dw = dw + jnp.where(rid == 3 - s, jnp.sum(dc * xs, axis=0, keepdims=True), 0.0)
        dx_ref[...] = dx.astype(BF16)
        dw_ref[...] = dw

    return pl.pallas_call(
        body, grid=(N_HEADS,),
        in_specs=[pl.BlockSpec((t, HD), lambda h: (0, h + group * N_HEADS)),
                  pl.BlockSpec((8, HD), lambda h: (0, h + group * N_HEADS)),
                  pl.BlockSpec((t, HD), lambda h: (0, h))],
        out_specs=[pl.BlockSpec((t, HD), lambda h: (0, h)), pl.BlockSpec((8, HD), lambda h: (0, h))],
        out_shape=[_sds((t, GW), BF16), _sds((8, GW), F32)], name=name,
        compiler_params=_params(1, VMEM_LIMIT))(proj, conv_w8, dn)


def _chunk_cumsum(g, rows):
    pos = rows % CHUNK
    s = 1
    while s < CHUNK:
        g = g + jnp.where(pos >= s, pltpu.roll(g, s, 0), 0.0)
        s *= 2
    return g


def _gates_fwd(name, proj, small_blk, alog_row, dtb_row):
    t = proj.shape[0]
    tm = min(256, t)

    def body(s_ref, a_ref, b_ref, beta_ref, gc_ref):
        sm = s_ref[...]
        beta = _sigmoid(sm)
        g = -jnp.exp(a_ref[...]) * _softplus(sm + b_ref[...])
        gc = _chunk_cumsum(g, _iota2((tm, HD), 0))
        lane = _iota2((tm, HD), 1)
        for h in range(N_HEADS):
            bcol = jnp.sum(jnp.where(lane == h, beta, 0.0), axis=1, keepdims=True)
            gcol = jnp.sum(jnp.where(lane == 8 + h, gc, 0.0), axis=1, keepdims=True)
            beta_ref[:, h * HD:(h + 1) * HD] = jnp.broadcast_to(bcol, (tm, HD))
            gc_ref[:, h * HD:(h + 1) * HD] = jnp.broadcast_to(gcol, (tm, HD))

    vec = pl.BlockSpec((1, HD), lambda i: (0, 0))
    wide = pl.BlockSpec((tm, GW), lambda i: (i, 0))
    return pl.pallas_call(
        body, grid=(t // tm,),
        in_specs=[pl.BlockSpec((tm, HD), lambda i: (i, small_blk)), vec, vec], out_specs=[wide, wide],
        out_shape=[_sds((t, GW), F32), _sds((t, GW), F32)], name=name,
        compiler_params=_params(1))(proj, alog_row, dtb_row)


def _gates_bwd(name, proj, small_blk, alog_row, dtb_row, dbeta_b, dg_b):
    t = proj.shape[0]
    tm = min(256, t)

    def body(s_ref, a_ref, b_ref, db_ref, dg_ref, ds_ref, da_ref, dbias_ref):
        sm = s_ref[...]
        lane = _iota2((tm, HD), 1)
        db = jnp.zeros((tm, HD), F32)
        dg = jnp.zeros((tm, HD), F32)
        for h in range(N_HEADS):
            db = db + jnp.where(lane == h, db_ref[:, h * HD:(h + 1) * HD], 0.0)
            dg = dg + jnp.where(lane == 8 + h, dg_ref[:, h * HD:(h + 1) * HD], 0.0)
        beta = _sigmoid(sm)
        ea = jnp.exp(a_ref[...])
        pre = sm + b_ref[...]
        g = -ea * _softplus(pre)
        dpre = dg * (-ea) * _sigmoid(pre)
        ds_ref[...] = (db * beta * (1.0 - beta) + dpre).astype(BF16)

        @pl.when(pl.program_id(0) == 0)
        def _():
            da_ref[...] = jnp.zeros_like(da_ref)
            dbias_ref[...] = jnp.zeros_like(dbias_ref)

        da_ref[...] += jnp.sum(dg * g, axis=0, keepdims=True)
        dbias_ref[...] += jnp.sum(dpre, axis=0, keepdims=True)

    vec = pl.BlockSpec((1, HD), lambda i: (0, 0))
    wide = pl.BlockSpec((tm, GW), lambda i: (i, 0))
    return pl.pallas_call(
        body, grid=(t // tm,),
        in_specs=[pl.BlockSpec((tm, HD), lambda i: (i, small_blk)), vec, vec, wide, wide],
        out_specs=[pl.BlockSpec((tm, HD), lambda i: (i, 0)), vec, vec],
        out_shape=[_sds((t, HD), BF16), _sds((1, HD), F32), _sds((1, HD), F32)], name=name,
        compiler_params=_params(1))(proj, alog_row, dtb_row, dbeta_b, dg_b)


def _pair_masks():
    ii = _iota2((PAIR, PAIR), 0)
    jj = _iota2((PAIR, PAIR), 1)
    same = (ii // CHUNK) == (jj // CHUNK)
    return ii, jj, same & (ii >= jj), same & (ii > jj)


def _to_row(col_b, ii, jj):
    return jnp.sum(jnp.where(ii == jj, col_b, 0.0), axis=0, keepdims=True)


def _to_col(row, ii, jj):
    return jnp.sum(jnp.where(ii == jj, jnp.broadcast_to(row, (PAIR, PAIR)), 0.0), axis=1, keepdims=True)


def _decay_parts(gc, last_a, last_b, ii, jj, causal):
    diff = gc - _to_row(gc, ii, jj)
    dmat = jnp.where(causal, jnp.exp(jnp.where(causal, diff, 0.0)), 0.0)
    glast = jnp.where(ii < CHUNK, last_a, last_b)
    return dmat, jnp.exp(gc), jnp.exp(glast - gc)


def _unit_lower_inverse(low, ii, jj):
    eye = jnp.where(ii == jj, 1.0, 0.0)
    mm = lambda a, b: _dot(a, b, 1, 0, HI)
    d1 = jnp.where((ii // 16) == (jj // 16), low, 0.0)
    d2 = mm(d1, d1)
    d4 = mm(d2, d2)
    d8 = mm(d4, d4)
    td = mm(mm(mm(eye - d1, eye + d2), eye + d4), eye + d8)
    n1 = mm(td, low - d1)
    n2 = mm(n1, n1)
    return mm(mm(eye - n1, eye + n2), td)


def _delta_prep(name, qn, kn, vv, beta_b, gc_b):
    t = qn.shape[0]

    def body(q_ref, k_ref, v_ref, b_ref, g_ref, u_ref, w_ref, p_ref, t_ref, qd_ref, kd_ref):
        ii, jj, causal, strict = _pair_masks()
        q, k, v, beta, gc = q_ref[...], k_ref[...], v_ref[...], b_ref[...], g_ref[...]
        last_a, last_b = g_ref[CHUNK - 1:CHUNK, :], g_ref[PAIR - 1:PAIR, :]
        dmat, gam, e2 = _decay_parts(gc, last_a, last_b, ii, jj, causal)
        k16 = _b16(k)
        kk = _dot(k16, k16, 1, 1)
        low = jnp.where(strict, beta * kk * dmat, 0.0)
        tinv = _unit_lower_inverse(low, ii, jj)
        u_ref[...] = _dot(tinv, v * beta, 1, 0, HI)
        w_ref[...] = _dot(tinv, k * (beta * gam), 1, 0, HI).astype(BF16)
        p_ref[...] = jnp.where(causal, _dot(_b16(q), k16, 1, 1) * dmat, 0.0).astype(BF16)
        t_ref[...] = tinv
        qd_ref[...] = (q * gam).astype(BF16)
        kd_ref[...] = (k * e2).astype(BF16)

    blk = pl.BlockSpec((PAIR, HD), lambda i, h: (i, h))
    return pl.pallas_call(
        body, grid=(t // PAIR, N_HEADS), in_specs=[blk] * 5, out_specs=[blk] * 6,
        out_shape=[_sds((t, GW), F32), _sds((t, GW), BF16), _sds((t, GW), BF16), _sds((t, GW), F32),
                   _sds((t, GW), BF16), _sds((t, GW), BF16)],
        name=name, compiler_params=_params(2))(qn, kn, vv, beta_b, gc_b)


def _delta_scan(name, u, w, p, qd, kd, gc_b):
    t = u.shape[0]
    n = t // CHUNK

    def body(u_ref, w_ref, p_ref, qd_ref, kd_ref, g_ref, o_ref, vn_ref, sh_ref, state):
        h = pl.program_id(1)

        @pl.when(pl.program_id(0) == 0)
        def _():
            state[h] = jnp.zeros((HD, HD), F32)

        s = state[h]
        sh_ref[...] = s
        s16 = _b16(s)
        vnew = u_ref[...] - _dot(w_ref[...], s16, 1, 0)
        vn16 = _b16(vnew)
        vpair = jnp.concatenate([vn16, vn16], axis=0)
        o_ref[...] = _dot(qd_ref[...], s16, 1, 0) + _dot(p_ref[...], vpair, 1, 0)
        vn_ref[...] = vn16
        dec = jnp.exp(g_ref[CHUNK - 1:CHUNK, :])
        state[h] = s * dec + _dot(kd_ref[...], vn16, 0, 0)

    blk = pl.BlockSpec((CHUNK, HD), lambda i, h: (i, h))
    return pl.pallas_call(
        body, grid=(n, N_HEADS), in_specs=[blk] * 6,
        out_specs=[blk, blk, pl.BlockSpec((None, None, HD, HD), lambda i, h: (i, h, 0, 0))],
        out_shape=[_sds((t, GW), F32), _sds((t, GW), BF16), _sds((n, N_HEADS, HD, HD), F32)],
        scratch_shapes=[pltpu.VMEM((N_HEADS, HD, HD), F32)], name=name,
        compiler_params=_params(2))(u, w, p, qd, kd, gc_b)


def _delta_scan_bwd(name, do, w, p, qd, kd, gc_b, vn, s_hist):
    t = do.shape[0]
    n = t // CHUNK

    def body(do_ref, w_ref, p_ref, qd_ref, kd_ref, g_ref, vn_ref, sh_ref,
             dvn_ref, dqd_ref, dkd_ref, dw_ref, ddec_ref, dstate):
        h = pl.program_id(1)

        @pl.when(pl.program_id(0) == 0)
        def _():
            dstate[h] = jnp.zeros((HD, HD), F32)

        ds = dstate[h]
        ds16 = _b16(ds)
        s_in = sh_ref[...]
        s16 = _b16(s_in)
        do16 = _b16(do_ref[...])
        ptdo = _dot(p_ref[...], do16, 0, 0)
        dvn = ptdo[:CHUNK, :] + ptdo[CHUNK:, :] + _dot(kd_ref[...], ds16, 1, 0)
        dvn16 = _b16(dvn)
        dec = jnp.exp(g_ref[CHUNK - 1:CHUNK, :])
        dstate[h] = ds * dec + _dot(qd_ref[...], do16, 0, 0) - _dot(w_ref[...], dvn16, 0, 0)
        dvn_ref[...] = dvn
        dqd_ref[...] = _dot(do16, s16, 1, 1)
        dw_ref[...] = -_dot(dvn16, s16, 1, 1)
        dkd_ref[...] = _dot(vn_ref[...], ds16, 1, 1)
        tot = jnp.sum(jnp.sum(s_in * ds, axis=1, keepdims=True), axis=0, keepdims=True)
        ddec_ref[...] = jnp.broadcast_to(tot, (8, HD))

    blk = pl.BlockSpec((CHUNK, HD), lambda i, h: (n - 1 - i, h))
    return pl.pallas_call(
        body, grid=(n, N_HEADS),
        in_specs=[blk] * 7 + [pl.BlockSpec((None, None, HD, HD), lambda i, h: (n - 1 - i, h, 0, 0))],
        out_specs=[blk] * 4 + [pl.BlockSpec((8, HD), lambda i, h: (n - 1 - i, h))],
        out_shape=[_sds((t, GW), F32)] * 4 + [_sds((n * 8, GW), F32)],
        scratch_shapes=[pltpu.VMEM((N_HEADS, HD, HD), F32)], name=name,
        compiler_params=_params(2))(do, w, p, qd, kd, gc_b, vn, s_hist)


def _delta_prep_bwd(name, qn, kn, vv, beta_b, gc_b, tinv, u, w, vn, do, dvn, dqd, dkd, dw, ddec):
    t = qn.shape[0]

    def body(q_ref, k_ref, v_ref, b_ref, g_ref, t_ref, u_ref, w_ref, vn_ref, do_ref, dvn_ref, dqd_ref,
             dkd_ref, dw_ref, ddec_ref, dq_ref, dk_ref, dv_ref, dbeta_ref, dg_ref):
        ii, jj, causal, strict = _pair_masks()
        q, k, v, beta, gc = q_ref[...], k_ref[...], v_ref[...], b_ref[...], g_ref[...]
        last_a, last_b = g_ref[CHUNK - 1:CHUNK, :], g_ref[PAIR - 1:PAIR, :]
        dmat, gam, e2 = _decay_parts(gc, last_a, last_b, ii, jj, causal)
        q16, k16 = _b16(q), _b16(k)
        kk = _dot(k16, k16, 1, 1)
        qk = _dot(q16, k16, 1, 1)
        dqd, dkd = dqd_ref[...], dkd_ref[...]
        dp = jnp.where(causal, _dot(_b16(do_ref[...]), vn_ref[...], 1, 1), 0.0)
        dpd16 = _b16(dp * dmat)
        tinv_v = t_ref[...]
        x = _dot(tinv_v, dvn_ref[...], 0, 0, HI)
        y = _dot(tinv_v, dw_ref[...], 0, 0, HI)
        da = -jnp.where(strict, _dot(_b16(x), _b16(u_ref[...]), 1, 1) + _dot(_b16(y), w_ref[...], 1, 1), 0.0)
        dkk16 = _b16(da * beta * dmat)
        dq_ref[...] = gam * dqd + _dot(dpd16, k16, 1, 0)
        dk_ref[...] = (e2 * dkd + _dot(dpd16, q16, 0, 0) + beta * gam * y
                       + _dot(dkk16, k16, 1, 0) + _dot(dkk16, k16, 0, 0))
        dv_ref[...] = beta * x
        rs = lambda a: jnp.sum(a, axis=1, keepdims=True)
        dbeta = rs(v * x) + rs(k * gam * y) + rs(da * kk * dmat)
        dbeta_ref[...] = jnp.broadcast_to(dbeta, (PAIR, HD))
        m = (dp * qk + da * beta * kk) * dmat
        dgam = rs(q * dqd) + rs(k * beta * y)
        de2 = rs(k * dkd)
        colsum = _to_col(jnp.sum(m, axis=0, keepdims=True), ii, jj)
        te2 = de2 * e2
        dgc = rs(m) - colsum + gam * dgam - te2
        first = ii < CHUNK
        tail_a = jnp.sum(jnp.where(first, te2, 0.0), axis=0, keepdims=True)
        tail_b = jnp.sum(jnp.where(first, 0.0, te2), axis=0, keepdims=True)
        dgc = dgc + jnp.where(ii == CHUNK - 1, tail_a + ddec_ref[0:1, :] * jnp.exp(last_a), 0.0)
        dgc = dgc + jnp.where(ii == PAIR - 1, tail_b + ddec_ref[8:9, :] * jnp.exp(last_b), 0.0)
        dgc_row = _to_row(dgc, ii, jj)
        suffix = ((ii // CHUNK) == (jj // CHUNK)) & (jj >= ii)
        dg = jnp.sum(jnp.where(suffix, jnp.broadcast_to(dgc_row, (PAIR, PAIR)), 0.0), axis=1, keepdims=True)
        dg_ref[...] = jnp.broadcast_to(dg, (PAIR, HD))

    blk = pl.BlockSpec((PAIR, HD), lambda i, h: (i, h))
    return pl.pallas_call(
        body, grid=(t // PAIR, N_HEADS),
        in_specs=[blk] * 14 + [pl.BlockSpec((16, HD), lambda i, h: (i, h))], out_specs=[blk] * 5,
        out_shape=[_sds((t, GW), F32)] * 5, name=name,
        compiler_params=_params(2))(qn, kn, vv, beta_b, gc_b, tinv, u, w, vn, do, dvn, dqd, dkd, dw, ddec)


def _rope_tables(pos_col, inv_row):
    ang = pos_col.astype(F32) * inv_row
    lane = _iota2(ang.shape, 1)
    return jnp.cos(ang), jnp.where(lane < HD // 2, -1.0, 1.0) * jnp.sin(ang)


def _head_rms(xh, wv):
    return xh * lax.rsqrt(jnp.mean(xh * xh, axis=-1, keepdims=True) + EPS) * wv


def _qk_fwd(name, proj, blk_idx, w_row, pos_col, inv_row, use_rope):
    t = proj.shape[0]
    tm = min(256, t)

    def body(x_ref, w_ref, pos_ref, inv_ref, o_ref):
        if use_rope:
            cos, sin = _rope_tables(pos_ref[...], inv_ref[...])
        for h in range(N_HEADS):
            xh = x_ref[:, h * HD:(h + 1) * HD]
            if use_rope:
                y = _head_rms(xh, w_ref[...])
                xh = y * cos + pltpu.roll(y, HD // 2, 1) * sin
            o_ref[:, h * HD:(h + 1) * HD] = xh.astype(BF16)

    vec = pl.BlockSpec((1, HD), lambda i: (0, 0))
    return pl.pallas_call(
        body, grid=(t // tm,),
        in_specs=[pl.BlockSpec((tm, GW), lambda i: (i, blk_idx)), vec, pl.BlockSpec((tm, 1), lambda i: (i, 0)), vec],
        out_specs=pl.BlockSpec((tm, GW), lambda i: (i, 0)), out_shape=_sds((t, GW), BF16), name=name,
        compiler_params=_params(1))(proj, w_row, pos_col, inv_row)


def _qk_bwd(name, proj, blk_idx, w_row, pos_col, inv_row, d1, d2, d3, use_rope):
    t = proj.shape[0]
    tm = min(256, t)

    def body(x_ref, w_ref, pos_ref, inv_ref, d1_ref, d2_ref, d3_ref, dx_ref, dw_ref):
        if use_rope:
            cos, sin = _rope_tables(pos_ref[...], inv_ref[...])
        dw = jnp.zeros((1, HD), F32)
        for h in range(N_HEADS):
            sl = slice(h * HD, (h + 1) * HD)
            dy = d1_ref[:, sl] + d2_ref[:, sl] + d3_ref[:, sl]
            if use_rope:
                dy = dy * cos - pltpu.roll(dy, HD // 2, 1) * sin
                _, vjp = jax.vjp(_head_rms, x_ref[:, sl], w_ref[...])
                dy, dwh = vjp(dy)
                dw = dw + dwh
            dx_ref[:, sl] = dy.astype(BF16)

        @pl.when(pl.program_id(0) == 0)
        def _():
            dw_ref[...] = jnp.zeros_like(dw_ref)

        dw_ref[...] += dw

    vec = pl.BlockSpec((1, HD), lambda i: (0, 0))
    wide = pl.BlockSpec((tm, GW), lambda i: (i, 0))
    return pl.pallas_call(
        body, grid=(t // tm,),
        in_specs=[pl.BlockSpec((tm, GW), lambda i: (i, blk_idx)), vec, pl.BlockSpec((tm, 1), lambda i: (i, 0)), vec,
                  wide, wide, wide],
        out_specs=[wide, vec], out_shape=[_sds((t, GW), BF16), _sds((1, HD), F32)], name=name,
        compiler_params=_params(1))(proj, w_row, pos_col, inv_row, d1, d2, d3)


def _band_mask(first):
    qi = _iota2((SPAN, 2 * SPAN), 0)
    ki = _iota2((SPAN, 2 * SPAN), 1)
    lo = jnp.where(first, SPAN, 0)
    return (ki >= qi) & (ki <= qi + SPAN) & (ki >= lo)


def _swa_fwd(name, q, k, v, r):
    t = q.shape[0]
    rows = t // r
    nb = rows // SPAN
    qv, kv, vv = (a.reshape(rows, r * GW) for a in (q, k, v))
    scale = HD ** -0.5

    def body(q_ref, kp_ref, kc_ref, vp_ref, vc_ref, o_ref, l_ref):
        first = pl.program_id(1) == 0
        mask = _band_mask(first)
        kcat = jnp.concatenate([kp_ref[...], kc_ref[...]], axis=0)
        vcat = jnp.concatenate([vp_ref[...], vc_ref[...]], axis=0)
        s = jnp.where(mask, _dot(q_ref[...], kcat, 1, 1) * scale, NEG)
        m = jnp.max(s, axis=1, keepdims=True)
        p = jnp.exp(s - m)
        den = jnp.sum(p, axis=1, keepdims=True)
        o_ref[...] = _dot(_b16(p), vcat, 1, 0) / den
        l_ref[...] = jnp.broadcast_to(m + jnp.log(den), (SPAN, HD))

    cur = pl.BlockSpec((SPAN, HD), lambda rho, n, h: (n, rho * N_HEADS + h))
    prev = pl.BlockSpec((SPAN, HD), lambda rho, n, h: (jnp.maximum(n - 1, 0), rho * N_HEADS + h))
    o, lse = pl.pallas_call(
        body, grid=(r, nb, N_HEADS), in_specs=[cur, prev, cur, prev, cur], out_specs=[cur, cur],
        out_shape=[_sds((rows, r * GW), F32), _sds((rows, r * GW), F32)], name=name,
        compiler_params=_params(3))(qv, kv, kv, vv, vv)
    return o.reshape(t, GW), lse.reshape(t, GW)


def _swa_bwd(name, q, k, v, do, lse, delta, r):
    t = q.shape[0]
    rows = t // r
    nb = rows // SPAN
    qv, kv, vv, dov, lv, dv_ = (a.reshape(rows, r * GW) for a in (q, k, v, do, lse, delta))
    scale = HD ** -0.5

    def body(qc_ref, qn_ref, kp_ref, kc_ref, vp_ref, vc_ref, doc_ref, don_ref, lc_ref, ln_ref, dc_ref, dn_ref,
             dq_ref, dk_ref, dv_ref):
        n = pl.program_id(1)
        first = n == 0
        last = n == nb - 1
        mask = _band_mask(first)
        kc, vc = kc_ref[...], vc_ref[...]
        kcat = jnp.concatenate([kp_ref[...], kc], axis=0)
        vcat = jnp.concatenate([vp_ref[...], vc], axis=0)
        qc, doc = qc_ref[...], doc_ref[...]
        lc = jnp.concatenate([lc_ref[...], lc_ref[...]], axis=1)
        dlt = jnp.concatenate([dc_ref[...], dc_ref[...]], axis=1)
        s = _dot(qc, kcat, 1, 1) * scale
        p = jnp.where(mask, jnp.exp(jnp.where(mask, s - lc, 0.0)), 0.0)
        ds = p * (_dot(doc, vcat, 1, 1) - dlt)
        dq_ref[...] = _dot(_b16(ds), kcat, 1, 0) * scale
        qi = _iota2((SPAN, SPAN), 0)
        ki = _iota2((SPAN, SPAN), 1)
        mask_n = (ki >= qi) & (ki < jnp.where(last, 0, SPAN))
        qn, don = qn_ref[...], don_ref[...]
        s_n = _dot(qn, kc, 1, 1) * scale
        p_n = jnp.where(mask_n, jnp.exp(jnp.where(mask_n, s_n - ln_ref[...], 0.0)), 0.0)
        ds_n = p_n * (_dot(don, vc, 1, 1) - dn_ref[...])
        p2 = _b16(jnp.concatenate([p[:, SPAN:], p_n], axis=0))
        ds2 = _b16(jnp.concatenate([ds[:, SPAN:], ds_n], axis=0))
        dv_ref[...] = _dot(p2, jnp.concatenate([doc, don], axis=0), 0, 0)
        dk_ref[...] = _dot(ds2, jnp.concatenate([qc, qn], axis=0), 0, 0) * scale

    cur = pl.BlockSpec((SPAN, HD), lambda rho, n, h: (n, rho * N_HEADS + h))
    prev = pl.BlockSpec((SPAN, HD), lambda rho, n, h: (jnp.maximum(n - 1, 0), rho * N_HEADS + h))
    nxt = pl.BlockSpec((SPAN, HD), lambda rho, n, h: (jnp.minimum(n + 1, nb - 1), rho * N_HEADS + h))
    outs = pl.pallas_call(
        body, grid=(r, nb, N_HEADS),
        in_specs=[cur, nxt, prev, cur, prev, cur, cur, nxt, cur, nxt, cur, nxt], out_specs=[cur] * 3,
        out_shape=[_sds((rows, r * GW), F32)] * 3, name=name,
        compiler_params=_params(3))(qv, qv, kv, kv, vv, vv, dov, dov, lv, lv, dv_, dv_)
    return tuple(a.reshape(t, GW) for a in outs)


def _merge(os_, ls_):
    m = jnp.maximum(jnp.maximum(ls_[0], ls_[1]), ls_[2])
    ws = [jnp.exp(l - m) for l in ls_]
    tot = ws[0] + ws[1] + ws[2]
    ob = (ws[0] * os_[0] + ws[1] * os_[1] + ws[2] * os_[2]) / tot
    return ob, m + jnp.log(tot)


def _gated_norm(oa, z, wv):
    return _head_rms(oa, wv) * _silu(z)


def _mix_fwd(name, oa_raw, proj, z_blk, o1, o2, o3, l1, l2, l3, w_dn, w_an):
    t = oa_raw.shape[0]
    tm = min(256, t)

    def body(oa_ref, z_ref, o1_ref, o2_ref, o3_ref, l1_ref, l2_ref, l3_ref, wd_ref, wa_ref,
             mix_ref, ob_ref, lse_ref):
        for h in range(N_HEADS):
            sl = slice(h * HD, (h + 1) * HD)
            mix_ref[:, sl] = _gated_norm(oa_ref[:, sl], z_ref[:, sl], wd_ref[...]).astype(BF16)
            ob, lse = _merge([o1_ref[:, sl], o2_ref[:, sl], o3_ref[:, sl]],
                             [l1_ref[:, sl], l2_ref[:, sl], l3_ref[:, sl]])
            ob_ref[:, sl] = ob
            lse_ref[:, sl] = lse
            mix_ref[:, GW + h * HD:GW + (h + 1) * HD] = _head_rms(ob, wa_ref[...]).astype(BF16)

    vec = pl.BlockSpec((1, HD), lambda i: (0, 0))
    wide = pl.BlockSpec((tm, GW), lambda i: (i, 0))
    return pl.pallas_call(
        body, grid=(t // tm,),
        in_specs=[wide, pl.BlockSpec((tm, GW), lambda i: (i, z_blk))] + [wide] * 6 + [vec, vec],
        out_specs=[pl.BlockSpec((tm, 2 * GW), lambda i: (i, 0)), wide, wide],
        out_shape=[_sds((t, 2 * GW), BF16), _sds((t, GW), F32), _sds((t, GW), F32)], name=name,
        compiler_params=_params(1))(oa_raw, proj, o1, o2, o3, l1, l2, l3, w_dn, w_an)


def _mix_bwd(name, dmixed, oa_raw, proj, z_blk, ob, w_dn, w_an):
    t = oa_raw.shape[0]
    tm = min(256, t)

    def body(dm_ref, oa_ref, z_ref, ob_ref, wd_ref, wa_ref, doa_ref, dz_ref, dob_ref, dl_ref, dwd_ref, dwa_ref):
        dwd = jnp.zeros((1, HD), F32)
        dwa = jnp.zeros((1, HD), F32)
        for h in range(N_HEADS):
            sl = slice(h * HD, (h + 1) * HD)
            _, vjp = jax.vjp(_gated_norm, oa_ref[:, sl], z_ref[:, sl], wd_ref[...])
            doa, dz, dw1 = vjp(dm_ref[:, sl])
            doa_ref[:, sl] = doa
            dz_ref[:, sl] = dz.astype(BF16)
            dwd = dwd + dw1
            obh = ob_ref[:, sl]
            _, vjp2 = jax.vjp(_head_rms, obh, wa_ref[...])
            dob, dw2 = vjp2(dm_ref[:, GW + h * HD:GW + (h + 1) * HD])
            dwa = dwa + dw2
            dob_ref[:, sl] = dob.astype(BF16)
            dl_ref[:, sl] = jnp.broadcast_to(jnp.sum(dob * obh, axis=1, keepdims=True), (tm, HD))

        @pl.when(pl.program_id(0) == 0)
        def _():
            dwd_ref[...] = jnp.zeros_like(dwd_ref)
            dwa_ref[...] = jnp.zeros_like(dwa_ref)

        dwd_ref[...] += dwd
        dwa_ref[...] += dwa

    vec = pl.BlockSpec((1, HD), lambda i: (0, 0))
    wide = pl.BlockSpec((tm, GW), lambda i: (i, 0))
    return pl.pallas_call(
        body, grid=(t // tm,),
        in_specs=[pl.BlockSpec((tm, 2 * GW), lambda i: (i, 0)), wide, pl.BlockSpec((tm, GW), lambda i: (i, z_blk)),
                  wide, vec, vec],
        out_specs=[wide, wide, wide, wide, vec, vec],
        out_shape=[_sds((t, GW), F32), _sds((t, GW), BF16), _sds((t, GW), BF16), _sds((t, GW), F32),
                   _sds((1, HD), F32), _sds((1, HD), F32)], name=name,
        compiler_params=_params(1))(dmixed, oa_raw, proj, ob, w_dn, w_an)


def _swiglu_fwd(name, gu3):
    _, t, f = gu3.shape
    tm, tn = min(512, t), 512

    def body(g_ref, o_ref):
        o_ref[...] = (_silu(g_ref[0]) * g_ref[1]).astype(BF16)

    return pl.pallas_call(
        body, grid=(t // tm, f // tn), in_specs=[pl.BlockSpec((2, tm, tn), lambda i, j: (0, i, j))],
        out_specs=pl.BlockSpec((tm, tn), lambda i, j: (i, j)), out_shape=_sds((t, f), BF16), name=name,
        compiler_params=_params(2))(gu3)


def _swiglu_bwd(name, gu3, dact):
    _, t, f = gu3.shape
    tm, tn = min(512, t), 512

    def body(g_ref, d_ref, o_ref):
        g, up, d = g_ref[0], g_ref[1], d_ref[...]
        sg = _sigmoid(g)
        o_ref[0] = (d * up * sg * (1.0 + g * (1.0 - sg))).astype(BF16)
        o_ref[1] = (d * g * sg).astype(BF16)

    return pl.pallas_call(
        body, grid=(t // tm, f // tn),
        in_specs=[pl.BlockSpec((2, tm, tn), lambda i, j: (0, i, j)), pl.BlockSpec((tm, tn), lambda i, j: (i, j))],
        out_specs=pl.BlockSpec((2, tm, tn), lambda i, j: (0, i, j)), out_shape=_sds((2, t, f), BF16), name=name,
        compiler_params=_params(2))(gu3, dact)


def _loss_head(name, y, target):
    t, d = y.shape
    tm = min(512, t)

    def body(y_ref, t_ref, dy_ref, l_ref):
        diff = y_ref[...] - t_ref[...]
        dy_ref[...] = diff * (1.0 / d)
        part = jnp.sum(jnp.sum(diff * diff, axis=1, keepdims=True), axis=0, keepdims=True) * (0.5 / d)

        @pl.when(pl.program_id(0) == 0)
        def _():
            l_ref[...] = jnp.zeros_like(l_ref)

        l_ref[...] += jnp.broadcast_to(part, (8, 128))

    row = pl.BlockSpec((tm, d), lambda i: (i, 0))
    return pl.pallas_call(body, grid=(t // tm,), in_specs=[row, row],
                          out_specs=[row, pl.BlockSpec((8, 128), lambda i: (0, 0))],
                          out_shape=[_sds((t, d), F32), _sds((8, 128), F32)], name=name,
                          compiler_params=_params(1))(y, target)


def _peer(me, k):
    pid = (me + k) % N_DEV
    return (pid // 4, (pid // 2) % 2, pid % 2)


def _my_id():
    return 4 * lax.axis_index("x") + 2 * lax.axis_index("y") + lax.axis_index("c")


def _exchange(name, arrays, scatter):
    n = len(arrays)

    def body(*refs):
        ins, outs = refs[:n], refs[n:2 * n]
        send_sems, recv_sems, local_sems = refs[2 * n:]
        me = _my_id()
        started = []
        for a in range(n):
            src = ins[a].at[me] if scatter[a] else ins[a]
            loc = pltpu.make_async_copy(src, outs[a].at[me], local_sems.at[a])
            loc.start()
            started.append(loc)
        remote = []
        for k in range(1, N_DEV):
            to = (me + k) % N_DEV
            for a in range(n):
                src = ins[a].at[to] if scatter[a] else ins[a]
                cp = pltpu.make_async_remote_copy(src_ref=src, dst_ref=outs[a].at[me],
                                                  send_sem=send_sems.at[a * (N_DEV - 1) + k - 1], recv_sem=recv_sems.at[a * (N_DEV - 1) + k - 1],
                                                  device_id=_peer(me, k), device_id_type=pl.DeviceIdType.MESH)
                cp.start()
                remote.append(cp)
        for k in range(1, N_DEV):
            frm = (me + N_DEV - k) % N_DEV
            for a in range(n):
                src = ins[a].at[frm] if scatter[a] else ins[a]
                pltpu.make_async_remote_copy(src_ref=src, dst_ref=outs[a].at[frm],
                                             send_sem=send_sems.at[a * (N_DEV - 1) + k - 1], recv_sem=recv_sems.at[a * (N_DEV - 1) + k - 1],
                                             device_id=_peer(me, k), device_id_type=pl.DeviceIdType.MESH).wait_recv()
        for cp in remote:
            cp.wait_send()
        for loc in started:
            loc.wait()

    out_shape = [_sds((N_DEV,) + (a.shape[1:] if sc else a.shape), a.dtype) for a, sc in zip(arrays, scatter)]
    return pl.pallas_call(
        body, in_specs=[ANY] * n, out_specs=[ANY] * n, out_shape=out_shape,
        scratch_shapes=[pltpu.SemaphoreType.DMA((n * (N_DEV - 1),)), pltpu.SemaphoreType.DMA((n * (N_DEV - 1),)),
                        pltpu.SemaphoreType.DMA((n,))],
        name=name)(*arrays)


def _adamw(name, parts, w, m, v):
    r, c = w.shape
    tr = r
    for cand in (128, 88, 64, 40, 8):
        if r % cand == 0:
            tr = cand
            break
    c1 = 1.0 / (1.0 - ADAM_B1 ** ADAM_STEP)
    c2 = 1.0 / (1.0 - ADAM_B2 ** ADAM_STEP)

    def body(p_ref, w_ref, m_ref, v_ref, g_ref, d_ref, nm_ref, nv_ref):
        g = p_ref[0]
        for s in range(1, N_DEV):
            g = g + p_ref[s]
        mn = ADAM_B1 * m_ref[...] + (1.0 - ADAM_B1) * g
        vn = ADAM_B2 * v_ref[...] + (1.0 - ADAM_B2) * (g * g)
        g_ref[...] = g
        nm_ref[...] = mn
        nv_ref[...] = vn
        d_ref[...] = -ADAM_LR * ((mn * c1) / (jnp.sqrt(vn * c2) + ADAM_EPS) + ADAM_WD * w_ref[...])

    blk = pl.BlockSpec((tr, c), lambda i: (i, 0))
    return pl.pallas_call(
        body, grid=(r // tr,), in_specs=[pl.BlockSpec((N_DEV, tr, c), lambda i: (0, i, 0)), blk, blk, blk],
        out_specs=[blk] * 4, out_shape=[_sds((r, c), F32)] * 4, name=name,
        compiler_params=_params(1, VMEM_LIMIT))(parts, w, m, v)


def _pad_rows(a, rows):
    return jnp.pad(a, ((0, rows - a.shape[0]), (0, 0)))


def _lane_row(vec8, offset):
    return jnp.pad(vec8.reshape(1, 8), ((0, 0), (offset, HD - 8 - offset)))


def kernel(x, positions, attn_norm_w, w_in, conv_w, a_log, dt_bias, delta_out_norm_w, q_norm_w, k_norm_w, attn_out_norm_w, w_out, ffn_norm_w, w_gate_up, w_down, loss_target, m_attn_norm_w, m_w_in, m_conv_w, m_a_log, m_dt_bias, m_delta_out_norm_w, m_q_norm_w, m_k_norm_w, m_attn_out_norm_w, m_w_out, m_ffn_norm_w, m_w_gate_up, m_w_down, v_attn_norm_w, v_w_in, v_conv_w, v_a_log, v_dt_bias, v_delta_out_norm_w, v_q_norm_w, v_k_norm_w, v_attn_out_norm_w, v_w_out, v_ffn_norm_w, v_w_gate_up, v_w_down):
    x2 = x[0]
    t, d = x2.shape
    target = loss_target[0]
    pos_col = positions.reshape(t, 1)
    half = HD // 2
    inv = (ROPE_THETA ** (-np.arange(half, dtype=np.float32) / half)).astype(np.float32)
    inv_row = jnp.asarray(np.concatenate([inv, inv]).reshape(1, HD))

    n_in = w_in.shape[2]
    n_gu = w_gate_up.shape[2]
    gathered = _exchange("gather_weights",
                         [w_in[0].astype(BF16), w_gate_up[0].astype(BF16), w_down[0].astype(BF16),
                          w_out[0].astype(BF16), _pad_rows(conv_w[0], 8)], [False] * 5)
    w_in_g, w_gu_g, w_down_g, w_out_g, conv_g = gathered
    w_in_full = jnp.transpose(w_in_g, (1, 0, 2)).reshape(d, N_DEV * n_in)
    n_main = 4 * GW
    n_small = 2 * N_HEADS
    w_cat = jnp.concatenate([w_in_full[:, :n_main], w_in_full[:, n_main + n_small:],
                             w_in_full[:, n_main:n_main + n_small],
                             jnp.zeros((d, HD - n_small), BF16)], axis=1)
    n_cat = w_cat.shape[1]
    small_blk = (7 * GW) // HD
    w_down_full = w_down_g.reshape(D_FF, d)
    w_out_full = w_out_g.reshape(2 * GW, d)
    conv_w8 = jnp.transpose(conv_g, (1, 0, 2)).reshape(8, 3 * GW)
    alog_row = _lane_row(a_log[0], 8)
    dtb_row = _lane_row(dt_bias[0], 8)

    tm = min(2048, t)
    h1 = _rms_fwd("norm1", x2, attn_norm_w)
    tn = 384
    proj = _mm("in_proj", h1, w_cat, grid=(t // tm, n_cat // tn, 1),
               a_spec=pl.BlockSpec((tm, d), lambda i, j, k: (i, 0)),
               b_spec=pl.BlockSpec((d, tn), lambda i, j, k: (0, j)),
               o_spec=pl.BlockSpec((tm, tn), lambda i, j, k: (i, j)),
               out_shape=_sds((t, n_cat), F32), ca=1, cb=0, nk=1)
    qn = _conv_fwd("conv_q", proj, conv_w8, 0, True, HD ** -0.5)
    kn = _conv_fwd("conv_k", proj, conv_w8, 1, True, 1.0)
    vv = _conv_fwd("conv_v", proj, conv_w8, 2, False, 1.0)
    beta_b, gc_b = _gates_fwd("gates", proj, small_blk, alog_row, dtb_row)
    u, w, p, tinv, qd, kd = _delta_prep("delta_prep", qn, kn, vv, beta_b, gc_b)
    oa_raw, vn, s_hist = _delta_scan("delta_scan", u, w, p, qd, kd, gc_b)

    aq = _qk_fwd("attn_q", proj, 4, q_norm_w, pos_col, inv_row, True)
    ak = _qk_fwd("attn_k", proj, 5, k_norm_w, pos_col, inv_row, True)
    av = _qk_fwd("attn_v", proj, 6, q_norm_w, pos_col, inv_row, False)
    branches = [_swa_fwd("swa_fwd_%d" % r, aq, ak, av, r) for r in DILATIONS]
    (o1, l1), (o2, l2), (o3, l3) = branches
    mixed, ob, lse = _mix_fwd("mix", oa_raw, proj, 3, o1, o2, o3, l1, l2, l3, delta_out_norm_w, attn_out_norm_w)
    tn = 512
    x1 = _mm("out_proj", mixed, w_out_full, grid=(t // tm, d // tn, 1),
             a_spec=pl.BlockSpec((tm, 2 * GW), lambda i, j, k: (i, 0)),
             b_spec=pl.BlockSpec((2 * GW, tn), lambda i, j, k: (0, j)),
             o_spec=pl.BlockSpec((tm, tn), lambda i, j, k: (i, j)),
             add=x2, add_spec=pl.BlockSpec((tm, tn), lambda i, j, k: (i, j)),
             out_shape=_sds((t, d), F32), ca=1, cb=0, nk=1)
    h2 = _rms_fwd("norm2", x1, ffn_norm_w)
    per = N_DEV // 2
    tmd = min(1024, t)
    gu3 = _mm("gate_up", h2, w_gu_g, grid=(t // tmd, N_DEV, 1),
              a_spec=pl.BlockSpec((tmd, d), lambda i, j, k: (i, 0)),
              b_spec=pl.BlockSpec((None, d, n_gu), lambda i, j, k: (j, 0, 0)),
              o_spec=pl.BlockSpec((None, tmd, n_gu), lambda i, j, k: (j // per, i, j % per)),
              out_shape=_sds((2, t, D_FF), F32), ca=1, cb=0, nk=1)
    act = _swiglu_fwd("swiglu", gu3)
    tmd, tkd = min(1024, t), D_FF // 2
    y = _mm("down_proj", act, w_down_full, grid=(t // tmd, d // tn, 2),
            a_spec=pl.BlockSpec((tmd, tkd), lambda i, j, k: (i, k)),
            b_spec=pl.BlockSpec((tkd, tn), lambda i, j, k: (k, j)),
            o_spec=pl.BlockSpec((tmd, tn), lambda i, j, k: (i, j)),
            add=x1, add_spec=pl.BlockSpec((tmd, tn), lambda i, j, k: (i, j)),
            out_shape=_sds((t, d), F32), ca=1, cb=0, nk=2)
    dy, loss_tile = _loss_head("loss_head", y, target)
    loss = lax.psum(loss_tile[0, 0], ("x", "y", "c"))

    dy16 = dy.astype(BF16)
    dact = _mm("d_act", dy16, w_down_full, grid=(t // tm, D_FF // tn, 1),
               a_spec=pl.BlockSpec((tm, d), lambda i, j, k: (i, 0)),
               b_spec=pl.BlockSpec((tn, d), lambda i, j, k: (j, 0)),
               o_spec=pl.BlockSpec((tm, tn), lambda i, j, k: (i, j)),
               out_shape=_sds((t, D_FF), F32), ca=1, cb=1, nk=1)
    tk = min(2048, t)
    nkt = t // tk
    g_down = _mm("g_down", act, dy16, grid=(D_FF // 512, 1, nkt),
                 a_spec=pl.BlockSpec((tk, 512), lambda i, j, k: (k, i)),
                 b_spec=pl.BlockSpec((tk, d), lambda i, j, k: (k, 0)),
                 o_spec=pl.BlockSpec((512, d), lambda i, j, k: (i, 0)),
                 out_shape=_sds((D_FF, d), F32), ca=0, cb=0, nk=nkt)
    dgu3 = _swiglu_bwd("swiglu_bwd", gu3, dact)
    dh2 = _mm("d_h2", dgu3, w_gu_g, grid=(t // tmd, d // tn, N_DEV),
              a_spec=pl.BlockSpec((None, tmd, n_gu), lambda i, j, k: (k // per, i, k % per)),
              b_spec=pl.BlockSpec((None, tn, n_gu), lambda i, j, k: (k, j, 0)),
              o_spec=pl.BlockSpec((tmd, tn), lambda i, j, k: (i, j)),
              out_shape=_sds((t, d), F32), ca=1, cb=1, nk=N_DEV)
    g_gu = _mm("g_gate_up", h2, dgu3, grid=(d // 512, N_DEV, nkt),
               a_spec=pl.BlockSpec((tk, 512), lambda i, j, k: (k, i)),
               b_spec=pl.BlockSpec((None, tk, n_gu), lambda i, j, k: (j // per, k, j % per)),
               o_spec=pl.BlockSpec((None, 512, n_gu), lambda i, j, k: (j, i, 0)),
               out_shape=_sds((N_DEV, d, n_gu), F32), ca=0, cb=0, nk=nkt)
    dx1, g_ffn_norm = _rms_bwd("norm2_bwd", x1, ffn_norm_w, dh2, dy)

    dx1_16 = dx1.astype(BF16)
    dmixed = _mm("d_mixed", dx1_16, w_out_full, grid=(t // tm, (2 * GW) // tn, 1),
                 a_spec=pl.BlockSpec((tm, d), lambda i, j, k: (i, 0)),
                 b_spec=pl.BlockSpec((tn, d), lambda i, j, k: (j, 0)),
                 o_spec=pl.BlockSpec((tm, tn), lambda i, j, k: (i, j)),
                 out_shape=_sds((t, 2 * GW), F32), ca=1, cb=1, nk=1)
    g_out = _mm("g_out", mixed, dx1_16, grid=((2 * GW) // 512, 1, nkt),
                a_spec=pl.BlockSpec((tk, 512), lambda i, j, k: (k, i)),
                b_spec=pl.BlockSpec((tk, d), lambda i, j, k: (k, 0)),
                o_spec=pl.BlockSpec((512, d), lambda i, j, k: (i, 0)),
                out_shape=_sds((2 * GW, d), F32), ca=0, cb=0, nk=nkt)
    doa, dz, dob, delta, g_dn, g_an = _mix_bwd("mix_bwd", dmixed, oa_raw, proj, 3, ob,
                                               delta_out_norm_w, attn_out_norm_w)
    grads = [_swa_bwd("swa_bwd_%d" % r, aq, ak, av, dob, lse, delta, r) for r in DILATIONS]
    daq, g_qn = _qk_bwd("attn_q_bwd", proj, 4, q_norm_w, pos_col, inv_row, grads[0][0], grads[1][0], grads[2][0], True)
    dak, g_kn = _qk_bwd("attn_k_bwd", proj, 5, k_norm_w, pos_col, inv_row, grads[0][1], grads[1][1], grads[2][1], True)
    dav, _ = _qk_bwd("attn_v_bwd", proj, 6, q_norm_w, pos_col, inv_row, grads[0][2], grads[1][2], grads[2][2], False)

    dvn, dqd, dkd, dw, ddec = _delta_scan_bwd("delta_scan_bwd", doa, w, p, qd, kd, gc_b, vn, s_hist)
    dqn, dkn, dvv, dbeta_b, dg_b = _delta_prep_bwd("delta_prep_bwd", qn, kn, vv, beta_b, gc_b, tinv, u, w, vn,
                                                   doa, dvn, dqd, dkd, dw, ddec)
    dxq, gcw_q = _conv_bwd("conv_q_bwd", proj, conv_w8, dqn, 0, True, HD ** -0.5)
    dxk, gcw_k = _conv_bwd("conv_k_bwd", proj, conv_w8, dkn, 1, True, 1.0)
    dxv, gcw_v = _conv_bwd("conv_v_bwd", proj, conv_w8, dvv, 2, False, 1.0)
    dsmall, g_alog_row, g_dtb_row = _gates_bwd("gates_bwd", proj, small_blk, alog_row, dtb_row, dbeta_b, dg_b)
    dproj = jnp.concatenate([dxq, dxk, dxv, dz, daq, dak, dav, dsmall], axis=1)
    tkc = n_cat // 3
    dh1 = _mm("d_h1", dproj, w_cat, grid=(t // tmd, d // tn, 3),
              a_spec=pl.BlockSpec((tmd, tkc), lambda i, j, k: (i, k)),
              b_spec=pl.BlockSpec((tn, tkc), lambda i, j, k: (j, k)),
              o_spec=pl.BlockSpec((tmd, tn), lambda i, j, k: (i, j)),
              out_shape=_sds((t, d), F32), ca=1, cb=1, nk=3)
    tnc = n_cat // 3
    g_cat = _mm("g_in", h1, dproj, grid=(d // 512, 3, nkt),
                a_spec=pl.BlockSpec((tk, 512), lambda i, j, k: (k, i)),
                b_spec=pl.BlockSpec((tk, tnc), lambda i, j, k: (k, j)),
                o_spec=pl.BlockSpec((512, tnc), lambda i, j, k: (i, j)),
                out_shape=_sds((d, n_cat), F32), ca=0, cb=0, nk=nkt)
    grad_x, g_attn_norm = _rms_bwd("norm1_bwd", x2, attn_norm_w, dh1, dx1)

    g_in_full = jnp.concatenate([g_cat[:, :n_main], g_cat[:, 7 * GW:7 * GW + n_small], g_cat[:, n_main:7 * GW]], axis=1)
    g_in_parts = jnp.transpose(g_in_full.reshape(d, N_DEV, n_in), (1, 0, 2))
    g_conv = jnp.concatenate([gcw_q, gcw_k, gcw_v], axis=1)
    n_cw = conv_w.shape[2]
    g_conv_parts = jnp.transpose(g_conv.reshape(8, N_DEV, n_cw), (1, 0, 2))
    small_rows = [g_attn_norm.reshape(d // HD, HD), g_ffn_norm.reshape(d // HD, HD), g_dn, g_qn, g_kn, g_an,
                  g_alog_row, g_dtb_row]
    small_pack = _pad_rows(jnp.concatenate(small_rows, axis=0), 40)
    r_in, r_gu, r_down, r_out, r_conv, r_small = _exchange(
        "reduce_grads",
        [g_in_parts, g_gu, g_down.reshape(N_DEV, D_FF // N_DEV, d), g_out.reshape(N_DEV, (2 * GW) // N_DEV, d),
         g_conv_parts, small_pack], [True, True, True, True, True, False])

    def pack_small(an, fn, dn, qn_, kn_, aon, al, db):
        rows = [an.reshape(d // HD, HD), fn.reshape(d // HD, HD), dn, qn_, kn_, aon,
                _lane_row(al[0], 8), _lane_row(db[0], 8)]
        return _pad_rows(jnp.concatenate(rows, axis=0), 40)

    def unpack_small(pk):
        nr = d // HD
        return dict(attn_norm_w=pk[:nr].reshape(1, d), ffn_norm_w=pk[nr:2 * nr].reshape(1, d),
                    delta_out_norm_w=pk[2 * nr:2 * nr + 1], q_norm_w=pk[2 * nr + 1:2 * nr + 2],
                    k_norm_w=pk[2 * nr + 2:2 * nr + 3], attn_out_norm_w=pk[2 * nr + 3:2 * nr + 4],
                    a_log=pk[2 * nr + 4:2 * nr + 5, 8:16], dt_bias=pk[2 * nr + 5:2 * nr + 6, 8:16])

    res_small = _adamw("adamw_small", r_small,
                       pack_small(attn_norm_w, ffn_norm_w, delta_out_norm_w, q_norm_w, k_norm_w, attn_out_norm_w, a_log, dt_bias),
                       pack_small(m_attn_norm_w, m_ffn_norm_w, m_delta_out_norm_w, m_q_norm_w, m_k_norm_w, m_attn_out_norm_w, m_a_log, m_dt_bias),
                       pack_small(v_attn_norm_w, v_ffn_norm_w, v_delta_out_norm_w, v_q_norm_w, v_k_norm_w, v_attn_out_norm_w, v_a_log, v_dt_bias))
    small = [unpack_small(a) for a in res_small]
    res_in = [a[None] for a in _adamw("adamw_in", r_in, w_in[0], m_w_in[0], v_w_in[0])]
    res_gu = [a[None] for a in _adamw("adamw_gate_up", r_gu, w_gate_up[0], m_w_gate_up[0], v_w_gate_up[0])]
    res_down = [a[None] for a in _adamw("adamw_down", r_down, w_down[0], m_w_down[0], v_w_down[0])]
    res_out = [a[None] for a in _adamw("adamw_out", r_out, w_out[0], m_w_out[0], v_w_out[0])]
    res_conv = [a[None, :4] for a in _adamw("adamw_conv", r_conv, _pad_rows(conv_w[0], 8), _pad_rows(m_conv_w[0], 8),
                                            _pad_rows(v_conv_w[0], 8))]

    outs = [loss, grad_x[None]]
    for i in range(4):
        s = small[i]
        outs += [s["attn_norm_w"], res_in[i], res_conv[i], s["a_log"], s["dt_bias"], s["delta_out_norm_w"],
                 s["q_norm_w"], s["k_norm_w"], s["attn_out_norm_w"], res_out[i], s["ffn_norm_w"], res_gu[i],
                 res_down[i]]
    return tuple(outs)
```

```python
import functools

import numpy as np
import jax
import jax.numpy as jnp
from jax import lax
from jax.experimental import pallas as pl
from jax.experimental.pallas import tpu as pltpu

F32 = jnp.float32
BF16 = jnp.bfloat16

N_DEV = 8
N_HEADS = 8
HD = 128
GW = N_HEADS * HD
CHUNK = 64
PAIR = 2 * CHUNK
SPAN = 128
DILATIONS = (1, 4, 16)
ROPE_THETA = 10000.0
EPS = 1e-6
D_FF = 5632
ADAM_LR, ADAM_B1, ADAM_B2, ADAM_EPS, ADAM_WD, ADAM_STEP = 0.001, 0.9, 0.999, 1e-8, 0.01, 10
NEG = -1e30
VMEM_LIMIT = 56 * 1024 * 1024
ANY = pl.BlockSpec(memory_space=pl.ANY)
HI = lax.Precision.HIGHEST


def _params(n_grid, vmem=VMEM_LIMIT):
    return pltpu.CompilerParams(dimension_semantics=("arbitrary",) * n_grid, vmem_limit_bytes=vmem)


def _sds(shape, dtype):
    return jax.ShapeDtypeStruct(tuple(shape), dtype)


def _sigmoid(x):
    return 1.0 / (1.0 + jnp.exp(-x))


def _silu(x):
    return x * _sigmoid(x)


def _softplus(x):
    return jnp.maximum(x, 0.0) + jnp.log(1.0 + jnp.exp(-jnp.abs(x)))


def _dot(a, b, ca, cb, precision=None):
    return lax.dot_general(a, b, (((ca,), (cb,)), ((), ())), precision=precision,
                           preferred_element_type=F32)


def _b16(x):
    return x if x.dtype == BF16 else x.astype(BF16)


def _iota2(shape, axis):
    return lax.broadcasted_iota(jnp.int32, shape, axis)


def _mm(name, a, b, *, grid, a_spec, b_spec, o_spec, out_shape, ca, cb, nk, add=None, add_spec=None,
        vmem=VMEM_LIMIT):
    has_add = add is not None

    def body(*refs):
        a_ref, b_ref = refs[0], refs[1]
        e_ref = refs[2] if has_add else None
        o_ref = refs[3] if has_add else refs[2]
        part = _dot(_b16(a_ref[...]), _b16(b_ref[...]), ca, cb)
        if nk == 1:
            if has_add:
                part = part + e_ref[...]
            o_ref[...] = part.astype(o_ref.dtype)
            return
        acc = refs[-1]
        k = pl.program_id(2)

        @pl.when(k == 0)
        def _():
            acc[...] = part

        @pl.when(k > 0)
        def _():
            acc[...] += part

        @pl.when(k == nk - 1)
        def _():
            res = acc[...]
            if has_add:
                res = res + e_ref[...]
            o_ref[...] = res.astype(o_ref.dtype)

    in_specs = [a_spec, b_spec] + ([add_spec] if has_add else [])
    args = (a, b) + ((add,) if has_add else ())
    blk = [d for d in o_spec.block_shape if d is not None]
    scratch = [pltpu.VMEM(tuple(blk), F32)] if nk > 1 else []
    return pl.pallas_call(body, grid=grid, in_specs=in_specs, out_specs=o_spec, out_shape=out_shape,
                          scratch_shapes=scratch, name=name, compiler_params=_params(3, vmem))(*args)


def _rms_f(xv, wv):
    return xv * lax.rsqrt(jnp.mean(xv * xv, axis=-1, keepdims=True) + EPS) * wv


def _rms_fwd(name, x, w):
    t, d = x.shape
    tm = min(512, t)

    def body(x_ref, w_ref, o_ref):
        o_ref[...] = _rms_f(x_ref[...], w_ref[...]).astype(BF16)

    row = pl.BlockSpec((tm, d), lambda i: (i, 0))
    vec = pl.BlockSpec((1, d), lambda i: (0, 0))
    return pl.pallas_call(body, grid=(t // tm,), in_specs=[row, vec], out_specs=row,
                          out_shape=_sds((t, d), BF16), name=name, compiler_params=_params(1))(x, w)


def _rms_bwd(name, x, w, dh, res):
    t, d = x.shape
    tm = min(256, t)

    def body(x_ref, w_ref, dh_ref, res_ref, dx_ref, dw_ref):
        _, vjp = jax.vjp(_rms_f, x_ref[...], w_ref[...])
        dxv, dwv = vjp(dh_ref[...])
        dx_ref[...] = dxv + res_ref[...]

        @pl.when(pl.program_id(0) == 0)
        def _():
            dw_ref[...] = jnp.zeros_like(dw_ref)

        dw_ref[...] += dwv

    row = pl.BlockSpec((tm, d), lambda i: (i, 0))
    vec = pl.BlockSpec((1, d), lambda i: (0, 0))
    return pl.pallas_call(body, grid=(t // tm,), in_specs=[row, vec, row, row], out_specs=[row, vec],
                          out_shape=[_sds((t, d), F32), _sds((1, d), F32)], name=name,
                          compiler_params=_params(1))(x, w, dh, res)


def _conv_taps(xv, w_ref, rows):
    c = w_ref[3:4, :] * xv
    for s in (1, 2, 3):
        c = c + w_ref[3 - s:4 - s, :] * jnp.where(rows >= s, pltpu.roll(xv, s, 0), 0.0)
    return c


def _post_conv(c, l2, scale):
    y = _silu(c)
    if l2:
        y = y * lax.rsqrt(jnp.sum(y * y, axis=-1, keepdims=True) + EPS) * scale
    return y


def _conv_fwd(name, proj, conv_w8, group, l2, scale):
    t = proj.shape[0]

    def body(x_ref, w_ref, o_ref):
        rows = _iota2((t, HD), 0)
        o_ref[...] = _post_conv(_conv_taps(x_ref[...], w_ref, rows), l2, scale)

    return pl.pallas_call(
        body, grid=(N_HEADS,),
        in_specs=[pl.BlockSpec((t, HD), lambda h: (0, h + group * N_HEADS)),
                  pl.BlockSpec((8, HD), lambda h: (0, h + group * N_HEADS))],
        out_specs=pl.BlockSpec((t, HD), lambda h: (0, h)),
        out_shape=_sds((t, GW), F32), name=name, compiler_params=_params(1, VMEM_LIMIT))(proj, conv_w8)


def _conv_bwd(name, proj, conv_w8, dn, group, l2, scale):
    t = proj.shape[0]

    def body(x_ref, w_ref, dn_ref, dx_ref, dw_ref):
        rows = _iota2((t, HD), 0)
        xv = x_ref[...]
        c = _conv_taps(xv, w_ref, rows)
        _, vjp = jax.vjp(lambda cc: _post_conv(cc, l2, scale), c)
        (dc,) = vjp(dn_ref[...])
        dx = w_ref[3:4, :] * dc
        dw = jnp.zeros((8, HD), F32)
        rid = _iota2((8, HD), 0)
        dw = dw + jnp.where(rid == 3, jnp.sum(dc * xv, axis=0, keepdims=True), 0.0)
        for s in (1, 2, 3):
            dx = dx + w_ref[3 - s:4 - s, :] * jnp.where(rows < t - s, pltpu.roll(dc, t - s, 0), 0.0)
            xs = jnp.where(rows >= s, pltpu.roll(xv, s, 0), 0.0)
            dw = dw + jnp.where(rid == 3 - s, jnp.sum(dc * xs, axis=0, keepdims=True), 0.0)
        dx_ref[...] = dx.astype(BF16)
        dw_ref[...] = dw

    return pl.pallas_call(
        body, grid=(N_HEADS,),
        in_specs=[pl.BlockSpec((t, HD), lambda h: (0, h + group * N_HEADS)),
                  pl.BlockSpec((8, HD), lambda h: (0, h + group * N_HEADS)),
                  pl.BlockSpec((t, HD), lambda h: (0, h))],
        out_specs=[pl.BlockSpec((t, HD), lambda h: (0, h)), pl.BlockSpec((8, HD), lambda h: (0, h))],
        out_shape=[_sds((t, GW), BF16), _sds((8, GW), F32)], name=name,
        compiler_params=_params(1, VMEM_LIMIT))(proj, conv_w8, dn)


def _chunk_cumsum(g, rows):
    pos = rows % CHUNK
    s = 1
    while s < CHUNK:
        g = g + jnp.where(pos >= s, pltpu.roll(g, s, 0), 0.0)
        s *= 2
    return g


def _gates_fwd(name, proj, small_blk, alog_row, dtb_row):
    t = proj.shape[0]
    tm = min(256, t)

    def body(s_ref, a_ref, b_ref, beta_ref, gc_ref):
        sm = s_ref[...]
        beta = _sigmoid(sm)
        g = -jnp.exp(a_ref[...]) * _softplus(sm + b_ref[...])
        gc = _chunk_cumsum(g, _iota2((tm, HD), 0))
        lane = _iota2((tm, HD), 1)
        for h in range(N_HEADS):
            bcol = jnp.sum(jnp.where(lane == h, beta, 0.0), axis=1, keepdims=True)
            gcol = jnp.sum(jnp.where(lane == 8 + h, gc, 0.0), axis=1, keepdims=True)
            beta_ref[:, h * HD:(h + 1) * HD] = jnp.broadcast_to(bcol, (tm, HD))
            gc_ref[:, h * HD:(h + 1) * HD] = jnp.broadcast_to(gcol, (tm, HD))

    vec = pl.BlockSpec((1, HD), lambda i: (0, 0))
    wide = pl.BlockSpec((tm, GW), lambda i: (i, 0))
    return pl.pallas_call(
        body, grid=(t // tm,),
        in_specs=[pl.BlockSpec((tm, HD), lambda i: (i, small_blk)), vec, vec], out_specs=[wide, wide],
        out_shape=[_sds((t, GW), F32), _sds((t, GW), F32)], name=name,
        compiler_params=_params(1))(proj, alog_row, dtb_row)


def _gates_bwd(name, proj, small_blk, alog_row, dtb_row, dbeta_b, dg_b):
    t = proj.shape[0]
    tm = min(256, t)

    def body(s_ref, a_ref, b_ref, db_ref, dg_ref, ds_ref, da_ref, dbias_ref):
        sm = s_ref[...]
        lane = _iota2((tm, HD), 1)
        db = jnp.zeros((tm, HD), F32)
        dg = jnp.zeros((tm, HD), F32)
        for h in range(N_HEADS):
            db = db + jnp.where(lane == h, db_ref[:, h * HD:(h + 1) * HD], 0.0)
            dg = dg + jnp.where(lane == 8 + h, dg_ref[:, h * HD:(h + 1) * HD], 0.0)
        beta = _sigmoid(sm)
        ea = jnp.exp(a_ref[...])
        pre = sm + b_ref[...]
        g = -ea * _softplus(pre)
        dpre = dg * (-ea) * _sigmoid(pre)
        ds_ref[...] = (db * beta * (1.0 - beta) + dpre).astype(BF16)

        @pl.when(pl.program_id(0) == 0)
        def _():
            da_ref[...] = jnp.zeros_like(da_ref)
            dbias_ref[...] = jnp.zeros_like(dbias_ref)

        da_ref[...] += jnp.sum(dg * g, axis=0, keepdims=True)
        dbias_ref[...] += jnp.sum(dpre, axis=0, keepdims=True)

    vec = pl.BlockSpec((1, HD), lambda i: (0, 0))
    wide = pl.BlockSpec((tm, GW), lambda i: (i, 0))
    return pl.pallas_call(
        body, grid=(t // tm,),
        in_specs=[pl.BlockSpec((tm, HD), lambda i: (i, small_blk)), vec, vec, wide, wide],
        out_specs=[pl.BlockSpec((tm, HD), lambda i: (i, 0)), vec, vec],
        out_shape=[_sds((t, HD), BF16), _sds((1, HD), F32), _sds((1, HD), F32)], name=name,
        compiler_params=_params(1))(proj, alog_row, dtb_row, dbeta_b, dg_b)


def _pair_masks():
    ii = _iota2((PAIR, PAIR), 0)
    jj = _iota2((PAIR, PAIR), 1)
    same = (ii // CHUNK) == (jj // CHUNK)
    return ii, jj, same & (ii >= jj), same & (ii > jj)


def _to_row(col_b, ii, jj):
    return jnp.sum(jnp.where(ii == jj, col_b, 0.0), axis=0, keepdims=True)


def _to_col(row, ii, jj):
    return jnp.sum(jnp.where(ii == jj, jnp.broadcast_to(row, (PAIR, PAIR)), 0.0), axis=1, keepdims=True)


def _decay_parts(gc, last_a, last_b, ii, jj, causal):
    diff = gc - _to_row(gc, ii, jj)
    dmat = jnp.where(causal, jnp.exp(jnp.where(causal, diff, 0.0)), 0.0)
    glast = jnp.where(ii < CHUNK, last_a, last_b)
    return dmat, jnp.exp(gc), jnp.exp(glast - gc)


def _unit_lower_inverse(low, ii, jj):
    eye = jnp.where(ii == jj, 1.0, 0.0)
    mm = lambda a, b: _dot(a, b, 1, 0, HI)
    d1 = jnp.where((ii // 16) == (jj // 16), low, 0.0)
    d2 = mm(d1, d1)
    d4 = mm(d2, d2)
    d8 = mm(d4, d4)
    td = mm(mm(mm(eye - d1, eye + d2), eye + d4), eye + d8)
    n1 = mm(td, low - d1)
    n2 = mm(n1, n1)
    return mm(mm(eye - n1, eye + n2), td)


def _delta_prep(name, qn, kn, vv, beta_b, gc_b):
    t = qn.shape[0]

    def body(q_ref, k_ref, v_ref, b_ref, g_ref, u_ref, w_ref, p_ref, t_ref, qd_ref, kd_ref):
        ii, jj, causal, strict = _pair_masks()
        q, k, v, beta, gc = q_ref[...], k_ref[...], v_ref[...], b_ref[...], g_ref[...]
        last_a, last_b = g_ref[CHUNK - 1:CHUNK, :], g_ref[PAIR - 1:PAIR, :]
        dmat, gam, e2 = _decay_parts(gc, last_a, last_b, ii, jj, causal)
        k16 = _b16(k)
        kk = _dot(k16, k16, 1, 1)
        low = jnp.where(strict, beta * kk * dmat, 0.0)
        tinv = _unit_lower_inverse(low, ii, jj)
        u_ref[...] = _dot(tinv, v * beta, 1, 0, HI)
        w_ref[...] = _dot(tinv, k * (beta * gam), 1, 0, HI).astype(BF16)
        p_ref[...] = jnp.where(causal, _dot(_b16(q), k16, 1, 1) * dmat, 0.0).astype(BF16)
        t_ref[...] = tinv
        qd_ref[...] = (q * gam).astype(BF16)
        kd_ref[...] = (k * e2).astype(BF16)

    blk = pl.BlockSpec((PAIR, HD), lambda i, h: (i, h))
    return pl.pallas_call(
        body, grid=(t // PAIR, N_HEADS), in_specs=[blk] * 5, out_specs=[blk] * 6,
        out_shape=[_sds((t, GW), F32), _sds((t, GW), BF16), _sds((t, GW), BF16), _sds((t, GW), F32),
                   _sds((t, GW), BF16), _sds((t, GW), BF16)],
        name=name, compiler_params=_params(2))(qn, kn, vv, beta_b, gc_b)


def _delta_scan(name, u, w, p, qd, kd, gc_b):
    t = u.shape[0]
    n = t // CHUNK

    def body(u_ref, w_ref, p_ref, qd_ref, kd_ref, g_ref, o_ref, vn_ref, sh_ref, state):
        h = pl.program_id(1)

        @pl.when(pl.program_id(0) == 0)
        def _():
            state[h] = jnp.zeros((HD, HD), F32)

        s = state[h]
        sh_ref[...] = s
        s16 = _b16(s)
        vnew = u_ref[...] - _dot(w_ref[...], s16, 1, 0)
        vn16 = _b16(vnew)
        vpair = jnp.concatenate([vn16, vn16], axis=0)
        o_ref[...] = _dot(qd_ref[...], s16, 1, 0) + _dot(p_ref[...], vpair, 1, 0)
        vn_ref[...] = vn16
        dec = jnp.exp(g_ref[CHUNK - 1:CHUNK, :])
        state[h] = s * dec + _dot(kd_ref[...], vn16, 0, 0)

    blk = pl.BlockSpec((CHUNK, HD), lambda i, h: (i, h))
    return pl.pallas_call(
        body, grid=(n, N_HEADS), in_specs=[blk] * 6,
        out_specs=[blk, blk, pl.BlockSpec((None, None, HD, HD), lambda i, h: (i, h, 0, 0))],
        out_shape=[_sds((t, GW), F32), _sds((t, GW), BF16), _sds((n, N_HEADS, HD, HD), F32)],
        scratch_shapes=[pltpu.VMEM((N_HEADS, HD, HD), F32)], name=name,
        compiler_params=_params(2))(u, w, p, qd, kd, gc_b)


def _delta_scan_bwd(name, do, w, p, qd, kd, gc_b, vn, s_hist):
    t = do.shape[0]
    n = t // CHUNK

    def body(do_ref, w_ref, p_ref, qd_ref, kd_ref, g_ref, vn_ref, sh_ref,
             dvn_ref, dqd_ref, dkd_ref, dw_ref, ddec_ref, dstate):
        h = pl.program_id(1)

        @pl.when(pl.program_id(0) == 0)
        def _():
            dstate[h] = jnp.zeros((HD, HD), F32)

        ds = dstate[h]
        ds16 = _b16(ds)
        s_in = sh_ref[...]
        s16 = _b16(s_in)
        do16 = _b16(do_ref[...])
        ptdo = _dot(p_ref[...], do16, 0, 0)
        dvn = ptdo[:CHUNK, :] + ptdo[CHUNK:, :] + _dot(kd_ref[...], ds16, 1, 0)
        dvn16 = _b16(dvn)
        dec = jnp.exp(g_ref[CHUNK - 1:CHUNK, :])
        dstate[h] = ds * dec + _dot(qd_ref[...], do16, 0, 0) - _dot(w_ref[...], dvn16, 0, 0)
        dvn_ref[...] = dvn
        dqd_ref[...] = _dot(do16, s16, 1, 1)
        dw_ref[...] = -_dot(dvn16, s16, 1, 1)
        dkd_ref[...] = _dot(vn_ref[...], ds16, 1, 1)
        tot = jnp.sum(jnp.sum(s_in * ds, axis=1, keepdims=True), axis=0, keepdims=True)
        ddec_ref[...] = jnp.broadcast_to(tot, (8, HD))

    blk = pl.BlockSpec((CHUNK, HD), lambda i, h: (n - 1 - i, h))
    return pl.pallas_call(
        body, grid=(n, N_HEADS),
        in_specs=[blk] * 7 + [pl.BlockSpec((None, None, HD, HD), lambda i, h: (n - 1 - i, h, 0, 0))],
        out_specs=[blk] * 4 + [pl.BlockSpec((8, HD), lambda i, h: (n - 1 - i, h))],
        out_shape=[_sds((t, GW), F32)] * 4 + [_sds((n * 8, GW), F32)],
        scratch_shapes=[pltpu.VMEM((N_HEADS, HD, HD), F32)], name=name,
        compiler_params=_params(2))(do, w, p, qd, kd, gc_b, vn, s_hist)


def _delta_prep_bwd(name, qn, kn, vv, beta_b, gc_b, tinv, u, w, vn, do, dvn, dqd, dkd, dw, ddec):
    t = qn.shape[0]

    def body(q_ref, k_ref, v_ref, b_ref, g_ref, t_ref, u_ref, w_ref, vn_ref, do_ref, dvn_ref, dqd_ref,
             dkd_ref, dw_ref, ddec_ref, dq_ref, dk_ref, dv_ref, dbeta_ref, dg_ref):
        ii, jj, causal, strict = _pair_masks()
        q, k, v, beta, gc = q_ref[...], k_ref[...], v_ref[...], b_ref[...], g_ref[...]
        last_a, last_b = g_ref[CHUNK - 1:CHUNK, :], g_ref[PAIR - 1:PAIR, :]
        dmat, gam, e2 = _decay_parts(gc, last_a, last_b, ii, jj, causal)
        q16, k16 = _b16(q), _b16(k)
        kk = _dot(k16, k16, 1, 1)
        qk = _dot(q16, k16, 1, 1)
        dqd, dkd = dqd_ref[...], dkd_ref[...]
        dp = jnp.where(causal, _dot(_b16(do_ref[...]), vn_ref[...], 1, 1), 0.0)
        dpd16 = _b16(dp * dmat)
        tinv_v = t_ref[...]
        x = _dot(tinv_v, dvn_ref[...], 0, 0, HI)
        y = _dot(tinv_v, dw_ref[...], 0, 0, HI)
        da = -jnp.where(strict, _dot(_b16(x), _b16(u_ref[...]), 1, 1) + _dot(_b16(y), w_ref[...], 1, 1), 0.0)
        dkk16 = _b16(da * beta * dmat)
        dq_ref[...] = gam * dqd + _dot(dpd16, k16, 1, 0)
        dk_ref[...] = (e2 * dkd + _dot(dpd16, q16, 0, 0) + beta * gam * y
                       + _dot(dkk16, k16, 1, 0) + _dot(dkk16, k16, 0, 0))
        dv_ref[...] = beta * x
        rs = lambda a: jnp.sum(a, axis=1, keepdims=True)
        dbeta = rs(v * x) + rs(k * gam * y) + rs(da * kk * dmat)
        dbeta_ref[...] = jnp.broadcast_to(dbeta, (PAIR, HD))
        m = (dp * qk + da * beta * kk) * dmat
        dgam = rs(q * dqd) + rs(k * beta * y)
        de2 = rs(k * dkd)
        colsum = _to_col(jnp.sum(m, axis=0, keepdims=True), ii, jj)
        te2 = de2 * e2
        dgc = rs(m) - colsum + gam * dgam - te2
        first = ii < CHUNK
        tail_a = jnp.sum(jnp.where(first, te2, 0.0), axis=0, keepdims=True)
        tail_b = jnp.sum(jnp.where(first, 0.0, te2), axis=0, keepdims=True)
        dgc = dgc + jnp.where(ii == CHUNK - 1, tail_a + ddec_ref[0:1, :] * jnp.exp(last_a), 0.0)
        dgc = dgc + jnp.where(ii == PAIR - 1, tail_b + ddec_ref[8:9, :] * jnp.exp(last_b), 0.0)
        dgc_row = _to_row(dgc, ii, jj)
        suffix = ((ii // CHUNK) == (jj // CHUNK)) & (jj >= ii)
        dg = jnp.sum(jnp.where(suffix, jnp.broadcast_to(dgc_row, (PAIR, PAIR)), 0.0), axis=1, keepdims=True)
        dg_ref[...] = jnp.broadcast_to(dg, (PAIR, HD))

    blk = pl.BlockSpec((PAIR, HD), lambda i, h: (i, h))
    return pl.pallas_call(
        body, grid=(t // PAIR, N_HEADS),
        in_specs=[blk] * 14 + [pl.BlockSpec((16, HD), lambda i, h: (i, h))], out_specs=[blk] * 5,
        out_shape=[_sds((t, GW), F32)] * 5, name=name,
        compiler_params=_params(2))(qn, kn, vv, beta_b, gc_b, tinv, u, w, vn, do, dvn, dqd, dkd, dw, ddec)


def _rope_tables(pos_col, inv_row):
    ang = pos_col.astype(F32) * inv_row
    lane = _iota2(ang.shape, 1)
    return jnp.cos(ang), jnp.where(lane < HD // 2, -1.0, 1.0) * jnp.sin(ang)


def _head_rms(xh, wv):
    return xh * lax.rsqrt(jnp.mean(xh * xh, axis=-1, keepdims=True) + EPS) * wv


def _qk_fwd(name, proj, blk_idx, w_row, pos_col, inv_row, use_rope):
    t = proj.shape[0]
    tm = min(256, t)

    def body(x_ref, w_ref, pos_ref, inv_ref, o_ref):
        if use_rope:
            cos, sin = _rope_tables(pos_ref[...], inv_ref[...])
        for h in range(N_HEADS):
            xh = x_ref[:, h * HD:(h + 1) * HD]
            if use_rope:
                y = _head_rms(xh, w_ref[...])
                xh = y * cos + pltpu.roll(y, HD // 2, 1) * sin
            o_ref[:, h * HD:(h + 1) * HD] = xh.astype(BF16)

    vec = pl.BlockSpec((1, HD), lambda i: (0, 0))
    return pl.pallas_call(
        body, grid=(t // tm,),
        in_specs=[pl.BlockSpec((tm, GW), lambda i: (i, blk_idx)), vec, pl.BlockSpec((tm, 1), lambda i: (i, 0)), vec],
        out_specs=pl.BlockSpec((tm, GW), lambda i: (i, 0)), out_shape=_sds((t, GW), BF16), name=name,
        compiler_params=_params(1))(proj, w_row, pos_col, inv_row)


def _qk_bwd(name, proj, blk_idx, w_row, pos_col, inv_row, d1, d2, d3, use_rope):
    t = proj.shape[0]
    tm = min(256, t)

    def body(x_ref, w_ref, pos_ref, inv_ref, d1_ref, d2_ref, d3_ref, dx_ref, dw_ref):
        if use_rope:
            cos, sin = _rope_tables(pos_ref[...], inv_ref[...])
        dw = jnp.zeros((1, HD), F32)
        for h in range(N_HEADS):
            sl = slice(h * HD, (h + 1) * HD)
            dy = d1_ref[:, sl] + d2_ref[:, sl] + d3_ref[:, sl]
            if use_rope:
                dy = dy * cos - pltpu.roll(dy, HD // 2, 1) * sin
                _, vjp = jax.vjp(_head_rms, x_ref[:, sl], w_ref[...])
                dy, dwh = vjp(dy)
                dw = dw + dwh
            dx_ref[:, sl] = dy.astype(BF16)

        @pl.when(pl.program_id(0) == 0)
        def _():
            dw_ref[...] = jnp.zeros_like(dw_ref)

        dw_ref[...] += dw

    vec = pl.BlockSpec((1, HD), lambda i: (0, 0))
    wide = pl.BlockSpec((tm, GW), lambda i: (i, 0))
    return pl.pallas_call(
        body, grid=(t // tm,),
        in_specs=[pl.BlockSpec((tm, GW), lambda i: (i, blk_idx)), vec, pl.BlockSpec((tm, 1), lambda i: (i, 0)), vec,
                  wide, wide, wide],
        out_specs=[wide, vec], out_shape=[_sds((t, GW), BF16), _sds((1, HD), F32)], name=name,
        compiler_params=_params(1))(proj, w_row, pos_col, inv_row, d1, d2, d3)


def _band_mask(first):
    qi = _iota2((SPAN, 2 * SPAN), 0)
    ki = _iota2((SPAN, 2 * SPAN), 1)
    lo = jnp.where(first, SPAN, 0)
    return (ki >= qi) & (ki <= qi + SPAN) & (ki >= lo)


def _swa_fwd(name, q, k, v, r):
    t = q.shape[0]
    rows = t // r
    nb = rows // SPAN
    qv, kv, vv = (a.reshape(rows, r * GW) for a in (q, k, v))
    scale = HD ** -0.5

    def body(q_ref, kp_ref, kc_ref, vp_ref, vc_ref, o_ref, l_ref):
        first = pl.program_id(1) == 0
        mask = _band_mask(first)
        kcat = jnp.concatenate([kp_ref[...], kc_ref[...]], axis=0)
        vcat = jnp.concatenate([vp_ref[...], vc_ref[...]], axis=0)
        s = jnp.where(mask, _dot(q_ref[...], kcat, 1, 1) * scale, NEG)
        m = jnp.max(s, axis=1, keepdims=True)
        p = jnp.exp(s - m)
        den = jnp.sum(p, axis=1, keepdims=True)
        o_ref[...] = _dot(_b16(p), vcat, 1, 0) / den
        l_ref[...] = jnp.broadcast_to(m + jnp.log(den), (SPAN, HD))

    cur = pl.BlockSpec((SPAN, HD), lambda rho, n, h: (n, rho * N_HEADS + h))
    prev = pl.BlockSpec((SPAN, HD), lambda rho, n, h: (jnp.maximum(n - 1, 0), rho * N_HEADS + h))
    o, lse = pl.pallas_call(
        body, grid=(r, nb, N_HEADS), in_specs=[cur, prev, cur, prev, cur], out_specs=[cur, cur],
        out_shape=[_sds((rows, r * GW), F32), _sds((rows, r * GW), F32)], name=name,
        compiler_params=_params(3))(qv, kv, kv, vv, vv)
    return o.reshape(t, GW), lse.reshape(t, GW)


def _swa_bwd(name, q, k, v, do, lse, delta, r):
    t = q.shape[0]
    rows = t // r
    nb = rows // SPAN
    qv, kv, vv, dov, lv, dv_ = (a.reshape(rows, r * GW) for a in (q, k, v, do, lse, delta))
    scale = HD ** -0.5

    def body(qc_ref, qn_ref, kp_ref, kc_ref, vp_ref, vc_ref, doc_ref, don_ref, lc_ref, ln_ref, dc_ref, dn_ref,
             dq_ref, dk_ref, dv_ref):
        n = pl.program_id(1)
        first = n == 0
        last = n == nb - 1
        mask = _band_mask(first)
        kc, vc = kc_ref[...], vc_ref[...]
        kcat = jnp.concatenate([kp_ref[...], kc], axis=0)
        vcat = jnp.concatenate([vp_ref[...], vc], axis=0)
        qc, doc = qc_ref[...], doc_ref[...]
        lc = jnp.concatenate([lc_ref[...], lc_ref[...]], axis=1)
        dlt = jnp.concatenate([dc_ref[...], dc_ref[...]], axis=1)
        s = _dot(qc, kcat, 1, 1) * scale
        p = jnp.where(mask, jnp.exp(jnp.where(mask, s - lc, 0.0)), 0.0)
        ds = p * (_dot(doc, vcat, 1, 1) - dlt)
        dq_ref[...] = _dot(_b16(ds), kcat, 1, 0) * scale
        qi = _iota2((SPAN, SPAN), 0)
        ki = _iota2((SPAN, SPAN), 1)
        mask_n = (ki >= qi) & (ki < jnp.where(last, 0, SPAN))
        qn, don = qn_ref[...], don_ref[...]
        s_n = _dot(qn, kc, 1, 1) * scale
        p_n = jnp.where(mask_n, jnp.exp(jnp.where(mask_n, s_n - ln_ref[...], 0.0)), 0.0)
        ds_n = p_n * (_dot(don, vc, 1, 1) - dn_ref[...])
        p2 = _b16(jnp.concatenate([p[:, SPAN:], p_n], axis=0))
        ds2 = _b16(jnp.concatenate([ds[:, SPAN:], ds_n], axis=0))
        dv_ref[...] = _dot(p2, jnp.concatenate([doc, don], axis=0), 0, 0)
        dk_ref[...] = _dot(ds2, jnp.concatenate([qc, qn], axis=0), 0, 0) * scale

    cur = pl.BlockSpec((SPAN, HD), lambda rho, n, h: (n, rho * N_HEADS + h))
    prev = pl.BlockSpec((SPAN, HD), lambda rho, n, h: (jnp.maximum(n - 1, 0), rho * N_HEADS + h))
    nxt = pl.BlockSpec((SPAN, HD), lambda rho, n, h: (jnp.minimum(n + 1, nb - 1), rho * N_HEADS + h))
    outs = pl.pallas_call(
        body, grid=(r, nb, N_HEADS),
        in_specs=[cur, nxt, prev, cur, prev, cur, cur, nxt, cur, nxt, cur, nxt], out_specs=[cur] * 3,
        out_shape=[_sds((rows, r * GW), F32)] * 3, name=name,
        compiler_params=_params(3))(qv, qv, kv, kv, vv, vv, dov, dov, lv, lv, dv_, dv_)
    return tuple(a.reshape(t, GW) for a in outs)


def _merge(os_, ls_):
    m = jnp.maximum(jnp.maximum(ls_[0], ls_[1]), ls_[2])
    ws = [jnp.exp(l - m) for l in ls_]
    tot = ws[0] + ws[1] + ws[2]
    ob = (ws[0] * os_[0] + ws[1] * os_[1] + ws[2] * os_[2]) / tot
    return ob, m + jnp.log(tot)


def _gated_norm(oa, z, wv):
    return _head_rms(oa, wv) * _silu(z)


def _mix_fwd(name, oa_raw, proj, z_blk, o1, o2, o3, l1, l2, l3, w_dn, w_an):
    t = oa_raw.shape[0]
    tm = min(256, t)

    def body(oa_ref, z_ref, o1_ref, o2_ref, o3_ref, l1_ref, l2_ref, l3_ref, wd_ref, wa_ref,
             mix_ref, ob_ref, lse_ref):
        for h in range(N_HEADS):
            sl = slice(h * HD, (h + 1) * HD)
            mix_ref[:, sl] = _gated_norm(oa_ref[:, sl], z_ref[:, sl], wd_ref[...]).astype(BF16)
            ob, lse = _merge([o1_ref[:, sl], o2_ref[:, sl], o3_ref[:, sl]],
                             [l1_ref[:, sl], l2_ref[:, sl], l3_ref[:, sl]])
            ob_ref[:, sl] = ob
            lse_ref[:, sl] = lse
            mix_ref[:, GW + h * HD:GW + (h + 1) * HD] = _head_rms(ob, wa_ref[...]).astype(BF16)

    vec = pl.BlockSpec((1, HD), lambda i: (0, 0))
    wide = pl.BlockSpec((tm, GW), lambda i: (i, 0))
    return pl.pallas_call(
        body, grid=(t // tm,),
        in_specs=[wide, pl.BlockSpec((tm, GW), lambda i: (i, z_blk))] + [wide] * 6 + [vec, vec],
        out_specs=[pl.BlockSpec((tm, 2 * GW), lambda i: (i, 0)), wide, wide],
        out_shape=[_sds((t, 2 * GW), BF16), _sds((t, GW), F32), _sds((t, GW), F32)], name=name,
        compiler_params=_params(1))(oa_raw, proj, o1, o2, o3, l1, l2, l3, w_dn, w_an)


def _mix_bwd(name, dmixed, oa_raw, proj, z_blk, ob, w_dn, w_an):
    t = oa_raw.shape[0]
    tm = min(256, t)

    def body(dm_ref, oa_ref, z_ref, ob_ref, wd_ref, wa_ref, doa_ref, dz_ref, dob_ref, dl_ref, dwd_ref, dwa_ref):
        dwd = jnp.zeros((1, HD), F32)
        dwa = jnp.zeros((1, HD), F32)
        for h in range(N_HEADS):
            sl = slice(h * HD, (h + 1) * HD)
            _, vjp = jax.vjp(_gated_norm, oa_ref[:, sl], z_ref[:, sl], wd_ref[...])
            doa, dz, dw1 = vjp(dm_ref[:, sl])
            doa_ref[:, sl] = doa
            dz_ref[:, sl] = dz.astype(BF16)
            dwd = dwd + dw1
            obh = ob_ref[:, sl]
            _, vjp2 = jax.vjp(_head_rms, obh, wa_ref[...])
            dob, dw2 = vjp2(dm_ref[:, GW + h * HD:GW + (h + 1) * HD])
            dwa = dwa + dw2
            dob_ref[:, sl] = dob.astype(BF16)
            dl_ref[:, sl] = jnp.broadcast_to(jnp.sum(dob * obh, axis=1, keepdims=True), (tm, HD))

        @pl.when(pl.program_id(0) == 0)
        def _():
            dwd_ref[...] = jnp.zeros_like(dwd_ref)
            dwa_ref[...] = jnp.zeros_like(dwa_ref)

        dwd_ref[...] += dwd
        dwa_ref[...] += dwa

    vec = pl.BlockSpec((1, HD), lambda i: (0, 0))
    wide = pl.BlockSpec((tm, GW), lambda i: (i, 0))
    return pl.pallas_call(
        body, grid=(t // tm,),
        in_specs=[pl.BlockSpec((tm, 2 * GW), lambda i: (i, 0)), wide, pl.BlockSpec((tm, GW), lambda i: (i, z_blk)),
                  wide, vec, vec],
        out_specs=[wide, wide, wide, wide, vec, vec],
        out_shape=[_sds((t, GW), F32), _sds((t, GW), BF16), _sds((t, GW), BF16), _sds((t, GW), F32),
                   _sds((1, HD), F32), _sds((1, HD), F32)], name=name,
        compiler_params=_params(1))(dmixed, oa_raw, proj, ob, w_dn, w_an)


def _swiglu_fwd(name, gu3):
    _, t, f = gu3.shape
    tm, tn = min(512, t), 512

    def body(g_ref, o_ref):
        o_ref[...] = (_silu(g_ref[0]) * g_ref[1]).astype(BF16)

    return pl.pallas_call(
        body, grid=(t // tm, f // tn), in_specs=[pl.BlockSpec((2, tm, tn), lambda i, j: (0, i, j))],
        out_specs=pl.BlockSpec((tm, tn), lambda i, j: (i, j)), out_shape=_sds((t, f), BF16), name=name,
        compiler_params=_params(2))(gu3)


def _swiglu_bwd(name, gu3, dact):
    _, t, f = gu3.shape
    tm, tn = min(512, t), 512

    def body(g_ref, d_ref, o_ref):
        g, up, d = g_ref[0], g_ref[1], d_ref[...]
        sg = _sigmoid(g)
        o_ref[0] = (d * up * sg * (1.0 + g * (1.0 - sg))).astype(BF16)
        o_ref[1] = (d * g * sg).astype(BF16)

    return pl.pallas_call(
        body, grid=(t // tm, f // tn),
        in_specs=[pl.BlockSpec((2, tm, tn), lambda i, j: (0, i, j)), pl.BlockSpec((tm, tn), lambda i, j: (i, j))],
        out_specs=pl.BlockSpec((2, tm, tn), lambda i, j: (0, i, j)), out_shape=_sds((2, t, f), BF16), name=name,
        compiler_params=_params(2))(gu3, dact)


def _loss_head(name, y, target):
    t, d = y.shape
    tm = min(512, t)

    def body(y_ref, t_ref, dy_ref, l_ref):
        diff = y_ref[...] - t_ref[...]
        dy_ref[...] = diff * (1.0 / d)
        part = jnp.sum(jnp.sum(diff * diff, axis=1, keepdims=True), axis=0, keepdims=True) * (0.5 / d)

        @pl.when(pl.program_id(0) == 0)
        def _():
            l_ref[...] = jnp.zeros_like(l_ref)

        l_ref[...] += jnp.broadcast_to(part, (8, 128))

    row = pl.BlockSpec((tm, d), lambda i: (i, 0))
    return pl.pallas_call(body, grid=(t // tm,), in_specs=[row, row],
                          out_specs=[row, pl.BlockSpec((8, 128), lambda i: (0, 0))],
                          out_shape=[_sds((t, d), F32), _sds((8, 128), F32)], name=name,
                          compiler_params=_params(1))(y, target)


def _peer(me, k):
    pid = (me + k) % N_DEV
    return (pid // 4, (pid // 2) % 2, pid % 2)


def _my_id():
    return 4 * lax.axis_index("x") + 2 * lax.axis_index("y") + lax.axis_index("c")


def _exchange(name, arrays, scatter):
    n = len(arrays)

    def body(*refs):
        ins, outs = refs[:n], refs[n:2 * n]
        send_sems, recv_sems, local_sems = refs[2 * n:]
        me = _my_id()
        started = []
        for a in range(n):
            src = ins[a].at[me] if scatter[a] else ins[a]
            loc = pltpu.make_async_copy(src, outs[a].at[me], local_sems.at[a])
            loc.start()
            started.append(loc)
        remote = []
        for k in range(1, N_DEV):
            to = (me + k) % N_DEV
            for a in range(n):
                src = ins[a].at[to] if scatter[a] else ins[a]
                cp = pltpu.make_async_remote_copy(src_ref=src, dst_ref=outs[a].at[me],
                                                  send_sem=send_sems.at[a * (N_DEV - 1) + k - 1], recv_sem=recv_sems.at[a * (N_DEV - 1) + k - 1],
                                                  device_id=_peer(me, k), device_id_type=pl.DeviceIdType.MESH)
                cp.start()
                remote.append(cp)
        for k in range(1, N_DEV):
            frm = (me + N_DEV - k) % N_DEV
            for a in range(n):
                src = ins[a].at[frm] if scatter[a] else ins[a]
                pltpu.make_async_remote_copy(src_ref=src, dst_ref=outs[a].at[frm],
                                             send_sem=send_sems.at[a * (N_DEV - 1) + k - 1], recv_sem=recv_sems.at[a * (N_DEV - 1) + k - 1],
                                             device_id=_peer(me, k), device_id_type=pl.DeviceIdType.MESH).wait_recv()
        for cp in remote:
            cp.wait_send()
        for loc in started:
            loc.wait()

    out_shape = [_sds((N_DEV,) + (a.shape[1:] if sc else a.shape), a.dtype) for a, sc in zip(arrays, scatter)]
    return pl.pallas_call(
        body, in_specs=[ANY] * n, out_specs=[ANY] * n, out_shape=out_shape,
        scratch_shapes=[pltpu.SemaphoreType.DMA((n * (N_DEV - 1),)), pltpu.SemaphoreType.DMA((n * (N_DEV - 1),)),
                        pltpu.SemaphoreType.DMA((n,))],
        name=name)(*arrays)


HBM = pl.BlockSpec(memory_space=pltpu.HBM)
SEM = pl.BlockSpec(memory_space=pltpu.SEMAPHORE)
EFFECT = pltpu.SideEffectType.DATAFLOW_SIDE_EFFECTING


def _remote_copies(srcs, lands, scatter, send_sems, recv_sems, me, incoming):
    out = []
    for k in range(1, N_DEV):
        other = (me + N_DEV - k) % N_DEV if incoming else (me + k) % N_DEV
        for a in range(len(srcs)):
            sem = a * (N_DEV - 1) + k - 1
            src = srcs[a].at[other] if scatter[a] else srcs[a]
            dst = lands[a].at[other if incoming else me]
            out.append(pltpu.make_async_remote_copy(src_ref=src, dst_ref=dst, send_sem=send_sems.at[sem],
                                                    recv_sem=recv_sems.at[sem], device_id=_peer(me, k),
                                                    device_id_type=pl.DeviceIdType.MESH))
    return out


def _exchange_start(name, arrays, scatter):
    n = len(arrays)
    lands = [lax.empty((N_DEV,) + (a.shape[1:] if sc else a.shape), a.dtype) for a, sc in zip(arrays, scatter)]

    def body(*refs):
        srcs, land_refs = refs[:n], refs[n:2 * n]
        send_sems, recv_sems = refs[2 * n], refs[2 * n + 1]
        token = refs[-1]
        for cp in _remote_copies(srcs, land_refs, scatter, send_sems, recv_sems, _my_id(), False):
            cp.start()
        token[...] = jnp.zeros_like(token)

    n_sem = n * (N_DEV - 1)
    out_shape = ([pltpu.SemaphoreType.DMA((n_sem,)), pltpu.SemaphoreType.DMA((n_sem,))]
                 + [pltpu.HBM(a.shape, a.dtype) for a in arrays] + [pltpu.HBM(l.shape, l.dtype) for l in lands]
                 + [_sds((8, 128), F32)])
    aliases = {i: 2 + i for i in range(2 * n)}
    args = [pltpu.with_memory_space_constraint(a, pltpu.HBM) for a in list(arrays) + lands]
    res = pl.pallas_call(
        body, name=name, in_specs=[HBM] * (2 * n), out_shape=out_shape,
        out_specs=[SEM, SEM] + [HBM] * (2 * n) + [pl.BlockSpec(memory_space=pltpu.VMEM)],
        input_output_aliases=aliases, compiler_params=pltpu.CompilerParams(has_side_effects=EFFECT))(*args)
    return dict(send=res[0], recv=res[1], srcs=res[2:2 + n], lands=res[2 + n:2 + 2 * n], token=res[-1],
                scatter=scatter)


def _exchange_wait(name, started, after):
    n = len(started["srcs"])
    scatter = started["scatter"]

    def body(*refs):
        srcs, land_refs = refs[:n], refs[n:2 * n]
        send_sems, recv_sems = refs[2 * n], refs[2 * n + 1]
        me = _my_id()
        for cp in _remote_copies(srcs, land_refs, scatter, send_sems, recv_sems, me, False):
            cp.wait_send()
        for cp in _remote_copies(srcs, land_refs, scatter, send_sems, recv_sems, me, True):
            cp.wait_recv()

    arrs = list(started["srcs"]) + list(started["lands"])
    res = pl.pallas_call(
        body, name=name, in_specs=[HBM] * (2 * n) + [SEM, SEM, ANY],
        out_shape=[pltpu.HBM(a.shape, a.dtype) for a in arrs], out_specs=[HBM] * (2 * n),
        input_output_aliases={i: i for i in range(2 * n)},
        compiler_params=pltpu.CompilerParams(has_side_effects=EFFECT))(*arrs, started["send"], started["recv"], after)
    me = _my_id()
    out = []
    for src, land, sc in zip(res[:n], res[n:], scatter):
        own = lax.dynamic_index_in_dim(src, me, 0, keepdims=True) if sc else src[None]
        out.append(lax.dynamic_update_slice(land, own, (me,) + (0,) * (land.ndim - 1)))
    return out


def _after(token, value):
    return lax.optimization_barrier((token, value))[1]


def _adamw(name, parts, w, m, v):
    r, c = w.shape
    tr = r
    for cand in (128, 88, 64, 40, 8):
        if r % cand == 0:
            tr = cand
            break
    c1 = 1.0 / (1.0 - ADAM_B1 ** ADAM_STEP)
    c2 = 1.0 / (1.0 - ADAM_B2 ** ADAM_STEP)

    def body(p_ref, w_ref, m_ref, v_ref, g_ref, d_ref, nm_ref, nv_ref):
        g = p_ref[0]
        for s in range(1, N_DEV):
            g = g + p_ref[s]
        mn = ADAM_B1 * m_ref[...] + (1.0 - ADAM_B1) * g
        vn = ADAM_B2 * v_ref[...] + (1.0 - ADAM_B2) * (g * g)
        g_ref[...] = g
        nm_ref[...] = mn
        nv_ref[...] = vn
        d_ref[...] = -ADAM_LR * ((mn * c1) / (jnp.sqrt(vn * c2) + ADAM_EPS) + ADAM_WD * w_ref[...])

    blk = pl.BlockSpec((tr, c), lambda i: (i, 0))
    return pl.pallas_call(
        body, grid=(r // tr,), in_specs=[pl.BlockSpec((N_DEV, tr, c), lambda i: (0, i, 0)), blk, blk, blk],
        out_specs=[blk] * 4, out_shape=[_sds((r, c), F32)] * 4, name=name,
        compiler_params=_params(1, VMEM_LIMIT))(parts, w, m, v)


def _pad_rows(a, rows):
    return jnp.pad(a, ((0, rows - a.shape[0]), (0, 0)))


def _lane_row(vec8, offset):
    return jnp.pad(vec8.reshape(1, 8), ((0, 0), (offset, HD - 8 - offset)))


def kernel(x, positions, attn_norm_w, w_in, conv_w, a_log, dt_bias, delta_out_norm_w, q_norm_w, k_norm_w, attn_out_norm_w, w_out, ffn_norm_w, w_gate_up, w_down, loss_target, m_attn_norm_w, m_w_in, m_conv_w, m_a_log, m_dt_bias, m_delta_out_norm_w, m_q_norm_w, m_k_norm_w, m_attn_out_norm_w, m_w_out, m_ffn_norm_w, m_w_gate_up, m_w_down, v_attn_norm_w, v_w_in, v_conv_w, v_a_log, v_dt_bias, v_delta_out_norm_w, v_q_norm_w, v_k_norm_w, v_attn_out_norm_w, v_w_out, v_ffn_norm_w, v_w_gate_up, v_w_down):
    x2 = x[0]
    t, d = x2.shape
    target = loss_target[0]
    pos_col = positions.reshape(t, 1)
    half = HD // 2
    inv = (ROPE_THETA ** (-np.arange(half, dtype=np.float32) / half)).astype(np.float32)
    inv_row = jnp.asarray(np.concatenate([inv, inv]).reshape(1, HD))

    n_in = w_in.shape[2]
    n_gu = w_gate_up.shape[2]
    w_in_g, conv_g = _exchange("gather_in", [w_in[0].astype(BF16), _pad_rows(conv_w[0], 8)], [False] * 2)
    ffn_own = [_after(conv_g, a) for a in (w_gate_up[0].astype(BF16), w_down[0].astype(BF16), w_out[0].astype(BF16))]
    ffn_fly = _exchange_start("gather_ffn_start", ffn_own, [False] * 3)
    w_in_full = jnp.transpose(w_in_g, (1, 0, 2)).reshape(d, N_DEV * n_in)
    n_main = 4 * GW
    n_small = 2 * N_HEADS
    w_cat = jnp.concatenate([w_in_full[:, :n_main], w_in_full[:, n_main + n_small:],
                             w_in_full[:, n_main:n_main + n_small],
                             jnp.zeros((d, HD - n_small), BF16)], axis=1)
    n_cat = w_cat.shape[1]
    small_blk = (7 * GW) // HD
    conv_w8 =jnp.transpose(conv_g, (1, 0, 2)).reshape(8, 3 * GW)
    alog_row = _lane_row(a_log[0], 8)
    dtb_row = _lane_row(dt_bias[0], 8)

    tm = min(2048, t)
    h1 = _rms_fwd("norm1", _after(ffn_fly["token"], x2), attn_norm_w)
    tn = 384
    proj = _mm("in_proj", h1, w_cat, grid=(t // tm, n_cat // tn, 1),
               a_spec=pl.BlockSpec((tm, d), lambda i, j, k: (i, 0)),
               b_spec=pl.BlockSpec((d, tn), lambda i, j, k: (0, j)),
               o_spec=pl.BlockSpec((tm, tn), lambda i, j, k: (i, j)),
               out_shape=_sds((t, n_cat), F32), ca=1, cb=0, nk=1)
    qn = _conv_fwd("conv_q", proj, conv_w8, 0, True, HD ** -0.5)
    kn = _conv_fwd("conv_k", proj, conv_w8, 1, True, 1.0)
    vv = _conv_fwd("conv_v", proj, conv_w8, 2, False, 1.0)
    beta_b, gc_b = _gates_fwd("gates", proj, small_blk, alog_row, dtb_row)
    u, w, p, tinv, qd, kd = _delta_prep("delta_prep", qn, kn, vv, beta_b, gc_b)
    oa_raw, vn, s_hist = _delta_scan("delta_scan", u, w, p, qd, kd, gc_b)

    aq = _qk_fwd("attn_q", proj, 4, q_norm_w, pos_col, inv_row, True)
    ak = _qk_fwd("attn_k", proj, 5, k_norm_w, pos_col, inv_row, True)
    av = _qk_fwd("attn_v", proj, 6, q_norm_w, pos_col, inv_row, False)
    branches = [_swa_fwd("swa_fwd_%d" % r, aq, ak, av, r) for r in DILATIONS]
    (o1, l1), (o2, l2), (o3, l3) = branches
    mixed, ob, lse = _mix_fwd("mix", oa_raw, proj, 3, o1, o2, o3, l1, l2, l3, delta_out_norm_w, attn_out_norm_w)
    w_gu_g, w_down_g, w_out_g = _exchange_wait("gather_ffn_wait", ffn_fly, mixed)
    w_down_full = w_down_g.reshape(D_FF, d)
    w_out_full = w_out_g.reshape(2 * GW, d)
    tn = 512
    x1 = _mm("out_proj", mixed, w_out_full, grid=(t // tm, d // tn, 1),
             a_spec=pl.BlockSpec((tm, 2 * GW), lambda i, j, k: (i, 0)),
             b_spec=pl.BlockSpec((2 * GW, tn), lambda i, j, k: (0, j)),
             o_spec=pl.BlockSpec((tm, tn), lambda i, j, k: (i, j)),
             add=x2, add_spec=pl.BlockSpec((tm, tn), lambda i, j, k: (i, j)),
             out_shape=_sds((t, d), F32), ca=1, cb=0, nk=1)
    h2 = _rms_fwd("norm2", x1, ffn_norm_w)
    per = N_DEV // 2
    tmd = min(1024, t)
    gu3 = _mm("gate_up", h2, w_gu_g, grid=(t // tmd, N_DEV, 1),
              a_spec=pl.BlockSpec((tmd, d), lambda i, j, k: (i, 0)),
              b_spec=pl.BlockSpec((None, d, n_gu), lambda i, j, k: (j, 0, 0)),
              o_spec=pl.BlockSpec((None, tmd, n_gu), lambda i, j, k: (j // per, i, j % per)),
              out_shape=_sds((2, t, D_FF), F32), ca=1, cb=0, nk=1)
    act = _swiglu_fwd("swiglu", gu3)
    tmd, tkd = min(1024, t), D_FF // 2
    y = _mm("down_proj", act, w_down_full, grid=(t // tmd, d // tn, 2),
            a_spec=pl.BlockSpec((tmd, tkd), lambda i, j, k: (i, k)),
            b_spec=pl.BlockSpec((tkd, tn), lambda i, j, k: (k, j)),
            o_spec=pl.BlockSpec((tmd, tn), lambda i, j, k: (i, j)),
            add=x1, add_spec=pl.BlockSpec((tmd, tn), lambda i, j, k: (i, j)),
            out_shape=_sds((t, d), F32), ca=1, cb=0, nk=2)
    dy, loss_tile = _loss_head("loss_head", y, target)
    loss = lax.psum(loss_tile[0, 0], ("x", "y", "c"))

    dy16 = dy.astype(BF16)
    dact = _mm("d_act", dy16, w_down_full, grid=(t // tm, D_FF // tn, 1),
               a_spec=pl.BlockSpec((tm, d), lambda i, j, k: (i, 0)),
               b_spec=pl.BlockSpec((tn, d), lambda i, j, k: (j, 0)),
               o_spec=pl.BlockSpec((tm, tn), lambda i, j, k: (i, j)),
               out_shape=_sds((t, D_FF), F32), ca=1, cb=1, nk=1)
    tk = min(2048, t)
    nkt = t // tk
    g_down = _mm("g_down", act, dy16, grid=(D_FF // 512, 1, nkt),
                 a_spec=pl.BlockSpec((tk, 512), lambda i, j, k: (k, i)),
                 b_spec=pl.BlockSpec((tk, d), lambda i, j, k: (k, 0)),
                 o_spec=pl.BlockSpec((512, d), lambda i, j, k: (i, 0)),
                 out_shape=_sds((D_FF, d), F32), ca=0, cb=0, nk=nkt)
    dgu3 = _swiglu_bwd("swiglu_bwd", gu3, dact)
    dh2 = _mm("d_h2", dgu3, w_gu_g, grid=(t // tmd, d // tn, N_DEV),
              a_spec=pl.BlockSpec((None, tmd, n_gu), lambda i, j, k: (k // per, i, k % per)),
              b_spec=pl.BlockSpec((None, tn, n_gu), lambda i, j, k: (k, j, 0)),
              o_spec=pl.BlockSpec((tmd, tn), lambda i, j, k: (i, j)),
              out_shape=_sds((t, d), F32), ca=1, cb=1, nk=N_DEV)
    g_gu = _mm("g_gate_up", h2, dgu3, grid=(d // 512, N_DEV, nkt),
               a_spec=pl.BlockSpec((tk, 512), lambda i, j, k: (k, i)),
               b_spec=pl.BlockSpec((None, tk, n_gu), lambda i, j, k: (j // per, k, j % per)),
               o_spec=pl.BlockSpec((None, 512, n_gu), lambda i, j, k: (j, i, 0)),
               out_shape=_sds((N_DEV, d, n_gu), F32), ca=0, cb=0, nk=nkt)
    dx1, g_ffn_norm = _rms_bwd("norm2_bwd", x1, ffn_norm_w, dh2, dy)

    dx1_16 = dx1.astype(BF16)
    dmixed = _mm("d_mixed", dx1_16, w_out_full, grid=(t // tm, (2 * GW) // tn, 1),
                 a_spec=pl.BlockSpec((tm, d), lambda i, j, k: (i, 0)),
                 b_spec=pl.BlockSpec((tn, d), lambda i, j, k: (j, 0)),
                 o_spec=pl.BlockSpec((tm, tn), lambda i, j, k: (i, j)),
                 out_shape=_sds((t, 2 * GW), F32), ca=1, cb=1, nk=1)
    g_out = _mm("g_out", mixed, dx1_16, grid=((2 * GW) // 512, 1, nkt),
                a_spec=pl.BlockSpec((tk, 512), lambda i, j, k: (k, i)),
                b_spec=pl.BlockSpec((tk, d), lambda i, j, k: (k, 0)),
                o_spec=pl.BlockSpec((512, d), lambda i, j, k: (i, 0)),
                out_shape=_sds((2 * GW, d), F32), ca=0, cb=0, nk=nkt)
    ffn_g_fly = _exchange_start("reduce_ffn_start",
                                [g_gu, g_down.reshape(N_DEV, D_FF // N_DEV, d), g_out.reshape(N_DEV, (2 * GW) // N_DEV, d)],
                                [True] * 3)
    doa, dz, dob, delta, g_dn, g_an = _mix_bwd("mix_bwd", _after(ffn_g_fly["token"], dmixed), oa_raw, proj, 3, ob,
                                               delta_out_norm_w, attn_out_norm_w)
    grads = [_swa_bwd("swa_bwd_%d" % r, aq, ak, av, dob, lse, delta, r) for r in DILATIONS]
    daq, g_qn = _qk_bwd("attn_q_bwd", proj, 4, q_norm_w, pos_col, inv_row, grads[0][0], grads[1][0], grads[2][0], True)
    dak, g_kn = _qk_bwd("attn_k_bwd", proj, 5, k_norm_w, pos_col, inv_row, grads[0][1], grads[1][1], grads[2][1], True)
    dav, _ = _qk_bwd("attn_v_bwd", proj, 6, q_norm_w, pos_col, inv_row, grads[0][2], grads[1][2], grads[2][2], False)

    dvn, dqd, dkd, dw, ddec = _delta_scan_bwd("delta_scan_bwd", doa, w, p, qd, kd, gc_b, vn, s_hist)
    dqn, dkn, dvv, dbeta_b, dg_b = _delta_prep_bwd("delta_prep_bwd", qn, kn, vv, beta_b, gc_b, tinv, u, w, vn,
                                                   doa, dvn, dqd, dkd, dw, ddec)
    dxq, gcw_q = _conv_bwd("conv_q_bwd", proj, conv_w8, dqn, 0, True, HD ** -0.5)
    dxk, gcw_k = _conv_bwd("conv_k_bwd", proj, conv_w8, dkn, 1, True, 1.0)
    dxv, gcw_v = _conv_bwd("conv_v_bwd", proj, conv_w8, dvv, 2, False, 1.0)
    dsmall, g_alog_row, g_dtb_row = _gates_bwd("gates_bwd", proj, small_blk, alog_row, dtb_row, dbeta_b, dg_b)
    dproj = jnp.concatenate([dxq, dxk, dxv, dz, daq, dak, dav, dsmall], axis=1)
    tnc = n_cat // 3
    g_cat = _mm("g_in", h1, dproj, grid=(d // 512, 3, nkt),
                a_spec=pl.BlockSpec((tk, 512), lambda i, j, k: (k, i)),
                b_spec=pl.BlockSpec((tk, tnc), lambda i, j, k: (k, j)),
                o_spec=pl.BlockSpec((512, tnc), lambda i, j, k: (i, j)),
                out_shape=_sds((d, n_cat), F32), ca=0, cb=0, nk=nkt)
    g_in_full = jnp.concatenate([g_cat[:, :n_main], g_cat[:, 7 * GW:7 * GW + n_small], g_cat[:, n_main:7 * GW]], axis=1)
    g_in_parts = jnp.transpose(g_in_full.reshape(d, N_DEV, n_in), (1, 0, 2))
    g_conv = jnp.concatenate([gcw_q, gcw_k, gcw_v], axis=1)
    n_cw = conv_w.shape[2]
    g_conv_parts = jnp.transpose(g_conv.reshape(8, N_DEV, n_cw), (1, 0, 2))
    in_g_fly = _exchange_start("reduce_in_start", [g_in_parts, g_conv_parts], [True] * 2)
    tkc = n_cat // 3
    dh1 = _mm("d_h1", _after(in_g_fly["token"], dproj), w_cat, grid=(t // tmd, d // tn, 3),
              a_spec=pl.BlockSpec((tmd, tkc), lambda i, j, k: (i, k)),
              b_spec=pl.BlockSpec((tn, tkc), lambda i, j, k: (j, k)),
              o_spec=pl.BlockSpec((tmd, tn), lambda i, j, k: (i, j)),
              out_shape=_sds((t, d), F32), ca=1, cb=1, nk=3)
    grad_x, g_attn_norm = _rms_bwd("norm1_bwd", x2, attn_norm_w, dh1, dx1)

    small_rows = [g_attn_norm.reshape(d // HD, HD), g_ffn_norm.reshape(d // HD, HD), g_dn, g_qn, g_kn, g_an,
                  g_alog_row, g_dtb_row]
    small_pack = _pad_rows(jnp.concatenate(small_rows, axis=0), 40)
    (r_small,) = _exchange("gather_small_grads", [small_pack], [False])

    def pack_small(an, fn, dn, qn_, kn_, aon, al, db):
        rows = [an.reshape(d // HD, HD), fn.reshape(d // HD, HD), dn, qn_, kn_, aon,
                _lane_row(al[0], 8), _lane_row(db[0], 8)]
        return _pad_rows(jnp.concatenate(rows, axis=0), 40)

    def unpack_small(pk):
        nr = d // HD
        return dict(attn_norm_w=pk[:nr].reshape(1, d), ffn_norm_w=pk[nr:2 * nr].reshape(1, d),
                    delta_out_norm_w=pk[2 * nr:2 * nr + 1], q_norm_w=pk[2 * nr + 1:2 * nr + 2],
                    k_norm_w=pk[2 * nr + 2:2 * nr + 3], attn_out_norm_w=pk[2 * nr + 3:2 * nr + 4],
                    a_log=pk[2 * nr + 4:2 * nr + 5, 8:16], dt_bias=pk[2 * nr + 5:2 * nr + 6, 8:16])

    res_small = _adamw("adamw_small", r_small,
                       pack_small(attn_norm_w, ffn_norm_w, delta_out_norm_w, q_norm_w, k_norm_w, attn_out_norm_w, a_log, dt_bias),
                       pack_small(m_attn_norm_w, m_ffn_norm_w, m_delta_out_norm_w, m_q_norm_w, m_k_norm_w, m_attn_out_norm_w, m_a_log, m_dt_bias),
                       pack_small(v_attn_norm_w, v_ffn_norm_w, v_delta_out_norm_w, v_q_norm_w, v_k_norm_w, v_attn_out_norm_w, v_a_log, v_dt_bias))
    small = [unpack_small(a) for a in res_small]
    r_gu, r_down, r_out = _exchange_wait("reduce_ffn_wait", ffn_g_fly, res_small[0])
    res_gu = [a[None] for a in _adamw("adamw_gate_up", r_gu, w_gate_up[0], m_w_gate_up[0], v_w_gate_up[0])]
    res_down = [a[None] for a in _adamw("adamw_down", r_down, w_down[0], m_w_down[0], v_w_down[0])]
    res_out = [a[None] for a in _adamw("adamw_out", r_out, w_out[0], m_w_out[0], v_w_out[0])]
    r_in, r_conv = _exchange_wait("reduce_in_wait", in_g_fly, res_out[0])
    res_in = [a[None] for a in _adamw("adamw_in", r_in, w_in[0], m_w_in[0], v_w_in[0])]
    res_conv =[a[None, :4] for a in _adamw("adamw_conv", r_conv, _pad_rows(conv_w[0], 8), _pad_rows(m_conv_w[0], 8),
                                            _pad_rows(v_conv_w[0], 8))]

    outs = [loss, grad_x[None]]
    for i in range(4):
        s = small[i]
        outs += [s["attn_norm_w"], res_in[i], res_conv[i], s["a_log"], s["dt_bias"], s["delta_out_norm_w"],
                 s["q_norm_w"], s["k_norm_w"], s["attn_out_norm_w"], res_out[i], s["ffn_norm_w"], res_gu[i],
                 res_down[i]]
    return tuple(outs)
```

```python
import functools

import numpy as np
import jax
import jax.numpy as jnp
from jax import lax
from jax.experimental import pallas as pl
from jax.experimental.pallas import tpu as pltpu

F32 = jnp.float32
BF16 = jnp.bfloat16

N_DEV = 8
N_HEADS = 8
HD = 128
GW = N_HEADS * HD
CHUNK = 64
PAIR = 2 * CHUNK
SPAN = 128
DILATIONS = (1, 4, 16)
ROPE_THETA = 10000.0
EPS = 1e-6
D_FF = 5632
ADAM_LR, ADAM_B1, ADAM_B2, ADAM_EPS, ADAM_WD, ADAM_STEP = 0.001, 0.9, 0.999, 1e-8, 0.01, 10
NEG = -1e30
VMEM_LIMIT = 56 * 1024 * 1024
ANY = pl.BlockSpec(memory_space=pl.ANY)
HI = lax.Precision.HIGHEST


def _params(n_grid, vmem=VMEM_LIMIT):
    return pltpu.CompilerParams(dimension_semantics=("arbitrary",) * n_grid, vmem_limit_bytes=vmem)


def _sds(shape, dtype):
    return jax.ShapeDtypeStruct(tuple(shape), dtype)


def _sigmoid(x):
    return 1.0 / (1.0 + jnp.exp(-x))


def _silu(x):
    return x * _sigmoid(x)


def _softplus(x):
    return jnp.maximum(x, 0.0) + jnp.log(1.0 + jnp.exp(-jnp.abs(x)))


def _dot(a, b, ca, cb, precision=None):
    return lax.dot_general(a, b, (((ca,), (cb,)), ((), ())), precision=precision,
                           preferred_element_type=F32)


def _b16(x):
    return x if x.dtype == BF16 else x.astype(BF16)


def _iota2(shape, axis):
    return lax.broadcasted_iota(jnp.int32, shape, axis)


def _mm(name, a, b, *, grid, a_spec, b_spec, o_spec, out_shape, ca, cb, nk, add=None, add_spec=None,
        dep=None, vmem=VMEM_LIMIT):
    has_add = add is not None
    n_in = 2 + has_add + (dep is not None)

    def body(*refs):
        a_ref, b_ref = refs[0], refs[1]
        e_ref = refs[2] if has_add else None
        o_ref = refs[n_in]
        part = _dot(_b16(a_ref[...]), _b16(b_ref[...]), ca, cb)
        if nk == 1:
            if has_add:
                part = part + e_ref[...]
            o_ref[...] = part.astype(o_ref.dtype)
            return
        acc = refs[-1]
        k = pl.program_id(2)

        @pl.when(k == 0)
        def _():
            acc[...] = part

        @pl.when(k > 0)
        def _():
            acc[...] += part

        @pl.when(k == nk - 1)
        def _():
            res = acc[...]
            if has_add:
                res = res + e_ref[...]
            o_ref[...] = res.astype(o_ref.dtype)

    in_specs = [a_spec, b_spec] + ([add_spec] if has_add else []) + ([ANY] if dep is not None else [])
    args = (a, b) + ((add,) if has_add else ()) + ((dep,) if dep is not None else ())
    blk = [d for d in o_spec.block_shape if d is not None]
    scratch = [pltpu.VMEM(tuple(blk), F32)] if nk > 1 else []
    return pl.pallas_call(body, grid=grid, in_specs=in_specs, out_specs=o_spec, out_shape=out_shape,
                          scratch_shapes=scratch, name=name, compiler_params=_params(3, vmem))(*args)


def _rms_f(xv, wv):
    return xv * lax.rsqrt(jnp.mean(xv * xv, axis=-1, keepdims=True) + EPS) * wv


def _rms_fwd(name, x, w, dep):
    t, d = x.shape
    tm = min(512, t)

    def body(x_ref, w_ref, dep_ref, o_ref):
        o_ref[...] = _rms_f(x_ref[...], w_ref[...]).astype(BF16)

    row = pl.BlockSpec((tm, d), lambda i: (i, 0))
    vec = pl.BlockSpec((1, d), lambda i: (0, 0))
    return pl.pallas_call(body, grid=(t // tm,), in_specs=[row, vec, ANY], out_specs=row,
                          out_shape=_sds((t, d), BF16), name=name, compiler_params=_params(1))(x, w, dep)


def _rms_bwd(name, x, w, dh, res):
    t, d = x.shape
    tm = min(256, t)

    def body(x_ref, w_ref, dh_ref, res_ref, dx_ref, dw_ref):
        _, vjp = jax.vjp(_rms_f, x_ref[...], w_ref[...])
        dxv, dwv = vjp(dh_ref[...])
        dx_ref[...] = dxv + res_ref[...]

        @pl.when(pl.program_id(0) == 0)
        def _():
            dw_ref[...] = jnp.zeros_like(dw_ref)

        dw_ref[...] += dwv

    row = pl.BlockSpec((tm, d), lambda i: (i, 0))
    vec = pl.BlockSpec((1, d), lambda i: (0, 0))
    return pl.pallas_call(body, grid=(t // tm,), in_specs=[row, vec, row, row], out_specs=[row, vec],
                          out_shape=[_sds((t, d), F32), _sds((1, d), F32)], name=name,
                          compiler_params=_params(1))(x, w, dh, res)


def _conv_taps(xv, w_ref, rows):
    c = w_ref[3:4, :] * xv
    for s in (1, 2, 3):
        c = c + w_ref[3 - s:4 - s, :] * jnp.where(rows >= s, pltpu.roll(xv, s, 0), 0.0)
    return c


def _post_conv(c, l2, scale):
    y = _silu(c)
    if l2:
        y = y * lax.rsqrt(jnp.sum(y * y, axis=-1, keepdims=True) + EPS) * scale
    return y


def _conv_fwd(name, proj, conv_w8, group, l2, scale):
    t = proj.shape[0]

    def body(x_ref, w_ref, o_ref):
        rows = _iota2((t, HD), 0)
        o_ref[...] = _post_conv(_conv_taps(x_ref[...], w_ref, rows), l2, scale)

    return pl.pallas_call(
        body, grid=(N_HEADS,),
        in_specs=[pl.BlockSpec((t, HD), lambda h: (0, h + group * N_HEADS)),
                  pl.BlockSpec((8, HD), lambda h: (0, h + group * N_HEADS))],
        out_specs=pl.BlockSpec((t, HD), lambda h: (0, h)),
        out_shape=_sds((t, GW), F32), name=name, compiler_params=_params(1, VMEM_LIMIT))(proj, conv_w8)


def _conv_bwd(name, proj, conv_w8, dn, group, l2, scale):
    t = proj.shape[0]

    def body(x_ref, w_ref, dn_ref, dx_ref, dw_ref):
        rows = _iota2((t, HD), 0)
        xv = x_ref[...]
        c = _conv_taps(xv, w_ref, rows)
        _, vjp = jax.vjp(lambda cc: _post_conv(cc, l2, scale), c)
        (dc,) = vjp(dn_ref[...])
        dx = w_ref[3:4, :] * dc
        dw = jnp.zeros((8, HD), F32)
        rid = _iota2((8, HD), 0)
        dw = dw + jnp.where(rid == 3, jnp.sum(dc * xv, axis=0, keepdims=True), 0.0)
        for s in (1, 2, 3):
            dx = dx + w_ref[3 - s:4 - s, :] * jnp.where(rows < t - s, pltpu.roll(dc, t - s, 0), 0.0)
            xs = jnp.where(rows >= s, pltpu.roll(xv, s, 0), 0.0)
            dw = dw + jnp.where(rid == 3 - s, jnp.sum(dc * xs, axis=0, keepdims=True), 0.0)
        dx_ref[...] = dx.astype(BF16)
        dw_ref[...] = dw

    return pl.pallas_call(
        body, grid=(N_HEADS,),
        in_specs=[pl.BlockSpec((t, HD), lambda h: (0, h + group * N_HEADS)),
                  pl.BlockSpec((8, HD), lambda h: (0, h + group * N_HEADS)),
                  pl.BlockSpec((t, HD), lambda h: (0, h))],
        out_specs=[pl.BlockSpec((t, HD), lambda h: (0, h)), pl.BlockSpec((8, HD), lambda h: (0, h))],
        out_shape=[_sds((t, GW), BF16), _sds((8, GW), F32)], name=name,
        compiler_params=_params(1, VMEM_LIMIT))(proj, conv_w8, dn)


def _chunk_cumsum(g, rows):
    pos = rows % CHUNK
    s = 1
    while s < CHUNK:
        g = g + jnp.where(pos >= s, pltpu.roll(g, s, 0), 0.0)
        s *= 2
    return g


def _gates_fwd(name, proj, small_blk, alog_row, dtb_row):
    t = proj.shape[0]
    tm = min(256, t)

    def body(s_ref, a_ref, b_ref, beta_ref, gc_ref):
        sm = s_ref[...]
        beta = _sigmoid(sm)
        g = -jnp.exp(a_ref[...]) * _softplus(sm + b_ref[...])
        gc = _chunk_cumsum(g, _iota2((tm, HD), 0))
        lane = _iota2((tm, HD), 1)
        for h in range(N_HEADS):
            bcol = jnp.sum(jnp.where(lane == h, beta, 0.0), axis=1, keepdims=True)
            gcol = jnp.sum(jnp.where(lane == 8 + h, gc, 0.0), axis=1, keepdims=True)
            beta_ref[:, h * HD:(h + 1) * HD] = jnp.broadcast_to(bcol, (tm, HD))
            gc_ref[:, h * HD:(h + 1) * HD] = jnp.broadcast_to(gcol, (tm, HD))

    vec = pl.BlockSpec((1, HD), lambda i: (0, 0))
    wide = pl.BlockSpec((tm, GW), lambda i: (i, 0))
    return pl.pallas_call(
        body, grid=(t // tm,),
        in_specs=[pl.BlockSpec((tm, HD), lambda i: (i, small_blk)), vec, vec], out_specs=[wide, wide],
        out_shape=[_sds((t, GW), F32), _sds((t, GW), F32)], name=name,
        compiler_params=_params(1))(proj, alog_row, dtb_row)


def _gates_bwd(name, proj, small_blk, alog_row, dtb_row, dbeta_b, dg_b):
    t = proj.shape[0]
    tm = min(256, t)

    def body(s_ref, a_ref, b_ref, db_ref, dg_ref, ds_ref, da_ref, dbias_ref):
        sm = s_ref[...]
        lane = _iota2((tm, HD), 1)
        db = jnp.zeros((tm, HD), F32)
        dg = jnp.zeros((tm, HD), F32)
        for h in range(N_HEADS):
            db = db + jnp.where(lane == h, db_ref[:, h * HD:(h + 1) * HD], 0.0)
            dg = dg + jnp.where(lane == 8 + h, dg_ref[:, h * HD:(h + 1) * HD], 0.0)
        beta = _sigmoid(sm)
        ea = jnp.exp(a_ref[...])
        pre = sm + b_ref[...]
        g = -ea * _softplus(pre)
        dpre = dg * (-ea) * _sigmoid(pre)
        ds_ref[...] = (db * beta * (1.0 - beta) + dpre).astype(BF16)

        @pl.when(pl.program_id(0) == 0)
        def _():
            da_ref[...] = jnp.zeros_like(da_ref)
            dbias_ref[...] = jnp.zeros_like(dbias_ref)

        da_ref[...] += jnp.sum(dg * g, axis=0, keepdims=True)
        dbias_ref[...] += jnp.sum(dpre, axis=0, keepdims=True)

    vec = pl.BlockSpec((1, HD), lambda i: (0, 0))
    wide = pl.BlockSpec((tm, GW), lambda i: (i, 0))
    return pl.pallas_call(
        body, grid=(t // tm,),
        in_specs=[pl.BlockSpec((tm, HD), lambda i: (i, small_blk)), vec, vec, wide, wide],
        out_specs=[pl.BlockSpec((tm, HD), lambda i: (i, 0)), vec, vec],
        out_shape=[_sds((t, HD), BF16), _sds((1, HD), F32), _sds((1, HD), F32)], name=name,
        compiler_params=_params(1))(proj, alog_row, dtb_row, dbeta_b, dg_b)


def _pair_masks():
    ii = _iota2((PAIR, PAIR), 0)
    jj = _iota2((PAIR, PAIR), 1)
    same = (ii // CHUNK) == (jj // CHUNK)
    return ii, jj, same & (ii >= jj), same & (ii > jj)


def _to_row(col_b, ii, jj):
    return jnp.sum(jnp.where(ii == jj, col_b, 0.0), axis=0, keepdims=True)


def _to_col(row, ii, jj):
    return jnp.sum(jnp.where(ii == jj, jnp.broadcast_to(row, (PAIR, PAIR)), 0.0), axis=1, keepdims=True)


def _decay_parts(gc, last_a, last_b, ii, jj, causal):
    diff = gc - _to_row(gc, ii, jj)
    dmat = jnp.where(causal, jnp.exp(jnp.where(causal, diff, 0.0)), 0.0)
    glast = jnp.where(ii < CHUNK, last_a, last_b)
    return dmat, jnp.exp(gc), jnp.exp(glast - gc)


def _unit_lower_inverse(low, ii, jj):
    eye = jnp.where(ii == jj, 1.0, 0.0)
    mm = lambda a, b: _dot(a, b, 1, 0, HI)
    d1 = jnp.where((ii // 16) == (jj // 16), low, 0.0)
    d2 = mm(d1, d1)
    d4 = mm(d2, d2)
    d8 = mm(d4, d4)
    td = mm(mm(mm(eye - d1, eye + d2), eye + d4), eye + d8)
    n1 = mm(td, low - d1)
    n2 = mm(n1, n1)
    return mm(mm(eye - n1, eye + n2), td)


def _delta_prep(name, qn, kn, vv, beta_b, gc_b):
    t = qn.shape[0]

    def body(q_ref, k_ref, v_ref, b_ref, g_ref, u_ref, w_ref, p_ref, t_ref, qd_ref, kd_ref):
        ii, jj, causal, strict = _pair_masks()
        q, k, v, beta, gc = q_ref[...], k_ref[...], v_ref[...], b_ref[...], g_ref[...]
        last_a, last_b = g_ref[CHUNK - 1:CHUNK, :], g_ref[PAIR - 1:PAIR, :]
        dmat, gam, e2 = _decay_parts(gc, last_a, last_b, ii, jj, causal)
        k16 = _b16(k)
        kk = _dot(k16, k16, 1, 1)
        low = jnp.where(strict, beta * kk * dmat, 0.0)
        tinv = _unit_lower_inverse(low, ii, jj)
        u_ref[...] = _dot(tinv, v * beta, 1, 0, HI)
        w_ref[...] = _dot(tinv, k * (beta * gam), 1, 0, HI).astype(BF16)
        p_ref[...] = jnp.where(causal, _dot(_b16(q), k16, 1, 1) * dmat, 0.0).astype(BF16)
        t_ref[...] = tinv
        qd_ref[...] = (q * gam).astype(BF16)
        kd_ref[...] = (k * e2).astype(BF16)

    blk = pl.BlockSpec((PAIR, HD), lambda i, h: (i, h))
    return pl.pallas_call(
        body, grid=(t // PAIR, N_HEADS), in_specs=[blk] * 5, out_specs=[blk] * 6,
        out_shape=[_sds((t, GW), F32), _sds((t, GW), BF16), _sds((t, GW), BF16), _sds((t, GW), F32),
                   _sds((t, GW), BF16), _sds((t, GW), BF16)],
        name=name, compiler_params=_params(2))(qn, kn, vv, beta_b, gc_b)


def _delta_scan(name, u, w, p, qd, kd, gc_b):
    t = u.shape[0]
    n = t // CHUNK

    def body(u_ref, w_ref, p_ref, qd_ref, kd_ref, g_ref, o_ref, vn_ref, sh_ref, state):
        h = pl.program_id(1)

        @pl.when(pl.program_id(0) == 0)
        def _():
            state[h] = jnp.zeros((HD, HD), F32)

        s = state[h]
        sh_ref[...] = s
        s16 = _b16(s)
        vnew = u_ref[...] - _dot(w_ref[...], s16, 1, 0)
        vn16 = _b16(vnew)
        vpair = jnp.concatenate([vn16, vn16], axis=0)
        o_ref[...] = _dot(qd_ref[...], s16, 1, 0) + _dot(p_ref[...], vpair, 1, 0)
        vn_ref[...] = vn16
        dec = jnp.exp(g_ref[CHUNK - 1:CHUNK, :])
        state[h] = s * dec + _dot(kd_ref[...], vn16, 0, 0)

    blk = pl.BlockSpec((CHUNK, HD), lambda i, h: (i, h))
    return pl.pallas_call(
        body, grid=(n, N_HEADS), in_specs=[blk] * 6,
        out_specs=[blk, blk, pl.BlockSpec((None, None, HD, HD), lambda i, h: (i, h, 0, 0))],
        out_shape=[_sds((t, GW), F32), _sds((t, GW), BF16), _sds((n, N_HEADS, HD, HD), F32)],
        scratch_shapes=[pltpu.VMEM((N_HEADS, HD, HD), F32)], name=name,
        compiler_params=_params(2))(u, w, p, qd, kd, gc_b)


def _delta_scan_bwd(name, do, w, p, qd, kd, gc_b, vn, s_hist):
    t = do.shape[0]
    n = t // CHUNK

    def body(do_ref, w_ref, p_ref, qd_ref, kd_ref, g_ref, vn_ref, sh_ref,
             dvn_ref, dqd_ref, dkd_ref, dw_ref, ddec_ref, dstate):
        h = pl.program_id(1)

        @pl.when(pl.program_id(0) == 0)
        def _():
            dstate[h] = jnp.zeros((HD, HD), F32)

        ds = dstate[h]
        ds16 = _b16(ds)
        s_in = sh_ref[...]
        s16 = _b16(s_in)
        do16 = _b16(do_ref[...])
        ptdo = _dot(p_ref[...], do16, 0, 0)
        dvn = ptdo[:CHUNK, :] + ptdo[CHUNK:, :] + _dot(kd_ref[...], ds16, 1, 0)
        dvn16 = _b16(dvn)
        dec = jnp.exp(g_ref[CHUNK - 1:CHUNK, :])
        dstate[h] = ds * dec + _dot(qd_ref[...], do16, 0, 0) - _dot(w_ref[...], dvn16, 0, 0)
        dvn_ref[...] = dvn
        dqd_ref[...] = _dot(do16, s16, 1, 1)
        dw_ref[...] = -_dot(dvn16, s16, 1, 1)
        dkd_ref[...] = _dot(vn_ref[...], ds16, 1, 1)
        tot = jnp.sum(jnp.sum(s_in * ds, axis=1, keepdims=True), axis=0, keepdims=True)
        ddec_ref[...] = jnp.broadcast_to(tot, (8, HD))

    blk = pl.BlockSpec((CHUNK, HD), lambda i, h: (n - 1 - i, h))
    return pl.pallas_call(
        body, grid=(n, N_HEADS),
        in_specs=[blk] * 7 + [pl.BlockSpec((None, None, HD, HD), lambda i, h: (n - 1 - i, h, 0, 0))],
        out_specs=[blk] * 4 + [pl.BlockSpec((8, HD), lambda i, h: (n - 1 - i, h))],
        out_shape=[_sds((t, GW), F32)] * 4 + [_sds((n * 8, GW), F32)],
        scratch_shapes=[pltpu.VMEM((N_HEADS, HD, HD), F32)], name=name,
        compiler_params=_params(2))(do, w, p, qd, kd, gc_b, vn, s_hist)


def _delta_prep_bwd(name, qn, kn, vv, beta_b, gc_b, tinv, u, w, vn, do, dvn, dqd, dkd, dw, ddec):
    t = qn.shape[0]

    def body(q_ref, k_ref, v_ref, b_ref, g_ref, t_ref, u_ref, w_ref, vn_ref, do_ref, dvn_ref, dqd_ref,
             dkd_ref, dw_ref, ddec_ref, dq_ref, dk_ref, dv_ref, dbeta_ref, dg_ref):
        ii, jj, causal, strict = _pair_masks()
        q, k, v, beta, gc = q_ref[...], k_ref[...], v_ref[...], b_ref[...], g_ref[...]
        last_a, last_b = g_ref[CHUNK - 1:CHUNK, :], g_ref[PAIR - 1:PAIR, :]
        dmat, gam, e2 = _decay_parts(gc, last_a, last_b, ii, jj, causal)
        q16, k16 = _b16(q), _b16(k)
        kk = _dot(k16, k16, 1, 1)
        qk = _dot(q16, k16, 1, 1)
        dqd, dkd = dqd_ref[...], dkd_ref[...]
        dp = jnp.where(causal, _dot(_b16(do_ref[...]), vn_ref[...], 1, 1), 0.0)
        dpd16 = _b16(dp * dmat)
        tinv_v = t_ref[...]
        x = _dot(tinv_v, dvn_ref[...], 0, 0, HI)
        y = _dot(tinv_v, dw_ref[...], 0, 0, HI)
        da = -jnp.where(strict, _dot(_b16(x), _b16(u_ref[...]), 1, 1) + _dot(_b16(y), w_ref[...], 1, 1), 0.0)
        dkk16 = _b16(da * beta * dmat)
        dq_ref[...] = gam * dqd + _dot(dpd16, k16, 1, 0)
        dk_ref[...] = (e2 * dkd + _dot(dpd16, q16, 0, 0) + beta * gam * y
                       + _dot(dkk16, k16, 1, 0) + _dot(dkk16, k16, 0, 0))
        dv_ref[...] = beta * x
        rs = lambda a: jnp.sum(a, axis=1, keepdims=True)
        dbeta = rs(v * x) + rs(k * gam * y) + rs(da * kk * dmat)
        dbeta_ref[...] = jnp.broadcast_to(dbeta, (PAIR, HD))
        m = (dp * qk + da * beta * kk) * dmat
        dgam = rs(q * dqd) + rs(k * beta * y)
        de2 = rs(k * dkd)
        colsum = _to_col(jnp.sum(m, axis=0, keepdims=True), ii, jj)
        te2 = de2 * e2
        dgc = rs(m) - colsum + gam * dgam - te2
        first = ii < CHUNK
        tail_a = jnp.sum(jnp.where(first, te2, 0.0), axis=0, keepdims=True)
        tail_b = jnp.sum(jnp.where(first, 0.0, te2), axis=0, keepdims=True)
        dgc = dgc + jnp.where(ii == CHUNK - 1, tail_a + ddec_ref[0:1, :] * jnp.exp(last_a), 0.0)
        dgc = dgc + jnp.where(ii == PAIR - 1, tail_b + ddec_ref[8:9, :] * jnp.exp(last_b), 0.0)
        dgc_row = _to_row(dgc, ii, jj)
        suffix = ((ii // CHUNK) == (jj // CHUNK)) & (jj >= ii)
        dg = jnp.sum(jnp.where(suffix, jnp.broadcast_to(dgc_row, (PAIR, PAIR)), 0.0), axis=1, keepdims=True)
        dg_ref[...] = jnp.broadcast_to(dg, (PAIR, HD))

    blk = pl.BlockSpec((PAIR, HD), lambda i, h: (i, h))
    return pl.pallas_call(
        body, grid=(t // PAIR, N_HEADS),
        in_specs=[blk] * 14 + [pl.BlockSpec((16, HD), lambda i, h: (i, h))], out_specs=[blk] * 5,
        out_shape=[_sds((t, GW), F32)] * 5, name=name,
        compiler_params=_params(2))(qn, kn, vv, beta_b, gc_b, tinv, u, w, vn, do, dvn, dqd, dkd, dw, ddec)


def _rope_tables(pos_col, inv_row):
    ang = pos_col.astype(F32) * inv_row
    lane = _iota2(ang.shape, 1)
    return jnp.cos(ang), jnp.where(lane < HD // 2, -1.0, 1.0) * jnp.sin(ang)


def _head_rms(xh, wv):
    return xh * lax.rsqrt(jnp.mean(xh * xh, axis=-1, keepdims=True) + EPS) * wv


def _qk_fwd(name, proj, blk_idx, w_row, pos_col, inv_row, use_rope):
    t = proj.shape[0]
    tm = min(256, t)

    def body(x_ref, w_ref, pos_ref, inv_ref, o_ref):
        if use_rope:
            cos, sin = _rope_tables(pos_ref[...], inv_ref[...])
        for h in range(N_HEADS):
            xh = x_ref[:, h * HD:(h + 1) * HD]
            if use_rope:
                y = _head_rms(xh, w_ref[...])
                xh = y * cos + pltpu.roll(y, HD // 2, 1) * sin
            o_ref[:, h * HD:(h + 1) * HD] = xh.astype(BF16)

    vec = pl.BlockSpec((1, HD), lambda i: (0, 0))
    return pl.pallas_call(
        body, grid=(t // tm,),
        in_specs=[pl.BlockSpec((tm, GW), lambda i: (i, blk_idx)), vec, pl.BlockSpec((tm, 1), lambda i: (i, 0)), vec],
        out_specs=pl.BlockSpec((tm, GW), lambda i: (i, 0)), out_shape=_sds((t, GW), BF16), name=name,
        compiler_params=_params(1))(proj, w_row, pos_col, inv_row)


def _qk_bwd(name, proj, blk_idx, w_row, pos_col, inv_row, d1, d2, d3, use_rope):
    t = proj.shape[0]
    tm = min(256, t)

    def body(x_ref, w_ref, pos_ref, inv_ref, d1_ref, d2_ref, d3_ref, dx_ref, dw_ref):
        if use_rope:
            cos, sin = _rope_tables(pos_ref[...], inv_ref[...])
        dw = jnp.zeros((1, HD), F32)
        for h in range(N_HEADS):
            sl = slice(h * HD, (h + 1) * HD)
            dy = d1_ref[:, sl] + d2_ref[:, sl] + d3_ref[:, sl]
            if use_rope:
                dy = dy * cos - pltpu.roll(dy, HD // 2, 1) * sin
                _, vjp = jax.vjp(_head_rms, x_ref[:, sl], w_ref[...])
                dy, dwh = vjp(dy)
                dw = dw + dwh
            dx_ref[:, sl] = dy.astype(BF16)

        @pl.when(pl.program_id(0) == 0)
        def _():
            dw_ref[...] = jnp.zeros_like(dw_ref)

        dw_ref[...] += dw

    vec = pl.BlockSpec((1, HD), lambda i: (0, 0))
    wide = pl.BlockSpec((tm, GW), lambda i: (i, 0))
    return pl.pallas_call(
        body, grid=(t // tm,),
        in_specs=[pl.BlockSpec((tm, GW), lambda i: (i, blk_idx)), vec, pl.BlockSpec((tm, 1), lambda i: (i, 0)), vec,
                  wide, wide, wide],
        out_specs=[wide, vec], out_shape=[_sds((t, GW), BF16), _sds((1, HD), F32)], name=name,
        compiler_params=_params(1))(proj, w_row, pos_col, inv_row, d1, d2, d3)


def _band_mask(first):
    qi = _iota2((SPAN, 2 * SPAN), 0)
    ki = _iota2((SPAN, 2 * SPAN), 1)
    lo = jnp.where(first, SPAN, 0)
    return (ki >= qi) & (ki <= qi + SPAN) & (ki >= lo)


def _swa_fwd(name, q, k, v, r):
    t = q.shape[0]
    rows = t // r
    nb = rows // SPAN
    qv, kv, vv = (a.reshape(rows, r * GW) for a in (q, k, v))
    scale = HD ** -0.5

    def body(q_ref, kp_ref, kc_ref, vp_ref, vc_ref, o_ref, l_ref):
        first = pl.program_id(1) == 0
        mask = _band_mask(first)
        kcat = jnp.concatenate([kp_ref[...], kc_ref[...]], axis=0)
        vcat = jnp.concatenate([vp_ref[...], vc_ref[...]], axis=0)
        s = jnp.where(mask, _dot(q_ref[...], kcat, 1, 1) * scale, NEG)
        m = jnp.max(s, axis=1, keepdims=True)
        p = jnp.exp(s - m)
        den = jnp.sum(p, axis=1, keepdims=True)
        o_ref[...] = _dot(_b16(p), vcat, 1, 0) / den
        l_ref[...] = jnp.broadcast_to(m + jnp.log(den), (SPAN, HD))

    cur = pl.BlockSpec((SPAN, HD), lambda rho, n, h: (n, rho * N_HEADS + h))
    prev = pl.BlockSpec((SPAN, HD), lambda rho, n, h: (jnp.maximum(n - 1, 0), rho * N_HEADS + h))
    o, lse = pl.pallas_call(
        body, grid=(r, nb, N_HEADS), in_specs=[cur, prev, cur, prev, cur], out_specs=[cur, cur],
        out_shape=[_sds((rows, r * GW), F32), _sds((rows, r * GW), F32)], name=name,
        compiler_params=_params(3))(qv, kv, kv, vv, vv)
    return o.reshape(t, GW), lse.reshape(t, GW)


def _swa_bwd(name, q, k, v, do, lse, delta, r):
    t = q.shape[0]
    rows = t // r
    nb = rows // SPAN
    qv, kv, vv, dov, lv, dv_ = (a.reshape(rows, r * GW) for a in (q, k, v, do, lse, delta))
    scale = HD ** -0.5

    def body(qc_ref, qn_ref, kp_ref, kc_ref, vp_ref, vc_ref, doc_ref, don_ref, lc_ref, ln_ref, dc_ref, dn_ref,
             dq_ref, dk_ref, dv_ref):
        n = pl.program_id(1)
        first = n == 0
        last = n == nb - 1
        mask = _band_mask(first)
        kc, vc = kc_ref[...], vc_ref[...]
        kcat = jnp.concatenate([kp_ref[...], kc], axis=0)
        vcat = jnp.concatenate([vp_ref[...], vc], axis=0)
        qc, doc = qc_ref[...], doc_ref[...]
        lc = jnp.concatenate([lc_ref[...], lc_ref[...]], axis=1)
        dlt = jnp.concatenate([dc_ref[...], dc_ref[...]], axis=1)
        s = _dot(qc, kcat, 1, 1) * scale
        p = jnp.where(mask, jnp.exp(jnp.where(mask, s - lc, 0.0)), 0.0)
        ds = p * (_dot(doc, vcat, 1, 1) - dlt)
        dq_ref[...] = _dot(_b16(ds), kcat, 1, 0) * scale
        qi = _iota2((SPAN, SPAN), 0)
        ki = _iota2((SPAN, SPAN), 1)
        mask_n = (ki >= qi) & (ki < jnp.where(last, 0, SPAN))
        qn, don = qn_ref[...], don_ref[...]
        s_n = _dot(qn, kc, 1, 1) * scale
        p_n = jnp.where(mask_n, jnp.exp(jnp.where(mask_n, s_n - ln_ref[...], 0.0)), 0.0)
        ds_n = p_n * (_dot(don, vc, 1, 1) - dn_ref[...])
        p2 = _b16(jnp.concatenate([p[:, SPAN:], p_n], axis=0))
        ds2 = _b16(jnp.concatenate([ds[:, SPAN:], ds_n], axis=0))
        dv_ref[...] = _dot(p2, jnp.concatenate([doc, don], axis=0), 0, 0)
        dk_ref[...] = _dot(ds2, jnp.concatenate([qc, qn], axis=0), 0, 0) * scale

    cur = pl.BlockSpec((SPAN, HD), lambda rho, n, h: (n, rho * N_HEADS + h))
    prev = pl.BlockSpec((SPAN, HD), lambda rho, n, h: (jnp.maximum(n - 1, 0), rho * N_HEADS + h))
    nxt = pl.BlockSpec((SPAN, HD), lambda rho, n, h: (jnp.minimum(n + 1, nb - 1), rho * N_HEADS + h))
    outs = pl.pallas_call(
        body, grid=(r, nb, N_HEADS),
        in_specs=[cur, nxt, prev, cur, prev, cur, cur, nxt, cur, nxt, cur, nxt], out_specs=[cur] * 3,
        out_shape=[_sds((rows, r * GW), F32)] * 3, name=name,
        compiler_params=_params(3))(qv, qv, kv, kv, vv, vv, dov, dov, lv, lv, dv_, dv_)
    return tuple(a.reshape(t, GW) for a in outs)


def _merge(os_, ls_):
    m = jnp.maximum(jnp.maximum(ls_[0], ls_[1]), ls_[2])
    ws = [jnp.exp(l - m) for l in ls_]
    tot = ws[0] + ws[1] + ws[2]
    ob = (ws[0] * os_[0] + ws[1] * os_[1] + ws[2] * os_[2]) / tot
    return ob, m + jnp.log(tot)


def _gated_norm(oa, z, wv):
    return _head_rms(oa, wv) * _silu(z)


def _mix_fwd(name, oa_raw, proj, z_blk, o1, o2, o3, l1, l2, l3, w_dn, w_an):
    t = oa_raw.shape[0]
    tm = min(256, t)

    def body(oa_ref, z_ref, o1_ref, o2_ref, o3_ref, l1_ref, l2_ref, l3_ref, wd_ref, wa_ref,
             mix_ref, ob_ref, lse_ref):
        for h in range(N_HEADS):
            sl = slice(h * HD, (h + 1) * HD)
            mix_ref[:, sl] = _gated_norm(oa_ref[:, sl], z_ref[:, sl], wd_ref[...]).astype(BF16)
            ob, lse = _merge([o1_ref[:, sl], o2_ref[:, sl], o3_ref[:, sl]],
                             [l1_ref[:, sl], l2_ref[:, sl], l3_ref[:, sl]])
            ob_ref[:, sl] = ob
            lse_ref[:, sl] = lse
            mix_ref[:, GW + h * HD:GW + (h + 1) * HD] = _head_rms(ob, wa_ref[...]).astype(BF16)

    vec = pl.BlockSpec((1, HD), lambda i: (0, 0))
    wide = pl.BlockSpec((tm, GW), lambda i: (i, 0))
    return pl.pallas_call(
        body, grid=(t // tm,),
        in_specs=[wide, pl.BlockSpec((tm, GW), lambda i: (i, z_blk))] + [wide] * 6 + [vec, vec],
        out_specs=[pl.BlockSpec((tm, 2 * GW), lambda i: (i, 0)), wide, wide],
        out_shape=[_sds((t, 2 * GW), BF16), _sds((t, GW), F32), _sds((t, GW), F32)], name=name,
        compiler_params=_params(1))(oa_raw, proj, o1, o2, o3, l1, l2, l3, w_dn, w_an)


def _mix_bwd(name, dmixed, oa_raw, proj, z_blk, ob, w_dn, w_an, dep):
    t = oa_raw.shape[0]
    tm = min(256, t)

    def body(dm_ref, oa_ref, z_ref, ob_ref, wd_ref, wa_ref, dep_ref,
             doa_ref, dz_ref, dob_ref, dl_ref, dwd_ref, dwa_ref):
        dwd = jnp.zeros((1, HD), F32)
        dwa = jnp.zeros((1, HD), F32)
        for h in range(N_HEADS):
            sl = slice(h * HD, (h + 1) * HD)
            _, vjp = jax.vjp(_gated_norm, oa_ref[:, sl], z_ref[:, sl], wd_ref[...])
            doa, dz, dw1 = vjp(dm_ref[:, sl])
            doa_ref[:, sl] = doa
            dz_ref[:, sl] = dz.astype(BF16)
            dwd = dwd + dw1
            obh = ob_ref[:, sl]
            _, vjp2 = jax.vjp(_head_rms, obh, wa_ref[...])
            dob, dw2 = vjp2(dm_ref[:, GW + h * HD:GW + (h + 1) * HD])
            dwa = dwa + dw2
            dob_ref[:, sl] = dob.astype(BF16)
            dl_ref[:, sl] = jnp.broadcast_to(jnp.sum(dob * obh, axis=1, keepdims=True), (tm, HD))

        @pl.when(pl.program_id(0) == 0)
        def _():
            dwd_ref[...] = jnp.zeros_like(dwd_ref)
            dwa_ref[...] = jnp.zeros_like(dwa_ref)

        dwd_ref[...] += dwd
        dwa_ref[...] += dwa

    vec = pl.BlockSpec((1, HD), lambda i: (0, 0))
    wide = pl.BlockSpec((tm, GW), lambda i: (i, 0))
    return pl.pallas_call(
        body, grid=(t // tm,),
        in_specs=[pl.BlockSpec((tm, 2 * GW), lambda i: (i, 0)), wide, pl.BlockSpec((tm, GW), lambda i: (i, z_blk)),
                  wide, vec, vec, ANY],
        out_specs=[wide, wide, wide, wide, vec, vec],
        out_shape=[_sds((t, GW), F32), _sds((t, GW), BF16), _sds((t, GW), BF16), _sds((t, GW), F32),
                   _sds((1, HD), F32), _sds((1, HD), F32)], name=name,
        compiler_params=_params(1))(dmixed, oa_raw, proj, ob, w_dn, w_an, dep)


def _swiglu_fwd(name, gu3):
    _, t, f = gu3.shape
    tm, tn = min(512, t), 512

    def body(g_ref, o_ref):
        o_ref[...] = (_silu(g_ref[0]) * g_ref[1]).astype(BF16)

    return pl.pallas_call(
        body, grid=(t // tm, f // tn), in_specs=[pl.BlockSpec((2, tm, tn), lambda i, j: (0, i, j))],
        out_specs=pl.BlockSpec((tm, tn), lambda i, j: (i, j)), out_shape=_sds((t, f), BF16), name=name,
        compiler_params=_params(2))(gu3)


def _swiglu_bwd(name, gu3, dact):
    _, t, f = gu3.shape
    tm, tn = min(512, t), 512

    def body(g_ref, d_ref, o_ref):
        g, up, d = g_ref[0], g_ref[1], d_ref[...]
        sg = _sigmoid(g)
        o_ref[0] = (d * up * sg * (1.0 + g * (1.0 - sg))).astype(BF16)
        o_ref[1] = (d * g * sg).astype(BF16)

    return pl.pallas_call(
        body, grid=(t // tm, f // tn),
        in_specs=[pl.BlockSpec((2, tm, tn), lambda i, j: (0, i, j)), pl.BlockSpec((tm, tn), lambda i, j: (i, j))],
        out_specs=pl.BlockSpec((2, tm, tn), lambda i, j: (0, i, j)), out_shape=_sds((2, t, f), BF16), name=name,
        compiler_params=_params(2))(gu3, dact)


def _loss_head(name, y, target):
    t, d = y.shape
    tm = min(512, t)

    def body(y_ref, t_ref, dy_ref, l_ref):
        diff = y_ref[...] - t_ref[...]
        dy_ref[...] = diff * (1.0 / d)
        part = jnp.sum(jnp.sum(diff * diff, axis=1, keepdims=True), axis=0, keepdims=True) * (0.5 / d)

        @pl.when(pl.program_id(0) == 0)
        def _():
            l_ref[...] = jnp.zeros_like(l_ref)

        l_ref[...] += jnp.broadcast_to(part, (8, 128))

    row = pl.BlockSpec((tm, d), lambda i: (i, 0))
    return pl.pallas_call(body, grid=(t // tm,), in_specs=[row, row],
                          out_specs=[row, pl.BlockSpec((8, 128), lambda i: (0, 0))],
                          out_shape=[_sds((t, d), F32), _sds((8, 128), F32)], name=name,
                          compiler_params=_params(1))(y, target)


def _peer(me, k):
    pid = (me + k) % N_DEV
    return (pid // 4, (pid // 2) % 2, pid % 2)


def _my_id():
    return 4 * lax.axis_index("x") + 2 * lax.axis_index("y") + lax.axis_index("c")


def _exchange(name, arrays, scatter):
    n = len(arrays)

    def body(*refs):
        ins, outs = refs[:n], refs[n:2 * n]
        send_sems, recv_sems, local_sems = refs[2 * n:]
        me = _my_id()
        started = []
        for a in range(n):
            src = ins[a].at[me] if scatter[a] else ins[a]
            loc = pltpu.make_async_copy(src, outs[a].at[me], local_sems.at[a])
            loc.start()
            started.append(loc)
        remote = []
        for k in range(1, N_DEV):
            to = (me + k) % N_DEV
            for a in range(n):
                src = ins[a].at[to] if scatter[a] else ins[a]
                cp = pltpu.make_async_remote_copy(src_ref=src, dst_ref=outs[a].at[me],
                                                  send_sem=send_sems.at[a * (N_DEV - 1) + k - 1], recv_sem=recv_sems.at[a * (N_DEV - 1) + k - 1],
                                                  device_id=_peer(me, k), device_id_type=pl.DeviceIdType.MESH)
                cp.start()
                remote.append(cp)
        for k in range(1, N_DEV):
            frm = (me + N_DEV - k) % N_DEV
            for a in range(n):
                src = ins[a].at[frm] if scatter[a] else ins[a]
                pltpu.make_async_remote_copy(src_ref=src, dst_ref=outs[a].at[frm],
                                             send_sem=send_sems.at[a * (N_DEV - 1) + k - 1], recv_sem=recv_sems.at[a * (N_DEV - 1) + k - 1],
                                             device_id=_peer(me, k), device_id_type=pl.DeviceIdType.MESH).wait_recv()
        for cp in remote:
            cp.wait_send()
        for loc in started:
            loc.wait()

    out_shape = [_sds((N_DEV,) + (a.shape[1:] if sc else a.shape), a.dtype) for a, sc in zip(arrays, scatter)]
    return pl.pallas_call(
        body, in_specs=[ANY] * n, out_specs=[ANY] * n, out_shape=out_shape,
        scratch_shapes=[pltpu.SemaphoreType.DMA((n * (N_DEV - 1),)), pltpu.SemaphoreType.DMA((n * (N_DEV - 1),)),
                        pltpu.SemaphoreType.DMA((n,))],
        name=name)(*arrays)


HBM = pl.BlockSpec(memory_space=pltpu.HBM)
SEM = pl.BlockSpec(memory_space=pltpu.SEMAPHORE)
EFFECT = pltpu.SideEffectType.DATAFLOW_SIDE_EFFECTING


def _remote_copies(srcs, lands, scatter, send_sems, recv_sems, me, incoming):
    out = []
    for k in range(1, N_DEV):
        other = (me + N_DEV - k) % N_DEV if incoming else (me + k) % N_DEV
        for a in range(len(srcs)):
            sem = a * (N_DEV - 1) + k - 1
            src = srcs[a].at[other] if scatter[a] else srcs[a]
            dst = lands[a].at[other if incoming else me]
            out.append(pltpu.make_async_remote_copy(src_ref=src, dst_ref=dst, send_sem=send_sems.at[sem],
                                                    recv_sem=recv_sems.at[sem], device_id=_peer(me, k),
                                                    device_id_type=pl.DeviceIdType.MESH))
    return out


def _exchange_start(name, arrays, scatter, dep):
    n = len(arrays)
    lands = [lax.empty((N_DEV,) + (a.shape[1:] if sc else a.shape), a.dtype) for a, sc in zip(arrays, scatter)]

    def body(*refs):
        srcs, land_refs = refs[:n], refs[n:2 * n]
        send_sems, recv_sems = refs[2 * n + 1], refs[2 * n + 2]
        token = refs[-1]
        for cp in _remote_copies(srcs, land_refs, scatter, send_sems, recv_sems, _my_id(), False):
            cp.start()
        token[...] = jnp.zeros_like(token)

    n_sem = n * (N_DEV - 1)
    out_shape = ([pltpu.SemaphoreType.DMA((n_sem,)), pltpu.SemaphoreType.DMA((n_sem,))]
                 + [pltpu.HBM(a.shape, a.dtype) for a in arrays] + [pltpu.HBM(l.shape, l.dtype) for l in lands]
                 + [_sds((8, 128), F32)])
    aliases = {i: 2 + i for i in range(2 * n)}
    args = [pltpu.with_memory_space_constraint(a, pltpu.HBM) for a in list(arrays) + lands] + [dep]
    res = pl.pallas_call(
        body, name=name, in_specs=[HBM] * (2 * n) + [ANY], out_shape=out_shape,
        out_specs=[SEM, SEM] + [HBM] * (2 * n) + [pl.BlockSpec(memory_space=pltpu.VMEM)],
        input_output_aliases=aliases, compiler_params=pltpu.CompilerParams(has_side_effects=EFFECT))(*args)
    return dict(send=res[0], recv=res[1], srcs=res[2:2 + n], lands=res[2 + n:2 + 2 * n], token=res[-1],
                scatter=scatter)


def _exchange_wait(name, started, after):
    n = len(started["srcs"])
    scatter = started["scatter"]

    def body(*refs):
        srcs, land_refs = refs[:n], refs[n:2 * n]
        send_sems, recv_sems = refs[2 * n], refs[2 * n + 1]
        me = _my_id()
        for cp in _remote_copies(srcs, land_refs, scatter, send_sems, recv_sems, me, False):
            cp.wait_send()
        for cp in _remote_copies(srcs, land_refs, scatter, send_sems, recv_sems, me, True):
            cp.wait_recv()

    arrs = list(started["srcs"]) + list(started["lands"])
    res = pl.pallas_call(
        body, name=name, in_specs=[HBM] * (2 * n) + [SEM, SEM, ANY],
        out_shape=[pltpu.HBM(a.shape, a.dtype) for a in arrs], out_specs=[HBM] * (2 * n),
        input_output_aliases={i: i for i in range(2 * n)},
        compiler_params=pltpu.CompilerParams(has_side_effects=EFFECT))(*arrs, started["send"], started["recv"], after)
    me = _my_id()
    out = []
    for src, land, sc in zip(res[:n], res[n:], scatter):
        own = lax.dynamic_index_in_dim(src, me, 0, keepdims=True) if sc else src[None]
        out.append(lax.dynamic_update_slice(land, own, (me,) + (0,) * (land.ndim - 1)))
    return out


def _adamw(name, parts, w, m, v):
    r, c = w.shape
    tr = r
    for cand in (128, 88, 64, 40, 8):
        if r % cand == 0:
            tr = cand
            break
    c1 = 1.0 / (1.0 - ADAM_B1 ** ADAM_STEP)
    c2 = 1.0 / (1.0 - ADAM_B2 ** ADAM_STEP)

    def body(p_ref, w_ref, m_ref, v_ref, g_ref, d_ref, nm_ref, nv_ref):
        g = p_ref[0]
        for s in range(1, N_DEV):
            g = g + p_ref[s]
        mn = ADAM_B1 * m_ref[...] + (1.0 - ADAM_B1) * g
        vn = ADAM_B2 * v_ref[...] + (1.0 - ADAM_B2) * (g * g)
        g_ref[...] = g
        nm_ref[...] = mn
        nv_ref[...] = vn
        d_ref[...] = -ADAM_LR * ((mn * c1) / (jnp.sqrt(vn * c2) + ADAM_EPS) + ADAM_WD * w_ref[...])

    blk = pl.BlockSpec((tr, c), lambda i: (i, 0))
    return pl.pallas_call(
        body, grid=(r // tr,), in_specs=[pl.BlockSpec((N_DEV, tr, c), lambda i: (0, i, 0)), blk, blk, blk],
        out_specs=[blk] * 4, out_shape=[_sds((r, c), F32)] * 4, name=name,
        compiler_params=_params(1, VMEM_LIMIT))(parts, w, m, v)


def _pad_rows(a, rows):
    return jnp.pad(a, ((0, rows - a.shape[0]), (0, 0)))


def _lane_row(vec8, offset):
    return jnp.pad(vec8.reshape(1, 8), ((0, 0), (offset, HD - 8 - offset)))


def kernel(x, positions, attn_norm_w, w_in, conv_w, a_log, dt_bias, delta_out_norm_w, q_norm_w, k_norm_w, attn_out_norm_w, w_out, ffn_norm_w, w_gate_up, w_down, loss_target, m_attn_norm_w, m_w_in, m_conv_w, m_a_log, m_dt_bias, m_delta_out_norm_w, m_q_norm_w, m_k_norm_w, m_attn_out_norm_w, m_w_out, m_ffn_norm_w, m_w_gate_up, m_w_down, v_attn_norm_w, v_w_in, v_conv_w, v_a_log, v_dt_bias, v_delta_out_norm_w, v_q_norm_w, v_k_norm_w, v_attn_out_norm_w, v_w_out, v_ffn_norm_w, v_w_gate_up, v_w_down):
    x2 = x[0]
    t, d = x2.shape
    target = loss_target[0]
    pos_col = positions.reshape(t, 1)
    half = HD // 2
    inv = (ROPE_THETA ** (-np.arange(half, dtype=np.float32) / half)).astype(np.float32)
    inv_row = jnp.asarray(np.concatenate([inv, inv]).reshape(1, HD))

    n_in = w_in.shape[2]
    n_gu = w_gate_up.shape[2]
    w_in_g, conv_g = _exchange("gather_in", [w_in[0].astype(BF16), _pad_rows(conv_w[0], 8)], [False] * 2)
    ffn_own = [w_gate_up[0].astype(BF16), w_down[0].astype(BF16), w_out[0].astype(BF16)]
    ffn_fly = _exchange_start("gather_ffn_start", ffn_own, [False] * 3, conv_g)
    w_in_full = jnp.transpose(w_in_g, (1, 0, 2)).reshape(d, N_DEV * n_in)
    n_main = 4 * GW
    n_small = 2 * N_HEADS
    w_cat = jnp.concatenate([w_in_full[:, :n_main], w_in_full[:, n_main + n_small:],
                             w_in_full[:, n_main:n_main + n_small],
                             jnp.zeros((d, HD - n_small), BF16)], axis=1)
    n_cat = w_cat.shape[1]
    small_blk = (7 * GW) // HD
    conv_w8 =jnp.transpose(conv_g, (1, 0, 2)).reshape(8, 3 * GW)
    alog_row = _lane_row(a_log[0], 8)
    dtb_row = _lane_row(dt_bias[0], 8)

    tm = min(2048, t)
    h1 = _rms_fwd("norm1", x2, attn_norm_w, ffn_fly["token"])
    tn = 384
    proj = _mm("in_proj", h1, w_cat, grid=(t // tm, n_cat // tn, 1),
               a_spec=pl.BlockSpec((tm, d), lambda i, j, k: (i, 0)),
               b_spec=pl.BlockSpec((d, tn), lambda i, j, k: (0, j)),
               o_spec=pl.BlockSpec((tm, tn), lambda i, j, k: (i, j)),
               out_shape=_sds((t, n_cat), F32), ca=1, cb=0, nk=1)
    qn = _conv_fwd("conv_q", proj, conv_w8, 0, True, HD ** -0.5)
    kn = _conv_fwd("conv_k", proj, conv_w8, 1, True, 1.0)
    vv = _conv_fwd("conv_v", proj, conv_w8, 2, False, 1.0)
    beta_b, gc_b = _gates_fwd("gates", proj, small_blk, alog_row, dtb_row)
    u, w, p, tinv, qd, kd = _delta_prep("delta_prep", qn, kn, vv, beta_b, gc_b)
    oa_raw, vn, s_hist = _delta_scan("delta_scan", u, w, p, qd, kd, gc_b)

    aq = _qk_fwd("attn_q", proj, 4, q_norm_w, pos_col, inv_row, True)
    ak = _qk_fwd("attn_k", proj, 5, k_norm_w, pos_col, inv_row, True)
    av = _qk_fwd("attn_v", proj, 6, q_norm_w, pos_col, inv_row, False)
    branches = [_swa_fwd("swa_fwd_%d" % r, aq, ak, av, r) for r in DILATIONS]
    (o1, l1), (o2, l2), (o3, l3) = branches
    mixed, ob, lse = _mix_fwd("mix", oa_raw, proj, 3, o1, o2, o3, l1, l2, l3, delta_out_norm_w, attn_out_norm_w)
    w_gu_g, w_down_g, w_out_g = _exchange_wait("gather_ffn_wait", ffn_fly, mixed)
    w_down_full = w_down_g.reshape(D_FF, d)
    w_out_full = w_out_g.reshape(2 * GW, d)
    tn = 512
    x1 = _mm("out_proj", mixed, w_out_full, grid=(t // tm, d // tn, 1),
             a_spec=pl.BlockSpec((tm, 2 * GW), lambda i, j, k: (i, 0)),
             b_spec=pl.BlockSpec((2 * GW, tn), lambda i, j, k: (0, j)),
             o_spec=pl.BlockSpec((tm, tn), lambda i, j, k: (i, j)),
             add=x2, add_spec=pl.BlockSpec((tm, tn), lambda i, j, k: (i, j)),
             out_shape=_sds((t, d), F32), ca=1, cb=0, nk=1)
    h2 = _rms_fwd("norm2", x1, ffn_norm_w, ffn_norm_w)
    per = N_DEV // 2
    tmd = min(1024, t)
    gu3 = _mm("gate_up", h2, w_gu_g, grid=(t // tmd, N_DEV, 1),
              a_spec=pl.BlockSpec((tmd, d), lambda i, j, k: (i, 0)),
              b_spec=pl.BlockSpec((None, d, n_gu), lambda i, j, k: (j, 0, 0)),
              o_spec=pl.BlockSpec((None, tmd, n_gu), lambda i, j, k: (j // per, i, j % per)),
              out_shape=_sds((2, t, D_FF), F32), ca=1, cb=0, nk=1)
    act = _swiglu_fwd("swiglu", gu3)
    tmd, tkd = min(1024, t), D_FF // 2
    y = _mm("down_proj", act, w_down_full, grid=(t // tmd, d // tn, 2),
            a_spec=pl.BlockSpec((tmd, tkd), lambda i, j, k: (i, k)),
            b_spec=pl.BlockSpec((tkd, tn), lambda i, j, k: (k, j)),
            o_spec=pl.BlockSpec((tmd, tn), lambda i, j, k: (i, j)),
            add=x1, add_spec=pl.BlockSpec((tmd, tn), lambda i, j, k: (i, j)),
            out_shape=_sds((t, d), F32), ca=1, cb=0, nk=2)
    dy, loss_tile = _loss_head("loss_head", y, target)
    loss = lax.psum(loss_tile[0, 0], ("x", "y", "c"))

    dy16 = dy.astype(BF16)
    dact = _mm("d_act", dy16, w_down_full, grid=(t // tm, D_FF // tn, 1),
               a_spec=pl.BlockSpec((tm, d), lambda i, j, k: (i, 0)),
               b_spec=pl.BlockSpec((tn, d), lambda i, j, k: (j, 0)),
               o_spec=pl.BlockSpec((tm, tn), lambda i, j, k: (i, j)),
               out_shape=_sds((t, D_FF), F32), ca=1, cb=1, nk=1)
    tk = min(2048, t)
    nkt = t // tk
    g_down = _mm("g_down", act, dy16, grid=(D_FF // 512, 1, nkt),
                 a_spec=pl.BlockSpec((tk, 512), lambda i, j, k: (k, i)),
                 b_spec=pl.BlockSpec((tk, d), lambda i, j, k: (k, 0)),
                 o_spec=pl.BlockSpec((512, d), lambda i, j, k: (i, 0)),
                 out_shape=_sds((D_FF, d), F32), ca=0, cb=0, nk=nkt)
    dgu3 = _swiglu_bwd("swiglu_bwd", gu3, dact)
    dh2 = _mm("d_h2", dgu3, w_gu_g, grid=(t // tmd, d // tn, N_DEV),
              a_spec=pl.BlockSpec((None, tmd, n_gu), lambda i, j, k: (k // per, i, k % per)),
              b_spec=pl.BlockSpec((None, tn, n_gu), lambda i, j, k: (k, j, 0)),
              o_spec=pl.BlockSpec((tmd, tn), lambda i, j, k: (i, j)),
              out_shape=_sds((t, d), F32), ca=1, cb=1, nk=N_DEV)
    g_gu = _mm("g_gate_up", h2, dgu3, grid=(d // 512, N_DEV, nkt),
               a_spec=pl.BlockSpec((tk, 512), lambda i, j, k: (k, i)),
               b_spec=pl.BlockSpec((None, tk, n_gu), lambda i, j, k: (j // per, k, j % per)),
               o_spec=pl.BlockSpec((None, 512, n_gu), lambda i, j, k: (j, i, 0)),
               out_shape=_sds((N_DEV, d, n_gu), F32), ca=0, cb=0, nk=nkt)
    dx1, g_ffn_norm = _rms_bwd("norm2_bwd", x1, ffn_norm_w, dh2, dy)

    dx1_16 = dx1.astype(BF16)
    dmixed = _mm("d_mixed", dx1_16, w_out_full, grid=(t // tm, (2 * GW) // tn, 1),
                 a_spec=pl.BlockSpec((tm, d), lambda i, j, k: (i, 0)),
                 b_spec=pl.BlockSpec((tn, d), lambda i, j, k: (j, 0)),
                 o_spec=pl.BlockSpec((tm, tn), lambda i, j, k: (i, j)),
                 out_shape=_sds((t, 2 * GW), F32), ca=1, cb=1, nk=1)
    g_out = _mm("g_out", mixed, dx1_16, grid=((2 * GW) // 512, 1, nkt),
                a_spec=pl.BlockSpec((tk, 512), lambda i, j, k: (k, i)),
                b_spec=pl.BlockSpec((tk, d), lambda i, j, k: (k, 0)),
                o_spec=pl.BlockSpec((512, d), lambda i, j, k: (i, 0)),
                out_shape=_sds((2 * GW, d), F32), ca=0, cb=0, nk=nkt)
    ffn_g_fly = _exchange_start("reduce_ffn_start",
                                [g_gu, g_down.reshape(N_DEV, D_FF // N_DEV, d), g_out.reshape(N_DEV, (2 * GW) // N_DEV, d)],
                                [True] * 3, g_ffn_norm)
    doa, dz, dob, delta, g_dn, g_an = _mix_bwd("mix_bwd", dmixed, oa_raw, proj, 3, ob,
                                               delta_out_norm_w, attn_out_norm_w, ffn_g_fly["token"])
    grads = [_swa_bwd("swa_bwd_%d" % r, aq, ak, av, dob, lse, delta, r) for r in DILATIONS]
    daq, g_qn = _qk_bwd("attn_q_bwd", proj, 4, q_norm_w, pos_col, inv_row, grads[0][0], grads[1][0], grads[2][0], True)
    dak, g_kn = _qk_bwd("attn_k_bwd", proj, 5, k_norm_w, pos_col, inv_row, grads[0][1], grads[1][1], grads[2][1], True)
    dav, _ = _qk_bwd("attn_v_bwd", proj, 6, q_norm_w, pos_col, inv_row, grads[0][2], grads[1][2], grads[2][2], False)

    dvn, dqd, dkd, dw, ddec = _delta_scan_bwd("delta_scan_bwd", doa, w, p, qd, kd, gc_b, vn, s_hist)
    dqn, dkn, dvv, dbeta_b, dg_b = _delta_prep_bwd("delta_prep_bwd", qn, kn, vv, beta_b, gc_b, tinv, u, w, vn,
                                                   doa, dvn, dqd, dkd, dw, ddec)
    dxq, gcw_q = _conv_bwd("conv_q_bwd", proj, conv_w8, dqn, 0, True, HD ** -0.5)
    dxk, gcw_k = _conv_bwd("conv_k_bwd", proj, conv_w8, dkn, 1, True, 1.0)
    dxv, gcw_v = _conv_bwd("conv_v_bwd", proj, conv_w8, dvv, 2, False, 1.0)
    dsmall, g_alog_row, g_dtb_row = _gates_bwd("gates_bwd", proj, small_blk, alog_row, dtb_row, dbeta_b, dg_b)
    dproj = jnp.concatenate([dxq, dxk, dxv, dz, daq, dak, dav, dsmall], axis=1)
    tnc = n_cat // 3
    g_cat = _mm("g_in", h1, dproj, grid=(d // 512, 3, nkt),
                a_spec=pl.BlockSpec((tk, 512), lambda i, j, k: (k, i)),
                b_spec=pl.BlockSpec((tk, tnc), lambda i, j, k: (k, j)),
                o_spec=pl.BlockSpec((512, tnc), lambda i, j, k: (i, j)),
                out_shape=_sds((d, n_cat), F32), ca=0, cb=0, nk=nkt)
    g_in_full = jnp.concatenate([g_cat[:, :n_main], g_cat[:, 7 * GW:7 * GW + n_small], g_cat[:, n_main:7 * GW]], axis=1)
    g_in_parts = jnp.transpose(g_in_full.reshape(d, N_DEV, n_in), (1, 0, 2))
    g_conv = jnp.concatenate([gcw_q, gcw_k, gcw_v], axis=1)
    n_cw = conv_w.shape[2]
    g_conv_parts = jnp.transpose(g_conv.reshape(8, N_DEV, n_cw), (1, 0, 2))
    in_g_fly = _exchange_start("reduce_in_start", [g_in_parts, g_conv_parts], [True] * 2, g_dtb_row)
    tkc = n_cat // 3
    dh1 = _mm("d_h1", dproj, w_cat, dep=in_g_fly["token"], grid=(t // tmd, d // tn, 3),
              a_spec=pl.BlockSpec((tmd, tkc), lambda i, j, k: (i, k)),
              b_spec=pl.BlockSpec((tn, tkc), lambda i, j, k: (j, k)),
              o_spec=pl.BlockSpec((tmd, tn), lambda i, j, k: (i, j)),
              out_shape=_sds((t, d), F32), ca=1, cb=1, nk=3)
    grad_x, g_attn_norm = _rms_bwd("norm1_bwd", x2, attn_norm_w, dh1, dx1)

    small_rows = [g_attn_norm.reshape(d // HD, HD), g_ffn_norm.reshape(d // HD, HD), g_dn, g_qn, g_kn, g_an,
                  g_alog_row, g_dtb_row]
    small_pack = _pad_rows(jnp.concatenate(small_rows, axis=0), 40)
    (r_small,) = _exchange("gather_small_grads", [small_pack], [False])

    def pack_small(an, fn, dn, qn_, kn_, aon, al, db):
        rows = [an.reshape(d // HD, HD), fn.reshape(d // HD, HD), dn, qn_, kn_, aon,
                _lane_row(al[0], 8), _lane_row(db[0], 8)]
        return _pad_rows(jnp.concatenate(rows, axis=0), 40)

    def unpack_small(pk):
        nr = d // HD
        return dict(attn_norm_w=pk[:nr].reshape(1, d), ffn_norm_w=pk[nr:2 * nr].reshape(1, d),
                    delta_out_norm_w=pk[2 * nr:2 * nr + 1], q_norm_w=pk[2 * nr + 1:2 * nr + 2],
                    k_norm_w=pk[2 * nr + 2:2 * nr + 3], attn_out_norm_w=pk[2 * nr + 3:2 * nr + 4],
                    a_log=pk[2 * nr + 4:2 * nr + 5, 8:16], dt_bias=pk[2 * nr + 5:2 * nr + 6, 8:16])

    res_small = _adamw("adamw_small", r_small,
                       pack_small(attn_norm_w, ffn_norm_w, delta_out_norm_w, q_norm_w, k_norm_w, attn_out_norm_w, a_log, dt_bias),
                       pack_small(m_attn_norm_w, m_ffn_norm_w, m_delta_out_norm_w, m_q_norm_w, m_k_norm_w, m_attn_out_norm_w, m_a_log, m_dt_bias),
                       pack_small(v_attn_norm_w, v_ffn_norm_w, v_delta_out_norm_w, v_q_norm_w, v_k_norm_w, v_attn_out_norm_w, v_a_log, v_dt_bias))
    small = [unpack_small(a) for a in res_small]
    r_gu, r_down, r_out = _exchange_wait("reduce_ffn_wait", ffn_g_fly, res_small[0])
    res_gu = [a[None] for a in _adamw("adamw_gate_up", r_gu, w_gate_up[0], m_w_gate_up[0], v_w_gate_up[0])]
    res_down = [a[None] for a in _adamw("adamw_down", r_down, w_down[0], m_w_down[0], v_w_down[0])]
    res_out = [a[None] for a in _adamw("adamw_out", r_out, w_out[0], m_w_out[0], v_w_out[0])]
    done = (res_gu[3][0, :1, :1] + res_down[3][0, :1, :1] + res_out[3][0, :1, :1])
    r_in, r_conv = _exchange_wait("reduce_in_wait", in_g_fly, done)
    res_in = [a[None] for a in _adamw("adamw_in", r_in, w_in[0], m_w_in[0], v_w_in[0])]
    res_conv =[a[None, :4] for a in _adamw("adamw_conv", r_conv, _pad_rows(conv_w[0], 8), _pad_rows(m_conv_w[0], 8),
                                            _pad_rows(v_conv_w[0], 8))]

    outs = [loss, grad_x[None]]
    for i in range(4):
        s = small[i]
        outs += [s["attn_norm_w"], res_in[i], res_conv[i], s["a_log"], s["dt_bias"], s["delta_out_norm_w"],
                 s["q_norm_w"], s["k_norm_w"], s["attn_out_norm_w"], res_out[i], s["ffn_norm_w"], res_gu[i],
                 res_down[i]]
    return tuple(outs)
```

```python
import functools

import numpy as np
import jax
import jax.numpy as jnp
from jax import lax
from jax.experimental import pallas as pl
from jax.experimental.pallas import tpu as pltpu

F32 = jnp.float32
BF16 = jnp.bfloat16

N_DEV = 8
N_HEADS = 8
HD = 128
GW = N_HEADS * HD
CHUNK = 64
PAIR = 2 * CHUNK
SPAN = 128
DILATIONS = (1, 4, 16)
ROPE_THETA = 10000.0
EPS = 1e-6
D_FF = 5632
ADAM_LR, ADAM_B1, ADAM_B2, ADAM_EPS, ADAM_WD, ADAM_STEP = 0.001, 0.9, 0.999, 1e-8, 0.01, 10
NEG = -1e30
VMEM_LIMIT = 56 * 1024 * 1024
ANY = pl.BlockSpec(memory_space=pl.ANY)
HEADS_PER_STEP = 8


def _params(n_grid, vmem=VMEM_LIMIT):
    return pltpu.CompilerParams(dimension_semantics=("arbitrary",) * n_grid, vmem_limit_bytes=vmem)


def _sds(shape, dtype):
    return jax.ShapeDtypeStruct(tuple(shape), dtype)


def _sigmoid(x):
    return 1.0 / (1.0 + jnp.exp(-x))


def _silu(x):
    return x * _sigmoid(x)


def _softplus(x):
    return jnp.maximum(x, 0.0) + jnp.log(1.0 + jnp.exp(-jnp.abs(x)))


def _dot(a, b, ca, cb, precision=None):
    return lax.dot_general(a, b, (((ca,), (cb,)), ((), ())), precision=precision,
                           preferred_element_type=F32)


def _b16(x):
    return x if x.dtype == BF16 else x.astype(BF16)


def _split(x):
    hi = x.astype(BF16)
    return hi, (x - hi.astype(F32)).astype(BF16)


def _dot3(a, b, ca, cb):
    a_hi, a_lo = _split(a)
    b_hi, b_lo = _split(b)
    return _dot(a_hi, b_hi, ca, cb) + (_dot(a_hi, b_lo, ca, cb) + _dot(a_lo, b_hi, ca, cb))


def _iota2(shape, axis):
    return lax.broadcasted_iota(jnp.int32, shape, axis)


def _mm(name, a, b, *, grid, a_spec, b_spec, o_spec, out_shape, ca, cb, nk, add=None, add_spec=None,
        dep=None, vmem=VMEM_LIMIT):
    has_add = add is not None
    n_in = 2 + has_add + (dep is not None)

    def body(*refs):
        a_ref, b_ref = refs[0], refs[1]
        e_ref = refs[2] if has_add else None
        o_ref = refs[n_in]
        part = _dot(_b16(a_ref[...]), _b16(b_ref[...]), ca, cb)
        if nk == 1:
            if has_add:
                part = part + e_ref[...]
            o_ref[...] = part.astype(o_ref.dtype)
            return
        acc = refs[-1]
        k = pl.program_id(2)

        @pl.when(k == 0)
        def _():
            acc[...] = part

        @pl.when(k > 0)
        def _():
            acc[...] += part

        @pl.when(k == nk - 1)
        def _():
            res = acc[...]
            if has_add:
                res = res + e_ref[...]
            o_ref[...] = res.astype(o_ref.dtype)

    in_specs = [a_spec, b_spec] + ([add_spec] if has_add else []) + ([ANY] if dep is not None else [])
    args = (a, b) + ((add,) if has_add else ()) + ((dep,) if dep is not None else ())
    blk = [d for d in o_spec.block_shape if d is not None]
    scratch = [pltpu.VMEM(tuple(blk), F32)] if nk > 1 else []
    return pl.pallas_call(body, grid=grid, in_specs=in_specs, out_specs=o_spec, out_shape=out_shape,
                          scratch_shapes=scratch, name=name, compiler_params=_params(3, vmem))(*args)


def _rms_f(xv, wv):
    return xv * lax.rsqrt(jnp.mean(xv * xv, axis=-1, keepdims=True) + EPS) * wv


def _rms_fwd(name, x, w, dep):
    t, d = x.shape
    tm = min(512, t)

    def body(x_ref, w_ref, dep_ref, o_ref):
        o_ref[...] = _rms_f(x_ref[...], w_ref[...]).astype(BF16)

    row = pl.BlockSpec((tm, d), lambda i: (i, 0))
    vec = pl.BlockSpec((1, d), lambda i: (0, 0))
    return pl.pallas_call(body, grid=(t // tm,), in_specs=[row, vec, ANY], out_specs=row,
                          out_shape=_sds((t, d), BF16), name=name, compiler_params=_params(1))(x, w, dep)


def _rms_bwd(name, x, w, dh, res):
    t, d = x.shape
    tm = min(256, t)

    def body(x_ref, w_ref, dh_ref, res_ref, dx_ref, dw_ref):
        _, vjp = jax.vjp(_rms_f, x_ref[...], w_ref[...])
        dxv, dwv = vjp(dh_ref[...])
        dx_ref[...] = dxv + res_ref[...]

        @pl.when(pl.program_id(0) == 0)
        def _():
            dw_ref[...] = jnp.zeros_like(dw_ref)

        dw_ref[...] += dwv

    row = pl.BlockSpec((tm, d), lambda i: (i, 0))
    vec = pl.BlockSpec((1, d), lambda i: (0, 0))
    return pl.pallas_call(body, grid=(t // tm,), in_specs=[row, vec, row, row], out_specs=[row, vec],
                          out_shape=[_sds((t, d), F32), _sds((1, d), F32)], name=name,
                          compiler_params=_params(1))(x, w, dh, res)


def _conv_taps(xv, w_ref, rows):
    c = w_ref[3:4, :] * xv
    for s in (1, 2, 3):
        c = c + w_ref[3 - s:4 - s, :] * jnp.where(rows >= s, pltpu.roll(xv, s, 0), 0.0)
    return c


def _post_conv(c, l2, scale):
    y = _silu(c)
    if l2:
        y = y * lax.rsqrt(jnp.sum(y * y, axis=-1, keepdims=True) + EPS) * scale
    return y


def _conv_fwd(name, proj, conv_w8, group, l2, scale):
    t = proj.shape[0]

    def body(x_ref, w_ref, o_ref):
        rows = _iota2((t, HD), 0)
        o_ref[...] = _post_conv(_conv_taps(x_ref[...], w_ref, rows), l2, scale)

    return pl.pallas_call(
        body, grid=(N_HEADS,),
        in_specs=[pl.BlockSpec((t, HD), lambda h: (0, h + group * N_HEADS)),
                  pl.BlockSpec((8, HD), lambda h: (0, h + group * N_HEADS))],
        out_specs=pl.BlockSpec((t, HD), lambda h: (0, h)),
        out_shape=_sds((t, GW), F32), name=name, compiler_params=_params(1, VMEM_LIMIT))(proj, conv_w8)


def _conv_bwd(name, proj, conv_w8, dn, group, l2, scale):
    t = proj.shape[0]

    def body(x_ref, w_ref, dn_ref, dx_ref, dw_ref):
        rows = _iota2((t, HD), 0)
        xv = x_ref[...]
        c = _conv_taps(xv, w_ref, rows)
        _, vjp = jax.vjp(lambda cc: _post_conv(cc, l2, scale), c)
        (dc,) = vjp(dn_ref[...])
        dx = w_ref[3:4, :] * dc
        dw = jnp.zeros((8, HD), F32)
        rid = _iota2((8, HD), 0)
        dw = dw + jnp.where(rid == 3, jnp.sum(dc * xv, axis=0, keepdims=True), 0.0)
        for s in (1, 2, 3):
            dx = dx + w_ref[3 - s:4 - s, :] * jnp.where(rows < t - s, pltpu.roll(dc, t - s, 0), 0.0)
            xs = jnp.where(rows >= s, pltpu.roll(xv, s, 0), 0.0)
            dw = dw + jnp.where(rid == 3 - s, jnp.sum(dc * xs, axis=0, keepdims=True), 0.0)
        dx_ref[...] = dx.astype(BF16)
        dw_ref[...] = dw

    return pl.pallas_call(
        body, grid=(N_HEADS,),
        in_specs=[pl.BlockSpec((t, HD), lambda h: (0, h + group * N_HEADS)),
                  pl.BlockSpec((8, HD), lambda h: (0, h + group * N_HEADS)),
                  pl.BlockSpec((t, HD), lambda h: (0, h))],
        out_specs=[pl.BlockSpec((t, HD), lambda h: (0, h)), pl.BlockSpec((8, HD), lambda h: (0, h))],
        out_shape=[_sds((t, GW), BF16), _sds((8, GW), F32)], name=name,
        compiler_params=_params(1, VMEM_LIMIT))(proj, conv_w8, dn)


def _chunk_cumsum(g, rows):
    pos = rows % CHUNK
    s = 1
    while s < CHUNK:
        g = g + jnp.where(pos >= s, pltpu.roll(g, s, 0), 0.0)
        s *= 2
    return g


def _gates_fwd(name, proj, small_blk, alog_row, dtb_row):
    t = proj.shape[0]
    tm = min(256, t)

    def body(s_ref, a_ref, b_ref, beta_ref, gc_ref):
        sm = s_ref[...]
        beta = _sigmoid(sm)
        g = -jnp.exp(a_ref[...]) * _softplus(sm + b_ref[...])
        gc = _chunk_cumsum(g, _iota2((tm, HD), 0))
        lane = _iota2((tm, HD), 1)
        for h in range(N_HEADS):
            bcol = jnp.sum(jnp.where(lane == h, beta, 0.0), axis=1, keepdims=True)
            gcol = jnp.sum(jnp.where(lane == 8 + h, gc, 0.0), axis=1, keepdims=True)
            beta_ref[:, h * HD:(h + 1) * HD] = jnp.broadcast_to(bcol, (tm, HD))
            gc_ref[:, h * HD:(h + 1) * HD] = jnp.broadcast_to(gcol, (tm, HD))

    vec = pl.BlockSpec((1, HD), lambda i: (0, 0))
    wide = pl.BlockSpec((tm, GW), lambda i: (i, 0))
    return pl.pallas_call(
        body, grid=(t // tm,),
        in_specs=[pl.BlockSpec((tm, HD), lambda i: (i, small_blk)), vec, vec], out_specs=[wide, wide],
        out_shape=[_sds((t, GW), F32), _sds((t, GW), F32)], name=name,
        compiler_params=_params(1))(proj, alog_row, dtb_row)


def _gates_bwd(name, proj, small_blk, alog_row, dtb_row, dbeta_b, dg_b):
    t = proj.shape[0]
    tm = min(256, t)

    def body(s_ref, a_ref, b_ref, db_ref, dg_ref, ds_ref, da_ref, dbias_ref):
        sm = s_ref[...]
        lane = _iota2((tm, HD), 1)
        db = jnp.zeros((tm, HD), F32)
        dg = jnp.zeros((tm, HD), F32)
        for h in range(N_HEADS):
            db = db + jnp.where(lane == h, db_ref[:, h * HD:(h + 1) * HD], 0.0)
            dg = dg + jnp.where(lane == 8 + h, dg_ref[:, h * HD:(h + 1) * HD], 0.0)
        beta = _sigmoid(sm)
        ea = jnp.exp(a_ref[...])
        pre = sm + b_ref[...]
        g = -ea * _softplus(pre)
        dpre = dg * (-ea) * _sigmoid(pre)
        ds_ref[...] = (db * beta * (1.0 - beta) + dpre).astype(BF16)

        @pl.when(pl.program_id(0) == 0)
        def _():
            da_ref[...] = jnp.zeros_like(da_ref)
            dbias_ref[...] = jnp.zeros_like(dbias_ref)

        da_ref[...] += jnp.sum(dg * g, axis=0, keepdims=True)
        dbias_ref[...] += jnp.sum(dpre, axis=0, keepdims=True)

    vec = pl.BlockSpec((1, HD), lambda i: (0, 0))
    wide = pl.BlockSpec((tm, GW), lambda i: (i, 0))
    return pl.pallas_call(
        body, grid=(t // tm,),
        in_specs=[pl.BlockSpec((tm, HD), lambda i: (i, small_blk)), vec, vec, wide, wide],
        out_specs=[pl.BlockSpec((tm, HD), lambda i: (i, 0)), vec, vec],
        out_shape=[_sds((t, HD), BF16), _sds((1, HD), F32), _sds((1, HD), F32)], name=name,
        compiler_params=_params(1))(proj, alog_row, dtb_row, dbeta_b, dg_b)


def _pair_masks():
    ii = _iota2((PAIR, PAIR), 0)
    jj = _iota2((PAIR, PAIR), 1)
    same = (ii // CHUNK) == (jj // CHUNK)
    return ii, jj, same & (ii >= jj), same & (ii > jj)


def _to_row(col_b, ii, jj):
    return jnp.sum(jnp.where(ii == jj, col_b, 0.0), axis=0, keepdims=True)


def _to_col(row, ii, jj):
    return jnp.sum(jnp.where(ii == jj, jnp.broadcast_to(row, (PAIR, PAIR)), 0.0), axis=1, keepdims=True)


def _decay_parts(gc, last_a, last_b, ii, jj, causal):
    diff = gc - _to_row(gc, ii, jj)
    dmat = jnp.where(causal, jnp.exp(jnp.where(causal, diff, 0.0)), 0.0)
    glast = jnp.where(ii < CHUNK, last_a, last_b)
    return dmat, jnp.exp(gc), jnp.exp(glast - gc)


def _unit_lower_inverse(lows, ii, jj):
    eye = jnp.where(ii == jj, 1.0, 0.0)
    mm = lambda xs, ys: [_dot3(a, b, 1, 0) for a, b in zip(xs, ys)]
    plus = lambda xs: [eye + a for a in xs]
    minus = lambda xs: [eye - a for a in xs]
    d1 = [jnp.where((ii // 16) == (jj // 16), low, 0.0) for low in lows]
    d2 = mm(d1, d1)
    a = mm(minus(d1), plus(d2))
    d4 = mm(d2, d2)
    a = mm(a, plus(d4))
    d8 = mm(d4, d4)
    td = mm(a, plus(d8))
    n1 = mm(td, [low - d for low, d in zip(lows, d1)])
    n2 = mm(n1, n1)
    return mm(mm(minus(n1), plus(n2)), td)


def _delta_prep(name, qn, kn, vv, beta_b, gc_b):
    t = qn.shape[0]

    def body(q_ref, k_ref, v_ref, b_ref, g_ref, u_ref, w_ref, p_ref, t_ref, qd_ref, kd_ref):
        ii, jj, causal, strict = _pair_masks()
        sls = [slice(hh * HD, (hh + 1) * HD) for hh in range(HEADS_PER_STEP)]
        lows = []
        for sl in sls:
            q, k, beta = q_ref[:, sl], k_ref[:, sl], b_ref[:, sl]
            dmat, gam, e2 = _decay_parts(g_ref[:, sl], g_ref[CHUNK - 1:CHUNK, sl], g_ref[PAIR - 1:PAIR, sl],
                                         ii, jj, causal)
            k16 = _b16(k)
            lows.append(jnp.where(strict, beta * _dot(k16, k16, 1, 1) * dmat, 0.0))
            p_ref[:, sl] = jnp.where(causal, _dot(_b16(q), k16, 1, 1) * dmat, 0.0).astype(BF16)
            qd_ref[:, sl] = (q * gam).astype(BF16)
            kd_ref[:, sl] = (k * e2).astype(BF16)
        for sl, tinv in zip(sls, _unit_lower_inverse(lows, ii, jj)):
            beta = b_ref[:, sl]
            t_ref[:, sl] = tinv
            u_ref[:, sl] = _dot3(tinv, v_ref[:, sl] * beta, 1, 0)
            w_ref[:, sl] = _dot3(tinv, k_ref[:, sl] * (beta * jnp.exp(g_ref[:, sl])), 1, 0).astype(BF16)

    blk = pl.BlockSpec((PAIR, HEADS_PER_STEP * HD), lambda i, h: (i, h))
    return pl.pallas_call(
        body, grid=(t // PAIR, N_HEADS // HEADS_PER_STEP), in_specs=[blk] * 5, out_specs=[blk] * 6,
        out_shape=[_sds((t, GW), F32), _sds((t, GW), BF16), _sds((t, GW), BF16), _sds((t, GW), F32),
                   _sds((t, GW), BF16), _sds((t, GW), BF16)],
        name=name, compiler_params=_params(2))(qn, kn, vv, beta_b, gc_b)


def _delta_scan(name, u, w, p, qd, kd, gc_b):
    t = u.shape[0]
    n = t // CHUNK

    def body(u_ref, w_ref, p_ref, qd_ref, kd_ref, g_ref, o_ref, vn_ref, sh_ref, state):
        @pl.when(pl.program_id(0) == 0)
        def _():
            state[...] = jnp.zeros_like(state)

        for h in range(N_HEADS):
            sl = slice(h * HD, (h + 1) * HD)
            s = state[h]
            sh_ref[h] = s
            s16 = _b16(s)
            vnew = u_ref[:, sl] - _dot(w_ref[:, sl], s16, 1, 0)
            vn16 = _b16(vnew)
            vpair = jnp.concatenate([vn16, vn16], axis=0)
            o_ref[:, sl] = _dot(qd_ref[:, sl], s16, 1, 0) + _dot(p_ref[:, sl], vpair, 1, 0)
            vn_ref[:, sl] = vn16
            dec = jnp.exp(g_ref[CHUNK - 1:CHUNK, sl])
            state[h] = s * dec + _dot(kd_ref[:, sl], vn16, 0, 0)

    blk = pl.BlockSpec((CHUNK, GW), lambda i: (i, 0))
    return pl.pallas_call(
        body, grid=(n,), in_specs=[blk] * 6,
        out_specs=[blk, blk, pl.BlockSpec((None, N_HEADS, HD, HD), lambda i: (i, 0, 0, 0))],
        out_shape=[_sds((t, GW), F32), _sds((t, GW), BF16), _sds((n, N_HEADS, HD, HD), F32)],
        scratch_shapes=[pltpu.VMEM((N_HEADS, HD, HD), F32)], name=name,
        compiler_params=_params(1))(u, w, p, qd, kd, gc_b)


def _delta_scan_bwd(name, do, w, p, qd, kd, gc_b, vn, s_hist):
    t = do.shape[0]
    n = t // CHUNK

    def body(do_ref, w_ref, p_ref, qd_ref, kd_ref, g_ref, vn_ref, sh_ref,
             dvn_ref, dqd_ref, dkd_ref, dw_ref, ddec_ref, dstate):
        @pl.when(pl.program_id(0) == 0)
        def _():
            dstate[...] = jnp.zeros_like(dstate)

        for h in range(N_HEADS):
            sl = slice(h * HD, (h + 1) * HD)
            ds = dstate[h]
            ds16 = _b16(ds)
            s_in = sh_ref[h]
            s16 = _b16(s_in)
            do16 = _b16(do_ref[:, sl])
            ptdo = _dot(p_ref[:, sl], do16, 0, 0)
            dvn = ptdo[:CHUNK, :] + ptdo[CHUNK:, :] + _dot(kd_ref[:, sl], ds16, 1, 0)
            dvn16 = _b16(dvn)
            dec = jnp.exp(g_ref[CHUNK - 1:CHUNK, sl])
            dstate[h] = ds * dec + _dot(qd_ref[:, sl], do16, 0, 0) - _dot(w_ref[:, sl], dvn16, 0, 0)
            dvn_ref[:, sl] = dvn
            dqd_ref[:, sl] = _dot(do16, s16, 1, 1)
            dw_ref[:, sl] = -_dot(dvn16, s16, 1, 1)
            dkd_ref[:, sl] = _dot(vn_ref[:, sl], ds16, 1, 1)
            tot = jnp.sum(jnp.sum(s_in * ds, axis=1, keepdims=True), axis=0, keepdims=True)
            ddec_ref[:, sl] = jnp.broadcast_to(tot, (8, HD))

    blk = pl.BlockSpec((CHUNK, GW), lambda i: (n - 1 - i, 0))
    return pl.pallas_call(
        body, grid=(n,),
        in_specs=[blk] * 7 + [pl.BlockSpec((None, N_HEADS, HD, HD), lambda i: (n - 1 - i, 0, 0, 0))],
        out_specs=[blk] * 4 + [pl.BlockSpec((8, GW), lambda i: (n - 1 - i, 0))],
        out_shape=[_sds((t, GW), F32)] * 4 + [_sds((n * 8, GW), F32)],
        scratch_shapes=[pltpu.VMEM((N_HEADS, HD, HD), F32)], name=name,
        compiler_params=_params(1))(do, w, p, qd, kd, gc_b, vn, s_hist)


def _delta_prep_bwd(name, qn, kn, vv, beta_b, gc_b, tinv, u, w, vn, do, dvn, dqd, dkd, dw, ddec):
    t = qn.shape[0]

    def body(q_ref, k_ref, v_ref, b_ref, g_ref, t_ref, u_ref, w_ref, vn_ref, do_ref, dvn_ref, dqd_ref,
             dkd_ref, dw_ref, ddec_ref, dq_ref, dk_ref, dv_ref, dbeta_ref, dg_ref):
        ii, jj, causal, strict = _pair_masks()
        suffix = ((ii // CHUNK) == (jj // CHUNK)) & (jj >= ii)
        first = ii < CHUNK
        rs = lambda a: jnp.sum(a, axis=1, keepdims=True)
        for hh in range(HEADS_PER_STEP):
            sl = slice(hh * HD, (hh + 1) * HD)
            q, k, v, beta, gc = q_ref[:, sl], k_ref[:, sl], v_ref[:, sl], b_ref[:, sl], g_ref[:, sl]
            last_a, last_b = g_ref[CHUNK - 1:CHUNK, sl], g_ref[PAIR - 1:PAIR, sl]
            dmat, gam, e2 = _decay_parts(gc, last_a, last_b, ii, jj, causal)
            q16, k16 = _b16(q), _b16(k)
            kk = _dot(k16, k16, 1, 1)
            qk = _dot(q16, k16, 1, 1)
            dqd, dkd = dqd_ref[:, sl], dkd_ref[:, sl]
            dp = jnp.where(causal, _dot(_b16(do_ref[:, sl]), vn_ref[:, sl], 1, 1), 0.0)
            dpd16 = _b16(dp * dmat)
            tinv_v = t_ref[:, sl]
            x = _dot3(tinv_v, dvn_ref[:, sl], 0, 0)
            y = _dot3(tinv_v, dw_ref[:, sl], 0, 0)
            da = -jnp.where(strict, _dot(_b16(x), _b16(u_ref[:, sl]), 1, 1) + _dot(_b16(y), w_ref[:, sl], 1, 1), 0.0)
            dkk16 = _b16(da * beta * dmat)
            dq_ref[:, sl] = gam * dqd + _dot(dpd16, k16, 1, 0)
            dk_ref[:, sl] = (e2 * dkd + _dot(dpd16, q16, 0, 0) + beta * gam * y
                             + _dot(dkk16, k16, 1, 0) + _dot(dkk16, k16, 0, 0))
            dv_ref[:, sl] = beta * x
            dbeta = rs(v * x) + rs(k * gam * y) + rs(da * kk * dmat)
            dbeta_ref[:, sl] = jnp.broadcast_to(dbeta, (PAIR, HD))
            m = (dp * qk + da * beta * kk) * dmat
            dgam = rs(q * dqd) + rs(k * beta * y)
            de2 = rs(k * dkd)
            colsum = _to_col(jnp.sum(m, axis=0, keepdims=True), ii, jj)
            te2 = de2 * e2
            dgc = rs(m) - colsum + gam * dgam - te2
            tail_a = jnp.sum(jnp.where(first, te2, 0.0), axis=0, keepdims=True)
            tail_b = jnp.sum(jnp.where(first, 0.0, te2), axis=0, keepdims=True)
            dgc = dgc + jnp.where(ii == CHUNK - 1, tail_a + ddec_ref[0:1, sl] * jnp.exp(last_a), 0.0)
            dgc = dgc + jnp.where(ii == PAIR - 1, tail_b + ddec_ref[8:9, sl] * jnp.exp(last_b), 0.0)
            dgc_row = _to_row(dgc, ii, jj)
            dg = jnp.sum(jnp.where(suffix, jnp.broadcast_to(dgc_row, (PAIR, PAIR)), 0.0), axis=1, keepdims=True)
            dg_ref[:, sl] = jnp.broadcast_to(dg, (PAIR, HD))

    blk = pl.BlockSpec((PAIR, HEADS_PER_STEP * HD), lambda i, h: (i, h))
    return pl.pallas_call(
        body, grid=(t // PAIR, N_HEADS // HEADS_PER_STEP),
        in_specs=[blk] * 14 + [pl.BlockSpec((16, HEADS_PER_STEP * HD), lambda i, h: (i, h))], out_specs=[blk] * 5,
        out_shape=[_sds((t, GW), F32)] * 5, name=name,
        compiler_params=_params(2))(qn, kn, vv, beta_b, gc_b, tinv, u, w, vn, do, dvn, dqd, dkd, dw, ddec)


def _rope_tables(pos_col, inv_row):
    ang = pos_col.astype(F32) * inv_row
    lane = _iota2(ang.shape, 1)
    return jnp.cos(ang), jnp.where(lane < HD // 2, -1.0, 1.0) * jnp.sin(ang)


def _head_rms(xh, wv):
    return xh * lax.rsqrt(jnp.mean(xh * xh, axis=-1, keepdims=True) + EPS) * wv


def _qk_fwd(name, proj, blk_idx, w_row, pos_col, inv_row):
    t = proj.shape[0]
    tm = min(256, t)

    def body(x_ref, w_ref, pos_ref, inv_ref, o_ref):
        cos, sin = _rope_tables(pos_ref[...], inv_ref[...])
        for h in range(N_HEADS):
            y = _head_rms(x_ref[:, h * HD:(h + 1) * HD], w_ref[...])
            o_ref[:, h * HD:(h + 1) * HD] = y * cos + pltpu.roll(y, HD // 2, 1) * sin

    vec = pl.BlockSpec((1, HD), lambda i: (0, 0))
    return pl.pallas_call(
        body, grid=(t // tm,),
        in_specs=[pl.BlockSpec((tm, GW), lambda i: (i, blk_idx)), vec, pl.BlockSpec((tm, 1), lambda i: (i, 0)), vec],
        out_specs=pl.BlockSpec((tm, GW), lambda i: (i, 0)), out_shape=_sds((t, GW), F32), name=name,
        compiler_params=_params(1))(proj, w_row, pos_col, inv_row)


def _qk_bwd(name, proj, blk_idx, w_row, pos_col, inv_row, dy_full):
    t = proj.shape[0]
    tm = min(256, t)

    def body(x_ref, w_ref, pos_ref, inv_ref, dy_ref, dx_ref, dw_ref):
        cos, sin = _rope_tables(pos_ref[...], inv_ref[...])
        dw = jnp.zeros((1, HD), F32)
        for h in range(N_HEADS):
            sl = slice(h * HD, (h + 1) * HD)
            dy = dy_ref[:, sl]
            dy = dy * cos - pltpu.roll(dy, HD // 2, 1) * sin
            _, vjp = jax.vjp(_head_rms, x_ref[:, sl], w_ref[...])
            dx, dwh = vjp(dy)
            dw = dw + dwh
            dx_ref[:, sl] = dx.astype(BF16)

        @pl.when(pl.program_id(0) == 0)
        def _():
            dw_ref[...] = jnp.zeros_like(dw_ref)

        dw_ref[...] += dw

    vec = pl.BlockSpec((1, HD), lambda i: (0, 0))
    wide = pl.BlockSpec((tm, GW), lambda i: (i, 0))
    return pl.pallas_call(
        body, grid=(t // tm,),
        in_specs=[pl.BlockSpec((tm, GW), lambda i: (i, blk_idx)), vec, pl.BlockSpec((tm, 1), lambda i: (i, 0)), vec,
                  wide],
        out_specs=[wide, vec], out_shape=[_sds((t, GW), BF16), _sds((1, HD), F32)], name=name,
        compiler_params=_params(1))(proj, w_row, pos_col, inv_row, dy_full)


GROUP = SPAN * max(DILATIONS)
SCALE = HD ** -0.5


def _band_mask(lo):
    qi = _iota2((SPAN, 2 * SPAN), 0)
    ki = _iota2((SPAN, 2 * SPAN), 1)
    return (ki >= qi) & (ki <= qi + SPAN) & (ki >= lo)


def _tiles():
    return [(pi, r, u, rho) for pi, r in enumerate(DILATIONS) for u in range(GROUP // (SPAN * r)) for rho in range(r)]


def _rows(r, u, rho):
    return pl.ds(u * SPAN * r + rho, SPAN, stride=r) if r > 1 else pl.ds(u * SPAN, SPAN)


def _attn_fwd(name, q, k, v, v_blk):
    t = q.shape[0]

    def body(qc_ref, kc_ref, vc_ref, kp_ref, vp_ref, ob_ref, lse_ref, o_scr, l_scr):
        mask_in = _band_mask(0)
        mask_edge = _band_mask(jnp.where(pl.program_id(0) == 0, SPAN, 0))
        for pi, r, u, rho in _tiles():
            rows = _rows(r, u, rho)
            if u > 0:
                prows, kp_src, vp_src, mask = _rows(r, u - 1, rho), kc_ref, vc_ref, mask_in
            else:
                prows, kp_src, vp_src, mask = _rows(r, GROUP // (SPAN * r) - 1, rho), kp_ref, vp_ref, mask_edge
            kcat = jnp.concatenate([kp_src[prows, :], kc_ref[rows, :]], axis=0).astype(BF16)
            vcat = jnp.concatenate([vp_src[prows, :], vc_ref[rows, :]], axis=0).astype(BF16)
            s = jnp.where(mask, _dot(qc_ref[rows, :].astype(BF16), kcat, 1, 1) * SCALE, NEG)
            m = jnp.max(s, axis=1, keepdims=True)
            p = jnp.exp(s - m)
            den = jnp.sum(p, axis=1, keepdims=True)
            o_scr[pi, rows, :] = _dot(_b16(p), vcat, 1, 0) / den
            l_scr[pi, rows, :] = jnp.broadcast_to(m + jnp.log(den), (SPAN, HD))
        step = 256
        for c in range(GROUP // step):
            sl = pl.ds(c * step, step)
            ob, lse = _merge([o_scr[i, sl, :] for i in range(3)], [l_scr[i, sl, :] for i in range(3)])
            ob_ref[sl, :] = ob
            lse_ref[sl, :] = lse

    cur = pl.BlockSpec((GROUP, HD), lambda g, h: (g, h))
    prev = pl.BlockSpec((GROUP, HD), lambda g, h: (jnp.maximum(g - 1, 0), h))
    vcur = pl.BlockSpec((GROUP, HD), lambda g, h: (g, v_blk * N_HEADS + h))
    vprev = pl.BlockSpec((GROUP, HD), lambda g, h: (jnp.maximum(g - 1, 0), v_blk * N_HEADS + h))
    return pl.pallas_call(
        body, grid=(t // GROUP, N_HEADS), in_specs=[cur, cur, vcur, prev, vprev], out_specs=[cur, cur],
        out_shape=[_sds((t, GW), F32), _sds((t, GW), F32)],
        scratch_shapes=[pltpu.VMEM((3, GROUP, HD), F32), pltpu.VMEM((3, GROUP, HD), F32)], name=name,
        compiler_params=_params(2))(q, k, v, k, v)


def _attn_bwd(name, q, k, v, v_blk, do, lse, delta):
    t = q.shape[0]
    ng = t // GROUP

    def pair(qt, dot, lt, dlt, kcat, vcat, mask):
        wide = kcat.shape[0] // SPAN
        lw = jnp.concatenate([lt] * wide, axis=1) if wide > 1 else lt
        dw = jnp.concatenate([dlt] * wide, axis=1) if wide > 1 else dlt
        s = _dot(qt, kcat, 1, 1) * SCALE
        p = jnp.where(mask, jnp.exp(jnp.where(mask, s - lw, 0.0)), 0.0)
        ds = p * (_dot(dot, vcat, 1, 1) - dw) * SCALE
        return _b16(ds), _b16(p)

    def body(qc_ref, kc_ref, vc_ref, doc_ref, lc_ref, dc_ref, kp_ref, vp_ref, qn_ref, don_ref, ln_ref, dn_ref,
             dq_ref, dk_ref, dv_ref):
        g = pl.program_id(0)
        mask_in = _band_mask(0)
        mask_edge = _band_mask(jnp.where(g == 0, SPAN, 0))
        dk_ref[...] = jnp.zeros_like(dk_ref)
        dv_ref[...] = jnp.zeros_like(dv_ref)
        for pi, r, u, rho in _tiles():
            rows = _rows(r, u, rho)
            if u > 0:
                prows, kp_src, vp_src, mask = _rows(r, u - 1, rho), kc_ref, vc_ref, mask_in
            else:
                prows, kp_src, vp_src, mask = _rows(r, GROUP // (SPAN * r) - 1, rho), kp_ref, vp_ref, mask_edge
            kcat = jnp.concatenate([kp_src[prows, :], kc_ref[rows, :]], axis=0).astype(BF16)
            vcat = jnp.concatenate([vp_src[prows, :], vc_ref[rows, :]], axis=0).astype(BF16)
            qt, dot = qc_ref[rows, :].astype(BF16), doc_ref[rows, :].astype(BF16)
            ds, p = pair(qt, dot, lc_ref[rows, :], dc_ref[rows, :], kcat, vcat, mask)
            dq_t = _dot(ds, kcat, 1, 0)
            if pi == 0:
                dq_ref[rows, :] = dq_t
            else:
                dq_ref[rows, :] += dq_t
            dk2 = _dot(ds, qt, 0, 0)
            dv2 = _dot(p, dot, 0, 0)
            dk_ref[rows, :] += dk2[SPAN:, :]
            dv_ref[rows, :] += dv2[SPAN:, :]
            if u > 0:
                dk_ref[prows, :] += dk2[:SPAN, :]
                dv_ref[prows, :] += dv2[:SPAN, :]
        qi = _iota2((SPAN, SPAN), 0)
        ki = _iota2((SPAN, SPAN), 1)
        mask_next = (ki >= qi) & (ki < jnp.where(g == ng - 1, 0, SPAN))
        for r in DILATIONS:
            for rho in range(r):
                krows, qrows = _rows(r, GROUP // (SPAN * r) - 1, rho), _rows(r, 0, rho)
                qt, dot = qn_ref[qrows, :].astype(BF16), don_ref[qrows, :].astype(BF16)
                ds, p = pair(qt, dot, ln_ref[qrows, :], dn_ref[qrows, :], kc_ref[krows, :].astype(BF16),
                             vc_ref[krows, :].astype(BF16), mask_next)
                dk_ref[krows, :] += _dot(ds, qt, 0, 0)
                dv_ref[krows, :] += _dot(p, dot, 0, 0)

    cur = pl.BlockSpec((GROUP, HD), lambda g, h: (g, h))
    prev = pl.BlockSpec((GROUP, HD), lambda g, h: (jnp.maximum(g - 1, 0), h))
    nxt = pl.BlockSpec((GROUP, HD), lambda g, h: (jnp.minimum(g + 1, ng - 1), h))
    vcur = pl.BlockSpec((GROUP, HD), lambda g, h: (g, v_blk * N_HEADS + h))
    vprev = pl.BlockSpec((GROUP, HD), lambda g, h: (jnp.maximum(g - 1, 0), v_blk * N_HEADS + h))
    return pl.pallas_call(
        body, grid=(ng, N_HEADS), in_specs=[cur, cur, vcur, cur, cur, cur, prev, vprev] + [nxt] * 4,
        out_specs=[cur] * 3,
        out_shape=[_sds((t, GW), F32)] * 3, name=name,
        compiler_params=_params(2))(q, k, v, do, lse, delta, k, v, q, do, lse, delta)


def _merge(os_, ls_):
    m = jnp.maximum(jnp.maximum(ls_[0], ls_[1]), ls_[2])
    ws = [jnp.exp(l - m) for l in ls_]
    tot = ws[0] + ws[1] + ws[2]
    ob = (ws[0] * os_[0] + ws[1] * os_[1] + ws[2] * os_[2]) / tot
    return ob, m + jnp.log(tot)


def _gated_norm(oa, z, wv):
    return _head_rms(oa, wv) * _silu(z)


def _mix_fwd(name, oa_raw, proj, z_blk, ob, w_dn, w_an):
    t = oa_raw.shape[0]
    tm = min(256, t)

    def body(oa_ref, z_ref, ob_ref, wd_ref, wa_ref, mix_ref):
        for h in range(N_HEADS):
            sl = slice(h * HD, (h + 1) * HD)
            mix_ref[:, sl] = _gated_norm(oa_ref[:, sl], z_ref[:, sl], wd_ref[...]).astype(BF16)
            mix_ref[:, GW + h * HD:GW + (h + 1) * HD] = _head_rms(ob_ref[:, sl], wa_ref[...]).astype(BF16)

    vec = pl.BlockSpec((1, HD), lambda i: (0, 0))
    wide = pl.BlockSpec((tm, GW), lambda i: (i, 0))
    return pl.pallas_call(
        body, grid=(t // tm,),
        in_specs=[wide, pl.BlockSpec((tm, GW), lambda i: (i, z_blk)), wide, vec, vec],
        out_specs=pl.BlockSpec((tm, 2 * GW), lambda i: (i, 0)),
        out_shape=_sds((t, 2 * GW), BF16), name=name,
        compiler_params=_params(1))(oa_raw, proj, ob, w_dn, w_an)


def _mix_bwd(name, dmixed, oa_raw, proj, z_blk, ob, w_dn, w_an, dep):
    t = oa_raw.shape[0]
    tm = min(256, t)

    def body(dm_ref, oa_ref, z_ref, ob_ref, wd_ref, wa_ref, dep_ref,
             doa_ref, dz_ref, dob_ref, dl_ref, dwd_ref, dwa_ref):
        dwd = jnp.zeros((1, HD), F32)
        dwa = jnp.zeros((1, HD), F32)
        for h in range(N_HEADS):
            sl = slice(h * HD, (h + 1) * HD)
            _, vjp = jax.vjp(_gated_norm, oa_ref[:, sl], z_ref[:, sl], wd_ref[...])
            doa, dz, dw1 = vjp(dm_ref[:, sl])
            doa_ref[:, sl] = doa
            dz_ref[:, sl] = dz.astype(BF16)
            dwd = dwd + dw1
            obh = ob_ref[:, sl]
            _, vjp2 = jax.vjp(_head_rms, obh, wa_ref[...])
            dob, dw2 = vjp2(dm_ref[:, GW + h * HD:GW + (h + 1) * HD])
            dwa = dwa + dw2
            dob_ref[:, sl] = dob
            dl_ref[:, sl] = jnp.broadcast_to(jnp.sum(dob * obh, axis=1, keepdims=True), (tm, HD))

        @pl.when(pl.program_id(0) == 0)
        def _():
            dwd_ref[...] = jnp.zeros_like(dwd_ref)
            dwa_ref[...] = jnp.zeros_like(dwa_ref)

        dwd_ref[...] += dwd
        dwa_ref[...] += dwa

    vec = pl.BlockSpec((1, HD), lambda i: (0, 0))
    wide = pl.BlockSpec((tm, GW), lambda i: (i, 0))
    return pl.pallas_call(
        body, grid=(t // tm,),
        in_specs=[pl.BlockSpec((tm, 2 * GW), lambda i: (i, 0)), wide, pl.BlockSpec((tm, GW), lambda i: (i, z_blk)),
                  wide, vec, vec, ANY],
        out_specs=[wide, wide, wide, wide, vec, vec],
        out_shape=[_sds((t, GW), F32), _sds((t, GW), BF16), _sds((t, GW), F32), _sds((t, GW), F32),
                   _sds((1, HD), F32), _sds((1, HD), F32)], name=name,
        compiler_params=_params(1))(dmixed, oa_raw, proj, ob, w_dn, w_an, dep)


def _swiglu_fwd(name, gu3):
    _, t, f = gu3.shape
    tm, tn = min(512, t), 512

    def body(g_ref, o_ref):
        o_ref[...] = (_silu(g_ref[0]) * g_ref[1]).astype(BF16)

    return pl.pallas_call(
        body, grid=(t // tm, f // tn), in_specs=[pl.BlockSpec((2, tm, tn), lambda i, j: (0, i, j))],
        out_specs=pl.BlockSpec((tm, tn), lambda i, j: (i, j)), out_shape=_sds((t, f), BF16), name=name,
        compiler_params=_params(2))(gu3)


def _swiglu_bwd(name, gu3, dact):
    _, t, f = gu3.shape
    tm, tn = min(512, t), 512

    def body(g_ref, d_ref, o_ref):
        g, up, d = g_ref[0], g_ref[1], d_ref[...]
        sg = _sigmoid(g)
        o_ref[0] = (d * up * sg * (1.0 + g * (1.0 - sg))).astype(BF16)
        o_ref[1] = (d * g * sg).astype(BF16)

    return pl.pallas_call(
        body, grid=(t // tm, f // tn),
        in_specs=[pl.BlockSpec((2, tm, tn), lambda i, j: (0, i, j)), pl.BlockSpec((tm, tn), lambda i, j: (i, j))],
        out_specs=pl.BlockSpec((2, tm, tn), lambda i, j: (0, i, j)), out_shape=_sds((2, t, f), BF16), name=name,
        compiler_params=_params(2))(gu3, dact)


def _loss_head(name, y, target):
    t, d = y.shape
    tm = min(512, t)

    def body(y_ref, t_ref, dy_ref, l_ref):
        diff = y_ref[...] - t_ref[...]
        dy_ref[...] = diff * (1.0 / d)
        part = jnp.sum(jnp.sum(diff * diff, axis=1, keepdims=True), axis=0, keepdims=True) * (0.5 / d)

        @pl.when(pl.program_id(0) == 0)
        def _():
            l_ref[...] = jnp.zeros_like(l_ref)

        l_ref[...] += jnp.broadcast_to(part, (8, 128))

    row = pl.BlockSpec((tm, d), lambda i: (i, 0))
    return pl.pallas_call(body, grid=(t // tm,), in_specs=[row, row],
                          out_specs=[row, pl.BlockSpec((8, 128), lambda i: (0, 0))],
                          out_shape=[_sds((t, d), F32), _sds((8, 128), F32)], name=name,
                          compiler_params=_params(1))(y, target)


def _peer(me, k):
    pid = (me + k) % N_DEV
    return (pid // 4, (pid // 2) % 2, pid % 2)


def _my_id():
    return 4 * lax.axis_index("x") + 2 * lax.axis_index("y") + lax.axis_index("c")


def _exchange(name, arrays, scatter):
    n = len(arrays)

    def body(*refs):
        ins, outs = refs[:n], refs[n:2 * n]
        send_sems, recv_sems, local_sems = refs[2 * n:]
        me = _my_id()
        started = []
        for a in range(n):
            src = ins[a].at[me] if scatter[a] else ins[a]
            loc = pltpu.make_async_copy(src, outs[a].at[me], local_sems.at[a])
            loc.start()
            started.append(loc)
        remote = []
        for k in range(1, N_DEV):
            to = (me + k) % N_DEV
            for a in range(n):
                src = ins[a].at[to] if scatter[a] else ins[a]
                cp = pltpu.make_async_remote_copy(src_ref=src, dst_ref=outs[a].at[me],
                                                  send_sem=send_sems.at[a * (N_DEV - 1) + k - 1], recv_sem=recv_sems.at[a * (N_DEV - 1) + k - 1],
                                                  device_id=_peer(me, k), device_id_type=pl.DeviceIdType.MESH)
                cp.start()
                remote.append(cp)
        for k in range(1, N_DEV):
            frm = (me + N_DEV - k) % N_DEV
            for a in range(n):
                src = ins[a].at[frm] if scatter[a] else ins[a]
                pltpu.make_async_remote_copy(src_ref=src, dst_ref=outs[a].at[frm],
                                             send_sem=send_sems.at[a * (N_DEV - 1) + k - 1], recv_sem=recv_sems.at[a * (N_DEV - 1) + k - 1],
                                             device_id=_peer(me, k), device_id_type=pl.DeviceIdType.MESH).wait_recv()
        for cp in remote:
            cp.wait_send()
        for loc in started:
            loc.wait()

    out_shape = [_sds((N_DEV,) + (a.shape[1:] if sc else a.shape), a.dtype) for a, sc in zip(arrays, scatter)]
    return pl.pallas_call(
        body, in_specs=[ANY] * n, out_specs=[ANY] * n, out_shape=out_shape,
        scratch_shapes=[pltpu.SemaphoreType.DMA((n * (N_DEV - 1),)), pltpu.SemaphoreType.DMA((n * (N_DEV - 1),)),
                        pltpu.SemaphoreType.DMA((n,))],
        name=name)(*arrays)


HBM = pl.BlockSpec(memory_space=pltpu.HBM)
SEM = pl.BlockSpec(memory_space=pltpu.SEMAPHORE)
EFFECT = pltpu.SideEffectType.DATAFLOW_SIDE_EFFECTING


def _remote_copies(srcs, lands, scatter, send_sems, recv_sems, me, incoming):
    out = []
    for k in range(1, N_DEV):
        other = (me + N_DEV - k) % N_DEV if incoming else (me + k) % N_DEV
        for a in range(len(srcs)):
            sem = a * (N_DEV - 1) + k - 1
            src = srcs[a].at[other] if scatter[a] else srcs[a]
            dst = lands[a].at[other if incoming else me]
            out.append(pltpu.make_async_remote_copy(src_ref=src, dst_ref=dst, send_sem=send_sems.at[sem],
                                                    recv_sem=recv_sems.at[sem], device_id=_peer(me, k),
                                                    device_id_type=pl.DeviceIdType.MESH))
    return out


def _exchange_start(name, arrays, scatter, dep):
    n = len(arrays)
    lands = [lax.empty((N_DEV,) + (a.shape[1:] if sc else a.shape), a.dtype) for a, sc in zip(arrays, scatter)]

    def body(*refs):
        srcs, land_refs = refs[:n], refs[n:2 * n]
        send_sems, recv_sems = refs[2 * n + 1], refs[2 * n + 2]
        token = refs[-1]
        for cp in _remote_copies(srcs, land_refs, scatter, send_sems, recv_sems, _my_id(), False):
            cp.start()
        token[...] = jnp.zeros_like(token)

    n_sem = n * (N_DEV - 1)
    out_shape = ([pltpu.SemaphoreType.DMA((n_sem,)), pltpu.SemaphoreType.DMA((n_sem,))]
                 + [pltpu.HBM(a.shape, a.dtype) for a in arrays] + [pltpu.HBM(l.shape, l.dtype) for l in lands]
                 + [_sds((8, 128), F32)])
    aliases = {i: 2 + i for i in range(2 * n)}
    args = [pltpu.with_memory_space_constraint(a, pltpu.HBM) for a in list(arrays) + lands] + [dep]
    res = pl.pallas_call(
        body, name=name, in_specs=[HBM] * (2 * n) + [ANY], out_shape=out_shape,
        out_specs=[SEM, SEM] + [HBM] * (2 * n) + [pl.BlockSpec(memory_space=pltpu.VMEM)],
        input_output_aliases=aliases, compiler_params=pltpu.CompilerParams(has_side_effects=EFFECT))(*args)
    return dict(send=res[0], recv=res[1], srcs=res[2:2 + n], lands=res[2 + n:2 + 2 * n], token=res[-1],
                scatter=scatter)


def _exchange_wait(name, started, after):
    n = len(started["srcs"])
    scatter = started["scatter"]

    def body(*refs):
        srcs, land_refs = refs[:n], refs[n:2 * n]
        send_sems, recv_sems = refs[2 * n], refs[2 * n + 1]
        me = _my_id()
        for cp in _remote_copies(srcs, land_refs, scatter, send_sems, recv_sems, me, False):
            cp.wait_send()
        for cp in _remote_copies(srcs, land_refs, scatter, send_sems, recv_sems, me, True):
            cp.wait_recv()

    arrs = list(started["srcs"]) + list(started["lands"])
    res = pl.pallas_call(
        body, name=name, in_specs=[HBM] * (2 * n) + [SEM, SEM, ANY],
        out_shape=[pltpu.HBM(a.shape, a.dtype) for a in arrs], out_specs=[HBM] * (2 * n),
        input_output_aliases={i: i for i in range(2 * n)},
        compiler_params=pltpu.CompilerParams(has_side_effects=EFFECT))(*arrs, started["send"], started["recv"], after)
    me = _my_id()
    out = []
    for src, land, sc in zip(res[:n], res[n:], scatter):
        own = lax.dynamic_index_in_dim(src, me, 0, keepdims=True) if sc else src[None]
        out.append(lax.dynamic_update_slice(land, own, (me,) + (0,) * (land.ndim - 1)))
    return out


def _adamw(name, parts, w, m, v):
    r, c = w.shape
    tr = r
    for cand in (128, 88, 64, 40, 8):
        if r % cand == 0:
            tr = cand
            break
    c1 = 1.0 / (1.0 - ADAM_B1 ** ADAM_STEP)
    c2 = 1.0 / (1.0 - ADAM_B2 ** ADAM_STEP)

    def body(p_ref, w_ref, m_ref, v_ref, g_ref, d_ref, nm_ref, nv_ref):
        g = p_ref[0]
        for s in range(1, N_DEV):
            g = g + p_ref[s]
        mn = ADAM_B1 * m_ref[...] + (1.0 - ADAM_B1) * g
        vn = ADAM_B2 * v_ref[...] + (1.0 - ADAM_B2) * (g * g)
        g_ref[...] = g
        nm_ref[...] = mn
        nv_ref[...] = vn
        d_ref[...] = -ADAM_LR * ((mn * c1) / (jnp.sqrt(vn * c2) + ADAM_EPS) + ADAM_WD * w_ref[...])

    blk = pl.BlockSpec((tr, c), lambda i: (i, 0))
    return pl.pallas_call(
        body, grid=(r // tr,), in_specs=[pl.BlockSpec((N_DEV, tr, c), lambda i: (0, i, 0)), blk, blk, blk],
        out_specs=[blk] * 4, out_shape=[_sds((r, c), F32)] * 4, name=name,
        compiler_params=_params(1, VMEM_LIMIT))(parts, w, m, v)


def _pad_rows(a, rows):
    return jnp.pad(a, ((0, rows - a.shape[0]), (0, 0)))


def _lane_row(vec8, offset):
    return jnp.pad(vec8.reshape(1, 8), ((0, 0), (offset, HD - 8 - offset)))


def kernel(x, positions, attn_norm_w, w_in, conv_w, a_log, dt_bias, delta_out_norm_w, q_norm_w, k_norm_w, attn_out_norm_w, w_out, ffn_norm_w, w_gate_up, w_down, loss_target, m_attn_norm_w, m_w_in, m_conv_w, m_a_log, m_dt_bias, m_delta_out_norm_w, m_q_norm_w, m_k_norm_w, m_attn_out_norm_w, m_w_out, m_ffn_norm_w, m_w_gate_up, m_w_down, v_attn_norm_w, v_w_in, v_conv_w, v_a_log, v_dt_bias, v_delta_out_norm_w, v_q_norm_w, v_k_norm_w, v_attn_out_norm_w, v_w_out, v_ffn_norm_w, v_w_gate_up, v_w_down):
    x2 = x[0]
    t, d = x2.shape
    target = loss_target[0]
    pos_col = positions.reshape(t, 1)
    half = HD // 2
    inv = (ROPE_THETA ** (-np.arange(half, dtype=np.float32) / half)).astype(np.float32)
    inv_row = jnp.asarray(np.concatenate([inv, inv]).reshape(1, HD))

    n_in = w_in.shape[2]
    n_gu = w_gate_up.shape[2]
    w_in_g, conv_g = _exchange("gather_in", [w_in[0].astype(BF16), _pad_rows(conv_w[0], 8)], [False] * 2)
    ffn_own = [w_gate_up[0].astype(BF16), w_down[0].astype(BF16), w_out[0].astype(BF16)]
    ffn_fly = _exchange_start("gather_ffn_start", ffn_own, [False] * 3, conv_g)
    w_in_full = jnp.transpose(w_in_g, (1, 0, 2)).reshape(d, N_DEV * n_in)
    n_main = 4 * GW
    n_small = 2 * N_HEADS
    w_cat = jnp.concatenate([w_in_full[:, :n_main], w_in_full[:, n_main + n_small:],
                             w_in_full[:, n_main:n_main + n_small],
                             jnp.zeros((d, HD - n_small), BF16)], axis=1)
    n_cat = w_cat.shape[1]
    small_blk = (7 * GW) // HD
    conv_w8 =jnp.transpose(conv_g, (1, 0, 2)).reshape(8, 3 * GW)
    alog_row = _lane_row(a_log[0], 8)
    dtb_row = _lane_row(dt_bias[0], 8)

    tm = min(2048, t)
    h1 = _rms_fwd("norm1", x2, attn_norm_w, ffn_fly["token"])
    tn = 384
    proj = _mm("in_proj", h1, w_cat, grid=(t // tm, n_cat // tn, 1),
               a_spec=pl.BlockSpec((tm, d), lambda i, j, k: (i, 0)),
               b_spec=pl.BlockSpec((d, tn), lambda i, j, k: (0, j)),
               o_spec=pl.BlockSpec((tm, tn), lambda i, j, k: (i, j)),
               out_shape=_sds((t, n_cat), F32), ca=1, cb=0, nk=1)
    qn = _conv_fwd("conv_q", proj, conv_w8, 0, True, HD ** -0.5)
    kn = _conv_fwd("conv_k", proj, conv_w8, 1, True, 1.0)
    vv = _conv_fwd("conv_v", proj, conv_w8, 2, False, 1.0)
    beta_b, gc_b = _gates_fwd("gates", proj, small_blk, alog_row, dtb_row)
    u, w, p, tinv, qd, kd = _delta_prep("delta_prep", qn, kn, vv, beta_b, gc_b)
    oa_raw, vn, s_hist = _delta_scan("delta_scan", u, w, p, qd, kd, gc_b)

    aq = _qk_fwd("attn_q", proj, 4, q_norm_w, pos_col, inv_row)
    ak = _qk_fwd("attn_k", proj, 5, k_norm_w, pos_col, inv_row)
    ob, lse = _attn_fwd("attn_fwd", aq, ak, proj, 6)
    mixed = _mix_fwd("mix", oa_raw, proj, 3, ob, delta_out_norm_w, attn_out_norm_w)
    w_gu_g, w_down_g, w_out_g = _exchange_wait("gather_ffn_wait", ffn_fly, mixed)
    w_down_full = w_down_g.reshape(D_FF, d)
    w_out_full = w_out_g.reshape(2 * GW, d)
    tn = 512
    x1 = _mm("out_proj", mixed, w_out_full, grid=(t // tm, d // tn, 1),
             a_spec=pl.BlockSpec((tm, 2 * GW), lambda i, j, k: (i, 0)),
             b_spec=pl.BlockSpec((2 * GW, tn), lambda i, j, k: (0, j)),
             o_spec=pl.BlockSpec((tm, tn), lambda i, j, k: (i, j)),
             add=x2, add_spec=pl.BlockSpec((tm, tn), lambda i, j, k: (i, j)),
             out_shape=_sds((t, d), F32), ca=1, cb=0, nk=1)
    h2 = _rms_fwd("norm2", x1, ffn_norm_w, ffn_norm_w)
    per = N_DEV // 2
    tmd = min(1024, t)
    gu3 = _mm("gate_up", h2, w_gu_g, grid=(t // tmd, N_DEV, 1),
              a_spec=pl.BlockSpec((tmd, d), lambda i, j, k: (i, 0)),
              b_spec=pl.BlockSpec((None, d, n_gu), lambda i, j, k: (j, 0, 0)),
              o_spec=pl.BlockSpec((None, tmd, n_gu), lambda i, j, k: (j // per, i, j % per)),
              out_shape=_sds((2, t, D_FF), F32), ca=1, cb=0, nk=1)
    act = _swiglu_fwd("swiglu", gu3)
    tmd, tkd = min(1024, t), D_FF // 2
    y = _mm("down_proj", act, w_down_full, grid=(t // tmd, d // tn, 2),
            a_spec=pl.BlockSpec((tmd, tkd), lambda i, j, k: (i, k)),
            b_spec=pl.BlockSpec((tkd, tn), lambda i, j, k: (k, j)),
            o_spec=pl.BlockSpec((tmd, tn), lambda i, j, k: (i, j)),
            add=x1, add_spec=pl.BlockSpec((tmd, tn), lambda i, j, k: (i, j)),
            out_shape=_sds((t, d), F32), ca=1, cb=0, nk=2)
    dy, loss_tile = _loss_head("loss_head", y, target)
    loss = lax.psum(loss_tile[0, 0], ("x", "y", "c"))

    dy16 = dy.astype(BF16)
    dact = _mm("d_act", dy16, w_down_full, grid=(t // tm, D_FF // tn, 1),
               a_spec=pl.BlockSpec((tm, d), lambda i, j, k: (i, 0)),
               b_spec=pl.BlockSpec((tn, d), lambda i, j, k: (j, 0)),
               o_spec=pl.BlockSpec((tm, tn), lambda i, j, k: (i, j)),
               out_shape=_sds((t, D_FF), F32), ca=1, cb=1, nk=1)
    tk = min(2048, t)
    nkt = t // tk
    g_down = _mm("g_down", act, dy16, grid=(D_FF // 512, 1, nkt),
                 a_spec=pl.BlockSpec((tk, 512), lambda i, j, k: (k, i)),
                 b_spec=pl.BlockSpec((tk, d), lambda i, j, k: (k, 0)),
                 o_spec=pl.BlockSpec((512, d), lambda i, j, k: (i, 0)),
                 out_shape=_sds((D_FF, d), F32), ca=0, cb=0, nk=nkt)
    dgu3 = _swiglu_bwd("swiglu_bwd", gu3, dact)
    dh2 = _mm("d_h2", dgu3, w_gu_g, grid=(t // tmd, d // tn, N_DEV),
              a_spec=pl.BlockSpec((None, tmd, n_gu), lambda i, j, k: (k // per, i, k % per)),
              b_spec=pl.BlockSpec((None, tn, n_gu), lambda i, j, k: (k, j, 0)),
              o_spec=pl.BlockSpec((tmd, tn), lambda i, j, k: (i, j)),
              out_shape=_sds((t, d), F32), ca=1, cb=1, nk=N_DEV)
    g_gu = _mm("g_gate_up", h2, dgu3, grid=(d // 512, N_DEV, nkt),
               a_spec=pl.BlockSpec((tk, 512), lambda i, j, k: (k, i)),
               b_spec=pl.BlockSpec((None, tk, n_gu), lambda i, j, k: (j // per, k, j % per)),
               o_spec=pl.BlockSpec((None, 512, n_gu), lambda i, j, k: (j, i, 0)),
               out_shape=_sds((N_DEV, d, n_gu), F32), ca=0, cb=0, nk=nkt)
    dx1, g_ffn_norm = _rms_bwd("norm2_bwd", x1, ffn_norm_w, dh2, dy)

    dx1_16 = dx1.astype(BF16)
    dmixed = _mm("d_mixed", dx1_16, w_out_full, grid=(t // tm, (2 * GW) // tn, 1),
                 a_spec=pl.BlockSpec((tm, d), lambda i, j, k: (i, 0)),
                 b_spec=pl.BlockSpec((tn, d), lambda i, j, k: (j, 0)),
                 o_spec=pl.BlockSpec((tm, tn), lambda i, j, k: (i, j)),
                 out_shape=_sds((t, 2 * GW), F32), ca=1, cb=1, nk=1)
    g_out = _mm("g_out", mixed, dx1_16, grid=((2 * GW) // 512, 1, nkt),
                a_spec=pl.BlockSpec((tk, 512), lambda i, j, k: (k, i)),
                b_spec=pl.BlockSpec((tk, d), lambda i, j, k: (k, 0)),
                o_spec=pl.BlockSpec((512, d), lambda i, j, k: (i, 0)),
                out_shape=_sds((2 * GW, d), F32), ca=0, cb=0, nk=nkt)
    ffn_g_fly = _exchange_start("reduce_ffn_start",
                                [g_gu, g_down.reshape(N_DEV, D_FF // N_DEV, d), g_out.reshape(N_DEV, (2 * GW) // N_DEV, d)],
                                [True] * 3, g_ffn_norm)
    doa, dz, dob, delta, g_dn, g_an = _mix_bwd("mix_bwd", dmixed, oa_raw, proj, 3, ob,
                                               delta_out_norm_w, attn_out_norm_w, ffn_g_fly["token"])
    d_aq, d_ak, d_av = _attn_bwd("attn_bwd", aq, ak, proj, 6, dob, lse, delta)
    daq, g_qn = _qk_bwd("attn_q_bwd", proj, 4, q_norm_w, pos_col, inv_row, d_aq)
    dak, g_kn = _qk_bwd("attn_k_bwd", proj, 5, k_norm_w, pos_col, inv_row, d_ak)
    dav = d_av.astype(BF16)

    dvn, dqd, dkd, dw, ddec = _delta_scan_bwd("delta_scan_bwd", doa, w, p, qd, kd, gc_b, vn, s_hist)
    dqn, dkn, dvv, dbeta_b, dg_b = _delta_prep_bwd("delta_prep_bwd", qn, kn, vv, beta_b, gc_b, tinv, u, w, vn,
                                                   doa, dvn, dqd, dkd, dw, ddec)
    dxq, gcw_q = _conv_bwd("conv_q_bwd", proj, conv_w8, dqn, 0, True, HD ** -0.5)
    dxk, gcw_k = _conv_bwd("conv_k_bwd", proj, conv_w8, dkn, 1, True, 1.0)
    dxv, gcw_v = _conv_bwd("conv_v_bwd", proj, conv_w8, dvv, 2, False, 1.0)
    dsmall, g_alog_row, g_dtb_row = _gates_bwd("gates_bwd", proj, small_blk, alog_row, dtb_row, dbeta_b, dg_b)
    dproj = jnp.concatenate([dxq, dxk, dxv, dz, daq, dak, dav, dsmall], axis=1)
    tnc = n_cat // 3
    g_cat = _mm("g_in", h1, dproj, grid=(d // 512, 3, nkt),
                a_spec=pl.BlockSpec((tk, 512), lambda i, j, k: (k, i)),
                b_spec=pl.BlockSpec((tk, tnc), lambda i, j, k: (k, j)),
                o_spec=pl.BlockSpec((512, tnc), lambda i, j, k: (i, j)),
                out_shape=_sds((d, n_cat), F32), ca=0, cb=0, nk=nkt)
    g_in_full = jnp.concatenate([g_cat[:, :n_main], g_cat[:, 7 * GW:7 * GW + n_small], g_cat[:, n_main:7 * GW]], axis=1)
    g_in_parts = jnp.transpose(g_in_full.reshape(d, N_DEV, n_in), (1, 0, 2))
    g_conv = jnp.concatenate([gcw_q, gcw_k, gcw_v], axis=1)
    n_cw = conv_w.shape[2]
    g_conv_parts = jnp.transpose(g_conv.reshape(8, N_DEV, n_cw), (1, 0, 2))
    in_g_fly = _exchange_start("reduce_in_start", [g_in_parts, g_conv_parts], [True] * 2, g_dtb_row)
    tkc = n_cat // 3
    dh1 = _mm("d_h1", dproj, w_cat, dep=in_g_fly["token"], grid=(t // tmd, d // tn, 3),
              a_spec=pl.BlockSpec((tmd, tkc), lambda i, j, k: (i, k)),
              b_spec=pl.BlockSpec((tn, tkc), lambda i, j, k: (j, k)),
              o_spec=pl.BlockSpec((tmd, tn), lambda i, j, k: (i, j)),
              out_shape=_sds((t, d), F32), ca=1, cb=1, nk=3)
    grad_x, g_attn_norm = _rms_bwd("norm1_bwd", x2, attn_norm_w, dh1, dx1)

    small_rows = [g_attn_norm.reshape(d // HD, HD), g_ffn_norm.reshape(d // HD, HD), g_dn, g_qn, g_kn, g_an,
                  g_alog_row, g_dtb_row]
    small_pack = _pad_rows(jnp.concatenate(small_rows, axis=0), 40)
    (r_small,) = _exchange("gather_small_grads", [small_pack], [False])

    def pack_small(an, fn, dn, qn_, kn_, aon, al, db):
        rows = [an.reshape(d // HD, HD), fn.reshape(d // HD, HD), dn, qn_, kn_, aon,
                _lane_row(al[0], 8), _lane_row(db[0], 8)]
        return _pad_rows(jnp.concatenate(rows, axis=0), 40)

    def unpack_small(pk):
        nr = d // HD
        return dict(attn_norm_w=pk[:nr].reshape(1, d), ffn_norm_w=pk[nr:2 * nr].reshape(1, d),
                    delta_out_norm_w=pk[2 * nr:2 * nr + 1], q_norm_w=pk[2 * nr + 1:2 * nr + 2],
                    k_norm_w=pk[2 * nr + 2:2 * nr + 3], attn_out_norm_w=pk[2 * nr + 3:2 * nr + 4],
                    a_log=pk[2 * nr + 4:2 * nr + 5, 8:16], dt_bias=pk[2 * nr + 5:2 * nr + 6, 8:16])

    res_small = _adamw("adamw_small", r_small,
                       pack_small(attn_norm_w, ffn_norm_w, delta_out_norm_w, q_norm_w, k_norm_w, attn_out_norm_w, a_log, dt_bias),
                       pack_small(m_attn_norm_w, m_ffn_norm_w, m_delta_out_norm_w, m_q_norm_w, m_k_norm_w, m_attn_out_norm_w, m_a_log, m_dt_bias),
                       pack_small(v_attn_norm_w, v_ffn_norm_w, v_delta_out_norm_w, v_q_norm_w, v_k_norm_w, v_attn_out_norm_w, v_a_log, v_dt_bias))
    small = [unpack_small(a) for a in res_small]
    r_gu, r_down, r_out = _exchange_wait("reduce_ffn_wait", ffn_g_fly, res_small[0])
    res_gu = [a[None] for a in _adamw("adamw_gate_up", r_gu, w_gate_up[0], m_w_gate_up[0], v_w_gate_up[0])]
    res_down = [a[None] for a in _adamw("adamw_down", r_down, w_down[0], m_w_down[0], v_w_down[0])]
    res_out = [a[None] for a in _adamw("adamw_out", r_out, w_out[0], m_w_out[0], v_w_out[0])]
    done = (res_gu[3][0, :1, :1] + res_down[3][0, :1, :1] + res_out[3][0, :1, :1])
    r_in, r_conv = _exchange_wait("reduce_in_wait", in_g_fly, done)
    res_in = [a[None] for a in _adamw("adamw_in", r_in, w_in[0], m_w_in[0], v_w_in[0])]
    res_conv =[a[None, :4] for a in _adamw("adamw_conv", r_conv, _pad_rows(conv_w[0], 8), _pad_rows(m_conv_w[0], 8),
                                            _pad_rows(v_conv_w[0], 8))]

    outs = [loss, grad_x[None]]
    for i in range(4):
        s = small[i]
        outs += [s["attn_norm_w"], res_in[i], res_conv[i], s["a_log"], s["dt_bias"], s["delta_out_norm_w"],
                 s["q_norm_w"], s["k_norm_w"], s["attn_out_norm_w"], res_out[i], s["ffn_norm_w"], res_gu[i],
                 res_down[i]]
    return tuple(outs)
```

```python
import functools

import numpy as np
import jax
import jax.numpy as jnp
from jax import lax
from jax.experimental import pallas as pl
from jax.experimental.pallas import tpu as pltpu

F32 = jnp.float32
BF16 = jnp.bfloat16

N_DEV = 8
N_HEADS = 8
HD = 128
GW = N_HEADS * HD
CHUNK = 64
PAIR = 2 * CHUNK
SPAN = 128
DILATIONS = (1, 4, 16)
ROPE_THETA = 10000.0
EPS = 1e-6
D_FF = 5632
ADAM_LR, ADAM_B1, ADAM_B2, ADAM_EPS, ADAM_WD, ADAM_STEP = 0.001, 0.9, 0.999, 1e-8, 0.01, 10
NEG = -1e30
VMEM_LIMIT = 56 * 1024 * 1024
ANY = pl.BlockSpec(memory_space=pl.ANY)
HEADS_PER_STEP = 8


def _params(n_grid, vmem=VMEM_LIMIT):
    return pltpu.CompilerParams(dimension_semantics=("arbitrary",) * n_grid, vmem_limit_bytes=vmem)


def _sds(shape, dtype):
    return jax.ShapeDtypeStruct(tuple(shape), dtype)


def _sigmoid(x):
    return 1.0 / (1.0 + jnp.exp(-x))


def _silu(x):
    return x * _sigmoid(x)


def _softplus(x):
    return jnp.maximum(x, 0.0) + jnp.log(1.0 + jnp.exp(-jnp.abs(x)))


def _dot(a, b, ca, cb, precision=None):
    return lax.dot_general(a, b, (((ca,), (cb,)), ((), ())), precision=precision,
                           preferred_element_type=F32)


def _b16(x):
    return x if x.dtype == BF16 else x.astype(BF16)


def _split(x):
    hi = x.astype(BF16)
    return hi, (x - hi.astype(F32)).astype(BF16)


def _dot3(a, b, ca, cb):
    a_hi, a_lo = _split(a)
    b_hi, b_lo = _split(b)
    return _dot(a_hi, b_hi, ca, cb) + (_dot(a_hi, b_lo, ca, cb) + _dot(a_lo, b_hi, ca, cb))


def _iota2(shape, axis):
    return lax.broadcasted_iota(jnp.int32, shape, axis)


def _mm(name, a, b, *, grid, a_spec, b_spec, o_spec, out_shape, ca, cb, nk, add=None, add_spec=None,
        dep=None, vmem=VMEM_LIMIT):
    has_add = add is not None
    n_in = 2 + has_add + (dep is not None)

    def body(*refs):
        a_ref, b_ref = refs[0], refs[1]
        e_ref = refs[2] if has_add else None
        o_ref = refs[n_in]
        part = _dot(_b16(a_ref[...]), _b16(b_ref[...]), ca, cb)
        if nk == 1:
            if has_add:
                part = part + e_ref[...]
            o_ref[...] = part.astype(o_ref.dtype)
            return
        acc = refs[-1]
        k = pl.program_id(2)

        @pl.when(k == 0)
        def _():
            acc[...] = part

        @pl.when(k > 0)
        def _():
            acc[...] += part

        @pl.when(k == nk - 1)
        def _():
            res = acc[...]
            if has_add:
                res = res + e_ref[...]
            o_ref[...] = res.astype(o_ref.dtype)

    in_specs = [a_spec, b_spec] + ([add_spec] if has_add else []) + ([ANY] if dep is not None else [])
    args = (a, b) + ((add,) if has_add else ()) + ((dep,) if dep is not None else ())
    blk = [d for d in o_spec.block_shape if d is not None]
    scratch = [pltpu.VMEM(tuple(blk), F32)] if nk > 1 else []
    return pl.pallas_call(body, grid=grid, in_specs=in_specs, out_specs=o_spec, out_shape=out_shape,
                          scratch_shapes=scratch, name=name, compiler_params=_params(3, vmem))(*args)


def _rms_f(xv, wv):
    return xv * lax.rsqrt(jnp.mean(xv * xv, axis=-1, keepdims=True) + EPS) * wv


def _rms_fwd(name, x, w, dep):
    t, d = x.shape
    tm = min(512, t)

    def body(x_ref, w_ref, dep_ref, o_ref):
        o_ref[...] = _rms_f(x_ref[...], w_ref[...]).astype(BF16)

    row = pl.BlockSpec((tm, d), lambda i: (i, 0))
    vec = pl.BlockSpec((1, d), lambda i: (0, 0))
    return pl.pallas_call(body, grid=(t // tm,), in_specs=[row, vec, ANY], out_specs=row,
                          out_shape=_sds((t, d), BF16), name=name, compiler_params=_params(1))(x, w, dep)


def _rms_bwd(name, x, w, dh, res):
    t, d = x.shape
    tm = min(256, t)

    def body(x_ref, w_ref, dh_ref, res_ref, dx_ref, dw_ref):
        _, vjp = jax.vjp(_rms_f, x_ref[...], w_ref[...])
        dxv, dwv = vjp(dh_ref[...])
        dx_ref[...] = dxv + res_ref[...]

        @pl.when(pl.program_id(0) == 0)
        def _():
            dw_ref[...] = jnp.zeros_like(dw_ref)

        dw_ref[...] += dwv

    row = pl.BlockSpec((tm, d), lambda i: (i, 0))
    vec = pl.BlockSpec((1, d), lambda i: (0, 0))
    return pl.pallas_call(body, grid=(t // tm,), in_specs=[row, vec, row, row], out_specs=[row, vec],
                          out_shape=[_sds((t, d), F32), _sds((1, d), F32)], name=name,
                          compiler_params=_params(1))(x, w, dh, res)


def _conv_taps(xv, w_ref, rows):
    c = w_ref[3:4, :] * xv
    for s in (1, 2, 3):
        c = c + w_ref[3 - s:4 - s, :] * jnp.where(rows >= s, pltpu.roll(xv, s, 0), 0.0)
    return c


def _post_conv(c, l2, scale):
    y = _silu(c)
    if l2:
        y = y * lax.rsqrt(jnp.sum(y * y, axis=-1, keepdims=True) + EPS) * scale
    return y


def _conv_fwd(name, proj, conv_w8, group, l2, scale):
    t = proj.shape[0]

    def body(x_ref, w_ref, o_ref):
        rows = _iota2((t, HD), 0)
        o_ref[...] = _post_conv(_conv_taps(x_ref[...], w_ref, rows), l2, scale)

    return pl.pallas_call(
        body, grid=(N_HEADS,),
        in_specs=[pl.BlockSpec((t, HD), lambda h: (0, h + group * N_HEADS)),
                  pl.BlockSpec((8, HD), lambda h: (0, h + group * N_HEADS))],
        out_specs=pl.BlockSpec((t, HD), lambda h: (0, h)),
        out_shape=_sds((t, GW), F32), name=name, compiler_params=_params(1, VMEM_LIMIT))(proj, conv_w8)


def _conv_bwd(name, proj, conv_w8, dn, group, l2, scale):
    t = proj.shape[0]

    def body(x_ref, w_ref, dn_ref, dx_ref, dw_ref):
        rows = _iota2((t, HD), 0)
        xv = x_ref[...]
        c = _conv_taps(xv, w_ref, rows)
        _, vjp = jax.vjp(lambda cc: _post_conv(cc, l2, scale), c)
        (dc,) = vjp(dn_ref[...])
        dx = w_ref[3:4, :] * dc
        dw = jnp.zeros((8, HD), F32)
        rid = _iota2((8, HD), 0)
        dw = dw + jnp.where(rid == 3, jnp.sum(dc * xv, axis=0, keepdims=True), 0.0)
        for s in (1, 2, 3):
            dx = dx + w_ref[3 - s:4 - s, :] * jnp.where(rows < t - s, pltpu.roll(dc, t - s, 0), 0.0)
            xs = jnp.where(rows >= s, pltpu.roll(xv, s, 0), 0.0)
            dw = dw + jnp.where(rid == 3 - s, jnp.sum(dc * xs, axis=0, keepdims=True), 0.0)
        dx_ref[...] = dx.astype(BF16)
        dw_ref[...] = dw

    return pl.pallas_call(
        body, grid=(N_HEADS,),
        in_specs=[pl.BlockSpec((t, HD), lambda h: (0, h + group * N_HEADS)),
                  pl.BlockSpec((8, HD), lambda h: (0, h + group * N_HEADS)),
                  pl.BlockSpec((t, HD), lambda h: (0, h))],
        out_specs=[pl.BlockSpec((t, HD), lambda h: (0, h)), pl.BlockSpec((8, HD), lambda h: (0, h))],
        out_shape=[_sds((t, GW), BF16), _sds((8, GW), F32)], name=name,
        compiler_params=_params(1, VMEM_LIMIT))(proj, conv_w8, dn)


def _chunk_cumsum(g, rows):
    pos = rows % CHUNK
    s = 1
    while s < CHUNK:
        g = g + jnp.where(pos >= s, pltpu.roll(g, s, 0), 0.0)
        s *= 2
    return g


def _gates_fwd(name, proj, small_blk, alog_row, dtb_row):
    t = proj.shape[0]
    tm = min(256, t)

    def body(s_ref, a_ref, b_ref, beta_ref, gc_ref):
        sm = s_ref[...]
        beta = _sigmoid(sm)
        g = -jnp.exp(a_ref[...]) * _softplus(sm + b_ref[...])
        gc = _chunk_cumsum(g, _iota2((tm, HD), 0))
        lane = _iota2((tm, HD), 1)
        for h in range(N_HEADS):
            bcol = jnp.sum(jnp.where(lane == h, beta, 0.0), axis=1, keepdims=True)
            gcol = jnp.sum(jnp.where(lane == 8 + h, gc, 0.0), axis=1, keepdims=True)
            beta_ref[:, h * HD:(h + 1) * HD] = jnp.broadcast_to(bcol, (tm, HD))
            gc_ref[:, h * HD:(h + 1) * HD] = jnp.broadcast_to(gcol, (tm, HD))

    vec = pl.BlockSpec((1, HD), lambda i: (0, 0))
    wide = pl.BlockSpec((tm, GW), lambda i: (i, 0))
    return pl.pallas_call(
        body, grid=(t // tm,),
        in_specs=[pl.BlockSpec((tm, HD), lambda i: (i, small_blk)), vec, vec], out_specs=[wide, wide],
        out_shape=[_sds((t, GW), F32), _sds((t, GW), F32)], name=name,
        compiler_params=_params(1))(proj, alog_row, dtb_row)


def _gates_bwd(name, proj, small_blk, alog_row, dtb_row, dbeta_b, dg_b):
    t = proj.shape[0]
    tm = min(256, t)

    def body(s_ref, a_ref, b_ref, db_ref, dg_ref, ds_ref, da_ref, dbias_ref):
        sm = s_ref[...]
        lane = _iota2((tm, HD), 1)
        db = jnp.zeros((tm, HD), F32)
        dg = jnp.zeros((tm, HD), F32)
        for h in range(N_HEADS):
            db = db + jnp.where(lane == h, db_ref[:, h * HD:(h + 1) * HD], 0.0)
            dg = dg + jnp.where(lane == 8 + h, dg_ref[:, h * HD:(h + 1) * HD], 0.0)
        beta = _sigmoid(sm)
        ea = jnp.exp(a_ref[...])
        pre = sm + b_ref[...]
        g = -ea * _softplus(pre)
        dpre = dg * (-ea) * _sigmoid(pre)
        ds_ref[...] = (db * beta * (1.0 - beta) + dpre).astype(BF16)

        @pl.when(pl.program_id(0) == 0)
        def _():
            da_ref[...] = jnp.zeros_like(da_ref)
            dbias_ref[...] = jnp.zeros_like(dbias_ref)

        da_ref[...] += jnp.sum(dg * g, axis=0, keepdims=True)
        dbias_ref[...] += jnp.sum(dpre, axis=0, keepdims=True)

    vec = pl.BlockSpec((1, HD), lambda i: (0, 0))
    wide = pl.BlockSpec((tm, GW), lambda i: (i, 0))
    return pl.pallas_call(
        body, grid=(t // tm,),
        in_specs=[pl.BlockSpec((tm, HD), lambda i: (i, small_blk)), vec, vec, wide, wide],
        out_specs=[pl.BlockSpec((tm, HD), lambda i: (i, 0)), vec, vec],
        out_shape=[_sds((t, HD), BF16), _sds((1, HD), F32), _sds((1, HD), F32)], name=name,
        compiler_params=_params(1))(proj, alog_row, dtb_row, dbeta_b, dg_b)


def _pair_masks():
    ii = _iota2((PAIR, PAIR), 0)
    jj = _iota2((PAIR, PAIR), 1)
    same = (ii // CHUNK) == (jj // CHUNK)
    return ii, jj, same & (ii >= jj), same & (ii > jj)


def _to_row(col_b, ii, jj):
    return jnp.sum(jnp.where(ii == jj, col_b, 0.0), axis=0, keepdims=True)


def _to_col(row, ii, jj):
    return jnp.sum(jnp.where(ii == jj, jnp.broadcast_to(row, (PAIR, PAIR)), 0.0), axis=1, keepdims=True)


def _decay_parts(gc, last_a, last_b, ii, jj, causal):
    diff = gc - _to_row(gc, ii, jj)
    dmat = jnp.where(causal, jnp.exp(jnp.where(causal, diff, 0.0)), 0.0)
    glast = jnp.where(ii < CHUNK, last_a, last_b)
    return dmat, jnp.exp(gc), jnp.exp(glast - gc)


def _unit_lower_inverse(lows, ii, jj):
    eye = jnp.where(ii == jj, 1.0, 0.0)
    mm = lambda xs, ys: [_dot3(a, b, 1, 0) for a, b in zip(xs, ys)]
    plus = lambda xs: [eye + a for a in xs]
    minus = lambda xs: [eye - a for a in xs]
    d1 = [jnp.where((ii // 16) == (jj // 16), low, 0.0) for low in lows]
    d2 = mm(d1, d1)
    a = mm(minus(d1), plus(d2))
    d4 = mm(d2, d2)
    a = mm(a, plus(d4))
    d8 = mm(d4, d4)
    td = mm(a, plus(d8))
    n1 = mm(td, [low - d for low, d in zip(lows, d1)])
    n2 = mm(n1, n1)
    return mm(mm(minus(n1), plus(n2)), td)


def _delta_prep(name, qn, kn, vv, beta_b, gc_b):
    t = qn.shape[0]

    def body(q_ref, k_ref, v_ref, b_ref, g_ref, u_ref, w_ref, p_ref, t_ref, qd_ref, kd_ref):
        ii, jj, causal, strict = _pair_masks()
        sls = [slice(hh * HD, (hh + 1) * HD) for hh in range(HEADS_PER_STEP)]
        lows = []
        for sl in sls:
            q, k, beta = q_ref[:, sl], k_ref[:, sl], b_ref[:, sl]
            dmat, gam, e2 = _decay_parts(g_ref[:, sl], g_ref[CHUNK - 1:CHUNK, sl], g_ref[PAIR - 1:PAIR, sl],
                                         ii, jj, causal)
            k16 = _b16(k)
            lows.append(jnp.where(strict, beta * _dot(k16, k16, 1, 1) * dmat, 0.0))
            p_ref[:, sl] = jnp.where(causal, _dot(_b16(q), k16, 1, 1) * dmat, 0.0).astype(BF16)
            qd_ref[:, sl] = (q * gam).astype(BF16)
            kd_ref[:, sl] = (k * e2).astype(BF16)
        for sl, tinv in zip(sls, _unit_lower_inverse(lows, ii, jj)):
            beta = b_ref[:, sl]
            t_ref[:, sl] = tinv
            u_ref[:, sl] = _dot3(tinv, v_ref[:, sl] * beta, 1, 0)
            w_ref[:, sl] = _dot3(tinv, k_ref[:, sl] * (beta * jnp.exp(g_ref[:, sl])), 1, 0).astype(BF16)

    blk = pl.BlockSpec((PAIR, HEADS_PER_STEP * HD), lambda i, h: (i, h))
    return pl.pallas_call(
        body, grid=(t // PAIR, N_HEADS // HEADS_PER_STEP), in_specs=[blk] * 5, out_specs=[blk] * 6,
        out_shape=[_sds((t, GW), F32), _sds((t, GW), BF16), _sds((t, GW), BF16), _sds((t, GW), F32),
                   _sds((t, GW), BF16), _sds((t, GW), BF16)],
        name=name, compiler_params=_params(2))(qn, kn, vv, beta_b, gc_b)


def _delta_scan(name, u, w, p, qd, kd, gc_b):
    t = u.shape[0]
    n = t // CHUNK

    def body(u_ref, w_ref, p_ref, qd_ref, kd_ref, g_ref, o_ref, vn_ref, sh_ref, state):
        @pl.when(pl.program_id(0) == 0)
        def _():
            state[...] = jnp.zeros_like(state)

        for h in range(N_HEADS):
            sl = slice(h * HD, (h + 1) * HD)
            s = state[h]
            sh_ref[h] = s
            s16 = _b16(s)
            vnew = u_ref[:, sl] - _dot(w_ref[:, sl], s16, 1, 0)
            vn16 = _b16(vnew)
            vpair = jnp.concatenate([vn16, vn16], axis=0)
            o_ref[:, sl] = _dot(qd_ref[:, sl], s16, 1, 0) + _dot(p_ref[:, sl], vpair, 1, 0)
            vn_ref[:, sl] = vn16
            dec = jnp.exp(g_ref[CHUNK - 1:CHUNK, sl])
            state[h] = s * dec + _dot(kd_ref[:, sl], vn16, 0, 0)

    blk = pl.BlockSpec((CHUNK, GW), lambda i: (i, 0))
    return pl.pallas_call(
        body, grid=(n,), in_specs=[blk] * 6,
        out_specs=[blk, blk, pl.BlockSpec((None, N_HEADS, HD, HD), lambda i: (i, 0, 0, 0))],
        out_shape=[_sds((t, GW), F32), _sds((t, GW), BF16), _sds((n, N_HEADS, HD, HD), F32)],
        scratch_shapes=[pltpu.VMEM((N_HEADS, HD, HD), F32)], name=name,
        compiler_params=_params(1))(u, w, p, qd, kd, gc_b)


def _delta_scan_bwd(name, do, w, p, qd, kd, gc_b, vn, s_hist):
    t = do.shape[0]
    n = t // CHUNK

    def body(do_ref, w_ref, p_ref, qd_ref, kd_ref, g_ref, vn_ref, sh_ref,
             dvn_ref, dqd_ref, dkd_ref, dw_ref, ddec_ref, dstate):
        @pl.when(pl.program_id(0) == 0)
        def _():
            dstate[...] = jnp.zeros_like(dstate)

        for h in range(N_HEADS):
            sl = slice(h * HD, (h + 1) * HD)
            ds = dstate[h]
            ds16 = _b16(ds)
            s_in = sh_ref[h]
            s16 = _b16(s_in)
            do16 = _b16(do_ref[:, sl])
            ptdo = _dot(p_ref[:, sl], do16, 0, 0)
            dvn = ptdo[:CHUNK, :] + ptdo[CHUNK:, :] + _dot(kd_ref[:, sl], ds16, 1, 0)
            dvn16 = _b16(dvn)
            dec = jnp.exp(g_ref[CHUNK - 1:CHUNK, sl])
            dstate[h] = ds * dec + _dot(qd_ref[:, sl], do16, 0, 0) - _dot(w_ref[:, sl], dvn16, 0, 0)
            dvn_ref[:, sl] = dvn
            dqd_ref[:, sl] = _dot(do16, s16, 1, 1)
            dw_ref[:, sl] = -_dot(dvn16, s16, 1, 1)
            dkd_ref[:, sl] = _dot(vn_ref[:, sl], ds16, 1, 1)
            tot = jnp.sum(jnp.sum(s_in * ds, axis=1, keepdims=True), axis=0, keepdims=True)
            ddec_ref[:, sl] = jnp.broadcast_to(tot, (8, HD))

    blk = pl.BlockSpec((CHUNK, GW), lambda i: (n - 1 - i, 0))
    return pl.pallas_call(
        body, grid=(n,),
        in_specs=[blk] * 7 + [pl.BlockSpec((None, N_HEADS, HD, HD), lambda i: (n - 1 - i, 0, 0, 0))],
        out_specs=[blk] * 4 + [pl.BlockSpec((8, GW), lambda i: (n - 1 - i, 0))],
        out_shape=[_sds((t, GW), F32)] * 4 + [_sds((n * 8, GW), F32)],
        scratch_shapes=[pltpu.VMEM((N_HEADS, HD, HD), F32)], name=name,
        compiler_params=_params(1))(do, w, p, qd, kd, gc_b, vn, s_hist)


def _delta_prep_bwd(name, qn, kn, vv, beta_b, gc_b, tinv, u, w, vn, do, dvn, dqd, dkd, dw, ddec):
    t = qn.shape[0]

    def body(q_ref, k_ref, v_ref, b_ref, g_ref, t_ref, u_ref, w_ref, vn_ref, do_ref, dvn_ref, dqd_ref,
             dkd_ref, dw_ref, ddec_ref, dq_ref, dk_ref, dv_ref, dbeta_ref, dg_ref):
        ii, jj, causal, strict = _pair_masks()
        suffix = ((ii // CHUNK) == (jj // CHUNK)) & (jj >= ii)
        first = ii < CHUNK
        rs = lambda a: jnp.sum(a, axis=1, keepdims=True)
        for hh in range(HEADS_PER_STEP):
            sl = slice(hh * HD, (hh + 1) * HD)
            q, k, v, beta, gc = q_ref[:, sl], k_ref[:, sl], v_ref[:, sl], b_ref[:, sl], g_ref[:, sl]
            last_a, last_b = g_ref[CHUNK - 1:CHUNK, sl], g_ref[PAIR - 1:PAIR, sl]
            dmat, gam, e2 = _decay_parts(gc, last_a, last_b, ii, jj, causal)
            q16, k16 = _b16(q), _b16(k)
            kk = _dot(k16, k16, 1, 1)
            qk = _dot(q16, k16, 1, 1)
            dqd, dkd = dqd_ref[:, sl], dkd_ref[:, sl]
            dp = jnp.where(causal, _dot(_b16(do_ref[:, sl]), vn_ref[:, sl], 1, 1), 0.0)
            dpd16 = _b16(dp * dmat)
            tinv_v = t_ref[:, sl]
            x = _dot3(tinv_v, dvn_ref[:, sl], 0, 0)
            y = _dot3(tinv_v, dw_ref[:, sl], 0, 0)
            da = -jnp.where(strict, _dot(_b16(x), _b16(u_ref[:, sl]), 1, 1) + _dot(_b16(y), w_ref[:, sl], 1, 1), 0.0)
            dkk16 = _b16(da * beta * dmat)
            dq_ref[:, sl] = gam * dqd + _dot(dpd16, k16, 1, 0)
            dk_ref[:, sl] = (e2 * dkd + _dot(dpd16, q16, 0, 0) + beta * gam * y
                             + _dot(dkk16, k16, 1, 0) + _dot(dkk16, k16, 0, 0))
            dv_ref[:, sl] = beta * x
            dbeta = rs(v * x) + rs(k * gam * y) + rs(da * kk * dmat)
            dbeta_ref[:, sl] = jnp.broadcast_to(dbeta, (PAIR, HD))
            m = (dp * qk + da * beta * kk) * dmat
            dgam = rs(q * dqd) + rs(k * beta * y)
            de2 = rs(k * dkd)
            colsum = _to_col(jnp.sum(m, axis=0, keepdims=True), ii, jj)
            te2 = de2 * e2
            dgc = rs(m) - colsum + gam * dgam - te2
            tail_a = jnp.sum(jnp.where(first, te2, 0.0), axis=0, keepdims=True)
            tail_b = jnp.sum(jnp.where(first, 0.0, te2), axis=0, keepdims=True)
            dgc = dgc + jnp.where(ii == CHUNK - 1, tail_a + ddec_ref[0:1, sl] * jnp.exp(last_a), 0.0)
            dgc = dgc + jnp.where(ii == PAIR - 1, tail_b + ddec_ref[8:9, sl] * jnp.exp(last_b), 0.0)
            dgc_row = _to_row(dgc, ii, jj)
            dg = jnp.sum(jnp.where(suffix, jnp.broadcast_to(dgc_row, (PAIR, PAIR)), 0.0), axis=1, keepdims=True)
            dg_ref[:, sl] = jnp.broadcast_to(dg, (PAIR, HD))

    blk = pl.BlockSpec((PAIR, HEADS_PER_STEP * HD), lambda i, h: (i, h))
    return pl.pallas_call(
        body, grid=(t // PAIR, N_HEADS // HEADS_PER_STEP),
        in_specs=[blk] * 14 + [pl.BlockSpec((16, HEADS_PER_STEP * HD), lambda i, h: (i, h))], out_specs=[blk] * 5,
        out_shape=[_sds((t, GW), F32)] * 5, name=name,
        compiler_params=_params(2))(qn, kn, vv, beta_b, gc_b, tinv, u, w, vn, do, dvn, dqd, dkd, dw, ddec)


def _rope_tables(pos_col, inv_row):
    ang = pos_col.astype(F32) * inv_row
    lane = _iota2(ang.shape, 1)
    return jnp.cos(ang), jnp.where(lane < HD // 2, -1.0, 1.0) * jnp.sin(ang)


def _head_rms(xh, wv):
    return xh * lax.rsqrt(jnp.mean(xh * xh, axis=-1, keepdims=True) + EPS) * wv


def _qk_fwd(name, proj, blk_idx, w_row, pos_col, inv_row):
    t = proj.shape[0]
    tm = min(256, t)

    def body(x_ref, w_ref, pos_ref, inv_ref, o_ref):
        cos, sin = _rope_tables(pos_ref[...], inv_ref[...])
        for h in range(N_HEADS):
            y = _head_rms(x_ref[:, h * HD:(h + 1) * HD], w_ref[...])
            o_ref[:, h * HD:(h + 1) * HD] = y * cos + pltpu.roll(y, HD // 2, 1) * sin

    vec = pl.BlockSpec((1, HD), lambda i: (0, 0))
    return pl.pallas_call(
        body, grid=(t // tm,),
        in_specs=[pl.BlockSpec((tm, GW), lambda i: (i, blk_idx)), vec, pl.BlockSpec((tm, 1), lambda i: (i, 0)), vec],
        out_specs=pl.BlockSpec((tm, GW), lambda i: (i, 0)), out_shape=_sds((t, GW), F32), name=name,
        compiler_params=_params(1))(proj, w_row, pos_col, inv_row)


def _qk_bwd(name, proj, blk_idx, w_row, pos_col, inv_row, dy_full):
    t = proj.shape[0]
    tm = min(256, t)

    def body(x_ref, w_ref, pos_ref, inv_ref, dy_ref, dx_ref, dw_ref):
        cos, sin = _rope_tables(pos_ref[...], inv_ref[...])
        dw = jnp.zeros((1, HD), F32)
        for h in range(N_HEADS):
            sl = slice(h * HD, (h + 1) * HD)
            dy = dy_ref[:, sl]
            dy = dy * cos - pltpu.roll(dy, HD // 2, 1) * sin
            _, vjp = jax.vjp(_head_rms, x_ref[:, sl], w_ref[...])
            dx, dwh = vjp(dy)
            dw = dw + dwh
            dx_ref[:, sl] = dx.astype(BF16)

        @pl.when(pl.program_id(0) == 0)
        def _():
            dw_ref[...] = jnp.zeros_like(dw_ref)

        dw_ref[...] += dw

    vec = pl.BlockSpec((1, HD), lambda i: (0, 0))
    wide = pl.BlockSpec((tm, GW), lambda i: (i, 0))
    return pl.pallas_call(
        body, grid=(t // tm,),
        in_specs=[pl.BlockSpec((tm, GW), lambda i: (i, blk_idx)), vec, pl.BlockSpec((tm, 1), lambda i: (i, 0)), vec,
                  wide],
        out_specs=[wide, vec], out_shape=[_sds((t, GW), BF16), _sds((1, HD), F32)], name=name,
        compiler_params=_params(1))(proj, w_row, pos_col, inv_row, dy_full)


GROUP = SPAN * max(DILATIONS)
SCALE = HD ** -0.5


def _band_mask(lo):
    qi = _iota2((SPAN, 2 * SPAN), 0)
    ki = _iota2((SPAN, 2 * SPAN), 1)
    return (ki >= qi) & (ki <= qi + SPAN) & (ki >= lo)


def _tiles():
    return [(pi, r, u, rho) for pi, r in enumerate(DILATIONS) for u in range(GROUP // (SPAN * r)) for rho in range(r)]


def _rows(r, u, rho):
    return pl.ds(u * SPAN * r + rho, SPAN, stride=r) if r > 1 else pl.ds(u * SPAN, SPAN)


def _attn_fwd(name, q, k, v, v_blk):
    t = q.shape[0]

    def body(qc_ref, kc_ref, vc_ref, kp_ref, vp_ref, ob_ref, lse_ref, o_scr, l_scr):
        mask_in = _band_mask(0)
        mask_edge = _band_mask(jnp.where(pl.program_id(0) == 0, SPAN, 0))
        for pi, r, u, rho in _tiles():
            rows = _rows(r, u, rho)
            if u > 0:
                prows, kp_src, vp_src, mask = _rows(r, u - 1, rho), kc_ref, vc_ref, mask_in
            else:
                prows, kp_src, vp_src, mask = _rows(r, GROUP // (SPAN * r) - 1, rho), kp_ref, vp_ref, mask_edge
            kcat = jnp.concatenate([kp_src[prows, :], kc_ref[rows, :]], axis=0).astype(BF16)
            vcat = jnp.concatenate([vp_src[prows, :], vc_ref[rows, :]], axis=0).astype(BF16)
            s = jnp.where(mask, _dot(qc_ref[rows, :].astype(BF16), kcat, 1, 1) * SCALE, NEG)
            m = jnp.max(s, axis=1, keepdims=True)
            p = jnp.exp(s - m)
            den = jnp.sum(p, axis=1, keepdims=True)
            o_scr[pi, rows, :] = _dot(_b16(p), vcat, 1, 0) / den
            l_scr[pi, rows, :] = jnp.broadcast_to(m + jnp.log(den), (SPAN, HD))
        step = 256
        for c in range(GROUP // step):
            sl = pl.ds(c * step, step)
            ob, lse = _merge([o_scr[i, sl, :] for i in range(3)], [l_scr[i, sl, :] for i in range(3)])
            ob_ref[sl, :] = ob
            lse_ref[sl, :] = lse

    cur = pl.BlockSpec((GROUP, HD), lambda g, h: (g, h))
    prev = pl.BlockSpec((GROUP, HD), lambda g, h: (jnp.maximum(g - 1, 0), h))
    vcur = pl.BlockSpec((GROUP, HD), lambda g, h: (g, v_blk * N_HEADS + h))
    vprev = pl.BlockSpec((GROUP, HD), lambda g, h: (jnp.maximum(g - 1, 0), v_blk * N_HEADS + h))
    return pl.pallas_call(
        body, grid=(t // GROUP, N_HEADS), in_specs=[cur, cur, vcur, prev, vprev], out_specs=[cur, cur],
        out_shape=[_sds((t, GW), F32), _sds((t, GW), F32)],
        scratch_shapes=[pltpu.VMEM((3, GROUP, HD), F32), pltpu.VMEM((3, GROUP, HD), F32)], name=name,
        compiler_params=_params(2))(q, k, v, k, v)


def _attn_bwd(name, q, k, v, v_blk, do, lse, delta):
    t = q.shape[0]
    ng = t // GROUP

    def pair(qt, dot, lt, dlt, kcat, vcat, mask):
        wide = kcat.shape[0] // SPAN
        lw = jnp.concatenate([lt] * wide, axis=1) if wide > 1 else lt
        dw = jnp.concatenate([dlt] * wide, axis=1) if wide > 1 else dlt
        s = _dot(qt, kcat, 1, 1) * SCALE
        p = jnp.where(mask, jnp.exp(jnp.where(mask, s - lw, 0.0)), 0.0)
        ds = p * (_dot(dot, vcat, 1, 1) - dw) * SCALE
        return _b16(ds), _b16(p)

    def body(qc_ref, kc_ref, vc_ref, doc_ref, lc_ref, dc_ref, kp_ref, vp_ref, qn_ref, don_ref, ln_ref, dn_ref,
             dq_ref, dk_ref, dv_ref):
        g = pl.program_id(0)
        mask_in = _band_mask(0)
        mask_edge = _band_mask(jnp.where(g == 0, SPAN, 0))
        dk_ref[...] = jnp.zeros_like(dk_ref)
        dv_ref[...] = jnp.zeros_like(dv_ref)
        for pi, r, u, rho in _tiles():
            rows = _rows(r, u, rho)
            if u > 0:
                prows, kp_src, vp_src, mask = _rows(r, u - 1, rho), kc_ref, vc_ref, mask_in
            else:
                prows, kp_src, vp_src, mask = _rows(r, GROUP // (SPAN * r) - 1, rho), kp_ref, vp_ref, mask_edge
            kcat = jnp.concatenate([kp_src[prows, :], kc_ref[rows, :]], axis=0).astype(BF16)
            vcat = jnp.concatenate([vp_src[prows, :], vc_ref[rows, :]], axis=0).astype(BF16)
            qt, dot = qc_ref[rows, :].astype(BF16), doc_ref[rows, :].astype(BF16)
            ds, p = pair(qt, dot, lc_ref[rows, :], dc_ref[rows, :], kcat, vcat, mask)
            dq_t = _dot(ds, kcat, 1, 0)
            if pi == 0:
                dq_ref[rows, :] = dq_t
            else:
                dq_ref[rows, :] += dq_t
            dk2 = _dot(ds, qt, 0, 0)
            dv2 = _dot(p, dot, 0, 0)
            dk_ref[rows, :] += dk2[SPAN:, :]
            dv_ref[rows, :] += dv2[SPAN:, :]
            if u > 0:
                dk_ref[prows, :] += dk2[:SPAN, :]
                dv_ref[prows, :] += dv2[:SPAN, :]
        qi = _iota2((SPAN, SPAN), 0)
        ki = _iota2((SPAN, SPAN), 1)
        mask_next = (ki >= qi) & (ki < jnp.where(g == ng - 1, 0, SPAN))
        for r in DILATIONS:
            for rho in range(r):
                krows, qrows = _rows(r, GROUP // (SPAN * r) - 1, rho), _rows(r, 0, rho)
                qt, dot = qn_ref[qrows, :].astype(BF16), don_ref[qrows, :].astype(BF16)
                ds, p = pair(qt, dot, ln_ref[qrows, :], dn_ref[qrows, :], kc_ref[krows, :].astype(BF16),
                             vc_ref[krows, :].astype(BF16), mask_next)
                dk_ref[krows, :] += _dot(ds, qt, 0, 0)
                dv_ref[krows, :] += _dot(p, dot, 0, 0)

    cur = pl.BlockSpec((GROUP, HD), lambda g, h: (g, h))
    prev = pl.BlockSpec((GROUP, HD), lambda g, h: (jnp.maximum(g - 1, 0), h))
    nxt = pl.BlockSpec((GROUP, HD), lambda g, h: (jnp.minimum(g + 1, ng - 1), h))
    vcur = pl.BlockSpec((GROUP, HD), lambda g, h: (g, v_blk * N_HEADS + h))
    vprev = pl.BlockSpec((GROUP, HD), lambda g, h: (jnp.maximum(g - 1, 0), v_blk * N_HEADS + h))
    return pl.pallas_call(
        body, grid=(ng, N_HEADS), in_specs=[cur, cur, vcur, cur, cur, cur, prev, vprev] + [nxt] * 4,
        out_specs=[cur] * 3,
        out_shape=[_sds((t, GW), F32)] * 3, name=name,
        compiler_params=_params(2))(q, k, v, do, lse, delta, k, v, q, do, lse, delta)


def _merge(os_, ls_):
    m = jnp.maximum(jnp.maximum(ls_[0], ls_[1]), ls_[2])
    ws = [jnp.exp(l - m) for l in ls_]
    tot = ws[0] + ws[1] + ws[2]
    ob = (ws[0] * os_[0] + ws[1] * os_[1] + ws[2] * os_[2]) / tot
    return ob, m + jnp.log(tot)


def _gated_norm(oa, z, wv):
    return _head_rms(oa, wv) * _silu(z)


def _mix_fwd(name, oa_raw, proj, z_blk, ob, w_dn, w_an):
    t = oa_raw.shape[0]
    tm = min(256, t)

    def body(oa_ref, z_ref, ob_ref, wd_ref, wa_ref, mix_ref):
        for h in range(N_HEADS):
            sl = slice(h * HD, (h + 1) * HD)
            mix_ref[:, sl] = _gated_norm(oa_ref[:, sl], z_ref[:, sl], wd_ref[...]).astype(BF16)
            mix_ref[:, GW + h * HD:GW + (h + 1) * HD] = _head_rms(ob_ref[:, sl], wa_ref[...]).astype(BF16)

    vec = pl.BlockSpec((1, HD), lambda i: (0, 0))
    wide = pl.BlockSpec((tm, GW), lambda i: (i, 0))
    return pl.pallas_call(
        body, grid=(t // tm,),
        in_specs=[wide, pl.BlockSpec((tm, GW), lambda i: (i, z_blk)), wide, vec, vec],
        out_specs=pl.BlockSpec((tm, 2 * GW), lambda i: (i, 0)),
        out_shape=_sds((t, 2 * GW), BF16), name=name,
        compiler_params=_params(1))(oa_raw, proj, ob, w_dn, w_an)


def _mix_bwd(name, dmixed, oa_raw, proj, z_blk, ob, w_dn, w_an, dep):
    t = oa_raw.shape[0]
    tm = min(256, t)

    def body(dm_ref, oa_ref, z_ref, ob_ref, wd_ref, wa_ref, dep_ref,
             doa_ref, dz_ref, dob_ref, dl_ref, dwd_ref, dwa_ref):
        dwd = jnp.zeros((1, HD), F32)
        dwa = jnp.zeros((1, HD), F32)
        for h in range(N_HEADS):
            sl = slice(h * HD, (h + 1) * HD)
            _, vjp = jax.vjp(_gated_norm, oa_ref[:, sl], z_ref[:, sl], wd_ref[...])
            doa, dz, dw1 = vjp(dm_ref[:, sl])
            doa_ref[:, sl] = doa
            dz_ref[:, sl] = dz.astype(BF16)
            dwd = dwd + dw1
            obh = ob_ref[:, sl]
            _, vjp2 = jax.vjp(_head_rms, obh, wa_ref[...])
            dob, dw2 = vjp2(dm_ref[:, GW + h * HD:GW + (h + 1) * HD])
            dwa = dwa + dw2
            dob_ref[:, sl] = dob
            dl_ref[:, sl] = jnp.broadcast_to(jnp.sum(dob * obh, axis=1, keepdims=True), (tm, HD))

        @pl.when(pl.program_id(0) == 0)
        def _():
            dwd_ref[...] = jnp.zeros_like(dwd_ref)
            dwa_ref[...] = jnp.zeros_like(dwa_ref)

        dwd_ref[...] += dwd
        dwa_ref[...] += dwa

    vec = pl.BlockSpec((1, HD), lambda i: (0, 0))
    wide = pl.BlockSpec((tm, GW), lambda i: (i, 0))
    return pl.pallas_call(
        body, grid=(t // tm,),
        in_specs=[pl.BlockSpec((tm, 2 * GW), lambda i: (i, 0)), wide, pl.BlockSpec((tm, GW), lambda i: (i, z_blk)),
                  wide, vec, vec, ANY],
        out_specs=[wide, wide, wide, wide, vec, vec],
        out_shape=[_sds((t, GW), F32), _sds((t, GW), BF16), _sds((t, GW), F32), _sds((t, GW), F32),
                   _sds((1, HD), F32), _sds((1, HD), F32)], name=name,
        compiler_params=_params(1))(dmixed, oa_raw, proj, ob, w_dn, w_an, dep)


def _gate_up_swiglu(name, h2, w_gu_g):
    t, d = h2.shape
    n = w_gu_g.shape[2]
    per = N_DEV // 2
    tm = min(512, t)

    def body(a_ref, bg_ref, bu_ref, gu_ref, act_ref):
        a = a_ref[...]
        g = _dot(a, bg_ref[...], 1, 0)
        up = _dot(a, bu_ref[...], 1, 0)
        gu_ref[0] = g
        gu_ref[1] = up
        act_ref[...] = (_silu(g) * up).astype(BF16)

    return pl.pallas_call(
        body, grid=(per, t // tm),
        in_specs=[pl.BlockSpec((tm, d), lambda j, i: (i, 0)), pl.BlockSpec((None, d, n), lambda j, i: (j, 0, 0)),
                  pl.BlockSpec((None, d, n), lambda j, i: (j + per, 0, 0))],
        out_specs=[pl.BlockSpec((2, tm, n), lambda j, i: (0, i, j)), pl.BlockSpec((tm, n), lambda j, i: (i, j))],
        out_shape=[_sds((2, t, per * n), F32), _sds((t, per * n), BF16)], name=name,
        compiler_params=_params(2))(h2, w_gu_g, w_gu_g)


def _d_gate_up(name, dy16, w_down, gu3, dep):
    t, d = dy16.shape
    f = w_down.shape[0]
    tm, tn = min(512, t), f // 4

    def body(a_ref, b_ref, g_ref, dep_ref, o_ref):
        dact = _dot(a_ref[...], b_ref[...], 1, 1)
        g, up = g_ref[0], g_ref[1]
        sg = _sigmoid(g)
        o_ref[0] = (dact * up * sg * (1.0 + g * (1.0 - sg))).astype(BF16)
        o_ref[1] = (dact * g * sg).astype(BF16)

    return pl.pallas_call(
        body, grid=(f // tn, t // tm),
        in_specs=[pl.BlockSpec((tm, d), lambda j, i: (i, 0)), pl.BlockSpec((tn, d), lambda j, i: (j, 0)),
                  pl.BlockSpec((2, tm, tn), lambda j, i: (0, i, j)), ANY],
        out_specs=pl.BlockSpec((2, tm, tn), lambda j, i: (0, i, j)), out_shape=_sds((2, t, f), BF16), name=name,
        compiler_params=_params(2))(dy16, w_down, gu3, dep)


def _loss_head(name, y, target):
    t, d = y.shape
    tm = min(512, t)

    def body(y_ref, t_ref, dy_ref, l_ref):
        diff = y_ref[...] - t_ref[...]
        dy_ref[...] = diff * (1.0 / d)
        part = jnp.sum(jnp.sum(diff * diff, axis=1, keepdims=True), axis=0, keepdims=True) * (0.5 / d)

        @pl.when(pl.program_id(0) == 0)
        def _():
            l_ref[...] = jnp.zeros_like(l_ref)

        l_ref[...] += jnp.broadcast_to(part, (8, 128))

    row = pl.BlockSpec((tm, d), lambda i: (i, 0))
    return pl.pallas_call(body, grid=(t // tm,), in_specs=[row, row],
                          out_specs=[row, pl.BlockSpec((8, 128), lambda i: (0, 0))],
                          out_shape=[_sds((t, d), F32), _sds((8, 128), F32)], name=name,
                          compiler_params=_params(1))(y, target)


def _peer(me, k):
    pid = (me + k) % N_DEV
    return (pid // 4, (pid // 2) % 2, pid % 2)


def _my_id():
    return 4 * lax.axis_index("x") + 2 * lax.axis_index("y") + lax.axis_index("c")


def _exchange(name, arrays, scatter):
    n = len(arrays)

    def body(*refs):
        ins, outs = refs[:n], refs[n:2 * n]
        send_sems, recv_sems, local_sems = refs[2 * n:]
        me = _my_id()
        started = []
        for a in range(n):
            src = ins[a].at[me] if scatter[a] else ins[a]
            loc = pltpu.make_async_copy(src, outs[a].at[me], local_sems.at[a])
            loc.start()
            started.append(loc)
        remote = []
        for k in range(1, N_DEV):
            to = (me + k) % N_DEV
            for a in range(n):
                src = ins[a].at[to] if scatter[a] else ins[a]
                cp = pltpu.make_async_remote_copy(src_ref=src, dst_ref=outs[a].at[me],
                                                  send_sem=send_sems.at[a * (N_DEV - 1) + k - 1], recv_sem=recv_sems.at[a * (N_DEV - 1) + k - 1],
                                                  device_id=_peer(me, k), device_id_type=pl.DeviceIdType.MESH)
                cp.start()
                remote.append(cp)
        for k in range(1, N_DEV):
            frm = (me + N_DEV - k) % N_DEV
            for a in range(n):
                src = ins[a].at[frm] if scatter[a] else ins[a]
                pltpu.make_async_remote_copy(src_ref=src, dst_ref=outs[a].at[frm],
                                             send_sem=send_sems.at[a * (N_DEV - 1) + k - 1], recv_sem=recv_sems.at[a * (N_DEV - 1) + k - 1],
                                             device_id=_peer(me, k), device_id_type=pl.DeviceIdType.MESH).wait_recv()
        for cp in remote:
            cp.wait_send()
        for loc in started:
            loc.wait()

    out_shape = [_sds((N_DEV,) + (a.shape[1:] if sc else a.shape), a.dtype) for a, sc in zip(arrays, scatter)]
    return pl.pallas_call(
        body, in_specs=[ANY] * n, out_specs=[ANY] * n, out_shape=out_shape,
        scratch_shapes=[pltpu.SemaphoreType.DMA((n * (N_DEV - 1),)), pltpu.SemaphoreType.DMA((n * (N_DEV - 1),)),
                        pltpu.SemaphoreType.DMA((n,))],
        name=name)(*arrays)


def _gather_two_level(name, arrays):
    n = len(arrays)
    per = N_DEV - 1

    def body(*refs):
        ins, outs = refs[:n], refs[n:2 * n]
        send_sems, recv_sems, local_sems = refs[2 * n:]
        x, y, c = lax.axis_index("x"), lax.axis_index("y"), lax.axis_index("c")
        me, sibling = (x, y, c), (x, y, 1 - c)
        chips = [(1 - x, y), (x, 1 - y), (1 - x, 1 - y)]

        def copy(a, k, block, to, src=None):
            slot = outs[a].at[4 * block[0] + 2 * block[1] + block[2]]
            return pltpu.make_async_remote_copy(
                src_ref=slot if src is None else src, dst_ref=slot, send_sem=send_sems.at[a * per + k],
                recv_sem=recv_sems.at[a * per + k], device_id=to, device_id_type=pl.DeviceIdType.MESH)

        mine = [pltpu.make_async_copy(ins[a], outs[a].at[4 * x + 2 * y + c], local_sems.at[a]) for a in range(n)]
        for cp in mine:
            cp.start()
        first = [copy(a, 0, me, sibling, src=ins[a]) for a in range(n)]
        first += [copy(a, 1 + j, me, (*chip, c), src=ins[a]) for j, chip in enumerate(chips) for a in range(n)]
        for cp in first:
            cp.start()
        passed = []
        for j, chip in enumerate(chips):
            for a in range(n):
                copy(a, 1 + j, (*chip, c), me).wait_recv()
                cp = copy(a, 4 + j, (*chip, c), sibling)
                cp.start()
                passed.append(cp)
        for a in range(n):
            copy(a, 0, sibling, me).wait_recv()
            for j, chip in enumerate(chips):
                copy(a, 4 + j, (*chip, 1 - c), me).wait_recv()
        for cp in first + passed:
            cp.wait_send()
        for cp in mine:
            cp.wait()

    return pl.pallas_call(
        body, in_specs=[ANY] * n, out_specs=[ANY] * n,
        out_shape=[_sds((N_DEV,) + a.shape, a.dtype) for a in arrays],
        scratch_shapes=[pltpu.SemaphoreType.DMA((n * per,)), pltpu.SemaphoreType.DMA((n * per,)),
                        pltpu.SemaphoreType.DMA((n,))],
        name=name)(*arrays)


HBM = pl.BlockSpec(memory_space=pltpu.HBM)
SEM = pl.BlockSpec(memory_space=pltpu.SEMAPHORE)
EFFECT = pltpu.SideEffectType.DATAFLOW_SIDE_EFFECTING


def _remote_copies(srcs, lands, scatter, send_sems, recv_sems, me, incoming):
    out = []
    for k in range(1, N_DEV):
        other = (me + N_DEV - k) % N_DEV if incoming else (me + k) % N_DEV
        for a in range(len(srcs)):
            sem = a * (N_DEV - 1) + k - 1
            src = srcs[a].at[other] if scatter[a] else srcs[a]
            dst = lands[a].at[other if incoming else me]
            out.append(pltpu.make_async_remote_copy(src_ref=src, dst_ref=dst, send_sem=send_sems.at[sem],
                                                    recv_sem=recv_sems.at[sem], device_id=_peer(me, k),
                                                    device_id_type=pl.DeviceIdType.MESH))
    return out


def _exchange_start(name, arrays, scatter, dep):
    n = len(arrays)
    lands = [lax.empty((N_DEV,) + (a.shape[1:] if sc else a.shape), a.dtype) for a, sc in zip(arrays, scatter)]

    def body(*refs):
        srcs, land_refs = refs[:n], refs[n:2 * n]
        send_sems, recv_sems = refs[2 * n + 1], refs[2 * n + 2]
        token = refs[-1]
        for cp in _remote_copies(srcs, land_refs, scatter, send_sems, recv_sems, _my_id(), False):
            cp.start()
        token[...] = jnp.zeros_like(token)

    n_sem = n * (N_DEV - 1)
    out_shape = ([pltpu.SemaphoreType.DMA((n_sem,)), pltpu.SemaphoreType.DMA((n_sem,))]
                 + [pltpu.HBM(a.shape, a.dtype) for a in arrays] + [pltpu.HBM(l.shape, l.dtype) for l in lands]
                 + [_sds((8, 128), F32)])
    aliases = {i: 2 + i for i in range(2 * n)}
    args = [pltpu.with_memory_space_constraint(a, pltpu.HBM) for a in list(arrays) + lands] + [dep]
    res = pl.pallas_call(
        body, name=name, in_specs=[HBM] * (2 * n) + [ANY], out_shape=out_shape,
        out_specs=[SEM, SEM] + [HBM] * (2 * n) + [pl.BlockSpec(memory_space=pltpu.VMEM)],
        input_output_aliases=aliases, compiler_params=pltpu.CompilerParams(has_side_effects=EFFECT))(*args)
    return dict(send=res[0], recv=res[1], srcs=res[2:2 + n], lands=res[2 + n:2 + 2 * n], token=res[-1],
                scatter=scatter)


def _exchange_wait(name, started, after):
    n = len(started["srcs"])
    scatter = started["scatter"]

    def body(*refs):
        srcs, land_refs = refs[:n], refs[n:2 * n]
        send_sems, recv_sems = refs[2 * n], refs[2 * n + 1]
        me = _my_id()
        for cp in _remote_copies(srcs, land_refs, scatter, send_sems, recv_sems, me, False):
            cp.wait_send()
        for cp in _remote_copies(srcs, land_refs, scatter, send_sems, recv_sems, me, True):
            cp.wait_recv()

    arrs = list(started["srcs"]) + list(started["lands"])
    res = pl.pallas_call(
        body, name=name, in_specs=[HBM] * (2 * n) + [SEM, SEM, ANY],
        out_shape=[pltpu.HBM(a.shape, a.dtype) for a in arrs], out_specs=[HBM] * (2 * n),
        input_output_aliases={i: i for i in range(2 * n)},
        compiler_params=pltpu.CompilerParams(has_side_effects=EFFECT))(*arrs, started["send"], started["recv"], after)
    me = _my_id()
    out = []
    for src, land, sc in zip(res[:n], res[n:], scatter):
        own = lax.dynamic_index_in_dim(src, me, 0, keepdims=True) if sc else src[None]
        out.append(lax.dynamic_update_slice(land, own, (me,) + (0,) * (land.ndim - 1)))
    return out


def _adamw(name, parts, w, m, v):
    r, c = w.shape
    tr = r
    for cand in (128, 88, 64, 40, 8):
        if r % cand == 0:
            tr = cand
            break
    c1 = 1.0 / (1.0 - ADAM_B1 ** ADAM_STEP)
    c2 = 1.0 / (1.0 - ADAM_B2 ** ADAM_STEP)

    def body(p_ref, w_ref, m_ref, v_ref, g_ref, d_ref, nm_ref, nv_ref):
        g = p_ref[0]
        for s in range(1, N_DEV):
            g = g + p_ref[s]
        mn = ADAM_B1 * m_ref[...] + (1.0 - ADAM_B1) * g
        vn = ADAM_B2 * v_ref[...] + (1.0 - ADAM_B2) * (g * g)
        g_ref[...] = g
        nm_ref[...] = mn
        nv_ref[...] = vn
        d_ref[...] = -ADAM_LR * ((mn * c1) / (jnp.sqrt(vn * c2) + ADAM_EPS) + ADAM_WD * w_ref[...])

    blk = pl.BlockSpec((tr, c), lambda i: (i, 0))
    return pl.pallas_call(
        body, grid=(r // tr,), in_specs=[pl.BlockSpec((N_DEV, tr, c), lambda i: (0, i, 0)), blk, blk, blk],
        out_specs=[blk] * 4, out_shape=[_sds((r, c), F32)] * 4, name=name,
        compiler_params=_params(1, VMEM_LIMIT))(parts, w, m, v)


def _pad_rows(a, rows):
    return jnp.pad(a, ((0, rows - a.shape[0]), (0, 0)))


def _lane_row(vec8, offset):
    return jnp.pad(vec8.reshape(1, 8), ((0, 0), (offset, HD - 8 - offset)))


def kernel(x, positions, attn_norm_w, w_in, conv_w, a_log, dt_bias, delta_out_norm_w, q_norm_w, k_norm_w, attn_out_norm_w, w_out, ffn_norm_w, w_gate_up, w_down, loss_target, m_attn_norm_w, m_w_in, m_conv_w, m_a_log, m_dt_bias, m_delta_out_norm_w, m_q_norm_w, m_k_norm_w, m_attn_out_norm_w, m_w_out, m_ffn_norm_w, m_w_gate_up, m_w_down, v_attn_norm_w, v_w_in, v_conv_w, v_a_log, v_dt_bias, v_delta_out_norm_w, v_q_norm_w, v_k_norm_w, v_attn_out_norm_w, v_w_out, v_ffn_norm_w, v_w_gate_up, v_w_down):
    x2 = x[0]
    t, d = x2.shape
    target = loss_target[0]
    pos_col = positions.reshape(t, 1)
    half = HD // 2
    inv = (ROPE_THETA ** (-np.arange(half, dtype=np.float32) / half)).astype(np.float32)
    inv_row = jnp.asarray(np.concatenate([inv, inv]).reshape(1, HD))

    n_in = w_in.shape[2]
    n_gu = w_gate_up.shape[2]
    w_in_g, conv_g = _gather_two_level("gather_in", [w_in[0].astype(BF16), _pad_rows(conv_w[0], 8)])
    ffn_own = [w_gate_up[0].astype(BF16), w_down[0].astype(BF16), w_out[0].astype(BF16)]
    ffn_fly = _exchange_start("gather_ffn_start", ffn_own, [False] * 3, conv_g)
    w_in_full = jnp.transpose(w_in_g, (1, 0, 2)).reshape(d, N_DEV * n_in)
    n_main = 4 * GW
    n_small = 2 * N_HEADS
    w_cat = jnp.concatenate([w_in_full[:, :n_main], w_in_full[:, n_main + n_small:],
                             w_in_full[:, n_main:n_main + n_small],
                             jnp.zeros((d, HD - n_small), BF16)], axis=1)
    n_cat = w_cat.shape[1]
    small_blk = (7 * GW) // HD
    conv_w8 =jnp.transpose(conv_g, (1, 0, 2)).reshape(8, 3 * GW)
    alog_row = _lane_row(a_log[0], 8)
    dtb_row = _lane_row(dt_bias[0], 8)

    tm = min(2048, t)
    h1 = _rms_fwd("norm1", x2, attn_norm_w, ffn_fly["token"])
    tn = 384
    proj = _mm("in_proj", h1, w_cat, grid=(t // tm, n_cat // tn, 1),
               a_spec=pl.BlockSpec((tm, d), lambda i, j, k: (i, 0)),
               b_spec=pl.BlockSpec((d, tn), lambda i, j, k: (0, j)),
               o_spec=pl.BlockSpec((tm, tn), lambda i, j, k: (i, j)),
               out_shape=_sds((t, n_cat), F32), ca=1, cb=0, nk=1)
    qn = _conv_fwd("conv_q", proj, conv_w8, 0, True, HD ** -0.5)
    kn = _conv_fwd("conv_k", proj, conv_w8, 1, True, 1.0)
    vv = _conv_fwd("conv_v", proj, conv_w8, 2, False, 1.0)
    beta_b, gc_b = _gates_fwd("gates", proj, small_blk, alog_row, dtb_row)
    u, w, p, tinv, qd, kd = _delta_prep("delta_prep", qn, kn, vv, beta_b, gc_b)
    oa_raw, vn, s_hist = _delta_scan("delta_scan", u, w, p, qd, kd, gc_b)

    aq = _qk_fwd("attn_q", proj, 4, q_norm_w, pos_col, inv_row)
    ak = _qk_fwd("attn_k", proj, 5, k_norm_w, pos_col, inv_row)
    ob, lse = _attn_fwd("attn_fwd", aq, ak, proj, 6)
    mixed = _mix_fwd("mix", oa_raw, proj, 3, ob, delta_out_norm_w, attn_out_norm_w)
    w_gu_g, w_down_g, w_out_g = _exchange_wait("gather_ffn_wait", ffn_fly, mixed)
    w_down_full = w_down_g.reshape(D_FF, d)
    w_out_full = w_out_g.reshape(2 * GW, d)
    tn = 512
    x1 = _mm("out_proj", mixed, w_out_full, grid=(t // tm, d // tn, 1),
             a_spec=pl.BlockSpec((tm, 2 * GW), lambda i, j, k: (i, 0)),
             b_spec=pl.BlockSpec((2 * GW, tn), lambda i, j, k: (0, j)),
             o_spec=pl.BlockSpec((tm, tn), lambda i, j, k: (i, j)),
             add=x2, add_spec=pl.BlockSpec((tm, tn), lambda i, j, k: (i, j)),
             out_shape=_sds((t, d), F32), ca=1, cb=0, nk=1)
    h2 = _rms_fwd("norm2", x1, ffn_norm_w, ffn_norm_w)
    per = N_DEV // 2
    gu3, act = _gate_up_swiglu("gate_up", h2, w_gu_g)
    tmd, tkd = min(1024, t), D_FF // 2
    y = _mm("down_proj", act, w_down_full, grid=(t // tmd, d // tn, 2),
            a_spec=pl.BlockSpec((tmd, tkd), lambda i, j, k: (i, k)),
            b_spec=pl.BlockSpec((tkd, tn), lambda i, j, k: (k, j)),
            o_spec=pl.BlockSpec((tmd, tn), lambda i, j, k: (i, j)),
            add=x1, add_spec=pl.BlockSpec((tmd, tn), lambda i, j, k: (i, j)),
            out_shape=_sds((t, d), F32), ca=1, cb=0, nk=2)
    dy, loss_tile = _loss_head("loss_head", y, target)
    loss = lax.psum(loss_tile[0, 0], ("x", "y", "c"))

    dy16 = dy.astype(BF16)
    tk = min(2048, t)
    nkt = t // tk
    g_down = _mm("g_down", act, dy16, dep=loss.reshape(1, 1), grid=(D_FF // 512, 1, nkt),
                 a_spec=pl.BlockSpec((tk, 512), lambda i, j, k: (k, i)),
                 b_spec=pl.BlockSpec((tk, d), lambda i, j, k: (k, 0)),
                 o_spec=pl.BlockSpec((512, d), lambda i, j, k: (i, 0)),
                 out_shape=_sds((D_FF, d), F32), ca=0, cb=0, nk=nkt)
    down_g_fly = _exchange_start("reduce_down_start", [g_down.reshape(N_DEV, D_FF // N_DEV, d)], [True], dy16)
    dgu3 = _d_gate_up("d_gate_up", dy16, w_down_full, gu3, down_g_fly["token"])
    g_gu = _mm("g_gate_up", h2, dgu3, grid=(d // 512, N_DEV, nkt),
               a_spec=pl.BlockSpec((tk, 512), lambda i, j, k: (k, i)),
               b_spec=pl.BlockSpec((None, tk, n_gu), lambda i, j, k: (j // per, k, j % per)),
               o_spec=pl.BlockSpec((None, 512, n_gu), lambda i, j, k: (j, i, 0)),
               out_shape=_sds((N_DEV, d, n_gu), F32), ca=0, cb=0, nk=nkt)
    gu_g_fly = _exchange_start("reduce_gate_up_start", [g_gu], [True], dy16)
    tmh, tnh = min(2048, t), 1024
    dh2 = _mm("d_h2", dgu3, w_gu_g, dep=gu_g_fly["token"], grid=(t // tmh, d // tnh, N_DEV),
              a_spec=pl.BlockSpec((None, tmh, n_gu), lambda i, j, k: (k // per, i, k % per)),
              b_spec=pl.BlockSpec((None, tnh, n_gu), lambda i, j, k: (k, j, 0)),
              o_spec=pl.BlockSpec((tmh, tnh), lambda i, j, k: (i, j)),
              out_shape=_sds((t, d), F32), ca=1, cb=1, nk=N_DEV)
    dx1, g_ffn_norm = _rms_bwd("norm2_bwd", x1, ffn_norm_w, dh2, dy)

    dx1_16 = dx1.astype(BF16)
    g_out = _mm("g_out", mixed, dx1_16, grid=((2 * GW) // 512, 1, nkt),
                a_spec=pl.BlockSpec((tk, 512), lambda i, j, k: (k, i)),
                b_spec=pl.BlockSpec((tk, d), lambda i, j, k: (k, 0)),
                o_spec=pl.BlockSpec((512, d), lambda i, j, k: (i, 0)),
                out_shape=_sds((2 * GW, d), F32), ca=0, cb=0, nk=nkt)
    out_g_fly = _exchange_start("reduce_out_start", [g_out.reshape(N_DEV, (2 * GW) // N_DEV, d)], [True], g_ffn_norm)
    dmixed = _mm("d_mixed", dx1_16, w_out_full, dep=out_g_fly["token"], grid=(t // tm, (2 * GW) // tn, 1),
                 a_spec=pl.BlockSpec((tm, d), lambda i, j, k: (i, 0)),
                 b_spec=pl.BlockSpec((tn, d), lambda i, j, k: (j, 0)),
                 o_spec=pl.BlockSpec((tm, tn), lambda i, j, k: (i, j)),
                 out_shape=_sds((t, 2 * GW), F32), ca=1, cb=1, nk=1)
    doa, dz, dob, delta, g_dn, g_an = _mix_bwd("mix_bwd", dmixed, oa_raw, proj, 3, ob,
                                               delta_out_norm_w, attn_out_norm_w, out_g_fly["token"])
    d_aq, d_ak, d_av = _attn_bwd("attn_bwd", aq, ak, proj, 6, dob, lse, delta)
    daq, g_qn = _qk_bwd("attn_q_bwd", proj, 4, q_norm_w, pos_col, inv_row, d_aq)
    dak, g_kn = _qk_bwd("attn_k_bwd", proj, 5, k_norm_w, pos_col, inv_row, d_ak)
    dav = d_av.astype(BF16)

    dvn, dqd, dkd, dw, ddec = _delta_scan_bwd("delta_scan_bwd", doa, w, p, qd, kd, gc_b, vn, s_hist)
    dqn, dkn, dvv, dbeta_b, dg_b = _delta_prep_bwd("delta_prep_bwd", qn, kn, vv, beta_b, gc_b, tinv, u, w, vn,
                                                   doa, dvn, dqd, dkd, dw, ddec)
    dxq, gcw_q = _conv_bwd("conv_q_bwd", proj, conv_w8, dqn, 0, True, HD ** -0.5)
    dxk, gcw_k = _conv_bwd("conv_k_bwd", proj, conv_w8, dkn, 1, True, 1.0)
    dxv, gcw_v = _conv_bwd("conv_v_bwd", proj, conv_w8, dvv, 2, False, 1.0)
    dsmall, g_alog_row, g_dtb_row = _gates_bwd("gates_bwd", proj, small_blk, alog_row, dtb_row, dbeta_b, dg_b)
    dproj = jnp.concatenate([dxq, dxk, dxv, dz, daq, dak, dav, dsmall], axis=1)
    tnc = n_cat // 3
    g_cat = _mm("g_in", h1, dproj, grid=(d // 512, 3, nkt),
                a_spec=pl.BlockSpec((tk, 512), lambda i, j, k: (k, i)),
                b_spec=pl.BlockSpec((tk, tnc), lambda i, j, k: (k, j)),
                o_spec=pl.BlockSpec((512, tnc), lambda i, j, k: (i, j)),
                out_shape=_sds((d, n_cat), F32), ca=0, cb=0, nk=nkt)
    g_in_full = jnp.concatenate([g_cat[:, :n_main], g_cat[:, 7 * GW:7 * GW + n_small], g_cat[:, n_main:7 * GW]], axis=1)
    g_in_parts = jnp.transpose(g_in_full.reshape(d, N_DEV, n_in), (1, 0, 2))
    g_conv = jnp.concatenate([gcw_q, gcw_k, gcw_v], axis=1)
    n_cw = conv_w.shape[2]
    g_conv_parts = jnp.transpose(g_conv.reshape(8, N_DEV, n_cw), (1, 0, 2))
    in_g_fly = _exchange_start("reduce_in_start", [g_in_parts, g_conv_parts], [True] * 2, g_dtb_row)
    tkc = n_cat // 3
    dh1 = _mm("d_h1", dproj, w_cat, dep=in_g_fly["token"], grid=(t // tmd, d // tn, 3),
              a_spec=pl.BlockSpec((tmd, tkc), lambda i, j, k: (i, k)),
              b_spec=pl.BlockSpec((tn, tkc), lambda i, j, k: (j, k)),
              o_spec=pl.BlockSpec((tmd, tn), lambda i, j, k: (i, j)),
              out_shape=_sds((t, d), F32), ca=1, cb=1, nk=3)
    grad_x, g_attn_norm = _rms_bwd("norm1_bwd", x2, attn_norm_w, dh1, dx1)

    small_rows = [g_attn_norm.reshape(d // HD, HD), g_ffn_norm.reshape(d // HD, HD), g_dn, g_qn, g_kn, g_an,
                  g_alog_row, g_dtb_row]
    small_pack = _pad_rows(jnp.concatenate(small_rows, axis=0), 40)
    (r_small,) = _exchange("gather_small_grads", [small_pack], [False])

    def pack_small(an, fn, dn, qn_, kn_, aon, al, db):
        rows = [an.reshape(d // HD, HD), fn.reshape(d // HD, HD), dn, qn_, kn_, aon,
                _lane_row(al[0], 8), _lane_row(db[0], 8)]
        return _pad_rows(jnp.concatenate(rows, axis=0), 40)

    def unpack_small(pk):
        nr = d // HD
        return dict(attn_norm_w=pk[:nr].reshape(1, d), ffn_norm_w=pk[nr:2 * nr].reshape(1, d),
                    delta_out_norm_w=pk[2 * nr:2 * nr + 1], q_norm_w=pk[2 * nr + 1:2 * nr + 2],
                    k_norm_w=pk[2 * nr + 2:2 * nr + 3], attn_out_norm_w=pk[2 * nr + 3:2 * nr + 4],
                    a_log=pk[2 * nr + 4:2 * nr + 5, 8:16], dt_bias=pk[2 * nr + 5:2 * nr + 6, 8:16])

    res_small = _adamw("adamw_small", r_small,
                       pack_small(attn_norm_w, ffn_norm_w, delta_out_norm_w, q_norm_w, k_norm_w, attn_out_norm_w, a_log, dt_bias),
                       pack_small(m_attn_norm_w, m_ffn_norm_w, m_delta_out_norm_w, m_q_norm_w, m_k_norm_w, m_attn_out_norm_w, m_a_log, m_dt_bias),
                       pack_small(v_attn_norm_w, v_ffn_norm_w, v_delta_out_norm_w, v_q_norm_w, v_k_norm_w, v_attn_out_norm_w, v_a_log, v_dt_bias))
    small = [unpack_small(a) for a in res_small]
    (r_down,) = _exchange_wait("reduce_down_wait", down_g_fly, res_small[0])
    (r_gu,) = _exchange_wait("reduce_gate_up_wait", gu_g_fly, res_small[0])
    (r_out,) = _exchange_wait("reduce_out_wait", out_g_fly, res_small[0])
    res_gu = [a[None] for a in _adamw("adamw_gate_up", r_gu, w_gate_up[0], m_w_gate_up[0], v_w_gate_up[0])]
    res_down = [a[None] for a in _adamw("adamw_down", r_down, w_down[0], m_w_down[0], v_w_down[0])]
    res_out = [a[None] for a in _adamw("adamw_out", r_out, w_out[0], m_w_out[0], v_w_out[0])]
    done = (res_gu[3][0, :1, :1] + res_down[3][0, :1, :1] + res_out[3][0, :1, :1])
    r_in, r_conv = _exchange_wait("reduce_in_wait", in_g_fly, done)
    res_in = [a[None] for a in _adamw("adamw_in", r_in, w_in[0], m_w_in[0], v_w_in[0])]
    res_conv =[a[None, :4] for a in _adamw("adamw_conv", r_conv, _pad_rows(conv_w[0], 8), _pad_rows(m_conv_w[0], 8),
                                            _pad_rows(v_conv_w[0], 8))]

    outs = [loss, grad_x[None]]
    for i in range(4):
        s = small[i]
        outs += [s["attn_norm_w"], res_in[i], res_conv[i], s["a_log"], s["dt_bias"], s["delta_out_norm_w"],
                 s["q_norm_w"], s["k_norm_w"], s["attn_out_norm_w"], res_out[i], s["ffn_norm_w"], res_gu[i],
                 res_down[i]]
    return tuple(outs)
```

```python
import functools

import numpy as np
import jax
import jax.numpy as jnp
from jax import lax
from jax.experimental import pallas as pl
from jax.experimental.pallas import tpu as pltpu

F32 = jnp.float32
BF16 = jnp.bfloat16

N_DEV = 8
N_HEADS = 8
HD = 128
GW = N_HEADS * HD
CHUNK = 64
PAIR = 2 * CHUNK
SPAN = 128
DILATIONS = (1, 4, 16)
ROPE_THETA = 10000.0
EPS = 1e-6
D_FF = 5632
ADAM_LR, ADAM_B1, ADAM_B2, ADAM_EPS, ADAM_WD, ADAM_STEP = 0.001, 0.9, 0.999, 1e-8, 0.01, 10
NEG = -1e30
VMEM_LIMIT = 56 * 1024 * 1024
ANY = pl.BlockSpec(memory_space=pl.ANY)
HEADS_PER_STEP = 8


def _params(n_grid, vmem=VMEM_LIMIT):
    return pltpu.CompilerParams(dimension_semantics=("arbitrary",) * n_grid, vmem_limit_bytes=vmem)


def _sds(shape, dtype):
    return jax.ShapeDtypeStruct(tuple(shape), dtype)


def _sigmoid(x):
    return 1.0 / (1.0 + jnp.exp(-x))


def _silu(x):
    return x * _sigmoid(x)


def _softplus(x):
    return jnp.maximum(x, 0.0) + jnp.log(1.0 + jnp.exp(-jnp.abs(x)))


def _dot(a, b, ca, cb, precision=None):
    return lax.dot_general(a, b, (((ca,), (cb,)), ((), ())), precision=precision,
                           preferred_element_type=F32)


def _b16(x):
    return x if x.dtype == BF16 else x.astype(BF16)


def _split(x):
    hi = x.astype(BF16)
    return hi, (x - hi.astype(F32)).astype(BF16)


def _dot3(a, b, ca, cb):
    a_hi, a_lo = _split(a)
    b_hi, b_lo = _split(b)
    return _dot(a_hi, b_hi, ca, cb) + (_dot(a_hi, b_lo, ca, cb) + _dot(a_lo, b_hi, ca, cb))


def _iota2(shape, axis):
    return lax.broadcasted_iota(jnp.int32, shape, axis)


def _mm(name, a, b, *, grid, a_spec, b_spec, o_spec, out_shape, ca, cb, nk, add=None, add_spec=None,
        dep=None, vmem=VMEM_LIMIT):
    has_add = add is not None
    n_in = 2 + has_add + (dep is not None)

    def body(*refs):
        a_ref, b_ref = refs[0], refs[1]
        e_ref = refs[2] if has_add else None
        o_ref = refs[n_in]
        part = _dot(_b16(a_ref[...]), _b16(b_ref[...]), ca, cb)
        if nk == 1:
            if has_add:
                part = part + e_ref[...]
            o_ref[...] = part.astype(o_ref.dtype)
            return
        acc = refs[-1]
        k = pl.program_id(2)

        @pl.when(k == 0)
        def _():
            acc[...] = part

        @pl.when(k > 0)
        def _():
            acc[...] += part

        @pl.when(k == nk - 1)
        def _():
            res = acc[...]
            if has_add:
                res = res + e_ref[...]
            o_ref[...] = res.astype(o_ref.dtype)

    in_specs = [a_spec, b_spec] + ([add_spec] if has_add else []) + ([ANY] if dep is not None else [])
    args = (a, b) + ((add,) if has_add else ()) + ((dep,) if dep is not None else ())
    blk = [d for d in o_spec.block_shape if d is not None]
    scratch = [pltpu.VMEM(tuple(blk), F32)] if nk > 1 else []
    return pl.pallas_call(body, grid=grid, in_specs=in_specs, out_specs=o_spec, out_shape=out_shape,
                          scratch_shapes=scratch, name=name, compiler_params=_params(3, vmem))(*args)


def _rms_f(xv, wv):
    return xv * lax.rsqrt(jnp.mean(xv * xv, axis=-1, keepdims=True) + EPS) * wv


def _rms_fwd(name, x, w, dep):
    t, d = x.shape
    tm = min(512, t)

    def body(x_ref, w_ref, dep_ref, o_ref):
        o_ref[...] = _rms_f(x_ref[...], w_ref[...]).astype(BF16)

    row = pl.BlockSpec((tm, d), lambda i: (i, 0))
    vec = pl.BlockSpec((1, d), lambda i: (0, 0))
    return pl.pallas_call(body, grid=(t // tm,), in_specs=[row, vec, ANY], out_specs=row,
                          out_shape=_sds((t, d), BF16), name=name, compiler_params=_params(1))(x, w, dep)


def _rms_bwd(name, x, w, dh, res):
    t, d = x.shape
    tm = min(256, t)

    def body(x_ref, w_ref, dh_ref, res_ref, dx_ref, dw_ref):
        _, vjp = jax.vjp(_rms_f, x_ref[...], w_ref[...])
        dxv, dwv = vjp(dh_ref[...])
        dx_ref[...] = dxv + res_ref[...]

        @pl.when(pl.program_id(0) == 0)
        def _():
            dw_ref[...] = jnp.zeros_like(dw_ref)

        dw_ref[...] += dwv

    row = pl.BlockSpec((tm, d), lambda i: (i, 0))
    vec = pl.BlockSpec((1, d), lambda i: (0, 0))
    return pl.pallas_call(body, grid=(t // tm,), in_specs=[row, vec, row, row], out_specs=[row, vec],
                          out_shape=[_sds((t, d), F32), _sds((1, d), F32)], name=name,
                          compiler_params=_params(1))(x, w, dh, res)


def _conv_taps(xv, w_ref, rows):
    c = w_ref[3:4, :] * xv
    for s in (1, 2, 3):
        c = c + w_ref[3 - s:4 - s, :] * jnp.where(rows >= s, pltpu.roll(xv, s, 0), 0.0)
    return c


def _post_conv(c, l2, scale):
    y = _silu(c)
    if l2:
        y = y * lax.rsqrt(jnp.sum(y * y, axis=-1, keepdims=True) + EPS) * scale
    return y


def _conv_fwd(name, proj, conv_w8, group, l2, scale):
    t = proj.shape[0]

    def body(x_ref, w_ref, o_ref):
        rows = _iota2((t, HD), 0)
        o_ref[...] = _post_conv(_conv_taps(x_ref[...], w_ref, rows), l2, scale)

    return pl.pallas_call(
        body, grid=(N_HEADS,),
        in_specs=[pl.BlockSpec((t, HD), lambda h: (0, h + group * N_HEADS)),
                  pl.BlockSpec((8, HD), lambda h: (0, h + group * N_HEADS))],
        out_specs=pl.BlockSpec((t, HD), lambda h: (0, h)),
        out_shape=_sds((t, GW), F32), name=name, compiler_params=_params(1, VMEM_LIMIT))(proj, conv_w8)


def _conv_bwd(name, proj, conv_w8, dn, group, l2, scale):
    t = proj.shape[0]

    def body(x_ref, w_ref, dn_ref, dx_ref, dw_ref):
        rows = _iota2((t, HD), 0)
        xv = x_ref[...]
        c = _conv_taps(xv, w_ref, rows)
        _, vjp = jax.vjp(lambda cc: _post_conv(cc, l2, scale), c)
        (dc,) = vjp(dn_ref[...])
        dx = w_ref[3:4, :] * dc
        dw = jnp.zeros((8, HD), F32)
        rid = _iota2((8, HD), 0)
        dw = dw + jnp.where(rid == 3, jnp.sum(dc * xv, axis=0, keepdims=True), 0.0)
        for s in (1, 2, 3):
            dx = dx + w_ref[3 - s:4 - s, :] * jnp.where(rows < t - s, pltpu.roll(dc, t - s, 0), 0.0)
            xs = jnp.where(rows >= s, pltpu.roll(xv, s, 0), 0.0)
            dw = dw + jnp.where(rid == 3 - s, jnp.sum(dc * xs, axis=0, keepdims=True), 0.0)
        dx_ref[...] = dx.astype(BF16)
        dw_ref[...] = dw

    return pl.pallas_call(
        body, grid=(N_HEADS,),
        in_specs=[pl.BlockSpec((t, HD), lambda h: (0, h + group * N_HEADS)),
                  pl.BlockSpec((8, HD), lambda h: (0, h + group * N_HEADS)),
                  pl.BlockSpec((t, HD), lambda h: (0, h))],
        out_specs=[pl.BlockSpec((t, HD), lambda h: (0, h)), pl.BlockSpec((8, HD), lambda h: (0, h))],
        out_shape=[_sds((t, GW), BF16), _sds((8, GW), F32)], name=name,
        compiler_params=_params(1, VMEM_LIMIT))(proj, conv_w8, dn)


def _chunk_cumsum(g, rows):
    pos = rows % CHUNK
    s = 1
    while s < CHUNK:
        g = g + jnp.where(pos >= s, pltpu.roll(g, s, 0), 0.0)
        s *= 2
    return g


def _gates_fwd(name, proj, small_blk, alog_row, dtb_row):
    t = proj.shape[0]
    tm = min(256, t)

    def body(s_ref, a_ref, b_ref, beta_ref, gc_ref):
        sm = s_ref[...]
        beta = _sigmoid(sm)
        g = -jnp.exp(a_ref[...]) * _softplus(sm + b_ref[...])
        gc = _chunk_cumsum(g, _iota2((tm, HD), 0))
        lane = _iota2((tm, HD), 1)
        for h in range(N_HEADS):
            bcol = jnp.sum(jnp.where(lane == h, beta, 0.0), axis=1, keepdims=True)
            gcol = jnp.sum(jnp.where(lane == 8 + h, gc, 0.0), axis=1, keepdims=True)
            beta_ref[:, h * HD:(h + 1) * HD] = jnp.broadcast_to(bcol, (tm, HD))
            gc_ref[:, h * HD:(h + 1) * HD] = jnp.broadcast_to(gcol, (tm, HD))

    vec = pl.BlockSpec((1, HD), lambda i: (0, 0))
    wide = pl.BlockSpec((tm, GW), lambda i: (i, 0))
    return pl.pallas_call(
        body, grid=(t // tm,),
        in_specs=[pl.BlockSpec((tm, HD), lambda i: (i, small_blk)), vec, vec], out_specs=[wide, wide],
        out_shape=[_sds((t, GW), F32), _sds((t, GW), F32)], name=name,
        compiler_params=_params(1))(proj, alog_row, dtb_row)


def _gates_bwd(name, proj, small_blk, alog_row, dtb_row, dbeta_b, dg_b):
    t = proj.shape[0]
    tm = min(256, t)

    def body(s_ref, a_ref, b_ref, db_ref, dg_ref, ds_ref, da_ref, dbias_ref):
        sm = s_ref[...]
        lane = _iota2((tm, HD), 1)
        db = jnp.zeros((tm, HD), F32)
        dg = jnp.zeros((tm, HD), F32)
        for h in range(N_HEADS):
            db = db + jnp.where(lane == h, db_ref[:, h * HD:(h + 1) * HD], 0.0)
            dg = dg + jnp.where(lane == 8 + h, dg_ref[:, h * HD:(h + 1) * HD], 0.0)
        beta = _sigmoid(sm)
        ea = jnp.exp(a_ref[...])
        pre = sm + b_ref[...]
        g = -ea * _softplus(pre)
        dpre = dg * (-ea) * _sigmoid(pre)
        ds_ref[...] = (db * beta * (1.0 - beta) + dpre).astype(BF16)

        @pl.when(pl.program_id(0) == 0)
        def _():
            da_ref[...] = jnp.zeros_like(da_ref)
            dbias_ref[...] = jnp.zeros_like(dbias_ref)

        da_ref[...] += jnp.sum(dg * g, axis=0, keepdims=True)
        dbias_ref[...] += jnp.sum(dpre, axis=0, keepdims=True)

    vec = pl.BlockSpec((1, HD), lambda i: (0, 0))
    wide = pl.BlockSpec((tm, GW), lambda i: (i, 0))
    return pl.pallas_call(
        body, grid=(t // tm,),
        in_specs=[pl.BlockSpec((tm, HD), lambda i: (i, small_blk)), vec, vec, wide, wide],
        out_specs=[pl.BlockSpec((tm, HD), lambda i: (i, 0)), vec, vec],
        out_shape=[_sds((t, HD), BF16), _sds((1, HD), F32), _sds((1, HD), F32)], name=name,
        compiler_params=_params(1))(proj, alog_row, dtb_row, dbeta_b, dg_b)


def _pair_masks():
    ii = _iota2((PAIR, PAIR), 0)
    jj = _iota2((PAIR, PAIR), 1)
    same = (ii // CHUNK) == (jj // CHUNK)
    return ii, jj, same & (ii >= jj), same & (ii > jj)


def _to_row(col_b, ii, jj):
    return jnp.sum(jnp.where(ii == jj, col_b, 0.0), axis=0, keepdims=True)


def _to_col(row, ii, jj):
    return jnp.sum(jnp.where(ii == jj, jnp.broadcast_to(row, (PAIR, PAIR)), 0.0), axis=1, keepdims=True)


def _decay_parts(gc, last_a, last_b, ii, jj, causal):
    diff = gc - _to_row(gc, ii, jj)
    dmat = jnp.where(causal, jnp.exp(jnp.where(causal, diff, 0.0)), 0.0)
    glast = jnp.where(ii < CHUNK, last_a, last_b)
    return dmat, jnp.exp(gc), jnp.exp(glast - gc)


def _unit_lower_inverse(lows, ii, jj):
    eye = jnp.where(ii == jj, 1.0, 0.0)
    mm = lambda xs, ys: [_dot3(a, b, 1, 0) for a, b in zip(xs, ys)]
    plus = lambda xs: [eye + a for a in xs]
    minus = lambda xs: [eye - a for a in xs]
    d1 = [jnp.where((ii // 16) == (jj // 16), low, 0.0) for low in lows]
    d2 = mm(d1, d1)
    a = mm(minus(d1), plus(d2))
    d4 = mm(d2, d2)
    a = mm(a, plus(d4))
    d8 = mm(d4, d4)
    td = mm(a, plus(d8))
    n1 = mm(td, [low - d for low, d in zip(lows, d1)])
    n2 = mm(n1, n1)
    return mm(mm(minus(n1), plus(n2)), td)


def _delta_prep(name, qn, kn, vv, beta_b, gc_b):
    t = qn.shape[0]

    def body(q_ref, k_ref, v_ref, b_ref, g_ref, u_ref, w_ref, p_ref, t_ref, qd_ref, kd_ref):
        ii, jj, causal, strict = _pair_masks()
        sls = [slice(hh * HD, (hh + 1) * HD) for hh in range(HEADS_PER_STEP)]
        lows = []
        for sl in sls:
            q, k, beta = q_ref[:, sl], k_ref[:, sl], b_ref[:, sl]
            dmat, gam, e2 = _decay_parts(g_ref[:, sl], g_ref[CHUNK - 1:CHUNK, sl], g_ref[PAIR - 1:PAIR, sl],
                                         ii, jj, causal)
            k16 = _b16(k)
            lows.append(jnp.where(strict, beta * _dot(k16, k16, 1, 1) * dmat, 0.0))
            p_ref[:, sl] = jnp.where(causal, _dot(_b16(q), k16, 1, 1) * dmat, 0.0).astype(BF16)
            qd_ref[:, sl] = (q * gam).astype(BF16)
            kd_ref[:, sl] = (k * e2).astype(BF16)
        for sl, tinv in zip(sls, _unit_lower_inverse(lows, ii, jj)):
            beta = b_ref[:, sl]
            t_ref[:, sl] = tinv
            u_ref[:, sl] = _dot3(tinv, v_ref[:, sl] * beta, 1, 0)
            w_ref[:, sl] = _dot3(tinv, k_ref[:, sl] * (beta * jnp.exp(g_ref[:, sl])), 1, 0).astype(BF16)

    blk = pl.BlockSpec((PAIR, HEADS_PER_STEP * HD), lambda i, h: (i, h))
    return pl.pallas_call(
        body, grid=(t // PAIR, N_HEADS // HEADS_PER_STEP), in_specs=[blk] * 5, out_specs=[blk] * 6,
        out_shape=[_sds((t, GW), F32), _sds((t, GW), BF16), _sds((t, GW), BF16), _sds((t, GW), F32),
                   _sds((t, GW), BF16), _sds((t, GW), BF16)],
        name=name, compiler_params=_params(2))(qn, kn, vv, beta_b, gc_b)


def _delta_scan(name, u, w, p, qd, kd, gc_b):
    t = u.shape[0]
    n = t // CHUNK

    def body(u_ref, w_ref, p_ref, qd_ref, kd_ref, g_ref, o_ref, vn_ref, sh_ref, state):
        @pl.when(pl.program_id(0) == 0)
        def _():
            state[...] = jnp.zeros_like(state)

        for h in range(N_HEADS):
            sl = slice(h * HD, (h + 1) * HD)
            s = state[h]
            sh_ref[h] = s
            s16 = _b16(s)
            vnew = u_ref[:, sl] - _dot(w_ref[:, sl], s16, 1, 0)
            vn16 = _b16(vnew)
            vpair = jnp.concatenate([vn16, vn16], axis=0)
            o_ref[:, sl] = _dot(qd_ref[:, sl], s16, 1, 0) + _dot(p_ref[:, sl], vpair, 1, 0)
            vn_ref[:, sl] = vn16
            dec = jnp.exp(g_ref[CHUNK - 1:CHUNK, sl])
            state[h] = s * dec + _dot(kd_ref[:, sl], vn16, 0, 0)

    blk = pl.BlockSpec((CHUNK, GW), lambda i: (i, 0))
    return pl.pallas_call(
        body, grid=(n,), in_specs=[blk] * 6,
        out_specs=[blk, blk, pl.BlockSpec((None, N_HEADS, HD, HD), lambda i: (i, 0, 0, 0))],
        out_shape=[_sds((t, GW), F32), _sds((t, GW), BF16), _sds((n, N_HEADS, HD, HD), F32)],
        scratch_shapes=[pltpu.VMEM((N_HEADS, HD, HD), F32)], name=name,
        compiler_params=_params(1))(u, w, p, qd, kd, gc_b)


def _delta_scan_bwd(name, do, w, p, qd, kd, gc_b, vn, s_hist):
    t = do.shape[0]
    n = t // CHUNK

    def body(do_ref, w_ref, p_ref, qd_ref, kd_ref, g_ref, vn_ref, sh_ref,
             dvn_ref, dqd_ref, dkd_ref, dw_ref, ddec_ref, dstate):
        @pl.when(pl.program_id(0) == 0)
        def _():
            dstate[...] = jnp.zeros_like(dstate)

        for h in range(N_HEADS):
            sl = slice(h * HD, (h + 1) * HD)
            ds = dstate[h]
            ds16 = _b16(ds)
            s_in = sh_ref[h]
            s16 = _b16(s_in)
            do16 = _b16(do_ref[:, sl])
            ptdo = _dot(p_ref[:, sl], do16, 0, 0)
            dvn = ptdo[:CHUNK, :] + ptdo[CHUNK:, :] + _dot(kd_ref[:, sl], ds16, 1, 0)
            dvn16 = _b16(dvn)
            dec = jnp.exp(g_ref[CHUNK - 1:CHUNK, sl])
            dstate[h] = ds * dec + _dot(qd_ref[:, sl], do16, 0, 0) - _dot(w_ref[:, sl], dvn16, 0, 0)
            dvn_ref[:, sl] = dvn
            dqd_ref[:, sl] = _dot(do16, s16, 1, 1)
            dw_ref[:, sl] = -_dot(dvn16, s16, 1, 1)
            dkd_ref[:, sl] = _dot(vn_ref[:, sl], ds16, 1, 1)
            tot = jnp.sum(jnp.sum(s_in * ds, axis=1, keepdims=True), axis=0, keepdims=True)
            ddec_ref[:, sl] = jnp.broadcast_to(tot, (8, HD))

    blk = pl.BlockSpec((CHUNK, GW), lambda i: (n - 1 - i, 0))
    return pl.pallas_call(
        body, grid=(n,),
        in_specs=[blk] * 7 + [pl.BlockSpec((None, N_HEADS, HD, HD), lambda i: (n - 1 - i, 0, 0, 0))],
        out_specs=[blk] * 4 + [pl.BlockSpec((8, GW), lambda i: (n - 1 - i, 0))],
        out_shape=[_sds((t, GW), F32)] * 4 + [_sds((n * 8, GW), F32)],
        scratch_shapes=[pltpu.VMEM((N_HEADS, HD, HD), F32)], name=name,
        compiler_params=_params(1))(do, w, p, qd, kd, gc_b, vn, s_hist)


def _delta_prep_bwd(name, qn, kn, vv, beta_b, gc_b, tinv, u, w, vn, do, dvn, dqd, dkd, dw, ddec):
    t = qn.shape[0]

    def body(q_ref, k_ref, v_ref, b_ref, g_ref, t_ref, u_ref, w_ref, vn_ref, do_ref, dvn_ref, dqd_ref,
             dkd_ref, dw_ref, ddec_ref, dq_ref, dk_ref, dv_ref, dbeta_ref, dg_ref):
        ii, jj, causal, strict = _pair_masks()
        suffix = ((ii // CHUNK) == (jj // CHUNK)) & (jj >= ii)
        first = ii < CHUNK
        rs = lambda a: jnp.sum(a, axis=1, keepdims=True)
        for hh in range(HEADS_PER_STEP):
            sl = slice(hh * HD, (hh + 1) * HD)
            q, k, v, beta, gc = q_ref[:, sl], k_ref[:, sl], v_ref[:, sl], b_ref[:, sl], g_ref[:, sl]
            last_a, last_b = g_ref[CHUNK - 1:CHUNK, sl], g_ref[PAIR - 1:PAIR, sl]
            dmat, gam, e2 = _decay_parts(gc, last_a, last_b, ii, jj, causal)
            q16, k16 = _b16(q), _b16(k)
            kk = _dot(k16, k16, 1, 1)
            qk = _dot(q16, k16, 1, 1)
            dqd, dkd = dqd_ref[:, sl], dkd_ref[:, sl]
            dp = jnp.where(causal, _dot(_b16(do_ref[:, sl]), vn_ref[:, sl], 1, 1), 0.0)
            dpd16 = _b16(dp * dmat)
            tinv_v = t_ref[:, sl]
            x = _dot3(tinv_v, dvn_ref[:, sl], 0, 0)
            y = _dot3(tinv_v, dw_ref[:, sl], 0, 0)
            da = -jnp.where(strict, _dot(_b16(x), _b16(u_ref[:, sl]), 1, 1) + _dot(_b16(y), w_ref[:, sl], 1, 1), 0.0)
            dkk16 = _b16(da * beta * dmat)
            dq_ref[:, sl] = gam * dqd + _dot(dpd16, k16, 1, 0)
            dk_ref[:, sl] = (e2 * dkd + _dot(dpd16, q16, 0, 0) + beta * gam * y
                             + _dot(dkk16, k16, 1, 0) + _dot(dkk16, k16, 0, 0))
            dv_ref[:, sl] = beta * x
            dbeta = rs(v * x) + rs(k * gam * y) + rs(da * kk * dmat)
            dbeta_ref[:, sl] = jnp.broadcast_to(dbeta, (PAIR, HD))
            m = (dp * qk + da * beta * kk) * dmat
            dgam = rs(q * dqd) + rs(k * beta * y)
            de2 = rs(k * dkd)
            colsum = _to_col(jnp.sum(m, axis=0, keepdims=True), ii, jj)
            te2 = de2 * e2
            dgc = rs(m) - colsum + gam * dgam - te2
            tail_a = jnp.sum(jnp.where(first, te2, 0.0), axis=0, keepdims=True)
            tail_b = jnp.sum(jnp.where(first, 0.0, te2), axis=0, keepdims=True)
            dgc = dgc + jnp.where(ii == CHUNK - 1, tail_a + ddec_ref[0:1, sl] * jnp.exp(last_a), 0.0)
            dgc = dgc + jnp.where(ii == PAIR - 1, tail_b + ddec_ref[8:9, sl] * jnp.exp(last_b), 0.0)
            dgc_row = _to_row(dgc, ii, jj)
            dg = jnp.sum(jnp.where(suffix, jnp.broadcast_to(dgc_row, (PAIR, PAIR)), 0.0), axis=1, keepdims=True)
            dg_ref[:, sl] = jnp.broadcast_to(dg, (PAIR, HD))

    blk = pl.BlockSpec((PAIR, HEADS_PER_STEP * HD), lambda i, h: (i, h))
    return pl.pallas_call(
        body, grid=(t // PAIR, N_HEADS // HEADS_PER_STEP),
        in_specs=[blk] * 14 + [pl.BlockSpec((16, HEADS_PER_STEP * HD), lambda i, h: (i, h))], out_specs=[blk] * 5,
        out_shape=[_sds((t, GW), F32)] * 5, name=name,
        compiler_params=_params(2))(qn, kn, vv, beta_b, gc_b, tinv, u, w, vn, do, dvn, dqd, dkd, dw, ddec)


def _rope_tables(pos_col, inv_row):
    ang = pos_col.astype(F32) * inv_row
    lane = _iota2(ang.shape, 1)
    return jnp.cos(ang), jnp.where(lane < HD // 2, -1.0, 1.0) * jnp.sin(ang)


def _head_rms(xh, wv):
    return xh * lax.rsqrt(jnp.mean(xh * xh, axis=-1, keepdims=True) + EPS) * wv


def _qk_fwd(name, proj, blk_idx, w_row, pos_col, inv_row):
    t = proj.shape[0]
    tm = min(256, t)

    def body(x_ref, w_ref, pos_ref, inv_ref, o_ref):
        cos, sin = _rope_tables(pos_ref[...], inv_ref[...])
        for h in range(N_HEADS):
            y = _head_rms(x_ref[:, h * HD:(h + 1) * HD], w_ref[...])
            o_ref[:, h * HD:(h + 1) * HD] = y * cos + pltpu.roll(y, HD // 2, 1) * sin

    vec = pl.BlockSpec((1, HD), lambda i: (0, 0))
    return pl.pallas_call(
        body, grid=(t // tm,),
        in_specs=[pl.BlockSpec((tm, GW), lambda i: (i, blk_idx)), vec, pl.BlockSpec((tm, 1), lambda i: (i, 0)), vec],
        out_specs=pl.BlockSpec((tm, GW), lambda i: (i, 0)), out_shape=_sds((t, GW), F32), name=name,
        compiler_params=_params(1))(proj, w_row, pos_col, inv_row)


def _qk_bwd(name, proj, blk_idx, w_row, pos_col, inv_row, dy_full):
    t = proj.shape[0]
    tm = min(256, t)

    def body(x_ref, w_ref, pos_ref, inv_ref, dy_ref, dx_ref, dw_ref):
        cos, sin = _rope_tables(pos_ref[...], inv_ref[...])
        dw = jnp.zeros((1, HD), F32)
        for h in range(N_HEADS):
            sl = slice(h * HD, (h + 1) * HD)
            dy = dy_ref[:, sl]
            dy = dy * cos - pltpu.roll(dy, HD // 2, 1) * sin
            _, vjp = jax.vjp(_head_rms, x_ref[:, sl], w_ref[...])
            dx, dwh = vjp(dy)
            dw = dw + dwh
            dx_ref[:, sl] = dx.astype(BF16)

        @pl.when(pl.program_id(0) == 0)
        def _():
            dw_ref[...] = jnp.zeros_like(dw_ref)

        dw_ref[...] += dw

    vec = pl.BlockSpec((1, HD), lambda i: (0, 0))
    wide = pl.BlockSpec((tm, GW), lambda i: (i, 0))
    return pl.pallas_call(
        body, grid=(t // tm,),
        in_specs=[pl.BlockSpec((tm, GW), lambda i: (i, blk_idx)), vec, pl.BlockSpec((tm, 1), lambda i: (i, 0)), vec,
                  wide],
        out_specs=[wide, vec], out_shape=[_sds((t, GW), BF16), _sds((1, HD), F32)], name=name,
        compiler_params=_params(1))(proj, w_row, pos_col, inv_row, dy_full)


GROUP = SPAN * max(DILATIONS)
SCALE = HD ** -0.5


def _band_mask(lo):
    qi = _iota2((SPAN, 2 * SPAN), 0)
    ki = _iota2((SPAN, 2 * SPAN), 1)
    return (ki >= qi) & (ki <= qi + SPAN) & (ki >= lo)


def _tiles():
    return [(pi, r, u, rho) for pi, r in enumerate(DILATIONS) for u in range(GROUP // (SPAN * r)) for rho in range(r)]


def _rows(r, u, rho):
    return pl.ds(u * SPAN * r + rho, SPAN, stride=r) if r > 1 else pl.ds(u * SPAN, SPAN)


def _attn_fwd(name, q, k, v, v_blk):
    t = q.shape[0]

    def body(qc_ref, kc_ref, vc_ref, kp_ref, vp_ref, ob_ref, lse_ref, o_scr, l_scr):
        mask_in = _band_mask(0)
        mask_edge = _band_mask(jnp.where(pl.program_id(0) == 0, SPAN, 0))
        for pi, r, u, rho in _tiles():
            rows = _rows(r, u, rho)
            if u > 0:
                prows, kp_src, vp_src, mask = _rows(r, u - 1, rho), kc_ref, vc_ref, mask_in
            else:
                prows, kp_src, vp_src, mask = _rows(r, GROUP // (SPAN * r) - 1, rho), kp_ref, vp_ref, mask_edge
            kcat = jnp.concatenate([kp_src[prows, :], kc_ref[rows, :]], axis=0).astype(BF16)
            vcat = jnp.concatenate([vp_src[prows, :], vc_ref[rows, :]], axis=0).astype(BF16)
            s = jnp.where(mask, _dot(qc_ref[rows, :].astype(BF16), kcat, 1, 1) * SCALE, NEG)
            m = jnp.max(s, axis=1, keepdims=True)
            p = jnp.exp(s - m)
            den = jnp.sum(p, axis=1, keepdims=True)
            o_scr[pi, rows, :] = _dot(_b16(p), vcat, 1, 0) / den
            l_scr[pi, rows, :] = jnp.broadcast_to(m + jnp.log(den), (SPAN, HD))
        step = 256
        for c in range(GROUP // step):
            sl = pl.ds(c * step, step)
            ob, lse = _merge([o_scr[i, sl, :] for i in range(3)], [l_scr[i, sl, :] for i in range(3)])
            ob_ref[sl, :] = ob
            lse_ref[sl, :] = lse

    cur = pl.BlockSpec((GROUP, HD), lambda g, h: (g, h))
    prev = pl.BlockSpec((GROUP, HD), lambda g, h: (jnp.maximum(g - 1, 0), h))
    vcur = pl.BlockSpec((GROUP, HD), lambda g, h: (g, v_blk * N_HEADS + h))
    vprev = pl.BlockSpec((GROUP, HD), lambda g, h: (jnp.maximum(g - 1, 0), v_blk * N_HEADS + h))
    return pl.pallas_call(
        body, grid=(t // GROUP, N_HEADS), in_specs=[cur, cur, vcur, prev, vprev], out_specs=[cur, cur],
        out_shape=[_sds((t, GW), F32), _sds((t, GW), F32)],
        scratch_shapes=[pltpu.VMEM((3, GROUP, HD), F32), pltpu.VMEM((3, GROUP, HD), F32)], name=name,
        compiler_params=_params(2))(q, k, v, k, v)


def _attn_bwd(name, q, k, v, v_blk, do, lse, delta):
    t = q.shape[0]
    ng = t // GROUP

    def pair(qt, dot, lt, dlt, kcat, vcat, mask):
        wide = kcat.shape[0] // SPAN
        lw = jnp.concatenate([lt] * wide, axis=1) if wide > 1 else lt
        dw = jnp.concatenate([dlt] * wide, axis=1) if wide > 1 else dlt
        s = _dot(qt, kcat, 1, 1) * SCALE
        p = jnp.where(mask, jnp.exp(jnp.where(mask, s - lw, 0.0)), 0.0)
        ds = p * (_dot(dot, vcat, 1, 1) - dw) * SCALE
        return _b16(ds), _b16(p)

    def body(qc_ref, kc_ref, vc_ref, doc_ref, lc_ref, dc_ref, kp_ref, vp_ref, qn_ref, don_ref, ln_ref, dn_ref,
             dq_ref, dk_ref, dv_ref):
        g = pl.program_id(0)
        mask_in = _band_mask(0)
        mask_edge = _band_mask(jnp.where(g == 0, SPAN, 0))
        dk_ref[...] = jnp.zeros_like(dk_ref)
        dv_ref[...] = jnp.zeros_like(dv_ref)
        for pi, r, u, rho in _tiles():
            rows = _rows(r, u, rho)
            if u > 0:
                prows, kp_src, vp_src, mask = _rows(r, u - 1, rho), kc_ref, vc_ref, mask_in
            else:
                prows, kp_src, vp_src, mask = _rows(r, GROUP // (SPAN * r) - 1, rho), kp_ref, vp_ref, mask_edge
            kcat = jnp.concatenate([kp_src[prows, :], kc_ref[rows, :]], axis=0).astype(BF16)
            vcat = jnp.concatenate([vp_src[prows, :], vc_ref[rows, :]], axis=0).astype(BF16)
            qt, dot = qc_ref[rows, :].astype(BF16), doc_ref[rows, :].astype(BF16)
            ds, p = pair(qt, dot, lc_ref[rows, :], dc_ref[rows, :], kcat, vcat, mask)
            dq_t = _dot(ds, kcat, 1, 0)
            if pi == 0:
                dq_ref[rows, :] = dq_t
            else:
                dq_ref[rows, :] += dq_t
            dk2 = _dot(ds, qt, 0, 0)
            dv2 = _dot(p, dot, 0, 0)
            dk_ref[rows, :] += dk2[SPAN:, :]
            dv_ref[rows, :] += dv2[SPAN:, :]
            if u > 0:
                dk_ref[prows, :] += dk2[:SPAN, :]
                dv_ref[prows, :] += dv2[:SPAN, :]
        qi = _iota2((SPAN, SPAN), 0)
        ki = _iota2((SPAN, SPAN), 1)
        mask_next = (ki >= qi) & (ki < jnp.where(g == ng - 1, 0, SPAN))
        for r in DILATIONS:
            for rho in range(r):
                krows, qrows = _rows(r, GROUP // (SPAN * r) - 1, rho), _rows(r, 0, rho)
                qt, dot = qn_ref[qrows, :].astype(BF16), don_ref[qrows, :].astype(BF16)
                ds, p = pair(qt, dot, ln_ref[qrows, :], dn_ref[qrows, :], kc_ref[krows, :].astype(BF16),
                             vc_ref[krows, :].astype(BF16), mask_next)
                dk_ref[krows, :] += _dot(ds, qt, 0, 0)
                dv_ref[krows, :] += _dot(p, dot, 0, 0)

    cur = pl.BlockSpec((GROUP, HD), lambda g, h: (g, h))
    prev = pl.BlockSpec((GROUP, HD), lambda g, h: (jnp.maximum(g - 1, 0), h))
    nxt = pl.BlockSpec((GROUP, HD), lambda g, h: (jnp.minimum(g + 1, ng - 1), h))
    vcur = pl.BlockSpec((GROUP, HD), lambda g, h: (g, v_blk * N_HEADS + h))
    vprev = pl.BlockSpec((GROUP, HD), lambda g, h: (jnp.maximum(g - 1, 0), v_blk * N_HEADS + h))
    return pl.pallas_call(
        body, grid=(ng, N_HEADS), in_specs=[cur, cur, vcur, cur, cur, cur, prev, vprev] + [nxt] * 4,
        out_specs=[cur] * 3,
        out_shape=[_sds((t, GW), F32)] * 3, name=name,
        compiler_params=_params(2))(q, k, v, do, lse, delta, k, v, q, do, lse, delta)


def _merge(os_, ls_):
    m = jnp.maximum(jnp.maximum(ls_[0], ls_[1]), ls_[2])
    ws = [jnp.exp(l - m) for l in ls_]
    tot = ws[0] + ws[1] + ws[2]
    ob = (ws[0] * os_[0] + ws[1] * os_[1] + ws[2] * os_[2]) / tot
    return ob, m + jnp.log(tot)


def _gated_norm(oa, z, wv):
    return _head_rms(oa, wv) * _silu(z)


def _mix_fwd(name, oa_raw, proj, z_blk, ob, w_dn, w_an):
    t = oa_raw.shape[0]
    tm = min(256, t)

    def body(oa_ref, z_ref, ob_ref, wd_ref, wa_ref, mix_ref):
        for h in range(N_HEADS):
            sl = slice(h * HD, (h + 1) * HD)
            mix_ref[:, sl] = _gated_norm(oa_ref[:, sl], z_ref[:, sl], wd_ref[...]).astype(BF16)
            mix_ref[:, GW + h * HD:GW + (h + 1) * HD] = _head_rms(ob_ref[:, sl], wa_ref[...]).astype(BF16)

    vec = pl.BlockSpec((1, HD), lambda i: (0, 0))
    wide = pl.BlockSpec((tm, GW), lambda i: (i, 0))
    return pl.pallas_call(
        body, grid=(t // tm,),
        in_specs=[wide, pl.BlockSpec((tm, GW), lambda i: (i, z_blk)), wide, vec, vec],
        out_specs=pl.BlockSpec((tm, 2 * GW), lambda i: (i, 0)),
        out_shape=_sds((t, 2 * GW), BF16), name=name,
        compiler_params=_params(1))(oa_raw, proj, ob, w_dn, w_an)


def _mix_bwd(name, dmixed, oa_raw, proj, z_blk, ob, w_dn, w_an, dep):
    t = oa_raw.shape[0]
    tm = min(256, t)

    def body(dm_ref, oa_ref, z_ref, ob_ref, wd_ref, wa_ref, dep_ref,
             doa_ref, dz_ref, dob_ref, dl_ref, dwd_ref, dwa_ref):
        dwd = jnp.zeros((1, HD), F32)
        dwa = jnp.zeros((1, HD), F32)
        for h in range(N_HEADS):
            sl = slice(h * HD, (h + 1) * HD)
            _, vjp = jax.vjp(_gated_norm, oa_ref[:, sl], z_ref[:, sl], wd_ref[...])
            doa, dz, dw1 = vjp(dm_ref[:, sl])
            doa_ref[:, sl] = doa
            dz_ref[:, sl] = dz.astype(BF16)
            dwd = dwd + dw1
            obh = ob_ref[:, sl]
            _, vjp2 = jax.vjp(_head_rms, obh, wa_ref[...])
            dob, dw2 = vjp2(dm_ref[:, GW + h * HD:GW + (h + 1) * HD])
            dwa = dwa + dw2
            dob_ref[:, sl] = dob
            dl_ref[:, sl] = jnp.broadcast_to(jnp.sum(dob * obh, axis=1, keepdims=True), (tm, HD))

        @pl.when(pl.program_id(0) == 0)
        def _():
            dwd_ref[...] = jnp.zeros_like(dwd_ref)
            dwa_ref[...] = jnp.zeros_like(dwa_ref)

        dwd_ref[...] += dwd
        dwa_ref[...] += dwa

    vec = pl.BlockSpec((1, HD), lambda i: (0, 0))
    wide = pl.BlockSpec((tm, GW), lambda i: (i, 0))
    return pl.pallas_call(
        body, grid=(t // tm,),
        in_specs=[pl.BlockSpec((tm, 2 * GW), lambda i: (i, 0)), wide, pl.BlockSpec((tm, GW), lambda i: (i, z_blk)),
                  wide, vec, vec, ANY],
        out_specs=[wide, wide, wide, wide, vec, vec],
        out_shape=[_sds((t, GW), F32), _sds((t, GW), BF16), _sds((t, GW), F32), _sds((t, GW), F32),
                   _sds((1, HD), F32), _sds((1, HD), F32)], name=name,
        compiler_params=_params(1))(dmixed, oa_raw, proj, ob, w_dn, w_an, dep)


def _gate_up_swiglu(name, h2, w_gu_g):
    t, d = h2.shape
    n = w_gu_g.shape[2]
    per = N_DEV // 2
    tm = min(512, t)

    def body(a_ref, bg_ref, bu_ref, gu_ref, act_ref):
        a = a_ref[...]
        g = _dot(a, bg_ref[...], 1, 0)
        up = _dot(a, bu_ref[...], 1, 0)
        gu_ref[0] = g
        gu_ref[1] = up
        act_ref[...] = (_silu(g) * up).astype(BF16)

    return pl.pallas_call(
        body, grid=(per, t // tm),
        in_specs=[pl.BlockSpec((tm, d), lambda j, i: (i, 0)), pl.BlockSpec((None, d, n), lambda j, i: (j, 0, 0)),
                  pl.BlockSpec((None, d, n), lambda j, i: (j + per, 0, 0))],
        out_specs=[pl.BlockSpec((2, tm, n), lambda j, i: (0, i, j)), pl.BlockSpec((tm, n), lambda j, i: (i, j))],
        out_shape=[_sds((2, t, per * n), F32), _sds((t, per * n), BF16)], name=name,
        compiler_params=_params(2))(h2, w_gu_g, w_gu_g)


def _d_gate_up(name, dy16, w_down, gu3, dep):
    t, d = dy16.shape
    f = w_down.shape[0]
    tm, tn = min(512, t), f // 4

    def body(a_ref, b_ref, g_ref, dep_ref, o_ref):
        dact = _dot(a_ref[...], b_ref[...], 1, 1)
        g, up = g_ref[0], g_ref[1]
        sg = _sigmoid(g)
        o_ref[0] = (dact * up * sg * (1.0 + g * (1.0 - sg))).astype(BF16)
        o_ref[1] = (dact * g * sg).astype(BF16)

    return pl.pallas_call(
        body, grid=(f // tn, t // tm),
        in_specs=[pl.BlockSpec((tm, d), lambda j, i: (i, 0)), pl.BlockSpec((tn, d), lambda j, i: (j, 0)),
                  pl.BlockSpec((2, tm, tn), lambda j, i: (0, i, j)), ANY],
        out_specs=pl.BlockSpec((2, tm, tn), lambda j, i: (0, i, j)), out_shape=_sds((2, t, f), BF16), name=name,
        compiler_params=_params(2))(dy16, w_down, gu3, dep)


def _loss_head(name, y, target):
    t, d = y.shape
    tm = min(512, t)

    def body(y_ref, t_ref, dy_ref, l_ref):
        diff = y_ref[...] - t_ref[...]
        dy_ref[...] = diff * (1.0 / d)
        part = jnp.sum(jnp.sum(diff * diff, axis=1, keepdims=True), axis=0, keepdims=True) * (0.5 / d)

        @pl.when(pl.program_id(0) == 0)
        def _():
            l_ref[...] = jnp.zeros_like(l_ref)

        l_ref[...] += jnp.broadcast_to(part, (8, 128))

    row = pl.BlockSpec((tm, d), lambda i: (i, 0))
    return pl.pallas_call(body, grid=(t // tm,), in_specs=[row, row],
                          out_specs=[row, pl.BlockSpec((8, 128), lambda i: (0, 0))],
                          out_shape=[_sds((t, d), F32), _sds((8, 128), F32)], name=name,
                          compiler_params=_params(1))(y, target)


def _peer(me, k):
    pid = (me + k) % N_DEV
    return (pid // 4, (pid // 2) % 2, pid % 2)


def _my_id():
    return 4 * lax.axis_index("x") + 2 * lax.axis_index("y") + lax.axis_index("c")


def _exchange(name, arrays, scatter):
    n = len(arrays)

    def body(*refs):
        ins, outs = refs[:n], refs[n:2 * n]
        send_sems, recv_sems, local_sems = refs[2 * n:]
        me = _my_id()
        started = []
        for a in range(n):
            src = ins[a].at[me] if scatter[a] else ins[a]
            loc = pltpu.make_async_copy(src, outs[a].at[me], local_sems.at[a])
            loc.start()
            started.append(loc)
        remote = []
        for k in range(1, N_DEV):
            to = (me + k) % N_DEV
            for a in range(n):
                src = ins[a].at[to] if scatter[a] else ins[a]
                cp = pltpu.make_async_remote_copy(src_ref=src, dst_ref=outs[a].at[me],
                                                  send_sem=send_sems.at[a * (N_DEV - 1) + k - 1], recv_sem=recv_sems.at[a * (N_DEV - 1) + k - 1],
                                                  device_id=_peer(me, k), device_id_type=pl.DeviceIdType.MESH)
                cp.start()
                remote.append(cp)
        for k in range(1, N_DEV):
            frm = (me + N_DEV - k) % N_DEV
            for a in range(n):
                src = ins[a].at[frm] if scatter[a] else ins[a]
                pltpu.make_async_remote_copy(src_ref=src, dst_ref=outs[a].at[frm],
                                             send_sem=send_sems.at[a * (N_DEV - 1) + k - 1], recv_sem=recv_sems.at[a * (N_DEV - 1) + k - 1],
                                             device_id=_peer(me, k), device_id_type=pl.DeviceIdType.MESH).wait_recv()
        for cp in remote:
            cp.wait_send()
        for loc in started:
            loc.wait()

    out_shape = [_sds((N_DEV,) + (a.shape[1:] if sc else a.shape), a.dtype) for a, sc in zip(arrays, scatter)]
    return pl.pallas_call(
        body, in_specs=[ANY] * n, out_specs=[ANY] * n, out_shape=out_shape,
        scratch_shapes=[pltpu.SemaphoreType.DMA((n * (N_DEV - 1),)), pltpu.SemaphoreType.DMA((n * (N_DEV - 1),)),
                        pltpu.SemaphoreType.DMA((n,))],
        name=name)(*arrays)


def _gather_two_level(name, arrays):
    n = len(arrays)
    per = N_DEV - 1

    def body(*refs):
        ins, outs = refs[:n], refs[n:2 * n]
        send_sems, recv_sems, local_sems = refs[2 * n:]
        x, y, c = lax.axis_index("x"), lax.axis_index("y"), lax.axis_index("c")
        me, sibling = (x, y, c), (x, y, 1 - c)
        chips = [(1 - x, y), (x, 1 - y), (1 - x, 1 - y)]

        def copy(a, k, block, to, src=None):
            slot = outs[a].at[4 * block[0] + 2 * block[1] + block[2]]
            return pltpu.make_async_remote_copy(
                src_ref=slot if src is None else src, dst_ref=slot, send_sem=send_sems.at[a * per + k],
                recv_sem=recv_sems.at[a * per + k], device_id=to, device_id_type=pl.DeviceIdType.MESH)

        mine = [pltpu.make_async_copy(ins[a], outs[a].at[4 * x + 2 * y + c], local_sems.at[a]) for a in range(n)]
        for cp in mine:
            cp.start()
        first = [copy(a, 0, me, sibling, src=ins[a]) for a in range(n)]
        first += [copy(a, 1 + j, me, (*chip, c), src=ins[a]) for j, chip in enumerate(chips) for a in range(n)]
        for cp in first:
            cp.start()
        passed = []
        for j, chip in enumerate(chips):
            for a in range(n):
                copy(a, 1 + j, (*chip, c), me).wait_recv()
                cp = copy(a, 4 + j, (*chip, c), sibling)
                cp.start()
                passed.append(cp)
        for a in range(n):
            copy(a, 0, sibling, me).wait_recv()
            for j, chip in enumerate(chips):
                copy(a, 4 + j, (*chip, 1 - c), me).wait_recv()
        for cp in first + passed:
            cp.wait_send()
        for cp in mine:
            cp.wait()

    return pl.pallas_call(
        body, in_specs=[ANY] * n, out_specs=[ANY] * n,
        out_shape=[_sds((N_DEV,) + a.shape, a.dtype) for a in arrays],
        scratch_shapes=[pltpu.SemaphoreType.DMA((n * per,)), pltpu.SemaphoreType.DMA((n * per,)),
                        pltpu.SemaphoreType.DMA((n,))],
        name=name)(*arrays)


HBM = pl.BlockSpec(memory_space=pltpu.HBM)
SEM = pl.BlockSpec(memory_space=pltpu.SEMAPHORE)
EFFECT = pltpu.SideEffectType.DATAFLOW_SIDE_EFFECTING


def _remote_copies(srcs, lands, scatter, send_sems, recv_sems, me, incoming):
    out = []
    for k in range(1, N_DEV):
        other = (me + N_DEV - k) % N_DEV if incoming else (me + k) % N_DEV
        for a in range(len(srcs)):
            sem = a * (N_DEV - 1) + k - 1
            src = srcs[a].at[other] if scatter[a] else srcs[a]
            dst = lands[a].at[other if incoming else me]
            out.append(pltpu.make_async_remote_copy(src_ref=src, dst_ref=dst, send_sem=send_sems.at[sem],
                                                    recv_sem=recv_sems.at[sem], device_id=_peer(me, k),
                                                    device_id_type=pl.DeviceIdType.MESH))
    return out


def _exchange_start(name, arrays, scatter, dep):
    n = len(arrays)
    lands = [lax.empty((N_DEV,) + (a.shape[1:] if sc else a.shape), a.dtype) for a, sc in zip(arrays, scatter)]

    def body(*refs):
        srcs, land_refs = refs[:n], refs[n:2 * n]
        send_sems, recv_sems = refs[2 * n + 1], refs[2 * n + 2]
        token = refs[-1]
        for cp in _remote_copies(srcs, land_refs, scatter, send_sems, recv_sems, _my_id(), False):
            cp.start()
        token[...] = jnp.zeros_like(token)

    n_sem = n * (N_DEV - 1)
    out_shape = ([pltpu.SemaphoreType.DMA((n_sem,)), pltpu.SemaphoreType.DMA((n_sem,))]
                 + [pltpu.HBM(a.shape, a.dtype) for a in arrays] + [pltpu.HBM(l.shape, l.dtype) for l in lands]
                 + [_sds((8, 128), F32)])
    aliases = {i: 2 + i for i in range(2 * n)}
    args = [pltpu.with_memory_space_constraint(a, pltpu.HBM) for a in list(arrays) + lands] + [dep]
    res = pl.pallas_call(
        body, name=name, in_specs=[HBM] * (2 * n) + [ANY], out_shape=out_shape,
        out_specs=[SEM, SEM] + [HBM] * (2 * n) + [pl.BlockSpec(memory_space=pltpu.VMEM)],
        input_output_aliases=aliases, compiler_params=pltpu.CompilerParams(has_side_effects=EFFECT))(*args)
    return dict(send=res[0], recv=res[1], srcs=res[2:2 + n], lands=res[2 + n:2 + 2 * n], token=res[-1],
                scatter=scatter)


def _exchange_wait(name, started, after):
    n = len(started["srcs"])
    scatter = started["scatter"]

    def body(*refs):
        srcs, land_refs = refs[:n], refs[n:2 * n]
        send_sems, recv_sems = refs[2 * n], refs[2 * n + 1]
        me = _my_id()
        for cp in _remote_copies(srcs, land_refs, scatter, send_sems, recv_sems, me, False):
            cp.wait_send()
        for cp in _remote_copies(srcs, land_refs, scatter, send_sems, recv_sems, me, True):
            cp.wait_recv()

    arrs = list(started["srcs"]) + list(started["lands"])
    res = pl.pallas_call(
        body, name=name, in_specs=[HBM] * (2 * n) + [SEM, SEM, ANY],
        out_shape=[pltpu.HBM(a.shape, a.dtype) for a in arrs], out_specs=[HBM] * (2 * n),
        input_output_aliases={i: i for i in range(2 * n)},
        compiler_params=pltpu.CompilerParams(has_side_effects=EFFECT))(*arrs, started["send"], started["recv"], after)
    me = _my_id()
    out = []
    for src, land, sc in zip(res[:n], res[n:], scatter):
        own = lax.dynamic_index_in_dim(src, me, 0, keepdims=True) if sc else src[None]
        out.append(lax.dynamic_update_slice(land, own, (me,) + (0,) * (land.ndim - 1)))
    return out


def _adamw(name, parts, w, m, v):
    r, c = w.shape
    tr = r
    for cand in (128, 88, 64, 40, 8):
        if r % cand == 0:
            tr = cand
            break
    c1 = 1.0 / (1.0 - ADAM_B1 ** ADAM_STEP)
    c2 = 1.0 / (1.0 - ADAM_B2 ** ADAM_STEP)

    def body(p_ref, w_ref, m_ref, v_ref, g_ref, d_ref, nm_ref, nv_ref):
        g = p_ref[0].astype(F32)
        for s in range(1, N_DEV):
            g = g + p_ref[s].astype(F32)
        mn = ADAM_B1 * m_ref[...] + (1.0 - ADAM_B1) * g
        vn = ADAM_B2 * v_ref[...] + (1.0 - ADAM_B2) * (g * g)
        g_ref[...] = g
        nm_ref[...] = mn
        nv_ref[...] = vn
        d_ref[...] = -ADAM_LR * ((mn * c1) / (jnp.sqrt(vn * c2) + ADAM_EPS) + ADAM_WD * w_ref[...])

    blk = pl.BlockSpec((tr, c), lambda i: (i, 0))
    return pl.pallas_call(
        body, grid=(r // tr,), in_specs=[pl.BlockSpec((N_DEV, tr, c), lambda i: (0, i, 0)), blk, blk, blk],
        out_specs=[blk] * 4, out_shape=[_sds((r, c), F32)] * 4, name=name,
        compiler_params=_params(1, VMEM_LIMIT))(parts, w, m, v)


def _pad_rows(a, rows):
    return jnp.pad(a, ((0, rows - a.shape[0]), (0, 0)))


def _lane_row(vec8, offset):
    return jnp.pad(vec8.reshape(1, 8), ((0, 0), (offset, HD - 8 - offset)))


def kernel(x, positions, attn_norm_w, w_in, conv_w, a_log, dt_bias, delta_out_norm_w, q_norm_w, k_norm_w, attn_out_norm_w, w_out, ffn_norm_w, w_gate_up, w_down, loss_target, m_attn_norm_w, m_w_in, m_conv_w, m_a_log, m_dt_bias, m_delta_out_norm_w, m_q_norm_w, m_k_norm_w, m_attn_out_norm_w, m_w_out, m_ffn_norm_w, m_w_gate_up, m_w_down, v_attn_norm_w, v_w_in, v_conv_w, v_a_log, v_dt_bias, v_delta_out_norm_w, v_q_norm_w, v_k_norm_w, v_attn_out_norm_w, v_w_out, v_ffn_norm_w, v_w_gate_up, v_w_down):
    x2 = x[0]
    t, d = x2.shape
    target = loss_target[0]
    pos_col = positions.reshape(t, 1)
    half = HD // 2
    inv = (ROPE_THETA ** (-np.arange(half, dtype=np.float32) / half)).astype(np.float32)
    inv_row = jnp.asarray(np.concatenate([inv, inv]).reshape(1, HD))

    n_in = w_in.shape[2]
    n_gu = w_gate_up.shape[2]
    w_in_g, conv_g = _gather_two_level("gather_in", [w_in[0].astype(BF16), _pad_rows(conv_w[0], 8)])
    ffn_own = [w_gate_up[0].astype(BF16), w_down[0].astype(BF16), w_out[0].astype(BF16)]
    ffn_fly = _exchange_start("gather_ffn_start", ffn_own, [False] * 3, conv_g)
    n_main = 4 * GW
    n_small = 2 * N_HEADS
    segments = [(0, n_main, 0), (n_main + n_small, N_DEV * n_in, n_main), (n_main, n_main + n_small, 7 * GW)]
    pieces = []
    for lo, hi, _ in segments:
        f = lo
        while f < hi:
            j = f // n_in
            end = min(hi, (j + 1) * n_in)
            pieces.append(w_in_g[j][:, f - j * n_in:end - j * n_in])
            f = end
    w_cat = jnp.concatenate(pieces + [jnp.zeros((d, HD - n_small), BF16)], axis=1)
    n_cat = w_cat.shape[1]
    small_blk = (7 * GW) // HD
    conv_w8 =jnp.transpose(conv_g, (1, 0, 2)).reshape(8, 3 * GW)
    alog_row = _lane_row(a_log[0], 8)
    dtb_row = _lane_row(dt_bias[0], 8)

    tm = min(2048, t)
    h1 = _rms_fwd("norm1", x2, attn_norm_w, ffn_fly["token"])
    tn = 384
    proj = _mm("in_proj", h1, w_cat, grid=(t // tm, n_cat // tn, 1),
               a_spec=pl.BlockSpec((tm, d), lambda i, j, k: (i, 0)),
               b_spec=pl.BlockSpec((d, tn), lambda i, j, k: (0, j)),
               o_spec=pl.BlockSpec((tm, tn), lambda i, j, k: (i, j)),
               out_shape=_sds((t, n_cat), F32), ca=1, cb=0, nk=1)
    qn = _conv_fwd("conv_q", proj, conv_w8, 0, True, HD ** -0.5)
    kn = _conv_fwd("conv_k", proj, conv_w8, 1, True, 1.0)
    vv = _conv_fwd("conv_v", proj, conv_w8, 2, False, 1.0)
    beta_b, gc_b = _gates_fwd("gates", proj, small_blk, alog_row, dtb_row)
    u, w, p, tinv, qd, kd = _delta_prep("delta_prep", qn, kn, vv, beta_b, gc_b)
    oa_raw, vn, s_hist = _delta_scan("delta_scan", u, w, p, qd, kd, gc_b)

    aq = _qk_fwd("attn_q", proj, 4, q_norm_w, pos_col, inv_row)
    ak = _qk_fwd("attn_k", proj, 5, k_norm_w, pos_col, inv_row)
    ob, lse = _attn_fwd("attn_fwd", aq, ak, proj, 6)
    mixed = _mix_fwd("mix", oa_raw, proj, 3, ob, delta_out_norm_w, attn_out_norm_w)
    w_gu_g, w_down_g, w_out_g = _exchange_wait("gather_ffn_wait", ffn_fly, mixed)
    w_down_full = w_down_g.reshape(D_FF, d)
    w_out_full = w_out_g.reshape(2 * GW, d)
    tn = 512
    x1 = _mm("out_proj", mixed, w_out_full, grid=(t // tm, d // tn, 1),
             a_spec=pl.BlockSpec((tm, 2 * GW), lambda i, j, k: (i, 0)),
             b_spec=pl.BlockSpec((2 * GW, tn), lambda i, j, k: (0, j)),
             o_spec=pl.BlockSpec((tm, tn), lambda i, j, k: (i, j)),
             add=x2, add_spec=pl.BlockSpec((tm, tn), lambda i, j, k: (i, j)),
             out_shape=_sds((t, d), F32), ca=1, cb=0, nk=1)
    h2 = _rms_fwd("norm2", x1, ffn_norm_w, ffn_norm_w)
    per = N_DEV // 2
    gu3, act = _gate_up_swiglu("gate_up", h2, w_gu_g)
    tmd, tkd = min(1024, t), D_FF // 2
    y = _mm("down_proj", act, w_down_full, grid=(t // tmd, d // tn, 2),
            a_spec=pl.BlockSpec((tmd, tkd), lambda i, j, k: (i, k)),
            b_spec=pl.BlockSpec((tkd, tn), lambda i, j, k: (k, j)),
            o_spec=pl.BlockSpec((tmd, tn), lambda i, j, k: (i, j)),
            add=x1, add_spec=pl.BlockSpec((tmd, tn), lambda i, j, k: (i, j)),
            out_shape=_sds((t, d), F32), ca=1, cb=0, nk=2)
    dy, loss_tile = _loss_head("loss_head", y, target)
    loss = lax.psum(loss_tile[0, 0], ("x", "y", "c"))

    dy16 = dy.astype(BF16)
    tk = min(2048, t)
    nkt = t // tk
    g_down = _mm("g_down", act, dy16, dep=loss.reshape(1, 1), grid=(D_FF // 512, 1, nkt),
                 a_spec=pl.BlockSpec((tk, 512), lambda i, j, k: (k, i)),
                 b_spec=pl.BlockSpec((tk, d), lambda i, j, k: (k, 0)),
                 o_spec=pl.BlockSpec((512, d), lambda i, j, k: (i, 0)),
                 out_shape=_sds((D_FF, d), F32), ca=0, cb=0, nk=nkt)
    down_g_fly = _exchange_start("reduce_down_start", [g_down.reshape(N_DEV, D_FF // N_DEV, d)], [True], dy16)
    dgu3 = _d_gate_up("d_gate_up", dy16, w_down_full, gu3, down_g_fly["token"])
    g_gu = _mm("g_gate_up", h2, dgu3, grid=(d // 512, N_DEV, nkt),
               a_spec=pl.BlockSpec((tk, 512), lambda i, j, k: (k, i)),
               b_spec=pl.BlockSpec((None, tk, n_gu), lambda i, j, k: (j // per, k, j % per)),
               o_spec=pl.BlockSpec((None, 512, n_gu), lambda i, j, k: (j, i, 0)),
               out_shape=_sds((N_DEV, d, n_gu), F32), ca=0, cb=0, nk=nkt)
    gu_g_fly = _exchange_start("reduce_gate_up_start", [g_gu], [True], dy16)
    tmh, tnh = min(2048, t), 1024
    dh2 = _mm("d_h2", dgu3, w_gu_g, dep=gu_g_fly["token"], grid=(t // tmh, d // tnh, N_DEV),
              a_spec=pl.BlockSpec((None, tmh, n_gu), lambda i, j, k: (k // per, i, k % per)),
              b_spec=pl.BlockSpec((None, tnh, n_gu), lambda i, j, k: (k, j, 0)),
              o_spec=pl.BlockSpec((tmh, tnh), lambda i, j, k: (i, j)),
              out_shape=_sds((t, d), F32), ca=1, cb=1, nk=N_DEV)
    dx1, g_ffn_norm = _rms_bwd("norm2_bwd", x1, ffn_norm_w, dh2, dy)

    dx1_16 = dx1.astype(BF16)
    g_out = _mm("g_out", mixed, dx1_16, grid=((2 * GW) // 512, 1, nkt),
                a_spec=pl.BlockSpec((tk, 512), lambda i, j, k: (k, i)),
                b_spec=pl.BlockSpec((tk, d), lambda i, j, k: (k, 0)),
                o_spec=pl.BlockSpec((512, d), lambda i, j, k: (i, 0)),
                out_shape=_sds((2 * GW, d), F32), ca=0, cb=0, nk=nkt)
    out_g_fly = _exchange_start("reduce_out_start", [g_out.reshape(N_DEV, (2 * GW) // N_DEV, d)], [True], g_ffn_norm)
    dmixed = _mm("d_mixed", dx1_16, w_out_full, dep=out_g_fly["token"], grid=(t // tm, (2 * GW) // tn, 1),
                 a_spec=pl.BlockSpec((tm, d), lambda i, j, k: (i, 0)),
                 b_spec=pl.BlockSpec((tn, d), lambda i, j, k: (j, 0)),
                 o_spec=pl.BlockSpec((tm, tn), lambda i, j, k: (i, j)),
                 out_shape=_sds((t, 2 * GW), F32), ca=1, cb=1, nk=1)
    doa, dz, dob, delta, g_dn, g_an = _mix_bwd("mix_bwd", dmixed, oa_raw, proj, 3, ob,
                                               delta_out_norm_w, attn_out_norm_w, out_g_fly["token"])
    d_aq, d_ak, d_av = _attn_bwd("attn_bwd", aq, ak, proj, 6, dob, lse, delta)
    daq, g_qn = _qk_bwd("attn_q_bwd", proj, 4, q_norm_w, pos_col, inv_row, d_aq)
    dak, g_kn = _qk_bwd("attn_k_bwd", proj, 5, k_norm_w, pos_col, inv_row, d_ak)
    dav = d_av.astype(BF16)

    dvn, dqd, dkd, dw, ddec = _delta_scan_bwd("delta_scan_bwd", doa, w, p, qd, kd, gc_b, vn, s_hist)
    dqn, dkn, dvv, dbeta_b, dg_b = _delta_prep_bwd("delta_prep_bwd", qn, kn, vv, beta_b, gc_b, tinv, u, w, vn,
                                                   doa, dvn, dqd, dkd, dw, ddec)
    dxq, gcw_q = _conv_bwd("conv_q_bwd", proj, conv_w8, dqn, 0, True, HD ** -0.5)
    dxk, gcw_k = _conv_bwd("conv_k_bwd", proj, conv_w8, dkn, 1, True, 1.0)
    dxv, gcw_v = _conv_bwd("conv_v_bwd", proj, conv_w8, dvv, 2, False, 1.0)
    dsmall, g_alog_row, g_dtb_row = _gates_bwd("gates_bwd", proj, small_blk, alog_row, dtb_row, dbeta_b, dg_b)
    dproj = jnp.concatenate([dxq, dxk, dxv, dz, daq, dak, dav, dsmall], axis=1)
    tnc = n_cat // 3
    g_cat = _mm("g_in", h1, dproj, grid=(d // 512, 3, nkt),
                a_spec=pl.BlockSpec((tk, 512), lambda i, j, k: (k, i)),
                b_spec=pl.BlockSpec((tk, tnc), lambda i, j, k: (k, j)),
                o_spec=pl.BlockSpec((512, tnc), lambda i, j, k: (i, j)),
                out_shape=_sds((d, n_cat), F32), ca=0, cb=0, nk=nkt)
    parts = []
    for j in range(N_DEV):
        cols = []
        for lo, hi, start in sorted(segments):
            a, b = max(lo, j * n_in), min(hi, (j + 1) * n_in)
            if a < b:
                cols.append(g_cat[:, start + a - lo:start + b - lo])
        parts.append(cols[0] if len(cols) == 1 else jnp.concatenate(cols, axis=1))
    g_in_parts = jnp.stack(parts).astype(BF16)
    g_conv = jnp.concatenate([gcw_q, gcw_k, gcw_v], axis=1)
    n_cw = conv_w.shape[2]
    g_conv_parts = jnp.transpose(g_conv.reshape(8, N_DEV, n_cw), (1, 0, 2))
    in_g_fly = _exchange_start("reduce_in_start", [g_in_parts, g_conv_parts], [True] * 2, g_dtb_row)
    tkc = n_cat // 3
    dh1 = _mm("d_h1", dproj, w_cat, dep=in_g_fly["token"], grid=(t // tmd, d // tn, 3),
              a_spec=pl.BlockSpec((tmd, tkc), lambda i, j, k: (i, k)),
              b_spec=pl.BlockSpec((tn, tkc), lambda i, j, k: (j, k)),
              o_spec=pl.BlockSpec((tmd, tn), lambda i, j, k: (i, j)),
              out_shape=_sds((t, d), F32), ca=1, cb=1, nk=3)
    grad_x, g_attn_norm = _rms_bwd("norm1_bwd", x2, attn_norm_w, dh1, dx1)

    small_rows = [g_attn_norm.reshape(d // HD, HD), g_ffn_norm.reshape(d // HD, HD), g_dn, g_qn, g_kn, g_an,
                  g_alog_row, g_dtb_row]
    small_pack = _pad_rows(jnp.concatenate(small_rows, axis=0), 40)
    (r_small,) = _exchange("gather_small_grads", [small_pack], [False])

    def pack_small(an, fn, dn, qn_, kn_, aon, al, db):
        rows = [an.reshape(d // HD, HD), fn.reshape(d // HD, HD), dn, qn_, kn_, aon,
                _lane_row(al[0], 8), _lane_row(db[0], 8)]
        return _pad_rows(jnp.concatenate(rows, axis=0), 40)

    def unpack_small(pk):
        nr = d // HD
        return dict(attn_norm_w=pk[:nr].reshape(1, d), ffn_norm_w=pk[nr:2 * nr].reshape(1, d),
                    delta_out_norm_w=pk[2 * nr:2 * nr + 1], q_norm_w=pk[2 * nr + 1:2 * nr + 2],
                    k_norm_w=pk[2 * nr + 2:2 * nr + 3], attn_out_norm_w=pk[2 * nr + 3:2 * nr + 4],
                    a_log=pk[2 * nr + 4:2 * nr + 5, 8:16], dt_bias=pk[2 * nr + 5:2 * nr + 6, 8:16])

    res_small = _adamw("adamw_small", r_small,
                       pack_small(attn_norm_w, ffn_norm_w, delta_out_norm_w, q_norm_w, k_norm_w, attn_out_norm_w, a_log, dt_bias),
                       pack_small(m_attn_norm_w, m_ffn_norm_w, m_delta_out_norm_w, m_q_norm_w, m_k_norm_w, m_attn_out_norm_w, m_a_log, m_dt_bias),
                       pack_small(v_attn_norm_w, v_ffn_norm_w, v_delta_out_norm_w, v_q_norm_w, v_k_norm_w, v_attn_out_norm_w, v_a_log, v_dt_bias))
    small = [unpack_small(a) for a in res_small]
    (r_down,) = _exchange_wait("reduce_down_wait", down_g_fly, res_small[0])
    (r_gu,) = _exchange_wait("reduce_gate_up_wait", gu_g_fly, res_small[0])
    (r_out,) = _exchange_wait("reduce_out_wait", out_g_fly, res_small[0])
    res_gu = [a[None] for a in _adamw("adamw_gate_up", r_gu, w_gate_up[0], m_w_gate_up[0], v_w_gate_up[0])]
    res_down = [a[None] for a in _adamw("adamw_down", r_down, w_down[0], m_w_down[0], v_w_down[0])]
    res_out = [a[None] for a in _adamw("adamw_out", r_out, w_out[0], m_w_out[0], v_w_out[0])]
    done = (res_gu[3][0, :1, :1] + res_down[3][0, :1, :1] + res_out[3][0, :1, :1])
    r_in, r_conv = _exchange_wait("reduce_in_wait", in_g_fly, done)
    res_in = [a[None] for a in _adamw("adamw_in", r_in, w_in[0], m_w_in[0], v_w_in[0])]
    res_conv =[a[None, :4] for a in _adamw("adamw_conv", r_conv, _pad_rows(conv_w[0], 8), _pad_rows(m_conv_w[0], 8),
                                            _pad_rows(v_conv_w[0], 8))]

    outs = [loss, grad_x[None]]
    for i in range(4):
        s = small[i]
        outs += [s["attn_norm_w"], res_in[i], res_conv[i], s["a_log"], s["dt_bias"], s["delta_out_norm_w"],
                 s["q_norm_w"], s["k_norm_w"], s["attn_out_norm_w"], res_out[i], s["ffn_norm_w"], res_gu[i],
                 res_down[i]]
    return tuple(outs)
```

```python
import functools

import numpy as np
import jax
import jax.numpy as jnp
from jax import lax
from jax.experimental import pallas as pl
from jax.experimental.pallas import tpu as pltpu

F32 = jnp.float32
BF16 = jnp.bfloat16

N_DEV = 8
N_HEADS = 8
HD = 128
GW = N_HEADS * HD
CHUNK = 64
PAIR = 2 * CHUNK
SPAN = 128
DILATIONS = (1, 4, 16)
ROPE_THETA = 10000.0
EPS = 1e-6
D_FF = 5632
ADAM_LR, ADAM_B1, ADAM_B2, ADAM_EPS, ADAM_WD, ADAM_STEP = 0.001, 0.9, 0.999, 1e-8, 0.01, 10
NEG = -1e30
VMEM_LIMIT = 56 * 1024 * 1024
ANY = pl.BlockSpec(memory_space=pl.ANY)
HEADS_PER_STEP = 8


def _params(n_grid, vmem=VMEM_LIMIT):
    return pltpu.CompilerParams(dimension_semantics=("arbitrary",) * n_grid, vmem_limit_bytes=vmem)


def _sds(shape, dtype):
    return jax.ShapeDtypeStruct(tuple(shape), dtype)


def _sigmoid(x):
    return 1.0 / (1.0 + jnp.exp(-x))


def _silu(x):
    return x * _sigmoid(x)


def _softplus(x):
    return jnp.maximum(x, 0.0) + jnp.log(1.0 + jnp.exp(-jnp.abs(x)))


def _dot(a, b, ca, cb, precision=None):
    return lax.dot_general(a, b, (((ca,), (cb,)), ((), ())), precision=precision,
                           preferred_element_type=F32)


def _b16(x):
    return x if x.dtype == BF16 else x.astype(BF16)


def _split(x):
    hi = x.astype(BF16)
    return hi, (x - hi.astype(F32)).astype(BF16)


def _dot3(a, b, ca, cb):
    a_hi, a_lo = _split(a)
    b_hi, b_lo = _split(b)
    return _dot(a_hi, b_hi, ca, cb) + (_dot(a_hi, b_lo, ca, cb) + _dot(a_lo, b_hi, ca, cb))


def _iota2(shape, axis):
    return lax.broadcasted_iota(jnp.int32, shape, axis)


def _mm(name, a, b, *, grid, a_spec, b_spec, o_spec, out_shape, ca, cb, nk, add=None, add_spec=None,
        dep=None, vmem=VMEM_LIMIT):
    has_add = add is not None
    n_in = 2 + has_add + (dep is not None)

    def body(*refs):
        a_ref, b_ref = refs[0], refs[1]
        e_ref = refs[2] if has_add else None
        o_ref = refs[n_in]
        part = _dot(_b16(a_ref[...]), _b16(b_ref[...]), ca, cb)
        if nk == 1:
            if has_add:
                part = part + e_ref[...]
            o_ref[...] = part.astype(o_ref.dtype)
            return
        acc = refs[-1]
        k = pl.program_id(2)

        @pl.when(k == 0)
        def _():
            acc[...] = part

        @pl.when(k > 0)
        def _():
            acc[...] += part

        @pl.when(k == nk - 1)
        def _():
            res = acc[...]
            if has_add:
                res = res + e_ref[...]
            o_ref[...] = res.astype(o_ref.dtype)

    in_specs = [a_spec, b_spec] + ([add_spec] if has_add else []) + ([ANY] if dep is not None else [])
    args = (a, b) + ((add,) if has_add else ()) + ((dep,) if dep is not None else ())
    blk = [d for d in o_spec.block_shape if d is not None]
    scratch = [pltpu.VMEM(tuple(blk), F32)] if nk > 1 else []
    return pl.pallas_call(body, grid=grid, in_specs=in_specs, out_specs=o_spec, out_shape=out_shape,
                          scratch_shapes=scratch, name=name, compiler_params=_params(3, vmem))(*args)


def _rms_f(xv, wv):
    return xv * lax.rsqrt(jnp.mean(xv * xv, axis=-1, keepdims=True) + EPS) * wv


def _rms_fwd(name, x, w, dep):
    t, d = x.shape
    tm = min(512, t)

    def body(x_ref, w_ref, dep_ref, o_ref):
        o_ref[...] = _rms_f(x_ref[...], w_ref[...]).astype(BF16)

    row = pl.BlockSpec((tm, d), lambda i: (i, 0))
    vec = pl.BlockSpec((1, d), lambda i: (0, 0))
    return pl.pallas_call(body, grid=(t // tm,), in_specs=[row, vec, ANY], out_specs=row,
                          out_shape=_sds((t, d), BF16), name=name, compiler_params=_params(1))(x, w, dep)


def _rms_bwd(name, x, w, dh, res):
    t, d = x.shape
    tm = min(256, t)

    def body(x_ref, w_ref, dh_ref, res_ref, dx_ref, dx16_ref, dw_ref):
        _, vjp = jax.vjp(_rms_f, x_ref[...], w_ref[...])
        dxv, dwv = vjp(dh_ref[...])
        dxv = dxv + res_ref[...]
        dx_ref[...] = dxv
        dx16_ref[...] = dxv.astype(BF16)

        @pl.when(pl.program_id(0) == 0)
        def _():
            dw_ref[...] = jnp.zeros_like(dw_ref)

        dw_ref[...] += dwv

    row = pl.BlockSpec((tm, d), lambda i: (i, 0))
    vec = pl.BlockSpec((1, d), lambda i: (0, 0))
    return pl.pallas_call(body, grid=(t // tm,), in_specs=[row, vec, row, row], out_specs=[row, row, vec],
                          out_shape=[_sds((t, d), F32), _sds((t, d), BF16), _sds((1, d), F32)], name=name,
                          compiler_params=_params(1))(x, w, dh, res)


def _conv_taps(xv, w_ref, rows):
    c = w_ref[3:4, :] * xv
    for s in (1, 2, 3):
        c = c + w_ref[3 - s:4 - s, :] * jnp.where(rows >= s, pltpu.roll(xv, s, 0), 0.0)
    return c


def _post_conv(c, l2, scale):
    y = _silu(c)
    if l2:
        y = y * lax.rsqrt(jnp.sum(y * y, axis=-1, keepdims=True) + EPS) * scale
    return y


def _conv_fwd(name, proj, conv_w8, group, l2, scale):
    t = proj.shape[0]

    def body(x_ref, w_ref, o_ref):
        rows = _iota2((t, HD), 0)
        o_ref[...] = _post_conv(_conv_taps(x_ref[...], w_ref, rows), l2, scale)

    return pl.pallas_call(
        body, grid=(N_HEADS,),
        in_specs=[pl.BlockSpec((t, HD), lambda h: (0, h + group * N_HEADS)),
                  pl.BlockSpec((8, HD), lambda h: (0, h + group * N_HEADS))],
        out_specs=pl.BlockSpec((t, HD), lambda h: (0, h)),
        out_shape=_sds((t, GW), F32), name=name, compiler_params=_params(1, VMEM_LIMIT))(proj, conv_w8)


def _conv_bwd(name, proj, conv_w8, dn, group, l2, scale):
    t = proj.shape[0]

    def body(x_ref, w_ref, dn_ref, dx_ref, dw_ref):
        rows = _iota2((t, HD), 0)
        xv = x_ref[...]
        c = _conv_taps(xv, w_ref, rows)
        _, vjp = jax.vjp(lambda cc: _post_conv(cc, l2, scale), c)
        (dc,) = vjp(dn_ref[...])
        dx = w_ref[3:4, :] * dc
        dw = jnp.zeros((8, HD), F32)
        rid = _iota2((8, HD), 0)
        dw = dw + jnp.where(rid == 3, jnp.sum(dc * xv, axis=0, keepdims=True), 0.0)
        for s in (1, 2, 3):
            dx = dx + w_ref[3 - s:4 - s, :] * jnp.where(rows < t - s, pltpu.roll(dc, t - s, 0), 0.0)
            xs = jnp.where(rows >= s, pltpu.roll(xv, s, 0), 0.0)
            dw = dw + jnp.where(rid == 3 - s, jnp.sum(dc * xs, axis=0, keepdims=True), 0.0)
        dx_ref[...] = dx.astype(BF16)
        dw_ref[...] = dw

    return pl.pallas_call(
        body, grid=(N_HEADS,),
        in_specs=[pl.BlockSpec((t, HD), lambda h: (0, h + group * N_HEADS)),
                  pl.BlockSpec((8, HD), lambda h: (0, h + group * N_HEADS)),
                  pl.BlockSpec((t, HD), lambda h: (0, h))],
        out_specs=[pl.BlockSpec((t, HD), lambda h: (0, h)), pl.BlockSpec((8, HD), lambda h: (0, h))],
        out_shape=[_sds((t, GW), BF16), _sds((8, GW), F32)], name=name,
        compiler_params=_params(1, VMEM_LIMIT))(proj, conv_w8, dn)


def _chunk_cumsum(g, rows):
    pos = rows % CHUNK
    s = 1
    while s < CHUNK:
        g = g + jnp.where(pos >= s, pltpu.roll(g, s, 0), 0.0)
        s *= 2
    return g


def _gates_fwd(name, proj, small_blk, alog_row, dtb_row):
    t = proj.shape[0]
    tm = min(256, t)

    def body(s_ref, a_ref, b_ref, beta_ref, gc_ref):
        sm = s_ref[...]
        beta = _sigmoid(sm)
        g = -jnp.exp(a_ref[...]) * _softplus(sm + b_ref[...])
        gc = _chunk_cumsum(g, _iota2((tm, HD), 0))
        lane = _iota2((tm, HD), 1)
        for h in range(N_HEADS):
            bcol = jnp.sum(jnp.where(lane == h, beta, 0.0), axis=1, keepdims=True)
            gcol = jnp.sum(jnp.where(lane == 8 + h, gc, 0.0), axis=1, keepdims=True)
            beta_ref[:, h * HD:(h + 1) * HD] = jnp.broadcast_to(bcol, (tm, HD))
            gc_ref[:, h * HD:(h + 1) * HD] = jnp.broadcast_to(gcol, (tm, HD))

    vec = pl.BlockSpec((1, HD), lambda i: (0, 0))
    wide = pl.BlockSpec((tm, GW), lambda i: (i, 0))
    return pl.pallas_call(
        body, grid=(t // tm,),
        in_specs=[pl.BlockSpec((tm, HD), lambda i: (i, small_blk)), vec, vec], out_specs=[wide, wide],
        out_shape=[_sds((t, GW), F32), _sds((t, GW), F32)], name=name,
        compiler_params=_params(1))(proj, alog_row, dtb_row)


def _gates_bwd(name, proj, small_blk, alog_row, dtb_row, dbeta_b, dg_b):
    t = proj.shape[0]
    tm = min(256, t)

    def body(s_ref, a_ref, b_ref, db_ref, dg_ref, ds_ref, da_ref, dbias_ref):
        sm = s_ref[...]
        lane = _iota2((tm, HD), 1)
        db = jnp.zeros((tm, HD), F32)
        dg = jnp.zeros((tm, HD), F32)
        for h in range(N_HEADS):
            db = db + jnp.where(lane == h, db_ref[:, h * HD:(h + 1) * HD], 0.0)
            dg = dg + jnp.where(lane == 8 + h, dg_ref[:, h * HD:(h + 1) * HD], 0.0)
        beta = _sigmoid(sm)
        ea = jnp.exp(a_ref[...])
        pre = sm + b_ref[...]
        g = -ea * _softplus(pre)
        dpre = dg * (-ea) * _sigmoid(pre)
        ds_ref[...] = (db * beta * (1.0 - beta) + dpre).astype(BF16)

        @pl.when(pl.program_id(0) == 0)
        def _():
            da_ref[...] = jnp.zeros_like(da_ref)
            dbias_ref[...] = jnp.zeros_like(dbias_ref)

        da_ref[...] += jnp.sum(dg * g, axis=0, keepdims=True)
        dbias_ref[...] += jnp.sum(dpre, axis=0, keepdims=True)

    vec = pl.BlockSpec((1, HD), lambda i: (0, 0))
    wide = pl.BlockSpec((tm, GW), lambda i: (i, 0))
    return pl.pallas_call(
        body, grid=(t // tm,),
        in_specs=[pl.BlockSpec((tm, HD), lambda i: (i, small_blk)), vec, vec, wide, wide],
        out_specs=[pl.BlockSpec((tm, HD), lambda i: (i, 0)), vec, vec],
        out_shape=[_sds((t, HD), BF16), _sds((1, HD), F32), _sds((1, HD), F32)], name=name,
        compiler_params=_params(1))(proj, alog_row, dtb_row, dbeta_b, dg_b)


def _pair_masks():
    ii = _iota2((PAIR, PAIR), 0)
    jj = _iota2((PAIR, PAIR), 1)
    same = (ii // CHUNK) == (jj // CHUNK)
    return ii, jj, same & (ii >= jj), same & (ii > jj)


def _to_row(col_b, ii, jj):
    return jnp.sum(jnp.where(ii == jj, col_b, 0.0), axis=0, keepdims=True)


def _to_col(row, ii, jj):
    return jnp.sum(jnp.where(ii == jj, jnp.broadcast_to(row, (PAIR, PAIR)), 0.0), axis=1, keepdims=True)


def _decay_parts(gc, last_a, last_b, ii, jj, causal):
    diff = gc - _to_row(gc, ii, jj)
    dmat = jnp.where(causal, jnp.exp(jnp.where(causal, diff, 0.0)), 0.0)
    glast = jnp.where(ii < CHUNK, last_a, last_b)
    return dmat, jnp.exp(gc), jnp.exp(glast - gc)


def _unit_lower_inverse(lows, ii, jj):
    eye = jnp.where(ii == jj, 1.0, 0.0)
    mm = lambda xs, ys: [_dot3(a, b, 1, 0) for a, b in zip(xs, ys)]
    plus = lambda xs: [eye + a for a in xs]
    minus = lambda xs: [eye - a for a in xs]
    d1 = [jnp.where((ii // 16) == (jj // 16), low, 0.0) for low in lows]
    d2 = mm(d1, d1)
    a = mm(minus(d1), plus(d2))
    d4 = mm(d2, d2)
    a = mm(a, plus(d4))
    d8 = mm(d4, d4)
    td = mm(a, plus(d8))
    n1 = mm(td, [low - d for low, d in zip(lows, d1)])
    n2 = mm(n1, n1)
    return mm(mm(minus(n1), plus(n2)), td)


def _delta_prep(name, qn, kn, vv, beta_b, gc_b):
    t = qn.shape[0]

    def body(q_ref, k_ref, v_ref, b_ref, g_ref, u_ref, w_ref, p_ref, t_ref, qd_ref, kd_ref):
        ii, jj, causal, strict = _pair_masks()
        sls = [slice(hh * HD, (hh + 1) * HD) for hh in range(HEADS_PER_STEP)]
        lows = []
        for sl in sls:
            q, k, beta = q_ref[:, sl], k_ref[:, sl], b_ref[:, sl]
            dmat, gam, e2 = _decay_parts(g_ref[:, sl], g_ref[CHUNK - 1:CHUNK, sl], g_ref[PAIR - 1:PAIR, sl],
                                         ii, jj, causal)
            k16 = _b16(k)
            lows.append(jnp.where(strict, beta * _dot(k16, k16, 1, 1) * dmat, 0.0))
            p_ref[:, sl] = jnp.where(causal, _dot(_b16(q), k16, 1, 1) * dmat, 0.0).astype(BF16)
            qd_ref[:, sl] = (q * gam).astype(BF16)
            kd_ref[:, sl] = (k * e2).astype(BF16)
        for sl, tinv in zip(sls, _unit_lower_inverse(lows, ii, jj)):
            beta = b_ref[:, sl]
            t_ref[:, sl] = tinv
            u_ref[:, sl] = _dot3(tinv, v_ref[:, sl] * beta, 1, 0)
            w_ref[:, sl] = _dot3(tinv, k_ref[:, sl] * (beta * jnp.exp(g_ref[:, sl])), 1, 0).astype(BF16)

    blk = pl.BlockSpec((PAIR, HEADS_PER_STEP * HD), lambda i, h: (i, h))
    return pl.pallas_call(
        body, grid=(t // PAIR, N_HEADS // HEADS_PER_STEP), in_specs=[blk] * 5, out_specs=[blk] * 6,
        out_shape=[_sds((t, GW), F32), _sds((t, GW), BF16), _sds((t, GW), BF16), _sds((t, GW), F32),
                   _sds((t, GW), BF16), _sds((t, GW), BF16)],
        name=name, compiler_params=_params(2))(qn, kn, vv, beta_b, gc_b)


def _delta_scan(name, u, w, p, qd, kd, gc_b):
    t = u.shape[0]
    n = t // CHUNK

    def body(u_ref, w_ref, p_ref, qd_ref, kd_ref, g_ref, o_ref, vn_ref, sh_ref, state):
        @pl.when(pl.program_id(0) == 0)
        def _():
            state[...] = jnp.zeros_like(state)

        sls = [slice(h * HD, (h + 1) * HD) for h in range(N_HEADS)]
        heads = range(N_HEADS)
        s = [state[h] for h in heads]
        for h in heads:
            sh_ref[h] = s[h]
        s16 = [_b16(a) for a in s]
        ws = [_dot(w_ref[:, sls[h]], s16[h], 1, 0) for h in heads]
        qs = [_dot(qd_ref[:, sls[h]], s16[h], 1, 0) for h in heads]
        vn16 = [_b16(u_ref[:, sls[h]] - ws[h]) for h in heads]
        pv = [_dot(p_ref[:, sls[h]], jnp.concatenate([vn16[h], vn16[h]], axis=0), 1, 0) for h in heads]
        kv = [_dot(kd_ref[:, sls[h]], vn16[h], 0, 0) for h in heads]
        for h in heads:
            o_ref[:, sls[h]] = qs[h] + pv[h]
            vn_ref[:, sls[h]] = vn16[h]
            state[h] = s[h] * jnp.exp(g_ref[CHUNK - 1:CHUNK, sls[h]]) + kv[h]

    blk = pl.BlockSpec((CHUNK, GW), lambda i: (i, 0))
    return pl.pallas_call(
        body, grid=(n,), in_specs=[blk] * 6,
        out_specs=[blk, blk, pl.BlockSpec((None, N_HEADS, HD, HD), lambda i: (i, 0, 0, 0))],
        out_shape=[_sds((t, GW), F32), _sds((t, GW), BF16), _sds((n, N_HEADS, HD, HD), F32)],
        scratch_shapes=[pltpu.VMEM((N_HEADS, HD, HD), F32)], name=name,
        compiler_params=_params(1))(u, w, p, qd, kd, gc_b)


def _delta_scan_bwd(name, do, w, p, qd, kd, gc_b, vn, s_hist):
    t = do.shape[0]
    n = t // CHUNK

    def body(do_ref, w_ref, p_ref, qd_ref, kd_ref, g_ref, vn_ref, sh_ref,
             dvn_ref, dqd_ref, dkd_ref, dw_ref, ddec_ref, dstate):
        @pl.when(pl.program_id(0) == 0)
        def _():
            dstate[...] = jnp.zeros_like(dstate)

        sls = [slice(h * HD, (h + 1) * HD) for h in range(N_HEADS)]
        heads = range(N_HEADS)
        ds = [dstate[h] for h in heads]
        ds16 = [_b16(a) for a in ds]
        s16 = [_b16(sh_ref[h]) for h in heads]
        do16 = [_b16(do_ref[:, sls[h]]) for h in heads]
        ptdo = [_dot(p_ref[:, sls[h]], do16[h], 0, 0) for h in heads]
        kds = [_dot(kd_ref[:, sls[h]], ds16[h], 1, 0) for h in heads]
        qdo = [_dot(qd_ref[:, sls[h]], do16[h], 0, 0) for h in heads]
        for h in heads:
            dqd_ref[:, sls[h]] = _dot(do16[h], s16[h], 1, 1)
            dkd_ref[:, sls[h]] = _dot(vn_ref[:, sls[h]], ds16[h], 1, 1)
        dvn = [ptdo[h][:CHUNK, :] + ptdo[h][CHUNK:, :] + kds[h] for h in heads]
        dvn16 = [_b16(a) for a in dvn]
        wdv = [_dot(w_ref[:, sls[h]], dvn16[h], 0, 0) for h in heads]
        for h in heads:
            dvn_ref[:, sls[h]] = dvn[h]
            dw_ref[:, sls[h]] = -_dot(dvn16[h], s16[h], 1, 1)
            tot = jnp.sum(jnp.sum(sh_ref[h] * ds[h], axis=1, keepdims=True), axis=0, keepdims=True)
            ddec_ref[:, sls[h]] = jnp.broadcast_to(tot, (8, HD))
            dstate[h] = ds[h] * jnp.exp(g_ref[CHUNK - 1:CHUNK, sls[h]]) + qdo[h] - wdv[h]

    blk = pl.BlockSpec((CHUNK, GW), lambda i: (n - 1 - i, 0))
    return pl.pallas_call(
        body, grid=(n,),
        in_specs=[blk] * 7 + [pl.BlockSpec((None, N_HEADS, HD, HD), lambda i: (n - 1 - i, 0, 0, 0))],
        out_specs=[blk] * 4 + [pl.BlockSpec((8, GW), lambda i: (n - 1 - i, 0))],
        out_shape=[_sds((t, GW), F32)] * 4 + [_sds((n * 8, GW), F32)],
        scratch_shapes=[pltpu.VMEM((N_HEADS, HD, HD), F32)], name=name,
        compiler_params=_params(1))(do, w, p, qd, kd, gc_b, vn, s_hist)


def _delta_prep_bwd(name, qn, kn, vv, beta_b, gc_b, tinv, u, w, vn, do, dvn, dqd, dkd, dw, ddec):
    t = qn.shape[0]

    def body(q_ref, k_ref, v_ref, b_ref, g_ref, t_ref, u_ref, w_ref, vn_ref, do_ref, dvn_ref, dqd_ref,
             dkd_ref, dw_ref, ddec_ref, dq_ref, dk_ref, dv_ref, dbeta_ref, dg_ref):
        ii, jj, causal, strict = _pair_masks()
        suffix = ((ii // CHUNK) == (jj // CHUNK)) & (jj >= ii)
        first = ii < CHUNK
        rs = lambda a: jnp.sum(a, axis=1, keepdims=True)
        for hh in range(HEADS_PER_STEP):
            sl = slice(hh * HD, (hh + 1) * HD)
            q, k, v, beta, gc = q_ref[:, sl], k_ref[:, sl], v_ref[:, sl], b_ref[:, sl], g_ref[:, sl]
            last_a, last_b = g_ref[CHUNK - 1:CHUNK, sl], g_ref[PAIR - 1:PAIR, sl]
            dmat, gam, e2 = _decay_parts(gc, last_a, last_b, ii, jj, causal)
            q16, k16 = _b16(q), _b16(k)
            kk = _dot(k16, k16, 1, 1)
            qk = _dot(q16, k16, 1, 1)
            dqd, dkd = dqd_ref[:, sl], dkd_ref[:, sl]
            dp = jnp.where(causal, _dot(_b16(do_ref[:, sl]), vn_ref[:, sl], 1, 1), 0.0)
            dpd16 = _b16(dp * dmat)
            tinv_v = t_ref[:, sl]
            x = _dot3(tinv_v, dvn_ref[:, sl], 0, 0)
            y = _dot3(tinv_v, dw_ref[:, sl], 0, 0)
            da = -jnp.where(strict, _dot(_b16(x), _b16(u_ref[:, sl]), 1, 1) + _dot(_b16(y), w_ref[:, sl], 1, 1), 0.0)
            dkk16 = _b16(da * beta * dmat)
            dq_ref[:, sl] = gam * dqd + _dot(dpd16, k16, 1, 0)
            dk_ref[:, sl] = (e2 * dkd + _dot(dpd16, q16, 0, 0) + beta * gam * y
                             + _dot(dkk16, k16, 1, 0) + _dot(dkk16, k16, 0, 0))
            dv_ref[:, sl] = beta * x
            dbeta = rs(v * x) + rs(k * gam * y) + rs(da * kk * dmat)
            dbeta_ref[:, sl] = jnp.broadcast_to(dbeta, (PAIR, HD))
            m = (dp * qk + da * beta * kk) * dmat
            dgam = rs(q * dqd) + rs(k * beta * y)
            de2 = rs(k * dkd)
            colsum = _to_col(jnp.sum(m, axis=0, keepdims=True), ii, jj)
            te2 = de2 * e2
            dgc = rs(m) - colsum + gam * dgam - te2
            tail_a = jnp.sum(jnp.where(first, te2, 0.0), axis=0, keepdims=True)
            tail_b = jnp.sum(jnp.where(first, 0.0, te2), axis=0, keepdims=True)
            dgc = dgc + jnp.where(ii == CHUNK - 1, tail_a + ddec_ref[0:1, sl] * jnp.exp(last_a), 0.0)
            dgc = dgc + jnp.where(ii == PAIR - 1, tail_b + ddec_ref[8:9, sl] * jnp.exp(last_b), 0.0)
            dgc_row = _to_row(dgc, ii, jj)
            dg = jnp.sum(jnp.where(suffix, jnp.broadcast_to(dgc_row, (PAIR, PAIR)), 0.0), axis=1, keepdims=True)
            dg_ref[:, sl] = jnp.broadcast_to(dg, (PAIR, HD))

    blk = pl.BlockSpec((PAIR, HEADS_PER_STEP * HD), lambda i, h: (i, h))
    return pl.pallas_call(
        body, grid=(t // PAIR, N_HEADS // HEADS_PER_STEP),
        in_specs=[blk] * 14 + [pl.BlockSpec((16, HEADS_PER_STEP * HD), lambda i, h: (i, h))], out_specs=[blk] * 5,
        out_shape=[_sds((t, GW), F32)] * 5, name=name,
        compiler_params=_params(2))(qn, kn, vv, beta_b, gc_b, tinv, u, w, vn, do, dvn, dqd, dkd, dw, ddec)


def _rope_tables(pos_col, inv_row):
    ang = pos_col.astype(F32) * inv_row
    lane = _iota2(ang.shape, 1)
    return jnp.cos(ang), jnp.where(lane < HD // 2, -1.0, 1.0) * jnp.sin(ang)


def _head_rms(xh, wv):
    return xh * lax.rsqrt(jnp.mean(xh * xh, axis=-1, keepdims=True) + EPS) * wv


def _qk_fwd(name, proj, blk_idx, w_row, pos_col, inv_row):
    t = proj.shape[0]
    tm = min(256, t)

    def body(x_ref, w_ref, pos_ref, inv_ref, o_ref):
        cos, sin = _rope_tables(pos_ref[...], inv_ref[...])
        for h in range(N_HEADS):
            y = _head_rms(x_ref[:, h * HD:(h + 1) * HD], w_ref[...])
            o_ref[:, h * HD:(h + 1) * HD] = y * cos + pltpu.roll(y, HD // 2, 1) * sin

    vec = pl.BlockSpec((1, HD), lambda i: (0, 0))
    return pl.pallas_call(
        body, grid=(t // tm,),
        in_specs=[pl.BlockSpec((tm, GW), lambda i: (i, blk_idx)), vec, pl.BlockSpec((tm, 1), lambda i: (i, 0)), vec],
        out_specs=pl.BlockSpec((tm, GW), lambda i: (i, 0)), out_shape=_sds((t, GW), F32), name=name,
        compiler_params=_params(1))(proj, w_row, pos_col, inv_row)


def _qk_bwd(name, proj, blk_idx, w_row, pos_col, inv_row, dy_full):
    t = proj.shape[0]
    tm = min(256, t)

    def body(x_ref, w_ref, pos_ref, inv_ref, dy_ref, dx_ref, dw_ref):
        cos, sin = _rope_tables(pos_ref[...], inv_ref[...])
        dw = jnp.zeros((1, HD), F32)
        for h in range(N_HEADS):
            sl = slice(h * HD, (h + 1) * HD)
            dy = dy_ref[:, sl]
            dy = dy * cos - pltpu.roll(dy, HD // 2, 1) * sin
            _, vjp = jax.vjp(_head_rms, x_ref[:, sl], w_ref[...])
            dx, dwh = vjp(dy)
            dw = dw + dwh
            dx_ref[:, sl] = dx.astype(BF16)

        @pl.when(pl.program_id(0) == 0)
        def _():
            dw_ref[...] = jnp.zeros_like(dw_ref)

        dw_ref[...] += dw

    vec = pl.BlockSpec((1, HD), lambda i: (0, 0))
    wide = pl.BlockSpec((tm, GW), lambda i: (i, 0))
    return pl.pallas_call(
        body, grid=(t // tm,),
        in_specs=[pl.BlockSpec((tm, GW), lambda i: (i, blk_idx)), vec, pl.BlockSpec((tm, 1), lambda i: (i, 0)), vec,
                  wide],
        out_specs=[wide, vec], out_shape=[_sds((t, GW), BF16), _sds((1, HD), F32)], name=name,
        compiler_params=_params(1))(proj, w_row, pos_col, inv_row, dy_full)


GROUP = SPAN * max(DILATIONS)
SCALE = HD ** -0.5


def _band_mask(lo):
    qi = _iota2((SPAN, 2 * SPAN), 0)
    ki = _iota2((SPAN, 2 * SPAN), 1)
    return (ki >= qi) & (ki <= qi + SPAN) & (ki >= lo)


def _tiles():
    return [(pi, r, u, rho) for pi, r in enumerate(DILATIONS) for u in range(GROUP // (SPAN * r)) for rho in range(r)]


def _rows(r, u, rho):
    return pl.ds(u * SPAN * r + rho, SPAN, stride=r) if r > 1 else pl.ds(u * SPAN, SPAN)


def _attn_fwd(name, q, k, v, v_blk):
    t = q.shape[0]

    def body(qc_ref, kc_ref, vc_ref, kp_ref, vp_ref, ob_ref, lse_ref, o_scr, l_scr):
        mask_in = _band_mask(0)
        mask_edge = _band_mask(jnp.where(pl.program_id(0) == 0, SPAN, 0))
        for pi, r, u, rho in _tiles():
            rows = _rows(r, u, rho)
            if u > 0:
                prows, kp_src, vp_src, mask = _rows(r, u - 1, rho), kc_ref, vc_ref, mask_in
            else:
                prows, kp_src, vp_src, mask = _rows(r, GROUP // (SPAN * r) - 1, rho), kp_ref, vp_ref, mask_edge
            kcat = jnp.concatenate([kp_src[prows, :], kc_ref[rows, :]], axis=0).astype(BF16)
            vcat = jnp.concatenate([vp_src[prows, :], vc_ref[rows, :]], axis=0).astype(BF16)
            s = jnp.where(mask, _dot(qc_ref[rows, :].astype(BF16), kcat, 1, 1) * SCALE, NEG)
            m = jnp.max(s, axis=1, keepdims=True)
            p = jnp.exp(s - m)
            den = jnp.sum(p, axis=1, keepdims=True)
            o_scr[pi, rows, :] = _dot(_b16(p), vcat, 1, 0) / den
            l_scr[pi, rows, :] = jnp.broadcast_to(m + jnp.log(den), (SPAN, HD))
        step = 256
        for c in range(GROUP // step):
            sl = pl.ds(c * step, step)
            ob, lse = _merge([o_scr[i, sl, :] for i in range(3)], [l_scr[i, sl, :] for i in range(3)])
            ob_ref[sl, :] = ob
            lse_ref[sl, :] = lse

    cur = pl.BlockSpec((GROUP, HD), lambda g, h: (g, h))
    prev = pl.BlockSpec((GROUP, HD), lambda g, h: (jnp.maximum(g - 1, 0), h))
    vcur = pl.BlockSpec((GROUP, HD), lambda g, h: (g, v_blk * N_HEADS + h))
    vprev = pl.BlockSpec((GROUP, HD), lambda g, h: (jnp.maximum(g - 1, 0), v_blk * N_HEADS + h))
    return pl.pallas_call(
        body, grid=(t // GROUP, N_HEADS), in_specs=[cur, cur, vcur, prev, vprev], out_specs=[cur, cur],
        out_shape=[_sds((t, GW), F32), _sds((t, GW), F32)],
        scratch_shapes=[pltpu.VMEM((3, GROUP, HD), F32), pltpu.VMEM((3, GROUP, HD), F32)], name=name,
        compiler_params=_params(2))(q, k, v, k, v)


def _attn_bwd(name, q, k, v, v_blk, do, lse, delta):
    t = q.shape[0]
    ng = t // GROUP

    def pair(qt, dot, lt, dlt, kcat, vcat, mask):
        wide = kcat.shape[0] // SPAN
        lw = jnp.concatenate([lt] * wide, axis=1) if wide > 1 else lt
        dw = jnp.concatenate([dlt] * wide, axis=1) if wide > 1 else dlt
        s = _dot(qt, kcat, 1, 1) * SCALE
        p = jnp.where(mask, jnp.exp(jnp.where(mask, s - lw, 0.0)), 0.0)
        ds = p * (_dot(dot, vcat, 1, 1) - dw) * SCALE
        return _b16(ds), _b16(p)

    def body(qc_ref, kc_ref, vc_ref, doc_ref, lc_ref, dc_ref, kp_ref, vp_ref, qn_ref, don_ref, ln_ref, dn_ref,
             dq_ref, dk_ref, dv_ref):
        g = pl.program_id(0)
        mask_in = _band_mask(0)
        mask_edge = _band_mask(jnp.where(g == 0, SPAN, 0))
        dk_ref[...] = jnp.zeros_like(dk_ref)
        dv_ref[...] = jnp.zeros_like(dv_ref)
        for pi, r, u, rho in _tiles():
            rows = _rows(r, u, rho)
            if u > 0:
                prows, kp_src, vp_src, mask = _rows(r, u - 1, rho), kc_ref, vc_ref, mask_in
            else:
                prows, kp_src, vp_src, mask = _rows(r, GROUP // (SPAN * r) - 1, rho), kp_ref, vp_ref, mask_edge
            kcat = jnp.concatenate([kp_src[prows, :], kc_ref[rows, :]], axis=0).astype(BF16)
            vcat = jnp.concatenate([vp_src[prows, :], vc_ref[rows, :]], axis=0).astype(BF16)
            qt, dot = qc_ref[rows, :].astype(BF16), doc_ref[rows, :].astype(BF16)
            ds, p = pair(qt, dot, lc_ref[rows, :], dc_ref[rows, :], kcat, vcat, mask)
            dq_t = _dot(ds, kcat, 1, 0)
            if pi == 0:
                dq_ref[rows, :] = dq_t
            else:
                dq_ref[rows, :] += dq_t
            dk2 = _dot(ds, qt, 0, 0)
            dv2 = _dot(p, dot, 0, 0)
            dk_ref[rows, :] += dk2[SPAN:, :]
            dv_ref[rows, :] += dv2[SPAN:, :]
            if u > 0:
                dk_ref[prows, :] += dk2[:SPAN, :]
                dv_ref[prows, :] += dv2[:SPAN, :]
        qi = _iota2((SPAN, SPAN), 0)
        ki = _iota2((SPAN, SPAN), 1)
        mask_next = (ki >= qi) & (ki < jnp.where(g == ng - 1, 0, SPAN))
        for r in DILATIONS:
            for rho in range(r):
                krows, qrows = _rows(r, GROUP // (SPAN * r) - 1, rho), _rows(r, 0, rho)
                qt, dot = qn_ref[qrows, :].astype(BF16), don_ref[qrows, :].astype(BF16)
                ds, p = pair(qt, dot, ln_ref[qrows, :], dn_ref[qrows, :], kc_ref[krows, :].astype(BF16),
                             vc_ref[krows, :].astype(BF16), mask_next)
                dk_ref[krows, :] += _dot(ds, qt, 0, 0)
                dv_ref[krows, :] += _dot(p, dot, 0, 0)

    cur = pl.BlockSpec((GROUP, HD), lambda g, h: (g, h))
    prev = pl.BlockSpec((GROUP, HD), lambda g, h: (jnp.maximum(g - 1, 0), h))
    nxt = pl.BlockSpec((GROUP, HD), lambda g, h: (jnp.minimum(g + 1, ng - 1), h))
    vcur = pl.BlockSpec((GROUP, HD), lambda g, h: (g, v_blk * N_HEADS + h))
    vprev = pl.BlockSpec((GROUP, HD), lambda g, h: (jnp.maximum(g - 1, 0), v_blk * N_HEADS + h))
    return pl.pallas_call(
        body, grid=(ng, N_HEADS), in_specs=[cur, cur, vcur, cur, cur, cur, prev, vprev] + [nxt] * 4,
        out_specs=[cur] * 3,
        out_shape=[_sds((t, GW), F32)] * 3, name=name,
        compiler_params=_params(2))(q, k, v, do, lse, delta, k, v, q, do, lse, delta)


def _merge(os_, ls_):
    m = jnp.maximum(jnp.maximum(ls_[0], ls_[1]), ls_[2])
    ws = [jnp.exp(l - m) for l in ls_]
    tot = ws[0] + ws[1] + ws[2]
    ob = (ws[0] * os_[0] + ws[1] * os_[1] + ws[2] * os_[2]) / tot
    return ob, m + jnp.log(tot)


def _gated_norm(oa, z, wv):
    return _head_rms(oa, wv) * _silu(z)


def _mix_fwd(name, oa_raw, proj, z_blk, ob, w_dn, w_an):
    t = oa_raw.shape[0]
    tm = min(256, t)

    def body(oa_ref, z_ref, ob_ref, wd_ref, wa_ref, mix_ref):
        for h in range(N_HEADS):
            sl = slice(h * HD, (h + 1) * HD)
            mix_ref[:, sl] = _gated_norm(oa_ref[:, sl], z_ref[:, sl], wd_ref[...]).astype(BF16)
            mix_ref[:, GW + h * HD:GW + (h + 1) * HD] = _head_rms(ob_ref[:, sl], wa_ref[...]).astype(BF16)

    vec = pl.BlockSpec((1, HD), lambda i: (0, 0))
    wide = pl.BlockSpec((tm, GW), lambda i: (i, 0))
    return pl.pallas_call(
        body, grid=(t // tm,),
        in_specs=[wide, pl.BlockSpec((tm, GW), lambda i: (i, z_blk)), wide, vec, vec],
        out_specs=pl.BlockSpec((tm, 2 * GW), lambda i: (i, 0)),
        out_shape=_sds((t, 2 * GW), BF16), name=name,
        compiler_params=_params(1))(oa_raw, proj, ob, w_dn, w_an)


def _mix_bwd(name, dmixed, oa_raw, proj, z_blk, ob, w_dn, w_an, dep):
    t = oa_raw.shape[0]
    tm = min(256, t)

    def body(dm_ref, oa_ref, z_ref, ob_ref, wd_ref, wa_ref, dep_ref,
             doa_ref, dz_ref, dob_ref, dl_ref, dwd_ref, dwa_ref):
        dwd = jnp.zeros((1, HD), F32)
        dwa = jnp.zeros((1, HD), F32)
        for h in range(N_HEADS):
            sl = slice(h * HD, (h + 1) * HD)
            _, vjp = jax.vjp(_gated_norm, oa_ref[:, sl], z_ref[:, sl], wd_ref[...])
            doa, dz, dw1 = vjp(dm_ref[:, sl])
            doa_ref[:, sl] = doa
            dz_ref[:, sl] = dz.astype(BF16)
            dwd = dwd + dw1
            obh = ob_ref[:, sl]
            _, vjp2 = jax.vjp(_head_rms, obh, wa_ref[...])
            dob, dw2 = vjp2(dm_ref[:, GW + h * HD:GW + (h + 1) * HD])
            dwa = dwa + dw2
            dob_ref[:, sl] = dob
            dl_ref[:, sl] = jnp.broadcast_to(jnp.sum(dob * obh, axis=1, keepdims=True), (tm, HD))

        @pl.when(pl.program_id(0) == 0)
        def _():
            dwd_ref[...] = jnp.zeros_like(dwd_ref)
            dwa_ref[...] = jnp.zeros_like(dwa_ref)

        dwd_ref[...] += dwd
        dwa_ref[...] += dwa

    vec = pl.BlockSpec((1, HD), lambda i: (0, 0))
    wide = pl.BlockSpec((tm, GW), lambda i: (i, 0))
    return pl.pallas_call(
        body, grid=(t // tm,),
        in_specs=[pl.BlockSpec((tm, 2 * GW), lambda i: (i, 0)), wide, pl.BlockSpec((tm, GW), lambda i: (i, z_blk)),
                  wide, vec, vec, ANY],
        out_specs=[wide, wide, wide, wide, vec, vec],
        out_shape=[_sds((t, GW), F32), _sds((t, GW), BF16), _sds((t, GW), F32), _sds((t, GW), F32),
                   _sds((1, HD), F32), _sds((1, HD), F32)], name=name,
        compiler_params=_params(1))(dmixed, oa_raw, proj, ob, w_dn, w_an, dep)


def _gate_up_swiglu(name, h2, w_gu_g):
    t, d = h2.shape
    n = w_gu_g.shape[2]
    per = N_DEV // 2
    tm = min(512, t)

    def body(a_ref, bg_ref, bu_ref, gu_ref, act_ref):
        a = a_ref[...]
        g = _dot(a, bg_ref[...], 1, 0)
        up = _dot(a, bu_ref[...], 1, 0)
        gu_ref[0] = g
        gu_ref[1] = up
        act_ref[...] = (_silu(g) * up).astype(BF16)

    return pl.pallas_call(
        body, grid=(per, t // tm),
        in_specs=[pl.BlockSpec((tm, d), lambda j, i: (i, 0)), pl.BlockSpec((None, d, n), lambda j, i: (j, 0, 0)),
                  pl.BlockSpec((None, d, n), lambda j, i: (j + per, 0, 0))],
        out_specs=[pl.BlockSpec((2, tm, n), lambda j, i: (0, i, j)), pl.BlockSpec((tm, n), lambda j, i: (i, j))],
        out_shape=[_sds((2, t, per * n), F32), _sds((t, per * n), BF16)], name=name,
        compiler_params=_params(2))(h2, w_gu_g, w_gu_g)


def _d_gate_up(name, dy16, w_down, gu3, dep):
    t, d = dy16.shape
    f = w_down.shape[0]
    tm, tn = min(512, t), f // 4

    def body(a_ref, b_ref, g_ref, dep_ref, o_ref):
        dact = _dot(a_ref[...], b_ref[...], 1, 1)
        g, up = g_ref[0], g_ref[1]
        sg = _sigmoid(g)
        o_ref[0] = (dact * up * sg * (1.0 + g * (1.0 - sg))).astype(BF16)
        o_ref[1] = (dact * g * sg).astype(BF16)

    return pl.pallas_call(
        body, grid=(f // tn, t // tm),
        in_specs=[pl.BlockSpec((tm, d), lambda j, i: (i, 0)), pl.BlockSpec((tn, d), lambda j, i: (j, 0)),
                  pl.BlockSpec((2, tm, tn), lambda j, i: (0, i, j)), ANY],
        out_specs=pl.BlockSpec((2, tm, tn), lambda j, i: (0, i, j)), out_shape=_sds((2, t, f), BF16), name=name,
        compiler_params=_params(2))(dy16, w_down, gu3, dep)


def _loss_head(name, y, target):
    t, d = y.shape
    tm = min(512, t)

    def body(y_ref, t_ref, dy_ref, dy16_ref, l_ref):
        diff = y_ref[...] - t_ref[...]
        dy_ref[...] = diff * (1.0 / d)
        dy16_ref[...] = (diff * (1.0 / d)).astype(BF16)
        part = jnp.sum(jnp.sum(diff * diff, axis=1, keepdims=True), axis=0, keepdims=True) * (0.5 / d)

        @pl.when(pl.program_id(0) == 0)
        def _():
            l_ref[...] = jnp.zeros_like(l_ref)

        l_ref[...] += jnp.broadcast_to(part, (8, 128))

    row = pl.BlockSpec((tm, d), lambda i: (i, 0))
    return pl.pallas_call(body, grid=(t // tm,), in_specs=[row, row],
                          out_specs=[row, row, pl.BlockSpec((8, 128), lambda i: (0, 0))],
                          out_shape=[_sds((t, d), F32), _sds((t, d), BF16), _sds((8, 128), F32)], name=name,
                          compiler_params=_params(1))(y, target)


def _peer(me, k):
    pid = (me + k) % N_DEV
    return (pid // 4, (pid // 2) % 2, pid % 2)


def _my_id():
    return 4 * lax.axis_index("x") + 2 * lax.axis_index("y") + lax.axis_index("c")


def _exchange(name, arrays, scatter, dep):
    n = len(arrays)

    def body(*refs):
        ins, outs = refs[:n], refs[n + 1:2 * n + 1]
        send_sems, recv_sems, local_sems = refs[2 * n + 1:]
        me = _my_id()
        started = []
        for a in range(n):
            src = ins[a].at[me] if scatter[a] else ins[a]
            loc = pltpu.make_async_copy(src, outs[a].at[me], local_sems.at[a])
            loc.start()
            started.append(loc)
        remote = []
        for k in range(1, N_DEV):
            to = (me + k) % N_DEV
            for a in range(n):
                src = ins[a].at[to] if scatter[a] else ins[a]
                cp = pltpu.make_async_remote_copy(src_ref=src, dst_ref=outs[a].at[me],
                                                  send_sem=send_sems.at[a * (N_DEV - 1) + k - 1], recv_sem=recv_sems.at[a * (N_DEV - 1) + k - 1],
                                                  device_id=_peer(me, k), device_id_type=pl.DeviceIdType.MESH)
                cp.start()
                remote.append(cp)
        for k in range(1, N_DEV):
            frm = (me + N_DEV - k) % N_DEV
            for a in range(n):
                src = ins[a].at[frm] if scatter[a] else ins[a]
                pltpu.make_async_remote_copy(src_ref=src, dst_ref=outs[a].at[frm],
                                             send_sem=send_sems.at[a * (N_DEV - 1) + k - 1], recv_sem=recv_sems.at[a * (N_DEV - 1) + k - 1],
                                             device_id=_peer(me, k), device_id_type=pl.DeviceIdType.MESH).wait_recv()
        for cp in remote:
            cp.wait_send()
        for loc in started:
            loc.wait()

    out_shape = [_sds((N_DEV,) + (a.shape[1:] if sc else a.shape), a.dtype) for a, sc in zip(arrays, scatter)]
    return pl.pallas_call(
        body, in_specs=[ANY] * (n + 1), out_specs=[ANY] * n, out_shape=out_shape,
        scratch_shapes=[pltpu.SemaphoreType.DMA((n * (N_DEV - 1),)), pltpu.SemaphoreType.DMA((n * (N_DEV - 1),)),
                        pltpu.SemaphoreType.DMA((n,))],
        name=name)(*arrays, dep)


def _gather_two_level(name, arrays):
    n = len(arrays)
    per = N_DEV - 1

    def body(*refs):
        ins, outs = refs[:n], refs[n:2 * n]
        send_sems, recv_sems, local_sems = refs[2 * n:]
        x, y, c = lax.axis_index("x"), lax.axis_index("y"), lax.axis_index("c")
        me, sibling = (x, y, c), (x, y, 1 - c)
        chips = [(1 - x, y), (x, 1 - y), (1 - x, 1 - y)]

        def copy(a, k, block, to, src=None):
            slot = outs[a].at[4 * block[0] + 2 * block[1] + block[2]]
            return pltpu.make_async_remote_copy(
                src_ref=slot if src is None else src, dst_ref=slot, send_sem=send_sems.at[a * per + k],
                recv_sem=recv_sems.at[a * per + k], device_id=to, device_id_type=pl.DeviceIdType.MESH)

        mine = [pltpu.make_async_copy(ins[a], outs[a].at[4 * x + 2 * y + c], local_sems.at[a]) for a in range(n)]
        for cp in mine:
            cp.start()
        first = [copy(a, 0, me, sibling, src=ins[a]) for a in range(n)]
        first += [copy(a, 1 + j, me, (*chip, c), src=ins[a]) for j, chip in enumerate(chips) for a in range(n)]
        for cp in first:
            cp.start()
        passed = []
        for j, chip in enumerate(chips):
            for a in range(n):
                copy(a, 1 + j, (*chip, c), me).wait_recv()
                cp = copy(a, 4 + j, (*chip, c), sibling)
                cp.start()
                passed.append(cp)
        for a in range(n):
            copy(a, 0, sibling, me).wait_recv()
            for j, chip in enumerate(chips):
                copy(a, 4 + j, (*chip, 1 - c), me).wait_recv()
        for cp in first + passed:
            cp.wait_send()
        for cp in mine:
            cp.wait()

    return pl.pallas_call(
        body, in_specs=[ANY] * n, out_specs=[ANY] * n,
        out_shape=[_sds((N_DEV,) + a.shape, a.dtype) for a in arrays],
        scratch_shapes=[pltpu.SemaphoreType.DMA((n * per,)), pltpu.SemaphoreType.DMA((n * per,)),
                        pltpu.SemaphoreType.DMA((n,))],
        name=name)(*arrays)


HBM = pl.BlockSpec(memory_space=pltpu.HBM)
SEM = pl.BlockSpec(memory_space=pltpu.SEMAPHORE)
EFFECT = pltpu.SideEffectType.DATAFLOW_SIDE_EFFECTING


def _remote_copies(srcs, lands, scatter, send_sems, recv_sems, me, incoming):
    out = []
    for k in range(1, N_DEV):
        other = (me + N_DEV - k) % N_DEV if incoming else (me + k) % N_DEV
        for a in range(len(srcs)):
            sem = a * (N_DEV - 1) + k - 1
            src = srcs[a].at[other] if scatter[a] else srcs[a]
            dst = lands[a].at[other if incoming else me]
            out.append(pltpu.make_async_remote_copy(src_ref=src, dst_ref=dst, send_sem=send_sems.at[sem],
                                                    recv_sem=recv_sems.at[sem], device_id=_peer(me, k),
                                                    device_id_type=pl.DeviceIdType.MESH))
    return out


def _exchange_start(name, arrays, scatter, dep):
    n = len(arrays)
    lands = [lax.empty((N_DEV,) + (a.shape[1:] if sc else a.shape), a.dtype) for a, sc in zip(arrays, scatter)]

    def body(*refs):
        srcs, land_refs = refs[:n], refs[n:2 * n]
        send_sems, recv_sems = refs[2 * n + 1], refs[2 * n + 2]
        token = refs[-1]
        for cp in _remote_copies(srcs, land_refs, scatter, send_sems, recv_sems, _my_id(), False):
            cp.start()
        token[...] = jnp.zeros_like(token)

    n_sem = n * (N_DEV - 1)
    out_shape = ([pltpu.SemaphoreType.DMA((n_sem,)), pltpu.SemaphoreType.DMA((n_sem,))]
                 + [pltpu.HBM(a.shape, a.dtype) for a in arrays] + [pltpu.HBM(l.shape, l.dtype) for l in lands]
                 + [_sds((8, 128), F32)])
    aliases = {i: 2 + i for i in range(2 * n)}
    args = [pltpu.with_memory_space_constraint(a, pltpu.HBM) for a in list(arrays) + lands] + [dep]
    res = pl.pallas_call(
        body, name=name, in_specs=[HBM] * (2 * n) + [ANY], out_shape=out_shape,
        out_specs=[SEM, SEM] + [HBM] * (2 * n) + [pl.BlockSpec(memory_space=pltpu.VMEM)],
        input_output_aliases=aliases, compiler_params=pltpu.CompilerParams(has_side_effects=EFFECT))(*args)
    return dict(send=res[0], recv=res[1], srcs=res[2:2 + n], lands=res[2 + n:2 + 2 * n], token=res[-1],
                scatter=scatter)


def _exchange_wait(name, started, after):
    n = len(started["srcs"])
    scatter = started["scatter"]

    def body(*refs):
        srcs, land_refs = refs[:n], refs[n:2 * n]
        send_sems, recv_sems = refs[2 * n], refs[2 * n + 1]
        me = _my_id()
        for cp in _remote_copies(srcs, land_refs, scatter, send_sems, recv_sems, me, False):
            cp.wait_send()
        for cp in _remote_copies(srcs, land_refs, scatter, send_sems, recv_sems, me, True):
            cp.wait_recv()

    arrs = list(started["srcs"]) + list(started["lands"])
    res = pl.pallas_call(
        body, name=name, in_specs=[HBM] * (2 * n) + [SEM, SEM, ANY],
        out_shape=[pltpu.HBM(a.shape, a.dtype) for a in arrs], out_specs=[HBM] * (2 * n),
        input_output_aliases={i: i for i in range(2 * n)},
        compiler_params=pltpu.CompilerParams(has_side_effects=EFFECT))(*arrs, started["send"], started["recv"], after)
    me = _my_id()
    out = []
    for src, land, sc in zip(res[:n], res[n:], scatter):
        own = lax.dynamic_index_in_dim(src, me, 0, keepdims=True) if sc else src[None]
        out.append(lax.dynamic_update_slice(land, own, (me,) + (0,) * (land.ndim - 1)))
    return out


def _adamw(name, parts, w, m, v):
    r, c = w.shape
    tr = r
    for cand in (128, 88, 64, 40, 8):
        if r % cand == 0:
            tr = cand
            break
    c1 = 1.0 / (1.0 - ADAM_B1 ** ADAM_STEP)
    c2 = 1.0 / (1.0 - ADAM_B2 ** ADAM_STEP)

    def body(p_ref, w_ref, m_ref, v_ref, g_ref, d_ref, nm_ref, nv_ref):
        g = p_ref[0].astype(F32)
        for s in range(1, N_DEV):
            g = g + p_ref[s].astype(F32)
        mn = ADAM_B1 * m_ref[...] + (1.0 - ADAM_B1) * g
        vn = ADAM_B2 * v_ref[...] + (1.0 - ADAM_B2) * (g * g)
        g_ref[...] = g
        nm_ref[...] = mn
        nv_ref[...] = vn
        d_ref[...] = -ADAM_LR * ((mn * c1) / (jnp.sqrt(vn * c2) + ADAM_EPS) + ADAM_WD * w_ref[...])

    blk = pl.BlockSpec((tr, c), lambda i: (i, 0))
    return pl.pallas_call(
        body, grid=(r // tr,), in_specs=[pl.BlockSpec((N_DEV, tr, c), lambda i: (0, i, 0)), blk, blk, blk],
        out_specs=[blk] * 4, out_shape=[_sds((r, c), F32)] * 4, name=name,
        compiler_params=_params(1, VMEM_LIMIT))(parts, w, m, v)


def _pad_rows(a, rows):
    return jnp.pad(a, ((0, rows - a.shape[0]), (0, 0)))


def _lane_row(vec8, offset):
    return jnp.pad(vec8.reshape(1, 8), ((0, 0), (offset, HD - 8 - offset)))


def kernel(x, positions, attn_norm_w, w_in, conv_w, a_log, dt_bias, delta_out_norm_w, q_norm_w, k_norm_w, attn_out_norm_w, w_out, ffn_norm_w, w_gate_up, w_down, loss_target, m_attn_norm_w, m_w_in, m_conv_w, m_a_log, m_dt_bias, m_delta_out_norm_w, m_q_norm_w, m_k_norm_w, m_attn_out_norm_w, m_w_out, m_ffn_norm_w, m_w_gate_up, m_w_down, v_attn_norm_w, v_w_in, v_conv_w, v_a_log, v_dt_bias, v_delta_out_norm_w, v_q_norm_w, v_k_norm_w, v_attn_out_norm_w, v_w_out, v_ffn_norm_w, v_w_gate_up, v_w_down):
    x2 = x[0]
    t, d = x2.shape
    target = loss_target[0]
    pos_col = positions.reshape(t, 1)
    half = HD // 2
    inv = (ROPE_THETA ** (-np.arange(half, dtype=np.float32) / half)).astype(np.float32)
    inv_row = jnp.asarray(np.concatenate([inv, inv]).reshape(1, HD))

    n_in = w_in.shape[2]
    n_gu = w_gate_up.shape[2]
    w_in_g, conv_g = _gather_two_level("gather_in", [w_in[0].astype(BF16), _pad_rows(conv_w[0], 8)])
    out_fly = _exchange_start("gather_out_start", [w_out[0].astype(BF16)], [False], conv_g)
    gu_fly = _exchange_start("gather_gate_up_start", [w_gate_up[0].astype(BF16)], [False], out_fly["token"])
    down_fly = _exchange_start("gather_down_start", [w_down[0].astype(BF16)], [False], gu_fly["token"])
    n_main = 4 * GW
    n_small = 2 * N_HEADS
    segments = [(0, n_main, 0), (n_main + n_small, N_DEV * n_in, n_main), (n_main, n_main + n_small, 7 * GW)]
    pieces = []
    for lo, hi, _ in segments:
        f = lo
        while f < hi:
            j = f // n_in
            end = min(hi, (j + 1) * n_in)
            pieces.append(w_in_g[j][:, f - j * n_in:end - j * n_in])
            f = end
    w_cat = jnp.concatenate(pieces + [jnp.zeros((d, HD - n_small), BF16)], axis=1)
    n_cat = w_cat.shape[1]
    small_blk = (7 * GW) // HD
    conv_w8 =jnp.transpose(conv_g, (1, 0, 2)).reshape(8, 3 * GW)
    alog_row = _lane_row(a_log[0], 8)
    dtb_row = _lane_row(dt_bias[0], 8)

    tm = min(2048, t)
    h1 = _rms_fwd("norm1", x2, attn_norm_w, down_fly["token"])
    tn = 384
    proj = _mm("in_proj", h1, w_cat, grid=(t // tm, n_cat // tn, 1),
               a_spec=pl.BlockSpec((tm, d), lambda i, j, k: (i, 0)),
               b_spec=pl.BlockSpec((d, tn), lambda i, j, k: (0, j)),
               o_spec=pl.BlockSpec((tm, tn), lambda i, j, k: (i, j)),
               out_shape=_sds((t, n_cat), F32), ca=1, cb=0, nk=1)
    qn = _conv_fwd("conv_q", proj, conv_w8, 0, True, HD ** -0.5)
    kn = _conv_fwd("conv_k", proj, conv_w8, 1, True, 1.0)
    vv = _conv_fwd("conv_v", proj, conv_w8, 2, False, 1.0)
    beta_b, gc_b = _gates_fwd("gates", proj, small_blk, alog_row, dtb_row)
    u, w, p, tinv, qd, kd = _delta_prep("delta_prep", qn, kn, vv, beta_b, gc_b)
    oa_raw, vn, s_hist = _delta_scan("delta_scan", u, w, p, qd, kd, gc_b)

    aq = _qk_fwd("attn_q", proj, 4, q_norm_w, pos_col, inv_row)
    ak = _qk_fwd("attn_k", proj, 5, k_norm_w, pos_col, inv_row)
    ob, lse = _attn_fwd("attn_fwd", aq, ak, proj, 6)
    mixed = _mix_fwd("mix", oa_raw, proj, 3, ob, delta_out_norm_w, attn_out_norm_w)
    (w_out_g,) = _exchange_wait("gather_out_wait", out_fly, mixed)
    w_out_full = w_out_g.reshape(2 * GW, d)
    tn = 512
    x1 = _mm("out_proj", mixed, w_out_full, grid=(t // tm, d // tn, 1),
             a_spec=pl.BlockSpec((tm, 2 * GW), lambda i, j, k: (i, 0)),
             b_spec=pl.BlockSpec((2 * GW, tn), lambda i, j, k: (0, j)),
             o_spec=pl.BlockSpec((tm, tn), lambda i, j, k: (i, j)),
             add=x2, add_spec=pl.BlockSpec((tm, tn), lambda i, j, k: (i, j)),
             out_shape=_sds((t, d), F32), ca=1, cb=0, nk=1)
    h2 = _rms_fwd("norm2", x1, ffn_norm_w, ffn_norm_w)
    per = N_DEV // 2
    (w_gu_g,) = _exchange_wait("gather_gate_up_wait", gu_fly, h2)
    gu3, act = _gate_up_swiglu("gate_up", h2, w_gu_g)
    (w_down_g,) = _exchange_wait("gather_down_wait", down_fly, act)
    w_down_full = w_down_g.reshape(D_FF, d)
    tmd, tkd = min(1024, t), D_FF // 2
    y = _mm("down_proj", act, w_down_full, grid=(t // tmd, d // tn, 2),
            a_spec=pl.BlockSpec((tmd, tkd), lambda i, j, k: (i, k)),
            b_spec=pl.BlockSpec((tkd, tn), lambda i, j, k: (k, j)),
            o_spec=pl.BlockSpec((tmd, tn), lambda i, j, k: (i, j)),
            add=x1, add_spec=pl.BlockSpec((tmd, tn), lambda i, j, k: (i, j)),
            out_shape=_sds((t, d), F32), ca=1, cb=0, nk=2)
    dy, dy16, loss_tile = _loss_head("loss_head", y, target)
    loss = lax.psum(loss_tile[0, 0], ("x", "y", "c"))

    tk = min(2048, t)
    nkt = t // tk
    g_down = _mm("g_down", act, dy16, dep=loss.reshape(1, 1), grid=(D_FF // 512, 1, nkt),
                 a_spec=pl.BlockSpec((tk, 512), lambda i, j, k: (k, i)),
                 b_spec=pl.BlockSpec((tk, d), lambda i, j, k: (k, 0)),
                 o_spec=pl.BlockSpec((512, d), lambda i, j, k: (i, 0)),
                 out_shape=_sds((D_FF, d), F32), ca=0, cb=0, nk=nkt)
    down_g_fly = _exchange_start("reduce_down_start", [g_down.reshape(N_DEV, D_FF // N_DEV, d)], [True], dy16)
    dgu3 = _d_gate_up("d_gate_up", dy16, w_down_full, gu3, down_g_fly["token"])
    g_gu = _mm("g_gate_up", h2, dgu3, grid=(d // 512, N_DEV, nkt),
               a_spec=pl.BlockSpec((tk, 512), lambda i, j, k: (k, i)),
               b_spec=pl.BlockSpec((None, tk, n_gu), lambda i, j, k: (j // per, k, j % per)),
               o_spec=pl.BlockSpec((None, 512, n_gu), lambda i, j, k: (j, i, 0)),
               out_shape=_sds((N_DEV, d, n_gu), F32), ca=0, cb=0, nk=nkt)
    gu_g_fly = _exchange_start("reduce_gate_up_start", [g_gu], [True], dy16)
    tmh, tnh = min(2048, t), 1024
    dh2 = _mm("d_h2", dgu3, w_gu_g, dep=gu_g_fly["token"], grid=(t // tmh, d // tnh, N_DEV),
              a_spec=pl.BlockSpec((None, tmh, n_gu), lambda i, j, k: (k // per, i, k % per)),
              b_spec=pl.BlockSpec((None, tnh, n_gu), lambda i, j, k: (k, j, 0)),
              o_spec=pl.BlockSpec((tmh, tnh), lambda i, j, k: (i, j)),
              out_shape=_sds((t, d), F32), ca=1, cb=1, nk=N_DEV)
    dx1, dx1_16, g_ffn_norm = _rms_bwd("norm2_bwd", x1, ffn_norm_w, dh2, dy)

    g_out = _mm("g_out", mixed, dx1_16, grid=((2 * GW) // 512, 1, nkt),
                a_spec=pl.BlockSpec((tk, 512), lambda i, j, k: (k, i)),
                b_spec=pl.BlockSpec((tk, d), lambda i, j, k: (k, 0)),
                o_spec=pl.BlockSpec((512, d), lambda i, j, k: (i, 0)),
                out_shape=_sds((2 * GW, d), F32), ca=0, cb=0, nk=nkt)
    out_g_fly = _exchange_start("reduce_out_start", [g_out.reshape(N_DEV, (2 * GW) // N_DEV, d)], [True], g_ffn_norm)
    dmixed = _mm("d_mixed", dx1_16, w_out_full, dep=out_g_fly["token"], grid=(t // tm, (2 * GW) // tn, 1),
                 a_spec=pl.BlockSpec((tm, d), lambda i, j, k: (i, 0)),
                 b_spec=pl.BlockSpec((tn, d), lambda i, j, k: (j, 0)),
                 o_spec=pl.BlockSpec((tm, tn), lambda i, j, k: (i, j)),
                 out_shape=_sds((t, 2 * GW), F32), ca=1, cb=1, nk=1)
    doa, dz, dob, delta, g_dn, g_an = _mix_bwd("mix_bwd", dmixed, oa_raw, proj, 3, ob,
                                               delta_out_norm_w, attn_out_norm_w, out_g_fly["token"])
    d_aq, d_ak, d_av = _attn_bwd("attn_bwd", aq, ak, proj, 6, dob, lse, delta)
    daq, g_qn = _qk_bwd("attn_q_bwd", proj, 4, q_norm_w, pos_col, inv_row, d_aq)
    dak, g_kn = _qk_bwd("attn_k_bwd", proj, 5, k_norm_w, pos_col, inv_row, d_ak)
    dav = d_av.astype(BF16)

    dvn, dqd, dkd, dw, ddec = _delta_scan_bwd("delta_scan_bwd", doa, w, p, qd, kd, gc_b, vn, s_hist)
    dqn, dkn, dvv, dbeta_b, dg_b = _delta_prep_bwd("delta_prep_bwd", qn, kn, vv, beta_b, gc_b, tinv, u, w, vn,
                                                   doa, dvn, dqd, dkd, dw, ddec)
    dxq, gcw_q = _conv_bwd("conv_q_bwd", proj, conv_w8, dqn, 0, True, HD ** -0.5)
    dxk, gcw_k = _conv_bwd("conv_k_bwd", proj, conv_w8, dkn, 1, True, 1.0)
    dxv, gcw_v = _conv_bwd("conv_v_bwd", proj, conv_w8, dvv, 2, False, 1.0)
    dsmall, g_alog_row, g_dtb_row = _gates_bwd("gates_bwd", proj, small_blk, alog_row, dtb_row, dbeta_b, dg_b)
    dproj = jnp.concatenate([dxq, dxk, dxv, dz, daq, dak, dav, dsmall], axis=1)
    tnc = n_cat // 3
    g_cat = _mm("g_in", h1, dproj, grid=(d // 512, 3, nkt),
                a_spec=pl.BlockSpec((tk, 512), lambda i, j, k: (k, i)),
                b_spec=pl.BlockSpec((tk, tnc), lambda i, j, k: (k, j)),
                o_spec=pl.BlockSpec((512, tnc), lambda i, j, k: (i, j)),
                out_shape=_sds((d, n_cat), F32), ca=0, cb=0, nk=nkt)
    parts = []
    for j in range(N_DEV):
        cols = []
        for lo, hi, start in sorted(segments):
            a, b = max(lo, j * n_in), min(hi, (j + 1) * n_in)
            if a < b:
                cols.append(g_cat[:, start + a - lo:start + b - lo])
        parts.append(cols[0] if len(cols) == 1 else jnp.concatenate(cols, axis=1))
    g_in_parts = jnp.stack(parts).astype(BF16)
    g_conv = jnp.concatenate([gcw_q, gcw_k, gcw_v], axis=1)
    n_cw = conv_w.shape[2]
    g_conv_parts = jnp.transpose(g_conv.reshape(8, N_DEV, n_cw), (1, 0, 2))
    in_g_fly = _exchange_start("reduce_in_start", [g_in_parts, g_conv_parts], [True] * 2, g_dtb_row)
    tkc = n_cat // 3
    dh1 = _mm("d_h1", dproj, w_cat, dep=in_g_fly["token"], grid=(t // tmd, d // tn, 3),
              a_spec=pl.BlockSpec((tmd, tkc), lambda i, j, k: (i, k)),
              b_spec=pl.BlockSpec((tn, tkc), lambda i, j, k: (j, k)),
              o_spec=pl.BlockSpec((tmd, tn), lambda i, j, k: (i, j)),
              out_shape=_sds((t, d), F32), ca=1, cb=1, nk=3)
    grad_x, _, g_attn_norm = _rms_bwd("norm1_bwd", x2, attn_norm_w, dh1, dx1)

    small_rows = [g_attn_norm.reshape(d // HD, HD), g_ffn_norm.reshape(d // HD, HD), g_dn, g_qn, g_kn, g_an,
                  g_alog_row, g_dtb_row]
    small_pack = _pad_rows(jnp.concatenate(small_rows, axis=0), 40)
    (r_down,) = _exchange_wait("reduce_down_wait", down_g_fly, grad_x)
    (r_gu,) = _exchange_wait("reduce_gate_up_wait", gu_g_fly, grad_x)
    (r_out,) = _exchange_wait("reduce_out_wait", out_g_fly, grad_x)
    res_gu = [a[None] for a in _adamw("adamw_gate_up", r_gu, w_gate_up[0], m_w_gate_up[0], v_w_gate_up[0])]
    res_down = [a[None] for a in _adamw("adamw_down", r_down, w_down[0], m_w_down[0], v_w_down[0])]
    res_out = [a[None] for a in _adamw("adamw_out", r_out, w_out[0], m_w_out[0], v_w_out[0])]
    done = (res_gu[3][0, :1, :1] + res_down[3][0, :1, :1] + res_out[3][0, :1, :1])
    (r_small,) = _exchange("gather_small_grads", [small_pack], [False], done)

    def pack_small(an, fn, dn, qn_, kn_, aon, al, db):
        rows = [an.reshape(d // HD, HD), fn.reshape(d // HD, HD), dn, qn_, kn_, aon,
                _lane_row(al[0], 8), _lane_row(db[0], 8)]
        return _pad_rows(jnp.concatenate(rows, axis=0), 40)

    def unpack_small(pk):
        nr = d // HD
        return dict(attn_norm_w=pk[:nr].reshape(1, d), ffn_norm_w=pk[nr:2 * nr].reshape(1, d),
                    delta_out_norm_w=pk[2 * nr:2 * nr + 1], q_norm_w=pk[2 * nr + 1:2 * nr + 2],
                    k_norm_w=pk[2 * nr + 2:2 * nr + 3], attn_out_norm_w=pk[2 * nr + 3:2 * nr + 4],
                    a_log=pk[2 * nr + 4:2 * nr + 5, 8:16], dt_bias=pk[2 * nr + 5:2 * nr + 6, 8:16])

    res_small = _adamw("adamw_small", r_small,
                       pack_small(attn_norm_w, ffn_norm_w, delta_out_norm_w, q_norm_w, k_norm_w, attn_out_norm_w, a_log, dt_bias),
                       pack_small(m_attn_norm_w, m_ffn_norm_w, m_delta_out_norm_w, m_q_norm_w, m_k_norm_w, m_attn_out_norm_w, m_a_log, m_dt_bias),
                       pack_small(v_attn_norm_w, v_ffn_norm_w, v_delta_out_norm_w, v_q_norm_w, v_k_norm_w, v_attn_out_norm_w, v_a_log, v_dt_bias))
    small = [unpack_small(a) for a in res_small]
    r_in, r_conv = _exchange_wait("reduce_in_wait", in_g_fly, res_small[0])
    res_in = [a[None] for a in _adamw("adamw_in", r_in, w_in[0], m_w_in[0], v_w_in[0])]
    res_conv =[a[None, :4] for a in _adamw("adamw_conv", r_conv, _pad_rows(conv_w[0], 8), _pad_rows(m_conv_w[0], 8),
                                            _pad_rows(v_conv_w[0], 8))]

    outs = [loss, grad_x[None]]
    for i in range(4):
        s = small[i]
        outs += [s["attn_norm_w"], res_in[i], res_conv[i], s["a_log"], s["dt_bias"], s["delta_out_norm_w"],
                 s["q_norm_w"], s["k_norm_w"], s["attn_out_norm_w"], res_out[i], s["ffn_norm_w"], res_gu[i],
                 res_down[i]]
    return tuple(outs)
```

```python
import functools

import numpy as np
import jax
import jax.numpy as jnp
from jax import lax
from jax.experimental import pallas as pl
from jax.experimental.pallas import tpu as pltpu

F32 = jnp.float32
BF16 = jnp.bfloat16

N_DEV = 8
N_HEADS = 8
HD = 128
GW = N_HEADS * HD
CHUNK = 64
PAIR = 2 * CHUNK
SPAN = 128
DILATIONS = (1, 4, 16)
ROPE_THETA = 10000.0
EPS = 1e-6
D_FF = 5632
ADAM_LR, ADAM_B1, ADAM_B2, ADAM_EPS, ADAM_WD, ADAM_STEP = 0.001, 0.9, 0.999, 1e-8, 0.01, 10
NEG = -1e30
VMEM_LIMIT = 56 * 1024 * 1024
ANY = pl.BlockSpec(memory_space=pl.ANY)
HEADS_PER_STEP = 8


def _params(n_grid, vmem=VMEM_LIMIT):
    return pltpu.CompilerParams(dimension_semantics=("arbitrary",) * n_grid, vmem_limit_bytes=vmem)


def _sds(shape, dtype):
    return jax.ShapeDtypeStruct(tuple(shape), dtype)


def _sigmoid(x):
    return 1.0 / (1.0 + jnp.exp(-x))


def _silu(x):
    return x * _sigmoid(x)


def _softplus(x):
    return jnp.maximum(x, 0.0) + jnp.log(1.0 + jnp.exp(-jnp.abs(x)))


def _dot(a, b, ca, cb, precision=None):
    return lax.dot_general(a, b, (((ca,), (cb,)), ((), ())), precision=precision,
                           preferred_element_type=F32)


def _b16(x):
    return x if x.dtype == BF16 else x.astype(BF16)


def _split(x):
    hi = x.astype(BF16)
    return hi, (x - hi.astype(F32)).astype(BF16)


def _dot3(a, b, ca, cb):
    a_hi, a_lo = _split(a)
    b_hi, b_lo = _split(b)
    return _dot(a_hi, b_hi, ca, cb) + (_dot(a_hi, b_lo, ca, cb) + _dot(a_lo, b_hi, ca, cb))


def _iota2(shape, axis):
    return lax.broadcasted_iota(jnp.int32, shape, axis)


def _mm(name, a, b, *, grid, a_spec, b_spec, o_spec, out_shape, ca, cb, nk, add=None, add_spec=None,
        dep=None, vmem=VMEM_LIMIT):
    has_add = add is not None
    n_in = 2 + has_add + (dep is not None)

    def body(*refs):
        a_ref, b_ref = refs[0], refs[1]
        e_ref = refs[2] if has_add else None
        o_ref = refs[n_in]
        part = _dot(_b16(a_ref[...]), _b16(b_ref[...]), ca, cb)
        if nk == 1:
            if has_add:
                part = part + e_ref[...]
            o_ref[...] = part.astype(o_ref.dtype)
            return
        acc = refs[-1]
        k = pl.program_id(2)

        @pl.when(k == 0)
        def _():
            acc[...] = part

        @pl.when(k > 0)
        def _():
            acc[...] += part

        @pl.when(k == nk - 1)
        def _():
            res = acc[...]
            if has_add:
                res = res + e_ref[...]
            o_ref[...] = res.astype(o_ref.dtype)

    in_specs = [a_spec, b_spec] + ([add_spec] if has_add else []) + ([ANY] if dep is not None else [])
    args = (a, b) + ((add,) if has_add else ()) + ((dep,) if dep is not None else ())
    blk = [d for d in o_spec.block_shape if d is not None]
    scratch = [pltpu.VMEM(tuple(blk), F32)] if nk > 1 else []
    return pl.pallas_call(body, grid=grid, in_specs=in_specs, out_specs=o_spec, out_shape=out_shape,
                          scratch_shapes=scratch, name=name, compiler_params=_params(3, vmem))(*args)


def _rms_f(xv, wv):
    return xv * lax.rsqrt(jnp.mean(xv * xv, axis=-1, keepdims=True) + EPS) * wv


def _rms_fwd(name, x, w, dep):
    t, d = x.shape
    tm = min(512, t)

    def body(x_ref, w_ref, dep_ref, o_ref):
        o_ref[...] = _rms_f(x_ref[...], w_ref[...]).astype(BF16)

    row = pl.BlockSpec((tm, d), lambda i: (i, 0))
    vec = pl.BlockSpec((1, d), lambda i: (0, 0))
    return pl.pallas_call(body, grid=(t // tm,), in_specs=[row, vec, ANY], out_specs=row,
                          out_shape=_sds((t, d), BF16), name=name, compiler_params=_params(1))(x, w, dep)


def _rms_bwd(name, x, w, dh, res):
    t, d = x.shape
    tm = min(256, t)

    def body(x_ref, w_ref, dh_ref, res_ref, dx_ref, dx16_ref, dw_ref):
        _, vjp = jax.vjp(_rms_f, x_ref[...], w_ref[...])
        dxv, dwv = vjp(dh_ref[...])
        dxv = dxv + res_ref[...]
        dx_ref[...] = dxv
        dx16_ref[...] = dxv.astype(BF16)

        @pl.when(pl.program_id(0) == 0)
        def _():
            dw_ref[...] = jnp.zeros_like(dw_ref)

        dw_ref[...] += dwv

    row = pl.BlockSpec((tm, d), lambda i: (i, 0))
    vec = pl.BlockSpec((1, d), lambda i: (0, 0))
    return pl.pallas_call(body, grid=(t // tm,), in_specs=[row, vec, row, row], out_specs=[row, row, vec],
                          out_shape=[_sds((t, d), F32), _sds((t, d), BF16), _sds((1, d), F32)], name=name,
                          compiler_params=_params(1))(x, w, dh, res)


def _conv_taps(xv, w_ref, rows):
    c = w_ref[3:4, :] * xv
    for s in (1, 2, 3):
        c = c + w_ref[3 - s:4 - s, :] * jnp.where(rows >= s, pltpu.roll(xv, s, 0), 0.0)
    return c


def _post_conv(c, l2, scale):
    y = _silu(c)
    if l2:
        y = y * lax.rsqrt(jnp.sum(y * y, axis=-1, keepdims=True) + EPS) * scale
    return y


def _conv_fwd(name, proj, conv_w8, group, l2, scale):
    t = proj.shape[0]

    def body(x_ref, w_ref, o_ref):
        rows = _iota2((t, HD), 0)
        o_ref[...] = _post_conv(_conv_taps(x_ref[...], w_ref, rows), l2, scale)

    return pl.pallas_call(
        body, grid=(N_HEADS,),
        in_specs=[pl.BlockSpec((t, HD), lambda h: (0, h + group * N_HEADS)),
                  pl.BlockSpec((8, HD), lambda h: (0, h + group * N_HEADS))],
        out_specs=pl.BlockSpec((t, HD), lambda h: (0, h)),
        out_shape=_sds((t, GW), F32), name=name, compiler_params=_params(1, VMEM_LIMIT))(proj, conv_w8)


def _conv_bwd(name, proj, conv_w8, dn, dproj, group, l2, scale):
    t = proj.shape[0]

    def body(x_ref, w_ref, dn_ref, dproj_ref, dx_ref, dw_ref):
        rows = _iota2((t, HD), 0)
        xv = x_ref[...]
        c = _conv_taps(xv, w_ref, rows)
        _, vjp = jax.vjp(lambda cc: _post_conv(cc, l2, scale), c)
        (dc,) = vjp(dn_ref[...])
        dx = w_ref[3:4, :] * dc
        dw = jnp.zeros((8, HD), F32)
        rid = _iota2((8, HD), 0)
        dw = dw + jnp.where(rid == 3, jnp.sum(dc * xv, axis=0, keepdims=True), 0.0)
        for s in (1, 2, 3):
            dx = dx + w_ref[3 - s:4 - s, :] * jnp.where(rows < t - s, pltpu.roll(dc, t - s, 0), 0.0)
            xs = jnp.where(rows >= s, pltpu.roll(xv, s, 0), 0.0)
            dw = dw + jnp.where(rid == 3 - s, jnp.sum(dc * xs, axis=0, keepdims=True), 0.0)
        dx_ref[...] = dx.astype(BF16)
        dw_ref[...] = dw

    return pl.pallas_call(
        body, grid=(N_HEADS,),
        in_specs=[pl.BlockSpec((t, HD), lambda h: (0, h + group * N_HEADS)),
                  pl.BlockSpec((8, HD), lambda h: (0, h + group * N_HEADS)),
                  pl.BlockSpec((t, HD), lambda h: (0, h)), ANY],
        out_specs=[pl.BlockSpec((t, HD), lambda h: (0, h + group * N_HEADS)), pl.BlockSpec((8, HD), lambda h: (0, h))],
        out_shape=[_sds(dproj.shape, BF16), _sds((8, GW), F32)], input_output_aliases={3: 0}, name=name,
        compiler_params=_params(1, VMEM_LIMIT))(proj, conv_w8, dn, dproj)


def _chunk_cumsum(g, rows):
    pos = rows % CHUNK
    s = 1
    while s < CHUNK:
        g = g + jnp.where(pos >= s, pltpu.roll(g, s, 0), 0.0)
        s *= 2
    return g


def _gates_fwd(name, proj, small_blk, alog_row, dtb_row):
    t = proj.shape[0]
    tm = min(256, t)

    def body(s_ref, a_ref, b_ref, beta_ref, gc_ref):
        sm = s_ref[...]
        beta = _sigmoid(sm)
        g = -jnp.exp(a_ref[...]) * _softplus(sm + b_ref[...])
        gc = _chunk_cumsum(g, _iota2((tm, HD), 0))
        lane = _iota2((tm, HD), 1)
        for h in range(N_HEADS):
            bcol = jnp.sum(jnp.where(lane == h, beta, 0.0), axis=1, keepdims=True)
            gcol = jnp.sum(jnp.where(lane == 8 + h, gc, 0.0), axis=1, keepdims=True)
            beta_ref[:, h * HD:(h + 1) * HD] = jnp.broadcast_to(bcol, (tm, HD))
            gc_ref[:, h * HD:(h + 1) * HD] = jnp.broadcast_to(gcol, (tm, HD))

    vec = pl.BlockSpec((1, HD), lambda i: (0, 0))
    wide = pl.BlockSpec((tm, GW), lambda i: (i, 0))
    return pl.pallas_call(
        body, grid=(t // tm,),
        in_specs=[pl.BlockSpec((tm, HD), lambda i: (i, small_blk)), vec, vec], out_specs=[wide, wide],
        out_shape=[_sds((t, GW), F32), _sds((t, GW), F32)], name=name,
        compiler_params=_params(1))(proj, alog_row, dtb_row)


def _gates_bwd(name, proj, small_blk, alog_row, dtb_row, dbeta_b, dg_b, dproj):
    t = proj.shape[0]
    tm = min(256, t)

    def body(s_ref, a_ref, b_ref, db_ref, dg_ref, dproj_ref, ds_ref, da_ref, dbias_ref):
        sm = s_ref[...]
        lane = _iota2((tm, HD), 1)
        db = jnp.zeros((tm, HD), F32)
        dg = jnp.zeros((tm, HD), F32)
        for h in range(N_HEADS):
            db = db + jnp.where(lane == h, db_ref[:, h * HD:(h + 1) * HD], 0.0)
            dg = dg + jnp.where(lane == 8 + h, dg_ref[:, h * HD:(h + 1) * HD], 0.0)
        beta = _sigmoid(sm)
        ea = jnp.exp(a_ref[...])
        pre = sm + b_ref[...]
        g = -ea * _softplus(pre)
        dpre = dg * (-ea) * _sigmoid(pre)
        ds_ref[...] = (db * beta * (1.0 - beta) + dpre).astype(BF16)

        @pl.when(pl.program_id(0) == 0)
        def _():
            da_ref[...] = jnp.zeros_like(da_ref)
            dbias_ref[...] = jnp.zeros_like(dbias_ref)

        da_ref[...] += jnp.sum(dg * g, axis=0, keepdims=True)
        dbias_ref[...] += jnp.sum(dpre, axis=0, keepdims=True)

    vec = pl.BlockSpec((1, HD), lambda i: (0, 0))
    wide = pl.BlockSpec((tm, GW), lambda i: (i, 0))
    return pl.pallas_call(
        body, grid=(t // tm,),
        in_specs=[pl.BlockSpec((tm, HD), lambda i: (i, small_blk)), vec, vec, wide, wide, ANY],
        out_specs=[pl.BlockSpec((tm, HD), lambda i: (i, small_blk)), vec, vec],
        out_shape=[_sds(dproj.shape, BF16), _sds((1, HD), F32), _sds((1, HD), F32)],
        input_output_aliases={5: 0}, name=name,
        compiler_params=_params(1))(proj, alog_row, dtb_row, dbeta_b, dg_b, dproj)


def _pair_masks():
    ii = _iota2((PAIR, PAIR), 0)
    jj = _iota2((PAIR, PAIR), 1)
    same = (ii // CHUNK) == (jj // CHUNK)
    return ii, jj, same & (ii >= jj), same & (ii > jj)


def _to_row(col_b, ii, jj):
    return jnp.sum(jnp.where(ii == jj, col_b, 0.0), axis=0, keepdims=True)


def _to_col(row, ii, jj):
    return jnp.sum(jnp.where(ii == jj, jnp.broadcast_to(row, (PAIR, PAIR)), 0.0), axis=1, keepdims=True)


def _decay_parts(gc, last_a, last_b, ii, jj, causal):
    diff = gc - _to_row(gc, ii, jj)
    dmat = jnp.where(causal, jnp.exp(jnp.where(causal, diff, 0.0)), 0.0)
    glast = jnp.where(ii < CHUNK, last_a, last_b)
    return dmat, jnp.exp(gc), jnp.exp(glast - gc)


def _unit_lower_inverse(lows, ii, jj):
    eye = jnp.where(ii == jj, 1.0, 0.0)
    mm = lambda xs, ys: [_dot3(a, b, 1, 0) for a, b in zip(xs, ys)]
    plus = lambda xs: [eye + a for a in xs]
    minus = lambda xs: [eye - a for a in xs]
    d1 = [jnp.where((ii // 16) == (jj // 16), low, 0.0) for low in lows]
    d2 = mm(d1, d1)
    a = mm(minus(d1), plus(d2))
    d4 = mm(d2, d2)
    a = mm(a, plus(d4))
    d8 = mm(d4, d4)
    td = mm(a, plus(d8))
    n1 = mm(td, [low - d for low, d in zip(lows, d1)])
    n2 = mm(n1, n1)
    return mm(mm(minus(n1), plus(n2)), td)


def _delta_prep(name, qn, kn, vv, beta_b, gc_b):
    t = qn.shape[0]

    def body(q_ref, k_ref, v_ref, b_ref, g_ref, u_ref, w_ref, p_ref, t_ref, qd_ref, kd_ref):
        ii, jj, causal, strict = _pair_masks()
        sls = [slice(hh * HD, (hh + 1) * HD) for hh in range(HEADS_PER_STEP)]
        lows = []
        for sl in sls:
            q, k, beta = q_ref[:, sl], k_ref[:, sl], b_ref[:, sl]
            dmat, gam, e2 = _decay_parts(g_ref[:, sl], g_ref[CHUNK - 1:CHUNK, sl], g_ref[PAIR - 1:PAIR, sl],
                                         ii, jj, causal)
            k16 = _b16(k)
            lows.append(jnp.where(strict, beta * _dot(k16, k16, 1, 1) * dmat, 0.0))
            p_ref[:, sl] = jnp.where(causal, _dot(_b16(q), k16, 1, 1) * dmat, 0.0).astype(BF16)
            qd_ref[:, sl] = (q * gam).astype(BF16)
            kd_ref[:, sl] = (k * e2).astype(BF16)
        for sl, tinv in zip(sls, _unit_lower_inverse(lows, ii, jj)):
            beta = b_ref[:, sl]
            t_ref[:, sl] = tinv
            u_ref[:, sl] = _dot3(tinv, v_ref[:, sl] * beta, 1, 0)
            w_ref[:, sl] = _dot3(tinv, k_ref[:, sl] * (beta * jnp.exp(g_ref[:, sl])), 1, 0).astype(BF16)

    blk = pl.BlockSpec((PAIR, HEADS_PER_STEP * HD), lambda i, h: (i, h))
    return pl.pallas_call(
        body, grid=(t // PAIR, N_HEADS // HEADS_PER_STEP), in_specs=[blk] * 5, out_specs=[blk] * 6,
        out_shape=[_sds((t, GW), F32), _sds((t, GW), BF16), _sds((t, GW), BF16), _sds((t, GW), F32),
                   _sds((t, GW), BF16), _sds((t, GW), BF16)],
        name=name, compiler_params=_params(2))(qn, kn, vv, beta_b, gc_b)


def _delta_scan(name, u, w, p, qd, kd, gc_b):
    t = u.shape[0]
    n = t // CHUNK

    def body(u_ref, w_ref, p_ref, qd_ref, kd_ref, g_ref, o_ref, vn_ref, sh_ref, state):
        @pl.when(pl.program_id(0) == 0)
        def _():
            state[...] = jnp.zeros_like(state)

        sls = [slice(h * HD, (h + 1) * HD) for h in range(N_HEADS)]
        heads = range(N_HEADS)
        s = [state[h] for h in heads]
        for h in heads:
            sh_ref[h] = s[h]
        s16 = [_b16(a) for a in s]
        ws = [_dot(w_ref[:, sls[h]], s16[h], 1, 0) for h in heads]
        qs = [_dot(qd_ref[:, sls[h]], s16[h], 1, 0) for h in heads]
        vn16 = [_b16(u_ref[:, sls[h]] - ws[h]) for h in heads]
        pv = [_dot(p_ref[:, sls[h]], jnp.concatenate([vn16[h], vn16[h]], axis=0), 1, 0) for h in heads]
        kv = [_dot(kd_ref[:, sls[h]], vn16[h], 0, 0) for h in heads]
        for h in heads:
            o_ref[:, sls[h]] = qs[h] + pv[h]
            vn_ref[:, sls[h]] = vn16[h]
            state[h] = s[h] * jnp.exp(g_ref[CHUNK - 1:CHUNK, sls[h]]) + kv[h]

    blk = pl.BlockSpec((CHUNK, GW), lambda i: (i, 0))
    return pl.pallas_call(
        body, grid=(n,), in_specs=[blk] * 6,
        out_specs=[blk, blk, pl.BlockSpec((None, N_HEADS, HD, HD), lambda i: (i, 0, 0, 0))],
        out_shape=[_sds((t, GW), F32), _sds((t, GW), BF16), _sds((n, N_HEADS, HD, HD), F32)],
        scratch_shapes=[pltpu.VMEM((N_HEADS, HD, HD), F32)], name=name,
        compiler_params=_params(1))(u, w, p, qd, kd, gc_b)


def _delta_scan_bwd(name, do, w, p, qd, kd, gc_b, vn, s_hist):
    t = do.shape[0]
    n = t // CHUNK

    def body(do_ref, w_ref, p_ref, qd_ref, kd_ref, g_ref, vn_ref, sh_ref,
             dvn_ref, dqd_ref, dkd_ref, dw_ref, ddec_ref, dstate):
        @pl.when(pl.program_id(0) == 0)
        def _():
            dstate[...] = jnp.zeros_like(dstate)

        sls = [slice(h * HD, (h + 1) * HD) for h in range(N_HEADS)]
        heads = range(N_HEADS)
        ds = [dstate[h] for h in heads]
        ds16 = [_b16(a) for a in ds]
        s16 = [_b16(sh_ref[h]) for h in heads]
        do16 = [_b16(do_ref[:, sls[h]]) for h in heads]
        ptdo = [_dot(p_ref[:, sls[h]], do16[h], 0, 0) for h in heads]
        kds = [_dot(kd_ref[:, sls[h]], ds16[h], 1, 0) for h in heads]
        qdo = [_dot(qd_ref[:, sls[h]], do16[h], 0, 0) for h in heads]
        for h in heads:
            dqd_ref[:, sls[h]] = _dot(do16[h], s16[h], 1, 1)
            dkd_ref[:, sls[h]] = _dot(vn_ref[:, sls[h]], ds16[h], 1, 1)
        dvn = [ptdo[h][:CHUNK, :] + ptdo[h][CHUNK:, :] + kds[h] for h in heads]
        dvn16 = [_b16(a) for a in dvn]
        wdv = [_dot(w_ref[:, sls[h]], dvn16[h], 0, 0) for h in heads]
        for h in heads:
            dvn_ref[:, sls[h]] = dvn[h]
            dw_ref[:, sls[h]] = -_dot(dvn16[h], s16[h], 1, 1)
            tot = jnp.sum(jnp.sum(sh_ref[h] * ds[h], axis=1, keepdims=True), axis=0, keepdims=True)
            ddec_ref[:, sls[h]] = jnp.broadcast_to(tot, (8, HD))
            dstate[h] = ds[h] * jnp.exp(g_ref[CHUNK - 1:CHUNK, sls[h]]) + qdo[h] - wdv[h]

    blk = pl.BlockSpec((CHUNK, GW), lambda i: (n - 1 - i, 0))
    return pl.pallas_call(
        body, grid=(n,),
        in_specs=[blk] * 7 + [pl.BlockSpec((None, N_HEADS, HD, HD), lambda i: (n - 1 - i, 0, 0, 0))],
        out_specs=[blk] * 4 + [pl.BlockSpec((8, GW), lambda i: (n - 1 - i, 0))],
        out_shape=[_sds((t, GW), F32)] * 4 + [_sds((n * 8, GW), F32)],
        scratch_shapes=[pltpu.VMEM((N_HEADS, HD, HD), F32)], name=name,
        compiler_params=_params(1))(do, w, p, qd, kd, gc_b, vn, s_hist)


def _delta_prep_bwd(name, qn, kn, vv, beta_b, gc_b, tinv, u, w, vn, do, dvn, dqd, dkd, dw, ddec):
    t = qn.shape[0]

    def body(q_ref, k_ref, v_ref, b_ref, g_ref, t_ref, u_ref, w_ref, vn_ref, do_ref, dvn_ref, dqd_ref,
             dkd_ref, dw_ref, ddec_ref, dq_ref, dk_ref, dv_ref, dbeta_ref, dg_ref):
        ii, jj, causal, strict = _pair_masks()
        suffix = ((ii // CHUNK) == (jj // CHUNK)) & (jj >= ii)
        first = ii < CHUNK
        rs = lambda a: jnp.sum(a, axis=1, keepdims=True)
        for hh in range(HEADS_PER_STEP):
            sl = slice(hh * HD, (hh + 1) * HD)
            q, k, v, beta, gc = q_ref[:, sl], k_ref[:, sl], v_ref[:, sl], b_ref[:, sl], g_ref[:, sl]
            last_a, last_b = g_ref[CHUNK - 1:CHUNK, sl], g_ref[PAIR - 1:PAIR, sl]
            dmat, gam, e2 = _decay_parts(gc, last_a, last_b, ii, jj, causal)
            q16, k16 = _b16(q), _b16(k)
            kk = _dot(k16, k16, 1, 1)
            qk = _dot(q16, k16, 1, 1)
            dqd, dkd = dqd_ref[:, sl], dkd_ref[:, sl]
            dp = jnp.where(causal, _dot(_b16(do_ref[:, sl]), vn_ref[:, sl], 1, 1), 0.0)
            dpd16 = _b16(dp * dmat)
            tinv_v = t_ref[:, sl]
            x = _dot3(tinv_v, dvn_ref[:, sl], 0, 0)
            y = _dot3(tinv_v, dw_ref[:, sl], 0, 0)
            da = -jnp.where(strict, _dot(_b16(x), _b16(u_ref[:, sl]), 1, 1) + _dot(_b16(y), w_ref[:, sl], 1, 1), 0.0)
            dkk16 = _b16(da * beta * dmat)
            dq_ref[:, sl] = gam * dqd + _dot(dpd16, k16, 1, 0)
            dk_ref[:, sl] = (e2 * dkd + _dot(dpd16, q16, 0, 0) + beta * gam * y
                             + _dot(dkk16, k16, 1, 0) + _dot(dkk16, k16, 0, 0))
            dv_ref[:, sl] = beta * x
            dbeta = rs(v * x) + rs(k * gam * y) + rs(da * kk * dmat)
            dbeta_ref[:, sl] = jnp.broadcast_to(dbeta, (PAIR, HD))
            m = (dp * qk + da * beta * kk) * dmat
            dgam = rs(q * dqd) + rs(k * beta * y)
            de2 = rs(k * dkd)
            colsum = _to_col(jnp.sum(m, axis=0, keepdims=True), ii, jj)
            te2 = de2 * e2
            dgc = rs(m) - colsum + gam * dgam - te2
            tail_a = jnp.sum(jnp.where(first, te2, 0.0), axis=0, keepdims=True)
            tail_b = jnp.sum(jnp.where(first, 0.0, te2), axis=0, keepdims=True)
            dgc = dgc + jnp.where(ii == CHUNK - 1, tail_a + ddec_ref[0:1, sl] * jnp.exp(last_a), 0.0)
            dgc = dgc + jnp.where(ii == PAIR - 1, tail_b + ddec_ref[8:9, sl] * jnp.exp(last_b), 0.0)
            dgc_row = _to_row(dgc, ii, jj)
            dg = jnp.sum(jnp.where(suffix, jnp.broadcast_to(dgc_row, (PAIR, PAIR)), 0.0), axis=1, keepdims=True)
            dg_ref[:, sl] = jnp.broadcast_to(dg, (PAIR, HD))

    blk = pl.BlockSpec((PAIR, HEADS_PER_STEP * HD), lambda i, h: (i, h))
    return pl.pallas_call(
        body, grid=(t // PAIR, N_HEADS // HEADS_PER_STEP),
        in_specs=[blk] * 14 + [pl.BlockSpec((16, HEADS_PER_STEP * HD), lambda i, h: (i, h))], out_specs=[blk] * 5,
        out_shape=[_sds((t, GW), F32)] * 5, name=name,
        compiler_params=_params(2))(qn, kn, vv, beta_b, gc_b, tinv, u, w, vn, do, dvn, dqd, dkd, dw, ddec)


def _rope_tables(pos_col, inv_row):
    ang = pos_col.astype(F32) * inv_row
    lane = _iota2(ang.shape, 1)
    return jnp.cos(ang), jnp.where(lane < HD // 2, -1.0, 1.0) * jnp.sin(ang)


def _head_rms(xh, wv):
    return xh * lax.rsqrt(jnp.mean(xh * xh, axis=-1, keepdims=True) + EPS) * wv


def _qk_fwd(name, proj, blk_idx, w_row, pos_col, inv_row):
    t = proj.shape[0]
    tm = min(256, t)

    def body(x_ref, w_ref, pos_ref, inv_ref, o_ref):
        cos, sin = _rope_tables(pos_ref[...], inv_ref[...])
        for h in range(N_HEADS):
            y = _head_rms(x_ref[:, h * HD:(h + 1) * HD], w_ref[...])
            o_ref[:, h * HD:(h + 1) * HD] = y * cos + pltpu.roll(y, HD // 2, 1) * sin

    vec = pl.BlockSpec((1, HD), lambda i: (0, 0))
    return pl.pallas_call(
        body, grid=(t // tm,),
        in_specs=[pl.BlockSpec((tm, GW), lambda i: (i, blk_idx)), vec, pl.BlockSpec((tm, 1), lambda i: (i, 0)), vec],
        out_specs=pl.BlockSpec((tm, GW), lambda i: (i, 0)), out_shape=_sds((t, GW), F32), name=name,
        compiler_params=_params(1))(proj, w_row, pos_col, inv_row)


def _qk_bwd(name, proj, blk_idx, w_row, pos_col, inv_row, dy_full, dproj):
    t = proj.shape[0]
    tm = min(256, t)

    def body(x_ref, w_ref, pos_ref, inv_ref, dy_ref, dproj_ref, dx_ref, dw_ref):
        cos, sin = _rope_tables(pos_ref[...], inv_ref[...])
        dw = jnp.zeros((1, HD), F32)
        for h in range(N_HEADS):
            sl = slice(h * HD, (h + 1) * HD)
            dy = dy_ref[:, sl]
            dy = dy * cos - pltpu.roll(dy, HD // 2, 1) * sin
            _, vjp = jax.vjp(_head_rms, x_ref[:, sl], w_ref[...])
            dx, dwh = vjp(dy)
            dw = dw + dwh
            dx_ref[:, sl] = dx.astype(BF16)

        @pl.when(pl.program_id(0) == 0)
        def _():
            dw_ref[...] = jnp.zeros_like(dw_ref)

        dw_ref[...] += dw

    vec = pl.BlockSpec((1, HD), lambda i: (0, 0))
    wide = pl.BlockSpec((tm, GW), lambda i: (i, 0))
    return pl.pallas_call(
        body, grid=(t // tm,),
        in_specs=[pl.BlockSpec((tm, GW), lambda i: (i, blk_idx)), vec, pl.BlockSpec((tm, 1), lambda i: (i, 0)), vec,
                  wide, ANY],
        out_specs=[pl.BlockSpec((tm, GW), lambda i: (i, blk_idx)), vec],
        out_shape=[_sds(dproj.shape, BF16), _sds((1, HD), F32)], input_output_aliases={5: 0}, name=name,
        compiler_params=_params(1))(proj, w_row, pos_col, inv_row, dy_full, dproj)


def _cast_into(name, x, dproj, blk_idx):
    t = x.shape[0]
    tm = min(512, t)

    def body(x_ref, dproj_ref, o_ref):
        o_ref[...] = x_ref[...].astype(BF16)

    return pl.pallas_call(
        body, grid=(t // tm,), in_specs=[pl.BlockSpec((tm, GW), lambda i: (i, 0)), ANY],
        out_specs=pl.BlockSpec((tm, GW), lambda i: (i, blk_idx)), out_shape=_sds(dproj.shape, BF16),
        input_output_aliases={1: 0}, name=name, compiler_params=_params(1))(x, dproj)


GROUP = SPAN * max(DILATIONS)
SCALE = HD ** -0.5


def _band_mask(lo):
    qi = _iota2((SPAN, 2 * SPAN), 0)
    ki = _iota2((SPAN, 2 * SPAN), 1)
    return (ki >= qi) & (ki <= qi + SPAN) & (ki >= lo)


def _tiles():
    return [(pi, r, u, rho) for pi, r in enumerate(DILATIONS) for u in range(GROUP // (SPAN * r)) for rho in range(r)]


def _rows(r, u, rho):
    return pl.ds(u * SPAN * r + rho, SPAN, stride=r) if r > 1 else pl.ds(u * SPAN, SPAN)


def _attn_fwd(name, q, k, v, v_blk):
    t = q.shape[0]

    def body(qc_ref, kc_ref, vc_ref, kp_ref, vp_ref, ob_ref, lse_ref, o_scr, l_scr):
        mask_in = _band_mask(0)
        mask_edge = _band_mask(jnp.where(pl.program_id(0) == 0, SPAN, 0))
        for pi, r, u, rho in _tiles():
            rows = _rows(r, u, rho)
            if u > 0:
                prows, kp_src, vp_src, mask = _rows(r, u - 1, rho), kc_ref, vc_ref, mask_in
            else:
                prows, kp_src, vp_src, mask = _rows(r, GROUP // (SPAN * r) - 1, rho), kp_ref, vp_ref, mask_edge
            kcat = jnp.concatenate([kp_src[prows, :], kc_ref[rows, :]], axis=0).astype(BF16)
            vcat = jnp.concatenate([vp_src[prows, :], vc_ref[rows, :]], axis=0).astype(BF16)
            s = jnp.where(mask, _dot(qc_ref[rows, :].astype(BF16), kcat, 1, 1) * SCALE, NEG)
            m = jnp.max(s, axis=1, keepdims=True)
            p = jnp.exp(s - m)
            den = jnp.sum(p, axis=1, keepdims=True)
            o_scr[pi, rows, :] = _dot(_b16(p), vcat, 1, 0) / den
            l_scr[pi, rows, :] = jnp.broadcast_to(m + jnp.log(den), (SPAN, HD))
        step = 256
        for c in range(GROUP // step):
            sl = pl.ds(c * step, step)
            ob, lse = _merge([o_scr[i, sl, :] for i in range(3)], [l_scr[i, sl, :] for i in range(3)])
            ob_ref[sl, :] = ob
            lse_ref[sl, :] = lse

    cur = pl.BlockSpec((GROUP, HD), lambda g, h: (g, h))
    prev = pl.BlockSpec((GROUP, HD), lambda g, h: (jnp.maximum(g - 1, 0), h))
    vcur = pl.BlockSpec((GROUP, HD), lambda g, h: (g, v_blk * N_HEADS + h))
    vprev = pl.BlockSpec((GROUP, HD), lambda g, h: (jnp.maximum(g - 1, 0), v_blk * N_HEADS + h))
    return pl.pallas_call(
        body, grid=(t // GROUP, N_HEADS), in_specs=[cur, cur, vcur, prev, vprev], out_specs=[cur, cur],
        out_shape=[_sds((t, GW), F32), _sds((t, GW), F32)],
        scratch_shapes=[pltpu.VMEM((3, GROUP, HD), F32), pltpu.VMEM((3, GROUP, HD), F32)], name=name,
        compiler_params=_params(2))(q, k, v, k, v)


def _attn_bwd(name, q, k, v, v_blk, do, lse, delta):
    t = q.shape[0]
    ng = t // GROUP

    def pair(qt, dot, lt, dlt, kcat, vcat, mask):
        wide = kcat.shape[0] // SPAN
        lw = jnp.concatenate([lt] * wide, axis=1) if wide > 1 else lt
        dw = jnp.concatenate([dlt] * wide, axis=1) if wide > 1 else dlt
        s = _dot(qt, kcat, 1, 1) * SCALE
        p = jnp.where(mask, jnp.exp(jnp.where(mask, s - lw, 0.0)), 0.0)
        ds = p * (_dot(dot, vcat, 1, 1) - dw) * SCALE
        return _b16(ds), _b16(p)

    def body(qc_ref, kc_ref, vc_ref, doc_ref, lc_ref, dc_ref, kp_ref, vp_ref, qn_ref, don_ref, ln_ref, dn_ref,
             dq_ref, dk_ref, dv_ref):
        g = pl.program_id(0)
        mask_in = _band_mask(0)
        mask_edge = _band_mask(jnp.where(g == 0, SPAN, 0))
        dk_ref[...] = jnp.zeros_like(dk_ref)
        dv_ref[...] = jnp.zeros_like(dv_ref)
        for pi, r, u, rho in _tiles():
            rows = _rows(r, u, rho)
            if u > 0:
                prows, kp_src, vp_src, mask = _rows(r, u - 1, rho), kc_ref, vc_ref, mask_in
            else:
                prows, kp_src, vp_src, mask = _rows(r, GROUP // (SPAN * r) - 1, rho), kp_ref, vp_ref, mask_edge
            kcat = jnp.concatenate([kp_src[prows, :], kc_ref[rows, :]], axis=0).astype(BF16)
            vcat = jnp.concatenate([vp_src[prows, :], vc_ref[rows, :]], axis=0).astype(BF16)
            qt, dot = qc_ref[rows, :].astype(BF16), doc_ref[rows, :].astype(BF16)
            ds, p = pair(qt, dot, lc_ref[rows, :], dc_ref[rows, :], kcat, vcat, mask)
            dq_t = _dot(ds, kcat, 1, 0)
            if pi == 0:
                dq_ref[rows, :] = dq_t
            else:
                dq_ref[rows, :] += dq_t
            dk2 = _dot(ds, qt, 0, 0)
            dv2 = _dot(p, dot, 0, 0)
            dk_ref[rows, :] += dk2[SPAN:, :]
            dv_ref[rows, :] += dv2[SPAN:, :]
            if u > 0:
                dk_ref[prows, :] += dk2[:SPAN, :]
                dv_ref[prows, :] += dv2[:SPAN, :]
        qi = _iota2((SPAN, SPAN), 0)
        ki = _iota2((SPAN, SPAN), 1)
        mask_next = (ki >= qi) & (ki < jnp.where(g == ng - 1, 0, SPAN))
        for r in DILATIONS:
            for rho in range(r):
                krows, qrows = _rows(r, GROUP // (SPAN * r) - 1, rho), _rows(r, 0, rho)
                qt, dot = qn_ref[qrows, :].astype(BF16), don_ref[qrows, :].astype(BF16)
                ds, p = pair(qt, dot, ln_ref[qrows, :], dn_ref[qrows, :], kc_ref[krows, :].astype(BF16),
                             vc_ref[krows, :].astype(BF16), mask_next)
                dk_ref[krows, :] += _dot(ds, qt, 0, 0)
                dv_ref[krows, :] += _dot(p, dot, 0, 0)

    cur = pl.BlockSpec((GROUP, HD), lambda g, h: (g, h))
    prev = pl.BlockSpec((GROUP, HD), lambda g, h: (jnp.maximum(g - 1, 0), h))
    nxt = pl.BlockSpec((GROUP, HD), lambda g, h: (jnp.minimum(g + 1, ng - 1), h))
    vcur = pl.BlockSpec((GROUP, HD), lambda g, h: (g, v_blk * N_HEADS + h))
    vprev = pl.BlockSpec((GROUP, HD), lambda g, h: (jnp.maximum(g - 1, 0), v_blk * N_HEADS + h))
    return pl.pallas_call(
        body, grid=(ng, N_HEADS), in_specs=[cur, cur, vcur, cur, cur, cur, prev, vprev] + [nxt] * 4,
        out_specs=[cur] * 3,
        out_shape=[_sds((t, GW), F32)] * 3, name=name,
        compiler_params=_params(2))(q, k, v, do, lse, delta, k, v, q, do, lse, delta)


def _merge(os_, ls_):
    m = jnp.maximum(jnp.maximum(ls_[0], ls_[1]), ls_[2])
    ws = [jnp.exp(l - m) for l in ls_]
    tot = ws[0] + ws[1] + ws[2]
    ob = (ws[0] * os_[0] + ws[1] * os_[1] + ws[2] * os_[2]) / tot
    return ob, m + jnp.log(tot)


def _gated_norm(oa, z, wv):
    return _head_rms(oa, wv) * _silu(z)


def _mix_fwd(name, oa_raw, proj, z_blk, ob, w_dn, w_an):
    t = oa_raw.shape[0]
    tm = min(256, t)

    def body(oa_ref, z_ref, ob_ref, wd_ref, wa_ref, mix_ref):
        for h in range(N_HEADS):
            sl = slice(h * HD, (h + 1) * HD)
            mix_ref[:, sl] = _gated_norm(oa_ref[:, sl], z_ref[:, sl], wd_ref[...]).astype(BF16)
            mix_ref[:, GW + h * HD:GW + (h + 1) * HD] = _head_rms(ob_ref[:, sl], wa_ref[...]).astype(BF16)

    vec = pl.BlockSpec((1, HD), lambda i: (0, 0))
    wide = pl.BlockSpec((tm, GW), lambda i: (i, 0))
    return pl.pallas_call(
        body, grid=(t // tm,),
        in_specs=[wide, pl.BlockSpec((tm, GW), lambda i: (i, z_blk)), wide, vec, vec],
        out_specs=pl.BlockSpec((tm, 2 * GW), lambda i: (i, 0)),
        out_shape=_sds((t, 2 * GW), BF16), name=name,
        compiler_params=_params(1))(oa_raw, proj, ob, w_dn, w_an)


def _mix_bwd(name, dmixed, oa_raw, proj, z_blk, ob, w_dn, w_an, dep):
    t = oa_raw.shape[0]
    tm = min(256, t)

    def body(dm_ref, oa_ref, z_ref, ob_ref, wd_ref, wa_ref, dep_ref,
             doa_ref, dz_ref, dob_ref, dl_ref, dwd_ref, dwa_ref):
        dwd = jnp.zeros((1, HD), F32)
        dwa = jnp.zeros((1, HD), F32)
        for h in range(N_HEADS):
            sl = slice(h * HD, (h + 1) * HD)
            _, vjp = jax.vjp(_gated_norm, oa_ref[:, sl], z_ref[:, sl], wd_ref[...])
            doa, dz, dw1 = vjp(dm_ref[:, sl])
            doa_ref[:, sl] = doa
            dz_ref[:, sl] = dz.astype(BF16)
            dwd = dwd + dw1
            obh = ob_ref[:, sl]
            _, vjp2 = jax.vjp(_head_rms, obh, wa_ref[...])
            dob, dw2 = vjp2(dm_ref[:, GW + h * HD:GW + (h + 1) * HD])
            dwa = dwa + dw2
            dob_ref[:, sl] = dob
            dl_ref[:, sl] = jnp.broadcast_to(jnp.sum(dob * obh, axis=1, keepdims=True), (tm, HD))

        @pl.when(pl.program_id(0) == 0)
        def _():
            dwd_ref[...] = jnp.zeros_like(dwd_ref)
            dwa_ref[...] = jnp.zeros_like(dwa_ref)

        dwd_ref[...] += dwd
        dwa_ref[...] += dwa

    vec = pl.BlockSpec((1, HD), lambda i: (0, 0))
    wide = pl.BlockSpec((tm, GW), lambda i: (i, 0))
    return pl.pallas_call(
        body, grid=(t // tm,),
        in_specs=[pl.BlockSpec((tm, 2 * GW), lambda i: (i, 0)), wide, pl.BlockSpec((tm, GW), lambda i: (i, z_blk)),
                  wide, vec, vec, ANY],
        out_specs=[wide, pl.BlockSpec((tm, GW), lambda i: (i, z_blk)), wide, wide, vec, vec],
        out_shape=[_sds((t, GW), F32), _sds(proj.shape, BF16), _sds((t, GW), F32), _sds((t, GW), F32),
                   _sds((1, HD), F32), _sds((1, HD), F32)], name=name,
        compiler_params=_params(1))(dmixed, oa_raw, proj, ob, w_dn, w_an, dep)


def _gate_up_swiglu(name, h2, w_gu_g):
    t, d = h2.shape
    n = w_gu_g.shape[2]
    per = N_DEV // 2
    tm = min(512, t)

    def body(a_ref, bg_ref, bu_ref, gu_ref, act_ref):
        a = a_ref[...]
        g = _dot(a, bg_ref[...], 1, 0)
        up = _dot(a, bu_ref[...], 1, 0)
        gu_ref[0] = g
        gu_ref[1] = up
        act_ref[...] = (_silu(g) * up).astype(BF16)

    return pl.pallas_call(
        body, grid=(per, t // tm),
        in_specs=[pl.BlockSpec((tm, d), lambda j, i: (i, 0)), pl.BlockSpec((None, d, n), lambda j, i: (j, 0, 0)),
                  pl.BlockSpec((None, d, n), lambda j, i: (j + per, 0, 0))],
        out_specs=[pl.BlockSpec((2, tm, n), lambda j, i: (0, i, j)), pl.BlockSpec((tm, n), lambda j, i: (i, j))],
        out_shape=[_sds((2, t, per * n), F32), _sds((t, per * n), BF16)], name=name,
        compiler_params=_params(2))(h2, w_gu_g, w_gu_g)


def _d_gate_up(name, dy16, w_down, gu3, dep):
    t, d = dy16.shape
    f = w_down.shape[0]
    tm, tn = min(512, t), f // 4

    def body(a_ref, b_ref, g_ref, dep_ref, o_ref):
        dact = _dot(a_ref[...], b_ref[...], 1, 1)
        g, up = g_ref[0], g_ref[1]
        sg = _sigmoid(g)
        o_ref[0] = (dact * up * sg * (1.0 + g * (1.0 - sg))).astype(BF16)
        o_ref[1] = (dact * g * sg).astype(BF16)

    return pl.pallas_call(
        body, grid=(f // tn, t // tm),
        in_specs=[pl.BlockSpec((tm, d), lambda j, i: (i, 0)), pl.BlockSpec((tn, d), lambda j, i: (j, 0)),
                  pl.BlockSpec((2, tm, tn), lambda j, i: (0, i, j)), ANY],
        out_specs=pl.BlockSpec((2, tm, tn), lambda j, i: (0, i, j)), out_shape=_sds((2, t, f), BF16), name=name,
        compiler_params=_params(2))(dy16, w_down, gu3, dep)


def _out_proj_norm(name, mixed, w_out, x, w_norm):
    t, d = x.shape
    kdim = mixed.shape[1]
    tm = min(512, t)

    def body(a_ref, b_ref, x_ref, w_ref, x1_ref, h_ref):
        x1 = x_ref[...] + _dot(a_ref[...], b_ref[...], 1, 0)
        x1_ref[...] = x1
        h_ref[...] = _rms_f(x1, w_ref[...]).astype(BF16)

    row = pl.BlockSpec((tm, d), lambda i: (i, 0))
    return pl.pallas_call(
        body, grid=(t // tm,),
        in_specs=[pl.BlockSpec((tm, kdim), lambda i: (i, 0)), pl.BlockSpec((kdim, d), lambda i: (0, 0)), row,
                  pl.BlockSpec((1, d), lambda i: (0, 0))],
        out_specs=[row, row], out_shape=[_sds((t, d), F32), _sds((t, d), BF16)], name=name,
        compiler_params=_params(1))(mixed, w_out, x, w_norm)


def _down_loss(name, act, w_down, x1, target):
    t, f = act.shape
    d = x1.shape[1]
    tm, tn, nk = min(1024, t), 512, 2
    tk = f // nk

    def body(a_ref, b_ref, x_ref, t_ref, dy_ref, dy16_ref, l_ref, acc):
        i, j, k = pl.program_id(0), pl.program_id(1), pl.program_id(2)
        part = _dot(a_ref[...], b_ref[...], 1, 0)

        @pl.when(k == 0)
        def _():
            acc[...] = part

        @pl.when(k > 0)
        def _():
            acc[...] += part

        @pl.when(k == nk - 1)
        def _():
            diff = acc[...] + x_ref[...] - t_ref[...]
            dyv = diff * (1.0 / d)
            dy_ref[...] = dyv
            dy16_ref[...] = dyv.astype(BF16)
            tot = jnp.sum(jnp.sum(diff * diff, axis=1, keepdims=True), axis=0, keepdims=True) * (0.5 / d)

            @pl.when((i == 0) & (j == 0))
            def _():
                l_ref[...] = jnp.zeros_like(l_ref)

            l_ref[...] += jnp.broadcast_to(tot, (8, 128))

    tile = pl.BlockSpec((tm, tn), lambda i, j, k: (i, j))
    return pl.pallas_call(
        body, grid=(t // tm, d // tn, nk),
        in_specs=[pl.BlockSpec((tm, tk), lambda i, j, k: (i, k)), pl.BlockSpec((tk, tn), lambda i, j, k: (k, j)),
                  tile, tile],
        out_specs=[tile, tile, pl.BlockSpec((8, 128), lambda i, j, k: (0, 0))],
        out_shape=[_sds((t, d), F32), _sds((t, d), BF16), _sds((8, 128), F32)],
        scratch_shapes=[pltpu.VMEM((tm, tn), F32)], name=name,
        compiler_params=_params(3))(act, w_down, x1, target)


def _peer(me, k):
    pid = (me + k) % N_DEV
    return (pid // 4, (pid // 2) % 2, pid % 2)


def _my_id():
    return 4 * lax.axis_index("x") + 2 * lax.axis_index("y") + lax.axis_index("c")


def _exchange(name, arrays, scatter, dep):
    n = len(arrays)

    def body(*refs):
        ins, outs = refs[:n], refs[n + 1:2 * n + 1]
        send_sems, recv_sems, local_sems = refs[2 * n + 1:]
        me = _my_id()
        started = []
        for a in range(n):
            src = ins[a].at[me] if scatter[a] else ins[a]
            loc = pltpu.make_async_copy(src, outs[a].at[me], local_sems.at[a])
            loc.start()
            started.append(loc)
        remote = []
        for k in range(1, N_DEV):
            to = (me + k) % N_DEV
            for a in range(n):
                src = ins[a].at[to] if scatter[a] else ins[a]
                cp = pltpu.make_async_remote_copy(src_ref=src, dst_ref=outs[a].at[me],
                                                  send_sem=send_sems.at[a * (N_DEV - 1) + k - 1], recv_sem=recv_sems.at[a * (N_DEV - 1) + k - 1],
                                                  device_id=_peer(me, k), device_id_type=pl.DeviceIdType.MESH)
                cp.start()
                remote.append(cp)
        for k in range(1, N_DEV):
            frm = (me + N_DEV - k) % N_DEV
            for a in range(n):
                src = ins[a].at[frm] if scatter[a] else ins[a]
                pltpu.make_async_remote_copy(src_ref=src, dst_ref=outs[a].at[frm],
                                             send_sem=send_sems.at[a * (N_DEV - 1) + k - 1], recv_sem=recv_sems.at[a * (N_DEV - 1) + k - 1],
                                             device_id=_peer(me, k), device_id_type=pl.DeviceIdType.MESH).wait_recv()
        for cp in remote:
            cp.wait_send()
        for loc in started:
            loc.wait()

    out_shape = [_sds((N_DEV,) + (a.shape[1:] if sc else a.shape), a.dtype) for a, sc in zip(arrays, scatter)]
    return pl.pallas_call(
        body, in_specs=[ANY] * (n + 1), out_specs=[ANY] * n, out_shape=out_shape,
        scratch_shapes=[pltpu.SemaphoreType.DMA((n * (N_DEV - 1),)), pltpu.SemaphoreType.DMA((n * (N_DEV - 1),)),
                        pltpu.SemaphoreType.DMA((n,))],
        name=name)(*arrays, dep)


def _gather_two_level(name, arrays):
    n = len(arrays)
    per = N_DEV - 1

    def body(*refs):
        ins, outs = refs[:n], refs[n:2 * n]
        send_sems, recv_sems, local_sems = refs[2 * n:]
        x, y, c = lax.axis_index("x"), lax.axis_index("y"), lax.axis_index("c")
        me, sibling = (x, y, c), (x, y, 1 - c)
        chips = [(1 - x, y), (x, 1 - y), (1 - x, 1 - y)]

        def copy(a, k, block, to, src=None):
            slot = outs[a].at[4 * block[0] + 2 * block[1] + block[2]]
            return pltpu.make_async_remote_copy(
                src_ref=slot if src is None else src, dst_ref=slot, send_sem=send_sems.at[a * per + k],
                recv_sem=recv_sems.at[a * per + k], device_id=to, device_id_type=pl.DeviceIdType.MESH)

        mine = [pltpu.make_async_copy(ins[a], outs[a].at[4 * x + 2 * y + c], local_sems.at[a]) for a in range(n)]
        for cp in mine:
            cp.start()
        first = [copy(a, 0, me, sibling, src=ins[a]) for a in range(n)]
        first += [copy(a, 1 + j, me, (*chip, c), src=ins[a]) for j, chip in enumerate(chips) for a in range(n)]
        for cp in first:
            cp.start()
        passed = []
        for j, chip in enumerate(chips):
            for a in range(n):
                copy(a, 1 + j, (*chip, c), me).wait_recv()
                cp = copy(a, 4 + j, (*chip, c), sibling)
                cp.start()
                passed.append(cp)
        for a in range(n):
            copy(a, 0, sibling, me).wait_recv()
            for j, chip in enumerate(chips):
                copy(a, 4 + j, (*chip, 1 - c), me).wait_recv()
        for cp in first + passed:
            cp.wait_send()
        for cp in mine:
            cp.wait()

    return pl.pallas_call(
        body, in_specs=[ANY] * n, out_specs=[ANY] * n,
        out_shape=[_sds((N_DEV,) + a.shape, a.dtype) for a in arrays],
        scratch_shapes=[pltpu.SemaphoreType.DMA((n * per,)), pltpu.SemaphoreType.DMA((n * per,)),
                        pltpu.SemaphoreType.DMA((n,))],
        name=name)(*arrays)


HBM = pl.BlockSpec(memory_space=pltpu.HBM)
SEM = pl.BlockSpec(memory_space=pltpu.SEMAPHORE)
EFFECT = pltpu.SideEffectType.DATAFLOW_SIDE_EFFECTING


def _remote_copies(srcs, lands, scatter, send_sems, recv_sems, me, incoming):
    out = []
    for k in range(1, N_DEV):
        other = (me + N_DEV - k) % N_DEV if incoming else (me + k) % N_DEV
        for a in range(len(srcs)):
            sem = a * (N_DEV - 1) + k - 1
            src = srcs[a].at[other] if scatter[a] else srcs[a]
            dst = lands[a].at[other if incoming else me]
            out.append(pltpu.make_async_remote_copy(src_ref=src, dst_ref=dst, send_sem=send_sems.at[sem],
                                                    recv_sem=recv_sems.at[sem], device_id=_peer(me, k),
                                                    device_id_type=pl.DeviceIdType.MESH))
    return out


def _exchange_start(name, arrays, scatter, dep):
    n = len(arrays)
    lands = [lax.empty((N_DEV,) + (a.shape[1:] if sc else a.shape), a.dtype) for a, sc in zip(arrays, scatter)]

    def body(*refs):
        srcs, land_refs = refs[:n], refs[n:2 * n]
        send_sems, recv_sems = refs[2 * n + 1], refs[2 * n + 2]
        token = refs[-1]
        for cp in _remote_copies(srcs, land_refs, scatter, send_sems, recv_sems, _my_id(), False):
            cp.start()
        token[...] = jnp.zeros_like(token)

    n_sem = n * (N_DEV - 1)
    out_shape = ([pltpu.SemaphoreType.DMA((n_sem,)), pltpu.SemaphoreType.DMA((n_sem,))]
                 + [pltpu.HBM(a.shape, a.dtype) for a in arrays] + [pltpu.HBM(l.shape, l.dtype) for l in lands]
                 + [_sds((8, 128), F32)])
    aliases = {i: 2 + i for i in range(2 * n)}
    args = [pltpu.with_memory_space_constraint(a, pltpu.HBM) for a in list(arrays) + lands] + [dep]
    res = pl.pallas_call(
        body, name=name, in_specs=[HBM] * (2 * n) + [ANY], out_shape=out_shape,
        out_specs=[SEM, SEM] + [HBM] * (2 * n) + [pl.BlockSpec(memory_space=pltpu.VMEM)],
        input_output_aliases=aliases, compiler_params=pltpu.CompilerParams(has_side_effects=EFFECT))(*args)
    return dict(send=res[0], recv=res[1], srcs=res[2:2 + n], lands=res[2 + n:2 + 2 * n], token=res[-1],
                scatter=scatter)


def _exchange_wait(name, started, after):
    n = len(started["srcs"])
    scatter = started["scatter"]

    def body(*refs):
        srcs, land_refs = refs[:n], refs[n:2 * n]
        send_sems, recv_sems = refs[2 * n], refs[2 * n + 1]
        me = _my_id()
        for cp in _remote_copies(srcs, land_refs, scatter, send_sems, recv_sems, me, False):
            cp.wait_send()
        for cp in _remote_copies(srcs, land_refs, scatter, send_sems, recv_sems, me, True):
            cp.wait_recv()

    arrs = list(started["srcs"]) + list(started["lands"])
    res = pl.pallas_call(
        body, name=name, in_specs=[HBM] * (2 * n) + [SEM, SEM, ANY],
        out_shape=[pltpu.HBM(a.shape, a.dtype) for a in arrs], out_specs=[HBM] * (2 * n),
        input_output_aliases={i: i for i in range(2 * n)},
        compiler_params=pltpu.CompilerParams(has_side_effects=EFFECT))(*arrs, started["send"], started["recv"], after)
    me = _my_id()
    out = []
    for src, land, sc in zip(res[:n], res[n:], scatter):
        own = lax.dynamic_index_in_dim(src, me, 0, keepdims=True) if sc else src[None]
        out.append(lax.dynamic_update_slice(land, own, (me,) + (0,) * (land.ndim - 1)))
    return out


def _adamw(name, parts, w, m, v):
    r, c = w.shape
    tr = r
    for cand in (128, 88, 64, 40, 8):
        if r % cand == 0:
            tr = cand
            break
    c1 = 1.0 / (1.0 - ADAM_B1 ** ADAM_STEP)
    c2 = 1.0 / (1.0 - ADAM_B2 ** ADAM_STEP)

    def body(p_ref, w_ref, m_ref, v_ref, g_ref, d_ref, nm_ref, nv_ref):
        g = p_ref[0].astype(F32)
        for s in range(1, N_DEV):
            g = g + p_ref[s].astype(F32)
        mn = ADAM_B1 * m_ref[...] + (1.0 - ADAM_B1) * g
        vn = ADAM_B2 * v_ref[...] + (1.0 - ADAM_B2) * (g * g)
        g_ref[...] = g
        nm_ref[...] = mn
        nv_ref[...] = vn
        d_ref[...] = -ADAM_LR * ((mn * c1) / (jnp.sqrt(vn * c2) + ADAM_EPS) + ADAM_WD * w_ref[...])

    blk = pl.BlockSpec((tr, c), lambda i: (i, 0))
    return pl.pallas_call(
        body, grid=(r // tr,), in_specs=[pl.BlockSpec((N_DEV, tr, c), lambda i: (0, i, 0)), blk, blk, blk],
        out_specs=[blk] * 4, out_shape=[_sds((r, c), F32)] * 4, name=name,
        compiler_params=_params(1, VMEM_LIMIT))(parts, w, m, v)


def _pad_rows(a, rows):
    return jnp.pad(a, ((0, rows - a.shape[0]), (0, 0)))


def _lane_row(vec8, offset):
    return jnp.pad(vec8.reshape(1, 8), ((0, 0), (offset, HD - 8 - offset)))


def kernel(x, positions, attn_norm_w, w_in, conv_w, a_log, dt_bias, delta_out_norm_w, q_norm_w, k_norm_w, attn_out_norm_w, w_out, ffn_norm_w, w_gate_up, w_down, loss_target, m_attn_norm_w, m_w_in, m_conv_w, m_a_log, m_dt_bias, m_delta_out_norm_w, m_q_norm_w, m_k_norm_w, m_attn_out_norm_w, m_w_out, m_ffn_norm_w, m_w_gate_up, m_w_down, v_attn_norm_w, v_w_in, v_conv_w, v_a_log, v_dt_bias, v_delta_out_norm_w, v_q_norm_w, v_k_norm_w, v_attn_out_norm_w, v_w_out, v_ffn_norm_w, v_w_gate_up, v_w_down):
    x2 = x[0]
    t, d = x2.shape
    target = loss_target[0]
    pos_col = positions.reshape(t, 1)
    half = HD // 2
    inv = (ROPE_THETA ** (-np.arange(half, dtype=np.float32) / half)).astype(np.float32)
    inv_row = jnp.asarray(np.concatenate([inv, inv]).reshape(1, HD))

    n_in = w_in.shape[2]
    n_gu = w_gate_up.shape[2]
    w_in_g, conv_g = _gather_two_level("gather_in", [w_in[0].astype(BF16), _pad_rows(conv_w[0], 8)])
    out_fly = _exchange_start("gather_out_start", [w_out[0].astype(BF16)], [False], conv_g)
    gu_fly = _exchange_start("gather_gate_up_start", [w_gate_up[0].astype(BF16)], [False], out_fly["token"])
    down_fly = _exchange_start("gather_down_start", [w_down[0].astype(BF16)], [False], gu_fly["token"])
    n_main = 4 * GW
    n_small = 2 * N_HEADS
    segments = [(0, n_main, 0), (n_main + n_small, N_DEV * n_in, n_main), (n_main, n_main + n_small, 7 * GW)]
    pieces = []
    for lo, hi, _ in segments:
        f = lo
        while f < hi:
            j = f // n_in
            end = min(hi, (j + 1) * n_in)
            pieces.append(w_in_g[j][:, f - j * n_in:end - j * n_in])
            f = end
    w_cat = jnp.concatenate(pieces + [jnp.zeros((d, HD - n_small), BF16)], axis=1)
    n_cat = w_cat.shape[1]
    small_blk = (7 * GW) // HD
    conv_w8 =jnp.transpose(conv_g, (1, 0, 2)).reshape(8, 3 * GW)
    alog_row = _lane_row(a_log[0], 8)
    dtb_row = _lane_row(dt_bias[0], 8)

    tm = min(2048, t)
    h1 = _rms_fwd("norm1", x2, attn_norm_w, down_fly["token"])
    tn = 384
    proj = _mm("in_proj", h1, w_cat, grid=(t // tm, n_cat // tn, 1),
               a_spec=pl.BlockSpec((tm, d), lambda i, j, k: (i, 0)),
               b_spec=pl.BlockSpec((d, tn), lambda i, j, k: (0, j)),
               o_spec=pl.BlockSpec((tm, tn), lambda i, j, k: (i, j)),
               out_shape=_sds((t, n_cat), F32), ca=1, cb=0, nk=1)
    qn = _conv_fwd("conv_q", proj, conv_w8, 0, True, HD ** -0.5)
    kn = _conv_fwd("conv_k", proj, conv_w8, 1, True, 1.0)
    vv = _conv_fwd("conv_v", proj, conv_w8, 2, False, 1.0)
    beta_b, gc_b = _gates_fwd("gates", proj, small_blk, alog_row, dtb_row)
    u, w, p, tinv, qd, kd = _delta_prep("delta_prep", qn, kn, vv, beta_b, gc_b)
    oa_raw, vn, s_hist = _delta_scan("delta_scan", u, w, p, qd, kd, gc_b)

    aq = _qk_fwd("attn_q", proj, 4, q_norm_w, pos_col, inv_row)
    ak = _qk_fwd("attn_k", proj, 5, k_norm_w, pos_col, inv_row)
    ob, lse = _attn_fwd("attn_fwd", aq, ak, proj, 6)
    mixed = _mix_fwd("mix", oa_raw, proj, 3, ob, delta_out_norm_w, attn_out_norm_w)
    (w_out_g,) = _exchange_wait("gather_out_wait", out_fly, mixed)
    w_out_full = w_out_g.reshape(2 * GW, d)
    tn = 512
    x1, h2 = _out_proj_norm("out_proj", mixed, w_out_full, x2, ffn_norm_w)
    per = N_DEV // 2
    (w_gu_g,) = _exchange_wait("gather_gate_up_wait", gu_fly, h2)
    gu3, act = _gate_up_swiglu("gate_up", h2, w_gu_g)
    (w_down_g,) = _exchange_wait("gather_down_wait", down_fly, act)
    w_down_full = w_down_g.reshape(D_FF, d)
    tmd = min(1024, t)
    dy, dy16, loss_tile = _down_loss("down_proj", act, w_down_full, x1, target)
    loss = lax.psum(loss_tile[0, 0], ("x", "y", "c"))

    tk = min(2048, t)
    nkt = t // tk
    g_down = _mm("g_down", act, dy16, dep=loss.reshape(1, 1), grid=(D_FF // 512, 1, nkt),
                 a_spec=pl.BlockSpec((tk, 512), lambda i, j, k: (k, i)),
                 b_spec=pl.BlockSpec((tk, d), lambda i, j, k: (k, 0)),
                 o_spec=pl.BlockSpec((512, d), lambda i, j, k: (i, 0)),
                 out_shape=_sds((D_FF, d), F32), ca=0, cb=0, nk=nkt)
    down_g_fly = _exchange_start("reduce_down_start", [g_down.reshape(N_DEV, D_FF // N_DEV, d)], [True], dy16)
    dgu3 = _d_gate_up("d_gate_up", dy16, w_down_full, gu3, down_g_fly["token"])
    g_gu = _mm("g_gate_up", h2, dgu3, grid=(d // 512, N_DEV, nkt),
               a_spec=pl.BlockSpec((tk, 512), lambda i, j, k: (k, i)),
               b_spec=pl.BlockSpec((None, tk, n_gu), lambda i, j, k: (j // per, k, j % per)),
               o_spec=pl.BlockSpec((None, 512, n_gu), lambda i, j, k: (j, i, 0)),
               out_shape=_sds((N_DEV, d, n_gu), F32), ca=0, cb=0, nk=nkt)
    gu_g_fly = _exchange_start("reduce_gate_up_start", [g_gu], [True], dy16)
    tmh, tnh = min(2048, t), 1024
    dh2 = _mm("d_h2", dgu3, w_gu_g, dep=gu_g_fly["token"], grid=(t // tmh, d // tnh, N_DEV),
              a_spec=pl.BlockSpec((None, tmh, n_gu), lambda i, j, k: (k // per, i, k % per)),
              b_spec=pl.BlockSpec((None, tnh, n_gu), lambda i, j, k: (k, j, 0)),
              o_spec=pl.BlockSpec((tmh, tnh), lambda i, j, k: (i, j)),
              out_shape=_sds((t, d), F32), ca=1, cb=1, nk=N_DEV)
    dx1, dx1_16, g_ffn_norm = _rms_bwd("norm2_bwd", x1, ffn_norm_w, dh2, dy)

    g_out = _mm("g_out", mixed, dx1_16, grid=((2 * GW) // 512, 1, nkt),
                a_spec=pl.BlockSpec((tk, 512), lambda i, j, k: (k, i)),
                b_spec=pl.BlockSpec((tk, d), lambda i, j, k: (k, 0)),
                o_spec=pl.BlockSpec((512, d), lambda i, j, k: (i, 0)),
                out_shape=_sds((2 * GW, d), F32), ca=0, cb=0, nk=nkt)
    out_g_fly = _exchange_start("reduce_out_start", [g_out.reshape(N_DEV, (2 * GW) // N_DEV, d)], [True], g_ffn_norm)
    dmixed = _mm("d_mixed", dx1_16, w_out_full, dep=out_g_fly["token"], grid=(t // tm, (2 * GW) // tn, 1),
                 a_spec=pl.BlockSpec((tm, d), lambda i, j, k: (i, 0)),
                 b_spec=pl.BlockSpec((tn, d), lambda i, j, k: (j, 0)),
                 o_spec=pl.BlockSpec((tm, tn), lambda i, j, k: (i, j)),
                 out_shape=_sds((t, 2 * GW), F32), ca=1, cb=1, nk=1)
    doa, dproj, dob, delta, g_dn, g_an = _mix_bwd("mix_bwd", dmixed, oa_raw, proj, 3, ob,
                                                  delta_out_norm_w, attn_out_norm_w, out_g_fly["token"])
    d_aq, d_ak, d_av = _attn_bwd("attn_bwd", aq, ak, proj, 6, dob, lse, delta)
    dproj, g_qn = _qk_bwd("attn_q_bwd", proj, 4, q_norm_w, pos_col, inv_row, d_aq, dproj)
    dproj, g_kn = _qk_bwd("attn_k_bwd", proj, 5, k_norm_w, pos_col, inv_row, d_ak, dproj)
    dproj = _cast_into("attn_v_bwd", d_av, dproj, 6)

    dvn, dqd, dkd, dw, ddec = _delta_scan_bwd("delta_scan_bwd", doa, w, p, qd, kd, gc_b, vn, s_hist)
    dqn, dkn, dvv, dbeta_b, dg_b = _delta_prep_bwd("delta_prep_bwd", qn, kn, vv, beta_b, gc_b, tinv, u, w, vn,
                                                   doa, dvn, dqd, dkd, dw, ddec)
    dproj, gcw_q = _conv_bwd("conv_q_bwd", proj, conv_w8, dqn, dproj, 0, True, HD ** -0.5)
    dproj, gcw_k = _conv_bwd("conv_k_bwd", proj, conv_w8, dkn, dproj, 1, True, 1.0)
    dproj, gcw_v = _conv_bwd("conv_v_bwd", proj, conv_w8, dvv, dproj, 2, False, 1.0)
    dproj, g_alog_row, g_dtb_row = _gates_bwd("gates_bwd", proj, small_blk, alog_row, dtb_row, dbeta_b, dg_b, dproj)
    tnc = n_cat // 3
    g_cat = _mm("g_in", h1, dproj, grid=(d // 512, 3, nkt),
                a_spec=pl.BlockSpec((tk, 512), lambda i, j, k: (k, i)),
                b_spec=pl.BlockSpec((tk, tnc), lambda i, j, k: (k, j)),
                o_spec=pl.BlockSpec((512, tnc), lambda i, j, k: (i, j)),
                out_shape=_sds((d, n_cat), F32), ca=0, cb=0, nk=nkt)
    parts = []
    for j in range(N_DEV):
        cols = []
        for lo, hi, start in sorted(segments):
            a, b = max(lo, j * n_in), min(hi, (j + 1) * n_in)
            if a < b:
                cols.append(g_cat[:, start + a - lo:start + b - lo])
        parts.append(cols[0] if len(cols) == 1 else jnp.concatenate(cols, axis=1))
    g_in_parts = jnp.stack(parts).astype(BF16)
    g_conv = jnp.concatenate([gcw_q, gcw_k, gcw_v], axis=1)
    n_cw = conv_w.shape[2]
    g_conv_parts = jnp.transpose(g_conv.reshape(8, N_DEV, n_cw), (1, 0, 2))
    in_g_fly = _exchange_start("reduce_in_start", [g_in_parts, g_conv_parts], [True] * 2, g_dtb_row)
    tkc = n_cat // 3
    dh1 = _mm("d_h1", dproj, w_cat, dep=in_g_fly["token"], grid=(t // tmd, d // tn, 3),
              a_spec=pl.BlockSpec((tmd, tkc), lambda i, j, k: (i, k)),
              b_spec=pl.BlockSpec((tn, tkc), lambda i, j, k: (j, k)),
              o_spec=pl.BlockSpec((tmd, tn), lambda i, j, k: (i, j)),
              out_shape=_sds((t, d), F32), ca=1, cb=1, nk=3)
    grad_x, _, g_attn_norm = _rms_bwd("norm1_bwd", x2, attn_norm_w, dh1, dx1)

    small_rows = [g_attn_norm.reshape(d // HD, HD), g_ffn_norm.reshape(d // HD, HD), g_dn, g_qn, g_kn, g_an,
                  g_alog_row, g_dtb_row]
    small_pack = _pad_rows(jnp.concatenate(small_rows, axis=0), 40)
    (r_down,) = _exchange_wait("reduce_down_wait", down_g_fly, grad_x)
    (r_gu,) = _exchange_wait("reduce_gate_up_wait", gu_g_fly, grad_x)
    (r_out,) = _exchange_wait("reduce_out_wait", out_g_fly, grad_x)
    res_gu = [a[None] for a in _adamw("adamw_gate_up", r_gu, w_gate_up[0], m_w_gate_up[0], v_w_gate_up[0])]
    res_down = [a[None] for a in _adamw("adamw_down", r_down, w_down[0], m_w_down[0], v_w_down[0])]
    res_out = [a[None] for a in _adamw("adamw_out", r_out, w_out[0], m_w_out[0], v_w_out[0])]
    done = (res_gu[3][0, :1, :1] + res_down[3][0, :1, :1] + res_out[3][0, :1, :1])
    (r_small,) = _exchange("gather_small_grads", [small_pack], [False], done)

    def pack_small(an, fn, dn, qn_, kn_, aon, al, db):
        rows = [an.reshape(d // HD, HD), fn.reshape(d // HD, HD), dn, qn_, kn_, aon,
                _lane_row(al[0], 8), _lane_row(db[0], 8)]
        return _pad_rows(jnp.concatenate(rows, axis=0), 40)

    def unpack_small(pk):
        nr = d // HD
        return dict(attn_norm_w=pk[:nr].reshape(1, d), ffn_norm_w=pk[nr:2 * nr].reshape(1, d),
                    delta_out_norm_w=pk[2 * nr:2 * nr + 1], q_norm_w=pk[2 * nr + 1:2 * nr + 2],
                    k_norm_w=pk[2 * nr + 2:2 * nr + 3], attn_out_norm_w=pk[2 * nr + 3:2 * nr + 4],
                    a_log=pk[2 * nr + 4:2 * nr + 5, 8:16], dt_bias=pk[2 * nr + 5:2 * nr + 6, 8:16])

    res_small = _adamw("adamw_small", r_small,
                       pack_small(attn_norm_w, ffn_norm_w, delta_out_norm_w, q_norm_w, k_norm_w, attn_out_norm_w, a_log, dt_bias),
                       pack_small(m_attn_norm_w, m_ffn_norm_w, m_delta_out_norm_w, m_q_norm_w, m_k_norm_w, m_attn_out_norm_w, m_a_log, m_dt_bias),
                       pack_small(v_attn_norm_w, v_ffn_norm_w, v_delta_out_norm_w, v_q_norm_w, v_k_norm_w, v_attn_out_norm_w, v_a_log, v_dt_bias))
    small = [unpack_small(a) for a in res_small]
    r_in, r_conv = _exchange_wait("reduce_in_wait", in_g_fly, res_small[0])
    res_in = [a[None] for a in _adamw("adamw_in", r_in, w_in[0], m_w_in[0], v_w_in[0])]
    res_conv =[a[None, :4] for a in _adamw("adamw_conv", r_conv, _pad_rows(conv_w[0], 8), _pad_rows(m_conv_w[0], 8),
                                            _pad_rows(v_conv_w[0], 8))]

    outs = [loss, grad_x[None]]
    for i in range(4):
        s = small[i]
        outs += [s["attn_norm_w"], res_in[i], res_conv[i], s["a_log"], s["dt_bias"], s["delta_out_norm_w"],
                 s["q_norm_w"], s["k_norm_w"], s["attn_out_norm_w"], res_out[i], s["ffn_norm_w"], res_gu[i],
                 res_down[i]]
    return tuple(outs)
```

```python
import functools

import numpy as np
import jax
import jax.numpy as jnp
from jax import lax
from jax.experimental import pallas as pl
from jax.experimental.pallas import tpu as pltpu

F32 = jnp.float32
BF16 = jnp.bfloat16

N_DEV = 8
N_HEADS = 8
HD = 128
GW = N_HEADS * HD
CHUNK = 64
PAIR = 2 * CHUNK
SPAN = 128
DILATIONS = (1, 4, 16)
ROPE_THETA = 10000.0
EPS = 1e-6
D_FF = 5632
ADAM_LR, ADAM_B1, ADAM_B2, ADAM_EPS, ADAM_WD, ADAM_STEP = 0.001, 0.9, 0.999, 1e-8, 0.01, 10
NEG = -1e30
VMEM_LIMIT = 56 * 1024 * 1024
ANY = pl.BlockSpec(memory_space=pl.ANY)
HEADS_PER_STEP = 8


def _params(n_grid, vmem=VMEM_LIMIT):
    return pltpu.CompilerParams(dimension_semantics=("arbitrary",) * n_grid, vmem_limit_bytes=vmem)


def _sds(shape, dtype):
    return jax.ShapeDtypeStruct(tuple(shape), dtype)


def _sigmoid(x):
    return 1.0 / (1.0 + jnp.exp(-x))


def _silu(x):
    return x * _sigmoid(x)


def _softplus(x):
    return jnp.maximum(x, 0.0) + jnp.log(1.0 + jnp.exp(-jnp.abs(x)))


def _dot(a, b, ca, cb, precision=None):
    return lax.dot_general(a, b, (((ca,), (cb,)), ((), ())), precision=precision,
                           preferred_element_type=F32)


def _b16(x):
    return x if x.dtype == BF16 else x.astype(BF16)


def _split(x):
    hi = x.astype(BF16)
    return hi, (x - hi.astype(F32)).astype(BF16)


def _dot3(a, b, ca, cb):
    a_hi, a_lo = _split(a)
    b_hi, b_lo = _split(b)
    return _dot(a_hi, b_hi, ca, cb) + (_dot(a_hi, b_lo, ca, cb) + _dot(a_lo, b_hi, ca, cb))


def _iota2(shape, axis):
    return lax.broadcasted_iota(jnp.int32, shape, axis)


def _mm(name, a, b, *, grid, a_spec, b_spec, o_spec, out_shape, ca, cb, nk, add=None, add_spec=None,
        dep=None, vmem=VMEM_LIMIT):
    has_add = add is not None
    n_in = 2 + has_add + (dep is not None)

    def body(*refs):
        a_ref, b_ref = refs[0], refs[1]
        e_ref = refs[2] if has_add else None
        o_ref = refs[n_in]
        part = _dot(_b16(a_ref[...]), _b16(b_ref[...]), ca, cb)
        if nk == 1:
            if has_add:
                part = part + e_ref[...]
            o_ref[...] = part.astype(o_ref.dtype)
            return
        acc = refs[-1]
        k = pl.program_id(2)

        @pl.when(k == 0)
        def _():
            acc[...] = part

        @pl.when(k > 0)
        def _():
            acc[...] += part

        @pl.when(k == nk - 1)
        def _():
            res = acc[...]
            if has_add:
                res = res + e_ref[...]
            o_ref[...] = res.astype(o_ref.dtype)

    in_specs = [a_spec, b_spec] + ([add_spec] if has_add else []) + ([ANY] if dep is not None else [])
    args = (a, b) + ((add,) if has_add else ()) + ((dep,) if dep is not None else ())
    blk = [d for d in o_spec.block_shape if d is not None]
    scratch = [pltpu.VMEM(tuple(blk), F32)] if nk > 1 else []
    return pl.pallas_call(body, grid=grid, in_specs=in_specs, out_specs=o_spec, out_shape=out_shape,
                          scratch_shapes=scratch, name=name, compiler_params=_params(3, vmem))(*args)


def _rms_f(xv, wv):
    return xv * lax.rsqrt(jnp.mean(xv * xv, axis=-1, keepdims=True) + EPS) * wv


def _rms_fwd(name, x, w, dep):
    t, d = x.shape
    tm = min(512, t)

    def body(x_ref, w_ref, dep_ref, o_ref):
        o_ref[...] = _rms_f(x_ref[...], w_ref[...]).astype(BF16)

    row = pl.BlockSpec((tm, d), lambda i: (i, 0))
    vec = pl.BlockSpec((1, d), lambda i: (0, 0))
    return pl.pallas_call(body, grid=(t // tm,), in_specs=[row, vec, ANY], out_specs=row,
                          out_shape=_sds((t, d), BF16), name=name, compiler_params=_params(1))(x, w, dep)


def _rms_bwd(name, x, w, dh, res):
    t, d = x.shape
    tm = min(256, t)

    def body(x_ref, w_ref, dh_ref, res_ref, dx_ref, dx16_ref, dw_ref):
        _, vjp = jax.vjp(_rms_f, x_ref[...], w_ref[...])
        dxv, dwv = vjp(dh_ref[...])
        dxv = dxv + res_ref[...]
        dx_ref[...] = dxv
        dx16_ref[...] = dxv.astype(BF16)

        @pl.when(pl.program_id(0) == 0)
        def _():
            dw_ref[...] = jnp.zeros_like(dw_ref)

        dw_ref[...] += dwv

    row = pl.BlockSpec((tm, d), lambda i: (i, 0))
    vec = pl.BlockSpec((1, d), lambda i: (0, 0))
    return pl.pallas_call(body, grid=(t // tm,), in_specs=[row, vec, row, row], out_specs=[row, row, vec],
                          out_shape=[_sds((t, d), F32), _sds((t, d), BF16), _sds((1, d), F32)], name=name,
                          compiler_params=_params(1))(x, w, dh, res)


def _conv_taps(xv, w_ref, rows):
    c = w_ref[3:4, :] * xv
    for s in (1, 2, 3):
        c = c + w_ref[3 - s:4 - s, :] * jnp.where(rows >= s, pltpu.roll(xv, s, 0), 0.0)
    return c


def _post_conv(c, l2, scale):
    y = _silu(c)
    if l2:
        y = y * lax.rsqrt(jnp.sum(y * y, axis=-1, keepdims=True) + EPS) * scale
    return y


def _conv_fwd(name, proj, conv_w8, group, l2, scale):
    t = proj.shape[0]

    def body(x_ref, w_ref, o_ref):
        rows = _iota2((t, HD), 0)
        o_ref[...] = _post_conv(_conv_taps(x_ref[...], w_ref, rows), l2, scale)

    return pl.pallas_call(
        body, grid=(N_HEADS,),
        in_specs=[pl.BlockSpec((t, HD), lambda h: (0, h + group * N_HEADS)),
                  pl.BlockSpec((8, HD), lambda h: (0, h + group * N_HEADS))],
        out_specs=pl.BlockSpec((t, HD), lambda h: (0, h)),
        out_shape=_sds((t, GW), F32), name=name, compiler_params=_params(1, VMEM_LIMIT))(proj, conv_w8)


def _conv_bwd(name, proj, conv_w8, dn, dproj, group, l2, scale):
    t = proj.shape[0]

    def body(x_ref, w_ref, dn_ref, dproj_ref, dx_ref, dw_ref):
        rows = _iota2((t, HD), 0)
        xv = x_ref[...]
        c = _conv_taps(xv, w_ref, rows)
        _, vjp = jax.vjp(lambda cc: _post_conv(cc, l2, scale), c)
        (dc,) = vjp(dn_ref[...])
        dx = w_ref[3:4, :] * dc
        dw = jnp.zeros((8, HD), F32)
        rid = _iota2((8, HD), 0)
        dw = dw + jnp.where(rid == 3, jnp.sum(dc * xv, axis=0, keepdims=True), 0.0)
        for s in (1, 2, 3):
            dx = dx + w_ref[3 - s:4 - s, :] * jnp.where(rows < t - s, pltpu.roll(dc, t - s, 0), 0.0)
            xs = jnp.where(rows >= s, pltpu.roll(xv, s, 0), 0.0)
            dw = dw + jnp.where(rid == 3 - s, jnp.sum(dc * xs, axis=0, keepdims=True), 0.0)
        dx_ref[...] = dx.astype(BF16)
        dw_ref[...] = dw

    return pl.pallas_call(
        body, grid=(N_HEADS,),
        in_specs=[pl.BlockSpec((t, HD), lambda h: (0, h + group * N_HEADS)),
                  pl.BlockSpec((8, HD), lambda h: (0, h + group * N_HEADS)),
                  pl.BlockSpec((t, HD), lambda h: (0, h)), ANY],
        out_specs=[pl.BlockSpec((t, HD), lambda h: (0, h + group * N_HEADS)), pl.BlockSpec((8, HD), lambda h: (0, h))],
        out_shape=[_sds(dproj.shape, BF16), _sds((8, GW), F32)], input_output_aliases={3: 0}, name=name,
        compiler_params=_params(1, VMEM_LIMIT))(proj, conv_w8, dn, dproj)


def _chunk_cumsum(g, rows):
    pos = rows % CHUNK
    s = 1
    while s < CHUNK:
        g = g + jnp.where(pos >= s, pltpu.roll(g, s, 0), 0.0)
        s *= 2
    return g


def _gates_fwd(name, proj, small_blk, alog_row, dtb_row):
    t = proj.shape[0]
    tm = min(256, t)

    def body(s_ref, a_ref, b_ref, beta_ref, gc_ref):
        sm = s_ref[...]
        beta = _sigmoid(sm)
        g = -jnp.exp(a_ref[...]) * _softplus(sm + b_ref[...])
        gc = _chunk_cumsum(g, _iota2((tm, HD), 0))
        lane = _iota2((tm, HD), 1)
        for h in range(N_HEADS):
            bcol = jnp.sum(jnp.where(lane == h, beta, 0.0), axis=1, keepdims=True)
            gcol = jnp.sum(jnp.where(lane == 8 + h, gc, 0.0), axis=1, keepdims=True)
            beta_ref[:, h * HD:(h + 1) * HD] = jnp.broadcast_to(bcol, (tm, HD))
            gc_ref[:, h * HD:(h + 1) * HD] = jnp.broadcast_to(gcol, (tm, HD))

    vec = pl.BlockSpec((1, HD), lambda i: (0, 0))
    wide = pl.BlockSpec((tm, GW), lambda i: (i, 0))
    return pl.pallas_call(
        body, grid=(t // tm,),
        in_specs=[pl.BlockSpec((tm, HD), lambda i: (i, small_blk)), vec, vec], out_specs=[wide, wide],
        out_shape=[_sds((t, GW), F32), _sds((t, GW), F32)], name=name,
        compiler_params=_params(1))(proj, alog_row, dtb_row)


def _gates_bwd(name, proj, small_blk, alog_row, dtb_row, dbeta_b, dg_b, dproj):
    t = proj.shape[0]
    tm = min(256, t)

    def body(s_ref, a_ref, b_ref, db_ref, dg_ref, dproj_ref, ds_ref, da_ref, dbias_ref):
        sm = s_ref[...]
        lane = _iota2((tm, HD), 1)
        db = jnp.zeros((tm, HD), F32)
        dg = jnp.zeros((tm, HD), F32)
        for h in range(N_HEADS):
            db = db + jnp.where(lane == h, db_ref[:, h * HD:(h + 1) * HD], 0.0)
            dg = dg + jnp.where(lane == 8 + h, dg_ref[:, h * HD:(h + 1) * HD], 0.0)
        beta = _sigmoid(sm)
        ea = jnp.exp(a_ref[...])
        pre = sm + b_ref[...]
        g = -ea * _softplus(pre)
        dpre = dg * (-ea) * _sigmoid(pre)
        ds_ref[...] = (db * beta * (1.0 - beta) + dpre).astype(BF16)

        @pl.when(pl.program_id(0) == 0)
        def _():
            da_ref[...] = jnp.zeros_like(da_ref)
            dbias_ref[...] = jnp.zeros_like(dbias_ref)

        da_ref[...] += jnp.sum(dg * g, axis=0, keepdims=True)
        dbias_ref[...] += jnp.sum(dpre, axis=0, keepdims=True)

    vec = pl.BlockSpec((1, HD), lambda i: (0, 0))
    wide = pl.BlockSpec((tm, GW), lambda i: (i, 0))
    return pl.pallas_call(
        body, grid=(t // tm,),
        in_specs=[pl.BlockSpec((tm, HD), lambda i: (i, small_blk)), vec, vec, wide, wide, ANY],
        out_specs=[pl.BlockSpec((tm, HD), lambda i: (i, small_blk)), vec, vec],
        out_shape=[_sds(dproj.shape, BF16), _sds((1, HD), F32), _sds((1, HD), F32)],
        input_output_aliases={5: 0}, name=name,
        compiler_params=_params(1))(proj, alog_row, dtb_row, dbeta_b, dg_b, dproj)


def _pair_masks():
    ii = _iota2((PAIR, PAIR), 0)
    jj = _iota2((PAIR, PAIR), 1)
    same = (ii // CHUNK) == (jj // CHUNK)
    return ii, jj, same & (ii >= jj), same & (ii > jj)


def _to_row(col_b, ii, jj):
    return jnp.sum(jnp.where(ii == jj, col_b, 0.0), axis=0, keepdims=True)


def _to_col(row, ii, jj):
    return jnp.sum(jnp.where(ii == jj, jnp.broadcast_to(row, (PAIR, PAIR)), 0.0), axis=1, keepdims=True)


def _decay_parts(gc, last_a, last_b, ii, jj, causal):
    diff = gc - _to_row(gc, ii, jj)
    dmat = jnp.where(causal, jnp.exp(jnp.where(causal, diff, 0.0)), 0.0)
    glast = jnp.where(ii < CHUNK, last_a, last_b)
    return dmat, jnp.exp(gc), jnp.exp(glast - gc)


def _unit_lower_inverse(lows, ii, jj):
    eye = jnp.where(ii == jj, 1.0, 0.0)
    mm = lambda xs, ys: [_dot3(a, b, 1, 0) for a, b in zip(xs, ys)]
    plus = lambda xs: [eye + a for a in xs]
    minus = lambda xs: [eye - a for a in xs]
    d1 = [jnp.where((ii // 16) == (jj // 16), low, 0.0) for low in lows]
    d2 = mm(d1, d1)
    a = mm(minus(d1), plus(d2))
    d4 = mm(d2, d2)
    a = mm(a, plus(d4))
    d8 = mm(d4, d4)
    td = mm(a, plus(d8))
    n1 = mm(td, [low - d for low, d in zip(lows, d1)])
    n2 = mm(n1, n1)
    return mm(mm(minus(n1), plus(n2)), td)


def _delta_prep(name, qn, kn, vv, beta_b, gc_b):
    t = qn.shape[0]

    def body(q_ref, k_ref, v_ref, b_ref, g_ref, u_ref, w_ref, p_ref, t_ref, qd_ref, kd_ref):
        ii, jj, causal, strict = _pair_masks()
        sls = [slice(hh * HD, (hh + 1) * HD) for hh in range(HEADS_PER_STEP)]
        lows = []
        for sl in sls:
            q, k, beta = q_ref[:, sl], k_ref[:, sl], b_ref[:, sl]
            dmat, gam, e2 = _decay_parts(g_ref[:, sl], g_ref[CHUNK - 1:CHUNK, sl], g_ref[PAIR - 1:PAIR, sl],
                                         ii, jj, causal)
            k16 = _b16(k)
            lows.append(jnp.where(strict, beta * _dot(k16, k16, 1, 1) * dmat, 0.0))
            p_ref[:, sl] = jnp.where(causal, _dot(_b16(q), k16, 1, 1) * dmat, 0.0).astype(BF16)
            qd_ref[:, sl] = (q * gam).astype(BF16)
            kd_ref[:, sl] = (k * e2).astype(BF16)
        for sl, tinv in zip(sls, _unit_lower_inverse(lows, ii, jj)):
            beta = b_ref[:, sl]
            t_ref[:, sl] = tinv
            u_ref[:, sl] = _dot3(tinv, v_ref[:, sl] * beta, 1, 0)
            w_ref[:, sl] = _dot3(tinv, k_ref[:, sl] * (beta * jnp.exp(g_ref[:, sl])), 1, 0).astype(BF16)

    blk = pl.BlockSpec((PAIR, HEADS_PER_STEP * HD), lambda i, h: (i, h))
    return pl.pallas_call(
        body, grid=(t // PAIR, N_HEADS // HEADS_PER_STEP), in_specs=[blk] * 5, out_specs=[blk] * 6,
        out_shape=[_sds((t, GW), F32), _sds((t, GW), BF16), _sds((t, GW), BF16), _sds((t, GW), F32),
                   _sds((t, GW), BF16), _sds((t, GW), BF16)],
        name=name, compiler_params=_params(2))(qn, kn, vv, beta_b, gc_b)


def _delta_scan(name, u, w, p, qd, kd, gc_b):
    t = u.shape[0]
    n = t // CHUNK

    def body(u_ref, w_ref, p_ref, qd_ref, kd_ref, g_ref, o_ref, vn_ref, sh_ref, state):
        @pl.when(pl.program_id(0) == 0)
        def _():
            state[...] = jnp.zeros_like(state)

        sls = [slice(h * HD, (h + 1) * HD) for h in range(N_HEADS)]
        heads = range(N_HEADS)
        s = [state[h] for h in heads]
        for h in heads:
            sh_ref[h] = s[h]
        s16 = [_b16(a) for a in s]
        ws = [_dot(w_ref[:, sls[h]], s16[h], 1, 0) for h in heads]
        qs = [_dot(qd_ref[:, sls[h]], s16[h], 1, 0) for h in heads]
        vn16 = [_b16(u_ref[:, sls[h]] - ws[h]) for h in heads]
        pv = [_dot(p_ref[:, sls[h]], jnp.concatenate([vn16[h], vn16[h]], axis=0), 1, 0) for h in heads]
        kv = [_dot(kd_ref[:, sls[h]], vn16[h], 0, 0) for h in heads]
        for h in heads:
            o_ref[:, sls[h]] = qs[h] + pv[h]
            vn_ref[:, sls[h]] = vn16[h]
            state[h] = s[h] * jnp.exp(g_ref[CHUNK - 1:CHUNK, sls[h]]) + kv[h]

    blk = pl.BlockSpec((CHUNK, GW), lambda i: (i, 0))
    return pl.pallas_call(
        body, grid=(n,), in_specs=[blk] * 6,
        out_specs=[blk, blk, pl.BlockSpec((None, N_HEADS, HD, HD), lambda i: (i, 0, 0, 0))],
        out_shape=[_sds((t, GW), F32), _sds((t, GW), BF16), _sds((n, N_HEADS, HD, HD), F32)],
        scratch_shapes=[pltpu.VMEM((N_HEADS, HD, HD), F32)], name=name,
        compiler_params=_params(1))(u, w, p, qd, kd, gc_b)


def _delta_scan_bwd(name, do, w, p, qd, kd, gc_b, vn, s_hist):
    t = do.shape[0]
    n = t // CHUNK

    def body(do_ref, w_ref, p_ref, qd_ref, kd_ref, g_ref, vn_ref, sh_ref,
             dvn_ref, dqd_ref, dkd_ref, dw_ref, ddec_ref, dstate):
        @pl.when(pl.program_id(0) == 0)
        def _():
            dstate[...] = jnp.zeros_like(dstate)

        sls = [slice(h * HD, (h + 1) * HD) for h in range(N_HEADS)]
        heads = range(N_HEADS)
        ds = [dstate[h] for h in heads]
        ds16 = [_b16(a) for a in ds]
        s16 = [_b16(sh_ref[h]) for h in heads]
        do16 = [_b16(do_ref[:, sls[h]]) for h in heads]
        ptdo = [_dot(p_ref[:, sls[h]], do16[h], 0, 0) for h in heads]
        kds = [_dot(kd_ref[:, sls[h]], ds16[h], 1, 0) for h in heads]
        qdo = [_dot(qd_ref[:, sls[h]], do16[h], 0, 0) for h in heads]
        for h in heads:
            dqd_ref[:, sls[h]] = _dot(do16[h], s16[h], 1, 1)
            dkd_ref[:, sls[h]] = _dot(vn_ref[:, sls[h]], ds16[h], 1, 1)
        dvn = [ptdo[h][:CHUNK, :] + ptdo[h][CHUNK:, :] + kds[h] for h in heads]
        dvn16 = [_b16(a) for a in dvn]
        wdv = [_dot(w_ref[:, sls[h]], dvn16[h], 0, 0) for h in heads]
        for h in heads:
            dvn_ref[:, sls[h]] = dvn[h]
            dw_ref[:, sls[h]] = -_dot(dvn16[h], s16[h], 1, 1)
            tot = jnp.sum(jnp.sum(sh_ref[h] * ds[h], axis=1, keepdims=True), axis=0, keepdims=True)
            ddec_ref[:, sls[h]] = jnp.broadcast_to(tot, (8, HD))
            dstate[h] = ds[h] * jnp.exp(g_ref[CHUNK - 1:CHUNK, sls[h]]) + qdo[h] - wdv[h]

    blk = pl.BlockSpec((CHUNK, GW), lambda i: (n - 1 - i, 0))
    return pl.pallas_call(
        body, grid=(n,),
        in_specs=[blk] * 7 + [pl.BlockSpec((None, N_HEADS, HD, HD), lambda i: (n - 1 - i, 0, 0, 0))],
        out_specs=[blk] * 4 + [pl.BlockSpec((8, GW), lambda i: (n - 1 - i, 0))],
        out_shape=[_sds((t, GW), F32)] * 4 + [_sds((n * 8, GW), F32)],
        scratch_shapes=[pltpu.VMEM((N_HEADS, HD, HD), F32)], name=name,
        compiler_params=_params(1))(do, w, p, qd, kd, gc_b, vn, s_hist)


def _delta_prep_bwd(name, qn, kn, vv, beta_b, gc_b, tinv, u, w, vn, do, dvn, dqd, dkd, dw, ddec):
    t = qn.shape[0]

    def body(q_ref, k_ref, v_ref, b_ref, g_ref, t_ref, u_ref, w_ref, vn_ref, do_ref, dvn_ref, dqd_ref,
             dkd_ref, dw_ref, ddec_ref, dq_ref, dk_ref, dv_ref, dbeta_ref, dg_ref):
        ii, jj, causal, strict = _pair_masks()
        suffix = ((ii // CHUNK) == (jj // CHUNK)) & (jj >= ii)
        first = ii < CHUNK
        rs = lambda a: jnp.sum(a, axis=1, keepdims=True)
        for hh in range(HEADS_PER_STEP):
            sl = slice(hh * HD, (hh + 1) * HD)
            q, k, v, beta, gc = q_ref[:, sl], k_ref[:, sl], v_ref[:, sl], b_ref[:, sl], g_ref[:, sl]
            last_a, last_b = g_ref[CHUNK - 1:CHUNK, sl], g_ref[PAIR - 1:PAIR, sl]
            dmat, gam, e2 = _decay_parts(gc, last_a, last_b, ii, jj, causal)
            q16, k16 = _b16(q), _b16(k)
            kk = _dot(k16, k16, 1, 1)
            qk = _dot(q16, k16, 1, 1)
            dqd, dkd = dqd_ref[:, sl], dkd_ref[:, sl]
            dp = jnp.where(causal, _dot(_b16(do_ref[:, sl]), vn_ref[:, sl], 1, 1), 0.0)
            dpd16 = _b16(dp * dmat)
            tinv_v = t_ref[:, sl]
            x = _dot3(tinv_v, dvn_ref[:, sl], 0, 0)
            y = _dot3(tinv_v, dw_ref[:, sl], 0, 0)
            da = -jnp.where(strict, _dot(_b16(x), _b16(u_ref[:, sl]), 1, 1) + _dot(_b16(y), w_ref[:, sl], 1, 1), 0.0)
            dkk16 = _b16(da * beta * dmat)
            dq_ref[:, sl] = gam * dqd + _dot(dpd16, k16, 1, 0)
            dk_ref[:, sl] = (e2 * dkd + _dot(dpd16, q16, 0, 0) + beta * gam * y
                             + _dot(dkk16, k16, 1, 0) + _dot(dkk16, k16, 0, 0))
            dv_ref[:, sl] = beta * x
            dbeta = rs(v * x) + rs(k * gam * y) + rs(da * kk * dmat)
            dbeta_ref[:, sl] = jnp.broadcast_to(dbeta, (PAIR, HD))
            m = (dp * qk + da * beta * kk) * dmat
            dgam = rs(q * dqd) + rs(k * beta * y)
            de2 = rs(k * dkd)
            colsum = _to_col(jnp.sum(m, axis=0, keepdims=True), ii, jj)
            te2 = de2 * e2
            dgc = rs(m) - colsum + gam * dgam - te2
            tail_a = jnp.sum(jnp.where(first, te2, 0.0), axis=0, keepdims=True)
            tail_b = jnp.sum(jnp.where(first, 0.0, te2), axis=0, keepdims=True)
            dgc = dgc + jnp.where(ii == CHUNK - 1, tail_a + ddec_ref[0:1, sl] * jnp.exp(last_a), 0.0)
            dgc = dgc + jnp.where(ii == PAIR - 1, tail_b + ddec_ref[8:9, sl] * jnp.exp(last_b), 0.0)
            dgc_row = _to_row(dgc, ii, jj)
            dg = jnp.sum(jnp.where(suffix, jnp.broadcast_to(dgc_row, (PAIR, PAIR)), 0.0), axis=1, keepdims=True)
            dg_ref[:, sl] = jnp.broadcast_to(dg, (PAIR, HD))

    blk = pl.BlockSpec((PAIR, HEADS_PER_STEP * HD), lambda i, h: (i, h))
    return pl.pallas_call(
        body, grid=(t // PAIR, N_HEADS // HEADS_PER_STEP),
        in_specs=[blk] * 14 + [pl.BlockSpec((16, HEADS_PER_STEP * HD), lambda i, h: (i, h))], out_specs=[blk] * 5,
        out_shape=[_sds((t, GW), F32)] * 5, name=name,
        compiler_params=_params(2))(qn, kn, vv, beta_b, gc_b, tinv, u, w, vn, do, dvn, dqd, dkd, dw, ddec)


def _rope_tables(pos_col, inv_row):
    ang = pos_col.astype(F32) * inv_row
    lane = _iota2(ang.shape, 1)
    return jnp.cos(ang), jnp.where(lane < HD // 2, -1.0, 1.0) * jnp.sin(ang)


def _head_rms(xh, wv):
    return xh * lax.rsqrt(jnp.mean(xh * xh, axis=-1, keepdims=True) + EPS) * wv


def _qk_fwd(name, proj, blk_idx, w_row, pos_col, inv_row):
    t = proj.shape[0]
    tm = min(256, t)

    def body(x_ref, w_ref, pos_ref, inv_ref, o_ref):
        cos, sin = _rope_tables(pos_ref[...], inv_ref[...])
        for h in range(N_HEADS):
            y = _head_rms(x_ref[:, h * HD:(h + 1) * HD], w_ref[...])
            o_ref[:, h * HD:(h + 1) * HD] = y * cos + pltpu.roll(y, HD // 2, 1) * sin

    vec = pl.BlockSpec((1, HD), lambda i: (0, 0))
    return pl.pallas_call(
        body, grid=(t // tm,),
        in_specs=[pl.BlockSpec((tm, GW), lambda i: (i, blk_idx)), vec, pl.BlockSpec((tm, 1), lambda i: (i, 0)), vec],
        out_specs=pl.BlockSpec((tm, GW), lambda i: (i, 0)), out_shape=_sds((t, GW), F32), name=name,
        compiler_params=_params(1))(proj, w_row, pos_col, inv_row)


def _qk_bwd(name, proj, blk_idx, w_row, pos_col, inv_row, dy_full, dproj):
    t = proj.shape[0]
    tm = min(256, t)

    def body(x_ref, w_ref, pos_ref, inv_ref, dy_ref, dproj_ref, dx_ref, dw_ref):
        cos, sin = _rope_tables(pos_ref[...], inv_ref[...])
        dw = jnp.zeros((1, HD), F32)
        for h in range(N_HEADS):
            sl = slice(h * HD, (h + 1) * HD)
            dy = dy_ref[:, sl]
            dy = dy * cos - pltpu.roll(dy, HD // 2, 1) * sin
            _, vjp = jax.vjp(_head_rms, x_ref[:, sl], w_ref[...])
            dx, dwh = vjp(dy)
            dw = dw + dwh
            dx_ref[:, sl] = dx.astype(BF16)

        @pl.when(pl.program_id(0) == 0)
        def _():
            dw_ref[...] = jnp.zeros_like(dw_ref)

        dw_ref[...] += dw

    vec = pl.BlockSpec((1, HD), lambda i: (0, 0))
    wide = pl.BlockSpec((tm, GW), lambda i: (i, 0))
    return pl.pallas_call(
        body, grid=(t // tm,),
        in_specs=[pl.BlockSpec((tm, GW), lambda i: (i, blk_idx)), vec, pl.BlockSpec((tm, 1), lambda i: (i, 0)), vec,
                  wide, ANY],
        out_specs=[pl.BlockSpec((tm, GW), lambda i: (i, blk_idx)), vec],
        out_shape=[_sds(dproj.shape, BF16), _sds((1, HD), F32)], input_output_aliases={5: 0}, name=name,
        compiler_params=_params(1))(proj, w_row, pos_col, inv_row, dy_full, dproj)


def _cast_into(name, x, dproj, blk_idx):
    t = x.shape[0]
    tm = min(512, t)

    def body(x_ref, dproj_ref, o_ref):
        o_ref[...] = x_ref[...].astype(BF16)

    return pl.pallas_call(
        body, grid=(t // tm,), in_specs=[pl.BlockSpec((tm, GW), lambda i: (i, 0)), ANY],
        out_specs=pl.BlockSpec((tm, GW), lambda i: (i, blk_idx)), out_shape=_sds(dproj.shape, BF16),
        input_output_aliases={1: 0}, name=name, compiler_params=_params(1))(x, dproj)


GROUP = SPAN * max(DILATIONS)
SCALE = HD ** -0.5


def _band_mask(lo):
    qi = _iota2((SPAN, 2 * SPAN), 0)
    ki = _iota2((SPAN, 2 * SPAN), 1)
    return (ki >= qi) & (ki <= qi + SPAN) & (ki >= lo)


def _tiles():
    return [(pi, r, u, rho) for pi, r in enumerate(DILATIONS) for u in range(GROUP // (SPAN * r)) for rho in range(r)]


def _rows(r, u, rho):
    return pl.ds(u * SPAN * r + rho, SPAN, stride=r) if r > 1 else pl.ds(u * SPAN, SPAN)


def _attn_fwd(name, q, k, v, v_blk):
    t = q.shape[0]

    def body(qc_ref, kc_ref, vc_ref, kp_ref, vp_ref, ob_ref, lse_ref, o_scr, l_scr):
        mask_in = _band_mask(0)
        mask_edge = _band_mask(jnp.where(pl.program_id(0) == 0, SPAN, 0))
        for pi, r, u, rho in _tiles():
            rows = _rows(r, u, rho)
            if u > 0:
                prows, kp_src, vp_src, mask = _rows(r, u - 1, rho), kc_ref, vc_ref, mask_in
            else:
                prows, kp_src, vp_src, mask = _rows(r, GROUP // (SPAN * r) - 1, rho), kp_ref, vp_ref, mask_edge
            kcat = jnp.concatenate([kp_src[prows, :], kc_ref[rows, :]], axis=0).astype(BF16)
            vcat = jnp.concatenate([vp_src[prows, :], vc_ref[rows, :]], axis=0).astype(BF16)
            s = jnp.where(mask, _dot(qc_ref[rows, :].astype(BF16), kcat, 1, 1) * SCALE, NEG)
            m = jnp.max(s, axis=1, keepdims=True)
            p = jnp.exp(s - m)
            den = jnp.sum(p, axis=1, keepdims=True)
            o_scr[pi, rows, :] = _dot(_b16(p), vcat, 1, 0) / den
            l_scr[pi, rows, :] = jnp.broadcast_to(m + jnp.log(den), (SPAN, HD))
        step = 256
        for c in range(GROUP // step):
            sl = pl.ds(c * step, step)
            ob, lse = _merge([o_scr[i, sl, :] for i in range(3)], [l_scr[i, sl, :] for i in range(3)])
            ob_ref[sl, :] = ob
            lse_ref[sl, :] = lse

    cur = pl.BlockSpec((GROUP, HD), lambda g, h: (g, h))
    prev = pl.BlockSpec((GROUP, HD), lambda g, h: (jnp.maximum(g - 1, 0), h))
    vcur = pl.BlockSpec((GROUP, HD), lambda g, h: (g, v_blk * N_HEADS + h))
    vprev = pl.BlockSpec((GROUP, HD), lambda g, h: (jnp.maximum(g - 1, 0), v_blk * N_HEADS + h))
    return pl.pallas_call(
        body, grid=(t // GROUP, N_HEADS), in_specs=[cur, cur, vcur, prev, vprev], out_specs=[cur, cur],
        out_shape=[_sds((t, GW), F32), _sds((t, GW), F32)],
        scratch_shapes=[pltpu.VMEM((3, GROUP, HD), F32), pltpu.VMEM((3, GROUP, HD), F32)], name=name,
        compiler_params=_params(2))(q, k, v, k, v)


def _attn_bwd(name, q, k, v, v_blk, do, lse, delta):
    t = q.shape[0]
    ng = t // GROUP

    def pair(qt, dot, lt, dlt, kcat, vcat, mask):
        wide = kcat.shape[0] // SPAN
        lw = jnp.concatenate([lt] * wide, axis=1) if wide > 1 else lt
        dw = jnp.concatenate([dlt] * wide, axis=1) if wide > 1 else dlt
        s = _dot(qt, kcat, 1, 1) * SCALE
        p = jnp.where(mask, jnp.exp(jnp.where(mask, s - lw, 0.0)), 0.0)
        ds = p * (_dot(dot, vcat, 1, 1) - dw) * SCALE
        return _b16(ds), _b16(p)

    def body(qc_ref, kc_ref, vc_ref, doc_ref, lc_ref, dc_ref, kp_ref, vp_ref, qn_ref, don_ref, ln_ref, dn_ref,
             dq_ref, dk_ref, dv_ref):
        g = pl.program_id(0)
        mask_in = _band_mask(0)
        mask_edge = _band_mask(jnp.where(g == 0, SPAN, 0))
        dk_ref[...] = jnp.zeros_like(dk_ref)
        dv_ref[...] = jnp.zeros_like(dv_ref)
        for pi, r, u, rho in _tiles():
            rows = _rows(r, u, rho)
            if u > 0:
                prows, kp_src, vp_src, mask = _rows(r, u - 1, rho), kc_ref, vc_ref, mask_in
            else:
                prows, kp_src, vp_src, mask = _rows(r, GROUP // (SPAN * r) - 1, rho), kp_ref, vp_ref, mask_edge
            kcat = jnp.concatenate([kp_src[prows, :], kc_ref[rows, :]], axis=0).astype(BF16)
            vcat = jnp.concatenate([vp_src[prows, :], vc_ref[rows, :]], axis=0).astype(BF16)
            qt, dot = qc_ref[rows, :].astype(BF16), doc_ref[rows, :].astype(BF16)
            ds, p = pair(qt, dot, lc_ref[rows, :], dc_ref[rows, :], kcat, vcat, mask)
            dq_t = _dot(ds, kcat, 1, 0)
            if pi == 0:
                dq_ref[rows, :] = dq_t
            else:
                dq_ref[rows, :] += dq_t
            dk2 = _dot(ds, qt, 0, 0)
            dv2 = _dot(p, dot, 0, 0)
            dk_ref[rows, :] += dk2[SPAN:, :]
            dv_ref[rows, :] += dv2[SPAN:, :]
            if u > 0:
                dk_ref[prows, :] += dk2[:SPAN, :]
                dv_ref[prows, :] += dv2[:SPAN, :]
        qi = _iota2((SPAN, SPAN), 0)
        ki = _iota2((SPAN, SPAN), 1)
        mask_next = (ki >= qi) & (ki < jnp.where(g == ng - 1, 0, SPAN))
        for r in DILATIONS:
            for rho in range(r):
                krows, qrows = _rows(r, GROUP // (SPAN * r) - 1, rho), _rows(r, 0, rho)
                qt, dot = qn_ref[qrows, :].astype(BF16), don_ref[qrows, :].astype(BF16)
                ds, p = pair(qt, dot, ln_ref[qrows, :], dn_ref[qrows, :], kc_ref[krows, :].astype(BF16),
                             vc_ref[krows, :].astype(BF16), mask_next)
                dk_ref[krows, :] += _dot(ds, qt, 0, 0)
                dv_ref[krows, :] += _dot(p, dot, 0, 0)

    cur = pl.BlockSpec((GROUP, HD), lambda g, h: (g, h))
    prev = pl.BlockSpec((GROUP, HD), lambda g, h: (jnp.maximum(g - 1, 0), h))
    nxt = pl.BlockSpec((GROUP, HD), lambda g, h: (jnp.minimum(g + 1, ng - 1), h))
    vcur = pl.BlockSpec((GROUP, HD), lambda g, h: (g, v_blk * N_HEADS + h))
    vprev = pl.BlockSpec((GROUP, HD), lambda g, h: (jnp.maximum(g - 1, 0), v_blk * N_HEADS + h))
    return pl.pallas_call(
        body, grid=(ng, N_HEADS), in_specs=[cur, cur, vcur, cur, cur, cur, prev, vprev] + [nxt] * 4,
        out_specs=[cur] * 3,
        out_shape=[_sds((t, GW), F32)] * 3, name=name,
        compiler_params=_params(2))(q, k, v, do, lse, delta, k, v, q, do, lse, delta)


def _merge(os_, ls_):
    m = jnp.maximum(jnp.maximum(ls_[0], ls_[1]), ls_[2])
    ws = [jnp.exp(l - m) for l in ls_]
    tot = ws[0] + ws[1] + ws[2]
    ob = (ws[0] * os_[0] + ws[1] * os_[1] + ws[2] * os_[2]) / tot
    return ob, m + jnp.log(tot)


def _gated_norm(oa, z, wv):
    return _head_rms(oa, wv) * _silu(z)


def _mix_fwd(name, oa_raw, proj, z_blk, ob, w_dn, w_an):
    t = oa_raw.shape[0]
    tm = min(256, t)

    def body(oa_ref, z_ref, ob_ref, wd_ref, wa_ref, mix_ref):
        for h in range(N_HEADS):
            sl = slice(h * HD, (h + 1) * HD)
            mix_ref[:, sl] = _gated_norm(oa_ref[:, sl], z_ref[:, sl], wd_ref[...]).astype(BF16)
            mix_ref[:, GW + h * HD:GW + (h + 1) * HD] = _head_rms(ob_ref[:, sl], wa_ref[...]).astype(BF16)

    vec = pl.BlockSpec((1, HD), lambda i: (0, 0))
    wide = pl.BlockSpec((tm, GW), lambda i: (i, 0))
    return pl.pallas_call(
        body, grid=(t // tm,),
        in_specs=[wide, pl.BlockSpec((tm, GW), lambda i: (i, z_blk)), wide, vec, vec],
        out_specs=pl.BlockSpec((tm, 2 * GW), lambda i: (i, 0)),
        out_shape=_sds((t, 2 * GW), BF16), name=name,
        compiler_params=_params(1))(oa_raw, proj, ob, w_dn, w_an)


def _mix_bwd(name, dmixed, oa_raw, proj, z_blk, ob, w_dn, w_an, dep):
    t = oa_raw.shape[0]
    tm = min(256, t)

    def body(dm_ref, oa_ref, z_ref, ob_ref, wd_ref, wa_ref, dep_ref,
             doa_ref, dz_ref, dob_ref, dl_ref, dwd_ref, dwa_ref):
        dwd = jnp.zeros((1, HD), F32)
        dwa = jnp.zeros((1, HD), F32)
        for h in range(N_HEADS):
            sl = slice(h * HD, (h + 1) * HD)
            _, vjp = jax.vjp(_gated_norm, oa_ref[:, sl], z_ref[:, sl], wd_ref[...])
            doa, dz, dw1 = vjp(dm_ref[:, sl])
            doa_ref[:, sl] = doa
            dz_ref[:, sl] = dz.astype(BF16)
            dwd = dwd + dw1
            obh = ob_ref[:, sl]
            _, vjp2 = jax.vjp(_head_rms, obh, wa_ref[...])
            dob, dw2 = vjp2(dm_ref[:, GW + h * HD:GW + (h + 1) * HD])
            dwa = dwa + dw2
            dob_ref[:, sl] = dob
            dl_ref[:, sl] = jnp.broadcast_to(jnp.sum(dob * obh, axis=1, keepdims=True), (tm, HD))

        @pl.when(pl.program_id(0) == 0)
        def _():
            dwd_ref[...] = jnp.zeros_like(dwd_ref)
            dwa_ref[...] = jnp.zeros_like(dwa_ref)

        dwd_ref[...] += dwd
        dwa_ref[...] += dwa

    vec = pl.BlockSpec((1, HD), lambda i: (0, 0))
    wide = pl.BlockSpec((tm, GW), lambda i: (i, 0))
    return pl.pallas_call(
        body, grid=(t // tm,),
        in_specs=[pl.BlockSpec((tm, 2 * GW), lambda i: (i, 0)), wide, pl.BlockSpec((tm, GW), lambda i: (i, z_blk)),
                  wide, vec, vec, ANY],
        out_specs=[wide, pl.BlockSpec((tm, GW), lambda i: (i, z_blk)), wide, wide, vec, vec],
        out_shape=[_sds((t, GW), F32), _sds(proj.shape, BF16), _sds((t, GW), F32), _sds((t, GW), F32),
                   _sds((1, HD), F32), _sds((1, HD), F32)], name=name,
        compiler_params=_params(1))(dmixed, oa_raw, proj, ob, w_dn, w_an, dep)


def _gate_up_swiglu(name, h2, w_gu_g):
    t, d = h2.shape
    n = w_gu_g.shape[2]
    per = N_DEV // 2
    tm = min(512, t)

    def body(a_ref, bg_ref, bu_ref, gu_ref, act_ref):
        a = a_ref[...]
        g = _dot(a, bg_ref[...], 1, 0)
        up = _dot(a, bu_ref[...], 1, 0)
        gu_ref[0] = g
        gu_ref[1] = up
        act_ref[...] = (_silu(g) * up).astype(BF16)

    return pl.pallas_call(
        body, grid=(per, t // tm),
        in_specs=[pl.BlockSpec((tm, d), lambda j, i: (i, 0)), pl.BlockSpec((None, d, n), lambda j, i: (j, 0, 0)),
                  pl.BlockSpec((None, d, n), lambda j, i: (j + per, 0, 0))],
        out_specs=[pl.BlockSpec((2, tm, n), lambda j, i: (0, i, j)), pl.BlockSpec((tm, n), lambda j, i: (i, j))],
        out_shape=[_sds((2, t, per * n), F32), _sds((t, per * n), BF16)], name=name,
        compiler_params=_params(2))(h2, w_gu_g, w_gu_g)


def _d_gate_up(name, dy16, w_down, gu3, dep):
    t, d = dy16.shape
    f = w_down.shape[0]
    tm, tn = min(512, t), f // 4

    def body(a_ref, b_ref, g_ref, dep_ref, o_ref):
        dact = _dot(a_ref[...], b_ref[...], 1, 1)
        g, up = g_ref[0], g_ref[1]
        sg = _sigmoid(g)
        o_ref[0] = (dact * up * sg * (1.0 + g * (1.0 - sg))).astype(BF16)
        o_ref[1] = (dact * g * sg).astype(BF16)

    return pl.pallas_call(
        body, grid=(f // tn, t // tm),
        in_specs=[pl.BlockSpec((tm, d), lambda j, i: (i, 0)), pl.BlockSpec((tn, d), lambda j, i: (j, 0)),
                  pl.BlockSpec((2, tm, tn), lambda j, i: (0, i, j)), ANY],
        out_specs=pl.BlockSpec((2, tm, tn), lambda j, i: (0, i, j)), out_shape=_sds((2, t, f), BF16), name=name,
        compiler_params=_params(2))(dy16, w_down, gu3, dep)


def _out_proj_norm(name, mixed, w_out, x, w_norm):
    t, d = x.shape
    kdim = mixed.shape[1]
    tm = min(512, t)

    def body(a_ref, b_ref, x_ref, w_ref, x1_ref, h_ref):
        x1 = x_ref[...] + _dot(a_ref[...], b_ref[...], 1, 0)
        x1_ref[...] = x1
        h_ref[...] = _rms_f(x1, w_ref[...]).astype(BF16)

    row = pl.BlockSpec((tm, d), lambda i: (i, 0))
    return pl.pallas_call(
        body, grid=(t // tm,),
        in_specs=[pl.BlockSpec((tm, kdim), lambda i: (i, 0)), pl.BlockSpec((kdim, d), lambda i: (0, 0)), row,
                  pl.BlockSpec((1, d), lambda i: (0, 0))],
        out_specs=[row, row], out_shape=[_sds((t, d), F32), _sds((t, d), BF16)], name=name,
        compiler_params=_params(1))(mixed, w_out, x, w_norm)


def _down_loss(name, act, w_down, x1, target):
    t, f = act.shape
    d = x1.shape[1]
    tm, tn, nk = min(1024, t), 512, 2
    tk = f // nk

    def body(a_ref, b_ref, x_ref, t_ref, dy_ref, dy16_ref, l_ref, acc):
        i, j, k = pl.program_id(0), pl.program_id(1), pl.program_id(2)
        part = _dot(a_ref[...], b_ref[...], 1, 0)

        @pl.when(k == 0)
        def _():
            acc[...] = part

        @pl.when(k > 0)
        def _():
            acc[...] += part

        @pl.when(k == nk - 1)
        def _():
            diff = acc[...] + x_ref[...] - t_ref[...]
            dyv = diff * (1.0 / d)
            dy_ref[...] = dyv
            dy16_ref[...] = dyv.astype(BF16)
            tot = jnp.sum(jnp.sum(diff * diff, axis=1, keepdims=True), axis=0, keepdims=True) * (0.5 / d)

            @pl.when((i == 0) & (j == 0))
            def _():
                l_ref[...] = jnp.zeros_like(l_ref)

            l_ref[...] += jnp.broadcast_to(tot, (8, 128))

    tile = pl.BlockSpec((tm, tn), lambda i, j, k: (i, j))
    return pl.pallas_call(
        body, grid=(t // tm, d // tn, nk),
        in_specs=[pl.BlockSpec((tm, tk), lambda i, j, k: (i, k)), pl.BlockSpec((tk, tn), lambda i, j, k: (k, j)),
                  tile, tile],
        out_specs=[tile, tile, pl.BlockSpec((8, 128), lambda i, j, k: (0, 0))],
        out_shape=[_sds((t, d), F32), _sds((t, d), BF16), _sds((8, 128), F32)],
        scratch_shapes=[pltpu.VMEM((tm, tn), F32)], name=name,
        compiler_params=_params(3))(act, w_down, x1, target)


def _peer(me, k):
    pid = (me + k) % N_DEV
    return (pid // 4, (pid // 2) % 2, pid % 2)


def _my_id():
    return 4 * lax.axis_index("x") + 2 * lax.axis_index("y") + lax.axis_index("c")


def _exchange(name, arrays, scatter, dep):
    n = len(arrays)

    def body(*refs):
        ins, outs = refs[:n], refs[n + 1:2 * n + 1]
        send_sems, recv_sems, local_sems = refs[2 * n + 1:]
        me = _my_id()
        started = []
        for a in range(n):
            src = ins[a].at[me] if scatter[a] else ins[a]
            loc = pltpu.make_async_copy(src, outs[a].at[me], local_sems.at[a])
            loc.start()
            started.append(loc)
        remote = []
        for k in range(1, N_DEV):
            to = (me + k) % N_DEV
            for a in range(n):
                src = ins[a].at[to] if scatter[a] else ins[a]
                cp = pltpu.make_async_remote_copy(src_ref=src, dst_ref=outs[a].at[me],
                                                  send_sem=send_sems.at[a * (N_DEV - 1) + k - 1], recv_sem=recv_sems.at[a * (N_DEV - 1) + k - 1],
                                                  device_id=_peer(me, k), device_id_type=pl.DeviceIdType.MESH)
                cp.start()
                remote.append(cp)
        for k in range(1, N_DEV):
            frm = (me + N_DEV - k) % N_DEV
            for a in range(n):
                src = ins[a].at[frm] if scatter[a] else ins[a]
                pltpu.make_async_remote_copy(src_ref=src, dst_ref=outs[a].at[frm],
                                             send_sem=send_sems.at[a * (N_DEV - 1) + k - 1], recv_sem=recv_sems.at[a * (N_DEV - 1) + k - 1],
                                             device_id=_peer(me, k), device_id_type=pl.DeviceIdType.MESH).wait_recv()
        for cp in remote:
            cp.wait_send()
        for loc in started:
            loc.wait()

    out_shape = [_sds((N_DEV,) + (a.shape[1:] if sc else a.shape), a.dtype) for a, sc in zip(arrays, scatter)]
    return pl.pallas_call(
        body, in_specs=[ANY] * (n + 1), out_specs=[ANY] * n, out_shape=out_shape,
        scratch_shapes=[pltpu.SemaphoreType.DMA((n * (N_DEV - 1),)), pltpu.SemaphoreType.DMA((n * (N_DEV - 1),)),
                        pltpu.SemaphoreType.DMA((n,))],
        name=name)(*arrays, dep)


def _gather_two_level(name, arrays):
    n = len(arrays)
    per = N_DEV - 1

    def body(*refs):
        ins, outs = refs[:n], refs[n:2 * n]
        send_sems, recv_sems, local_sems = refs[2 * n:]
        x, y, c = lax.axis_index("x"), lax.axis_index("y"), lax.axis_index("c")
        me, sibling = (x, y, c), (x, y, 1 - c)
        chips = [(1 - x, y), (x, 1 - y), (1 - x, 1 - y)]

        def copy(a, k, block, to, src=None):
            slot = outs[a].at[4 * block[0] + 2 * block[1] + block[2]]
            return pltpu.make_async_remote_copy(
                src_ref=slot if src is None else src, dst_ref=slot, send_sem=send_sems.at[a * per + k],
                recv_sem=recv_sems.at[a * per + k], device_id=to, device_id_type=pl.DeviceIdType.MESH)

        mine = [pltpu.make_async_copy(ins[a], outs[a].at[4 * x + 2 * y + c], local_sems.at[a]) for a in range(n)]
        for cp in mine:
            cp.start()
        first = [copy(a, 0, me, sibling, src=ins[a]) for a in range(n)]
        first += [copy(a, 1 + j, me, (*chip, c), src=ins[a]) for j, chip in enumerate(chips) for a in range(n)]
        for cp in first:
            cp.start()
        passed = []
        for j, chip in enumerate(chips):
            for a in range(n):
                copy(a, 1 + j, (*chip, c), me).wait_recv()
                cp = copy(a, 4 + j, (*chip, c), sibling)
                cp.start()
                passed.append(cp)
        for a in range(n):
            copy(a, 0, sibling, me).wait_recv()
            for j, chip in enumerate(chips):
                copy(a, 4 + j, (*chip, 1 - c), me).wait_recv()
        for cp in first + passed:
            cp.wait_send()
        for cp in mine:
            cp.wait()

    return pl.pallas_call(
        body, in_specs=[ANY] * n, out_specs=[ANY] * n,
        out_shape=[_sds((N_DEV,) + a.shape, a.dtype) for a in arrays],
        scratch_shapes=[pltpu.SemaphoreType.DMA((n * per,)), pltpu.SemaphoreType.DMA((n * per,)),
                        pltpu.SemaphoreType.DMA((n,))],
        name=name)(*arrays)


HBM = pl.BlockSpec(memory_space=pltpu.HBM)
SEM = pl.BlockSpec(memory_space=pltpu.SEMAPHORE)
EFFECT = pltpu.SideEffectType.DATAFLOW_SIDE_EFFECTING


def _remote_copies(srcs, lands, scatter, send_sems, recv_sems, me, incoming):
    out = []
    for k in range(1, N_DEV):
        other = (me + N_DEV - k) % N_DEV if incoming else (me + k) % N_DEV
        for a in range(len(srcs)):
            sem = a * (N_DEV - 1) + k - 1
            src = srcs[a].at[other] if scatter[a] else srcs[a]
            dst = lands[a].at[other if incoming else me]
            out.append(pltpu.make_async_remote_copy(src_ref=src, dst_ref=dst, send_sem=send_sems.at[sem],
                                                    recv_sem=recv_sems.at[sem], device_id=_peer(me, k),
                                                    device_id_type=pl.DeviceIdType.MESH))
    return out


def _exchange_start(name, arrays, scatter, dep):
    n = len(arrays)
    lands = [lax.empty((N_DEV,) + (a.shape[1:] if sc else a.shape), a.dtype) for a, sc in zip(arrays, scatter)]

    def body(*refs):
        srcs, land_refs = refs[:n], refs[n:2 * n]
        send_sems, recv_sems = refs[2 * n + 1], refs[2 * n + 2]
        token = refs[-1]
        for cp in _remote_copies(srcs, land_refs, scatter, send_sems, recv_sems, _my_id(), False):
            cp.start()
        token[...] = jnp.zeros_like(token)

    n_sem = n * (N_DEV - 1)
    out_shape = ([pltpu.SemaphoreType.DMA((n_sem,)), pltpu.SemaphoreType.DMA((n_sem,))]
                 + [pltpu.HBM(a.shape, a.dtype) for a in arrays] + [pltpu.HBM(l.shape, l.dtype) for l in lands]
                 + [_sds((8, 128), F32)])
    aliases = {i: 2 + i for i in range(2 * n)}
    args = [pltpu.with_memory_space_constraint(a, pltpu.HBM) for a in list(arrays) + lands] + [dep]
    res = pl.pallas_call(
        body, name=name, in_specs=[HBM] * (2 * n) + [ANY], out_shape=out_shape,
        out_specs=[SEM, SEM] + [HBM] * (2 * n) + [pl.BlockSpec(memory_space=pltpu.VMEM)],
        input_output_aliases=aliases, compiler_params=pltpu.CompilerParams(has_side_effects=EFFECT))(*args)
    return dict(send=res[0], recv=res[1], srcs=res[2:2 + n], lands=res[2 + n:2 + 2 * n], token=res[-1],
                scatter=scatter)


def _exchange_wait(name, started, after):
    n = len(started["srcs"])
    scatter = started["scatter"]

    def body(*refs):
        srcs, land_refs = refs[:n], refs[n:2 * n]
        send_sems, recv_sems = refs[2 * n], refs[2 * n + 1]
        me = _my_id()
        for cp in _remote_copies(srcs, land_refs, scatter, send_sems, recv_sems, me, False):
            cp.wait_send()
        for cp in _remote_copies(srcs, land_refs, scatter, send_sems, recv_sems, me, True):
            cp.wait_recv()

    arrs = list(started["srcs"]) + list(started["lands"])
    res = pl.pallas_call(
        body, name=name, in_specs=[HBM] * (2 * n) + [SEM, SEM, ANY],
        out_shape=[pltpu.HBM(a.shape, a.dtype) for a in arrs], out_specs=[HBM] * (2 * n),
        input_output_aliases={i: i for i in range(2 * n)},
        compiler_params=pltpu.CompilerParams(has_side_effects=EFFECT))(*arrs, started["send"], started["recv"], after)
    me = _my_id()
    out = []
    for src, land, sc in zip(res[:n], res[n:], scatter):
        own = lax.dynamic_index_in_dim(src, me, 0, keepdims=True) if sc else src[None]
        out.append(lax.dynamic_update_slice(land, own, (me,) + (0,) * (land.ndim - 1)))
    return out


def _adamw(name, parts, w, m, v):
    r, c = w.shape
    tr, tc = r, c
    if r % 8 == 0:
        tr = next(cand for cand in (128, 88, 64, 40, 8) if r % cand == 0)
    else:
        tc = 256
    c1 = 1.0 / (1.0 - ADAM_B1 ** ADAM_STEP)
    c2 = 1.0 / (1.0 - ADAM_B2 ** ADAM_STEP)

    def body(p_ref, w_ref, m_ref, v_ref, g_ref, d_ref, nm_ref, nv_ref):
        g = p_ref[0].astype(F32)
        for s in range(1, N_DEV):
            g = g + p_ref[s].astype(F32)
        mn = ADAM_B1 * m_ref[...] + (1.0 - ADAM_B1) * g
        vn = ADAM_B2 * v_ref[...] + (1.0 - ADAM_B2) * (g * g)
        g_ref[...] = g
        nm_ref[...] = mn
        nv_ref[...] = vn
        d_ref[...] = -ADAM_LR * ((mn * c1) / (jnp.sqrt(vn * c2) + ADAM_EPS) + ADAM_WD * w_ref[...])

    blk = pl.BlockSpec((tr, tc), lambda i, j: (i, j))
    return pl.pallas_call(
        body, grid=(r // tr, c // tc),
        in_specs=[pl.BlockSpec((N_DEV, tr, tc), lambda i, j: (0, i, j)), blk, blk, blk],
        out_specs=[blk] * 4, out_shape=[_sds((r, c), F32)] * 4, name=name,
        compiler_params=_params(2, VMEM_LIMIT))(parts, w, m, v)


def _pad_rows(a, rows):
    return jnp.pad(a, ((0, rows - a.shape[0]), (0, 0)))


def _lane_row(vec8, offset):
    return jnp.pad(vec8.reshape(1, 8), ((0, 0), (offset, HD - 8 - offset)))


def kernel(x, positions, attn_norm_w, w_in, conv_w, a_log, dt_bias, delta_out_norm_w, q_norm_w, k_norm_w, attn_out_norm_w, w_out, ffn_norm_w, w_gate_up, w_down, loss_target, m_attn_norm_w, m_w_in, m_conv_w, m_a_log, m_dt_bias, m_delta_out_norm_w, m_q_norm_w, m_k_norm_w, m_attn_out_norm_w, m_w_out, m_ffn_norm_w, m_w_gate_up, m_w_down, v_attn_norm_w, v_w_in, v_conv_w, v_a_log, v_dt_bias, v_delta_out_norm_w, v_q_norm_w, v_k_norm_w, v_attn_out_norm_w, v_w_out, v_ffn_norm_w, v_w_gate_up, v_w_down):
    x2 = x[0]
    t, d = x2.shape
    target = loss_target[0]
    pos_col = positions.reshape(t, 1)
    half = HD // 2
    inv = (ROPE_THETA ** (-np.arange(half, dtype=np.float32) / half)).astype(np.float32)
    inv_row = jnp.asarray(np.concatenate([inv, inv]).reshape(1, HD))

    n_in = w_in.shape[2]
    n_gu = w_gate_up.shape[2]
    w_in_g, conv_g = _gather_two_level("gather_in", [jnp.transpose(w_in[0]).astype(BF16), _pad_rows(conv_w[0], 8)])
    out_fly = _exchange_start("gather_out_start", [w_out[0].astype(BF16)], [False], conv_g)
    gu_fly = _exchange_start("gather_gate_up_start", [w_gate_up[0].astype(BF16)], [False], out_fly["token"])
    down_fly = _exchange_start("gather_down_start", [w_down[0].astype(BF16)], [False], gu_fly["token"])
    n_main = 4 * GW
    n_small = 2 * N_HEADS
    segments = [(0, n_main, 0), (n_main + n_small, N_DEV * n_in, n_main), (n_main, n_main + n_small, 7 * GW)]
    pieces = []
    for lo, hi, _ in segments:
        f = lo
        while f < hi:
            j = f // n_in
            end = min(hi, (j + 1) * n_in)
            pieces.append(w_in_g[j][f - j * n_in:end - j * n_in])
            f = end
    w_cat = jnp.concatenate(pieces + [jnp.zeros((HD - n_small, d), BF16)], axis=0)
    n_cat = w_cat.shape[0]
    small_blk = (7 * GW) // HD
    conv_w8 =jnp.transpose(conv_g, (1, 0, 2)).reshape(8, 3 * GW)
    alog_row = _lane_row(a_log[0], 8)
    dtb_row = _lane_row(dt_bias[0], 8)

    tm = min(2048, t)
    h1 = _rms_fwd("norm1", x2, attn_norm_w, down_fly["token"])
    tn = 384
    proj = _mm("in_proj", h1, w_cat, grid=(t // tm, n_cat // tn, 1),
               a_spec=pl.BlockSpec((tm, d), lambda i, j, k: (i, 0)),
               b_spec=pl.BlockSpec((tn, d), lambda i, j, k: (j, 0)),
               o_spec=pl.BlockSpec((tm, tn), lambda i, j, k: (i, j)),
               out_shape=_sds((t, n_cat), F32), ca=1, cb=1, nk=1)
    qn = _conv_fwd("conv_q", proj, conv_w8, 0, True, HD ** -0.5)
    kn = _conv_fwd("conv_k", proj, conv_w8, 1, True, 1.0)
    vv = _conv_fwd("conv_v", proj, conv_w8, 2, False, 1.0)
    beta_b, gc_b = _gates_fwd("gates", proj, small_blk, alog_row, dtb_row)
    u, w, p, tinv, qd, kd = _delta_prep("delta_prep", qn, kn, vv, beta_b, gc_b)
    oa_raw, vn, s_hist = _delta_scan("delta_scan", u, w, p, qd, kd, gc_b)

    aq = _qk_fwd("attn_q", proj, 4, q_norm_w, pos_col, inv_row)
    ak = _qk_fwd("attn_k", proj, 5, k_norm_w, pos_col, inv_row)
    ob, lse = _attn_fwd("attn_fwd", aq, ak, proj, 6)
    mixed = _mix_fwd("mix", oa_raw, proj, 3, ob, delta_out_norm_w, attn_out_norm_w)
    (w_out_g,) = _exchange_wait("gather_out_wait", out_fly, mixed)
    w_out_full = w_out_g.reshape(2 * GW, d)
    tn = 512
    x1, h2 = _out_proj_norm("out_proj", mixed, w_out_full, x2, ffn_norm_w)
    per = N_DEV // 2
    (w_gu_g,) = _exchange_wait("gather_gate_up_wait", gu_fly, h2)
    gu3, act = _gate_up_swiglu("gate_up", h2, w_gu_g)
    (w_down_g,) = _exchange_wait("gather_down_wait", down_fly, act)
    w_down_full = w_down_g.reshape(D_FF, d)
    tmd = min(1024, t)
    dy, dy16, loss_tile = _down_loss("down_proj", act, w_down_full, x1, target)
    loss = lax.psum(loss_tile[0, 0], ("x", "y", "c"))

    tk = min(2048, t)
    nkt = t // tk
    g_down = _mm("g_down", act, dy16, dep=loss.reshape(1, 1), grid=(D_FF // 512, 1, nkt),
                 a_spec=pl.BlockSpec((tk, 512), lambda i, j, k: (k, i)),
                 b_spec=pl.BlockSpec((tk, d), lambda i, j, k: (k, 0)),
                 o_spec=pl.BlockSpec((512, d), lambda i, j, k: (i, 0)),
                 out_shape=_sds((D_FF, d), F32), ca=0, cb=0, nk=nkt)
    down_g_fly = _exchange_start("reduce_down_start", [g_down.reshape(N_DEV, D_FF // N_DEV, d)], [True], dy16)
    dgu3 = _d_gate_up("d_gate_up", dy16, w_down_full, gu3, down_g_fly["token"])
    g_gu = _mm("g_gate_up", h2, dgu3, grid=(d // 512, N_DEV, nkt),
               a_spec=pl.BlockSpec((tk, 512), lambda i, j, k: (k, i)),
               b_spec=pl.BlockSpec((None, tk, n_gu), lambda i, j, k: (j // per, k, j % per)),
               o_spec=pl.BlockSpec((None, 512, n_gu), lambda i, j, k: (j, i, 0)),
               out_shape=_sds((N_DEV, d, n_gu), F32), ca=0, cb=0, nk=nkt)
    gu_g_fly = _exchange_start("reduce_gate_up_start", [g_gu], [True], dy16)
    tmh, tnh = min(2048, t), 1024
    dh2 = _mm("d_h2", dgu3, w_gu_g, dep=gu_g_fly["token"], grid=(t // tmh, d // tnh, N_DEV),
              a_spec=pl.BlockSpec((None, tmh, n_gu), lambda i, j, k: (k // per, i, k % per)),
              b_spec=pl.BlockSpec((None, tnh, n_gu), lambda i, j, k: (k, j, 0)),
              o_spec=pl.BlockSpec((tmh, tnh), lambda i, j, k: (i, j)),
              out_shape=_sds((t, d), F32), ca=1, cb=1, nk=N_DEV)
    dx1, dx1_16, g_ffn_norm = _rms_bwd("norm2_bwd", x1, ffn_norm_w, dh2, dy)

    g_out = _mm("g_out", mixed, dx1_16, grid=((2 * GW) // 512, 1, nkt),
                a_spec=pl.BlockSpec((tk, 512), lambda i, j, k: (k, i)),
                b_spec=pl.BlockSpec((tk, d), lambda i, j, k: (k, 0)),
                o_spec=pl.BlockSpec((512, d), lambda i, j, k: (i, 0)),
                out_shape=_sds((2 * GW, d), F32), ca=0, cb=0, nk=nkt)
    out_g_fly = _exchange_start("reduce_out_start", [g_out.reshape(N_DEV, (2 * GW) // N_DEV, d)], [True], g_ffn_norm)
    dmixed = _mm("d_mixed", dx1_16, w_out_full, dep=out_g_fly["token"], grid=(t // tm, (2 * GW) // tn, 1),
                 a_spec=pl.BlockSpec((tm, d), lambda i, j, k: (i, 0)),
                 b_spec=pl.BlockSpec((tn, d), lambda i, j, k: (j, 0)),
                 o_spec=pl.BlockSpec((tm, tn), lambda i, j, k: (i, j)),
                 out_shape=_sds((t, 2 * GW), F32), ca=1, cb=1, nk=1)
    doa, dproj, dob, delta, g_dn, g_an = _mix_bwd("mix_bwd", dmixed, oa_raw, proj, 3, ob,
                                                  delta_out_norm_w, attn_out_norm_w, out_g_fly["token"])
    d_aq, d_ak, d_av = _attn_bwd("attn_bwd", aq, ak, proj, 6, dob, lse, delta)
    dproj, g_qn = _qk_bwd("attn_q_bwd", proj, 4, q_norm_w, pos_col, inv_row, d_aq, dproj)
    dproj, g_kn = _qk_bwd("attn_k_bwd", proj, 5, k_norm_w, pos_col, inv_row, d_ak, dproj)
    dproj = _cast_into("attn_v_bwd", d_av, dproj, 6)

    dvn, dqd, dkd, dw, ddec = _delta_scan_bwd("delta_scan_bwd", doa, w, p, qd, kd, gc_b, vn, s_hist)
    dqn, dkn, dvv, dbeta_b, dg_b = _delta_prep_bwd("delta_prep_bwd", qn, kn, vv, beta_b, gc_b, tinv, u, w, vn,
                                                   doa, dvn, dqd, dkd, dw, ddec)
    dproj, gcw_q = _conv_bwd("conv_q_bwd", proj, conv_w8, dqn, dproj, 0, True, HD ** -0.5)
    dproj, gcw_k = _conv_bwd("conv_k_bwd", proj, conv_w8, dkn, dproj, 1, True, 1.0)
    dproj, gcw_v = _conv_bwd("conv_v_bwd", proj, conv_w8, dvv, dproj, 2, False, 1.0)
    dproj, g_alog_row, g_dtb_row = _gates_bwd("gates_bwd", proj, small_blk, alog_row, dtb_row, dbeta_b, dg_b, dproj)
    tmc = n_cat // 3
    g_cat = _mm("g_in", dproj, h1, grid=(3, d // 512, nkt),
                a_spec=pl.BlockSpec((tk, tmc), lambda i, j, k: (k, i)),
                b_spec=pl.BlockSpec((tk, 512), lambda i, j, k: (k, j)),
                o_spec=pl.BlockSpec((tmc, 512), lambda i, j, k: (i, j)),
                out_shape=_sds((n_cat, d), F32), ca=0, cb=0, nk=nkt)
    parts = []
    for j in range(N_DEV):
        cols = []
        for lo, hi, start in sorted(segments):
            a, b = max(lo, j * n_in), min(hi, (j + 1) * n_in)
            if a < b:
                cols.append(g_cat[start + a - lo:start + b - lo])
        parts.append(cols[0] if len(cols) == 1 else jnp.concatenate(cols, axis=0))
    g_in_parts = jnp.stack(parts).astype(BF16)
    g_conv = jnp.concatenate([gcw_q, gcw_k, gcw_v], axis=1)
    n_cw = conv_w.shape[2]
    g_conv_parts = jnp.transpose(g_conv.reshape(8, N_DEV, n_cw), (1, 0, 2))
    in_g_fly = _exchange_start("reduce_in_start", [g_in_parts, g_conv_parts], [True] * 2, g_dtb_row)
    tkc = n_cat // 3
    dh1 = _mm("d_h1", dproj, w_cat, dep=in_g_fly["token"], grid=(t // tmd, d // tn, 3),
              a_spec=pl.BlockSpec((tmd, tkc), lambda i, j, k: (i, k)),
              b_spec=pl.BlockSpec((tkc, tn), lambda i, j, k: (k, j)),
              o_spec=pl.BlockSpec((tmd, tn), lambda i, j, k: (i, j)),
              out_shape=_sds((t, d), F32), ca=1, cb=0, nk=3)
    grad_x, _, g_attn_norm = _rms_bwd("norm1_bwd", x2, attn_norm_w, dh1, dx1)

    small_rows = [g_attn_norm.reshape(d // HD, HD), g_ffn_norm.reshape(d // HD, HD), g_dn, g_qn, g_kn, g_an,
                  g_alog_row, g_dtb_row]
    small_pack = _pad_rows(jnp.concatenate(small_rows, axis=0), 40)
    (r_down,) = _exchange_wait("reduce_down_wait", down_g_fly, grad_x)
    (r_gu,) = _exchange_wait("reduce_gate_up_wait", gu_g_fly, grad_x)
    (r_out,) = _exchange_wait("reduce_out_wait", out_g_fly, grad_x)
    res_gu = [a[None] for a in _adamw("adamw_gate_up", r_gu, w_gate_up[0], m_w_gate_up[0], v_w_gate_up[0])]
    res_down = [a[None] for a in _adamw("adamw_down", r_down, w_down[0], m_w_down[0], v_w_down[0])]
    res_out = [a[None] for a in _adamw("adamw_out", r_out, w_out[0], m_w_out[0], v_w_out[0])]
    done = (res_gu[3][0, :1, :1] + res_down[3][0, :1, :1] + res_out[3][0, :1, :1])
    (r_small,) = _exchange("gather_small_grads", [small_pack], [False], done)

    def pack_small(an, fn, dn, qn_, kn_, aon, al, db):
        rows = [an.reshape(d // HD, HD), fn.reshape(d // HD, HD), dn, qn_, kn_, aon,
                _lane_row(al[0], 8), _lane_row(db[0], 8)]
        return _pad_rows(jnp.concatenate(rows, axis=0), 40)

    def unpack_small(pk):
        nr = d // HD
        return dict(attn_norm_w=pk[:nr].reshape(1, d), ffn_norm_w=pk[nr:2 * nr].reshape(1, d),
                    delta_out_norm_w=pk[2 * nr:2 * nr + 1], q_norm_w=pk[2 * nr + 1:2 * nr + 2],
                    k_norm_w=pk[2 * nr + 2:2 * nr + 3], attn_out_norm_w=pk[2 * nr + 3:2 * nr + 4],
                    a_log=pk[2 * nr + 4:2 * nr + 5, 8:16], dt_bias=pk[2 * nr + 5:2 * nr + 6, 8:16])

    res_small = _adamw("adamw_small", r_small,
                       pack_small(attn_norm_w, ffn_norm_w, delta_out_norm_w, q_norm_w, k_norm_w, attn_out_norm_w, a_log, dt_bias),
                       pack_small(m_attn_norm_w, m_ffn_norm_w, m_delta_out_norm_w, m_q_norm_w, m_k_norm_w, m_attn_out_norm_w, m_a_log, m_dt_bias),
                       pack_small(v_attn_norm_w, v_ffn_norm_w, v_delta_out_norm_w, v_q_norm_w, v_k_norm_w, v_attn_out_norm_w, v_a_log, v_dt_bias))
    small = [unpack_small(a) for a in res_small]
    r_in, r_conv = _exchange_wait("reduce_in_wait", in_g_fly, res_small[0])
    res_in = [jnp.transpose(a)[None] for a in _adamw("adamw_in", r_in, jnp.transpose(w_in[0]), jnp.transpose(m_w_in[0]),
                                                     jnp.transpose(v_w_in[0]))]
    res_conv =[a[None, :4] for a in _adamw("adamw_conv", r_conv, _pad_rows(conv_w[0], 8), _pad_rows(m_conv_w[0], 8),
                                            _pad_rows(v_conv_w[0], 8))]

    outs = [loss, grad_x[None]]
    for i in range(4):
        s = small[i]
        outs += [s["attn_norm_w"], res_in[i], res_conv[i], s["a_log"], s["dt_bias"], s["delta_out_norm_w"],
                 s["q_norm_w"], s["k_norm_w"], s["attn_out_norm_w"], res_out[i], s["ffn_norm_w"], res_gu[i],
                 res_down[i]]
    return tuple(outs)
```

```python
import functools

import numpy as np
import jax
import jax.numpy as jnp
from jax import lax
from jax.experimental import pallas as pl
from jax.experimental.pallas import tpu as pltpu

F32 = jnp.float32
BF16 = jnp.bfloat16

N_DEV = 8
N_HEADS = 8
HD = 128
GW = N_HEADS * HD
CHUNK = 64
PAIR = 2 * CHUNK
SPAN = 128
DILATIONS = (1, 4, 16)
ROPE_THETA = 10000.0
EPS = 1e-6
D_FF = 5632
ADAM_LR, ADAM_B1, ADAM_B2, ADAM_EPS, ADAM_WD, ADAM_STEP = 0.001, 0.9, 0.999, 1e-8, 0.01, 10
NEG = -1e30
VMEM_LIMIT = 56 * 1024 * 1024
ANY = pl.BlockSpec(memory_space=pl.ANY)
HEADS_PER_STEP = 8


def _params(n_grid, vmem=VMEM_LIMIT):
    return pltpu.CompilerParams(dimension_semantics=("arbitrary",) * n_grid, vmem_limit_bytes=vmem)


def _sds(shape, dtype):
    return jax.ShapeDtypeStruct(tuple(shape), dtype)


def _sigmoid(x):
    return 1.0 / (1.0 + jnp.exp(-x))


def _silu(x):
    return x * _sigmoid(x)


def _softplus(x):
    return jnp.maximum(x, 0.0) + jnp.log(1.0 + jnp.exp(-jnp.abs(x)))


def _dot(a, b, ca, cb, precision=None):
    return lax.dot_general(a, b, (((ca,), (cb,)), ((), ())), precision=precision,
                           preferred_element_type=F32)


def _b16(x):
    return x if x.dtype == BF16 else x.astype(BF16)


def _split(x):
    hi = x.astype(BF16)
    return hi, (x - hi.astype(F32)).astype(BF16)


def _dot3(a, b, ca, cb):
    a_hi, a_lo = _split(a)
    b_hi, b_lo = _split(b)
    return _dot(a_hi, b_hi, ca, cb) + (_dot(a_hi, b_lo, ca, cb) + _dot(a_lo, b_hi, ca, cb))


def _iota2(shape, axis):
    return lax.broadcasted_iota(jnp.int32, shape, axis)


def _mm(name, a, b, *, grid, a_spec, b_spec, o_spec, out_shape, ca, cb, nk, add=None, add_spec=None,
        dep=None, vmem=VMEM_LIMIT):
    has_add = add is not None
    n_in = 2 + has_add + (dep is not None)

    def body(*refs):
        a_ref, b_ref = refs[0], refs[1]
        e_ref = refs[2] if has_add else None
        o_ref = refs[n_in]
        part = _dot(_b16(a_ref[...]), _b16(b_ref[...]), ca, cb)
        if nk == 1:
            if has_add:
                part = part + e_ref[...]
            o_ref[...] = part.astype(o_ref.dtype)
            return
        acc = refs[-1]
        k = pl.program_id(2)

        @pl.when(k == 0)
        def _():
            acc[...] = part

        @pl.when(k > 0)
        def _():
            acc[...] += part

        @pl.when(k == nk - 1)
        def _():
            res = acc[...]
            if has_add:
                res = res + e_ref[...]
            o_ref[...] = res.astype(o_ref.dtype)

    in_specs = [a_spec, b_spec] + ([add_spec] if has_add else []) + ([ANY] if dep is not None else [])
    args = (a, b) + ((add,) if has_add else ()) + ((dep,) if dep is not None else ())
    blk = [d for d in o_spec.block_shape if d is not None]
    scratch = [pltpu.VMEM(tuple(blk), F32)] if nk > 1 else []
    return pl.pallas_call(body, grid=grid, in_specs=in_specs, out_specs=o_spec, out_shape=out_shape,
                          scratch_shapes=scratch, name=name, compiler_params=_params(3, vmem))(*args)


def _rms_f(xv, wv):
    return xv * lax.rsqrt(jnp.mean(xv * xv, axis=-1, keepdims=True) + EPS) * wv


def _rms_fwd(name, x, w, dep):
    t, d = x.shape
    tm = min(512, t)

    def body(x_ref, w_ref, dep_ref, o_ref):
        o_ref[...] = _rms_f(x_ref[...], w_ref[...]).astype(BF16)

    row = pl.BlockSpec((tm, d), lambda i: (i, 0))
    vec = pl.BlockSpec((1, d), lambda i: (0, 0))
    return pl.pallas_call(body, grid=(t // tm,), in_specs=[row, vec, ANY], out_specs=row,
                          out_shape=_sds((t, d), BF16), name=name, compiler_params=_params(1))(x, w, dep)


def _rms_bwd(name, x, w, dh, res):
    t, d = x.shape
    tm = min(256, t)

    def body(x_ref, w_ref, dh_ref, res_ref, dx_ref, dx16_ref, dw_ref):
        _, vjp = jax.vjp(_rms_f, x_ref[...], w_ref[...])
        dxv, dwv = vjp(dh_ref[...])
        dxv = dxv + res_ref[...]
        dx_ref[...] = dxv
        dx16_ref[...] = dxv.astype(BF16)

        @pl.when(pl.program_id(0) == 0)
        def _():
            dw_ref[...] = jnp.zeros_like(dw_ref)

        dw_ref[...] += dwv

    row = pl.BlockSpec((tm, d), lambda i: (i, 0))
    vec = pl.BlockSpec((1, d), lambda i: (0, 0))
    return pl.pallas_call(body, grid=(t // tm,), in_specs=[row, vec, row, row], out_specs=[row, row, vec],
                          out_shape=[_sds((t, d), F32), _sds((t, d), BF16), _sds((1, d), F32)], name=name,
                          compiler_params=_params(1))(x, w, dh, res)


def _conv_taps(xv, w_ref, rows):
    c = w_ref[3:4, :] * xv
    for s in (1, 2, 3):
        c = c + w_ref[3 - s:4 - s, :] * jnp.where(rows >= s, pltpu.roll(xv, s, 0), 0.0)
    return c


def _post_conv(c, l2, scale):
    y = _silu(c)
    if l2:
        y = y * lax.rsqrt(jnp.sum(y * y, axis=-1, keepdims=True) + EPS) * scale
    return y


def _conv_fwd(name, proj, conv_w8, group, l2, scale):
    t = proj.shape[0]

    def body(x_ref, w_ref, o_ref):
        rows = _iota2((t, HD), 0)
        o_ref[...] = _post_conv(_conv_taps(x_ref[...], w_ref, rows), l2, scale)

    return pl.pallas_call(
        body, grid=(N_HEADS,),
        in_specs=[pl.BlockSpec((t, HD), lambda h: (0, h + group * N_HEADS)),
                  pl.BlockSpec((8, HD), lambda h: (0, h + group * N_HEADS))],
        out_specs=pl.BlockSpec((t, HD), lambda h: (0, h)),
        out_shape=_sds((t, GW), F32), name=name, compiler_params=_params(1, VMEM_LIMIT))(proj, conv_w8)


def _conv_bwd(name, proj, conv_w8, dn, dproj, group, l2, scale):
    t = proj.shape[0]

    def body(x_ref, w_ref, dn_ref, dproj_ref, dx_ref, dw_ref):
        rows = _iota2((t, HD), 0)
        xv = x_ref[...]
        c = _conv_taps(xv, w_ref, rows)
        _, vjp = jax.vjp(lambda cc: _post_conv(cc, l2, scale), c)
        (dc,) = vjp(dn_ref[...])
        dx = w_ref[3:4, :] * dc
        dw = jnp.zeros((8, HD), F32)
        rid = _iota2((8, HD), 0)
        dw = dw + jnp.where(rid == 3, jnp.sum(dc * xv, axis=0, keepdims=True), 0.0)
        for s in (1, 2, 3):
            dx = dx + w_ref[3 - s:4 - s, :] * jnp.where(rows < t - s, pltpu.roll(dc, t - s, 0), 0.0)
            xs = jnp.where(rows >= s, pltpu.roll(xv, s, 0), 0.0)
            dw = dw + jnp.where(rid == 3 - s, jnp.sum(dc * xs, axis=0, keepdims=True), 0.0)
        dx_ref[...] = dx.astype(BF16)
        dw_ref[...] = dw

    return pl.pallas_call(
        body, grid=(N_HEADS,),
        in_specs=[pl.BlockSpec((t, HD), lambda h: (0, h + group * N_HEADS)),
                  pl.BlockSpec((8, HD), lambda h: (0, h + group * N_HEADS)),
                  pl.BlockSpec((t, HD), lambda h: (0, h)), ANY],
        out_specs=[pl.BlockSpec((t, HD), lambda h: (0, h + group * N_HEADS)), pl.BlockSpec((8, HD), lambda h: (0, h))],
        out_shape=[_sds(dproj.shape, BF16), _sds((8, GW), F32)], input_output_aliases={3: 0}, name=name,
        compiler_params=_params(1, VMEM_LIMIT))(proj, conv_w8, dn, dproj)


def _chunk_cumsum(g, rows):
    pos = rows % CHUNK
    s = 1
    while s < CHUNK:
        g = g + jnp.where(pos >= s, pltpu.roll(g, s, 0), 0.0)
        s *= 2
    return g


def _gates_fwd(name, proj, small_blk, alog_row, dtb_row):
    t = proj.shape[0]
    tm = min(256, t)

    def body(s_ref, a_ref, b_ref, beta_ref, gc_ref):
        sm = s_ref[...]
        beta = _sigmoid(sm)
        g = -jnp.exp(a_ref[...]) * _softplus(sm + b_ref[...])
        gc = _chunk_cumsum(g, _iota2((tm, HD), 0))
        lane = _iota2((tm, HD), 1)
        for h in range(N_HEADS):
            bcol = jnp.sum(jnp.where(lane == h, beta, 0.0), axis=1, keepdims=True)
            gcol = jnp.sum(jnp.where(lane == 8 + h, gc, 0.0), axis=1, keepdims=True)
            beta_ref[:, h * HD:(h + 1) * HD] = jnp.broadcast_to(bcol, (tm, HD))
            gc_ref[:, h * HD:(h + 1) * HD] = jnp.broadcast_to(gcol, (tm, HD))

    vec = pl.BlockSpec((1, HD), lambda i: (0, 0))
    wide = pl.BlockSpec((tm, GW), lambda i: (i, 0))
    return pl.pallas_call(
        body, grid=(t // tm,),
        in_specs=[pl.BlockSpec((tm, HD), lambda i: (i, small_blk)), vec, vec], out_specs=[wide, wide],
        out_shape=[_sds((t, GW), F32), _sds((t, GW), F32)], name=name,
        compiler_params=_params(1))(proj, alog_row, dtb_row)


def _gates_bwd(name, proj, small_blk, alog_row, dtb_row, dbeta_b, dg_b, dproj):
    t = proj.shape[0]
    tm = min(256, t)

    def body(s_ref, a_ref, b_ref, db_ref, dg_ref, dproj_ref, ds_ref, da_ref, dbias_ref):
        sm = s_ref[...]
        lane = _iota2((tm, HD), 1)
        db = jnp.zeros((tm, HD), F32)
        dg = jnp.zeros((tm, HD), F32)
        for h in range(N_HEADS):
            db = db + jnp.where(lane == h, db_ref[:, h * HD:(h + 1) * HD], 0.0)
            dg = dg + jnp.where(lane == 8 + h, dg_ref[:, h * HD:(h + 1) * HD], 0.0)
        beta = _sigmoid(sm)
        ea = jnp.exp(a_ref[...])
        pre = sm + b_ref[...]
        g = -ea * _softplus(pre)
        dpre = dg * (-ea) * _sigmoid(pre)
        ds_ref[...] = (db * beta * (1.0 - beta) + dpre).astype(BF16)

        @pl.when(pl.program_id(0) == 0)
        def _():
            da_ref[...] = jnp.zeros_like(da_ref)
            dbias_ref[...] = jnp.zeros_like(dbias_ref)

        da_ref[...] += jnp.sum(dg * g, axis=0, keepdims=True)
        dbias_ref[...] += jnp.sum(dpre, axis=0, keepdims=True)

    vec = pl.BlockSpec((1, HD), lambda i: (0, 0))
    wide = pl.BlockSpec((tm, GW), lambda i: (i, 0))
    return pl.pallas_call(
        body, grid=(t // tm,),
        in_specs=[pl.BlockSpec((tm, HD), lambda i: (i, small_blk)), vec, vec, wide, wide, ANY],
        out_specs=[pl.BlockSpec((tm, HD), lambda i: (i, small_blk)), vec, vec],
        out_shape=[_sds(dproj.shape, BF16), _sds((1, HD), F32), _sds((1, HD), F32)],
        input_output_aliases={5: 0}, name=name,
        compiler_params=_params(1))(proj, alog_row, dtb_row, dbeta_b, dg_b, dproj)


def _pair_masks():
    ii = _iota2((PAIR, PAIR), 0)
    jj = _iota2((PAIR, PAIR), 1)
    same = (ii // CHUNK) == (jj // CHUNK)
    return ii, jj, same & (ii >= jj), same & (ii > jj)


def _to_row(col_b, ii, jj):
    return jnp.sum(jnp.where(ii == jj, col_b, 0.0), axis=0, keepdims=True)


def _to_col(row, ii, jj):
    return jnp.sum(jnp.where(ii == jj, jnp.broadcast_to(row, (PAIR, PAIR)), 0.0), axis=1, keepdims=True)


def _decay_parts(gc, last_a, last_b, ii, jj, causal):
    diff = gc - _to_row(gc, ii, jj)
    dmat = jnp.where(causal, jnp.exp(jnp.where(causal, diff, 0.0)), 0.0)
    glast = jnp.where(ii < CHUNK, last_a, last_b)
    return dmat, jnp.exp(gc), jnp.exp(glast - gc)


def _unit_lower_inverse(lows, ii, jj):
    eye = jnp.where(ii == jj, 1.0, 0.0)
    mm = lambda xs, ys: [_dot3(a, b, 1, 0) for a, b in zip(xs, ys)]
    plus = lambda xs: [eye + a for a in xs]
    minus = lambda xs: [eye - a for a in xs]
    d1 = [jnp.where((ii // 16) == (jj // 16), low, 0.0) for low in lows]
    d2 = mm(d1, d1)
    a = mm(minus(d1), plus(d2))
    d4 = mm(d2, d2)
    a = mm(a, plus(d4))
    d8 = mm(d4, d4)
    td = mm(a, plus(d8))
    n1 = mm(td, [low - d for low, d in zip(lows, d1)])
    n2 = mm(n1, n1)
    return mm(mm(minus(n1), plus(n2)), td)


def _delta_prep(name, qn, kn, vv, beta_b, gc_b):
    t = qn.shape[0]

    def body(q_ref, k_ref, v_ref, b_ref, g_ref, u_ref, w_ref, p_ref, t_ref, qd_ref, kd_ref):
        ii, jj, causal, strict = _pair_masks()
        sls = [slice(hh * HD, (hh + 1) * HD) for hh in range(HEADS_PER_STEP)]
        lows = []
        for sl in sls:
            q, k, beta = q_ref[:, sl], k_ref[:, sl], b_ref[:, sl]
            dmat, gam, e2 = _decay_parts(g_ref[:, sl], g_ref[CHUNK - 1:CHUNK, sl], g_ref[PAIR - 1:PAIR, sl],
                                         ii, jj, causal)
            k16 = _b16(k)
            lows.append(jnp.where(strict, beta * _dot(k16, k16, 1, 1) * dmat, 0.0))
            p_ref[:, sl] = jnp.where(causal, _dot(_b16(q), k16, 1, 1) * dmat, 0.0).astype(BF16)
            qd_ref[:, sl] = (q * gam).astype(BF16)
            kd_ref[:, sl] = (k * e2).astype(BF16)
        for sl, tinv in zip(sls, _unit_lower_inverse(lows, ii, jj)):
            beta = b_ref[:, sl]
            t_ref[:, sl] = tinv
            u_ref[:, sl] = _dot3(tinv, v_ref[:, sl] * beta, 1, 0)
            w_ref[:, sl] = _dot3(tinv, k_ref[:, sl] * (beta * jnp.exp(g_ref[:, sl])), 1, 0).astype(BF16)

    blk = pl.BlockSpec((PAIR, HEADS_PER_STEP * HD), lambda i, h: (i, h))
    return pl.pallas_call(
        body, grid=(t // PAIR, N_HEADS // HEADS_PER_STEP), in_specs=[blk] * 5, out_specs=[blk] * 6,
        out_shape=[_sds((t, GW), F32), _sds((t, GW), BF16), _sds((t, GW), BF16), _sds((t, GW), F32),
                   _sds((t, GW), BF16), _sds((t, GW), BF16)],
        name=name, compiler_params=_params(2))(qn, kn, vv, beta_b, gc_b)


def _delta_scan(name, u, w, p, qd, kd, gc_b):
    t = u.shape[0]
    n = t // CHUNK

    def body(u_ref, w_ref, p_ref, qd_ref, kd_ref, g_ref, o_ref, vn_ref, sh_ref, state):
        @pl.when(pl.program_id(0) == 0)
        def _():
            state[...] = jnp.zeros_like(state)

        sls = [slice(h * HD, (h + 1) * HD) for h in range(N_HEADS)]
        heads = range(N_HEADS)
        s = [state[h] for h in heads]
        for h in heads:
            sh_ref[h] = s[h]
        s16 = [_b16(a) for a in s]
        ws = [_dot(w_ref[:, sls[h]], s16[h], 1, 0) for h in heads]
        qs = [_dot(qd_ref[:, sls[h]], s16[h], 1, 0) for h in heads]
        vn16 = [_b16(u_ref[:, sls[h]] - ws[h]) for h in heads]
        pv = [_dot(p_ref[:, sls[h]], jnp.concatenate([vn16[h], vn16[h]], axis=0), 1, 0) for h in heads]
        kv = [_dot(kd_ref[:, sls[h]], vn16[h], 0, 0) for h in heads]
        for h in heads:
            o_ref[:, sls[h]] = qs[h] + pv[h]
            vn_ref[:, sls[h]] = vn16[h]
            state[h] = s[h] * jnp.exp(g_ref[CHUNK - 1:CHUNK, sls[h]]) + kv[h]

    blk = pl.BlockSpec((CHUNK, GW), lambda i: (i, 0))
    return pl.pallas_call(
        body, grid=(n,), in_specs=[blk] * 6,
        out_specs=[blk, blk, pl.BlockSpec((None, N_HEADS, HD, HD), lambda i: (i, 0, 0, 0))],
        out_shape=[_sds((t, GW), F32), _sds((t, GW), BF16), _sds((n, N_HEADS, HD, HD), F32)],
        scratch_shapes=[pltpu.VMEM((N_HEADS, HD, HD), F32)], name=name,
        compiler_params=_params(1))(u, w, p, qd, kd, gc_b)


def _delta_scan_bwd(name, do, w, p, qd, kd, gc_b, vn, s_hist):
    t = do.shape[0]
    n = t // CHUNK

    def body(do_ref, w_ref, p_ref, qd_ref, kd_ref, g_ref, vn_ref, sh_ref,
             dvn_ref, dqd_ref, dkd_ref, dw_ref, ddec_ref, dstate):
        @pl.when(pl.program_id(0) == 0)
        def _():
            dstate[...] = jnp.zeros_like(dstate)

        sls = [slice(h * HD, (h + 1) * HD) for h in range(N_HEADS)]
        heads = range(N_HEADS)
        ds = [dstate[h] for h in heads]
        ds16 = [_b16(a) for a in ds]
        s16 = [_b16(sh_ref[h]) for h in heads]
        do16 = [_b16(do_ref[:, sls[h]]) for h in heads]
        ptdo = [_dot(p_ref[:, sls[h]], do16[h], 0, 0) for h in heads]
        kds = [_dot(kd_ref[:, sls[h]], ds16[h], 1, 0) for h in heads]
        qdo = [_dot(qd_ref[:, sls[h]], do16[h], 0, 0) for h in heads]
        for h in heads:
            dqd_ref[:, sls[h]] = _dot(do16[h], s16[h], 1, 1)
            dkd_ref[:, sls[h]] = _dot(vn_ref[:, sls[h]], ds16[h], 1, 1)
        dvn = [ptdo[h][:CHUNK, :] + ptdo[h][CHUNK:, :] + kds[h] for h in heads]
        dvn16 = [_b16(a) for a in dvn]
        wdv = [_dot(w_ref[:, sls[h]], dvn16[h], 0, 0) for h in heads]
        for h in heads:
            dvn_ref[:, sls[h]] = dvn[h]
            dw_ref[:, sls[h]] = -_dot(dvn16[h], s16[h], 1, 1)
            tot = jnp.sum(jnp.sum(sh_ref[h] * ds[h], axis=1, keepdims=True), axis=0, keepdims=True)
            ddec_ref[:, sls[h]] = jnp.broadcast_to(tot, (8, HD))
            dstate[h] = ds[h] * jnp.exp(g_ref[CHUNK - 1:CHUNK, sls[h]]) + qdo[h] - wdv[h]

    blk = pl.BlockSpec((CHUNK, GW), lambda i: (n - 1 - i, 0))
    return pl.pallas_call(
        body, grid=(n,),
        in_specs=[blk] * 7 + [pl.BlockSpec((None, N_HEADS, HD, HD), lambda i: (n - 1 - i, 0, 0, 0))],
        out_specs=[blk] * 4 + [pl.BlockSpec((8, GW), lambda i: (n - 1 - i, 0))],
        out_shape=[_sds((t, GW), F32)] * 4 + [_sds((n * 8, GW), F32)],
        scratch_shapes=[pltpu.VMEM((N_HEADS, HD, HD), F32)], name=name,
        compiler_params=_params(1))(do, w, p, qd, kd, gc_b, vn, s_hist)


def _delta_prep_bwd(name, qn, kn, vv, beta_b, gc_b, tinv, u, w, vn, do, dvn, dqd, dkd, dw, ddec):
    t = qn.shape[0]

    def body(q_ref, k_ref, v_ref, b_ref, g_ref, t_ref, u_ref, w_ref, vn_ref, do_ref, dvn_ref, dqd_ref,
             dkd_ref, dw_ref, ddec_ref, dq_ref, dk_ref, dv_ref, dbeta_ref, dg_ref):
        ii, jj, causal, strict = _pair_masks()
        suffix = ((ii // CHUNK) == (jj // CHUNK)) & (jj >= ii)
        first = ii < CHUNK
        rs = lambda a: jnp.sum(a, axis=1, keepdims=True)
        for hh in range(HEADS_PER_STEP):
            sl = slice(hh * HD, (hh + 1) * HD)
            q, k, v, beta, gc = q_ref[:, sl], k_ref[:, sl], v_ref[:, sl], b_ref[:, sl], g_ref[:, sl]
            last_a, last_b = g_ref[CHUNK - 1:CHUNK, sl], g_ref[PAIR - 1:PAIR, sl]
            dmat, gam, e2 = _decay_parts(gc, last_a, last_b, ii, jj, causal)
            q16, k16 = _b16(q), _b16(k)
            kk = _dot(k16, k16, 1, 1)
            qk = _dot(q16, k16, 1, 1)
            dqd, dkd = dqd_ref[:, sl], dkd_ref[:, sl]
            dp = jnp.where(causal, _dot(_b16(do_ref[:, sl]), vn_ref[:, sl], 1, 1), 0.0)
            dpd16 = _b16(dp * dmat)
            tinv_v = t_ref[:, sl]
            x = _dot3(tinv_v, dvn_ref[:, sl], 0, 0)
            y = _dot3(tinv_v, dw_ref[:, sl], 0, 0)
            da = -jnp.where(strict, _dot(_b16(x), _b16(u_ref[:, sl]), 1, 1) + _dot(_b16(y), w_ref[:, sl], 1, 1), 0.0)
            dkk16 = _b16(da * beta * dmat)
            dq_ref[:, sl] = gam * dqd + _dot(dpd16, k16, 1, 0)
            dk_ref[:, sl] = (e2 * dkd + _dot(dpd16, q16, 0, 0) + beta * gam * y
                             + _dot(dkk16, k16, 1, 0) + _dot(dkk16, k16, 0, 0))
            dv_ref[:, sl] = beta * x
            dbeta = rs(v * x) + rs(k * gam * y) + rs(da * kk * dmat)
            dbeta_ref[:, sl] = jnp.broadcast_to(dbeta, (PAIR, HD))
            m = (dp * qk + da * beta * kk) * dmat
            dgam = rs(q * dqd) + rs(k * beta * y)
            de2 = rs(k * dkd)
            colsum = _to_col(jnp.sum(m, axis=0, keepdims=True), ii, jj)
            te2 = de2 * e2
            dgc = rs(m) - colsum + gam * dgam - te2
            tail_a = jnp.sum(jnp.where(first, te2, 0.0), axis=0, keepdims=True)
            tail_b = jnp.sum(jnp.where(first, 0.0, te2), axis=0, keepdims=True)
            dgc = dgc + jnp.where(ii == CHUNK - 1, tail_a + ddec_ref[0:1, sl] * jnp.exp(last_a), 0.0)
            dgc = dgc + jnp.where(ii == PAIR - 1, tail_b + ddec_ref[8:9, sl] * jnp.exp(last_b), 0.0)
            dgc_row = _to_row(dgc, ii, jj)
            dg = jnp.sum(jnp.where(suffix, jnp.broadcast_to(dgc_row, (PAIR, PAIR)), 0.0), axis=1, keepdims=True)
            dg_ref[:, sl] = jnp.broadcast_to(dg, (PAIR, HD))

    blk = pl.BlockSpec((PAIR, HEADS_PER_STEP * HD), lambda i, h: (i, h))
    return pl.pallas_call(
        body, grid=(t // PAIR, N_HEADS // HEADS_PER_STEP),
        in_specs=[blk] * 14 + [pl.BlockSpec((16, HEADS_PER_STEP * HD), lambda i, h: (i, h))], out_specs=[blk] * 5,
        out_shape=[_sds((t, GW), F32)] * 5, name=name,
        compiler_params=_params(2))(qn, kn, vv, beta_b, gc_b, tinv, u, w, vn, do, dvn, dqd, dkd, dw, ddec)


def _rope_tables(pos_col, inv_row):
    ang = pos_col.astype(F32) * inv_row
    lane = _iota2(ang.shape, 1)
    return jnp.cos(ang), jnp.where(lane < HD // 2, -1.0, 1.0) * jnp.sin(ang)


def _head_rms(xh, wv):
    return xh * lax.rsqrt(jnp.mean(xh * xh, axis=-1, keepdims=True) + EPS) * wv


def _qk_fwd(name, proj, blk_idx, w_row, pos_col, inv_row):
    t = proj.shape[0]
    tm = min(256, t)

    def body(x_ref, w_ref, pos_ref, inv_ref, o_ref):
        cos, sin = _rope_tables(pos_ref[...], inv_ref[...])
        for h in range(N_HEADS):
            y = _head_rms(x_ref[:, h * HD:(h + 1) * HD], w_ref[...])
            o_ref[:, h * HD:(h + 1) * HD] = y * cos + pltpu.roll(y, HD // 2, 1) * sin

    vec = pl.BlockSpec((1, HD), lambda i: (0, 0))
    return pl.pallas_call(
        body, grid=(t // tm,),
        in_specs=[pl.BlockSpec((tm, GW), lambda i: (i, blk_idx)), vec, pl.BlockSpec((tm, 1), lambda i: (i, 0)), vec],
        out_specs=pl.BlockSpec((tm, GW), lambda i: (i, 0)), out_shape=_sds((t, GW), F32), name=name,
        compiler_params=_params(1))(proj, w_row, pos_col, inv_row)


def _qk_bwd(name, proj, blk_idx, w_row, pos_col, inv_row, dy_full, dproj):
    t = proj.shape[0]
    tm = min(256, t)

    def body(x_ref, w_ref, pos_ref, inv_ref, dy_ref, dproj_ref, dx_ref, dw_ref):
        cos, sin = _rope_tables(pos_ref[...], inv_ref[...])
        dw = jnp.zeros((1, HD), F32)
        for h in range(N_HEADS):
            sl = slice(h * HD, (h + 1) * HD)
            dy = dy_ref[:, sl]
            dy = dy * cos - pltpu.roll(dy, HD // 2, 1) * sin
            _, vjp = jax.vjp(_head_rms, x_ref[:, sl], w_ref[...])
            dx, dwh = vjp(dy)
            dw = dw + dwh
            dx_ref[:, sl] = dx.astype(BF16)

        @pl.when(pl.program_id(0) == 0)
        def _():
            dw_ref[...] = jnp.zeros_like(dw_ref)

        dw_ref[...] += dw

    vec = pl.BlockSpec((1, HD), lambda i: (0, 0))
    wide = pl.BlockSpec((tm, GW), lambda i: (i, 0))
    return pl.pallas_call(
        body, grid=(t // tm,),
        in_specs=[pl.BlockSpec((tm, GW), lambda i: (i, blk_idx)), vec, pl.BlockSpec((tm, 1), lambda i: (i, 0)), vec,
                  wide, ANY],
        out_specs=[pl.BlockSpec((tm, GW), lambda i: (i, blk_idx)), vec],
        out_shape=[_sds(dproj.shape, BF16), _sds((1, HD), F32)], input_output_aliases={5: 0}, name=name,
        compiler_params=_params(1))(proj, w_row, pos_col, inv_row, dy_full, dproj)


def _cast_into(name, x, dproj, blk_idx):
    t = x.shape[0]
    tm = min(512, t)

    def body(x_ref, dproj_ref, o_ref):
        o_ref[...] = x_ref[...].astype(BF16)

    return pl.pallas_call(
        body, grid=(t // tm,), in_specs=[pl.BlockSpec((tm, GW), lambda i: (i, 0)), ANY],
        out_specs=pl.BlockSpec((tm, GW), lambda i: (i, blk_idx)), out_shape=_sds(dproj.shape, BF16),
        input_output_aliases={1: 0}, name=name, compiler_params=_params(1))(x, dproj)


GROUP = SPAN * max(DILATIONS)
SCALE = HD ** -0.5


def _band_mask(lo):
    qi = _iota2((SPAN, 2 * SPAN), 0)
    ki = _iota2((SPAN, 2 * SPAN), 1)
    return (ki >= qi) & (ki <= qi + SPAN) & (ki >= lo)


def _tiles():
    return [(pi, r, u, rho) for pi, r in enumerate(DILATIONS) for u in range(GROUP // (SPAN * r)) for rho in range(r)]


def _rows(r, u, rho):
    return pl.ds(u * SPAN * r + rho, SPAN, stride=r) if r > 1 else pl.ds(u * SPAN, SPAN)


def _attn_fwd(name, q, k, v, v_blk):
    t = q.shape[0]

    def body(qc_ref, kc_ref, vc_ref, kp_ref, vp_ref, ob_ref, lse_ref, o_scr, l_scr):
        mask_in = _band_mask(0)
        mask_edge = _band_mask(jnp.where(pl.program_id(0) == 0, SPAN, 0))
        for pi, r, u, rho in _tiles():
            rows = _rows(r, u, rho)
            if u > 0:
                prows, kp_src, vp_src, mask = _rows(r, u - 1, rho), kc_ref, vc_ref, mask_in
            else:
                prows, kp_src, vp_src, mask = _rows(r, GROUP // (SPAN * r) - 1, rho), kp_ref, vp_ref, mask_edge
            kcat = jnp.concatenate([kp_src[prows, :], kc_ref[rows, :]], axis=0).astype(BF16)
            vcat = jnp.concatenate([vp_src[prows, :], vc_ref[rows, :]], axis=0).astype(BF16)
            s = jnp.where(mask, _dot(qc_ref[rows, :].astype(BF16), kcat, 1, 1) * SCALE, NEG)
            m = jnp.max(s, axis=1, keepdims=True)
            p = jnp.exp(s - m)
            den = jnp.sum(p, axis=1, keepdims=True)
            o_scr[pi, rows, :] = _dot(_b16(p), vcat, 1, 0) / den
            l_scr[pi, rows, :] = jnp.broadcast_to(m + jnp.log(den), (SPAN, HD))
        step = 256
        for c in range(GROUP // step):
            sl = pl.ds(c * step, step)
            ob, lse = _merge([o_scr[i, sl, :] for i in range(3)], [l_scr[i, sl, :] for i in range(3)])
            ob_ref[sl, :] = ob
            lse_ref[sl, :] = lse

    cur = pl.BlockSpec((GROUP, HD), lambda g, h: (g, h))
    prev = pl.BlockSpec((GROUP, HD), lambda g, h: (jnp.maximum(g - 1, 0), h))
    vcur = pl.BlockSpec((GROUP, HD), lambda g, h: (g, v_blk * N_HEADS + h))
    vprev = pl.BlockSpec((GROUP, HD), lambda g, h: (jnp.maximum(g - 1, 0), v_blk * N_HEADS + h))
    return pl.pallas_call(
        body, grid=(t // GROUP, N_HEADS), in_specs=[cur, cur, vcur, prev, vprev], out_specs=[cur, cur],
        out_shape=[_sds((t, GW), F32), _sds((t, GW), F32)],
        scratch_shapes=[pltpu.VMEM((3, GROUP, HD), F32), pltpu.VMEM((3, GROUP, HD), F32)], name=name,
        compiler_params=_params(2))(q, k, v, k, v)


def _attn_bwd(name, q, k, v, v_blk, do, lse, delta):
    t = q.shape[0]
    ng = t // GROUP

    def pair(qt, dot, lt, dlt, kcat, vcat, mask):
        wide = kcat.shape[0] // SPAN
        lw = jnp.concatenate([lt] * wide, axis=1) if wide > 1 else lt
        dw = jnp.concatenate([dlt] * wide, axis=1) if wide > 1 else dlt
        s = _dot(qt, kcat, 1, 1) * SCALE
        p = jnp.where(mask, jnp.exp(jnp.where(mask, s - lw, 0.0)), 0.0)
        ds = p * (_dot(dot, vcat, 1, 1) - dw) * SCALE
        return _b16(ds), _b16(p)

    def body(qc_ref, kc_ref, vc_ref, doc_ref, lc_ref, dc_ref, kp_ref, vp_ref, qn_ref, don_ref, ln_ref, dn_ref,
             dq_ref, dk_ref, dv_ref):
        g = pl.program_id(0)
        mask_in = _band_mask(0)
        mask_edge = _band_mask(jnp.where(g == 0, SPAN, 0))
        dk_ref[...] = jnp.zeros_like(dk_ref)
        dv_ref[...] = jnp.zeros_like(dv_ref)
        for pi, r, u, rho in _tiles():
            rows = _rows(r, u, rho)
            if u > 0:
                prows, kp_src, vp_src, mask = _rows(r, u - 1, rho), kc_ref, vc_ref, mask_in
            else:
                prows, kp_src, vp_src, mask = _rows(r, GROUP // (SPAN * r) - 1, rho), kp_ref, vp_ref, mask_edge
            kcat = jnp.concatenate([kp_src[prows, :], kc_ref[rows, :]], axis=0).astype(BF16)
            vcat = jnp.concatenate([vp_src[prows, :], vc_ref[rows, :]], axis=0).astype(BF16)
            qt, dot = qc_ref[rows, :].astype(BF16), doc_ref[rows, :].astype(BF16)
            ds, p = pair(qt, dot, lc_ref[rows, :], dc_ref[rows, :], kcat, vcat, mask)
            dq_t = _dot(ds, kcat, 1, 0)
            if pi == 0:
                dq_ref[rows, :] = dq_t
            else:
                dq_ref[rows, :] += dq_t
            dk2 = _dot(ds, qt, 0, 0)
            dv2 = _dot(p, dot, 0, 0)
            dk_ref[rows, :] += dk2[SPAN:, :]
            dv_ref[rows, :] += dv2[SPAN:, :]
            if u > 0:
                dk_ref[prows, :] += dk2[:SPAN, :]
                dv_ref[prows, :] += dv2[:SPAN, :]
        qi = _iota2((SPAN, SPAN), 0)
        ki = _iota2((SPAN, SPAN), 1)
        mask_next = (ki >= qi) & (ki < jnp.where(g == ng - 1, 0, SPAN))
        for r in DILATIONS:
            for rho in range(r):
                krows, qrows = _rows(r, GROUP // (SPAN * r) - 1, rho), _rows(r, 0, rho)
                qt, dot = qn_ref[qrows, :].astype(BF16), don_ref[qrows, :].astype(BF16)
                ds, p = pair(qt, dot, ln_ref[qrows, :], dn_ref[qrows, :], kc_ref[krows, :].astype(BF16),
                             vc_ref[krows, :].astype(BF16), mask_next)
                dk_ref[krows, :] += _dot(ds, qt, 0, 0)
                dv_ref[krows, :] += _dot(p, dot, 0, 0)

    cur = pl.BlockSpec((GROUP, HD), lambda g, h: (g, h))
    prev = pl.BlockSpec((GROUP, HD), lambda g, h: (jnp.maximum(g - 1, 0), h))
    nxt = pl.BlockSpec((GROUP, HD), lambda g, h: (jnp.minimum(g + 1, ng - 1), h))
    vcur = pl.BlockSpec((GROUP, HD), lambda g, h: (g, v_blk * N_HEADS + h))
    vprev = pl.BlockSpec((GROUP, HD), lambda g, h: (jnp.maximum(g - 1, 0), v_blk * N_HEADS + h))
    return pl.pallas_call(
        body, grid=(ng, N_HEADS), in_specs=[cur, cur, vcur, cur, cur, cur, prev, vprev] + [nxt] * 4,
        out_specs=[cur] * 3,
        out_shape=[_sds((t, GW), F32)] * 3, name=name,
        compiler_params=_params(2))(q, k, v, do, lse, delta, k, v, q, do, lse, delta)


def _merge(os_, ls_):
    m = jnp.maximum(jnp.maximum(ls_[0], ls_[1]), ls_[2])
    ws = [jnp.exp(l - m) for l in ls_]
    tot = ws[0] + ws[1] + ws[2]
    ob = (ws[0] * os_[0] + ws[1] * os_[1] + ws[2] * os_[2]) / tot
    return ob, m + jnp.log(tot)


def _gated_norm(oa, z, wv):
    return _head_rms(oa, wv) * _silu(z)


def _mix_fwd(name, oa_raw, proj, z_blk, ob, w_dn, w_an):
    t = oa_raw.shape[0]
    tm = min(256, t)

    def body(oa_ref, z_ref, ob_ref, wd_ref, wa_ref, mix_ref):
        for h in range(N_HEADS):
            sl = slice(h * HD, (h + 1) * HD)
            mix_ref[:, sl] = _gated_norm(oa_ref[:, sl], z_ref[:, sl], wd_ref[...]).astype(BF16)
            mix_ref[:, GW + h * HD:GW + (h + 1) * HD] = _head_rms(ob_ref[:, sl], wa_ref[...]).astype(BF16)

    vec = pl.BlockSpec((1, HD), lambda i: (0, 0))
    wide = pl.BlockSpec((tm, GW), lambda i: (i, 0))
    return pl.pallas_call(
        body, grid=(t // tm,),
        in_specs=[wide, pl.BlockSpec((tm, GW), lambda i: (i, z_blk)), wide, vec, vec],
        out_specs=pl.BlockSpec((tm, 2 * GW), lambda i: (i, 0)),
        out_shape=_sds((t, 2 * GW), BF16), name=name,
        compiler_params=_params(1))(oa_raw, proj, ob, w_dn, w_an)


def _mix_bwd(name, dmixed, oa_raw, proj, z_blk, ob, w_dn, w_an, dep):
    t = oa_raw.shape[0]
    tm = min(256, t)

    def body(dm_ref, oa_ref, z_ref, ob_ref, wd_ref, wa_ref, dep_ref,
             doa_ref, dz_ref, dob_ref, dl_ref, dwd_ref, dwa_ref):
        dwd = jnp.zeros((1, HD), F32)
        dwa = jnp.zeros((1, HD), F32)
        for h in range(N_HEADS):
            sl = slice(h * HD, (h + 1) * HD)
            _, vjp = jax.vjp(_gated_norm, oa_ref[:, sl], z_ref[:, sl], wd_ref[...])
            doa, dz, dw1 = vjp(dm_ref[:, sl])
            doa_ref[:, sl] = doa
            dz_ref[:, sl] = dz.astype(BF16)
            dwd = dwd + dw1
            obh = ob_ref[:, sl]
            _, vjp2 = jax.vjp(_head_rms, obh, wa_ref[...])
            dob, dw2 = vjp2(dm_ref[:, GW + h * HD:GW + (h + 1) * HD])
            dwa = dwa + dw2
            dob_ref[:, sl] = dob
            dl_ref[:, sl] = jnp.broadcast_to(jnp.sum(dob * obh, axis=1, keepdims=True), (tm, HD))

        @pl.when(pl.program_id(0) == 0)
        def _():
            dwd_ref[...] = jnp.zeros_like(dwd_ref)
            dwa_ref[...] = jnp.zeros_like(dwa_ref)

        dwd_ref[...] += dwd
        dwa_ref[...] += dwa

    vec = pl.BlockSpec((1, HD), lambda i: (0, 0))
    wide = pl.BlockSpec((tm, GW), lambda i: (i, 0))
    return pl.pallas_call(
        body, grid=(t // tm,),
        in_specs=[pl.BlockSpec((tm, 2 * GW), lambda i: (i, 0)), wide, pl.BlockSpec((tm, GW), lambda i: (i, z_blk)),
                  wide, vec, vec, ANY],
        out_specs=[wide, pl.BlockSpec((tm, GW), lambda i: (i, z_blk)), wide, wide, vec, vec],
        out_shape=[_sds((t, GW), F32), _sds(proj.shape, BF16), _sds((t, GW), F32), _sds((t, GW), F32),
                   _sds((1, HD), F32), _sds((1, HD), F32)], name=name,
        compiler_params=_params(1))(dmixed, oa_raw, proj, ob, w_dn, w_an, dep)


def _gate_up_swiglu(name, h2, w_gu_g):
    t, d = h2.shape
    n = w_gu_g.shape[2]
    per = N_DEV // 2
    tm = min(512, t)

    def body(a_ref, bg_ref, bu_ref, gu_ref, act_ref):
        a = a_ref[...]
        g = _dot(a, bg_ref[...], 1, 0)
        up = _dot(a, bu_ref[...], 1, 0)
        gu_ref[0] = g.astype(BF16)
        gu_ref[1] = up.astype(BF16)
        act_ref[...] = (_silu(g) * up).astype(BF16)

    return pl.pallas_call(
        body, grid=(per, t // tm),
        in_specs=[pl.BlockSpec((tm, d), lambda j, i: (i, 0)), pl.BlockSpec((None, d, n), lambda j, i: (j, 0, 0)),
                  pl.BlockSpec((None, d, n), lambda j, i: (j + per, 0, 0))],
        out_specs=[pl.BlockSpec((2, tm, n), lambda j, i: (0, i, j)), pl.BlockSpec((tm, n), lambda j, i: (i, j))],
        out_shape=[_sds((2, t, per * n), BF16), _sds((t, per * n), BF16)], name=name,
        compiler_params=_params(2))(h2, w_gu_g, w_gu_g)


def _d_gate_up(name, dy16, w_down, gu3, dep):
    t, d = dy16.shape
    f = w_down.shape[0]
    tm, tn = min(1024, t), f // 4

    def body(a_ref, b_ref, g_ref, dep_ref, o_ref):
        dact = _dot(a_ref[...], b_ref[...], 1, 1)
        g, up = g_ref[0].astype(F32), g_ref[1].astype(F32)
        sg = _sigmoid(g)
        o_ref[0] = (dact * up * sg * (1.0 + g * (1.0 - sg))).astype(BF16)
        o_ref[1] = (dact * g * sg).astype(BF16)

    return pl.pallas_call(
        body, grid=(f // tn, t // tm),
        in_specs=[pl.BlockSpec((tm, d), lambda j, i: (i, 0)), pl.BlockSpec((tn, d), lambda j, i: (j, 0)),
                  pl.BlockSpec((2, tm, tn), lambda j, i: (0, i, j)), ANY],
        out_specs=pl.BlockSpec((2, tm, tn), lambda j, i: (0, i, j)), out_shape=_sds((2, t, f), BF16), name=name,
        compiler_params=_params(2))(dy16, w_down, gu3, dep)


def _out_proj_norm(name, mixed, w_out, x, w_norm):
    t, d = x.shape
    kdim = mixed.shape[1]
    tm = min(512, t)

    def body(a_ref, b_ref, x_ref, w_ref, x1_ref, h_ref):
        x1 = x_ref[...] + _dot(a_ref[...], b_ref[...], 1, 0)
        x1_ref[...] = x1
        h_ref[...] = _rms_f(x1, w_ref[...]).astype(BF16)

    row = pl.BlockSpec((tm, d), lambda i: (i, 0))
    return pl.pallas_call(
        body, grid=(t // tm,),
        in_specs=[pl.BlockSpec((tm, kdim), lambda i: (i, 0)), pl.BlockSpec((kdim, d), lambda i: (0, 0)), row,
                  pl.BlockSpec((1, d), lambda i: (0, 0))],
        out_specs=[row, row], out_shape=[_sds((t, d), F32), _sds((t, d), BF16)], name=name,
        compiler_params=_params(1))(mixed, w_out, x, w_norm)


def _down_loss(name, act, w_down, x1, target):
    t, f = act.shape
    d = x1.shape[1]
    tm, tn, nk = min(1024, t), 512, 2
    tk = f // nk

    def body(a_ref, b_ref, x_ref, t_ref, dy_ref, dy16_ref, l_ref, acc):
        i, j, k = pl.program_id(0), pl.program_id(1), pl.program_id(2)
        part = _dot(a_ref[...], b_ref[...], 1, 0)

        @pl.when(k == 0)
        def _():
            acc[...] = part

        @pl.when(k > 0)
        def _():
            acc[...] += part

        @pl.when(k == nk - 1)
        def _():
            diff = acc[...] + x_ref[...] - t_ref[...]
            dyv = diff * (1.0 / d)
            dy_ref[...] = dyv
            dy16_ref[...] = dyv.astype(BF16)
            tot = jnp.sum(jnp.sum(diff * diff, axis=1, keepdims=True), axis=0, keepdims=True) * (0.5 / d)

            @pl.when((i == 0) & (j == 0))
            def _():
                l_ref[...] = jnp.zeros_like(l_ref)

            l_ref[...] += jnp.broadcast_to(tot, (8, 128))

    tile = pl.BlockSpec((tm, tn), lambda i, j, k: (i, j))
    return pl.pallas_call(
        body, grid=(t // tm, d // tn, nk),
        in_specs=[pl.BlockSpec((tm, tk), lambda i, j, k: (i, k)), pl.BlockSpec((tk, tn), lambda i, j, k: (k, j)),
                  tile, tile],
        out_specs=[tile, tile, pl.BlockSpec((8, 128), lambda i, j, k: (0, 0))],
        out_shape=[_sds((t, d), F32), _sds((t, d), BF16), _sds((8, 128), F32)],
        scratch_shapes=[pltpu.VMEM((tm, tn), F32)], name=name,
        compiler_params=_params(3))(act, w_down, x1, target)


def _peer(me, k):
    pid = (me + k) % N_DEV
    return (pid // 4, (pid // 2) % 2, pid % 2)


def _my_id():
    return 4 * lax.axis_index("x") + 2 * lax.axis_index("y") + lax.axis_index("c")


def _exchange(name, arrays, scatter, dep):
    n = len(arrays)

    def body(*refs):
        ins, outs = refs[:n], refs[n + 1:2 * n + 1]
        send_sems, recv_sems, local_sems = refs[2 * n + 1:]
        me = _my_id()
        started = []
        for a in range(n):
            src = ins[a].at[me] if scatter[a] else ins[a]
            loc = pltpu.make_async_copy(src, outs[a].at[me], local_sems.at[a])
            loc.start()
            started.append(loc)
        remote = []
        for k in range(1, N_DEV):
            to = (me + k) % N_DEV
            for a in range(n):
                src = ins[a].at[to] if scatter[a] else ins[a]
                cp = pltpu.make_async_remote_copy(src_ref=src, dst_ref=outs[a].at[me],
                                                  send_sem=send_sems.at[a * (N_DEV - 1) + k - 1], recv_sem=recv_sems.at[a * (N_DEV - 1) + k - 1],
                                                  device_id=_peer(me, k), device_id_type=pl.DeviceIdType.MESH)
                cp.start()
                remote.append(cp)
        for k in range(1, N_DEV):
            frm = (me + N_DEV - k) % N_DEV
            for a in range(n):
                src = ins[a].at[frm] if scatter[a] else ins[a]
                pltpu.make_async_remote_copy(src_ref=src, dst_ref=outs[a].at[frm],
                                             send_sem=send_sems.at[a * (N_DEV - 1) + k - 1], recv_sem=recv_sems.at[a * (N_DEV - 1) + k - 1],
                                             device_id=_peer(me, k), device_id_type=pl.DeviceIdType.MESH).wait_recv()
        for cp in remote:
            cp.wait_send()
        for loc in started:
            loc.wait()

    out_shape = [_sds((N_DEV,) + (a.shape[1:] if sc else a.shape), a.dtype) for a, sc in zip(arrays, scatter)]
    return pl.pallas_call(
        body, in_specs=[ANY] * (n + 1), out_specs=[ANY] * n, out_shape=out_shape,
        scratch_shapes=[pltpu.SemaphoreType.DMA((n * (N_DEV - 1),)), pltpu.SemaphoreType.DMA((n * (N_DEV - 1),)),
                        pltpu.SemaphoreType.DMA((n,))],
        name=name)(*arrays, dep)


def _gather_two_level(name, arrays):
    n = len(arrays)
    per = N_DEV - 1

    def body(*refs):
        ins, outs = refs[:n], refs[n:2 * n]
        send_sems, recv_sems, local_sems = refs[2 * n:]
        x, y, c = lax.axis_index("x"), lax.axis_index("y"), lax.axis_index("c")
        me, sibling = (x, y, c), (x, y, 1 - c)
        chips = [(1 - x, y), (x, 1 - y), (1 - x, 1 - y)]

        def copy(a, k, block, to, src=None):
            slot = outs[a].at[4 * block[0] + 2 * block[1] + block[2]]
            return pltpu.make_async_remote_copy(
                src_ref=slot if src is None else src, dst_ref=slot, send_sem=send_sems.at[a * per + k],
                recv_sem=recv_sems.at[a * per + k], device_id=to, device_id_type=pl.DeviceIdType.MESH)

        mine = [pltpu.make_async_copy(ins[a], outs[a].at[4 * x + 2 * y + c], local_sems.at[a]) for a in range(n)]
        for cp in mine:
            cp.start()
        first = [copy(a, 0, me, sibling, src=ins[a]) for a in range(n)]
        first += [copy(a, 1 + j, me, (*chip, c), src=ins[a]) for j, chip in enumerate(chips) for a in range(n)]
        for cp in first:
            cp.start()
        passed = []
        for j, chip in enumerate(chips):
            for a in range(n):
                copy(a, 1 + j, (*chip, c), me).wait_recv()
                cp = copy(a, 4 + j, (*chip, c), sibling)
                cp.start()
                passed.append(cp)
        for a in range(n):
            copy(a, 0, sibling, me).wait_recv()
            for j, chip in enumerate(chips):
                copy(a, 4 + j, (*chip, 1 - c), me).wait_recv()
        for cp in first + passed:
            cp.wait_send()
        for cp in mine:
            cp.wait()

    return pl.pallas_call(
        body, in_specs=[ANY] * n, out_specs=[ANY] * n,
        out_shape=[_sds((N_DEV,) + a.shape, a.dtype) for a in arrays],
        scratch_shapes=[pltpu.SemaphoreType.DMA((n * per,)), pltpu.SemaphoreType.DMA((n * per,)),
                        pltpu.SemaphoreType.DMA((n,))],
        name=name)(*arrays)


HBM = pl.BlockSpec(memory_space=pltpu.HBM)
SEM = pl.BlockSpec(memory_space=pltpu.SEMAPHORE)
EFFECT = pltpu.SideEffectType.DATAFLOW_SIDE_EFFECTING


def _remote_copies(srcs, lands, scatter, send_sems, recv_sems, me, incoming):
    out = []
    for k in range(1, N_DEV):
        other = (me + N_DEV - k) % N_DEV if incoming else (me + k) % N_DEV
        for a in range(len(srcs)):
            sem = a * (N_DEV - 1) + k - 1
            src = srcs[a].at[other] if scatter[a] else srcs[a]
            dst = lands[a].at[other if incoming else me]
            out.append(pltpu.make_async_remote_copy(src_ref=src, dst_ref=dst, send_sem=send_sems.at[sem],
                                                    recv_sem=recv_sems.at[sem], device_id=_peer(me, k),
                                                    device_id_type=pl.DeviceIdType.MESH))
    return out


def _exchange_start(name, arrays, scatter, dep):
    n = len(arrays)
    lands = [lax.empty((N_DEV,) + (a.shape[1:] if sc else a.shape), a.dtype) for a, sc in zip(arrays, scatter)]

    def body(*refs):
        srcs, land_refs = refs[:n], refs[n:2 * n]
        send_sems, recv_sems = refs[2 * n + 1], refs[2 * n + 2]
        token = refs[-1]
        for cp in _remote_copies(srcs, land_refs, scatter, send_sems, recv_sems, _my_id(), False):
            cp.start()
        token[...] = jnp.zeros_like(token)

    n_sem = n * (N_DEV - 1)
    out_shape = ([pltpu.SemaphoreType.DMA((n_sem,)), pltpu.SemaphoreType.DMA((n_sem,))]
                 + [pltpu.HBM(a.shape, a.dtype) for a in arrays] + [pltpu.HBM(l.shape, l.dtype) for l in lands]
                 + [_sds((8, 128), F32)])
    aliases = {i: 2 + i for i in range(2 * n)}
    args = [pltpu.with_memory_space_constraint(a, pltpu.HBM) for a in list(arrays) + lands] + [dep]
    res = pl.pallas_call(
        body, name=name, in_specs=[HBM] * (2 * n) + [ANY], out_shape=out_shape,
        out_specs=[SEM, SEM] + [HBM] * (2 * n) + [pl.BlockSpec(memory_space=pltpu.VMEM)],
        input_output_aliases=aliases, compiler_params=pltpu.CompilerParams(has_side_effects=EFFECT))(*args)
    return dict(send=res[0], recv=res[1], srcs=res[2:2 + n], lands=res[2 + n:2 + 2 * n], token=res[-1],
                scatter=scatter)


def _exchange_wait(name, started, after):
    n = len(started["srcs"])
    scatter = started["scatter"]

    def body(*refs):
        srcs, land_refs = refs[:n], refs[n:2 * n]
        send_sems, recv_sems = refs[2 * n], refs[2 * n + 1]
        me = _my_id()
        for cp in _remote_copies(srcs, land_refs, scatter, send_sems, recv_sems, me, False):
            cp.wait_send()
        for cp in _remote_copies(srcs, land_refs, scatter, send_sems, recv_sems, me, True):
            cp.wait_recv()

    arrs = list(started["srcs"]) + list(started["lands"])
    res = pl.pallas_call(
        body, name=name, in_specs=[HBM] * (2 * n) + [SEM, SEM, ANY],
        out_shape=[pltpu.HBM(a.shape, a.dtype) for a in arrs], out_specs=[HBM] * (2 * n),
        input_output_aliases={i: i for i in range(2 * n)},
        compiler_params=pltpu.CompilerParams(has_side_effects=EFFECT))(*arrs, started["send"], started["recv"], after)
    me = _my_id()
    out = []
    for src, land, sc in zip(res[:n], res[n:], scatter):
        own = lax.dynamic_index_in_dim(src, me, 0, keepdims=True) if sc else src[None]
        out.append(lax.dynamic_update_slice(land, own, (me,) + (0,) * (land.ndim - 1)))
    return out


def _adamw(name, parts, w, m, v):
    r, c = w.shape
    tr, tc = r, c
    if r % 8 == 0:
        tr = next(cand for cand in (128, 88, 64, 40, 8) if r % cand == 0)
    else:
        tc = 256
    c1 = 1.0 / (1.0 - ADAM_B1 ** ADAM_STEP)
    c2 = 1.0 / (1.0 - ADAM_B2 ** ADAM_STEP)

    def body(p_ref, w_ref, m_ref, v_ref, g_ref, d_ref, nm_ref, nv_ref):
        g = p_ref[0].astype(F32)
        for s in range(1, N_DEV):
            g = g + p_ref[s].astype(F32)
        mn = ADAM_B1 * m_ref[...] + (1.0 - ADAM_B1) * g
        vn = ADAM_B2 * v_ref[...] + (1.0 - ADAM_B2) * (g * g)
        g_ref[...] = g
        nm_ref[...] = mn
        nv_ref[...] = vn
        d_ref[...] = -ADAM_LR * ((mn * c1) / (jnp.sqrt(vn * c2) + ADAM_EPS) + ADAM_WD * w_ref[...])

    blk = pl.BlockSpec((tr, tc), lambda i, j: (i, j))
    return pl.pallas_call(
        body, grid=(r // tr, c // tc),
        in_specs=[pl.BlockSpec((N_DEV, tr, tc), lambda i, j: (0, i, j)), blk, blk, blk],
        out_specs=[blk] * 4, out_shape=[_sds((r, c), F32)] * 4, name=name,
        compiler_params=_params(2, VMEM_LIMIT))(parts, w, m, v)


def _pad_rows(a, rows):
    return jnp.pad(a, ((0, rows - a.shape[0]), (0, 0)))


def _lane_row(vec8, offset):
    return jnp.pad(vec8.reshape(1, 8), ((0, 0), (offset, HD - 8 - offset)))


def kernel(x, positions, attn_norm_w, w_in, conv_w, a_log, dt_bias, delta_out_norm_w, q_norm_w, k_norm_w, attn_out_norm_w, w_out, ffn_norm_w, w_gate_up, w_down, loss_target, m_attn_norm_w, m_w_in, m_conv_w, m_a_log, m_dt_bias, m_delta_out_norm_w, m_q_norm_w, m_k_norm_w, m_attn_out_norm_w, m_w_out, m_ffn_norm_w, m_w_gate_up, m_w_down, v_attn_norm_w, v_w_in, v_conv_w, v_a_log, v_dt_bias, v_delta_out_norm_w, v_q_norm_w, v_k_norm_w, v_attn_out_norm_w, v_w_out, v_ffn_norm_w, v_w_gate_up, v_w_down):
    x2 = x[0]
    t, d = x2.shape
    target = loss_target[0]
    pos_col = positions.reshape(t, 1)
    half = HD // 2
    inv = (ROPE_THETA ** (-np.arange(half, dtype=np.float32) / half)).astype(np.float32)
    inv_row = jnp.asarray(np.concatenate([inv, inv]).reshape(1, HD))

    n_in = w_in.shape[2]
    n_gu = w_gate_up.shape[2]
    w_in_g, conv_g = _gather_two_level("gather_in", [w_in[0].astype(BF16), _pad_rows(conv_w[0], 8)])
    out_fly = _exchange_start("gather_out_start", [w_out[0].astype(BF16)], [False], conv_g)
    gu_fly = _exchange_start("gather_gate_up_start", [w_gate_up[0].astype(BF16)], [False], out_fly["token"])
    down_fly = _exchange_start("gather_down_start", [w_down[0].astype(BF16)], [False], gu_fly["token"])
    n_main = 4 * GW
    n_small = 2 * N_HEADS
    segments = [(0, n_main, 0), (n_main + n_small, N_DEV * n_in, n_main), (n_main, n_main + n_small, 7 * GW)]
    pieces = []
    for lo, hi, _ in segments:
        f = lo
        while f < hi:
            j = f // n_in
            end = min(hi, (j + 1) * n_in)
            pieces.append(w_in_g[j][:, f - j * n_in:end - j * n_in])
            f = end
    w_cat = jnp.concatenate(pieces + [jnp.zeros((d, HD - n_small), BF16)], axis=1)
    n_cat = w_cat.shape[1]
    small_blk = (7 * GW) // HD
    conv_w8 =jnp.transpose(conv_g, (1, 0, 2)).reshape(8, 3 * GW)
    alog_row = _lane_row(a_log[0], 8)
    dtb_row = _lane_row(dt_bias[0], 8)

    tm = min(2048, t)
    h1 = _rms_fwd("norm1", x2, attn_norm_w, down_fly["token"])
    tn = 384
    proj = _mm("in_proj", h1, w_cat, grid=(t // tm, n_cat // tn, 1),
               a_spec=pl.BlockSpec((tm, d), lambda i, j, k: (i, 0)),
               b_spec=pl.BlockSpec((d, tn), lambda i, j, k: (0, j)),
               o_spec=pl.BlockSpec((tm, tn), lambda i, j, k: (i, j)),
               out_shape=_sds((t, n_cat), F32), ca=1, cb=0, nk=1)
    qn = _conv_fwd("conv_q", proj, conv_w8, 0, True, HD ** -0.5)
    kn = _conv_fwd("conv_k", proj, conv_w8, 1, True, 1.0)
    vv = _conv_fwd("conv_v", proj, conv_w8, 2, False, 1.0)
    beta_b, gc_b = _gates_fwd("gates", proj, small_blk, alog_row, dtb_row)
    u, w, p, tinv, qd, kd = _delta_prep("delta_prep", qn, kn, vv, beta_b, gc_b)
    oa_raw, vn, s_hist = _delta_scan("delta_scan", u, w, p, qd, kd, gc_b)

    aq = _qk_fwd("attn_q", proj, 4, q_norm_w, pos_col, inv_row)
    ak = _qk_fwd("attn_k", proj, 5, k_norm_w, pos_col, inv_row)
    ob, lse = _attn_fwd("attn_fwd", aq, ak, proj, 6)
    mixed = _mix_fwd("mix", oa_raw, proj, 3, ob, delta_out_norm_w, attn_out_norm_w)
    (w_out_g,) = _exchange_wait("gather_out_wait", out_fly, mixed)
    w_out_full = w_out_g.reshape(2 * GW, d)
    tn = 512
    x1, h2 = _out_proj_norm("out_proj", mixed, w_out_full, x2, ffn_norm_w)
    per = N_DEV // 2
    (w_gu_g,) = _exchange_wait("gather_gate_up_wait", gu_fly, h2)
    gu3, act = _gate_up_swiglu("gate_up", h2, w_gu_g)
    (w_down_g,) = _exchange_wait("gather_down_wait", down_fly, act)
    w_down_full = w_down_g.reshape(D_FF, d)
    tmd = min(1024, t)
    dy, dy16, loss_tile = _down_loss("down_proj", act, w_down_full, x1, target)
    loss = lax.psum(loss_tile[0, 0], ("x", "y", "c"))

    tk = min(2048, t)
    nkt = t // tk
    g_down = _mm("g_down", act, dy16, dep=loss.reshape(1, 1), grid=(D_FF // 512, 1, nkt),
                 a_spec=pl.BlockSpec((tk, 512), lambda i, j, k: (k, i)),
                 b_spec=pl.BlockSpec((tk, d), lambda i, j, k: (k, 0)),
                 o_spec=pl.BlockSpec((512, d), lambda i, j, k: (i, 0)),
                 out_shape=_sds((D_FF, d), F32), ca=0, cb=0, nk=nkt)
    down_g_fly = _exchange_start("reduce_down_start", [g_down.reshape(N_DEV, D_FF // N_DEV, d)], [True], dy16)
    dgu3 = _d_gate_up("d_gate_up", dy16, w_down_full, gu3, down_g_fly["token"])
    g_gu = _mm("g_gate_up", h2, dgu3, grid=(d // 512, N_DEV, nkt),
               a_spec=pl.BlockSpec((tk, 512), lambda i, j, k: (k, i)),
               b_spec=pl.BlockSpec((None, tk, n_gu), lambda i, j, k: (j // per, k, j % per)),
               o_spec=pl.BlockSpec((None, 512, n_gu), lambda i, j, k: (j, i, 0)),
               out_shape=_sds((N_DEV, d, n_gu), F32), ca=0, cb=0, nk=nkt)
    gu_g_fly = _exchange_start("reduce_gate_up_start", [g_gu], [True], dy16)
    tmh, tnh = min(2048, t), 1024
    dh2 = _mm("d_h2", dgu3, w_gu_g, dep=gu_g_fly["token"], grid=(t // tmh, d // tnh, N_DEV),
              a_spec=pl.BlockSpec((None, tmh, n_gu), lambda i, j, k: (k // per, i, k % per)),
              b_spec=pl.BlockSpec((None, tnh, n_gu), lambda i, j, k: (k, j, 0)),
              o_spec=pl.BlockSpec((tmh, tnh), lambda i, j, k: (i, j)),
              out_shape=_sds((t, d), F32), ca=1, cb=1, nk=N_DEV)
    dx1, dx1_16, g_ffn_norm = _rms_bwd("norm2_bwd", x1, ffn_norm_w, dh2, dy)

    g_out = _mm("g_out", mixed, dx1_16, grid=((2 * GW) // 512, 1, nkt),
                a_spec=pl.BlockSpec((tk, 512), lambda i, j, k: (k, i)),
                b_spec=pl.BlockSpec((tk, d), lambda i, j, k: (k, 0)),
                o_spec=pl.BlockSpec((512, d), lambda i, j, k: (i, 0)),
                out_shape=_sds((2 * GW, d), F32), ca=0, cb=0, nk=nkt)
    out_g_fly = _exchange_start("reduce_out_start", [g_out.reshape(N_DEV, (2 * GW) // N_DEV, d)], [True], g_ffn_norm)
    dmixed = _mm("d_mixed", dx1_16, w_out_full, dep=out_g_fly["token"], grid=(t // tm, (2 * GW) // tn, 1),
                 a_spec=pl.BlockSpec((tm, d), lambda i, j, k: (i, 0)),
                 b_spec=pl.BlockSpec((tn, d), lambda i, j, k: (j, 0)),
                 o_spec=pl.BlockSpec((tm, tn), lambda i, j, k: (i, j)),
                 out_shape=_sds((t, 2 * GW), F32), ca=1, cb=1, nk=1)
    doa, dproj, dob, delta, g_dn, g_an = _mix_bwd("mix_bwd", dmixed, oa_raw, proj, 3, ob,
                                                  delta_out_norm_w, attn_out_norm_w, out_g_fly["token"])
    d_aq, d_ak, d_av = _attn_bwd("attn_bwd", aq, ak, proj, 6, dob, lse, delta)
    dproj, g_qn = _qk_bwd("attn_q_bwd", proj, 4, q_norm_w, pos_col, inv_row, d_aq, dproj)
    dproj, g_kn = _qk_bwd("attn_k_bwd", proj, 5, k_norm_w, pos_col, inv_row, d_ak, dproj)
    dproj = _cast_into("attn_v_bwd", d_av, dproj, 6)

    dvn, dqd, dkd, dw, ddec = _delta_scan_bwd("delta_scan_bwd", doa, w, p, qd, kd, gc_b, vn, s_hist)
    dqn, dkn, dvv, dbeta_b, dg_b = _delta_prep_bwd("delta_prep_bwd", qn, kn, vv, beta_b, gc_b, tinv, u, w, vn,
                                                   doa, dvn, dqd, dkd, dw, ddec)
    dproj, gcw_q = _conv_bwd("conv_q_bwd", proj, conv_w8, dqn, dproj, 0, True, HD ** -0.5)
    dproj, gcw_k = _conv_bwd("conv_k_bwd", proj, conv_w8, dkn, dproj, 1, True, 1.0)
    dproj, gcw_v = _conv_bwd("conv_v_bwd", proj, conv_w8, dvv, dproj, 2, False, 1.0)
    dproj, g_alog_row, g_dtb_row = _gates_bwd("gates_bwd", proj, small_blk, alog_row, dtb_row, dbeta_b, dg_b, dproj)
    tmc = n_cat // 3
    g_cat = _mm("g_in", dproj, h1, grid=(3, d // 512, nkt),
                a_spec=pl.BlockSpec((tk, tmc), lambda i, j, k: (k, i)),
                b_spec=pl.BlockSpec((tk, 512), lambda i, j, k: (k, j)),
                o_spec=pl.BlockSpec((tmc, 512), lambda i, j, k: (i, j)),
                out_shape=_sds((n_cat, d), F32), ca=0, cb=0, nk=nkt)
    parts = []
    for j in range(N_DEV):
        cols = []
        for lo, hi, start in sorted(segments):
            a, b = max(lo, j * n_in), min(hi, (j + 1) * n_in)
            if a < b:
                cols.append(g_cat[start + a - lo:start + b - lo])
        parts.append(cols[0] if len(cols) == 1 else jnp.concatenate(cols, axis=0))
    g_in_parts = jnp.stack(parts).astype(BF16)
    g_conv = jnp.concatenate([gcw_q, gcw_k, gcw_v], axis=1)
    n_cw = conv_w.shape[2]
    g_conv_parts = jnp.transpose(g_conv.reshape(8, N_DEV, n_cw), (1, 0, 2))
    in_g_fly = _exchange_start("reduce_in_start", [g_in_parts, g_conv_parts], [True] * 2, g_dtb_row)
    tkc = n_cat // 3
    dh1 = _mm("d_h1", dproj, w_cat, dep=in_g_fly["token"], grid=(t // tmd, d // tn, 3),
              a_spec=pl.BlockSpec((tmd, tkc), lambda i, j, k: (i, k)),
              b_spec=pl.BlockSpec((tn, tkc), lambda i, j, k: (j, k)),
              o_spec=pl.BlockSpec((tmd, tn), lambda i, j, k: (i, j)),
              out_shape=_sds((t, d), F32), ca=1, cb=1, nk=3)
    grad_x, _, g_attn_norm = _rms_bwd("norm1_bwd", x2, attn_norm_w, dh1, dx1)

    small_rows = [g_attn_norm.reshape(d // HD, HD), g_ffn_norm.reshape(d // HD, HD), g_dn, g_qn, g_kn, g_an,
                  g_alog_row, g_dtb_row]
    small_pack = _pad_rows(jnp.concatenate(small_rows, axis=0), 40)
    (r_down,) = _exchange_wait("reduce_down_wait", down_g_fly, grad_x)
    (r_gu,) = _exchange_wait("reduce_gate_up_wait", gu_g_fly, grad_x)
    (r_out,) = _exchange_wait("reduce_out_wait", out_g_fly, grad_x)
    res_gu = [a[None] for a in _adamw("adamw_gate_up", r_gu, w_gate_up[0], m_w_gate_up[0], v_w_gate_up[0])]
    res_down = [a[None] for a in _adamw("adamw_down", r_down, w_down[0], m_w_down[0], v_w_down[0])]
    res_out = [a[None] for a in _adamw("adamw_out", r_out, w_out[0], m_w_out[0], v_w_out[0])]
    done = (res_gu[3][0, :1, :1] + res_down[3][0, :1, :1] + res_out[3][0, :1, :1])
    (r_small,) = _exchange("gather_small_grads", [small_pack], [False], done)

    def pack_small(an, fn, dn, qn_, kn_, aon, al, db):
        rows = [an.reshape(d // HD, HD), fn.reshape(d // HD, HD), dn, qn_, kn_, aon,
                _lane_row(al[0], 8), _lane_row(db[0], 8)]
        return _pad_rows(jnp.concatenate(rows, axis=0), 40)

    def unpack_small(pk):
        nr = d // HD
        return dict(attn_norm_w=pk[:nr].reshape(1, d), ffn_norm_w=pk[nr:2 * nr].reshape(1, d),
                    delta_out_norm_w=pk[2 * nr:2 * nr + 1], q_norm_w=pk[2 * nr + 1:2 * nr + 2],
                    k_norm_w=pk[2 * nr + 2:2 * nr + 3], attn_out_norm_w=pk[2 * nr + 3:2 * nr + 4],
                    a_log=pk[2 * nr + 4:2 * nr + 5, 8:16], dt_bias=pk[2 * nr + 5:2 * nr + 6, 8:16])

    res_small = _adamw("adamw_small", r_small,
                       pack_small(attn_norm_w, ffn_norm_w, delta_out_norm_w, q_norm_w, k_norm_w, attn_out_norm_w, a_log, dt_bias),
                       pack_small(m_attn_norm_w, m_ffn_norm_w, m_delta_out_norm_w, m_q_norm_w, m_k_norm_w, m_attn_out_norm_w, m_a_log, m_dt_bias),
                       pack_small(v_attn_norm_w, v_ffn_norm_w, v_delta_out_norm_w, v_q_norm_w, v_k_norm_w, v_attn_out_norm_w, v_a_log, v_dt_bias))
    small = [unpack_small(a) for a in res_small]
    r_in, r_conv = _exchange_wait("reduce_in_wait", in_g_fly, res_small[0])
    res_in = [jnp.transpose(a)[None] for a in _adamw("adamw_in", r_in, jnp.transpose(w_in[0]), jnp.transpose(m_w_in[0]),
                                                     jnp.transpose(v_w_in[0]))]
    res_conv =[a[None, :4] for a in _adamw("adamw_conv", r_conv, _pad_rows(conv_w[0], 8), _pad_rows(m_conv_w[0], 8),
                                            _pad_rows(v_conv_w[0], 8))]

    outs = [loss, grad_x[None]]
    for i in range(4):
        s = small[i]
        outs += [s["attn_norm_w"], res_in[i], res_conv[i], s["a_log"], s["dt_bias"], s["delta_out_norm_w"],
                 s["q_norm_w"], s["k_norm_w"], s["attn_out_norm_w"], res_out[i], s["ffn_norm_w"], res_gu[i],
                 res_down[i]]
    return tuple(outs)
```

```python
import functools

import numpy as np
import jax
import jax.numpy as jnp
from jax import lax
from jax.experimental import pallas as pl
from jax.experimental.pallas import tpu as pltpu

F32 = jnp.float32
BF16 = jnp.bfloat16

N_DEV = 8
N_HEADS = 8
HD = 128
GW = N_HEADS * HD
CHUNK = 64
PAIR = 2 * CHUNK
SPAN = 128
DILATIONS = (1, 4, 16)
ROPE_THETA = 10000.0
EPS = 1e-6
D_FF = 5632
ADAM_LR, ADAM_B1, ADAM_B2, ADAM_EPS, ADAM_WD, ADAM_STEP = 0.001, 0.9, 0.999, 1e-8, 0.01, 10
NEG = -1e30
VMEM_LIMIT = 56 * 1024 * 1024
ANY = pl.BlockSpec(memory_space=pl.ANY)
HEADS_PER_STEP = 8


def _params(n_grid, vmem=VMEM_LIMIT):
    return pltpu.CompilerParams(dimension_semantics=("arbitrary",) * n_grid, vmem_limit_bytes=vmem)


def _sds(shape, dtype):
    return jax.ShapeDtypeStruct(tuple(shape), dtype)


def _sigmoid(x):
    return 1.0 / (1.0 + jnp.exp(-x))


def _silu(x):
    return x * _sigmoid(x)


def _softplus(x):
    return jnp.maximum(x, 0.0) + jnp.log(1.0 + jnp.exp(-jnp.abs(x)))


def _dot(a, b, ca, cb, precision=None):
    return lax.dot_general(a, b, (((ca,), (cb,)), ((), ())), precision=precision,
                           preferred_element_type=F32)


def _b16(x):
    return x if x.dtype == BF16 else x.astype(BF16)


def _split(x):
    hi = x.astype(BF16)
    return hi, (x - hi.astype(F32)).astype(BF16)


def _dot3(a, b, ca, cb):
    a_hi, a_lo = _split(a)
    b_hi, b_lo = _split(b)
    return _dot(a_hi, b_hi, ca, cb) + (_dot(a_hi, b_lo, ca, cb) + _dot(a_lo, b_hi, ca, cb))


def _iota2(shape, axis):
    return lax.broadcasted_iota(jnp.int32, shape, axis)


def _mm(name, a, b, *, grid, a_spec, b_spec, o_spec, out_shape, ca, cb, nk, add=None, add_spec=None,
        dep=None, vmem=VMEM_LIMIT):
    has_add = add is not None
    n_in = 2 + has_add + (dep is not None)

    def body(*refs):
        a_ref, b_ref = refs[0], refs[1]
        e_ref = refs[2] if has_add else None
        o_ref = refs[n_in]
        part = _dot(_b16(a_ref[...]), _b16(b_ref[...]), ca, cb)
        if nk == 1:
            if has_add:
                part = part + e_ref[...]
            o_ref[...] = part.astype(o_ref.dtype)
            return
        acc = refs[-1]
        k = pl.program_id(2)

        @pl.when(k == 0)
        def _():
            acc[...] = part

        @pl.when(k > 0)
        def _():
            acc[...] += part

        @pl.when(k == nk - 1)
        def _():
            res = acc[...]
            if has_add:
                res = res + e_ref[...]
            o_ref[...] = res.astype(o_ref.dtype)

    in_specs = [a_spec, b_spec] + ([add_spec] if has_add else []) + ([ANY] if dep is not None else [])
    args = (a, b) + ((add,) if has_add else ()) + ((dep,) if dep is not None else ())
    blk = [d for d in o_spec.block_shape if d is not None]
    scratch = [pltpu.VMEM(tuple(blk), F32)] if nk > 1 else []
    return pl.pallas_call(body, grid=grid, in_specs=in_specs, out_specs=o_spec, out_shape=out_shape,
                          scratch_shapes=scratch, name=name, compiler_params=_params(3, vmem))(*args)


def _rms_f(xv, wv):
    return xv * lax.rsqrt(jnp.mean(xv * xv, axis=-1, keepdims=True) + EPS) * wv


def _rms_fwd(name, x, w, dep):
    t, d = x.shape
    tm = min(512, t)

    def body(x_ref, w_ref, dep_ref, o_ref):
        o_ref[...] = _rms_f(x_ref[...], w_ref[...]).astype(BF16)

    row = pl.BlockSpec((tm, d), lambda i: (i, 0))
    vec = pl.BlockSpec((1, d), lambda i: (0, 0))
    return pl.pallas_call(body, grid=(t // tm,), in_specs=[row, vec, ANY], out_specs=row,
                          out_shape=_sds((t, d), BF16), name=name, compiler_params=_params(1))(x, w, dep)


def _rms_bwd(name, x, w, dh, res):
    t, d = x.shape
    tm = min(256, t)

    def body(x_ref, w_ref, dh_ref, res_ref, dx_ref, dx16_ref, dw_ref):
        _, vjp = jax.vjp(_rms_f, x_ref[...], w_ref[...])
        dxv, dwv = vjp(dh_ref[...])
        dxv = dxv + res_ref[...]
        dx_ref[...] = dxv
        dx16_ref[...] = dxv.astype(BF16)

        @pl.when(pl.program_id(0) == 0)
        def _():
            dw_ref[...] = jnp.zeros_like(dw_ref)

        dw_ref[...] += dwv

    row = pl.BlockSpec((tm, d), lambda i: (i, 0))
    vec = pl.BlockSpec((1, d), lambda i: (0, 0))
    return pl.pallas_call(body, grid=(t // tm,), in_specs=[row, vec, row, row], out_specs=[row, row, vec],
                          out_shape=[_sds((t, d), F32), _sds((t, d), BF16), _sds((1, d), F32)], name=name,
                          compiler_params=_params(1))(x, w, dh, res)


def _conv_taps(xv, w_ref, rows):
    c = w_ref[3:4, :] * xv
    for s in (1, 2, 3):
        c = c + w_ref[3 - s:4 - s, :] * jnp.where(rows >= s, pltpu.roll(xv, s, 0), 0.0)
    return c


def _post_conv(c, l2, scale):
    y = _silu(c)
    if l2:
        y = y * lax.rsqrt(jnp.sum(y * y, axis=-1, keepdims=True) + EPS) * scale
    return y


def _conv_fwd(name, proj, conv_w8, group, l2, scale):
    t = proj.shape[0]

    def body(x_ref, w_ref, o_ref):
        rows = _iota2((t, HD), 0)
        o_ref[...] = _post_conv(_conv_taps(x_ref[...], w_ref, rows), l2, scale)

    return pl.pallas_call(
        body, grid=(N_HEADS,),
        in_specs=[pl.BlockSpec((t, HD), lambda h: (0, h + group * N_HEADS)),
                  pl.BlockSpec((8, HD), lambda h: (0, h + group * N_HEADS))],
        out_specs=pl.BlockSpec((t, HD), lambda h: (0, h)),
        out_shape=_sds((t, GW), F32), name=name, compiler_params=_params(1, VMEM_LIMIT))(proj, conv_w8)


def _conv_bwd(name, proj, conv_w8, dn, dproj, group, l2, scale):
    t = proj.shape[0]

    def body(x_ref, w_ref, dn_ref, dproj_ref, dx_ref, dw_ref):
        rows = _iota2((t, HD), 0)
        xv = x_ref[...]
        c = _conv_taps(xv, w_ref, rows)
        _, vjp = jax.vjp(lambda cc: _post_conv(cc, l2, scale), c)
        (dc,) = vjp(dn_ref[...])
        dx = w_ref[3:4, :] * dc
        dw = jnp.zeros((8, HD), F32)
        rid = _iota2((8, HD), 0)
        dw = dw + jnp.where(rid == 3, jnp.sum(dc * xv, axis=0, keepdims=True), 0.0)
        for s in (1, 2, 3):
            dx = dx + w_ref[3 - s:4 - s, :] * jnp.where(rows < t - s, pltpu.roll(dc, t - s, 0), 0.0)
            xs = jnp.where(rows >= s, pltpu.roll(xv, s, 0), 0.0)
            dw = dw + jnp.where(rid == 3 - s, jnp.sum(dc * xs, axis=0, keepdims=True), 0.0)
        dx_ref[...] = dx.astype(BF16)
        dw_ref[...] = dw

    return pl.pallas_call(
        body, grid=(N_HEADS,),
        in_specs=[pl.BlockSpec((t, HD), lambda h: (0, h + group * N_HEADS)),
                  pl.BlockSpec((8, HD), lambda h: (0, h + group * N_HEADS)),
                  pl.BlockSpec((t, HD), lambda h: (0, h)), ANY],
        out_specs=[pl.BlockSpec((t, HD), lambda h: (0, h + group * N_HEADS)), pl.BlockSpec((8, HD), lambda h: (0, h))],
        out_shape=[_sds(dproj.shape, BF16), _sds((8, GW), F32)], input_output_aliases={3: 0}, name=name,
        compiler_params=_params(1, VMEM_LIMIT))(proj, conv_w8, dn, dproj)


def _chunk_cumsum(g, rows):
    pos = rows % CHUNK
    s = 1
    while s < CHUNK:
        g = g + jnp.where(pos >= s, pltpu.roll(g, s, 0), 0.0)
        s *= 2
    return g


def _gates_fwd(name, proj, small_blk, alog_row, dtb_row):
    t = proj.shape[0]
    tm = min(256, t)

    def body(s_ref, a_ref, b_ref, beta_ref, gc_ref):
        sm = s_ref[...]
        beta = _sigmoid(sm)
        g = -jnp.exp(a_ref[...]) * _softplus(sm + b_ref[...])
        gc = _chunk_cumsum(g, _iota2((tm, HD), 0))
        lane = _iota2((tm, HD), 1)
        for h in range(N_HEADS):
            bcol = jnp.sum(jnp.where(lane == h, beta, 0.0), axis=1, keepdims=True)
            gcol = jnp.sum(jnp.where(lane == 8 + h, gc, 0.0), axis=1, keepdims=True)
            beta_ref[:, h * HD:(h + 1) * HD] = jnp.broadcast_to(bcol, (tm, HD))
            gc_ref[:, h * HD:(h + 1) * HD] = jnp.broadcast_to(gcol, (tm, HD))

    vec = pl.BlockSpec((1, HD), lambda i: (0, 0))
    wide = pl.BlockSpec((tm, GW), lambda i: (i, 0))
    return pl.pallas_call(
        body, grid=(t // tm,),
        in_specs=[pl.BlockSpec((tm, HD), lambda i: (i, small_blk)), vec, vec], out_specs=[wide, wide],
        out_shape=[_sds((t, GW), F32), _sds((t, GW), F32)], name=name,
        compiler_params=_params(1))(proj, alog_row, dtb_row)


def _gates_bwd(name, proj, small_blk, alog_row, dtb_row, dbeta_b, dg_b, dproj):
    t = proj.shape[0]
    tm = min(256, t)

    def body(s_ref, a_ref, b_ref, db_ref, dg_ref, dproj_ref, ds_ref, da_ref, dbias_ref):
        sm = s_ref[...]
        lane = _iota2((tm, HD), 1)
        db = jnp.zeros((tm, HD), F32)
        dg = jnp.zeros((tm, HD), F32)
        for h in range(N_HEADS):
            db = db + jnp.where(lane == h, db_ref[:, h * HD:(h + 1) * HD], 0.0)
            dg = dg + jnp.where(lane == 8 + h, dg_ref[:, h * HD:(h + 1) * HD], 0.0)
        beta = _sigmoid(sm)
        ea = jnp.exp(a_ref[...])
        pre = sm + b_ref[...]
        g = -ea * _softplus(pre)
        dpre = dg * (-ea) * _sigmoid(pre)
        ds_ref[...] = (db * beta * (1.0 - beta) + dpre).astype(BF16)

        @pl.when(pl.program_id(0) == 0)
        def _():
            da_ref[...] = jnp.zeros_like(da_ref)
            dbias_ref[...] = jnp.zeros_like(dbias_ref)

        da_ref[...] += jnp.sum(dg * g, axis=0, keepdims=True)
        dbias_ref[...] += jnp.sum(dpre, axis=0, keepdims=True)

    vec = pl.BlockSpec((1, HD), lambda i: (0, 0))
    wide = pl.BlockSpec((tm, GW), lambda i: (i, 0))
    return pl.pallas_call(
        body, grid=(t // tm,),
        in_specs=[pl.BlockSpec((tm, HD), lambda i: (i, small_blk)), vec, vec, wide, wide, ANY],
        out_specs=[pl.BlockSpec((tm, HD), lambda i: (i, small_blk)), vec, vec],
        out_shape=[_sds(dproj.shape, BF16), _sds((1, HD), F32), _sds((1, HD), F32)],
        input_output_aliases={5: 0}, name=name,
        compiler_params=_params(1))(proj, alog_row, dtb_row, dbeta_b, dg_b, dproj)


def _pair_masks():
    ii = _iota2((PAIR, PAIR), 0)
    jj = _iota2((PAIR, PAIR), 1)
    same = (ii // CHUNK) == (jj // CHUNK)
    return ii, jj, same & (ii >= jj), same & (ii > jj)


def _to_row(col_b, ii, jj):
    return jnp.sum(jnp.where(ii == jj, col_b, 0.0), axis=0, keepdims=True)


def _to_col(row, ii, jj):
    return jnp.sum(jnp.where(ii == jj, jnp.broadcast_to(row, (PAIR, PAIR)), 0.0), axis=1, keepdims=True)


def _decay_parts(gc, last_a, last_b, ii, jj, causal):
    diff = gc - _to_row(gc, ii, jj)
    dmat = jnp.where(causal, jnp.exp(jnp.where(causal, diff, 0.0)), 0.0)
    glast = jnp.where(ii < CHUNK, last_a, last_b)
    return dmat, jnp.exp(gc), jnp.exp(glast - gc)


def _unit_lower_inverse(lows, ii, jj):
    eye = jnp.where(ii == jj, 1.0, 0.0)
    mm = lambda xs, ys: [_dot3(a, b, 1, 0) for a, b in zip(xs, ys)]
    plus = lambda xs: [eye + a for a in xs]
    minus = lambda xs: [eye - a for a in xs]
    d1 = [jnp.where((ii // 16) == (jj // 16), low, 0.0) for low in lows]
    d2 = mm(d1, d1)
    a = mm(minus(d1), plus(d2))
    d4 = mm(d2, d2)
    a = mm(a, plus(d4))
    d8 = mm(d4, d4)
    td = mm(a, plus(d8))
    n1 = mm(td, [low - d for low, d in zip(lows, d1)])
    n2 = mm(n1, n1)
    return mm(mm(minus(n1), plus(n2)), td)


def _delta_prep(name, qn, kn, vv, beta_b, gc_b):
    t = qn.shape[0]

    def body(q_ref, k_ref, v_ref, b_ref, g_ref, u_ref, w_ref, p_ref, t_ref, qd_ref, kd_ref):
        ii, jj, causal, strict = _pair_masks()
        sls = [slice(hh * HD, (hh + 1) * HD) for hh in range(HEADS_PER_STEP)]
        lows = []
        for sl in sls:
            q, k, beta = q_ref[:, sl], k_ref[:, sl], b_ref[:, sl]
            dmat, gam, e2 = _decay_parts(g_ref[:, sl], g_ref[CHUNK - 1:CHUNK, sl], g_ref[PAIR - 1:PAIR, sl],
                                         ii, jj, causal)
            k16 = _b16(k)
            lows.append(jnp.where(strict, beta * _dot(k16, k16, 1, 1) * dmat, 0.0))
            p_ref[:, sl] = jnp.where(causal, _dot(_b16(q), k16, 1, 1) * dmat, 0.0).astype(BF16)
            qd_ref[:, sl] = (q * gam).astype(BF16)
            kd_ref[:, sl] = (k * e2).astype(BF16)
        for sl, tinv in zip(sls, _unit_lower_inverse(lows, ii, jj)):
            beta = b_ref[:, sl]
            t_ref[:, sl] = tinv
            u_ref[:, sl] = _dot3(tinv, v_ref[:, sl] * beta, 1, 0)
            w_ref[:, sl] = _dot3(tinv, k_ref[:, sl] * (beta * jnp.exp(g_ref[:, sl])), 1, 0).astype(BF16)

    blk = pl.BlockSpec((PAIR, HEADS_PER_STEP * HD), lambda i, h: (i, h))
    return pl.pallas_call(
        body, grid=(t // PAIR, N_HEADS // HEADS_PER_STEP), in_specs=[blk] * 5, out_specs=[blk] * 6,
        out_shape=[_sds((t, GW), F32), _sds((t, GW), BF16), _sds((t, GW), BF16), _sds((t, GW), F32),
                   _sds((t, GW), BF16), _sds((t, GW), BF16)],
        name=name, compiler_params=_params(2))(qn, kn, vv, beta_b, gc_b)


def _delta_scan(name, u, w, p, qd, kd, gc_b):
    t = u.shape[0]
    n = t // CHUNK

    def body(u_ref, w_ref, p_ref, qd_ref, kd_ref, g_ref, o_ref, vn_ref, sh_ref, state):
        @pl.when(pl.program_id(0) == 0)
        def _():
            state[...] = jnp.zeros_like(state)

        sls = [slice(h * HD, (h + 1) * HD) for h in range(N_HEADS)]
        heads = range(N_HEADS)
        s = [state[h] for h in heads]
        for h in heads:
            sh_ref[h] = s[h]
        s16 = [_b16(a) for a in s]
        ws = [_dot(w_ref[:, sls[h]], s16[h], 1, 0) for h in heads]
        qs = [_dot(qd_ref[:, sls[h]], s16[h], 1, 0) for h in heads]
        vn16 = [_b16(u_ref[:, sls[h]] - ws[h]) for h in heads]
        pv = [_dot(p_ref[:, sls[h]], jnp.concatenate([vn16[h], vn16[h]], axis=0), 1, 0) for h in heads]
        kv = [_dot(kd_ref[:, sls[h]], vn16[h], 0, 0) for h in heads]
        for h in heads:
            o_ref[:, sls[h]] = qs[h] + pv[h]
            vn_ref[:, sls[h]] = vn16[h]
            state[h] = s[h] * jnp.exp(g_ref[CHUNK - 1:CHUNK, sls[h]]) + kv[h]

    blk = pl.BlockSpec((CHUNK, GW), lambda i: (i, 0))
    return pl.pallas_call(
        body, grid=(n,), in_specs=[blk] * 6,
        out_specs=[blk, blk, pl.BlockSpec((None, N_HEADS, HD, HD), lambda i: (i, 0, 0, 0))],
        out_shape=[_sds((t, GW), F32), _sds((t, GW), BF16), _sds((n, N_HEADS, HD, HD), F32)],
        scratch_shapes=[pltpu.VMEM((N_HEADS, HD, HD), F32)], name=name,
        compiler_params=_params(1))(u, w, p, qd, kd, gc_b)


def _delta_scan_bwd(name, do, w, p, qd, kd, gc_b, vn, s_hist):
    t = do.shape[0]
    n = t // CHUNK

    def body(do_ref, w_ref, p_ref, qd_ref, kd_ref, g_ref, vn_ref, sh_ref,
             dvn_ref, dqd_ref, dkd_ref, dw_ref, ddec_ref, dstate):
        @pl.when(pl.program_id(0) == 0)
        def _():
            dstate[...] = jnp.zeros_like(dstate)

        sls = [slice(h * HD, (h + 1) * HD) for h in range(N_HEADS)]
        heads = range(N_HEADS)
        ds = [dstate[h] for h in heads]
        ds16 = [_b16(a) for a in ds]
        s16 = [_b16(sh_ref[h]) for h in heads]
        do16 = [_b16(do_ref[:, sls[h]]) for h in heads]
        ptdo = [_dot(p_ref[:, sls[h]], do16[h], 0, 0) for h in heads]
        kds = [_dot(kd_ref[:, sls[h]], ds16[h], 1, 0) for h in heads]
        qdo = [_dot(qd_ref[:, sls[h]], do16[h], 0, 0) for h in heads]
        for h in heads:
            dqd_ref[:, sls[h]] = _dot(do16[h], s16[h], 1, 1)
            dkd_ref[:, sls[h]] = _dot(vn_ref[:, sls[h]], ds16[h], 1, 1)
        dvn = [ptdo[h][:CHUNK, :] + ptdo[h][CHUNK:, :] + kds[h] for h in heads]
        dvn16 = [_b16(a) for a in dvn]
        wdv = [_dot(w_ref[:, sls[h]], dvn16[h], 0, 0) for h in heads]
        for h in heads:
            dvn_ref[:, sls[h]] = dvn[h]
            dw_ref[:, sls[h]] = -_dot(dvn16[h], s16[h], 1, 1)
            tot = jnp.sum(jnp.sum(sh_ref[h] * ds[h], axis=1, keepdims=True), axis=0, keepdims=True)
            ddec_ref[:, sls[h]] = jnp.broadcast_to(tot, (8, HD))
            dstate[h] = ds[h] * jnp.exp(g_ref[CHUNK - 1:CHUNK, sls[h]]) + qdo[h] - wdv[h]

    blk = pl.BlockSpec((CHUNK, GW), lambda i: (n - 1 - i, 0))
    return pl.pallas_call(
        body, grid=(n,),
        in_specs=[blk] * 7 + [pl.BlockSpec((None, N_HEADS, HD, HD), lambda i: (n - 1 - i, 0, 0, 0))],
        out_specs=[blk] * 4 + [pl.BlockSpec((8, GW), lambda i: (n - 1 - i, 0))],
        out_shape=[_sds((t, GW), F32)] * 4 + [_sds((n * 8, GW), F32)],
        scratch_shapes=[pltpu.VMEM((N_HEADS, HD, HD), F32)], name=name,
        compiler_params=_params(1))(do, w, p, qd, kd, gc_b, vn, s_hist)


def _delta_prep_bwd(name, qn, kn, vv, beta_b, gc_b, tinv, u, w, vn, do, dvn, dqd, dkd, dw, ddec):
    t = qn.shape[0]

    def body(q_ref, k_ref, v_ref, b_ref, g_ref, t_ref, u_ref, w_ref, vn_ref, do_ref, dvn_ref, dqd_ref,
             dkd_ref, dw_ref, ddec_ref, dq_ref, dk_ref, dv_ref, dbeta_ref, dg_ref):
        ii, jj, causal, strict = _pair_masks()
        suffix = ((ii // CHUNK) == (jj // CHUNK)) & (jj >= ii)
        first = ii < CHUNK
        rs = lambda a: jnp.sum(a, axis=1, keepdims=True)
        sls = [slice(hh * HD, (hh + 1) * HD) for hh in range(HEADS_PER_STEP)]
        xs = [_dot3(t_ref[:, sl], dvn_ref[:, sl], 0, 0) for sl in sls]
        ys = [_dot3(t_ref[:, sl], dw_ref[:, sl], 0, 0) for sl in sls]
        k16s = [_b16(k_ref[:, sl]) for sl in sls]
        kks = [_dot(k16, k16, 1, 1) for k16 in k16s]
        qks = [_dot(_b16(q_ref[:, sl]), k16, 1, 1) for sl, k16 in zip(sls, k16s)]
        dps = [jnp.where(causal, _dot(_b16(do_ref[:, sl]), vn_ref[:, sl], 1, 1), 0.0) for sl in sls]
        das = [-jnp.where(strict, _dot(_b16(x), _b16(u_ref[:, sl]), 1, 1) + _dot(_b16(y), w_ref[:, sl], 1, 1), 0.0)
               for sl, x, y in zip(sls, xs, ys)]
        for hh, sl in enumerate(sls):
            q, k, v, beta, gc = q_ref[:, sl], k_ref[:, sl], v_ref[:, sl], b_ref[:, sl], g_ref[:, sl]
            last_a, last_b = g_ref[CHUNK - 1:CHUNK, sl], g_ref[PAIR - 1:PAIR, sl]
            dmat, gam, e2 = _decay_parts(gc, last_a, last_b, ii, jj, causal)
            q16, k16 = _b16(q), k16s[hh]
            kk, qk, dp, x, y, da = kks[hh], qks[hh], dps[hh], xs[hh], ys[hh], das[hh]
            dqd, dkd = dqd_ref[:, sl], dkd_ref[:, sl]
            dpd16 = _b16(dp * dmat)
            dkk16 = _b16(da * beta * dmat)
            dq_ref[:, sl] = gam * dqd + _dot(dpd16, k16, 1, 0)
            dk_ref[:, sl] = (e2 * dkd + _dot(dpd16, q16, 0, 0) + beta * gam * y
                             + _dot(dkk16, k16, 1, 0) + _dot(dkk16, k16, 0, 0))
            dv_ref[:, sl] = beta * x
            dbeta = rs(v * x) + rs(k * gam * y) + rs(da * kk * dmat)
            dbeta_ref[:, sl] = jnp.broadcast_to(dbeta, (PAIR, HD))
            m = (dp * qk + da * beta * kk) * dmat
            dgam = rs(q * dqd) + rs(k * beta * y)
            de2 = rs(k * dkd)
            colsum = _to_col(jnp.sum(m, axis=0, keepdims=True), ii, jj)
            te2 = de2 * e2
            dgc = rs(m) - colsum + gam * dgam - te2
            tail_a = jnp.sum(jnp.where(first, te2, 0.0), axis=0, keepdims=True)
            tail_b = jnp.sum(jnp.where(first, 0.0, te2), axis=0, keepdims=True)
            dgc = dgc + jnp.where(ii == CHUNK - 1, tail_a + ddec_ref[0:1, sl] * jnp.exp(last_a), 0.0)
            dgc = dgc + jnp.where(ii == PAIR - 1, tail_b + ddec_ref[8:9, sl] * jnp.exp(last_b), 0.0)
            dgc_row = _to_row(dgc, ii, jj)
            dg = jnp.sum(jnp.where(suffix, jnp.broadcast_to(dgc_row, (PAIR, PAIR)), 0.0), axis=1, keepdims=True)
            dg_ref[:, sl] = jnp.broadcast_to(dg, (PAIR, HD))

    blk = pl.BlockSpec((PAIR, HEADS_PER_STEP * HD), lambda i, h: (i, h))
    return pl.pallas_call(
        body, grid=(t // PAIR, N_HEADS // HEADS_PER_STEP),
        in_specs=[blk] * 14 + [pl.BlockSpec((16, HEADS_PER_STEP * HD), lambda i, h: (i, h))], out_specs=[blk] * 5,
        out_shape=[_sds((t, GW), F32)] * 5, name=name,
        compiler_params=_params(2))(qn, kn, vv, beta_b, gc_b, tinv, u, w, vn, do, dvn, dqd, dkd, dw, ddec)


def _rope_tables(pos_col, inv_row):
    ang = pos_col.astype(F32) * inv_row
    lane = _iota2(ang.shape, 1)
    return jnp.cos(ang), jnp.where(lane < HD // 2, -1.0, 1.0) * jnp.sin(ang)


def _head_rms(xh, wv):
    return xh * lax.rsqrt(jnp.mean(xh * xh, axis=-1, keepdims=True) + EPS) * wv


def _qk_fwd(name, proj, blk_idx, w_row, pos_col, inv_row):
    t = proj.shape[0]
    tm = min(256, t)

    def body(x_ref, w_ref, pos_ref, inv_ref, o_ref):
        cos, sin = _rope_tables(pos_ref[...], inv_ref[...])
        for h in range(N_HEADS):
            y = _head_rms(x_ref[:, h * HD:(h + 1) * HD], w_ref[...])
            o_ref[:, h * HD:(h + 1) * HD] = y * cos + pltpu.roll(y, HD // 2, 1) * sin

    vec = pl.BlockSpec((1, HD), lambda i: (0, 0))
    return pl.pallas_call(
        body, grid=(t // tm,),
        in_specs=[pl.BlockSpec((tm, GW), lambda i: (i, blk_idx)), vec, pl.BlockSpec((tm, 1), lambda i: (i, 0)), vec],
        out_specs=pl.BlockSpec((tm, GW), lambda i: (i, 0)), out_shape=_sds((t, GW), F32), name=name,
        compiler_params=_params(1))(proj, w_row, pos_col, inv_row)


def _qk_bwd(name, proj, blk_idx, w_row, pos_col, inv_row, dy_full, dproj):
    t = proj.shape[0]
    tm = min(256, t)

    def body(x_ref, w_ref, pos_ref, inv_ref, dy_ref, dproj_ref, dx_ref, dw_ref):
        cos, sin = _rope_tables(pos_ref[...], inv_ref[...])
        dw = jnp.zeros((1, HD), F32)
        for h in range(N_HEADS):
            sl = slice(h * HD, (h + 1) * HD)
            dy = dy_ref[:, sl]
            dy = dy * cos - pltpu.roll(dy, HD // 2, 1) * sin
            _, vjp = jax.vjp(_head_rms, x_ref[:, sl], w_ref[...])
            dx, dwh = vjp(dy)
            dw = dw + dwh
            dx_ref[:, sl] = dx.astype(BF16)

        @pl.when(pl.program_id(0) == 0)
        def _():
            dw_ref[...] = jnp.zeros_like(dw_ref)

        dw_ref[...] += dw

    vec = pl.BlockSpec((1, HD), lambda i: (0, 0))
    wide = pl.BlockSpec((tm, GW), lambda i: (i, 0))
    return pl.pallas_call(
        body, grid=(t // tm,),
        in_specs=[pl.BlockSpec((tm, GW), lambda i: (i, blk_idx)), vec, pl.BlockSpec((tm, 1), lambda i: (i, 0)), vec,
                  wide, ANY],
        out_specs=[pl.BlockSpec((tm, GW), lambda i: (i, blk_idx)), vec],
        out_shape=[_sds(dproj.shape, BF16), _sds((1, HD), F32)], input_output_aliases={5: 0}, name=name,
        compiler_params=_params(1))(proj, w_row, pos_col, inv_row, dy_full, dproj)


def _cast_into(name, x, dproj, blk_idx):
    t = x.shape[0]
    tm = min(512, t)

    def body(x_ref, dproj_ref, o_ref):
        o_ref[...] = x_ref[...].astype(BF16)

    return pl.pallas_call(
        body, grid=(t // tm,), in_specs=[pl.BlockSpec((tm, GW), lambda i: (i, 0)), ANY],
        out_specs=pl.BlockSpec((tm, GW), lambda i: (i, blk_idx)), out_shape=_sds(dproj.shape, BF16),
        input_output_aliases={1: 0}, name=name, compiler_params=_params(1))(x, dproj)


GROUP = SPAN * max(DILATIONS)
SCALE = HD ** -0.5
TILE_BATCH = 8


def _band_mask(lo):
    qi = _iota2((SPAN, 2 * SPAN), 0)
    ki = _iota2((SPAN, 2 * SPAN), 1)
    return (ki >= qi) & (ki <= qi + SPAN) & (ki >= lo)


def _tiles():
    return [(pi, r, u, rho) for pi, r in enumerate(DILATIONS) for rho in range(r) for u in range(GROUP // (SPAN * r))]


def _rows(r, u, rho):
    return pl.ds(u * SPAN * r + rho, SPAN, stride=r) if r > 1 else pl.ds(u * SPAN, SPAN)


def _attn_fwd(name, q, k, v, v_blk):
    t = q.shape[0]

    def body(qc_ref, kc_ref, vc_ref, kp_ref, vp_ref, ob_ref, lse_ref, o_scr, l_scr):
        mask_in = _band_mask(0)
        mask_edge = _band_mask(jnp.where(pl.program_id(0) == 0, SPAN, 0))
        tiles = _tiles()
        k_own = v_own = None
        for b0 in range(0, len(tiles), TILE_BATCH):
            work = []
            for pi, r, u, rho in tiles[b0:b0 + TILE_BATCH]:
                rows = _rows(r, u, rho)
                if u > 0:
                    k_prev, v_prev, mask = k_own, v_own, mask_in
                else:
                    prows = _rows(r, GROUP // (SPAN * r) - 1, rho)
                    k_prev, v_prev, mask = kp_ref[prows, :].astype(BF16), vp_ref[prows, :].astype(BF16), mask_edge
                k_own, v_own = kc_ref[rows, :].astype(BF16), vc_ref[rows, :].astype(BF16)
                work.append((pi, rows, mask, qc_ref[rows, :].astype(BF16), jnp.concatenate([k_prev, k_own], axis=0),
                             jnp.concatenate([v_prev, v_own], axis=0)))
            scores = [_dot(qt, kcat, 1, 1) for _, _, _, qt, kcat, _ in work]
            soft = []
            for (_, _, mask, _, _, _), s in zip(work, scores):
                s = jnp.where(mask, s * SCALE, NEG)
                m = jnp.max(s, axis=1, keepdims=True)
                p = jnp.exp(s - m)
                soft.append((m, _b16(p), jnp.sum(p, axis=1, keepdims=True)))
            outs = [_dot(p, vcat, 1, 0) for (_, p, _), (_, _, _, _, _, vcat) in zip(soft, work)]
            for (pi, rows, _, _, _, _), (m, _, den), o in zip(work, soft, outs):
                o_scr[pi, rows, :] = o / den
                l_scr[pi, rows, :] = jnp.broadcast_to(m + jnp.log(den), (SPAN, HD))
        step = 256
        for c in range(GROUP // step):
            sl = pl.ds(c * step, step)
            ob, lse = _merge([o_scr[i, sl, :] for i in range(3)], [l_scr[i, sl, :] for i in range(3)])
            ob_ref[sl, :] = ob
            lse_ref[sl, :] = lse

    cur = pl.BlockSpec((GROUP, HD), lambda g, h: (g, h))
    prev = pl.BlockSpec((GROUP, HD), lambda g, h: (jnp.maximum(g - 1, 0), h))
    vcur = pl.BlockSpec((GROUP, HD), lambda g, h: (g, v_blk * N_HEADS + h))
    vprev = pl.BlockSpec((GROUP, HD), lambda g, h: (jnp.maximum(g - 1, 0), v_blk * N_HEADS + h))
    return pl.pallas_call(
        body, grid=(t // GROUP, N_HEADS), in_specs=[cur, cur, vcur, prev, vprev], out_specs=[cur, cur],
        out_shape=[_sds((t, GW), F32), _sds((t, GW), F32)],
        scratch_shapes=[pltpu.VMEM((3, GROUP, HD), F32), pltpu.VMEM((3, GROUP, HD), F32)], name=name,
        compiler_params=_params(2))(q, k, v, k, v)


def _attn_bwd(name, q, k, v, v_blk, do, lse, delta):
    t = q.shape[0]
    ng = t // GROUP

    def probs(work):
        scores = [_dot(qt, kcat, 1, 1) for qt, _, _, _, kcat, _, _ in work]
        dps = [_dot(dot, vcat, 1, 1) for _, dot, _, _, _, vcat, _ in work]
        out = []
        for (_, _, lt, dlt, kcat, _, mask), s, dp in zip(work, scores, dps):
            wide = kcat.shape[0] // SPAN
            lw = jnp.concatenate([lt] * wide, axis=1) if wide > 1 else lt
            dw = jnp.concatenate([dlt] * wide, axis=1) if wide > 1 else dlt
            p = jnp.exp(jnp.where(mask, s * SCALE - lw, NEG))
            out.append((_b16(p * (dp - dw) * SCALE), _b16(p)))
        return out

    def body(qc_ref, kc_ref, vc_ref, doc_ref, lc_ref, dc_ref, kp_ref, vp_ref, qn_ref, don_ref, ln_ref, dn_ref,
             dq_ref, dk_ref, dv_ref):
        g = pl.program_id(0)
        mask_in = _band_mask(0)
        mask_edge = _band_mask(jnp.where(g == 0, SPAN, 0))
        dk_ref[...] = jnp.zeros_like(dk_ref)
        dv_ref[...] = jnp.zeros_like(dv_ref)
        tiles = _tiles()
        k_own = v_own = None
        for b0 in range(0, len(tiles), TILE_BATCH):
            where, work = [], []
            for pi, r, u, rho in tiles[b0:b0 + TILE_BATCH]:
                rows = _rows(r, u, rho)
                if u > 0:
                    prows, k_prev, v_prev, mask = _rows(r, u - 1, rho), k_own, v_own, mask_in
                else:
                    prows = _rows(r, GROUP // (SPAN * r) - 1, rho)
                    k_prev, v_prev, mask = kp_ref[prows, :].astype(BF16), vp_ref[prows, :].astype(BF16), mask_edge
                k_own, v_own = kc_ref[rows, :].astype(BF16), vc_ref[rows, :].astype(BF16)
                where.append((pi, u, rows, prows))
                work.append((qc_ref[rows, :].astype(BF16), doc_ref[rows, :].astype(BF16), lc_ref[rows, :], dc_ref[rows, :],
                             jnp.concatenate([k_prev, k_own], axis=0), jnp.concatenate([v_prev, v_own], axis=0), mask))
            dsp = probs(work)
            dqs = [_dot(ds, w[4], 1, 0) for (ds, _), w in zip(dsp, work)]
            dks = [_dot(ds, w[0], 0, 0) for (ds, _), w in zip(dsp, work)]
            dvs = [_dot(p, w[1], 0, 0) for (_, p), w in zip(dsp, work)]
            for (pi, u, rows, prows), dq_t, dk2, dv2 in zip(where, dqs, dks, dvs):
                if pi == 0:
                    dq_ref[rows, :] = dq_t
                else:
                    dq_ref[rows, :] += dq_t
                dk_ref[rows, :] += dk2[SPAN:, :]
                dv_ref[rows, :] += dv2[SPAN:, :]
                if u > 0:
                    dk_ref[prows, :] += dk2[:SPAN, :]
                    dv_ref[prows, :] += dv2[:SPAN, :]
        qi = _iota2((SPAN, SPAN), 0)
        ki = _iota2((SPAN, SPAN), 1)
        mask_next = (ki >= qi) & (ki < jnp.where(g == ng - 1, 0, SPAN))
        edge = [(r, rho) for r in DILATIONS for rho in range(r)]
        for b0 in range(0, len(edge), TILE_BATCH):
            where, work = [], []
            for r, rho in edge[b0:b0 + TILE_BATCH]:
                krows, qrows = _rows(r, GROUP // (SPAN * r) - 1, rho), _rows(r, 0, rho)
                where.append(krows)
                work.append((qn_ref[qrows, :].astype(BF16), don_ref[qrows, :].astype(BF16), ln_ref[qrows, :],
                             dn_ref[qrows, :], kc_ref[krows, :].astype(BF16), vc_ref[krows, :].astype(BF16), mask_next))
            dsp = probs(work)
            dks = [_dot(ds, w[0], 0, 0) for (ds, _), w in zip(dsp, work)]
            dvs = [_dot(p, w[1], 0, 0) for (_, p), w in zip(dsp, work)]
            for krows, dk1, dv1 in zip(where, dks, dvs):
                dk_ref[krows, :] += dk1
                dv_ref[krows, :] += dv1

    cur = pl.BlockSpec((GROUP, HD), lambda g, h: (g, h))
    prev = pl.BlockSpec((GROUP, HD), lambda g, h: (jnp.maximum(g - 1, 0), h))
    nxt = pl.BlockSpec((GROUP, HD), lambda g, h: (jnp.minimum(g + 1, ng - 1), h))
    vcur = pl.BlockSpec((GROUP, HD), lambda g, h: (g, v_blk * N_HEADS + h))
    vprev = pl.BlockSpec((GROUP, HD), lambda g, h: (jnp.maximum(g - 1, 0), v_blk * N_HEADS + h))
    return pl.pallas_call(
        body, grid=(ng, N_HEADS), in_specs=[cur, cur, vcur, cur, cur, cur, prev, vprev] + [nxt] * 4,
        out_specs=[cur] * 3,
        out_shape=[_sds((t, GW), F32)] * 3, name=name,
        compiler_params=_params(2))(q, k, v, do, lse, delta, k, v, q, do, lse, delta)


def _merge(os_, ls_):
    m = jnp.maximum(jnp.maximum(ls_[0], ls_[1]), ls_[2])
    ws = [jnp.exp(l - m) for l in ls_]
    tot = ws[0] + ws[1] + ws[2]
    ob = (ws[0] * os_[0] + ws[1] * os_[1] + ws[2] * os_[2]) / tot
    return ob, m + jnp.log(tot)


def _gated_norm(oa, z, wv):
    return _head_rms(oa, wv) * _silu(z)


def _mix_fwd(name, oa_raw, proj, z_blk, ob, w_dn, w_an):
    t = oa_raw.shape[0]
    tm = min(256, t)

    def body(oa_ref, z_ref, ob_ref, wd_ref, wa_ref, mix_ref):
        for h in range(N_HEADS):
            sl = slice(h * HD, (h + 1) * HD)
            mix_ref[:, sl] = _gated_norm(oa_ref[:, sl], z_ref[:, sl], wd_ref[...]).astype(BF16)
            mix_ref[:, GW + h * HD:GW + (h + 1) * HD] = _head_rms(ob_ref[:, sl], wa_ref[...]).astype(BF16)

    vec = pl.BlockSpec((1, HD), lambda i: (0, 0))
    wide = pl.BlockSpec((tm, GW), lambda i: (i, 0))
    return pl.pallas_call(
        body, grid=(t // tm,),
        in_specs=[wide, pl.BlockSpec((tm, GW), lambda i: (i, z_blk)), wide, vec, vec],
        out_specs=pl.BlockSpec((tm, 2 * GW), lambda i: (i, 0)),
        out_shape=_sds((t, 2 * GW), BF16), name=name,
        compiler_params=_params(1))(oa_raw, proj, ob, w_dn, w_an)


def _mix_bwd(name, dmixed, oa_raw, proj, z_blk, ob, w_dn, w_an, dep):
    t = oa_raw.shape[0]
    tm = min(256, t)

    def body(dm_ref, oa_ref, z_ref, ob_ref, wd_ref, wa_ref, dep_ref,
             doa_ref, dz_ref, dob_ref, dl_ref, dwd_ref, dwa_ref):
        dwd = jnp.zeros((1, HD), F32)
        dwa = jnp.zeros((1, HD), F32)
        for h in range(N_HEADS):
            sl = slice(h * HD, (h + 1) * HD)
            _, vjp = jax.vjp(_gated_norm, oa_ref[:, sl], z_ref[:, sl], wd_ref[...])
            doa, dz, dw1 = vjp(dm_ref[:, sl])
            doa_ref[:, sl] = doa
            dz_ref[:, sl] = dz.astype(BF16)
            dwd = dwd + dw1
            obh = ob_ref[:, sl]
            _, vjp2 = jax.vjp(_head_rms, obh, wa_ref[...])
            dob, dw2 = vjp2(dm_ref[:, GW + h * HD:GW + (h + 1) * HD])
            dwa = dwa + dw2
            dob_ref[:, sl] = dob
            dl_ref[:, sl] = jnp.broadcast_to(jnp.sum(dob * obh, axis=1, keepdims=True), (tm, HD))

        @pl.when(pl.program_id(0) == 0)
        def _():
            dwd_ref[...] = jnp.zeros_like(dwd_ref)
            dwa_ref[...] = jnp.zeros_like(dwa_ref)

        dwd_ref[...] += dwd
        dwa_ref[...] += dwa

    vec = pl.BlockSpec((1, HD), lambda i: (0, 0))
    wide = pl.BlockSpec((tm, GW), lambda i: (i, 0))
    return pl.pallas_call(
        body, grid=(t // tm,),
        in_specs=[pl.BlockSpec((tm, 2 * GW), lambda i: (i, 0)), wide, pl.BlockSpec((tm, GW), lambda i: (i, z_blk)),
                  wide, vec, vec, ANY],
        out_specs=[wide, pl.BlockSpec((tm, GW), lambda i: (i, z_blk)), wide, wide, vec, vec],
        out_shape=[_sds((t, GW), F32), _sds(proj.shape, BF16), _sds((t, GW), F32), _sds((t, GW), F32),
                   _sds((1, HD), F32), _sds((1, HD), F32)], name=name,
        compiler_params=_params(1))(dmixed, oa_raw, proj, ob, w_dn, w_an, dep)


def _gate_up_swiglu(name, h2, w_gu_g):
    t, d = h2.shape
    n = w_gu_g.shape[2]
    per = N_DEV // 2
    tm = min(512, t)

    def body(a_ref, bg_ref, bu_ref, gu_ref, act_ref):
        a = a_ref[...]
        g = _dot(a, bg_ref[...], 1, 0)
        up = _dot(a, bu_ref[...], 1, 0)
        gu_ref[0] = g.astype(BF16)
        gu_ref[1] = up.astype(BF16)
        act_ref[...] = (_silu(g) * up).astype(BF16)

    return pl.pallas_call(
        body, grid=(per, t // tm),
        in_specs=[pl.BlockSpec((tm, d), lambda j, i: (i, 0)), pl.BlockSpec((None, d, n), lambda j, i: (j, 0, 0)),
                  pl.BlockSpec((None, d, n), lambda j, i: (j + per, 0, 0))],
        out_specs=[pl.BlockSpec((2, tm, n), lambda j, i: (0, i, j)), pl.BlockSpec((tm, n), lambda j, i: (i, j))],
        out_shape=[_sds((2, t, per * n), BF16), _sds((t, per * n), BF16)], name=name,
        compiler_params=_params(2))(h2, w_gu_g, w_gu_g)


def _d_gate_up(name, dy16, w_down, gu3, dep):
    t, d = dy16.shape
    f = w_down.shape[0]
    tm, tn = min(1024, t), f // 4

    def body(a_ref, b_ref, g_ref, dep_ref, o_ref):
        dact = _dot(a_ref[...], b_ref[...], 1, 1)
        g, up = g_ref[0].astype(F32), g_ref[1].astype(F32)
        sg = _sigmoid(g)
        o_ref[0] = (dact * up * sg * (1.0 + g * (1.0 - sg))).astype(BF16)
        o_ref[1] = (dact * g * sg).astype(BF16)

    return pl.pallas_call(
        body, grid=(f // tn, t // tm),
        in_specs=[pl.BlockSpec((tm, d), lambda j, i: (i, 0)), pl.BlockSpec((tn, d), lambda j, i: (j, 0)),
                  pl.BlockSpec((2, tm, tn), lambda j, i: (0, i, j)), ANY],
        out_specs=pl.BlockSpec((2, tm, tn), lambda j, i: (0, i, j)), out_shape=_sds((2, t, f), BF16), name=name,
        compiler_params=_params(2))(dy16, w_down, gu3, dep)


def _out_proj_norm(name, mixed, w_out, x, w_norm):
    t, d = x.shape
    kdim = mixed.shape[1]
    tm = min(512, t)

    def body(a_ref, b_ref, x_ref, w_ref, x1_ref, h_ref):
        x1 = x_ref[...] + _dot(a_ref[...], b_ref[...], 1, 0)
        x1_ref[...] = x1
        h_ref[...] = _rms_f(x1, w_ref[...]).astype(BF16)

    row = pl.BlockSpec((tm, d), lambda i: (i, 0))
    return pl.pallas_call(
        body, grid=(t // tm,),
        in_specs=[pl.BlockSpec((tm, kdim), lambda i: (i, 0)), pl.BlockSpec((kdim, d), lambda i: (0, 0)), row,
                  pl.BlockSpec((1, d), lambda i: (0, 0))],
        out_specs=[row, row], out_shape=[_sds((t, d), F32), _sds((t, d), BF16)], name=name,
        compiler_params=_params(1))(mixed, w_out, x, w_norm)


def _down_loss(name, act, w_down, x1, target):
    t, f = act.shape
    d = x1.shape[1]
    tm, tn, nk = min(1024, t), 512, 2
    tk = f // nk

    def body(a_ref, b_ref, x_ref, t_ref, dy_ref, dy16_ref, l_ref, acc):
        i, j, k = pl.program_id(0), pl.program_id(1), pl.program_id(2)
        part = _dot(a_ref[...], b_ref[...], 1, 0)

        @pl.when(k == 0)
        def _():
            acc[...] = part

        @pl.when(k > 0)
        def _():
            acc[...] += part

        @pl.when(k == nk - 1)
        def _():
            diff = acc[...] + x_ref[...] - t_ref[...]
            dyv = diff * (1.0 / d)
            dy_ref[...] = dyv
            dy16_ref[...] = dyv.astype(BF16)
            tot = jnp.sum(jnp.sum(diff * diff, axis=1, keepdims=True), axis=0, keepdims=True) * (0.5 / d)

            @pl.when((i == 0) & (j == 0))
            def _():
                l_ref[...] = jnp.zeros_like(l_ref)

            l_ref[...] += jnp.broadcast_to(tot, (8, 128))

    tile = pl.BlockSpec((tm, tn), lambda i, j, k: (i, j))
    return pl.pallas_call(
        body, grid=(t // tm, d // tn, nk),
        in_specs=[pl.BlockSpec((tm, tk), lambda i, j, k: (i, k)), pl.BlockSpec((tk, tn), lambda i, j, k: (k, j)),
                  tile, tile],
        out_specs=[tile, tile, pl.BlockSpec((8, 128), lambda i, j, k: (0, 0))],
        out_shape=[_sds((t, d), F32), _sds((t, d), BF16), _sds((8, 128), F32)],
        scratch_shapes=[pltpu.VMEM((tm, tn), F32)], name=name,
        compiler_params=_params(3))(act, w_down, x1, target)


def _peer(me, k):
    pid = (me + k) % N_DEV
    return (pid // 4, (pid // 2) % 2, pid % 2)


def _my_id():
    return 4 * lax.axis_index("x") + 2 * lax.axis_index("y") + lax.axis_index("c")


def _exchange(name, arrays, scatter, dep):
    n = len(arrays)

    def body(*refs):
        ins, outs = refs[:n], refs[n + 1:2 * n + 1]
        send_sems, recv_sems, local_sems = refs[2 * n + 1:]
        me = _my_id()
        started = []
        for a in range(n):
            src = ins[a].at[me] if scatter[a] else ins[a]
            loc = pltpu.make_async_copy(src, outs[a].at[me], local_sems.at[a])
            loc.start()
            started.append(loc)
        remote = []
        for k in range(1, N_DEV):
            to = (me + k) % N_DEV
            for a in range(n):
                src = ins[a].at[to] if scatter[a] else ins[a]
                cp = pltpu.make_async_remote_copy(src_ref=src, dst_ref=outs[a].at[me],
                                                  send_sem=send_sems.at[a * (N_DEV - 1) + k - 1], recv_sem=recv_sems.at[a * (N_DEV - 1) + k - 1],
                                                  device_id=_peer(me, k), device_id_type=pl.DeviceIdType.MESH)
                cp.start()
                remote.append(cp)
        for k in range(1, N_DEV):
            frm = (me + N_DEV - k) % N_DEV
            for a in range(n):
                src = ins[a].at[frm] if scatter[a] else ins[a]
                pltpu.make_async_remote_copy(src_ref=src, dst_ref=outs[a].at[frm],
                                             send_sem=send_sems.at[a * (N_DEV - 1) + k - 1], recv_sem=recv_sems.at[a * (N_DEV - 1) + k - 1],
                                             device_id=_peer(me, k), device_id_type=pl.DeviceIdType.MESH).wait_recv()
        for cp in remote:
            cp.wait_send()
        for loc in started:
            loc.wait()

    out_shape = [_sds((N_DEV,) + (a.shape[1:] if sc else a.shape), a.dtype) for a, sc in zip(arrays, scatter)]
    return pl.pallas_call(
        body, in_specs=[ANY] * (n + 1), out_specs=[ANY] * n, out_shape=out_shape,
        scratch_shapes=[pltpu.SemaphoreType.DMA((n * (N_DEV - 1),)), pltpu.SemaphoreType.DMA((n * (N_DEV - 1),)),
                        pltpu.SemaphoreType.DMA((n,))],
        name=name)(*arrays, dep)


def _gather_two_level(name, arrays):
    n = len(arrays)
    per = N_DEV - 1

    def body(*refs):
        ins, outs = refs[:n], refs[n:2 * n]
        send_sems, recv_sems, local_sems = refs[2 * n:]
        x, y, c = lax.axis_index("x"), lax.axis_index("y"), lax.axis_index("c")
        me, sibling = (x, y, c), (x, y, 1 - c)
        chips = [(1 - x, y), (x, 1 - y), (1 - x, 1 - y)]

        def copy(a, k, block, to, src=None):
            slot = outs[a].at[4 * block[0] + 2 * block[1] + block[2]]
            return pltpu.make_async_remote_copy(
                src_ref=slot if src is None else src, dst_ref=slot, send_sem=send_sems.at[a * per + k],
                recv_sem=recv_sems.at[a * per + k], device_id=to, device_id_type=pl.DeviceIdType.MESH)

        mine = [pltpu.make_async_copy(ins[a], outs[a].at[4 * x + 2 * y + c], local_sems.at[a]) for a in range(n)]
        for cp in mine:
            cp.start()
        first = [copy(a, 0, me, sibling, src=ins[a]) for a in range(n)]
        first += [copy(a, 1 + j, me, (*chip, c), src=ins[a]) for j, chip in enumerate(chips) for a in range(n)]
        for cp in first:
            cp.start()
        passed = []
        for j, chip in enumerate(chips):
            for a in range(n):
                copy(a, 1 + j, (*chip, c), me).wait_recv()
                cp = copy(a, 4 + j, (*chip, c), sibling)
                cp.start()
                passed.append(cp)
        for a in range(n):
            copy(a, 0, sibling, me).wait_recv()
            for j, chip in enumerate(chips):
                copy(a, 4 + j, (*chip, 1 - c), me).wait_recv()
        for cp in first + passed:
            cp.wait_send()
        for cp in mine:
            cp.wait()

    return pl.pallas_call(
        body, in_specs=[ANY] * n, out_specs=[ANY] * n,
        out_shape=[_sds((N_DEV,) + a.shape, a.dtype) for a in arrays],
        scratch_shapes=[pltpu.SemaphoreType.DMA((n * per,)), pltpu.SemaphoreType.DMA((n * per,)),
                        pltpu.SemaphoreType.DMA((n,))],
        name=name)(*arrays)


HBM = pl.BlockSpec(memory_space=pltpu.HBM)
SEM = pl.BlockSpec(memory_space=pltpu.SEMAPHORE)
EFFECT = pltpu.SideEffectType.DATAFLOW_SIDE_EFFECTING


def _remote_copies(srcs, lands, scatter, send_sems, recv_sems, me, incoming):
    out = []
    for k in range(1, N_DEV):
        other = (me + N_DEV - k) % N_DEV if incoming else (me + k) % N_DEV
        for a in range(len(srcs)):
            sem = a * (N_DEV - 1) + k - 1
            src = srcs[a].at[other] if scatter[a] else srcs[a]
            dst = lands[a].at[other if incoming else me]
            out.append(pltpu.make_async_remote_copy(src_ref=src, dst_ref=dst, send_sem=send_sems.at[sem],
                                                    recv_sem=recv_sems.at[sem], device_id=_peer(me, k),
                                                    device_id_type=pl.DeviceIdType.MESH))
    return out


def _exchange_start(name, arrays, scatter, dep):
    n = len(arrays)
    lands = [lax.empty((N_DEV,) + (a.shape[1:] if sc else a.shape), a.dtype) for a, sc in zip(arrays, scatter)]

    def body(*refs):
        srcs, land_refs = refs[:n], refs[n:2 * n]
        send_sems, recv_sems = refs[2 * n + 1], refs[2 * n + 2]
        token = refs[-1]
        for cp in _remote_copies(srcs, land_refs, scatter, send_sems, recv_sems, _my_id(), False):
            cp.start()
        token[...] = jnp.zeros_like(token)

    n_sem = n * (N_DEV - 1)
    out_shape = ([pltpu.SemaphoreType.DMA((n_sem,)), pltpu.SemaphoreType.DMA((n_sem,))]
                 + [pltpu.HBM(a.shape, a.dtype) for a in arrays] + [pltpu.HBM(l.shape, l.dtype) for l in lands]
                 + [_sds((8, 128), F32)])
    aliases = {i: 2 + i for i in range(2 * n)}
    args = [pltpu.with_memory_space_constraint(a, pltpu.HBM) for a in list(arrays) + lands] + [dep]
    res = pl.pallas_call(
        body, name=name, in_specs=[HBM] * (2 * n) + [ANY], out_shape=out_shape,
        out_specs=[SEM, SEM] + [HBM] * (2 * n) + [pl.BlockSpec(memory_space=pltpu.VMEM)],
        input_output_aliases=aliases, compiler_params=pltpu.CompilerParams(has_side_effects=EFFECT))(*args)
    return dict(send=res[0], recv=res[1], srcs=res[2:2 + n], lands=res[2 + n:2 + 2 * n], token=res[-1],
                scatter=scatter)


def _exchange_wait(name, started, after):
    n = len(started["srcs"])
    scatter = started["scatter"]

    def body(*refs):
        srcs, land_refs = refs[:n], refs[n:2 * n]
        send_sems, recv_sems = refs[2 * n], refs[2 * n + 1]
        me = _my_id()
        for cp in _remote_copies(srcs, land_refs, scatter, send_sems, recv_sems, me, False):
            cp.wait_send()
        for cp in _remote_copies(srcs, land_refs, scatter, send_sems, recv_sems, me, True):
            cp.wait_recv()

    arrs = list(started["srcs"]) + list(started["lands"])
    res = pl.pallas_call(
        body, name=name, in_specs=[HBM] * (2 * n) + [SEM, SEM, ANY],
        out_shape=[pltpu.HBM(a.shape, a.dtype) for a in arrs], out_specs=[HBM] * (2 * n),
        input_output_aliases={i: i for i in range(2 * n)},
        compiler_params=pltpu.CompilerParams(has_side_effects=EFFECT))(*arrs, started["send"], started["recv"], after)
    me = _my_id()
    out = []
    for src, land, sc in zip(res[:n], res[n:], scatter):
        own = lax.dynamic_index_in_dim(src, me, 0, keepdims=True) if sc else src[None]
        out.append(lax.dynamic_update_slice(land, own, (me,) + (0,) * (land.ndim - 1)))
    return out


def _adamw(name, parts, w, m, v):
    r, c = w.shape
    tr, tc = r, c
    if r % 8 == 0:
        tr = next(cand for cand in (128, 88, 64, 40, 8) if r % cand == 0)
    else:
        tc = 256
    c1 = 1.0 / (1.0 - ADAM_B1 ** ADAM_STEP)
    c2 = 1.0 / (1.0 - ADAM_B2 ** ADAM_STEP)

    def body(p_ref, w_ref, m_ref, v_ref, g_ref, d_ref, nm_ref, nv_ref):
        g = p_ref[0].astype(F32)
        for s in range(1, N_DEV):
            g = g + p_ref[s].astype(F32)
        mn = ADAM_B1 * m_ref[...] + (1.0 - ADAM_B1) * g
        vn = ADAM_B2 * v_ref[...] + (1.0 - ADAM_B2) * (g * g)
        g_ref[...] = g
        nm_ref[...] = mn
        nv_ref[...] = vn
        d_ref[...] = -ADAM_LR * ((mn * c1) / (jnp.sqrt(vn * c2) + ADAM_EPS) + ADAM_WD * w_ref[...])

    blk = pl.BlockSpec((tr, tc), lambda i, j: (i, j))
    return pl.pallas_call(
        body, grid=(r // tr, c // tc),
        in_specs=[pl.BlockSpec((N_DEV, tr, tc), lambda i, j: (0, i, j)), blk, blk, blk],
        out_specs=[blk] * 4, out_shape=[_sds((r, c), F32)] * 4, name=name,
        compiler_params=_params(2, VMEM_LIMIT))(parts, w, m, v)


def _pad_rows(a, rows):
    return jnp.pad(a, ((0, rows - a.shape[0]), (0, 0)))


def _lane_row(vec8, offset):
    return jnp.pad(vec8.reshape(1, 8), ((0, 0), (offset, HD - 8 - offset)))


def kernel(x, positions, attn_norm_w, w_in, conv_w, a_log, dt_bias, delta_out_norm_w, q_norm_w, k_norm_w, attn_out_norm_w, w_out, ffn_norm_w, w_gate_up, w_down, loss_target, m_attn_norm_w, m_w_in, m_conv_w, m_a_log, m_dt_bias, m_delta_out_norm_w, m_q_norm_w, m_k_norm_w, m_attn_out_norm_w, m_w_out, m_ffn_norm_w, m_w_gate_up, m_w_down, v_attn_norm_w, v_w_in, v_conv_w, v_a_log, v_dt_bias, v_delta_out_norm_w, v_q_norm_w, v_k_norm_w, v_attn_out_norm_w, v_w_out, v_ffn_norm_w, v_w_gate_up, v_w_down):
    x2 = x[0]
    t, d = x2.shape
    target = loss_target[0]
    pos_col = positions.reshape(t, 1)
    half = HD // 2
    inv = (ROPE_THETA ** (-np.arange(half, dtype=np.float32) / half)).astype(np.float32)
    inv_row = jnp.asarray(np.concatenate([inv, inv]).reshape(1, HD))

    n_in = w_in.shape[2]
    n_gu = w_gate_up.shape[2]
    w_in_g, conv_g = _gather_two_level("gather_in", [w_in[0].astype(BF16), _pad_rows(conv_w[0], 8)])
    out_fly = _exchange_start("gather_out_start", [w_out[0].astype(BF16)], [False], conv_g)
    gu_fly = _exchange_start("gather_gate_up_start", [w_gate_up[0].astype(BF16)], [False], out_fly["token"])
    down_fly = _exchange_start("gather_down_start", [w_down[0].astype(BF16)], [False], gu_fly["token"])
    n_main = 4 * GW
    n_small = 2 * N_HEADS
    segments = [(0, n_main, 0), (n_main + n_small, N_DEV * n_in, n_main), (n_main, n_main + n_small, 7 * GW)]
    pieces = []
    for lo, hi, _ in segments:
        f = lo
        while f < hi:
            j = f // n_in
            end = min(hi, (j + 1) * n_in)
            pieces.append(w_in_g[j][:, f - j * n_in:end - j * n_in])
            f = end
    w_cat = jnp.concatenate(pieces + [jnp.zeros((d, HD - n_small), BF16)], axis=1)
    n_cat = w_cat.shape[1]
    small_blk = (7 * GW) // HD
    conv_w8 =jnp.transpose(conv_g, (1, 0, 2)).reshape(8, 3 * GW)
    alog_row = _lane_row(a_log[0], 8)
    dtb_row = _lane_row(dt_bias[0], 8)

    tm = min(2048, t)
    h1 = _rms_fwd("norm1", x2, attn_norm_w, down_fly["token"])
    tn = 384
    proj = _mm("in_proj", h1, w_cat, grid=(t // tm, n_cat // tn, 1),
               a_spec=pl.BlockSpec((tm, d), lambda i, j, k: (i, 0)),
               b_spec=pl.BlockSpec((d, tn), lambda i, j, k: (0, j)),
               o_spec=pl.BlockSpec((tm, tn), lambda i, j, k: (i, j)),
               out_shape=_sds((t, n_cat), F32), ca=1, cb=0, nk=1)
    qn = _conv_fwd("conv_q", proj, conv_w8, 0, True, HD ** -0.5)
    kn = _conv_fwd("conv_k", proj, conv_w8, 1, True, 1.0)
    vv = _conv_fwd("conv_v", proj, conv_w8, 2, False, 1.0)
    beta_b, gc_b = _gates_fwd("gates", proj, small_blk, alog_row, dtb_row)
    u, w, p, tinv, qd, kd = _delta_prep("delta_prep", qn, kn, vv, beta_b, gc_b)
    oa_raw, vn, s_hist = _delta_scan("delta_scan", u, w, p, qd, kd, gc_b)

    aq = _qk_fwd("attn_q", proj, 4, q_norm_w, pos_col, inv_row)
    ak = _qk_fwd("attn_k", proj, 5, k_norm_w, pos_col, inv_row)
    ob, lse = _attn_fwd("attn_fwd", aq, ak, proj, 6)
    mixed = _mix_fwd("mix", oa_raw, proj, 3, ob, delta_out_norm_w, attn_out_norm_w)
    (w_out_g,) = _exchange_wait("gather_out_wait", out_fly, mixed)
    w_out_full = w_out_g.reshape(2 * GW, d)
    tn = 512
    x1, h2 = _out_proj_norm("out_proj", mixed, w_out_full, x2, ffn_norm_w)
    per = N_DEV // 2
    (w_gu_g,) = _exchange_wait("gather_gate_up_wait", gu_fly, h2)
    gu3, act = _gate_up_swiglu("gate_up", h2, w_gu_g)
    (w_down_g,) = _exchange_wait("gather_down_wait", down_fly, act)
    w_down_full = w_down_g.reshape(D_FF, d)
    tmd = min(1024, t)
    dy, dy16, loss_tile = _down_loss("down_proj", act, w_down_full, x1, target)
    loss = lax.psum(loss_tile[0, 0], ("x", "y", "c"))

    tk = min(2048, t)
    nkt = t // tk
    g_down = _mm("g_down", act, dy16, dep=loss.reshape(1, 1), grid=(D_FF // 512, 1, nkt),
                 a_spec=pl.BlockSpec((tk, 512), lambda i, j, k: (k, i)),
                 b_spec=pl.BlockSpec((tk, d), lambda i, j, k: (k, 0)),
                 o_spec=pl.BlockSpec((512, d), lambda i, j, k: (i, 0)),
                 out_shape=_sds((D_FF, d), F32), ca=0, cb=0, nk=nkt)
    down_g_fly = _exchange_start("reduce_down_start", [g_down.reshape(N_DEV, D_FF // N_DEV, d)], [True], dy16)
    dgu3 = _d_gate_up("d_gate_up", dy16, w_down_full, gu3, down_g_fly["token"])
    g_gu = _mm("g_gate_up", h2, dgu3, grid=(d // 512, N_DEV, nkt),
               a_spec=pl.BlockSpec((tk, 512), lambda i, j, k: (k, i)),
               b_spec=pl.BlockSpec((None, tk, n_gu), lambda i, j, k: (j // per, k, j % per)),
               o_spec=pl.BlockSpec((None, 512, n_gu), lambda i, j, k: (j, i, 0)),
               out_shape=_sds((N_DEV, d, n_gu), F32), ca=0, cb=0, nk=nkt)
    gu_g_fly = _exchange_start("reduce_gate_up_start", [g_gu], [True], dy16)
    tmh, tnh = min(2048, t), 1024
    dh2 = _mm("d_h2", dgu3, w_gu_g, dep=gu_g_fly["token"], grid=(t // tmh, d // tnh, N_DEV),
              a_spec=pl.BlockSpec((None, tmh, n_gu), lambda i, j, k: (k // per, i, k % per)),
              b_spec=pl.BlockSpec((None, tnh, n_gu), lambda i, j, k: (k, j, 0)),
              o_spec=pl.BlockSpec((tmh, tnh), lambda i, j, k: (i, j)),
              out_shape=_sds((t, d), F32), ca=1, cb=1, nk=N_DEV)
    dx1, dx1_16, g_ffn_norm = _rms_bwd("norm2_bwd", x1, ffn_norm_w, dh2, dy)

    g_out = _mm("g_out", mixed, dx1_16, grid=((2 * GW) // 512, 1, nkt),
                a_spec=pl.BlockSpec((tk, 512), lambda i, j, k: (k, i)),
                b_spec=pl.BlockSpec((tk, d), lambda i, j, k: (k, 0)),
                o_spec=pl.BlockSpec((512, d), lambda i, j, k: (i, 0)),
                out_shape=_sds((2 * GW, d), F32), ca=0, cb=0, nk=nkt)
    out_g_fly = _exchange_start("reduce_out_start", [g_out.reshape(N_DEV, (2 * GW) // N_DEV, d)], [True], g_ffn_norm)
    dmixed = _mm("d_mixed", dx1_16, w_out_full, dep=out_g_fly["token"], grid=(t // tm, (2 * GW) // tn, 1),
                 a_spec=pl.BlockSpec((tm, d), lambda i, j, k: (i, 0)),
                 b_spec=pl.BlockSpec((tn, d), lambda i, j, k: (j, 0)),
                 o_spec=pl.BlockSpec((tm, tn), lambda i, j, k: (i, j)),
                 out_shape=_sds((t, 2 * GW), F32), ca=1, cb=1, nk=1)
    doa, dproj, dob, delta, g_dn, g_an = _mix_bwd("mix_bwd", dmixed, oa_raw, proj, 3, ob,
                                                  delta_out_norm_w, attn_out_norm_w, out_g_fly["token"])
    d_aq, d_ak, d_av = _attn_bwd("attn_bwd", aq, ak, proj, 6, dob, lse, delta)
    dproj, g_qn = _qk_bwd("attn_q_bwd", proj, 4, q_norm_w, pos_col, inv_row, d_aq, dproj)
    dproj, g_kn = _qk_bwd("attn_k_bwd", proj, 5, k_norm_w, pos_col, inv_row, d_ak, dproj)
    dproj = _cast_into("attn_v_bwd", d_av, dproj, 6)

    dvn, dqd, dkd, dw, ddec = _delta_scan_bwd("delta_scan_bwd", doa, w, p, qd, kd, gc_b, vn, s_hist)
    dqn, dkn, dvv, dbeta_b, dg_b = _delta_prep_bwd("delta_prep_bwd", qn, kn, vv, beta_b, gc_b, tinv, u, w, vn,
                                                   doa, dvn, dqd, dkd, dw, ddec)
    dproj, gcw_q = _conv_bwd("conv_q_bwd", proj, conv_w8, dqn, dproj, 0, True, HD ** -0.5)
    dproj, gcw_k = _conv_bwd("conv_k_bwd", proj, conv_w8, dkn, dproj, 1, True, 1.0)
    dproj, gcw_v = _conv_bwd("conv_v_bwd", proj, conv_w8, dvv, dproj, 2, False, 1.0)
    dproj, g_alog_row, g_dtb_row = _gates_bwd("gates_bwd", proj, small_blk, alog_row, dtb_row, dbeta_b, dg_b, dproj)
    tmc = n_cat // 3
    g_cat = _mm("g_in", dproj, h1, grid=(3, d // 512, nkt),
                a_spec=pl.BlockSpec((tk, tmc), lambda i, j, k: (k, i)),
                b_spec=pl.BlockSpec((tk, 512), lambda i, j, k: (k, j)),
                o_spec=pl.BlockSpec((tmc, 512), lambda i, j, k: (i, j)),
                out_shape=_sds((n_cat, d), F32), ca=0, cb=0, nk=nkt)
    parts = []
    for j in range(N_DEV):
        cols = []
        for lo, hi, start in sorted(segments):
            a, b = max(lo, j * n_in), min(hi, (j + 1) * n_in)
            if a < b:
                cols.append(g_cat[start + a - lo:start + b - lo])
        parts.append(cols[0] if len(cols) == 1 else jnp.concatenate(cols, axis=0))
    g_in_parts = jnp.stack(parts).astype(BF16)
    g_conv = jnp.concatenate([gcw_q, gcw_k, gcw_v], axis=1)
    n_cw = conv_w.shape[2]
    g_conv_parts = jnp.transpose(g_conv.reshape(8, N_DEV, n_cw), (1, 0, 2))
    in_g_fly = _exchange_start("reduce_in_start", [g_in_parts, g_conv_parts], [True] * 2, g_dtb_row)
    tkc = n_cat // 3
    dh1 = _mm("d_h1", dproj, w_cat, dep=in_g_fly["token"], grid=(t // tmd, d // tn, 3),
              a_spec=pl.BlockSpec((tmd, tkc), lambda i, j, k: (i, k)),
              b_spec=pl.BlockSpec((tn, tkc), lambda i, j, k: (j, k)),
              o_spec=pl.BlockSpec((tmd, tn), lambda i, j, k: (i, j)),
              out_shape=_sds((t, d), F32), ca=1, cb=1, nk=3)
    grad_x, _, g_attn_norm = _rms_bwd("norm1_bwd", x2, attn_norm_w, dh1, dx1)

    small_rows = [g_attn_norm.reshape(d // HD, HD), g_ffn_norm.reshape(d // HD, HD), g_dn, g_qn, g_kn, g_an,
                  g_alog_row, g_dtb_row]
    small_pack = _pad_rows(jnp.concatenate(small_rows, axis=0), 40)
    (r_down,) = _exchange_wait("reduce_down_wait", down_g_fly, grad_x)
    (r_gu,) = _exchange_wait("reduce_gate_up_wait", gu_g_fly, grad_x)
    (r_out,) = _exchange_wait("reduce_out_wait", out_g_fly, grad_x)
    res_gu = [a[None] for a in _adamw("adamw_gate_up", r_gu, w_gate_up[0], m_w_gate_up[0], v_w_gate_up[0])]
    res_down = [a[None] for a in _adamw("adamw_down", r_down, w_down[0], m_w_down[0], v_w_down[0])]
    res_out = [a[None] for a in _adamw("adamw_out", r_out, w_out[0], m_w_out[0], v_w_out[0])]
    done = (res_gu[3][0, :1, :1] + res_down[3][0, :1, :1] + res_out[3][0, :1, :1])
    (r_small,) = _exchange("gather_small_grads", [small_pack], [False], done)

    def pack_small(an, fn, dn, qn_, kn_, aon, al, db):
        rows = [an.reshape(d // HD, HD), fn.reshape(d // HD, HD), dn, qn_, kn_, aon,
                _lane_row(al[0], 8), _lane_row(db[0], 8)]
        return _pad_rows(jnp.concatenate(rows, axis=0), 40)

    def unpack_small(pk):
        nr = d // HD
        return dict(attn_norm_w=pk[:nr].reshape(1, d), ffn_norm_w=pk[nr:2 * nr].reshape(1, d),
                    delta_out_norm_w=pk[2 * nr:2 * nr + 1], q_norm_w=pk[2 * nr + 1:2 * nr + 2],
                    k_norm_w=pk[2 * nr + 2:2 * nr + 3], attn_out_norm_w=pk[2 * nr + 3:2 * nr + 4],
                    a_log=pk[2 * nr + 4:2 * nr + 5, 8:16], dt_bias=pk[2 * nr + 5:2 * nr + 6, 8:16])

    res_small = _adamw("adamw_small", r_small,
                       pack_small(attn_norm_w, ffn_norm_w, delta_out_norm_w, q_norm_w, k_norm_w, attn_out_norm_w, a_log, dt_bias),
                       pack_small(m_attn_norm_w, m_ffn_norm_w, m_delta_out_norm_w, m_q_norm_w, m_k_norm_w, m_attn_out_norm_w, m_a_log, m_dt_bias),
                       pack_small(v_attn_norm_w, v_ffn_norm_w, v_delta_out_norm_w, v_q_norm_w, v_k_norm_w, v_attn_out_norm_w, v_a_log, v_dt_bias))
    small = [unpack_small(a) for a in res_small]
    r_in, r_conv = _exchange_wait("reduce_in_wait", in_g_fly, res_small[0])
    res_in = [jnp.transpose(a)[None] for a in _adamw("adamw_in", r_in, jnp.transpose(w_in[0]), jnp.transpose(m_w_in[0]),
                                                     jnp.transpose(v_w_in[0]))]
    res_conv =[a[None, :4] for a in _adamw("adamw_conv", r_conv, _pad_rows(conv_w[0], 8), _pad_rows(m_conv_w[0], 8),
                                            _pad_rows(v_conv_w[0], 8))]

    outs = [loss, grad_x[None]]
    for i in range(4):
        s = small[i]
        outs += [s["attn_norm_w"], res_in[i], res_conv[i], s["a_log"], s["dt_bias"], s["delta_out_norm_w"],
                 s["q_norm_w"], s["k_norm_w"], s["attn_out_norm_w"], res_out[i], s["ffn_norm_w"], res_gu[i],
                 res_down[i]]
    return tuple(outs)
```

```python
import functools

import numpy as np
import jax
import jax.numpy as jnp
from jax import lax
from jax.experimental import pallas as pl
from jax.experimental.pallas import tpu as pltpu

F32 = jnp.float32
BF16 = jnp.bfloat16

N_DEV = 8
N_HEADS = 8
HD = 128
GW = N_HEADS * HD
CHUNK = 64
PAIR = 2 * CHUNK
SPAN = 128
DILATIONS = (1, 4, 16)
ROPE_THETA = 10000.0
EPS = 1e-6
D_FF = 5632
ADAM_LR, ADAM_B1, ADAM_B2, ADAM_EPS, ADAM_WD, ADAM_STEP = 0.001, 0.9, 0.999, 1e-8, 0.01, 10
NEG = -1e30
VMEM_LIMIT = 56 * 1024 * 1024
ANY = pl.BlockSpec(memory_space=pl.ANY)
HEADS_PER_STEP = 8


def _params(n_grid, vmem=VMEM_LIMIT):
    return pltpu.CompilerParams(dimension_semantics=("arbitrary",) * n_grid, vmem_limit_bytes=vmem)


def _sds(shape, dtype):
    return jax.ShapeDtypeStruct(tuple(shape), dtype)


def _sigmoid(x):
    return 1.0 / (1.0 + jnp.exp(-x))


def _silu(x):
    return x * _sigmoid(x)


def _softplus(x):
    return jnp.maximum(x, 0.0) + jnp.log(1.0 + jnp.exp(-jnp.abs(x)))


def _dot(a, b, ca, cb, precision=None):
    return lax.dot_general(a, b, (((ca,), (cb,)), ((), ())), precision=precision,
                           preferred_element_type=F32)


def _b16(x):
    return x if x.dtype == BF16 else x.astype(BF16)


def _split(x):
    hi = x.astype(BF16)
    return hi, (x - hi.astype(F32)).astype(BF16)


def _dot3(a, b, ca, cb):
    a_hi, a_lo = _split(a)
    b_hi, b_lo = _split(b)
    return _dot(a_hi, b_hi, ca, cb) + (_dot(a_hi, b_lo, ca, cb) + _dot(a_lo, b_hi, ca, cb))


def _iota2(shape, axis):
    return lax.broadcasted_iota(jnp.int32, shape, axis)


def _mm(name, a, b, *, grid, a_spec, b_spec, o_spec, out_shape, ca, cb, nk, add=None, add_spec=None,
        dep=None, vmem=VMEM_LIMIT):
    has_add = add is not None
    n_in = 2 + has_add + (dep is not None)

    def body(*refs):
        a_ref, b_ref = refs[0], refs[1]
        e_ref = refs[2] if has_add else None
        o_ref = refs[n_in]
        part = _dot(_b16(a_ref[...]), _b16(b_ref[...]), ca, cb)
        if nk == 1:
            if has_add:
                part = part + e_ref[...]
            o_ref[...] = part.astype(o_ref.dtype)
            return
        acc = refs[-1]
        k = pl.program_id(2)

        @pl.when(k == 0)
        def _():
            acc[...] = part

        @pl.when(k > 0)
        def _():
            acc[...] += part

        @pl.when(k == nk - 1)
        def _():
            res = acc[...]
            if has_add:
                res = res + e_ref[...]
            o_ref[...] = res.astype(o_ref.dtype)

    in_specs = [a_spec, b_spec] + ([add_spec] if has_add else []) + ([ANY] if dep is not None else [])
    args = (a, b) + ((add,) if has_add else ()) + ((dep,) if dep is not None else ())
    blk = [d for d in o_spec.block_shape if d is not None]
    scratch = [pltpu.VMEM(tuple(blk), F32)] if nk > 1 else []
    return pl.pallas_call(body, grid=grid, in_specs=in_specs, out_specs=o_spec, out_shape=out_shape,
                          scratch_shapes=scratch, name=name, compiler_params=_params(3, vmem))(*args)


def _rms_f(xv, wv):
    return xv * lax.rsqrt(jnp.mean(xv * xv, axis=-1, keepdims=True) + EPS) * wv


def _rms_fwd(name, x, w, dep):
    t, d = x.shape
    tm = min(512, t)

    def body(x_ref, w_ref, dep_ref, o_ref):
        o_ref[...] = _rms_f(x_ref[...], w_ref[...]).astype(BF16)

    row = pl.BlockSpec((tm, d), lambda i: (i, 0))
    vec = pl.BlockSpec((1, d), lambda i: (0, 0))
    return pl.pallas_call(body, grid=(t // tm,), in_specs=[row, vec, ANY], out_specs=row,
                          out_shape=_sds((t, d), BF16), name=name, compiler_params=_params(1))(x, w, dep)


def _rms_bwd(name, x, w, dh, res):
    t, d = x.shape
    tm = min(256, t)

    def body(x_ref, w_ref, dh_ref, res_ref, dx_ref, dx16_ref, dw_ref):
        _, vjp = jax.vjp(_rms_f, x_ref[...], w_ref[...])
        dxv, dwv = vjp(dh_ref[...])
        dxv = dxv + res_ref[...]
        dx_ref[...] = dxv
        dx16_ref[...] = dxv.astype(BF16)

        @pl.when(pl.program_id(0) == 0)
        def _():
            dw_ref[...] = jnp.zeros_like(dw_ref)

        dw_ref[...] += dwv

    row = pl.BlockSpec((tm, d), lambda i: (i, 0))
    vec = pl.BlockSpec((1, d), lambda i: (0, 0))
    return pl.pallas_call(body, grid=(t // tm,), in_specs=[row, vec, row, row], out_specs=[row, row, vec],
                          out_shape=[_sds((t, d), F32), _sds((t, d), BF16), _sds((1, d), F32)], name=name,
                          compiler_params=_params(1))(x, w, dh, res)


def _conv_taps(xv, w_ref, rows):
    c = w_ref[3:4, :] * xv
    for s in (1, 2, 3):
        c = c + w_ref[3 - s:4 - s, :] * jnp.where(rows >= s, pltpu.roll(xv, s, 0), 0.0)
    return c


def _post_conv(c, l2, scale):
    y = _silu(c)
    if l2:
        y = y * lax.rsqrt(jnp.sum(y * y, axis=-1, keepdims=True) + EPS) * scale
    return y


def _conv_fwd(name, proj, conv_w8, group, l2, scale):
    t = proj.shape[0]

    def body(x_ref, w_ref, o_ref):
        rows = _iota2((t, HD), 0)
        o_ref[...] = _post_conv(_conv_taps(x_ref[...], w_ref, rows), l2, scale)

    return pl.pallas_call(
        body, grid=(N_HEADS,),
        in_specs=[pl.BlockSpec((t, HD), lambda h: (0, h + group * N_HEADS)),
                  pl.BlockSpec((8, HD), lambda h: (0, h + group * N_HEADS))],
        out_specs=pl.BlockSpec((t, HD), lambda h: (0, h)),
        out_shape=_sds((t, GW), F32), name=name, compiler_params=_params(1, VMEM_LIMIT))(proj, conv_w8)


def _conv_bwd(name, proj, conv_w8, dn, dproj, group, l2, scale):
    t = proj.shape[0]

    def body(x_ref, w_ref, dn_ref, dproj_ref, dx_ref, dw_ref):
        rows = _iota2((t, HD), 0)
        xv = x_ref[...]
        c = _conv_taps(xv, w_ref, rows)
        _, vjp = jax.vjp(lambda cc: _post_conv(cc, l2, scale), c)
        (dc,) = vjp(dn_ref[...])
        dx = w_ref[3:4, :] * dc
        dw = jnp.zeros((8, HD), F32)
        rid = _iota2((8, HD), 0)
        dw = dw + jnp.where(rid == 3, jnp.sum(dc * xv, axis=0, keepdims=True), 0.0)
        for s in (1, 2, 3):
            dx = dx + w_ref[3 - s:4 - s, :] * jnp.where(rows < t - s, pltpu.roll(dc, t - s, 0), 0.0)
            xs = jnp.where(rows >= s, pltpu.roll(xv, s, 0), 0.0)
            dw = dw + jnp.where(rid == 3 - s, jnp.sum(dc * xs, axis=0, keepdims=True), 0.0)
        dx_ref[...] = dx.astype(BF16)
        dw_ref[...] = dw

    return pl.pallas_call(
        body, grid=(N_HEADS,),
        in_specs=[pl.BlockSpec((t, HD), lambda h: (0, h + group * N_HEADS)),
                  pl.BlockSpec((8, HD), lambda h: (0, h + group * N_HEADS)),
                  pl.BlockSpec((t, HD), lambda h: (0, h)), ANY],
        out_specs=[pl.BlockSpec((t, HD), lambda h: (0, h + group * N_HEADS)), pl.BlockSpec((8, HD), lambda h: (0, h))],
        out_shape=[_sds(dproj.shape, BF16), _sds((8, GW), F32)], input_output_aliases={3: 0}, name=name,
        compiler_params=_params(1, VMEM_LIMIT))(proj, conv_w8, dn, dproj)


def _chunk_cumsum(g, rows):
    pos = rows % CHUNK
    s = 1
    while s < CHUNK:
        g = g + jnp.where(pos >= s, pltpu.roll(g, s, 0), 0.0)
        s *= 2
    return g


def _gates_fwd(name, proj, small_blk, alog_row, dtb_row):
    t = proj.shape[0]
    tm = min(256, t)

    def body(s_ref, a_ref, b_ref, beta_ref, gc_ref):
        sm = s_ref[...]
        beta = _sigmoid(sm)
        g = -jnp.exp(a_ref[...]) * _softplus(sm + b_ref[...])
        gc = _chunk_cumsum(g, _iota2((tm, HD), 0))
        lane = _iota2((tm, HD), 1)
        for h in range(N_HEADS):
            bcol = jnp.sum(jnp.where(lane == h, beta, 0.0), axis=1, keepdims=True)
            gcol = jnp.sum(jnp.where(lane == 8 + h, gc, 0.0), axis=1, keepdims=True)
            beta_ref[:, h * HD:(h + 1) * HD] = jnp.broadcast_to(bcol, (tm, HD))
            gc_ref[:, h * HD:(h + 1) * HD] = jnp.broadcast_to(gcol, (tm, HD))

    vec = pl.BlockSpec((1, HD), lambda i: (0, 0))
    wide = pl.BlockSpec((tm, GW), lambda i: (i, 0))
    return pl.pallas_call(
        body, grid=(t // tm,),
        in_specs=[pl.BlockSpec((tm, HD), lambda i: (i, small_blk)), vec, vec], out_specs=[wide, wide],
        out_shape=[_sds((t, GW), F32), _sds((t, GW), F32)], name=name,
        compiler_params=_params(1))(proj, alog_row, dtb_row)


def _gates_bwd(name, proj, small_blk, alog_row, dtb_row, dbeta_b, dg_b, dproj):
    t = proj.shape[0]
    tm = min(256, t)

    def body(s_ref, a_ref, b_ref, db_ref, dg_ref, dproj_ref, ds_ref, da_ref, dbias_ref):
        sm = s_ref[...]
        lane = _iota2((tm, HD), 1)
        db = jnp.zeros((tm, HD), F32)
        dg = jnp.zeros((tm, HD), F32)
        for h in range(N_HEADS):
            db = db + jnp.where(lane == h, db_ref[:, h * HD:(h + 1) * HD], 0.0)
            dg = dg + jnp.where(lane == 8 + h, dg_ref[:, h * HD:(h + 1) * HD], 0.0)
        beta = _sigmoid(sm)
        ea = jnp.exp(a_ref[...])
        pre = sm + b_ref[...]
        g = -ea * _softplus(pre)
        dpre = dg * (-ea) * _sigmoid(pre)
        ds_ref[...] = (db * beta * (1.0 - beta) + dpre).astype(BF16)

        @pl.when(pl.program_id(0) == 0)
        def _():
            da_ref[...] = jnp.zeros_like(da_ref)
            dbias_ref[...] = jnp.zeros_like(dbias_ref)

        da_ref[...] += jnp.sum(dg * g, axis=0, keepdims=True)
        dbias_ref[...] += jnp.sum(dpre, axis=0, keepdims=True)

    vec = pl.BlockSpec((1, HD), lambda i: (0, 0))
    wide = pl.BlockSpec((tm, GW), lambda i: (i, 0))
    return pl.pallas_call(
        body, grid=(t // tm,),
        in_specs=[pl.BlockSpec((tm, HD), lambda i: (i, small_blk)), vec, vec, wide, wide, ANY],
        out_specs=[pl.BlockSpec((tm, HD), lambda i: (i, small_blk)), vec, vec],
        out_shape=[_sds(dproj.shape, BF16), _sds((1, HD), F32), _sds((1, HD), F32)],
        input_output_aliases={5: 0}, name=name,
        compiler_params=_params(1))(proj, alog_row, dtb_row, dbeta_b, dg_b, dproj)


def _pair_masks():
    ii = _iota2((PAIR, PAIR), 0)
    jj = _iota2((PAIR, PAIR), 1)
    same = (ii // CHUNK) == (jj // CHUNK)
    return ii, jj, same & (ii >= jj), same & (ii > jj)


def _to_row(col_b, ii, jj):
    return jnp.sum(jnp.where(ii == jj, col_b, 0.0), axis=0, keepdims=True)


def _to_col(row, ii, jj):
    return jnp.sum(jnp.where(ii == jj, jnp.broadcast_to(row, (PAIR, PAIR)), 0.0), axis=1, keepdims=True)


def _decay_parts(gc, last_a, last_b, ii, jj, causal):
    diff = gc - _to_row(gc, ii, jj)
    dmat = jnp.where(causal, jnp.exp(jnp.where(causal, diff, 0.0)), 0.0)
    glast = jnp.where(ii < CHUNK, last_a, last_b)
    return dmat, jnp.exp(gc), jnp.exp(glast - gc)


def _unit_lower_inverse(lows, ii, jj):
    eye = jnp.where(ii == jj, 1.0, 0.0)
    mm = lambda xs, ys: [_dot3(a, b, 1, 0) for a, b in zip(xs, ys)]
    plus = lambda xs: [eye + a for a in xs]
    minus = lambda xs: [eye - a for a in xs]
    d1 = [jnp.where((ii // 16) == (jj // 16), low, 0.0) for low in lows]
    d2 = mm(d1, d1)
    a = mm(minus(d1), plus(d2))
    d4 = mm(d2, d2)
    a = mm(a, plus(d4))
    d8 = mm(d4, d4)
    td = mm(a, plus(d8))
    n1 = mm(td, [low - d for low, d in zip(lows, d1)])
    n2 = mm(n1, n1)
    return mm(mm(minus(n1), plus(n2)), td)


def _delta_prep(name, qn, kn, vv, beta_b, gc_b):
    t = qn.shape[0]

    def body(q_ref, k_ref, v_ref, b_ref, g_ref, u_ref, w_ref, p_ref, t_ref, qd_ref, kd_ref):
        ii, jj, causal, strict = _pair_masks()
        sls = [slice(hh * HD, (hh + 1) * HD) for hh in range(HEADS_PER_STEP)]
        lows = []
        for sl in sls:
            q, k, beta = q_ref[:, sl], k_ref[:, sl], b_ref[:, sl]
            dmat, gam, e2 = _decay_parts(g_ref[:, sl], g_ref[CHUNK - 1:CHUNK, sl], g_ref[PAIR - 1:PAIR, sl],
                                         ii, jj, causal)
            k16 = _b16(k)
            lows.append(jnp.where(strict, beta * _dot(k16, k16, 1, 1) * dmat, 0.0))
            p_ref[:, sl] = jnp.where(causal, _dot(_b16(q), k16, 1, 1) * dmat, 0.0).astype(BF16)
            qd_ref[:, sl] = (q * gam).astype(BF16)
            kd_ref[:, sl] = (k * e2).astype(BF16)
        for sl, tinv in zip(sls, _unit_lower_inverse(lows, ii, jj)):
            beta = b_ref[:, sl]
            t_ref[:, sl] = tinv
            u_ref[:, sl] = _dot3(tinv, v_ref[:, sl] * beta, 1, 0)
            w_ref[:, sl] = _dot3(tinv, k_ref[:, sl] * (beta * jnp.exp(g_ref[:, sl])), 1, 0).astype(BF16)

    blk = pl.BlockSpec((PAIR, HEADS_PER_STEP * HD), lambda i, h: (i, h))
    return pl.pallas_call(
        body, grid=(t // PAIR, N_HEADS // HEADS_PER_STEP), in_specs=[blk] * 5, out_specs=[blk] * 6,
        out_shape=[_sds((t, GW), F32), _sds((t, GW), BF16), _sds((t, GW), BF16), _sds((t, GW), F32),
                   _sds((t, GW), BF16), _sds((t, GW), BF16)],
        name=name, compiler_params=_params(2))(qn, kn, vv, beta_b, gc_b)


def _delta_scan(name, u, w, p, qd, kd, gc_b):
    t = u.shape[0]
    n = t // CHUNK

    def body(u_ref, w_ref, p_ref, qd_ref, kd_ref, g_ref, o_ref, vn_ref, sh_ref, state):
        @pl.when(pl.program_id(0) == 0)
        def _():
            state[...] = jnp.zeros_like(state)

        sls = [slice(h * HD, (h + 1) * HD) for h in range(N_HEADS)]
        heads = range(N_HEADS)
        s = [state[h] for h in heads]
        for h in heads:
            sh_ref[h] = s[h]
        s16 = [_b16(a) for a in s]
        ws = [_dot(w_ref[:, sls[h]], s16[h], 1, 0) for h in heads]
        qs = [_dot(qd_ref[:, sls[h]], s16[h], 1, 0) for h in heads]
        vn16 = [_b16(u_ref[:, sls[h]] - ws[h]) for h in heads]
        pv = [_dot(p_ref[:, sls[h]], jnp.concatenate([vn16[h], vn16[h]], axis=0), 1, 0) for h in heads]
        kv = [_dot(kd_ref[:, sls[h]], vn16[h], 0, 0) for h in heads]
        for h in heads:
            o_ref[:, sls[h]] = qs[h] + pv[h]
            vn_ref[:, sls[h]] = vn16[h]
            state[h] = s[h] * jnp.exp(g_ref[CHUNK - 1:CHUNK, sls[h]]) + kv[h]

    blk = pl.BlockSpec((CHUNK, GW), lambda i: (i, 0))
    return pl.pallas_call(
        body, grid=(n,), in_specs=[blk] * 6,
        out_specs=[blk, blk, pl.BlockSpec((None, N_HEADS, HD, HD), lambda i: (i, 0, 0, 0))],
        out_shape=[_sds((t, GW), F32), _sds((t, GW), BF16), _sds((n, N_HEADS, HD, HD), F32)],
        scratch_shapes=[pltpu.VMEM((N_HEADS, HD, HD), F32)], name=name,
        compiler_params=_params(1))(u, w, p, qd, kd, gc_b)


def _delta_scan_bwd(name, do, w, p, qd, kd, gc_b, vn, s_hist):
    t = do.shape[0]
    n = t // CHUNK

    def body(do_ref, w_ref, p_ref, qd_ref, kd_ref, g_ref, vn_ref, sh_ref,
             dvn_ref, dqd_ref, dkd_ref, dw_ref, ddec_ref, dstate):
        @pl.when(pl.program_id(0) == 0)
        def _():
            dstate[...] = jnp.zeros_like(dstate)

        sls = [slice(h * HD, (h + 1) * HD) for h in range(N_HEADS)]
        heads = range(N_HEADS)
        ds = [dstate[h] for h in heads]
        ds16 = [_b16(a) for a in ds]
        s16 = [_b16(sh_ref[h]) for h in heads]
        do16 = [_b16(do_ref[:, sls[h]]) for h in heads]
        ptdo = [_dot(p_ref[:, sls[h]], do16[h], 0, 0) for h in heads]
        kds = [_dot(kd_ref[:, sls[h]], ds16[h], 1, 0) for h in heads]
        qdo = [_dot(qd_ref[:, sls[h]], do16[h], 0, 0) for h in heads]
        for h in heads:
            dqd_ref[:, sls[h]] = _dot(do16[h], s16[h], 1, 1)
            dkd_ref[:, sls[h]] = _dot(vn_ref[:, sls[h]], ds16[h], 1, 1)
        dvn = [ptdo[h][:CHUNK, :] + ptdo[h][CHUNK:, :] + kds[h] for h in heads]
        dvn16 = [_b16(a) for a in dvn]
        wdv = [_dot(w_ref[:, sls[h]], dvn16[h], 0, 0) for h in heads]
        for h in heads:
            dvn_ref[:, sls[h]] = dvn[h]
            dw_ref[:, sls[h]] = -_dot(dvn16[h], s16[h], 1, 1)
            tot = jnp.sum(jnp.sum(sh_ref[h] * ds[h], axis=1, keepdims=True), axis=0, keepdims=True)
            ddec_ref[:, sls[h]] = jnp.broadcast_to(tot, (8, HD))
            dstate[h] = ds[h] * jnp.exp(g_ref[CHUNK - 1:CHUNK, sls[h]]) + qdo[h] - wdv[h]

    blk = pl.BlockSpec((CHUNK, GW), lambda i: (n - 1 - i, 0))
    return pl.pallas_call(
        body, grid=(n,),
        in_specs=[blk] * 7 + [pl.BlockSpec((None, N_HEADS, HD, HD), lambda i: (n - 1 - i, 0, 0, 0))],
        out_specs=[blk] * 4 + [pl.BlockSpec((8, GW), lambda i: (n - 1 - i, 0))],
        out_shape=[_sds((t, GW), F32)] * 4 + [_sds((n * 8, GW), F32)],
        scratch_shapes=[pltpu.VMEM((N_HEADS, HD, HD), F32)], name=name,
        compiler_params=_params(1))(do, w, p, qd, kd, gc_b, vn, s_hist)


def _delta_prep_bwd(name, qn, kn, vv, beta_b, gc_b, tinv, u, w, vn, do, dvn, dqd, dkd, dw, ddec):
    t = qn.shape[0]

    def body(q_ref, k_ref, v_ref, b_ref, g_ref, t_ref, u_ref, w_ref, vn_ref, do_ref, dvn_ref, dqd_ref,
             dkd_ref, dw_ref, ddec_ref, dq_ref, dk_ref, dv_ref, dbeta_ref, dg_ref):
        ii, jj, causal, strict = _pair_masks()
        suffix = ((ii // CHUNK) == (jj // CHUNK)) & (jj >= ii)
        first = ii < CHUNK
        rs = lambda a: jnp.sum(a, axis=1, keepdims=True)
        sls = [slice(hh * HD, (hh + 1) * HD) for hh in range(HEADS_PER_STEP)]
        xs = [_dot3(t_ref[:, sl], dvn_ref[:, sl], 0, 0) for sl in sls]
        ys = [_dot3(t_ref[:, sl], dw_ref[:, sl], 0, 0) for sl in sls]
        k16s = [_b16(k_ref[:, sl]) for sl in sls]
        kks = [_dot(k16, k16, 1, 1) for k16 in k16s]
        qks = [_dot(_b16(q_ref[:, sl]), k16, 1, 1) for sl, k16 in zip(sls, k16s)]
        dps = [jnp.where(causal, _dot(_b16(do_ref[:, sl]), vn_ref[:, sl], 1, 1), 0.0) for sl in sls]
        das = [-jnp.where(strict, _dot(_b16(x), _b16(u_ref[:, sl]), 1, 1) + _dot(_b16(y), w_ref[:, sl], 1, 1), 0.0)
               for sl, x, y in zip(sls, xs, ys)]
        for hh, sl in enumerate(sls):
            q, k, v, beta, gc = q_ref[:, sl], k_ref[:, sl], v_ref[:, sl], b_ref[:, sl], g_ref[:, sl]
            last_a, last_b = g_ref[CHUNK - 1:CHUNK, sl], g_ref[PAIR - 1:PAIR, sl]
            dmat, gam, e2 = _decay_parts(gc, last_a, last_b, ii, jj, causal)
            q16, k16 = _b16(q), k16s[hh]
            kk, qk, dp, x, y, da = kks[hh], qks[hh], dps[hh], xs[hh], ys[hh], das[hh]
            dqd, dkd = dqd_ref[:, sl], dkd_ref[:, sl]
            dpd16 = _b16(dp * dmat)
            dkk16 = _b16(da * beta * dmat)
            dq_ref[:, sl] = gam * dqd + _dot(dpd16, k16, 1, 0)
            dk_ref[:, sl] = (e2 * dkd + _dot(dpd16, q16, 0, 0) + beta * gam * y
                             + _dot(dkk16, k16, 1, 0) + _dot(dkk16, k16, 0, 0))
            dv_ref[:, sl] = beta * x
            dbeta = rs(v * x) + rs(k * gam * y) + rs(da * kk * dmat)
            dbeta_ref[:, sl] = jnp.broadcast_to(dbeta, (PAIR, HD))
            m = (dp * qk + da * beta * kk) * dmat
            dgam = rs(q * dqd) + rs(k * beta * y)
            de2 = rs(k * dkd)
            colsum = _to_col(jnp.sum(m, axis=0, keepdims=True), ii, jj)
            te2 = de2 * e2
            dgc = rs(m) - colsum + gam * dgam - te2
            tail_a = jnp.sum(jnp.where(first, te2, 0.0), axis=0, keepdims=True)
            tail_b = jnp.sum(jnp.where(first, 0.0, te2), axis=0, keepdims=True)
            dgc = dgc + jnp.where(ii == CHUNK - 1, tail_a + ddec_ref[0:1, sl] * jnp.exp(last_a), 0.0)
            dgc = dgc + jnp.where(ii == PAIR - 1, tail_b + ddec_ref[8:9, sl] * jnp.exp(last_b), 0.0)
            dgc_row = _to_row(dgc, ii, jj)
            dg = jnp.sum(jnp.where(suffix, jnp.broadcast_to(dgc_row, (PAIR, PAIR)), 0.0), axis=1, keepdims=True)
            dg_ref[:, sl] = jnp.broadcast_to(dg, (PAIR, HD))

    blk = pl.BlockSpec((PAIR, HEADS_PER_STEP * HD), lambda i, h: (i, h))
    return pl.pallas_call(
        body, grid=(t // PAIR, N_HEADS // HEADS_PER_STEP),
        in_specs=[blk] * 14 + [pl.BlockSpec((16, HEADS_PER_STEP * HD), lambda i, h: (i, h))], out_specs=[blk] * 5,
        out_shape=[_sds((t, GW), F32)] * 5, name=name,
        compiler_params=_params(2))(qn, kn, vv, beta_b, gc_b, tinv, u, w, vn, do, dvn, dqd, dkd, dw, ddec)


def _rope_tables(pos_col, inv_row):
    ang = pos_col.astype(F32) * inv_row
    lane = _iota2(ang.shape, 1)
    return jnp.cos(ang), jnp.where(lane < HD // 2, -1.0, 1.0) * jnp.sin(ang)


def _head_rms(xh, wv):
    return xh * lax.rsqrt(jnp.mean(xh * xh, axis=-1, keepdims=True) + EPS) * wv


def _qk_fwd(name, proj, blk_idx, w_row, pos_col, inv_row):
    t = proj.shape[0]
    tm = min(256, t)

    def body(x_ref, w_ref, pos_ref, inv_ref, o_ref):
        cos, sin = _rope_tables(pos_ref[...], inv_ref[...])
        for h in range(N_HEADS):
            y = _head_rms(x_ref[:, h * HD:(h + 1) * HD], w_ref[...])
            o_ref[:, h * HD:(h + 1) * HD] = y * cos + pltpu.roll(y, HD // 2, 1) * sin

    vec = pl.BlockSpec((1, HD), lambda i: (0, 0))
    return pl.pallas_call(
        body, grid=(t // tm,),
        in_specs=[pl.BlockSpec((tm, GW), lambda i: (i, blk_idx)), vec, pl.BlockSpec((tm, 1), lambda i: (i, 0)), vec],
        out_specs=pl.BlockSpec((tm, GW), lambda i: (i, 0)), out_shape=_sds((t, GW), F32), name=name,
        compiler_params=_params(1))(proj, w_row, pos_col, inv_row)


def _qk_bwd(name, proj, blk_idx, w_row, pos_col, inv_row, dy_full, dproj):
    t = proj.shape[0]
    tm = min(256, t)

    def body(x_ref, w_ref, pos_ref, inv_ref, dy_ref, dproj_ref, dx_ref, dw_ref):
        cos, sin = _rope_tables(pos_ref[...], inv_ref[...])
        dw = jnp.zeros((1, HD), F32)
        for h in range(N_HEADS):
            sl = slice(h * HD, (h + 1) * HD)
            dy = dy_ref[:, sl]
            dy = dy * cos - pltpu.roll(dy, HD // 2, 1) * sin
            _, vjp = jax.vjp(_head_rms, x_ref[:, sl], w_ref[...])
            dx, dwh = vjp(dy)
            dw = dw + dwh
            dx_ref[:, sl] = dx.astype(BF16)

        @pl.when(pl.program_id(0) == 0)
        def _():
            dw_ref[...] = jnp.zeros_like(dw_ref)

        dw_ref[...] += dw

    vec = pl.BlockSpec((1, HD), lambda i: (0, 0))
    wide = pl.BlockSpec((tm, GW), lambda i: (i, 0))
    return pl.pallas_call(
        body, grid=(t // tm,),
        in_specs=[pl.BlockSpec((tm, GW), lambda i: (i, blk_idx)), vec, pl.BlockSpec((tm, 1), lambda i: (i, 0)), vec,
                  wide, ANY],
        out_specs=[pl.BlockSpec((tm, GW), lambda i: (i, blk_idx)), vec],
        out_shape=[_sds(dproj.shape, BF16), _sds((1, HD), F32)], input_output_aliases={5: 0}, name=name,
        compiler_params=_params(1))(proj, w_row, pos_col, inv_row, dy_full, dproj)


def _cast_into(name, x, dproj, blk_idx):
    t = x.shape[0]
    tm = min(512, t)

    def body(x_ref, dproj_ref, o_ref):
        o_ref[...] = x_ref[...].astype(BF16)

    return pl.pallas_call(
        body, grid=(t // tm,), in_specs=[pl.BlockSpec((tm, GW), lambda i: (i, 0)), ANY],
        out_specs=pl.BlockSpec((tm, GW), lambda i: (i, blk_idx)), out_shape=_sds(dproj.shape, BF16),
        input_output_aliases={1: 0}, name=name, compiler_params=_params(1))(x, dproj)


GROUP = SPAN * max(DILATIONS)
SCALE = HD ** -0.5
TILE_BATCH = 8


def _band_mask(lo):
    qi = _iota2((SPAN, 2 * SPAN), 0)
    ki = _iota2((SPAN, 2 * SPAN), 1)
    return (ki >= qi) & (ki <= qi + SPAN) & (ki >= lo)


def _tiles():
    return [(pi, r, u, rho) for pi, r in enumerate(DILATIONS) for rho in range(r) for u in range(GROUP // (SPAN * r))]


def _rows(r, u, rho):
    return pl.ds(u * SPAN * r + rho, SPAN, stride=r) if r > 1 else pl.ds(u * SPAN, SPAN)


def _attn_fwd(name, q, k, v, v_blk):
    t = q.shape[0]

    def body(qc_ref, kc_ref, vc_ref, kp_ref, vp_ref, ob_ref, lse_ref, o_scr, l_scr):
        mask_in = _band_mask(0)
        mask_edge = _band_mask(jnp.where(pl.program_id(0) == 0, SPAN, 0))
        tiles = _tiles()
        k_own = v_own = None
        for b0 in range(0, len(tiles), TILE_BATCH):
            work = []
            for pi, r, u, rho in tiles[b0:b0 + TILE_BATCH]:
                rows = _rows(r, u, rho)
                if u > 0:
                    k_prev, v_prev, mask = k_own, v_own, mask_in
                else:
                    prows = _rows(r, GROUP // (SPAN * r) - 1, rho)
                    k_prev, v_prev, mask = kp_ref[prows, :].astype(BF16), vp_ref[prows, :].astype(BF16), mask_edge
                k_own, v_own = kc_ref[rows, :].astype(BF16), vc_ref[rows, :].astype(BF16)
                work.append((pi, rows, mask, qc_ref[rows, :].astype(BF16), jnp.concatenate([k_prev, k_own], axis=0),
                             jnp.concatenate([v_prev, v_own], axis=0)))
            scores = [_dot(qt, kcat, 1, 1) for _, _, _, qt, kcat, _ in work]
            soft = []
            for (_, _, mask, _, _, _), s in zip(work, scores):
                s = jnp.where(mask, s * SCALE, NEG)
                m = jnp.max(s, axis=1, keepdims=True)
                p = jnp.exp(s - m)
                soft.append((m, _b16(p), jnp.sum(p, axis=1, keepdims=True)))
            outs = [_dot(p, vcat, 1, 0) for (_, p, _), (_, _, _, _, _, vcat) in zip(soft, work)]
            for (pi, rows, _, _, _, _), (m, _, den), o in zip(work, soft, outs):
                o_scr[pi, rows, :] = o / den
                l_scr[pi, rows, :] = jnp.broadcast_to(m + jnp.log(den), (SPAN, HD))
        step = 256
        for c in range(GROUP // step):
            sl = pl.ds(c * step, step)
            ob, lse = _merge([o_scr[i, sl, :] for i in range(3)], [l_scr[i, sl, :] for i in range(3)])
            ob_ref[sl, :] = ob
            lse_ref[sl, :] = lse

    cur = pl.BlockSpec((GROUP, HD), lambda g, h: (g, h))
    prev = pl.BlockSpec((GROUP, HD), lambda g, h: (jnp.maximum(g - 1, 0), h))
    vcur = pl.BlockSpec((GROUP, HD), lambda g, h: (g, v_blk * N_HEADS + h))
    vprev = pl.BlockSpec((GROUP, HD), lambda g, h: (jnp.maximum(g - 1, 0), v_blk * N_HEADS + h))
    return pl.pallas_call(
        body, grid=(t // GROUP, N_HEADS), in_specs=[cur, cur, vcur, prev, vprev], out_specs=[cur, cur],
        out_shape=[_sds((t, GW), F32), _sds((t, GW), F32)],
        scratch_shapes=[pltpu.VMEM((3, GROUP, HD), F32), pltpu.VMEM((3, GROUP, HD), F32)], name=name,
        compiler_params=_params(2))(q, k, v, k, v)


def _attn_bwd(name, q, k, v, v_blk, do, lse, delta):
    t = q.shape[0]
    ng = t // GROUP

    def probs(work):
        scores = [_dot(qt, kcat, 1, 1) for qt, _, _, _, kcat, _, _ in work]
        dps = [_dot(dot, vcat, 1, 1) for _, dot, _, _, _, vcat, _ in work]
        out = []
        for (_, _, lt, dlt, kcat, _, mask), s, dp in zip(work, scores, dps):
            wide = kcat.shape[0] // SPAN
            lw = jnp.concatenate([lt] * wide, axis=1) if wide > 1 else lt
            dw = jnp.concatenate([dlt] * wide, axis=1) if wide > 1 else dlt
            p = jnp.exp(jnp.where(mask, s * SCALE - lw, NEG))
            out.append((_b16(p * (dp - dw) * SCALE), _b16(p)))
        return out

    def body(qc_ref, kc_ref, vc_ref, doc_ref, lc_ref, dc_ref, kp_ref, vp_ref, qn_ref, don_ref, ln_ref, dn_ref,
             dq_ref, dk_ref, dv_ref):
        g = pl.program_id(0)
        mask_in = _band_mask(0)
        mask_edge = _band_mask(jnp.where(g == 0, SPAN, 0))
        dk_ref[...] = jnp.zeros_like(dk_ref)
        dv_ref[...] = jnp.zeros_like(dv_ref)
        tiles = _tiles()
        k_own = v_own = None
        for b0 in range(0, len(tiles), TILE_BATCH):
            where, work = [], []
            for pi, r, u, rho in tiles[b0:b0 + TILE_BATCH]:
                rows = _rows(r, u, rho)
                if u > 0:
                    prows, k_prev, v_prev, mask = _rows(r, u - 1, rho), k_own, v_own, mask_in
                else:
                    prows = _rows(r, GROUP // (SPAN * r) - 1, rho)
                    k_prev, v_prev, mask = kp_ref[prows, :].astype(BF16), vp_ref[prows, :].astype(BF16), mask_edge
                k_own, v_own = kc_ref[rows, :].astype(BF16), vc_ref[rows, :].astype(BF16)
                where.append((pi, u, rows, prows))
                work.append((qc_ref[rows, :].astype(BF16), doc_ref[rows, :].astype(BF16), lc_ref[rows, :], dc_ref[rows, :],
                             jnp.concatenate([k_prev, k_own], axis=0), jnp.concatenate([v_prev, v_own], axis=0), mask))
            dsp = probs(work)
            dqs = [_dot(ds, w[4], 1, 0) for (ds, _), w in zip(dsp, work)]
            dks = [_dot(ds, w[0], 0, 0) for (ds, _), w in zip(dsp, work)]
            dvs = [_dot(p, w[1], 0, 0) for (_, p), w in zip(dsp, work)]
            for (pi, u, rows, prows), dq_t, dk2, dv2 in zip(where, dqs, dks, dvs):
                if pi == 0:
                    dq_ref[rows, :] = dq_t
                else:
                    dq_ref[rows, :] += dq_t
                dk_ref[rows, :] += dk2[SPAN:, :]
                dv_ref[rows, :] += dv2[SPAN:, :]
                if u > 0:
                    dk_ref[prows, :] += dk2[:SPAN, :]
                    dv_ref[prows, :] += dv2[:SPAN, :]
        qi = _iota2((SPAN, SPAN), 0)
        ki = _iota2((SPAN, SPAN), 1)
        mask_next = (ki >= qi) & (ki < jnp.where(g == ng - 1, 0, SPAN))
        edge = [(r, rho) for r in DILATIONS for rho in range(r)]
        for b0 in range(0, len(edge), TILE_BATCH):
            where, work = [], []
            for r, rho in edge[b0:b0 + TILE_BATCH]:
                krows, qrows = _rows(r, GROUP // (SPAN * r) - 1, rho), _rows(r, 0, rho)
                where.append(krows)
                work.append((qn_ref[qrows, :].astype(BF16), don_ref[qrows, :].astype(BF16), ln_ref[qrows, :],
                             dn_ref[qrows, :], kc_ref[krows, :].astype(BF16), vc_ref[krows, :].astype(BF16), mask_next))
            dsp = probs(work)
            dks = [_dot(ds, w[0], 0, 0) for (ds, _), w in zip(dsp, work)]
            dvs = [_dot(p, w[1], 0, 0) for (_, p), w in zip(dsp, work)]
            for krows, dk1, dv1 in zip(where, dks, dvs):
                dk_ref[krows, :] += dk1
                dv_ref[krows, :] += dv1

    cur = pl.BlockSpec((GROUP, HD), lambda g, h: (g, h))
    prev = pl.BlockSpec((GROUP, HD), lambda g, h: (jnp.maximum(g - 1, 0), h))
    nxt = pl.BlockSpec((GROUP, HD), lambda g, h: (jnp.minimum(g + 1, ng - 1), h))
    vcur = pl.BlockSpec((GROUP, HD), lambda g, h: (g, v_blk * N_HEADS + h))
    vprev = pl.BlockSpec((GROUP, HD), lambda g, h: (jnp.maximum(g - 1, 0), v_blk * N_HEADS + h))
    return pl.pallas_call(
        body, grid=(ng, N_HEADS), in_specs=[cur, cur, vcur, cur, cur, cur, prev, vprev] + [nxt] * 4,
        out_specs=[cur] * 3,
        out_shape=[_sds((t, GW), F32)] * 3, name=name,
        compiler_params=_params(2))(q, k, v, do, lse, delta, k, v, q, do, lse, delta)


def _merge(os_, ls_):
    m = jnp.maximum(jnp.maximum(ls_[0], ls_[1]), ls_[2])
    ws = [jnp.exp(l - m) for l in ls_]
    tot = ws[0] + ws[1] + ws[2]
    ob = (ws[0] * os_[0] + ws[1] * os_[1] + ws[2] * os_[2]) / tot
    return ob, m + jnp.log(tot)


def _gated_norm(oa, z, wv):
    return _head_rms(oa, wv) * _silu(z)


def _mix_fwd(name, oa_raw, proj, z_blk, ob, w_dn, w_an):
    t = oa_raw.shape[0]
    tm = min(256, t)

    def body(oa_ref, z_ref, ob_ref, wd_ref, wa_ref, mix_ref):
        for h in range(N_HEADS):
            sl = slice(h * HD, (h + 1) * HD)
            mix_ref[:, sl] = _gated_norm(oa_ref[:, sl], z_ref[:, sl], wd_ref[...]).astype(BF16)
            mix_ref[:, GW + h * HD:GW + (h + 1) * HD] = _head_rms(ob_ref[:, sl], wa_ref[...]).astype(BF16)

    vec = pl.BlockSpec((1, HD), lambda i: (0, 0))
    wide = pl.BlockSpec((tm, GW), lambda i: (i, 0))
    return pl.pallas_call(
        body, grid=(t // tm,),
        in_specs=[wide, pl.BlockSpec((tm, GW), lambda i: (i, z_blk)), wide, vec, vec],
        out_specs=pl.BlockSpec((tm, 2 * GW), lambda i: (i, 0)),
        out_shape=_sds((t, 2 * GW), BF16), name=name,
        compiler_params=_params(1))(oa_raw, proj, ob, w_dn, w_an)


def _mix_bwd(name, dmixed, oa_raw, proj, z_blk, ob, w_dn, w_an, dep):
    t = oa_raw.shape[0]
    tm = min(256, t)

    def body(dm_ref, oa_ref, z_ref, ob_ref, wd_ref, wa_ref, dep_ref,
             doa_ref, dz_ref, dob_ref, dl_ref, dwd_ref, dwa_ref):
        dwd = jnp.zeros((1, HD), F32)
        dwa = jnp.zeros((1, HD), F32)
        for h in range(N_HEADS):
            sl = slice(h * HD, (h + 1) * HD)
            _, vjp = jax.vjp(_gated_norm, oa_ref[:, sl], z_ref[:, sl], wd_ref[...])
            doa, dz, dw1 = vjp(dm_ref[:, sl])
            doa_ref[:, sl] = doa
            dz_ref[:, sl] = dz.astype(BF16)
            dwd = dwd + dw1
            obh = ob_ref[:, sl]
            _, vjp2 = jax.vjp(_head_rms, obh, wa_ref[...])
            dob, dw2 = vjp2(dm_ref[:, GW + h * HD:GW + (h + 1) * HD])
            dwa = dwa + dw2
            dob_ref[:, sl] = dob
            dl_ref[:, sl] = jnp.broadcast_to(jnp.sum(dob * obh, axis=1, keepdims=True), (tm, HD))

        @pl.when(pl.program_id(0) == 0)
        def _():
            dwd_ref[...] = jnp.zeros_like(dwd_ref)
            dwa_ref[...] = jnp.zeros_like(dwa_ref)

        dwd_ref[...] += dwd
        dwa_ref[...] += dwa

    vec = pl.BlockSpec((1, HD), lambda i: (0, 0))
    wide = pl.BlockSpec((tm, GW), lambda i: (i, 0))
    return pl.pallas_call(
        body, grid=(t // tm,),
        in_specs=[pl.BlockSpec((tm, 2 * GW), lambda i: (i, 0)), wide, pl.BlockSpec((tm, GW), lambda i: (i, z_blk)),
                  wide, vec, vec, ANY],
        out_specs=[wide, pl.BlockSpec((tm, GW), lambda i: (i, z_blk)), wide, wide, vec, vec],
        out_shape=[_sds((t, GW), F32), _sds(proj.shape, BF16), _sds((t, GW), F32), _sds((t, GW), F32),
                   _sds((1, HD), F32), _sds((1, HD), F32)], name=name,
        compiler_params=_params(1))(dmixed, oa_raw, proj, ob, w_dn, w_an, dep)


def _gate_up_swiglu(name, h2, w_gu_g):
    t, d = h2.shape
    n = w_gu_g.shape[2]
    per = N_DEV // 2
    tm = min(512, t)

    def body(a_ref, bg_ref, bu_ref, gu_ref, act_ref):
        a = a_ref[...]
        g = _dot(a, bg_ref[...], 1, 0)
        up = _dot(a, bu_ref[...], 1, 0)
        gu_ref[0] = g.astype(BF16)
        gu_ref[1] = up.astype(BF16)
        act_ref[...] = (_silu(g) * up).astype(BF16)

    return pl.pallas_call(
        body, grid=(per, t // tm),
        in_specs=[pl.BlockSpec((tm, d), lambda j, i: (i, 0)), pl.BlockSpec((None, d, n), lambda j, i: (j, 0, 0)),
                  pl.BlockSpec((None, d, n), lambda j, i: (j + per, 0, 0))],
        out_specs=[pl.BlockSpec((2, tm, n), lambda j, i: (0, i, j)), pl.BlockSpec((tm, n), lambda j, i: (i, j))],
        out_shape=[_sds((2, t, per * n), BF16), _sds((t, per * n), BF16)], name=name,
        compiler_params=_params(2))(h2, w_gu_g, w_gu_g)


def _d_gate_up(name, dy16, w_down, gu3, dep):
    t, d = dy16.shape
    f = w_down.shape[0]
    tm, tn = min(1024, t), f // 4

    def body(a_ref, b_ref, g_ref, dep_ref, o_ref):
        dact = _dot(a_ref[...], b_ref[...], 1, 1)
        g, up = g_ref[0].astype(F32), g_ref[1].astype(F32)
        sg = _sigmoid(g)
        o_ref[0] = (dact * up * sg * (1.0 + g * (1.0 - sg))).astype(BF16)
        o_ref[1] = (dact * g * sg).astype(BF16)

    return pl.pallas_call(
        body, grid=(f // tn, t // tm),
        in_specs=[pl.BlockSpec((tm, d), lambda j, i: (i, 0)), pl.BlockSpec((tn, d), lambda j, i: (j, 0)),
                  pl.BlockSpec((2, tm, tn), lambda j, i: (0, i, j)), ANY],
        out_specs=pl.BlockSpec((2, tm, tn), lambda j, i: (0, i, j)), out_shape=_sds((2, t, f), BF16), name=name,
        compiler_params=_params(2))(dy16, w_down, gu3, dep)


def _d_h2(name, dgu3, w_gu_g, dep):
    _, t, f = dgu3.shape
    n_dev, d, n = w_gu_g.shape
    per = n_dev // 2
    tm, tn = min(256, t), 512

    def body(g_ref, u_ref, b_ref, dep_ref, o_ref):
        acc = None
        for s in range(n_dev):
            a_ref = g_ref if s < per else u_ref
            part = _dot(a_ref[:, (s % per) * n:(s % per + 1) * n], b_ref[s], 1, 1)
            acc = part if acc is None else acc + part
        o_ref[...] = acc

    return pl.pallas_call(
        body, grid=(d // tn, t // tm),
        in_specs=[pl.BlockSpec((None, tm, f), lambda j, i: (0, i, 0)), pl.BlockSpec((None, tm, f), lambda j, i: (1, i, 0)),
                  pl.BlockSpec((n_dev, tn, n), lambda j, i: (0, j, 0)), ANY],
        out_specs=pl.BlockSpec((tm, tn), lambda j, i: (i, j)), out_shape=_sds((t, d), F32), name=name,
        compiler_params=_params(2))(dgu3, dgu3, w_gu_g, dep)


def _out_proj_norm(name, mixed, w_out, x, w_norm):
    t, d = x.shape
    kdim = mixed.shape[1]
    tm = min(512, t)

    def body(a_ref, b_ref, x_ref, w_ref, x1_ref, h_ref):
        x1 = x_ref[...] + _dot(a_ref[...], b_ref[...], 1, 0)
        x1_ref[...] = x1
        h_ref[...] = _rms_f(x1, w_ref[...]).astype(BF16)

    row = pl.BlockSpec((tm, d), lambda i: (i, 0))
    return pl.pallas_call(
        body, grid=(t // tm,),
        in_specs=[pl.BlockSpec((tm, kdim), lambda i: (i, 0)), pl.BlockSpec((kdim, d), lambda i: (0, 0)), row,
                  pl.BlockSpec((1, d), lambda i: (0, 0))],
        out_specs=[row, row], out_shape=[_sds((t, d), F32), _sds((t, d), BF16)], name=name,
        compiler_params=_params(1))(mixed, w_out, x, w_norm)


def _down_loss(name, act, w_down, x1, target):
    t, f = act.shape
    d = x1.shape[1]
    tm, tn = min(1024, t), 512

    def body(a_ref, b_ref, x_ref, t_ref, dy_ref, dy16_ref, l_ref):
        diff = _dot(a_ref[...], b_ref[...], 1, 0) + x_ref[...] - t_ref[...]
        dyv = diff * (1.0 / d)
        dy_ref[...] = dyv
        dy16_ref[...] = dyv.astype(BF16)
        tot = jnp.sum(jnp.sum(diff * diff, axis=1, keepdims=True), axis=0, keepdims=True) * (0.5 / d)

        @pl.when((pl.program_id(0) == 0) & (pl.program_id(1) == 0))
        def _():
            l_ref[...] = jnp.zeros_like(l_ref)

        l_ref[...] += jnp.broadcast_to(tot, (8, 128))

    tile = pl.BlockSpec((tm, tn), lambda i, j: (i, j))
    return pl.pallas_call(
        body, grid=(t // tm, d // tn),
        in_specs=[pl.BlockSpec((tm, f), lambda i, j: (i, 0)), pl.BlockSpec((f, tn), lambda i, j: (0, j)), tile, tile],
        out_specs=[tile, tile, pl.BlockSpec((8, 128), lambda i, j: (0, 0))],
        out_shape=[_sds((t, d), F32), _sds((t, d), BF16), _sds((8, 128), F32)], name=name,
        compiler_params=_params(2))(act, w_down, x1, target)


def _peer(me, k):
    pid = (me + k) % N_DEV
    return (pid // 4, (pid // 2) % 2, pid % 2)


def _my_id():
    return 4 * lax.axis_index("x") + 2 * lax.axis_index("y") + lax.axis_index("c")


def _exchange(name, arrays, scatter, dep):
    n = len(arrays)

    def body(*refs):
        ins, outs = refs[:n], refs[n + 1:2 * n + 1]
        send_sems, recv_sems, local_sems = refs[2 * n + 1:]
        me = _my_id()
        started = []
        for a in range(n):
            src = ins[a].at[me] if scatter[a] else ins[a]
            loc = pltpu.make_async_copy(src, outs[a].at[me], local_sems.at[a])
            loc.start()
            started.append(loc)
        remote = []
        for k in range(1, N_DEV):
            to = (me + k) % N_DEV
            for a in range(n):
                src = ins[a].at[to] if scatter[a] else ins[a]
                cp = pltpu.make_async_remote_copy(src_ref=src, dst_ref=outs[a].at[me],
                                                  send_sem=send_sems.at[a * (N_DEV - 1) + k - 1], recv_sem=recv_sems.at[a * (N_DEV - 1) + k - 1],
                                                  device_id=_peer(me, k), device_id_type=pl.DeviceIdType.MESH)
                cp.start()
                remote.append(cp)
        for k in range(1, N_DEV):
            frm = (me + N_DEV - k) % N_DEV
            for a in range(n):
                src = ins[a].at[frm] if scatter[a] else ins[a]
                pltpu.make_async_remote_copy(src_ref=src, dst_ref=outs[a].at[frm],
                                             send_sem=send_sems.at[a * (N_DEV - 1) + k - 1], recv_sem=recv_sems.at[a * (N_DEV - 1) + k - 1],
                                             device_id=_peer(me, k), device_id_type=pl.DeviceIdType.MESH).wait_recv()
        for cp in remote:
            cp.wait_send()
        for loc in started:
            loc.wait()

    out_shape = [_sds((N_DEV,) + (a.shape[1:] if sc else a.shape), a.dtype) for a, sc in zip(arrays, scatter)]
    return pl.pallas_call(
        body, in_specs=[ANY] * (n + 1), out_specs=[ANY] * n, out_shape=out_shape,
        scratch_shapes=[pltpu.SemaphoreType.DMA((n * (N_DEV - 1),)), pltpu.SemaphoreType.DMA((n * (N_DEV - 1),)),
                        pltpu.SemaphoreType.DMA((n,))],
        name=name)(*arrays, dep)


def _gather_two_level(name, arrays):
    n = len(arrays)
    per = N_DEV - 1

    def body(*refs):
        ins, outs = refs[:n], refs[n:2 * n]
        send_sems, recv_sems, local_sems = refs[2 * n:]
        x, y, c = lax.axis_index("x"), lax.axis_index("y"), lax.axis_index("c")
        me, sibling = (x, y, c), (x, y, 1 - c)
        chips = [(1 - x, y), (x, 1 - y), (1 - x, 1 - y)]

        def copy(a, k, block, to, src=None):
            slot = outs[a].at[4 * block[0] + 2 * block[1] + block[2]]
            return pltpu.make_async_remote_copy(
                src_ref=slot if src is None else src, dst_ref=slot, send_sem=send_sems.at[a * per + k],
                recv_sem=recv_sems.at[a * per + k], device_id=to, device_id_type=pl.DeviceIdType.MESH)

        mine = [pltpu.make_async_copy(ins[a], outs[a].at[4 * x + 2 * y + c], local_sems.at[a]) for a in range(n)]
        for cp in mine:
            cp.start()
        first = [copy(a, 0, me, sibling, src=ins[a]) for a in range(n)]
        first += [copy(a, 1 + j, me, (*chip, c), src=ins[a]) for j, chip in enumerate(chips) for a in range(n)]
        for cp in first:
            cp.start()
        passed = []
        for j, chip in enumerate(chips):
            for a in range(n):
                copy(a, 1 + j, (*chip, c), me).wait_recv()
                cp = copy(a, 4 + j, (*chip, c), sibling)
                cp.start()
                passed.append(cp)
        for a in range(n):
            copy(a, 0, sibling, me).wait_recv()
            for j, chip in enumerate(chips):
                copy(a, 4 + j, (*chip, 1 - c), me).wait_recv()
        for cp in first + passed:
            cp.wait_send()
        for cp in mine:
            cp.wait()

    return pl.pallas_call(
        body, in_specs=[ANY] * n, out_specs=[ANY] * n,
        out_shape=[_sds((N_DEV,) + a.shape, a.dtype) for a in arrays],
        scratch_shapes=[pltpu.SemaphoreType.DMA((n * per,)), pltpu.SemaphoreType.DMA((n * per,)),
                        pltpu.SemaphoreType.DMA((n,))],
        name=name)(*arrays)


HBM = pl.BlockSpec(memory_space=pltpu.HBM)
SEM = pl.BlockSpec(memory_space=pltpu.SEMAPHORE)
EFFECT = pltpu.SideEffectType.DATAFLOW_SIDE_EFFECTING


def _remote_copies(srcs, lands, scatter, send_sems, recv_sems, me, incoming):
    out = []
    for k in range(1, N_DEV):
        other = (me + N_DEV - k) % N_DEV if incoming else (me + k) % N_DEV
        for a in range(len(srcs)):
            sem = a * (N_DEV - 1) + k - 1
            src = srcs[a].at[other] if scatter[a] else srcs[a]
            dst = lands[a].at[other if incoming else me]
            out.append(pltpu.make_async_remote_copy(src_ref=src, dst_ref=dst, send_sem=send_sems.at[sem],
                                                    recv_sem=recv_sems.at[sem], device_id=_peer(me, k),
                                                    device_id_type=pl.DeviceIdType.MESH))
    return out


def _exchange_start(name, arrays, scatter, dep):
    n = len(arrays)
    lands = [lax.empty((N_DEV,) + (a.shape[1:] if sc else a.shape), a.dtype) for a, sc in zip(arrays, scatter)]

    def body(*refs):
        srcs, land_refs = refs[:n], refs[n:2 * n]
        send_sems, recv_sems = refs[2 * n + 1], refs[2 * n + 2]
        token = refs[-1]
        for cp in _remote_copies(srcs, land_refs, scatter, send_sems, recv_sems, _my_id(), False):
            cp.start()
        token[...] = jnp.zeros_like(token)

    n_sem = n * (N_DEV - 1)
    out_shape = ([pltpu.SemaphoreType.DMA((n_sem,)), pltpu.SemaphoreType.DMA((n_sem,))]
                 + [pltpu.HBM(a.shape, a.dtype) for a in arrays] + [pltpu.HBM(l.shape, l.dtype) for l in lands]
                 + [_sds((8, 128), F32)])
    aliases = {i: 2 + i for i in range(2 * n)}
    args = [pltpu.with_memory_space_constraint(a, pltpu.HBM) for a in list(arrays) + lands] + [dep]
    res = pl.pallas_call(
        body, name=name, in_specs=[HBM] * (2 * n) + [ANY], out_shape=out_shape,
        out_specs=[SEM, SEM] + [HBM] * (2 * n) + [pl.BlockSpec(memory_space=pltpu.VMEM)],
        input_output_aliases=aliases, compiler_params=pltpu.CompilerParams(has_side_effects=EFFECT))(*args)
    return dict(send=res[0], recv=res[1], srcs=res[2:2 + n], lands=res[2 + n:2 + 2 * n], token=res[-1],
                scatter=scatter)


def _exchange_wait(name, started, after):
    n = len(started["srcs"])
    scatter = started["scatter"]

    def body(*refs):
        srcs, land_refs = refs[:n], refs[n:2 * n]
        send_sems, recv_sems = refs[2 * n], refs[2 * n + 1]
        me = _my_id()
        for cp in _remote_copies(srcs, land_refs, scatter, send_sems, recv_sems, me, False):
            cp.wait_send()
        for cp in _remote_copies(srcs, land_refs, scatter, send_sems, recv_sems, me, True):
            cp.wait_recv()

    arrs = list(started["srcs"]) + list(started["lands"])
    res = pl.pallas_call(
        body, name=name, in_specs=[HBM] * (2 * n) + [SEM, SEM, ANY],
        out_shape=[pltpu.HBM(a.shape, a.dtype) for a in arrs], out_specs=[HBM] * (2 * n),
        input_output_aliases={i: i for i in range(2 * n)},
        compiler_params=pltpu.CompilerParams(has_side_effects=EFFECT))(*arrs, started["send"], started["recv"], after)
    me = _my_id()
    out = []
    for src, land, sc in zip(res[:n], res[n:], scatter):
        own = lax.dynamic_index_in_dim(src, me, 0, keepdims=True) if sc else src[None]
        out.append(lax.dynamic_update_slice(land, own, (me,) + (0,) * (land.ndim - 1)))
    return out


def _adamw(name, parts, w, m, v):
    r, c = w.shape
    tr, tc = r, c
    if r % 8 == 0:
        tr = next(cand for cand in (128, 88, 64, 40, 8) if r % cand == 0)
    else:
        tc = 256
    c1 = 1.0 / (1.0 - ADAM_B1 ** ADAM_STEP)
    c2 = 1.0 / (1.0 - ADAM_B2 ** ADAM_STEP)

    def body(p_ref, w_ref, m_ref, v_ref, g_ref, d_ref, nm_ref, nv_ref):
        g = p_ref[0].astype(F32)
        for s in range(1, N_DEV):
            g = g + p_ref[s].astype(F32)
        mn = ADAM_B1 * m_ref[...] + (1.0 - ADAM_B1) * g
        vn = ADAM_B2 * v_ref[...] + (1.0 - ADAM_B2) * (g * g)
        g_ref[...] = g
        nm_ref[...] = mn
        nv_ref[...] = vn
        d_ref[...] = -ADAM_LR * ((mn * c1) / (jnp.sqrt(vn * c2) + ADAM_EPS) + ADAM_WD * w_ref[...])

    blk = pl.BlockSpec((tr, tc), lambda i, j: (i, j))
    return pl.pallas_call(
        body, grid=(r // tr, c // tc),
        in_specs=[pl.BlockSpec((N_DEV, tr, tc), lambda i, j: (0, i, j)), blk, blk, blk],
        out_specs=[blk] * 4, out_shape=[_sds((r, c), F32)] * 4, name=name,
        compiler_params=_params(2, VMEM_LIMIT))(parts, w, m, v)


def _pad_rows(a, rows):
    return jnp.pad(a, ((0, rows - a.shape[0]), (0, 0)))


def _lane_row(vec8, offset):
    return jnp.pad(vec8.reshape(1, 8), ((0, 0), (offset, HD - 8 - offset)))


def kernel(x, positions, attn_norm_w, w_in, conv_w, a_log, dt_bias, delta_out_norm_w, q_norm_w, k_norm_w, attn_out_norm_w, w_out, ffn_norm_w, w_gate_up, w_down, loss_target, m_attn_norm_w, m_w_in, m_conv_w, m_a_log, m_dt_bias, m_delta_out_norm_w, m_q_norm_w, m_k_norm_w, m_attn_out_norm_w, m_w_out, m_ffn_norm_w, m_w_gate_up, m_w_down, v_attn_norm_w, v_w_in, v_conv_w, v_a_log, v_dt_bias, v_delta_out_norm_w, v_q_norm_w, v_k_norm_w, v_attn_out_norm_w, v_w_out, v_ffn_norm_w, v_w_gate_up, v_w_down):
    x2 = x[0]
    t, d = x2.shape
    target = loss_target[0]
    pos_col = positions.reshape(t, 1)
    half = HD // 2
    inv = (ROPE_THETA ** (-np.arange(half, dtype=np.float32) / half)).astype(np.float32)
    inv_row = jnp.asarray(np.concatenate([inv, inv]).reshape(1, HD))

    n_in = w_in.shape[2]
    n_gu = w_gate_up.shape[2]
    w_in_g, conv_g = _gather_two_level("gather_in", [w_in[0].astype(BF16), _pad_rows(conv_w[0], 8)])
    out_fly = _exchange_start("gather_out_start", [w_out[0].astype(BF16)], [False], conv_g)
    gu_fly = _exchange_start("gather_gate_up_start", [w_gate_up[0].astype(BF16)], [False], out_fly["token"])
    down_fly = _exchange_start("gather_down_start", [w_down[0].astype(BF16)], [False], gu_fly["token"])
    n_main = 4 * GW
    n_small = 2 * N_HEADS
    segments = [(0, n_main, 0), (n_main + n_small, N_DEV * n_in, n_main), (n_main, n_main + n_small, 7 * GW)]
    pieces = []
    for lo, hi, _ in segments:
        f = lo
        while f < hi:
            j = f // n_in
            end = min(hi, (j + 1) * n_in)
            pieces.append(w_in_g[j][:, f - j * n_in:end - j * n_in])
            f = end
    w_cat = jnp.concatenate(pieces + [jnp.zeros((d, HD - n_small), BF16)], axis=1)
    n_cat = w_cat.shape[1]
    small_blk = (7 * GW) // HD
    conv_w8 =jnp.transpose(conv_g, (1, 0, 2)).reshape(8, 3 * GW)
    alog_row = _lane_row(a_log[0], 8)
    dtb_row = _lane_row(dt_bias[0], 8)

    tm = min(2048, t)
    h1 = _rms_fwd("norm1", x2, attn_norm_w, down_fly["token"])
    tmp, tnp = min(1024, t), n_cat // 3
    proj = _mm("in_proj", h1, w_cat, grid=(t // tmp, n_cat // tnp, 1),
               a_spec=pl.BlockSpec((tmp, d), lambda i, j, k: (i, 0)),
               b_spec=pl.BlockSpec((d, tnp), lambda i, j, k: (0, j)),
               o_spec=pl.BlockSpec((tmp, tnp), lambda i, j, k: (i, j)),
               out_shape=_sds((t, n_cat), F32), ca=1, cb=0, nk=1)
    qn = _conv_fwd("conv_q", proj, conv_w8, 0, True, HD ** -0.5)
    kn = _conv_fwd("conv_k", proj, conv_w8, 1, True, 1.0)
    vv = _conv_fwd("conv_v", proj, conv_w8, 2, False, 1.0)
    beta_b, gc_b = _gates_fwd("gates", proj, small_blk, alog_row, dtb_row)
    u, w, p, tinv, qd, kd = _delta_prep("delta_prep", qn, kn, vv, beta_b, gc_b)
    oa_raw, vn, s_hist = _delta_scan("delta_scan", u, w, p, qd, kd, gc_b)

    aq = _qk_fwd("attn_q", proj, 4, q_norm_w, pos_col, inv_row)
    ak = _qk_fwd("attn_k", proj, 5, k_norm_w, pos_col, inv_row)
    ob, lse = _attn_fwd("attn_fwd", aq, ak, proj, 6)
    mixed = _mix_fwd("mix", oa_raw, proj, 3, ob, delta_out_norm_w, attn_out_norm_w)
    (w_out_g,) = _exchange_wait("gather_out_wait", out_fly, mixed)
    w_out_full = w_out_g.reshape(2 * GW, d)
    tn = 512
    x1, h2 = _out_proj_norm("out_proj", mixed, w_out_full, x2, ffn_norm_w)
    per = N_DEV // 2
    (w_gu_g,) = _exchange_wait("gather_gate_up_wait", gu_fly, h2)
    gu3, act = _gate_up_swiglu("gate_up", h2, w_gu_g)
    (w_down_g,) = _exchange_wait("gather_down_wait", down_fly, act)
    w_down_full = w_down_g.reshape(D_FF, d)
    tmd = min(1024, t)
    dy, dy16, loss_tile = _down_loss("down_proj", act, w_down_full, x1, target)
    loss = lax.psum(loss_tile[0, 0], ("x", "y", "c"))

    tk, nkt = t, 1
    g_down = _mm("g_down", act, dy16, dep=loss.reshape(1, 1), grid=(D_FF // 1408, d // 512, nkt),
                 a_spec=pl.BlockSpec((tk, 1408), lambda i, j, k: (k, i)),
                 b_spec=pl.BlockSpec((tk, 512), lambda i, j, k: (k, j)),
                 o_spec=pl.BlockSpec((1408, 512), lambda i, j, k: (i, j)),
                 out_shape=_sds((D_FF, d), F32), ca=0, cb=0, nk=nkt)
    down_g_fly = _exchange_start("reduce_down_start", [g_down.reshape(N_DEV, D_FF // N_DEV, d)], [True], dy16)
    dgu3 = _d_gate_up("d_gate_up", dy16, w_down_full, gu3, down_g_fly["token"])
    g_gu = _mm("g_gate_up", h2, dgu3, grid=(d // 512, N_DEV, nkt),
               a_spec=pl.BlockSpec((tk, 512), lambda i, j, k: (k, i)),
               b_spec=pl.BlockSpec((None, tk, n_gu), lambda i, j, k: (j // per, k, j % per)),
               o_spec=pl.BlockSpec((None, 512, n_gu), lambda i, j, k: (j, i, 0)),
               out_shape=_sds((N_DEV, d, n_gu), F32), ca=0, cb=0, nk=nkt)
    gu_g_fly = _exchange_start("reduce_gate_up_start", [g_gu], [True], dy16)
    dh2 = _d_h2("d_h2", dgu3, w_gu_g, gu_g_fly["token"])
    dx1, dx1_16, g_ffn_norm = _rms_bwd("norm2_bwd", x1, ffn_norm_w, dh2, dy)

    g_out = _mm("g_out", mixed, dx1_16, grid=((2 * GW) // 512, 1, nkt),
                a_spec=pl.BlockSpec((tk, 512), lambda i, j, k: (k, i)),
                b_spec=pl.BlockSpec((tk, d), lambda i, j, k: (k, 0)),
                o_spec=pl.BlockSpec((512, d), lambda i, j, k: (i, 0)),
                out_shape=_sds((2 * GW, d), F32), ca=0, cb=0, nk=nkt)
    out_g_fly = _exchange_start("reduce_out_start", [g_out.reshape(N_DEV, (2 * GW) // N_DEV, d)], [True], g_ffn_norm)
    dmixed = _mm("d_mixed", dx1_16, w_out_full, dep=out_g_fly["token"], grid=(t // tm, (2 * GW) // tn, 1),
                 a_spec=pl.BlockSpec((tm, d), lambda i, j, k: (i, 0)),
                 b_spec=pl.BlockSpec((tn, d), lambda i, j, k: (j, 0)),
                 o_spec=pl.BlockSpec((tm, tn), lambda i, j, k: (i, j)),
                 out_shape=_sds((t, 2 * GW), F32), ca=1, cb=1, nk=1)
    doa, dproj, dob, delta, g_dn, g_an = _mix_bwd("mix_bwd", dmixed, oa_raw, proj, 3, ob,
                                                  delta_out_norm_w, attn_out_norm_w, out_g_fly["token"])
    d_aq, d_ak, d_av = _attn_bwd("attn_bwd", aq, ak, proj, 6, dob, lse, delta)
    dproj, g_qn = _qk_bwd("attn_q_bwd", proj, 4, q_norm_w, pos_col, inv_row, d_aq, dproj)
    dproj, g_kn = _qk_bwd("attn_k_bwd", proj, 5, k_norm_w, pos_col, inv_row, d_ak, dproj)
    dproj = _cast_into("attn_v_bwd", d_av, dproj, 6)

    dvn, dqd, dkd, dw, ddec = _delta_scan_bwd("delta_scan_bwd", doa, w, p, qd, kd, gc_b, vn, s_hist)
    dqn, dkn, dvv, dbeta_b, dg_b = _delta_prep_bwd("delta_prep_bwd", qn, kn, vv, beta_b, gc_b, tinv, u, w, vn,
                                                   doa, dvn, dqd, dkd, dw, ddec)
    dproj, gcw_q = _conv_bwd("conv_q_bwd", proj, conv_w8, dqn, dproj, 0, True, HD ** -0.5)
    dproj, gcw_k = _conv_bwd("conv_k_bwd", proj, conv_w8, dkn, dproj, 1, True, 1.0)
    dproj, gcw_v = _conv_bwd("conv_v_bwd", proj, conv_w8, dvv, dproj, 2, False, 1.0)
    dproj, g_alog_row, g_dtb_row = _gates_bwd("gates_bwd", proj, small_blk, alog_row, dtb_row, dbeta_b, dg_b, dproj)
    tmc = 384
    g_cat = _mm("g_in", dproj, h1, grid=(n_cat // tmc, 1, nkt),
                a_spec=pl.BlockSpec((tk, tmc), lambda i, j, k: (k, i)),
                b_spec=pl.BlockSpec((tk, d), lambda i, j, k: (k, 0)),
                o_spec=pl.BlockSpec((tmc, d), lambda i, j, k: (i, 0)),
                out_shape=_sds((n_cat, d), F32), ca=0, cb=0, nk=nkt)
    parts = []
    for j in range(N_DEV):
        cols = []
        for lo, hi, start in sorted(segments):
            a, b = max(lo, j * n_in), min(hi, (j + 1) * n_in)
            if a < b:
                cols.append(g_cat[start + a - lo:start + b - lo])
        parts.append(cols[0] if len(cols) == 1 else jnp.concatenate(cols, axis=0))
    g_in_parts = jnp.stack(parts).astype(BF16)
    g_conv = jnp.concatenate([gcw_q, gcw_k, gcw_v], axis=1)
    n_cw = conv_w.shape[2]
    g_conv_parts = jnp.transpose(g_conv.reshape(8, N_DEV, n_cw), (1, 0, 2))
    in_g_fly = _exchange_start("reduce_in_start", [g_in_parts, g_conv_parts], [True] * 2, g_dtb_row)
    tmh1 = min(512, t)
    dh1 = _mm("d_h1", dproj, w_cat, dep=in_g_fly["token"], grid=(t // tmh1, d // 1024, 1),
              a_spec=pl.BlockSpec((tmh1, n_cat), lambda i, j, k: (i, 0)),
              b_spec=pl.BlockSpec((1024, n_cat), lambda i, j, k: (j, 0)),
              o_spec=pl.BlockSpec((tmh1, 1024), lambda i, j, k: (i, j)),
              out_shape=_sds((t, d), F32), ca=1, cb=1, nk=1)
    grad_x, _, g_attn_norm = _rms_bwd("norm1_bwd", x2, attn_norm_w, dh1, dx1)

    small_rows = [g_attn_norm.reshape(d // HD, HD), g_ffn_norm.reshape(d // HD, HD), g_dn, g_qn, g_kn, g_an,
                  g_alog_row, g_dtb_row]
    small_pack = _pad_rows(jnp.concatenate(small_rows, axis=0), 40)
    (r_down,) = _exchange_wait("reduce_down_wait", down_g_fly, grad_x)
    (r_gu,) = _exchange_wait("reduce_gate_up_wait", gu_g_fly, grad_x)
    (r_out,) = _exchange_wait("reduce_out_wait", out_g_fly, grad_x)
    res_gu = [a[None] for a in _adamw("adamw_gate_up", r_gu, w_gate_up[0], m_w_gate_up[0], v_w_gate_up[0])]
    res_down = [a[None] for a in _adamw("adamw_down", r_down, w_down[0], m_w_down[0], v_w_down[0])]
    res_out = [a[None] for a in _adamw("adamw_out", r_out, w_out[0], m_w_out[0], v_w_out[0])]
    done = (res_gu[3][0, :1, :1] + res_down[3][0, :1, :1] + res_out[3][0, :1, :1])
    (r_small,) = _exchange("gather_small_grads", [small_pack], [False], done)

    def pack_small(an, fn, dn, qn_, kn_, aon, al, db):
        rows = [an.reshape(d // HD, HD), fn.reshape(d // HD, HD), dn, qn_, kn_, aon,
                _lane_row(al[0], 8), _lane_row(db[0], 8)]
        return _pad_rows(jnp.concatenate(rows, axis=0), 40)

    def unpack_small(pk):
        nr = d // HD
        return dict(attn_norm_w=pk[:nr].reshape(1, d), ffn_norm_w=pk[nr:2 * nr].reshape(1, d),
                    delta_out_norm_w=pk[2 * nr:2 * nr + 1], q_norm_w=pk[2 * nr + 1:2 * nr + 2],
                    k_norm_w=pk[2 * nr + 2:2 * nr + 3], attn_out_norm_w=pk[2 * nr + 3:2 * nr + 4],
                    a_log=pk[2 * nr + 4:2 * nr + 5, 8:16], dt_bias=pk[2 * nr + 5:2 * nr + 6, 8:16])

    res_small = _adamw("adamw_small", r_small,
                       pack_small(attn_norm_w, ffn_norm_w, delta_out_norm_w, q_norm_w, k_norm_w, attn_out_norm_w, a_log, dt_bias),
                       pack_small(m_attn_norm_w, m_ffn_norm_w, m_delta_out_norm_w, m_q_norm_w, m_k_norm_w, m_attn_out_norm_w, m_a_log, m_dt_bias),
                       pack_small(v_attn_norm_w, v_ffn_norm_w, v_delta_out_norm_w, v_q_norm_w, v_k_norm_w, v_attn_out_norm_w, v_a_log, v_dt_bias))
    small = [unpack_small(a) for a in res_small]
    r_in, r_conv = _exchange_wait("reduce_in_wait", in_g_fly, res_small[0])
    res_in = [jnp.transpose(a)[None] for a in _adamw("adamw_in", r_in, jnp.transpose(w_in[0]), jnp.transpose(m_w_in[0]),
                                                     jnp.transpose(v_w_in[0]))]
    res_conv =[a[None, :4] for a in _adamw("adamw_conv", r_conv, _pad_rows(conv_w[0], 8), _pad_rows(m_conv_w[0], 8),
                                            _pad_rows(v_conv_w[0], 8))]

    outs = [loss, grad_x[None]]
    for i in range(4):
        s = small[i]
        outs += [s["attn_norm_w"], res_in[i], res_conv[i], s["a_log"], s["dt_bias"], s["delta_out_norm_w"],
                 s["q_norm_w"], s["k_norm_w"], s["attn_out_norm_w"], res_out[i], s["ffn_norm_w"], res_gu[i],
                 res_down[i]]
    return tuple(outs)
```

```python
import functools

import numpy as np
import jax
import jax.numpy as jnp
from jax import lax
from jax.experimental import pallas as pl
from jax.experimental.pallas import tpu as pltpu

F32 = jnp.float32
BF16 = jnp.bfloat16

N_DEV = 8
N_HEADS = 8
HD = 128
GW = N_HEADS * HD
CHUNK = 64
PAIR = 2 * CHUNK
SPAN = 128
DILATIONS = (1, 4, 16)
ROPE_THETA = 10000.0
EPS = 1e-6
D_FF = 5632
ADAM_LR, ADAM_B1, ADAM_B2, ADAM_EPS, ADAM_WD, ADAM_STEP = 0.001, 0.9, 0.999, 1e-8, 0.01, 10
NEG = -1e30
VMEM_LIMIT = 56 * 1024 * 1024
ANY = pl.BlockSpec(memory_space=pl.ANY)
HEADS_PER_STEP = 8


def _params(n_grid, vmem=VMEM_LIMIT):
    return pltpu.CompilerParams(dimension_semantics=("arbitrary",) * n_grid, vmem_limit_bytes=vmem)


def _sds(shape, dtype):
    return jax.ShapeDtypeStruct(tuple(shape), dtype)


def _sigmoid(x):
    return 1.0 / (1.0 + jnp.exp(-x))


def _silu(x):
    return x * _sigmoid(x)


def _softplus(x):
    return jnp.maximum(x, 0.0) + jnp.log(1.0 + jnp.exp(-jnp.abs(x)))


def _dot(a, b, ca, cb, precision=None):
    return lax.dot_general(a, b, (((ca,), (cb,)), ((), ())), precision=precision,
                           preferred_element_type=F32)


def _b16(x):
    return x if x.dtype == BF16 else x.astype(BF16)


def _split(x):
    hi = x.astype(BF16)
    return hi, (x - hi.astype(F32)).astype(BF16)


def _dot3(a, b, ca, cb):
    a_hi, a_lo = _split(a)
    b_hi, b_lo = _split(b)
    return _dot(a_hi, b_hi, ca, cb) + (_dot(a_hi, b_lo, ca, cb) + _dot(a_lo, b_hi, ca, cb))


def _iota2(shape, axis):
    return lax.broadcasted_iota(jnp.int32, shape, axis)


def _mm(name, a, b, *, grid, a_spec, b_spec, o_spec, out_shape, ca, cb, nk, add=None, add_spec=None,
        dep=None, vmem=VMEM_LIMIT):
    has_add = add is not None
    n_in = 2 + has_add + (dep is not None)

    def body(*refs):
        a_ref, b_ref = refs[0], refs[1]
        e_ref = refs[2] if has_add else None
        o_ref = refs[n_in]
        part = _dot(_b16(a_ref[...]), _b16(b_ref[...]), ca, cb)
        if nk == 1:
            if has_add:
                part = part + e_ref[...]
            o_ref[...] = part.astype(o_ref.dtype)
            return
        acc = refs[-1]
        k = pl.program_id(2)

        @pl.when(k == 0)
        def _():
            acc[...] = part

        @pl.when(k > 0)
        def _():
            acc[...] += part

        @pl.when(k == nk - 1)
        def _():
            res = acc[...]
            if has_add:
                res = res + e_ref[...]
            o_ref[...] = res.astype(o_ref.dtype)

    in_specs = [a_spec, b_spec] + ([add_spec] if has_add else []) + ([ANY] if dep is not None else [])
    args = (a, b) + ((add,) if has_add else ()) + ((dep,) if dep is not None else ())
    blk = [d for d in o_spec.block_shape if d is not None]
    scratch = [pltpu.VMEM(tuple(blk), F32)] if nk > 1 else []
    return pl.pallas_call(body, grid=grid, in_specs=in_specs, out_specs=o_spec, out_shape=out_shape,
                          scratch_shapes=scratch, name=name, compiler_params=_params(3, vmem))(*args)


def _rms_f(xv, wv):
    return xv * lax.rsqrt(jnp.mean(xv * xv, axis=-1, keepdims=True) + EPS) * wv


def _rms_fwd(name, x, w, dep):
    t, d = x.shape
    tm = min(512, t)

    def body(x_ref, w_ref, dep_ref, o_ref):
        o_ref[...] = _rms_f(x_ref[...], w_ref[...]).astype(BF16)

    row = pl.BlockSpec((tm, d), lambda i: (i, 0))
    vec = pl.BlockSpec((1, d), lambda i: (0, 0))
    return pl.pallas_call(body, grid=(t // tm,), in_specs=[row, vec, ANY], out_specs=row,
                          out_shape=_sds((t, d), BF16), name=name, compiler_params=_params(1))(x, w, dep)


def _rms_bwd(name, x, w, dh, res):
    t, d = x.shape
    tm = min(256, t)

    def body(x_ref, w_ref, dh_ref, res_ref, dx_ref, dx16_ref, dw_ref):
        _, vjp = jax.vjp(_rms_f, x_ref[...], w_ref[...])
        dxv, dwv = vjp(dh_ref[...])
        dxv = dxv + res_ref[...]
        dx_ref[...] = dxv
        dx16_ref[...] = dxv.astype(BF16)

        @pl.when(pl.program_id(0) == 0)
        def _():
            dw_ref[...] = jnp.zeros_like(dw_ref)

        dw_ref[...] += dwv

    row = pl.BlockSpec((tm, d), lambda i: (i, 0))
    vec = pl.BlockSpec((1, d), lambda i: (0, 0))
    return pl.pallas_call(body, grid=(t // tm,), in_specs=[row, vec, row, row], out_specs=[row, row, vec],
                          out_shape=[_sds((t, d), F32), _sds((t, d), BF16), _sds((1, d), F32)], name=name,
                          compiler_params=_params(1))(x, w, dh, res)


def _conv_taps(xv, w_ref, rows):
    c = w_ref[3:4, :] * xv
    for s in (1, 2, 3):
        c = c + w_ref[3 - s:4 - s, :] * jnp.where(rows >= s, pltpu.roll(xv, s, 0), 0.0)
    return c


def _post_conv(c, l2, scale):
    y = _silu(c)
    if l2:
        y = y * lax.rsqrt(jnp.sum(y * y, axis=-1, keepdims=True) + EPS) * scale
    return y


def _conv_fwd(name, proj, conv_w8, group, l2, scale):
    t = proj.shape[0]

    def body(x_ref, w_ref, o_ref):
        rows = _iota2((t, HD), 0)
        o_ref[...] = _post_conv(_conv_taps(x_ref[...], w_ref, rows), l2, scale)

    return pl.pallas_call(
        body, grid=(N_HEADS,),
        in_specs=[pl.BlockSpec((t, HD), lambda h: (0, h + group * N_HEADS)),
                  pl.BlockSpec((8, HD), lambda h: (0, h + group * N_HEADS))],
        out_specs=pl.BlockSpec((t, HD), lambda h: (0, h)),
        out_shape=_sds((t, GW), F32), name=name, compiler_params=_params(1, VMEM_LIMIT))(proj, conv_w8)


def _conv_bwd(name, proj, conv_w8, dn, dproj, group, l2, scale):
    t = proj.shape[0]

    def body(x_ref, w_ref, dn_ref, dproj_ref, dx_ref, dw_ref):
        rows = _iota2((t, HD), 0)
        xv = x_ref[...]
        c = _conv_taps(xv, w_ref, rows)
        _, vjp = jax.vjp(lambda cc: _post_conv(cc, l2, scale), c)
        (dc,) = vjp(dn_ref[...])
        dx = w_ref[3:4, :] * dc
        dw = jnp.zeros((8, HD), F32)
        rid = _iota2((8, HD), 0)
        dw = dw + jnp.where(rid == 3, jnp.sum(dc * xv, axis=0, keepdims=True), 0.0)
        for s in (1, 2, 3):
            dx = dx + w_ref[3 - s:4 - s, :] * jnp.where(rows < t - s, pltpu.roll(dc, t - s, 0), 0.0)
            xs = jnp.where(rows >= s, pltpu.roll(xv, s, 0), 0.0)
            dw = dw + jnp.where(rid == 3 - s, jnp.sum(dc * xs, axis=0, keepdims=True), 0.0)
        dx_ref[...] = dx.astype(BF16)
        dw_ref[...] = dw

    return pl.pallas_call(
        body, grid=(N_HEADS,),
        in_specs=[pl.BlockSpec((t, HD), lambda h: (0, h + group * N_HEADS)),
                  pl.BlockSpec((8, HD), lambda h: (0, h + group * N_HEADS)),
                  pl.BlockSpec((t, HD), lambda h: (0, h)), ANY],
        out_specs=[pl.BlockSpec((t, HD), lambda h: (0, h + group * N_HEADS)), pl.BlockSpec((8, HD), lambda h: (0, h))],
        out_shape=[_sds(dproj.shape, BF16), _sds((8, GW), F32)], input_output_aliases={3: 0}, name=name,
        compiler_params=_params(1, VMEM_LIMIT))(proj, conv_w8, dn, dproj)


def _chunk_cumsum(g, rows):
    pos = rows % CHUNK
    s = 1
    while s < CHUNK:
        g = g + jnp.where(pos >= s, pltpu.roll(g, s, 0), 0.0)
        s *= 2
    return g


def _gates_fwd(name, proj, small_blk, alog_row, dtb_row):
    t = proj.shape[0]
    tm = min(256, t)

    def body(s_ref, a_ref, b_ref, beta_ref, gc_ref):
        sm = s_ref[...]
        beta = _sigmoid(sm)
        g = -jnp.exp(a_ref[...]) * _softplus(sm + b_ref[...])
        gc = _chunk_cumsum(g, _iota2((tm, HD), 0))
        lane = _iota2((tm, HD), 1)
        for h in range(N_HEADS):
            bcol = jnp.sum(jnp.where(lane == h, beta, 0.0), axis=1, keepdims=True)
            gcol = jnp.sum(jnp.where(lane == 8 + h, gc, 0.0), axis=1, keepdims=True)
            beta_ref[:, h * HD:(h + 1) * HD] = jnp.broadcast_to(bcol, (tm, HD))
            gc_ref[:, h * HD:(h + 1) * HD] = jnp.broadcast_to(gcol, (tm, HD))

    vec = pl.BlockSpec((1, HD), lambda i: (0, 0))
    wide = pl.BlockSpec((tm, GW), lambda i: (i, 0))
    return pl.pallas_call(
        body, grid=(t // tm,),
        in_specs=[pl.BlockSpec((tm, HD), lambda i: (i, small_blk)), vec, vec], out_specs=[wide, wide],
        out_shape=[_sds((t, GW), F32), _sds((t, GW), F32)], name=name,
        compiler_params=_params(1))(proj, alog_row, dtb_row)


def _gates_bwd(name, proj, small_blk, alog_row, dtb_row, dbeta_b, dg_b, dproj):
    t = proj.shape[0]
    tm = min(256, t)

    def body(s_ref, a_ref, b_ref, db_ref, dg_ref, dproj_ref, ds_ref, da_ref, dbias_ref):
        sm = s_ref[...]
        lane = _iota2((tm, HD), 1)
        db = jnp.zeros((tm, HD), F32)
        dg = jnp.zeros((tm, HD), F32)
        for h in range(N_HEADS):
            db = db + jnp.where(lane == h, db_ref[:, h * HD:(h + 1) * HD], 0.0)
            dg = dg + jnp.where(lane == 8 + h, dg_ref[:, h * HD:(h + 1) * HD], 0.0)
        beta = _sigmoid(sm)
        ea = jnp.exp(a_ref[...])
        pre = sm + b_ref[...]
        g = -ea * _softplus(pre)
        dpre = dg * (-ea) * _sigmoid(pre)
        ds_ref[...] = (db * beta * (1.0 - beta) + dpre).astype(BF16)

        @pl.when(pl.program_id(0) == 0)
        def _():
            da_ref[...] = jnp.zeros_like(da_ref)
            dbias_ref[...] = jnp.zeros_like(dbias_ref)

        da_ref[...] += jnp.sum(dg * g, axis=0, keepdims=True)
        dbias_ref[...] += jnp.sum(dpre, axis=0, keepdims=True)

    vec = pl.BlockSpec((1, HD), lambda i: (0, 0))
    wide = pl.BlockSpec((tm, GW), lambda i: (i, 0))
    return pl.pallas_call(
        body, grid=(t // tm,),
        in_specs=[pl.BlockSpec((tm, HD), lambda i: (i, small_blk)), vec, vec, wide, wide, ANY],
        out_specs=[pl.BlockSpec((tm, HD), lambda i: (i, small_blk)), vec, vec],
        out_shape=[_sds(dproj.shape, BF16), _sds((1, HD), F32), _sds((1, HD), F32)],
        input_output_aliases={5: 0}, name=name,
        compiler_params=_params(1))(proj, alog_row, dtb_row, dbeta_b, dg_b, dproj)


def _pair_masks():
    ii = _iota2((PAIR, PAIR), 0)
    jj = _iota2((PAIR, PAIR), 1)
    same = (ii // CHUNK) == (jj // CHUNK)
    return ii, jj, same & (ii >= jj), same & (ii > jj)


def _to_row(col_b, ii, jj):
    return jnp.sum(jnp.where(ii == jj, col_b, 0.0), axis=0, keepdims=True)


def _to_col(row, ii, jj):
    return jnp.sum(jnp.where(ii == jj, jnp.broadcast_to(row, (PAIR, PAIR)), 0.0), axis=1, keepdims=True)


def _decay_parts(gc, last_a, last_b, ii, jj, causal):
    diff = gc - _to_row(gc, ii, jj)
    dmat = jnp.where(causal, jnp.exp(jnp.where(causal, diff, 0.0)), 0.0)
    glast = jnp.where(ii < CHUNK, last_a, last_b)
    return dmat, jnp.exp(gc), jnp.exp(glast - gc)


def _unit_lower_inverse(lows, ii, jj):
    eye = jnp.where(ii == jj, 1.0, 0.0)
    mm = lambda xs, ys: [_dot3(a, b, 1, 0) for a, b in zip(xs, ys)]
    plus = lambda xs: [eye + a for a in xs]
    minus = lambda xs: [eye - a for a in xs]
    d1 = [jnp.where((ii // 16) == (jj // 16), low, 0.0) for low in lows]
    d2 = mm(d1, d1)
    a = mm(minus(d1), plus(d2))
    d4 = mm(d2, d2)
    a = mm(a, plus(d4))
    d8 = mm(d4, d4)
    td = mm(a, plus(d8))
    n1 = mm(td, [low - d for low, d in zip(lows, d1)])
    n2 = mm(n1, n1)
    return mm(mm(minus(n1), plus(n2)), td)


def _delta_prep(name, qn, kn, vv, beta_b, gc_b):
    t = qn.shape[0]

    def body(q_ref, k_ref, v_ref, b_ref, g_ref, u_ref, w_ref, p_ref, t_ref, qd_ref, kd_ref):
        ii, jj, causal, strict = _pair_masks()
        sls = [slice(hh * HD, (hh + 1) * HD) for hh in range(HEADS_PER_STEP)]
        lows = []
        for sl in sls:
            q, k, beta = q_ref[:, sl], k_ref[:, sl], b_ref[:, sl]
            dmat, gam, e2 = _decay_parts(g_ref[:, sl], g_ref[CHUNK - 1:CHUNK, sl], g_ref[PAIR - 1:PAIR, sl],
                                         ii, jj, causal)
            k16 = _b16(k)
            lows.append(jnp.where(strict, beta * _dot(k16, k16, 1, 1) * dmat, 0.0))
            p_ref[:, sl] = jnp.where(causal, _dot(_b16(q), k16, 1, 1) * dmat, 0.0).astype(BF16)
            qd_ref[:, sl] = (q * gam).astype(BF16)
            kd_ref[:, sl] = (k * e2).astype(BF16)
        for sl, tinv in zip(sls, _unit_lower_inverse(lows, ii, jj)):
            beta = b_ref[:, sl]
            t_ref[:, sl] = tinv
            u_ref[:, sl] = _dot3(tinv, v_ref[:, sl] * beta, 1, 0)
            w_ref[:, sl] = _dot3(tinv, k_ref[:, sl] * (beta * jnp.exp(g_ref[:, sl])), 1, 0).astype(BF16)

    blk = pl.BlockSpec((PAIR, HEADS_PER_STEP * HD), lambda i, h: (i, h))
    return pl.pallas_call(
        body, grid=(t // PAIR, N_HEADS // HEADS_PER_STEP), in_specs=[blk] * 5, out_specs=[blk] * 6,
        out_shape=[_sds((t, GW), F32), _sds((t, GW), BF16), _sds((t, GW), BF16), _sds((t, GW), F32),
                   _sds((t, GW), BF16), _sds((t, GW), BF16)],
        name=name, compiler_params=_params(2))(qn, kn, vv, beta_b, gc_b)


def _delta_scan(name, u, w, p, qd, kd, gc_b):
    t = u.shape[0]
    n = t // CHUNK

    def body(u_ref, w_ref, p_ref, qd_ref, kd_ref, g_ref, o_ref, vn_ref, sh_ref, state):
        @pl.when(pl.program_id(0) == 0)
        def _():
            state[...] = jnp.zeros_like(state)

        sls = [slice(h * HD, (h + 1) * HD) for h in range(N_HEADS)]
        heads = range(N_HEADS)
        s = [state[h] for h in heads]
        for h in heads:
            sh_ref[h] = s[h]
        s16 = [_b16(a) for a in s]
        ws = [_dot(w_ref[:, sls[h]], s16[h], 1, 0) for h in heads]
        qs = [_dot(qd_ref[:, sls[h]], s16[h], 1, 0) for h in heads]
        vn16 = [_b16(u_ref[:, sls[h]] - ws[h]) for h in heads]
        pv = [_dot(p_ref[:, sls[h]], jnp.concatenate([vn16[h], vn16[h]], axis=0), 1, 0) for h in heads]
        kv = [_dot(kd_ref[:, sls[h]], vn16[h], 0, 0) for h in heads]
        for h in heads:
            o_ref[:, sls[h]] = qs[h] + pv[h]
            vn_ref[:, sls[h]] = vn16[h]
            state[h] = s[h] * jnp.exp(g_ref[CHUNK - 1:CHUNK, sls[h]]) + kv[h]

    blk = pl.BlockSpec((CHUNK, GW), lambda i: (i, 0))
    return pl.pallas_call(
        body, grid=(n,), in_specs=[blk] * 6,
        out_specs=[blk, blk, pl.BlockSpec((None, N_HEADS, HD, HD), lambda i: (i, 0, 0, 0))],
        out_shape=[_sds((t, GW), F32), _sds((t, GW), BF16), _sds((n, N_HEADS, HD, HD), F32)],
        scratch_shapes=[pltpu.VMEM((N_HEADS, HD, HD), F32)], name=name,
        compiler_params=_params(1))(u, w, p, qd, kd, gc_b)


def _delta_scan_bwd(name, do, w, p, qd, kd, gc_b, vn, s_hist):
    t = do.shape[0]
    n = t // CHUNK

    def body(do_ref, w_ref, p_ref, qd_ref, kd_ref, g_ref, vn_ref, sh_ref,
             dvn_ref, dqd_ref, dkd_ref, dw_ref, ddec_ref, dstate):
        @pl.when(pl.program_id(0) == 0)
        def _():
            dstate[...] = jnp.zeros_like(dstate)

        sls = [slice(h * HD, (h + 1) * HD) for h in range(N_HEADS)]
        heads = range(N_HEADS)
        ds = [dstate[h] for h in heads]
        ds16 = [_b16(a) for a in ds]
        s16 = [_b16(sh_ref[h]) for h in heads]
        do16 = [_b16(do_ref[:, sls[h]]) for h in heads]
        ptdo = [_dot(p_ref[:, sls[h]], do16[h], 0, 0) for h in heads]
        kds = [_dot(kd_ref[:, sls[h]], ds16[h], 1, 0) for h in heads]
        qdo = [_dot(qd_ref[:, sls[h]], do16[h], 0, 0) for h in heads]
        for h in heads:
            dqd_ref[:, sls[h]] = _dot(do16[h], s16[h], 1, 1)
            dkd_ref[:, sls[h]] = _dot(vn_ref[:, sls[h]], ds16[h], 1, 1)
        dvn = [ptdo[h][:CHUNK, :] + ptdo[h][CHUNK:, :] + kds[h] for h in heads]
        dvn16 = [_b16(a) for a in dvn]
        wdv = [_dot(w_ref[:, sls[h]], dvn16[h], 0, 0) for h in heads]
        for h in heads:
            dvn_ref[:, sls[h]] = dvn[h]
            dw_ref[:, sls[h]] = -_dot(dvn16[h], s16[h], 1, 1)
            tot = jnp.sum(jnp.sum(sh_ref[h] * ds[h], axis=1, keepdims=True), axis=0, keepdims=True)
            ddec_ref[:, sls[h]] = jnp.broadcast_to(tot, (8, HD))
            dstate[h] = ds[h] * jnp.exp(g_ref[CHUNK - 1:CHUNK, sls[h]]) + qdo[h] - wdv[h]

    blk = pl.BlockSpec((CHUNK, GW), lambda i: (n - 1 - i, 0))
    return pl.pallas_call(
        body, grid=(n,),
        in_specs=[blk] * 7 + [pl.BlockSpec((None, N_HEADS, HD, HD), lambda i: (n - 1 - i, 0, 0, 0))],
        out_specs=[blk] * 4 + [pl.BlockSpec((8, GW), lambda i: (n - 1 - i, 0))],
        out_shape=[_sds((t, GW), F32)] * 4 + [_sds((n * 8, GW), F32)],
        scratch_shapes=[pltpu.VMEM((N_HEADS, HD, HD), F32)], name=name,
        compiler_params=_params(1))(do, w, p, qd, kd, gc_b, vn, s_hist)


def _delta_prep_bwd(name, qn, kn, vv, beta_b, gc_b, tinv, u, w, vn, do, dvn, dqd, dkd, dw, ddec):
    t = qn.shape[0]

    def body(q_ref, k_ref, v_ref, b_ref, g_ref, t_ref, u_ref, w_ref, vn_ref, do_ref, dvn_ref, dqd_ref,
             dkd_ref, dw_ref, ddec_ref, dq_ref, dk_ref, dv_ref, dbeta_ref, dg_ref):
        ii, jj, causal, strict = _pair_masks()
        suffix = ((ii // CHUNK) == (jj // CHUNK)) & (jj >= ii)
        first = ii < CHUNK
        rs = lambda a: jnp.sum(a, axis=1, keepdims=True)
        sls = [slice(hh * HD, (hh + 1) * HD) for hh in range(HEADS_PER_STEP)]
        xs = [_dot3(t_ref[:, sl], dvn_ref[:, sl], 0, 0) for sl in sls]
        ys = [_dot3(t_ref[:, sl], dw_ref[:, sl], 0, 0) for sl in sls]
        k16s = [_b16(k_ref[:, sl]) for sl in sls]
        kks = [_dot(k16, k16, 1, 1) for k16 in k16s]
        qks = [_dot(_b16(q_ref[:, sl]), k16, 1, 1) for sl, k16 in zip(sls, k16s)]
        dps = [jnp.where(causal, _dot(_b16(do_ref[:, sl]), vn_ref[:, sl], 1, 1), 0.0) for sl in sls]
        das = [-jnp.where(strict, _dot(_b16(x), _b16(u_ref[:, sl]), 1, 1) + _dot(_b16(y), w_ref[:, sl], 1, 1), 0.0)
               for sl, x, y in zip(sls, xs, ys)]
        for hh, sl in enumerate(sls):
            q, k, v, beta, gc = q_ref[:, sl], k_ref[:, sl], v_ref[:, sl], b_ref[:, sl], g_ref[:, sl]
            last_a, last_b = g_ref[CHUNK - 1:CHUNK, sl], g_ref[PAIR - 1:PAIR, sl]
            dmat, gam, e2 = _decay_parts(gc, last_a, last_b, ii, jj, causal)
            q16, k16 = _b16(q), k16s[hh]
            kk, qk, dp, x, y, da = kks[hh], qks[hh], dps[hh], xs[hh], ys[hh], das[hh]
            dqd, dkd = dqd_ref[:, sl], dkd_ref[:, sl]
            dpd16 = _b16(dp * dmat)
            dkk16 = _b16(da * beta * dmat)
            dq_ref[:, sl] = gam * dqd + _dot(dpd16, k16, 1, 0)
            dk_ref[:, sl] = (e2 * dkd + _dot(dpd16, q16, 0, 0) + beta * gam * y
                             + _dot(dkk16, k16, 1, 0) + _dot(dkk16, k16, 0, 0))
            dv_ref[:, sl] = beta * x
            dbeta = rs(v * x) + rs(k * gam * y) + rs(da * kk * dmat)
            dbeta_ref[:, sl] = jnp.broadcast_to(dbeta, (PAIR, HD))
            m = (dp * qk + da * beta * kk) * dmat
            dgam = rs(q * dqd) + rs(k * beta * y)
            de2 = rs(k * dkd)
            colsum = _to_col(jnp.sum(m, axis=0, keepdims=True), ii, jj)
            te2 = de2 * e2
            dgc = rs(m) - colsum + gam * dgam - te2
            tail_a = jnp.sum(jnp.where(first, te2, 0.0), axis=0, keepdims=True)
            tail_b = jnp.sum(jnp.where(first, 0.0, te2), axis=0, keepdims=True)
            dgc = dgc + jnp.where(ii == CHUNK - 1, tail_a + ddec_ref[0:1, sl] * jnp.exp(last_a), 0.0)
            dgc = dgc + jnp.where(ii == PAIR - 1, tail_b + ddec_ref[8:9, sl] * jnp.exp(last_b), 0.0)
            dgc_row = _to_row(dgc, ii, jj)
            dg = jnp.sum(jnp.where(suffix, jnp.broadcast_to(dgc_row, (PAIR, PAIR)), 0.0), axis=1, keepdims=True)
            dg_ref[:, sl] = jnp.broadcast_to(dg, (PAIR, HD))

    blk = pl.BlockSpec((PAIR, HEADS_PER_STEP * HD), lambda i, h: (i, h))
    return pl.pallas_call(
        body, grid=(t // PAIR, N_HEADS // HEADS_PER_STEP),
        in_specs=[blk] * 14 + [pl.BlockSpec((16, HEADS_PER_STEP * HD), lambda i, h: (i, h))], out_specs=[blk] * 5,
        out_shape=[_sds((t, GW), F32)] * 5, name=name,
        compiler_params=_params(2))(qn, kn, vv, beta_b, gc_b, tinv, u, w, vn, do, dvn, dqd, dkd, dw, ddec)


def _rope_tables(name, pos_col, inv_row):
    t = pos_col.shape[0]
    tm = min(1024, t)

    def body(pos_ref, inv_ref, cos_ref, sin_ref):
        ang = pos_ref[...].astype(F32) * inv_ref[...]
        lane = _iota2(ang.shape, 1)
        cos_ref[...] = jnp.cos(ang)
        sin_ref[...] = jnp.where(lane < HD // 2, -1.0, 1.0) * jnp.sin(ang)

    tab = pl.BlockSpec((tm, HD), lambda i: (i, 0))
    return pl.pallas_call(
        body, grid=(t // tm,), in_specs=[pl.BlockSpec((tm, 1), lambda i: (i, 0)), pl.BlockSpec((1, HD), lambda i: (0, 0))],
        out_specs=[tab, tab], out_shape=[_sds((t, HD), F32)] * 2, name=name,
        compiler_params=_params(1))(pos_col, inv_row)


def _head_rms(xh, wv):
    return xh * lax.rsqrt(jnp.mean(xh * xh, axis=-1, keepdims=True) + EPS) * wv


def _qk_fwd(name, proj, pair_blk, wq_row, wk_row, cos_t, sin_t):
    t = proj.shape[0]
    tm = min(256, t)

    def body(x_ref, wq_ref, wk_ref, cos_ref, sin_ref, q_ref, k_ref):
        cos, sin = cos_ref[...], sin_ref[...]
        for o_ref, w_ref, base in ((q_ref, wq_ref, 0), (k_ref, wk_ref, GW)):
            for h in range(N_HEADS):
                y = _head_rms(x_ref[:, base + h * HD:base + (h + 1) * HD], w_ref[...])
                o_ref[:, h * HD:(h + 1) * HD] = y * cos + pltpu.roll(y, HD // 2, 1) * sin

    vec = pl.BlockSpec((1, HD), lambda i: (0, 0))
    tab = pl.BlockSpec((tm, HD), lambda i: (i, 0))
    wide = pl.BlockSpec((tm, GW), lambda i: (i, 0))
    return pl.pallas_call(
        body, grid=(t // tm,),
        in_specs=[pl.BlockSpec((tm, 2 * GW), lambda i: (i, pair_blk)), vec, vec, tab, tab],
        out_specs=[wide, wide], out_shape=[_sds((t, GW), F32)] * 2, name=name,
        compiler_params=_params(1))(proj, wq_row, wk_row, cos_t, sin_t)


def _qk_bwd(name, proj, pair_blk, wq_row, wk_row, cos_t, sin_t, dq_full, dk_full, dproj):
    t = proj.shape[0]
    tm = min(256, t)

    def body(x_ref, wq_ref, wk_ref, cos_ref, sin_ref, dq_ref, dk_ref, dproj_ref, dx_ref, dwq_ref, dwk_ref):
        cos, sin = cos_ref[...], sin_ref[...]

        @pl.when(pl.program_id(0) == 0)
        def _():
            dwq_ref[...] = jnp.zeros_like(dwq_ref)
            dwk_ref[...] = jnp.zeros_like(dwk_ref)

        for dy_ref, w_ref, dw_ref, base in ((dq_ref, wq_ref, dwq_ref, 0), (dk_ref, wk_ref, dwk_ref, GW)):
            dw = jnp.zeros((1, HD), F32)
            for h in range(N_HEADS):
                dy = dy_ref[:, h * HD:(h + 1) * HD]
                dy = dy * cos - pltpu.roll(dy, HD // 2, 1) * sin
                _, vjp = jax.vjp(_head_rms, x_ref[:, base + h * HD:base + (h + 1) * HD], w_ref[...])
                dx, dwh = vjp(dy)
                dw = dw + dwh
                dx_ref[:, base + h * HD:base + (h + 1) * HD] = dx.astype(BF16)
            dw_ref[...] += dw

    vec = pl.BlockSpec((1, HD), lambda i: (0, 0))
    tab = pl.BlockSpec((tm, HD), lambda i: (i, 0))
    wide = pl.BlockSpec((tm, GW), lambda i: (i, 0))
    pair = pl.BlockSpec((tm, 2 * GW), lambda i: (i, pair_blk))
    return pl.pallas_call(
        body, grid=(t // tm,), in_specs=[pair, vec, vec, tab, tab, wide, wide, ANY],
        out_specs=[pair, vec, vec],
        out_shape=[_sds(dproj.shape, BF16), _sds((1, HD), F32), _sds((1, HD), F32)], input_output_aliases={7: 0},
        name=name, compiler_params=_params(1))(proj, wq_row, wk_row, cos_t, sin_t, dq_full, dk_full, dproj)


def _cast_into(name, x, dproj, blk_idx):
    t = x.shape[0]
    tm = min(512, t)

    def body(x_ref, dproj_ref, o_ref):
        o_ref[...] = x_ref[...].astype(BF16)

    return pl.pallas_call(
        body, grid=(t // tm,), in_specs=[pl.BlockSpec((tm, GW), lambda i: (i, 0)), ANY],
        out_specs=pl.BlockSpec((tm, GW), lambda i: (i, blk_idx)), out_shape=_sds(dproj.shape, BF16),
        input_output_aliases={1: 0}, name=name, compiler_params=_params(1))(x, dproj)


GROUP = SPAN * max(DILATIONS)
SCALE = HD ** -0.5
TILE_BATCH = 8


def _band_mask(lo):
    qi = _iota2((SPAN, 2 * SPAN), 0)
    ki = _iota2((SPAN, 2 * SPAN), 1)
    return (ki >= qi) & (ki <= qi + SPAN) & (ki >= lo)


def _tiles():
    return [(pi, r, u, rho) for pi, r in enumerate(DILATIONS) for rho in range(r) for u in range(GROUP // (SPAN * r))]


def _rows(r, u, rho):
    return pl.ds(u * SPAN * r + rho, SPAN, stride=r) if r > 1 else pl.ds(u * SPAN, SPAN)


def _attn_fwd(name, q, k, v, v_blk):
    t = q.shape[0]

    def body(qc_ref, kc_ref, vc_ref, kp_ref, vp_ref, ob_ref, lse_ref, o_scr, l_scr):
        mask_in = _band_mask(0)
        mask_edge = _band_mask(jnp.where(pl.program_id(0) == 0, SPAN, 0))
        tiles = _tiles()
        k_own = v_own = None
        for b0 in range(0, len(tiles), TILE_BATCH):
            work = []
            for pi, r, u, rho in tiles[b0:b0 + TILE_BATCH]:
                rows = _rows(r, u, rho)
                if u > 0:
                    k_prev, v_prev, mask = k_own, v_own, mask_in
                else:
                    prows = _rows(r, GROUP // (SPAN * r) - 1, rho)
                    k_prev, v_prev, mask = kp_ref[prows, :].astype(BF16), vp_ref[prows, :].astype(BF16), mask_edge
                k_own, v_own = kc_ref[rows, :].astype(BF16), vc_ref[rows, :].astype(BF16)
                work.append((pi, rows, mask, qc_ref[rows, :].astype(BF16), jnp.concatenate([k_prev, k_own], axis=0),
                             jnp.concatenate([v_prev, v_own], axis=0)))
            scores = [_dot(qt, kcat, 1, 1) for _, _, _, qt, kcat, _ in work]
            soft = []
            for (_, _, mask, _, _, _), s in zip(work, scores):
                s = jnp.where(mask, s * SCALE, NEG)
                m = jnp.max(s, axis=1, keepdims=True)
                p = jnp.exp(s - m)
                soft.append((m, _b16(p), jnp.sum(p, axis=1, keepdims=True)))
            outs = [_dot(p, vcat, 1, 0) for (_, p, _), (_, _, _, _, _, vcat) in zip(soft, work)]
            for (pi, rows, _, _, _, _), (m, _, den), o in zip(work, soft, outs):
                o_scr[pi, rows, :] = o / den
                l_scr[pi, rows, :] = jnp.broadcast_to(m + jnp.log(den), (SPAN, HD))
        step = 256
        for c in range(GROUP // step):
            sl = pl.ds(c * step, step)
            ob, lse = _merge([o_scr[i, sl, :] for i in range(3)], [l_scr[i, sl, :] for i in range(3)])
            ob_ref[sl, :] = ob
            lse_ref[sl, :] = lse

    cur = pl.BlockSpec((GROUP, HD), lambda g, h: (g, h))
    prev = pl.BlockSpec((GROUP, HD), lambda g, h: (jnp.maximum(g - 1, 0), h))
    vcur = pl.BlockSpec((GROUP, HD), lambda g, h: (g, v_blk * N_HEADS + h))
    vprev = pl.BlockSpec((GROUP, HD), lambda g, h: (jnp.maximum(g - 1, 0), v_blk * N_HEADS + h))
    return pl.pallas_call(
        body, grid=(t // GROUP, N_HEADS), in_specs=[cur, cur, vcur, prev, vprev], out_specs=[cur, cur],
        out_shape=[_sds((t, GW), F32), _sds((t, GW), F32)],
        scratch_shapes=[pltpu.VMEM((3, GROUP, HD), F32), pltpu.VMEM((3, GROUP, HD), F32)], name=name,
        compiler_params=_params(2))(q, k, v, k, v)


def _attn_bwd(name, q, k, v, v_blk, do, lse, delta):
    t = q.shape[0]
    ng = t // GROUP

    def probs(work):
        scores = [_dot(qt, kcat, 1, 1) for qt, _, _, _, kcat, _, _ in work]
        dps = [_dot(dot, vcat, 1, 1) for _, dot, _, _, _, vcat, _ in work]
        out = []
        for (_, _, lt, dlt, kcat, _, mask), s, dp in zip(work, scores, dps):
            wide = kcat.shape[0] // SPAN
            lw = jnp.concatenate([lt] * wide, axis=1) if wide > 1 else lt
            dw = jnp.concatenate([dlt] * wide, axis=1) if wide > 1 else dlt
            p = jnp.exp(jnp.where(mask, s * SCALE - lw, NEG))
            out.append((_b16(p * (dp - dw) * SCALE), _b16(p)))
        return out

    def body(qc_ref, kc_ref, vc_ref, doc_ref, lc_ref, dc_ref, kp_ref, vp_ref, qn_ref, don_ref, ln_ref, dn_ref,
             dq_ref, dk_ref, dv_ref):
        g = pl.program_id(0)
        mask_in = _band_mask(0)
        mask_edge = _band_mask(jnp.where(g == 0, SPAN, 0))
        dk_ref[...] = jnp.zeros_like(dk_ref)
        dv_ref[...] = jnp.zeros_like(dv_ref)
        tiles = _tiles()
        k_own = v_own = None
        for b0 in range(0, len(tiles), TILE_BATCH):
            where, work = [], []
            for pi, r, u, rho in tiles[b0:b0 + TILE_BATCH]:
                rows = _rows(r, u, rho)
                if u > 0:
                    prows, k_prev, v_prev, mask = _rows(r, u - 1, rho), k_own, v_own, mask_in
                else:
                    prows = _rows(r, GROUP // (SPAN * r) - 1, rho)
                    k_prev, v_prev, mask = kp_ref[prows, :].astype(BF16), vp_ref[prows, :].astype(BF16), mask_edge
                k_own, v_own = kc_ref[rows, :].astype(BF16), vc_ref[rows, :].astype(BF16)
                where.append((pi, u, rows, prows))
                work.append((qc_ref[rows, :].astype(BF16), doc_ref[rows, :].astype(BF16), lc_ref[rows, :], dc_ref[rows, :],
                             jnp.concatenate([k_prev, k_own], axis=0), jnp.concatenate([v_prev, v_own], axis=0), mask))
            dsp = probs(work)
            dqs = [_dot(ds, w[4], 1, 0) for (ds, _), w in zip(dsp, work)]
            dks = [_dot(ds, w[0], 0, 0) for (ds, _), w in zip(dsp, work)]
            dvs = [_dot(p, w[1], 0, 0) for (_, p), w in zip(dsp, work)]
            for (pi, u, rows, prows), dq_t, dk2, dv2 in zip(where, dqs, dks, dvs):
                if pi == 0:
                    dq_ref[rows, :] = dq_t
                else:
                    dq_ref[rows, :] += dq_t
                dk_ref[rows, :] += dk2[SPAN:, :]
                dv_ref[rows, :] += dv2[SPAN:, :]
                if u > 0:
                    dk_ref[prows, :] += dk2[:SPAN, :]
                    dv_ref[prows, :] += dv2[:SPAN, :]
        qi = _iota2((SPAN, SPAN), 0)
        ki = _iota2((SPAN, SPAN), 1)
        mask_next = (ki >= qi) & (ki < jnp.where(g == ng - 1, 0, SPAN))
        edge = [(r, rho) for r in DILATIONS for rho in range(r)]
        for b0 in range(0, len(edge), TILE_BATCH):
            where, work = [], []
            for r, rho in edge[b0:b0 + TILE_BATCH]:
                krows, qrows = _rows(r, GROUP // (SPAN * r) - 1, rho), _rows(r, 0, rho)
                where.append(krows)
                work.append((qn_ref[qrows, :].astype(BF16), don_ref[qrows, :].astype(BF16), ln_ref[qrows, :],
                             dn_ref[qrows, :], kc_ref[krows, :].astype(BF16), vc_ref[krows, :].astype(BF16), mask_next))
            dsp = probs(work)
            dks = [_dot(ds, w[0], 0, 0) for (ds, _), w in zip(dsp, work)]
            dvs = [_dot(p, w[1], 0, 0) for (_, p), w in zip(dsp, work)]
            for krows, dk1, dv1 in zip(where, dks, dvs):
                dk_ref[krows, :] += dk1
                dv_ref[krows, :] += dv1

    cur = pl.BlockSpec((GROUP, HD), lambda g, h: (g, h))
    prev = pl.BlockSpec((GROUP, HD), lambda g, h: (jnp.maximum(g - 1, 0), h))
    nxt = pl.BlockSpec((GROUP, HD), lambda g, h: (jnp.minimum(g + 1, ng - 1), h))
    vcur = pl.BlockSpec((GROUP, HD), lambda g, h: (g, v_blk * N_HEADS + h))
    vprev = pl.BlockSpec((GROUP, HD), lambda g, h: (jnp.maximum(g - 1, 0), v_blk * N_HEADS + h))
    return pl.pallas_call(
        body, grid=(ng, N_HEADS), in_specs=[cur, cur, vcur, cur, cur, cur, prev, vprev] + [nxt] * 4,
        out_specs=[cur] * 3,
        out_shape=[_sds((t, GW), F32)] * 3, name=name,
        compiler_params=_params(2))(q, k, v, do, lse, delta, k, v, q, do, lse, delta)


def _merge(os_, ls_):
    m = jnp.maximum(jnp.maximum(ls_[0], ls_[1]), ls_[2])
    ws = [jnp.exp(l - m) for l in ls_]
    tot = ws[0] + ws[1] + ws[2]
    ob = (ws[0] * os_[0] + ws[1] * os_[1] + ws[2] * os_[2]) / tot
    return ob, m + jnp.log(tot)


def _gated_norm(oa, z, wv):
    return _head_rms(oa, wv) * _silu(z)


def _mix_fwd(name, oa_raw, proj, z_blk, ob, w_dn, w_an):
    t = oa_raw.shape[0]
    tm = min(256, t)

    def body(oa_ref, z_ref, ob_ref, wd_ref, wa_ref, mix_ref):
        for h in range(N_HEADS):
            sl = slice(h * HD, (h + 1) * HD)
            mix_ref[:, sl] = _gated_norm(oa_ref[:, sl], z_ref[:, sl], wd_ref[...]).astype(BF16)
            mix_ref[:, GW + h * HD:GW + (h + 1) * HD] = _head_rms(ob_ref[:, sl], wa_ref[...]).astype(BF16)

    vec = pl.BlockSpec((1, HD), lambda i: (0, 0))
    wide = pl.BlockSpec((tm, GW), lambda i: (i, 0))
    return pl.pallas_call(
        body, grid=(t // tm,),
        in_specs=[wide, pl.BlockSpec((tm, GW), lambda i: (i, z_blk)), wide, vec, vec],
        out_specs=pl.BlockSpec((tm, 2 * GW), lambda i: (i, 0)),
        out_shape=_sds((t, 2 * GW), BF16), name=name,
        compiler_params=_params(1))(oa_raw, proj, ob, w_dn, w_an)


def _mix_bwd(name, dmixed, oa_raw, proj, z_blk, ob, w_dn, w_an, dep):
    t = oa_raw.shape[0]
    tm = min(256, t)

    def body(dm_ref, oa_ref, z_ref, ob_ref, wd_ref, wa_ref, dep_ref,
             doa_ref, dz_ref, dob_ref, dl_ref, dwd_ref, dwa_ref):
        dwd = jnp.zeros((1, HD), F32)
        dwa = jnp.zeros((1, HD), F32)
        for h in range(N_HEADS):
            sl = slice(h * HD, (h + 1) * HD)
            _, vjp = jax.vjp(_gated_norm, oa_ref[:, sl], z_ref[:, sl], wd_ref[...])
            doa, dz, dw1 = vjp(dm_ref[:, sl])
            doa_ref[:, sl] = doa
            dz_ref[:, sl] = dz.astype(BF16)
            dwd = dwd + dw1
            obh = ob_ref[:, sl]
            _, vjp2 = jax.vjp(_head_rms, obh, wa_ref[...])
            dob, dw2 = vjp2(dm_ref[:, GW + h * HD:GW + (h + 1) * HD])
            dwa = dwa + dw2
            dob_ref[:, sl] = dob
            dl_ref[:, sl] = jnp.broadcast_to(jnp.sum(dob * obh, axis=1, keepdims=True), (tm, HD))

        @pl.when(pl.program_id(0) == 0)
        def _():
            dwd_ref[...] = jnp.zeros_like(dwd_ref)
            dwa_ref[...] = jnp.zeros_like(dwa_ref)

        dwd_ref[...] += dwd
        dwa_ref[...] += dwa

    vec = pl.BlockSpec((1, HD), lambda i: (0, 0))
    wide = pl.BlockSpec((tm, GW), lambda i: (i, 0))
    return pl.pallas_call(
        body, grid=(t // tm,),
        in_specs=[pl.BlockSpec((tm, 2 * GW), lambda i: (i, 0)), wide, pl.BlockSpec((tm, GW), lambda i: (i, z_blk)),
                  wide, vec, vec, ANY],
        out_specs=[wide, pl.BlockSpec((tm, GW), lambda i: (i, z_blk)), wide, wide, vec, vec],
        out_shape=[_sds((t, GW), F32), _sds(proj.shape, BF16), _sds((t, GW), F32), _sds((t, GW), F32),
                   _sds((1, HD), F32), _sds((1, HD), F32)], name=name,
        compiler_params=_params(1))(dmixed, oa_raw, proj, ob, w_dn, w_an, dep)


def _halves(n):
    cut = (n // 256) * 128
    return [(0, cut), (cut, n)]


def _gate_up_swiglu(name, h2, w_gu_g):
    t, d = h2.shape
    n = w_gu_g.shape[2]
    per = N_DEV // 2
    tm = min(512, t)

    def body(a_ref, bg_ref, bu_ref, gu_ref, act_ref):
        a = a_ref[...]
        cuts = _halves(n)
        gs = [_dot(a, bg_ref[:, c0:c1], 1, 0) for c0, c1 in cuts]
        ups = [_dot(a, bu_ref[:, c0:c1], 1, 0) for c0, c1 in cuts]
        for (c0, c1), g, up in zip(cuts, gs, ups):
            gu_ref[0, :, c0:c1] = g.astype(BF16)
            gu_ref[1, :, c0:c1] = up.astype(BF16)
            act_ref[:, c0:c1] = (_silu(g) * up).astype(BF16)

    return pl.pallas_call(
        body, grid=(per, t // tm),
        in_specs=[pl.BlockSpec((tm, d), lambda j, i: (i, 0)), pl.BlockSpec((None, d, n), lambda j, i: (j, 0, 0)),
                  pl.BlockSpec((None, d, n), lambda j, i: (j + per, 0, 0))],
        out_specs=[pl.BlockSpec((2, tm, n), lambda j, i: (0, i, j)), pl.BlockSpec((tm, n), lambda j, i: (i, j))],
        out_shape=[_sds((2, t, per * n), BF16), _sds((t, per * n), BF16)], name=name,
        compiler_params=_params(2))(h2, w_gu_g, w_gu_g)


def _d_gate_up(name, dy16, w_down, gu3, dep):
    t, d = dy16.shape
    f = w_down.shape[0]
    tm, tn = min(1024, t), f // 4

    def body(a_ref, b_ref, g_ref, dep_ref, o_ref):
        a = a_ref[...]
        cuts = _halves(tn)
        dacts = [_dot(a, b_ref[c0:c1, :], 1, 1) for c0, c1 in cuts]
        for (c0, c1), dact in zip(cuts, dacts):
            g, up = g_ref[0, :, c0:c1].astype(F32), g_ref[1, :, c0:c1].astype(F32)
            sg = _sigmoid(g)
            o_ref[0, :, c0:c1] = (dact * up * sg * (1.0 + g * (1.0 - sg))).astype(BF16)
            o_ref[1, :, c0:c1] = (dact * g * sg).astype(BF16)

    return pl.pallas_call(
        body, grid=(f // tn, t // tm),
        in_specs=[pl.BlockSpec((tm, d), lambda j, i: (i, 0)), pl.BlockSpec((tn, d), lambda j, i: (j, 0)),
                  pl.BlockSpec((2, tm, tn), lambda j, i: (0, i, j)), ANY],
        out_specs=pl.BlockSpec((2, tm, tn), lambda j, i: (0, i, j)), out_shape=_sds((2, t, f), BF16), name=name,
        compiler_params=_params(2))(dy16, w_down, gu3, dep)


def _d_h2(name, dgu3, w_gu_g, dep):
    _, t, f = dgu3.shape
    n_dev, d, n = w_gu_g.shape
    per = n_dev // 2
    tm, tn = min(256, t), 512

    def body(g_ref, u_ref, b_ref, dep_ref, o_ref):
        acc = None
        for s in range(n_dev):
            a_ref = g_ref if s < per else u_ref
            part = _dot(a_ref[:, (s % per) * n:(s % per + 1) * n], b_ref[s], 1, 1)
            acc = part if acc is None else acc + part
        o_ref[...] = acc

    return pl.pallas_call(
        body, grid=(d // tn, t // tm),
        in_specs=[pl.BlockSpec((None, tm, f), lambda j, i: (0, i, 0)), pl.BlockSpec((None, tm, f), lambda j, i: (1, i, 0)),
                  pl.BlockSpec((n_dev, tn, n), lambda j, i: (0, j, 0)), ANY],
        out_specs=pl.BlockSpec((tm, tn), lambda j, i: (i, j)), out_shape=_sds((t, d), F32), name=name,
        compiler_params=_params(2))(dgu3, dgu3, w_gu_g, dep)


def _out_proj_norm(name, mixed, w_out, x, w_norm):
    t, d = x.shape
    kdim = mixed.shape[1]
    tm = min(512, t)

    def body(a_ref, b_ref, x_ref, w_ref, x1_ref, h_ref):
        x1 = x_ref[...] + _dot(a_ref[...], b_ref[...], 1, 0)
        x1_ref[...] = x1
        h_ref[...] = _rms_f(x1, w_ref[...]).astype(BF16)

    row = pl.BlockSpec((tm, d), lambda i: (i, 0))
    return pl.pallas_call(
        body, grid=(t // tm,),
        in_specs=[pl.BlockSpec((tm, kdim), lambda i: (i, 0)), pl.BlockSpec((kdim, d), lambda i: (0, 0)), row,
                  pl.BlockSpec((1, d), lambda i: (0, 0))],
        out_specs=[row, row], out_shape=[_sds((t, d), F32), _sds((t, d), BF16)], name=name,
        compiler_params=_params(1))(mixed, w_out, x, w_norm)


def _down_loss(name, act, w_down, x1, target):
    t, f = act.shape
    d = x1.shape[1]
    tm, tn = min(1024, t), 512

    def body(a_ref, b_ref, x_ref, t_ref, dy_ref, dy16_ref, l_ref):
        diff = _dot(a_ref[...], b_ref[...], 1, 0) + x_ref[...] - t_ref[...]
        dyv = diff * (1.0 / d)
        dy_ref[...] = dyv
        dy16_ref[...] = dyv.astype(BF16)
        tot = jnp.sum(jnp.sum(diff * diff, axis=1, keepdims=True), axis=0, keepdims=True) * (0.5 / d)

        @pl.when((pl.program_id(0) == 0) & (pl.program_id(1) == 0))
        def _():
            l_ref[...] = jnp.zeros_like(l_ref)

        l_ref[...] += jnp.broadcast_to(tot, (8, 128))

    tile = pl.BlockSpec((tm, tn), lambda i, j: (i, j))
    return pl.pallas_call(
        body, grid=(t // tm, d // tn),
        in_specs=[pl.BlockSpec((tm, f), lambda i, j: (i, 0)), pl.BlockSpec((f, tn), lambda i, j: (0, j)), tile, tile],
        out_specs=[tile, tile, pl.BlockSpec((8, 128), lambda i, j: (0, 0))],
        out_shape=[_sds((t, d), F32), _sds((t, d), BF16), _sds((8, 128), F32)], name=name,
        compiler_params=_params(2))(act, w_down, x1, target)


def _peer(me, k):
    pid = (me + k) % N_DEV
    return (pid // 4, (pid // 2) % 2, pid % 2)


def _my_id():
    return 4 * lax.axis_index("x") + 2 * lax.axis_index("y") + lax.axis_index("c")


def _exchange(name, arrays, scatter, dep):
    n = len(arrays)

    def body(*refs):
        ins, outs = refs[:n], refs[n + 1:2 * n + 1]
        send_sems, recv_sems, local_sems = refs[2 * n + 1:]
        me = _my_id()
        started = []
        for a in range(n):
            src = ins[a].at[me] if scatter[a] else ins[a]
            loc = pltpu.make_async_copy(src, outs[a].at[me], local_sems.at[a])
            loc.start()
            started.append(loc)
        remote = []
        for k in range(1, N_DEV):
            to = (me + k) % N_DEV
            for a in range(n):
                src = ins[a].at[to] if scatter[a] else ins[a]
                cp = pltpu.make_async_remote_copy(src_ref=src, dst_ref=outs[a].at[me],
                                                  send_sem=send_sems.at[a * (N_DEV - 1) + k - 1], recv_sem=recv_sems.at[a * (N_DEV - 1) + k - 1],
                                                  device_id=_peer(me, k), device_id_type=pl.DeviceIdType.MESH)
                cp.start()
                remote.append(cp)
        for k in range(1, N_DEV):
            frm = (me + N_DEV - k) % N_DEV
            for a in range(n):
                src = ins[a].at[frm] if scatter[a] else ins[a]
                pltpu.make_async_remote_copy(src_ref=src, dst_ref=outs[a].at[frm],
                                             send_sem=send_sems.at[a * (N_DEV - 1) + k - 1], recv_sem=recv_sems.at[a * (N_DEV - 1) + k - 1],
                                             device_id=_peer(me, k), device_id_type=pl.DeviceIdType.MESH).wait_recv()
        for cp in remote:
            cp.wait_send()
        for loc in started:
            loc.wait()

    out_shape = [_sds((N_DEV,) + (a.shape[1:] if sc else a.shape), a.dtype) for a, sc in zip(arrays, scatter)]
    return pl.pallas_call(
        body, in_specs=[ANY] * (n + 1), out_specs=[ANY] * n, out_shape=out_shape,
        scratch_shapes=[pltpu.SemaphoreType.DMA((n * (N_DEV - 1),)), pltpu.SemaphoreType.DMA((n * (N_DEV - 1),)),
                        pltpu.SemaphoreType.DMA((n,))],
        name=name)(*arrays, dep)


def _gather_two_level(name, arrays):
    n = len(arrays)
    per = N_DEV - 1

    def body(*refs):
        ins, outs = refs[:n], refs[n:2 * n]
        send_sems, recv_sems, local_sems = refs[2 * n:]
        x, y, c = lax.axis_index("x"), lax.axis_index("y"), lax.axis_index("c")
        me, sibling = (x, y, c), (x, y, 1 - c)
        chips = [(1 - x, y), (x, 1 - y), (1 - x, 1 - y)]

        def copy(a, k, block, to, src=None):
            slot = outs[a].at[4 * block[0] + 2 * block[1] + block[2]]
            return pltpu.make_async_remote_copy(
                src_ref=slot if src is None else src, dst_ref=slot, send_sem=send_sems.at[a * per + k],
                recv_sem=recv_sems.at[a * per + k], device_id=to, device_id_type=pl.DeviceIdType.MESH)

        mine = [pltpu.make_async_copy(ins[a], outs[a].at[4 * x + 2 * y + c], local_sems.at[a]) for a in range(n)]
        for cp in mine:
            cp.start()
        first = [copy(a, 0, me, sibling, src=ins[a]) for a in range(n)]
        first += [copy(a, 1 + j, me, (*chip, c), src=ins[a]) for j, chip in enumerate(chips) for a in range(n)]
        for cp in first:
            cp.start()
        passed = []
        for j, chip in enumerate(chips):
            for a in range(n):
                copy(a, 1 + j, (*chip, c), me).wait_recv()
                cp = copy(a, 4 + j, (*chip, c), sibling)
                cp.start()
                passed.append(cp)
        for a in range(n):
            copy(a, 0, sibling, me).wait_recv()
            for j, chip in enumerate(chips):
                copy(a, 4 + j, (*chip, 1 - c), me).wait_recv()
        for cp in first + passed:
            cp.wait_send()
        for cp in mine:
            cp.wait()

    return pl.pallas_call(
        body, in_specs=[ANY] * n, out_specs=[ANY] * n,
        out_shape=[_sds((N_DEV,) + a.shape, a.dtype) for a in arrays],
        scratch_shapes=[pltpu.SemaphoreType.DMA((n * per,)), pltpu.SemaphoreType.DMA((n * per,)),
                        pltpu.SemaphoreType.DMA((n,))],
        name=name)(*arrays)


HBM = pl.BlockSpec(memory_space=pltpu.HBM)
SEM = pl.BlockSpec(memory_space=pltpu.SEMAPHORE)
EFFECT = pltpu.SideEffectType.DATAFLOW_SIDE_EFFECTING


def _remote_copies(srcs, lands, scatter, send_sems, recv_sems, me, incoming):
    out = []
    for k in range(1, N_DEV):
        other = (me + N_DEV - k) % N_DEV if incoming else (me + k) % N_DEV
        for a in range(len(srcs)):
            sem = a * (N_DEV - 1) + k - 1
            src = srcs[a].at[other] if scatter[a] else srcs[a]
            dst = lands[a].at[other if incoming else me]
            out.append(pltpu.make_async_remote_copy(src_ref=src, dst_ref=dst, send_sem=send_sems.at[sem],
                                                    recv_sem=recv_sems.at[sem], device_id=_peer(me, k),
                                                    device_id_type=pl.DeviceIdType.MESH))
    return out


def _exchange_start(name, arrays, scatter, dep):
    n = len(arrays)
    lands = [lax.empty((N_DEV,) + (a.shape[1:] if sc else a.shape), a.dtype) for a, sc in zip(arrays, scatter)]

    def body(*refs):
        srcs, land_refs = refs[:n], refs[n:2 * n]
        send_sems, recv_sems = refs[2 * n + 1], refs[2 * n + 2]
        token = refs[-1]
        for cp in _remote_copies(srcs, land_refs, scatter, send_sems, recv_sems, _my_id(), False):
            cp.start()
        token[...] = jnp.zeros_like(token)

    n_sem = n * (N_DEV - 1)
    out_shape = ([pltpu.SemaphoreType.DMA((n_sem,)), pltpu.SemaphoreType.DMA((n_sem,))]
                 + [pltpu.HBM(a.shape, a.dtype) for a in arrays] + [pltpu.HBM(l.shape, l.dtype) for l in lands]
                 + [_sds((8, 128), F32)])
    aliases = {i: 2 + i for i in range(2 * n)}
    args = [pltpu.with_memory_space_constraint(a, pltpu.HBM) for a in list(arrays) + lands] + [dep]
    res = pl.pallas_call(
        body, name=name, in_specs=[HBM] * (2 * n) + [ANY], out_shape=out_shape,
        out_specs=[SEM, SEM] + [HBM] * (2 * n) + [pl.BlockSpec(memory_space=pltpu.VMEM)],
        input_output_aliases=aliases, compiler_params=pltpu.CompilerParams(has_side_effects=EFFECT))(*args)
    return dict(send=res[0], recv=res[1], srcs=res[2:2 + n], lands=res[2 + n:2 + 2 * n], token=res[-1],
                scatter=scatter)


def _exchange_wait(name, started, after):
    n = len(started["srcs"])
    scatter = started["scatter"]

    def body(*refs):
        srcs, land_refs = refs[:n], refs[n:2 * n]
        send_sems, recv_sems = refs[2 * n], refs[2 * n + 1]
        me = _my_id()
        for cp in _remote_copies(srcs, land_refs, scatter, send_sems, recv_sems, me, False):
            cp.wait_send()
        for cp in _remote_copies(srcs, land_refs, scatter, send_sems, recv_sems, me, True):
            cp.wait_recv()

    arrs = list(started["srcs"]) + list(started["lands"])
    res = pl.pallas_call(
        body, name=name, in_specs=[HBM] * (2 * n) + [SEM, SEM, ANY],
        out_shape=[pltpu.HBM(a.shape, a.dtype) for a in arrs], out_specs=[HBM] * (2 * n),
        input_output_aliases={i: i for i in range(2 * n)},
        compiler_params=pltpu.CompilerParams(has_side_effects=EFFECT))(*arrs, started["send"], started["recv"], after)
    me = _my_id()
    out = []
    for src, land, sc in zip(res[:n], res[n:], scatter):
        own = lax.dynamic_index_in_dim(src, me, 0, keepdims=True) if sc else src[None]
        out.append(lax.dynamic_update_slice(land, own, (me,) + (0,) * (land.ndim - 1)))
    return out


def _adamw(name, parts, w, m, v):
    r, c = w.shape
    tr, tc = r, c
    if r % 8 == 0:
        tr = next(cand for cand in (128, 88, 64, 40, 8) if r % cand == 0)
    else:
        tc = 256
    c1 = 1.0 / (1.0 - ADAM_B1 ** ADAM_STEP)
    c2 = 1.0 / (1.0 - ADAM_B2 ** ADAM_STEP)

    def body(p_ref, w_ref, m_ref, v_ref, g_ref, d_ref, nm_ref, nv_ref):
        g = p_ref[0].astype(F32)
        for s in range(1, N_DEV):
            g = g + p_ref[s].astype(F32)
        mn = ADAM_B1 * m_ref[...] + (1.0 - ADAM_B1) * g
        vn = ADAM_B2 * v_ref[...] + (1.0 - ADAM_B2) * (g * g)
        g_ref[...] = g
        nm_ref[...] = mn
        nv_ref[...] = vn
        d_ref[...] = -ADAM_LR * ((mn * c1) / (jnp.sqrt(vn * c2) + ADAM_EPS) + ADAM_WD * w_ref[...])

    blk = pl.BlockSpec((tr, tc), lambda i, j: (i, j))
    return pl.pallas_call(
        body, grid=(r // tr, c // tc),
        in_specs=[pl.BlockSpec((N_DEV, tr, tc), lambda i, j: (0, i, j)), blk, blk, blk],
        out_specs=[blk] * 4, out_shape=[_sds((r, c), F32)] * 4, name=name,
        compiler_params=_params(2, VMEM_LIMIT))(parts, w, m, v)


def _pad_rows(a, rows):
    return jnp.pad(a, ((0, rows - a.shape[0]), (0, 0)))


def _lane_row(vec8, offset):
    return jnp.pad(vec8.reshape(1, 8), ((0, 0), (offset, HD - 8 - offset)))


def kernel(x, positions, attn_norm_w, w_in, conv_w, a_log, dt_bias, delta_out_norm_w, q_norm_w, k_norm_w, attn_out_norm_w, w_out, ffn_norm_w, w_gate_up, w_down, loss_target, m_attn_norm_w, m_w_in, m_conv_w, m_a_log, m_dt_bias, m_delta_out_norm_w, m_q_norm_w, m_k_norm_w, m_attn_out_norm_w, m_w_out, m_ffn_norm_w, m_w_gate_up, m_w_down, v_attn_norm_w, v_w_in, v_conv_w, v_a_log, v_dt_bias, v_delta_out_norm_w, v_q_norm_w, v_k_norm_w, v_attn_out_norm_w, v_w_out, v_ffn_norm_w, v_w_gate_up, v_w_down):
    x2 = x[0]
    t, d = x2.shape
    target = loss_target[0]
    pos_col = positions.reshape(t, 1)
    half = HD // 2
    inv = (ROPE_THETA ** (-np.arange(half, dtype=np.float32) / half)).astype(np.float32)
    inv_row = jnp.asarray(np.concatenate([inv, inv]).reshape(1, HD))

    n_in = w_in.shape[2]
    n_gu = w_gate_up.shape[2]
    w_in_g, conv_g = _gather_two_level("gather_in", [w_in[0].astype(BF16), _pad_rows(conv_w[0], 8)])
    out_fly = _exchange_start("gather_out_start", [w_out[0].astype(BF16)], [False], conv_g)
    gu_fly = _exchange_start("gather_gate_up_start", [w_gate_up[0].astype(BF16)], [False], out_fly["token"])
    down_fly = _exchange_start("gather_down_start", [w_down[0].astype(BF16)], [False], gu_fly["token"])
    n_main = 4 * GW
    n_small = 2 * N_HEADS
    segments = [(0, n_main, 0), (n_main + n_small, N_DEV * n_in, n_main), (n_main, n_main + n_small, 7 * GW)]
    pieces = []
    for lo, hi, _ in segments:
        f = lo
        while f < hi:
            j = f // n_in
            end = min(hi, (j + 1) * n_in)
            pieces.append(w_in_g[j][:, f - j * n_in:end - j * n_in])
            f = end
    w_cat = jnp.concatenate(pieces + [jnp.zeros((d, HD - n_small), BF16)], axis=1)
    n_cat = w_cat.shape[1]
    small_blk = (7 * GW) // HD
    conv_w8 =jnp.transpose(conv_g, (1, 0, 2)).reshape(8, 3 * GW)
    alog_row = _lane_row(a_log[0], 8)
    dtb_row = _lane_row(dt_bias[0], 8)

    tm = min(2048, t)
    h1 = _rms_fwd("norm1", x2, attn_norm_w, down_fly["token"])
    tmp, tnp = min(1024, t), n_cat // 3
    proj = _mm("in_proj", h1, w_cat, grid=(t // tmp, n_cat // tnp, 1),
               a_spec=pl.BlockSpec((tmp, d), lambda i, j, k: (i, 0)),
               b_spec=pl.BlockSpec((d, tnp), lambda i, j, k: (0, j)),
               o_spec=pl.BlockSpec((tmp, tnp), lambda i, j, k: (i, j)),
               out_shape=_sds((t, n_cat), F32), ca=1, cb=0, nk=1)
    qn = _conv_fwd("conv_q", proj, conv_w8, 0, True, HD ** -0.5)
    kn = _conv_fwd("conv_k", proj, conv_w8, 1, True, 1.0)
    vv = _conv_fwd("conv_v", proj, conv_w8, 2, False, 1.0)
    beta_b, gc_b = _gates_fwd("gates", proj, small_blk, alog_row, dtb_row)
    u, w, p, tinv, qd, kd = _delta_prep("delta_prep", qn, kn, vv, beta_b, gc_b)
    oa_raw, vn, s_hist = _delta_scan("delta_scan", u, w, p, qd, kd, gc_b)

    cos_t, sin_t = _rope_tables("rope_tables", pos_col, inv_row)
    aq, ak = _qk_fwd("attn_qk", proj, 2, q_norm_w, k_norm_w, cos_t, sin_t)
    ob, lse = _attn_fwd("attn_fwd", aq, ak, proj, 6)
    mixed = _mix_fwd("mix", oa_raw, proj, 3, ob, delta_out_norm_w, attn_out_norm_w)
    (w_out_g,) = _exchange_wait("gather_out_wait", out_fly, mixed)
    w_out_full = w_out_g.reshape(2 * GW, d)
    tn = 512
    x1, h2 = _out_proj_norm("out_proj", mixed, w_out_full, x2, ffn_norm_w)
    per = N_DEV // 2
    (w_gu_g,) = _exchange_wait("gather_gate_up_wait", gu_fly, h2)
    gu3, act = _gate_up_swiglu("gate_up", h2, w_gu_g)
    (w_down_g,) = _exchange_wait("gather_down_wait", down_fly, act)
    w_down_full = w_down_g.reshape(D_FF, d)
    tmd = min(1024, t)
    dy, dy16, loss_tile = _down_loss("down_proj", act, w_down_full, x1, target)
    loss = lax.psum(loss_tile[0, 0], ("x", "y", "c"))

    tk, nkt = t, 1
    g_down = _mm("g_down", act, dy16, dep=loss.reshape(1, 1), grid=(D_FF // 1408, d // 512, nkt),
                 a_spec=pl.BlockSpec((tk, 1408), lambda i, j, k: (k, i)),
                 b_spec=pl.BlockSpec((tk, 512), lambda i, j, k: (k, j)),
                 o_spec=pl.BlockSpec((1408, 512), lambda i, j, k: (i, j)),
                 out_shape=_sds((D_FF, d), F32), ca=0, cb=0, nk=nkt)
    down_g_fly = _exchange_start("reduce_down_start", [g_down.reshape(N_DEV, D_FF // N_DEV, d)], [True], dy16)
    dgu3 = _d_gate_up("d_gate_up", dy16, w_down_full, gu3, down_g_fly["token"])
    g_gu = _mm("g_gate_up", h2, dgu3, grid=(d // 512, N_DEV, nkt),
               a_spec=pl.BlockSpec((tk, 512), lambda i, j, k: (k, i)),
               b_spec=pl.BlockSpec((None, tk, n_gu), lambda i, j, k: (j // per, k, j % per)),
               o_spec=pl.BlockSpec((None, 512, n_gu), lambda i, j, k: (j, i, 0)),
               out_shape=_sds((N_DEV, d, n_gu), F32), ca=0, cb=0, nk=nkt)
    gu_g_fly = _exchange_start("reduce_gate_up_start", [g_gu], [True], dy16)
    dh2 = _d_h2("d_h2", dgu3, w_gu_g, gu_g_fly["token"])
    dx1, dx1_16, g_ffn_norm = _rms_bwd("norm2_bwd", x1, ffn_norm_w, dh2, dy)

    g_out = _mm("g_out", mixed, dx1_16, grid=((2 * GW) // 512, 1, nkt),
                a_spec=pl.BlockSpec((tk, 512), lambda i, j, k: (k, i)),
                b_spec=pl.BlockSpec((tk, d), lambda i, j, k: (k, 0)),
                o_spec=pl.BlockSpec((512, d), lambda i, j, k: (i, 0)),
                out_shape=_sds((2 * GW, d), F32), ca=0, cb=0, nk=nkt)
    out_g_fly = _exchange_start("reduce_out_start", [g_out.reshape(N_DEV, (2 * GW) // N_DEV, d)], [True], g_ffn_norm)
    dmixed = _mm("d_mixed", dx1_16, w_out_full, dep=out_g_fly["token"], grid=(t // tm, (2 * GW) // tn, 1),
                 a_spec=pl.BlockSpec((tm, d), lambda i, j, k: (i, 0)),
                 b_spec=pl.BlockSpec((tn, d), lambda i, j, k: (j, 0)),
                 o_spec=pl.BlockSpec((tm, tn), lambda i, j, k: (i, j)),
                 out_shape=_sds((t, 2 * GW), F32), ca=1, cb=1, nk=1)
    doa, dproj, dob, delta, g_dn, g_an = _mix_bwd("mix_bwd", dmixed, oa_raw, proj, 3, ob,
                                                  delta_out_norm_w, attn_out_norm_w, out_g_fly["token"])
    d_aq, d_ak, d_av = _attn_bwd("attn_bwd", aq, ak, proj, 6, dob, lse, delta)
    dproj, g_qn, g_kn = _qk_bwd("attn_qk_bwd", proj, 2, q_norm_w, k_norm_w, cos_t, sin_t, d_aq, d_ak, dproj)
    dproj = _cast_into("attn_v_bwd", d_av, dproj, 6)

    dvn, dqd, dkd, dw, ddec = _delta_scan_bwd("delta_scan_bwd", doa, w, p, qd, kd, gc_b, vn, s_hist)
    dqn, dkn, dvv, dbeta_b, dg_b = _delta_prep_bwd("delta_prep_bwd", qn, kn, vv, beta_b, gc_b, tinv, u, w, vn,
                                                   doa, dvn, dqd, dkd, dw, ddec)
    dproj, gcw_q = _conv_bwd("conv_q_bwd", proj, conv_w8, dqn, dproj, 0, True, HD ** -0.5)
    dproj, gcw_k = _conv_bwd("conv_k_bwd", proj, conv_w8, dkn, dproj, 1, True, 1.0)
    dproj, gcw_v = _conv_bwd("conv_v_bwd", proj, conv_w8, dvv, dproj, 2, False, 1.0)
    dproj, g_alog_row, g_dtb_row = _gates_bwd("gates_bwd", proj, small_blk, alog_row, dtb_row, dbeta_b, dg_b, dproj)
    tmc = 384
    g_cat = _mm("g_in", dproj, h1, grid=(n_cat // tmc, 1, nkt),
                a_spec=pl.BlockSpec((tk, tmc), lambda i, j, k: (k, i)),
                b_spec=pl.BlockSpec((tk, d), lambda i, j, k: (k, 0)),
                o_spec=pl.BlockSpec((tmc, d), lambda i, j, k: (i, 0)),
                out_shape=_sds((n_cat, d), F32), ca=0, cb=0, nk=nkt)
    parts = []
    for j in range(N_DEV):
        cols = []
        for lo, hi, start in sorted(segments):
            a, b = max(lo, j * n_in), min(hi, (j + 1) * n_in)
            if a < b:
                cols.append(g_cat[start + a - lo:start + b - lo])
        parts.append(cols[0] if len(cols) == 1 else jnp.concatenate(cols, axis=0))
    g_in_parts = jnp.stack(parts).astype(BF16)
    g_conv = jnp.concatenate([gcw_q, gcw_k, gcw_v], axis=1)
    n_cw = conv_w.shape[2]
    g_conv_parts = jnp.transpose(g_conv.reshape(8, N_DEV, n_cw), (1, 0, 2))
    in_g_fly = _exchange_start("reduce_in_start", [g_in_parts, g_conv_parts], [True] * 2, g_dtb_row)
    tmh1 = min(512, t)
    dh1 = _mm("d_h1", dproj, w_cat, dep=in_g_fly["token"], grid=(t // tmh1, d // 1024, 1),
              a_spec=pl.BlockSpec((tmh1, n_cat), lambda i, j, k: (i, 0)),
              b_spec=pl.BlockSpec((1024, n_cat), lambda i, j, k: (j, 0)),
              o_spec=pl.BlockSpec((tmh1, 1024), lambda i, j, k: (i, j)),
              out_shape=_sds((t, d), F32), ca=1, cb=1, nk=1)
    grad_x, _, g_attn_norm = _rms_bwd("norm1_bwd", x2, attn_norm_w, dh1, dx1)

    small_rows = [g_attn_norm.reshape(d // HD, HD), g_ffn_norm.reshape(d // HD, HD), g_dn, g_qn, g_kn, g_an,
                  g_alog_row, g_dtb_row]
    small_pack = _pad_rows(jnp.concatenate(small_rows, axis=0), 40)
    (r_down,) = _exchange_wait("reduce_down_wait", down_g_fly, grad_x)
    (r_gu,) = _exchange_wait("reduce_gate_up_wait", gu_g_fly, grad_x)
    (r_out,) = _exchange_wait("reduce_out_wait", out_g_fly, grad_x)
    res_gu = [a[None] for a in _adamw("adamw_gate_up", r_gu, w_gate_up[0], m_w_gate_up[0], v_w_gate_up[0])]
    res_down = [a[None] for a in _adamw("adamw_down", r_down, w_down[0], m_w_down[0], v_w_down[0])]
    res_out = [a[None] for a in _adamw("adamw_out", r_out, w_out[0], m_w_out[0], v_w_out[0])]
    done = (res_gu[3][0, :1, :1] + res_down[3][0, :1, :1] + res_out[3][0, :1, :1])
    (r_small,) = _exchange("gather_small_grads", [small_pack], [False], done)

    def pack_small(an, fn, dn, qn_, kn_, aon, al, db):
        rows = [an.reshape(d // HD, HD), fn.reshape(d // HD, HD), dn, qn_, kn_, aon,
                _lane_row(al[0], 8), _lane_row(db[0], 8)]
        return _pad_rows(jnp.concatenate(rows, axis=0), 40)

    def unpack_small(pk):
        nr = d // HD
        return dict(attn_norm_w=pk[:nr].reshape(1, d), ffn_norm_w=pk[nr:2 * nr].reshape(1, d),
                    delta_out_norm_w=pk[2 * nr:2 * nr + 1], q_norm_w=pk[2 * nr + 1:2 * nr + 2],
                    k_norm_w=pk[2 * nr + 2:2 * nr + 3], attn_out_norm_w=pk[2 * nr + 3:2 * nr + 4],
                    a_log=pk[2 * nr + 4:2 * nr + 5, 8:16], dt_bias=pk[2 * nr + 5:2 * nr + 6, 8:16])

    res_small = _adamw("adamw_small", r_small,
                       pack_small(attn_norm_w, ffn_norm_w, delta_out_norm_w, q_norm_w, k_norm_w, attn_out_norm_w, a_log, dt_bias),
                       pack_small(m_attn_norm_w, m_ffn_norm_w, m_delta_out_norm_w, m_q_norm_w, m_k_norm_w, m_attn_out_norm_w, m_a_log, m_dt_bias),
                       pack_small(v_attn_norm_w, v_ffn_norm_w, v_delta_out_norm_w, v_q_norm_w, v_k_norm_w, v_attn_out_norm_w, v_a_log, v_dt_bias))
    small = [unpack_small(a) for a in res_small]
    r_in, r_conv = _exchange_wait("reduce_in_wait", in_g_fly, res_small[0])
    res_in = [jnp.transpose(a)[None] for a in _adamw("adamw_in", r_in, jnp.transpose(w_in[0]), jnp.transpose(m_w_in[0]),
                                                     jnp.transpose(v_w_in[0]))]
    res_conv =[a[None, :4] for a in _adamw("adamw_conv", r_conv, _pad_rows(conv_w[0], 8), _pad_rows(m_conv_w[0], 8),
                                            _pad_rows(v_conv_w[0], 8))]

    outs = [loss, grad_x[None]]
    for i in range(4):
        s = small[i]
        outs += [s["attn_norm_w"], res_in[i], res_conv[i], s["a_log"], s["dt_bias"], s["delta_out_norm_w"],
                 s["q_norm_w"], s["k_norm_w"], s["attn_out_norm_w"], res_out[i], s["ffn_norm_w"], res_gu[i],
                 res_down[i]]
    return tuple(outs)
```

```python
import functools

import numpy as np
import jax
import jax.numpy as jnp
from jax import lax
from jax.experimental import pallas as pl
from jax.experimental.pallas import tpu as pltpu

F32 = jnp.float32
BF16 = jnp.bfloat16

N_DEV = 8
N_HEADS = 8
HD = 128
GW = N_HEADS * HD
CHUNK = 64
PAIR = 2 * CHUNK
SPAN = 128
DILATIONS = (1, 4, 16)
ROPE_THETA = 10000.0
EPS = 1e-6
D_FF = 5632
ADAM_LR, ADAM_B1, ADAM_B2, ADAM_EPS, ADAM_WD, ADAM_STEP = 0.001, 0.9, 0.999, 1e-8, 0.01, 10
NEG = -1e30
VMEM_LIMIT = 56 * 1024 * 1024
ANY = pl.BlockSpec(memory_space=pl.ANY)
HEADS_PER_STEP = 8


def _params(n_grid, vmem=VMEM_LIMIT):
    return pltpu.CompilerParams(dimension_semantics=("arbitrary",) * n_grid, vmem_limit_bytes=vmem)


def _sds(shape, dtype):
    return jax.ShapeDtypeStruct(tuple(shape), dtype)


def _sigmoid(x):
    return 1.0 / (1.0 + jnp.exp(-x))


def _silu(x):
    return x * _sigmoid(x)


def _softplus(x):
    return jnp.maximum(x, 0.0) + jnp.log(1.0 + jnp.exp(-jnp.abs(x)))


def _dot(a, b, ca, cb, precision=None):
    return lax.dot_general(a, b, (((ca,), (cb,)), ((), ())), precision=precision,
                           preferred_element_type=F32)


def _b16(x):
    return x if x.dtype == BF16 else x.astype(BF16)


def _split(x):
    hi = x.astype(BF16)
    return hi, (x - hi.astype(F32)).astype(BF16)


def _dot3(a, b, ca, cb):
    a_hi, a_lo = _split(a)
    b_hi, b_lo = _split(b)
    return _dot(a_hi, b_hi, ca, cb) + (_dot(a_hi, b_lo, ca, cb) + _dot(a_lo, b_hi, ca, cb))


def _iota2(shape, axis):
    return lax.broadcasted_iota(jnp.int32, shape, axis)


def _mm(name, a, b, *, grid, a_spec, b_spec, o_spec, out_shape, ca, cb, nk, add=None, add_spec=None,
        dep=None, vmem=VMEM_LIMIT):
    has_add = add is not None
    n_in = 2 + has_add + (dep is not None)

    def body(*refs):
        a_ref, b_ref = refs[0], refs[1]
        e_ref = refs[2] if has_add else None
        o_ref = refs[n_in]
        part = _dot(_b16(a_ref[...]), _b16(b_ref[...]), ca, cb)
        if nk == 1:
            if has_add:
                part = part + e_ref[...]
            o_ref[...] = part.astype(o_ref.dtype)
            return
        acc = refs[-1]
        k = pl.program_id(2)

        @pl.when(k == 0)
        def _():
            acc[...] = part

        @pl.when(k > 0)
        def _():
            acc[...] += part

        @pl.when(k == nk - 1)
        def _():
            res = acc[...]
            if has_add:
                res = res + e_ref[...]
            o_ref[...] = res.astype(o_ref.dtype)

    in_specs = [a_spec, b_spec] + ([add_spec] if has_add else []) + ([ANY] if dep is not None else [])
    args = (a, b) + ((add,) if has_add else ()) + ((dep,) if dep is not None else ())
    blk = [d for d in o_spec.block_shape if d is not None]
    scratch = [pltpu.VMEM(tuple(blk), F32)] if nk > 1 else []
    return pl.pallas_call(body, grid=grid, in_specs=in_specs, out_specs=o_spec, out_shape=out_shape,
                          scratch_shapes=scratch, name=name, compiler_params=_params(3, vmem))(*args)


def _rms_f(xv, wv):
    return xv * lax.rsqrt(jnp.mean(xv * xv, axis=-1, keepdims=True) + EPS) * wv


def _rms_fwd(name, x, w, dep):
    t, d = x.shape
    tm = min(512, t)

    def body(x_ref, w_ref, dep_ref, o_ref):
        o_ref[...] = _rms_f(x_ref[...], w_ref[...]).astype(BF16)

    row = pl.BlockSpec((tm, d), lambda i: (i, 0))
    vec = pl.BlockSpec((1, d), lambda i: (0, 0))
    return pl.pallas_call(body, grid=(t // tm,), in_specs=[row, vec, ANY], out_specs=row,
                          out_shape=_sds((t, d), BF16), name=name, compiler_params=_params(1))(x, w, dep)


def _rms_bwd(name, x, w, dh, res):
    t, d = x.shape
    tm = min(256, t)

    def body(x_ref, w_ref, dh_ref, res_ref, dx_ref, dx16_ref, dw_ref):
        _, vjp = jax.vjp(_rms_f, x_ref[...], w_ref[...])
        dxv, dwv = vjp(dh_ref[...])
        dxv = dxv + res_ref[...]
        dx_ref[...] = dxv
        dx16_ref[...] = dxv.astype(BF16)

        @pl.when(pl.program_id(0) == 0)
        def _():
            dw_ref[...] = jnp.zeros_like(dw_ref)

        dw_ref[...] += dwv

    row = pl.BlockSpec((tm, d), lambda i: (i, 0))
    vec = pl.BlockSpec((1, d), lambda i: (0, 0))
    return pl.pallas_call(body, grid=(t // tm,), in_specs=[row, vec, row, row], out_specs=[row, row, vec],
                          out_shape=[_sds((t, d), F32), _sds((t, d), BF16), _sds((1, d), F32)], name=name,
                          compiler_params=_params(1))(x, w, dh, res)


def _shift_rows(x, s):
    t = x.shape[0]
    r = pltpu.roll(x, s % t, 0)
    row8 = _iota2((8, x.shape[1]), 0)
    if s > 0:
        return jnp.concatenate([jnp.where(row8 >= s, r[:8], 0.0), r[8:]], axis=0)
    return jnp.concatenate([r[:t - 8], jnp.where(row8 < 8 + s, r[t - 8:], 0.0)], axis=0)


def _conv_taps(xv, w_ref):
    c = w_ref[3:4, :] * xv
    for s in (1, 2, 3):
        c = c + w_ref[3 - s:4 - s, :] * _shift_rows(xv, s)
    return c


def _post_conv(c, l2, scale):
    y = _silu(c)
    if l2:
        y = y * lax.rsqrt(jnp.sum(y * y, axis=-1, keepdims=True) + EPS) * scale
    return y


def _conv_fwd(name, proj, conv_w8, group, l2, scale):
    t = proj.shape[0]

    def body(x_ref, w_ref, o_ref):
        o_ref[...] = _post_conv(_conv_taps(x_ref[...], w_ref), l2, scale)

    return pl.pallas_call(
        body, grid=(N_HEADS,),
        in_specs=[pl.BlockSpec((t, HD), lambda h: (0, h + group * N_HEADS)),
                  pl.BlockSpec((8, HD), lambda h: (0, h + group * N_HEADS))],
        out_specs=pl.BlockSpec((t, HD), lambda h: (0, h)),
        out_shape=_sds((t, GW), F32), name=name, compiler_params=_params(1, VMEM_LIMIT))(proj, conv_w8)


def _conv_bwd(name, proj, conv_w8, dn, dproj, group, l2, scale):
    t = proj.shape[0]

    def body(x_ref, w_ref, dn_ref, dproj_ref, dx_ref, dw_ref):
        xv = x_ref[...]
        c = _conv_taps(xv, w_ref)
        _, vjp = jax.vjp(lambda cc: _post_conv(cc, l2, scale), c)
        (dc,) = vjp(dn_ref[...])
        dx = w_ref[3:4, :] * dc
        dw = jnp.zeros((8, HD), F32)
        rid = _iota2((8, HD), 0)
        dw = dw + jnp.where(rid == 3, jnp.sum(dc * xv, axis=0, keepdims=True), 0.0)
        for s in (1, 2, 3):
            dx = dx + w_ref[3 - s:4 - s, :] * _shift_rows(dc, -s)
            dw = dw + jnp.where(rid == 3 - s, jnp.sum(dc * _shift_rows(xv, s), axis=0, keepdims=True), 0.0)
        dx_ref[...] = dx.astype(BF16)
        dw_ref[...] = dw

    return pl.pallas_call(
        body, grid=(N_HEADS,),
        in_specs=[pl.BlockSpec((t, HD), lambda h: (0, h + group * N_HEADS)),
                  pl.BlockSpec((8, HD), lambda h: (0, h + group * N_HEADS)),
                  pl.BlockSpec((t, HD), lambda h: (0, h)), ANY],
        out_specs=[pl.BlockSpec((t, HD), lambda h: (0, h + group * N_HEADS)), pl.BlockSpec((8, HD), lambda h: (0, h))],
        out_shape=[_sds(dproj.shape, BF16), _sds((8, GW), F32)], input_output_aliases={3: 0}, name=name,
        compiler_params=_params(1, VMEM_LIMIT))(proj, conv_w8, dn, dproj)


def _chunk_cumsum(g, rows):
    pos = rows % CHUNK
    s = 1
    while s < CHUNK:
        g = g + jnp.where(pos >= s, pltpu.roll(g, s, 0), 0.0)
        s *= 2
    return g


def _gates_fwd(name, proj, small_blk, alog_row, dtb_row):
    t = proj.shape[0]
    tm = min(256, t)

    def body(s_ref, a_ref, b_ref, beta_ref, gc_ref):
        sm = s_ref[...]
        beta = _sigmoid(sm)
        g = -jnp.exp(a_ref[...]) * _softplus(sm + b_ref[...])
        gc = _chunk_cumsum(g, _iota2((tm, HD), 0))
        lane = _iota2((tm, HD), 1)
        for h in range(N_HEADS):
            bcol = jnp.sum(jnp.where(lane == h, beta, 0.0), axis=1, keepdims=True)
            gcol = jnp.sum(jnp.where(lane == 8 + h, gc, 0.0), axis=1, keepdims=True)
            beta_ref[:, h * HD:(h + 1) * HD] = jnp.broadcast_to(bcol, (tm, HD))
            gc_ref[:, h * HD:(h + 1) * HD] = jnp.broadcast_to(gcol, (tm, HD))

    vec = pl.BlockSpec((1, HD), lambda i: (0, 0))
    wide = pl.BlockSpec((tm, GW), lambda i: (i, 0))
    return pl.pallas_call(
        body, grid=(t // tm,),
        in_specs=[pl.BlockSpec((tm, HD), lambda i: (i, small_blk)), vec, vec], out_specs=[wide, wide],
        out_shape=[_sds((t, GW), F32), _sds((t, GW), F32)], name=name,
        compiler_params=_params(1))(proj, alog_row, dtb_row)


def _gates_bwd(name, proj, small_blk, alog_row, dtb_row, dbeta_b, dg_b, dproj):
    t = proj.shape[0]
    tm = min(256, t)

    def body(s_ref, a_ref, b_ref, db_ref, dg_ref, dproj_ref, ds_ref, da_ref, dbias_ref):
        sm = s_ref[...]
        lane = _iota2((tm, HD), 1)
        db = jnp.zeros((tm, HD), F32)
        dg = jnp.zeros((tm, HD), F32)
        for h in range(N_HEADS):
            db = db + jnp.where(lane == h, db_ref[:, h * HD:(h + 1) * HD], 0.0)
            dg = dg + jnp.where(lane == 8 + h, dg_ref[:, h * HD:(h + 1) * HD], 0.0)
        beta = _sigmoid(sm)
        ea = jnp.exp(a_ref[...])
        pre = sm + b_ref[...]
        g = -ea * _softplus(pre)
        dpre = dg * (-ea) * _sigmoid(pre)
        ds_ref[...] = (db * beta * (1.0 - beta) + dpre).astype(BF16)

        @pl.when(pl.program_id(0) == 0)
        def _():
            da_ref[...] = jnp.zeros_like(da_ref)
            dbias_ref[...] = jnp.zeros_like(dbias_ref)

        da_ref[...] += jnp.sum(dg * g, axis=0, keepdims=True)
        dbias_ref[...] += jnp.sum(dpre, axis=0, keepdims=True)

    vec = pl.BlockSpec((1, HD), lambda i: (0, 0))
    wide = pl.BlockSpec((tm, GW), lambda i: (i, 0))
    return pl.pallas_call(
        body, grid=(t // tm,),
        in_specs=[pl.BlockSpec((tm, HD), lambda i: (i, small_blk)), vec, vec, wide, wide, ANY],
        out_specs=[pl.BlockSpec((tm, HD), lambda i: (i, small_blk)), vec, vec],
        out_shape=[_sds(dproj.shape, BF16), _sds((1, HD), F32), _sds((1, HD), F32)],
        input_output_aliases={5: 0}, name=name,
        compiler_params=_params(1))(proj, alog_row, dtb_row, dbeta_b, dg_b, dproj)


def _pair_masks():
    ii = _iota2((PAIR, PAIR), 0)
    jj = _iota2((PAIR, PAIR), 1)
    same = (ii // CHUNK) == (jj // CHUNK)
    return ii, jj, same & (ii >= jj), same & (ii > jj)


def _to_row(col_b, ii, jj):
    return jnp.sum(jnp.where(ii == jj, col_b, 0.0), axis=0, keepdims=True)


def _to_col(row, ii, jj):
    return jnp.sum(jnp.where(ii == jj, jnp.broadcast_to(row, (PAIR, PAIR)), 0.0), axis=1, keepdims=True)


def _decay_parts(gc, last_a, last_b, ii, jj, causal):
    diff = gc - _to_row(gc, ii, jj)
    dmat = jnp.where(causal, jnp.exp(jnp.where(causal, diff, 0.0)), 0.0)
    glast = jnp.where(ii < CHUNK, last_a, last_b)
    return dmat, jnp.exp(gc), jnp.exp(glast - gc)


def _unit_lower_inverse(lows, ii, jj):
    eye = jnp.where(ii == jj, 1.0, 0.0)
    mm = lambda xs, ys: [_dot3(a, b, 1, 0) for a, b in zip(xs, ys)]
    plus = lambda xs: [eye + a for a in xs]
    minus = lambda xs: [eye - a for a in xs]
    d1 = [jnp.where((ii // 16) == (jj // 16), low, 0.0) for low in lows]
    d2 = mm(d1, d1)
    a = mm(minus(d1), plus(d2))
    d4 = mm(d2, d2)
    a = mm(a, plus(d4))
    d8 = mm(d4, d4)
    td = mm(a, plus(d8))
    n1 = mm(td, [low - d for low, d in zip(lows, d1)])
    n2 = mm(n1, n1)
    return mm(mm(minus(n1), plus(n2)), td)


def _delta_prep(name, qn, kn, vv, beta_b, gc_b):
    t = qn.shape[0]

    def body(q_ref, k_ref, v_ref, b_ref, g_ref, u_ref, w_ref, p_ref, t_ref, qd_ref, kd_ref):
        ii, jj, causal, strict = _pair_masks()
        sls = [slice(hh * HD, (hh + 1) * HD) for hh in range(HEADS_PER_STEP)]
        lows = []
        for sl in sls:
            q, k, beta = q_ref[:, sl], k_ref[:, sl], b_ref[:, sl]
            dmat, gam, e2 = _decay_parts(g_ref[:, sl], g_ref[CHUNK - 1:CHUNK, sl], g_ref[PAIR - 1:PAIR, sl],
                                         ii, jj, causal)
            k16 = _b16(k)
            lows.append(jnp.where(strict, beta * _dot(k16, k16, 1, 1) * dmat, 0.0))
            p_ref[:, sl] = jnp.where(causal, _dot(_b16(q), k16, 1, 1) * dmat, 0.0).astype(BF16)
            qd_ref[:, sl] = (q * gam).astype(BF16)
            kd_ref[:, sl] = (k * e2).astype(BF16)
        for sl, tinv in zip(sls, _unit_lower_inverse(lows, ii, jj)):
            beta = b_ref[:, sl]
            t_ref[:, sl] = tinv
            u_ref[:, sl] = _dot3(tinv, v_ref[:, sl] * beta, 1, 0)
            w_ref[:, sl] = _dot3(tinv, k_ref[:, sl] * (beta * jnp.exp(g_ref[:, sl])), 1, 0).astype(BF16)

    blk = pl.BlockSpec((PAIR, HEADS_PER_STEP * HD), lambda i, h: (i, h))
    return pl.pallas_call(
        body, grid=(t // PAIR, N_HEADS // HEADS_PER_STEP), in_specs=[blk] * 5, out_specs=[blk] * 6,
        out_shape=[_sds((t, GW), F32), _sds((t, GW), BF16), _sds((t, GW), BF16), _sds((t, GW), F32),
                   _sds((t, GW), BF16), _sds((t, GW), BF16)],
        name=name, compiler_params=_params(2))(qn, kn, vv, beta_b, gc_b)


def _delta_scan(name, u, w, p, qd, kd, gc_b):
    t = u.shape[0]
    n = t // CHUNK

    def body(u_ref, w_ref, p_ref, qd_ref, kd_ref, g_ref, o_ref, vn_ref, sh_ref, state):
        @pl.when(pl.program_id(0) == 0)
        def _():
            state[...] = jnp.zeros_like(state)

        sls = [slice(h * HD, (h + 1) * HD) for h in range(N_HEADS)]
        heads = range(N_HEADS)
        s = [state[h] for h in heads]
        for c in range(PAIR // CHUNK):
            rows = slice(c * CHUNK, (c + 1) * CHUNK)
            last = slice((c + 1) * CHUNK - 1, (c + 1) * CHUNK)
            for h in heads:
                sh_ref[c, h] = s[h]
            s16 = [_b16(a) for a in s]
            ws = [_dot(w_ref[rows, sls[h]], s16[h], 1, 0) for h in heads]
            qs = [_dot(qd_ref[rows, sls[h]], s16[h], 1, 0) for h in heads]
            vn16 = [_b16(u_ref[rows, sls[h]] - ws[h]) for h in heads]
            pv = [_dot(p_ref[rows, sls[h]], jnp.concatenate([vn16[h], vn16[h]], axis=0), 1, 0) for h in heads]
            kv = [_dot(kd_ref[rows, sls[h]], vn16[h], 0, 0) for h in heads]
            for h in heads:
                o_ref[rows, sls[h]] = qs[h] + pv[h]
                vn_ref[rows, sls[h]] = vn16[h]
            s = [s[h] * jnp.exp(g_ref[last, sls[h]]) + kv[h] for h in heads]
        for h in heads:
            state[h] = s[h]

    blk = pl.BlockSpec((PAIR, GW), lambda i: (i, 0))
    return pl.pallas_call(
        body, grid=(t // PAIR,), in_specs=[blk] * 6,
        out_specs=[blk, blk, pl.BlockSpec((PAIR // CHUNK, N_HEADS, HD, HD), lambda i: (i, 0, 0, 0))],
        out_shape=[_sds((t, GW), F32), _sds((t, GW), BF16), _sds((n, N_HEADS, HD, HD), F32)],
        scratch_shapes=[pltpu.VMEM((N_HEADS, HD, HD), F32)], name=name,
        compiler_params=_params(1))(u, w, p, qd, kd, gc_b)


def _delta_scan_bwd(name, do, w, p, qd, kd, gc_b, vn, s_hist):
    t = do.shape[0]
    n = t // CHUNK

    def body(do_ref, w_ref, p_ref, qd_ref, kd_ref, g_ref, vn_ref, sh_ref,
             dvn_ref, dqd_ref, dkd_ref, dw_ref, ddec_ref, dstate):
        @pl.when(pl.program_id(0) == 0)
        def _():
            dstate[...] = jnp.zeros_like(dstate)

        sls = [slice(h * HD, (h + 1) * HD) for h in range(N_HEADS)]
        heads = range(N_HEADS)
        ds = [dstate[h] for h in heads]
        for c in reversed(range(PAIR // CHUNK)):
            rows = slice(c * CHUNK, (c + 1) * CHUNK)
            last = slice((c + 1) * CHUNK - 1, (c + 1) * CHUNK)
            ds16 = [_b16(a) for a in ds]
            s16 = [_b16(sh_ref[c, h]) for h in heads]
            do16 = [_b16(do_ref[rows, sls[h]]) for h in heads]
            ptdo = [_dot(p_ref[rows, sls[h]], do16[h], 0, 0) for h in heads]
            kds = [_dot(kd_ref[rows, sls[h]], ds16[h], 1, 0) for h in heads]
            qdo = [_dot(qd_ref[rows, sls[h]], do16[h], 0, 0) for h in heads]
            for h in heads:
                dqd_ref[rows, sls[h]] = _dot(do16[h], s16[h], 1, 1)
                dkd_ref[rows, sls[h]] = _dot(vn_ref[rows, sls[h]], ds16[h], 1, 1)
            dvn = [ptdo[h][:CHUNK, :] + ptdo[h][CHUNK:, :] + kds[h] for h in heads]
            dvn16 = [_b16(a) for a in dvn]
            wdv = [_dot(w_ref[rows, sls[h]], dvn16[h], 0, 0) for h in heads]
            for h in heads:
                dvn_ref[rows, sls[h]] = dvn[h]
                dw_ref[rows, sls[h]] = -_dot(dvn16[h], s16[h], 1, 1)
                tot = jnp.sum(jnp.sum(sh_ref[c, h] * ds[h], axis=1, keepdims=True), axis=0, keepdims=True)
                ddec_ref[c * 8:(c + 1) * 8, sls[h]] = jnp.broadcast_to(tot, (8, HD))
            ds = [ds[h] * jnp.exp(g_ref[last, sls[h]]) + qdo[h] - wdv[h] for h in heads]
        for h in heads:
            dstate[h] = ds[h]

    npair = t // PAIR
    blk = pl.BlockSpec((PAIR, GW), lambda i: (npair - 1 - i, 0))
    return pl.pallas_call(
        body, grid=(npair,),
        in_specs=[blk] * 7 + [pl.BlockSpec((PAIR // CHUNK, N_HEADS, HD, HD), lambda i: (npair - 1 - i, 0, 0, 0))],
        out_specs=[blk] * 4 + [pl.BlockSpec((16, GW), lambda i: (npair - 1 - i, 0))],
        out_shape=[_sds((t, GW), F32)] * 4 + [_sds((n * 8, GW), F32)],
        scratch_shapes=[pltpu.VMEM((N_HEADS, HD, HD), F32)], name=name,
        compiler_params=_params(1))(do, w, p, qd, kd, gc_b, vn, s_hist)


def _delta_prep_bwd(name, qn, kn, vv, beta_b, gc_b, tinv, u, w, vn, do, dvn, dqd, dkd, dw, ddec):
    t = qn.shape[0]

    def body(q_ref, k_ref, v_ref, b_ref, g_ref, t_ref, u_ref, w_ref, vn_ref, do_ref, dvn_ref, dqd_ref,
             dkd_ref, dw_ref, ddec_ref, dq_ref, dk_ref, dv_ref, dbeta_ref, dg_ref):
        ii, jj, causal, strict = _pair_masks()
        suffix = ((ii // CHUNK) == (jj // CHUNK)) & (jj >= ii)
        first = ii < CHUNK
        rs = lambda a: jnp.sum(a, axis=1, keepdims=True)
        sls = [slice(hh * HD, (hh + 1) * HD) for hh in range(HEADS_PER_STEP)]
        xs = [_dot3(t_ref[:, sl], dvn_ref[:, sl], 0, 0) for sl in sls]
        ys = [_dot3(t_ref[:, sl], dw_ref[:, sl], 0, 0) for sl in sls]
        k16s = [_b16(k_ref[:, sl]) for sl in sls]
        kks = [_dot(k16, k16, 1, 1) for k16 in k16s]
        qks = [_dot(_b16(q_ref[:, sl]), k16, 1, 1) for sl, k16 in zip(sls, k16s)]
        dps = [jnp.where(causal, _dot(_b16(do_ref[:, sl]), vn_ref[:, sl], 1, 1), 0.0) for sl in sls]
        das = [-jnp.where(strict, _dot(_b16(x), _b16(u_ref[:, sl]), 1, 1) + _dot(_b16(y), w_ref[:, sl], 1, 1), 0.0)
               for sl, x, y in zip(sls, xs, ys)]
        for hh, sl in enumerate(sls):
            q, k, v, beta, gc = q_ref[:, sl], k_ref[:, sl], v_ref[:, sl], b_ref[:, sl], g_ref[:, sl]
            last_a, last_b = g_ref[CHUNK - 1:CHUNK, sl], g_ref[PAIR - 1:PAIR, sl]
            dmat, gam, e2 = _decay_parts(gc, last_a, last_b, ii, jj, causal)
            q16, k16 = _b16(q), k16s[hh]
            kk, qk, dp, x, y, da = kks[hh], qks[hh], dps[hh], xs[hh], ys[hh], das[hh]
            dqd, dkd = dqd_ref[:, sl], dkd_ref[:, sl]
            dpd16 = _b16(dp * dmat)
            dkk16 = _b16(da * beta * dmat)
            dq_ref[:, sl] = gam * dqd + _dot(dpd16, k16, 1, 0)
            dk_ref[:, sl] = (e2 * dkd + _dot(dpd16, q16, 0, 0) + beta * gam * y
                             + _dot(dkk16, k16, 1, 0) + _dot(dkk16, k16, 0, 0))
            dv_ref[:, sl] = beta * x
            dbeta = rs(v * x) + rs(k * gam * y) + rs(da * kk * dmat)
            dbeta_ref[:, sl] = jnp.broadcast_to(dbeta, (PAIR, HD))
            m = (dp * qk + da * beta * kk) * dmat
            dgam = rs(q * dqd) + rs(k * beta * y)
            de2 = rs(k * dkd)
            colsum = _to_col(jnp.sum(m, axis=0, keepdims=True), ii, jj)
            te2 = de2 * e2
            dgc = rs(m) - colsum + gam * dgam - te2
            tail_a = jnp.sum(jnp.where(first, te2, 0.0), axis=0, keepdims=True)
            tail_b = jnp.sum(jnp.where(first, 0.0, te2), axis=0, keepdims=True)
            dgc = dgc + jnp.where(ii == CHUNK - 1, tail_a + ddec_ref[0:1, sl] * jnp.exp(last_a), 0.0)
            dgc = dgc + jnp.where(ii == PAIR - 1, tail_b + ddec_ref[8:9, sl] * jnp.exp(last_b), 0.0)
            dgc_row = _to_row(dgc, ii, jj)
            dg = jnp.sum(jnp.where(suffix, jnp.broadcast_to(dgc_row, (PAIR, PAIR)), 0.0), axis=1, keepdims=True)
            dg_ref[:, sl] = jnp.broadcast_to(dg, (PAIR, HD))

    blk = pl.BlockSpec((PAIR, HEADS_PER_STEP * HD), lambda i, h: (i, h))
    return pl.pallas_call(
        body, grid=(t // PAIR, N_HEADS // HEADS_PER_STEP),
        in_specs=[blk] * 14 + [pl.BlockSpec((16, HEADS_PER_STEP * HD), lambda i, h: (i, h))], out_specs=[blk] * 5,
        out_shape=[_sds((t, GW), F32)] * 5, name=name,
        compiler_params=_params(2))(qn, kn, vv, beta_b, gc_b, tinv, u, w, vn, do, dvn, dqd, dkd, dw, ddec)


def _rope_tables(name, pos_col, inv_row):
    t = pos_col.shape[0]
    tm = min(1024, t)

    def body(pos_ref, inv_ref, cos_ref, sin_ref):
        ang = pos_ref[...].astype(F32) * inv_ref[...]
        lane = _iota2(ang.shape, 1)
        cos_ref[...] = jnp.cos(ang)
        sin_ref[...] = jnp.where(lane < HD // 2, -1.0, 1.0) * jnp.sin(ang)

    tab = pl.BlockSpec((tm, HD), lambda i: (i, 0))
    return pl.pallas_call(
        body, grid=(t // tm,), in_specs=[pl.BlockSpec((tm, 1), lambda i: (i, 0)), pl.BlockSpec((1, HD), lambda i: (0, 0))],
        out_specs=[tab, tab], out_shape=[_sds((t, HD), F32)] * 2, name=name,
        compiler_params=_params(1))(pos_col, inv_row)


def _head_rms(xh, wv):
    return xh * lax.rsqrt(jnp.mean(xh * xh, axis=-1, keepdims=True) + EPS) * wv


def _qk_fwd(name, proj, pair_blk, wq_row, wk_row, cos_t, sin_t):
    t = proj.shape[0]
    tm = min(256, t)

    def body(x_ref, wq_ref, wk_ref, cos_ref, sin_ref, q_ref, k_ref):
        cos, sin = cos_ref[...], sin_ref[...]
        for o_ref, w_ref, base in ((q_ref, wq_ref, 0), (k_ref, wk_ref, GW)):
            for h in range(N_HEADS):
                y = _head_rms(x_ref[:, base + h * HD:base + (h + 1) * HD], w_ref[...])
                o_ref[:, h * HD:(h + 1) * HD] = y * cos + pltpu.roll(y, HD // 2, 1) * sin

    vec = pl.BlockSpec((1, HD), lambda i: (0, 0))
    tab = pl.BlockSpec((tm, HD), lambda i: (i, 0))
    wide = pl.BlockSpec((tm, GW), lambda i: (i, 0))
    return pl.pallas_call(
        body, grid=(t // tm,),
        in_specs=[pl.BlockSpec((tm, 2 * GW), lambda i: (i, pair_blk)), vec, vec, tab, tab],
        out_specs=[wide, wide], out_shape=[_sds((t, GW), F32)] * 2, name=name,
        compiler_params=_params(1))(proj, wq_row, wk_row, cos_t, sin_t)


def _qk_bwd(name, proj, pair_blk, wq_row, wk_row, cos_t, sin_t, dq_full, dk_full, dproj):
    t = proj.shape[0]
    tm = min(256, t)

    def body(x_ref, wq_ref, wk_ref, cos_ref, sin_ref, dq_ref, dk_ref, dproj_ref, dx_ref, dwq_ref, dwk_ref):
        cos, sin = cos_ref[...], sin_ref[...]

        @pl.when(pl.program_id(0) == 0)
        def _():
            dwq_ref[...] = jnp.zeros_like(dwq_ref)
            dwk_ref[...] = jnp.zeros_like(dwk_ref)

        for dy_ref, w_ref, dw_ref, base in ((dq_ref, wq_ref, dwq_ref, 0), (dk_ref, wk_ref, dwk_ref, GW)):
            dw = jnp.zeros((1, HD), F32)
            for h in range(N_HEADS):
                dy = dy_ref[:, h * HD:(h + 1) * HD]
                dy = dy * cos - pltpu.roll(dy, HD // 2, 1) * sin
                _, vjp = jax.vjp(_head_rms, x_ref[:, base + h * HD:base + (h + 1) * HD], w_ref[...])
                dx, dwh = vjp(dy)
                dw = dw + dwh
                dx_ref[:, base + h * HD:base + (h + 1) * HD] = dx.astype(BF16)
            dw_ref[...] += dw

    vec = pl.BlockSpec((1, HD), lambda i: (0, 0))
    tab = pl.BlockSpec((tm, HD), lambda i: (i, 0))
    wide = pl.BlockSpec((tm, GW), lambda i: (i, 0))
    pair = pl.BlockSpec((tm, 2 * GW), lambda i: (i, pair_blk))
    return pl.pallas_call(
        body, grid=(t // tm,), in_specs=[pair, vec, vec, tab, tab, wide, wide, ANY],
        out_specs=[pair, vec, vec],
        out_shape=[_sds(dproj.shape, BF16), _sds((1, HD), F32), _sds((1, HD), F32)], input_output_aliases={7: 0},
        name=name, compiler_params=_params(1))(proj, wq_row, wk_row, cos_t, sin_t, dq_full, dk_full, dproj)


def _cast_into(name, x, dproj, blk_idx):
    t = x.shape[0]
    tm = min(512, t)

    def body(x_ref, dproj_ref, o_ref):
        o_ref[...] = x_ref[...].astype(BF16)

    return pl.pallas_call(
        body, grid=(t // tm,), in_specs=[pl.BlockSpec((tm, GW), lambda i: (i, 0)), ANY],
        out_specs=pl.BlockSpec((tm, GW), lambda i: (i, blk_idx)), out_shape=_sds(dproj.shape, BF16),
        input_output_aliases={1: 0}, name=name, compiler_params=_params(1))(x, dproj)


GROUP = SPAN * max(DILATIONS)
SCALE = HD ** -0.5
TILE_BATCH = 8


def _band_mask(lo):
    qi = _iota2((SPAN, 2 * SPAN), 0)
    ki = _iota2((SPAN, 2 * SPAN), 1)
    return (ki >= qi) & (ki <= qi + SPAN) & (ki >= lo)


def _tiles():
    return [(pi, r, u, rho) for pi, r in enumerate(DILATIONS) for rho in range(r) for u in range(GROUP // (SPAN * r))]


def _rows(r, u, rho):
    return pl.ds(u * SPAN * r + rho, SPAN, stride=r) if r > 1 else pl.ds(u * SPAN, SPAN)


def _attn_fwd(name, q, k, v, v_blk):
    t = q.shape[0]

    def body(qc_ref, kc_ref, vc_ref, kp_ref, vp_ref, ob_ref, lse_ref, o_scr, l_scr):
        mask_in = _band_mask(0)
        mask_edge = _band_mask(jnp.where(pl.program_id(0) == 0, SPAN, 0))
        tiles = _tiles()
        k_own = v_own = None
        for b0 in range(0, len(tiles), TILE_BATCH):
            work = []
            for pi, r, u, rho in tiles[b0:b0 + TILE_BATCH]:
                rows = _rows(r, u, rho)
                if u > 0:
                    k_prev, v_prev, mask = k_own, v_own, mask_in
                else:
                    prows = _rows(r, GROUP // (SPAN * r) - 1, rho)
                    k_prev, v_prev, mask = kp_ref[prows, :].astype(BF16), vp_ref[prows, :].astype(BF16), mask_edge
                k_own, v_own = kc_ref[rows, :].astype(BF16), vc_ref[rows, :].astype(BF16)
                work.append((pi, rows, mask, qc_ref[rows, :].astype(BF16), jnp.concatenate([k_prev, k_own], axis=0),
                             jnp.concatenate([v_prev, v_own], axis=0)))
            scores = [_dot(qt, kcat, 1, 1) for _, _, _, qt, kcat, _ in work]
            soft = []
            for (_, _, mask, _, _, _), s in zip(work, scores):
                s = jnp.where(mask, s * SCALE, NEG)
                m = jnp.max(s, axis=1, keepdims=True)
                p = jnp.exp(s - m)
                soft.append((m, _b16(p), jnp.sum(p, axis=1, keepdims=True)))
            outs = [_dot(p, vcat, 1, 0) for (_, p, _), (_, _, _, _, _, vcat) in zip(soft, work)]
            for (pi, rows, _, _, _, _), (m, _, den), o in zip(work, soft, outs):
                o_scr[pi, rows, :] = o / den
                l_scr[pi, rows, :] = jnp.broadcast_to(m + jnp.log(den), (SPAN, HD))
        step = 256
        for c in range(GROUP // step):
            sl = pl.ds(c * step, step)
            ob, lse = _merge([o_scr[i, sl, :] for i in range(3)], [l_scr[i, sl, :] for i in range(3)])
            ob_ref[sl, :] = ob
            lse_ref[sl, :] = lse

    cur = pl.BlockSpec((GROUP, HD), lambda g, h: (g, h))
    prev = pl.BlockSpec((GROUP, HD), lambda g, h: (jnp.maximum(g - 1, 0), h))
    vcur = pl.BlockSpec((GROUP, HD), lambda g, h: (g, v_blk * N_HEADS + h))
    vprev = pl.BlockSpec((GROUP, HD), lambda g, h: (jnp.maximum(g - 1, 0), v_blk * N_HEADS + h))
    return pl.pallas_call(
        body, grid=(t // GROUP, N_HEADS), in_specs=[cur, cur, vcur, prev, vprev], out_specs=[cur, cur],
        out_shape=[_sds((t, GW), F32), _sds((t, GW), F32)],
        scratch_shapes=[pltpu.VMEM((3, GROUP, HD), F32), pltpu.VMEM((3, GROUP, HD), F32)], name=name,
        compiler_params=_params(2))(q, k, v, k, v)


def _attn_bwd(name, q, k, v, v_blk, do, lse, delta):
    t = q.shape[0]
    ng = t // GROUP

    def probs(work):
        scores = [_dot(qt, kcat, 1, 1) for qt, _, _, _, kcat, _, _ in work]
        dps = [_dot(dot, vcat, 1, 1) for _, dot, _, _, _, vcat, _ in work]
        out = []
        for (_, _, lt, dlt, kcat, _, mask), s, dp in zip(work, scores, dps):
            wide = kcat.shape[0] // SPAN
            lw = jnp.concatenate([lt] * wide, axis=1) if wide > 1 else lt
            dw = jnp.concatenate([dlt] * wide, axis=1) if wide > 1 else dlt
            p = jnp.exp(jnp.where(mask, s * SCALE - lw, NEG))
            out.append((_b16(p * (dp - dw) * SCALE), _b16(p)))
        return out

    def body(qc_ref, kc_ref, vc_ref, doc_ref, lc_ref, dc_ref, kp_ref, vp_ref, qn_ref, don_ref, ln_ref, dn_ref,
             dq_ref, dk_ref, dv_ref):
        g = pl.program_id(0)
        mask_in = _band_mask(0)
        mask_edge = _band_mask(jnp.where(g == 0, SPAN, 0))
        dk_ref[...] = jnp.zeros_like(dk_ref)
        dv_ref[...] = jnp.zeros_like(dv_ref)
        tiles = _tiles()
        k_own = v_own = None
        for b0 in range(0, len(tiles), TILE_BATCH):
            where, work = [], []
            for pi, r, u, rho in tiles[b0:b0 + TILE_BATCH]:
                rows = _rows(r, u, rho)
                if u > 0:
                    prows, k_prev, v_prev, mask = _rows(r, u - 1, rho), k_own, v_own, mask_in
                else:
                    prows = _rows(r, GROUP // (SPAN * r) - 1, rho)
                    k_prev, v_prev, mask = kp_ref[prows, :].astype(BF16), vp_ref[prows, :].astype(BF16), mask_edge
                k_own, v_own = kc_ref[rows, :].astype(BF16), vc_ref[rows, :].astype(BF16)
                where.append((pi, u, rows, prows))
                work.append((qc_ref[rows, :].astype(BF16), doc_ref[rows, :].astype(BF16), lc_ref[rows, :], dc_ref[rows, :],
                             jnp.concatenate([k_prev, k_own], axis=0), jnp.concatenate([v_prev, v_own], axis=0), mask))
            dsp = probs(work)
            dqs = [_dot(ds, w[4], 1, 0) for (ds, _), w in zip(dsp, work)]
            dks = [_dot(ds, w[0], 0, 0) for (ds, _), w in zip(dsp, work)]
            dvs = [_dot(p, w[1], 0, 0) for (_, p), w in zip(dsp, work)]
            for (pi, u, rows, prows), dq_t, dk2, dv2 in zip(where, dqs, dks, dvs):
                if pi == 0:
                    dq_ref[rows, :] = dq_t
                else:
                    dq_ref[rows, :] += dq_t
                dk_ref[rows, :] += dk2[SPAN:, :]
                dv_ref[rows, :] += dv2[SPAN:, :]
                if u > 0:
                    dk_ref[prows, :] += dk2[:SPAN, :]
                    dv_ref[prows, :] += dv2[:SPAN, :]
        qi = _iota2((SPAN, SPAN), 0)
        ki = _iota2((SPAN, SPAN), 1)
        mask_next = (ki >= qi) & (ki < jnp.where(g == ng - 1, 0, SPAN))
        edge = [(r, rho) for r in DILATIONS for rho in range(r)]
        for b0 in range(0, len(edge), TILE_BATCH):
            where, work = [], []
            for r, rho in edge[b0:b0 + TILE_BATCH]:
                krows, qrows = _rows(r, GROUP // (SPAN * r) - 1, rho), _rows(r, 0, rho)
                where.append(krows)
                work.append((qn_ref[qrows, :].astype(BF16), don_ref[qrows, :].astype(BF16), ln_ref[qrows, :],
                             dn_ref[qrows, :], kc_ref[krows, :].astype(BF16), vc_ref[krows, :].astype(BF16), mask_next))
            dsp = probs(work)
            dks = [_dot(ds, w[0], 0, 0) for (ds, _), w in zip(dsp, work)]
            dvs = [_dot(p, w[1], 0, 0) for (_, p), w in zip(dsp, work)]
            for krows, dk1, dv1 in zip(where, dks, dvs):
                dk_ref[krows, :] += dk1
                dv_ref[krows, :] += dv1

    cur = pl.BlockSpec((GROUP, HD), lambda g, h: (g, h))
    prev = pl.BlockSpec((GROUP, HD), lambda g, h: (jnp.maximum(g - 1, 0), h))
    nxt = pl.BlockSpec((GROUP, HD), lambda g, h: (jnp.minimum(g + 1, ng - 1), h))
    vcur = pl.BlockSpec((GROUP, HD), lambda g, h: (g, v_blk * N_HEADS + h))
    vprev = pl.BlockSpec((GROUP, HD), lambda g, h: (jnp.maximum(g - 1, 0), v_blk * N_HEADS + h))
    return pl.pallas_call(
        body, grid=(ng, N_HEADS), in_specs=[cur, cur, vcur, cur, cur, cur, prev, vprev] + [nxt] * 4,
        out_specs=[cur] * 3,
        out_shape=[_sds((t, GW), F32)] * 3, name=name,
        compiler_params=_params(2))(q, k, v, do, lse, delta, k, v, q, do, lse, delta)


def _merge(os_, ls_):
    m = jnp.maximum(jnp.maximum(ls_[0], ls_[1]), ls_[2])
    ws = [jnp.exp(l - m) for l in ls_]
    tot = ws[0] + ws[1] + ws[2]
    ob = (ws[0] * os_[0] + ws[1] * os_[1] + ws[2] * os_[2]) / tot
    return ob, m + jnp.log(tot)


def _gated_norm(oa, z, wv):
    return _head_rms(oa, wv) * _silu(z)


def _mix_fwd(name, oa_raw, proj, z_blk, ob, w_dn, w_an):
    t = oa_raw.shape[0]
    tm = min(256, t)

    def body(oa_ref, z_ref, ob_ref, wd_ref, wa_ref, mix_ref):
        for h in range(N_HEADS):
            sl = slice(h * HD, (h + 1) * HD)
            mix_ref[:, sl] = _gated_norm(oa_ref[:, sl], z_ref[:, sl], wd_ref[...]).astype(BF16)
            mix_ref[:, GW + h * HD:GW + (h + 1) * HD] = _head_rms(ob_ref[:, sl], wa_ref[...]).astype(BF16)

    vec = pl.BlockSpec((1, HD), lambda i: (0, 0))
    wide = pl.BlockSpec((tm, GW), lambda i: (i, 0))
    return pl.pallas_call(
        body, grid=(t // tm,),
        in_specs=[wide, pl.BlockSpec((tm, GW), lambda i: (i, z_blk)), wide, vec, vec],
        out_specs=pl.BlockSpec((tm, 2 * GW), lambda i: (i, 0)),
        out_shape=_sds((t, 2 * GW), BF16), name=name,
        compiler_params=_params(1))(oa_raw, proj, ob, w_dn, w_an)


def _mix_bwd(name, dmixed, oa_raw, proj, z_blk, ob, w_dn, w_an, dep):
    t = oa_raw.shape[0]
    tm = min(256, t)

    def body(dm_ref, oa_ref, z_ref, ob_ref, wd_ref, wa_ref, dep_ref,
             doa_ref, dz_ref, dob_ref, dl_ref, dwd_ref, dwa_ref):
        dwd = jnp.zeros((1, HD), F32)
        dwa = jnp.zeros((1, HD), F32)
        for h in range(N_HEADS):
            sl = slice(h * HD, (h + 1) * HD)
            _, vjp = jax.vjp(_gated_norm, oa_ref[:, sl], z_ref[:, sl], wd_ref[...])
            doa, dz, dw1 = vjp(dm_ref[:, sl])
            doa_ref[:, sl] = doa
            dz_ref[:, sl] = dz.astype(BF16)
            dwd = dwd + dw1
            obh = ob_ref[:, sl]
            _, vjp2 = jax.vjp(_head_rms, obh, wa_ref[...])
            dob, dw2 = vjp2(dm_ref[:, GW + h * HD:GW + (h + 1) * HD])
            dwa = dwa + dw2
            dob_ref[:, sl] = dob
            dl_ref[:, sl] = jnp.broadcast_to(jnp.sum(dob * obh, axis=1, keepdims=True), (tm, HD))

        @pl.when(pl.program_id(0) == 0)
        def _():
            dwd_ref[...] = jnp.zeros_like(dwd_ref)
            dwa_ref[...] = jnp.zeros_like(dwa_ref)

        dwd_ref[...] += dwd
        dwa_ref[...] += dwa

    vec = pl.BlockSpec((1, HD), lambda i: (0, 0))
    wide = pl.BlockSpec((tm, GW), lambda i: (i, 0))
    return pl.pallas_call(
        body, grid=(t // tm,),
        in_specs=[pl.BlockSpec((tm, 2 * GW), lambda i: (i, 0)), wide, pl.BlockSpec((tm, GW), lambda i: (i, z_blk)),
                  wide, vec, vec, ANY],
        out_specs=[wide, pl.BlockSpec((tm, GW), lambda i: (i, z_blk)), wide, wide, vec, vec],
        out_shape=[_sds((t, GW), F32), _sds(proj.shape, BF16), _sds((t, GW), F32), _sds((t, GW), F32),
                   _sds((1, HD), F32), _sds((1, HD), F32)], name=name,
        compiler_params=_params(1))(dmixed, oa_raw, proj, ob, w_dn, w_an, dep)


def _halves(n):
    cut = (n // 256) * 128
    return [(0, cut), (cut, n)]


def _gate_up_swiglu(name, h2, w_gu_g):
    t, d = h2.shape
    n = w_gu_g.shape[2]
    per = N_DEV // 2
    tm = min(512, t)

    def body(a_ref, bg_ref, bu_ref, gu_ref, act_ref):
        a = a_ref[...]
        cuts = _halves(n)
        gs = [_dot(a, bg_ref[:, c0:c1], 1, 0) for c0, c1 in cuts]
        ups = [_dot(a, bu_ref[:, c0:c1], 1, 0) for c0, c1 in cuts]
        for (c0, c1), g, up in zip(cuts, gs, ups):
            gu_ref[0, :, c0:c1] = g.astype(BF16)
            gu_ref[1, :, c0:c1] = up.astype(BF16)
            act_ref[:, c0:c1] = (_silu(g) * up).astype(BF16)

    return pl.pallas_call(
        body, grid=(per, t // tm),
        in_specs=[pl.BlockSpec((tm, d), lambda j, i: (i, 0)), pl.BlockSpec((None, d, n), lambda j, i: (j, 0, 0)),
                  pl.BlockSpec((None, d, n), lambda j, i: (j + per, 0, 0))],
        out_specs=[pl.BlockSpec((2, tm, n), lambda j, i: (0, i, j)), pl.BlockSpec((tm, n), lambda j, i: (i, j))],
        out_shape=[_sds((2, t, per * n), BF16), _sds((t, per * n), BF16)], name=name,
        compiler_params=_params(2))(h2, w_gu_g, w_gu_g)


def _d_gate_up(name, dy16, w_down, gu3, dep):
    t, d = dy16.shape
    f = w_down.shape[0]
    tm, tn = min(1024, t), f // 4

    def body(a_ref, b_ref, g_ref, dep_ref, o_ref):
        a = a_ref[...]
        cuts = _halves(tn)
        dacts = [_dot(a, b_ref[c0:c1, :], 1, 1) for c0, c1 in cuts]
        for (c0, c1), dact in zip(cuts, dacts):
            g, up = g_ref[0, :, c0:c1].astype(F32), g_ref[1, :, c0:c1].astype(F32)
            sg = _sigmoid(g)
            o_ref[0, :, c0:c1] = (dact * up * sg * (1.0 + g * (1.0 - sg))).astype(BF16)
            o_ref[1, :, c0:c1] = (dact * g * sg).astype(BF16)

    return pl.pallas_call(
        body, grid=(f // tn, t // tm),
        in_specs=[pl.BlockSpec((tm, d), lambda j, i: (i, 0)), pl.BlockSpec((tn, d), lambda j, i: (j, 0)),
                  pl.BlockSpec((2, tm, tn), lambda j, i: (0, i, j)), ANY],
        out_specs=pl.BlockSpec((2, tm, tn), lambda j, i: (0, i, j)), out_shape=_sds((2, t, f), BF16), name=name,
        compiler_params=_params(2))(dy16, w_down, gu3, dep)


def _d_h2(name, dgu3, w_gu_g, dep):
    _, t, f = dgu3.shape
    n_dev, d, n = w_gu_g.shape
    per = n_dev // 2
    tm, tn = min(256, t), 512

    def body(g_ref, u_ref, b_ref, dep_ref, o_ref):
        acc = None
        for s in range(n_dev):
            a_ref = g_ref if s < per else u_ref
            part = _dot(a_ref[:, (s % per) * n:(s % per + 1) * n], b_ref[s], 1, 1)
            acc = part if acc is None else acc + part
        o_ref[...] = acc

    return pl.pallas_call(
        body, grid=(d // tn, t // tm),
        in_specs=[pl.BlockSpec((None, tm, f), lambda j, i: (0, i, 0)), pl.BlockSpec((None, tm, f), lambda j, i: (1, i, 0)),
                  pl.BlockSpec((n_dev, tn, n), lambda j, i: (0, j, 0)), ANY],
        out_specs=pl.BlockSpec((tm, tn), lambda j, i: (i, j)), out_shape=_sds((t, d), F32), name=name,
        compiler_params=_params(2))(dgu3, dgu3, w_gu_g, dep)


def _out_proj_norm(name, mixed, w_out, x, w_norm):
    t, d = x.shape
    kdim = mixed.shape[1]
    tm = min(512, t)

    def body(a_ref, b_ref, x_ref, w_ref, x1_ref, h_ref):
        x1 = x_ref[...] + _dot(a_ref[...], b_ref[...], 1, 0)
        x1_ref[...] = x1
        h_ref[...] = _rms_f(x1, w_ref[...]).astype(BF16)

    row = pl.BlockSpec((tm, d), lambda i: (i, 0))
    return pl.pallas_call(
        body, grid=(t // tm,),
        in_specs=[pl.BlockSpec((tm, kdim), lambda i: (i, 0)), pl.BlockSpec((kdim, d), lambda i: (0, 0)), row,
                  pl.BlockSpec((1, d), lambda i: (0, 0))],
        out_specs=[row, row], out_shape=[_sds((t, d), F32), _sds((t, d), BF16)], name=name,
        compiler_params=_params(1))(mixed, w_out, x, w_norm)


def _down_loss(name, act, w_down, x1, target):
    t, f = act.shape
    d = x1.shape[1]
    tm, tn = min(1024, t), 512

    def body(a_ref, b_ref, x_ref, t_ref, dy_ref, dy16_ref, l_ref):
        diff = _dot(a_ref[...], b_ref[...], 1, 0) + x_ref[...] - t_ref[...]
        dyv = diff * (1.0 / d)
        dy_ref[...] = dyv
        dy16_ref[...] = dyv.astype(BF16)
        tot = jnp.sum(jnp.sum(diff * diff, axis=1, keepdims=True), axis=0, keepdims=True) * (0.5 / d)

        @pl.when((pl.program_id(0) == 0) & (pl.program_id(1) == 0))
        def _():
            l_ref[...] = jnp.zeros_like(l_ref)

        l_ref[...] += jnp.broadcast_to(tot, (8, 128))

    tile = pl.BlockSpec((tm, tn), lambda i, j: (i, j))
    return pl.pallas_call(
        body, grid=(t // tm, d // tn),
        in_specs=[pl.BlockSpec((tm, f), lambda i, j: (i, 0)), pl.BlockSpec((f, tn), lambda i, j: (0, j)), tile, tile],
        out_specs=[tile, tile, pl.BlockSpec((8, 128), lambda i, j: (0, 0))],
        out_shape=[_sds((t, d), F32), _sds((t, d), BF16), _sds((8, 128), F32)], name=name,
        compiler_params=_params(2))(act, w_down, x1, target)


def _peer(me, k):
    pid = (me + k) % N_DEV
    return (pid // 4, (pid // 2) % 2, pid % 2)


def _my_id():
    return 4 * lax.axis_index("x") + 2 * lax.axis_index("y") + lax.axis_index("c")


def _exchange(name, arrays, scatter, dep):
    n = len(arrays)

    def body(*refs):
        ins, outs = refs[:n], refs[n + 1:2 * n + 1]
        send_sems, recv_sems, local_sems = refs[2 * n + 1:]
        me = _my_id()
        started = []
        for a in range(n):
            src = ins[a].at[me] if scatter[a] else ins[a]
            loc = pltpu.make_async_copy(src, outs[a].at[me], local_sems.at[a])
            loc.start()
            started.append(loc)
        remote = []
        for k in range(1, N_DEV):
            to = (me + k) % N_DEV
            for a in range(n):
                src = ins[a].at[to] if scatter[a] else ins[a]
                cp = pltpu.make_async_remote_copy(src_ref=src, dst_ref=outs[a].at[me],
                                                  send_sem=send_sems.at[a * (N_DEV - 1) + k - 1], recv_sem=recv_sems.at[a * (N_DEV - 1) + k - 1],
                                                  device_id=_peer(me, k), device_id_type=pl.DeviceIdType.MESH)
                cp.start()
                remote.append(cp)
        for k in range(1, N_DEV):
            frm = (me + N_DEV - k) % N_DEV
            for a in range(n):
                src = ins[a].at[frm] if scatter[a] else ins[a]
                pltpu.make_async_remote_copy(src_ref=src, dst_ref=outs[a].at[frm],
                                             send_sem=send_sems.at[a * (N_DEV - 1) + k - 1], recv_sem=recv_sems.at[a * (N_DEV - 1) + k - 1],
                                             device_id=_peer(me, k), device_id_type=pl.DeviceIdType.MESH).wait_recv()
        for cp in remote:
            cp.wait_send()
        for loc in started:
            loc.wait()

    out_shape = [_sds((N_DEV,) + (a.shape[1:] if sc else a.shape), a.dtype) for a, sc in zip(arrays, scatter)]
    return pl.pallas_call(
        body, in_specs=[ANY] * (n + 1), out_specs=[ANY] * n, out_shape=out_shape,
        scratch_shapes=[pltpu.SemaphoreType.DMA((n * (N_DEV - 1),)), pltpu.SemaphoreType.DMA((n * (N_DEV - 1),)),
                        pltpu.SemaphoreType.DMA((n,))],
        name=name)(*arrays, dep)


def _gather_two_level(name, arrays):
    n = len(arrays)
    per = N_DEV - 1

    def body(*refs):
        ins, outs = refs[:n], refs[n:2 * n]
        send_sems, recv_sems, local_sems = refs[2 * n:]
        x, y, c = lax.axis_index("x"), lax.axis_index("y"), lax.axis_index("c")
        me, sibling = (x, y, c), (x, y, 1 - c)
        chips = [(1 - x, y), (x, 1 - y), (1 - x, 1 - y)]

        def copy(a, k, block, to, src=None):
            slot = outs[a].at[4 * block[0] + 2 * block[1] + block[2]]
            return pltpu.make_async_remote_copy(
                src_ref=slot if src is None else src, dst_ref=slot, send_sem=send_sems.at[a * per + k],
                recv_sem=recv_sems.at[a * per + k], device_id=to, device_id_type=pl.DeviceIdType.MESH)

        mine = [pltpu.make_async_copy(ins[a], outs[a].at[4 * x + 2 * y + c], local_sems.at[a]) for a in range(n)]
        for cp in mine:
            cp.start()
        first = [copy(a, 0, me, sibling, src=ins[a]) for a in range(n)]
        first += [copy(a, 1 + j, me, (*chip, c), src=ins[a]) for j, chip in enumerate(chips) for a in range(n)]
        for cp in first:
            cp.start()
        passed = []
        for j, chip in enumerate(chips):
            for a in range(n):
                copy(a, 1 + j, (*chip, c), me).wait_recv()
                cp = copy(a, 4 + j, (*chip, c), sibling)
                cp.start()
                passed.append(cp)
        for a in range(n):
            copy(a, 0, sibling, me).wait_recv()
            for j, chip in enumerate(chips):
                copy(a, 4 + j, (*chip, 1 - c), me).wait_recv()
        for cp in first + passed:
            cp.wait_send()
        for cp in mine:
            cp.wait()

    return pl.pallas_call(
        body, in_specs=[ANY] * n, out_specs=[ANY] * n,
        out_shape=[_sds((N_DEV,) + a.shape, a.dtype) for a in arrays],
        scratch_shapes=[pltpu.SemaphoreType.DMA((n * per,)), pltpu.SemaphoreType.DMA((n * per,)),
                        pltpu.SemaphoreType.DMA((n,))],
        name=name)(*arrays)


HBM = pl.BlockSpec(memory_space=pltpu.HBM)
SEM = pl.BlockSpec(memory_space=pltpu.SEMAPHORE)
EFFECT = pltpu.SideEffectType.DATAFLOW_SIDE_EFFECTING


def _remote_copies(srcs, lands, scatter, send_sems, recv_sems, me, incoming):
    out = []
    for k in range(1, N_DEV):
        other = (me + N_DEV - k) % N_DEV if incoming else (me + k) % N_DEV
        for a in range(len(srcs)):
            sem = a * (N_DEV - 1) + k - 1
            src = srcs[a].at[other] if scatter[a] else srcs[a]
            dst = lands[a].at[other if incoming else me]
            out.append(pltpu.make_async_remote_copy(src_ref=src, dst_ref=dst, send_sem=send_sems.at[sem],
                                                    recv_sem=recv_sems.at[sem], device_id=_peer(me, k),
                                                    device_id_type=pl.DeviceIdType.MESH))
    return out


def _exchange_start(name, arrays, scatter, dep):
    n = len(arrays)
    lands = [lax.empty((N_DEV,) + (a.shape[1:] if sc else a.shape), a.dtype) for a, sc in zip(arrays, scatter)]

    def body(*refs):
        srcs, land_refs = refs[:n], refs[n:2 * n]
        send_sems, recv_sems = refs[2 * n + 1], refs[2 * n + 2]
        token = refs[-1]
        for cp in _remote_copies(srcs, land_refs, scatter, send_sems, recv_sems, _my_id(), False):
            cp.start()
        token[...] = jnp.zeros_like(token)

    n_sem = n * (N_DEV - 1)
    out_shape = ([pltpu.SemaphoreType.DMA((n_sem,)), pltpu.SemaphoreType.DMA((n_sem,))]
                 + [pltpu.HBM(a.shape, a.dtype) for a in arrays] + [pltpu.HBM(l.shape, l.dtype) for l in lands]
                 + [_sds((8, 128), F32)])
    aliases = {i: 2 + i for i in range(2 * n)}
    args = [pltpu.with_memory_space_constraint(a, pltpu.HBM) for a in list(arrays) + lands] + [dep]
    res = pl.pallas_call(
        body, name=name, in_specs=[HBM] * (2 * n) + [ANY], out_shape=out_shape,
        out_specs=[SEM, SEM] + [HBM] * (2 * n) + [pl.BlockSpec(memory_space=pltpu.VMEM)],
        input_output_aliases=aliases, compiler_params=pltpu.CompilerParams(has_side_effects=EFFECT))(*args)
    return dict(send=res[0], recv=res[1], srcs=res[2:2 + n], lands=res[2 + n:2 + 2 * n], token=res[-1],
                scatter=scatter)


def _exchange_wait(name, started, after):
    n = len(started["srcs"])
    scatter = started["scatter"]

    def body(*refs):
        srcs, land_refs = refs[:n], refs[n:2 * n]
        send_sems, recv_sems = refs[2 * n], refs[2 * n + 1]
        me = _my_id()
        for cp in _remote_copies(srcs, land_refs, scatter, send_sems, recv_sems, me, False):
            cp.wait_send()
        for cp in _remote_copies(srcs, land_refs, scatter, send_sems, recv_sems, me, True):
            cp.wait_recv()

    arrs = list(started["srcs"]) + list(started["lands"])
    res = pl.pallas_call(
        body, name=name, in_specs=[HBM] * (2 * n) + [SEM, SEM, ANY],
        out_shape=[pltpu.HBM(a.shape, a.dtype) for a in arrs], out_specs=[HBM] * (2 * n),
        input_output_aliases={i: i for i in range(2 * n)},
        compiler_params=pltpu.CompilerParams(has_side_effects=EFFECT))(*arrs, started["send"], started["recv"], after)
    me = _my_id()
    out = []
    for src, land, sc in zip(res[:n], res[n:], scatter):
        own = lax.dynamic_index_in_dim(src, me, 0, keepdims=True) if sc else src[None]
        out.append(lax.dynamic_update_slice(land, own, (me,) + (0,) * (land.ndim - 1)))
    return out


def _adamw(name, parts, w, m, v):
    r, c = w.shape
    tr, tc = r, c
    if r % 8 == 0:
        tr = next(cand for cand in (128, 88, 64, 40, 8) if r % cand == 0)
    else:
        tc = 256
    c1 = 1.0 / (1.0 - ADAM_B1 ** ADAM_STEP)
    c2 = 1.0 / (1.0 - ADAM_B2 ** ADAM_STEP)

    def body(p_ref, w_ref, m_ref, v_ref, g_ref, d_ref, nm_ref, nv_ref):
        g = p_ref[0].astype(F32)
        for s in range(1, N_DEV):
            g = g + p_ref[s].astype(F32)
        mn = ADAM_B1 * m_ref[...] + (1.0 - ADAM_B1) * g
        vn = ADAM_B2 * v_ref[...] + (1.0 - ADAM_B2) * (g * g)
        g_ref[...] = g
        nm_ref[...] = mn
        nv_ref[...] = vn
        d_ref[...] = -ADAM_LR * ((mn * c1) / (jnp.sqrt(vn * c2) + ADAM_EPS) + ADAM_WD * w_ref[...])

    blk = pl.BlockSpec((tr, tc), lambda i, j: (i, j))
    return pl.pallas_call(
        body, grid=(r // tr, c // tc),
        in_specs=[pl.BlockSpec((N_DEV, tr, tc), lambda i, j: (0, i, j)), blk, blk, blk],
        out_specs=[blk] * 4, out_shape=[_sds((r, c), F32)] * 4, name=name,
        compiler_params=_params(2, VMEM_LIMIT))(parts, w, m, v)


def _pad_rows(a, rows):
    return jnp.pad(a, ((0, rows - a.shape[0]), (0, 0)))


def _lane_row(vec8, offset):
    return jnp.pad(vec8.reshape(1, 8), ((0, 0), (offset, HD - 8 - offset)))


def kernel(x, positions, attn_norm_w, w_in, conv_w, a_log, dt_bias, delta_out_norm_w, q_norm_w, k_norm_w, attn_out_norm_w, w_out, ffn_norm_w, w_gate_up, w_down, loss_target, m_attn_norm_w, m_w_in, m_conv_w, m_a_log, m_dt_bias, m_delta_out_norm_w, m_q_norm_w, m_k_norm_w, m_attn_out_norm_w, m_w_out, m_ffn_norm_w, m_w_gate_up, m_w_down, v_attn_norm_w, v_w_in, v_conv_w, v_a_log, v_dt_bias, v_delta_out_norm_w, v_q_norm_w, v_k_norm_w, v_attn_out_norm_w, v_w_out, v_ffn_norm_w, v_w_gate_up, v_w_down):
    x2 = x[0]
    t, d = x2.shape
    target = loss_target[0]
    pos_col = positions.reshape(t, 1)
    half = HD // 2
    inv = (ROPE_THETA ** (-np.arange(half, dtype=np.float32) / half)).astype(np.float32)
    inv_row = jnp.asarray(np.concatenate([inv, inv]).reshape(1, HD))

    n_in = w_in.shape[2]
    n_gu = w_gate_up.shape[2]
    w_in_g, conv_g = _gather_two_level("gather_in", [w_in[0].astype(BF16), _pad_rows(conv_w[0], 8)])
    out_fly = _exchange_start("gather_out_start", [w_out[0].astype(BF16)], [False], conv_g)
    gu_fly = _exchange_start("gather_gate_up_start", [w_gate_up[0].astype(BF16)], [False], out_fly["token"])
    down_fly = _exchange_start("gather_down_start", [w_down[0].astype(BF16)], [False], gu_fly["token"])
    n_main = 4 * GW
    n_small = 2 * N_HEADS
    segments = [(0, n_main, 0), (n_main + n_small, N_DEV * n_in, n_main), (n_main, n_main + n_small, 7 * GW)]
    pieces = []
    for lo, hi, _ in segments:
        f = lo
        while f < hi:
            j = f // n_in
            end = min(hi, (j + 1) * n_in)
            pieces.append(w_in_g[j][:, f - j * n_in:end - j * n_in])
            f = end
    w_cat = jnp.concatenate(pieces + [jnp.zeros((d, HD - n_small), BF16)], axis=1)
    n_cat = w_cat.shape[1]
    small_blk = (7 * GW) // HD
    conv_w8 =jnp.transpose(conv_g, (1, 0, 2)).reshape(8, 3 * GW)
    alog_row = _lane_row(a_log[0], 8)
    dtb_row = _lane_row(dt_bias[0], 8)

    tm = min(2048, t)
    h1 = _rms_fwd("norm1", x2, attn_norm_w, down_fly["token"])
    tmp, tnp = min(1024, t), n_cat // 3
    proj = _mm("in_proj", h1, w_cat, grid=(t // tmp, n_cat // tnp, 1),
               a_spec=pl.BlockSpec((tmp, d), lambda i, j, k: (i, 0)),
               b_spec=pl.BlockSpec((d, tnp), lambda i, j, k: (0, j)),
               o_spec=pl.BlockSpec((tmp, tnp), lambda i, j, k: (i, j)),
               out_shape=_sds((t, n_cat), F32), ca=1, cb=0, nk=1)
    qn = _conv_fwd("conv_q", proj, conv_w8, 0, True, HD ** -0.5)
    kn = _conv_fwd("conv_k", proj, conv_w8, 1, True, 1.0)
    vv = _conv_fwd("conv_v", proj, conv_w8, 2, False, 1.0)
    beta_b, gc_b = _gates_fwd("gates", proj, small_blk, alog_row, dtb_row)
    u, w, p, tinv, qd, kd = _delta_prep("delta_prep", qn, kn, vv, beta_b, gc_b)
    oa_raw, vn, s_hist = _delta_scan("delta_scan", u, w, p, qd, kd, gc_b)

    cos_t, sin_t = _rope_tables("rope_tables", pos_col, inv_row)
    aq, ak = _qk_fwd("attn_qk", proj, 2, q_norm_w, k_norm_w, cos_t, sin_t)
    ob, lse = _attn_fwd("attn_fwd", aq, ak, proj, 6)
    mixed = _mix_fwd("mix", oa_raw, proj, 3, ob, delta_out_norm_w, attn_out_norm_w)
    (w_out_g,) = _exchange_wait("gather_out_wait", out_fly, mixed)
    w_out_full = w_out_g.reshape(2 * GW, d)
    tn = 512
    x1, h2 = _out_proj_norm("out_proj", mixed, w_out_full, x2, ffn_norm_w)
    per = N_DEV // 2
    (w_gu_g,) = _exchange_wait("gather_gate_up_wait", gu_fly, h2)
    gu3, act = _gate_up_swiglu("gate_up", h2, w_gu_g)
    (w_down_g,) = _exchange_wait("gather_down_wait", down_fly, act)
    w_down_full = w_down_g.reshape(D_FF, d)
    tmd = min(1024, t)
    dy, dy16, loss_tile = _down_loss("down_proj", act, w_down_full, x1, target)
    loss = lax.psum(loss_tile[0, 0], ("x", "y", "c"))

    tk, nkt = t, 1
    g_down = _mm("g_down", act, dy16, dep=loss.reshape(1, 1), grid=(D_FF // 1408, d // 512, nkt),
                 a_spec=pl.BlockSpec((tk, 1408), lambda i, j, k: (k, i)),
                 b_spec=pl.BlockSpec((tk, 512), lambda i, j, k: (k, j)),
                 o_spec=pl.BlockSpec((1408, 512), lambda i, j, k: (i, j)),
                 out_shape=_sds((D_FF, d), F32), ca=0, cb=0, nk=nkt)
    down_g_fly = _exchange_start("reduce_down_start", [g_down.reshape(N_DEV, D_FF // N_DEV, d)], [True], dy16)
    dgu3 = _d_gate_up("d_gate_up", dy16, w_down_full, gu3, down_g_fly["token"])
    g_gu = _mm("g_gate_up", h2, dgu3, grid=(d // 512, N_DEV, nkt),
               a_spec=pl.BlockSpec((tk, 512), lambda i, j, k: (k, i)),
               b_spec=pl.BlockSpec((None, tk, n_gu), lambda i, j, k: (j // per, k, j % per)),
               o_spec=pl.BlockSpec((None, 512, n_gu), lambda i, j, k: (j, i, 0)),
               out_shape=_sds((N_DEV, d, n_gu), F32), ca=0, cb=0, nk=nkt)
    gu_g_fly = _exchange_start("reduce_gate_up_start", [g_gu], [True], dy16)
    dh2 = _d_h2("d_h2", dgu3, w_gu_g, gu_g_fly["token"])
    dx1, dx1_16, g_ffn_norm = _rms_bwd("norm2_bwd", x1, ffn_norm_w, dh2, dy)

    g_out = _mm("g_out", mixed, dx1_16, grid=((2 * GW) // 512, 1, nkt),
                a_spec=pl.BlockSpec((tk, 512), lambda i, j, k: (k, i)),
                b_spec=pl.BlockSpec((tk, d), lambda i, j, k: (k, 0)),
                o_spec=pl.BlockSpec((512, d), lambda i, j, k: (i, 0)),
                out_shape=_sds((2 * GW, d), F32), ca=0, cb=0, nk=nkt)
    out_g_fly = _exchange_start("reduce_out_start", [g_out.reshape(N_DEV, (2 * GW) // N_DEV, d)], [True], g_ffn_norm)
    dmixed = _mm("d_mixed", dx1_16, w_out_full, dep=out_g_fly["token"], grid=(t // tm, (2 * GW) // tn, 1),
                 a_spec=pl.BlockSpec((tm, d), lambda i, j, k: (i, 0)),
                 b_spec=pl.BlockSpec((tn, d), lambda i, j, k: (j, 0)),
                 o_spec=pl.BlockSpec((tm, tn), lambda i, j, k: (i, j)),
                 out_shape=_sds((t, 2 * GW), F32), ca=1, cb=1, nk=1)
    doa, dproj, dob, delta, g_dn, g_an = _mix_bwd("mix_bwd", dmixed, oa_raw, proj, 3, ob,
                                                  delta_out_norm_w, attn_out_norm_w, out_g_fly["token"])
    d_aq, d_ak, d_av = _attn_bwd("attn_bwd", aq, ak, proj, 6, dob, lse, delta)
    dproj, g_qn, g_kn = _qk_bwd("attn_qk_bwd", proj, 2, q_norm_w, k_norm_w, cos_t, sin_t, d_aq, d_ak, dproj)
    dproj = _cast_into("attn_v_bwd", d_av, dproj, 6)

    dvn, dqd, dkd, dw, ddec = _delta_scan_bwd("delta_scan_bwd", doa, w, p, qd, kd, gc_b, vn, s_hist)
    dqn, dkn, dvv, dbeta_b, dg_b = _delta_prep_bwd("delta_prep_bwd", qn, kn, vv, beta_b, gc_b, tinv, u, w, vn,
                                                   doa, dvn, dqd, dkd, dw, ddec)
    dproj, gcw_q = _conv_bwd("conv_q_bwd", proj, conv_w8, dqn, dproj, 0, True, HD ** -0.5)
    dproj, gcw_k = _conv_bwd("conv_k_bwd", proj, conv_w8, dkn, dproj, 1, True, 1.0)
    dproj, gcw_v = _conv_bwd("conv_v_bwd", proj, conv_w8, dvv, dproj, 2, False, 1.0)
    dproj, g_alog_row, g_dtb_row = _gates_bwd("gates_bwd", proj, small_blk, alog_row, dtb_row, dbeta_b, dg_b, dproj)
    tmc = 384
    g_cat = _mm("g_in", dproj, h1, grid=(n_cat // tmc, 1, nkt),
                a_spec=pl.BlockSpec((tk, tmc), lambda i, j, k: (k, i)),
                b_spec=pl.BlockSpec((tk, d), lambda i, j, k: (k, 0)),
                o_spec=pl.BlockSpec((tmc, d), lambda i, j, k: (i, 0)),
                out_shape=_sds((n_cat, d), F32), ca=0, cb=0, nk=nkt)
    parts = []
    for j in range(N_DEV):
        cols = []
        for lo, hi, start in sorted(segments):
            a, b = max(lo, j * n_in), min(hi, (j + 1) * n_in)
            if a < b:
                cols.append(g_cat[start + a - lo:start + b - lo])
        parts.append(cols[0] if len(cols) == 1 else jnp.concatenate(cols, axis=0))
    g_in_parts = jnp.stack(parts).astype(BF16)
    g_conv = jnp.concatenate([gcw_q, gcw_k, gcw_v], axis=1)
    n_cw = conv_w.shape[2]
    g_conv_parts = jnp.transpose(g_conv.reshape(8, N_DEV, n_cw), (1, 0, 2))
    in_g_fly = _exchange_start("reduce_in_start", [g_in_parts, g_conv_parts], [True] * 2, g_dtb_row)
    tmh1 = min(512, t)
    dh1 = _mm("d_h1", dproj, w_cat, dep=in_g_fly["token"], grid=(t // tmh1, d // 1024, 1),
              a_spec=pl.BlockSpec((tmh1, n_cat), lambda i, j, k: (i, 0)),
              b_spec=pl.BlockSpec((1024, n_cat), lambda i, j, k: (j, 0)),
              o_spec=pl.BlockSpec((tmh1, 1024), lambda i, j, k: (i, j)),
              out_shape=_sds((t, d), F32), ca=1, cb=1, nk=1)
    grad_x, _, g_attn_norm = _rms_bwd("norm1_bwd", x2, attn_norm_w, dh1, dx1)

    small_rows = [g_attn_norm.reshape(d // HD, HD), g_ffn_norm.reshape(d // HD, HD), g_dn, g_qn, g_kn, g_an,
                  g_alog_row, g_dtb_row]
    small_pack = _pad_rows(jnp.concatenate(small_rows, axis=0), 40)
    (r_down,) = _exchange_wait("reduce_down_wait", down_g_fly, grad_x)
    (r_gu,) = _exchange_wait("reduce_gate_up_wait", gu_g_fly, grad_x)
    (r_out,) = _exchange_wait("reduce_out_wait", out_g_fly, grad_x)
    res_gu = [a[None] for a in _adamw("adamw_gate_up", r_gu, w_gate_up[0], m_w_gate_up[0], v_w_gate_up[0])]
    res_down = [a[None] for a in _adamw("adamw_down", r_down, w_down[0], m_w_down[0], v_w_down[0])]
    res_out = [a[None] for a in _adamw("adamw_out", r_out, w_out[0], m_w_out[0], v_w_out[0])]
    done = (res_gu[3][0, :1, :1] + res_down[3][0, :1, :1] + res_out[3][0, :1, :1])
    (r_small,) = _exchange("gather_small_grads", [small_pack], [False], done)

    def pack_small(an, fn, dn, qn_, kn_, aon, al, db):
        rows = [an.reshape(d // HD, HD), fn.reshape(d // HD, HD), dn, qn_, kn_, aon,
                _lane_row(al[0], 8), _lane_row(db[0], 8)]
        return _pad_rows(jnp.concatenate(rows, axis=0), 40)

    def unpack_small(pk):
        nr = d // HD
        return dict(attn_norm_w=pk[:nr].reshape(1, d), ffn_norm_w=pk[nr:2 * nr].reshape(1, d),
                    delta_out_norm_w=pk[2 * nr:2 * nr + 1], q_norm_w=pk[2 * nr + 1:2 * nr + 2],
                    k_norm_w=pk[2 * nr + 2:2 * nr + 3], attn_out_norm_w=pk[2 * nr + 3:2 * nr + 4],
                    a_log=pk[2 * nr + 4:2 * nr + 5, 8:16], dt_bias=pk[2 * nr + 5:2 * nr + 6, 8:16])

    res_small = _adamw("adamw_small", r_small,
                       pack_small(attn_norm_w, ffn_norm_w, delta_out_norm_w, q_norm_w, k_norm_w, attn_out_norm_w, a_log, dt_bias),
                       pack_small(m_attn_norm_w, m_ffn_norm_w, m_delta_out_norm_w, m_q_norm_w, m_k_norm_w, m_attn_out_norm_w, m_a_log, m_dt_bias),
                       pack_small(v_attn_norm_w, v_ffn_norm_w, v_delta_out_norm_w, v_q_norm_w, v_k_norm_w, v_attn_out_norm_w, v_a_log, v_dt_bias))
    small = [unpack_small(a) for a in res_small]
    r_in, r_conv = _exchange_wait("reduce_in_wait", in_g_fly, res_small[0])
    res_in = [jnp.transpose(a)[None] for a in _adamw("adamw_in", r_in, jnp.transpose(w_in[0]), jnp.transpose(m_w_in[0]),
                                                     jnp.transpose(v_w_in[0]))]
    res_conv =[a[None, :4] for a in _adamw("adamw_conv", r_conv, _pad_rows(conv_w[0], 8), _pad_rows(m_conv_w[0], 8),
                                            _pad_rows(v_conv_w[0], 8))]

    outs = [loss, grad_x[None]]
    for i in range(4):
        s = small[i]
        outs += [s["attn_norm_w"], res_in[i], res_conv[i], s["a_log"], s["dt_bias"], s["delta_out_norm_w"],
                 s["q_norm_w"], s["k_norm_w"], s["attn_out_norm_w"], res_out[i], s["ffn_norm_w"], res_gu[i],
                 res_down[i]]
    return tuple(outs)
```

```python
import numpy as np
import jax
import jax.numpy as jnp
from jax import lax
from jax.experimental import pallas as pl
from jax.experimental.pallas import tpu as pltpu

F32 = jnp.float32
BF16 = jnp.bfloat16

N_DEV = 8
N_HEADS = 8
HD = 128
GW = N_HEADS * HD
CHUNK = 64
PAIR = 2 * CHUNK
SPAN = 128
DILATIONS = (1, 4, 16)
ROPE_THETA = 10000.0
EPS = 1e-6
D_FF = 5632
ADAM_LR, ADAM_B1, ADAM_B2, ADAM_EPS, ADAM_WD, ADAM_STEP = 0.001, 0.9, 0.999, 1e-8, 0.01, 10
NEG = -1e30
VMEM_LIMIT = 56 * 1024 * 1024
ANY = pl.BlockSpec(memory_space=pl.ANY)
HEADS_PER_STEP = 8


def _params(n_grid, vmem=VMEM_LIMIT):
    return pltpu.CompilerParams(dimension_semantics=("arbitrary",) * n_grid, vmem_limit_bytes=vmem)


def _sds(shape, dtype):
    return jax.ShapeDtypeStruct(tuple(shape), dtype)


def _sigmoid(x):
    return 1.0 / (1.0 + jnp.exp(-x))


def _silu(x):
    return x * _sigmoid(x)


def _softplus(x):
    return jnp.maximum(x, 0.0) + jnp.log(1.0 + jnp.exp(-jnp.abs(x)))


def _dot(a, b, ca, cb):
    return lax.dot_general(a, b, (((ca,), (cb,)), ((), ())), preferred_element_type=F32)


def _b16(x):
    return x if x.dtype == BF16 else x.astype(BF16)


def _split(x):
    hi = x.astype(BF16)
    return hi, (x - hi.astype(F32)).astype(BF16)


def _dot3(a, b, ca, cb):
    a_hi, a_lo = _split(a)
    b_hi, b_lo = _split(b)
    return _dot(a_hi, b_hi, ca, cb) + (_dot(a_hi, b_lo, ca, cb) + _dot(a_lo, b_hi, ca, cb))


def _iota2(shape, axis):
    return lax.broadcasted_iota(jnp.int32, shape, axis)


def _mm(name, a, b, *, grid, a_spec, b_spec, o_spec, out_shape, ca, cb, nk, dep=None):
    assert nk == 1 and grid[2] == 1

    def body(*refs):
        refs[-1][...] = _dot(_b16(refs[0][...]), _b16(refs[1][...]), ca, cb).astype(refs[-1].dtype)

    in_specs = [a_spec, b_spec] + ([ANY] if dep is not None else [])
    args = (a, b) + ((dep,) if dep is not None else ())
    return pl.pallas_call(body, grid=grid, in_specs=in_specs, out_specs=o_spec, out_shape=out_shape,
                          name=name, compiler_params=_params(3))(*args)


def _rms_f(xv, wv):
    return xv * lax.rsqrt(jnp.mean(xv * xv, axis=-1, keepdims=True) + EPS) * wv


def _rms_fwd(name, x, w, dep):
    t, d = x.shape
    tm = min(512, t)

    def body(x_ref, w_ref, dep_ref, o_ref):
        o_ref[...] = _rms_f(x_ref[...], w_ref[...]).astype(BF16)

    row = pl.BlockSpec((tm, d), lambda i: (i, 0))
    vec = pl.BlockSpec((1, d), lambda i: (0, 0))
    return pl.pallas_call(body, grid=(t // tm,), in_specs=[row, vec, ANY], out_specs=row,
                          out_shape=_sds((t, d), BF16), name=name, compiler_params=_params(1))(x, w, dep)


def _rms_bwd(name, x, w, dh, res):
    t, d = x.shape
    tm = min(256, t)

    def body(x_ref, w_ref, dh_ref, res_ref, dx_ref, dx16_ref, dw_ref):
        _, vjp = jax.vjp(_rms_f, x_ref[...], w_ref[...])
        dxv, dwv = vjp(dh_ref[...])
        dxv = dxv + res_ref[...]
        dx_ref[...] = dxv
        dx16_ref[...] = dxv.astype(BF16)

        @pl.when(pl.program_id(0) == 0)
        def _():
            dw_ref[...] = jnp.zeros_like(dw_ref)

        dw_ref[...] += dwv

    row = pl.BlockSpec((tm, d), lambda i: (i, 0))
    vec = pl.BlockSpec((1, d), lambda i: (0, 0))
    return pl.pallas_call(body, grid=(t // tm,), in_specs=[row, vec, row, row], out_specs=[row, row, vec],
                          out_shape=[_sds((t, d), F32), _sds((t, d), BF16), _sds((1, d), F32)], name=name,
                          compiler_params=_params(1))(x, w, dh, res)


def _shift_rows(x, s):
    t = x.shape[0]
    r = pltpu.roll(x, s % t, 0)
    row8 = _iota2((8, x.shape[1]), 0)
    if s > 0:
        return jnp.concatenate([jnp.where(row8 >= s, r[:8], 0.0), r[8:]], axis=0)
    return jnp.concatenate([r[:t - 8], jnp.where(row8 < 8 + s, r[t - 8:], 0.0)], axis=0)


def _conv_taps(xv, w_ref):
    c = w_ref[3:4, :] * xv
    for s in (1, 2, 3):
        c = c + w_ref[3 - s:4 - s, :] * _shift_rows(xv, s)
    return c


def _post_conv(c, l2, scale):
    y = _silu(c)
    if l2:
        y = y * lax.rsqrt(jnp.sum(y * y, axis=-1, keepdims=True) + EPS) * scale
    return y


def _conv_fwd(name, proj, conv_w8, group, l2, scale):
    t = proj.shape[0]

    def body(x_ref, w_ref, o_ref):
        o_ref[...] = _post_conv(_conv_taps(x_ref[...], w_ref), l2, scale)

    return pl.pallas_call(
        body, grid=(N_HEADS,),
        in_specs=[pl.BlockSpec((t, HD), lambda h: (0, h + group * N_HEADS)),
                  pl.BlockSpec((8, HD), lambda h: (0, h + group * N_HEADS))],
        out_specs=pl.BlockSpec((t, HD), lambda h: (0, h)),
        out_shape=_sds((t, GW), F32), name=name, compiler_params=_params(1, VMEM_LIMIT))(proj, conv_w8)


def _conv_bwd(name, proj, conv_w8, dn, dproj, group, l2, scale):
    t = proj.shape[0]

    def body(x_ref, w_ref, dn_ref, dproj_ref, dx_ref, dw_ref):
        xv = x_ref[...]
        c = _conv_taps(xv, w_ref)
        _, vjp = jax.vjp(lambda cc: _post_conv(cc, l2, scale), c)
        (dc,) = vjp(dn_ref[...])
        dx = w_ref[3:4, :] * dc
        dw = jnp.zeros((8, HD), F32)
        rid = _iota2((8, HD), 0)
        dw = dw + jnp.where(rid == 3, jnp.sum(dc * xv, axis=0, keepdims=True), 0.0)
        for s in (1, 2, 3):
            dx = dx + w_ref[3 - s:4 - s, :] * _shift_rows(dc, -s)
            dw = dw + jnp.where(rid == 3 - s, jnp.sum(dc * _shift_rows(xv, s), axis=0, keepdims=True), 0.0)
        dx_ref[...] = dx.astype(BF16)
        dw_ref[...] = dw

    return pl.pallas_call(
        body, grid=(N_HEADS,),
        in_specs=[pl.BlockSpec((t, HD), lambda h: (0, h + group * N_HEADS)),
                  pl.BlockSpec((8, HD), lambda h: (0, h + group * N_HEADS)),
                  pl.BlockSpec((t, HD), lambda h: (0, h)), ANY],
        out_specs=[pl.BlockSpec((t, HD), lambda h: (0, h + group * N_HEADS)), pl.BlockSpec((8, HD), lambda h: (0, h))],
        out_shape=[_sds(dproj.shape, BF16), _sds((8, GW), F32)], input_output_aliases={3: 0}, name=name,
        compiler_params=_params(1, VMEM_LIMIT))(proj, conv_w8, dn, dproj)


def _chunk_cumsum(g, rows):
    pos = rows % CHUNK
    s = 1
    while s < CHUNK:
        g = g + jnp.where(pos >= s, pltpu.roll(g, s, 0), 0.0)
        s *= 2
    return g


def _gates_fwd(name, proj, small_blk, alog_row, dtb_row):
    t = proj.shape[0]
    tm = min(256, t)

    def body(s_ref, a_ref, b_ref, beta_ref, gc_ref):
        sm = s_ref[...]
        beta = _sigmoid(sm)
        g = -jnp.exp(a_ref[...]) * _softplus(sm + b_ref[...])
        gc = _chunk_cumsum(g, _iota2((tm, HD), 0))
        lane = _iota2((tm, HD), 1)
        for h in range(N_HEADS):
            bcol = jnp.sum(jnp.where(lane == h, beta, 0.0), axis=1, keepdims=True)
            gcol = jnp.sum(jnp.where(lane == 8 + h, gc, 0.0), axis=1, keepdims=True)
            beta_ref[:, h * HD:(h + 1) * HD] = jnp.broadcast_to(bcol, (tm, HD))
            gc_ref[:, h * HD:(h + 1) * HD] = jnp.broadcast_to(gcol, (tm, HD))

    vec = pl.BlockSpec((1, HD), lambda i: (0, 0))
    wide = pl.BlockSpec((tm, GW), lambda i: (i, 0))
    return pl.pallas_call(
        body, grid=(t // tm,),
        in_specs=[pl.BlockSpec((tm, HD), lambda i: (i, small_blk)), vec, vec], out_specs=[wide, wide],
        out_shape=[_sds((t, GW), F32), _sds((t, GW), F32)], name=name,
        compiler_params=_params(1))(proj, alog_row, dtb_row)


def _gates_bwd(name, proj, small_blk, alog_row, dtb_row, dbeta_b, dg_b, dproj):
    t = proj.shape[0]
    tm = min(256, t)

    def body(s_ref, a_ref, b_ref, db_ref, dg_ref, dproj_ref, ds_ref, da_ref, dbias_ref):
        sm = s_ref[...]
        lane = _iota2((tm, HD), 1)
        db = jnp.zeros((tm, HD), F32)
        dg = jnp.zeros((tm, HD), F32)
        for h in range(N_HEADS):
            db = db + jnp.where(lane == h, db_ref[:, h * HD:(h + 1) * HD], 0.0)
            dg = dg + jnp.where(lane == 8 + h, dg_ref[:, h * HD:(h + 1) * HD], 0.0)
        beta = _sigmoid(sm)
        ea = jnp.exp(a_ref[...])
        pre = sm + b_ref[...]
        g = -ea * _softplus(pre)
        dpre = dg * (-ea) * _sigmoid(pre)
        ds_ref[...] = (db * beta * (1.0 - beta) + dpre).astype(BF16)

        @pl.when(pl.program_id(0) == 0)
        def _():
            da_ref[...] = jnp.zeros_like(da_ref)
            dbias_ref[...] = jnp.zeros_like(dbias_ref)

        da_ref[...] += jnp.sum(dg * g, axis=0, keepdims=True)
        dbias_ref[...] += jnp.sum(dpre, axis=0, keepdims=True)

    vec = pl.BlockSpec((1, HD), lambda i: (0, 0))
    wide = pl.BlockSpec((tm, GW), lambda i: (i, 0))
    return pl.pallas_call(
        body, grid=(t // tm,),
        in_specs=[pl.BlockSpec((tm, HD), lambda i: (i, small_blk)), vec, vec, wide, wide, ANY],
        out_specs=[pl.BlockSpec((tm, HD), lambda i: (i, small_blk)), vec, vec],
        out_shape=[_sds(dproj.shape, BF16), _sds((1, HD), F32), _sds((1, HD), F32)],
        input_output_aliases={5: 0}, name=name,
        compiler_params=_params(1))(proj, alog_row, dtb_row, dbeta_b, dg_b, dproj)


def _pair_masks():
    ii = _iota2((PAIR, PAIR), 0)
    jj = _iota2((PAIR, PAIR), 1)
    same = (ii // CHUNK) == (jj // CHUNK)
    return ii, jj, same & (ii >= jj), same & (ii > jj)


def _to_row(col_b, ii, jj):
    return jnp.sum(jnp.where(ii == jj, col_b, 0.0), axis=0, keepdims=True)


def _to_col(row, ii, jj):
    return jnp.sum(jnp.where(ii == jj, jnp.broadcast_to(row, (PAIR, PAIR)), 0.0), axis=1, keepdims=True)


def _decay_parts(gc, last_a, last_b, ii, jj, causal):
    diff = gc - _to_row(gc, ii, jj)
    dmat = jnp.where(causal, jnp.exp(jnp.where(causal, diff, 0.0)), 0.0)
    glast = jnp.where(ii < CHUNK, last_a, last_b)
    return dmat, jnp.exp(gc), jnp.exp(glast - gc)


def _unit_lower_inverse(lows, ii, jj):
    eye = jnp.where(ii == jj, 1.0, 0.0)
    mm = lambda xs, ys: [_dot3(a, b, 1, 0) for a, b in zip(xs, ys)]
    plus = lambda xs: [eye + a for a in xs]
    minus = lambda xs: [eye - a for a in xs]
    d1 = [jnp.where((ii // 16) == (jj // 16), low, 0.0) for low in lows]
    d2 = mm(d1, d1)
    a = mm(minus(d1), plus(d2))
    d4 = mm(d2, d2)
    a = mm(a, plus(d4))
    d8 = mm(d4, d4)
    td = mm(a, plus(d8))
    n1 = mm(td, [low - d for low, d in zip(lows, d1)])
    n2 = mm(n1, n1)
    return mm(mm(minus(n1), plus(n2)), td)


def _delta_prep(name, qn, kn, vv, beta_b, gc_b):
    t = qn.shape[0]

    def body(q_ref, k_ref, v_ref, b_ref, g_ref, u_ref, w_ref, p_ref, t_ref, qd_ref, kd_ref):
        ii, jj, causal, strict = _pair_masks()
        sls = [slice(hh * HD, (hh + 1) * HD) for hh in range(HEADS_PER_STEP)]
        lows = []
        for sl in sls:
            q, k, beta = q_ref[:, sl], k_ref[:, sl], b_ref[:, sl]
            dmat, gam, e2 = _decay_parts(g_ref[:, sl], g_ref[CHUNK - 1:CHUNK, sl], g_ref[PAIR - 1:PAIR, sl],
                                         ii, jj, causal)
            k16 = _b16(k)
            lows.append(jnp.where(strict, beta * _dot(k16, k16, 1, 1) * dmat, 0.0))
            p_ref[:, sl] = jnp.where(causal, _dot(_b16(q), k16, 1, 1) * dmat, 0.0).astype(BF16)
            qd_ref[:, sl] = (q * gam).astype(BF16)
            kd_ref[:, sl] = (k * e2).astype(BF16)
        for sl, tinv in zip(sls, _unit_lower_inverse(lows, ii, jj)):
            beta = b_ref[:, sl]
            t_ref[:, sl] = tinv
            u_ref[:, sl] = _dot3(tinv, v_ref[:, sl] * beta, 1, 0)
            w_ref[:, sl] = _dot3(tinv, k_ref[:, sl] * (beta * jnp.exp(g_ref[:, sl])), 1, 0).astype(BF16)

    blk = pl.BlockSpec((PAIR, HEADS_PER_STEP * HD), lambda i, h: (i, h))
    return pl.pallas_call(
        body, grid=(t // PAIR, N_HEADS // HEADS_PER_STEP), in_specs=[blk] * 5, out_specs=[blk] * 6,
        out_shape=[_sds((t, GW), F32), _sds((t, GW), BF16), _sds((t, GW), BF16), _sds((t, GW), F32),
                   _sds((t, GW), BF16), _sds((t, GW), BF16)],
        name=name, compiler_params=_params(2))(qn, kn, vv, beta_b, gc_b)


def _delta_scan(name, u, w, p, qd, kd, gc_b):
    t = u.shape[0]
    n = t // CHUNK

    def body(u_ref, w_ref, p_ref, qd_ref, kd_ref, g_ref, o_ref, vn_ref, sh_ref, state):
        @pl.when(pl.program_id(0) == 0)
        def _():
            state[...] = jnp.zeros_like(state)

        sls = [slice(h * HD, (h + 1) * HD) for h in range(N_HEADS)]
        heads = range(N_HEADS)
        s = [state[h] for h in heads]
        for c in range(PAIR // CHUNK):
            rows = slice(c * CHUNK, (c + 1) * CHUNK)
            last = slice((c + 1) * CHUNK - 1, (c + 1) * CHUNK)
            for h in heads:
                sh_ref[c, h] = s[h]
            s16 = [_b16(a) for a in s]
            ws = [_dot(w_ref[rows, sls[h]], s16[h], 1, 0) for h in heads]
            qs = [_dot(qd_ref[rows, sls[h]], s16[h], 1, 0) for h in heads]
            vn16 = [_b16(u_ref[rows, sls[h]] - ws[h]) for h in heads]
            pv = [_dot(p_ref[rows, sls[h]], jnp.concatenate([vn16[h], vn16[h]], axis=0), 1, 0) for h in heads]
            kv = [_dot(kd_ref[rows, sls[h]], vn16[h], 0, 0) for h in heads]
            for h in heads:
                o_ref[rows, sls[h]] = qs[h] + pv[h]
                vn_ref[rows, sls[h]] = vn16[h]
            s = [s[h] * jnp.exp(g_ref[last, sls[h]]) + kv[h] for h in heads]
        for h in heads:
            state[h] = s[h]

    blk = pl.BlockSpec((PAIR, GW), lambda i: (i, 0))
    return pl.pallas_call(
        body, grid=(t // PAIR,), in_specs=[blk] * 6,
        out_specs=[blk, blk, pl.BlockSpec((PAIR // CHUNK, N_HEADS, HD, HD), lambda i: (i, 0, 0, 0))],
        out_shape=[_sds((t, GW), F32), _sds((t, GW), BF16), _sds((n, N_HEADS, HD, HD), F32)],
        scratch_shapes=[pltpu.VMEM((N_HEADS, HD, HD), F32)], name=name,
        compiler_params=_params(1))(u, w, p, qd, kd, gc_b)


def _delta_scan_bwd(name, do, w, p, qd, kd, gc_b, vn, s_hist):
    t = do.shape[0]
    n = t // CHUNK

    def body(do_ref, w_ref, p_ref, qd_ref, kd_ref, g_ref, vn_ref, sh_ref,
             dvn_ref, dqd_ref, dkd_ref, dw_ref, ddec_ref, dstate):
        @pl.when(pl.program_id(0) == 0)
        def _():
            dstate[...] = jnp.zeros_like(dstate)

        sls = [slice(h * HD, (h + 1) * HD) for h in range(N_HEADS)]
        heads = range(N_HEADS)
        ds = [dstate[h] for h in heads]
        for c in reversed(range(PAIR // CHUNK)):
            rows = slice(c * CHUNK, (c + 1) * CHUNK)
            last = slice((c + 1) * CHUNK - 1, (c + 1) * CHUNK)
            ds16 = [_b16(a) for a in ds]
            s16 = [_b16(sh_ref[c, h]) for h in heads]
            do16 = [_b16(do_ref[rows, sls[h]]) for h in heads]
            ptdo = [_dot(p_ref[rows, sls[h]], do16[h], 0, 0) for h in heads]
            kds = [_dot(kd_ref[rows, sls[h]], ds16[h], 1, 0) for h in heads]
            qdo = [_dot(qd_ref[rows, sls[h]], do16[h], 0, 0) for h in heads]
            for h in heads:
                dqd_ref[rows, sls[h]] = _dot(do16[h], s16[h], 1, 1)
                dkd_ref[rows, sls[h]] = _dot(vn_ref[rows, sls[h]], ds16[h], 1, 1)
            dvn = [ptdo[h][:CHUNK, :] + ptdo[h][CHUNK:, :] + kds[h] for h in heads]
            dvn16 = [_b16(a) for a in dvn]
            wdv = [_dot(w_ref[rows, sls[h]], dvn16[h], 0, 0) for h in heads]
            for h in heads:
                dvn_ref[rows, sls[h]] = dvn[h]
                dw_ref[rows, sls[h]] = -_dot(dvn16[h], s16[h], 1, 1)
                tot = jnp.sum(jnp.sum(sh_ref[c, h] * ds[h], axis=1, keepdims=True), axis=0, keepdims=True)
                ddec_ref[c * 8:(c + 1) * 8, sls[h]] = jnp.broadcast_to(tot, (8, HD))
            ds = [ds[h] * jnp.exp(g_ref[last, sls[h]]) + qdo[h] - wdv[h] for h in heads]
        for h in heads:
            dstate[h] = ds[h]

    npair = t // PAIR
    blk = pl.BlockSpec((PAIR, GW), lambda i: (npair - 1 - i, 0))
    return pl.pallas_call(
        body, grid=(npair,),
        in_specs=[blk] * 7 + [pl.BlockSpec((PAIR // CHUNK, N_HEADS, HD, HD), lambda i: (npair - 1 - i, 0, 0, 0))],
        out_specs=[blk] * 4 + [pl.BlockSpec((16, GW), lambda i: (npair - 1 - i, 0))],
        out_shape=[_sds((t, GW), F32)] * 4 + [_sds((n * 8, GW), F32)],
        scratch_shapes=[pltpu.VMEM((N_HEADS, HD, HD), F32)], name=name,
        compiler_params=_params(1))(do, w, p, qd, kd, gc_b, vn, s_hist)


def _delta_prep_bwd(name, qn, kn, vv, beta_b, gc_b, tinv, u, w, vn, do, dvn, dqd, dkd, dw, ddec):
    t = qn.shape[0]

    def body(q_ref, k_ref, v_ref, b_ref, g_ref, t_ref, u_ref, w_ref, vn_ref, do_ref, dvn_ref, dqd_ref,
             dkd_ref, dw_ref, ddec_ref, dq_ref, dk_ref, dv_ref, dbeta_ref, dg_ref):
        ii, jj, causal, strict = _pair_masks()
        suffix = ((ii // CHUNK) == (jj // CHUNK)) & (jj >= ii)
        first = ii < CHUNK
        rs = lambda a: jnp.sum(a, axis=1, keepdims=True)
        sls = [slice(hh * HD, (hh + 1) * HD) for hh in range(HEADS_PER_STEP)]
        xs = [_dot3(t_ref[:, sl], dvn_ref[:, sl], 0, 0) for sl in sls]
        ys = [_dot3(t_ref[:, sl], dw_ref[:, sl], 0, 0) for sl in sls]
        k16s = [_b16(k_ref[:, sl]) for sl in sls]
        kks = [_dot(k16, k16, 1, 1) for k16 in k16s]
        qks = [_dot(_b16(q_ref[:, sl]), k16, 1, 1) for sl, k16 in zip(sls, k16s)]
        dps = [jnp.where(causal, _dot(_b16(do_ref[:, sl]), vn_ref[:, sl], 1, 1), 0.0) for sl in sls]
        das = [-jnp.where(strict, _dot(_b16(x), _b16(u_ref[:, sl]), 1, 1) + _dot(_b16(y), w_ref[:, sl], 1, 1), 0.0)
               for sl, x, y in zip(sls, xs, ys)]
        for hh, sl in enumerate(sls):
            q, k, v, beta, gc = q_ref[:, sl], k_ref[:, sl], v_ref[:, sl], b_ref[:, sl], g_ref[:, sl]
            last_a, last_b = g_ref[CHUNK - 1:CHUNK, sl], g_ref[PAIR - 1:PAIR, sl]
            dmat, gam, e2 = _decay_parts(gc, last_a, last_b, ii, jj, causal)
            q16, k16 = _b16(q), k16s[hh]
            kk, qk, dp, x, y, da = kks[hh], qks[hh], dps[hh], xs[hh], ys[hh], das[hh]
            dqd, dkd = dqd_ref[:, sl], dkd_ref[:, sl]
            dpd16 = _b16(dp * dmat)
            dkk16 = _b16(da * beta * dmat)
            dq_ref[:, sl] = gam * dqd + _dot(dpd16, k16, 1, 0)
            dk_ref[:, sl] = (e2 * dkd + _dot(dpd16, q16, 0, 0) + beta * gam * y
                             + _dot(dkk16, k16, 1, 0) + _dot(dkk16, k16, 0, 0))
            dv_ref[:, sl] = beta * x
            dbeta = rs(v * x) + rs(k * gam * y) + rs(da * kk * dmat)
            dbeta_ref[:, sl] = jnp.broadcast_to(dbeta, (PAIR, HD))
            m = (dp * qk + da * beta * kk) * dmat
            dgam = rs(q * dqd) + rs(k * beta * y)
            de2 = rs(k * dkd)
            colsum = _to_col(jnp.sum(m, axis=0, keepdims=True), ii, jj)
            te2 = de2 * e2
            dgc = rs(m) - colsum + gam * dgam - te2
            tail_a = jnp.sum(jnp.where(first, te2, 0.0), axis=0, keepdims=True)
            tail_b = jnp.sum(jnp.where(first, 0.0, te2), axis=0, keepdims=True)
            dgc = dgc + jnp.where(ii == CHUNK - 1, tail_a + ddec_ref[0:1, sl] * jnp.exp(last_a), 0.0)
            dgc = dgc + jnp.where(ii == PAIR - 1, tail_b + ddec_ref[8:9, sl] * jnp.exp(last_b), 0.0)
            dgc_row = _to_row(dgc, ii, jj)
            dg = jnp.sum(jnp.where(suffix, jnp.broadcast_to(dgc_row, (PAIR, PAIR)), 0.0), axis=1, keepdims=True)
            dg_ref[:, sl] = jnp.broadcast_to(dg, (PAIR, HD))

    blk = pl.BlockSpec((PAIR, HEADS_PER_STEP * HD), lambda i, h: (i, h))
    return pl.pallas_call(
        body, grid=(t // PAIR, N_HEADS // HEADS_PER_STEP),
        in_specs=[blk] * 14 + [pl.BlockSpec((16, HEADS_PER_STEP * HD), lambda i, h: (i, h))], out_specs=[blk] * 5,
        out_shape=[_sds((t, GW), F32)] * 5, name=name,
        compiler_params=_params(2))(qn, kn, vv, beta_b, gc_b, tinv, u, w, vn, do, dvn, dqd, dkd, dw, ddec)


def _rope_tables(name, pos_col, inv_row):
    t = pos_col.shape[0]
    tm = min(1024, t)

    def body(pos_ref, inv_ref, cos_ref, sin_ref):
        ang = pos_ref[...].astype(F32) * inv_ref[...]
        lane = _iota2(ang.shape, 1)
        cos_ref[...] = jnp.cos(ang)
        sin_ref[...] = jnp.where(lane < HD // 2, -1.0, 1.0) * jnp.sin(ang)

    tab = pl.BlockSpec((tm, HD), lambda i: (i, 0))
    return pl.pallas_call(
        body, grid=(t // tm,), in_specs=[pl.BlockSpec((tm, 1), lambda i: (i, 0)), pl.BlockSpec((1, HD), lambda i: (0, 0))],
        out_specs=[tab, tab], out_shape=[_sds((t, HD), F32)] * 2, name=name,
        compiler_params=_params(1))(pos_col, inv_row)


def _head_rms(xh, wv):
    return xh * lax.rsqrt(jnp.mean(xh * xh, axis=-1, keepdims=True) + EPS) * wv


def _qk_fwd(name, proj, pair_blk, wq_row, wk_row, cos_t, sin_t):
    t = proj.shape[0]
    tm = min(256, t)

    def body(x_ref, wq_ref, wk_ref, cos_ref, sin_ref, q_ref, k_ref):
        cos, sin = cos_ref[...], sin_ref[...]
        for o_ref, w_ref, base in ((q_ref, wq_ref, 0), (k_ref, wk_ref, GW)):
            for h in range(N_HEADS):
                y = _head_rms(x_ref[:, base + h * HD:base + (h + 1) * HD], w_ref[...])
                o_ref[:, h * HD:(h + 1) * HD] = y * cos + pltpu.roll(y, HD // 2, 1) * sin

    vec = pl.BlockSpec((1, HD), lambda i: (0, 0))
    tab = pl.BlockSpec((tm, HD), lambda i: (i, 0))
    wide = pl.BlockSpec((tm, GW), lambda i: (i, 0))
    return pl.pallas_call(
        body, grid=(t // tm,),
        in_specs=[pl.BlockSpec((tm, 2 * GW), lambda i: (i, pair_blk)), vec, vec, tab, tab],
        out_specs=[wide, wide], out_shape=[_sds((t, GW), F32)] * 2, name=name,
        compiler_params=_params(1))(proj, wq_row, wk_row, cos_t, sin_t)


def _qk_bwd(name, proj, pair_blk, wq_row, wk_row, cos_t, sin_t, dq_full, dk_full, dproj):
    t = proj.shape[0]
    tm = min(256, t)

    def body(x_ref, wq_ref, wk_ref, cos_ref, sin_ref, dq_ref, dk_ref, dproj_ref, dx_ref, dwq_ref, dwk_ref):
        cos, sin = cos_ref[...], sin_ref[...]

        @pl.when(pl.program_id(0) == 0)
        def _():
            dwq_ref[...] = jnp.zeros_like(dwq_ref)
            dwk_ref[...] = jnp.zeros_like(dwk_ref)

        for dy_ref, w_ref, dw_ref, base in ((dq_ref, wq_ref, dwq_ref, 0), (dk_ref, wk_ref, dwk_ref, GW)):
            dw = jnp.zeros((1, HD), F32)
            for h in range(N_HEADS):
                dy = dy_ref[:, h * HD:(h + 1) * HD]
                dy = dy * cos - pltpu.roll(dy, HD // 2, 1) * sin
                _, vjp = jax.vjp(_head_rms, x_ref[:, base + h * HD:base + (h + 1) * HD], w_ref[...])
                dx, dwh = vjp(dy)
                dw = dw + dwh
                dx_ref[:, base + h * HD:base + (h + 1) * HD] = dx.astype(BF16)
            dw_ref[...] += dw

    vec = pl.BlockSpec((1, HD), lambda i: (0, 0))
    tab = pl.BlockSpec((tm, HD), lambda i: (i, 0))
    wide = pl.BlockSpec((tm, GW), lambda i: (i, 0))
    pair = pl.BlockSpec((tm, 2 * GW), lambda i: (i, pair_blk))
    return pl.pallas_call(
        body, grid=(t // tm,), in_specs=[pair, vec, vec, tab, tab, wide, wide, ANY],
        out_specs=[pair, vec, vec],
        out_shape=[_sds(dproj.shape, BF16), _sds((1, HD), F32), _sds((1, HD), F32)], input_output_aliases={7: 0},
        name=name, compiler_params=_params(1))(proj, wq_row, wk_row, cos_t, sin_t, dq_full, dk_full, dproj)


def _cast_into(name, x, dproj, blk_idx):
    t = x.shape[0]
    tm = min(512, t)

    def body(x_ref, dproj_ref, o_ref):
        o_ref[...] = x_ref[...].astype(BF16)

    return pl.pallas_call(
        body, grid=(t // tm,), in_specs=[pl.BlockSpec((tm, GW), lambda i: (i, 0)), ANY],
        out_specs=pl.BlockSpec((tm, GW), lambda i: (i, blk_idx)), out_shape=_sds(dproj.shape, BF16),
        input_output_aliases={1: 0}, name=name, compiler_params=_params(1))(x, dproj)


GROUP = SPAN * max(DILATIONS)
SCALE = HD ** -0.5
TILE_BATCH = 8


def _band_mask(lo):
    qi = _iota2((SPAN, 2 * SPAN), 0)
    ki = _iota2((SPAN, 2 * SPAN), 1)
    return (ki >= qi) & (ki <= qi + SPAN) & (ki >= lo)


def _tiles():
    return [(pi, r, u, rho) for pi, r in enumerate(DILATIONS) for rho in range(r) for u in range(GROUP // (SPAN * r))]


def _rows(r, u, rho):
    return pl.ds(u * SPAN * r + rho, SPAN, stride=r) if r > 1 else pl.ds(u * SPAN, SPAN)


def _attn_fwd(name, q, k, v, v_blk):
    t = q.shape[0]

    def body(qc_ref, kc_ref, vc_ref, kp_ref, vp_ref, ob_ref, lse_ref, o_scr, l_scr):
        mask_in = _band_mask(0)
        mask_edge = _band_mask(jnp.where(pl.program_id(0) == 0, SPAN, 0))
        tiles = _tiles()
        k_own = v_own = None
        for b0 in range(0, len(tiles), TILE_BATCH):
            work = []
            for pi, r, u, rho in tiles[b0:b0 + TILE_BATCH]:
                rows = _rows(r, u, rho)
                if u > 0:
                    k_prev, v_prev, mask = k_own, v_own, mask_in
                else:
                    prows = _rows(r, GROUP // (SPAN * r) - 1, rho)
                    k_prev, v_prev, mask = kp_ref[prows, :].astype(BF16), vp_ref[prows, :].astype(BF16), mask_edge
                k_own, v_own = kc_ref[rows, :].astype(BF16), vc_ref[rows, :].astype(BF16)
                work.append((pi, rows, mask, qc_ref[rows, :].astype(BF16), jnp.concatenate([k_prev, k_own], axis=0),
                             jnp.concatenate([v_prev, v_own], axis=0)))
            scores = [_dot(qt, kcat, 1, 1) for _, _, _, qt, kcat, _ in work]
            soft = []
            for (_, _, mask, _, _, _), s in zip(work, scores):
                s = jnp.where(mask, s * SCALE, NEG)
                m = jnp.max(s, axis=1, keepdims=True)
                p = jnp.exp(s - m)
                soft.append((m, _b16(p), jnp.sum(p, axis=1, keepdims=True)))
            outs = [_dot(p, vcat, 1, 0) for (_, p, _), (_, _, _, _, _, vcat) in zip(soft, work)]
            for (pi, rows, _, _, _, _), (m, _, den), o in zip(work, soft, outs):
                o_scr[pi, rows, :] = o / den
                l_scr[pi, rows, :] = jnp.broadcast_to(m + jnp.log(den), (SPAN, HD))
        step = 256
        for c in range(GROUP // step):
            sl = pl.ds(c * step, step)
            ob, lse = _merge([o_scr[i, sl, :] for i in range(3)], [l_scr[i, sl, :] for i in range(3)])
            ob_ref[sl, :] = ob
            lse_ref[sl, :] = lse

    cur = pl.BlockSpec((GROUP, HD), lambda g, h: (g, h))
    prev = pl.BlockSpec((GROUP, HD), lambda g, h: (jnp.maximum(g - 1, 0), h))
    vcur = pl.BlockSpec((GROUP, HD), lambda g, h: (g, v_blk * N_HEADS + h))
    vprev = pl.BlockSpec((GROUP, HD), lambda g, h: (jnp.maximum(g - 1, 0), v_blk * N_HEADS + h))
    return pl.pallas_call(
        body, grid=(t // GROUP, N_HEADS), in_specs=[cur, cur, vcur, prev, vprev], out_specs=[cur, cur],
        out_shape=[_sds((t, GW), F32), _sds((t, GW), F32)],
        scratch_shapes=[pltpu.VMEM((3, GROUP, HD), F32), pltpu.VMEM((3, GROUP, HD), F32)], name=name,
        compiler_params=_params(2))(q, k, v, k, v)


def _attn_bwd(name, q, k, v, v_blk, do, lse, delta):
    t = q.shape[0]
    ng = t // GROUP

    def probs(work):
        scores = [_dot(qt, kcat, 1, 1) for qt, _, _, _, kcat, _, _ in work]
        dps = [_dot(dot, vcat, 1, 1) for _, dot, _, _, _, vcat, _ in work]
        out = []
        for (_, _, lt, dlt, kcat, _, mask), s, dp in zip(work, scores, dps):
            wide = kcat.shape[0] // SPAN
            lw = jnp.concatenate([lt] * wide, axis=1) if wide > 1 else lt
            dw = jnp.concatenate([dlt] * wide, axis=1) if wide > 1 else dlt
            p = jnp.exp(jnp.where(mask, s * SCALE - lw, NEG))
            out.append((_b16(p * (dp - dw) * SCALE), _b16(p)))
        return out

    def body(qc_ref, kc_ref, vc_ref, doc_ref, lc_ref, dc_ref, kp_ref, vp_ref, qn_ref, don_ref, ln_ref, dn_ref,
             dq_ref, dk_ref, dv_ref):
        g = pl.program_id(0)
        mask_in = _band_mask(0)
        mask_edge = _band_mask(jnp.where(g == 0, SPAN, 0))
        dk_ref[...] = jnp.zeros_like(dk_ref)
        dv_ref[...] = jnp.zeros_like(dv_ref)
        tiles = _tiles()
        k_own = v_own = None
        for b0 in range(0, len(tiles), TILE_BATCH):
            where, work = [], []
            for pi, r, u, rho in tiles[b0:b0 + TILE_BATCH]:
                rows = _rows(r, u, rho)
                if u > 0:
                    prows, k_prev, v_prev, mask = _rows(r, u - 1, rho), k_own, v_own, mask_in
                else:
                    prows = _rows(r, GROUP // (SPAN * r) - 1, rho)
                    k_prev, v_prev, mask = kp_ref[prows, :].astype(BF16), vp_ref[prows, :].astype(BF16), mask_edge
                k_own, v_own = kc_ref[rows, :].astype(BF16), vc_ref[rows, :].astype(BF16)
                where.append((pi, u, rows, prows))
                work.append((qc_ref[rows, :].astype(BF16), doc_ref[rows, :].astype(BF16), lc_ref[rows, :], dc_ref[rows, :],
                             jnp.concatenate([k_prev, k_own], axis=0), jnp.concatenate([v_prev, v_own], axis=0), mask))
            dsp = probs(work)
            dqs = [_dot(ds, w[4], 1, 0) for (ds, _), w in zip(dsp, work)]
            dks = [_dot(ds, w[0], 0, 0) for (ds, _), w in zip(dsp, work)]
            dvs = [_dot(p, w[1], 0, 0) for (_, p), w in zip(dsp, work)]
            for (pi, u, rows, prows), dq_t, dk2, dv2 in zip(where, dqs, dks, dvs):
                if pi == 0:
                    dq_ref[rows, :] = dq_t
                else:
                    dq_ref[rows, :] += dq_t
                dk_ref[rows, :] += dk2[SPAN:, :]
                dv_ref[rows, :] += dv2[SPAN:, :]
                if u > 0:
                    dk_ref[prows, :] += dk2[:SPAN, :]
                    dv_ref[prows, :] += dv2[:SPAN, :]
        qi = _iota2((SPAN, SPAN), 0)
        ki = _iota2((SPAN, SPAN), 1)
        mask_next = (ki >= qi) & (ki < jnp.where(g == ng - 1, 0, SPAN))
        edge = [(r, rho) for r in DILATIONS for rho in range(r)]
        for b0 in range(0, len(edge), TILE_BATCH):
            where, work = [], []
            for r, rho in edge[b0:b0 + TILE_BATCH]:
                krows, qrows = _rows(r, GROUP // (SPAN * r) - 1, rho), _rows(r, 0, rho)
                where.append(krows)
                work.append((qn_ref[qrows, :].astype(BF16), don_ref[qrows, :].astype(BF16), ln_ref[qrows, :],
                             dn_ref[qrows, :], kc_ref[krows, :].astype(BF16), vc_ref[krows, :].astype(BF16), mask_next))
            dsp = probs(work)
            dks = [_dot(ds, w[0], 0, 0) for (ds, _), w in zip(dsp, work)]
            dvs = [_dot(p, w[1], 0, 0) for (_, p), w in zip(dsp, work)]
            for krows, dk1, dv1 in zip(where, dks, dvs):
                dk_ref[krows, :] += dk1
                dv_ref[krows, :] += dv1

    cur = pl.BlockSpec((GROUP, HD), lambda g, h: (g, h))
    prev = pl.BlockSpec((GROUP, HD), lambda g, h: (jnp.maximum(g - 1, 0), h))
    nxt = pl.BlockSpec((GROUP, HD), lambda g, h: (jnp.minimum(g + 1, ng - 1), h))
    vcur = pl.BlockSpec((GROUP, HD), lambda g, h: (g, v_blk * N_HEADS + h))
    vprev = pl.BlockSpec((GROUP, HD), lambda g, h: (jnp.maximum(g - 1, 0), v_blk * N_HEADS + h))
    return pl.pallas_call(
        body, grid=(ng, N_HEADS), in_specs=[cur, cur, vcur, cur, cur, cur, prev, vprev] + [nxt] * 4,
        out_specs=[cur] * 3,
        out_shape=[_sds((t, GW), F32)] * 3, name=name,
        compiler_params=_params(2))(q, k, v, do, lse, delta, k, v, q, do, lse, delta)


def _merge(os_, ls_):
    m = jnp.maximum(jnp.maximum(ls_[0], ls_[1]), ls_[2])
    ws = [jnp.exp(l - m) for l in ls_]
    tot = ws[0] + ws[1] + ws[2]
    ob = (ws[0] * os_[0] + ws[1] * os_[1] + ws[2] * os_[2]) / tot
    return ob, m + jnp.log(tot)


def _gated_norm(oa, z, wv):
    return _head_rms(oa, wv) * _silu(z)


def _mix_fwd(name, oa_raw, proj, z_blk, ob, w_dn, w_an):
    t = oa_raw.shape[0]
    tm = min(256, t)

    def body(oa_ref, z_ref, ob_ref, wd_ref, wa_ref, mix_ref):
        for h in range(N_HEADS):
            sl = slice(h * HD, (h + 1) * HD)
            mix_ref[:, sl] = _gated_norm(oa_ref[:, sl], z_ref[:, sl], wd_ref[...]).astype(BF16)
            mix_ref[:, GW + h * HD:GW + (h + 1) * HD] = _head_rms(ob_ref[:, sl], wa_ref[...]).astype(BF16)

    vec = pl.BlockSpec((1, HD), lambda i: (0, 0))
    wide = pl.BlockSpec((tm, GW), lambda i: (i, 0))
    return pl.pallas_call(
        body, grid=(t // tm,),
        in_specs=[wide, pl.BlockSpec((tm, GW), lambda i: (i, z_blk)), wide, vec, vec],
        out_specs=pl.BlockSpec((tm, 2 * GW), lambda i: (i, 0)),
        out_shape=_sds((t, 2 * GW), BF16), name=name,
        compiler_params=_params(1))(oa_raw, proj, ob, w_dn, w_an)


def _mix_bwd(name, dmixed, oa_raw, proj, z_blk, ob, w_dn, w_an, dep):
    t = oa_raw.shape[0]
    tm = min(256, t)

    def body(dm_ref, oa_ref, z_ref, ob_ref, wd_ref, wa_ref, dep_ref,
             doa_ref, dz_ref, dob_ref, dl_ref, dwd_ref, dwa_ref):
        dwd = jnp.zeros((1, HD), F32)
        dwa = jnp.zeros((1, HD), F32)
        for h in range(N_HEADS):
            sl = slice(h * HD, (h + 1) * HD)
            _, vjp = jax.vjp(_gated_norm, oa_ref[:, sl], z_ref[:, sl], wd_ref[...])
            doa, dz, dw1 = vjp(dm_ref[:, sl])
            doa_ref[:, sl] = doa
            dz_ref[:, sl] = dz.astype(BF16)
            dwd = dwd + dw1
            obh = ob_ref[:, sl]
            _, vjp2 = jax.vjp(_head_rms, obh, wa_ref[...])
            dob, dw2 = vjp2(dm_ref[:, GW + h * HD:GW + (h + 1) * HD])
            dwa = dwa + dw2
            dob_ref[:, sl] = dob
            dl_ref[:, sl] = jnp.broadcast_to(jnp.sum(dob * obh, axis=1, keepdims=True), (tm, HD))

        @pl.when(pl.program_id(0) == 0)
        def _():
            dwd_ref[...] = jnp.zeros_like(dwd_ref)
            dwa_ref[...] = jnp.zeros_like(dwa_ref)

        dwd_ref[...] += dwd
        dwa_ref[...] += dwa

    vec = pl.BlockSpec((1, HD), lambda i: (0, 0))
    wide = pl.BlockSpec((tm, GW), lambda i: (i, 0))
    return pl.pallas_call(
        body, grid=(t // tm,),
        in_specs=[pl.BlockSpec((tm, 2 * GW), lambda i: (i, 0)), wide, pl.BlockSpec((tm, GW), lambda i: (i, z_blk)),
                  wide, vec, vec, ANY],
        out_specs=[wide, pl.BlockSpec((tm, GW), lambda i: (i, z_blk)), wide, wide, vec, vec],
        out_shape=[_sds((t, GW), F32), _sds(proj.shape, BF16), _sds((t, GW), F32), _sds((t, GW), F32),
                   _sds((1, HD), F32), _sds((1, HD), F32)], name=name,
        compiler_params=_params(1))(dmixed, oa_raw, proj, ob, w_dn, w_an, dep)


def _halves(n):
    cut = (n // 256) * 128
    return [(0, cut), (cut, n)]


def _gate_up_swiglu(name, h2, w_gu_g):
    t, d = h2.shape
    n = w_gu_g.shape[2]
    per = N_DEV // 2
    tm = min(512, t)

    def body(a_ref, bg_ref, bu_ref, gu_ref, act_ref):
        a = a_ref[...]
        cuts = _halves(n)
        gs = [_dot(a, bg_ref[:, c0:c1], 1, 0) for c0, c1 in cuts]
        ups = [_dot(a, bu_ref[:, c0:c1], 1, 0) for c0, c1 in cuts]
        for (c0, c1), g, up in zip(cuts, gs, ups):
            gu_ref[0, :, c0:c1] = g.astype(BF16)
            gu_ref[1, :, c0:c1] = up.astype(BF16)
            act_ref[:, c0:c1] = (_silu(g) * up).astype(BF16)

    return pl.pallas_call(
        body, grid=(per, t // tm),
        in_specs=[pl.BlockSpec((tm, d), lambda j, i: (i, 0)), pl.BlockSpec((None, d, n), lambda j, i: (j, 0, 0)),
                  pl.BlockSpec((None, d, n), lambda j, i: (j + per, 0, 0))],
        out_specs=[pl.BlockSpec((2, tm, n), lambda j, i: (0, i, j)), pl.BlockSpec((tm, n), lambda j, i: (i, j))],
        out_shape=[_sds((2, t, per * n), BF16), _sds((t, per * n), BF16)], name=name,
        compiler_params=_params(2))(h2, w_gu_g, w_gu_g)


def _d_gate_up(name, dy16, w_down, gu3, dep):
    t, d = dy16.shape
    f = w_down.shape[0]
    tm, tn = min(1024, t), f // 4

    def body(a_ref, b_ref, g_ref, dep_ref, o_ref):
        a = a_ref[...]
        cuts = _halves(tn)
        dacts = [_dot(a, b_ref[c0:c1, :], 1, 1) for c0, c1 in cuts]
        for (c0, c1), dact in zip(cuts, dacts):
            g, up = g_ref[0, :, c0:c1].astype(F32), g_ref[1, :, c0:c1].astype(F32)
            sg = _sigmoid(g)
            o_ref[0, :, c0:c1] = (dact * up * sg * (1.0 + g * (1.0 - sg))).astype(BF16)
            o_ref[1, :, c0:c1] = (dact * g * sg).astype(BF16)

    return pl.pallas_call(
        body, grid=(f // tn, t // tm),
        in_specs=[pl.BlockSpec((tm, d), lambda j, i: (i, 0)), pl.BlockSpec((tn, d), lambda j, i: (j, 0)),
                  pl.BlockSpec((2, tm, tn), lambda j, i: (0, i, j)), ANY],
        out_specs=pl.BlockSpec((2, tm, tn), lambda j, i: (0, i, j)), out_shape=_sds((2, t, f), BF16), name=name,
        compiler_params=_params(2))(dy16, w_down, gu3, dep)


def _d_h2(name, dgu3, w_gu_g, dep):
    _, t, f = dgu3.shape
    n_dev, d, n = w_gu_g.shape
    per = n_dev // 2
    tm, tn = min(512, t), 512

    def body(g_ref, u_ref, b_ref, dep_ref, o_ref):
        acc = None
        for s in range(n_dev):
            a_ref = g_ref if s < per else u_ref
            part = _dot(a_ref[:, (s % per) * n:(s % per + 1) * n], b_ref[s], 1, 1)
            acc = part if acc is None else acc + part
        o_ref[...] = acc

    return pl.pallas_call(
        body, grid=(d // tn, t // tm),
        in_specs=[pl.BlockSpec((None, tm, f), lambda j, i: (0, i, 0)), pl.BlockSpec((None, tm, f), lambda j, i: (1, i, 0)),
                  pl.BlockSpec((n_dev, tn, n), lambda j, i: (0, j, 0)), ANY],
        out_specs=pl.BlockSpec((tm, tn), lambda j, i: (i, j)), out_shape=_sds((t, d), F32), name=name,
        compiler_params=_params(2))(dgu3, dgu3, w_gu_g, dep)


def _out_proj_norm(name, mixed, w_out, x, w_norm):
    t, d = x.shape
    kdim = mixed.shape[1]
    tm = min(512, t)

    def body(a_ref, b_ref, x_ref, w_ref, x1_ref, h_ref):
        x1 = x_ref[...] + _dot(a_ref[...], b_ref[...], 1, 0)
        x1_ref[...] = x1
        h_ref[...] = _rms_f(x1, w_ref[...]).astype(BF16)

    row = pl.BlockSpec((tm, d), lambda i: (i, 0))
    return pl.pallas_call(
        body, grid=(t // tm,),
        in_specs=[pl.BlockSpec((tm, kdim), lambda i: (i, 0)), pl.BlockSpec((kdim, d), lambda i: (0, 0)), row,
                  pl.BlockSpec((1, d), lambda i: (0, 0))],
        out_specs=[row, row], out_shape=[_sds((t, d), F32), _sds((t, d), BF16)], name=name,
        compiler_params=_params(1))(mixed, w_out, x, w_norm)


def _down_loss(name, act, w_down, x1, target):
    t, f = act.shape
    d = x1.shape[1]
    tm, tn = min(1024, t), 512

    def body(a_ref, b_ref, x_ref, t_ref, dy_ref, dy16_ref, l_ref):
        diff = _dot(a_ref[...], b_ref[...], 1, 0) + x_ref[...] - t_ref[...]
        dyv = diff * (1.0 / d)
        dy_ref[...] = dyv
        dy16_ref[...] = dyv.astype(BF16)
        tot = jnp.sum(jnp.sum(diff * diff, axis=1, keepdims=True), axis=0, keepdims=True) * (0.5 / d)

        @pl.when((pl.program_id(0) == 0) & (pl.program_id(1) == 0))
        def _():
            l_ref[...] = jnp.zeros_like(l_ref)

        l_ref[...] += jnp.broadcast_to(tot, (8, 128))

    tile = pl.BlockSpec((tm, tn), lambda i, j: (i, j))
    return pl.pallas_call(
        body, grid=(t // tm, d // tn),
        in_specs=[pl.BlockSpec((tm, f), lambda i, j: (i, 0)), pl.BlockSpec((f, tn), lambda i, j: (0, j)), tile, tile],
        out_specs=[tile, tile, pl.BlockSpec((8, 128), lambda i, j: (0, 0))],
        out_shape=[_sds((t, d), F32), _sds((t, d), BF16), _sds((8, 128), F32)], name=name,
        compiler_params=_params(2))(act, w_down, x1, target)


def _peer(me, k):
    pid = (me + k) % N_DEV
    return (pid // 4, (pid // 2) % 2, pid % 2)


def _my_id():
    return 4 * lax.axis_index("x") + 2 * lax.axis_index("y") + lax.axis_index("c")


def _exchange(name, arrays, scatter, dep):
    n = len(arrays)

    def body(*refs):
        ins, outs = refs[:n], refs[n + 1:2 * n + 1]
        send_sems, recv_sems, local_sems = refs[2 * n + 1:]
        me = _my_id()
        started = []
        for a in range(n):
            src = ins[a].at[me] if scatter[a] else ins[a]
            loc = pltpu.make_async_copy(src, outs[a].at[me], local_sems.at[a])
            loc.start()
            started.append(loc)
        remote = []
        for k in range(1, N_DEV):
            to = (me + k) % N_DEV
            for a in range(n):
                src = ins[a].at[to] if scatter[a] else ins[a]
                cp = pltpu.make_async_remote_copy(src_ref=src, dst_ref=outs[a].at[me],
                                                  send_sem=send_sems.at[a * (N_DEV - 1) + k - 1], recv_sem=recv_sems.at[a * (N_DEV - 1) + k - 1],
                                                  device_id=_peer(me, k), device_id_type=pl.DeviceIdType.MESH)
                cp.start()
                remote.append(cp)
        for k in range(1, N_DEV):
            frm = (me + N_DEV - k) % N_DEV
            for a in range(n):
                src = ins[a].at[frm] if scatter[a] else ins[a]
                pltpu.make_async_remote_copy(src_ref=src, dst_ref=outs[a].at[frm],
                                             send_sem=send_sems.at[a * (N_DEV - 1) + k - 1], recv_sem=recv_sems.at[a * (N_DEV - 1) + k - 1],
                                             device_id=_peer(me, k), device_id_type=pl.DeviceIdType.MESH).wait_recv()
        for cp in remote:
            cp.wait_send()
        for loc in started:
            loc.wait()

    out_shape = [_sds((N_DEV,) + (a.shape[1:] if sc else a.shape), a.dtype) for a, sc in zip(arrays, scatter)]
    return pl.pallas_call(
        body, in_specs=[ANY] * (n + 1), out_specs=[ANY] * n, out_shape=out_shape,
        scratch_shapes=[pltpu.SemaphoreType.DMA((n * (N_DEV - 1),)), pltpu.SemaphoreType.DMA((n * (N_DEV - 1),)),
                        pltpu.SemaphoreType.DMA((n,))],
        name=name)(*arrays, dep)


def _gather_two_level(name, arrays):
    n = len(arrays)
    per = N_DEV - 1

    def body(*refs):
        ins, outs = refs[:n], refs[n:2 * n]
        send_sems, recv_sems, local_sems = refs[2 * n:]
        x, y, c = lax.axis_index("x"), lax.axis_index("y"), lax.axis_index("c")
        me, sibling = (x, y, c), (x, y, 1 - c)
        chips = [(1 - x, y), (x, 1 - y), (1 - x, 1 - y)]

        def copy(a, k, block, to, src=None):
            slot = outs[a].at[4 * block[0] + 2 * block[1] + block[2]]
            return pltpu.make_async_remote_copy(
                src_ref=slot if src is None else src, dst_ref=slot, send_sem=send_sems.at[a * per + k],
                recv_sem=recv_sems.at[a * per + k], device_id=to, device_id_type=pl.DeviceIdType.MESH)

        mine = [pltpu.make_async_copy(ins[a], outs[a].at[4 * x + 2 * y + c], local_sems.at[a]) for a in range(n)]
        for cp in mine:
            cp.start()
        first = [copy(a, 0, me, sibling, src=ins[a]) for a in range(n)]
        first += [copy(a, 1 + j, me, (*chip, c), src=ins[a]) for j, chip in enumerate(chips) for a in range(n)]
        for cp in first:
            cp.start()
        passed = []
        for j, chip in enumerate(chips):
            for a in range(n):
                copy(a, 1 + j, (*chip, c), me).wait_recv()
                cp = copy(a, 4 + j, (*chip, c), sibling)
                cp.start()
                passed.append(cp)
        for a in range(n):
            copy(a, 0, sibling, me).wait_recv()
            for j, chip in enumerate(chips):
                copy(a, 4 + j, (*chip, 1 - c), me).wait_recv()
        for cp in first + passed:
            cp.wait_send()
        for cp in mine:
            cp.wait()

    return pl.pallas_call(
        body, in_specs=[ANY] * n, out_specs=[ANY] * n,
        out_shape=[_sds((N_DEV,) + a.shape, a.dtype) for a in arrays],
        scratch_shapes=[pltpu.SemaphoreType.DMA((n * per,)), pltpu.SemaphoreType.DMA((n * per,)),
                        pltpu.SemaphoreType.DMA((n,))],
        name=name)(*arrays)


HBM = pl.BlockSpec(memory_space=pltpu.HBM)
SEM = pl.BlockSpec(memory_space=pltpu.SEMAPHORE)
EFFECT = pltpu.SideEffectType.DATAFLOW_SIDE_EFFECTING


def _remote_copies(srcs, lands, scatter, send_sems, recv_sems, me, incoming):
    out = []
    for k in range(1, N_DEV):
        other = (me + N_DEV - k) % N_DEV if incoming else (me + k) % N_DEV
        for a in range(len(srcs)):
            sem = a * (N_DEV - 1) + k - 1
            src = srcs[a].at[other] if scatter[a] else srcs[a]
            dst = lands[a].at[other if incoming else me]
            out.append(pltpu.make_async_remote_copy(src_ref=src, dst_ref=dst, send_sem=send_sems.at[sem],
                                                    recv_sem=recv_sems.at[sem], device_id=_peer(me, k),
                                                    device_id_type=pl.DeviceIdType.MESH))
    return out


def _exchange_start(name, arrays, scatter, dep):
    n = len(arrays)
    lands = [lax.empty((N_DEV,) + (a.shape[1:] if sc else a.shape), a.dtype) for a, sc in zip(arrays, scatter)]

    def body(*refs):
        srcs, land_refs = refs[:n], refs[n:2 * n]
        send_sems, recv_sems = refs[2 * n + 1], refs[2 * n + 2]
        token = refs[-1]
        for cp in _remote_copies(srcs, land_refs, scatter, send_sems, recv_sems, _my_id(), False):
            cp.start()
        token[...] = jnp.zeros_like(token)

    n_sem = n * (N_DEV - 1)
    out_shape = ([pltpu.SemaphoreType.DMA((n_sem,)), pltpu.SemaphoreType.DMA((n_sem,))]
                 + [pltpu.HBM(a.shape, a.dtype) for a in arrays] + [pltpu.HBM(l.shape, l.dtype) for l in lands]
                 + [_sds((8, 128), F32)])
    aliases = {i: 2 + i for i in range(2 * n)}
    args = [pltpu.with_memory_space_constraint(a, pltpu.HBM) for a in list(arrays) + lands] + [dep]
    res = pl.pallas_call(
        body, name=name, in_specs=[HBM] * (2 * n) + [ANY], out_shape=out_shape,
        out_specs=[SEM, SEM] + [HBM] * (2 * n) + [pl.BlockSpec(memory_space=pltpu.VMEM)],
        input_output_aliases=aliases, compiler_params=pltpu.CompilerParams(has_side_effects=EFFECT))(*args)
    return dict(send=res[0], recv=res[1], srcs=res[2:2 + n], lands=res[2 + n:2 + 2 * n], token=res[-1],
                scatter=scatter)


def _exchange_wait(name, started, after):
    n = len(started["srcs"])
    scatter = started["scatter"]

    def body(*refs):
        srcs, land_refs = refs[:n], refs[n:2 * n]
        send_sems, recv_sems = refs[2 * n], refs[2 * n + 1]
        me = _my_id()
        for cp in _remote_copies(srcs, land_refs, scatter, send_sems, recv_sems, me, False):
            cp.wait_send()
        for cp in _remote_copies(srcs, land_refs, scatter, send_sems, recv_sems, me, True):
            cp.wait_recv()

    arrs = list(started["srcs"]) + list(started["lands"])
    res = pl.pallas_call(
        body, name=name, in_specs=[HBM] * (2 * n) + [SEM, SEM, ANY],
        out_shape=[pltpu.HBM(a.shape, a.dtype) for a in arrs], out_specs=[HBM] * (2 * n),
        input_output_aliases={i: i for i in range(2 * n)},
        compiler_params=pltpu.CompilerParams(has_side_effects=EFFECT))(*arrs, started["send"], started["recv"], after)
    me = _my_id()
    out = []
    for src, land, sc in zip(res[:n], res[n:], scatter):
        own = lax.dynamic_index_in_dim(src, me, 0, keepdims=True) if sc else src[None]
        out.append(lax.dynamic_update_slice(land, own, (me,) + (0,) * (land.ndim - 1)))
    return out


def _adamw(name, parts, w, m, v):
    r, c = w.shape
    tr, tc = r, c
    if r % 8 == 0:
        tr = next(cand for cand in (128, 88, 64, 40, 8) if r % cand == 0)
    else:
        tc = 256
    c1 = 1.0 / (1.0 - ADAM_B1 ** ADAM_STEP)
    c2 = 1.0 / (1.0 - ADAM_B2 ** ADAM_STEP)

    def body(p_ref, w_ref, m_ref, v_ref, g_ref, d_ref, nm_ref, nv_ref):
        g = p_ref[0].astype(F32)
        for s in range(1, N_DEV):
            g = g + p_ref[s].astype(F32)
        mn = ADAM_B1 * m_ref[...] + (1.0 - ADAM_B1) * g
        vn = ADAM_B2 * v_ref[...] + (1.0 - ADAM_B2) * (g * g)
        g_ref[...] = g
        nm_ref[...] = mn
        nv_ref[...] = vn
        d_ref[...] = -ADAM_LR * ((mn * c1) / (jnp.sqrt(vn * c2) + ADAM_EPS) + ADAM_WD * w_ref[...])

    blk = pl.BlockSpec((tr, tc), lambda i, j: (i, j))
    return pl.pallas_call(
        body, grid=(r // tr, c // tc),
        in_specs=[pl.BlockSpec((N_DEV, tr, tc), lambda i, j: (0, i, j)), blk, blk, blk],
        out_specs=[blk] * 4, out_shape=[_sds((r, c), F32)] * 4, name=name,
        compiler_params=_params(2, VMEM_LIMIT))(parts, w, m, v)


def _pad_rows(a, rows):
    return jnp.pad(a, ((0, rows - a.shape[0]), (0, 0)))


def _lane_row(vec8, offset):
    return jnp.pad(vec8.reshape(1, 8), ((0, 0), (offset, HD - 8 - offset)))


def kernel(x, positions, attn_norm_w, w_in, conv_w, a_log, dt_bias, delta_out_norm_w, q_norm_w, k_norm_w, attn_out_norm_w, w_out, ffn_norm_w, w_gate_up, w_down, loss_target, m_attn_norm_w, m_w_in, m_conv_w, m_a_log, m_dt_bias, m_delta_out_norm_w, m_q_norm_w, m_k_norm_w, m_attn_out_norm_w, m_w_out, m_ffn_norm_w, m_w_gate_up, m_w_down, v_attn_norm_w, v_w_in, v_conv_w, v_a_log, v_dt_bias, v_delta_out_norm_w, v_q_norm_w, v_k_norm_w, v_attn_out_norm_w, v_w_out, v_ffn_norm_w, v_w_gate_up, v_w_down):
    x2 = x[0]
    t, d = x2.shape
    target = loss_target[0]
    pos_col = positions.reshape(t, 1)
    half = HD // 2
    inv = (ROPE_THETA ** (-np.arange(half, dtype=np.float32) / half)).astype(np.float32)
    inv_row = jnp.asarray(np.concatenate([inv, inv]).reshape(1, HD))

    n_in = w_in.shape[2]
    n_gu = w_gate_up.shape[2]
    w_in_g, conv_g = _gather_two_level("gather_in", [w_in[0].astype(BF16), _pad_rows(conv_w[0], 8)])
    out_fly = _exchange_start("gather_out_start", [w_out[0].astype(BF16)], [False], conv_g)
    gu_fly = _exchange_start("gather_gate_up_start", [w_gate_up[0].astype(BF16)], [False], out_fly["token"])
    down_fly = _exchange_start("gather_down_start", [w_down[0].astype(BF16)], [False], gu_fly["token"])
    n_main = 4 * GW
    n_small = 2 * N_HEADS
    segments = [(0, n_main, 0), (n_main + n_small, N_DEV * n_in, n_main), (n_main, n_main + n_small, 7 * GW)]
    pieces = []
    for lo, hi, _ in segments:
        f = lo
        while f < hi:
            j = f // n_in
            end = min(hi, (j + 1) * n_in)
            pieces.append(w_in_g[j][:, f - j * n_in:end - j * n_in])
            f = end
    w_cat = jnp.concatenate(pieces + [jnp.zeros((d, HD - n_small), BF16)], axis=1)
    n_cat = w_cat.shape[1]
    small_blk = (7 * GW) // HD
    conv_w8 =jnp.transpose(conv_g, (1, 0, 2)).reshape(8, 3 * GW)
    alog_row = _lane_row(a_log[0], 8)
    dtb_row = _lane_row(dt_bias[0], 8)

    tm = min(2048, t)
    h1 = _rms_fwd("norm1", x2, attn_norm_w, down_fly["token"])
    tmp, tnp = min(1024, t), n_cat // 3
    proj = _mm("in_proj", h1, w_cat, grid=(t // tmp, n_cat // tnp, 1),
               a_spec=pl.BlockSpec((tmp, d), lambda i, j, k: (i, 0)),
               b_spec=pl.BlockSpec((d, tnp), lambda i, j, k: (0, j)),
               o_spec=pl.BlockSpec((tmp, tnp), lambda i, j, k: (i, j)),
               out_shape=_sds((t, n_cat), F32), ca=1, cb=0, nk=1)
    qn = _conv_fwd("conv_q", proj, conv_w8, 0, True, HD ** -0.5)
    kn = _conv_fwd("conv_k", proj, conv_w8, 1, True, 1.0)
    vv = _conv_fwd("conv_v", proj, conv_w8, 2, False, 1.0)
    beta_b, gc_b = _gates_fwd("gates", proj, small_blk, alog_row, dtb_row)
    u, w, p, tinv, qd, kd = _delta_prep("delta_prep", qn, kn, vv, beta_b, gc_b)
    oa_raw, vn, s_hist = _delta_scan("delta_scan", u, w, p, qd, kd, gc_b)

    cos_t, sin_t = _rope_tables("rope_tables", pos_col, inv_row)
    aq, ak = _qk_fwd("attn_qk", proj, 2, q_norm_w, k_norm_w, cos_t, sin_t)
    ob, lse = _attn_fwd("attn_fwd", aq, ak, proj, 6)
    mixed = _mix_fwd("mix", oa_raw, proj, 3, ob, delta_out_norm_w, attn_out_norm_w)
    (w_out_g,) = _exchange_wait("gather_out_wait", out_fly, mixed)
    w_out_full = w_out_g.reshape(2 * GW, d)
    tn = 512
    x1, h2 = _out_proj_norm("out_proj", mixed, w_out_full, x2, ffn_norm_w)
    per = N_DEV // 2
    (w_gu_g,) = _exchange_wait("gather_gate_up_wait", gu_fly, h2)
    gu3, act = _gate_up_swiglu("gate_up", h2, w_gu_g)
    (w_down_g,) = _exchange_wait("gather_down_wait", down_fly, act)
    w_down_full = w_down_g.reshape(D_FF, d)
    tmd = min(1024, t)
    dy, dy16, loss_tile = _down_loss("down_proj", act, w_down_full, x1, target)
    loss = lax.psum(loss_tile[0, 0], ("x", "y", "c"))

    tk, nkt = t, 1
    g_down = _mm("g_down", act, dy16, dep=loss.reshape(1, 1), grid=(D_FF // 1408, d // 512, nkt),
                 a_spec=pl.BlockSpec((tk, 1408), lambda i, j, k: (k, i)),
                 b_spec=pl.BlockSpec((tk, 512), lambda i, j, k: (k, j)),
                 o_spec=pl.BlockSpec((1408, 512), lambda i, j, k: (i, j)),
                 out_shape=_sds((D_FF, d), F32), ca=0, cb=0, nk=nkt)
    down_g_fly = _exchange_start("reduce_down_start", [g_down.reshape(N_DEV, D_FF // N_DEV, d)], [True], dy16)
    dgu3 = _d_gate_up("d_gate_up", dy16, w_down_full, gu3, down_g_fly["token"])
    g_gu = _mm("g_gate_up", h2, dgu3, grid=(d // 512, N_DEV, nkt),
               a_spec=pl.BlockSpec((tk, 512), lambda i, j, k: (k, i)),
               b_spec=pl.BlockSpec((None, tk, n_gu), lambda i, j, k: (j // per, k, j % per)),
               o_spec=pl.BlockSpec((None, 512, n_gu), lambda i, j, k: (j, i, 0)),
               out_shape=_sds((N_DEV, d, n_gu), F32), ca=0, cb=0, nk=nkt)
    gu_g_fly = _exchange_start("reduce_gate_up_start", [g_gu], [True], dy16)
    dh2 = _d_h2("d_h2", dgu3, w_gu_g, gu_g_fly["token"])
    dx1, dx1_16, g_ffn_norm = _rms_bwd("norm2_bwd", x1, ffn_norm_w, dh2, dy)

    g_out = _mm("g_out", mixed, dx1_16, grid=((2 * GW) // 512, 1, nkt),
                a_spec=pl.BlockSpec((tk, 512), lambda i, j, k: (k, i)),
                b_spec=pl.BlockSpec((tk, d), lambda i, j, k: (k, 0)),
                o_spec=pl.BlockSpec((512, d), lambda i, j, k: (i, 0)),
                out_shape=_sds((2 * GW, d), F32), ca=0, cb=0, nk=nkt)
    out_g_fly = _exchange_start("reduce_out_start", [g_out.reshape(N_DEV, (2 * GW) // N_DEV, d)], [True], g_ffn_norm)
    dmixed = _mm("d_mixed", dx1_16, w_out_full, dep=out_g_fly["token"], grid=(t // tm, (2 * GW) // tn, 1),
                 a_spec=pl.BlockSpec((tm, d), lambda i, j, k: (i, 0)),
                 b_spec=pl.BlockSpec((tn, d), lambda i, j, k: (j, 0)),
                 o_spec=pl.BlockSpec((tm, tn), lambda i, j, k: (i, j)),
                 out_shape=_sds((t, 2 * GW), F32), ca=1, cb=1, nk=1)
    doa, dproj, dob, delta, g_dn, g_an = _mix_bwd("mix_bwd", dmixed, oa_raw, proj, 3, ob,
                                                  delta_out_norm_w, attn_out_norm_w, out_g_fly["token"])
    d_aq, d_ak, d_av = _attn_bwd("attn_bwd", aq, ak, proj, 6, dob, lse, delta)
    dproj, g_qn, g_kn = _qk_bwd("attn_qk_bwd", proj, 2, q_norm_w, k_norm_w, cos_t, sin_t, d_aq, d_ak, dproj)
    dproj = _cast_into("attn_v_bwd", d_av, dproj, 6)

    dvn, dqd, dkd, dw, ddec = _delta_scan_bwd("delta_scan_bwd", doa, w, p, qd, kd, gc_b, vn, s_hist)
    dqn, dkn, dvv, dbeta_b, dg_b = _delta_prep_bwd("delta_prep_bwd", qn, kn, vv, beta_b, gc_b, tinv, u, w, vn,
                                                   doa, dvn, dqd, dkd, dw, ddec)
    dproj, gcw_q = _conv_bwd("conv_q_bwd", proj, conv_w8, dqn, dproj, 0, True, HD ** -0.5)
    dproj, gcw_k = _conv_bwd("conv_k_bwd", proj, conv_w8, dkn, dproj, 1, True, 1.0)
    dproj, gcw_v = _conv_bwd("conv_v_bwd", proj, conv_w8, dvv, dproj, 2, False, 1.0)
    dproj, g_alog_row, g_dtb_row = _gates_bwd("gates_bwd", proj, small_blk, alog_row, dtb_row, dbeta_b, dg_b, dproj)
    tmc = 384
    g_cat = _mm("g_in", dproj, h1, grid=(n_cat // tmc, 1, nkt),
                a_spec=pl.BlockSpec((tk, tmc), lambda i, j, k: (k, i)),
                b_spec=pl.BlockSpec((tk, d), lambda i, j, k: (k, 0)),
                o_spec=pl.BlockSpec((tmc, d), lambda i, j, k: (i, 0)),
                out_shape=_sds((n_cat, d), BF16), ca=0, cb=0, nk=nkt)
    parts = []
    for j in range(N_DEV):
        cols = []
        for lo, hi, start in sorted(segments):
            a, b = max(lo, j * n_in), min(hi, (j + 1) * n_in)
            if a < b:
                cols.append(g_cat[start + a - lo:start + b - lo])
        parts.append(cols[0] if len(cols) == 1 else jnp.concatenate(cols, axis=0))
    g_in_parts = jnp.stack(parts)
    g_conv = jnp.concatenate([gcw_q, gcw_k, gcw_v], axis=1)
    n_cw = conv_w.shape[2]
    g_conv_parts = jnp.transpose(g_conv.reshape(8, N_DEV, n_cw), (1, 0, 2))
    in_g_fly = _exchange_start("reduce_in_start", [g_in_parts, g_conv_parts], [True] * 2, g_dtb_row)
    tmh1 = min(512, t)
    dh1 = _mm("d_h1", dproj, w_cat, dep=in_g_fly["token"], grid=(t // tmh1, d // 1024, 1),
              a_spec=pl.BlockSpec((tmh1, n_cat), lambda i, j, k: (i, 0)),
              b_spec=pl.BlockSpec((1024, n_cat), lambda i, j, k: (j, 0)),
              o_spec=pl.BlockSpec((tmh1, 1024), lambda i, j, k: (i, j)),
              out_shape=_sds((t, d), F32), ca=1, cb=1, nk=1)
    grad_x, _, g_attn_norm = _rms_bwd("norm1_bwd", x2, attn_norm_w, dh1, dx1)

    small_rows = [g_attn_norm.reshape(d // HD, HD), g_ffn_norm.reshape(d // HD, HD), g_dn, g_qn, g_kn, g_an,
                  g_alog_row, g_dtb_row]
    small_pack = _pad_rows(jnp.concatenate(small_rows, axis=0), 40)
    (r_down,) = _exchange_wait("reduce_down_wait", down_g_fly, grad_x)
    (r_gu,) = _exchange_wait("reduce_gate_up_wait", gu_g_fly, grad_x)
    (r_out,) = _exchange_wait("reduce_out_wait", out_g_fly, grad_x)
    res_gu = [a[None] for a in _adamw("adamw_gate_up", r_gu, w_gate_up[0], m_w_gate_up[0], v_w_gate_up[0])]
    res_down = [a[None] for a in _adamw("adamw_down", r_down, w_down[0], m_w_down[0], v_w_down[0])]
    res_out = [a[None] for a in _adamw("adamw_out", r_out, w_out[0], m_w_out[0], v_w_out[0])]
    done = (res_gu[3][0, :1, :1] + res_down[3][0, :1, :1] + res_out[3][0, :1, :1])
    (r_small,) = _exchange("gather_small_grads", [small_pack], [False], done)

    def pack_small(an, fn, dn, qn_, kn_, aon, al, db):
        rows = [an.reshape(d // HD, HD), fn.reshape(d // HD, HD), dn, qn_, kn_, aon,
                _lane_row(al[0], 8), _lane_row(db[0], 8)]
        return _pad_rows(jnp.concatenate(rows, axis=0), 40)

    def unpack_small(pk):
        nr = d // HD
        return dict(attn_norm_w=pk[:nr].reshape(1, d), ffn_norm_w=pk[nr:2 * nr].reshape(1, d),
                    delta_out_norm_w=pk[2 * nr:2 * nr + 1], q_norm_w=pk[2 * nr + 1:2 * nr + 2],
                    k_norm_w=pk[2 * nr + 2:2 * nr + 3], attn_out_norm_w=pk[2 * nr + 3:2 * nr + 4],
                    a_log=pk[2 * nr + 4:2 * nr + 5, 8:16], dt_bias=pk[2 * nr + 5:2 * nr + 6, 8:16])

    res_small = _adamw("adamw_small", r_small,
                       pack_small(attn_norm_w, ffn_norm_w, delta_out_norm_w, q_norm_w, k_norm_w, attn_out_norm_w, a_log, dt_bias),
                       pack_small(m_attn_norm_w, m_ffn_norm_w, m_delta_out_norm_w, m_q_norm_w, m_k_norm_w, m_attn_out_norm_w, m_a_log, m_dt_bias),
                       pack_small(v_attn_norm_w, v_ffn_norm_w, v_delta_out_norm_w, v_q_norm_w, v_k_norm_w, v_attn_out_norm_w, v_a_log, v_dt_bias))
    small = [unpack_small(a) for a in res_small]
    r_in, r_conv = _exchange_wait("reduce_in_wait", in_g_fly, res_small[0])
    res_in = [jnp.transpose(a)[None] for a in _adamw("adamw_in", r_in, jnp.transpose(w_in[0]), jnp.transpose(m_w_in[0]),
                                                     jnp.transpose(v_w_in[0]))]
    res_conv =[a[None, :4] for a in _adamw("adamw_conv", r_conv, _pad_rows(conv_w[0], 8), _pad_rows(m_conv_w[0], 8),
                                            _pad_rows(v_conv_w[0], 8))]

    outs = [loss, grad_x[None]]
    for i in range(4):
        s = small[i]
        outs += [s["attn_norm_w"], res_in[i], res_conv[i], s["a_log"], s["dt_bias"], s["delta_out_norm_w"],
                 s["q_norm_w"], s["k_norm_w"], s["attn_out_norm_w"], res_out[i], s["ffn_norm_w"], res_gu[i],
                 res_down[i]]
    return tuple(outs)
```

```python
import numpy as np
import jax
import jax.numpy as jnp
from jax import lax
from jax.experimental import pallas as pl
from jax.experimental.pallas import tpu as pltpu

F32 = jnp.float32
BF16 = jnp.bfloat16

N_DEV = 8
N_HEADS = 8
HD = 128
GW = N_HEADS * HD
CHUNK = 64
PAIR = 2 * CHUNK
SCAN_CHUNKS = 4
SCAN_ROWS = SCAN_CHUNKS * CHUNK
SPAN = 128
DILATIONS = (1, 4, 16)
ROPE_THETA = 10000.0
EPS = 1e-6
D_FF = 5632
ADAM_LR, ADAM_B1, ADAM_B2, ADAM_EPS, ADAM_WD, ADAM_STEP = 0.001, 0.9, 0.999, 1e-8, 0.01, 10
NEG = -1e30
VMEM_LIMIT = 56 * 1024 * 1024
ANY = pl.BlockSpec(memory_space=pl.ANY)
HEADS_PER_STEP = 8


def _params(n_grid, vmem=VMEM_LIMIT):
    return pltpu.CompilerParams(dimension_semantics=("arbitrary",) * n_grid, vmem_limit_bytes=vmem)


def _sds(shape, dtype):
    return jax.ShapeDtypeStruct(tuple(shape), dtype)


def _sigmoid(x):
    return 1.0 / (1.0 + jnp.exp(-x))


def _silu(x):
    return x * _sigmoid(x)


def _softplus(x):
    return jnp.maximum(x, 0.0) + jnp.log(1.0 + jnp.exp(-jnp.abs(x)))


def _dot(a, b, ca, cb):
    return lax.dot_general(a, b, (((ca,), (cb,)), ((), ())), preferred_element_type=F32)


def _b16(x):
    return x if x.dtype == BF16 else x.astype(BF16)


def _split(x):
    hi = x.astype(BF16)
    return hi, (x - hi.astype(F32)).astype(BF16)


def _dot3(a, b, ca, cb):
    a_hi, a_lo = _split(a)
    b_hi, b_lo = _split(b)
    return _dot(a_hi, b_hi, ca, cb) + (_dot(a_hi, b_lo, ca, cb) + _dot(a_lo, b_hi, ca, cb))


def _iota2(shape, axis):
    return lax.broadcasted_iota(jnp.int32, shape, axis)


def _mm(name, a, b, *, grid, a_spec, b_spec, o_spec, out_shape, ca, cb, nk, dep=None):
    assert nk == 1 and grid[2] == 1

    def body(*refs):
        refs[-1][...] = _dot(_b16(refs[0][...]), _b16(refs[1][...]), ca, cb).astype(refs[-1].dtype)

    in_specs = [a_spec, b_spec] + ([ANY] if dep is not None else [])
    args = (a, b) + ((dep,) if dep is not None else ())
    return pl.pallas_call(body, grid=grid, in_specs=in_specs, out_specs=o_spec, out_shape=out_shape,
                          name=name, compiler_params=_params(3))(*args)


def _rms_f(xv, wv):
    return xv * lax.rsqrt(jnp.mean(xv * xv, axis=-1, keepdims=True) + EPS) * wv


def _rms_fwd(name, x, w, dep):
    t, d = x.shape
    tm = min(512, t)

    def body(x_ref, w_ref, dep_ref, o_ref):
        o_ref[...] = _rms_f(x_ref[...], w_ref[...]).astype(BF16)

    row = pl.BlockSpec((tm, d), lambda i: (i, 0))
    vec = pl.BlockSpec((1, d), lambda i: (0, 0))
    return pl.pallas_call(body, grid=(t // tm,), in_specs=[row, vec, ANY], out_specs=row,
                          out_shape=_sds((t, d), BF16), name=name, compiler_params=_params(1))(x, w, dep)


def _rms_bwd(name, x, w, dh, res):
    t, d = x.shape
    tm = min(256, t)

    def body(x_ref, w_ref, dh_ref, res_ref, dx_ref, dx16_ref, dw_ref):
        _, vjp = jax.vjp(_rms_f, x_ref[...], w_ref[...])
        dxv, dwv = vjp(dh_ref[...])
        dxv = dxv + res_ref[...]
        dx_ref[...] = dxv
        dx16_ref[...] = dxv.astype(BF16)

        @pl.when(pl.program_id(0) == 0)
        def _():
            dw_ref[...] = jnp.zeros_like(dw_ref)

        dw_ref[...] += dwv

    row = pl.BlockSpec((tm, d), lambda i: (i, 0))
    vec = pl.BlockSpec((1, d), lambda i: (0, 0))
    return pl.pallas_call(body, grid=(t // tm,), in_specs=[row, vec, row, row], out_specs=[row, row, vec],
                          out_shape=[_sds((t, d), F32), _sds((t, d), BF16), _sds((1, d), F32)], name=name,
                          compiler_params=_params(1))(x, w, dh, res)


def _shift_rows(x, s):
    t = x.shape[0]
    r = pltpu.roll(x, s % t, 0)
    row8 = _iota2((8, x.shape[1]), 0)
    if s > 0:
        return jnp.concatenate([jnp.where(row8 >= s, r[:8], 0.0), r[8:]], axis=0)
    return jnp.concatenate([r[:t - 8], jnp.where(row8 < 8 + s, r[t - 8:], 0.0)], axis=0)


def _conv_taps(xv, w_ref):
    c = w_ref[3:4, :] * xv
    for s in (1, 2, 3):
        c = c + w_ref[3 - s:4 - s, :] * _shift_rows(xv, s)
    return c


def _post_conv(c, l2, scale):
    y = _silu(c)
    if l2:
        y = y * lax.rsqrt(jnp.sum(y * y, axis=-1, keepdims=True) + EPS) * scale
    return y


def _conv_fwd(name, proj, conv_w8, group, l2, scale):
    t = proj.shape[0]

    def body(x_ref, w_ref, o_ref):
        o_ref[...] = _post_conv(_conv_taps(x_ref[...], w_ref), l2, scale)

    return pl.pallas_call(
        body, grid=(N_HEADS,),
        in_specs=[pl.BlockSpec((t, HD), lambda h: (0, h + group * N_HEADS)),
                  pl.BlockSpec((8, HD), lambda h: (0, h + group * N_HEADS))],
        out_specs=pl.BlockSpec((t, HD), lambda h: (0, h)),
        out_shape=_sds((t, GW), F32), name=name, compiler_params=_params(1, VMEM_LIMIT))(proj, conv_w8)


def _conv_bwd(name, proj, conv_w8, dn, dproj, group, l2, scale):
    t = proj.shape[0]

    def body(x_ref, w_ref, dn_ref, dproj_ref, dx_ref, dw_ref):
        xv = x_ref[...]
        c = _conv_taps(xv, w_ref)
        _, vjp = jax.vjp(lambda cc: _post_conv(cc, l2, scale), c)
        (dc,) = vjp(dn_ref[...])
        dx = w_ref[3:4, :] * dc
        dw = jnp.zeros((8, HD), F32)
        rid = _iota2((8, HD), 0)
        dw = dw + jnp.where(rid == 3, jnp.sum(dc * xv, axis=0, keepdims=True), 0.0)
        for s in (1, 2, 3):
            dx = dx + w_ref[3 - s:4 - s, :] * _shift_rows(dc, -s)
            dw = dw + jnp.where(rid == 3 - s, jnp.sum(dc * _shift_rows(xv, s), axis=0, keepdims=True), 0.0)
        dx_ref[...] = dx.astype(BF16)
        dw_ref[...] = dw

    return pl.pallas_call(
        body, grid=(N_HEADS,),
        in_specs=[pl.BlockSpec((t, HD), lambda h: (0, h + group * N_HEADS)),
                  pl.BlockSpec((8, HD), lambda h: (0, h + group * N_HEADS)),
                  pl.BlockSpec((t, HD), lambda h: (0, h)), ANY],
        out_specs=[pl.BlockSpec((t, HD), lambda h: (0, h + group * N_HEADS)), pl.BlockSpec((8, HD), lambda h: (0, h))],
        out_shape=[_sds(dproj.shape, BF16), _sds((8, GW), F32)], input_output_aliases={3: 0}, name=name,
        compiler_params=_params(1, VMEM_LIMIT))(proj, conv_w8, dn, dproj)


def _chunk_cumsum(g, rows):
    pos = rows % CHUNK
    s = 1
    while s < CHUNK:
        g = g + jnp.where(pos >= s, pltpu.roll(g, s, 0), 0.0)
        s *= 2
    return g


def _gates_fwd(name, proj, small_blk, alog_row, dtb_row):
    t = proj.shape[0]
    tm = min(256, t)

    def body(s_ref, a_ref, b_ref, beta_ref, gc_ref):
        sm = s_ref[...]
        beta = _sigmoid(sm)
        g = -jnp.exp(a_ref[...]) * _softplus(sm + b_ref[...])
        gc = _chunk_cumsum(g, _iota2((tm, HD), 0))
        lane = _iota2((tm, HD), 1)
        for h in range(N_HEADS):
            bcol = jnp.sum(jnp.where(lane == h, beta, 0.0), axis=1, keepdims=True)
            gcol = jnp.sum(jnp.where(lane == 8 + h, gc, 0.0), axis=1, keepdims=True)
            beta_ref[:, h * HD:(h + 1) * HD] = jnp.broadcast_to(bcol, (tm, HD))
            gc_ref[:, h * HD:(h + 1) * HD] = jnp.broadcast_to(gcol, (tm, HD))

    vec = pl.BlockSpec((1, HD), lambda i: (0, 0))
    wide = pl.BlockSpec((tm, GW), lambda i: (i, 0))
    return pl.pallas_call(
        body, grid=(t // tm,),
        in_specs=[pl.BlockSpec((tm, HD), lambda i: (i, small_blk)), vec, vec], out_specs=[wide, wide],
        out_shape=[_sds((t, GW), F32), _sds((t, GW), F32)], name=name,
        compiler_params=_params(1))(proj, alog_row, dtb_row)


def _gates_bwd(name, proj, small_blk, alog_row, dtb_row, dbeta_b, dg_b, dproj):
    t = proj.shape[0]
    tm = min(256, t)

    def body(s_ref, a_ref, b_ref, db_ref, dg_ref, dproj_ref, ds_ref, da_ref, dbias_ref):
        sm = s_ref[...]
        lane = _iota2((tm, HD), 1)
        db = jnp.zeros((tm, HD), F32)
        dg = jnp.zeros((tm, HD), F32)
        for h in range(N_HEADS):
            db = db + jnp.where(lane == h, db_ref[:, h * HD:(h + 1) * HD], 0.0)
            dg = dg + jnp.where(lane == 8 + h, dg_ref[:, h * HD:(h + 1) * HD], 0.0)
        beta = _sigmoid(sm)
        ea = jnp.exp(a_ref[...])
        pre = sm + b_ref[...]
        g = -ea * _softplus(pre)
        dpre = dg * (-ea) * _sigmoid(pre)
        ds_ref[...] = (db * beta * (1.0 - beta) + dpre).astype(BF16)

        @pl.when(pl.program_id(0) == 0)
        def _():
            da_ref[...] = jnp.zeros_like(da_ref)
            dbias_ref[...] = jnp.zeros_like(dbias_ref)

        da_ref[...] += jnp.sum(dg * g, axis=0, keepdims=True)
        dbias_ref[...] += jnp.sum(dpre, axis=0, keepdims=True)

    vec = pl.BlockSpec((1, HD), lambda i: (0, 0))
    wide = pl.BlockSpec((tm, GW), lambda i: (i, 0))
    return pl.pallas_call(
        body, grid=(t // tm,),
        in_specs=[pl.BlockSpec((tm, HD), lambda i: (i, small_blk)), vec, vec, wide, wide, ANY],
        out_specs=[pl.BlockSpec((tm, HD), lambda i: (i, small_blk)), vec, vec],
        out_shape=[_sds(dproj.shape, BF16), _sds((1, HD), F32), _sds((1, HD), F32)],
        input_output_aliases={5: 0}, name=name,
        compiler_params=_params(1))(proj, alog_row, dtb_row, dbeta_b, dg_b, dproj)


def _pair_masks():
    ii = _iota2((PAIR, PAIR), 0)
    jj = _iota2((PAIR, PAIR), 1)
    same = (ii // CHUNK) == (jj // CHUNK)
    return ii, jj, same & (ii >= jj), same & (ii > jj)


def _to_row(col_b, ii, jj):
    return jnp.sum(jnp.where(ii == jj, col_b, 0.0), axis=0, keepdims=True)


def _to_col(row, ii, jj):
    return jnp.sum(jnp.where(ii == jj, jnp.broadcast_to(row, (PAIR, PAIR)), 0.0), axis=1, keepdims=True)


def _decay_parts(gc, last_a, last_b, ii, jj, causal):
    diff = gc - _to_row(gc, ii, jj)
    dmat = jnp.where(causal, jnp.exp(jnp.where(causal, diff, 0.0)), 0.0)
    glast = jnp.where(ii < CHUNK, last_a, last_b)
    return dmat, jnp.exp(gc), jnp.exp(glast - gc)


def _unit_lower_inverse(lows, ii, jj):
    eye = jnp.where(ii == jj, 1.0, 0.0)
    mm = lambda xs, ys: [_dot3(a, b, 1, 0) for a, b in zip(xs, ys)]
    plus = lambda xs: [eye + a for a in xs]
    minus = lambda xs: [eye - a for a in xs]
    d1 = [jnp.where((ii // 16) == (jj // 16), low, 0.0) for low in lows]
    d2 = mm(d1, d1)
    a = mm(minus(d1), plus(d2))
    d4 = mm(d2, d2)
    a = mm(a, plus(d4))
    d8 = mm(d4, d4)
    td = mm(a, plus(d8))
    n1 = mm(td, [low - d for low, d in zip(lows, d1)])
    n2 = mm(n1, n1)
    return mm(mm(minus(n1), plus(n2)), td)


def _delta_prep(name, qn, kn, vv, beta_b, gc_b):
    t = qn.shape[0]

    def body(q_ref, k_ref, v_ref, b_ref, g_ref, u_ref, w_ref, p_ref, t_ref, qd_ref, kd_ref):
        ii, jj, causal, strict = _pair_masks()
        sls = [slice(hh * HD, (hh + 1) * HD) for hh in range(HEADS_PER_STEP)]
        lows = []
        for sl in sls:
            q, k, beta = q_ref[:, sl], k_ref[:, sl], b_ref[:, sl]
            dmat, gam, e2 = _decay_parts(g_ref[:, sl], g_ref[CHUNK - 1:CHUNK, sl], g_ref[PAIR - 1:PAIR, sl],
                                         ii, jj, causal)
            k16 = _b16(k)
            lows.append(jnp.where(strict, beta * _dot(k16, k16, 1, 1) * dmat, 0.0))
            p_ref[:, sl] = jnp.where(causal, _dot(_b16(q), k16, 1, 1) * dmat, 0.0).astype(BF16)
            qd_ref[:, sl] = (q * gam).astype(BF16)
            kd_ref[:, sl] = (k * e2).astype(BF16)
        for sl, tinv in zip(sls, _unit_lower_inverse(lows, ii, jj)):
            beta = b_ref[:, sl]
            t_ref[:, sl] = tinv
            u_ref[:, sl] = _dot3(tinv, v_ref[:, sl] * beta, 1, 0)
            w_ref[:, sl] = _dot3(tinv, k_ref[:, sl] * (beta * jnp.exp(g_ref[:, sl])), 1, 0).astype(BF16)

    blk = pl.BlockSpec((PAIR, HEADS_PER_STEP * HD), lambda i, h: (i, h))
    return pl.pallas_call(
        body, grid=(t // PAIR, N_HEADS // HEADS_PER_STEP), in_specs=[blk] * 5, out_specs=[blk] * 6,
        out_shape=[_sds((t, GW), F32), _sds((t, GW), BF16), _sds((t, GW), BF16), _sds((t, GW), F32),
                   _sds((t, GW), BF16), _sds((t, GW), BF16)],
        name=name, compiler_params=_params(2))(qn, kn, vv, beta_b, gc_b)


def _delta_scan(name, u, w, p, qd, kd, gc_b):
    t = u.shape[0]
    n = t // CHUNK

    def body(u_ref, w_ref, p_ref, qd_ref, kd_ref, g_ref, o_ref, vn_ref, sh_ref, state):
        @pl.when(pl.program_id(0) == 0)
        def _():
            state[...] = jnp.zeros_like(state)

        sls = [slice(h * HD, (h + 1) * HD) for h in range(N_HEADS)]
        heads = range(N_HEADS)
        s = [state[h] for h in heads]
        for c in range(SCAN_CHUNKS):
            rows = slice(c * CHUNK, (c + 1) * CHUNK)
            last = slice((c + 1) * CHUNK - 1, (c + 1) * CHUNK)
            for h in heads:
                sh_ref[c, h] = s[h]
            s16 = [_b16(a) for a in s]
            ws = [_dot(w_ref[rows, sls[h]], s16[h], 1, 0) for h in heads]
            qs = [_dot(qd_ref[rows, sls[h]], s16[h], 1, 0) for h in heads]
            vn16 = [_b16(u_ref[rows, sls[h]] - ws[h]) for h in heads]
            pv = [_dot(p_ref[rows, sls[h]], jnp.concatenate([vn16[h], vn16[h]], axis=0), 1, 0) for h in heads]
            kv = [_dot(kd_ref[rows, sls[h]], vn16[h], 0, 0) for h in heads]
            for h in heads:
                o_ref[rows, sls[h]] = qs[h] + pv[h]
                vn_ref[rows, sls[h]] = vn16[h]
            s = [s[h] * jnp.exp(g_ref[last, sls[h]]) + kv[h] for h in heads]
        for h in heads:
            state[h] = s[h]

    blk = pl.BlockSpec((SCAN_ROWS, GW), lambda i: (i, 0))
    return pl.pallas_call(
        body, grid=(t // SCAN_ROWS,), in_specs=[blk] * 6,
        out_specs=[blk, blk, pl.BlockSpec((SCAN_CHUNKS, N_HEADS, HD, HD), lambda i: (i, 0, 0, 0))],
        out_shape=[_sds((t, GW), F32), _sds((t, GW), BF16), _sds((n, N_HEADS, HD, HD), F32)],
        scratch_shapes=[pltpu.VMEM((N_HEADS, HD, HD), F32)], name=name,
        compiler_params=_params(1))(u, w, p, qd, kd, gc_b)


def _delta_scan_bwd(name, do, w, p, qd, kd, gc_b, vn, s_hist):
    t = do.shape[0]
    n = t // CHUNK

    def body(do_ref, w_ref, p_ref, qd_ref, kd_ref, g_ref, vn_ref, sh_ref,
             dvn_ref, dqd_ref, dkd_ref, dw_ref, ddec_ref, dstate):
        @pl.when(pl.program_id(0) == 0)
        def _():
            dstate[...] = jnp.zeros_like(dstate)

        sls = [slice(h * HD, (h + 1) * HD) for h in range(N_HEADS)]
        heads = range(N_HEADS)
        ds = [dstate[h] for h in heads]
        for c in reversed(range(SCAN_CHUNKS)):
            rows = slice(c * CHUNK, (c + 1) * CHUNK)
            last = slice((c + 1) * CHUNK - 1, (c + 1) * CHUNK)
            ds16 = [_b16(a) for a in ds]
            s16 = [_b16(sh_ref[c, h]) for h in heads]
            do16 = [_b16(do_ref[rows, sls[h]]) for h in heads]
            ptdo = [_dot(p_ref[rows, sls[h]], do16[h], 0, 0) for h in heads]
            kds = [_dot(kd_ref[rows, sls[h]], ds16[h], 1, 0) for h in heads]
            qdo = [_dot(qd_ref[rows, sls[h]], do16[h], 0, 0) for h in heads]
            for h in heads:
                dqd_ref[rows, sls[h]] = _dot(do16[h], s16[h], 1, 1)
                dkd_ref[rows, sls[h]] = _dot(vn_ref[rows, sls[h]], ds16[h], 1, 1)
            dvn = [ptdo[h][:CHUNK, :] + ptdo[h][CHUNK:, :] + kds[h] for h in heads]
            dvn16 = [_b16(a) for a in dvn]
            wdv = [_dot(w_ref[rows, sls[h]], dvn16[h], 0, 0) for h in heads]
            for h in heads:
                dvn_ref[rows, sls[h]] = dvn[h]
                dw_ref[rows, sls[h]] = -_dot(dvn16[h], s16[h], 1, 1)
                tot = jnp.sum(jnp.sum(sh_ref[c, h] * ds[h], axis=1, keepdims=True), axis=0, keepdims=True)
                ddec_ref[c * 8:(c + 1) * 8, sls[h]] = jnp.broadcast_to(tot, (8, HD))
            ds = [ds[h] * jnp.exp(g_ref[last, sls[h]]) + qdo[h] - wdv[h] for h in heads]
        for h in heads:
            dstate[h] = ds[h]

    npair = t // SCAN_ROWS
    blk = pl.BlockSpec((SCAN_ROWS, GW), lambda i: (npair - 1 - i, 0))
    return pl.pallas_call(
        body, grid=(npair,),
        in_specs=[blk] * 7 + [pl.BlockSpec((SCAN_CHUNKS, N_HEADS, HD, HD), lambda i: (npair - 1 - i, 0, 0, 0))],
        out_specs=[blk] * 4 + [pl.BlockSpec((8 * SCAN_CHUNKS, GW), lambda i: (npair - 1 - i, 0))],
        out_shape=[_sds((t, GW), F32)] * 4 + [_sds((n * 8, GW), F32)],
        scratch_shapes=[pltpu.VMEM((N_HEADS, HD, HD), F32)], name=name,
        compiler_params=_params(1))(do, w, p, qd, kd, gc_b, vn, s_hist)


def _delta_prep_bwd(name, qn, kn, vv, beta_b, gc_b, tinv, u, w, vn, do, dvn, dqd, dkd, dw, ddec):
    t = qn.shape[0]

    def body(q_ref, k_ref, v_ref, b_ref, g_ref, t_ref, u_ref, w_ref, vn_ref, do_ref, dvn_ref, dqd_ref,
             dkd_ref, dw_ref, ddec_ref, dq_ref, dk_ref, dv_ref, dbeta_ref, dg_ref):
        ii, jj, causal, strict = _pair_masks()
        suffix = ((ii // CHUNK) == (jj // CHUNK)) & (jj >= ii)
        first = ii < CHUNK
        rs = lambda a: jnp.sum(a, axis=1, keepdims=True)
        sls = [slice(hh * HD, (hh + 1) * HD) for hh in range(HEADS_PER_STEP)]
        xs = [_dot3(t_ref[:, sl], dvn_ref[:, sl], 0, 0) for sl in sls]
        ys = [_dot3(t_ref[:, sl], dw_ref[:, sl], 0, 0) for sl in sls]
        k16s = [_b16(k_ref[:, sl]) for sl in sls]
        kks = [_dot(k16, k16, 1, 1) for k16 in k16s]
        qks = [_dot(_b16(q_ref[:, sl]), k16, 1, 1) for sl, k16 in zip(sls, k16s)]
        dps = [jnp.where(causal, _dot(_b16(do_ref[:, sl]), vn_ref[:, sl], 1, 1), 0.0) for sl in sls]
        das = [-jnp.where(strict, _dot(_b16(x), _b16(u_ref[:, sl]), 1, 1) + _dot(_b16(y), w_ref[:, sl], 1, 1), 0.0)
               for sl, x, y in zip(sls, xs, ys)]
        for hh, sl in enumerate(sls):
            q, k, v, beta, gc = q_ref[:, sl], k_ref[:, sl], v_ref[:, sl], b_ref[:, sl], g_ref[:, sl]
            last_a, last_b = g_ref[CHUNK - 1:CHUNK, sl], g_ref[PAIR - 1:PAIR, sl]
            dmat, gam, e2 = _decay_parts(gc, last_a, last_b, ii, jj, causal)
            q16, k16 = _b16(q), k16s[hh]
            kk, qk, dp, x, y, da = kks[hh], qks[hh], dps[hh], xs[hh], ys[hh], das[hh]
            dqd, dkd = dqd_ref[:, sl], dkd_ref[:, sl]
            dpd16 = _b16(dp * dmat)
            dkk16 = _b16(da * beta * dmat)
            dq_ref[:, sl] = gam * dqd + _dot(dpd16, k16, 1, 0)
            dk_ref[:, sl] = (e2 * dkd + _dot(dpd16, q16, 0, 0) + beta * gam * y
                             + _dot(dkk16, k16, 1, 0) + _dot(dkk16, k16, 0, 0))
            dv_ref[:, sl] = beta * x
            dbeta = rs(v * x) + rs(k * gam * y) + rs(da * kk * dmat)
            dbeta_ref[:, sl] = jnp.broadcast_to(dbeta, (PAIR, HD))
            m = (dp * qk + da * beta * kk) * dmat
            dgam = rs(q * dqd) + rs(k * beta * y)
            de2 = rs(k * dkd)
            colsum = _to_col(jnp.sum(m, axis=0, keepdims=True), ii, jj)
            te2 = de2 * e2
            dgc = rs(m) - colsum + gam * dgam - te2
            tail_a = jnp.sum(jnp.where(first, te2, 0.0), axis=0, keepdims=True)
            tail_b = jnp.sum(jnp.where(first, 0.0, te2), axis=0, keepdims=True)
            dgc = dgc + jnp.where(ii == CHUNK - 1, tail_a + ddec_ref[0:1, sl] * jnp.exp(last_a), 0.0)
            dgc = dgc + jnp.where(ii == PAIR - 1, tail_b + ddec_ref[8:9, sl] * jnp.exp(last_b), 0.0)
            dgc_row = _to_row(dgc, ii, jj)
            dg = jnp.sum(jnp.where(suffix, jnp.broadcast_to(dgc_row, (PAIR, PAIR)), 0.0), axis=1, keepdims=True)
            dg_ref[:, sl] = jnp.broadcast_to(dg, (PAIR, HD))

    blk = pl.BlockSpec((PAIR, HEADS_PER_STEP * HD), lambda i, h: (i, h))
    return pl.pallas_call(
        body, grid=(t // PAIR, N_HEADS // HEADS_PER_STEP),
        in_specs=[blk] * 14 + [pl.BlockSpec((16, HEADS_PER_STEP * HD), lambda i, h: (i, h))], out_specs=[blk] * 5,
        out_shape=[_sds((t, GW), F32)] * 5, name=name,
        compiler_params=_params(2))(qn, kn, vv, beta_b, gc_b, tinv, u, w, vn, do, dvn, dqd, dkd, dw, ddec)


def _rope_tables(name, pos_col, inv_row):
    t = pos_col.shape[0]
    tm = min(1024, t)

    def body(pos_ref, inv_ref, cos_ref, sin_ref):
        ang = pos_ref[...].astype(F32) * inv_ref[...]
        lane = _iota2(ang.shape, 1)
        cos_ref[...] = jnp.cos(ang)
        sin_ref[...] = jnp.where(lane < HD // 2, -1.0, 1.0) * jnp.sin(ang)

    tab = pl.BlockSpec((tm, HD), lambda i: (i, 0))
    return pl.pallas_call(
        body, grid=(t // tm,), in_specs=[pl.BlockSpec((tm, 1), lambda i: (i, 0)), pl.BlockSpec((1, HD), lambda i: (0, 0))],
        out_specs=[tab, tab], out_shape=[_sds((t, HD), F32)] * 2, name=name,
        compiler_params=_params(1))(pos_col, inv_row)


def _head_rms(xh, wv):
    return xh * lax.rsqrt(jnp.mean(xh * xh, axis=-1, keepdims=True) + EPS) * wv


def _qk_fwd(name, proj, pair_blk, wq_row, wk_row, cos_t, sin_t):
    t = proj.shape[0]
    tm = min(256, t)

    def body(x_ref, wq_ref, wk_ref, cos_ref, sin_ref, q_ref, k_ref):
        cos, sin = cos_ref[...], sin_ref[...]
        for o_ref, w_ref, base in ((q_ref, wq_ref, 0), (k_ref, wk_ref, GW)):
            for h in range(N_HEADS):
                y = _head_rms(x_ref[:, base + h * HD:base + (h + 1) * HD], w_ref[...])
                o_ref[:, h * HD:(h + 1) * HD] = y * cos + pltpu.roll(y, HD // 2, 1) * sin

    vec = pl.BlockSpec((1, HD), lambda i: (0, 0))
    tab = pl.BlockSpec((tm, HD), lambda i: (i, 0))
    wide = pl.BlockSpec((tm, GW), lambda i: (i, 0))
    return pl.pallas_call(
        body, grid=(t // tm,),
        in_specs=[pl.BlockSpec((tm, 2 * GW), lambda i: (i, pair_blk)), vec, vec, tab, tab],
        out_specs=[wide, wide], out_shape=[_sds((t, GW), F32)] * 2, name=name,
        compiler_params=_params(1))(proj, wq_row, wk_row, cos_t, sin_t)


def _qk_bwd(name, proj, pair_blk, wq_row, wk_row, cos_t, sin_t, dq_full, dk_full, dproj):
    t = proj.shape[0]
    tm = min(256, t)

    def body(x_ref, wq_ref, wk_ref, cos_ref, sin_ref, dq_ref, dk_ref, dproj_ref, dx_ref, dwq_ref, dwk_ref):
        cos, sin = cos_ref[...], sin_ref[...]

        @pl.when(pl.program_id(0) == 0)
        def _():
            dwq_ref[...] = jnp.zeros_like(dwq_ref)
            dwk_ref[...] = jnp.zeros_like(dwk_ref)

        for dy_ref, w_ref, dw_ref, base in ((dq_ref, wq_ref, dwq_ref, 0), (dk_ref, wk_ref, dwk_ref, GW)):
            dw = jnp.zeros((1, HD), F32)
            for h in range(N_HEADS):
                dy = dy_ref[:, h * HD:(h + 1) * HD]
                dy = dy * cos - pltpu.roll(dy, HD // 2, 1) * sin
                _, vjp = jax.vjp(_head_rms, x_ref[:, base + h * HD:base + (h + 1) * HD], w_ref[...])
                dx, dwh = vjp(dy)
                dw = dw + dwh
                dx_ref[:, base + h * HD:base + (h + 1) * HD] = dx.astype(BF16)
            dw_ref[...] += dw

    vec = pl.BlockSpec((1, HD), lambda i: (0, 0))
    tab = pl.BlockSpec((tm, HD), lambda i: (i, 0))
    wide = pl.BlockSpec((tm, GW), lambda i: (i, 0))
    pair = pl.BlockSpec((tm, 2 * GW), lambda i: (i, pair_blk))
    return pl.pallas_call(
        body, grid=(t // tm,), in_specs=[pair, vec, vec, tab, tab, wide, wide, ANY],
        out_specs=[pair, vec, vec],
        out_shape=[_sds(dproj.shape, BF16), _sds((1, HD), F32), _sds((1, HD), F32)], input_output_aliases={7: 0},
        name=name, compiler_params=_params(1))(proj, wq_row, wk_row, cos_t, sin_t, dq_full, dk_full, dproj)


def _cast_into(name, x, dproj, blk_idx):
    t = x.shape[0]
    tm = min(512, t)

    def body(x_ref, dproj_ref, o_ref):
        o_ref[...] = x_ref[...].astype(BF16)

    return pl.pallas_call(
        body, grid=(t // tm,), in_specs=[pl.BlockSpec((tm, GW), lambda i: (i, 0)), ANY],
        out_specs=pl.BlockSpec((tm, GW), lambda i: (i, blk_idx)), out_shape=_sds(dproj.shape, BF16),
        input_output_aliases={1: 0}, name=name, compiler_params=_params(1))(x, dproj)


GROUP = SPAN * max(DILATIONS)
SCALE = HD ** -0.5
TILE_BATCH = 8


def _band_mask(lo):
    qi = _iota2((SPAN, 2 * SPAN), 0)
    ki = _iota2((SPAN, 2 * SPAN), 1)
    return (ki >= qi) & (ki <= qi + SPAN) & (ki >= lo)


def _tiles():
    return [(pi, r, u, rho) for pi, r in enumerate(DILATIONS) for rho in range(r) for u in range(GROUP // (SPAN * r))]


def _rows(r, u, rho):
    return pl.ds(u * SPAN * r + rho, SPAN, stride=r) if r > 1 else pl.ds(u * SPAN, SPAN)


def _attn_fwd(name, q, k, v, v_blk):
    t = q.shape[0]

    def body(qc_ref, kc_ref, vc_ref, kp_ref, vp_ref, ob_ref, lse_ref, o_scr, l_scr):
        mask_in = _band_mask(0)
        mask_edge = _band_mask(jnp.where(pl.program_id(0) == 0, SPAN, 0))
        tiles = _tiles()
        k_own = v_own = None
        for b0 in range(0, len(tiles), TILE_BATCH):
            work = []
            for pi, r, u, rho in tiles[b0:b0 + TILE_BATCH]:
                rows = _rows(r, u, rho)
                if u > 0:
                    k_prev, v_prev, mask = k_own, v_own, mask_in
                else:
                    prows = _rows(r, GROUP // (SPAN * r) - 1, rho)
                    k_prev, v_prev, mask = kp_ref[prows, :].astype(BF16), vp_ref[prows, :].astype(BF16), mask_edge
                k_own, v_own = kc_ref[rows, :].astype(BF16), vc_ref[rows, :].astype(BF16)
                work.append((pi, rows, mask, qc_ref[rows, :].astype(BF16), jnp.concatenate([k_prev, k_own], axis=0),
                             jnp.concatenate([v_prev, v_own], axis=0)))
            scores = [_dot(qt, kcat, 1, 1) for _, _, _, qt, kcat, _ in work]
            soft = []
            for (_, _, mask, _, _, _), s in zip(work, scores):
                s = jnp.where(mask, s * SCALE, NEG)
                m = jnp.max(s, axis=1, keepdims=True)
                p = jnp.exp(s - m)
                soft.append((m, _b16(p), jnp.sum(p, axis=1, keepdims=True)))
            outs = [_dot(p, vcat, 1, 0) for (_, p, _), (_, _, _, _, _, vcat) in zip(soft, work)]
            for (pi, rows, _, _, _, _), (m, _, den), o in zip(work, soft, outs):
                o_scr[pi, rows, :] = o / den
                l_scr[pi, rows, :] = jnp.broadcast_to(m + jnp.log(den), (SPAN, HD))
        step = 256
        for c in range(GROUP // step):
            sl = pl.ds(c * step, step)
            ob, lse = _merge([o_scr[i, sl, :] for i in range(3)], [l_scr[i, sl, :] for i in range(3)])
            ob_ref[sl, :] = ob
            lse_ref[sl, :] = lse

    cur = pl.BlockSpec((GROUP, HD), lambda g, h: (g, h))
    prev = pl.BlockSpec((GROUP, HD), lambda g, h: (jnp.maximum(g - 1, 0), h))
    vcur = pl.BlockSpec((GROUP, HD), lambda g, h: (g, v_blk * N_HEADS + h))
    vprev = pl.BlockSpec((GROUP, HD), lambda g, h: (jnp.maximum(g - 1, 0), v_blk * N_HEADS + h))
    return pl.pallas_call(
        body, grid=(t // GROUP, N_HEADS), in_specs=[cur, cur, vcur, prev, vprev], out_specs=[cur, cur],
        out_shape=[_sds((t, GW), F32), _sds((t, GW), F32)],
        scratch_shapes=[pltpu.VMEM((3, GROUP, HD), F32), pltpu.VMEM((3, GROUP, HD), F32)], name=name,
        compiler_params=_params(2))(q, k, v, k, v)


def _attn_bwd(name, q, k, v, v_blk, do, lse, delta):
    t = q.shape[0]
    ng = t // GROUP

    def probs(work):
        scores = [_dot(qt, kcat, 1, 1) for qt, _, _, _, kcat, _, _ in work]
        dps = [_dot(dot, vcat, 1, 1) for _, dot, _, _, _, vcat, _ in work]
        out = []
        for (_, _, lt, dlt, kcat, _, mask), s, dp in zip(work, scores, dps):
            wide = kcat.shape[0] // SPAN
            lw = jnp.concatenate([lt] * wide, axis=1) if wide > 1 else lt
            dw = jnp.concatenate([dlt] * wide, axis=1) if wide > 1 else dlt
            p = jnp.exp(jnp.where(mask, s * SCALE - lw, NEG))
            out.append((_b16(p * (dp - dw) * SCALE), _b16(p)))
        return out

    def body(qc_ref, kc_ref, vc_ref, doc_ref, lc_ref, dc_ref, kp_ref, vp_ref, qn_ref, don_ref, ln_ref, dn_ref,
             dq_ref, dk_ref, dv_ref):
        g = pl.program_id(0)
        mask_in = _band_mask(0)
        mask_edge = _band_mask(jnp.where(g == 0, SPAN, 0))
        dk_ref[...] = jnp.zeros_like(dk_ref)
        dv_ref[...] = jnp.zeros_like(dv_ref)
        tiles = _tiles()
        k_own = v_own = None
        for b0 in range(0, len(tiles), TILE_BATCH):
            where, work = [], []
            for pi, r, u, rho in tiles[b0:b0 + TILE_BATCH]:
                rows = _rows(r, u, rho)
                if u > 0:
                    prows, k_prev, v_prev, mask = _rows(r, u - 1, rho), k_own, v_own, mask_in
                else:
                    prows = _rows(r, GROUP // (SPAN * r) - 1, rho)
                    k_prev, v_prev, mask = kp_ref[prows, :].astype(BF16), vp_ref[prows, :].astype(BF16), mask_edge
                k_own, v_own = kc_ref[rows, :].astype(BF16), vc_ref[rows, :].astype(BF16)
                where.append((pi, u, rows, prows))
                work.append((qc_ref[rows, :].astype(BF16), doc_ref[rows, :].astype(BF16), lc_ref[rows, :], dc_ref[rows, :],
                             jnp.concatenate([k_prev, k_own], axis=0), jnp.concatenate([v_prev, v_own], axis=0), mask))
            dsp = probs(work)
            dqs = [_dot(ds, w[4], 1, 0) for (ds, _), w in zip(dsp, work)]
            dks = [_dot(ds, w[0], 0, 0) for (ds, _), w in zip(dsp, work)]
            dvs = [_dot(p, w[1], 0, 0) for (_, p), w in zip(dsp, work)]
            for (pi, u, rows, prows), dq_t, dk2, dv2 in zip(where, dqs, dks, dvs):
                if pi == 0:
                    dq_ref[rows, :] = dq_t
                else:
                    dq_ref[rows, :] += dq_t
                dk_ref[rows, :] += dk2[SPAN:, :]
                dv_ref[rows, :] += dv2[SPAN:, :]
                if u > 0:
                    dk_ref[prows, :] += dk2[:SPAN, :]
                    dv_ref[prows, :] += dv2[:SPAN, :]
        qi = _iota2((SPAN, SPAN), 0)
        ki = _iota2((SPAN, SPAN), 1)
        mask_next = (ki >= qi) & (ki < jnp.where(g == ng - 1, 0, SPAN))
        edge = [(r, rho) for r in DILATIONS for rho in range(r)]
        for b0 in range(0, len(edge), TILE_BATCH):
            where, work = [], []
            for r, rho in edge[b0:b0 + TILE_BATCH]:
                krows, qrows = _rows(r, GROUP // (SPAN * r) - 1, rho), _rows(r, 0, rho)
                where.append(krows)
                work.append((qn_ref[qrows, :].astype(BF16), don_ref[qrows, :].astype(BF16), ln_ref[qrows, :],
                             dn_ref[qrows, :], kc_ref[krows, :].astype(BF16), vc_ref[krows, :].astype(BF16), mask_next))
            dsp = probs(work)
            dks = [_dot(ds, w[0], 0, 0) for (ds, _), w in zip(dsp, work)]
            dvs = [_dot(p, w[1], 0, 0) for (_, p), w in zip(dsp, work)]
            for krows, dk1, dv1 in zip(where, dks, dvs):
                dk_ref[krows, :] += dk1
                dv_ref[krows, :] += dv1

    cur = pl.BlockSpec((GROUP, HD), lambda g, h: (g, h))
    prev = pl.BlockSpec((GROUP, HD), lambda g, h: (jnp.maximum(g - 1, 0), h))
    nxt = pl.BlockSpec((GROUP, HD), lambda g, h: (jnp.minimum(g + 1, ng - 1), h))
    vcur = pl.BlockSpec((GROUP, HD), lambda g, h: (g, v_blk * N_HEADS + h))
    vprev = pl.BlockSpec((GROUP, HD), lambda g, h: (jnp.maximum(g - 1, 0), v_blk * N_HEADS + h))
    return pl.pallas_call(
        body, grid=(ng, N_HEADS), in_specs=[cur, cur, vcur, cur, cur, cur, prev, vprev] + [nxt] * 4,
        out_specs=[cur] * 3,
        out_shape=[_sds((t, GW), F32)] * 3, name=name,
        compiler_params=_params(2))(q, k, v, do, lse, delta, k, v, q, do, lse, delta)


def _merge(os_, ls_):
    m = jnp.maximum(jnp.maximum(ls_[0], ls_[1]), ls_[2])
    ws = [jnp.exp(l - m) for l in ls_]
    tot = ws[0] + ws[1] + ws[2]
    ob = (ws[0] * os_[0] + ws[1] * os_[1] + ws[2] * os_[2]) / tot
    return ob, m + jnp.log(tot)


def _gated_norm(oa, z, wv):
    return _head_rms(oa, wv) * _silu(z)


def _mix_fwd(name, oa_raw, proj, z_blk, ob, w_dn, w_an):
    t = oa_raw.shape[0]
    tm = min(256, t)

    def body(oa_ref, z_ref, ob_ref, wd_ref, wa_ref, mix_ref):
        for h in range(N_HEADS):
            sl = slice(h * HD, (h + 1) * HD)
            mix_ref[:, sl] = _gated_norm(oa_ref[:, sl], z_ref[:, sl], wd_ref[...]).astype(BF16)
            mix_ref[:, GW + h * HD:GW + (h + 1) * HD] = _head_rms(ob_ref[:, sl], wa_ref[...]).astype(BF16)

    vec = pl.BlockSpec((1, HD), lambda i: (0, 0))
    wide = pl.BlockSpec((tm, GW), lambda i: (i, 0))
    return pl.pallas_call(
        body, grid=(t // tm,),
        in_specs=[wide, pl.BlockSpec((tm, GW), lambda i: (i, z_blk)), wide, vec, vec],
        out_specs=pl.BlockSpec((tm, 2 * GW), lambda i: (i, 0)),
        out_shape=_sds((t, 2 * GW), BF16), name=name,
        compiler_params=_params(1))(oa_raw, proj, ob, w_dn, w_an)


def _mix_bwd(name, dmixed, oa_raw, proj, z_blk, ob, w_dn, w_an, dep):
    t = oa_raw.shape[0]
    tm = min(256, t)

    def body(dm_ref, oa_ref, z_ref, ob_ref, wd_ref, wa_ref, dep_ref,
             doa_ref, dz_ref, dob_ref, dl_ref, dwd_ref, dwa_ref):
        dwd = jnp.zeros((1, HD), F32)
        dwa = jnp.zeros((1, HD), F32)
        for h in range(N_HEADS):
            sl = slice(h * HD, (h + 1) * HD)
            _, vjp = jax.vjp(_gated_norm, oa_ref[:, sl], z_ref[:, sl], wd_ref[...])
            doa, dz, dw1 = vjp(dm_ref[:, sl])
            doa_ref[:, sl] = doa
            dz_ref[:, sl] = dz.astype(BF16)
            dwd = dwd + dw1
            obh = ob_ref[:, sl]
            _, vjp2 = jax.vjp(_head_rms, obh, wa_ref[...])
            dob, dw2 = vjp2(dm_ref[:, GW + h * HD:GW + (h + 1) * HD])
            dwa = dwa + dw2
            dob_ref[:, sl] = dob
            dl_ref[:, sl] = jnp.broadcast_to(jnp.sum(dob * obh, axis=1, keepdims=True), (tm, HD))

        @pl.when(pl.program_id(0) == 0)
        def _():
            dwd_ref[...] = jnp.zeros_like(dwd_ref)
            dwa_ref[...] = jnp.zeros_like(dwa_ref)

        dwd_ref[...] += dwd
        dwa_ref[...] += dwa

    vec = pl.BlockSpec((1, HD), lambda i: (0, 0))
    wide = pl.BlockSpec((tm, GW), lambda i: (i, 0))
    return pl.pallas_call(
        body, grid=(t // tm,),
        in_specs=[pl.BlockSpec((tm, 2 * GW), lambda i: (i, 0)), wide, pl.BlockSpec((tm, GW), lambda i: (i, z_blk)),
                  wide, vec, vec, ANY],
        out_specs=[wide, pl.BlockSpec((tm, GW), lambda i: (i, z_blk)), wide, wide, vec, vec],
        out_shape=[_sds((t, GW), F32), _sds(proj.shape, BF16), _sds((t, GW), F32), _sds((t, GW), F32),
                   _sds((1, HD), F32), _sds((1, HD), F32)], name=name,
        compiler_params=_params(1))(dmixed, oa_raw, proj, ob, w_dn, w_an, dep)


def _halves(n):
    cut = (n // 256) * 128
    return [(0, cut), (cut, n)]


def _gate_up_swiglu(name, h2, w_gu_g):
    t, d = h2.shape
    n = w_gu_g.shape[2]
    per = N_DEV // 2
    tm = min(512, t)

    def body(a_ref, bg_ref, bu_ref, gu_ref, act_ref):
        a = a_ref[...]
        cuts = _halves(n)
        gs = [_dot(a, bg_ref[:, c0:c1], 1, 0) for c0, c1 in cuts]
        ups = [_dot(a, bu_ref[:, c0:c1], 1, 0) for c0, c1 in cuts]
        for (c0, c1), g, up in zip(cuts, gs, ups):
            gu_ref[0, :, c0:c1] = g.astype(BF16)
            gu_ref[1, :, c0:c1] = up.astype(BF16)
            act_ref[:, c0:c1] = (_silu(g) * up).astype(BF16)

    return pl.pallas_call(
        body, grid=(per, t // tm),
        in_specs=[pl.BlockSpec((tm, d), lambda j, i: (i, 0)), pl.BlockSpec((None, d, n), lambda j, i: (j, 0, 0)),
                  pl.BlockSpec((None, d, n), lambda j, i: (j + per, 0, 0))],
        out_specs=[pl.BlockSpec((2, tm, n), lambda j, i: (0, i, j)), pl.BlockSpec((tm, n), lambda j, i: (i, j))],
        out_shape=[_sds((2, t, per * n), BF16), _sds((t, per * n), BF16)], name=name,
        compiler_params=_params(2))(h2, w_gu_g, w_gu_g)


def _d_gate_up(name, dy16, w_down, gu3, dep):
    t, d = dy16.shape
    f = w_down.shape[0]
    tm, tn = min(1024, t), f // 4

    def body(a_ref, b_ref, g_ref, dep_ref, o_ref):
        a = a_ref[...]
        cuts = _halves(tn)
        dacts = [_dot(a, b_ref[c0:c1, :], 1, 1) for c0, c1 in cuts]
        for (c0, c1), dact in zip(cuts, dacts):
            g, up = g_ref[0, :, c0:c1].astype(F32), g_ref[1, :, c0:c1].astype(F32)
            sg = _sigmoid(g)
            o_ref[0, :, c0:c1] = (dact * up * sg * (1.0 + g * (1.0 - sg))).astype(BF16)
            o_ref[1, :, c0:c1] = (dact * g * sg).astype(BF16)

    return pl.pallas_call(
        body, grid=(f // tn, t // tm),
        in_specs=[pl.BlockSpec((tm, d), lambda j, i: (i, 0)), pl.BlockSpec((tn, d), lambda j, i: (j, 0)),
                  pl.BlockSpec((2, tm, tn), lambda j, i: (0, i, j)), ANY],
        out_specs=pl.BlockSpec((2, tm, tn), lambda j, i: (0, i, j)), out_shape=_sds((2, t, f), BF16), name=name,
        compiler_params=_params(2))(dy16, w_down, gu3, dep)


def _d_h2(name, dgu3, w_gu_g, dep):
    _, t, f = dgu3.shape
    n_dev, d, n = w_gu_g.shape
    per = n_dev // 2
    tm, tn = min(512, t), 512

    def body(g_ref, u_ref, b_ref, dep_ref, o_ref):
        acc = None
        for s in range(n_dev):
            a_ref = g_ref if s < per else u_ref
            part = _dot(a_ref[:, (s % per) * n:(s % per + 1) * n], b_ref[s], 1, 1)
            acc = part if acc is None else acc + part
        o_ref[...] = acc

    return pl.pallas_call(
        body, grid=(d // tn, t // tm),
        in_specs=[pl.BlockSpec((None, tm, f), lambda j, i: (0, i, 0)), pl.BlockSpec((None, tm, f), lambda j, i: (1, i, 0)),
                  pl.BlockSpec((n_dev, tn, n), lambda j, i: (0, j, 0)), ANY],
        out_specs=pl.BlockSpec((tm, tn), lambda j, i: (i, j)), out_shape=_sds((t, d), F32), name=name,
        compiler_params=_params(2))(dgu3, dgu3, w_gu_g, dep)


def _out_proj_norm(name, mixed, w_out, x, w_norm):
    t, d = x.shape
    kdim = mixed.shape[1]
    tm = min(512, t)

    def body(a_ref, b_ref, x_ref, w_ref, x1_ref, h_ref):
        x1 = x_ref[...] + _dot(a_ref[...], b_ref[...], 1, 0)
        x1_ref[...] = x1
        h_ref[...] = _rms_f(x1, w_ref[...]).astype(BF16)

    row = pl.BlockSpec((tm, d), lambda i: (i, 0))
    return pl.pallas_call(
        body, grid=(t // tm,),
        in_specs=[pl.BlockSpec((tm, kdim), lambda i: (i, 0)), pl.BlockSpec((kdim, d), lambda i: (0, 0)), row,
                  pl.BlockSpec((1, d), lambda i: (0, 0))],
        out_specs=[row, row], out_shape=[_sds((t, d), F32), _sds((t, d), BF16)], name=name,
        compiler_params=_params(1))(mixed, w_out, x, w_norm)


def _down_loss(name, act, w_down, x1, target):
    t, f = act.shape
    d = x1.shape[1]
    tm, tn = min(1024, t), 512

    def body(a_ref, b_ref, x_ref, t_ref, dy_ref, dy16_ref, l_ref):
        diff = _dot(a_ref[...], b_ref[...], 1, 0) + x_ref[...] - t_ref[...]
        dyv = diff * (1.0 / d)
        dy_ref[...] = dyv
        dy16_ref[...] = dyv.astype(BF16)
        tot = jnp.sum(jnp.sum(diff * diff, axis=1, keepdims=True), axis=0, keepdims=True) * (0.5 / d)

        @pl.when((pl.program_id(0) == 0) & (pl.program_id(1) == 0))
        def _():
            l_ref[...] = jnp.zeros_like(l_ref)

        l_ref[...] += jnp.broadcast_to(tot, (8, 128))

    tile = pl.BlockSpec((tm, tn), lambda i, j: (i, j))
    return pl.pallas_call(
        body, grid=(t // tm, d // tn),
        in_specs=[pl.BlockSpec((tm, f), lambda i, j: (i, 0)), pl.BlockSpec((f, tn), lambda i, j: (0, j)), tile, tile],
        out_specs=[tile, tile, pl.BlockSpec((8, 128), lambda i, j: (0, 0))],
        out_shape=[_sds((t, d), F32), _sds((t, d), BF16), _sds((8, 128), F32)], name=name,
        compiler_params=_params(2))(act, w_down, x1, target)


def _peer(me, k):
    pid = (me + k) % N_DEV
    return (pid // 4, (pid // 2) % 2, pid % 2)


def _my_id():
    return 4 * lax.axis_index("x") + 2 * lax.axis_index("y") + lax.axis_index("c")


def _exchange(name, arrays, scatter, dep):
    n = len(arrays)

    def body(*refs):
        ins, outs = refs[:n], refs[n + 1:2 * n + 1]
        send_sems, recv_sems, local_sems = refs[2 * n + 1:]
        me = _my_id()
        started = []
        for a in range(n):
            src = ins[a].at[me] if scatter[a] else ins[a]
            loc = pltpu.make_async_copy(src, outs[a].at[me], local_sems.at[a])
            loc.start()
            started.append(loc)
        remote = []
        for k in range(1, N_DEV):
            to = (me + k) % N_DEV
            for a in range(n):
                src = ins[a].at[to] if scatter[a] else ins[a]
                cp = pltpu.make_async_remote_copy(src_ref=src, dst_ref=outs[a].at[me],
                                                  send_sem=send_sems.at[a * (N_DEV - 1) + k - 1], recv_sem=recv_sems.at[a * (N_DEV - 1) + k - 1],
                                                  device_id=_peer(me, k), device_id_type=pl.DeviceIdType.MESH)
                cp.start()
                remote.append(cp)
        for k in range(1, N_DEV):
            frm = (me + N_DEV - k) % N_DEV
            for a in range(n):
                src = ins[a].at[frm] if scatter[a] else ins[a]
                pltpu.make_async_remote_copy(src_ref=src, dst_ref=outs[a].at[frm],
                                             send_sem=send_sems.at[a * (N_DEV - 1) + k - 1], recv_sem=recv_sems.at[a * (N_DEV - 1) + k - 1],
                                             device_id=_peer(me, k), device_id_type=pl.DeviceIdType.MESH).wait_recv()
        for cp in remote:
            cp.wait_send()
        for loc in started:
            loc.wait()

    out_shape = [_sds((N_DEV,) + (a.shape[1:] if sc else a.shape), a.dtype) for a, sc in zip(arrays, scatter)]
    return pl.pallas_call(
        body, in_specs=[ANY] * (n + 1), out_specs=[ANY] * n, out_shape=out_shape,
        scratch_shapes=[pltpu.SemaphoreType.DMA((n * (N_DEV - 1),)), pltpu.SemaphoreType.DMA((n * (N_DEV - 1),)),
                        pltpu.SemaphoreType.DMA((n,))],
        name=name)(*arrays, dep)


def _gather_two_level(name, arrays):
    n = len(arrays)
    per = N_DEV - 1

    def body(*refs):
        ins, outs = refs[:n], refs[n:2 * n]
        send_sems, recv_sems, local_sems = refs[2 * n:]
        x, y, c = lax.axis_index("x"), lax.axis_index("y"), lax.axis_index("c")
        me, sibling = (x, y, c), (x, y, 1 - c)
        chips = [(1 - x, y), (x, 1 - y), (1 - x, 1 - y)]

        def copy(a, k, block, to, src=None):
            slot = outs[a].at[4 * block[0] + 2 * block[1] + block[2]]
            return pltpu.make_async_remote_copy(
                src_ref=slot if src is None else src, dst_ref=slot, send_sem=send_sems.at[a * per + k],
                recv_sem=recv_sems.at[a * per + k], device_id=to, device_id_type=pl.DeviceIdType.MESH)

        mine = [pltpu.make_async_copy(ins[a], outs[a].at[4 * x + 2 * y + c], local_sems.at[a]) for a in range(n)]
        for cp in mine:
            cp.start()
        first = [copy(a, 0, me, sibling, src=ins[a]) for a in range(n)]
        first += [copy(a, 1 + j, me, (*chip, c), src=ins[a]) for j, chip in enumerate(chips) for a in range(n)]
        for cp in first:
            cp.start()
        passed = []
        for j, chip in enumerate(chips):
            for a in range(n):
                copy(a, 1 + j, (*chip, c), me).wait_recv()
                cp = copy(a, 4 + j, (*chip, c), sibling)
                cp.start()
                passed.append(cp)
        for a in range(n):
            copy(a, 0, sibling, me).wait_recv()
            for j, chip in enumerate(chips):
                copy(a, 4 + j, (*chip, 1 - c), me).wait_recv()
        for cp in first + passed:
            cp.wait_send()
        for cp in mine:
            cp.wait()

    return pl.pallas_call(
        body, in_specs=[ANY] * n, out_specs=[ANY] * n,
        out_shape=[_sds((N_DEV,) + a.shape, a.dtype) for a in arrays],
        scratch_shapes=[pltpu.SemaphoreType.DMA((n * per,)), pltpu.SemaphoreType.DMA((n * per,)),
                        pltpu.SemaphoreType.DMA((n,))],
        name=name)(*arrays)


HBM = pl.BlockSpec(memory_space=pltpu.HBM)
SEM = pl.BlockSpec(memory_space=pltpu.SEMAPHORE)
EFFECT = pltpu.SideEffectType.DATAFLOW_SIDE_EFFECTING


def _remote_copies(srcs, lands, scatter, send_sems, recv_sems, me, incoming):
    out = []
    for k in range(1, N_DEV):
        other = (me + N_DEV - k) % N_DEV if incoming else (me + k) % N_DEV
        for a in range(len(srcs)):
            sem = a * (N_DEV - 1) + k - 1
            src = srcs[a].at[other] if scatter[a] else srcs[a]
            dst = lands[a].at[other if incoming else me]
            out.append(pltpu.make_async_remote_copy(src_ref=src, dst_ref=dst, send_sem=send_sems.at[sem],
                                                    recv_sem=recv_sems.at[sem], device_id=_peer(me, k),
                                                    device_id_type=pl.DeviceIdType.MESH))
    return out


def _exchange_start(name, arrays, scatter, dep):
    n = len(arrays)
    lands = [lax.empty((N_DEV,) + (a.shape[1:] if sc else a.shape), a.dtype) for a, sc in zip(arrays, scatter)]

    def body(*refs):
        srcs, land_refs = refs[:n], refs[n:2 * n]
        send_sems, recv_sems = refs[2 * n + 1], refs[2 * n + 2]
        token = refs[-1]
        for cp in _remote_copies(srcs, land_refs, scatter, send_sems, recv_sems, _my_id(), False):
            cp.start()
        token[...] = jnp.zeros_like(token)

    n_sem = n * (N_DEV - 1)
    out_shape = ([pltpu.SemaphoreType.DMA((n_sem,)), pltpu.SemaphoreType.DMA((n_sem,))]
                 + [pltpu.HBM(a.shape, a.dtype) for a in arrays] + [pltpu.HBM(l.shape, l.dtype) for l in lands]
                 + [_sds((8, 128), F32)])
    aliases = {i: 2 + i for i in range(2 * n)}
    args = [pltpu.with_memory_space_constraint(a, pltpu.HBM) for a in list(arrays) + lands] + [dep]
    res = pl.pallas_call(
        body, name=name, in_specs=[HBM] * (2 * n) + [ANY], out_shape=out_shape,
        out_specs=[SEM, SEM] + [HBM] * (2 * n) + [pl.BlockSpec(memory_space=pltpu.VMEM)],
        input_output_aliases=aliases, compiler_params=pltpu.CompilerParams(has_side_effects=EFFECT))(*args)
    return dict(send=res[0], recv=res[1], srcs=res[2:2 + n], lands=res[2 + n:2 + 2 * n], token=res[-1],
                scatter=scatter)


def _exchange_wait(name, started, after):
    n = len(started["srcs"])
    scatter = started["scatter"]

    def body(*refs):
        srcs, land_refs = refs[:n], refs[n:2 * n]
        send_sems, recv_sems = refs[2 * n], refs[2 * n + 1]
        me = _my_id()
        for cp in _remote_copies(srcs, land_refs, scatter, send_sems, recv_sems, me, False):
            cp.wait_send()
        for cp in _remote_copies(srcs, land_refs, scatter, send_sems, recv_sems, me, True):
            cp.wait_recv()

    arrs = list(started["srcs"]) + list(started["lands"])
    res = pl.pallas_call(
        body, name=name, in_specs=[HBM] * (2 * n) + [SEM, SEM, ANY],
        out_shape=[pltpu.HBM(a.shape, a.dtype) for a in arrs], out_specs=[HBM] * (2 * n),
        input_output_aliases={i: i for i in range(2 * n)},
        compiler_params=pltpu.CompilerParams(has_side_effects=EFFECT))(*arrs, started["send"], started["recv"], after)
    me = _my_id()
    out = []
    for src, land, sc in zip(res[:n], res[n:], scatter):
        own = lax.dynamic_index_in_dim(src, me, 0, keepdims=True) if sc else src[None]
        out.append(lax.dynamic_update_slice(land, own, (me,) + (0,) * (land.ndim - 1)))
    return out


def _adamw(name, parts, w, m, v):
    r, c = w.shape
    tr, tc = r, c
    if r % 8 == 0:
        tr = next(cand for cand in (128, 88, 64, 40, 8) if r % cand == 0)
    else:
        tc = 256
    c1 = 1.0 / (1.0 - ADAM_B1 ** ADAM_STEP)
    c2 = 1.0 / (1.0 - ADAM_B2 ** ADAM_STEP)

    def body(p_ref, w_ref, m_ref, v_ref, g_ref, d_ref, nm_ref, nv_ref):
        g = p_ref[0].astype(F32)
        for s in range(1, N_DEV):
            g = g + p_ref[s].astype(F32)
        mn = ADAM_B1 * m_ref[...] + (1.0 - ADAM_B1) * g
        vn = ADAM_B2 * v_ref[...] + (1.0 - ADAM_B2) * (g * g)
        g_ref[...] = g
        nm_ref[...] = mn
        nv_ref[...] = vn
        d_ref[...] = -ADAM_LR * ((mn * c1) / (jnp.sqrt(vn * c2) + ADAM_EPS) + ADAM_WD * w_ref[...])

    blk = pl.BlockSpec((tr, tc), lambda i, j: (i, j))
    return pl.pallas_call(
        body, grid=(r // tr, c // tc),
        in_specs=[pl.BlockSpec((N_DEV, tr, tc), lambda i, j: (0, i, j)), blk, blk, blk],
        out_specs=[blk] * 4, out_shape=[_sds((r, c), F32)] * 4, name=name,
        compiler_params=_params(2, VMEM_LIMIT))(parts, w, m, v)


def _pad_rows(a, rows):
    return jnp.pad(a, ((0, rows - a.shape[0]), (0, 0)))


def _lane_row(vec8, offset):
    return jnp.pad(vec8.reshape(1, 8), ((0, 0), (offset, HD - 8 - offset)))


def kernel(x, positions, attn_norm_w, w_in, conv_w, a_log, dt_bias, delta_out_norm_w, q_norm_w, k_norm_w, attn_out_norm_w, w_out, ffn_norm_w, w_gate_up, w_down, loss_target, m_attn_norm_w, m_w_in, m_conv_w, m_a_log, m_dt_bias, m_delta_out_norm_w, m_q_norm_w, m_k_norm_w, m_attn_out_norm_w, m_w_out, m_ffn_norm_w, m_w_gate_up, m_w_down, v_attn_norm_w, v_w_in, v_conv_w, v_a_log, v_dt_bias, v_delta_out_norm_w, v_q_norm_w, v_k_norm_w, v_attn_out_norm_w, v_w_out, v_ffn_norm_w, v_w_gate_up, v_w_down):
    x2 = x[0]
    t, d = x2.shape
    target = loss_target[0]
    pos_col = positions.reshape(t, 1)
    half = HD // 2
    inv = (ROPE_THETA ** (-np.arange(half, dtype=np.float32) / half)).astype(np.float32)
    inv_row = jnp.asarray(np.concatenate([inv, inv]).reshape(1, HD))

    n_in = w_in.shape[2]
    n_gu = w_gate_up.shape[2]
    w_in_g, conv_g = _gather_two_level("gather_in", [w_in[0].astype(BF16), _pad_rows(conv_w[0], 8)])
    out_fly = _exchange_start("gather_out_start", [w_out[0].astype(BF16)], [False], conv_g)
    gu_fly = _exchange_start("gather_gate_up_start", [w_gate_up[0].astype(BF16)], [False], out_fly["token"])
    down_fly = _exchange_start("gather_down_start", [w_down[0].astype(BF16)], [False], gu_fly["token"])
    n_main = 4 * GW
    n_small = 2 * N_HEADS
    segments = [(0, n_main, 0), (n_main + n_small, N_DEV * n_in, n_main), (n_main, n_main + n_small, 7 * GW)]
    pieces = []
    for lo, hi, _ in segments:
        f = lo
        while f < hi:
            j = f // n_in
            end = min(hi, (j + 1) * n_in)
            pieces.append(w_in_g[j][:, f - j * n_in:end - j * n_in])
            f = end
    w_cat = jnp.concatenate(pieces + [jnp.zeros((d, HD - n_small), BF16)], axis=1)
    n_cat = w_cat.shape[1]
    small_blk = (7 * GW) // HD
    conv_w8 =jnp.transpose(conv_g, (1, 0, 2)).reshape(8, 3 * GW)
    alog_row = _lane_row(a_log[0], 8)
    dtb_row = _lane_row(dt_bias[0], 8)

    tm = min(2048, t)
    h1 = _rms_fwd("norm1", x2, attn_norm_w, down_fly["token"])
    tmp, tnp = min(1024, t), n_cat // 3
    proj = _mm("in_proj", h1, w_cat, grid=(t // tmp, n_cat // tnp, 1),
               a_spec=pl.BlockSpec((tmp, d), lambda i, j, k: (i, 0)),
               b_spec=pl.BlockSpec((d, tnp), lambda i, j, k: (0, j)),
               o_spec=pl.BlockSpec((tmp, tnp), lambda i, j, k: (i, j)),
               out_shape=_sds((t, n_cat), F32), ca=1, cb=0, nk=1)
    qn = _conv_fwd("conv_q", proj, conv_w8, 0, True, HD ** -0.5)
    kn = _conv_fwd("conv_k", proj, conv_w8, 1, True, 1.0)
    vv = _conv_fwd("conv_v", proj, conv_w8, 2, False, 1.0)
    beta_b, gc_b = _gates_fwd("gates", proj, small_blk, alog_row, dtb_row)
    u, w, p, tinv, qd, kd = _delta_prep("delta_prep", qn, kn, vv, beta_b, gc_b)
    oa_raw, vn, s_hist = _delta_scan("delta_scan", u, w, p, qd, kd, gc_b)

    cos_t, sin_t = _rope_tables("rope_tables", pos_col, inv_row)
    aq, ak = _qk_fwd("attn_qk", proj, 2, q_norm_w, k_norm_w, cos_t, sin_t)
    ob, lse = _attn_fwd("attn_fwd", aq, ak, proj, 6)
    mixed = _mix_fwd("mix", oa_raw, proj, 3, ob, delta_out_norm_w, attn_out_norm_w)
    (w_out_g,) = _exchange_wait("gather_out_wait", out_fly, mixed)
    w_out_full = w_out_g.reshape(2 * GW, d)
    tn = 512
    x1, h2 = _out_proj_norm("out_proj", mixed, w_out_full, x2, ffn_norm_w)
    per = N_DEV // 2
    (w_gu_g,) = _exchange_wait("gather_gate_up_wait", gu_fly, h2)
    gu3, act = _gate_up_swiglu("gate_up", h2, w_gu_g)
    (w_down_g,) = _exchange_wait("gather_down_wait", down_fly, act)
    w_down_full = w_down_g.reshape(D_FF, d)
    tmd = min(1024, t)
    dy, dy16, loss_tile = _down_loss("down_proj", act, w_down_full, x1, target)
    loss = lax.psum(loss_tile[0, 0], ("x", "y", "c"))

    tk, nkt = t, 1
    g_down = _mm("g_down", act, dy16, dep=loss.reshape(1, 1), grid=(D_FF // 1408, d // 512, nkt),
                 a_spec=pl.BlockSpec((tk, 1408), lambda i, j, k: (k, i)),
                 b_spec=pl.BlockSpec((tk, 512), lambda i, j, k: (k, j)),
                 o_spec=pl.BlockSpec((1408, 512), lambda i, j, k: (i, j)),
                 out_shape=_sds((D_FF, d), F32), ca=0, cb=0, nk=nkt)
    down_g_fly = _exchange_start("reduce_down_start", [g_down.reshape(N_DEV, D_FF // N_DEV, d)], [True], dy16)
    dgu3 = _d_gate_up("d_gate_up", dy16, w_down_full, gu3, down_g_fly["token"])
    g_gu = _mm("g_gate_up", h2, dgu3, grid=(d // 512, N_DEV, nkt),
               a_spec=pl.BlockSpec((tk, 512), lambda i, j, k: (k, i)),
               b_spec=pl.BlockSpec((None, tk, n_gu), lambda i, j, k: (j // per, k, j % per)),
               o_spec=pl.BlockSpec((None, 512, n_gu), lambda i, j, k: (j, i, 0)),
               out_shape=_sds((N_DEV, d, n_gu), F32), ca=0, cb=0, nk=nkt)
    gu_g_fly = _exchange_start("reduce_gate_up_start", [g_gu], [True], dy16)
    dh2 = _d_h2("d_h2", dgu3, w_gu_g, gu_g_fly["token"])
    dx1, dx1_16, g_ffn_norm = _rms_bwd("norm2_bwd", x1, ffn_norm_w, dh2, dy)

    g_out = _mm("g_out", mixed, dx1_16, grid=((2 * GW) // 512, 1, nkt),
                a_spec=pl.BlockSpec((tk, 512), lambda i, j, k: (k, i)),
                b_spec=pl.BlockSpec((tk, d), lambda i, j, k: (k, 0)),
                o_spec=pl.BlockSpec((512, d), lambda i, j, k: (i, 0)),
                out_shape=_sds((2 * GW, d), F32), ca=0, cb=0, nk=nkt)
    out_g_fly = _exchange_start("reduce_out_start", [g_out.reshape(N_DEV, (2 * GW) // N_DEV, d)], [True], g_ffn_norm)
    dmixed = _mm("d_mixed", dx1_16, w_out_full, dep=out_g_fly["token"], grid=(t // tm, (2 * GW) // tn, 1),
                 a_spec=pl.BlockSpec((tm, d), lambda i, j, k: (i, 0)),
                 b_spec=pl.BlockSpec((tn, d), lambda i, j, k: (j, 0)),
                 o_spec=pl.BlockSpec((tm, tn), lambda i, j, k: (i, j)),
                 out_shape=_sds((t, 2 * GW), F32), ca=1, cb=1, nk=1)
    doa, dproj, dob, delta, g_dn, g_an = _mix_bwd("mix_bwd", dmixed, oa_raw, proj, 3, ob,
                                                  delta_out_norm_w, attn_out_norm_w, out_g_fly["token"])
    d_aq, d_ak, d_av = _attn_bwd("attn_bwd", aq, ak, proj, 6, dob, lse, delta)
    dproj, g_qn, g_kn = _qk_bwd("attn_qk_bwd", proj, 2, q_norm_w, k_norm_w, cos_t, sin_t, d_aq, d_ak, dproj)
    dproj = _cast_into("attn_v_bwd", d_av, dproj, 6)

    dvn, dqd, dkd, dw, ddec = _delta_scan_bwd("delta_scan_bwd", doa, w, p, qd, kd, gc_b, vn, s_hist)
    dqn, dkn, dvv, dbeta_b, dg_b = _delta_prep_bwd("delta_prep_bwd", qn, kn, vv, beta_b, gc_b, tinv, u, w, vn,
                                                   doa, dvn, dqd, dkd, dw, ddec)
    dproj, gcw_q = _conv_bwd("conv_q_bwd", proj, conv_w8, dqn, dproj, 0, True, HD ** -0.5)
    dproj, gcw_k = _conv_bwd("conv_k_bwd", proj, conv_w8, dkn, dproj, 1, True, 1.0)
    dproj, gcw_v = _conv_bwd("conv_v_bwd", proj, conv_w8, dvv, dproj, 2, False, 1.0)
    dproj, g_alog_row, g_dtb_row = _gates_bwd("gates_bwd", proj, small_blk, alog_row, dtb_row, dbeta_b, dg_b, dproj)
    tmc = 384
    g_cat = _mm("g_in", dproj, h1, grid=(n_cat // tmc, 1, nkt),
                a_spec=pl.BlockSpec((tk, tmc), lambda i, j, k: (k, i)),
                b_spec=pl.BlockSpec((tk, d), lambda i, j, k: (k, 0)),
                o_spec=pl.BlockSpec((tmc, d), lambda i, j, k: (i, 0)),
                out_shape=_sds((n_cat, d), BF16), ca=0, cb=0, nk=nkt)
    parts = []
    for j in range(N_DEV):
        cols = []
        for lo, hi, start in sorted(segments):
            a, b = max(lo, j * n_in), min(hi, (j + 1) * n_in)
            if a < b:
                cols.append(g_cat[start + a - lo:start + b - lo])
        parts.append(cols[0] if len(cols) == 1 else jnp.concatenate(cols, axis=0))
    g_in_parts = jnp.stack(parts)
    g_conv = jnp.concatenate([gcw_q, gcw_k, gcw_v], axis=1)
    n_cw = conv_w.shape[2]
    g_conv_parts = jnp.transpose(g_conv.reshape(8, N_DEV, n_cw), (1, 0, 2))
    in_g_fly = _exchange_start("reduce_in_start", [g_in_parts, g_conv_parts], [True] * 2, g_dtb_row)
    tmh1 = min(512, t)
    dh1 = _mm("d_h1", dproj, w_cat, dep=in_g_fly["token"], grid=(t // tmh1, d // 1024, 1),
              a_spec=pl.BlockSpec((tmh1, n_cat), lambda i, j, k: (i, 0)),
              b_spec=pl.BlockSpec((1024, n_cat), lambda i, j, k: (j, 0)),
              o_spec=pl.BlockSpec((tmh1, 1024), lambda i, j, k: (i, j)),
              out_shape=_sds((t, d), F32), ca=1, cb=1, nk=1)
    grad_x, _, g_attn_norm = _rms_bwd("norm1_bwd", x2, attn_norm_w, dh1, dx1)

    small_rows = [g_attn_norm.reshape(d // HD, HD), g_ffn_norm.reshape(d // HD, HD), g_dn, g_qn, g_kn, g_an,
                  g_alog_row, g_dtb_row]
    small_pack = _pad_rows(jnp.concatenate(small_rows, axis=0), 40)
    (r_down,) = _exchange_wait("reduce_down_wait", down_g_fly, grad_x)
    (r_gu,) = _exchange_wait("reduce_gate_up_wait", gu_g_fly, grad_x)
    (r_out,) = _exchange_wait("reduce_out_wait", out_g_fly, grad_x)
    res_gu = [a[None] for a in _adamw("adamw_gate_up", r_gu, w_gate_up[0], m_w_gate_up[0], v_w_gate_up[0])]
    res_down = [a[None] for a in _adamw("adamw_down", r_down, w_down[0], m_w_down[0], v_w_down[0])]
    res_out = [a[None] for a in _adamw("adamw_out", r_out, w_out[0], m_w_out[0], v_w_out[0])]
    done = (res_gu[3][0, :1, :1] + res_down[3][0, :1, :1] + res_out[3][0, :1, :1])
    r_in, r_conv = _exchange_wait("reduce_in_wait", in_g_fly, done)
    upd_in = _adamw("adamw_in", r_in, jnp.transpose(w_in[0]), jnp.transpose(m_w_in[0]), jnp.transpose(v_w_in[0]))
    res_in = [jnp.transpose(a)[None] for a in upd_in]
    (r_small,) = _exchange("gather_small_grads", [small_pack], [False], upd_in[0])

    def pack_small(an, fn, dn, qn_, kn_, aon, al, db):
        rows = [an.reshape(d // HD, HD), fn.reshape(d // HD, HD), dn, qn_, kn_, aon,
                _lane_row(al[0], 8), _lane_row(db[0], 8)]
        return _pad_rows(jnp.concatenate(rows, axis=0), 40)

    def unpack_small(pk):
        nr = d // HD
        return dict(attn_norm_w=pk[:nr].reshape(1, d), ffn_norm_w=pk[nr:2 * nr].reshape(1, d),
                    delta_out_norm_w=pk[2 * nr:2 * nr + 1], q_norm_w=pk[2 * nr + 1:2 * nr + 2],
                    k_norm_w=pk[2 * nr + 2:2 * nr + 3], attn_out_norm_w=pk[2 * nr + 3:2 * nr + 4],
                    a_log=pk[2 * nr + 4:2 * nr + 5, 8:16], dt_bias=pk[2 * nr + 5:2 * nr + 6, 8:16])

    res_small = _adamw("adamw_small", r_small,
                       pack_small(attn_norm_w, ffn_norm_w, delta_out_norm_w, q_norm_w, k_norm_w, attn_out_norm_w, a_log, dt_bias),
                       pack_small(m_attn_norm_w, m_ffn_norm_w, m_delta_out_norm_w, m_q_norm_w, m_k_norm_w, m_attn_out_norm_w, m_a_log, m_dt_bias),
                       pack_small(v_attn_norm_w, v_ffn_norm_w, v_delta_out_norm_w, v_q_norm_w, v_k_norm_w, v_attn_out_norm_w, v_a_log, v_dt_bias))
    small = [unpack_small(a) for a in res_small]
    res_conv =[a[None, :4] for a in _adamw("adamw_conv", r_conv, _pad_rows(conv_w[0], 8), _pad_rows(m_conv_w[0], 8),
                                            _pad_rows(v_conv_w[0], 8))]

    outs = [loss, grad_x[None]]
    for i in range(4):
        s = small[i]
        outs += [s["attn_norm_w"], res_in[i], res_conv[i], s["a_log"], s["dt_bias"], s["delta_out_norm_w"],
                 s["q_norm_w"], s["k_norm_w"], s["attn_out_norm_w"], res_out[i], s["ffn_norm_w"], res_gu[i],
                 res_down[i]]
    return tuple(outs)
```

```python
import numpy as np
import jax
import jax.numpy as jnp
from jax import lax
from jax.experimental import pallas as pl
from jax.experimental.pallas import tpu as pltpu

F32 = jnp.float32
BF16 = jnp.bfloat16

N_DEV = 8
N_HEADS = 8
HD = 128
GW = N_HEADS * HD
CHUNK = 64
PAIR = 2 * CHUNK
SCAN_CHUNKS = 4
SCAN_ROWS = SCAN_CHUNKS * CHUNK
SPAN = 128
DILATIONS = (1, 4, 16)
ROPE_THETA = 10000.0
EPS = 1e-6
D_FF = 5632
ADAM_LR, ADAM_B1, ADAM_B2, ADAM_EPS, ADAM_WD, ADAM_STEP = 0.001, 0.9, 0.999, 1e-8, 0.01, 10
NEG = -1e30
VMEM_LIMIT = 56 * 1024 * 1024
ANY = pl.BlockSpec(memory_space=pl.ANY)
HEADS_PER_STEP = 8


def _params(n_grid, vmem=VMEM_LIMIT):
    return pltpu.CompilerParams(dimension_semantics=("arbitrary",) * n_grid, vmem_limit_bytes=vmem)


def _sds(shape, dtype):
    return jax.ShapeDtypeStruct(tuple(shape), dtype)


def _sigmoid(x):
    return 1.0 / (1.0 + jnp.exp(-x))


def _silu(x):
    return x * _sigmoid(x)


def _softplus(x):
    return jnp.maximum(x, 0.0) + jnp.log(1.0 + jnp.exp(-jnp.abs(x)))


def _dot(a, b, ca, cb):
    return lax.dot_general(a, b, (((ca,), (cb,)), ((), ())), preferred_element_type=F32)


def _b16(x):
    return x if x.dtype == BF16 else x.astype(BF16)


def _split(x):
    hi = x.astype(BF16)
    return hi, (x - hi.astype(F32)).astype(BF16)


def _dot3(a, b, ca, cb):
    a_hi, a_lo = _split(a)
    b_hi, b_lo = _split(b)
    return _dot(a_hi, b_hi, ca, cb) + (_dot(a_hi, b_lo, ca, cb) + _dot(a_lo, b_hi, ca, cb))


def _iota2(shape, axis):
    return lax.broadcasted_iota(jnp.int32, shape, axis)


def _mm(name, a, b, *, grid, a_spec, b_spec, o_spec, out_shape, ca, cb, nk, dep=None):
    assert nk == 1 and grid[2] == 1

    def body(*refs):
        refs[-1][...] = _dot(_b16(refs[0][...]), _b16(refs[1][...]), ca, cb).astype(refs[-1].dtype)

    in_specs = [a_spec, b_spec] + ([ANY] if dep is not None else [])
    args = (a, b) + ((dep,) if dep is not None else ())
    return pl.pallas_call(body, grid=grid, in_specs=in_specs, out_specs=o_spec, out_shape=out_shape,
                          name=name, compiler_params=_params(3))(*args)


def _rms_f(xv, wv):
    return xv * lax.rsqrt(jnp.mean(xv * xv, axis=-1, keepdims=True) + EPS) * wv


def _rms_fwd(name, x, w, dep):
    t, d = x.shape
    tm = min(512, t)

    def body(x_ref, w_ref, dep_ref, o_ref):
        o_ref[...] = _rms_f(x_ref[...], w_ref[...]).astype(BF16)

    row = pl.BlockSpec((tm, d), lambda i: (i, 0))
    vec = pl.BlockSpec((1, d), lambda i: (0, 0))
    return pl.pallas_call(body, grid=(t // tm,), in_specs=[row, vec, ANY], out_specs=row,
                          out_shape=_sds((t, d), BF16), name=name, compiler_params=_params(1))(x, w, dep)


def _rms_bwd(name, x, w, dh, res):
    t, d = x.shape
    tm = min(256, t)

    def body(x_ref, w_ref, dh_ref, res_ref, dx_ref, dx16_ref, dw_ref):
        _, vjp = jax.vjp(_rms_f, x_ref[...], w_ref[...])
        dxv, dwv = vjp(dh_ref[...])
        dxv = dxv + res_ref[...]
        dx_ref[...] = dxv
        dx16_ref[...] = dxv.astype(BF16)

        @pl.when(pl.program_id(0) == 0)
        def _():
            dw_ref[...] = jnp.zeros_like(dw_ref)

        dw_ref[...] += dwv

    row = pl.BlockSpec((tm, d), lambda i: (i, 0))
    vec = pl.BlockSpec((1, d), lambda i: (0, 0))
    return pl.pallas_call(body, grid=(t // tm,), in_specs=[row, vec, row, row], out_specs=[row, row, vec],
                          out_shape=[_sds((t, d), F32), _sds((t, d), BF16), _sds((1, d), F32)], name=name,
                          compiler_params=_params(1))(x, w, dh, res)


def _shift_rows(x, s):
    t = x.shape[0]
    r = pltpu.roll(x, s % t, 0)
    row8 = _iota2((8, x.shape[1]), 0)
    if s > 0:
        return jnp.concatenate([jnp.where(row8 >= s, r[:8], 0.0), r[8:]], axis=0)
    return jnp.concatenate([r[:t - 8], jnp.where(row8 < 8 + s, r[t - 8:], 0.0)], axis=0)


def _conv_taps(xv, w_ref):
    c = w_ref[3:4, :] * xv
    for s in (1, 2, 3):
        c = c + w_ref[3 - s:4 - s, :] * _shift_rows(xv, s)
    return c


def _post_conv(c, l2, scale):
    y = _silu(c)
    if l2:
        y = y * lax.rsqrt(jnp.sum(y * y, axis=-1, keepdims=True) + EPS) * scale
    return y


def _conv_fwd(name, proj, conv_w8, group, l2, scale):
    t = proj.shape[0]

    def body(x_ref, w_ref, o_ref):
        o_ref[...] = _post_conv(_conv_taps(x_ref[...], w_ref), l2, scale)

    return pl.pallas_call(
        body, grid=(N_HEADS,),
        in_specs=[pl.BlockSpec((t, HD), lambda h: (0, h + group * N_HEADS)),
                  pl.BlockSpec((8, HD), lambda h: (0, h + group * N_HEADS))],
        out_specs=pl.BlockSpec((t, HD), lambda h: (0, h)),
        out_shape=_sds((t, GW), F32), name=name, compiler_params=_params(1, VMEM_LIMIT))(proj, conv_w8)


def _conv_bwd(name, proj, conv_w8, dn, dproj, group, l2, scale):
    t = proj.shape[0]

    def body(x_ref, w_ref, dn_ref, dproj_ref, dx_ref, dw_ref):
        xv = x_ref[...]
        c = _conv_taps(xv, w_ref)
        _, vjp = jax.vjp(lambda cc: _post_conv(cc, l2, scale), c)
        (dc,) = vjp(dn_ref[...])
        dx = w_ref[3:4, :] * dc
        dw = jnp.zeros((8, HD), F32)
        rid = _iota2((8, HD), 0)
        dw = dw + jnp.where(rid == 3, jnp.sum(dc * xv, axis=0, keepdims=True), 0.0)
        for s in (1, 2, 3):
            dx = dx + w_ref[3 - s:4 - s, :] * _shift_rows(dc, -s)
            dw = dw + jnp.where(rid == 3 - s, jnp.sum(dc * _shift_rows(xv, s), axis=0, keepdims=True), 0.0)
        dx_ref[...] = dx.astype(BF16)
        dw_ref[...] = dw

    return pl.pallas_call(
        body, grid=(N_HEADS,),
        in_specs=[pl.BlockSpec((t, HD), lambda h: (0, h + group * N_HEADS)),
                  pl.BlockSpec((8, HD), lambda h: (0, h + group * N_HEADS)),
                  pl.BlockSpec((t, HD), lambda h: (0, h)), ANY],
        out_specs=[pl.BlockSpec((t, HD), lambda h: (0, h + group * N_HEADS)), pl.BlockSpec((8, HD), lambda h: (0, h))],
        out_shape=[_sds(dproj.shape, BF16), _sds((8, GW), F32)], input_output_aliases={3: 0}, name=name,
        compiler_params=_params(1, VMEM_LIMIT))(proj, conv_w8, dn, dproj)


def _chunk_cumsum(g, rows):
    pos = rows % CHUNK
    s = 1
    while s < CHUNK:
        g = g + jnp.where(pos >= s, pltpu.roll(g, s, 0), 0.0)
        s *= 2
    return g


def _gates_fwd(name, proj, small_blk, alog_row, dtb_row):
    t = proj.shape[0]
    tm = min(256, t)

    def body(s_ref, a_ref, b_ref, beta_ref, gc_ref):
        sm = s_ref[...]
        beta = _sigmoid(sm)
        g = -jnp.exp(a_ref[...]) * _softplus(sm + b_ref[...])
        gc = _chunk_cumsum(g, _iota2((tm, HD), 0))
        lane = _iota2((tm, HD), 1)
        for h in range(N_HEADS):
            bcol = jnp.sum(jnp.where(lane == h, beta, 0.0), axis=1, keepdims=True)
            gcol = jnp.sum(jnp.where(lane == 8 + h, gc, 0.0), axis=1, keepdims=True)
            beta_ref[:, h * HD:(h + 1) * HD] = jnp.broadcast_to(bcol, (tm, HD))
            gc_ref[:, h * HD:(h + 1) * HD] = jnp.broadcast_to(gcol, (tm, HD))

    vec = pl.BlockSpec((1, HD), lambda i: (0, 0))
    wide = pl.BlockSpec((tm, GW), lambda i: (i, 0))
    return pl.pallas_call(
        body, grid=(t // tm,),
        in_specs=[pl.BlockSpec((tm, HD), lambda i: (i, small_blk)), vec, vec], out_specs=[wide, wide],
        out_shape=[_sds((t, GW), F32), _sds((t, GW), F32)], name=name,
        compiler_params=_params(1))(proj, alog_row, dtb_row)


def _gates_bwd(name, proj, small_blk, alog_row, dtb_row, dbeta_b, dg_b, dproj):
    t = proj.shape[0]
    tm = min(256, t)

    def body(s_ref, a_ref, b_ref, db_ref, dg_ref, dproj_ref, ds_ref, da_ref, dbias_ref):
        sm = s_ref[...]
        lane = _iota2((tm, HD), 1)
        db = jnp.zeros((tm, HD), F32)
        dg = jnp.zeros((tm, HD), F32)
        for h in range(N_HEADS):
            db = db + jnp.where(lane == h, db_ref[:, h * HD:(h + 1) * HD], 0.0)
            dg = dg + jnp.where(lane == 8 + h, dg_ref[:, h * HD:(h + 1) * HD], 0.0)
        beta = _sigmoid(sm)
        ea = jnp.exp(a_ref[...])
        pre = sm + b_ref[...]
        g = -ea * _softplus(pre)
        dpre = dg * (-ea) * _sigmoid(pre)
        ds_ref[...] = (db * beta * (1.0 - beta) + dpre).astype(BF16)

        @pl.when(pl.program_id(0) == 0)
        def _():
            da_ref[...] = jnp.zeros_like(da_ref)
            dbias_ref[...] = jnp.zeros_like(dbias_ref)

        da_ref[...] += jnp.sum(dg * g, axis=0, keepdims=True)
        dbias_ref[...] += jnp.sum(dpre, axis=0, keepdims=True)

    vec = pl.BlockSpec((1, HD), lambda i: (0, 0))
    wide = pl.BlockSpec((tm, GW), lambda i: (i, 0))
    return pl.pallas_call(
        body, grid=(t // tm,),
        in_specs=[pl.BlockSpec((tm, HD), lambda i: (i, small_blk)), vec, vec, wide, wide, ANY],
        out_specs=[pl.BlockSpec((tm, HD), lambda i: (i, small_blk)), vec, vec],
        out_shape=[_sds(dproj.shape, BF16), _sds((1, HD), F32), _sds((1, HD), F32)],
        input_output_aliases={5: 0}, name=name,
        compiler_params=_params(1))(proj, alog_row, dtb_row, dbeta_b, dg_b, dproj)


def _pair_masks():
    ii = _iota2((PAIR, PAIR), 0)
    jj = _iota2((PAIR, PAIR), 1)
    same = (ii // CHUNK) == (jj // CHUNK)
    return ii, jj, same & (ii >= jj), same & (ii > jj)


def _to_row(col_b, ii, jj):
    return jnp.sum(jnp.where(ii == jj, col_b, 0.0), axis=0, keepdims=True)


def _to_col(row, ii, jj):
    return jnp.sum(jnp.where(ii == jj, jnp.broadcast_to(row, (PAIR, PAIR)), 0.0), axis=1, keepdims=True)


def _decay_parts(gc, last_a, last_b, ii, jj, causal):
    diff = gc - _to_row(gc, ii, jj)
    dmat = jnp.where(causal, jnp.exp(jnp.where(causal, diff, 0.0)), 0.0)
    glast = jnp.where(ii < CHUNK, last_a, last_b)
    return dmat, jnp.exp(gc), jnp.exp(glast - gc)


def _unit_lower_inverse(lows, ii, jj):
    eye = jnp.where(ii == jj, 1.0, 0.0)
    mm = lambda xs, ys: [_dot3(a, b, 1, 0) for a, b in zip(xs, ys)]
    plus = lambda xs: [eye + a for a in xs]
    minus = lambda xs: [eye - a for a in xs]
    d1 = [jnp.where((ii // 16) == (jj // 16), low, 0.0) for low in lows]
    d2 = mm(d1, d1)
    a = mm(minus(d1), plus(d2))
    d4 = mm(d2, d2)
    a = mm(a, plus(d4))
    d8 = mm(d4, d4)
    td = mm(a, plus(d8))
    n1 = mm(td, [low - d for low, d in zip(lows, d1)])
    n2 = mm(n1, n1)
    return mm(mm(minus(n1), plus(n2)), td)


def _delta_prep(name, qn, kn, vv, beta_b, gc_b):
    t = qn.shape[0]

    def body(q_ref, k_ref, v_ref, b_ref, g_ref, u_ref, w_ref, p_ref, t_ref, qd_ref, kd_ref):
        ii, jj, causal, strict = _pair_masks()
        sls = [slice(hh * HD, (hh + 1) * HD) for hh in range(HEADS_PER_STEP)]
        lows = []
        for sl in sls:
            q, k, beta = q_ref[:, sl], k_ref[:, sl], b_ref[:, sl]
            dmat, gam, e2 = _decay_parts(g_ref[:, sl], g_ref[CHUNK - 1:CHUNK, sl], g_ref[PAIR - 1:PAIR, sl],
                                         ii, jj, causal)
            k16 = _b16(k)
            lows.append(jnp.where(strict, beta * _dot(k16, k16, 1, 1) * dmat, 0.0))
            p_ref[:, sl] = jnp.where(causal, _dot(_b16(q), k16, 1, 1) * dmat, 0.0).astype(BF16)
            qd_ref[:, sl] = (q * gam).astype(BF16)
            kd_ref[:, sl] = (k * e2).astype(BF16)
        for sl, tinv in zip(sls, _unit_lower_inverse(lows, ii, jj)):
            beta = b_ref[:, sl]
            t_ref[:, sl] = tinv
            u_ref[:, sl] = _dot3(tinv, v_ref[:, sl] * beta, 1, 0)
            w_ref[:, sl] = _dot3(tinv, k_ref[:, sl] * (beta * jnp.exp(g_ref[:, sl])), 1, 0).astype(BF16)

    blk = pl.BlockSpec((PAIR, HEADS_PER_STEP * HD), lambda i, h: (i, h))
    return pl.pallas_call(
        body, grid=(t // PAIR, N_HEADS // HEADS_PER_STEP), in_specs=[blk] * 5, out_specs=[blk] * 6,
        out_shape=[_sds((t, GW), F32), _sds((t, GW), BF16), _sds((t, GW), BF16), _sds((t, GW), F32),
                   _sds((t, GW), BF16), _sds((t, GW), BF16)],
        name=name, compiler_params=_params(2))(qn, kn, vv, beta_b, gc_b)


def _delta_scan(name, u, w, p, qd, kd, gc_b):
    t = u.shape[0]
    n = t // CHUNK

    def body(u_ref, w_ref, p_ref, qd_ref, kd_ref, g_ref, o_ref, vn_ref, sh_ref, state):
        @pl.when(pl.program_id(0) == 0)
        def _():
            state[...] = jnp.zeros_like(state)

        sls = [slice(h * HD, (h + 1) * HD) for h in range(N_HEADS)]
        heads = range(N_HEADS)
        s = [state[h] for h in heads]
        for c in range(SCAN_CHUNKS):
            rows = slice(c * CHUNK, (c + 1) * CHUNK)
            last = slice((c + 1) * CHUNK - 1, (c + 1) * CHUNK)
            for h in heads:
                sh_ref[c, h] = s[h]
            s16 = [_b16(a) for a in s]
            ws = [_dot(w_ref[rows, sls[h]], s16[h], 1, 0) for h in heads]
            qs = [_dot(qd_ref[rows, sls[h]], s16[h], 1, 0) for h in heads]
            vn16 = [_b16(u_ref[rows, sls[h]] - ws[h]) for h in heads]
            pv = [_dot(p_ref[rows, sls[h]], jnp.concatenate([vn16[h], vn16[h]], axis=0), 1, 0) for h in heads]
            kv = [_dot(kd_ref[rows, sls[h]], vn16[h], 0, 0) for h in heads]
            for h in heads:
                o_ref[rows, sls[h]] = qs[h] + pv[h]
                vn_ref[rows, sls[h]] = vn16[h]
            s = [s[h] * jnp.exp(g_ref[last, sls[h]]) + kv[h] for h in heads]
        for h in heads:
            state[h] = s[h]

    blk = pl.BlockSpec((SCAN_ROWS, GW), lambda i: (i, 0))
    return pl.pallas_call(
        body, grid=(t // SCAN_ROWS,), in_specs=[blk] * 6,
        out_specs=[blk, blk, pl.BlockSpec((SCAN_CHUNKS, N_HEADS, HD, HD), lambda i: (i, 0, 0, 0))],
        out_shape=[_sds((t, GW), F32), _sds((t, GW), BF16), _sds((n, N_HEADS, HD, HD), F32)],
        scratch_shapes=[pltpu.VMEM((N_HEADS, HD, HD), F32)], name=name,
        compiler_params=_params(1))(u, w, p, qd, kd, gc_b)


def _delta_scan_bwd(name, do, w, p, qd, kd, gc_b, vn, s_hist):
    t = do.shape[0]
    n = t // CHUNK

    def body(do_ref, w_ref, p_ref, qd_ref, kd_ref, g_ref, vn_ref, sh_ref,
             dvn_ref, dqd_ref, dkd_ref, dw_ref, ddec_ref, dstate):
        @pl.when(pl.program_id(0) == 0)
        def _():
            dstate[...] = jnp.zeros_like(dstate)

        sls = [slice(h * HD, (h + 1) * HD) for h in range(N_HEADS)]
        heads = range(N_HEADS)
        ds = [dstate[h] for h in heads]
        for c in reversed(range(SCAN_CHUNKS)):
            rows = slice(c * CHUNK, (c + 1) * CHUNK)
            last = slice((c + 1) * CHUNK - 1, (c + 1) * CHUNK)
            ds16 = [_b16(a) for a in ds]
            s16 = [_b16(sh_ref[c, h]) for h in heads]
            do16 = [_b16(do_ref[rows, sls[h]]) for h in heads]
            ptdo = [_dot(p_ref[rows, sls[h]], do16[h], 0, 0) for h in heads]
            kds = [_dot(kd_ref[rows, sls[h]], ds16[h], 1, 0) for h in heads]
            qdo = [_dot(qd_ref[rows, sls[h]], do16[h], 0, 0) for h in heads]
            for h in heads:
                dqd_ref[rows, sls[h]] = _dot(do16[h], s16[h], 1, 1)
                dkd_ref[rows, sls[h]] = _dot(vn_ref[rows, sls[h]], ds16[h], 1, 1)
            dvn = [ptdo[h][:CHUNK, :] + ptdo[h][CHUNK:, :] + kds[h] for h in heads]
            dvn16 = [_b16(a) for a in dvn]
            wdv = [_dot(w_ref[rows, sls[h]], dvn16[h], 0, 0) for h in heads]
            for h in heads:
                dvn_ref[rows, sls[h]] = dvn[h]
                dw_ref[rows, sls[h]] = -_dot(dvn16[h], s16[h], 1, 1)
                tot = jnp.sum(jnp.sum(sh_ref[c, h] * ds[h], axis=1, keepdims=True), axis=0, keepdims=True)
                ddec_ref[c * 8:(c + 1) * 8, sls[h]] = jnp.broadcast_to(tot, (8, HD))
            ds = [ds[h] * jnp.exp(g_ref[last, sls[h]]) + qdo[h] - wdv[h] for h in heads]
        for h in heads:
            dstate[h] = ds[h]

    npair = t // SCAN_ROWS
    blk = pl.BlockSpec((SCAN_ROWS, GW), lambda i: (npair - 1 - i, 0))
    return pl.pallas_call(
        body, grid=(npair,),
        in_specs=[blk] * 7 + [pl.BlockSpec((SCAN_CHUNKS, N_HEADS, HD, HD), lambda i: (npair - 1 - i, 0, 0, 0))],
        out_specs=[blk] * 4 + [pl.BlockSpec((8 * SCAN_CHUNKS, GW), lambda i: (npair - 1 - i, 0))],
        out_shape=[_sds((t, GW), F32)] * 4 + [_sds((n * 8, GW), F32)],
        scratch_shapes=[pltpu.VMEM((N_HEADS, HD, HD), F32)], name=name,
        compiler_params=_params(1))(do, w, p, qd, kd, gc_b, vn, s_hist)


def _delta_prep_bwd(name, qn, kn, vv, beta_b, gc_b, tinv, u, w, vn, do, dvn, dqd, dkd, dw, ddec):
    t = qn.shape[0]

    def body(q_ref, k_ref, v_ref, b_ref, g_ref, t_ref, u_ref, w_ref, vn_ref, do_ref, dvn_ref, dqd_ref,
             dkd_ref, dw_ref, ddec_ref, dq_ref, dk_ref, dv_ref, dbeta_ref, dg_ref):
        ii, jj, causal, strict = _pair_masks()
        suffix = ((ii // CHUNK) == (jj // CHUNK)) & (jj >= ii)
        first = ii < CHUNK
        rs = lambda a: jnp.sum(a, axis=1, keepdims=True)
        sls = [slice(hh * HD, (hh + 1) * HD) for hh in range(HEADS_PER_STEP)]
        xs = [_dot3(t_ref[:, sl], dvn_ref[:, sl], 0, 0) for sl in sls]
        ys = [_dot3(t_ref[:, sl], dw_ref[:, sl], 0, 0) for sl in sls]
        k16s = [_b16(k_ref[:, sl]) for sl in sls]
        kks = [_dot(k16, k16, 1, 1) for k16 in k16s]
        qks = [_dot(_b16(q_ref[:, sl]), k16, 1, 1) for sl, k16 in zip(sls, k16s)]
        dps = [jnp.where(causal, _dot(_b16(do_ref[:, sl]), vn_ref[:, sl], 1, 1), 0.0) for sl in sls]
        das = [-jnp.where(strict, _dot(_b16(x), _b16(u_ref[:, sl]), 1, 1) + _dot(_b16(y), w_ref[:, sl], 1, 1), 0.0)
               for sl, x, y in zip(sls, xs, ys)]
        for hh, sl in enumerate(sls):
            q, k, v, beta, gc = q_ref[:, sl], k_ref[:, sl], v_ref[:, sl], b_ref[:, sl], g_ref[:, sl]
            last_a, last_b = g_ref[CHUNK - 1:CHUNK, sl], g_ref[PAIR - 1:PAIR, sl]
            dmat, gam, e2 = _decay_parts(gc, last_a, last_b, ii, jj, causal)
            q16, k16 = _b16(q), k16s[hh]
            kk, qk, dp, x, y, da = kks[hh], qks[hh], dps[hh], xs[hh], ys[hh], das[hh]
            dqd, dkd = dqd_ref[:, sl], dkd_ref[:, sl]
            dpd16 = _b16(dp * dmat)
            dkk16 = _b16(da * beta * dmat)
            dq_ref[:, sl] = gam * dqd + _dot(dpd16, k16, 1, 0)
            dk_ref[:, sl] = (e2 * dkd + _dot(dpd16, q16, 0, 0) + beta * gam * y
                             + _dot(dkk16, k16, 1, 0) + _dot(dkk16, k16, 0, 0))
            dv_ref[:, sl] = beta * x
            dbeta = rs(v * x) + rs(k * gam * y) + rs(da * kk * dmat)
            dbeta_ref[:, sl] = jnp.broadcast_to(dbeta, (PAIR, HD))
            m = (dp * qk + da * beta * kk) * dmat
            dgam = rs(q * dqd) + rs(k * beta * y)
            de2 = rs(k * dkd)
            colsum = _to_col(jnp.sum(m, axis=0, keepdims=True), ii, jj)
            te2 = de2 * e2
            dgc = rs(m) - colsum + gam * dgam - te2
            tail_a = jnp.sum(jnp.where(first, te2, 0.0), axis=0, keepdims=True)
            tail_b = jnp.sum(jnp.where(first, 0.0, te2), axis=0, keepdims=True)
            dgc = dgc + jnp.where(ii == CHUNK - 1, tail_a + ddec_ref[0:1, sl] * jnp.exp(last_a), 0.0)
            dgc = dgc + jnp.where(ii == PAIR - 1, tail_b + ddec_ref[8:9, sl] * jnp.exp(last_b), 0.0)
            dgc_row = _to_row(dgc, ii, jj)
            dg = jnp.sum(jnp.where(suffix, jnp.broadcast_to(dgc_row, (PAIR, PAIR)), 0.0), axis=1, keepdims=True)
            dg_ref[:, sl] = jnp.broadcast_to(dg, (PAIR, HD))

    blk = pl.BlockSpec((PAIR, HEADS_PER_STEP * HD), lambda i, h: (i, h))
    return pl.pallas_call(
        body, grid=(t // PAIR, N_HEADS // HEADS_PER_STEP),
        in_specs=[blk] * 14 + [pl.BlockSpec((16, HEADS_PER_STEP * HD), lambda i, h: (i, h))], out_specs=[blk] * 5,
        out_shape=[_sds((t, GW), F32)] * 5, name=name,
        compiler_params=_params(2))(qn, kn, vv, beta_b, gc_b, tinv, u, w, vn, do, dvn, dqd, dkd, dw, ddec)


def _rope_tables(name, pos_col, inv_row):
    t = pos_col.shape[0]
    tm = min(1024, t)

    def body(pos_ref, inv_ref, cos_ref, sin_ref):
        ang = pos_ref[...].astype(F32) * inv_ref[...]
        lane = _iota2(ang.shape, 1)
        cos_ref[...] = jnp.cos(ang)
        sin_ref[...] = jnp.where(lane < HD // 2, -1.0, 1.0) * jnp.sin(ang)

    tab = pl.BlockSpec((tm, HD), lambda i: (i, 0))
    return pl.pallas_call(
        body, grid=(t // tm,), in_specs=[pl.BlockSpec((tm, 1), lambda i: (i, 0)), pl.BlockSpec((1, HD), lambda i: (0, 0))],
        out_specs=[tab, tab], out_shape=[_sds((t, HD), F32)] * 2, name=name,
        compiler_params=_params(1))(pos_col, inv_row)


def _head_rms(xh, wv):
    return xh * lax.rsqrt(jnp.mean(xh * xh, axis=-1, keepdims=True) + EPS) * wv


def _qk_fwd(name, proj, pair_blk, wq_row, wk_row, cos_t, sin_t):
    t = proj.shape[0]
    tm = min(256, t)

    def body(x_ref, wq_ref, wk_ref, cos_ref, sin_ref, q_ref, k_ref):
        cos, sin = cos_ref[...], sin_ref[...]
        for o_ref, w_ref, base in ((q_ref, wq_ref, 0), (k_ref, wk_ref, GW)):
            for h in range(N_HEADS):
                y = _head_rms(x_ref[:, base + h * HD:base + (h + 1) * HD], w_ref[...])
                o_ref[:, h * HD:(h + 1) * HD] = y * cos + pltpu.roll(y, HD // 2, 1) * sin

    vec = pl.BlockSpec((1, HD), lambda i: (0, 0))
    tab = pl.BlockSpec((tm, HD), lambda i: (i, 0))
    wide = pl.BlockSpec((tm, GW), lambda i: (i, 0))
    return pl.pallas_call(
        body, grid=(t // tm,),
        in_specs=[pl.BlockSpec((tm, 2 * GW), lambda i: (i, pair_blk)), vec, vec, tab, tab],
        out_specs=[wide, wide], out_shape=[_sds((t, GW), F32)] * 2, name=name,
        compiler_params=_params(1))(proj, wq_row, wk_row, cos_t, sin_t)


def _qk_bwd(name, proj, pair_blk, wq_row, wk_row, cos_t, sin_t, dq_full, dk_full, dproj):
    t = proj.shape[0]
    tm = min(256, t)

    def body(x_ref, wq_ref, wk_ref, cos_ref, sin_ref, dq_ref, dk_ref, dproj_ref, dx_ref, dwq_ref, dwk_ref):
        cos, sin = cos_ref[...], sin_ref[...]

        @pl.when(pl.program_id(0) == 0)
        def _():
            dwq_ref[...] = jnp.zeros_like(dwq_ref)
            dwk_ref[...] = jnp.zeros_like(dwk_ref)

        for dy_ref, w_ref, dw_ref, base in ((dq_ref, wq_ref, dwq_ref, 0), (dk_ref, wk_ref, dwk_ref, GW)):
            dw = jnp.zeros((1, HD), F32)
            for h in range(N_HEADS):
                dy = dy_ref[:, h * HD:(h + 1) * HD]
                dy = dy * cos - pltpu.roll(dy, HD // 2, 1) * sin
                _, vjp = jax.vjp(_head_rms, x_ref[:, base + h * HD:base + (h + 1) * HD], w_ref[...])
                dx, dwh = vjp(dy)
                dw = dw + dwh
                dx_ref[:, base + h * HD:base + (h + 1) * HD] = dx.astype(BF16)
            dw_ref[...] += dw

    vec = pl.BlockSpec((1, HD), lambda i: (0, 0))
    tab = pl.BlockSpec((tm, HD), lambda i: (i, 0))
    wide = pl.BlockSpec((tm, GW), lambda i: (i, 0))
    pair = pl.BlockSpec((tm, 2 * GW), lambda i: (i, pair_blk))
    return pl.pallas_call(
        body, grid=(t // tm,), in_specs=[pair, vec, vec, tab, tab, wide, wide, ANY],
        out_specs=[pair, vec, vec],
        out_shape=[_sds(dproj.shape, BF16), _sds((1, HD), F32), _sds((1, HD), F32)], input_output_aliases={7: 0},
        name=name, compiler_params=_params(1))(proj, wq_row, wk_row, cos_t, sin_t, dq_full, dk_full, dproj)


def _cast_into(name, x, dproj, blk_idx):
    t = x.shape[0]
    tm = min(512, t)

    def body(x_ref, dproj_ref, o_ref):
        o_ref[...] = x_ref[...].astype(BF16)

    return pl.pallas_call(
        body, grid=(t // tm,), in_specs=[pl.BlockSpec((tm, GW), lambda i: (i, 0)), ANY],
        out_specs=pl.BlockSpec((tm, GW), lambda i: (i, blk_idx)), out_shape=_sds(dproj.shape, BF16),
        input_output_aliases={1: 0}, name=name, compiler_params=_params(1))(x, dproj)


GROUP = SPAN * max(DILATIONS)
SCALE = HD ** -0.5
TILE_BATCH = 8


def _band_mask(lo):
    qi = _iota2((SPAN, 2 * SPAN), 0)
    ki = _iota2((SPAN, 2 * SPAN), 1)
    return (ki >= qi) & (ki <= qi + SPAN) & (ki >= lo)


def _tiles():
    return [(pi, r, u, rho) for pi, r in enumerate(DILATIONS) for rho in range(r) for u in range(GROUP // (SPAN * r))]


def _rows(r, u, rho):
    return pl.ds(u * SPAN * r + rho, SPAN, stride=r) if r > 1 else pl.ds(u * SPAN, SPAN)


def _attn_fwd(name, q, k, v, v_blk):
    t = q.shape[0]

    def body(qc_ref, kc_ref, vc_ref, kp_ref, vp_ref, ob_ref, lse_ref, o_scr, l_scr):
        mask_in = _band_mask(0)
        mask_edge = _band_mask(jnp.where(pl.program_id(0) == 0, SPAN, 0))
        tiles = _tiles()
        k_own = v_own = None
        for b0 in range(0, len(tiles), TILE_BATCH):
            work = []
            for pi, r, u, rho in tiles[b0:b0 + TILE_BATCH]:
                rows = _rows(r, u, rho)
                if u > 0:
                    k_prev, v_prev, mask = k_own, v_own, mask_in
                else:
                    prows = _rows(r, GROUP // (SPAN * r) - 1, rho)
                    k_prev, v_prev, mask = kp_ref[prows, :].astype(BF16), vp_ref[prows, :].astype(BF16), mask_edge
                k_own, v_own = kc_ref[rows, :].astype(BF16), vc_ref[rows, :].astype(BF16)
                work.append((pi, rows, mask, qc_ref[rows, :].astype(BF16), jnp.concatenate([k_prev, k_own], axis=0),
                             jnp.concatenate([v_prev, v_own], axis=0)))
            scores = [_dot(qt, kcat, 1, 1) for _, _, _, qt, kcat, _ in work]
            soft = []
            for (_, _, mask, _, _, _), s in zip(work, scores):
                s = jnp.where(mask, s * SCALE, NEG)
                m = jnp.max(s, axis=1, keepdims=True)
                p = jnp.exp(s - m)
                soft.append((m, _b16(p), jnp.sum(p, axis=1, keepdims=True)))
            outs = [_dot(p, vcat, 1, 0) for (_, p, _), (_, _, _, _, _, vcat) in zip(soft, work)]
            for (pi, rows, _, _, _, _), (m, _, den), o in zip(work, soft, outs):
                o_scr[pi, rows, :] = o / den
                l_scr[pi, rows, :] = jnp.broadcast_to(m + jnp.log(den), (SPAN, HD))
        step = 256
        for c in range(GROUP // step):
            sl = pl.ds(c * step, step)
            ob, lse = _merge([o_scr[i, sl, :] for i in range(3)], [l_scr[i, sl, :] for i in range(3)])
            ob_ref[sl, :] = ob
            lse_ref[sl, :] = lse

    cur = pl.BlockSpec((GROUP, HD), lambda g, h: (g, h))
    prev = pl.BlockSpec((GROUP, HD), lambda g, h: (jnp.maximum(g - 1, 0), h))
    vcur = pl.BlockSpec((GROUP, HD), lambda g, h: (g, v_blk * N_HEADS + h))
    vprev = pl.BlockSpec((GROUP, HD), lambda g, h: (jnp.maximum(g - 1, 0), v_blk * N_HEADS + h))
    return pl.pallas_call(
        body, grid=(t // GROUP, N_HEADS), in_specs=[cur, cur, vcur, prev, vprev], out_specs=[cur, cur],
        out_shape=[_sds((t, GW), F32), _sds((t, GW), F32)],
        scratch_shapes=[pltpu.VMEM((3, GROUP, HD), F32), pltpu.VMEM((3, GROUP, HD), F32)], name=name,
        compiler_params=_params(2))(q, k, v, k, v)


def _attn_bwd(name, q, k, v, v_blk, do, lse, delta):
    t = q.shape[0]
    ng = t // GROUP

    def probs(work):
        scores = [_dot(qt, kcat, 1, 1) for qt, _, _, _, kcat, _, _ in work]
        dps = [_dot(dot, vcat, 1, 1) for _, dot, _, _, _, vcat, _ in work]
        out = []
        for (_, _, lt, dlt, kcat, _, mask), s, dp in zip(work, scores, dps):
            wide = kcat.shape[0] // SPAN
            lw = jnp.concatenate([lt] * wide, axis=1) if wide > 1 else lt
            dw = jnp.concatenate([dlt] * wide, axis=1) if wide > 1 else dlt
            p = jnp.exp(jnp.where(mask, s * SCALE - lw, NEG))
            out.append((_b16(p * (dp - dw) * SCALE), _b16(p)))
        return out

    def body(qc_ref, kc_ref, vc_ref, doc_ref, lc_ref, dc_ref, kp_ref, vp_ref, qn_ref, don_ref, ln_ref, dn_ref,
             dq_ref, dk_ref, dv_ref):
        g = pl.program_id(0)
        mask_in = _band_mask(0)
        mask_edge = _band_mask(jnp.where(g == 0, SPAN, 0))
        dk_ref[...] = jnp.zeros_like(dk_ref)
        dv_ref[...] = jnp.zeros_like(dv_ref)
        tiles = _tiles()
        k_own = v_own = None
        for b0 in range(0, len(tiles), TILE_BATCH):
            where, work = [], []
            for pi, r, u, rho in tiles[b0:b0 + TILE_BATCH]:
                rows = _rows(r, u, rho)
                if u > 0:
                    prows, k_prev, v_prev, mask = _rows(r, u - 1, rho), k_own, v_own, mask_in
                else:
                    prows = _rows(r, GROUP // (SPAN * r) - 1, rho)
                    k_prev, v_prev, mask = kp_ref[prows, :].astype(BF16), vp_ref[prows, :].astype(BF16), mask_edge
                k_own, v_own = kc_ref[rows, :].astype(BF16), vc_ref[rows, :].astype(BF16)
                where.append((pi, u, rows, prows))
                work.append((qc_ref[rows, :].astype(BF16), doc_ref[rows, :].astype(BF16), lc_ref[rows, :], dc_ref[rows, :],
                             jnp.concatenate([k_prev, k_own], axis=0), jnp.concatenate([v_prev, v_own], axis=0), mask))
            dsp = probs(work)
            dqs = [_dot(ds, w[4], 1, 0) for (ds, _), w in zip(dsp, work)]
            dks = [_dot(ds, w[0], 0, 0) for (ds, _), w in zip(dsp, work)]
            dvs = [_dot(p, w[1], 0, 0) for (_, p), w in zip(dsp, work)]
            for (pi, u, rows, prows), dq_t, dk2, dv2 in zip(where, dqs, dks, dvs):
                if pi == 0:
                    dq_ref[rows, :] = dq_t
                else:
                    dq_ref[rows, :] += dq_t
                dk_ref[rows, :] += dk2[SPAN:, :]
                dv_ref[rows, :] += dv2[SPAN:, :]
                if u > 0:
                    dk_ref[prows, :] += dk2[:SPAN, :]
                    dv_ref[prows, :] += dv2[:SPAN, :]
        qi = _iota2((SPAN, SPAN), 0)
        ki = _iota2((SPAN, SPAN), 1)
        mask_next = (ki >= qi) & (ki < jnp.where(g == ng - 1, 0, SPAN))
        edge = [(r, rho) for r in DILATIONS for rho in range(r)]
        for b0 in range(0, len(edge), TILE_BATCH):
            where, work = [], []
            for r, rho in edge[b0:b0 + TILE_BATCH]:
                krows, qrows = _rows(r, GROUP // (SPAN * r) - 1, rho), _rows(r, 0, rho)
                where.append(krows)
                work.append((qn_ref[qrows, :].astype(BF16), don_ref[qrows, :].astype(BF16), ln_ref[qrows, :],
                             dn_ref[qrows, :], kc_ref[krows, :].astype(BF16), vc_ref[krows, :].astype(BF16), mask_next))
            dsp = probs(work)
            dks = [_dot(ds, w[0], 0, 0) for (ds, _), w in zip(dsp, work)]
            dvs = [_dot(p, w[1], 0, 0) for (_, p), w in zip(dsp, work)]
            for krows, dk1, dv1 in zip(where, dks, dvs):
                dk_ref[krows, :] += dk1
                dv_ref[krows, :] += dv1

    cur = pl.BlockSpec((GROUP, HD), lambda g, h: (g, h))
    prev = pl.BlockSpec((GROUP, HD), lambda g, h: (jnp.maximum(g - 1, 0), h))
    nxt = pl.BlockSpec((GROUP, HD), lambda g, h: (jnp.minimum(g + 1, ng - 1), h))
    vcur = pl.BlockSpec((GROUP, HD), lambda g, h: (g, v_blk * N_HEADS + h))
    vprev = pl.BlockSpec((GROUP, HD), lambda g, h: (jnp.maximum(g - 1, 0), v_blk * N_HEADS + h))
    return pl.pallas_call(
        body, grid=(ng, N_HEADS), in_specs=[cur, cur, vcur, cur, cur, cur, prev, vprev] + [nxt] * 4,
        out_specs=[cur] * 3,
        out_shape=[_sds((t, GW), F32)] * 3, name=name,
        compiler_params=_params(2))(q, k, v, do, lse, delta, k, v, q, do, lse, delta)


def _merge(os_, ls_):
    m = jnp.maximum(jnp.maximum(ls_[0], ls_[1]), ls_[2])
    ws = [jnp.exp(l - m) for l in ls_]
    tot = ws[0] + ws[1] + ws[2]
    ob = (ws[0] * os_[0] + ws[1] * os_[1] + ws[2] * os_[2]) / tot
    return ob, m + jnp.log(tot)


def _gated_norm(oa, z, wv):
    return _head_rms(oa, wv) * _silu(z)


def _mix_fwd(name, oa_raw, proj, z_blk, ob, w_dn, w_an):
    t = oa_raw.shape[0]
    tm = min(256, t)

    def body(oa_ref, z_ref, ob_ref, wd_ref, wa_ref, mix_ref):
        for h in range(N_HEADS):
            sl = slice(h * HD, (h + 1) * HD)
            mix_ref[:, sl] = _gated_norm(oa_ref[:, sl], z_ref[:, sl], wd_ref[...]).astype(BF16)
            mix_ref[:, GW + h * HD:GW + (h + 1) * HD] = _head_rms(ob_ref[:, sl], wa_ref[...]).astype(BF16)

    vec = pl.BlockSpec((1, HD), lambda i: (0, 0))
    wide = pl.BlockSpec((tm, GW), lambda i: (i, 0))
    return pl.pallas_call(
        body, grid=(t // tm,),
        in_specs=[wide, pl.BlockSpec((tm, GW), lambda i: (i, z_blk)), wide, vec, vec],
        out_specs=pl.BlockSpec((tm, 2 * GW), lambda i: (i, 0)),
        out_shape=_sds((t, 2 * GW), BF16), name=name,
        compiler_params=_params(1))(oa_raw, proj, ob, w_dn, w_an)


def _mix_bwd(name, dmixed, oa_raw, proj, z_blk, ob, w_dn, w_an, dep):
    t = oa_raw.shape[0]
    tm = min(256, t)

    def body(dm_ref, oa_ref, z_ref, ob_ref, wd_ref, wa_ref, dep_ref,
             doa_ref, dz_ref, dob_ref, dl_ref, dwd_ref, dwa_ref):
        dwd = jnp.zeros((1, HD), F32)
        dwa = jnp.zeros((1, HD), F32)
        for h in range(N_HEADS):
            sl = slice(h * HD, (h + 1) * HD)
            _, vjp = jax.vjp(_gated_norm, oa_ref[:, sl], z_ref[:, sl], wd_ref[...])
            doa, dz, dw1 = vjp(dm_ref[:, sl])
            doa_ref[:, sl] = doa
            dz_ref[:, sl] = dz.astype(BF16)
            dwd = dwd + dw1
            obh = ob_ref[:, sl]
            _, vjp2 = jax.vjp(_head_rms, obh, wa_ref[...])
            dob, dw2 = vjp2(dm_ref[:, GW + h * HD:GW + (h + 1) * HD])
            dwa = dwa + dw2
            dob_ref[:, sl] = dob
            dl_ref[:, sl] = jnp.broadcast_to(jnp.sum(dob * obh, axis=1, keepdims=True), (tm, HD))

        @pl.when(pl.program_id(0) == 0)
        def _():
            dwd_ref[...] = jnp.zeros_like(dwd_ref)
            dwa_ref[...] = jnp.zeros_like(dwa_ref)

        dwd_ref[...] += dwd
        dwa_ref[...] += dwa

    vec = pl.BlockSpec((1, HD), lambda i: (0, 0))
    wide = pl.BlockSpec((tm, GW), lambda i: (i, 0))
    return pl.pallas_call(
        body, grid=(t // tm,),
        in_specs=[pl.BlockSpec((tm, 2 * GW), lambda i: (i, 0)), wide, pl.BlockSpec((tm, GW), lambda i: (i, z_blk)),
                  wide, vec, vec, ANY],
        out_specs=[wide, pl.BlockSpec((tm, GW), lambda i: (i, z_blk)), wide, wide, vec, vec],
        out_shape=[_sds((t, GW), F32), _sds(proj.shape, BF16), _sds((t, GW), F32), _sds((t, GW), F32),
                   _sds((1, HD), F32), _sds((1, HD), F32)], name=name,
        compiler_params=_params(1))(dmixed, oa_raw, proj, ob, w_dn, w_an, dep)


def _halves(n):
    cut = (n // 256) * 128
    return [(0, cut), (cut, n)]


def _gate_up_swiglu(name, h2, w_gu_g):
    t, d = h2.shape
    n = w_gu_g.shape[2]
    per = N_DEV // 2
    tm = min(512, t)

    def body(a_ref, bg_ref, bu_ref, gu_ref, act_ref):
        a = a_ref[...]
        cuts = _halves(n)
        gs = [_dot(a, bg_ref[:, c0:c1], 1, 0) for c0, c1 in cuts]
        ups = [_dot(a, bu_ref[:, c0:c1], 1, 0) for c0, c1 in cuts]
        for (c0, c1), g, up in zip(cuts, gs, ups):
            gu_ref[0, :, c0:c1] = g.astype(BF16)
            gu_ref[1, :, c0:c1] = up.astype(BF16)
            act_ref[:, c0:c1] = (_silu(g) * up).astype(BF16)

    return pl.pallas_call(
        body, grid=(per, t // tm),
        in_specs=[pl.BlockSpec((tm, d), lambda j, i: (i, 0)), pl.BlockSpec((None, d, n), lambda j, i: (j, 0, 0)),
                  pl.BlockSpec((None, d, n), lambda j, i: (j + per, 0, 0))],
        out_specs=[pl.BlockSpec((2, tm, n), lambda j, i: (0, i, j)), pl.BlockSpec((tm, n), lambda j, i: (i, j))],
        out_shape=[_sds((2, t, per * n), BF16), _sds((t, per * n), BF16)], name=name,
        compiler_params=_params(2))(h2, w_gu_g, w_gu_g)


def _d_gate_up(name, dy16, w_down, gu3, dep):
    t, d = dy16.shape
    f = w_down.shape[0]
    tm, tn = min(1024, t), f // 4

    def body(a_ref, b_ref, g_ref, dep_ref, o_ref):
        a = a_ref[...]
        cuts = _halves(tn)
        dacts = [_dot(a, b_ref[c0:c1, :], 1, 1) for c0, c1 in cuts]
        for (c0, c1), dact in zip(cuts, dacts):
            g, up = g_ref[0, :, c0:c1].astype(F32), g_ref[1, :, c0:c1].astype(F32)
            sg = _sigmoid(g)
            o_ref[0, :, c0:c1] = (dact * up * sg * (1.0 + g * (1.0 - sg))).astype(BF16)
            o_ref[1, :, c0:c1] = (dact * g * sg).astype(BF16)

    return pl.pallas_call(
        body, grid=(f // tn, t // tm),
        in_specs=[pl.BlockSpec((tm, d), lambda j, i: (i, 0)), pl.BlockSpec((tn, d), lambda j, i: (j, 0)),
                  pl.BlockSpec((2, tm, tn), lambda j, i: (0, i, j)), ANY],
        out_specs=pl.BlockSpec((2, tm, tn), lambda j, i: (0, i, j)), out_shape=_sds((2, t, f), BF16), name=name,
        compiler_params=_params(2))(dy16, w_down, gu3, dep)


def _d_h2(name, dgu3, w_gu_g, dep):
    _, t, f = dgu3.shape
    n_dev, d, n = w_gu_g.shape
    per = n_dev // 2
    tm, tn = min(512, t), 512

    def body(g_ref, u_ref, b_ref, dep_ref, o_ref):
        acc = None
        for s in range(n_dev):
            a_ref = g_ref if s < per else u_ref
            part = _dot(a_ref[:, (s % per) * n:(s % per + 1) * n], b_ref[s], 1, 1)
            acc = part if acc is None else acc + part
        o_ref[...] = acc

    return pl.pallas_call(
        body, grid=(d // tn, t // tm),
        in_specs=[pl.BlockSpec((None, tm, f), lambda j, i: (0, i, 0)), pl.BlockSpec((None, tm, f), lambda j, i: (1, i, 0)),
                  pl.BlockSpec((n_dev, tn, n), lambda j, i: (0, j, 0)), ANY],
        out_specs=pl.BlockSpec((tm, tn), lambda j, i: (i, j)), out_shape=_sds((t, d), F32), name=name,
        compiler_params=_params(2))(dgu3, dgu3, w_gu_g, dep)


def _out_proj_norm(name, mixed, w_out, x, w_norm):
    t, d = x.shape
    kdim = mixed.shape[1]
    tm = min(512, t)

    def body(a_ref, b_ref, x_ref, w_ref, x1_ref, h_ref):
        x1 = x_ref[...] + _dot(a_ref[...], b_ref[...], 1, 0)
        x1_ref[...] = x1
        h_ref[...] = _rms_f(x1, w_ref[...]).astype(BF16)

    row = pl.BlockSpec((tm, d), lambda i: (i, 0))
    return pl.pallas_call(
        body, grid=(t // tm,),
        in_specs=[pl.BlockSpec((tm, kdim), lambda i: (i, 0)), pl.BlockSpec((kdim, d), lambda i: (0, 0)), row,
                  pl.BlockSpec((1, d), lambda i: (0, 0))],
        out_specs=[row, row], out_shape=[_sds((t, d), F32), _sds((t, d), BF16)], name=name,
        compiler_params=_params(1))(mixed, w_out, x, w_norm)


def _down_loss(name, act, w_down, x1, target):
    t, f = act.shape
    d = x1.shape[1]
    tm, tn = min(1024, t), 512

    def body(a_ref, b_ref, x_ref, t_ref, dy_ref, dy16_ref, l_ref):
        diff = _dot(a_ref[...], b_ref[...], 1, 0) + x_ref[...] - t_ref[...]
        dyv = diff * (1.0 / d)
        dy_ref[...] = dyv
        dy16_ref[...] = dyv.astype(BF16)
        tot = jnp.sum(jnp.sum(diff * diff, axis=1, keepdims=True), axis=0, keepdims=True) * (0.5 / d)

        @pl.when((pl.program_id(0) == 0) & (pl.program_id(1) == 0))
        def _():
            l_ref[...] = jnp.zeros_like(l_ref)

        l_ref[...] += jnp.broadcast_to(tot, (8, 128))

    tile = pl.BlockSpec((tm, tn), lambda i, j: (i, j))
    return pl.pallas_call(
        body, grid=(t // tm, d // tn),
        in_specs=[pl.BlockSpec((tm, f), lambda i, j: (i, 0)), pl.BlockSpec((f, tn), lambda i, j: (0, j)), tile, tile],
        out_specs=[tile, tile, pl.BlockSpec((8, 128), lambda i, j: (0, 0))],
        out_shape=[_sds((t, d), F32), _sds((t, d), BF16), _sds((8, 128), F32)], name=name,
        compiler_params=_params(2))(act, w_down, x1, target)


def _peer(me, k):
    pid = (me + k) % N_DEV
    return (pid // 4, (pid // 2) % 2, pid % 2)


def _my_id():
    return 4 * lax.axis_index("x") + 2 * lax.axis_index("y") + lax.axis_index("c")


def _exchange(name, arrays, scatter, dep):
    n = len(arrays)

    def body(*refs):
        ins, outs = refs[:n], refs[n + 1:2 * n + 1]
        send_sems, recv_sems, local_sems = refs[2 * n + 1:]
        me = _my_id()
        started = []
        for a in range(n):
            src = ins[a].at[me] if scatter[a] else ins[a]
            loc = pltpu.make_async_copy(src, outs[a].at[me], local_sems.at[a])
            loc.start()
            started.append(loc)
        remote = []
        for k in range(1, N_DEV):
            to = (me + k) % N_DEV
            for a in range(n):
                src = ins[a].at[to] if scatter[a] else ins[a]
                cp = pltpu.make_async_remote_copy(src_ref=src, dst_ref=outs[a].at[me],
                                                  send_sem=send_sems.at[a * (N_DEV - 1) + k - 1], recv_sem=recv_sems.at[a * (N_DEV - 1) + k - 1],
                                                  device_id=_peer(me, k), device_id_type=pl.DeviceIdType.MESH)
                cp.start()
                remote.append(cp)
        for k in range(1, N_DEV):
            frm = (me + N_DEV - k) % N_DEV
            for a in range(n):
                src = ins[a].at[frm] if scatter[a] else ins[a]
                pltpu.make_async_remote_copy(src_ref=src, dst_ref=outs[a].at[frm],
                                             send_sem=send_sems.at[a * (N_DEV - 1) + k - 1], recv_sem=recv_sems.at[a * (N_DEV - 1) + k - 1],
                                             device_id=_peer(me, k), device_id_type=pl.DeviceIdType.MESH).wait_recv()
        for cp in remote:
            cp.wait_send()
        for loc in started:
            loc.wait()

    out_shape = [_sds((N_DEV,) + (a.shape[1:] if sc else a.shape), a.dtype) for a, sc in zip(arrays, scatter)]
    return pl.pallas_call(
        body, in_specs=[ANY] * (n + 1), out_specs=[ANY] * n, out_shape=out_shape,
        scratch_shapes=[pltpu.SemaphoreType.DMA((n * (N_DEV - 1),)), pltpu.SemaphoreType.DMA((n * (N_DEV - 1),)),
                        pltpu.SemaphoreType.DMA((n,))],
        name=name)(*arrays, dep)


def _gather_two_level(name, arrays):
    n = len(arrays)
    per = N_DEV - 1

    def body(*refs):
        ins, outs = refs[:n], refs[n:2 * n]
        send_sems, recv_sems, local_sems = refs[2 * n:]
        x, y, c = lax.axis_index("x"), lax.axis_index("y"), lax.axis_index("c")
        me, sibling = (x, y, c), (x, y, 1 - c)
        flip = lambda v, on: v + on - 2 * v * on
        relayed = (flip(x, c), flip(y, 1 - c), c)
        other = (flip(x, 1 - c), flip(y, c), c)
        diagonal = (1 - x, 1 - y, c)
        k_relayed, k_other = 2 - c, 1 + c

        def copy(a, k, block, to, src=None):
            slot = outs[a].at[4 * block[0] + 2 * block[1] + block[2]]
            return pltpu.make_async_remote_copy(
                src_ref=slot if src is None else src, dst_ref=slot, send_sem=send_sems.at[a * per + k],
                recv_sem=recv_sems.at[a * per + k], device_id=to, device_id_type=pl.DeviceIdType.MESH)

        mine = [pltpu.make_async_copy(ins[a], outs[a].at[4 * x + 2 * y + c], local_sems.at[a]) for a in range(n)]
        for cp in mine:
            cp.start()
        sent = [copy(a, 0, me, sibling, src=ins[a]) for a in range(n)]
        sent += [copy(a, 1, me, (1 - x, y, c), src=ins[a]) for a in range(n)]
        sent += [copy(a, 2, me, (x, 1 - y, c), src=ins[a]) for a in range(n)]
        for cp in sent:
            cp.start()
        for a in range(n):
            copy(a, k_relayed, relayed, me).wait_recv()
            sent.append(copy(a, 3, relayed, other))
            sent.append(copy(a, 3 + k_relayed, relayed, sibling))
            sent[-2].start()
            sent[-1].start()
        for a in range(n):
            copy(a, k_other, other, me).wait_recv()
            sent.append(copy(a, 3 + k_other, other, sibling))
            sent[-1].start()
        for a in range(n):
            copy(a, 3, diagonal, me).wait_recv()
            sent.append(copy(a, 6, diagonal, sibling))
            sent[-1].start()
        for a in range(n):
            copy(a, 0, sibling, me).wait_recv()
            for j, chip in enumerate([(1 - x, y), (x, 1 - y), (1 - x, 1 - y)]):
                copy(a, 4 + j, (*chip, 1 - c), me).wait_recv()
        for cp in sent:
            cp.wait_send()
        for cp in mine:
            cp.wait()

    return pl.pallas_call(
        body, in_specs=[ANY] * n, out_specs=[ANY] * n,
        out_shape=[_sds((N_DEV,) + a.shape, a.dtype) for a in arrays],
        scratch_shapes=[pltpu.SemaphoreType.DMA((n * per,)), pltpu.SemaphoreType.DMA((n * per,)),
                        pltpu.SemaphoreType.DMA((n,))],
        name=name)(*arrays)


HBM = pl.BlockSpec(memory_space=pltpu.HBM)
SEM = pl.BlockSpec(memory_space=pltpu.SEMAPHORE)
EFFECT = pltpu.SideEffectType.DATAFLOW_SIDE_EFFECTING


def _remote_copies(srcs, lands, scatter, send_sems, recv_sems, me, incoming):
    out = []
    for k in range(1, N_DEV):
        other = (me + N_DEV - k) % N_DEV if incoming else (me + k) % N_DEV
        for a in range(len(srcs)):
            sem = a * (N_DEV - 1) + k - 1
            src = srcs[a].at[other] if scatter[a] else srcs[a]
            dst = lands[a].at[other if incoming else me]
            out.append(pltpu.make_async_remote_copy(src_ref=src, dst_ref=dst, send_sem=send_sems.at[sem],
                                                    recv_sem=recv_sems.at[sem], device_id=_peer(me, k),
                                                    device_id_type=pl.DeviceIdType.MESH))
    return out


def _exchange_start(name, arrays, scatter, dep):
    n = len(arrays)
    lands = [lax.empty((N_DEV,) + (a.shape[1:] if sc else a.shape), a.dtype) for a, sc in zip(arrays, scatter)]

    def body(*refs):
        srcs, land_refs = refs[:n], refs[n:2 * n]
        send_sems, recv_sems = refs[2 * n + 1], refs[2 * n + 2]
        token = refs[-1]
        for cp in _remote_copies(srcs, land_refs, scatter, send_sems, recv_sems, _my_id(), False):
            cp.start()
        token[...] = jnp.zeros_like(token)

    n_sem = n * (N_DEV - 1)
    out_shape = ([pltpu.SemaphoreType.DMA((n_sem,)), pltpu.SemaphoreType.DMA((n_sem,))]
                 + [pltpu.HBM(a.shape, a.dtype) for a in arrays] + [pltpu.HBM(l.shape, l.dtype) for l in lands]
                 + [_sds((8, 128), F32)])
    aliases = {i: 2 + i for i in range(2 * n)}
    args = [pltpu.with_memory_space_constraint(a, pltpu.HBM) for a in list(arrays) + lands] + [dep]
    res = pl.pallas_call(
        body, name=name, in_specs=[HBM] * (2 * n) + [ANY], out_shape=out_shape,
        out_specs=[SEM, SEM] + [HBM] * (2 * n) + [pl.BlockSpec(memory_space=pltpu.VMEM)],
        input_output_aliases=aliases, compiler_params=pltpu.CompilerParams(has_side_effects=EFFECT))(*args)
    return dict(send=res[0], recv=res[1], srcs=res[2:2 + n], lands=res[2 + n:2 + 2 * n], token=res[-1],
                scatter=scatter)


def _exchange_wait(name, started, after):
    n = len(started["srcs"])
    scatter = started["scatter"]

    def body(*refs):
        srcs, land_refs = refs[:n], refs[n:2 * n]
        send_sems, recv_sems = refs[2 * n], refs[2 * n + 1]
        me = _my_id()
        for cp in _remote_copies(srcs, land_refs, scatter, send_sems, recv_sems, me, False):
            cp.wait_send()
        for cp in _remote_copies(srcs, land_refs, scatter, send_sems, recv_sems, me, True):
            cp.wait_recv()

    arrs = list(started["srcs"]) + list(started["lands"])
    res = pl.pallas_call(
        body, name=name, in_specs=[HBM] * (2 * n) + [SEM, SEM, ANY],
        out_shape=[pltpu.HBM(a.shape, a.dtype) for a in arrs], out_specs=[HBM] * (2 * n),
        input_output_aliases={i: i for i in range(2 * n)},
        compiler_params=pltpu.CompilerParams(has_side_effects=EFFECT))(*arrs, started["send"], started["recv"], after)
    me = _my_id()
    out = []
    for src, land, sc in zip(res[:n], res[n:], scatter):
        own = lax.dynamic_index_in_dim(src, me, 0, keepdims=True) if sc else src[None]
        out.append(lax.dynamic_update_slice(land, own, (me,) + (0,) * (land.ndim - 1)))
    return out


def _adamw(name, parts, w, m, v):
    r, c = w.shape
    tr, tc = r, c
    if r % 8 == 0:
        tr = next(cand for cand in (128, 88, 64, 40, 8) if r % cand == 0)
    else:
        tc = 256
    c1 = 1.0 / (1.0 - ADAM_B1 ** ADAM_STEP)
    c2 = 1.0 / (1.0 - ADAM_B2 ** ADAM_STEP)

    def body(p_ref, w_ref, m_ref, v_ref, g_ref, d_ref, nm_ref, nv_ref):
        g = p_ref[0].astype(F32)
        for s in range(1, N_DEV):
            g = g + p_ref[s].astype(F32)
        mn = ADAM_B1 * m_ref[...] + (1.0 - ADAM_B1) * g
        vn = ADAM_B2 * v_ref[...] + (1.0 - ADAM_B2) * (g * g)
        g_ref[...] = g
        nm_ref[...] = mn
        nv_ref[...] = vn
        d_ref[...] = -ADAM_LR * ((mn * c1) / (jnp.sqrt(vn * c2) + ADAM_EPS) + ADAM_WD * w_ref[...])

    blk = pl.BlockSpec((tr, tc), lambda i, j: (i, j))
    return pl.pallas_call(
        body, grid=(r // tr, c // tc),
        in_specs=[pl.BlockSpec((N_DEV, tr, tc), lambda i, j: (0, i, j)), blk, blk, blk],
        out_specs=[blk] * 4, out_shape=[_sds((r, c), F32)] * 4, name=name,
        compiler_params=_params(2, VMEM_LIMIT))(parts, w, m, v)


def _pad_rows(a, rows):
    return jnp.pad(a, ((0, rows - a.shape[0]), (0, 0)))


def _lane_row(vec8, offset):
    return jnp.pad(vec8.reshape(1, 8), ((0, 0), (offset, HD - 8 - offset)))


def kernel(x, positions, attn_norm_w, w_in, conv_w, a_log, dt_bias, delta_out_norm_w, q_norm_w, k_norm_w, attn_out_norm_w, w_out, ffn_norm_w, w_gate_up, w_down, loss_target, m_attn_norm_w, m_w_in, m_conv_w, m_a_log, m_dt_bias, m_delta_out_norm_w, m_q_norm_w, m_k_norm_w, m_attn_out_norm_w, m_w_out, m_ffn_norm_w, m_w_gate_up, m_w_down, v_attn_norm_w, v_w_in, v_conv_w, v_a_log, v_dt_bias, v_delta_out_norm_w, v_q_norm_w, v_k_norm_w, v_attn_out_norm_w, v_w_out, v_ffn_norm_w, v_w_gate_up, v_w_down):
    x2 = x[0]
    t, d = x2.shape
    target = loss_target[0]
    pos_col = positions.reshape(t, 1)
    half = HD // 2
    inv = (ROPE_THETA ** (-np.arange(half, dtype=np.float32) / half)).astype(np.float32)
    inv_row = jnp.asarray(np.concatenate([inv, inv]).reshape(1, HD))

    n_in = w_in.shape[2]
    n_gu = w_gate_up.shape[2]
    w_in_g, conv_g = _gather_two_level("gather_in", [w_in[0].astype(BF16), _pad_rows(conv_w[0], 8)])
    out_fly = _exchange_start("gather_out_start", [w_out[0].astype(BF16)], [False], conv_g)
    gu_fly = _exchange_start("gather_gate_up_start", [w_gate_up[0].astype(BF16)], [False], out_fly["token"])
    down_fly = _exchange_start("gather_down_start", [w_down[0].astype(BF16)], [False], gu_fly["token"])
    n_main = 4 * GW
    n_small = 2 * N_HEADS
    segments = [(0, n_main, 0), (n_main + n_small, N_DEV * n_in, n_main), (n_main, n_main + n_small, 7 * GW)]
    pieces = []
    for lo, hi, _ in segments:
        f = lo
        while f < hi:
            j = f // n_in
            end = min(hi, (j + 1) * n_in)
            pieces.append(w_in_g[j][:, f - j * n_in:end - j * n_in])
            f = end
    w_cat = jnp.concatenate(pieces + [jnp.zeros((d, HD - n_small), BF16)], axis=1)
    n_cat = w_cat.shape[1]
    small_blk = (7 * GW) // HD
    conv_w8 =jnp.transpose(conv_g, (1, 0, 2)).reshape(8, 3 * GW)
    alog_row = _lane_row(a_log[0], 8)
    dtb_row = _lane_row(dt_bias[0], 8)

    tm = min(2048, t)
    h1 = _rms_fwd("norm1", x2, attn_norm_w, down_fly["token"])
    tmp, tnp = min(1024, t), n_cat // 3
    proj = _mm("in_proj", h1, w_cat, grid=(t // tmp, n_cat // tnp, 1),
               a_spec=pl.BlockSpec((tmp, d), lambda i, j, k: (i, 0)),
               b_spec=pl.BlockSpec((d, tnp), lambda i, j, k: (0, j)),
               o_spec=pl.BlockSpec((tmp, tnp), lambda i, j, k: (i, j)),
               out_shape=_sds((t, n_cat), F32), ca=1, cb=0, nk=1)
    qn = _conv_fwd("conv_q", proj, conv_w8, 0, True, HD ** -0.5)
    kn = _conv_fwd("conv_k", proj, conv_w8, 1, True, 1.0)
    vv = _conv_fwd("conv_v", proj, conv_w8, 2, False, 1.0)
    beta_b, gc_b = _gates_fwd("gates", proj, small_blk, alog_row, dtb_row)
    u, w, p, tinv, qd, kd = _delta_prep("delta_prep", qn, kn, vv, beta_b, gc_b)
    oa_raw, vn, s_hist = _delta_scan("delta_scan", u, w, p, qd, kd, gc_b)

    cos_t, sin_t = _rope_tables("rope_tables", pos_col, inv_row)
    aq, ak = _qk_fwd("attn_qk", proj, 2, q_norm_w, k_norm_w, cos_t, sin_t)
    ob, lse = _attn_fwd("attn_fwd", aq, ak, proj, 6)
    mixed = _mix_fwd("mix", oa_raw, proj, 3, ob, delta_out_norm_w, attn_out_norm_w)
    (w_out_g,) = _exchange_wait("gather_out_wait", out_fly, mixed)
    w_out_full = w_out_g.reshape(2 * GW, d)
    tn = 512
    x1, h2 = _out_proj_norm("out_proj", mixed, w_out_full, x2, ffn_norm_w)
    per = N_DEV // 2
    (w_gu_g,) = _exchange_wait("gather_gate_up_wait", gu_fly, h2)
    gu3, act = _gate_up_swiglu("gate_up", h2, w_gu_g)
    (w_down_g,) = _exchange_wait("gather_down_wait", down_fly, act)
    w_down_full = w_down_g.reshape(D_FF, d)
    tmd = min(1024, t)
    dy, dy16, loss_tile = _down_loss("down_proj", act, w_down_full, x1, target)
    loss = lax.psum(loss_tile[0, 0], ("x", "y", "c"))

    tk, nkt = t, 1
    g_down = _mm("g_down", act, dy16, dep=loss.reshape(1, 1), grid=(D_FF // 1408, d // 512, nkt),
                 a_spec=pl.BlockSpec((tk, 1408), lambda i, j, k: (k, i)),
                 b_spec=pl.BlockSpec((tk, 512), lambda i, j, k: (k, j)),
                 o_spec=pl.BlockSpec((1408, 512), lambda i, j, k: (i, j)),
                 out_shape=_sds((D_FF, d), F32), ca=0, cb=0, nk=nkt)
    down_g_fly = _exchange_start("reduce_down_start", [g_down.reshape(N_DEV, D_FF // N_DEV, d)], [True], dy16)
    dgu3 = _d_gate_up("d_gate_up", dy16, w_down_full, gu3, down_g_fly["token"])
    g_gu = _mm("g_gate_up", h2, dgu3, grid=(d // 512, N_DEV, nkt),
               a_spec=pl.BlockSpec((tk, 512), lambda i, j, k: (k, i)),
               b_spec=pl.BlockSpec((None, tk, n_gu), lambda i, j, k: (j // per, k, j % per)),
               o_spec=pl.BlockSpec((None, 512, n_gu), lambda i, j, k: (j, i, 0)),
               out_shape=_sds((N_DEV, d, n_gu), F32), ca=0, cb=0, nk=nkt)
    gu_g_fly = _exchange_start("reduce_gate_up_start", [g_gu], [True], dy16)
    dh2 = _d_h2("d_h2", dgu3, w_gu_g, gu_g_fly["token"])
    dx1, dx1_16, g_ffn_norm = _rms_bwd("norm2_bwd", x1, ffn_norm_w, dh2, dy)

    g_out = _mm("g_out", mixed, dx1_16, grid=((2 * GW) // 512, 1, nkt),
                a_spec=pl.BlockSpec((tk, 512), lambda i, j, k: (k, i)),
                b_spec=pl.BlockSpec((tk, d), lambda i, j, k: (k, 0)),
                o_spec=pl.BlockSpec((512, d), lambda i, j, k: (i, 0)),
                out_shape=_sds((2 * GW, d), F32), ca=0, cb=0, nk=nkt)
    out_g_fly = _exchange_start("reduce_out_start", [g_out.reshape(N_DEV, (2 * GW) // N_DEV, d)], [True], g_ffn_norm)
    dmixed = _mm("d_mixed", dx1_16, w_out_full, dep=out_g_fly["token"], grid=(t // tm, (2 * GW) // tn, 1),
                 a_spec=pl.BlockSpec((tm, d), lambda i, j, k: (i, 0)),
                 b_spec=pl.BlockSpec((tn, d), lambda i, j, k: (j, 0)),
                 o_spec=pl.BlockSpec((tm, tn), lambda i, j, k: (i, j)),
                 out_shape=_sds((t, 2 * GW), F32), ca=1, cb=1, nk=1)
    doa, dproj, dob, delta, g_dn, g_an = _mix_bwd("mix_bwd", dmixed, oa_raw, proj, 3, ob,
                                                  delta_out_norm_w, attn_out_norm_w, out_g_fly["token"])
    d_aq, d_ak, d_av = _attn_bwd("attn_bwd", aq, ak, proj, 6, dob, lse, delta)
    dproj, g_qn, g_kn = _qk_bwd("attn_qk_bwd", proj, 2, q_norm_w, k_norm_w, cos_t, sin_t, d_aq, d_ak, dproj)
    dproj = _cast_into("attn_v_bwd", d_av, dproj, 6)

    dvn, dqd, dkd, dw, ddec = _delta_scan_bwd("delta_scan_bwd", doa, w, p, qd, kd, gc_b, vn, s_hist)
    dqn, dkn, dvv, dbeta_b, dg_b = _delta_prep_bwd("delta_prep_bwd", qn, kn, vv, beta_b, gc_b, tinv, u, w, vn,
                                                   doa, dvn, dqd, dkd, dw, ddec)
    dproj, gcw_q = _conv_bwd("conv_q_bwd", proj, conv_w8, dqn, dproj, 0, True, HD ** -0.5)
    dproj, gcw_k = _conv_bwd("conv_k_bwd", proj, conv_w8, dkn, dproj, 1, True, 1.0)
    dproj, gcw_v = _conv_bwd("conv_v_bwd", proj, conv_w8, dvv, dproj, 2, False, 1.0)
    dproj, g_alog_row, g_dtb_row = _gates_bwd("gates_bwd", proj, small_blk, alog_row, dtb_row, dbeta_b, dg_b, dproj)
    tmc = 384
    g_cat = _mm("g_in", dproj, h1, grid=(n_cat // tmc, 1, nkt),
                a_spec=pl.BlockSpec((tk, tmc), lambda i, j, k: (k, i)),
                b_spec=pl.BlockSpec((tk, d), lambda i, j, k: (k, 0)),
                o_spec=pl.BlockSpec((tmc, d), lambda i, j, k: (i, 0)),
                out_shape=_sds((n_cat, d), BF16), ca=0, cb=0, nk=nkt)
    parts = []
    for j in range(N_DEV):
        cols = []
        for lo, hi, start in sorted(segments):
            a, b = max(lo, j * n_in), min(hi, (j + 1) * n_in)
            if a < b:
                cols.append(g_cat[start + a - lo:start + b - lo])
        parts.append(cols[0] if len(cols) == 1 else jnp.concatenate(cols, axis=0))
    g_in_parts = jnp.stack(parts)
    g_conv = jnp.concatenate([gcw_q, gcw_k, gcw_v], axis=1)
    n_cw = conv_w.shape[2]
    g_conv_parts = jnp.transpose(g_conv.reshape(8, N_DEV, n_cw), (1, 0, 2))
    in_g_fly = _exchange_start("reduce_in_start", [g_in_parts, g_conv_parts], [True] * 2, g_dtb_row)
    tmh1 = min(512, t)
    dh1 = _mm("d_h1", dproj, w_cat, dep=in_g_fly["token"], grid=(t // tmh1, d // 1024, 1),
              a_spec=pl.BlockSpec((tmh1, n_cat), lambda i, j, k: (i, 0)),
              b_spec=pl.BlockSpec((1024, n_cat), lambda i, j, k: (j, 0)),
              o_spec=pl.BlockSpec((tmh1, 1024), lambda i, j, k: (i, j)),
              out_shape=_sds((t, d), F32), ca=1, cb=1, nk=1)
    grad_x, _, g_attn_norm = _rms_bwd("norm1_bwd", x2, attn_norm_w, dh1, dx1)

    small_rows = [g_attn_norm.reshape(d // HD, HD), g_ffn_norm.reshape(d // HD, HD), g_dn, g_qn, g_kn, g_an,
                  g_alog_row, g_dtb_row]
    small_pack = _pad_rows(jnp.concatenate(small_rows, axis=0), 40)
    (r_down,) = _exchange_wait("reduce_down_wait", down_g_fly, grad_x)
    (r_gu,) = _exchange_wait("reduce_gate_up_wait", gu_g_fly, grad_x)
    (r_out,) = _exchange_wait("reduce_out_wait", out_g_fly, grad_x)
    res_gu = [a[None] for a in _adamw("adamw_gate_up", r_gu, w_gate_up[0], m_w_gate_up[0], v_w_gate_up[0])]
    res_down = [a[None] for a in _adamw("adamw_down", r_down, w_down[0], m_w_down[0], v_w_down[0])]
    res_out = [a[None] for a in _adamw("adamw_out", r_out, w_out[0], m_w_out[0], v_w_out[0])]
    done = (res_gu[3][0, :1, :1] + res_down[3][0, :1, :1] + res_out[3][0, :1, :1])
    r_in, r_conv = _exchange_wait("reduce_in_wait", in_g_fly, done)
    upd_in = _adamw("adamw_in", r_in, jnp.transpose(w_in[0]), jnp.transpose(m_w_in[0]), jnp.transpose(v_w_in[0]))
    res_in = [jnp.transpose(a)[None] for a in upd_in]
    (r_small,) = _exchange("gather_small_grads", [small_pack], [False], upd_in[0])

    def pack_small(an, fn, dn, qn_, kn_, aon, al, db):
        rows = [an.reshape(d // HD, HD), fn.reshape(d // HD, HD), dn, qn_, kn_, aon,
                _lane_row(al[0], 8), _lane_row(db[0], 8)]
        return _pad_rows(jnp.concatenate(rows, axis=0), 40)

    def unpack_small(pk):
        nr = d // HD
        return dict(attn_norm_w=pk[:nr].reshape(1, d), ffn_norm_w=pk[nr:2 * nr].reshape(1, d),
                    delta_out_norm_w=pk[2 * nr:2 * nr + 1], q_norm_w=pk[2 * nr + 1:2 * nr + 2],
                    k_norm_w=pk[2 * nr + 2:2 * nr + 3], attn_out_norm_w=pk[2 * nr + 3:2 * nr + 4],
                    a_log=pk[2 * nr + 4:2 * nr + 5, 8:16], dt_bias=pk[2 * nr + 5:2 * nr + 6, 8:16])

    res_small = _adamw("adamw_small", r_small,
                       pack_small(attn_norm_w, ffn_norm_w, delta_out_norm_w, q_norm_w, k_norm_w, attn_out_norm_w, a_log, dt_bias),
                       pack_small(m_attn_norm_w, m_ffn_norm_w, m_delta_out_norm_w, m_q_norm_w, m_k_norm_w, m_attn_out_norm_w, m_a_log, m_dt_bias),
                       pack_small(v_attn_norm_w, v_ffn_norm_w, v_delta_out_norm_w, v_q_norm_w, v_k_norm_w, v_attn_out_norm_w, v_a_log, v_dt_bias))
    small = [unpack_small(a) for a in res_small]
    res_conv =[a[None, :4] for a in _adamw("adamw_conv", r_conv, _pad_rows(conv_w[0], 8), _pad_rows(m_conv_w[0], 8),
                                            _pad_rows(v_conv_w[0], 8))]

    outs = [loss, grad_x[None]]
    for i in range(4):
        s = small[i]
        outs += [s["attn_norm_w"], res_in[i], res_conv[i], s["a_log"], s["dt_bias"], s["delta_out_norm_w"],
                 s["q_norm_w"], s["k_norm_w"], s["attn_out_norm_w"], res_out[i], s["ffn_norm_w"], res_gu[i],
                 res_down[i]]
    return tuple(outs)
```

```python
import numpy as np
import jax
import jax.numpy as jnp
from jax import lax
from jax.experimental import pallas as pl
from jax.experimental.pallas import tpu as pltpu

F32 = jnp.float32
BF16 = jnp.bfloat16

N_DEV = 8
N_HEADS = 8
HD = 128
GW = N_HEADS * HD
CHUNK = 64
PAIR = 2 * CHUNK
SCAN_CHUNKS = 4
SCAN_ROWS = SCAN_CHUNKS * CHUNK
SPAN = 128
DILATIONS = (1, 4, 16)
ROPE_THETA = 10000.0
EPS = 1e-6
D_FF = 5632
ADAM_LR, ADAM_B1, ADAM_B2, ADAM_EPS, ADAM_WD, ADAM_STEP = 0.001, 0.9, 0.999, 1e-8, 0.01, 10
NEG = -1e30
VMEM_LIMIT = 56 * 1024 * 1024
ANY = pl.BlockSpec(memory_space=pl.ANY)
HEADS_PER_STEP = 8


def _params(n_grid, vmem=VMEM_LIMIT):
    return pltpu.CompilerParams(dimension_semantics=("arbitrary",) * n_grid, vmem_limit_bytes=vmem)


def _sds(shape, dtype):
    return jax.ShapeDtypeStruct(tuple(shape), dtype)


def _sigmoid(x):
    return 1.0 / (1.0 + jnp.exp(-x))


def _silu(x):
    return x * _sigmoid(x)


def _softplus(x):
    return jnp.maximum(x, 0.0) + jnp.log(1.0 + jnp.exp(-jnp.abs(x)))


def _dot(a, b, ca, cb):
    return lax.dot_general(a, b, (((ca,), (cb,)), ((), ())), preferred_element_type=F32)


def _b16(x):
    return x if x.dtype == BF16 else x.astype(BF16)


def _split(x):
    hi = x.astype(BF16)
    return hi, (x - hi.astype(F32)).astype(BF16)


def _dot3(a, b, ca, cb):
    a_hi, a_lo = _split(a)
    b_hi, b_lo = _split(b)
    return _dot(a_hi, b_hi, ca, cb) + (_dot(a_hi, b_lo, ca, cb) + _dot(a_lo, b_hi, ca, cb))


def _iota2(shape, axis):
    return lax.broadcasted_iota(jnp.int32, shape, axis)


def _mm(name, a, b, *, grid, a_spec, b_spec, o_spec, out_shape, ca, cb, nk, dep=None):
    assert nk == 1 and grid[2] == 1

    def body(*refs):
        refs[-1][...] = _dot(_b16(refs[0][...]), _b16(refs[1][...]), ca, cb).astype(refs[-1].dtype)

    in_specs = [a_spec, b_spec] + ([ANY] if dep is not None else [])
    args = (a, b) + ((dep,) if dep is not None else ())
    return pl.pallas_call(body, grid=grid, in_specs=in_specs, out_specs=o_spec, out_shape=out_shape,
                          name=name, compiler_params=_params(3))(*args)


def _rms_f(xv, wv):
    return xv * lax.rsqrt(jnp.mean(xv * xv, axis=-1, keepdims=True) + EPS) * wv


def _rms_fwd(name, x, w, dep):
    t, d = x.shape
    tm = min(512, t)

    def body(x_ref, w_ref, dep_ref, o_ref):
        o_ref[...] = _rms_f(x_ref[...], w_ref[...]).astype(BF16)

    row = pl.BlockSpec((tm, d), lambda i: (i, 0))
    vec = pl.BlockSpec((1, d), lambda i: (0, 0))
    return pl.pallas_call(body, grid=(t // tm,), in_specs=[row, vec, ANY], out_specs=row,
                          out_shape=_sds((t, d), BF16), name=name, compiler_params=_params(1))(x, w, dep)


def _rms_bwd(name, x, w, dh, res):
    t, d = x.shape
    tm = min(256, t)

    def body(x_ref, w_ref, dh_ref, res_ref, dx_ref, dx16_ref, dw_ref):
        _, vjp = jax.vjp(_rms_f, x_ref[...], w_ref[...])
        dxv, dwv = vjp(dh_ref[...])
        dxv = dxv + res_ref[...]
        dx_ref[...] = dxv
        dx16_ref[...] = dxv.astype(BF16)

        @pl.when(pl.program_id(0) == 0)
        def _():
            dw_ref[...] = jnp.zeros_like(dw_ref)

        dw_ref[...] += dwv

    row = pl.BlockSpec((tm, d), lambda i: (i, 0))
    vec = pl.BlockSpec((1, d), lambda i: (0, 0))
    return pl.pallas_call(body, grid=(t // tm,), in_specs=[row, vec, row, row], out_specs=[row, row, vec],
                          out_shape=[_sds((t, d), F32), _sds((t, d), BF16), _sds((1, d), F32)], name=name,
                          compiler_params=_params(1))(x, w, dh, res)


def _shift_rows(x, s):
    t = x.shape[0]
    r = pltpu.roll(x, s % t, 0)
    row8 = _iota2((8, x.shape[1]), 0)
    if s > 0:
        return jnp.concatenate([jnp.where(row8 >= s, r[:8], 0.0), r[8:]], axis=0)
    return jnp.concatenate([r[:t - 8], jnp.where(row8 < 8 + s, r[t - 8:], 0.0)], axis=0)


def _conv_taps(xv, w_ref):
    c = w_ref[3:4, :] * xv
    for s in (1, 2, 3):
        c = c + w_ref[3 - s:4 - s, :] * _shift_rows(xv, s)
    return c


def _post_conv(c, l2, scale):
    y = _silu(c)
    if l2:
        y = y * lax.rsqrt(jnp.sum(y * y, axis=-1, keepdims=True) + EPS) * scale
    return y


def _conv_fwd(name, proj, conv_w8, group, l2, scale):
    t = proj.shape[0]

    def body(x_ref, w_ref, o_ref):
        o_ref[...] = _post_conv(_conv_taps(x_ref[...], w_ref), l2, scale)

    return pl.pallas_call(
        body, grid=(N_HEADS,),
        in_specs=[pl.BlockSpec((t, HD), lambda h: (0, h + group * N_HEADS)),
                  pl.BlockSpec((8, HD), lambda h: (0, h + group * N_HEADS))],
        out_specs=pl.BlockSpec((t, HD), lambda h: (0, h)),
        out_shape=_sds((t, GW), F32), name=name, compiler_params=_params(1, VMEM_LIMIT))(proj, conv_w8)


def _conv_bwd(name, proj, conv_w8, dn, dproj, group, l2, scale):
    t = proj.shape[0]

    def body(x_ref, w_ref, dn_ref, dproj_ref, dx_ref, dw_ref):
        xv = x_ref[...]
        c = _conv_taps(xv, w_ref)
        _, vjp = jax.vjp(lambda cc: _post_conv(cc, l2, scale), c)
        (dc,) = vjp(dn_ref[...])
        dx = w_ref[3:4, :] * dc
        dw = jnp.zeros((8, HD), F32)
        rid = _iota2((8, HD), 0)
        dw = dw + jnp.where(rid == 3, jnp.sum(dc * xv, axis=0, keepdims=True), 0.0)
        for s in (1, 2, 3):
            dx = dx + w_ref[3 - s:4 - s, :] * _shift_rows(dc, -s)
            dw = dw + jnp.where(rid == 3 - s, jnp.sum(dc * _shift_rows(xv, s), axis=0, keepdims=True), 0.0)
        dx_ref[...] = dx.astype(BF16)
        dw_ref[...] = dw

    return pl.pallas_call(
        body, grid=(N_HEADS,),
        in_specs=[pl.BlockSpec((t, HD), lambda h: (0, h + group * N_HEADS)),
                  pl.BlockSpec((8, HD), lambda h: (0, h + group * N_HEADS)),
                  pl.BlockSpec((t, HD), lambda h: (0, h)), ANY],
        out_specs=[pl.BlockSpec((t, HD), lambda h: (0, h + group * N_HEADS)), pl.BlockSpec((8, HD), lambda h: (0, h))],
        out_shape=[_sds(dproj.shape, BF16), _sds((8, GW), F32)], input_output_aliases={3: 0}, name=name,
        compiler_params=_params(1, VMEM_LIMIT))(proj, conv_w8, dn, dproj)


def _chunk_cumsum(g, rows):
    pos = rows % CHUNK
    s = 1
    while s < CHUNK:
        g = g + jnp.where(pos >= s, pltpu.roll(g, s, 0), 0.0)
        s *= 2
    return g


def _gates_fwd(name, proj, small_blk, alog_row, dtb_row):
    t = proj.shape[0]
    tm = min(256, t)

    def body(s_ref, a_ref, b_ref, beta_ref, gc_ref):
        sm = s_ref[...]
        beta = _sigmoid(sm)
        g = -jnp.exp(a_ref[...]) * _softplus(sm + b_ref[...])
        gc = _chunk_cumsum(g, _iota2((tm, HD), 0))
        lane = _iota2((tm, HD), 1)
        for h in range(N_HEADS):
            bcol = jnp.sum(jnp.where(lane == h, beta, 0.0), axis=1, keepdims=True)
            gcol = jnp.sum(jnp.where(lane == 8 + h, gc, 0.0), axis=1, keepdims=True)
            beta_ref[:, h * HD:(h + 1) * HD] = jnp.broadcast_to(bcol, (tm, HD))
            gc_ref[:, h * HD:(h + 1) * HD] = jnp.broadcast_to(gcol, (tm, HD))

    vec = pl.BlockSpec((1, HD), lambda i: (0, 0))
    wide = pl.BlockSpec((tm, GW), lambda i: (i, 0))
    return pl.pallas_call(
        body, grid=(t // tm,),
        in_specs=[pl.BlockSpec((tm, HD), lambda i: (i, small_blk)), vec, vec], out_specs=[wide, wide],
        out_shape=[_sds((t, GW), F32), _sds((t, GW), F32)], name=name,
        compiler_params=_params(1))(proj, alog_row, dtb_row)


def _gates_bwd(name, proj, small_blk, alog_row, dtb_row, dbeta_b, dg_b, dproj):
    t = proj.shape[0]
    tm = min(256, t)

    def body(s_ref, a_ref, b_ref, db_ref, dg_ref, dproj_ref, ds_ref, da_ref, dbias_ref):
        sm = s_ref[...]
        lane = _iota2((tm, HD), 1)
        db = jnp.zeros((tm, HD), F32)
        dg = jnp.zeros((tm, HD), F32)
        for h in range(N_HEADS):
            db = db + jnp.where(lane == h, db_ref[:, h * HD:(h + 1) * HD], 0.0)
            dg = dg + jnp.where(lane == 8 + h, dg_ref[:, h * HD:(h + 1) * HD], 0.0)
        beta = _sigmoid(sm)
        ea = jnp.exp(a_ref[...])
        pre = sm + b_ref[...]
        g = -ea * _softplus(pre)
        dpre = dg * (-ea) * _sigmoid(pre)
        ds_ref[...] = (db * beta * (1.0 - beta) + dpre).astype(BF16)

        @pl.when(pl.program_id(0) == 0)
        def _():
            da_ref[...] = jnp.zeros_like(da_ref)
            dbias_ref[...] = jnp.zeros_like(dbias_ref)

        da_ref[...] += jnp.sum(dg * g, axis=0, keepdims=True)
        dbias_ref[...] += jnp.sum(dpre, axis=0, keepdims=True)

    vec = pl.BlockSpec((1, HD), lambda i: (0, 0))
    wide = pl.BlockSpec((tm, GW), lambda i: (i, 0))
    return pl.pallas_call(
        body, grid=(t // tm,),
        in_specs=[pl.BlockSpec((tm, HD), lambda i: (i, small_blk)), vec, vec, wide, wide, ANY],
        out_specs=[pl.BlockSpec((tm, HD), lambda i: (i, small_blk)), vec, vec],
        out_shape=[_sds(dproj.shape, BF16), _sds((1, HD), F32), _sds((1, HD), F32)],
        input_output_aliases={5: 0}, name=name,
        compiler_params=_params(1))(proj, alog_row, dtb_row, dbeta_b, dg_b, dproj)


def _pair_masks():
    ii = _iota2((PAIR, PAIR), 0)
    jj = _iota2((PAIR, PAIR), 1)
    same = (ii // CHUNK) == (jj // CHUNK)
    return ii, jj, same & (ii >= jj), same & (ii > jj)


def _to_row(col_b, ii, jj):
    return jnp.sum(jnp.where(ii == jj, col_b, 0.0), axis=0, keepdims=True)


def _to_col(row, ii, jj):
    return jnp.sum(jnp.where(ii == jj, jnp.broadcast_to(row, (PAIR, PAIR)), 0.0), axis=1, keepdims=True)


def _decay_parts(gc, last_a, last_b, ii, jj, causal):
    diff = gc - _to_row(gc, ii, jj)
    dmat = jnp.where(causal, jnp.exp(jnp.where(causal, diff, 0.0)), 0.0)
    glast = jnp.where(ii < CHUNK, last_a, last_b)
    return dmat, jnp.exp(gc), jnp.exp(glast - gc)


def _unit_lower_inverse(lows, ii, jj):
    eye = jnp.where(ii == jj, 1.0, 0.0)
    mm = lambda xs, ys: [_dot3(a, b, 1, 0) for a, b in zip(xs, ys)]
    plus = lambda xs: [eye + a for a in xs]
    minus = lambda xs: [eye - a for a in xs]
    d1 = [jnp.where((ii // 16) == (jj // 16), low, 0.0) for low in lows]
    d2 = mm(d1, d1)
    a = mm(minus(d1), plus(d2))
    d4 = mm(d2, d2)
    a = mm(a, plus(d4))
    d8 = mm(d4, d4)
    td = mm(a, plus(d8))
    n1 = mm(td, [low - d for low, d in zip(lows, d1)])
    n2 = mm(n1, n1)
    return mm(mm(minus(n1), plus(n2)), td)


def _delta_prep(name, qn, kn, vv, beta_b, gc_b):
    t = qn.shape[0]

    def body(q_ref, k_ref, v_ref, b_ref, g_ref, u_ref, w_ref, p_ref, t_ref, qd_ref, kd_ref):
        ii, jj, causal, strict = _pair_masks()
        sls = [slice(hh * HD, (hh + 1) * HD) for hh in range(HEADS_PER_STEP)]
        lows = []
        for sl in sls:
            q, k, beta = q_ref[:, sl], k_ref[:, sl], b_ref[:, sl]
            dmat, gam, e2 = _decay_parts(g_ref[:, sl], g_ref[CHUNK - 1:CHUNK, sl], g_ref[PAIR - 1:PAIR, sl],
                                         ii, jj, causal)
            k16 = _b16(k)
            lows.append(jnp.where(strict, beta * _dot(k16, k16, 1, 1) * dmat, 0.0))
            p_ref[:, sl] = jnp.where(causal, _dot(_b16(q), k16, 1, 1) * dmat, 0.0).astype(BF16)
            qd_ref[:, sl] = (q * gam).astype(BF16)
            kd_ref[:, sl] = (k * e2).astype(BF16)
        for sl, tinv in zip(sls, _unit_lower_inverse(lows, ii, jj)):
            beta = b_ref[:, sl]
            t_ref[:, sl] = tinv
            u_ref[:, sl] = _dot3(tinv, v_ref[:, sl] * beta, 1, 0)
            w_ref[:, sl] = _dot3(tinv, k_ref[:, sl] * (beta * jnp.exp(g_ref[:, sl])), 1, 0).astype(BF16)

    blk = pl.BlockSpec((PAIR, HEADS_PER_STEP * HD), lambda i, h: (i, h))
    return pl.pallas_call(
        body, grid=(t // PAIR, N_HEADS // HEADS_PER_STEP), in_specs=[blk] * 5, out_specs=[blk] * 6,
        out_shape=[_sds((t, GW), F32), _sds((t, GW), BF16), _sds((t, GW), BF16), _sds((t, GW), F32),
                   _sds((t, GW), BF16), _sds((t, GW), BF16)],
        name=name, compiler_params=_params(2))(qn, kn, vv, beta_b, gc_b)


def _delta_scan(name, u, w, p, qd, kd, gc_b):
    t = u.shape[0]
    n = t // CHUNK

    def body(u_ref, w_ref, p_ref, qd_ref, kd_ref, g_ref, o_ref, vn_ref, sh_ref, state):
        @pl.when(pl.program_id(0) == 0)
        def _():
            state[...] = jnp.zeros_like(state)

        sls = [slice(h * HD, (h + 1) * HD) for h in range(N_HEADS)]
        heads = range(N_HEADS)
        s = [state[h] for h in heads]
        for c in range(SCAN_CHUNKS):
            rows = slice(c * CHUNK, (c + 1) * CHUNK)
            last = slice((c + 1) * CHUNK - 1, (c + 1) * CHUNK)
            for h in heads:
                sh_ref[c, h] = s[h]
            s16 = [_b16(a) for a in s]
            ws = [_dot(w_ref[rows, sls[h]], s16[h], 1, 0) for h in heads]
            qs = [_dot(qd_ref[rows, sls[h]], s16[h], 1, 0) for h in heads]
            vn16 = [_b16(u_ref[rows, sls[h]] - ws[h]) for h in heads]
            pv = [_dot(p_ref[rows, sls[h]], jnp.concatenate([vn16[h], vn16[h]], axis=0), 1, 0) for h in heads]
            kv = [_dot(kd_ref[rows, sls[h]], vn16[h], 0, 0) for h in heads]
            for h in heads:
                o_ref[rows, sls[h]] = qs[h] + pv[h]
                vn_ref[rows, sls[h]] = vn16[h]
            s = [s[h] * jnp.exp(g_ref[last, sls[h]]) + kv[h] for h in heads]
        for h in heads:
            state[h] = s[h]

    blk = pl.BlockSpec((SCAN_ROWS, GW), lambda i: (i, 0))
    return pl.pallas_call(
        body, grid=(t // SCAN_ROWS,), in_specs=[blk] * 6,
        out_specs=[blk, blk, pl.BlockSpec((SCAN_CHUNKS, N_HEADS, HD, HD), lambda i: (i, 0, 0, 0))],
        out_shape=[_sds((t, GW), F32), _sds((t, GW), BF16), _sds((n, N_HEADS, HD, HD), F32)],
        scratch_shapes=[pltpu.VMEM((N_HEADS, HD, HD), F32)], name=name,
        compiler_params=_params(1))(u, w, p, qd, kd, gc_b)


def _delta_scan_bwd(name, do, w, p, qd, kd, gc_b, vn, s_hist):
    t = do.shape[0]
    n = t // CHUNK

    def body(do_ref, w_ref, p_ref, qd_ref, kd_ref, g_ref, vn_ref, sh_ref,
             dvn_ref, dqd_ref, dkd_ref, dw_ref, ddec_ref, dstate):
        @pl.when(pl.program_id(0) == 0)
        def _():
            dstate[...] = jnp.zeros_like(dstate)

        sls = [slice(h * HD, (h + 1) * HD) for h in range(N_HEADS)]
        heads = range(N_HEADS)
        ds = [dstate[h] for h in heads]
        for c in reversed(range(SCAN_CHUNKS)):
            rows = slice(c * CHUNK, (c + 1) * CHUNK)
            last = slice((c + 1) * CHUNK - 1, (c + 1) * CHUNK)
            ds16 = [_b16(a) for a in ds]
            s16 = [_b16(sh_ref[c, h]) for h in heads]
            do16 = [_b16(do_ref[rows, sls[h]]) for h in heads]
            ptdo = [_dot(p_ref[rows, sls[h]], do16[h], 0, 0) for h in heads]
            kds = [_dot(kd_ref[rows, sls[h]], ds16[h], 1, 0) for h in heads]
            qdo = [_dot(qd_ref[rows, sls[h]], do16[h], 0, 0) for h in heads]
            for h in heads:
                dqd_ref[rows, sls[h]] = _dot(do16[h], s16[h], 1, 1)
                dkd_ref[rows, sls[h]] = _dot(vn_ref[rows, sls[h]], ds16[h], 1, 1)
            dvn = [ptdo[h][:CHUNK, :] + ptdo[h][CHUNK:, :] + kds[h] for h in heads]
            dvn16 = [_b16(a) for a in dvn]
            wdv = [_dot(w_ref[rows, sls[h]], dvn16[h], 0, 0) for h in heads]
            for h in heads:
                dvn_ref[rows, sls[h]] = dvn[h]
                dw_ref[rows, sls[h]] = -_dot(dvn16[h], s16[h], 1, 1)
                tot = jnp.sum(jnp.sum(sh_ref[c, h] * ds[h], axis=1, keepdims=True), axis=0, keepdims=True)
                ddec_ref[c * 8:(c + 1) * 8, sls[h]] = jnp.broadcast_to(tot, (8, HD))
            ds = [ds[h] * jnp.exp(g_ref[last, sls[h]]) + qdo[h] - wdv[h] for h in heads]
        for h in heads:
            dstate[h] = ds[h]

    npair = t // SCAN_ROWS
    blk = pl.BlockSpec((SCAN_ROWS, GW), lambda i: (npair - 1 - i, 0))
    return pl.pallas_call(
        body, grid=(npair,),
        in_specs=[blk] * 7 + [pl.BlockSpec((SCAN_CHUNKS, N_HEADS, HD, HD), lambda i: (npair - 1 - i, 0, 0, 0))],
        out_specs=[blk] * 4 + [pl.BlockSpec((8 * SCAN_CHUNKS, GW), lambda i: (npair - 1 - i, 0))],
        out_shape=[_sds((t, GW), F32)] * 4 + [_sds((n * 8, GW), F32)],
        scratch_shapes=[pltpu.VMEM((N_HEADS, HD, HD), F32)], name=name,
        compiler_params=_params(1))(do, w, p, qd, kd, gc_b, vn, s_hist)


def _delta_prep_bwd(name, qn, kn, vv, beta_b, gc_b, tinv, u, w, vn, do, dvn, dqd, dkd, dw, ddec):
    t = qn.shape[0]

    def body(q_ref, k_ref, v_ref, b_ref, g_ref, t_ref, u_ref, w_ref, vn_ref, do_ref, dvn_ref, dqd_ref,
             dkd_ref, dw_ref, ddec_ref, dq_ref, dk_ref, dv_ref, dbeta_ref, dg_ref):
        ii, jj, causal, strict = _pair_masks()
        suffix = ((ii // CHUNK) == (jj // CHUNK)) & (jj >= ii)
        first = ii < CHUNK
        rs = lambda a: jnp.sum(a, axis=1, keepdims=True)
        sls = [slice(hh * HD, (hh + 1) * HD) for hh in range(HEADS_PER_STEP)]
        xs = [_dot3(t_ref[:, sl], dvn_ref[:, sl], 0, 0) for sl in sls]
        ys = [_dot3(t_ref[:, sl], dw_ref[:, sl], 0, 0) for sl in sls]
        k16s = [_b16(k_ref[:, sl]) for sl in sls]
        kks = [_dot(k16, k16, 1, 1) for k16 in k16s]
        qks = [_dot(_b16(q_ref[:, sl]), k16, 1, 1) for sl, k16 in zip(sls, k16s)]
        dps = [jnp.where(causal, _dot(_b16(do_ref[:, sl]), vn_ref[:, sl], 1, 1), 0.0) for sl in sls]
        das = [-jnp.where(strict, _dot(_b16(x), _b16(u_ref[:, sl]), 1, 1) + _dot(_b16(y), w_ref[:, sl], 1, 1), 0.0)
               for sl, x, y in zip(sls, xs, ys)]
        for hh, sl in enumerate(sls):
            q, k, v, beta, gc = q_ref[:, sl], k_ref[:, sl], v_ref[:, sl], b_ref[:, sl], g_ref[:, sl]
            last_a, last_b = g_ref[CHUNK - 1:CHUNK, sl], g_ref[PAIR - 1:PAIR, sl]
            dmat, gam, e2 = _decay_parts(gc, last_a, last_b, ii, jj, causal)
            q16, k16 = _b16(q), k16s[hh]
            kk, qk, dp, x, y, da = kks[hh], qks[hh], dps[hh], xs[hh], ys[hh], das[hh]
            dqd, dkd = dqd_ref[:, sl], dkd_ref[:, sl]
            dpd16 = _b16(dp * dmat)
            dkk16 = _b16(da * beta * dmat)
            dq_ref[:, sl] = gam * dqd + _dot(dpd16, k16, 1, 0)
            dk_ref[:, sl] = (e2 * dkd + _dot(dpd16, q16, 0, 0) + beta * gam * y
                             + _dot(dkk16, k16, 1, 0) + _dot(dkk16, k16, 0, 0))
            dv_ref[:, sl] = beta * x
            dbeta = rs(v * x) + rs(k * gam * y) + rs(da * kk * dmat)
            dbeta_ref[:, sl] = jnp.broadcast_to(dbeta, (PAIR, HD))
            m = (dp * qk + da * beta * kk) * dmat
            dgam = rs(q * dqd) + rs(k * beta * y)
            de2 = rs(k * dkd)
            colsum = _to_col(jnp.sum(m, axis=0, keepdims=True), ii, jj)
            te2 = de2 * e2
            dgc = rs(m) - colsum + gam * dgam - te2
            tail_a = jnp.sum(jnp.where(first, te2, 0.0), axis=0, keepdims=True)
            tail_b = jnp.sum(jnp.where(first, 0.0, te2), axis=0, keepdims=True)
            dgc = dgc + jnp.where(ii == CHUNK - 1, tail_a + ddec_ref[0:1, sl] * jnp.exp(last_a), 0.0)
            dgc = dgc + jnp.where(ii == PAIR - 1, tail_b + ddec_ref[8:9, sl] * jnp.exp(last_b), 0.0)
            dgc_row = _to_row(dgc, ii, jj)
            dg = jnp.sum(jnp.where(suffix, jnp.broadcast_to(dgc_row, (PAIR, PAIR)), 0.0), axis=1, keepdims=True)
            dg_ref[:, sl] = jnp.broadcast_to(dg, (PAIR, HD))

    blk = pl.BlockSpec((PAIR, HEADS_PER_STEP * HD), lambda i, h: (i, h))
    return pl.pallas_call(
        body, grid=(t // PAIR, N_HEADS // HEADS_PER_STEP),
        in_specs=[blk] * 14 + [pl.BlockSpec((16, HEADS_PER_STEP * HD), lambda i, h: (i, h))], out_specs=[blk] * 5,
        out_shape=[_sds((t, GW), F32)] * 5, name=name,
        compiler_params=_params(2))(qn, kn, vv, beta_b, gc_b, tinv, u, w, vn, do, dvn, dqd, dkd, dw, ddec)


def _rope_tables(name, pos_col, inv_row):
    t = pos_col.shape[0]
    tm = min(1024, t)

    def body(pos_ref, inv_ref, cos_ref, sin_ref):
        ang = pos_ref[...].astype(F32) * inv_ref[...]
        lane = _iota2(ang.shape, 1)
        cos_ref[...] = jnp.cos(ang)
        sin_ref[...] = jnp.where(lane < HD // 2, -1.0, 1.0) * jnp.sin(ang)

    tab = pl.BlockSpec((tm, HD), lambda i: (i, 0))
    return pl.pallas_call(
        body, grid=(t // tm,), in_specs=[pl.BlockSpec((tm, 1), lambda i: (i, 0)), pl.BlockSpec((1, HD), lambda i: (0, 0))],
        out_specs=[tab, tab], out_shape=[_sds((t, HD), F32)] * 2, name=name,
        compiler_params=_params(1))(pos_col, inv_row)


def _head_rms(xh, wv):
    return xh * lax.rsqrt(jnp.mean(xh * xh, axis=-1, keepdims=True) + EPS) * wv


def _qk_fwd(name, proj, pair_blk, wq_row, wk_row, cos_t, sin_t):
    t = proj.shape[0]
    tm = min(256, t)

    def body(x_ref, wq_ref, wk_ref, cos_ref, sin_ref, q_ref, k_ref):
        cos, sin = cos_ref[...], sin_ref[...]
        for o_ref, w_ref, base in ((q_ref, wq_ref, 0), (k_ref, wk_ref, GW)):
            for h in range(N_HEADS):
                y = _head_rms(x_ref[:, base + h * HD:base + (h + 1) * HD], w_ref[...])
                o_ref[:, h * HD:(h + 1) * HD] = y * cos + pltpu.roll(y, HD // 2, 1) * sin

    vec = pl.BlockSpec((1, HD), lambda i: (0, 0))
    tab = pl.BlockSpec((tm, HD), lambda i: (i, 0))
    wide = pl.BlockSpec((tm, GW), lambda i: (i, 0))
    return pl.pallas_call(
        body, grid=(t // tm,),
        in_specs=[pl.BlockSpec((tm, 2 * GW), lambda i: (i, pair_blk)), vec, vec, tab, tab],
        out_specs=[wide, wide], out_shape=[_sds((t, GW), F32)] * 2, name=name,
        compiler_params=_params(1))(proj, wq_row, wk_row, cos_t, sin_t)


def _qk_bwd(name, proj, pair_blk, wq_row, wk_row, cos_t, sin_t, dq_full, dk_full, dproj):
    t = proj.shape[0]
    tm = min(256, t)

    def body(x_ref, wq_ref, wk_ref, cos_ref, sin_ref, dq_ref, dk_ref, dproj_ref, dx_ref, dwq_ref, dwk_ref):
        cos, sin = cos_ref[...], sin_ref[...]

        @pl.when(pl.program_id(0) == 0)
        def _():
            dwq_ref[...] = jnp.zeros_like(dwq_ref)
            dwk_ref[...] = jnp.zeros_like(dwk_ref)

        for dy_ref, w_ref, dw_ref, base in ((dq_ref, wq_ref, dwq_ref, 0), (dk_ref, wk_ref, dwk_ref, GW)):
            dw = jnp.zeros((1, HD), F32)
            for h in range(N_HEADS):
                dy = dy_ref[:, h * HD:(h + 1) * HD]
                dy = dy * cos - pltpu.roll(dy, HD // 2, 1) * sin
                _, vjp = jax.vjp(_head_rms, x_ref[:, base + h * HD:base + (h + 1) * HD], w_ref[...])
                dx, dwh = vjp(dy)
                dw = dw + dwh
                dx_ref[:, base + h * HD:base + (h + 1) * HD] = dx.astype(BF16)
            dw_ref[...] += dw

    vec = pl.BlockSpec((1, HD), lambda i: (0, 0))
    tab = pl.BlockSpec((tm, HD), lambda i: (i, 0))
    wide = pl.BlockSpec((tm, GW), lambda i: (i, 0))
    pair = pl.BlockSpec((tm, 2 * GW), lambda i: (i, pair_blk))
    return pl.pallas_call(
        body, grid=(t // tm,), in_specs=[pair, vec, vec, tab, tab, wide, wide, ANY],
        out_specs=[pair, vec, vec],
        out_shape=[_sds(dproj.shape, BF16), _sds((1, HD), F32), _sds((1, HD), F32)], input_output_aliases={7: 0},
        name=name, compiler_params=_params(1))(proj, wq_row, wk_row, cos_t, sin_t, dq_full, dk_full, dproj)


def _cast_into(name, x, dproj, blk_idx):
    t = x.shape[0]
    tm = min(512, t)

    def body(x_ref, dproj_ref, o_ref):
        o_ref[...] = x_ref[...].astype(BF16)

    return pl.pallas_call(
        body, grid=(t // tm,), in_specs=[pl.BlockSpec((tm, GW), lambda i: (i, 0)), ANY],
        out_specs=pl.BlockSpec((tm, GW), lambda i: (i, blk_idx)), out_shape=_sds(dproj.shape, BF16),
        input_output_aliases={1: 0}, name=name, compiler_params=_params(1))(x, dproj)


GROUP = SPAN * max(DILATIONS)
SCALE = HD ** -0.5
TILE_BATCH = 8


def _band_mask(lo):
    qi = _iota2((SPAN, 2 * SPAN), 0)
    ki = _iota2((SPAN, 2 * SPAN), 1)
    return (ki >= qi) & (ki <= qi + SPAN) & (ki >= lo)


def _tiles():
    return [(pi, r, u, rho) for pi, r in enumerate(DILATIONS) for rho in range(r) for u in range(GROUP // (SPAN * r))]


def _rows(r, u, rho):
    return pl.ds(u * SPAN * r + rho, SPAN, stride=r) if r > 1 else pl.ds(u * SPAN, SPAN)


def _attn_fwd(name, q, k, v, v_blk):
    t = q.shape[0]

    def body(qc_ref, kc_ref, vc_ref, kp_ref, vp_ref, ob_ref, lse_ref, o_scr, l_scr):
        mask_in = _band_mask(0)
        mask_edge = _band_mask(jnp.where(pl.program_id(0) == 0, SPAN, 0))
        tiles = _tiles()
        k_own = v_own = None
        for b0 in range(0, len(tiles), TILE_BATCH):
            work = []
            for pi, r, u, rho in tiles[b0:b0 + TILE_BATCH]:
                rows = _rows(r, u, rho)
                if u > 0:
                    k_prev, v_prev, mask = k_own, v_own, mask_in
                else:
                    prows = _rows(r, GROUP // (SPAN * r) - 1, rho)
                    k_prev, v_prev, mask = kp_ref[prows, :].astype(BF16), vp_ref[prows, :].astype(BF16), mask_edge
                k_own, v_own = kc_ref[rows, :].astype(BF16), vc_ref[rows, :].astype(BF16)
                work.append((pi, rows, mask, qc_ref[rows, :].astype(BF16), jnp.concatenate([k_prev, k_own], axis=0),
                             jnp.concatenate([v_prev, v_own], axis=0)))
            scores = [_dot(qt, kcat, 1, 1) for _, _, _, qt, kcat, _ in work]
            soft = []
            for (_, _, mask, _, _, _), s in zip(work, scores):
                s = jnp.where(mask, s * SCALE, NEG)
                m = jnp.max(s, axis=1, keepdims=True)
                p = jnp.exp(s - m)
                soft.append((m, _b16(p), jnp.sum(p, axis=1, keepdims=True)))
            outs = [_dot(p, vcat, 1, 0) for (_, p, _), (_, _, _, _, _, vcat) in zip(soft, work)]
            for (pi, rows, _, _, _, _), (m, _, den), o in zip(work, soft, outs):
                o_scr[pi, rows, :] = o / den
                l_scr[pi, rows, :] = jnp.broadcast_to(m + jnp.log(den), (SPAN, HD))
        step = 256
        for c in range(GROUP // step):
            sl = pl.ds(c * step, step)
            ob, lse = _merge([o_scr[i, sl, :] for i in range(3)], [l_scr[i, sl, :] for i in range(3)])
            ob_ref[sl, :] = ob
            lse_ref[sl, :] = lse

    cur = pl.BlockSpec((GROUP, HD), lambda g, h: (g, h))
    prev = pl.BlockSpec((GROUP, HD), lambda g, h: (jnp.maximum(g - 1, 0), h))
    vcur = pl.BlockSpec((GROUP, HD), lambda g, h: (g, v_blk * N_HEADS + h))
    vprev = pl.BlockSpec((GROUP, HD), lambda g, h: (jnp.maximum(g - 1, 0), v_blk * N_HEADS + h))
    return pl.pallas_call(
        body, grid=(t // GROUP, N_HEADS), in_specs=[cur, cur, vcur, prev, vprev], out_specs=[cur, cur],
        out_shape=[_sds((t, GW), F32), _sds((t, GW), F32)],
        scratch_shapes=[pltpu.VMEM((3, GROUP, HD), F32), pltpu.VMEM((3, GROUP, HD), F32)], name=name,
        compiler_params=_params(2))(q, k, v, k, v)


def _attn_bwd(name, q, k, v, v_blk, do, lse, delta):
    t = q.shape[0]
    ng = t // GROUP

    def probs(work):
        scores = [_dot(qt, kcat, 1, 1) for qt, _, _, _, kcat, _, _ in work]
        dps = [_dot(dot, vcat, 1, 1) for _, dot, _, _, _, vcat, _ in work]
        out = []
        for (_, _, lt, dlt, kcat, _, mask), s, dp in zip(work, scores, dps):
            wide = kcat.shape[0] // SPAN
            lw = jnp.concatenate([lt] * wide, axis=1) if wide > 1 else lt
            dw = jnp.concatenate([dlt] * wide, axis=1) if wide > 1 else dlt
            p = jnp.exp(jnp.where(mask, s * SCALE - lw, NEG))
            out.append((_b16(p * (dp - dw) * SCALE), _b16(p)))
        return out

    def body(qc_ref, kc_ref, vc_ref, doc_ref, lc_ref, dc_ref, kp_ref, vp_ref, qn_ref, don_ref, ln_ref, dn_ref,
             dq_ref, dk_ref, dv_ref):
        g = pl.program_id(0)
        mask_in = _band_mask(0)
        mask_edge = _band_mask(jnp.where(g == 0, SPAN, 0))
        dk_ref[...] = jnp.zeros_like(dk_ref)
        dv_ref[...] = jnp.zeros_like(dv_ref)
        tiles = _tiles()
        k_own = v_own = None
        for b0 in range(0, len(tiles), TILE_BATCH):
            where, work = [], []
            for pi, r, u, rho in tiles[b0:b0 + TILE_BATCH]:
                rows = _rows(r, u, rho)
                if u > 0:
                    prows, k_prev, v_prev, mask = _rows(r, u - 1, rho), k_own, v_own, mask_in
                else:
                    prows = _rows(r, GROUP // (SPAN * r) - 1, rho)
                    k_prev, v_prev, mask = kp_ref[prows, :].astype(BF16), vp_ref[prows, :].astype(BF16), mask_edge
                k_own, v_own = kc_ref[rows, :].astype(BF16), vc_ref[rows, :].astype(BF16)
                where.append((pi, u, rows, prows))
                work.append((qc_ref[rows, :].astype(BF16), doc_ref[rows, :].astype(BF16), lc_ref[rows, :], dc_ref[rows, :],
                             jnp.concatenate([k_prev, k_own], axis=0), jnp.concatenate([v_prev, v_own], axis=0), mask))
            dsp = probs(work)
            dqs = [_dot(ds, w[4], 1, 0) for (ds, _), w in zip(dsp, work)]
            dks = [_dot(ds, w[0], 0, 0) for (ds, _), w in zip(dsp, work)]
            dvs = [_dot(p, w[1], 0, 0) for (_, p), w in zip(dsp, work)]
            for (pi, u, rows, prows), dq_t, dk2, dv2 in zip(where, dqs, dks, dvs):
                if pi == 0:
                    dq_ref[rows, :] = dq_t
                else:
                    dq_ref[rows, :] += dq_t
                dk_ref[rows, :] += dk2[SPAN:, :]
                dv_ref[rows, :] += dv2[SPAN:, :]
                if u > 0:
                    dk_ref[prows, :] += dk2[:SPAN, :]
                    dv_ref[prows, :] += dv2[:SPAN, :]
        qi = _iota2((SPAN, SPAN), 0)
        ki = _iota2((SPAN, SPAN), 1)
        mask_next = (ki >= qi) & (ki < jnp.where(g == ng - 1, 0, SPAN))
        edge = [(r, rho) for r in DILATIONS for rho in range(r)]
        for b0 in range(0, len(edge), TILE_BATCH):
            where, work = [], []
            for r, rho in edge[b0:b0 + TILE_BATCH]:
                krows, qrows = _rows(r, GROUP // (SPAN * r) - 1, rho), _rows(r, 0, rho)
                where.append(krows)
                work.append((qn_ref[qrows, :].astype(BF16), don_ref[qrows, :].astype(BF16), ln_ref[qrows, :],
                             dn_ref[qrows, :], kc_ref[krows, :].astype(BF16), vc_ref[krows, :].astype(BF16), mask_next))
            dsp = probs(work)
            dks = [_dot(ds, w[0], 0, 0) for (ds, _), w in zip(dsp, work)]
            dvs = [_dot(p, w[1], 0, 0) for (_, p), w in zip(dsp, work)]
            for krows, dk1, dv1 in zip(where, dks, dvs):
                dk_ref[krows, :] += dk1
                dv_ref[krows, :] += dv1

    cur = pl.BlockSpec((GROUP, HD), lambda g, h: (g, h))
    prev = pl.BlockSpec((GROUP, HD), lambda g, h: (jnp.maximum(g - 1, 0), h))
    nxt = pl.BlockSpec((GROUP, HD), lambda g, h: (jnp.minimum(g + 1, ng - 1), h))
    vcur = pl.BlockSpec((GROUP, HD), lambda g, h: (g, v_blk * N_HEADS + h))
    vprev = pl.BlockSpec((GROUP, HD), lambda g, h: (jnp.maximum(g - 1, 0), v_blk * N_HEADS + h))
    return pl.pallas_call(
        body, grid=(ng, N_HEADS), in_specs=[cur, cur, vcur, cur, cur, cur, prev, vprev] + [nxt] * 4,
        out_specs=[cur] * 3,
        out_shape=[_sds((t, GW), F32)] * 3, name=name,
        compiler_params=_params(2))(q, k, v, do, lse, delta, k, v, q, do, lse, delta)


def _merge(os_, ls_):
    m = jnp.maximum(jnp.maximum(ls_[0], ls_[1]), ls_[2])
    ws = [jnp.exp(l - m) for l in ls_]
    tot = ws[0] + ws[1] + ws[2]
    ob = (ws[0] * os_[0] + ws[1] * os_[1] + ws[2] * os_[2]) / tot
    return ob, m + jnp.log(tot)


def _gated_norm(oa, z, wv):
    return _head_rms(oa, wv) * _silu(z)


def _mix_fwd(name, oa_raw, proj, z_blk, ob, w_dn, w_an):
    t = oa_raw.shape[0]
    tm = min(256, t)

    def body(oa_ref, z_ref, ob_ref, wd_ref, wa_ref, mix_ref):
        for h in range(N_HEADS):
            sl = slice(h * HD, (h + 1) * HD)
            mix_ref[:, sl] = _gated_norm(oa_ref[:, sl], z_ref[:, sl], wd_ref[...]).astype(BF16)
            mix_ref[:, GW + h * HD:GW + (h + 1) * HD] = _head_rms(ob_ref[:, sl], wa_ref[...]).astype(BF16)

    vec = pl.BlockSpec((1, HD), lambda i: (0, 0))
    wide = pl.BlockSpec((tm, GW), lambda i: (i, 0))
    return pl.pallas_call(
        body, grid=(t // tm,),
        in_specs=[wide, pl.BlockSpec((tm, GW), lambda i: (i, z_blk)), wide, vec, vec],
        out_specs=pl.BlockSpec((tm, 2 * GW), lambda i: (i, 0)),
        out_shape=_sds((t, 2 * GW), BF16), name=name,
        compiler_params=_params(1))(oa_raw, proj, ob, w_dn, w_an)


def _mix_bwd(name, dmixed, oa_raw, proj, z_blk, ob, w_dn, w_an, dep):
    t = oa_raw.shape[0]
    tm = min(256, t)

    def body(dm_ref, oa_ref, z_ref, ob_ref, wd_ref, wa_ref, dep_ref,
             doa_ref, dz_ref, dob_ref, dl_ref, dwd_ref, dwa_ref):
        dwd = jnp.zeros((1, HD), F32)
        dwa = jnp.zeros((1, HD), F32)
        for h in range(N_HEADS):
            sl = slice(h * HD, (h + 1) * HD)
            _, vjp = jax.vjp(_gated_norm, oa_ref[:, sl], z_ref[:, sl], wd_ref[...])
            doa, dz, dw1 = vjp(dm_ref[:, sl])
            doa_ref[:, sl] = doa
            dz_ref[:, sl] = dz.astype(BF16)
            dwd = dwd + dw1
            obh = ob_ref[:, sl]
            _, vjp2 = jax.vjp(_head_rms, obh, wa_ref[...])
            dob, dw2 = vjp2(dm_ref[:, GW + h * HD:GW + (h + 1) * HD])
            dwa = dwa + dw2
            dob_ref[:, sl] = dob
            dl_ref[:, sl] = jnp.broadcast_to(jnp.sum(dob * obh, axis=1, keepdims=True), (tm, HD))

        @pl.when(pl.program_id(0) == 0)
        def _():
            dwd_ref[...] = jnp.zeros_like(dwd_ref)
            dwa_ref[...] = jnp.zeros_like(dwa_ref)

        dwd_ref[...] += dwd
        dwa_ref[...] += dwa

    vec = pl.BlockSpec((1, HD), lambda i: (0, 0))
    wide = pl.BlockSpec((tm, GW), lambda i: (i, 0))
    return pl.pallas_call(
        body, grid=(t // tm,),
        in_specs=[pl.BlockSpec((tm, 2 * GW), lambda i: (i, 0)), wide, pl.BlockSpec((tm, GW), lambda i: (i, z_blk)),
                  wide, vec, vec, ANY],
        out_specs=[wide, pl.BlockSpec((tm, GW), lambda i: (i, z_blk)), wide, wide, vec, vec],
        out_shape=[_sds((t, GW), F32), _sds(proj.shape, BF16), _sds((t, GW), F32), _sds((t, GW), F32),
                   _sds((1, HD), F32), _sds((1, HD), F32)], name=name,
        compiler_params=_params(1))(dmixed, oa_raw, proj, ob, w_dn, w_an, dep)


def _halves(n):
    cut = (n // 256) * 128
    return [(0, cut), (cut, n)]


def _gate_up_swiglu(name, h2, w_gu_g):
    t, d = h2.shape
    n = w_gu_g.shape[2]
    per = N_DEV // 2
    tm = min(512, t)

    def body(a_ref, bg_ref, bu_ref, gu_ref, act_ref):
        a = a_ref[...]
        cuts = _halves(n)
        gs = [_dot(a, bg_ref[:, c0:c1], 1, 0) for c0, c1 in cuts]
        ups = [_dot(a, bu_ref[:, c0:c1], 1, 0) for c0, c1 in cuts]
        for (c0, c1), g, up in zip(cuts, gs, ups):
            gu_ref[0, :, c0:c1] = g.astype(BF16)
            gu_ref[1, :, c0:c1] = up.astype(BF16)
            act_ref[:, c0:c1] = (_silu(g) * up).astype(BF16)

    return pl.pallas_call(
        body, grid=(per, t // tm),
        in_specs=[pl.BlockSpec((tm, d), lambda j, i: (i, 0)), pl.BlockSpec((None, d, n), lambda j, i: (j, 0, 0)),
                  pl.BlockSpec((None, d, n), lambda j, i: (j + per, 0, 0))],
        out_specs=[pl.BlockSpec((2, tm, n), lambda j, i: (0, i, j)), pl.BlockSpec((tm, n), lambda j, i: (i, j))],
        out_shape=[_sds((2, t, per * n), BF16), _sds((t, per * n), BF16)], name=name,
        compiler_params=_params(2))(h2, w_gu_g, w_gu_g)


def _d_gate_up(name, dy16, w_down, gu3, dep):
    t, d = dy16.shape
    f = w_down.shape[0]
    tm, tn = min(1024, t), f // 4

    def body(a_ref, b_ref, g_ref, dep_ref, o_ref):
        a = a_ref[...]
        cuts = _halves(tn)
        dacts = [_dot(a, b_ref[c0:c1, :], 1, 1) for c0, c1 in cuts]
        for (c0, c1), dact in zip(cuts, dacts):
            g, up = g_ref[0, :, c0:c1].astype(F32), g_ref[1, :, c0:c1].astype(F32)
            sg = _sigmoid(g)
            o_ref[0, :, c0:c1] = (dact * up * sg * (1.0 + g * (1.0 - sg))).astype(BF16)
            o_ref[1, :, c0:c1] = (dact * g * sg).astype(BF16)

    return pl.pallas_call(
        body, grid=(f // tn, t // tm),
        in_specs=[pl.BlockSpec((tm, d), lambda j, i: (i, 0)), pl.BlockSpec((tn, d), lambda j, i: (j, 0)),
                  pl.BlockSpec((2, tm, tn), lambda j, i: (0, i, j)), ANY],
        out_specs=pl.BlockSpec((2, tm, tn), lambda j, i: (0, i, j)), out_shape=_sds((2, t, f), BF16), name=name,
        compiler_params=_params(2))(dy16, w_down, gu3, dep)


def _d_h2(name, dgu3, w_gu_g, dep):
    _, t, f = dgu3.shape
    n_dev, d, n = w_gu_g.shape
    per = n_dev // 2
    tm, tn = min(512, t), 512

    def body(g_ref, u_ref, b_ref, dep_ref, o_ref):
        acc = None
        for s in range(n_dev):
            a_ref = g_ref if s < per else u_ref
            part = _dot(a_ref[:, (s % per) * n:(s % per + 1) * n], b_ref[s], 1, 1)
            acc = part if acc is None else acc + part
        o_ref[...] = acc

    return pl.pallas_call(
        body, grid=(d // tn, t // tm),
        in_specs=[pl.BlockSpec((None, tm, f), lambda j, i: (0, i, 0)), pl.BlockSpec((None, tm, f), lambda j, i: (1, i, 0)),
                  pl.BlockSpec((n_dev, tn, n), lambda j, i: (0, j, 0)), ANY],
        out_specs=pl.BlockSpec((tm, tn), lambda j, i: (i, j)), out_shape=_sds((t, d), F32), name=name,
        compiler_params=_params(2))(dgu3, dgu3, w_gu_g, dep)


def _out_proj_norm(name, mixed, w_out, x, w_norm):
    t, d = x.shape
    kdim = mixed.shape[1]
    tm = min(512, t)

    def body(a_ref, b_ref, x_ref, w_ref, x1_ref, h_ref):
        x1 = x_ref[...] + _dot(a_ref[...], b_ref[...], 1, 0)
        x1_ref[...] = x1
        h_ref[...] = _rms_f(x1, w_ref[...]).astype(BF16)

    row = pl.BlockSpec((tm, d), lambda i: (i, 0))
    return pl.pallas_call(
        body, grid=(t // tm,),
        in_specs=[pl.BlockSpec((tm, kdim), lambda i: (i, 0)), pl.BlockSpec((kdim, d), lambda i: (0, 0)), row,
                  pl.BlockSpec((1, d), lambda i: (0, 0))],
        out_specs=[row, row], out_shape=[_sds((t, d), F32), _sds((t, d), BF16)], name=name,
        compiler_params=_params(1))(mixed, w_out, x, w_norm)


def _down_loss(name, act, w_down, x1, target):
    t, f = act.shape
    d = x1.shape[1]
    tm, tn = min(1024, t), 512

    def body(a_ref, b_ref, x_ref, t_ref, dy_ref, dy16_ref, l_ref):
        diff = _dot(a_ref[...], b_ref[...], 1, 0) + x_ref[...] - t_ref[...]
        dyv = diff * (1.0 / d)
        dy_ref[...] = dyv
        dy16_ref[...] = dyv.astype(BF16)
        tot = jnp.sum(jnp.sum(diff * diff, axis=1, keepdims=True), axis=0, keepdims=True) * (0.5 / d)

        @pl.when((pl.program_id(0) == 0) & (pl.program_id(1) == 0))
        def _():
            l_ref[...] = jnp.zeros_like(l_ref)

        l_ref[...] += jnp.broadcast_to(tot, (8, 128))

    tile = pl.BlockSpec((tm, tn), lambda i, j: (i, j))
    return pl.pallas_call(
        body, grid=(t // tm, d // tn),
        in_specs=[pl.BlockSpec((tm, f), lambda i, j: (i, 0)), pl.BlockSpec((f, tn), lambda i, j: (0, j)), tile, tile],
        out_specs=[tile, tile, pl.BlockSpec((8, 128), lambda i, j: (0, 0))],
        out_shape=[_sds((t, d), F32), _sds((t, d), BF16), _sds((8, 128), F32)], name=name,
        compiler_params=_params(2))(act, w_down, x1, target)


def _peer(me, k):
    pid = (me + k) % N_DEV
    return (pid // 4, (pid // 2) % 2, pid % 2)


def _my_id():
    return 4 * lax.axis_index("x") + 2 * lax.axis_index("y") + lax.axis_index("c")


def _exchange(name, arrays, scatter, dep):
    n = len(arrays)

    def body(*refs):
        ins, outs = refs[:n], refs[n + 1:2 * n + 1]
        send_sems, recv_sems, local_sems = refs[2 * n + 1:]
        me = _my_id()
        started = []
        for a in range(n):
            src = ins[a].at[me] if scatter[a] else ins[a]
            loc = pltpu.make_async_copy(src, outs[a].at[me], local_sems.at[a])
            loc.start()
            started.append(loc)
        remote = []
        for k in range(1, N_DEV):
            to = (me + k) % N_DEV
            for a in range(n):
                src = ins[a].at[to] if scatter[a] else ins[a]
                cp = pltpu.make_async_remote_copy(src_ref=src, dst_ref=outs[a].at[me],
                                                  send_sem=send_sems.at[a * (N_DEV - 1) + k - 1], recv_sem=recv_sems.at[a * (N_DEV - 1) + k - 1],
                                                  device_id=_peer(me, k), device_id_type=pl.DeviceIdType.MESH)
                cp.start()
                remote.append(cp)
        for k in range(1, N_DEV):
            frm = (me + N_DEV - k) % N_DEV
            for a in range(n):
                src = ins[a].at[frm] if scatter[a] else ins[a]
                pltpu.make_async_remote_copy(src_ref=src, dst_ref=outs[a].at[frm],
                                             send_sem=send_sems.at[a * (N_DEV - 1) + k - 1], recv_sem=recv_sems.at[a * (N_DEV - 1) + k - 1],
                                             device_id=_peer(me, k), device_id_type=pl.DeviceIdType.MESH).wait_recv()
        for cp in remote:
            cp.wait_send()
        for loc in started:
            loc.wait()

    out_shape = [_sds((N_DEV,) + (a.shape[1:] if sc else a.shape), a.dtype) for a, sc in zip(arrays, scatter)]
    return pl.pallas_call(
        body, in_specs=[ANY] * (n + 1), out_specs=[ANY] * n, out_shape=out_shape,
        scratch_shapes=[pltpu.SemaphoreType.DMA((n * (N_DEV - 1),)), pltpu.SemaphoreType.DMA((n * (N_DEV - 1),)),
                        pltpu.SemaphoreType.DMA((n,))],
        name=name)(*arrays, dep)


def _gather_two_level(name, arrays):
    n = len(arrays)
    per = N_DEV - 1
    units = []
    for a, arr in enumerate(arrays):
        cuts = 4 if arr.shape[0] % 64 == 0 and arr.shape[0] >= 1024 else 1
        units += [(a, p * (arr.shape[0] // cuts), arr.shape[0] // cuts) for p in range(cuts)]
    nu = len(units)

    def body(*refs):
        ins, outs = refs[:n], refs[n:2 * n]
        send_sems, recv_sems, local_sems = refs[2 * n:]
        x, y, c = lax.axis_index("x"), lax.axis_index("y"), lax.axis_index("c")
        me, sibling = (x, y, c), (x, y, 1 - c)
        flip = lambda v, on: v + on - 2 * v * on
        relayed = (flip(x, c), flip(y, 1 - c), c)
        other = (flip(x, 1 - c), flip(y, c), c)
        diagonal = (1 - x, 1 - y, c)
        k_relayed, k_other = 2 - c, 1 + c

        def copy(u, k, block, to, from_input=False):
            a, r0, nr = units[u]
            slot = outs[a].at[4 * block[0] + 2 * block[1] + block[2], pl.ds(r0, nr)]
            return pltpu.make_async_remote_copy(
                src_ref=ins[a].at[pl.ds(r0, nr)] if from_input else slot, dst_ref=slot,
                send_sem=send_sems.at[u * per + k], recv_sem=recv_sems.at[u * per + k], device_id=to,
                device_id_type=pl.DeviceIdType.MESH)

        mine = [pltpu.make_async_copy(ins[a], outs[a].at[4 * x + 2 * y + c], local_sems.at[a]) for a in range(n)]
        for cp in mine:
            cp.start()
        sent = [copy(u, 1, me, (1 - x, y, c), True) for u in range(nu)]
        sent += [copy(u, 2, me, (x, 1 - y, c), True) for u in range(nu)]
        sent += [copy(u, 0, me, sibling, True) for u in range(nu)]
        for cp in sent:
            cp.start()
        for u in range(nu):
            copy(u, k_relayed, relayed, me).wait_recv()
            sent.append(copy(u, 3, relayed, other))
            sent.append(copy(u, 3 + k_relayed, relayed, sibling))
            sent[-2].start()
            sent[-1].start()
        for u in range(nu):
            copy(u, k_other, other, me).wait_recv()
            sent.append(copy(u, 3 + k_other, other, sibling))
            sent[-1].start()
        for u in range(nu):
            copy(u, 3, diagonal, me).wait_recv()
            sent.append(copy(u, 6, diagonal, sibling))
            sent[-1].start()
        for u in range(nu):
            copy(u, 0, sibling, me).wait_recv()
            for j, chip in enumerate([(1 - x, y), (x, 1 - y), (1 - x, 1 - y)]):
                copy(u, 4 + j, (*chip, 1 - c), me).wait_recv()
        for cp in sent:
            cp.wait_send()
        for cp in mine:
            cp.wait()

    return pl.pallas_call(
        body, in_specs=[ANY] * n, out_specs=[ANY] * n,
        out_shape=[_sds((N_DEV,) + a.shape, a.dtype) for a in arrays],
        scratch_shapes=[pltpu.SemaphoreType.DMA((nu * per,)), pltpu.SemaphoreType.DMA((nu * per,)),
                        pltpu.SemaphoreType.DMA((n,))],
        name=name)(*arrays)


HBM = pl.BlockSpec(memory_space=pltpu.HBM)
SEM = pl.BlockSpec(memory_space=pltpu.SEMAPHORE)
EFFECT = pltpu.SideEffectType.DATAFLOW_SIDE_EFFECTING


def _remote_copies(srcs, lands, scatter, send_sems, recv_sems, me, incoming):
    out = []
    for k in range(1, N_DEV):
        other = (me + N_DEV - k) % N_DEV if incoming else (me + k) % N_DEV
        for a in range(len(srcs)):
            sem = a * (N_DEV - 1) + k - 1
            src = srcs[a].at[other] if scatter[a] else srcs[a]
            dst = lands[a].at[other if incoming else me]
            out.append(pltpu.make_async_remote_copy(src_ref=src, dst_ref=dst, send_sem=send_sems.at[sem],
                                                    recv_sem=recv_sems.at[sem], device_id=_peer(me, k),
                                                    device_id_type=pl.DeviceIdType.MESH))
    return out


def _exchange_start(name, arrays, scatter, dep):
    n = len(arrays)
    lands = [lax.empty((N_DEV,) + (a.shape[1:] if sc else a.shape), a.dtype) for a, sc in zip(arrays, scatter)]

    def body(*refs):
        srcs, land_refs = refs[:n], refs[n:2 * n]
        send_sems, recv_sems = refs[2 * n + 1], refs[2 * n + 2]
        token = refs[-1]
        for cp in _remote_copies(srcs, land_refs, scatter, send_sems, recv_sems, _my_id(), False):
            cp.start()
        token[...] = jnp.zeros_like(token)

    n_sem = n * (N_DEV - 1)
    out_shape = ([pltpu.SemaphoreType.DMA((n_sem,)), pltpu.SemaphoreType.DMA((n_sem,))]
                 + [pltpu.HBM(a.shape, a.dtype) for a in arrays] + [pltpu.HBM(l.shape, l.dtype) for l in lands]
                 + [_sds((8, 128), F32)])
    aliases = {i: 2 + i for i in range(2 * n)}
    args = [pltpu.with_memory_space_constraint(a, pltpu.HBM) for a in list(arrays) + lands] + [dep]
    res = pl.pallas_call(
        body, name=name, in_specs=[HBM] * (2 * n) + [ANY], out_shape=out_shape,
        out_specs=[SEM, SEM] + [HBM] * (2 * n) + [pl.BlockSpec(memory_space=pltpu.VMEM)],
        input_output_aliases=aliases, compiler_params=pltpu.CompilerParams(has_side_effects=EFFECT))(*args)
    return dict(send=res[0], recv=res[1], srcs=res[2:2 + n], lands=res[2 + n:2 + 2 * n], token=res[-1],
                scatter=scatter)


def _exchange_wait(name, started, after):
    n = len(started["srcs"])
    scatter = started["scatter"]

    def body(*refs):
        srcs, land_refs = refs[:n], refs[n:2 * n]
        send_sems, recv_sems = refs[2 * n], refs[2 * n + 1]
        me = _my_id()
        for cp in _remote_copies(srcs, land_refs, scatter, send_sems, recv_sems, me, False):
            cp.wait_send()
        for cp in _remote_copies(srcs, land_refs, scatter, send_sems, recv_sems, me, True):
            cp.wait_recv()

    arrs = list(started["srcs"]) + list(started["lands"])
    res = pl.pallas_call(
        body, name=name, in_specs=[HBM] * (2 * n) + [SEM, SEM, ANY],
        out_shape=[pltpu.HBM(a.shape, a.dtype) for a in arrs], out_specs=[HBM] * (2 * n),
        input_output_aliases={i: i for i in range(2 * n)},
        compiler_params=pltpu.CompilerParams(has_side_effects=EFFECT))(*arrs, started["send"], started["recv"], after)
    me = _my_id()
    out = []
    for src, land, sc in zip(res[:n], res[n:], scatter):
        own = lax.dynamic_index_in_dim(src, me, 0, keepdims=True) if sc else src[None]
        out.append(lax.dynamic_update_slice(land, own, (me,) + (0,) * (land.ndim - 1)))
    return out


def _adamw(name, parts, w, m, v):
    r, c = w.shape
    tr, tc = r, c
    if r % 8 == 0:
        tr = next(cand for cand in (128, 88, 64, 40, 8) if r % cand == 0)
    else:
        tc = 256
    c1 = 1.0 / (1.0 - ADAM_B1 ** ADAM_STEP)
    c2 = 1.0 / (1.0 - ADAM_B2 ** ADAM_STEP)

    def body(p_ref, w_ref, m_ref, v_ref, g_ref, d_ref, nm_ref, nv_ref):
        g = p_ref[0].astype(F32)
        for s in range(1, N_DEV):
            g = g + p_ref[s].astype(F32)
        mn = ADAM_B1 * m_ref[...] + (1.0 - ADAM_B1) * g
        vn = ADAM_B2 * v_ref[...] + (1.0 - ADAM_B2) * (g * g)
        g_ref[...] = g
        nm_ref[...] = mn
        nv_ref[...] = vn
        d_ref[...] = -ADAM_LR * ((mn * c1) / (jnp.sqrt(vn * c2) + ADAM_EPS) + ADAM_WD * w_ref[...])

    blk = pl.BlockSpec((tr, tc), lambda i, j: (i, j))
    return pl.pallas_call(
        body, grid=(r // tr, c // tc),
        in_specs=[pl.BlockSpec((N_DEV, tr, tc), lambda i, j: (0, i, j)), blk, blk, blk],
        out_specs=[blk] * 4, out_shape=[_sds((r, c), F32)] * 4, name=name,
        compiler_params=_params(2, VMEM_LIMIT))(parts, w, m, v)


def _pad_rows(a, rows):
    return jnp.pad(a, ((0, rows - a.shape[0]), (0, 0)))


def _lane_row(vec8, offset):
    return jnp.pad(vec8.reshape(1, 8), ((0, 0), (offset, HD - 8 - offset)))


def kernel(x, positions, attn_norm_w, w_in, conv_w, a_log, dt_bias, delta_out_norm_w, q_norm_w, k_norm_w, attn_out_norm_w, w_out, ffn_norm_w, w_gate_up, w_down, loss_target, m_attn_norm_w, m_w_in, m_conv_w, m_a_log, m_dt_bias, m_delta_out_norm_w, m_q_norm_w, m_k_norm_w, m_attn_out_norm_w, m_w_out, m_ffn_norm_w, m_w_gate_up, m_w_down, v_attn_norm_w, v_w_in, v_conv_w, v_a_log, v_dt_bias, v_delta_out_norm_w, v_q_norm_w, v_k_norm_w, v_attn_out_norm_w, v_w_out, v_ffn_norm_w, v_w_gate_up, v_w_down):
    x2 = x[0]
    t, d = x2.shape
    target = loss_target[0]
    pos_col = positions.reshape(t, 1)
    half = HD // 2
    inv = (ROPE_THETA ** (-np.arange(half, dtype=np.float32) / half)).astype(np.float32)
    inv_row = jnp.asarray(np.concatenate([inv, inv]).reshape(1, HD))

    n_in = w_in.shape[2]
    n_gu = w_gate_up.shape[2]
    w_in_g, conv_g = _gather_two_level("gather_in", [w_in[0].astype(BF16), _pad_rows(conv_w[0], 8)])
    out_fly = _exchange_start("gather_out_start", [w_out[0].astype(BF16)], [False], conv_g)
    gu_fly = _exchange_start("gather_gate_up_start", [w_gate_up[0].astype(BF16)], [False], out_fly["token"])
    down_fly = _exchange_start("gather_down_start", [w_down[0].astype(BF16)], [False], gu_fly["token"])
    n_main = 4 * GW
    n_small = 2 * N_HEADS
    segments = [(0, n_main, 0), (n_main + n_small, N_DEV * n_in, n_main), (n_main, n_main + n_small, 7 * GW)]
    pieces = []
    for lo, hi, _ in segments:
        f = lo
        while f < hi:
            j = f // n_in
            end = min(hi, (j + 1) * n_in)
            pieces.append(w_in_g[j][:, f - j * n_in:end - j * n_in])
            f = end
    w_cat = jnp.concatenate(pieces + [jnp.zeros((d, HD - n_small), BF16)], axis=1)
    n_cat = w_cat.shape[1]
    small_blk = (7 * GW) // HD
    conv_w8 =jnp.transpose(conv_g, (1, 0, 2)).reshape(8, 3 * GW)
    alog_row = _lane_row(a_log[0], 8)
    dtb_row = _lane_row(dt_bias[0], 8)

    tm = min(2048, t)
    h1 = _rms_fwd("norm1", x2, attn_norm_w, down_fly["token"])
    tmp, tnp = min(1024, t), n_cat // 3
    proj = _mm("in_proj", h1, w_cat, grid=(t // tmp, n_cat // tnp, 1),
               a_spec=pl.BlockSpec((tmp, d), lambda i, j, k: (i, 0)),
               b_spec=pl.BlockSpec((d, tnp), lambda i, j, k: (0, j)),
               o_spec=pl.BlockSpec((tmp, tnp), lambda i, j, k: (i, j)),
               out_shape=_sds((t, n_cat), F32), ca=1, cb=0, nk=1)
    qn = _conv_fwd("conv_q", proj, conv_w8, 0, True, HD ** -0.5)
    kn = _conv_fwd("conv_k", proj, conv_w8, 1, True, 1.0)
    vv = _conv_fwd("conv_v", proj, conv_w8, 2, False, 1.0)
    beta_b, gc_b = _gates_fwd("gates", proj, small_blk, alog_row, dtb_row)
    u, w, p, tinv, qd, kd = _delta_prep("delta_prep", qn, kn, vv, beta_b, gc_b)
    oa_raw, vn, s_hist = _delta_scan("delta_scan", u, w, p, qd, kd, gc_b)

    cos_t, sin_t = _rope_tables("rope_tables", pos_col, inv_row)
    aq, ak = _qk_fwd("attn_qk", proj, 2, q_norm_w, k_norm_w, cos_t, sin_t)
    ob, lse = _attn_fwd("attn_fwd", aq, ak, proj, 6)
    mixed = _mix_fwd("mix", oa_raw, proj, 3, ob, delta_out_norm_w, attn_out_norm_w)
    (w_out_g,) = _exchange_wait("gather_out_wait", out_fly, mixed)
    w_out_full = w_out_g.reshape(2 * GW, d)
    tn = 512
    x1, h2 = _out_proj_norm("out_proj", mixed, w_out_full, x2, ffn_norm_w)
    per = N_DEV // 2
    (w_gu_g,) = _exchange_wait("gather_gate_up_wait", gu_fly, h2)
    gu3, act = _gate_up_swiglu("gate_up", h2, w_gu_g)
    (w_down_g,) = _exchange_wait("gather_down_wait", down_fly, act)
    w_down_full = w_down_g.reshape(D_FF, d)
    tmd = min(1024, t)
    dy, dy16, loss_tile = _down_loss("down_proj", act, w_down_full, x1, target)
    loss = lax.psum(loss_tile[0, 0], ("x", "y", "c"))

    tk, nkt = t, 1
    g_down = _mm("g_down", act, dy16, dep=loss.reshape(1, 1), grid=(D_FF // 1408, d // 512, nkt),
                 a_spec=pl.BlockSpec((tk, 1408), lambda i, j, k: (k, i)),
                 b_spec=pl.BlockSpec((tk, 512), lambda i, j, k: (k, j)),
                 o_spec=pl.BlockSpec((1408, 512), lambda i, j, k: (i, j)),
                 out_shape=_sds((D_FF, d), F32), ca=0, cb=0, nk=nkt)
    down_g_fly = _exchange_start("reduce_down_start", [g_down.reshape(N_DEV, D_FF // N_DEV, d)], [True], dy16)
    dgu3 = _d_gate_up("d_gate_up", dy16, w_down_full, gu3, down_g_fly["token"])
    g_gu = _mm("g_gate_up", h2, dgu3, grid=(d // 512, N_DEV, nkt),
               a_spec=pl.BlockSpec((tk, 512), lambda i, j, k: (k, i)),
               b_spec=pl.BlockSpec((None, tk, n_gu), lambda i, j, k: (j // per, k, j % per)),
               o_spec=pl.BlockSpec((None, 512, n_gu), lambda i, j, k: (j, i, 0)),
               out_shape=_sds((N_DEV, d, n_gu), F32), ca=0, cb=0, nk=nkt)
    gu_g_fly = _exchange_start("reduce_gate_up_start", [g_gu], [True], dy16)
    dh2 = _d_h2("d_h2", dgu3, w_gu_g, gu_g_fly["token"])
    dx1, dx1_16, g_ffn_norm = _rms_bwd("norm2_bwd", x1, ffn_norm_w, dh2, dy)

    g_out = _mm("g_out", mixed, dx1_16, grid=((2 * GW) // 512, 1, nkt),
                a_spec=pl.BlockSpec((tk, 512), lambda i, j, k: (k, i)),
                b_spec=pl.BlockSpec((tk, d), lambda i, j, k: (k, 0)),
                o_spec=pl.BlockSpec((512, d), lambda i, j, k: (i, 0)),
                out_shape=_sds((2 * GW, d), F32), ca=0, cb=0, nk=nkt)
    out_g_fly = _exchange_start("reduce_out_start", [g_out.reshape(N_DEV, (2 * GW) // N_DEV, d)], [True], g_ffn_norm)
    dmixed = _mm("d_mixed", dx1_16, w_out_full, dep=out_g_fly["token"], grid=(t // tm, (2 * GW) // tn, 1),
                 a_spec=pl.BlockSpec((tm, d), lambda i, j, k: (i, 0)),
                 b_spec=pl.BlockSpec((tn, d), lambda i, j, k: (j, 0)),
                 o_spec=pl.BlockSpec((tm, tn), lambda i, j, k: (i, j)),
                 out_shape=_sds((t, 2 * GW), F32), ca=1, cb=1, nk=1)
    doa, dproj, dob, delta, g_dn, g_an = _mix_bwd("mix_bwd", dmixed, oa_raw, proj, 3, ob,
                                                  delta_out_norm_w, attn_out_norm_w, out_g_fly["token"])
    d_aq, d_ak, d_av = _attn_bwd("attn_bwd", aq, ak, proj, 6, dob, lse, delta)
    dproj, g_qn, g_kn = _qk_bwd("attn_qk_bwd", proj, 2, q_norm_w, k_norm_w, cos_t, sin_t, d_aq, d_ak, dproj)
    dproj = _cast_into("attn_v_bwd", d_av, dproj, 6)

    dvn, dqd, dkd, dw, ddec = _delta_scan_bwd("delta_scan_bwd", doa, w, p, qd, kd, gc_b, vn, s_hist)
    dqn, dkn, dvv, dbeta_b, dg_b = _delta_prep_bwd("delta_prep_bwd", qn, kn, vv, beta_b, gc_b, tinv, u, w, vn,
                                                   doa, dvn, dqd, dkd, dw, ddec)
    dproj, gcw_q = _conv_bwd("conv_q_bwd", proj, conv_w8, dqn, dproj, 0, True, HD ** -0.5)
    dproj, gcw_k = _conv_bwd("conv_k_bwd", proj, conv_w8, dkn, dproj, 1, True, 1.0)
    dproj, gcw_v = _conv_bwd("conv_v_bwd", proj, conv_w8, dvv, dproj, 2, False, 1.0)
    dproj, g_alog_row, g_dtb_row = _gates_bwd("gates_bwd", proj, small_blk, alog_row, dtb_row, dbeta_b, dg_b, dproj)
    tmc = 384
    g_cat = _mm("g_in", dproj, h1, grid=(n_cat // tmc, 1, nkt),
                a_spec=pl.BlockSpec((tk, tmc), lambda i, j, k: (k, i)),
                b_spec=pl.BlockSpec((tk, d), lambda i, j, k: (k, 0)),
                o_spec=pl.BlockSpec((tmc, d), lambda i, j, k: (i, 0)),
                out_shape=_sds((n_cat, d), BF16), ca=0, cb=0, nk=nkt)
    parts = []
    for j in range(N_DEV):
        cols = []
        for lo, hi, start in sorted(segments):
            a, b = max(lo, j * n_in), min(hi, (j + 1) * n_in)
            if a < b:
                cols.append(g_cat[start + a - lo:start + b - lo])
        parts.append(cols[0] if len(cols) == 1 else jnp.concatenate(cols, axis=0))
    g_in_parts = jnp.stack(parts)
    g_conv = jnp.concatenate([gcw_q, gcw_k, gcw_v], axis=1)
    n_cw = conv_w.shape[2]
    g_conv_parts = jnp.transpose(g_conv.reshape(8, N_DEV, n_cw), (1, 0, 2))
    in_g_fly = _exchange_start("reduce_in_start", [g_in_parts, g_conv_parts], [True] * 2, g_dtb_row)
    tmh1 = min(512, t)
    dh1 = _mm("d_h1", dproj, w_cat, dep=in_g_fly["token"], grid=(t // tmh1, d // 1024, 1),
              a_spec=pl.BlockSpec((tmh1, n_cat), lambda i, j, k: (i, 0)),
              b_spec=pl.BlockSpec((1024, n_cat), lambda i, j, k: (j, 0)),
              o_spec=pl.BlockSpec((tmh1, 1024), lambda i, j, k: (i, j)),
              out_shape=_sds((t, d), F32), ca=1, cb=1, nk=1)
    grad_x, _, g_attn_norm = _rms_bwd("norm1_bwd", x2, attn_norm_w, dh1, dx1)

    small_rows = [g_attn_norm.reshape(d // HD, HD), g_ffn_norm.reshape(d // HD, HD), g_dn, g_qn, g_kn, g_an,
                  g_alog_row, g_dtb_row]
    small_pack = _pad_rows(jnp.concatenate(small_rows, axis=0), 40)
    (r_down,) = _exchange_wait("reduce_down_wait", down_g_fly, grad_x)
    (r_gu,) = _exchange_wait("reduce_gate_up_wait", gu_g_fly, grad_x)
    (r_out,) = _exchange_wait("reduce_out_wait", out_g_fly, grad_x)
    res_gu = [a[None] for a in _adamw("adamw_gate_up", r_gu, w_gate_up[0], m_w_gate_up[0], v_w_gate_up[0])]
    res_down = [a[None] for a in _adamw("adamw_down", r_down, w_down[0], m_w_down[0], v_w_down[0])]
    res_out = [a[None] for a in _adamw("adamw_out", r_out, w_out[0], m_w_out[0], v_w_out[0])]
    done = (res_gu[3][0, :1, :1] + res_down[3][0, :1, :1] + res_out[3][0, :1, :1])
    r_in, r_conv = _exchange_wait("reduce_in_wait", in_g_fly, done)
    upd_in = _adamw("adamw_in", r_in, jnp.transpose(w_in[0]), jnp.transpose(m_w_in[0]), jnp.transpose(v_w_in[0]))
    res_in = [jnp.transpose(a)[None] for a in upd_in]
    (r_small,) = _exchange("gather_small_grads", [small_pack], [False], upd_in[0])

    def pack_small(an, fn, dn, qn_, kn_, aon, al, db):
        rows = [an.reshape(d // HD, HD), fn.reshape(d // HD, HD), dn, qn_, kn_, aon,
                _lane_row(al[0], 8), _lane_row(db[0], 8)]
        return _pad_rows(jnp.concatenate(rows, axis=0), 40)

    def unpack_small(pk):
        nr = d // HD
        return dict(attn_norm_w=pk[:nr].reshape(1, d), ffn_norm_w=pk[nr:2 * nr].reshape(1, d),
                    delta_out_norm_w=pk[2 * nr:2 * nr + 1], q_norm_w=pk[2 * nr + 1:2 * nr + 2],
                    k_norm_w=pk[2 * nr + 2:2 * nr + 3], attn_out_norm_w=pk[2 * nr + 3:2 * nr + 4],
                    a_log=pk[2 * nr + 4:2 * nr + 5, 8:16], dt_bias=pk[2 * nr + 5:2 * nr + 6, 8:16])

    res_small = _adamw("adamw_small", r_small,
                       pack_small(attn_norm_w, ffn_norm_w, delta_out_norm_w, q_norm_w, k_norm_w, attn_out_norm_w, a_log, dt_bias),
                       pack_small(m_attn_norm_w, m_ffn_norm_w, m_delta_out_norm_w, m_q_norm_w, m_k_norm_w, m_attn_out_norm_w, m_a_log, m_dt_bias),
                       pack_small(v_attn_norm_w, v_ffn_norm_w, v_delta_out_norm_w, v_q_norm_w, v_k_norm_w, v_attn_out_norm_w, v_a_log, v_dt_bias))
    small = [unpack_small(a) for a in res_small]
    res_conv =[a[None, :4] for a in _adamw("adamw_conv", r_conv, _pad_rows(conv_w[0], 8), _pad_rows(m_conv_w[0], 8),
                                            _pad_rows(v_conv_w[0], 8))]

    outs = [loss, grad_x[None]]
    for i in range(4):
        s = small[i]
        outs += [s["attn_norm_w"], res_in[i], res_conv[i], s["a_log"], s["dt_bias"], s["delta_out_norm_w"],
                 s["q_norm_w"], s["k_norm_w"], s["attn_out_norm_w"], res_out[i], s["ffn_norm_w"], res_gu[i],
                 res_down[i]]
    return tuple(outs)
```

```python
import numpy as np
import jax
import jax.numpy as jnp
from jax import lax
from jax.experimental import pallas as pl
from jax.experimental.pallas import tpu as pltpu

F32 = jnp.float32
BF16 = jnp.bfloat16

N_DEV = 8
N_HEADS = 8
HD = 128
GW = N_HEADS * HD
CHUNK = 64
PAIR = 2 * CHUNK
SCAN_CHUNKS = 4
SCAN_ROWS = SCAN_CHUNKS * CHUNK
SPAN = 128
DILATIONS = (1, 4, 16)
ROPE_THETA = 10000.0
EPS = 1e-6
D_FF = 5632
ADAM_LR, ADAM_B1, ADAM_B2, ADAM_EPS, ADAM_WD, ADAM_STEP = 0.001, 0.9, 0.999, 1e-8, 0.01, 10
NEG = -1e30
VMEM_LIMIT = 56 * 1024 * 1024
ANY = pl.BlockSpec(memory_space=pl.ANY)
HEADS_PER_STEP = 8


def _params(n_grid, vmem=VMEM_LIMIT):
    return pltpu.CompilerParams(dimension_semantics=("arbitrary",) * n_grid, vmem_limit_bytes=vmem)


def _sds(shape, dtype):
    return jax.ShapeDtypeStruct(tuple(shape), dtype)


def _sigmoid(x):
    return 1.0 / (1.0 + jnp.exp(-x))


def _silu(x):
    return x * _sigmoid(x)


def _softplus(x):
    return jnp.maximum(x, 0.0) + jnp.log(1.0 + jnp.exp(-jnp.abs(x)))


def _dot(a, b, ca, cb):
    return lax.dot_general(a, b, (((ca,), (cb,)), ((), ())), preferred_element_type=F32)


def _b16(x):
    return x if x.dtype == BF16 else x.astype(BF16)


def _split(x):
    hi = x.astype(BF16)
    return hi, (x - hi.astype(F32)).astype(BF16)


def _dot3(a, b, ca, cb):
    a_hi, a_lo = _split(a)
    b_hi, b_lo = _split(b)
    return _dot(a_hi, b_hi, ca, cb) + (_dot(a_hi, b_lo, ca, cb) + _dot(a_lo, b_hi, ca, cb))


def _iota2(shape, axis):
    return lax.broadcasted_iota(jnp.int32, shape, axis)


def _mm(name, a, b, *, grid, a_spec, b_spec, o_spec, out_shape, ca, cb, nk, dep=None):
    assert nk == 1 and grid[2] == 1

    def body(*refs):
        refs[-1][...] = _dot(_b16(refs[0][...]), _b16(refs[1][...]), ca, cb).astype(refs[-1].dtype)

    in_specs = [a_spec, b_spec] + ([ANY] if dep is not None else [])
    args = (a, b) + ((dep,) if dep is not None else ())
    return pl.pallas_call(body, grid=grid, in_specs=in_specs, out_specs=o_spec, out_shape=out_shape,
                          name=name, compiler_params=_params(3))(*args)


def _rms_f(xv, wv):
    return xv * lax.rsqrt(jnp.mean(xv * xv, axis=-1, keepdims=True) + EPS) * wv


def _rms_fwd(name, x, w, dep):
    t, d = x.shape
    tm = min(512, t)

    def body(x_ref, w_ref, dep_ref, o_ref):
        o_ref[...] = _rms_f(x_ref[...], w_ref[...]).astype(BF16)

    row = pl.BlockSpec((tm, d), lambda i: (i, 0))
    vec = pl.BlockSpec((1, d), lambda i: (0, 0))
    return pl.pallas_call(body, grid=(t // tm,), in_specs=[row, vec, ANY], out_specs=row,
                          out_shape=_sds((t, d), BF16), name=name, compiler_params=_params(1))(x, w, dep)


def _rms_bwd(name, x, w, dh, res):
    t, d = x.shape
    tm = min(256, t)

    def body(x_ref, w_ref, dh_ref, res_ref, dx_ref, dx16_ref, dw_ref):
        _, vjp = jax.vjp(_rms_f, x_ref[...], w_ref[...])
        dxv, dwv = vjp(dh_ref[...])
        dxv = dxv + res_ref[...]
        dx_ref[...] = dxv
        dx16_ref[...] = dxv.astype(BF16)

        @pl.when(pl.program_id(0) == 0)
        def _():
            dw_ref[...] = jnp.zeros_like(dw_ref)

        dw_ref[...] += dwv

    row = pl.BlockSpec((tm, d), lambda i: (i, 0))
    vec = pl.BlockSpec((1, d), lambda i: (0, 0))
    return pl.pallas_call(body, grid=(t // tm,), in_specs=[row, vec, row, row], out_specs=[row, row, vec],
                          out_shape=[_sds((t, d), F32), _sds((t, d), BF16), _sds((1, d), F32)], name=name,
                          compiler_params=_params(1))(x, w, dh, res)


def _shift_rows(x, s):
    t = x.shape[0]
    r = pltpu.roll(x, s % t, 0)
    row8 = _iota2((8, x.shape[1]), 0)
    if s > 0:
        return jnp.concatenate([jnp.where(row8 >= s, r[:8], 0.0), r[8:]], axis=0)
    return jnp.concatenate([r[:t - 8], jnp.where(row8 < 8 + s, r[t - 8:], 0.0)], axis=0)


def _conv_taps(xv, w_ref):
    c = w_ref[3:4, :] * xv
    for s in (1, 2, 3):
        c = c + w_ref[3 - s:4 - s, :] * _shift_rows(xv, s)
    return c


def _post_conv(c, l2, scale):
    y = _silu(c)
    if l2:
        y = y * lax.rsqrt(jnp.sum(y * y, axis=-1, keepdims=True) + EPS) * scale
    return y


def _conv_fwd(name, proj, conv_w8, group, l2, scale):
    t = proj.shape[0]

    def body(x_ref, w_ref, o_ref):
        o_ref[...] = _post_conv(_conv_taps(x_ref[...], w_ref), l2, scale)

    return pl.pallas_call(
        body, grid=(N_HEADS,),
        in_specs=[pl.BlockSpec((t, HD), lambda h: (0, h + group * N_HEADS)),
                  pl.BlockSpec((8, HD), lambda h: (0, h + group * N_HEADS))],
        out_specs=pl.BlockSpec((t, HD), lambda h: (0, h)),
        out_shape=_sds((t, GW), F32), name=name, compiler_params=_params(1, VMEM_LIMIT))(proj, conv_w8)


def _conv_bwd(name, proj, conv_w8, dn, dproj, group, l2, scale):
    t = proj.shape[0]

    def body(x_ref, w_ref, dn_ref, dproj_ref, dx_ref, dw_ref):
        xv = x_ref[...]
        c = _conv_taps(xv, w_ref)
        _, vjp = jax.vjp(lambda cc: _post_conv(cc, l2, scale), c)
        (dc,) = vjp(dn_ref[...])
        dx = w_ref[3:4, :] * dc
        dw = jnp.zeros((8, HD), F32)
        rid = _iota2((8, HD), 0)
        dw = dw + jnp.where(rid == 3, jnp.sum(dc * xv, axis=0, keepdims=True), 0.0)
        for s in (1, 2, 3):
            dx = dx + w_ref[3 - s:4 - s, :] * _shift_rows(dc, -s)
            dw = dw + jnp.where(rid == 3 - s, jnp.sum(dc * _shift_rows(xv, s), axis=0, keepdims=True), 0.0)
        dx_ref[...] = dx.astype(BF16)
        dw_ref[...] = dw

    return pl.pallas_call(
        body, grid=(N_HEADS,),
        in_specs=[pl.BlockSpec((t, HD), lambda h: (0, h + group * N_HEADS)),
                  pl.BlockSpec((8, HD), lambda h: (0, h + group * N_HEADS)),
                  pl.BlockSpec((t, HD), lambda h: (0, h)), ANY],
        out_specs=[pl.BlockSpec((t, HD), lambda h: (0, h + group * N_HEADS)), pl.BlockSpec((8, HD), lambda h: (0, h))],
        out_shape=[_sds(dproj.shape, BF16), _sds((8, GW), F32)], input_output_aliases={3: 0}, name=name,
        compiler_params=_params(1, VMEM_LIMIT))(proj, conv_w8, dn, dproj)


def _chunk_cumsum(g, rows):
    pos = rows % CHUNK
    s = 1
    while s < CHUNK:
        g = g + jnp.where(pos >= s, pltpu.roll(g, s, 0), 0.0)
        s *= 2
    return g


def _gates_fwd(name, proj, small_blk, alog_row, dtb_row):
    t = proj.shape[0]
    tm = min(256, t)

    def body(s_ref, a_ref, b_ref, beta_ref, gc_ref):
        sm = s_ref[...]
        beta = _sigmoid(sm)
        g = -jnp.exp(a_ref[...]) * _softplus(sm + b_ref[...])
        gc = _chunk_cumsum(g, _iota2((tm, HD), 0))
        lane = _iota2((tm, HD), 1)
        for h in range(N_HEADS):
            bcol = jnp.sum(jnp.where(lane == h, beta, 0.0), axis=1, keepdims=True)
            gcol = jnp.sum(jnp.where(lane == 8 + h, gc, 0.0), axis=1, keepdims=True)
            beta_ref[:, h * HD:(h + 1) * HD] = jnp.broadcast_to(bcol, (tm, HD))
            gc_ref[:, h * HD:(h + 1) * HD] = jnp.broadcast_to(gcol, (tm, HD))

    vec = pl.BlockSpec((1, HD), lambda i: (0, 0))
    wide = pl.BlockSpec((tm, GW), lambda i: (i, 0))
    return pl.pallas_call(
        body, grid=(t // tm,),
        in_specs=[pl.BlockSpec((tm, HD), lambda i: (i, small_blk)), vec, vec], out_specs=[wide, wide],
        out_shape=[_sds((t, GW), F32), _sds((t, GW), F32)], name=name,
        compiler_params=_params(1))(proj, alog_row, dtb_row)


def _gates_bwd(name, proj, small_blk, alog_row, dtb_row, dbeta_b, dg_b, dproj):
    t = proj.shape[0]
    tm = min(256, t)

    def body(s_ref, a_ref, b_ref, db_ref, dg_ref, dproj_ref, ds_ref, da_ref, dbias_ref):
        sm = s_ref[...]
        lane = _iota2((tm, HD), 1)
        db = jnp.zeros((tm, HD), F32)
        dg = jnp.zeros((tm, HD), F32)
        for h in range(N_HEADS):
            db = db + jnp.where(lane == h, db_ref[:, h * HD:(h + 1) * HD], 0.0)
            dg = dg + jnp.where(lane == 8 + h, dg_ref[:, h * HD:(h + 1) * HD], 0.0)
        beta = _sigmoid(sm)
        ea = jnp.exp(a_ref[...])
        pre = sm + b_ref[...]
        g = -ea * _softplus(pre)
        dpre = dg * (-ea) * _sigmoid(pre)
        ds_ref[...] = (db * beta * (1.0 - beta) + dpre).astype(BF16)

        @pl.when(pl.program_id(0) == 0)
        def _():
            da_ref[...] = jnp.zeros_like(da_ref)
            dbias_ref[...] = jnp.zeros_like(dbias_ref)

        da_ref[...] += jnp.sum(dg * g, axis=0, keepdims=True)
        dbias_ref[...] += jnp.sum(dpre, axis=0, keepdims=True)

    vec = pl.BlockSpec((1, HD), lambda i: (0, 0))
    wide = pl.BlockSpec((tm, GW), lambda i: (i, 0))
    return pl.pallas_call(
        body, grid=(t // tm,),
        in_specs=[pl.BlockSpec((tm, HD), lambda i: (i, small_blk)), vec, vec, wide, wide, ANY],
        out_specs=[pl.BlockSpec((tm, HD), lambda i: (i, small_blk)), vec, vec],
        out_shape=[_sds(dproj.shape, BF16), _sds((1, HD), F32), _sds((1, HD), F32)],
        input_output_aliases={5: 0}, name=name,
        compiler_params=_params(1))(proj, alog_row, dtb_row, dbeta_b, dg_b, dproj)


def _pair_masks():
    ii = _iota2((PAIR, PAIR), 0)
    jj = _iota2((PAIR, PAIR), 1)
    same = (ii // CHUNK) == (jj // CHUNK)
    return ii, jj, same & (ii >= jj), same & (ii > jj)


def _to_row(col_b, ii, jj):
    return jnp.sum(jnp.where(ii == jj, col_b, 0.0), axis=0, keepdims=True)


def _to_col(row, ii, jj):
    return jnp.sum(jnp.where(ii == jj, jnp.broadcast_to(row, (PAIR, PAIR)), 0.0), axis=1, keepdims=True)


def _decay_parts(gc, last_a, last_b, ii, jj, causal):
    diff = gc - _to_row(gc, ii, jj)
    dmat = jnp.where(causal, jnp.exp(jnp.where(causal, diff, 0.0)), 0.0)
    glast = jnp.where(ii < CHUNK, last_a, last_b)
    return dmat, jnp.exp(gc), jnp.exp(glast - gc)


def _unit_lower_inverse(lows, ii, jj):
    eye = jnp.where(ii == jj, 1.0, 0.0)
    mm = lambda xs, ys: [_dot3(a, b, 1, 0) for a, b in zip(xs, ys)]
    plus = lambda xs: [eye + a for a in xs]
    minus = lambda xs: [eye - a for a in xs]
    d1 = [jnp.where((ii // 16) == (jj // 16), low, 0.0) for low in lows]
    d2 = mm(d1, d1)
    a = mm(minus(d1), plus(d2))
    d4 = mm(d2, d2)
    a = mm(a, plus(d4))
    d8 = mm(d4, d4)
    td = mm(a, plus(d8))
    n1 = mm(td, [low - d for low, d in zip(lows, d1)])
    n2 = mm(n1, n1)
    return mm(mm(minus(n1), plus(n2)), td)


def _delta_prep(name, qn, kn, vv, beta_b, gc_b):
    t = qn.shape[0]

    def body(q_ref, k_ref, v_ref, b_ref, g_ref, u_ref, w_ref, p_ref, t_ref, qd_ref, kd_ref):
        ii, jj, causal, strict = _pair_masks()
        sls = [slice(hh * HD, (hh + 1) * HD) for hh in range(HEADS_PER_STEP)]
        lows = []
        for sl in sls:
            q, k, beta = q_ref[:, sl], k_ref[:, sl], b_ref[:, sl]
            dmat, gam, e2 = _decay_parts(g_ref[:, sl], g_ref[CHUNK - 1:CHUNK, sl], g_ref[PAIR - 1:PAIR, sl],
                                         ii, jj, causal)
            k16 = _b16(k)
            lows.append(jnp.where(strict, beta * _dot(k16, k16, 1, 1) * dmat, 0.0))
            p_ref[:, sl] = jnp.where(causal, _dot(_b16(q), k16, 1, 1) * dmat, 0.0).astype(BF16)
            qd_ref[:, sl] = (q * gam).astype(BF16)
            kd_ref[:, sl] = (k * e2).astype(BF16)
        for sl, tinv in zip(sls, _unit_lower_inverse(lows, ii, jj)):
            beta = b_ref[:, sl]
            t_ref[:, sl] = tinv
            u_ref[:, sl] = _dot3(tinv, v_ref[:, sl] * beta, 1, 0)
            w_ref[:, sl] = _dot3(tinv, k_ref[:, sl] * (beta * jnp.exp(g_ref[:, sl])), 1, 0).astype(BF16)

    blk = pl.BlockSpec((PAIR, HEADS_PER_STEP * HD), lambda i, h: (i, h))
    return pl.pallas_call(
        body, grid=(t // PAIR, N_HEADS // HEADS_PER_STEP), in_specs=[blk] * 5, out_specs=[blk] * 6,
        out_shape=[_sds((t, GW), F32), _sds((t, GW), BF16), _sds((t, GW), BF16), _sds((t, GW), F32),
                   _sds((t, GW), BF16), _sds((t, GW), BF16)],
        name=name, compiler_params=_params(2))(qn, kn, vv, beta_b, gc_b)


def _delta_scan(name, u, w, p, qd, kd, gc_b):
    t = u.shape[0]
    n = t // CHUNK

    def body(u_ref, w_ref, p_ref, qd_ref, kd_ref, g_ref, o_ref, vn_ref, sh_ref, state):
        @pl.when(pl.program_id(0) == 0)
        def _():
            state[...] = jnp.zeros_like(state)

        sls = [slice(h * HD, (h + 1) * HD) for h in range(N_HEADS)]
        heads = range(N_HEADS)
        s = [state[h] for h in heads]
        for c in range(SCAN_CHUNKS):
            rows = slice(c * CHUNK, (c + 1) * CHUNK)
            last = slice((c + 1) * CHUNK - 1, (c + 1) * CHUNK)
            for h in heads:
                sh_ref[c, h] = s[h]
            s16 = [_b16(a) for a in s]
            ws = [_dot(w_ref[rows, sls[h]], s16[h], 1, 0) for h in heads]
            qs = [_dot(qd_ref[rows, sls[h]], s16[h], 1, 0) for h in heads]
            vn16 = [_b16(u_ref[rows, sls[h]] - ws[h]) for h in heads]
            pv = [_dot(p_ref[rows, sls[h]], jnp.concatenate([vn16[h], vn16[h]], axis=0), 1, 0) for h in heads]
            kv = [_dot(kd_ref[rows, sls[h]], vn16[h], 0, 0) for h in heads]
            for h in heads:
                o_ref[rows, sls[h]] = qs[h] + pv[h]
                vn_ref[rows, sls[h]] = vn16[h]
            s = [s[h] * jnp.exp(g_ref[last, sls[h]]) + kv[h] for h in heads]
        for h in heads:
            state[h] = s[h]

    blk = pl.BlockSpec((SCAN_ROWS, GW), lambda i: (i, 0))
    return pl.pallas_call(
        body, grid=(t // SCAN_ROWS,), in_specs=[blk] * 6,
        out_specs=[blk, blk, pl.BlockSpec((SCAN_CHUNKS, N_HEADS, HD, HD), lambda i: (i, 0, 0, 0))],
        out_shape=[_sds((t, GW), F32), _sds((t, GW), BF16), _sds((n, N_HEADS, HD, HD), F32)],
        scratch_shapes=[pltpu.VMEM((N_HEADS, HD, HD), F32)], name=name,
        compiler_params=_params(1))(u, w, p, qd, kd, gc_b)


def _delta_scan_bwd(name, do, w, p, qd, kd, gc_b, vn, s_hist):
    t = do.shape[0]
    n = t // CHUNK

    def body(do_ref, w_ref, p_ref, qd_ref, kd_ref, g_ref, vn_ref, sh_ref,
             dvn_ref, dqd_ref, dkd_ref, dw_ref, ddec_ref, dstate):
        @pl.when(pl.program_id(0) == 0)
        def _():
            dstate[...] = jnp.zeros_like(dstate)

        sls = [slice(h * HD, (h + 1) * HD) for h in range(N_HEADS)]
        heads = range(N_HEADS)
        ds = [dstate[h] for h in heads]
        for c in reversed(range(SCAN_CHUNKS)):
            rows = slice(c * CHUNK, (c + 1) * CHUNK)
            last = slice((c + 1) * CHUNK - 1, (c + 1) * CHUNK)
            ds16 = [_b16(a) for a in ds]
            s16 = [_b16(sh_ref[c, h]) for h in heads]
            do16 = [_b16(do_ref[rows, sls[h]]) for h in heads]
            ptdo = [_dot(p_ref[rows, sls[h]], do16[h], 0, 0) for h in heads]
            kds = [_dot(kd_ref[rows, sls[h]], ds16[h], 1, 0) for h in heads]
            qdo = [_dot(qd_ref[rows, sls[h]], do16[h], 0, 0) for h in heads]
            for h in heads:
                dqd_ref[rows, sls[h]] = _dot(do16[h], s16[h], 1, 1)
                dkd_ref[rows, sls[h]] = _dot(vn_ref[rows, sls[h]], ds16[h], 1, 1)
            dvn = [ptdo[h][:CHUNK, :] + ptdo[h][CHUNK:, :] + kds[h] for h in heads]
            dvn16 = [_b16(a) for a in dvn]
            wdv = [_dot(w_ref[rows, sls[h]], dvn16[h], 0, 0) for h in heads]
            for h in heads:
                dvn_ref[rows, sls[h]] = dvn[h]
                dw_ref[rows, sls[h]] = -_dot(dvn16[h], s16[h], 1, 1)
                tot = jnp.sum(jnp.sum(sh_ref[c, h] * ds[h], axis=1, keepdims=True), axis=0, keepdims=True)
                ddec_ref[c * 8:(c + 1) * 8, sls[h]] = jnp.broadcast_to(tot, (8, HD))
            ds = [ds[h] * jnp.exp(g_ref[last, sls[h]]) + qdo[h] - wdv[h] for h in heads]
        for h in heads:
            dstate[h] = ds[h]

    npair = t // SCAN_ROWS
    blk = pl.BlockSpec((SCAN_ROWS, GW), lambda i: (npair - 1 - i, 0))
    return pl.pallas_call(
        body, grid=(npair,),
        in_specs=[blk] * 7 + [pl.BlockSpec((SCAN_CHUNKS, N_HEADS, HD, HD), lambda i: (npair - 1 - i, 0, 0, 0))],
        out_specs=[blk] * 4 + [pl.BlockSpec((8 * SCAN_CHUNKS, GW), lambda i: (npair - 1 - i, 0))],
        out_shape=[_sds((t, GW), F32)] * 4 + [_sds((n * 8, GW), F32)],
        scratch_shapes=[pltpu.VMEM((N_HEADS, HD, HD), F32)], name=name,
        compiler_params=_params(1))(do, w, p, qd, kd, gc_b, vn, s_hist)


def _delta_prep_bwd(name, qn, kn, vv, beta_b, gc_b, tinv, u, w, vn, do, dvn, dqd, dkd, dw, ddec):
    t = qn.shape[0]

    def body(q_ref, k_ref, v_ref, b_ref, g_ref, t_ref, u_ref, w_ref, vn_ref, do_ref, dvn_ref, dqd_ref,
             dkd_ref, dw_ref, ddec_ref, dq_ref, dk_ref, dv_ref, dbeta_ref, dg_ref):
        ii, jj, causal, strict = _pair_masks()
        suffix = ((ii // CHUNK) == (jj // CHUNK)) & (jj >= ii)
        first = ii < CHUNK
        rs = lambda a: jnp.sum(a, axis=1, keepdims=True)
        sls = [slice(hh * HD, (hh + 1) * HD) for hh in range(HEADS_PER_STEP)]
        xs = [_dot3(t_ref[:, sl], dvn_ref[:, sl], 0, 0) for sl in sls]
        ys = [_dot3(t_ref[:, sl], dw_ref[:, sl], 0, 0) for sl in sls]
        k16s = [_b16(k_ref[:, sl]) for sl in sls]
        kks = [_dot(k16, k16, 1, 1) for k16 in k16s]
        qks = [_dot(_b16(q_ref[:, sl]), k16, 1, 1) for sl, k16 in zip(sls, k16s)]
        dps = [jnp.where(causal, _dot(_b16(do_ref[:, sl]), vn_ref[:, sl], 1, 1), 0.0) for sl in sls]
        das = [-jnp.where(strict, _dot(_b16(x), _b16(u_ref[:, sl]), 1, 1) + _dot(_b16(y), w_ref[:, sl], 1, 1), 0.0)
               for sl, x, y in zip(sls, xs, ys)]
        for hh, sl in enumerate(sls):
            q, k, v, beta, gc = q_ref[:, sl], k_ref[:, sl], v_ref[:, sl], b_ref[:, sl], g_ref[:, sl]
            last_a, last_b = g_ref[CHUNK - 1:CHUNK, sl], g_ref[PAIR - 1:PAIR, sl]
            dmat, gam, e2 = _decay_parts(gc, last_a, last_b, ii, jj, causal)
            q16, k16 = _b16(q), k16s[hh]
            kk, qk, dp, x, y, da = kks[hh], qks[hh], dps[hh], xs[hh], ys[hh], das[hh]
            dqd, dkd = dqd_ref[:, sl], dkd_ref[:, sl]
            dpd16 = _b16(dp * dmat)
            dkk16 = _b16(da * beta * dmat)
            dq_ref[:, sl] = gam * dqd + _dot(dpd16, k16, 1, 0)
            dk_ref[:, sl] = (e2 * dkd + _dot(dpd16, q16, 0, 0) + beta * gam * y
                             + _dot(dkk16, k16, 1, 0) + _dot(dkk16, k16, 0, 0))
            dv_ref[:, sl] = beta * x
            dbeta = rs(v * x) + rs(k * gam * y) + rs(da * kk * dmat)
            dbeta_ref[:, sl] = jnp.broadcast_to(dbeta, (PAIR, HD))
            m = (dp * qk + da * beta * kk) * dmat
            dgam = rs(q * dqd) + rs(k * beta * y)
            de2 = rs(k * dkd)
            colsum = _to_col(jnp.sum(m, axis=0, keepdims=True), ii, jj)
            te2 = de2 * e2
            dgc = rs(m) - colsum + gam * dgam - te2
            tail_a = jnp.sum(jnp.where(first, te2, 0.0), axis=0, keepdims=True)
            tail_b = jnp.sum(jnp.where(first, 0.0, te2), axis=0, keepdims=True)
            dgc = dgc + jnp.where(ii == CHUNK - 1, tail_a + ddec_ref[0:1, sl] * jnp.exp(last_a), 0.0)
            dgc = dgc + jnp.where(ii == PAIR - 1, tail_b + ddec_ref[8:9, sl] * jnp.exp(last_b), 0.0)
            dgc_row = _to_row(dgc, ii, jj)
            dg = jnp.sum(jnp.where(suffix, jnp.broadcast_to(dgc_row, (PAIR, PAIR)), 0.0), axis=1, keepdims=True)
            dg_ref[:, sl] = jnp.broadcast_to(dg, (PAIR, HD))

    blk = pl.BlockSpec((PAIR, HEADS_PER_STEP * HD), lambda i, h: (i, h))
    return pl.pallas_call(
        body, grid=(t // PAIR, N_HEADS // HEADS_PER_STEP),
        in_specs=[blk] * 14 + [pl.BlockSpec((16, HEADS_PER_STEP * HD), lambda i, h: (i, h))], out_specs=[blk] * 5,
        out_shape=[_sds((t, GW), F32)] * 5, name=name,
        compiler_params=_params(2))(qn, kn, vv, beta_b, gc_b, tinv, u, w, vn, do, dvn, dqd, dkd, dw, ddec)


def _rope_tables(name, pos_col, inv_row):
    t = pos_col.shape[0]
    tm = min(1024, t)

    def body(pos_ref, inv_ref, cos_ref, sin_ref):
        ang = pos_ref[...].astype(F32) * inv_ref[...]
        lane = _iota2(ang.shape, 1)
        cos_ref[...] = jnp.cos(ang)
        sin_ref[...] = jnp.where(lane < HD // 2, -1.0, 1.0) * jnp.sin(ang)

    tab = pl.BlockSpec((tm, HD), lambda i: (i, 0))
    return pl.pallas_call(
        body, grid=(t // tm,), in_specs=[pl.BlockSpec((tm, 1), lambda i: (i, 0)), pl.BlockSpec((1, HD), lambda i: (0, 0))],
        out_specs=[tab, tab], out_shape=[_sds((t, HD), F32)] * 2, name=name,
        compiler_params=_params(1))(pos_col, inv_row)


def _head_rms(xh, wv):
    return xh * lax.rsqrt(jnp.mean(xh * xh, axis=-1, keepdims=True) + EPS) * wv


def _qk_fwd(name, proj, pair_blk, wq_row, wk_row, cos_t, sin_t):
    t = proj.shape[0]
    tm = min(256, t)

    def body(x_ref, wq_ref, wk_ref, cos_ref, sin_ref, q_ref, k_ref):
        cos, sin = cos_ref[...], sin_ref[...]
        for o_ref, w_ref, base in ((q_ref, wq_ref, 0), (k_ref, wk_ref, GW)):
            for h in range(N_HEADS):
                y = _head_rms(x_ref[:, base + h * HD:base + (h + 1) * HD], w_ref[...])
                o_ref[:, h * HD:(h + 1) * HD] = y * cos + pltpu.roll(y, HD // 2, 1) * sin

    vec = pl.BlockSpec((1, HD), lambda i: (0, 0))
    tab = pl.BlockSpec((tm, HD), lambda i: (i, 0))
    wide = pl.BlockSpec((tm, GW), lambda i: (i, 0))
    return pl.pallas_call(
        body, grid=(t // tm,),
        in_specs=[pl.BlockSpec((tm, 2 * GW), lambda i: (i, pair_blk)), vec, vec, tab, tab],
        out_specs=[wide, wide], out_shape=[_sds((t, GW), F32)] * 2, name=name,
        compiler_params=_params(1))(proj, wq_row, wk_row, cos_t, sin_t)


def _qk_bwd(name, proj, pair_blk, wq_row, wk_row, cos_t, sin_t, dq_full, dk_full, dproj):
    t = proj.shape[0]
    tm = min(256, t)

    def body(x_ref, wq_ref, wk_ref, cos_ref, sin_ref, dq_ref, dk_ref, dproj_ref, dx_ref, dwq_ref, dwk_ref):
        cos, sin = cos_ref[...], sin_ref[...]

        @pl.when(pl.program_id(0) == 0)
        def _():
            dwq_ref[...] = jnp.zeros_like(dwq_ref)
            dwk_ref[...] = jnp.zeros_like(dwk_ref)

        for dy_ref, w_ref, dw_ref, base in ((dq_ref, wq_ref, dwq_ref, 0), (dk_ref, wk_ref, dwk_ref, GW)):
            dw = jnp.zeros((1, HD), F32)
            for h in range(N_HEADS):
                dy = dy_ref[:, h * HD:(h + 1) * HD]
                dy = dy * cos - pltpu.roll(dy, HD // 2, 1) * sin
                _, vjp = jax.vjp(_head_rms, x_ref[:, base + h * HD:base + (h + 1) * HD], w_ref[...])
                dx, dwh = vjp(dy)
                dw = dw + dwh
                dx_ref[:, base + h * HD:base + (h + 1) * HD] = dx.astype(BF16)
            dw_ref[...] += dw

    vec = pl.BlockSpec((1, HD), lambda i: (0, 0))
    tab = pl.BlockSpec((tm, HD), lambda i: (i, 0))
    wide = pl.BlockSpec((tm, GW), lambda i: (i, 0))
    pair = pl.BlockSpec((tm, 2 * GW), lambda i: (i, pair_blk))
    return pl.pallas_call(
        body, grid=(t // tm,), in_specs=[pair, vec, vec, tab, tab, wide, wide, ANY],
        out_specs=[pair, vec, vec],
        out_shape=[_sds(dproj.shape, BF16), _sds((1, HD), F32), _sds((1, HD), F32)], input_output_aliases={7: 0},
        name=name, compiler_params=_params(1))(proj, wq_row, wk_row, cos_t, sin_t, dq_full, dk_full, dproj)


def _cast_into(name, x, dproj, blk_idx):
    t = x.shape[0]
    tm = min(512, t)

    def body(x_ref, dproj_ref, o_ref):
        o_ref[...] = x_ref[...].astype(BF16)

    return pl.pallas_call(
        body, grid=(t // tm,), in_specs=[pl.BlockSpec((tm, GW), lambda i: (i, 0)), ANY],
        out_specs=pl.BlockSpec((tm, GW), lambda i: (i, blk_idx)), out_shape=_sds(dproj.shape, BF16),
        input_output_aliases={1: 0}, name=name, compiler_params=_params(1))(x, dproj)


GROUP = SPAN * max(DILATIONS)
SCALE = HD ** -0.5
TILE_BATCH = 8


def _band_mask(lo):
    qi = _iota2((SPAN, 2 * SPAN), 0)
    ki = _iota2((SPAN, 2 * SPAN), 1)
    return (ki >= qi) & (ki <= qi + SPAN) & (ki >= lo)


def _tiles():
    return [(pi, r, u, rho) for pi, r in enumerate(DILATIONS) for rho in range(r) for u in range(GROUP // (SPAN * r))]


def _rows(r, u, rho):
    return pl.ds(u * SPAN * r + rho, SPAN, stride=r) if r > 1 else pl.ds(u * SPAN, SPAN)


def _attn_fwd(name, q, k, v, v_blk):
    t = q.shape[0]

    def body(qc_ref, kc_ref, vc_ref, kp_ref, vp_ref, ob_ref, lse_ref, o_scr, l_scr):
        mask_in = _band_mask(0)
        mask_edge = _band_mask(jnp.where(pl.program_id(0) == 0, SPAN, 0))
        tiles = _tiles()
        k_own = v_own = None
        for b0 in range(0, len(tiles), TILE_BATCH):
            work = []
            for pi, r, u, rho in tiles[b0:b0 + TILE_BATCH]:
                rows = _rows(r, u, rho)
                if u > 0:
                    k_prev, v_prev, mask = k_own, v_own, mask_in
                else:
                    prows = _rows(r, GROUP // (SPAN * r) - 1, rho)
                    k_prev, v_prev, mask = kp_ref[prows, :].astype(BF16), vp_ref[prows, :].astype(BF16), mask_edge
                k_own, v_own = kc_ref[rows, :].astype(BF16), vc_ref[rows, :].astype(BF16)
                work.append((pi, rows, mask, qc_ref[rows, :].astype(BF16), jnp.concatenate([k_prev, k_own], axis=0),
                             jnp.concatenate([v_prev, v_own], axis=0)))
            scores = [_dot(qt, kcat, 1, 1) for _, _, _, qt, kcat, _ in work]
            soft = []
            for (_, _, mask, _, _, _), s in zip(work, scores):
                s = jnp.where(mask, s * SCALE, NEG)
                m = jnp.max(s, axis=1, keepdims=True)
                p = jnp.exp(s - m)
                soft.append((m, _b16(p), jnp.sum(p, axis=1, keepdims=True)))
            outs = [_dot(p, vcat, 1, 0) for (_, p, _), (_, _, _, _, _, vcat) in zip(soft, work)]
            for (pi, rows, _, _, _, _), (m, _, den), o in zip(work, soft, outs):
                o_scr[pi, rows, :] = o / den
                l_scr[pi, rows, :] = jnp.broadcast_to(m + jnp.log(den), (SPAN, HD))
        step = 256
        for c in range(GROUP // step):
            sl = pl.ds(c * step, step)
            ob, lse = _merge([o_scr[i, sl, :] for i in range(3)], [l_scr[i, sl, :] for i in range(3)])
            ob_ref[sl, :] = ob
            lse_ref[sl, :] = lse

    cur = pl.BlockSpec((GROUP, HD), lambda g, h: (g, h))
    prev = pl.BlockSpec((GROUP, HD), lambda g, h: (jnp.maximum(g - 1, 0), h))
    vcur = pl.BlockSpec((GROUP, HD), lambda g, h: (g, v_blk * N_HEADS + h))
    vprev = pl.BlockSpec((GROUP, HD), lambda g, h: (jnp.maximum(g - 1, 0), v_blk * N_HEADS + h))
    return pl.pallas_call(
        body, grid=(t // GROUP, N_HEADS), in_specs=[cur, cur, vcur, prev, vprev], out_specs=[cur, cur],
        out_shape=[_sds((t, GW), F32), _sds((t, GW), F32)],
        scratch_shapes=[pltpu.VMEM((3, GROUP, HD), F32), pltpu.VMEM((3, GROUP, HD), F32)], name=name,
        compiler_params=_params(2))(q, k, v, k, v)


def _attn_bwd(name, q, k, v, v_blk, do, lse, delta):
    t = q.shape[0]
    ng = t // GROUP

    def probs(work):
        scores = [_dot(qt, kcat, 1, 1) for qt, _, _, _, kcat, _, _ in work]
        dps = [_dot(dot, vcat, 1, 1) for _, dot, _, _, _, vcat, _ in work]
        out = []
        for (_, _, lt, dlt, kcat, _, mask), s, dp in zip(work, scores, dps):
            wide = kcat.shape[0] // SPAN
            lw = jnp.concatenate([lt] * wide, axis=1) if wide > 1 else lt
            dw = jnp.concatenate([dlt] * wide, axis=1) if wide > 1 else dlt
            p = jnp.exp(jnp.where(mask, s * SCALE - lw, NEG))
            out.append((_b16(p * (dp - dw) * SCALE), _b16(p)))
        return out

    def body(qc_ref, kc_ref, vc_ref, doc_ref, lc_ref, dc_ref, kp_ref, vp_ref, qn_ref, don_ref, ln_ref, dn_ref,
             dq_ref, dk_ref, dv_ref):
        g = pl.program_id(0)
        mask_in = _band_mask(0)
        mask_edge = _band_mask(jnp.where(g == 0, SPAN, 0))
        dk_ref[...] = jnp.zeros_like(dk_ref)
        dv_ref[...] = jnp.zeros_like(dv_ref)
        tiles = _tiles()
        k_own = v_own = None
        for b0 in range(0, len(tiles), TILE_BATCH):
            where, work = [], []
            for pi, r, u, rho in tiles[b0:b0 + TILE_BATCH]:
                rows = _rows(r, u, rho)
                if u > 0:
                    prows, k_prev, v_prev, mask = _rows(r, u - 1, rho), k_own, v_own, mask_in
                else:
                    prows = _rows(r, GROUP // (SPAN * r) - 1, rho)
                    k_prev, v_prev, mask = kp_ref[prows, :].astype(BF16), vp_ref[prows, :].astype(BF16), mask_edge
                k_own, v_own = kc_ref[rows, :].astype(BF16), vc_ref[rows, :].astype(BF16)
                where.append((pi, u, rows, prows))
                work.append((qc_ref[rows, :].astype(BF16), doc_ref[rows, :].astype(BF16), lc_ref[rows, :], dc_ref[rows, :],
                             jnp.concatenate([k_prev, k_own], axis=0), jnp.concatenate([v_prev, v_own], axis=0), mask))
            dsp = probs(work)
            dqs = [_dot(ds, w[4], 1, 0) for (ds, _), w in zip(dsp, work)]
            dks = [_dot(ds, w[0], 0, 0) for (ds, _), w in zip(dsp, work)]
            dvs = [_dot(p, w[1], 0, 0) for (_, p), w in zip(dsp, work)]
            for (pi, u, rows, prows), dq_t, dk2, dv2 in zip(where, dqs, dks, dvs):
                if pi == 0:
                    dq_ref[rows, :] = dq_t
                else:
                    dq_ref[rows, :] += dq_t
                dk_ref[rows, :] += dk2[SPAN:, :]
                dv_ref[rows, :] += dv2[SPAN:, :]
                if u > 0:
                    dk_ref[prows, :] += dk2[:SPAN, :]
                    dv_ref[prows, :] += dv2[:SPAN, :]
        qi = _iota2((SPAN, SPAN), 0)
        ki = _iota2((SPAN, SPAN), 1)
        mask_next = (ki >= qi) & (ki < jnp.where(g == ng - 1, 0, SPAN))
        edge = [(r, rho) for r in DILATIONS for rho in range(r)]
        for b0 in range(0, len(edge), TILE_BATCH):
            where, work = [], []
            for r, rho in edge[b0:b0 + TILE_BATCH]:
                krows, qrows = _rows(r, GROUP // (SPAN * r) - 1, rho), _rows(r, 0, rho)
                where.append(krows)
                work.append((qn_ref[qrows, :].astype(BF16), don_ref[qrows, :].astype(BF16), ln_ref[qrows, :],
                             dn_ref[qrows, :], kc_ref[krows, :].astype(BF16), vc_ref[krows, :].astype(BF16), mask_next))
            dsp = probs(work)
            dks = [_dot(ds, w[0], 0, 0) for (ds, _), w in zip(dsp, work)]
            dvs = [_dot(p, w[1], 0, 0) for (_, p), w in zip(dsp, work)]
            for krows, dk1, dv1 in zip(where, dks, dvs):
                dk_ref[krows, :] += dk1
                dv_ref[krows, :] += dv1

    cur = pl.BlockSpec((GROUP, HD), lambda g, h: (g, h))
    prev = pl.BlockSpec((GROUP, HD), lambda g, h: (jnp.maximum(g - 1, 0), h))
    nxt = pl.BlockSpec((GROUP, HD), lambda g, h: (jnp.minimum(g + 1, ng - 1), h))
    vcur = pl.BlockSpec((GROUP, HD), lambda g, h: (g, v_blk * N_HEADS + h))
    vprev = pl.BlockSpec((GROUP, HD), lambda g, h: (jnp.maximum(g - 1, 0), v_blk * N_HEADS + h))
    return pl.pallas_call(
        body, grid=(ng, N_HEADS), in_specs=[cur, cur, vcur, cur, cur, cur, prev, vprev] + [nxt] * 4,
        out_specs=[cur] * 3,
        out_shape=[_sds((t, GW), F32)] * 3, name=name,
        compiler_params=_params(2))(q, k, v, do, lse, delta, k, v, q, do, lse, delta)


def _merge(os_, ls_):
    m = jnp.maximum(jnp.maximum(ls_[0], ls_[1]), ls_[2])
    ws = [jnp.exp(l - m) for l in ls_]
    tot = ws[0] + ws[1] + ws[2]
    ob = (ws[0] * os_[0] + ws[1] * os_[1] + ws[2] * os_[2]) / tot
    return ob, m + jnp.log(tot)


def _gated_norm(oa, z, wv):
    return _head_rms(oa, wv) * _silu(z)


def _mix_fwd(name, oa_raw, proj, z_blk, ob, w_dn, w_an):
    t = oa_raw.shape[0]
    tm = min(256, t)

    def body(oa_ref, z_ref, ob_ref, wd_ref, wa_ref, mix_ref):
        for h in range(N_HEADS):
            sl = slice(h * HD, (h + 1) * HD)
            mix_ref[:, sl] = _gated_norm(oa_ref[:, sl], z_ref[:, sl], wd_ref[...]).astype(BF16)
            mix_ref[:, GW + h * HD:GW + (h + 1) * HD] = _head_rms(ob_ref[:, sl], wa_ref[...]).astype(BF16)

    vec = pl.BlockSpec((1, HD), lambda i: (0, 0))
    wide = pl.BlockSpec((tm, GW), lambda i: (i, 0))
    return pl.pallas_call(
        body, grid=(t // tm,),
        in_specs=[wide, pl.BlockSpec((tm, GW), lambda i: (i, z_blk)), wide, vec, vec],
        out_specs=pl.BlockSpec((tm, 2 * GW), lambda i: (i, 0)),
        out_shape=_sds((t, 2 * GW), BF16), name=name,
        compiler_params=_params(1))(oa_raw, proj, ob, w_dn, w_an)


def _mix_bwd(name, dmixed, oa_raw, proj, z_blk, ob, w_dn, w_an, dep):
    t = oa_raw.shape[0]
    tm = min(256, t)

    def body(dm_ref, oa_ref, z_ref, ob_ref, wd_ref, wa_ref, dep_ref,
             doa_ref, dz_ref, dob_ref, dl_ref, dwd_ref, dwa_ref):
        dwd = jnp.zeros((1, HD), F32)
        dwa = jnp.zeros((1, HD), F32)
        for h in range(N_HEADS):
            sl = slice(h * HD, (h + 1) * HD)
            _, vjp = jax.vjp(_gated_norm, oa_ref[:, sl], z_ref[:, sl], wd_ref[...])
            doa, dz, dw1 = vjp(dm_ref[:, sl])
            doa_ref[:, sl] = doa
            dz_ref[:, sl] = dz.astype(BF16)
            dwd = dwd + dw1
            obh = ob_ref[:, sl]
            _, vjp2 = jax.vjp(_head_rms, obh, wa_ref[...])
            dob, dw2 = vjp2(dm_ref[:, GW + h * HD:GW + (h + 1) * HD])
            dwa = dwa + dw2
            dob_ref[:, sl] = dob
            dl_ref[:, sl] = jnp.broadcast_to(jnp.sum(dob * obh, axis=1, keepdims=True), (tm, HD))

        @pl.when(pl.program_id(0) == 0)
        def _():
            dwd_ref[...] = jnp.zeros_like(dwd_ref)
            dwa_ref[...] = jnp.zeros_like(dwa_ref)

        dwd_ref[...] += dwd
        dwa_ref[...] += dwa

    vec = pl.BlockSpec((1, HD), lambda i: (0, 0))
    wide = pl.BlockSpec((tm, GW), lambda i: (i, 0))
    return pl.pallas_call(
        body, grid=(t // tm,),
        in_specs=[pl.BlockSpec((tm, 2 * GW), lambda i: (i, 0)), wide, pl.BlockSpec((tm, GW), lambda i: (i, z_blk)),
                  wide, vec, vec, ANY],
        out_specs=[wide, pl.BlockSpec((tm, GW), lambda i: (i, z_blk)), wide, wide, vec, vec],
        out_shape=[_sds((t, GW), F32), _sds(proj.shape, BF16), _sds((t, GW), F32), _sds((t, GW), F32),
                   _sds((1, HD), F32), _sds((1, HD), F32)], name=name,
        compiler_params=_params(1))(dmixed, oa_raw, proj, ob, w_dn, w_an, dep)


def _halves(n):
    cut = (n // 256) * 128
    return [(0, cut), (cut, n)]


def _gate_up_swiglu(name, h2, w_gu_g):
    t, d = h2.shape
    n = w_gu_g.shape[2]
    per = N_DEV // 2
    tm = min(512, t)

    def body(a_ref, bg_ref, bu_ref, gu_ref, act_ref):
        a = a_ref[...]
        cuts = _halves(n)
        gs = [_dot(a, bg_ref[:, c0:c1], 1, 0) for c0, c1 in cuts]
        ups = [_dot(a, bu_ref[:, c0:c1], 1, 0) for c0, c1 in cuts]
        for (c0, c1), g, up in zip(cuts, gs, ups):
            gu_ref[0, :, c0:c1] = g.astype(BF16)
            gu_ref[1, :, c0:c1] = up.astype(BF16)
            act_ref[:, c0:c1] = (_silu(g) * up).astype(BF16)

    return pl.pallas_call(
        body, grid=(per, t // tm),
        in_specs=[pl.BlockSpec((tm, d), lambda j, i: (i, 0)), pl.BlockSpec((None, d, n), lambda j, i: (j, 0, 0)),
                  pl.BlockSpec((None, d, n), lambda j, i: (j + per, 0, 0))],
        out_specs=[pl.BlockSpec((2, tm, n), lambda j, i: (0, i, j)), pl.BlockSpec((tm, n), lambda j, i: (i, j))],
        out_shape=[_sds((2, t, per * n), BF16), _sds((t, per * n), BF16)], name=name,
        compiler_params=_params(2))(h2, w_gu_g, w_gu_g)


def _d_gate_up(name, dy16, w_down, gu3, dep):
    t, d = dy16.shape
    f = w_down.shape[0]
    tm, tn = min(1024, t), f // 4

    def body(a_ref, b_ref, g_ref, dep_ref, o_ref):
        a = a_ref[...]
        cuts = _halves(tn)
        dacts = [_dot(a, b_ref[c0:c1, :], 1, 1) for c0, c1 in cuts]
        for (c0, c1), dact in zip(cuts, dacts):
            g, up = g_ref[0, :, c0:c1].astype(F32), g_ref[1, :, c0:c1].astype(F32)
            sg = _sigmoid(g)
            o_ref[0, :, c0:c1] = (dact * up * sg * (1.0 + g * (1.0 - sg))).astype(BF16)
            o_ref[1, :, c0:c1] = (dact * g * sg).astype(BF16)

    return pl.pallas_call(
        body, grid=(f // tn, t // tm),
        in_specs=[pl.BlockSpec((tm, d), lambda j, i: (i, 0)), pl.BlockSpec((tn, d), lambda j, i: (j, 0)),
                  pl.BlockSpec((2, tm, tn), lambda j, i: (0, i, j)), ANY],
        out_specs=pl.BlockSpec((2, tm, tn), lambda j, i: (0, i, j)), out_shape=_sds((2, t, f), BF16), name=name,
        compiler_params=_params(2))(dy16, w_down, gu3, dep)


def _d_h2(name, dgu3, w_gu_g, dep):
    _, t, f = dgu3.shape
    n_dev, d, n = w_gu_g.shape
    per = n_dev // 2
    tm, tn = min(512, t), 512

    def body(g_ref, u_ref, b_ref, dep_ref, o_ref):
        acc = None
        for s in range(n_dev):
            a_ref = g_ref if s < per else u_ref
            part = _dot(a_ref[:, (s % per) * n:(s % per + 1) * n], b_ref[s], 1, 1)
            acc = part if acc is None else acc + part
        o_ref[...] = acc

    return pl.pallas_call(
        body, grid=(d // tn, t // tm),
        in_specs=[pl.BlockSpec((None, tm, f), lambda j, i: (0, i, 0)), pl.BlockSpec((None, tm, f), lambda j, i: (1, i, 0)),
                  pl.BlockSpec((n_dev, tn, n), lambda j, i: (0, j, 0)), ANY],
        out_specs=pl.BlockSpec((tm, tn), lambda j, i: (i, j)), out_shape=_sds((t, d), F32), name=name,
        compiler_params=_params(2))(dgu3, dgu3, w_gu_g, dep)


def _out_proj_norm(name, mixed, w_out, x, w_norm):
    t, d = x.shape
    kdim = mixed.shape[1]
    tm = min(512, t)

    def body(a_ref, b_ref, x_ref, w_ref, x1_ref, h_ref):
        x1 = x_ref[...] + _dot(a_ref[...], b_ref[...], 1, 0)
        x1_ref[...] = x1
        h_ref[...] = _rms_f(x1, w_ref[...]).astype(BF16)

    row = pl.BlockSpec((tm, d), lambda i: (i, 0))
    return pl.pallas_call(
        body, grid=(t // tm,),
        in_specs=[pl.BlockSpec((tm, kdim), lambda i: (i, 0)), pl.BlockSpec((kdim, d), lambda i: (0, 0)), row,
                  pl.BlockSpec((1, d), lambda i: (0, 0))],
        out_specs=[row, row], out_shape=[_sds((t, d), F32), _sds((t, d), BF16)], name=name,
        compiler_params=_params(1))(mixed, w_out, x, w_norm)


def _down_loss(name, act, w_down, x1, target):
    t, f = act.shape
    d = x1.shape[1]
    tm, tn = min(1024, t), 512

    def body(a_ref, b_ref, x_ref, t_ref, dy_ref, dy16_ref, l_ref):
        diff = _dot(a_ref[...], b_ref[...], 1, 0) + x_ref[...] - t_ref[...]
        dyv = diff * (1.0 / d)
        dy_ref[...] = dyv
        dy16_ref[...] = dyv.astype(BF16)
        tot = jnp.sum(jnp.sum(diff * diff, axis=1, keepdims=True), axis=0, keepdims=True) * (0.5 / d)

        @pl.when((pl.program_id(0) == 0) & (pl.program_id(1) == 0))
        def _():
            l_ref[...] = jnp.zeros_like(l_ref)

        l_ref[...] += jnp.broadcast_to(tot, (8, 128))

    tile = pl.BlockSpec((tm, tn), lambda i, j: (i, j))
    return pl.pallas_call(
        body, grid=(t // tm, d // tn),
        in_specs=[pl.BlockSpec((tm, f), lambda i, j: (i, 0)), pl.BlockSpec((f, tn), lambda i, j: (0, j)), tile, tile],
        out_specs=[tile, tile, pl.BlockSpec((8, 128), lambda i, j: (0, 0))],
        out_shape=[_sds((t, d), F32), _sds((t, d), BF16), _sds((8, 128), F32)], name=name,
        compiler_params=_params(2))(act, w_down, x1, target)


def _peer(me, k):
    pid = (me + k) % N_DEV
    return (pid // 4, (pid // 2) % 2, pid % 2)


def _my_id():
    return 4 * lax.axis_index("x") + 2 * lax.axis_index("y") + lax.axis_index("c")


def _exchange(name, arrays, scatter, dep):
    n = len(arrays)

    def body(*refs):
        ins, outs = refs[:n], refs[n + 1:2 * n + 1]
        send_sems, recv_sems, local_sems = refs[2 * n + 1:]
        me = _my_id()
        started = []
        for a in range(n):
            src = ins[a].at[me] if scatter[a] else ins[a]
            loc = pltpu.make_async_copy(src, outs[a].at[me], local_sems.at[a])
            loc.start()
            started.append(loc)
        remote = []
        for k in range(1, N_DEV):
            to = (me + k) % N_DEV
            for a in range(n):
                src = ins[a].at[to] if scatter[a] else ins[a]
                cp = pltpu.make_async_remote_copy(src_ref=src, dst_ref=outs[a].at[me],
                                                  send_sem=send_sems.at[a * (N_DEV - 1) + k - 1], recv_sem=recv_sems.at[a * (N_DEV - 1) + k - 1],
                                                  device_id=_peer(me, k), device_id_type=pl.DeviceIdType.MESH)
                cp.start()
                remote.append(cp)
        for k in range(1, N_DEV):
            frm = (me + N_DEV - k) % N_DEV
            for a in range(n):
                src = ins[a].at[frm] if scatter[a] else ins[a]
                pltpu.make_async_remote_copy(src_ref=src, dst_ref=outs[a].at[frm],
                                             send_sem=send_sems.at[a * (N_DEV - 1) + k - 1], recv_sem=recv_sems.at[a * (N_DEV - 1) + k - 1],
                                             device_id=_peer(me, k), device_id_type=pl.DeviceIdType.MESH).wait_recv()
        for cp in remote:
            cp.wait_send()
        for loc in started:
            loc.wait()

    out_shape = [_sds((N_DEV,) + (a.shape[1:] if sc else a.shape), a.dtype) for a, sc in zip(arrays, scatter)]
    return pl.pallas_call(
        body, in_specs=[ANY] * (n + 1), out_specs=[ANY] * n, out_shape=out_shape,
        scratch_shapes=[pltpu.SemaphoreType.DMA((n * (N_DEV - 1),)), pltpu.SemaphoreType.DMA((n * (N_DEV - 1),)),
                        pltpu.SemaphoreType.DMA((n,))],
        name=name)(*arrays, dep)


def _gather_two_level(name, arrays):
    n = len(arrays)
    per = N_DEV - 1
    units = []
    for a, arr in enumerate(arrays):
        cuts = 4 if arr.shape[0] % 64 == 0 and arr.shape[0] >= 1024 else 1
        units += [(a, p * (arr.shape[0] // cuts), arr.shape[0] // cuts) for p in range(cuts)]
    nu = len(units)

    def body(*refs):
        ins, outs = refs[:n], refs[n:2 * n]
        send_sems, recv_sems, local_sems = refs[2 * n:]
        x, y, c = lax.axis_index("x"), lax.axis_index("y"), lax.axis_index("c")
        me, sibling = (x, y, c), (x, y, 1 - c)
        flip = lambda v, on: v + on - 2 * v * on
        relayed = (flip(x, c), flip(y, 1 - c), c)
        other = (flip(x, 1 - c), flip(y, c), c)
        diagonal = (1 - x, 1 - y, c)
        k_relayed, k_other = 2 - c, 1 + c

        def copy(u, k, block, to, from_input=False):
            a, r0, nr = units[u]
            slot = outs[a].at[4 * block[0] + 2 * block[1] + block[2], pl.ds(r0, nr)]
            return pltpu.make_async_remote_copy(
                src_ref=ins[a].at[pl.ds(r0, nr)] if from_input else slot, dst_ref=slot,
                send_sem=send_sems.at[u * per + k], recv_sem=recv_sems.at[u * per + k], device_id=to,
                device_id_type=pl.DeviceIdType.MESH)

        mine = [pltpu.make_async_copy(ins[a], outs[a].at[4 * x + 2 * y + c], local_sems.at[a]) for a in range(n)]
        for cp in mine:
            cp.start()
        sent = [copy(u, 1, me, (1 - x, y, c), True) for u in range(nu)]
        sent += [copy(u, 2, me, (x, 1 - y, c), True) for u in range(nu)]
        sent += [copy(u, 0, me, sibling, True) for u in range(nu)]
        for cp in sent:
            cp.start()
        for u in range(nu):
            copy(u, k_relayed, relayed, me).wait_recv()
            sent.append(copy(u, 3, relayed, other))
            sent.append(copy(u, 3 + k_relayed, relayed, sibling))
            sent[-2].start()
            sent[-1].start()
        for u in range(nu):
            copy(u, k_other, other, me).wait_recv()
            sent.append(copy(u, 3 + k_other, other, sibling))
            sent[-1].start()
        for u in range(nu):
            copy(u, 3, diagonal, me).wait_recv()
            sent.append(copy(u, 6, diagonal, sibling))
            sent[-1].start()
        for u in range(nu):
            copy(u, 0, sibling, me).wait_recv()
            for j, chip in enumerate([(1 - x, y), (x, 1 - y), (1 - x, 1 - y)]):
                copy(u, 4 + j, (*chip, 1 - c), me).wait_recv()
        for cp in sent:
            cp.wait_send()
        for cp in mine:
            cp.wait()

    return pl.pallas_call(
        body, in_specs=[ANY] * n, out_specs=[ANY] * n,
        out_shape=[_sds((N_DEV,) + a.shape, a.dtype) for a in arrays],
        scratch_shapes=[pltpu.SemaphoreType.DMA((nu * per,)), pltpu.SemaphoreType.DMA((nu * per,)),
                        pltpu.SemaphoreType.DMA((n,))],
        name=name)(*arrays)


HBM = pl.BlockSpec(memory_space=pltpu.HBM)
SEM = pl.BlockSpec(memory_space=pltpu.SEMAPHORE)
EFFECT = pltpu.SideEffectType.DATAFLOW_SIDE_EFFECTING


def _remote_copies(srcs, lands, scatter, send_sems, recv_sems, me, incoming):
    out = []
    for k in range(1, N_DEV):
        other = (me + N_DEV - k) % N_DEV if incoming else (me + k) % N_DEV
        for a in range(len(srcs)):
            sem = a * (N_DEV - 1) + k - 1
            src = srcs[a].at[other] if scatter[a] else srcs[a]
            dst = lands[a].at[other if incoming else me]
            out.append(pltpu.make_async_remote_copy(src_ref=src, dst_ref=dst, send_sem=send_sems.at[sem],
                                                    recv_sem=recv_sems.at[sem], device_id=_peer(me, k),
                                                    device_id_type=pl.DeviceIdType.MESH))
    return out


def _exchange_start(name, arrays, scatter, dep):
    n = len(arrays)
    lands = [lax.empty((N_DEV,) + (a.shape[1:] if sc else a.shape), a.dtype) for a, sc in zip(arrays, scatter)]

    def body(*refs):
        srcs, land_refs = refs[:n], refs[n:2 * n]
        send_sems, recv_sems = refs[2 * n + 1], refs[2 * n + 2]
        token = refs[-1]
        for cp in _remote_copies(srcs, land_refs, scatter, send_sems, recv_sems, _my_id(), False):
            cp.start()
        token[...] = jnp.zeros_like(token)

    n_sem = n * (N_DEV - 1)
    out_shape = ([pltpu.SemaphoreType.DMA((n_sem,)), pltpu.SemaphoreType.DMA((n_sem,))]
                 + [pltpu.HBM(a.shape, a.dtype) for a in arrays] + [pltpu.HBM(l.shape, l.dtype) for l in lands]
                 + [_sds((8, 128), F32)])
    aliases = {i: 2 + i for i in range(2 * n)}
    args = [pltpu.with_memory_space_constraint(a, pltpu.HBM) for a in list(arrays) + lands] + [dep]
    res = pl.pallas_call(
        body, name=name, in_specs=[HBM] * (2 * n) + [ANY], out_shape=out_shape,
        out_specs=[SEM, SEM] + [HBM] * (2 * n) + [pl.BlockSpec(memory_space=pltpu.VMEM)],
        input_output_aliases=aliases, compiler_params=pltpu.CompilerParams(has_side_effects=EFFECT))(*args)
    return dict(send=res[0], recv=res[1], srcs=res[2:2 + n], lands=res[2 + n:2 + 2 * n], token=res[-1],
                scatter=scatter)


def _exchange_wait(name, started, after):
    n = len(started["srcs"])
    scatter = started["scatter"]

    def body(*refs):
        srcs, land_refs = refs[:n], refs[n:2 * n]
        send_sems, recv_sems = refs[2 * n], refs[2 * n + 1]
        me = _my_id()
        for cp in _remote_copies(srcs, land_refs, scatter, send_sems, recv_sems, me, False):
            cp.wait_send()
        for cp in _remote_copies(srcs, land_refs, scatter, send_sems, recv_sems, me, True):
            cp.wait_recv()

    arrs = list(started["srcs"]) + list(started["lands"])
    res = pl.pallas_call(
        body, name=name, in_specs=[HBM] * (2 * n) + [SEM, SEM, ANY],
        out_shape=[pltpu.HBM(a.shape, a.dtype) for a in arrs], out_specs=[HBM] * (2 * n),
        input_output_aliases={i: i for i in range(2 * n)},
        compiler_params=pltpu.CompilerParams(has_side_effects=EFFECT))(*arrs, started["send"], started["recv"], after)
    me = _my_id()
    out = []
    for src, land, sc in zip(res[:n], res[n:], scatter):
        own = lax.dynamic_index_in_dim(src, me, 0, keepdims=True) if sc else src[None]
        out.append(lax.dynamic_update_slice(land, own, (me,) + (0,) * (land.ndim - 1)))
    return out


def _adamw(name, parts, w, m, v):
    r, c = w.shape
    tr, tc = r, c
    if r % 8 == 0:
        tr = next(cand for cand in (128, 88, 64, 40, 8) if r % cand == 0)
    else:
        tc = 256
    c1 = 1.0 / (1.0 - ADAM_B1 ** ADAM_STEP)
    c2 = 1.0 / (1.0 - ADAM_B2 ** ADAM_STEP)

    def body(p_ref, w_ref, m_ref, v_ref, g_ref, d_ref, nm_ref, nv_ref):
        g = p_ref[0].astype(F32)
        for s in range(1, N_DEV):
            g = g + p_ref[s].astype(F32)
        mn = ADAM_B1 * m_ref[...] + (1.0 - ADAM_B1) * g
        vn = ADAM_B2 * v_ref[...] + (1.0 - ADAM_B2) * (g * g)
        g_ref[...] = g
        nm_ref[...] = mn
        nv_ref[...] = vn
        d_ref[...] = -ADAM_LR * ((mn * c1) / (jnp.sqrt(vn * c2) + ADAM_EPS) + ADAM_WD * w_ref[...])

    blk = pl.BlockSpec((tr, tc), lambda i, j: (i, j))
    return pl.pallas_call(
        body, grid=(r // tr, c // tc),
        in_specs=[pl.BlockSpec((N_DEV, tr, tc), lambda i, j: (0, i, j)), blk, blk, blk],
        out_specs=[blk] * 4, out_shape=[_sds((r, c), F32)] * 4, name=name,
        compiler_params=_params(2, VMEM_LIMIT))(parts, w, m, v)


def _pad_rows(a, rows):
    return jnp.pad(a, ((0, rows - a.shape[0]), (0, 0)))


def _lane_row(vec8, offset):
    return jnp.pad(vec8.reshape(1, 8), ((0, 0), (offset, HD - 8 - offset)))


def kernel(x, positions, attn_norm_w, w_in, conv_w, a_log, dt_bias, delta_out_norm_w, q_norm_w, k_norm_w, attn_out_norm_w, w_out, ffn_norm_w, w_gate_up, w_down, loss_target, m_attn_norm_w, m_w_in, m_conv_w, m_a_log, m_dt_bias, m_delta_out_norm_w, m_q_norm_w, m_k_norm_w, m_attn_out_norm_w, m_w_out, m_ffn_norm_w, m_w_gate_up, m_w_down, v_attn_norm_w, v_w_in, v_conv_w, v_a_log, v_dt_bias, v_delta_out_norm_w, v_q_norm_w, v_k_norm_w, v_attn_out_norm_w, v_w_out, v_ffn_norm_w, v_w_gate_up, v_w_down):
    x2 = x[0]
    t, d = x2.shape
    target = loss_target[0]
    pos_col = positions.reshape(t, 1)
    half = HD // 2
    inv = (ROPE_THETA ** (-np.arange(half, dtype=np.float32) / half)).astype(np.float32)
    inv_row = jnp.asarray(np.concatenate([inv, inv]).reshape(1, HD))

    n_in = w_in.shape[2]
    n_gu = w_gate_up.shape[2]
    w_in_g, conv_g = _gather_two_level("gather_in", [w_in[0].astype(BF16), _pad_rows(conv_w[0], 8)])
    out_fly = _exchange_start("gather_out_start", [w_out[0].astype(BF16)], [False], conv_g)
    gu_fly = _exchange_start("gather_gate_up_start", [w_gate_up[0].astype(BF16)], [False], out_fly["token"])
    down_fly = _exchange_start("gather_down_start", [w_down[0].astype(BF16)], [False], gu_fly["token"])
    n_main = 4 * GW
    n_small = 2 * N_HEADS
    segments = [(0, n_main, 0), (n_main + n_small, N_DEV * n_in, n_main), (n_main, n_main + n_small, 7 * GW)]
    pieces = []
    for lo, hi, _ in segments:
        f = lo
        while f < hi:
            j = f // n_in
            end = min(hi, (j + 1) * n_in)
            pieces.append(w_in_g[j][:, f - j * n_in:end - j * n_in])
            f = end
    w_cat = jnp.concatenate(pieces + [jnp.zeros((d, HD - n_small), BF16)], axis=1)
    n_cat = w_cat.shape[1]
    small_blk = (7 * GW) // HD
    conv_w8 =jnp.transpose(conv_g, (1, 0, 2)).reshape(8, 3 * GW)
    alog_row = _lane_row(a_log[0], 8)
    dtb_row = _lane_row(dt_bias[0], 8)

    tm = min(2048, t)
    h1 = _rms_fwd("norm1", x2, attn_norm_w, down_fly["token"])
    tmp, tnp = min(1024, t), n_cat // 3
    proj = _mm("in_proj", h1, w_cat, grid=(t // tmp, n_cat // tnp, 1),
               a_spec=pl.BlockSpec((tmp, d), lambda i, j, k: (i, 0)),
               b_spec=pl.BlockSpec((d, tnp), lambda i, j, k: (0, j)),
               o_spec=pl.BlockSpec((tmp, tnp), lambda i, j, k: (i, j)),
               out_shape=_sds((t, n_cat), F32), ca=1, cb=0, nk=1)
    qn = _conv_fwd("conv_q", proj, conv_w8, 0, True, HD ** -0.5)
    kn = _conv_fwd("conv_k", proj, conv_w8, 1, True, 1.0)
    vv = _conv_fwd("conv_v", proj, conv_w8, 2, False, 1.0)
    beta_b, gc_b = _gates_fwd("gates", proj, small_blk, alog_row, dtb_row)
    u, w, p, tinv, qd, kd = _delta_prep("delta_prep", qn, kn, vv, beta_b, gc_b)
    oa_raw, vn, s_hist = _delta_scan("delta_scan", u, w, p, qd, kd, gc_b)

    cos_t, sin_t = _rope_tables("rope_tables", pos_col, inv_row)
    aq, ak = _qk_fwd("attn_qk", proj, 2, q_norm_w, k_norm_w, cos_t, sin_t)
    ob, lse = _attn_fwd("attn_fwd", aq, ak, proj, 6)
    mixed = _mix_fwd("mix", oa_raw, proj, 3, ob, delta_out_norm_w, attn_out_norm_w)
    (w_out_g,) = _exchange_wait("gather_out_wait", out_fly, mixed)
    w_out_full = w_out_g.reshape(2 * GW, d)
    tn = 512
    x1, h2 = _out_proj_norm("out_proj", mixed, w_out_full, x2, ffn_norm_w)
    per = N_DEV // 2
    (w_gu_g,) = _exchange_wait("gather_gate_up_wait", gu_fly, h2)
    gu3, act = _gate_up_swiglu("gate_up", h2, w_gu_g)
    (w_down_g,) = _exchange_wait("gather_down_wait", down_fly, act)
    w_down_full = w_down_g.reshape(D_FF, d)
    tmd = min(1024, t)
    dy, dy16, loss_tile = _down_loss("down_proj", act, w_down_full, x1, target)

    tk, nkt = t, 1
    g_down = _mm("g_down", act, dy16, grid=(D_FF // 1408, d // 512, nkt),
                 a_spec=pl.BlockSpec((tk, 1408), lambda i, j, k: (k, i)),
                 b_spec=pl.BlockSpec((tk, 512), lambda i, j, k: (k, j)),
                 o_spec=pl.BlockSpec((1408, 512), lambda i, j, k: (i, j)),
                 out_shape=_sds((D_FF, d), F32), ca=0, cb=0, nk=nkt)
    down_g_fly = _exchange_start("reduce_down_start", [g_down.reshape(N_DEV, D_FF // N_DEV, d)], [True], dy16)
    dgu3 = _d_gate_up("d_gate_up", dy16, w_down_full, gu3, down_g_fly["token"])
    g_gu = _mm("g_gate_up", h2, dgu3, grid=(d // 512, N_DEV, nkt),
               a_spec=pl.BlockSpec((tk, 512), lambda i, j, k: (k, i)),
               b_spec=pl.BlockSpec((None, tk, n_gu), lambda i, j, k: (j // per, k, j % per)),
               o_spec=pl.BlockSpec((None, 512, n_gu), lambda i, j, k: (j, i, 0)),
               out_shape=_sds((N_DEV, d, n_gu), F32), ca=0, cb=0, nk=nkt)
    gu_g_fly = _exchange_start("reduce_gate_up_start", [g_gu], [True], dy16)
    dh2 = _d_h2("d_h2", dgu3, w_gu_g, gu_g_fly["token"])
    dx1, dx1_16, g_ffn_norm = _rms_bwd("norm2_bwd", x1, ffn_norm_w, dh2, dy)

    g_out = _mm("g_out", mixed, dx1_16, grid=((2 * GW) // 512, 1, nkt),
                a_spec=pl.BlockSpec((tk, 512), lambda i, j, k: (k, i)),
                b_spec=pl.BlockSpec((tk, d), lambda i, j, k: (k, 0)),
                o_spec=pl.BlockSpec((512, d), lambda i, j, k: (i, 0)),
                out_shape=_sds((2 * GW, d), F32), ca=0, cb=0, nk=nkt)
    out_g_fly = _exchange_start("reduce_out_start", [g_out.reshape(N_DEV, (2 * GW) // N_DEV, d)], [True], g_ffn_norm)
    dmixed = _mm("d_mixed", dx1_16, w_out_full, dep=out_g_fly["token"], grid=(t // tm, (2 * GW) // tn, 1),
                 a_spec=pl.BlockSpec((tm, d), lambda i, j, k: (i, 0)),
                 b_spec=pl.BlockSpec((tn, d), lambda i, j, k: (j, 0)),
                 o_spec=pl.BlockSpec((tm, tn), lambda i, j, k: (i, j)),
                 out_shape=_sds((t, 2 * GW), F32), ca=1, cb=1, nk=1)
    doa, dproj, dob, delta, g_dn, g_an = _mix_bwd("mix_bwd", dmixed, oa_raw, proj, 3, ob,
                                                  delta_out_norm_w, attn_out_norm_w, out_g_fly["token"])
    d_aq, d_ak, d_av = _attn_bwd("attn_bwd", aq, ak, proj, 6, dob, lse, delta)
    dproj, g_qn, g_kn = _qk_bwd("attn_qk_bwd", proj, 2, q_norm_w, k_norm_w, cos_t, sin_t, d_aq, d_ak, dproj)
    dproj = _cast_into("attn_v_bwd", d_av, dproj, 6)

    dvn, dqd, dkd, dw, ddec = _delta_scan_bwd("delta_scan_bwd", doa, w, p, qd, kd, gc_b, vn, s_hist)
    dqn, dkn, dvv, dbeta_b, dg_b = _delta_prep_bwd("delta_prep_bwd", qn, kn, vv, beta_b, gc_b, tinv, u, w, vn,
                                                   doa, dvn, dqd, dkd, dw, ddec)
    dproj, gcw_q = _conv_bwd("conv_q_bwd", proj, conv_w8, dqn, dproj, 0, True, HD ** -0.5)
    dproj, gcw_k = _conv_bwd("conv_k_bwd", proj, conv_w8, dkn, dproj, 1, True, 1.0)
    dproj, gcw_v = _conv_bwd("conv_v_bwd", proj, conv_w8, dvv, dproj, 2, False, 1.0)
    dproj, g_alog_row, g_dtb_row = _gates_bwd("gates_bwd", proj, small_blk, alog_row, dtb_row, dbeta_b, dg_b, dproj)
    tmc = 384
    g_cat = _mm("g_in", dproj, h1, grid=(n_cat // tmc, 1, nkt),
                a_spec=pl.BlockSpec((tk, tmc), lambda i, j, k: (k, i)),
                b_spec=pl.BlockSpec((tk, d), lambda i, j, k: (k, 0)),
                o_spec=pl.BlockSpec((tmc, d), lambda i, j, k: (i, 0)),
                out_shape=_sds((n_cat, d), BF16), ca=0, cb=0, nk=nkt)
    parts = []
    for j in range(N_DEV):
        cols = []
        for lo, hi, start in sorted(segments):
            a, b = max(lo, j * n_in), min(hi, (j + 1) * n_in)
            if a < b:
                cols.append(g_cat[start + a - lo:start + b - lo])
        parts.append(cols[0] if len(cols) == 1 else jnp.concatenate(cols, axis=0))
    g_in_parts = jnp.stack(parts)
    g_conv = jnp.concatenate([gcw_q, gcw_k, gcw_v], axis=1)
    n_cw = conv_w.shape[2]
    g_conv_parts = jnp.transpose(g_conv.reshape(8, N_DEV, n_cw), (1, 0, 2))
    in_g_fly = _exchange_start("reduce_in_start", [g_in_parts, g_conv_parts], [True] * 2, g_dtb_row)
    tmh1 = min(512, t)
    dh1 = _mm("d_h1", dproj, w_cat, dep=in_g_fly["token"], grid=(t // tmh1, d // 1024, 1),
              a_spec=pl.BlockSpec((tmh1, n_cat), lambda i, j, k: (i, 0)),
              b_spec=pl.BlockSpec((1024, n_cat), lambda i, j, k: (j, 0)),
              o_spec=pl.BlockSpec((tmh1, 1024), lambda i, j, k: (i, j)),
              out_shape=_sds((t, d), F32), ca=1, cb=1, nk=1)
    grad_x, _, g_attn_norm = _rms_bwd("norm1_bwd", x2, attn_norm_w, dh1, dx1)

    small_rows = [g_attn_norm.reshape(d // HD, HD), g_ffn_norm.reshape(d // HD, HD), g_dn, g_qn, g_kn, g_an,
                  g_alog_row, g_dtb_row, loss_tile[:1]]
    loss_row = sum(r.shape[0] for r in small_rows) - 1
    small_pack = _pad_rows(jnp.concatenate(small_rows, axis=0), 40)
    (r_down,) = _exchange_wait("reduce_down_wait", down_g_fly, grad_x)
    (r_gu,) = _exchange_wait("reduce_gate_up_wait", gu_g_fly, grad_x)
    (r_out,) = _exchange_wait("reduce_out_wait", out_g_fly, grad_x)
    res_gu = [a[None] for a in _adamw("adamw_gate_up", r_gu, w_gate_up[0], m_w_gate_up[0], v_w_gate_up[0])]
    res_down = [a[None] for a in _adamw("adamw_down", r_down, w_down[0], m_w_down[0], v_w_down[0])]
    res_out = [a[None] for a in _adamw("adamw_out", r_out, w_out[0], m_w_out[0], v_w_out[0])]
    done = (res_gu[3][0, :1, :1] + res_down[3][0, :1, :1] + res_out[3][0, :1, :1])
    r_in, r_conv = _exchange_wait("reduce_in_wait", in_g_fly, done)
    upd_in = _adamw("adamw_in", r_in, jnp.transpose(w_in[0]), jnp.transpose(m_w_in[0]), jnp.transpose(v_w_in[0]))
    res_in = [jnp.transpose(a)[None] for a in upd_in]
    (r_small,) = _exchange("gather_small_grads", [small_pack], [False], upd_in[0])

    def pack_small(an, fn, dn, qn_, kn_, aon, al, db):
        rows = [an.reshape(d // HD, HD), fn.reshape(d // HD, HD), dn, qn_, kn_, aon,
                _lane_row(al[0], 8), _lane_row(db[0], 8)]
        return _pad_rows(jnp.concatenate(rows, axis=0), 40)

    def unpack_small(pk):
        nr = d // HD
        return dict(attn_norm_w=pk[:nr].reshape(1, d), ffn_norm_w=pk[nr:2 * nr].reshape(1, d),
                    delta_out_norm_w=pk[2 * nr:2 * nr + 1], q_norm_w=pk[2 * nr + 1:2 * nr + 2],
                    k_norm_w=pk[2 * nr + 2:2 * nr + 3], attn_out_norm_w=pk[2 * nr + 3:2 * nr + 4],
                    a_log=pk[2 * nr + 4:2 * nr + 5, 8:16], dt_bias=pk[2 * nr + 5:2 * nr + 6, 8:16])

    res_small = _adamw("adamw_small", r_small,
                       pack_small(attn_norm_w, ffn_norm_w, delta_out_norm_w, q_norm_w, k_norm_w, attn_out_norm_w, a_log, dt_bias),
                       pack_small(m_attn_norm_w, m_ffn_norm_w, m_delta_out_norm_w, m_q_norm_w, m_k_norm_w, m_attn_out_norm_w, m_a_log, m_dt_bias),
                       pack_small(v_attn_norm_w, v_ffn_norm_w, v_delta_out_norm_w, v_q_norm_w, v_k_norm_w, v_attn_out_norm_w, v_a_log, v_dt_bias))
    small = [unpack_small(a) for a in res_small]
    res_conv =[a[None, :4] for a in _adamw("adamw_conv", r_conv, _pad_rows(conv_w[0], 8), _pad_rows(m_conv_w[0], 8),
                                            _pad_rows(v_conv_w[0], 8))]

    loss = jnp.sum(r_small[:, loss_row, 0])
    outs = [loss, grad_x[None]]
    for i in range(4):
        s = small[i]
        outs += [s["attn_norm_w"], res_in[i], res_conv[i], s["a_log"], s["dt_bias"], s["delta_out_norm_w"],
                 s["q_norm_w"], s["k_norm_w"], s["attn_out_norm_w"], res_out[i], s["ffn_norm_w"], res_gu[i],
                 res_down[i]]
    return tuple(outs)
```

```python
import numpy as np
import jax
import jax.numpy as jnp
from jax import lax
from jax.experimental import pallas as pl
from jax.experimental.pallas import tpu as pltpu

F32 = jnp.float32
BF16 = jnp.bfloat16

N_DEV = 8
N_HEADS = 8
HD = 128
GW = N_HEADS * HD
CHUNK = 64
PAIR = 2 * CHUNK
SCAN_CHUNKS = 4
SCAN_ROWS = SCAN_CHUNKS * CHUNK
SPAN = 128
DILATIONS = (1, 4, 16)
ROPE_THETA = 10000.0
EPS = 1e-6
D_FF = 5632
ADAM_LR, ADAM_B1, ADAM_B2, ADAM_EPS, ADAM_WD, ADAM_STEP = 0.001, 0.9, 0.999, 1e-8, 0.01, 10
NEG = -1e30
VMEM_LIMIT = 56 * 1024 * 1024
ANY = pl.BlockSpec(memory_space=pl.ANY)
HEADS_PER_STEP = 8


def _params(n_grid, vmem=VMEM_LIMIT):
    return pltpu.CompilerParams(dimension_semantics=("arbitrary",) * n_grid, vmem_limit_bytes=vmem)


def _sds(shape, dtype):
    return jax.ShapeDtypeStruct(tuple(shape), dtype)


def _sigmoid(x):
    return 1.0 / (1.0 + jnp.exp(-x))


def _silu(x):
    return x * _sigmoid(x)


def _softplus(x):
    return jnp.maximum(x, 0.0) + jnp.log(1.0 + jnp.exp(-jnp.abs(x)))


def _dot(a, b, ca, cb):
    return lax.dot_general(a, b, (((ca,), (cb,)), ((), ())), preferred_element_type=F32)


def _b16(x):
    return x if x.dtype == BF16 else x.astype(BF16)


def _split(x):
    hi = x.astype(BF16)
    return hi, (x - hi.astype(F32)).astype(BF16)


def _dot3(a, b, ca, cb):
    a_hi, a_lo = _split(a)
    b_hi, b_lo = _split(b)
    return _dot(a_hi, b_hi, ca, cb) + (_dot(a_hi, b_lo, ca, cb) + _dot(a_lo, b_hi, ca, cb))


def _iota2(shape, axis):
    return lax.broadcasted_iota(jnp.int32, shape, axis)


def _mm(name, a, b, *, grid, a_spec, b_spec, o_spec, out_shape, ca, cb, nk, dep=None):
    assert nk == 1 and grid[2] == 1

    def body(*refs):
        refs[-1][...] = _dot(_b16(refs[0][...]), _b16(refs[1][...]), ca, cb).astype(refs[-1].dtype)

    in_specs = [a_spec, b_spec] + ([ANY] if dep is not None else [])
    args = (a, b) + ((dep,) if dep is not None else ())
    return pl.pallas_call(body, grid=grid, in_specs=in_specs, out_specs=o_spec, out_shape=out_shape,
                          name=name, compiler_params=_params(3))(*args)


def _rms_f(xv, wv):
    return xv * lax.rsqrt(jnp.mean(xv * xv, axis=-1, keepdims=True) + EPS) * wv


def _rms_fwd(name, x, w, dep):
    t, d = x.shape
    tm = min(512, t)

    def body(x_ref, w_ref, dep_ref, o_ref):
        o_ref[...] = _rms_f(x_ref[...], w_ref[...]).astype(BF16)

    row = pl.BlockSpec((tm, d), lambda i: (i, 0))
    vec = pl.BlockSpec((1, d), lambda i: (0, 0))
    return pl.pallas_call(body, grid=(t // tm,), in_specs=[row, vec, ANY], out_specs=row,
                          out_shape=_sds((t, d), BF16), name=name, compiler_params=_params(1))(x, w, dep)


def _rms_bwd(name, x, w, dh, res):
    t, d = x.shape
    tm = min(256, t)

    def body(x_ref, w_ref, dh_ref, res_ref, dx_ref, dx16_ref, dw_ref):
        _, vjp = jax.vjp(_rms_f, x_ref[...], w_ref[...])
        dxv, dwv = vjp(dh_ref[...])
        dxv = dxv + res_ref[...]
        dx_ref[...] = dxv
        dx16_ref[...] = dxv.astype(BF16)

        @pl.when(pl.program_id(0) == 0)
        def _():
            dw_ref[...] = jnp.zeros_like(dw_ref)

        dw_ref[...] += dwv

    row = pl.BlockSpec((tm, d), lambda i: (i, 0))
    vec = pl.BlockSpec((1, d), lambda i: (0, 0))
    return pl.pallas_call(body, grid=(t // tm,), in_specs=[row, vec, row, row], out_specs=[row, row, vec],
                          out_shape=[_sds((t, d), F32), _sds((t, d), BF16), _sds((1, d), F32)], name=name,
                          compiler_params=_params(1))(x, w, dh, res)


def _shift_rows(x, s):
    t = x.shape[0]
    r = pltpu.roll(x, s % t, 0)
    row8 = _iota2((8, x.shape[1]), 0)
    if s > 0:
        return jnp.concatenate([jnp.where(row8 >= s, r[:8], 0.0), r[8:]], axis=0)
    return jnp.concatenate([r[:t - 8], jnp.where(row8 < 8 + s, r[t - 8:], 0.0)], axis=0)


def _conv_taps(xv, w_ref):
    c = w_ref[3:4, :] * xv
    for s in (1, 2, 3):
        c = c + w_ref[3 - s:4 - s, :] * _shift_rows(xv, s)
    return c


def _post_conv(c, l2, scale):
    y = _silu(c)
    if l2:
        y = y * lax.rsqrt(jnp.sum(y * y, axis=-1, keepdims=True) + EPS) * scale
    return y


def _conv_fwd(name, proj, conv_w8, group, l2, scale):
    t = proj.shape[0]

    def body(x_ref, w_ref, o_ref):
        o_ref[...] = _post_conv(_conv_taps(x_ref[...], w_ref), l2, scale)

    return pl.pallas_call(
        body, grid=(N_HEADS,),
        in_specs=[pl.BlockSpec((t, HD), lambda h: (0, h + group * N_HEADS)),
                  pl.BlockSpec((8, HD), lambda h: (0, h + group * N_HEADS))],
        out_specs=pl.BlockSpec((t, HD), lambda h: (0, h)),
        out_shape=_sds((t, GW), F32), name=name, compiler_params=_params(1, VMEM_LIMIT))(proj, conv_w8)


def _conv_bwd(name, proj, conv_w8, dn, dproj, group, l2, scale):
    t = proj.shape[0]

    def body(x_ref, w_ref, dn_ref, dproj_ref, dx_ref, dw_ref):
        xv = x_ref[...]
        c = _conv_taps(xv, w_ref)
        _, vjp = jax.vjp(lambda cc: _post_conv(cc, l2, scale), c)
        (dc,) = vjp(dn_ref[...])
        dx = w_ref[3:4, :] * dc
        dw = jnp.zeros((8, HD), F32)
        rid = _iota2((8, HD), 0)
        dw = dw + jnp.where(rid == 3, jnp.sum(dc * xv, axis=0, keepdims=True), 0.0)
        for s in (1, 2, 3):
            dx = dx + w_ref[3 - s:4 - s, :] * _shift_rows(dc, -s)
            dw = dw + jnp.where(rid == 3 - s, jnp.sum(dc * _shift_rows(xv, s), axis=0, keepdims=True), 0.0)
        dx_ref[...] = dx.astype(BF16)
        dw_ref[...] = dw

    return pl.pallas_call(
        body, grid=(N_HEADS,),
        in_specs=[pl.BlockSpec((t, HD), lambda h: (0, h + group * N_HEADS)),
                  pl.BlockSpec((8, HD), lambda h: (0, h + group * N_HEADS)),
                  pl.BlockSpec((t, HD), lambda h: (0, h)), ANY],
        out_specs=[pl.BlockSpec((t, HD), lambda h: (0, h + group * N_HEADS)), pl.BlockSpec((8, HD), lambda h: (0, h))],
        out_shape=[_sds(dproj.shape, BF16), _sds((8, GW), F32)], input_output_aliases={3: 0}, name=name,
        compiler_params=_params(1, VMEM_LIMIT))(proj, conv_w8, dn, dproj)


def _chunk_cumsum(g, rows):
    pos = rows % CHUNK
    s = 1
    while s < CHUNK:
        g = g + jnp.where(pos >= s, pltpu.roll(g, s, 0), 0.0)
        s *= 2
    return g


def _gates_fwd(name, proj, small_blk, alog_row, dtb_row):
    t = proj.shape[0]
    tm = min(256, t)

    def body(s_ref, a_ref, b_ref, beta_ref, gc_ref):
        sm = s_ref[...]
        beta = _sigmoid(sm)
        g = -jnp.exp(a_ref[...]) * _softplus(sm + b_ref[...])
        gc = _chunk_cumsum(g, _iota2((tm, HD), 0))
        lane = _iota2((tm, HD), 1)
        for h in range(N_HEADS):
            bcol = jnp.sum(jnp.where(lane == h, beta, 0.0), axis=1, keepdims=True)
            gcol = jnp.sum(jnp.where(lane == 8 + h, gc, 0.0), axis=1, keepdims=True)
            beta_ref[:, h * HD:(h + 1) * HD] = jnp.broadcast_to(bcol, (tm, HD))
            gc_ref[:, h * HD:(h + 1) * HD] = jnp.broadcast_to(gcol, (tm, HD))

    vec = pl.BlockSpec((1, HD), lambda i: (0, 0))
    wide = pl.BlockSpec((tm, GW), lambda i: (i, 0))
    return pl.pallas_call(
        body, grid=(t // tm,),
        in_specs=[pl.BlockSpec((tm, HD), lambda i: (i, small_blk)), vec, vec], out_specs=[wide, wide],
        out_shape=[_sds((t, GW), F32), _sds((t, GW), F32)], name=name,
        compiler_params=_params(1))(proj, alog_row, dtb_row)


def _gates_bwd(name, proj, small_blk, alog_row, dtb_row, dbeta_b, dg_b, dproj):
    t = proj.shape[0]
    tm = min(256, t)

    def body(s_ref, a_ref, b_ref, db_ref, dg_ref, dproj_ref, ds_ref, da_ref, dbias_ref):
        sm = s_ref[...]
        lane = _iota2((tm, HD), 1)
        db = jnp.zeros((tm, HD), F32)
        dg = jnp.zeros((tm, HD), F32)
        for h in range(N_HEADS):
            db = db + jnp.where(lane == h, db_ref[:, h * HD:(h + 1) * HD], 0.0)
            dg = dg + jnp.where(lane == 8 + h, dg_ref[:, h * HD:(h + 1) * HD], 0.0)
        beta = _sigmoid(sm)
        ea = jnp.exp(a_ref[...])
        pre = sm + b_ref[...]
        g = -ea * _softplus(pre)
        dpre = dg * (-ea) * _sigmoid(pre)
        ds_ref[...] = (db * beta * (1.0 - beta) + dpre).astype(BF16)

        @pl.when(pl.program_id(0) == 0)
        def _():
            da_ref[...] = jnp.zeros_like(da_ref)
            dbias_ref[...] = jnp.zeros_like(dbias_ref)

        da_ref[...] += jnp.sum(dg * g, axis=0, keepdims=True)
        dbias_ref[...] += jnp.sum(dpre, axis=0, keepdims=True)

    vec = pl.BlockSpec((1, HD), lambda i: (0, 0))
    wide = pl.BlockSpec((tm, GW), lambda i: (i, 0))
    return pl.pallas_call(
        body, grid=(t // tm,),
        in_specs=[pl.BlockSpec((tm, HD), lambda i: (i, small_blk)), vec, vec, wide, wide, ANY],
        out_specs=[pl.BlockSpec((tm, HD), lambda i: (i, small_blk)), vec, vec],
        out_shape=[_sds(dproj.shape, BF16), _sds((1, HD), F32), _sds((1, HD), F32)],
        input_output_aliases={5: 0}, name=name,
        compiler_params=_params(1))(proj, alog_row, dtb_row, dbeta_b, dg_b, dproj)


def _pair_masks():
    ii = _iota2((PAIR, PAIR), 0)
    jj = _iota2((PAIR, PAIR), 1)
    same = (ii // CHUNK) == (jj // CHUNK)
    return ii, jj, same & (ii >= jj), same & (ii > jj)


def _to_row(col_b, ii, jj):
    return jnp.sum(jnp.where(ii == jj, col_b, 0.0), axis=0, keepdims=True)


def _to_col(row, ii, jj):
    return jnp.sum(jnp.where(ii == jj, jnp.broadcast_to(row, (PAIR, PAIR)), 0.0), axis=1, keepdims=True)


def _decay_parts(gc, last_a, last_b, ii, jj, causal):
    diff = gc - _to_row(gc, ii, jj)
    dmat = jnp.where(causal, jnp.exp(jnp.where(causal, diff, 0.0)), 0.0)
    glast = jnp.where(ii < CHUNK, last_a, last_b)
    return dmat, jnp.exp(gc), jnp.exp(glast - gc)


def _unit_lower_inverse(lows, ii, jj):
    eye = jnp.where(ii == jj, 1.0, 0.0)
    mm = lambda xs, ys: [_dot3(a, b, 1, 0) for a, b in zip(xs, ys)]
    plus = lambda xs: [eye + a for a in xs]
    minus = lambda xs: [eye - a for a in xs]
    d1 = [jnp.where((ii // 16) == (jj // 16), low, 0.0) for low in lows]
    d2 = mm(d1, d1)
    a = mm(minus(d1), plus(d2))
    d4 = mm(d2, d2)
    a = mm(a, plus(d4))
    d8 = mm(d4, d4)
    td = mm(a, plus(d8))
    n1 = mm(td, [low - d for low, d in zip(lows, d1)])
    n2 = mm(n1, n1)
    return mm(mm(minus(n1), plus(n2)), td)


def _delta_prep(name, qn, kn, vv, beta_b, gc_b):
    t = qn.shape[0]

    def body(q_ref, k_ref, v_ref, b_ref, g_ref, u_ref, w_ref, p_ref, t_ref, qd_ref, kd_ref):
        ii, jj, causal, strict = _pair_masks()
        sls = [slice(hh * HD, (hh + 1) * HD) for hh in range(HEADS_PER_STEP)]
        lows = []
        for sl in sls:
            q, k, beta = q_ref[:, sl], k_ref[:, sl], b_ref[:, sl]
            dmat, gam, e2 = _decay_parts(g_ref[:, sl], g_ref[CHUNK - 1:CHUNK, sl], g_ref[PAIR - 1:PAIR, sl],
                                         ii, jj, causal)
            k16 = _b16(k)
            lows.append(jnp.where(strict, beta * _dot(k16, k16, 1, 1) * dmat, 0.0))
            p_ref[:, sl] = jnp.where(causal, _dot(_b16(q), k16, 1, 1) * dmat, 0.0).astype(BF16)
            qd_ref[:, sl] = (q * gam).astype(BF16)
            kd_ref[:, sl] = (k * e2).astype(BF16)
        for sl, tinv in zip(sls, _unit_lower_inverse(lows, ii, jj)):
            beta = b_ref[:, sl]
            t_ref[:, sl] = tinv
            u_ref[:, sl] = _dot3(tinv, v_ref[:, sl] * beta, 1, 0)
            w_ref[:, sl] = _dot3(tinv, k_ref[:, sl] * (beta * jnp.exp(g_ref[:, sl])), 1, 0).astype(BF16)

    blk = pl.BlockSpec((PAIR, HEADS_PER_STEP * HD), lambda i, h: (i, h))
    return pl.pallas_call(
        body, grid=(t // PAIR, N_HEADS // HEADS_PER_STEP), in_specs=[blk] * 5, out_specs=[blk] * 6,
        out_shape=[_sds((t, GW), F32), _sds((t, GW), BF16), _sds((t, GW), BF16), _sds((t, GW), F32),
                   _sds((t, GW), BF16), _sds((t, GW), BF16)],
        name=name, compiler_params=_params(2))(qn, kn, vv, beta_b, gc_b)


def _delta_scan(name, u, w, p, qd, kd, gc_b):
    t = u.shape[0]
    n = t // CHUNK

    def body(u_ref, w_ref, p_ref, qd_ref, kd_ref, g_ref, o_ref, vn_ref, sh_ref, state):
        @pl.when(pl.program_id(0) == 0)
        def _():
            state[...] = jnp.zeros_like(state)

        sls = [slice(h * HD, (h + 1) * HD) for h in range(N_HEADS)]
        heads = range(N_HEADS)
        s = [state[h] for h in heads]
        for c in range(SCAN_CHUNKS):
            rows = slice(c * CHUNK, (c + 1) * CHUNK)
            last = slice((c + 1) * CHUNK - 1, (c + 1) * CHUNK)
            for h in heads:
                sh_ref[c, h] = s[h]
            s16 = [_b16(a) for a in s]
            ws = [_dot(w_ref[rows, sls[h]], s16[h], 1, 0) for h in heads]
            qs = [_dot(qd_ref[rows, sls[h]], s16[h], 1, 0) for h in heads]
            vn16 = [_b16(u_ref[rows, sls[h]] - ws[h]) for h in heads]
            pv = [_dot(p_ref[rows, sls[h]], jnp.concatenate([vn16[h], vn16[h]], axis=0), 1, 0) for h in heads]
            kv = [_dot(kd_ref[rows, sls[h]], vn16[h], 0, 0) for h in heads]
            for h in heads:
                o_ref[rows, sls[h]] = qs[h] + pv[h]
                vn_ref[rows, sls[h]] = vn16[h]
            s = [s[h] * jnp.exp(g_ref[last, sls[h]]) + kv[h] for h in heads]
        for h in heads:
            state[h] = s[h]

    blk = pl.BlockSpec((SCAN_ROWS, GW), lambda i: (i, 0))
    return pl.pallas_call(
        body, grid=(t // SCAN_ROWS,), in_specs=[blk] * 6,
        out_specs=[blk, blk, pl.BlockSpec((SCAN_CHUNKS, N_HEADS, HD, HD), lambda i: (i, 0, 0, 0))],
        out_shape=[_sds((t, GW), F32), _sds((t, GW), BF16), _sds((n, N_HEADS, HD, HD), F32)],
        scratch_shapes=[pltpu.VMEM((N_HEADS, HD, HD), F32)], name=name,
        compiler_params=_params(1))(u, w, p, qd, kd, gc_b)


def _delta_scan_bwd(name, do, w, p, qd, kd, gc_b, vn, s_hist):
    t = do.shape[0]
    n = t // CHUNK

    def body(do_ref, w_ref, p_ref, qd_ref, kd_ref, g_ref, vn_ref, sh_ref,
             dvn_ref, dqd_ref, dkd_ref, dw_ref, ddec_ref, dstate):
        @pl.when(pl.program_id(0) == 0)
        def _():
            dstate[...] = jnp.zeros_like(dstate)

        sls = [slice(h * HD, (h + 1) * HD) for h in range(N_HEADS)]
        heads = range(N_HEADS)
        ds = [dstate[h] for h in heads]
        for c in reversed(range(SCAN_CHUNKS)):
            rows = slice(c * CHUNK, (c + 1) * CHUNK)
            last = slice((c + 1) * CHUNK - 1, (c + 1) * CHUNK)
            ds16 = [_b16(a) for a in ds]
            s16 = [_b16(sh_ref[c, h]) for h in heads]
            do16 = [_b16(do_ref[rows, sls[h]]) for h in heads]
            ptdo = [_dot(p_ref[rows, sls[h]], do16[h], 0, 0) for h in heads]
            kds = [_dot(kd_ref[rows, sls[h]], ds16[h], 1, 0) for h in heads]
            qdo = [_dot(qd_ref[rows, sls[h]], do16[h], 0, 0) for h in heads]
            for h in heads:
                dqd_ref[rows, sls[h]] = _dot(do16[h], s16[h], 1, 1)
                dkd_ref[rows, sls[h]] = _dot(vn_ref[rows, sls[h]], ds16[h], 1, 1)
            dvn = [ptdo[h][:CHUNK, :] + ptdo[h][CHUNK:, :] + kds[h] for h in heads]
            dvn16 = [_b16(a) for a in dvn]
            wdv = [_dot(w_ref[rows, sls[h]], dvn16[h], 0, 0) for h in heads]
            for h in heads:
                dvn_ref[rows, sls[h]] = dvn[h]
                dw_ref[rows, sls[h]] = -_dot(dvn16[h], s16[h], 1, 1)
                tot = jnp.sum(jnp.sum(sh_ref[c, h] * ds[h], axis=1, keepdims=True), axis=0, keepdims=True)
                ddec_ref[c * 8:(c + 1) * 8, sls[h]] = jnp.broadcast_to(tot, (8, HD))
            ds = [ds[h] * jnp.exp(g_ref[last, sls[h]]) + qdo[h] - wdv[h] for h in heads]
        for h in heads:
            dstate[h] = ds[h]

    npair = t // SCAN_ROWS
    blk = pl.BlockSpec((SCAN_ROWS, GW), lambda i: (npair - 1 - i, 0))
    return pl.pallas_call(
        body, grid=(npair,),
        in_specs=[blk] * 7 + [pl.BlockSpec((SCAN_CHUNKS, N_HEADS, HD, HD), lambda i: (npair - 1 - i, 0, 0, 0))],
        out_specs=[blk] * 4 + [pl.BlockSpec((8 * SCAN_CHUNKS, GW), lambda i: (npair - 1 - i, 0))],
        out_shape=[_sds((t, GW), F32)] * 4 + [_sds((n * 8, GW), F32)],
        scratch_shapes=[pltpu.VMEM((N_HEADS, HD, HD), F32)], name=name,
        compiler_params=_params(1))(do, w, p, qd, kd, gc_b, vn, s_hist)


def _delta_prep_bwd(name, qn, kn, vv, beta_b, gc_b, tinv, u, w, vn, do, dvn, dqd, dkd, dw, ddec):
    t = qn.shape[0]

    def body(q_ref, k_ref, v_ref, b_ref, g_ref, t_ref, u_ref, w_ref, vn_ref, do_ref, dvn_ref, dqd_ref,
             dkd_ref, dw_ref, ddec_ref, dq_ref, dk_ref, dv_ref, dbeta_ref, dg_ref):
        ii, jj, causal, strict = _pair_masks()
        suffix = ((ii // CHUNK) == (jj // CHUNK)) & (jj >= ii)
        first = ii < CHUNK
        rs = lambda a: jnp.sum(a, axis=1, keepdims=True)
        sls = [slice(hh * HD, (hh + 1) * HD) for hh in range(HEADS_PER_STEP)]
        xs = [_dot3(t_ref[:, sl], dvn_ref[:, sl], 0, 0) for sl in sls]
        ys = [_dot3(t_ref[:, sl], dw_ref[:, sl], 0, 0) for sl in sls]
        k16s = [_b16(k_ref[:, sl]) for sl in sls]
        kks = [_dot(k16, k16, 1, 1) for k16 in k16s]
        qks = [_dot(_b16(q_ref[:, sl]), k16, 1, 1) for sl, k16 in zip(sls, k16s)]
        dps = [jnp.where(causal, _dot(_b16(do_ref[:, sl]), vn_ref[:, sl], 1, 1), 0.0) for sl in sls]
        das = [-jnp.where(strict, _dot(_b16(x), _b16(u_ref[:, sl]), 1, 1) + _dot(_b16(y), w_ref[:, sl], 1, 1), 0.0)
               for sl, x, y in zip(sls, xs, ys)]
        for hh, sl in enumerate(sls):
            q, k, v, beta, gc = q_ref[:, sl], k_ref[:, sl], v_ref[:, sl], b_ref[:, sl], g_ref[:, sl]
            last_a, last_b = g_ref[CHUNK - 1:CHUNK, sl], g_ref[PAIR - 1:PAIR, sl]
            dmat, gam, e2 = _decay_parts(gc, last_a, last_b, ii, jj, causal)
            q16, k16 = _b16(q), k16s[hh]
            kk, qk, dp, x, y, da = kks[hh], qks[hh], dps[hh], xs[hh], ys[hh], das[hh]
            dqd, dkd = dqd_ref[:, sl], dkd_ref[:, sl]
            dpd16 = _b16(dp * dmat)
            dkk16 = _b16(da * beta * dmat)
            dq_ref[:, sl] = gam * dqd + _dot(dpd16, k16, 1, 0)
            dk_ref[:, sl] = (e2 * dkd + _dot(dpd16, q16, 0, 0) + beta * gam * y
                             + _dot(dkk16, k16, 1, 0) + _dot(dkk16, k16, 0, 0))
            dv_ref[:, sl] = beta * x
            dbeta = rs(v * x) + rs(k * gam * y) + rs(da * kk * dmat)
            dbeta_ref[:, sl] = jnp.broadcast_to(dbeta, (PAIR, HD))
            m = (dp * qk + da * beta * kk) * dmat
            dgam = rs(q * dqd) + rs(k * beta * y)
            de2 = rs(k * dkd)
            colsum = _to_col(jnp.sum(m, axis=0, keepdims=True), ii, jj)
            te2 = de2 * e2
            dgc = rs(m) - colsum + gam * dgam - te2
            tail_a = jnp.sum(jnp.where(first, te2, 0.0), axis=0, keepdims=True)
            tail_b = jnp.sum(jnp.where(first, 0.0, te2), axis=0, keepdims=True)
            dgc = dgc + jnp.where(ii == CHUNK - 1, tail_a + ddec_ref[0:1, sl] * jnp.exp(last_a), 0.0)
            dgc = dgc + jnp.where(ii == PAIR - 1, tail_b + ddec_ref[8:9, sl] * jnp.exp(last_b), 0.0)
            dgc_row = _to_row(dgc, ii, jj)
            dg = jnp.sum(jnp.where(suffix, jnp.broadcast_to(dgc_row, (PAIR, PAIR)), 0.0), axis=1, keepdims=True)
            dg_ref[:, sl] = jnp.broadcast_to(dg, (PAIR, HD))

    blk = pl.BlockSpec((PAIR, HEADS_PER_STEP * HD), lambda i, h: (i, h))
    return pl.pallas_call(
        body, grid=(t // PAIR, N_HEADS // HEADS_PER_STEP),
        in_specs=[blk] * 14 + [pl.BlockSpec((16, HEADS_PER_STEP * HD), lambda i, h: (i, h))], out_specs=[blk] * 5,
        out_shape=[_sds((t, GW), F32)] * 5, name=name,
        compiler_params=_params(2))(qn, kn, vv, beta_b, gc_b, tinv, u, w, vn, do, dvn, dqd, dkd, dw, ddec)


def _rope_tables(name, pos_col, inv_row):
    t = pos_col.shape[0]
    tm = min(1024, t)

    def body(pos_ref, inv_ref, cos_ref, sin_ref):
        ang = pos_ref[...].astype(F32) * inv_ref[...]
        lane = _iota2(ang.shape, 1)
        cos_ref[...] = jnp.cos(ang)
        sin_ref[...] = jnp.where(lane < HD // 2, -1.0, 1.0) * jnp.sin(ang)

    tab = pl.BlockSpec((tm, HD), lambda i: (i, 0))
    return pl.pallas_call(
        body, grid=(t // tm,), in_specs=[pl.BlockSpec((tm, 1), lambda i: (i, 0)), pl.BlockSpec((1, HD), lambda i: (0, 0))],
        out_specs=[tab, tab], out_shape=[_sds((t, HD), F32)] * 2, name=name,
        compiler_params=_params(1))(pos_col, inv_row)


def _head_rms(xh, wv):
    return xh * lax.rsqrt(jnp.mean(xh * xh, axis=-1, keepdims=True) + EPS) * wv


def _qk_fwd(name, proj, pair_blk, wq_row, wk_row, cos_t, sin_t):
    t = proj.shape[0]
    tm = min(256, t)

    def body(x_ref, wq_ref, wk_ref, cos_ref, sin_ref, q_ref, k_ref):
        cos, sin = cos_ref[...], sin_ref[...]
        for o_ref, w_ref, base in ((q_ref, wq_ref, 0), (k_ref, wk_ref, GW)):
            for h in range(N_HEADS):
                y = _head_rms(x_ref[:, base + h * HD:base + (h + 1) * HD], w_ref[...])
                o_ref[:, h * HD:(h + 1) * HD] = y * cos + pltpu.roll(y, HD // 2, 1) * sin

    vec = pl.BlockSpec((1, HD), lambda i: (0, 0))
    tab = pl.BlockSpec((tm, HD), lambda i: (i, 0))
    wide = pl.BlockSpec((tm, GW), lambda i: (i, 0))
    return pl.pallas_call(
        body, grid=(t // tm,),
        in_specs=[pl.BlockSpec((tm, 2 * GW), lambda i: (i, pair_blk)), vec, vec, tab, tab],
        out_specs=[wide, wide], out_shape=[_sds((t, GW), F32)] * 2, name=name,
        compiler_params=_params(1))(proj, wq_row, wk_row, cos_t, sin_t)


def _qk_bwd(name, proj, pair_blk, wq_row, wk_row, cos_t, sin_t, dq_full, dk_full, dproj):
    t = proj.shape[0]
    tm = min(256, t)

    def body(x_ref, wq_ref, wk_ref, cos_ref, sin_ref, dq_ref, dk_ref, dproj_ref, dx_ref, dwq_ref, dwk_ref):
        cos, sin = cos_ref[...], sin_ref[...]

        @pl.when(pl.program_id(0) == 0)
        def _():
            dwq_ref[...] = jnp.zeros_like(dwq_ref)
            dwk_ref[...] = jnp.zeros_like(dwk_ref)

        for dy_ref, w_ref, dw_ref, base in ((dq_ref, wq_ref, dwq_ref, 0), (dk_ref, wk_ref, dwk_ref, GW)):
            dw = jnp.zeros((1, HD), F32)
            for h in range(N_HEADS):
                dy = dy_ref[:, h * HD:(h + 1) * HD]
                dy = dy * cos - pltpu.roll(dy, HD // 2, 1) * sin
                _, vjp = jax.vjp(_head_rms, x_ref[:, base + h * HD:base + (h + 1) * HD], w_ref[...])
                dx, dwh = vjp(dy)
                dw = dw + dwh
                dx_ref[:, base + h * HD:base + (h + 1) * HD] = dx.astype(BF16)
            dw_ref[...] += dw

    vec = pl.BlockSpec((1, HD), lambda i: (0, 0))
    tab = pl.BlockSpec((tm, HD), lambda i: (i, 0))
    wide = pl.BlockSpec((tm, GW), lambda i: (i, 0))
    pair = pl.BlockSpec((tm, 2 * GW), lambda i: (i, pair_blk))
    return pl.pallas_call(
        body, grid=(t // tm,), in_specs=[pair, vec, vec, tab, tab, wide, wide, ANY],
        out_specs=[pair, vec, vec],
        out_shape=[_sds(dproj.shape, BF16), _sds((1, HD), F32), _sds((1, HD), F32)], input_output_aliases={7: 0},
        name=name, compiler_params=_params(1))(proj, wq_row, wk_row, cos_t, sin_t, dq_full, dk_full, dproj)


def _cast_into(name, x, dproj, blk_idx):
    t = x.shape[0]
    tm = min(512, t)

    def body(x_ref, dproj_ref, o_ref):
        o_ref[...] = x_ref[...].astype(BF16)

    return pl.pallas_call(
        body, grid=(t // tm,), in_specs=[pl.BlockSpec((tm, GW), lambda i: (i, 0)), ANY],
        out_specs=pl.BlockSpec((tm, GW), lambda i: (i, blk_idx)), out_shape=_sds(dproj.shape, BF16),
        input_output_aliases={1: 0}, name=name, compiler_params=_params(1))(x, dproj)


GROUP = SPAN * max(DILATIONS)
SCALE = HD ** -0.5
TILE_BATCH = 8


def _band_mask(lo):
    qi = _iota2((SPAN, 2 * SPAN), 0)
    ki = _iota2((SPAN, 2 * SPAN), 1)
    return (ki >= qi) & (ki <= qi + SPAN) & (ki >= lo)


def _tiles():
    return [(pi, r, u, rho) for pi, r in enumerate(DILATIONS) for rho in range(r) for u in range(GROUP // (SPAN * r))]


def _rows(r, u, rho):
    return pl.ds(u * SPAN * r + rho, SPAN, stride=r) if r > 1 else pl.ds(u * SPAN, SPAN)


def _attn_fwd(name, q, k, v, v_blk):
    t = q.shape[0]

    def body(qc_ref, kc_ref, vc_ref, kp_ref, vp_ref, ob_ref, lse_ref, o_scr, l_scr):
        mask_in = _band_mask(0)
        mask_edge = _band_mask(jnp.where(pl.program_id(0) == 0, SPAN, 0))
        tiles = _tiles()
        k_own = v_own = None
        for b0 in range(0, len(tiles), TILE_BATCH):
            work = []
            for pi, r, u, rho in tiles[b0:b0 + TILE_BATCH]:
                rows = _rows(r, u, rho)
                if u > 0:
                    k_prev, v_prev, mask = k_own, v_own, mask_in
                else:
                    prows = _rows(r, GROUP // (SPAN * r) - 1, rho)
                    k_prev, v_prev, mask = kp_ref[prows, :].astype(BF16), vp_ref[prows, :].astype(BF16), mask_edge
                k_own, v_own = kc_ref[rows, :].astype(BF16), vc_ref[rows, :].astype(BF16)
                work.append((pi, rows, mask, qc_ref[rows, :].astype(BF16), jnp.concatenate([k_prev, k_own], axis=0),
                             jnp.concatenate([v_prev, v_own], axis=0)))
            scores = [_dot(qt, kcat, 1, 1) for _, _, _, qt, kcat, _ in work]
            soft = []
            for (_, _, mask, _, _, _), s in zip(work, scores):
                s = jnp.where(mask, s * SCALE, NEG)
                m = jnp.max(s, axis=1, keepdims=True)
                p = jnp.exp(s - m)
                soft.append((m, _b16(p), jnp.sum(p, axis=1, keepdims=True)))
            outs = [_dot(p, vcat, 1, 0) for (_, p, _), (_, _, _, _, _, vcat) in zip(soft, work)]
            for (pi, rows, _, _, _, _), (m, _, den), o in zip(work, soft, outs):
                o_scr[pi, rows, :] = o / den
                l_scr[pi, rows, :] = jnp.broadcast_to(m + jnp.log(den), (SPAN, HD))
        step = 256
        for c in range(GROUP // step):
            sl = pl.ds(c * step, step)
            ob, lse = _merge([o_scr[i, sl, :] for i in range(3)], [l_scr[i, sl, :] for i in range(3)])
            ob_ref[sl, :] = ob
            lse_ref[sl, :] = lse

    cur = pl.BlockSpec((GROUP, HD), lambda g, h: (g, h))
    prev = pl.BlockSpec((GROUP, HD), lambda g, h: (jnp.maximum(g - 1, 0), h))
    vcur = pl.BlockSpec((GROUP, HD), lambda g, h: (g, v_blk * N_HEADS + h))
    vprev = pl.BlockSpec((GROUP, HD), lambda g, h: (jnp.maximum(g - 1, 0), v_blk * N_HEADS + h))
    return pl.pallas_call(
        body, grid=(t // GROUP, N_HEADS), in_specs=[cur, cur, vcur, prev, vprev], out_specs=[cur, cur],
        out_shape=[_sds((t, GW), F32), _sds((t, GW), F32)],
        scratch_shapes=[pltpu.VMEM((3, GROUP, HD), F32), pltpu.VMEM((3, GROUP, HD), F32)], name=name,
        compiler_params=_params(2))(q, k, v, k, v)


def _attn_bwd(name, q, k, v, v_blk, do, lse, delta):
    t = q.shape[0]
    ng = t // GROUP

    def probs(work):
        scores = [_dot(qt, kcat, 1, 1) for qt, _, _, _, kcat, _, _ in work]
        dps = [_dot(dot, vcat, 1, 1) for _, dot, _, _, _, vcat, _ in work]
        out = []
        for (_, _, lt, dlt, kcat, _, mask), s, dp in zip(work, scores, dps):
            wide = kcat.shape[0] // SPAN
            lw = jnp.concatenate([lt] * wide, axis=1) if wide > 1 else lt
            dw = jnp.concatenate([dlt] * wide, axis=1) if wide > 1 else dlt
            p = jnp.exp(jnp.where(mask, s * SCALE - lw, NEG))
            out.append((_b16(p * (dp - dw) * SCALE), _b16(p)))
        return out

    def body(qc_ref, kc_ref, vc_ref, doc_ref, lc_ref, dc_ref, kp_ref, vp_ref, qn_ref, don_ref, ln_ref, dn_ref,
             dq_ref, dk_ref, dv_ref):
        g = pl.program_id(0)
        mask_in = _band_mask(0)
        mask_edge = _band_mask(jnp.where(g == 0, SPAN, 0))
        dk_ref[...] = jnp.zeros_like(dk_ref)
        dv_ref[...] = jnp.zeros_like(dv_ref)
        tiles = _tiles()
        k_own = v_own = None
        for b0 in range(0, len(tiles), TILE_BATCH):
            where, work = [], []
            for pi, r, u, rho in tiles[b0:b0 + TILE_BATCH]:
                rows = _rows(r, u, rho)
                if u > 0:
                    prows, k_prev, v_prev, mask = _rows(r, u - 1, rho), k_own, v_own, mask_in
                else:
                    prows = _rows(r, GROUP // (SPAN * r) - 1, rho)
                    k_prev, v_prev, mask = kp_ref[prows, :].astype(BF16), vp_ref[prows, :].astype(BF16), mask_edge
                k_own, v_own = kc_ref[rows, :].astype(BF16), vc_ref[rows, :].astype(BF16)
                where.append((pi, u, rows, prows))
                work.append((qc_ref[rows, :].astype(BF16), doc_ref[rows, :].astype(BF16), lc_ref[rows, :], dc_ref[rows, :],
                             jnp.concatenate([k_prev, k_own], axis=0), jnp.concatenate([v_prev, v_own], axis=0), mask))
            dsp = probs(work)
            dqs = [_dot(ds, w[4], 1, 0) for (ds, _), w in zip(dsp, work)]
            dks = [_dot(ds, w[0], 0, 0) for (ds, _), w in zip(dsp, work)]
            dvs = [_dot(p, w[1], 0, 0) for (_, p), w in zip(dsp, work)]
            for (pi, u, rows, prows), dq_t, dk2, dv2 in zip(where, dqs, dks, dvs):
                if pi == 0:
                    dq_ref[rows, :] = dq_t
                else:
                    dq_ref[rows, :] += dq_t
                dk_ref[rows, :] += dk2[SPAN:, :]
                dv_ref[rows, :] += dv2[SPAN:, :]
                if u > 0:
                    dk_ref[prows, :] += dk2[:SPAN, :]
                    dv_ref[prows, :] += dv2[:SPAN, :]
        qi = _iota2((SPAN, SPAN), 0)
        ki = _iota2((SPAN, SPAN), 1)
        mask_next = (ki >= qi) & (ki < jnp.where(g == ng - 1, 0, SPAN))
        edge = [(r, rho) for r in DILATIONS for rho in range(r)]
        for b0 in range(0, len(edge), TILE_BATCH):
            where, work = [], []
            for r, rho in edge[b0:b0 + TILE_BATCH]:
                krows, qrows = _rows(r, GROUP // (SPAN * r) - 1, rho), _rows(r, 0, rho)
                where.append(krows)
                work.append((qn_ref[qrows, :].astype(BF16), don_ref[qrows, :].astype(BF16), ln_ref[qrows, :],
                             dn_ref[qrows, :], kc_ref[krows, :].astype(BF16), vc_ref[krows, :].astype(BF16), mask_next))
            dsp = probs(work)
            dks = [_dot(ds, w[0], 0, 0) for (ds, _), w in zip(dsp, work)]
            dvs = [_dot(p, w[1], 0, 0) for (_, p), w in zip(dsp, work)]
            for krows, dk1, dv1 in zip(where, dks, dvs):
                dk_ref[krows, :] += dk1
                dv_ref[krows, :] += dv1

    cur = pl.BlockSpec((GROUP, HD), lambda g, h: (g, h))
    prev = pl.BlockSpec((GROUP, HD), lambda g, h: (jnp.maximum(g - 1, 0), h))
    nxt = pl.BlockSpec((GROUP, HD), lambda g, h: (jnp.minimum(g + 1, ng - 1), h))
    vcur = pl.BlockSpec((GROUP, HD), lambda g, h: (g, v_blk * N_HEADS + h))
    vprev = pl.BlockSpec((GROUP, HD), lambda g, h: (jnp.maximum(g - 1, 0), v_blk * N_HEADS + h))
    return pl.pallas_call(
        body, grid=(ng, N_HEADS), in_specs=[cur, cur, vcur, cur, cur, cur, prev, vprev] + [nxt] * 4,
        out_specs=[cur] * 3,
        out_shape=[_sds((t, GW), F32)] * 3, name=name,
        compiler_params=_params(2))(q, k, v, do, lse, delta, k, v, q, do, lse, delta)


def _merge(os_, ls_):
    m = jnp.maximum(jnp.maximum(ls_[0], ls_[1]), ls_[2])
    ws = [jnp.exp(l - m) for l in ls_]
    tot = ws[0] + ws[1] + ws[2]
    ob = (ws[0] * os_[0] + ws[1] * os_[1] + ws[2] * os_[2]) / tot
    return ob, m + jnp.log(tot)


def _gated_norm(oa, z, wv):
    return _head_rms(oa, wv) * _silu(z)


def _mix_fwd(name, oa_raw, proj, z_blk, ob, w_dn, w_an):
    t = oa_raw.shape[0]
    tm = min(256, t)

    def body(oa_ref, z_ref, ob_ref, wd_ref, wa_ref, mix_ref):
        for h in range(N_HEADS):
            sl = slice(h * HD, (h + 1) * HD)
            mix_ref[:, sl] = _gated_norm(oa_ref[:, sl], z_ref[:, sl], wd_ref[...]).astype(BF16)
            mix_ref[:, GW + h * HD:GW + (h + 1) * HD] = _head_rms(ob_ref[:, sl], wa_ref[...]).astype(BF16)

    vec = pl.BlockSpec((1, HD), lambda i: (0, 0))
    wide = pl.BlockSpec((tm, GW), lambda i: (i, 0))
    return pl.pallas_call(
        body, grid=(t // tm,),
        in_specs=[wide, pl.BlockSpec((tm, GW), lambda i: (i, z_blk)), wide, vec, vec],
        out_specs=pl.BlockSpec((tm, 2 * GW), lambda i: (i, 0)),
        out_shape=_sds((t, 2 * GW), BF16), name=name,
        compiler_params=_params(1))(oa_raw, proj, ob, w_dn, w_an)


def _mix_bwd(name, dx1_16, w_out, oa_raw, proj, z_blk, ob, w_dn, w_an, dep):
    t, d = dx1_16.shape
    tm = min(512, t)

    def body(dx_ref, wo_ref, oa_ref, z_ref, ob_ref, wd_ref, wa_ref, dep_ref,
             doa_ref, dz_ref, dob_ref, dl_ref, dwd_ref, dwa_ref):
        dwd = jnp.zeros((1, HD), F32)
        dwa = jnp.zeros((1, HD), F32)
        dxv = dx_ref[...]
        pairs = [_dot(dxv, wo_ref[2 * p * HD:2 * (p + 1) * HD, :], 1, 1) for p in range(N_HEADS)]
        heads = [half for pr in pairs for half in (pr[:, :HD], pr[:, HD:])]
        dm_a, dm_b = heads[:N_HEADS], heads[N_HEADS:]
        for h in range(N_HEADS):
            sl = slice(h * HD, (h + 1) * HD)
            _, vjp = jax.vjp(_gated_norm, oa_ref[:, sl], z_ref[:, sl], wd_ref[...])
            doa, dz, dw1 = vjp(dm_a[h])
            doa_ref[:, sl] = doa
            dz_ref[:, sl] = dz.astype(BF16)
            dwd = dwd + dw1
            obh = ob_ref[:, sl]
            _, vjp2 = jax.vjp(_head_rms, obh, wa_ref[...])
            dob, dw2 = vjp2(dm_b[h])
            dwa = dwa + dw2
            dob_ref[:, sl] = dob
            dl_ref[:, sl] = jnp.broadcast_to(jnp.sum(dob * obh, axis=1, keepdims=True), (tm, HD))

        @pl.when(pl.program_id(0) == 0)
        def _():
            dwd_ref[...] = jnp.zeros_like(dwd_ref)
            dwa_ref[...] = jnp.zeros_like(dwa_ref)

        dwd_ref[...] += dwd
        dwa_ref[...] += dwa

    vec = pl.BlockSpec((1, HD), lambda i: (0, 0))
    wide = pl.BlockSpec((tm, GW), lambda i: (i, 0))
    return pl.pallas_call(
        body, grid=(t // tm,),
        in_specs=[pl.BlockSpec((tm, d), lambda i: (i, 0)), pl.BlockSpec((2 * GW, d), lambda i: (0, 0)), wide,
                  pl.BlockSpec((tm, GW), lambda i: (i, z_blk)), wide, vec, vec, ANY],
        out_specs=[wide, pl.BlockSpec((tm, GW), lambda i: (i, z_blk)), wide, wide, vec, vec],
        out_shape=[_sds((t, GW), F32), _sds(proj.shape, BF16), _sds((t, GW), F32), _sds((t, GW), F32),
                   _sds((1, HD), F32), _sds((1, HD), F32)], name=name,
        compiler_params=_params(1))(dx1_16, w_out, oa_raw, proj, ob, w_dn, w_an, dep)


def _halves(n):
    cut = (n // 256) * 128
    return [(0, cut), (cut, n)]


def _gate_up_swiglu(name, h2, w_gu_g):
    t, d = h2.shape
    n = w_gu_g.shape[2]
    per = N_DEV // 2
    tm = min(512, t)

    def body(a_ref, bg_ref, bu_ref, gu_ref, act_ref):
        a = a_ref[...]
        cuts = _halves(n)
        gs = [_dot(a, bg_ref[:, c0:c1], 1, 0) for c0, c1 in cuts]
        ups = [_dot(a, bu_ref[:, c0:c1], 1, 0) for c0, c1 in cuts]
        for (c0, c1), g, up in zip(cuts, gs, ups):
            gu_ref[0, :, c0:c1] = g.astype(BF16)
            gu_ref[1, :, c0:c1] = up.astype(BF16)
            act_ref[:, c0:c1] = (_silu(g) * up).astype(BF16)

    return pl.pallas_call(
        body, grid=(per, t // tm),
        in_specs=[pl.BlockSpec((tm, d), lambda j, i: (i, 0)), pl.BlockSpec((None, d, n), lambda j, i: (j, 0, 0)),
                  pl.BlockSpec((None, d, n), lambda j, i: (j + per, 0, 0))],
        out_specs=[pl.BlockSpec((2, tm, n), lambda j, i: (0, i, j)), pl.BlockSpec((tm, n), lambda j, i: (i, j))],
        out_shape=[_sds((2, t, per * n), BF16), _sds((t, per * n), BF16)], name=name,
        compiler_params=_params(2))(h2, w_gu_g, w_gu_g)


def _d_gate_up(name, dy16, w_down, gu3, dep):
    t, d = dy16.shape
    f = w_down.shape[0]
    tm, tn = min(1024, t), f // 4

    def body(a_ref, b_ref, g_ref, dep_ref, o_ref):
        a = a_ref[...]
        cuts = _halves(tn)
        dacts = [_dot(a, b_ref[c0:c1, :], 1, 1) for c0, c1 in cuts]
        for (c0, c1), dact in zip(cuts, dacts):
            g, up = g_ref[0, :, c0:c1].astype(F32), g_ref[1, :, c0:c1].astype(F32)
            sg = _sigmoid(g)
            o_ref[0, :, c0:c1] = (dact * up * sg * (1.0 + g * (1.0 - sg))).astype(BF16)
            o_ref[1, :, c0:c1] = (dact * g * sg).astype(BF16)

    return pl.pallas_call(
        body, grid=(f // tn, t // tm),
        in_specs=[pl.BlockSpec((tm, d), lambda j, i: (i, 0)), pl.BlockSpec((tn, d), lambda j, i: (j, 0)),
                  pl.BlockSpec((2, tm, tn), lambda j, i: (0, i, j)), ANY],
        out_specs=pl.BlockSpec((2, tm, tn), lambda j, i: (0, i, j)), out_shape=_sds((2, t, f), BF16), name=name,
        compiler_params=_params(2))(dy16, w_down, gu3, dep)


def _d_h2(name, dgu3, w_gu_g, dep):
    _, t, f = dgu3.shape
    n_dev, d, n = w_gu_g.shape
    per = n_dev // 2
    tm, tn = min(512, t), 512

    def body(g_ref, u_ref, b_ref, dep_ref, o_ref):
        acc = None
        for s in range(n_dev):
            a_ref = g_ref if s < per else u_ref
            part = _dot(a_ref[:, (s % per) * n:(s % per + 1) * n], b_ref[s], 1, 1)
            acc = part if acc is None else acc + part
        o_ref[...] = acc

    return pl.pallas_call(
        body, grid=(d // tn, t // tm),
        in_specs=[pl.BlockSpec((None, tm, f), lambda j, i: (0, i, 0)), pl.BlockSpec((None, tm, f), lambda j, i: (1, i, 0)),
                  pl.BlockSpec((n_dev, tn, n), lambda j, i: (0, j, 0)), ANY],
        out_specs=pl.BlockSpec((tm, tn), lambda j, i: (i, j)), out_shape=_sds((t, d), F32), name=name,
        compiler_params=_params(2))(dgu3, dgu3, w_gu_g, dep)


def _out_proj_norm(name, mixed, w_out, x, w_norm):
    t, d = x.shape
    kdim = mixed.shape[1]
    tm = min(512, t)

    def body(a_ref, b_ref, x_ref, w_ref, x1_ref, h_ref):
        x1 = x_ref[...] + _dot(a_ref[...], b_ref[...], 1, 0)
        x1_ref[...] = x1
        h_ref[...] = _rms_f(x1, w_ref[...]).astype(BF16)

    row = pl.BlockSpec((tm, d), lambda i: (i, 0))
    return pl.pallas_call(
        body, grid=(t // tm,),
        in_specs=[pl.BlockSpec((tm, kdim), lambda i: (i, 0)), pl.BlockSpec((kdim, d), lambda i: (0, 0)), row,
                  pl.BlockSpec((1, d), lambda i: (0, 0))],
        out_specs=[row, row], out_shape=[_sds((t, d), F32), _sds((t, d), BF16)], name=name,
        compiler_params=_params(1))(mixed, w_out, x, w_norm)


def _down_loss(name, act, w_down, x1, target):
    t, f = act.shape
    d = x1.shape[1]
    tm, tn = min(1024, t), 512

    def body(a_ref, b_ref, x_ref, t_ref, dy_ref, dy16_ref, l_ref):
        diff = _dot(a_ref[...], b_ref[...], 1, 0) + x_ref[...] - t_ref[...]
        dyv = diff * (1.0 / d)
        dy_ref[...] = dyv
        dy16_ref[...] = dyv.astype(BF16)
        tot = jnp.sum(jnp.sum(diff * diff, axis=1, keepdims=True), axis=0, keepdims=True) * (0.5 / d)

        @pl.when((pl.program_id(0) == 0) & (pl.program_id(1) == 0))
        def _():
            l_ref[...] = jnp.zeros_like(l_ref)

        l_ref[...] += jnp.broadcast_to(tot, (8, 128))

    tile = pl.BlockSpec((tm, tn), lambda i, j: (i, j))
    return pl.pallas_call(
        body, grid=(t // tm, d // tn),
        in_specs=[pl.BlockSpec((tm, f), lambda i, j: (i, 0)), pl.BlockSpec((f, tn), lambda i, j: (0, j)), tile, tile],
        out_specs=[tile, tile, pl.BlockSpec((8, 128), lambda i, j: (0, 0))],
        out_shape=[_sds((t, d), F32), _sds((t, d), BF16), _sds((8, 128), F32)], name=name,
        compiler_params=_params(2))(act, w_down, x1, target)


def _peer(me, k):
    pid = (me + k) % N_DEV
    return (pid // 4, (pid // 2) % 2, pid % 2)


def _my_id():
    return 4 * lax.axis_index("x") + 2 * lax.axis_index("y") + lax.axis_index("c")


def _exchange(name, arrays, scatter, dep):
    n = len(arrays)

    def body(*refs):
        ins, outs = refs[:n], refs[n + 1:2 * n + 1]
        send_sems, recv_sems, local_sems = refs[2 * n + 1:]
        me = _my_id()
        started = []
        for a in range(n):
            src = ins[a].at[me] if scatter[a] else ins[a]
            loc = pltpu.make_async_copy(src, outs[a].at[me], local_sems.at[a])
            loc.start()
            started.append(loc)
        remote = []
        for k in range(1, N_DEV):
            to = (me + k) % N_DEV
            for a in range(n):
                src = ins[a].at[to] if scatter[a] else ins[a]
                cp = pltpu.make_async_remote_copy(src_ref=src, dst_ref=outs[a].at[me],
                                                  send_sem=send_sems.at[a * (N_DEV - 1) + k - 1], recv_sem=recv_sems.at[a * (N_DEV - 1) + k - 1],
                                                  device_id=_peer(me, k), device_id_type=pl.DeviceIdType.MESH)
                cp.start()
                remote.append(cp)
        for k in range(1, N_DEV):
            frm = (me + N_DEV - k) % N_DEV
            for a in range(n):
                src = ins[a].at[frm] if scatter[a] else ins[a]
                pltpu.make_async_remote_copy(src_ref=src, dst_ref=outs[a].at[frm],
                                             send_sem=send_sems.at[a * (N_DEV - 1) + k - 1], recv_sem=recv_sems.at[a * (N_DEV - 1) + k - 1],
                                             device_id=_peer(me, k), device_id_type=pl.DeviceIdType.MESH).wait_recv()
        for cp in remote:
            cp.wait_send()
        for loc in started:
            loc.wait()

    out_shape = [_sds((N_DEV,) + (a.shape[1:] if sc else a.shape), a.dtype) for a, sc in zip(arrays, scatter)]
    return pl.pallas_call(
        body, in_specs=[ANY] * (n + 1), out_specs=[ANY] * n, out_shape=out_shape,
        scratch_shapes=[pltpu.SemaphoreType.DMA((n * (N_DEV - 1),)), pltpu.SemaphoreType.DMA((n * (N_DEV - 1),)),
                        pltpu.SemaphoreType.DMA((n,))],
        name=name)(*arrays, dep)


def _gather_two_level(name, arrays):
    n = len(arrays)
    per = N_DEV - 1
    units = []
    for a, arr in enumerate(arrays):
        cuts = 4 if arr.shape[0] % 64 == 0 and arr.shape[0] >= 1024 else 1
        units += [(a, p * (arr.shape[0] // cuts), arr.shape[0] // cuts) for p in range(cuts)]
    nu = len(units)

    def body(*refs):
        ins, outs = refs[:n], refs[n:2 * n]
        send_sems, recv_sems, local_sems = refs[2 * n:]
        x, y, c = lax.axis_index("x"), lax.axis_index("y"), lax.axis_index("c")
        me, sibling = (x, y, c), (x, y, 1 - c)
        flip = lambda v, on: v + on - 2 * v * on
        relayed = (flip(x, c), flip(y, 1 - c), c)
        other = (flip(x, 1 - c), flip(y, c), c)
        diagonal = (1 - x, 1 - y, c)
        k_relayed, k_other = 2 - c, 1 + c

        def copy(u, k, block, to, from_input=False):
            a, r0, nr = units[u]
            slot = outs[a].at[4 * block[0] + 2 * block[1] + block[2], pl.ds(r0, nr)]
            return pltpu.make_async_remote_copy(
                src_ref=ins[a].at[pl.ds(r0, nr)] if from_input else slot, dst_ref=slot,
                send_sem=send_sems.at[u * per + k], recv_sem=recv_sems.at[u * per + k], device_id=to,
                device_id_type=pl.DeviceIdType.MESH)

        mine = [pltpu.make_async_copy(ins[a], outs[a].at[4 * x + 2 * y + c], local_sems.at[a]) for a in range(n)]
        for cp in mine:
            cp.start()
        sent = [copy(u, 1, me, (1 - x, y, c), True) for u in range(nu)]
        sent += [copy(u, 2, me, (x, 1 - y, c), True) for u in range(nu)]
        sent += [copy(u, 0, me, sibling, True) for u in range(nu)]
        for cp in sent:
            cp.start()
        for u in range(nu):
            copy(u, k_relayed, relayed, me).wait_recv()
            sent.append(copy(u, 3, relayed, other))
            sent.append(copy(u, 3 + k_relayed, relayed, sibling))
            sent[-2].start()
            sent[-1].start()
        for u in range(nu):
            copy(u, k_other, other, me).wait_recv()
            sent.append(copy(u, 3 + k_other, other, sibling))
            sent[-1].start()
        for u in range(nu):
            copy(u, 3, diagonal, me).wait_recv()
            sent.append(copy(u, 6, diagonal, sibling))
            sent[-1].start()
        for u in range(nu):
            copy(u, 0, sibling, me).wait_recv()
            for j, chip in enumerate([(1 - x, y), (x, 1 - y), (1 - x, 1 - y)]):
                copy(u, 4 + j, (*chip, 1 - c), me).wait_recv()
        for cp in sent:
            cp.wait_send()
        for cp in mine:
            cp.wait()

    return pl.pallas_call(
        body, in_specs=[ANY] * n, out_specs=[ANY] * n,
        out_shape=[_sds((N_DEV,) + a.shape, a.dtype) for a in arrays],
        scratch_shapes=[pltpu.SemaphoreType.DMA((nu * per,)), pltpu.SemaphoreType.DMA((nu * per,)),
                        pltpu.SemaphoreType.DMA((n,))],
        name=name)(*arrays)


HBM = pl.BlockSpec(memory_space=pltpu.HBM)
SEM = pl.BlockSpec(memory_space=pltpu.SEMAPHORE)
EFFECT = pltpu.SideEffectType.DATAFLOW_SIDE_EFFECTING


def _remote_copies(srcs, lands, scatter, send_sems, recv_sems, me, incoming):
    out = []
    for k in range(1, N_DEV):
        other = (me + N_DEV - k) % N_DEV if incoming else (me + k) % N_DEV
        for a in range(len(srcs)):
            sem = a * (N_DEV - 1) + k - 1
            src = srcs[a].at[other] if scatter[a] else srcs[a]
            dst = lands[a].at[other if incoming else me]
            out.append(pltpu.make_async_remote_copy(src_ref=src, dst_ref=dst, send_sem=send_sems.at[sem],
                                                    recv_sem=recv_sems.at[sem], device_id=_peer(me, k),
                                                    device_id_type=pl.DeviceIdType.MESH))
    return out


def _exchange_start(name, arrays, scatter, dep):
    n = len(arrays)
    lands = [lax.empty((N_DEV,) + (a.shape[1:] if sc else a.shape), a.dtype) for a, sc in zip(arrays, scatter)]

    def body(*refs):
        srcs, land_refs = refs[:n], refs[n:2 * n]
        send_sems, recv_sems = refs[2 * n + 1], refs[2 * n + 2]
        token = refs[-1]
        for cp in _remote_copies(srcs, land_refs, scatter, send_sems, recv_sems, _my_id(), False):
            cp.start()
        token[...] = jnp.zeros_like(token)

    n_sem = n * (N_DEV - 1)
    out_shape = ([pltpu.SemaphoreType.DMA((n_sem,)), pltpu.SemaphoreType.DMA((n_sem,))]
                 + [pltpu.HBM(a.shape, a.dtype) for a in arrays] + [pltpu.HBM(l.shape, l.dtype) for l in lands]
                 + [_sds((8, 128), F32)])
    aliases = {i: 2 + i for i in range(2 * n)}
    args = [pltpu.with_memory_space_constraint(a, pltpu.HBM) for a in list(arrays) + lands] + [dep]
    res = pl.pallas_call(
        body, name=name, in_specs=[HBM] * (2 * n) + [ANY], out_shape=out_shape,
        out_specs=[SEM, SEM] + [HBM] * (2 * n) + [pl.BlockSpec(memory_space=pltpu.VMEM)],
        input_output_aliases=aliases, compiler_params=pltpu.CompilerParams(has_side_effects=EFFECT))(*args)
    return dict(send=res[0], recv=res[1], srcs=res[2:2 + n], lands=res[2 + n:2 + 2 * n], token=res[-1],
                scatter=scatter)


def _exchange_wait(name, started, after):
    n = len(started["srcs"])
    scatter = started["scatter"]

    def body(*refs):
        srcs, land_refs = refs[:n], refs[n:2 * n]
        send_sems, recv_sems = refs[2 * n], refs[2 * n + 1]
        me = _my_id()
        for cp in _remote_copies(srcs, land_refs, scatter, send_sems, recv_sems, me, False):
            cp.wait_send()
        for cp in _remote_copies(srcs, land_refs, scatter, send_sems, recv_sems, me, True):
            cp.wait_recv()

    arrs = list(started["srcs"]) + list(started["lands"])
    res = pl.pallas_call(
        body, name=name, in_specs=[HBM] * (2 * n) + [SEM, SEM, ANY],
        out_shape=[pltpu.HBM(a.shape, a.dtype) for a in arrs], out_specs=[HBM] * (2 * n),
        input_output_aliases={i: i for i in range(2 * n)},
        compiler_params=pltpu.CompilerParams(has_side_effects=EFFECT))(*arrs, started["send"], started["recv"], after)
    me = _my_id()
    out = []
    for src, land, sc in zip(res[:n], res[n:], scatter):
        own = lax.dynamic_index_in_dim(src, me, 0, keepdims=True) if sc else src[None]
        out.append(lax.dynamic_update_slice(land, own, (me,) + (0,) * (land.ndim - 1)))
    return out


def _adamw(name, parts, w, m, v):
    r, c = w.shape
    tr, tc = r, c
    if r % 8 == 0:
        tr = next(cand for cand in (128, 88, 64, 40, 8) if r % cand == 0)
    else:
        tc = 256
    c1 = 1.0 / (1.0 - ADAM_B1 ** ADAM_STEP)
    c2 = 1.0 / (1.0 - ADAM_B2 ** ADAM_STEP)

    def body(p_ref, w_ref, m_ref, v_ref, g_ref, d_ref, nm_ref, nv_ref):
        g = p_ref[0].astype(F32)
        for s in range(1, N_DEV):
            g = g + p_ref[s].astype(F32)
        mn = ADAM_B1 * m_ref[...] + (1.0 - ADAM_B1) * g
        vn = ADAM_B2 * v_ref[...] + (1.0 - ADAM_B2) * (g * g)
        g_ref[...] = g
        nm_ref[...] = mn
        nv_ref[...] = vn
        d_ref[...] = -ADAM_LR * ((mn * c1) / (jnp.sqrt(vn * c2) + ADAM_EPS) + ADAM_WD * w_ref[...])

    blk = pl.BlockSpec((tr, tc), lambda i, j: (i, j))
    return pl.pallas_call(
        body, grid=(r // tr, c // tc),
        in_specs=[pl.BlockSpec((N_DEV, tr, tc), lambda i, j: (0, i, j)), blk, blk, blk],
        out_specs=[blk] * 4, out_shape=[_sds((r, c), F32)] * 4, name=name,
        compiler_params=_params(2, VMEM_LIMIT))(parts, w, m, v)


def _pad_rows(a, rows):
    return jnp.pad(a, ((0, rows - a.shape[0]), (0, 0)))


def _lane_row(vec8, offset):
    return jnp.pad(vec8.reshape(1, 8), ((0, 0), (offset, HD - 8 - offset)))


def kernel(x, positions, attn_norm_w, w_in, conv_w, a_log, dt_bias, delta_out_norm_w, q_norm_w, k_norm_w, attn_out_norm_w, w_out, ffn_norm_w, w_gate_up, w_down, loss_target, m_attn_norm_w, m_w_in, m_conv_w, m_a_log, m_dt_bias, m_delta_out_norm_w, m_q_norm_w, m_k_norm_w, m_attn_out_norm_w, m_w_out, m_ffn_norm_w, m_w_gate_up, m_w_down, v_attn_norm_w, v_w_in, v_conv_w, v_a_log, v_dt_bias, v_delta_out_norm_w, v_q_norm_w, v_k_norm_w, v_attn_out_norm_w, v_w_out, v_ffn_norm_w, v_w_gate_up, v_w_down):
    x2 = x[0]
    t, d = x2.shape
    target = loss_target[0]
    pos_col = positions.reshape(t, 1)
    half = HD // 2
    inv = (ROPE_THETA ** (-np.arange(half, dtype=np.float32) / half)).astype(np.float32)
    inv_row = jnp.asarray(np.concatenate([inv, inv]).reshape(1, HD))

    n_in = w_in.shape[2]
    n_gu = w_gate_up.shape[2]
    w_in_g, conv_g = _gather_two_level("gather_in", [w_in[0].astype(BF16), _pad_rows(conv_w[0], 8)])
    out_fly = _exchange_start("gather_out_start", [w_out[0].astype(BF16)], [False], conv_g)
    gu_fly = _exchange_start("gather_gate_up_start", [w_gate_up[0].astype(BF16)], [False], out_fly["token"])
    down_fly = _exchange_start("gather_down_start", [w_down[0].astype(BF16)], [False], gu_fly["token"])
    n_main = 4 * GW
    n_small = 2 * N_HEADS
    segments = [(0, n_main, 0), (n_main + n_small, N_DEV * n_in, n_main), (n_main, n_main + n_small, 7 * GW)]
    pieces = []
    for lo, hi, _ in segments:
        f = lo
        while f < hi:
            j = f // n_in
            end = min(hi, (j + 1) * n_in)
            pieces.append(w_in_g[j][:, f - j * n_in:end - j * n_in])
            f = end
    w_cat = jnp.concatenate(pieces + [jnp.zeros((d, HD - n_small), BF16)], axis=1)
    n_cat = w_cat.shape[1]
    small_blk = (7 * GW) // HD
    conv_w8 =jnp.transpose(conv_g, (1, 0, 2)).reshape(8, 3 * GW)
    alog_row = _lane_row(a_log[0], 8)
    dtb_row = _lane_row(dt_bias[0], 8)

    tm = min(2048, t)
    h1 = _rms_fwd("norm1", x2, attn_norm_w, down_fly["token"])
    tmp, tnp = min(1024, t), n_cat // 3
    proj = _mm("in_proj", h1, w_cat, grid=(t // tmp, n_cat // tnp, 1),
               a_spec=pl.BlockSpec((tmp, d), lambda i, j, k: (i, 0)),
               b_spec=pl.BlockSpec((d, tnp), lambda i, j, k: (0, j)),
               o_spec=pl.BlockSpec((tmp, tnp), lambda i, j, k: (i, j)),
               out_shape=_sds((t, n_cat), F32), ca=1, cb=0, nk=1)
    qn = _conv_fwd("conv_q", proj, conv_w8, 0, True, HD ** -0.5)
    kn = _conv_fwd("conv_k", proj, conv_w8, 1, True, 1.0)
    vv = _conv_fwd("conv_v", proj, conv_w8, 2, False, 1.0)
    beta_b, gc_b = _gates_fwd("gates", proj, small_blk, alog_row, dtb_row)
    u, w, p, tinv, qd, kd = _delta_prep("delta_prep", qn, kn, vv, beta_b, gc_b)
    oa_raw, vn, s_hist = _delta_scan("delta_scan", u, w, p, qd, kd, gc_b)

    cos_t, sin_t = _rope_tables("rope_tables", pos_col, inv_row)
    aq, ak = _qk_fwd("attn_qk", proj, 2, q_norm_w, k_norm_w, cos_t, sin_t)
    ob, lse = _attn_fwd("attn_fwd", aq, ak, proj, 6)
    mixed = _mix_fwd("mix", oa_raw, proj, 3, ob, delta_out_norm_w, attn_out_norm_w)
    (w_out_g,) = _exchange_wait("gather_out_wait", out_fly, mixed)
    w_out_full = w_out_g.reshape(2 * GW, d)
    tn = 512
    x1, h2 = _out_proj_norm("out_proj", mixed, w_out_full, x2, ffn_norm_w)
    per = N_DEV // 2
    (w_gu_g,) = _exchange_wait("gather_gate_up_wait", gu_fly, h2)
    gu3, act = _gate_up_swiglu("gate_up", h2, w_gu_g)
    (w_down_g,) = _exchange_wait("gather_down_wait", down_fly, act)
    w_down_full = w_down_g.reshape(D_FF, d)
    tmd = min(1024, t)
    dy, dy16, loss_tile = _down_loss("down_proj", act, w_down_full, x1, target)

    tk, nkt = t, 1
    g_down = _mm("g_down", act, dy16, grid=(D_FF // 1408, d // 512, nkt),
                 a_spec=pl.BlockSpec((tk, 1408), lambda i, j, k: (k, i)),
                 b_spec=pl.BlockSpec((tk, 512), lambda i, j, k: (k, j)),
                 o_spec=pl.BlockSpec((1408, 512), lambda i, j, k: (i, j)),
                 out_shape=_sds((D_FF, d), F32), ca=0, cb=0, nk=nkt)
    down_g_fly = _exchange_start("reduce_down_start", [g_down.reshape(N_DEV, D_FF // N_DEV, d)], [True], dy16)
    dgu3 = _d_gate_up("d_gate_up", dy16, w_down_full, gu3, down_g_fly["token"])
    g_gu = _mm("g_gate_up", h2, dgu3, grid=(d // 512, N_DEV, nkt),
               a_spec=pl.BlockSpec((tk, 512), lambda i, j, k: (k, i)),
               b_spec=pl.BlockSpec((None, tk, n_gu), lambda i, j, k: (j // per, k, j % per)),
               o_spec=pl.BlockSpec((None, 512, n_gu), lambda i, j, k: (j, i, 0)),
               out_shape=_sds((N_DEV, d, n_gu), F32), ca=0, cb=0, nk=nkt)
    gu_g_fly = _exchange_start("reduce_gate_up_start", [g_gu], [True], dy16)
    dh2 = _d_h2("d_h2", dgu3, w_gu_g, gu_g_fly["token"])
    dx1, dx1_16, g_ffn_norm = _rms_bwd("norm2_bwd", x1, ffn_norm_w, dh2, dy)

    g_out = _mm("g_out", mixed, dx1_16, grid=((2 * GW) // 512, 1, nkt),
                a_spec=pl.BlockSpec((tk, 512), lambda i, j, k: (k, i)),
                b_spec=pl.BlockSpec((tk, d), lambda i, j, k: (k, 0)),
                o_spec=pl.BlockSpec((512, d), lambda i, j, k: (i, 0)),
                out_shape=_sds((2 * GW, d), F32), ca=0, cb=0, nk=nkt)
    out_g_fly = _exchange_start("reduce_out_start", [g_out.reshape(N_DEV, (2 * GW) // N_DEV, d)], [True], g_ffn_norm)
    doa, dproj, dob, delta, g_dn, g_an = _mix_bwd("mix_bwd", dx1_16, w_out_full, oa_raw, proj, 3, ob,
                                                  delta_out_norm_w, attn_out_norm_w, out_g_fly["token"])
    d_aq, d_ak, d_av = _attn_bwd("attn_bwd", aq, ak, proj, 6, dob, lse, delta)
    dproj, g_qn, g_kn = _qk_bwd("attn_qk_bwd", proj, 2, q_norm_w, k_norm_w, cos_t, sin_t, d_aq, d_ak, dproj)
    dproj = _cast_into("attn_v_bwd", d_av, dproj, 6)

    dvn, dqd, dkd, dw, ddec = _delta_scan_bwd("delta_scan_bwd", doa, w, p, qd, kd, gc_b, vn, s_hist)
    dqn, dkn, dvv, dbeta_b, dg_b = _delta_prep_bwd("delta_prep_bwd", qn, kn, vv, beta_b, gc_b, tinv, u, w, vn,
                                                   doa, dvn, dqd, dkd, dw, ddec)
    dproj, gcw_q = _conv_bwd("conv_q_bwd", proj, conv_w8, dqn, dproj, 0, True, HD ** -0.5)
    dproj, gcw_k = _conv_bwd("conv_k_bwd", proj, conv_w8, dkn, dproj, 1, True, 1.0)
    dproj, gcw_v = _conv_bwd("conv_v_bwd", proj, conv_w8, dvv, dproj, 2, False, 1.0)
    dproj, g_alog_row, g_dtb_row = _gates_bwd("gates_bwd", proj, small_blk, alog_row, dtb_row, dbeta_b, dg_b, dproj)
    tmc = 384
    g_cat = _mm("g_in", dproj, h1, grid=(n_cat // tmc, 1, nkt),
                a_spec=pl.BlockSpec((tk, tmc), lambda i, j, k: (k, i)),
                b_spec=pl.BlockSpec((tk, d), lambda i, j, k: (k, 0)),
                o_spec=pl.BlockSpec((tmc, d), lambda i, j, k: (i, 0)),
                out_shape=_sds((n_cat, d), BF16), ca=0, cb=0, nk=nkt)
    parts = []
    for j in range(N_DEV):
        cols = []
        for lo, hi, start in sorted(segments):
            a, b = max(lo, j * n_in), min(hi, (j + 1) * n_in)
            if a < b:
                cols.append(g_cat[start + a - lo:start + b - lo])
        parts.append(cols[0] if len(cols) == 1 else jnp.concatenate(cols, axis=0))
    g_in_parts = jnp.stack(parts)
    g_conv = jnp.concatenate([gcw_q, gcw_k, gcw_v], axis=1)
    n_cw = conv_w.shape[2]
    g_conv_parts = jnp.transpose(g_conv.reshape(8, N_DEV, n_cw), (1, 0, 2))
    in_g_fly = _exchange_start("reduce_in_start", [g_in_parts, g_conv_parts], [True] * 2, g_dtb_row)
    tmh1 = min(512, t)
    dh1 = _mm("d_h1", dproj, w_cat, dep=in_g_fly["token"], grid=(t // tmh1, d // 1024, 1),
              a_spec=pl.BlockSpec((tmh1, n_cat), lambda i, j, k: (i, 0)),
              b_spec=pl.BlockSpec((1024, n_cat), lambda i, j, k: (j, 0)),
              o_spec=pl.BlockSpec((tmh1, 1024), lambda i, j, k: (i, j)),
              out_shape=_sds((t, d), F32), ca=1, cb=1, nk=1)
    grad_x, _, g_attn_norm = _rms_bwd("norm1_bwd", x2, attn_norm_w, dh1, dx1)

    small_rows = [g_attn_norm.reshape(d // HD, HD), g_ffn_norm.reshape(d // HD, HD), g_dn, g_qn, g_kn, g_an,
                  g_alog_row, g_dtb_row, loss_tile[:1]]
    loss_row = sum(r.shape[0] for r in small_rows) - 1
    small_pack = _pad_rows(jnp.concatenate(small_rows, axis=0), 40)
    (r_down,) = _exchange_wait("reduce_down_wait", down_g_fly, grad_x)
    (r_gu,) = _exchange_wait("reduce_gate_up_wait", gu_g_fly, grad_x)
    (r_out,) = _exchange_wait("reduce_out_wait", out_g_fly, grad_x)
    res_gu = [a[None] for a in _adamw("adamw_gate_up", r_gu, w_gate_up[0], m_w_gate_up[0], v_w_gate_up[0])]
    res_down = [a[None] for a in _adamw("adamw_down", r_down, w_down[0], m_w_down[0], v_w_down[0])]
    res_out = [a[None] for a in _adamw("adamw_out", r_out, w_out[0], m_w_out[0], v_w_out[0])]
    done = (res_gu[3][0, :1, :1] + res_down[3][0, :1, :1] + res_out[3][0, :1, :1])
    r_in, r_conv = _exchange_wait("reduce_in_wait", in_g_fly, done)
    upd_in = _adamw("adamw_in", r_in, jnp.transpose(w_in[0]), jnp.transpose(m_w_in[0]), jnp.transpose(v_w_in[0]))
    res_in = [jnp.transpose(a)[None] for a in upd_in]
    (r_small,) = _exchange("gather_small_grads", [small_pack], [False], upd_in[0])

    def pack_small(an, fn, dn, qn_, kn_, aon, al, db):
        rows = [an.reshape(d // HD, HD), fn.reshape(d // HD, HD), dn, qn_, kn_, aon,
                _lane_row(al[0], 8), _lane_row(db[0], 8)]
        return _pad_rows(jnp.concatenate(rows, axis=0), 40)

    def unpack_small(pk):
        nr = d // HD
        return dict(attn_norm_w=pk[:nr].reshape(1, d), ffn_norm_w=pk[nr:2 * nr].reshape(1, d),
                    delta_out_norm_w=pk[2 * nr:2 * nr + 1], q_norm_w=pk[2 * nr + 1:2 * nr + 2],
                    k_norm_w=pk[2 * nr + 2:2 * nr + 3], attn_out_norm_w=pk[2 * nr + 3:2 * nr + 4],
                    a_log=pk[2 * nr + 4:2 * nr + 5, 8:16], dt_bias=pk[2 * nr + 5:2 * nr + 6, 8:16])

    res_small = _adamw("adamw_small", r_small,
                       pack_small(attn_norm_w, ffn_norm_w, delta_out_norm_w, q_norm_w, k_norm_w, attn_out_norm_w, a_log, dt_bias),
                       pack_small(m_attn_norm_w, m_ffn_norm_w, m_delta_out_norm_w, m_q_norm_w, m_k_norm_w, m_attn_out_norm_w, m_a_log, m_dt_bias),
                       pack_small(v_attn_norm_w, v_ffn_norm_w, v_delta_out_norm_w, v_q_norm_w, v_k_norm_w, v_attn_out_norm_w, v_a_log, v_dt_bias))
    small = [unpack_small(a) for a in res_small]
    res_conv =[a[None, :4] for a in _adamw("adamw_conv", r_conv, _pad_rows(conv_w[0], 8), _pad_rows(m_conv_w[0], 8),
                                            _pad_rows(v_conv_w[0], 8))]

    loss = jnp.sum(r_small[:, loss_row, 0])
    outs = [loss, grad_x[None]]
    for i in range(4):
        s = small[i]
        outs += [s["attn_norm_w"], res_in[i], res_conv[i], s["a_log"], s["dt_bias"], s["delta_out_norm_w"],
                 s["q_norm_w"], s["k_norm_w"], s["attn_out_norm_w"], res_out[i], s["ffn_norm_w"], res_gu[i],
                 res_down[i]]
    return tuple(outs)
```

```python
import numpy as np
import jax
import jax.numpy as jnp
from jax import lax
from jax.experimental import pallas as pl
from jax.experimental.pallas import tpu as pltpu

F32 = jnp.float32
BF16 = jnp.bfloat16

N_DEV = 8
N_HEADS = 8
HD = 128
GW = N_HEADS * HD
CHUNK = 64
PAIR = 2 * CHUNK
SCAN_CHUNKS = 4
SCAN_ROWS = SCAN_CHUNKS * CHUNK
SPAN = 128
DILATIONS = (1, 4, 16)
ROPE_THETA = 10000.0
EPS = 1e-6
D_FF = 5632
ADAM_LR, ADAM_B1, ADAM_B2, ADAM_EPS, ADAM_WD, ADAM_STEP = 0.001, 0.9, 0.999, 1e-8, 0.01, 10
NEG = -1e30
VMEM_LIMIT = 56 * 1024 * 1024
ANY = pl.BlockSpec(memory_space=pl.ANY)
HEADS_PER_STEP = 8


def _params(n_grid, vmem=VMEM_LIMIT):
    return pltpu.CompilerParams(dimension_semantics=("arbitrary",) * n_grid, vmem_limit_bytes=vmem)


def _sds(shape, dtype):
    return jax.ShapeDtypeStruct(tuple(shape), dtype)


def _sigmoid(x):
    return 1.0 / (1.0 + jnp.exp(-x))


def _silu(x):
    return x * _sigmoid(x)


def _softplus(x):
    return jnp.maximum(x, 0.0) + jnp.log(1.0 + jnp.exp(-jnp.abs(x)))


def _dot(a, b, ca, cb):
    return lax.dot_general(a, b, (((ca,), (cb,)), ((), ())), preferred_element_type=F32)


def _b16(x):
    return x if x.dtype == BF16 else x.astype(BF16)


def _split(x):
    hi = x.astype(BF16)
    return hi, (x - hi.astype(F32)).astype(BF16)


def _dot3(a, b, ca, cb):
    a_hi, a_lo = _split(a)
    b_hi, b_lo = _split(b)
    return _dot(a_hi, b_hi, ca, cb) + (_dot(a_hi, b_lo, ca, cb) + _dot(a_lo, b_hi, ca, cb))


def _iota2(shape, axis):
    return lax.broadcasted_iota(jnp.int32, shape, axis)


def _mm(name, a, b, *, grid, a_spec, b_spec, o_spec, out_shape, ca, cb, nk, dep=None):
    assert nk == 1 and grid[2] == 1

    def body(*refs):
        refs[-1][...] = _dot(_b16(refs[0][...]), _b16(refs[1][...]), ca, cb).astype(refs[-1].dtype)

    in_specs = [a_spec, b_spec] + ([ANY] if dep is not None else [])
    args = (a, b) + ((dep,) if dep is not None else ())
    return pl.pallas_call(body, grid=grid, in_specs=in_specs, out_specs=o_spec, out_shape=out_shape,
                          name=name, compiler_params=_params(3))(*args)


def _rms_f(xv, wv):
    return xv * lax.rsqrt(jnp.mean(xv * xv, axis=-1, keepdims=True) + EPS) * wv


def _rms_fwd(name, x, w, dep):
    t, d = x.shape
    tm = min(512, t)

    def body(x_ref, w_ref, dep_ref, o_ref):
        o_ref[...] = _rms_f(x_ref[...], w_ref[...]).astype(BF16)

    row = pl.BlockSpec((tm, d), lambda i: (i, 0))
    vec = pl.BlockSpec((1, d), lambda i: (0, 0))
    return pl.pallas_call(body, grid=(t // tm,), in_specs=[row, vec, ANY], out_specs=row,
                          out_shape=_sds((t, d), BF16), name=name, compiler_params=_params(1))(x, w, dep)


def _rms_bwd(name, x, w, dh, res):
    t, d = x.shape
    tm = min(256, t)

    def body(x_ref, w_ref, dh_ref, res_ref, dx_ref, dx16_ref, dw_ref):
        _, vjp = jax.vjp(_rms_f, x_ref[...], w_ref[...])
        dxv, dwv = vjp(dh_ref[...])
        dxv = dxv + res_ref[...]
        dx_ref[...] = dxv
        dx16_ref[...] = dxv.astype(BF16)

        @pl.when(pl.program_id(0) == 0)
        def _():
            dw_ref[...] = jnp.zeros_like(dw_ref)

        dw_ref[...] += dwv

    row = pl.BlockSpec((tm, d), lambda i: (i, 0))
    vec = pl.BlockSpec((1, d), lambda i: (0, 0))
    return pl.pallas_call(body, grid=(t // tm,), in_specs=[row, vec, row, row], out_specs=[row, row, vec],
                          out_shape=[_sds((t, d), F32), _sds((t, d), BF16), _sds((1, d), F32)], name=name,
                          compiler_params=_params(1))(x, w, dh, res)


def _shift_rows(x, s):
    t = x.shape[0]
    r = pltpu.roll(x, s % t, 0)
    row8 = _iota2((8, x.shape[1]), 0)
    if s > 0:
        return jnp.concatenate([jnp.where(row8 >= s, r[:8], 0.0), r[8:]], axis=0)
    return jnp.concatenate([r[:t - 8], jnp.where(row8 < 8 + s, r[t - 8:], 0.0)], axis=0)


def _conv_taps(xv, w_ref):
    c = w_ref[3:4, :] * xv
    for s in (1, 2, 3):
        c = c + w_ref[3 - s:4 - s, :] * _shift_rows(xv, s)
    return c


def _post_conv(c, l2, scale):
    y = _silu(c)
    if l2:
        y = y * lax.rsqrt(jnp.sum(y * y, axis=-1, keepdims=True) + EPS) * scale
    return y


def _conv_fwd(name, proj, conv_w8, group, l2, scale):
    t = proj.shape[0]

    def body(x_ref, w_ref, o_ref):
        o_ref[...] = _post_conv(_conv_taps(x_ref[...], w_ref), l2, scale)

    return pl.pallas_call(
        body, grid=(N_HEADS,),
        in_specs=[pl.BlockSpec((t, HD), lambda h: (0, h + group * N_HEADS)),
                  pl.BlockSpec((8, HD), lambda h: (0, h + group * N_HEADS))],
        out_specs=pl.BlockSpec((t, HD), lambda h: (0, h)),
        out_shape=_sds((t, GW), F32), name=name, compiler_params=_params(1, VMEM_LIMIT))(proj, conv_w8)


def _conv_bwd(name, proj, conv_w8, dn, dproj, group, l2, scale):
    t = proj.shape[0]

    def body(x_ref, w_ref, dn_ref, dproj_ref, dx_ref, dw_ref):
        xv = x_ref[...]
        c = _conv_taps(xv, w_ref)
        _, vjp = jax.vjp(lambda cc: _post_conv(cc, l2, scale), c)
        (dc,) = vjp(dn_ref[...])
        dx = w_ref[3:4, :] * dc
        dw = jnp.zeros((8, HD), F32)
        rid = _iota2((8, HD), 0)
        dw = dw + jnp.where(rid == 3, jnp.sum(dc * xv, axis=0, keepdims=True), 0.0)
        for s in (1, 2, 3):
            dx = dx + w_ref[3 - s:4 - s, :] * _shift_rows(dc, -s)
            dw = dw + jnp.where(rid == 3 - s, jnp.sum(dc * _shift_rows(xv, s), axis=0, keepdims=True), 0.0)
        dx_ref[...] = dx.astype(BF16)
        dw_ref[...] = dw

    return pl.pallas_call(
        body, grid=(N_HEADS,),
        in_specs=[pl.BlockSpec((t, HD), lambda h: (0, h + group * N_HEADS)),
                  pl.BlockSpec((8, HD), lambda h: (0, h + group * N_HEADS)),
                  pl.BlockSpec((t, HD), lambda h: (0, h)), ANY],
        out_specs=[pl.BlockSpec((t, HD), lambda h: (0, h + group * N_HEADS)), pl.BlockSpec((8, HD), lambda h: (0, h))],
        out_shape=[_sds(dproj.shape, BF16), _sds((8, GW), F32)], input_output_aliases={3: 0}, name=name,
        compiler_params=_params(1, VMEM_LIMIT))(proj, conv_w8, dn, dproj)


def _chunk_cumsum(g, rows):
    pos = rows % CHUNK
    s = 1
    while s < CHUNK:
        g = g + jnp.where(pos >= s, pltpu.roll(g, s, 0), 0.0)
        s *= 2
    return g


def _gates_fwd(name, proj, small_blk, alog_row, dtb_row):
    t = proj.shape[0]
    tm = min(256, t)

    def body(s_ref, a_ref, b_ref, beta_ref, gc_ref):
        sm = s_ref[...]
        beta = _sigmoid(sm)
        g = -jnp.exp(a_ref[...]) * _softplus(sm + b_ref[...])
        gc = _chunk_cumsum(g, _iota2((tm, HD), 0))
        lane = _iota2((tm, HD), 1)
        for h in range(N_HEADS):
            bcol = jnp.sum(jnp.where(lane == h, beta, 0.0), axis=1, keepdims=True)
            gcol = jnp.sum(jnp.where(lane == 8 + h, gc, 0.0), axis=1, keepdims=True)
            beta_ref[:, h * HD:(h + 1) * HD] = jnp.broadcast_to(bcol, (tm, HD))
            gc_ref[:, h * HD:(h + 1) * HD] = jnp.broadcast_to(gcol, (tm, HD))

    vec = pl.BlockSpec((1, HD), lambda i: (0, 0))
    wide = pl.BlockSpec((tm, GW), lambda i: (i, 0))
    return pl.pallas_call(
        body, grid=(t // tm,),
        in_specs=[pl.BlockSpec((tm, HD), lambda i: (i, small_blk)), vec, vec], out_specs=[wide, wide],
        out_shape=[_sds((t, GW), F32), _sds((t, GW), F32)], name=name,
        compiler_params=_params(1))(proj, alog_row, dtb_row)


def _gates_bwd(name, proj, small_blk, alog_row, dtb_row, dbeta_b, dg_b, dproj):
    t = proj.shape[0]
    tm = min(256, t)

    def body(s_ref, a_ref, b_ref, db_ref, dg_ref, dproj_ref, ds_ref, da_ref, dbias_ref):
        sm = s_ref[...]
        lane = _iota2((tm, HD), 1)
        db = jnp.zeros((tm, HD), F32)
        dg = jnp.zeros((tm, HD), F32)
        for h in range(N_HEADS):
            db = db + jnp.where(lane == h, db_ref[:, h * HD:(h + 1) * HD], 0.0)
            dg = dg + jnp.where(lane == 8 + h, dg_ref[:, h * HD:(h + 1) * HD], 0.0)
        beta = _sigmoid(sm)
        ea = jnp.exp(a_ref[...])
        pre = sm + b_ref[...]
        g = -ea * _softplus(pre)
        dpre = dg * (-ea) * _sigmoid(pre)
        ds_ref[...] = (db * beta * (1.0 - beta) + dpre).astype(BF16)

        @pl.when(pl.program_id(0) == 0)
        def _():
            da_ref[...] = jnp.zeros_like(da_ref)
            dbias_ref[...] = jnp.zeros_like(dbias_ref)

        da_ref[...] += jnp.sum(dg * g, axis=0, keepdims=True)
        dbias_ref[...] += jnp.sum(dpre, axis=0, keepdims=True)

    vec = pl.BlockSpec((1, HD), lambda i: (0, 0))
    wide = pl.BlockSpec((tm, GW), lambda i: (i, 0))
    return pl.pallas_call(
        body, grid=(t // tm,),
        in_specs=[pl.BlockSpec((tm, HD), lambda i: (i, small_blk)), vec, vec, wide, wide, ANY],
        out_specs=[pl.BlockSpec((tm, HD), lambda i: (i, small_blk)), vec, vec],
        out_shape=[_sds(dproj.shape, BF16), _sds((1, HD), F32), _sds((1, HD), F32)],
        input_output_aliases={5: 0}, name=name,
        compiler_params=_params(1))(proj, alog_row, dtb_row, dbeta_b, dg_b, dproj)


def _pair_masks():
    ii = _iota2((PAIR, PAIR), 0)
    jj = _iota2((PAIR, PAIR), 1)
    same = (ii // CHUNK) == (jj // CHUNK)
    return ii, jj, same & (ii >= jj), same & (ii > jj)


def _to_row(col_b, ii, jj):
    return jnp.sum(jnp.where(ii == jj, col_b, 0.0), axis=0, keepdims=True)


def _to_col(row, ii, jj):
    return jnp.sum(jnp.where(ii == jj, jnp.broadcast_to(row, (PAIR, PAIR)), 0.0), axis=1, keepdims=True)


def _decay_parts(gc, last_a, last_b, ii, jj, causal):
    diff = gc - _to_row(gc, ii, jj)
    dmat = jnp.where(causal, jnp.exp(jnp.where(causal, diff, 0.0)), 0.0)
    glast = jnp.where(ii < CHUNK, last_a, last_b)
    return dmat, jnp.exp(gc), jnp.exp(glast - gc)


def _unit_lower_inverse(lows, ii, jj):
    eye = jnp.where(ii == jj, 1.0, 0.0)
    mm = lambda xs, ys: [_dot3(a, b, 1, 0) for a, b in zip(xs, ys)]
    plus = lambda xs: [eye + a for a in xs]
    minus = lambda xs: [eye - a for a in xs]
    d1 = [jnp.where((ii // 16) == (jj // 16), low, 0.0) for low in lows]
    d2 = mm(d1, d1)
    a = mm(minus(d1), plus(d2))
    d4 = mm(d2, d2)
    a = mm(a, plus(d4))
    d8 = mm(d4, d4)
    td = mm(a, plus(d8))
    n1 = mm(td, [low - d for low, d in zip(lows, d1)])
    n2 = mm(n1, n1)
    return mm(mm(minus(n1), plus(n2)), td)


def _delta_prep(name, qn, kn, vv, beta_b, gc_b):
    t = qn.shape[0]

    def body(q_ref, k_ref, v_ref, b_ref, g_ref, u_ref, w_ref, p_ref, t_ref, qd_ref, kd_ref):
        ii, jj, causal, strict = _pair_masks()
        sls = [slice(hh * HD, (hh + 1) * HD) for hh in range(HEADS_PER_STEP)]
        lows = []
        for sl in sls:
            q, k, beta = q_ref[:, sl], k_ref[:, sl], b_ref[:, sl]
            dmat, gam, e2 = _decay_parts(g_ref[:, sl], g_ref[CHUNK - 1:CHUNK, sl], g_ref[PAIR - 1:PAIR, sl],
                                         ii, jj, causal)
            k16 = _b16(k)
            lows.append(jnp.where(strict, beta * _dot(k16, k16, 1, 1) * dmat, 0.0))
            p_ref[:, sl] = jnp.where(causal, _dot(_b16(q), k16, 1, 1) * dmat, 0.0).astype(BF16)
            qd_ref[:, sl] = (q * gam).astype(BF16)
            kd_ref[:, sl] = (k * e2).astype(BF16)
        for sl, tinv in zip(sls, _unit_lower_inverse(lows, ii, jj)):
            beta = b_ref[:, sl]
            t_ref[:, sl] = tinv
            u_ref[:, sl] = _dot3(tinv, v_ref[:, sl] * beta, 1, 0)
            w_ref[:, sl] = _dot3(tinv, k_ref[:, sl] * (beta * jnp.exp(g_ref[:, sl])), 1, 0).astype(BF16)

    blk = pl.BlockSpec((PAIR, HEADS_PER_STEP * HD), lambda i, h: (i, h))
    return pl.pallas_call(
        body, grid=(t // PAIR, N_HEADS // HEADS_PER_STEP), in_specs=[blk] * 5, out_specs=[blk] * 6,
        out_shape=[_sds((t, GW), F32), _sds((t, GW), BF16), _sds((t, GW), BF16), _sds((t, GW), F32),
                   _sds((t, GW), BF16), _sds((t, GW), BF16)],
        name=name, compiler_params=_params(2))(qn, kn, vv, beta_b, gc_b)


def _delta_scan(name, u, w, p, qd, kd, gc_b):
    t = u.shape[0]
    n = t // CHUNK

    def body(u_ref, w_ref, p_ref, qd_ref, kd_ref, g_ref, o_ref, vn_ref, sh_ref, state):
        @pl.when(pl.program_id(0) == 0)
        def _():
            state[...] = jnp.zeros_like(state)

        sls = [slice(h * HD, (h + 1) * HD) for h in range(N_HEADS)]
        heads = range(N_HEADS)
        s = [state[h] for h in heads]
        for c in range(SCAN_CHUNKS):
            rows = slice(c * CHUNK, (c + 1) * CHUNK)
            last = slice((c + 1) * CHUNK - 1, (c + 1) * CHUNK)
            for h in heads:
                sh_ref[c, h] = s[h].astype(BF16)
            s16 = [_b16(a) for a in s]
            ws = [_dot(w_ref[rows, sls[h]], s16[h], 1, 0) for h in heads]
            qs = [_dot(qd_ref[rows, sls[h]], s16[h], 1, 0) for h in heads]
            vn16 = [_b16(u_ref[rows, sls[h]] - ws[h]) for h in heads]
            pv = [_dot(p_ref[rows, sls[h]], jnp.concatenate([vn16[h], vn16[h]], axis=0), 1, 0) for h in heads]
            kv = [_dot(kd_ref[rows, sls[h]], vn16[h], 0, 0) for h in heads]
            for h in heads:
                o_ref[rows, sls[h]] = qs[h] + pv[h]
                vn_ref[rows, sls[h]] = vn16[h]
            s = [s[h] * jnp.exp(g_ref[last, sls[h]]) + kv[h] for h in heads]
        for h in heads:
            state[h] = s[h]

    blk = pl.BlockSpec((SCAN_ROWS, GW), lambda i: (i, 0))
    return pl.pallas_call(
        body, grid=(t // SCAN_ROWS,), in_specs=[blk] * 6,
        out_specs=[blk, blk, pl.BlockSpec((SCAN_CHUNKS, N_HEADS, HD, HD), lambda i: (i, 0, 0, 0))],
        out_shape=[_sds((t, GW), F32), _sds((t, GW), BF16), _sds((n, N_HEADS, HD, HD), BF16)],
        scratch_shapes=[pltpu.VMEM((N_HEADS, HD, HD), F32)], name=name,
        compiler_params=_params(1))(u, w, p, qd, kd, gc_b)


def _delta_scan_bwd(name, do, w, p, qd, kd, gc_b, vn, s_hist):
    t = do.shape[0]
    n = t // CHUNK

    def body(do_ref, w_ref, p_ref, qd_ref, kd_ref, g_ref, vn_ref, sh_ref,
             dvn_ref, dqd_ref, dkd_ref, dw_ref, ddec_ref, dstate):
        @pl.when(pl.program_id(0) == 0)
        def _():
            dstate[...] = jnp.zeros_like(dstate)

        sls = [slice(h * HD, (h + 1) * HD) for h in range(N_HEADS)]
        heads = range(N_HEADS)
        ds = [dstate[h] for h in heads]
        for c in reversed(range(SCAN_CHUNKS)):
            rows = slice(c * CHUNK, (c + 1) * CHUNK)
            last = slice((c + 1) * CHUNK - 1, (c + 1) * CHUNK)
            ds16 = [_b16(a) for a in ds]
            s16 = [_b16(sh_ref[c, h]) for h in heads]
            do16 = [_b16(do_ref[rows, sls[h]]) for h in heads]
            ptdo = [_dot(p_ref[rows, sls[h]], do16[h], 0, 0) for h in heads]
            kds = [_dot(kd_ref[rows, sls[h]], ds16[h], 1, 0) for h in heads]
            qdo = [_dot(qd_ref[rows, sls[h]], do16[h], 0, 0) for h in heads]
            for h in heads:
                dqd_ref[rows, sls[h]] = _dot(do16[h], s16[h], 1, 1)
                dkd_ref[rows, sls[h]] = _dot(vn_ref[rows, sls[h]], ds16[h], 1, 1)
            dvn = [ptdo[h][:CHUNK, :] + ptdo[h][CHUNK:, :] + kds[h] for h in heads]
            dvn16 = [_b16(a) for a in dvn]
            wdv = [_dot(w_ref[rows, sls[h]], dvn16[h], 0, 0) for h in heads]
            for h in heads:
                dvn_ref[rows, sls[h]] = dvn[h]
                dw_ref[rows, sls[h]] = -_dot(dvn16[h], s16[h], 1, 1)
                tot = jnp.sum(jnp.sum(sh_ref[c, h].astype(F32) * ds[h], axis=1, keepdims=True), axis=0, keepdims=True)
                ddec_ref[c * 8:(c + 1) * 8, sls[h]] = jnp.broadcast_to(tot, (8, HD))
            ds = [ds[h] * jnp.exp(g_ref[last, sls[h]]) + qdo[h] - wdv[h] for h in heads]
        for h in heads:
            dstate[h] = ds[h]

    npair = t // SCAN_ROWS
    blk = pl.BlockSpec((SCAN_ROWS, GW), lambda i: (npair - 1 - i, 0))
    return pl.pallas_call(
        body, grid=(npair,),
        in_specs=[blk] * 7 + [pl.BlockSpec((SCAN_CHUNKS, N_HEADS, HD, HD), lambda i: (npair - 1 - i, 0, 0, 0))],
        out_specs=[blk] * 4 + [pl.BlockSpec((8 * SCAN_CHUNKS, GW), lambda i: (npair - 1 - i, 0))],
        out_shape=[_sds((t, GW), F32)] * 4 + [_sds((n * 8, GW), F32)],
        scratch_shapes=[pltpu.VMEM((N_HEADS, HD, HD), F32)], name=name,
        compiler_params=_params(1))(do, w, p, qd, kd, gc_b, vn, s_hist)


def _delta_prep_bwd(name, qn, kn, vv, beta_b, gc_b, tinv, u, w, vn, do, dvn, dqd, dkd, dw, ddec):
    t = qn.shape[0]

    def body(q_ref, k_ref, v_ref, b_ref, g_ref, t_ref, u_ref, w_ref, vn_ref, do_ref, dvn_ref, dqd_ref,
             dkd_ref, dw_ref, ddec_ref, dq_ref, dk_ref, dv_ref, dbeta_ref, dg_ref):
        ii, jj, causal, strict = _pair_masks()
        suffix = ((ii // CHUNK) == (jj // CHUNK)) & (jj >= ii)
        first = ii < CHUNK
        rs = lambda a: jnp.sum(a, axis=1, keepdims=True)
        sls = [slice(hh * HD, (hh + 1) * HD) for hh in range(HEADS_PER_STEP)]
        xs = [_dot3(t_ref[:, sl], dvn_ref[:, sl], 0, 0) for sl in sls]
        ys = [_dot3(t_ref[:, sl], dw_ref[:, sl], 0, 0) for sl in sls]
        k16s = [_b16(k_ref[:, sl]) for sl in sls]
        kks = [_dot(k16, k16, 1, 1) for k16 in k16s]
        qks = [_dot(_b16(q_ref[:, sl]), k16, 1, 1) for sl, k16 in zip(sls, k16s)]
        dps = [jnp.where(causal, _dot(_b16(do_ref[:, sl]), vn_ref[:, sl], 1, 1), 0.0) for sl in sls]
        das = [-jnp.where(strict, _dot(_b16(x), _b16(u_ref[:, sl]), 1, 1) + _dot(_b16(y), w_ref[:, sl], 1, 1), 0.0)
               for sl, x, y in zip(sls, xs, ys)]
        for hh, sl in enumerate(sls):
            q, k, v, beta, gc = q_ref[:, sl], k_ref[:, sl], v_ref[:, sl], b_ref[:, sl], g_ref[:, sl]
            last_a, last_b = g_ref[CHUNK - 1:CHUNK, sl], g_ref[PAIR - 1:PAIR, sl]
            dmat, gam, e2 = _decay_parts(gc, last_a, last_b, ii, jj, causal)
            q16, k16 = _b16(q), k16s[hh]
            kk, qk, dp, x, y, da = kks[hh], qks[hh], dps[hh], xs[hh], ys[hh], das[hh]
            dqd, dkd = dqd_ref[:, sl], dkd_ref[:, sl]
            dpd16 = _b16(dp * dmat)
            dkk16 = _b16(da * beta * dmat)
            dq_ref[:, sl] = gam * dqd + _dot(dpd16, k16, 1, 0)
            dk_ref[:, sl] = (e2 * dkd + _dot(dpd16, q16, 0, 0) + beta * gam * y
                             + _dot(dkk16, k16, 1, 0) + _dot(dkk16, k16, 0, 0))
            dv_ref[:, sl] = beta * x
            dbeta = rs(v * x) + rs(k * gam * y) + rs(da * kk * dmat)
            dbeta_ref[:, sl] = jnp.broadcast_to(dbeta, (PAIR, HD))
            m = (dp * qk + da * beta * kk) * dmat
            dgam = rs(q * dqd) + rs(k * beta * y)
            de2 = rs(k * dkd)
            colsum = _to_col(jnp.sum(m, axis=0, keepdims=True), ii, jj)
            te2 = de2 * e2
            dgc = rs(m) - colsum + gam * dgam - te2
            tail_a = jnp.sum(jnp.where(first, te2, 0.0), axis=0, keepdims=True)
            tail_b = jnp.sum(jnp.where(first, 0.0, te2), axis=0, keepdims=True)
            dgc = dgc + jnp.where(ii == CHUNK - 1, tail_a + ddec_ref[0:1, sl] * jnp.exp(last_a), 0.0)
            dgc = dgc + jnp.where(ii == PAIR - 1, tail_b + ddec_ref[8:9, sl] * jnp.exp(last_b), 0.0)
            dgc_row = _to_row(dgc, ii, jj)
            dg = jnp.sum(jnp.where(suffix, jnp.broadcast_to(dgc_row, (PAIR, PAIR)), 0.0), axis=1, keepdims=True)
            dg_ref[:, sl] = jnp.broadcast_to(dg, (PAIR, HD))

    blk = pl.BlockSpec((PAIR, HEADS_PER_STEP * HD), lambda i, h: (i, h))
    return pl.pallas_call(
        body, grid=(t // PAIR, N_HEADS // HEADS_PER_STEP),
        in_specs=[blk] * 14 + [pl.BlockSpec((16, HEADS_PER_STEP * HD), lambda i, h: (i, h))], out_specs=[blk] * 5,
        out_shape=[_sds((t, GW), F32)] * 5, name=name,
        compiler_params=_params(2))(qn, kn, vv, beta_b, gc_b, tinv, u, w, vn, do, dvn, dqd, dkd, dw, ddec)


def _rope_tables(name, pos_col, inv_row):
    t = pos_col.shape[0]
    tm = min(1024, t)

    def body(pos_ref, inv_ref, cos_ref, sin_ref):
        ang = pos_ref[...].astype(F32) * inv_ref[...]
        lane = _iota2(ang.shape, 1)
        cos_ref[...] = jnp.cos(ang)
        sin_ref[...] = jnp.where(lane < HD // 2, -1.0, 1.0) * jnp.sin(ang)

    tab = pl.BlockSpec((tm, HD), lambda i: (i, 0))
    return pl.pallas_call(
        body, grid=(t // tm,), in_specs=[pl.BlockSpec((tm, 1), lambda i: (i, 0)), pl.BlockSpec((1, HD), lambda i: (0, 0))],
        out_specs=[tab, tab], out_shape=[_sds((t, HD), F32)] * 2, name=name,
        compiler_params=_params(1))(pos_col, inv_row)


def _head_rms(xh, wv):
    return xh * lax.rsqrt(jnp.mean(xh * xh, axis=-1, keepdims=True) + EPS) * wv


def _qk_fwd(name, proj, pair_blk, wq_row, wk_row, cos_t, sin_t):
    t = proj.shape[0]
    tm = min(256, t)

    def body(x_ref, wq_ref, wk_ref, cos_ref, sin_ref, q_ref, k_ref):
        cos, sin = cos_ref[...], sin_ref[...]
        for o_ref, w_ref, base in ((q_ref, wq_ref, 0), (k_ref, wk_ref, GW)):
            for h in range(N_HEADS):
                y = _head_rms(x_ref[:, base + h * HD:base + (h + 1) * HD], w_ref[...])
                o_ref[:, h * HD:(h + 1) * HD] = y * cos + pltpu.roll(y, HD // 2, 1) * sin

    vec = pl.BlockSpec((1, HD), lambda i: (0, 0))
    tab = pl.BlockSpec((tm, HD), lambda i: (i, 0))
    wide = pl.BlockSpec((tm, GW), lambda i: (i, 0))
    return pl.pallas_call(
        body, grid=(t // tm,),
        in_specs=[pl.BlockSpec((tm, 2 * GW), lambda i: (i, pair_blk)), vec, vec, tab, tab],
        out_specs=[wide, wide], out_shape=[_sds((t, GW), F32)] * 2, name=name,
        compiler_params=_params(1))(proj, wq_row, wk_row, cos_t, sin_t)


def _qk_bwd(name, proj, pair_blk, wq_row, wk_row, cos_t, sin_t, dq_full, dk_full, dproj):
    t = proj.shape[0]
    tm = min(256, t)

    def body(x_ref, wq_ref, wk_ref, cos_ref, sin_ref, dq_ref, dk_ref, dproj_ref, dx_ref, dwq_ref, dwk_ref):
        cos, sin = cos_ref[...], sin_ref[...]

        @pl.when(pl.program_id(0) == 0)
        def _():
            dwq_ref[...] = jnp.zeros_like(dwq_ref)
            dwk_ref[...] = jnp.zeros_like(dwk_ref)

        for dy_ref, w_ref, dw_ref, base in ((dq_ref, wq_ref, dwq_ref, 0), (dk_ref, wk_ref, dwk_ref, GW)):
            dw = jnp.zeros((1, HD), F32)
            for h in range(N_HEADS):
                dy = dy_ref[:, h * HD:(h + 1) * HD]
                dy = dy * cos - pltpu.roll(dy, HD // 2, 1) * sin
                _, vjp = jax.vjp(_head_rms, x_ref[:, base + h * HD:base + (h + 1) * HD], w_ref[...])
                dx, dwh = vjp(dy)
                dw = dw + dwh
                dx_ref[:, base + h * HD:base + (h + 1) * HD] = dx.astype(BF16)
            dw_ref[...] += dw

    vec = pl.BlockSpec((1, HD), lambda i: (0, 0))
    tab = pl.BlockSpec((tm, HD), lambda i: (i, 0))
    wide = pl.BlockSpec((tm, GW), lambda i: (i, 0))
    pair = pl.BlockSpec((tm, 2 * GW), lambda i: (i, pair_blk))
    return pl.pallas_call(
        body, grid=(t // tm,), in_specs=[pair, vec, vec, tab, tab, wide, wide, ANY],
        out_specs=[pair, vec, vec],
        out_shape=[_sds(dproj.shape, BF16), _sds((1, HD), F32), _sds((1, HD), F32)], input_output_aliases={7: 0},
        name=name, compiler_params=_params(1))(proj, wq_row, wk_row, cos_t, sin_t, dq_full, dk_full, dproj)


def _cast_into(name, x, dproj, blk_idx):
    t = x.shape[0]
    tm = min(512, t)

    def body(x_ref, dproj_ref, o_ref):
        o_ref[...] = x_ref[...].astype(BF16)

    return pl.pallas_call(
        body, grid=(t // tm,), in_specs=[pl.BlockSpec((tm, GW), lambda i: (i, 0)), ANY],
        out_specs=pl.BlockSpec((tm, GW), lambda i: (i, blk_idx)), out_shape=_sds(dproj.shape, BF16),
        input_output_aliases={1: 0}, name=name, compiler_params=_params(1))(x, dproj)


GROUP = SPAN * max(DILATIONS)
SCALE = HD ** -0.5
TILE_BATCH = 8


def _band_mask(lo):
    qi = _iota2((SPAN, 2 * SPAN), 0)
    ki = _iota2((SPAN, 2 * SPAN), 1)
    return (ki >= qi) & (ki <= qi + SPAN) & (ki >= lo)


def _tiles():
    return [(pi, r, u, rho) for pi, r in enumerate(DILATIONS) for rho in range(r) for u in range(GROUP // (SPAN * r))]


def _rows(r, u, rho):
    return pl.ds(u * SPAN * r + rho, SPAN, stride=r) if r > 1 else pl.ds(u * SPAN, SPAN)


def _attn_fwd(name, q, k, v, v_blk):
    t = q.shape[0]

    def body(qc_ref, kc_ref, vc_ref, kp_ref, vp_ref, ob_ref, lse_ref, o_scr, l_scr):
        mask_in = _band_mask(0)
        mask_edge = _band_mask(jnp.where(pl.program_id(0) == 0, SPAN, 0))
        tiles = _tiles()
        k_own = v_own = None
        for b0 in range(0, len(tiles), TILE_BATCH):
            work = []
            for pi, r, u, rho in tiles[b0:b0 + TILE_BATCH]:
                rows = _rows(r, u, rho)
                if u > 0:
                    k_prev, v_prev, mask = k_own, v_own, mask_in
                else:
                    prows = _rows(r, GROUP // (SPAN * r) - 1, rho)
                    k_prev, v_prev, mask = kp_ref[prows, :].astype(BF16), vp_ref[prows, :].astype(BF16), mask_edge
                k_own, v_own = kc_ref[rows, :].astype(BF16), vc_ref[rows, :].astype(BF16)
                work.append((pi, rows, mask, qc_ref[rows, :].astype(BF16), jnp.concatenate([k_prev, k_own], axis=0),
                             jnp.concatenate([v_prev, v_own], axis=0)))
            scores = [_dot(qt, kcat, 1, 1) for _, _, _, qt, kcat, _ in work]
            soft = []
            for (_, _, mask, _, _, _), s in zip(work, scores):
                s = jnp.where(mask, s * SCALE, NEG)
                m = jnp.max(s, axis=1, keepdims=True)
                p = jnp.exp(s - m)
                soft.append((m, _b16(p), jnp.sum(p, axis=1, keepdims=True)))
            outs = [_dot(p, vcat, 1, 0) for (_, p, _), (_, _, _, _, _, vcat) in zip(soft, work)]
            for (pi, rows, _, _, _, _), (m, _, den), o in zip(work, soft, outs):
                o_scr[pi, rows, :] = o / den
                l_scr[pi, rows, :] = jnp.broadcast_to(m + jnp.log(den), (SPAN, HD))
        step = 256
        for c in range(GROUP // step):
            sl = pl.ds(c * step, step)
            ob, lse = _merge([o_scr[i, sl, :] for i in range(3)], [l_scr[i, sl, :] for i in range(3)])
            ob_ref[sl, :] = ob
            lse_ref[sl, :] = lse

    cur = pl.BlockSpec((GROUP, HD), lambda g, h: (g, h))
    prev = pl.BlockSpec((GROUP, HD), lambda g, h: (jnp.maximum(g - 1, 0), h))
    vcur = pl.BlockSpec((GROUP, HD), lambda g, h: (g, v_blk * N_HEADS + h))
    vprev = pl.BlockSpec((GROUP, HD), lambda g, h: (jnp.maximum(g - 1, 0), v_blk * N_HEADS + h))
    return pl.pallas_call(
        body, grid=(t // GROUP, N_HEADS), in_specs=[cur, cur, vcur, prev, vprev], out_specs=[cur, cur],
        out_shape=[_sds((t, GW), F32), _sds((t, GW), F32)],
        scratch_shapes=[pltpu.VMEM((3, GROUP, HD), F32), pltpu.VMEM((3, GROUP, HD), F32)], name=name,
        compiler_params=_params(2))(q, k, v, k, v)


def _attn_bwd(name, q, k, v, v_blk, do, lse, delta):
    t = q.shape[0]
    ng = t // GROUP

    def probs(work):
        scores = [_dot(qt, kcat, 1, 1) for qt, _, _, _, kcat, _, _ in work]
        dps = [_dot(dot, vcat, 1, 1) for _, dot, _, _, _, vcat, _ in work]
        out = []
        for (_, _, lt, dlt, kcat, _, mask), s, dp in zip(work, scores, dps):
            wide = kcat.shape[0] // SPAN
            lw = jnp.concatenate([lt] * wide, axis=1) if wide > 1 else lt
            dw = jnp.concatenate([dlt] * wide, axis=1) if wide > 1 else dlt
            p = jnp.exp(jnp.where(mask, s * SCALE - lw, NEG))
            out.append((_b16(p * (dp - dw) * SCALE), _b16(p)))
        return out

    def body(qc_ref, kc_ref, vc_ref, doc_ref, lc_ref, dc_ref, kp_ref, vp_ref, qn_ref, don_ref, ln_ref, dn_ref,
             dq_ref, dk_ref, dv_ref):
        g = pl.program_id(0)
        mask_in = _band_mask(0)
        mask_edge = _band_mask(jnp.where(g == 0, SPAN, 0))
        dk_ref[...] = jnp.zeros_like(dk_ref)
        dv_ref[...] = jnp.zeros_like(dv_ref)
        tiles = _tiles()
        k_own = v_own = None
        for b0 in range(0, len(tiles), TILE_BATCH):
            where, work = [], []
            for pi, r, u, rho in tiles[b0:b0 + TILE_BATCH]:
                rows = _rows(r, u, rho)
                if u > 0:
                    prows, k_prev, v_prev, mask = _rows(r, u - 1, rho), k_own, v_own, mask_in
                else:
                    prows = _rows(r, GROUP // (SPAN * r) - 1, rho)
                    k_prev, v_prev, mask = kp_ref[prows, :].astype(BF16), vp_ref[prows, :].astype(BF16), mask_edge
                k_own, v_own = kc_ref[rows, :].astype(BF16), vc_ref[rows, :].astype(BF16)
                where.append((pi, u, rows, prows))
                work.append((qc_ref[rows, :].astype(BF16), doc_ref[rows, :].astype(BF16), lc_ref[rows, :], dc_ref[rows, :],
                             jnp.concatenate([k_prev, k_own], axis=0), jnp.concatenate([v_prev, v_own], axis=0), mask))
            dsp = probs(work)
            dqs = [_dot(ds, w[4], 1, 0) for (ds, _), w in zip(dsp, work)]
            dks = [_dot(ds, w[0], 0, 0) for (ds, _), w in zip(dsp, work)]
            dvs = [_dot(p, w[1], 0, 0) for (_, p), w in zip(dsp, work)]
            for (pi, u, rows, prows), dq_t, dk2, dv2 in zip(where, dqs, dks, dvs):
                if pi == 0:
                    dq_ref[rows, :] = dq_t
                else:
                    dq_ref[rows, :] += dq_t
                dk_ref[rows, :] += dk2[SPAN:, :]
                dv_ref[rows, :] += dv2[SPAN:, :]
                if u > 0:
                    dk_ref[prows, :] += dk2[:SPAN, :]
                    dv_ref[prows, :] += dv2[:SPAN, :]
        qi = _iota2((SPAN, SPAN), 0)
        ki = _iota2((SPAN, SPAN), 1)
        mask_next = (ki >= qi) & (ki < jnp.where(g == ng - 1, 0, SPAN))
        edge = [(r, rho) for r in DILATIONS for rho in range(r)]
        for b0 in range(0, len(edge), TILE_BATCH):
            where, work = [], []
            for r, rho in edge[b0:b0 + TILE_BATCH]:
                krows, qrows = _rows(r, GROUP // (SPAN * r) - 1, rho), _rows(r, 0, rho)
                where.append(krows)
                work.append((qn_ref[qrows, :].astype(BF16), don_ref[qrows, :].astype(BF16), ln_ref[qrows, :],
                             dn_ref[qrows, :], kc_ref[krows, :].astype(BF16), vc_ref[krows, :].astype(BF16), mask_next))
            dsp = probs(work)
            dks = [_dot(ds, w[0], 0, 0) for (ds, _), w in zip(dsp, work)]
            dvs = [_dot(p, w[1], 0, 0) for (_, p), w in zip(dsp, work)]
            for krows, dk1, dv1 in zip(where, dks, dvs):
                dk_ref[krows, :] += dk1
                dv_ref[krows, :] += dv1

    cur = pl.BlockSpec((GROUP, HD), lambda g, h: (g, h))
    prev = pl.BlockSpec((GROUP, HD), lambda g, h: (jnp.maximum(g - 1, 0), h))
    nxt = pl.BlockSpec((GROUP, HD), lambda g, h: (jnp.minimum(g + 1, ng - 1), h))
    vcur = pl.BlockSpec((GROUP, HD), lambda g, h: (g, v_blk * N_HEADS + h))
    vprev = pl.BlockSpec((GROUP, HD), lambda g, h: (jnp.maximum(g - 1, 0), v_blk * N_HEADS + h))
    return pl.pallas_call(
        body, grid=(ng, N_HEADS), in_specs=[cur, cur, vcur, cur, cur, cur, prev, vprev] + [nxt] * 4,
        out_specs=[cur] * 3,
        out_shape=[_sds((t, GW), F32)] * 3, name=name,
        compiler_params=_params(2))(q, k, v, do, lse, delta, k, v, q, do, lse, delta)


def _merge(os_, ls_):
    m = jnp.maximum(jnp.maximum(ls_[0], ls_[1]), ls_[2])
    ws = [jnp.exp(l - m) for l in ls_]
    tot = ws[0] + ws[1] + ws[2]
    ob = (ws[0] * os_[0] + ws[1] * os_[1] + ws[2] * os_[2]) / tot
    return ob, m + jnp.log(tot)


def _gated_norm(oa, z, wv):
    return _head_rms(oa, wv) * _silu(z)


def _mix_fwd(name, oa_raw, proj, z_blk, ob, w_dn, w_an):
    t = oa_raw.shape[0]
    tm = min(256, t)

    def body(oa_ref, z_ref, ob_ref, wd_ref, wa_ref, mix_ref):
        for h in range(N_HEADS):
            sl = slice(h * HD, (h + 1) * HD)
            mix_ref[:, sl] = _gated_norm(oa_ref[:, sl], z_ref[:, sl], wd_ref[...]).astype(BF16)
            mix_ref[:, GW + h * HD:GW + (h + 1) * HD] = _head_rms(ob_ref[:, sl], wa_ref[...]).astype(BF16)

    vec = pl.BlockSpec((1, HD), lambda i: (0, 0))
    wide = pl.BlockSpec((tm, GW), lambda i: (i, 0))
    return pl.pallas_call(
        body, grid=(t // tm,),
        in_specs=[wide, pl.BlockSpec((tm, GW), lambda i: (i, z_blk)), wide, vec, vec],
        out_specs=pl.BlockSpec((tm, 2 * GW), lambda i: (i, 0)),
        out_shape=_sds((t, 2 * GW), BF16), name=name,
        compiler_params=_params(1))(oa_raw, proj, ob, w_dn, w_an)


def _mix_bwd(name, dx1_16, w_out, oa_raw, proj, z_blk, ob, w_dn, w_an, dep):
    t, d = dx1_16.shape
    tm = min(512, t)

    def body(dx_ref, wo_ref, oa_ref, z_ref, ob_ref, wd_ref, wa_ref, dep_ref,
             doa_ref, dz_ref, dob_ref, dl_ref, dwd_ref, dwa_ref):
        dwd = jnp.zeros((1, HD), F32)
        dwa = jnp.zeros((1, HD), F32)
        dxv = dx_ref[...]
        pairs = [_dot(dxv, wo_ref[2 * p * HD:2 * (p + 1) * HD, :], 1, 1) for p in range(N_HEADS)]
        heads = [half for pr in pairs for half in (pr[:, :HD], pr[:, HD:])]
        dm_a, dm_b = heads[:N_HEADS], heads[N_HEADS:]
        for h in range(N_HEADS):
            sl = slice(h * HD, (h + 1) * HD)
            _, vjp = jax.vjp(_gated_norm, oa_ref[:, sl], z_ref[:, sl], wd_ref[...])
            doa, dz, dw1 = vjp(dm_a[h])
            doa_ref[:, sl] = doa
            dz_ref[:, sl] = dz.astype(BF16)
            dwd = dwd + dw1
            obh = ob_ref[:, sl]
            _, vjp2 = jax.vjp(_head_rms, obh, wa_ref[...])
            dob, dw2 = vjp2(dm_b[h])
            dwa = dwa + dw2
            dob_ref[:, sl] = dob
            dl_ref[:, sl] = jnp.broadcast_to(jnp.sum(dob * obh, axis=1, keepdims=True), (tm, HD))

        @pl.when(pl.program_id(0) == 0)
        def _():
            dwd_ref[...] = jnp.zeros_like(dwd_ref)
            dwa_ref[...] = jnp.zeros_like(dwa_ref)

        dwd_ref[...] += dwd
        dwa_ref[...] += dwa

    vec = pl.BlockSpec((1, HD), lambda i: (0, 0))
    wide = pl.BlockSpec((tm, GW), lambda i: (i, 0))
    return pl.pallas_call(
        body, grid=(t // tm,),
        in_specs=[pl.BlockSpec((tm, d), lambda i: (i, 0)), pl.BlockSpec((2 * GW, d), lambda i: (0, 0)), wide,
                  pl.BlockSpec((tm, GW), lambda i: (i, z_blk)), wide, vec, vec, ANY],
        out_specs=[wide, pl.BlockSpec((tm, GW), lambda i: (i, z_blk)), wide, wide, vec, vec],
        out_shape=[_sds((t, GW), F32), _sds(proj.shape, BF16), _sds((t, GW), F32), _sds((t, GW), F32),
                   _sds((1, HD), F32), _sds((1, HD), F32)], name=name,
        compiler_params=_params(1))(dx1_16, w_out, oa_raw, proj, ob, w_dn, w_an, dep)


def _halves(n):
    cut = (n // 256) * 128
    return [(0, cut), (cut, n)]


def _gate_up_swiglu(name, h2, w_gu_g):
    t, d = h2.shape
    n = w_gu_g.shape[2]
    per = N_DEV // 2
    tm = min(512, t)

    def body(a_ref, bg_ref, bu_ref, gu_ref, act_ref):
        a = a_ref[...]
        cuts = _halves(n)
        gs = [_dot(a, bg_ref[:, c0:c1], 1, 0) for c0, c1 in cuts]
        ups = [_dot(a, bu_ref[:, c0:c1], 1, 0) for c0, c1 in cuts]
        for (c0, c1), g, up in zip(cuts, gs, ups):
            gu_ref[0, :, c0:c1] = g.astype(BF16)
            gu_ref[1, :, c0:c1] = up.astype(BF16)
            act_ref[:, c0:c1] = (_silu(g) * up).astype(BF16)

    return pl.pallas_call(
        body, grid=(per, t // tm),
        in_specs=[pl.BlockSpec((tm, d), lambda j, i: (i, 0)), pl.BlockSpec((None, d, n), lambda j, i: (j, 0, 0)),
                  pl.BlockSpec((None, d, n), lambda j, i: (j + per, 0, 0))],
        out_specs=[pl.BlockSpec((2, tm, n), lambda j, i: (0, i, j)), pl.BlockSpec((tm, n), lambda j, i: (i, j))],
        out_shape=[_sds((2, t, per * n), BF16), _sds((t, per * n), BF16)], name=name,
        compiler_params=_params(2))(h2, w_gu_g, w_gu_g)


def _d_gate_up(name, dy16, w_down, gu3, dep):
    t, d = dy16.shape
    f = w_down.shape[0]
    tm, tn = min(1024, t), f // 4

    def body(a_ref, b_ref, g_ref, dep_ref, o_ref):
        a = a_ref[...]
        cuts = _halves(tn)
        dacts = [_dot(a, b_ref[c0:c1, :], 1, 1) for c0, c1 in cuts]
        for (c0, c1), dact in zip(cuts, dacts):
            g, up = g_ref[0, :, c0:c1].astype(F32), g_ref[1, :, c0:c1].astype(F32)
            sg = _sigmoid(g)
            o_ref[0, :, c0:c1] = (dact * up * sg * (1.0 + g * (1.0 - sg))).astype(BF16)
            o_ref[1, :, c0:c1] = (dact * g * sg).astype(BF16)

    return pl.pallas_call(
        body, grid=(f // tn, t // tm),
        in_specs=[pl.BlockSpec((tm, d), lambda j, i: (i, 0)), pl.BlockSpec((tn, d), lambda j, i: (j, 0)),
                  pl.BlockSpec((2, tm, tn), lambda j, i: (0, i, j)), ANY],
        out_specs=pl.BlockSpec((2, tm, tn), lambda j, i: (0, i, j)), out_shape=_sds((2, t, f), BF16), name=name,
        compiler_params=_params(2))(dy16, w_down, gu3, dep)


def _d_h2(name, dgu3, w_gu_g, dep):
    _, t, f = dgu3.shape
    n_dev, d, n = w_gu_g.shape
    per = n_dev // 2
    tm, tn = min(512, t), 512

    def body(g_ref, u_ref, b_ref, dep_ref, o_ref):
        acc = None
        for s in range(n_dev):
            a_ref = g_ref if s < per else u_ref
            part = _dot(a_ref[:, (s % per) * n:(s % per + 1) * n], b_ref[s], 1, 1)
            acc = part if acc is None else acc + part
        o_ref[...] = acc

    return pl.pallas_call(
        body, grid=(d // tn, t // tm),
        in_specs=[pl.BlockSpec((None, tm, f), lambda j, i: (0, i, 0)), pl.BlockSpec((None, tm, f), lambda j, i: (1, i, 0)),
                  pl.BlockSpec((n_dev, tn, n), lambda j, i: (0, j, 0)), ANY],
        out_specs=pl.BlockSpec((tm, tn), lambda j, i: (i, j)), out_shape=_sds((t, d), F32), name=name,
        compiler_params=_params(2))(dgu3, dgu3, w_gu_g, dep)


def _out_proj_norm(name, mixed, w_out, x, w_norm):
    t, d = x.shape
    kdim = mixed.shape[1]
    tm = min(512, t)

    def body(a_ref, b_ref, x_ref, w_ref, x1_ref, h_ref):
        x1 = x_ref[...] + _dot(a_ref[...], b_ref[...], 1, 0)
        x1_ref[...] = x1
        h_ref[...] = _rms_f(x1, w_ref[...]).astype(BF16)

    row = pl.BlockSpec((tm, d), lambda i: (i, 0))
    return pl.pallas_call(
        body, grid=(t // tm,),
        in_specs=[pl.BlockSpec((tm, kdim), lambda i: (i, 0)), pl.BlockSpec((kdim, d), lambda i: (0, 0)), row,
                  pl.BlockSpec((1, d), lambda i: (0, 0))],
        out_specs=[row, row], out_shape=[_sds((t, d), F32), _sds((t, d), BF16)], name=name,
        compiler_params=_params(1))(mixed, w_out, x, w_norm)


def _down_loss(name, act, w_down, x1, target):
    t, f = act.shape
    d = x1.shape[1]
    tm, tn = min(1024, t), 512

    def body(a_ref, b_ref, x_ref, t_ref, dy_ref, dy16_ref, l_ref):
        diff = _dot(a_ref[...], b_ref[...], 1, 0) + x_ref[...] - t_ref[...]
        dyv = diff * (1.0 / d)
        dy_ref[...] = dyv
        dy16_ref[...] = dyv.astype(BF16)
        tot = jnp.sum(jnp.sum(diff * diff, axis=1, keepdims=True), axis=0, keepdims=True) * (0.5 / d)

        @pl.when((pl.program_id(0) == 0) & (pl.program_id(1) == 0))
        def _():
            l_ref[...] = jnp.zeros_like(l_ref)

        l_ref[...] += jnp.broadcast_to(tot, (8, 128))

    tile = pl.BlockSpec((tm, tn), lambda i, j: (i, j))
    return pl.pallas_call(
        body, grid=(t // tm, d // tn),
        in_specs=[pl.BlockSpec((tm, f), lambda i, j: (i, 0)), pl.BlockSpec((f, tn), lambda i, j: (0, j)), tile, tile],
        out_specs=[tile, tile, pl.BlockSpec((8, 128), lambda i, j: (0, 0))],
        out_shape=[_sds((t, d), F32), _sds((t, d), BF16), _sds((8, 128), F32)], name=name,
        compiler_params=_params(2))(act, w_down, x1, target)


def _peer(me, k):
    pid = (me + k) % N_DEV
    return (pid // 4, (pid // 2) % 2, pid % 2)


def _my_id():
    return 4 * lax.axis_index("x") + 2 * lax.axis_index("y") + lax.axis_index("c")


def _exchange(name, arrays, scatter, dep):
    n = len(arrays)

    def body(*refs):
        ins, outs = refs[:n], refs[n + 1:2 * n + 1]
        send_sems, recv_sems, local_sems = refs[2 * n + 1:]
        me = _my_id()
        started = []
        for a in range(n):
            src = ins[a].at[me] if scatter[a] else ins[a]
            loc = pltpu.make_async_copy(src, outs[a].at[me], local_sems.at[a])
            loc.start()
            started.append(loc)
        remote = []
        for k in range(1, N_DEV):
            to = (me + k) % N_DEV
            for a in range(n):
                src = ins[a].at[to] if scatter[a] else ins[a]
                cp = pltpu.make_async_remote_copy(src_ref=src, dst_ref=outs[a].at[me],
                                                  send_sem=send_sems.at[a * (N_DEV - 1) + k - 1], recv_sem=recv_sems.at[a * (N_DEV - 1) + k - 1],
                                                  device_id=_peer(me, k), device_id_type=pl.DeviceIdType.MESH)
                cp.start()
                remote.append(cp)
        for k in range(1, N_DEV):
            frm = (me + N_DEV - k) % N_DEV
            for a in range(n):
                src = ins[a].at[frm] if scatter[a] else ins[a]
                pltpu.make_async_remote_copy(src_ref=src, dst_ref=outs[a].at[frm],
                                             send_sem=send_sems.at[a * (N_DEV - 1) + k - 1], recv_sem=recv_sems.at[a * (N_DEV - 1) + k - 1],
                                             device_id=_peer(me, k), device_id_type=pl.DeviceIdType.MESH).wait_recv()
        for cp in remote:
            cp.wait_send()
        for loc in started:
            loc.wait()

    out_shape = [_sds((N_DEV,) + (a.shape[1:] if sc else a.shape), a.dtype) for a, sc in zip(arrays, scatter)]
    return pl.pallas_call(
        body, in_specs=[ANY] * (n + 1), out_specs=[ANY] * n, out_shape=out_shape,
        scratch_shapes=[pltpu.SemaphoreType.DMA((n * (N_DEV - 1),)), pltpu.SemaphoreType.DMA((n * (N_DEV - 1),)),
                        pltpu.SemaphoreType.DMA((n,))],
        name=name)(*arrays, dep)


def _gather_two_level(name, arrays):
    n = len(arrays)
    per = N_DEV - 1
    units = []
    for a, arr in enumerate(arrays):
        cuts = 4 if arr.shape[0] % 64 == 0 and arr.shape[0] >= 1024 else 1
        units += [(a, p * (arr.shape[0] // cuts), arr.shape[0] // cuts) for p in range(cuts)]
    nu = len(units)

    def body(*refs):
        ins, outs = refs[:n], refs[n:2 * n]
        send_sems, recv_sems, local_sems = refs[2 * n:]
        x, y, c = lax.axis_index("x"), lax.axis_index("y"), lax.axis_index("c")
        me, sibling = (x, y, c), (x, y, 1 - c)
        flip = lambda v, on: v + on - 2 * v * on
        relayed = (flip(x, c), flip(y, 1 - c), c)
        other = (flip(x, 1 - c), flip(y, c), c)
        diagonal = (1 - x, 1 - y, c)
        k_relayed, k_other = 2 - c, 1 + c

        def copy(u, k, block, to, from_input=False):
            a, r0, nr = units[u]
            slot = outs[a].at[4 * block[0] + 2 * block[1] + block[2], pl.ds(r0, nr)]
            return pltpu.make_async_remote_copy(
                src_ref=ins[a].at[pl.ds(r0, nr)] if from_input else slot, dst_ref=slot,
                send_sem=send_sems.at[u * per + k], recv_sem=recv_sems.at[u * per + k], device_id=to,
                device_id_type=pl.DeviceIdType.MESH)

        mine = [pltpu.make_async_copy(ins[a], outs[a].at[4 * x + 2 * y + c], local_sems.at[a]) for a in range(n)]
        for cp in mine:
            cp.start()
        sent = [copy(u, 1, me, (1 - x, y, c), True) for u in range(nu)]
        sent += [copy(u, 2, me, (x, 1 - y, c), True) for u in range(nu)]
        sent += [copy(u, 0, me, sibling, True) for u in range(nu)]
        for cp in sent:
            cp.start()
        for u in range(nu):
            copy(u, k_relayed, relayed, me).wait_recv()
            sent.append(copy(u, 3, relayed, other))
            sent.append(copy(u, 3 + k_relayed, relayed, sibling))
            sent[-2].start()
            sent[-1].start()
        for u in range(nu):
            copy(u, k_other, other, me).wait_recv()
            sent.append(copy(u, 3 + k_other, other, sibling))
            sent[-1].start()
        for u in range(nu):
            copy(u, 3, diagonal, me).wait_recv()
            sent.append(copy(u, 6, diagonal, sibling))
            sent[-1].start()
        for u in range(nu):
            copy(u, 0, sibling, me).wait_recv()
            for j, chip in enumerate([(1 - x, y), (x, 1 - y), (1 - x, 1 - y)]):
                copy(u, 4 + j, (*chip, 1 - c), me).wait_recv()
        for cp in sent:
            cp.wait_send()
        for cp in mine:
            cp.wait()

    return pl.pallas_call(
        body, in_specs=[ANY] * n, out_specs=[ANY] * n,
        out_shape=[_sds((N_DEV,) + a.shape, a.dtype) for a in arrays],
        scratch_shapes=[pltpu.SemaphoreType.DMA((nu * per,)), pltpu.SemaphoreType.DMA((nu * per,)),
                        pltpu.SemaphoreType.DMA((n,))],
        name=name)(*arrays)


HBM = pl.BlockSpec(memory_space=pltpu.HBM)
SEM = pl.BlockSpec(memory_space=pltpu.SEMAPHORE)
EFFECT = pltpu.SideEffectType.DATAFLOW_SIDE_EFFECTING


def _remote_copies(srcs, lands, scatter, send_sems, recv_sems, me, incoming):
    out = []
    for k in range(1, N_DEV):
        other = (me + N_DEV - k) % N_DEV if incoming else (me + k) % N_DEV
        for a in range(len(srcs)):
            sem = a * (N_DEV - 1) + k - 1
            src = srcs[a].at[other] if scatter[a] else srcs[a]
            dst = lands[a].at[other if incoming else me]
            out.append(pltpu.make_async_remote_copy(src_ref=src, dst_ref=dst, send_sem=send_sems.at[sem],
                                                    recv_sem=recv_sems.at[sem], device_id=_peer(me, k),
                                                    device_id_type=pl.DeviceIdType.MESH))
    return out


def _exchange_start(name, arrays, scatter, dep):
    n = len(arrays)
    lands = [lax.empty((N_DEV,) + (a.shape[1:] if sc else a.shape), a.dtype) for a, sc in zip(arrays, scatter)]

    def body(*refs):
        srcs, land_refs = refs[:n], refs[n:2 * n]
        send_sems, recv_sems = refs[2 * n + 1], refs[2 * n + 2]
        token = refs[-1]
        for cp in _remote_copies(srcs, land_refs, scatter, send_sems, recv_sems, _my_id(), False):
            cp.start()
        token[...] = jnp.zeros_like(token)

    n_sem = n * (N_DEV - 1)
    out_shape = ([pltpu.SemaphoreType.DMA((n_sem,)), pltpu.SemaphoreType.DMA((n_sem,))]
                 + [pltpu.HBM(a.shape, a.dtype) for a in arrays] + [pltpu.HBM(l.shape, l.dtype) for l in lands]
                 + [_sds((8, 128), F32)])
    aliases = {i: 2 + i for i in range(2 * n)}
    args = [pltpu.with_memory_space_constraint(a, pltpu.HBM) for a in list(arrays) + lands] + [dep]
    res = pl.pallas_call(
        body, name=name, in_specs=[HBM] * (2 * n) + [ANY], out_shape=out_shape,
        out_specs=[SEM, SEM] + [HBM] * (2 * n) + [pl.BlockSpec(memory_space=pltpu.VMEM)],
        input_output_aliases=aliases, compiler_params=pltpu.CompilerParams(has_side_effects=EFFECT))(*args)
    return dict(send=res[0], recv=res[1], srcs=res[2:2 + n], lands=res[2 + n:2 + 2 * n], token=res[-1],
                scatter=scatter)


def _exchange_wait(name, started, after):
    n = len(started["srcs"])
    scatter = started["scatter"]

    def body(*refs):
        srcs, land_refs = refs[:n], refs[n:2 * n]
        send_sems, recv_sems = refs[2 * n], refs[2 * n + 1]
        me = _my_id()
        for cp in _remote_copies(srcs, land_refs, scatter, send_sems, recv_sems, me, False):
            cp.wait_send()
        for cp in _remote_copies(srcs, land_refs, scatter, send_sems, recv_sems, me, True):
            cp.wait_recv()

    arrs = list(started["srcs"]) + list(started["lands"])
    res = pl.pallas_call(
        body, name=name, in_specs=[HBM] * (2 * n) + [SEM, SEM, ANY],
        out_shape=[pltpu.HBM(a.shape, a.dtype) for a in arrs], out_specs=[HBM] * (2 * n),
        input_output_aliases={i: i for i in range(2 * n)},
        compiler_params=pltpu.CompilerParams(has_side_effects=EFFECT))(*arrs, started["send"], started["recv"], after)
    me = _my_id()
    out = []
    for src, land, sc in zip(res[:n], res[n:], scatter):
        own = lax.dynamic_index_in_dim(src, me, 0, keepdims=True) if sc else src[None]
        out.append(lax.dynamic_update_slice(land, own, (me,) + (0,) * (land.ndim - 1)))
    return out


def _adamw(name, parts, w, m, v):
    r, c = w.shape
    tr, tc = r, c
    if r % 8 == 0:
        tr = next(cand for cand in (128, 88, 64, 40, 8) if r % cand == 0)
    else:
        tc = 256
    c1 = 1.0 / (1.0 - ADAM_B1 ** ADAM_STEP)
    c2 = 1.0 / (1.0 - ADAM_B2 ** ADAM_STEP)

    def body(p_ref, w_ref, m_ref, v_ref, g_ref, d_ref, nm_ref, nv_ref):
        g = p_ref[0].astype(F32)
        for s in range(1, N_DEV):
            g = g + p_ref[s].astype(F32)
        mn = ADAM_B1 * m_ref[...] + (1.0 - ADAM_B1) * g
        vn = ADAM_B2 * v_ref[...] + (1.0 - ADAM_B2) * (g * g)
        g_ref[...] = g
        nm_ref[...] = mn
        nv_ref[...] = vn
        d_ref[...] = -ADAM_LR * ((mn * c1) / (jnp.sqrt(vn * c2) + ADAM_EPS) + ADAM_WD * w_ref[...])

    blk = pl.BlockSpec((tr, tc), lambda i, j: (i, j))
    return pl.pallas_call(
        body, grid=(r // tr, c // tc),
        in_specs=[pl.BlockSpec((N_DEV, tr, tc), lambda i, j: (0, i, j)), blk, blk, blk],
        out_specs=[blk] * 4, out_shape=[_sds((r, c), F32)] * 4, name=name,
        compiler_params=_params(2, VMEM_LIMIT))(parts, w, m, v)


def _pad_rows(a, rows):
    return jnp.pad(a, ((0, rows - a.shape[0]), (0, 0)))


def _lane_row(vec8, offset):
    return jnp.pad(vec8.reshape(1, 8), ((0, 0), (offset, HD - 8 - offset)))


def kernel(x, positions, attn_norm_w, w_in, conv_w, a_log, dt_bias, delta_out_norm_w, q_norm_w, k_norm_w, attn_out_norm_w, w_out, ffn_norm_w, w_gate_up, w_down, loss_target, m_attn_norm_w, m_w_in, m_conv_w, m_a_log, m_dt_bias, m_delta_out_norm_w, m_q_norm_w, m_k_norm_w, m_attn_out_norm_w, m_w_out, m_ffn_norm_w, m_w_gate_up, m_w_down, v_attn_norm_w, v_w_in, v_conv_w, v_a_log, v_dt_bias, v_delta_out_norm_w, v_q_norm_w, v_k_norm_w, v_attn_out_norm_w, v_w_out, v_ffn_norm_w, v_w_gate_up, v_w_down):
    x2 = x[0]
    t, d = x2.shape
    target = loss_target[0]
    pos_col = positions.reshape(t, 1)
    half = HD // 2
    inv = (ROPE_THETA ** (-np.arange(half, dtype=np.float32) / half)).astype(np.float32)
    inv_row = jnp.asarray(np.concatenate([inv, inv]).reshape(1, HD))

    n_in = w_in.shape[2]
    n_gu = w_gate_up.shape[2]
    w_in_g, conv_g = _gather_two_level("gather_in", [w_in[0].astype(BF16), _pad_rows(conv_w[0], 8)])
    out_fly = _exchange_start("gather_out_start", [w_out[0].astype(BF16)], [False], conv_g)
    gu_fly = _exchange_start("gather_gate_up_start", [w_gate_up[0].astype(BF16)], [False], out_fly["token"])
    down_fly = _exchange_start("gather_down_start", [w_down[0].astype(BF16)], [False], gu_fly["token"])
    n_main = 4 * GW
    n_small = 2 * N_HEADS
    segments = [(0, n_main, 0), (n_main + n_small, N_DEV * n_in, n_main), (n_main, n_main + n_small, 7 * GW)]
    pieces = []
    for lo, hi, _ in segments:
        f = lo
        while f < hi:
            j = f // n_in
            end = min(hi, (j + 1) * n_in)
            pieces.append(w_in_g[j][:, f - j * n_in:end - j * n_in])
            f = end
    w_cat = jnp.concatenate(pieces + [jnp.zeros((d, HD - n_small), BF16)], axis=1)
    n_cat = w_cat.shape[1]
    small_blk = (7 * GW) // HD
    conv_w8 =jnp.transpose(conv_g, (1, 0, 2)).reshape(8, 3 * GW)
    alog_row = _lane_row(a_log[0], 8)
    dtb_row = _lane_row(dt_bias[0], 8)

    tm = min(2048, t)
    h1 = _rms_fwd("norm1", x2, attn_norm_w, down_fly["token"])
    tmp, tnp = min(1024, t), n_cat // 3
    proj = _mm("in_proj", h1, w_cat, grid=(t // tmp, n_cat // tnp, 1),
               a_spec=pl.BlockSpec((tmp, d), lambda i, j, k: (i, 0)),
               b_spec=pl.BlockSpec((d, tnp), lambda i, j, k: (0, j)),
               o_spec=pl.BlockSpec((tmp, tnp), lambda i, j, k: (i, j)),
               out_shape=_sds((t, n_cat), F32), ca=1, cb=0, nk=1)
    qn = _conv_fwd("conv_q", proj, conv_w8, 0, True, HD ** -0.5)
    kn = _conv_fwd("conv_k", proj, conv_w8, 1, True, 1.0)
    vv = _conv_fwd("conv_v", proj, conv_w8, 2, False, 1.0)
    beta_b, gc_b = _gates_fwd("gates", proj, small_blk, alog_row, dtb_row)
    u, w, p, tinv, qd, kd = _delta_prep("delta_prep", qn, kn, vv, beta_b, gc_b)
    oa_raw, vn, s_hist = _delta_scan("delta_scan", u, w, p, qd, kd, gc_b)

    cos_t, sin_t = _rope_tables("rope_tables", pos_col, inv_row)
    aq, ak = _qk_fwd("attn_qk", proj, 2, q_norm_w, k_norm_w, cos_t, sin_t)
    ob, lse = _attn_fwd("attn_fwd", aq, ak, proj, 6)
    mixed = _mix_fwd("mix", oa_raw, proj, 3, ob, delta_out_norm_w, attn_out_norm_w)
    (w_out_g,) = _exchange_wait("gather_out_wait", out_fly, mixed)
    w_out_full = w_out_g.reshape(2 * GW, d)
    tn = 512
    x1, h2 = _out_proj_norm("out_proj", mixed, w_out_full, x2, ffn_norm_w)
    per = N_DEV // 2
    (w_gu_g,) = _exchange_wait("gather_gate_up_wait", gu_fly, h2)
    gu3, act = _gate_up_swiglu("gate_up", h2, w_gu_g)
    (w_down_g,) = _exchange_wait("gather_down_wait", down_fly, act)
    w_down_full = w_down_g.reshape(D_FF, d)
    tmd = min(1024, t)
    dy, dy16, loss_tile = _down_loss("down_proj", act, w_down_full, x1, target)

    tk, nkt = t, 1
    g_down = _mm("g_down", act, dy16, grid=(D_FF // 1408, d // 512, nkt),
                 a_spec=pl.BlockSpec((tk, 1408), lambda i, j, k: (k, i)),
                 b_spec=pl.BlockSpec((tk, 512), lambda i, j, k: (k, j)),
                 o_spec=pl.BlockSpec((1408, 512), lambda i, j, k: (i, j)),
                 out_shape=_sds((D_FF, d), F32), ca=0, cb=0, nk=nkt)
    down_g_fly = _exchange_start("reduce_down_start", [g_down.reshape(N_DEV, D_FF // N_DEV, d)], [True], dy16)
    dgu3 = _d_gate_up("d_gate_up", dy16, w_down_full, gu3, down_g_fly["token"])
    g_gu = _mm("g_gate_up", h2, dgu3, grid=(d // 512, N_DEV, nkt),
               a_spec=pl.BlockSpec((tk, 512), lambda i, j, k: (k, i)),
               b_spec=pl.BlockSpec((None, tk, n_gu), lambda i, j, k: (j // per, k, j % per)),
               o_spec=pl.BlockSpec((None, 512, n_gu), lambda i, j, k: (j, i, 0)),
               out_shape=_sds((N_DEV, d, n_gu), F32), ca=0, cb=0, nk=nkt)
    gu_g_fly = _exchange_start("reduce_gate_up_start", [g_gu], [True], dy16)
    dh2 = _d_h2("d_h2", dgu3, w_gu_g, gu_g_fly["token"])
    dx1, dx1_16, g_ffn_norm = _rms_bwd("norm2_bwd", x1, ffn_norm_w, dh2, dy)

    g_out = _mm("g_out", mixed, dx1_16, grid=((2 * GW) // 512, 1, nkt),
                a_spec=pl.BlockSpec((tk, 512), lambda i, j, k: (k, i)),
                b_spec=pl.BlockSpec((tk, d), lambda i, j, k: (k, 0)),
                o_spec=pl.BlockSpec((512, d), lambda i, j, k: (i, 0)),
                out_shape=_sds((2 * GW, d), F32), ca=0, cb=0, nk=nkt)
    out_g_fly = _exchange_start("reduce_out_start", [g_out.reshape(N_DEV, (2 * GW) // N_DEV, d)], [True], g_ffn_norm)
    doa, dproj, dob, delta, g_dn, g_an = _mix_bwd("mix_bwd", dx1_16, w_out_full, oa_raw, proj, 3, ob,
                                                  delta_out_norm_w, attn_out_norm_w, out_g_fly["token"])
    d_aq, d_ak, d_av = _attn_bwd("attn_bwd", aq, ak, proj, 6, dob, lse, delta)
    dproj, g_qn, g_kn = _qk_bwd("attn_qk_bwd", proj, 2, q_norm_w, k_norm_w, cos_t, sin_t, d_aq, d_ak, dproj)
    dproj = _cast_into("attn_v_bwd", d_av, dproj, 6)

    dvn, dqd, dkd, dw, ddec = _delta_scan_bwd("delta_scan_bwd", doa, w, p, qd, kd, gc_b, vn, s_hist)
    dqn, dkn, dvv, dbeta_b, dg_b = _delta_prep_bwd("delta_prep_bwd", qn, kn, vv, beta_b, gc_b, tinv, u, w, vn,
                                                   doa, dvn, dqd, dkd, dw, ddec)
    dproj, gcw_q = _conv_bwd("conv_q_bwd", proj, conv_w8, dqn, dproj, 0, True, HD ** -0.5)
    dproj, gcw_k = _conv_bwd("conv_k_bwd", proj, conv_w8, dkn, dproj, 1, True, 1.0)
    dproj, gcw_v = _conv_bwd("conv_v_bwd", proj, conv_w8, dvv, dproj, 2, False, 1.0)
    dproj, g_alog_row, g_dtb_row = _gates_bwd("gates_bwd", proj, small_blk, alog_row, dtb_row, dbeta_b, dg_b, dproj)
    tmc = 384
    g_cat = _mm("g_in", dproj, h1, grid=(n_cat // tmc, 1, nkt),
                a_spec=pl.BlockSpec((tk, tmc), lambda i, j, k: (k, i)),
                b_spec=pl.BlockSpec((tk, d), lambda i, j, k: (k, 0)),
                o_spec=pl.BlockSpec((tmc, d), lambda i, j, k: (i, 0)),
                out_shape=_sds((n_cat, d), BF16), ca=0, cb=0, nk=nkt)
    parts = []
    for j in range(N_DEV):
        cols = []
        for lo, hi, start in sorted(segments):
            a, b = max(lo, j * n_in), min(hi, (j + 1) * n_in)
            if a < b:
                cols.append(g_cat[start + a - lo:start + b - lo])
        parts.append(cols[0] if len(cols) == 1 else jnp.concatenate(cols, axis=0))
    g_in_parts = jnp.stack(parts)
    g_conv = jnp.concatenate([gcw_q, gcw_k, gcw_v], axis=1)
    n_cw = conv_w.shape[2]
    g_conv_parts = jnp.transpose(g_conv.reshape(8, N_DEV, n_cw), (1, 0, 2))
    in_g_fly = _exchange_start("reduce_in_start", [g_in_parts, g_conv_parts], [True] * 2, g_dtb_row)
    tmh1 = min(512, t)
    dh1 = _mm("d_h1", dproj, w_cat, dep=in_g_fly["token"], grid=(t // tmh1, d // 1024, 1),
              a_spec=pl.BlockSpec((tmh1, n_cat), lambda i, j, k: (i, 0)),
              b_spec=pl.BlockSpec((1024, n_cat), lambda i, j, k: (j, 0)),
              o_spec=pl.BlockSpec((tmh1, 1024), lambda i, j, k: (i, j)),
              out_shape=_sds((t, d), F32), ca=1, cb=1, nk=1)
    grad_x, _, g_attn_norm = _rms_bwd("norm1_bwd", x2, attn_norm_w, dh1, dx1)

    small_rows = [g_attn_norm.reshape(d // HD, HD), g_ffn_norm.reshape(d // HD, HD), g_dn, g_qn, g_kn, g_an,
                  g_alog_row, g_dtb_row, loss_tile[:1]]
    loss_row = sum(r.shape[0] for r in small_rows) - 1
    small_pack = _pad_rows(jnp.concatenate(small_rows, axis=0), 40)
    (r_down,) = _exchange_wait("reduce_down_wait", down_g_fly, grad_x)
    (r_gu,) = _exchange_wait("reduce_gate_up_wait", gu_g_fly, grad_x)
    (r_out,) = _exchange_wait("reduce_out_wait", out_g_fly, grad_x)
    res_gu = [a[None] for a in _adamw("adamw_gate_up", r_gu, w_gate_up[0], m_w_gate_up[0], v_w_gate_up[0])]
    res_down = [a[None] for a in _adamw("adamw_down", r_down, w_down[0], m_w_down[0], v_w_down[0])]
    res_out = [a[None] for a in _adamw("adamw_out", r_out, w_out[0], m_w_out[0], v_w_out[0])]
    done = (res_gu[3][0, :1, :1] + res_down[3][0, :1, :1] + res_out[3][0, :1, :1])
    r_in, r_conv = _exchange_wait("reduce_in_wait", in_g_fly, done)
    upd_in = _adamw("adamw_in", r_in, jnp.transpose(w_in[0]), jnp.transpose(m_w_in[0]), jnp.transpose(v_w_in[0]))
    res_in = [jnp.transpose(a)[None] for a in upd_in]
    (r_small,) = _exchange("gather_small_grads", [small_pack], [False], upd_in[0])

    def pack_small(an, fn, dn, qn_, kn_, aon, al, db):
        rows = [an.reshape(d // HD, HD), fn.reshape(d // HD, HD), dn, qn_, kn_, aon,
                _lane_row(al[0], 8), _lane_row(db[0], 8)]
        return _pad_rows(jnp.concatenate(rows, axis=0), 40)

    def unpack_small(pk):
        nr = d // HD
        return dict(attn_norm_w=pk[:nr].reshape(1, d), ffn_norm_w=pk[nr:2 * nr].reshape(1, d),
                    delta_out_norm_w=pk[2 * nr:2 * nr + 1], q_norm_w=pk[2 * nr + 1:2 * nr + 2],
                    k_norm_w=pk[2 * nr + 2:2 * nr + 3], attn_out_norm_w=pk[2 * nr + 3:2 * nr + 4],
                    a_log=pk[2 * nr + 4:2 * nr + 5, 8:16], dt_bias=pk[2 * nr + 5:2 * nr + 6, 8:16])

    res_small = _adamw("adamw_small", r_small,
                       pack_small(attn_norm_w, ffn_norm_w, delta_out_norm_w, q_norm_w, k_norm_w, attn_out_norm_w, a_log, dt_bias),
                       pack_small(m_attn_norm_w, m_ffn_norm_w, m_delta_out_norm_w, m_q_norm_w, m_k_norm_w, m_attn_out_norm_w, m_a_log, m_dt_bias),
                       pack_small(v_attn_norm_w, v_ffn_norm_w, v_delta_out_norm_w, v_q_norm_w, v_k_norm_w, v_attn_out_norm_w, v_a_log, v_dt_bias))
    small = [unpack_small(a) for a in res_small]
    res_conv =[a[None, :4] for a in _adamw("adamw_conv", r_conv, _pad_rows(conv_w[0], 8), _pad_rows(m_conv_w[0], 8),
                                            _pad_rows(v_conv_w[0], 8))]

    loss = jnp.sum(r_small[:, loss_row, 0])
    outs = [loss, grad_x[None]]
    for i in range(4):
        s = small[i]
        outs += [s["attn_norm_w"], res_in[i], res_conv[i], s["a_log"], s["dt_bias"], s["delta_out_norm_w"],
                 s["q_norm_w"], s["k_norm_w"], s["attn_out_norm_w"], res_out[i], s["ffn_norm_w"], res_gu[i],
                 res_down[i]]
    return tuple(outs)
```

```python
import numpy as np
import jax
import jax.numpy as jnp
from jax import lax
from jax.experimental import pallas as pl
from jax.experimental.pallas import tpu as pltpu

F32 = jnp.float32
BF16 = jnp.bfloat16

N_DEV = 8
N_HEADS = 8
HD = 128
GW = N_HEADS * HD
CHUNK = 64
PAIR = 2 * CHUNK
SCAN_CHUNKS = 4
SCAN_ROWS = SCAN_CHUNKS * CHUNK
SPAN = 128
DILATIONS = (1, 4, 16)
ROPE_THETA = 10000.0
EPS = 1e-6
D_FF = 5632
ADAM_LR, ADAM_B1, ADAM_B2, ADAM_EPS, ADAM_WD, ADAM_STEP = 0.001, 0.9, 0.999, 1e-8, 0.01, 10
NEG = -1e30
VMEM_LIMIT = 56 * 1024 * 1024
ANY = pl.BlockSpec(memory_space=pl.ANY)
HEADS_PER_STEP = 8


def _params(n_grid, vmem=VMEM_LIMIT):
    return pltpu.CompilerParams(dimension_semantics=("arbitrary",) * n_grid, vmem_limit_bytes=vmem)


def _sds(shape, dtype):
    return jax.ShapeDtypeStruct(tuple(shape), dtype)


def _sigmoid(x):
    return 1.0 / (1.0 + jnp.exp(-x))


def _silu(x):
    return x * _sigmoid(x)


def _softplus(x):
    return jnp.maximum(x, 0.0) + jnp.log(1.0 + jnp.exp(-jnp.abs(x)))


def _dot(a, b, ca, cb):
    return lax.dot_general(a, b, (((ca,), (cb,)), ((), ())), preferred_element_type=F32)


def _b16(x):
    return x if x.dtype == BF16 else x.astype(BF16)


def _split(x):
    hi = x.astype(BF16)
    return hi, (x - hi.astype(F32)).astype(BF16)


def _dot3(a, b, ca, cb):
    a_hi, a_lo = _split(a)
    b_hi, b_lo = _split(b)
    return _dot(a_hi, b_hi, ca, cb) + (_dot(a_hi, b_lo, ca, cb) + _dot(a_lo, b_hi, ca, cb))


def _iota2(shape, axis):
    return lax.broadcasted_iota(jnp.int32, shape, axis)


def _mm(name, a, b, *, grid, a_spec, b_spec, o_spec, out_shape, ca, cb, nk, dep=None):
    assert nk == 1 and grid[2] == 1

    def body(*refs):
        refs[-1][...] = _dot(_b16(refs[0][...]), _b16(refs[1][...]), ca, cb).astype(refs[-1].dtype)

    in_specs = [a_spec, b_spec] + ([ANY] if dep is not None else [])
    args = (a, b) + ((dep,) if dep is not None else ())
    return pl.pallas_call(body, grid=grid, in_specs=in_specs, out_specs=o_spec, out_shape=out_shape,
                          name=name, compiler_params=_params(3))(*args)


def _rms_f(xv, wv):
    return xv * lax.rsqrt(jnp.mean(xv * xv, axis=-1, keepdims=True) + EPS) * wv


def _rms_fwd(name, x, w, dep):
    t, d = x.shape
    tm = min(512, t)

    def body(x_ref, w_ref, dep_ref, o_ref):
        o_ref[...] = _rms_f(x_ref[...], w_ref[...]).astype(BF16)

    row = pl.BlockSpec((tm, d), lambda i: (i, 0))
    vec = pl.BlockSpec((1, d), lambda i: (0, 0))
    return pl.pallas_call(body, grid=(t // tm,), in_specs=[row, vec, ANY], out_specs=row,
                          out_shape=_sds((t, d), BF16), name=name, compiler_params=_params(1))(x, w, dep)


def _rms_bwd(name, x, w, dh, res):
    t, d = x.shape
    tm = min(256, t)

    def body(x_ref, w_ref, dh_ref, res_ref, dx_ref, dx16_ref, dw_ref):
        _, vjp = jax.vjp(_rms_f, x_ref[...], w_ref[...])
        dxv, dwv = vjp(dh_ref[...])
        dxv = dxv + res_ref[...]
        dx_ref[...] = dxv
        dx16_ref[...] = dxv.astype(BF16)

        @pl.when(pl.program_id(0) == 0)
        def _():
            dw_ref[...] = jnp.zeros_like(dw_ref)

        dw_ref[...] += dwv

    row = pl.BlockSpec((tm, d), lambda i: (i, 0))
    vec = pl.BlockSpec((1, d), lambda i: (0, 0))
    return pl.pallas_call(body, grid=(t // tm,), in_specs=[row, vec, row, row], out_specs=[row, row, vec],
                          out_shape=[_sds((t, d), F32), _sds((t, d), BF16), _sds((1, d), F32)], name=name,
                          compiler_params=_params(1))(x, w, dh, res)


def _shift_rows(x, s):
    t = x.shape[0]
    r = pltpu.roll(x, s % t, 0)
    row8 = _iota2((8, x.shape[1]), 0)
    if s > 0:
        return jnp.concatenate([jnp.where(row8 >= s, r[:8], 0.0), r[8:]], axis=0)
    return jnp.concatenate([r[:t - 8], jnp.where(row8 < 8 + s, r[t - 8:], 0.0)], axis=0)


def _conv_taps(xv, w_ref):
    c = w_ref[3:4, :] * xv
    for s in (1, 2, 3):
        c = c + w_ref[3 - s:4 - s, :] * _shift_rows(xv, s)
    return c


def _post_conv(c, l2, scale):
    y = _silu(c)
    if l2:
        y = y * lax.rsqrt(jnp.sum(y * y, axis=-1, keepdims=True) + EPS) * scale
    return y


def _conv_fwd(name, proj, conv_w8, group, l2, scale):
    t = proj.shape[0]

    def body(x_ref, w_ref, o_ref):
        o_ref[...] = _post_conv(_conv_taps(x_ref[...], w_ref), l2, scale)

    return pl.pallas_call(
        body, grid=(N_HEADS,),
        in_specs=[pl.BlockSpec((t, HD), lambda h: (0, h + group * N_HEADS)),
                  pl.BlockSpec((8, HD), lambda h: (0, h + group * N_HEADS))],
        out_specs=pl.BlockSpec((t, HD), lambda h: (0, h)),
        out_shape=_sds((t, GW), F32), name=name, compiler_params=_params(1, VMEM_LIMIT))(proj, conv_w8)


def _conv_bwd(name, proj, conv_w8, dn, dproj, group, l2, scale):
    t = proj.shape[0]

    def body(x_ref, w_ref, dn_ref, dproj_ref, dx_ref, dw_ref):
        xv = x_ref[...]
        c = _conv_taps(xv, w_ref)
        _, vjp = jax.vjp(lambda cc: _post_conv(cc, l2, scale), c)
        (dc,) = vjp(dn_ref[...])
        dx = w_ref[3:4, :] * dc
        dw = jnp.zeros((8, HD), F32)
        rid = _iota2((8, HD), 0)
        dw = dw + jnp.where(rid == 3, jnp.sum(dc * xv, axis=0, keepdims=True), 0.0)
        for s in (1, 2, 3):
            dx = dx + w_ref[3 - s:4 - s, :] * _shift_rows(dc, -s)
            dw = dw + jnp.where(rid == 3 - s, jnp.sum(dc * _shift_rows(xv, s), axis=0, keepdims=True), 0.0)
        dx_ref[...] = dx.astype(BF16)
        dw_ref[...] = dw

    return pl.pallas_call(
        body, grid=(N_HEADS,),
        in_specs=[pl.BlockSpec((t, HD), lambda h: (0, h + group * N_HEADS)),
                  pl.BlockSpec((8, HD), lambda h: (0, h + group * N_HEADS)),
                  pl.BlockSpec((t, HD), lambda h: (0, h)), ANY],
        out_specs=[pl.BlockSpec((t, HD), lambda h: (0, h + group * N_HEADS)), pl.BlockSpec((8, HD), lambda h: (0, h))],
        out_shape=[_sds(dproj.shape, BF16), _sds((8, GW), F32)], input_output_aliases={3: 0}, name=name,
        compiler_params=_params(1, VMEM_LIMIT))(proj, conv_w8, dn, dproj)


def _chunk_cumsum(g, rows):
    pos = rows % CHUNK
    s = 1
    while s < CHUNK:
        g = g + jnp.where(pos >= s, pltpu.roll(g, s, 0), 0.0)
        s *= 2
    return g


def _gates_fwd(name, proj, small_blk, alog_row, dtb_row):
    t = proj.shape[0]
    tm = min(512, t)

    def body(s_ref, a_ref, b_ref, beta_ref, gc_ref):
        sm = s_ref[...]
        beta = _sigmoid(sm)
        g = -jnp.exp(a_ref[...]) * _softplus(sm + b_ref[...])
        gc = _chunk_cumsum(g, _iota2((tm, HD), 0))
        lane = _iota2((tm, HD), 1)
        for h in range(N_HEADS):
            bcol = jnp.sum(jnp.where(lane == h, beta, 0.0), axis=1, keepdims=True)
            gcol = jnp.sum(jnp.where(lane == 8 + h, gc, 0.0), axis=1, keepdims=True)
            beta_ref[:, h * HD:(h + 1) * HD] = jnp.broadcast_to(bcol, (tm, HD))
            gc_ref[:, h * HD:(h + 1) * HD] = jnp.broadcast_to(gcol, (tm, HD))

    vec = pl.BlockSpec((1, HD), lambda i: (0, 0))
    wide = pl.BlockSpec((tm, GW), lambda i: (i, 0))
    return pl.pallas_call(
        body, grid=(t // tm,),
        in_specs=[pl.BlockSpec((tm, HD), lambda i: (i, small_blk)), vec, vec], out_specs=[wide, wide],
        out_shape=[_sds((t, GW), F32), _sds((t, GW), F32)], name=name,
        compiler_params=_params(1))(proj, alog_row, dtb_row)


def _gates_bwd(name, proj, small_blk, alog_row, dtb_row, dbeta_b, dg_b, dproj):
    t = proj.shape[0]
    tm = min(512, t)

    def body(s_ref, a_ref, b_ref, db_ref, dg_ref, dproj_ref, ds_ref, da_ref, dbias_ref):
        sm = s_ref[...]
        lane = _iota2((tm, HD), 1)
        db = jnp.zeros((tm, HD), F32)
        dg = jnp.zeros((tm, HD), F32)
        for h in range(N_HEADS):
            db = db + jnp.where(lane == h, db_ref[:, h * HD:(h + 1) * HD], 0.0)
            dg = dg + jnp.where(lane == 8 + h, dg_ref[:, h * HD:(h + 1) * HD], 0.0)
        beta = _sigmoid(sm)
        ea = jnp.exp(a_ref[...])
        pre = sm + b_ref[...]
        g = -ea * _softplus(pre)
        dpre = dg * (-ea) * _sigmoid(pre)
        ds_ref[...] = (db * beta * (1.0 - beta) + dpre).astype(BF16)

        @pl.when(pl.program_id(0) == 0)
        def _():
            da_ref[...] = jnp.zeros_like(da_ref)
            dbias_ref[...] = jnp.zeros_like(dbias_ref)

        da_ref[...] += jnp.sum(dg * g, axis=0, keepdims=True)
        dbias_ref[...] += jnp.sum(dpre, axis=0, keepdims=True)

    vec = pl.BlockSpec((1, HD), lambda i: (0, 0))
    wide = pl.BlockSpec((tm, GW), lambda i: (i, 0))
    return pl.pallas_call(
        body, grid=(t // tm,),
        in_specs=[pl.BlockSpec((tm, HD), lambda i: (i, small_blk)), vec, vec, wide, wide, ANY],
        out_specs=[pl.BlockSpec((tm, HD), lambda i: (i, small_blk)), vec, vec],
        out_shape=[_sds(dproj.shape, BF16), _sds((1, HD), F32), _sds((1, HD), F32)],
        input_output_aliases={5: 0}, name=name,
        compiler_params=_params(1))(proj, alog_row, dtb_row, dbeta_b, dg_b, dproj)


def _pair_masks():
    ii = _iota2((PAIR, PAIR), 0)
    jj = _iota2((PAIR, PAIR), 1)
    same = (ii // CHUNK) == (jj // CHUNK)
    return ii, jj, same & (ii >= jj), same & (ii > jj)


def _to_row(col_b, ii, jj):
    return jnp.sum(jnp.where(ii == jj, col_b, 0.0), axis=0, keepdims=True)


def _to_col(row, ii, jj):
    return jnp.sum(jnp.where(ii == jj, jnp.broadcast_to(row, (PAIR, PAIR)), 0.0), axis=1, keepdims=True)


def _decay_parts(gc, last_a, last_b, ii, jj, causal):
    diff = gc - _to_row(gc, ii, jj)
    dmat = jnp.where(causal, jnp.exp(jnp.where(causal, diff, 0.0)), 0.0)
    glast = jnp.where(ii < CHUNK, last_a, last_b)
    return dmat, jnp.exp(gc), jnp.exp(glast - gc)


def _unit_lower_inverse(lows, ii, jj):
    eye = jnp.where(ii == jj, 1.0, 0.0)
    mm = lambda xs, ys: [_dot3(a, b, 1, 0) for a, b in zip(xs, ys)]
    plus = lambda xs: [eye + a for a in xs]
    minus = lambda xs: [eye - a for a in xs]
    d1 = [jnp.where((ii // 16) == (jj // 16), low, 0.0) for low in lows]
    d2 = mm(d1, d1)
    a = mm(minus(d1), plus(d2))
    d4 = mm(d2, d2)
    a = mm(a, plus(d4))
    d8 = mm(d4, d4)
    td = mm(a, plus(d8))
    n1 = mm(td, [low - d for low, d in zip(lows, d1)])
    n2 = mm(n1, n1)
    return mm(mm(minus(n1), plus(n2)), td)


def _delta_prep(name, qn, kn, vv, beta_b, gc_b):
    t = qn.shape[0]

    def body(q_ref, k_ref, v_ref, b_ref, g_ref, u_ref, w_ref, p_ref, t_ref, qd_ref, kd_ref):
        ii, jj, causal, strict = _pair_masks()
        sls = [slice(hh * HD, (hh + 1) * HD) for hh in range(HEADS_PER_STEP)]
        lows = []
        for sl in sls:
            q, k, beta = q_ref[:, sl], k_ref[:, sl], b_ref[:, sl]
            dmat, gam, e2 = _decay_parts(g_ref[:, sl], g_ref[CHUNK - 1:CHUNK, sl], g_ref[PAIR - 1:PAIR, sl],
                                         ii, jj, causal)
            k16 = _b16(k)
            lows.append(jnp.where(strict, beta * _dot(k16, k16, 1, 1) * dmat, 0.0))
            p_ref[:, sl] = jnp.where(causal, _dot(_b16(q), k16, 1, 1) * dmat, 0.0).astype(BF16)
            qd_ref[:, sl] = (q * gam).astype(BF16)
            kd_ref[:, sl] = (k * e2).astype(BF16)
        for sl, tinv in zip(sls, _unit_lower_inverse(lows, ii, jj)):
            beta = b_ref[:, sl]
            t_ref[:, sl] = tinv
            u_ref[:, sl] = _dot3(tinv, v_ref[:, sl] * beta, 1, 0)
            w_ref[:, sl] = _dot3(tinv, k_ref[:, sl] * (beta * jnp.exp(g_ref[:, sl])), 1, 0).astype(BF16)

    blk = pl.BlockSpec((PAIR, HEADS_PER_STEP * HD), lambda i, h: (i, h))
    return pl.pallas_call(
        body, grid=(t // PAIR, N_HEADS // HEADS_PER_STEP), in_specs=[blk] * 5, out_specs=[blk] * 6,
        out_shape=[_sds((t, GW), F32), _sds((t, GW), BF16), _sds((t, GW), BF16), _sds((t, GW), F32),
                   _sds((t, GW), BF16), _sds((t, GW), BF16)],
        name=name, compiler_params=_params(2))(qn, kn, vv, beta_b, gc_b)


def _delta_scan(name, u, w, p, qd, kd, gc_b):
    t = u.shape[0]
    n = t // CHUNK

    def body(u_ref, w_ref, p_ref, qd_ref, kd_ref, g_ref, o_ref, vn_ref, sh_ref, state):
        @pl.when(pl.program_id(0) == 0)
        def _():
            state[...] = jnp.zeros_like(state)

        sls = [slice(h * HD, (h + 1) * HD) for h in range(N_HEADS)]
        heads = range(N_HEADS)
        s = [state[h] for h in heads]
        for c in range(SCAN_CHUNKS):
            rows = slice(c * CHUNK, (c + 1) * CHUNK)
            last = slice((c + 1) * CHUNK - 1, (c + 1) * CHUNK)
            for h in heads:
                sh_ref[c, h] = s[h]
            s16 = [_b16(a) for a in s]
            ws = [_dot(w_ref[rows, sls[h]], s16[h], 1, 0) for h in heads]
            qs = [_dot(qd_ref[rows, sls[h]], s16[h], 1, 0) for h in heads]
            vn16 = [_b16(u_ref[rows, sls[h]] - ws[h]) for h in heads]
            pv = [_dot(p_ref[rows, sls[h]], jnp.concatenate([vn16[h], vn16[h]], axis=0), 1, 0) for h in heads]
            kv = [_dot(kd_ref[rows, sls[h]], vn16[h], 0, 0) for h in heads]
            for h in heads:
                o_ref[rows, sls[h]] = qs[h] + pv[h]
                vn_ref[rows, sls[h]] = vn16[h]
            s = [s[h] * jnp.exp(g_ref[last, sls[h]]) + kv[h] for h in heads]
        for h in heads:
            state[h] = s[h]

    blk = pl.BlockSpec((SCAN_ROWS, GW), lambda i: (i, 0))
    return pl.pallas_call(
        body, grid=(t // SCAN_ROWS,), in_specs=[blk] * 6,
        out_specs=[blk, blk, pl.BlockSpec((SCAN_CHUNKS, N_HEADS, HD, HD), lambda i: (i, 0, 0, 0))],
        out_shape=[_sds((t, GW), F32), _sds((t, GW), BF16), _sds((n, N_HEADS, HD, HD), F32)],
        scratch_shapes=[pltpu.VMEM((N_HEADS, HD, HD), F32)], name=name,
        compiler_params=_params(1))(u, w, p, qd, kd, gc_b)


def _delta_scan_bwd(name, do, w, p, qd, kd, gc_b, vn, s_hist):
    t = do.shape[0]
    n = t // CHUNK

    def body(do_ref, w_ref, p_ref, qd_ref, kd_ref, g_ref, vn_ref, sh_ref,
             dvn_ref, dqd_ref, dkd_ref, dw_ref, ddec_ref, dstate):
        @pl.when(pl.program_id(0) == 0)
        def _():
            dstate[...] = jnp.zeros_like(dstate)

        sls = [slice(h * HD, (h + 1) * HD) for h in range(N_HEADS)]
        heads = range(N_HEADS)
        ds = [dstate[h] for h in heads]
        for c in reversed(range(SCAN_CHUNKS)):
            rows = slice(c * CHUNK, (c + 1) * CHUNK)
            last = slice((c + 1) * CHUNK - 1, (c + 1) * CHUNK)
            ds16 = [_b16(a) for a in ds]
            s16 = [_b16(sh_ref[c, h]) for h in heads]
            do16 = [_b16(do_ref[rows, sls[h]]) for h in heads]
            ptdo = [_dot(p_ref[rows, sls[h]], do16[h], 0, 0) for h in heads]
            kds = [_dot(kd_ref[rows, sls[h]], ds16[h], 1, 0) for h in heads]
            qdo = [_dot(qd_ref[rows, sls[h]], do16[h], 0, 0) for h in heads]
            for h in heads:
                dqd_ref[rows, sls[h]] = _dot(do16[h], s16[h], 1, 1)
                dkd_ref[rows, sls[h]] = _dot(vn_ref[rows, sls[h]], ds16[h], 1, 1)
            dvn = [ptdo[h][:CHUNK, :] + ptdo[h][CHUNK:, :] + kds[h] for h in heads]
            dvn16 = [_b16(a) for a in dvn]
            wdv = [_dot(w_ref[rows, sls[h]], dvn16[h], 0, 0) for h in heads]
            for h in heads:
                dvn_ref[rows, sls[h]] = dvn[h]
                dw_ref[rows, sls[h]] = -_dot(dvn16[h], s16[h], 1, 1)
                tot = jnp.sum(jnp.sum(sh_ref[c, h] * ds[h], axis=1, keepdims=True), axis=0, keepdims=True)
                ddec_ref[c * 8:(c + 1) * 8, sls[h]] = jnp.broadcast_to(tot, (8, HD))
            ds = [ds[h] * jnp.exp(g_ref[last, sls[h]]) + qdo[h] - wdv[h] for h in heads]
        for h in heads:
            dstate[h] = ds[h]

    npair = t // SCAN_ROWS
    blk = pl.BlockSpec((SCAN_ROWS, GW), lambda i: (npair - 1 - i, 0))
    return pl.pallas_call(
        body, grid=(npair,),
        in_specs=[blk] * 7 + [pl.BlockSpec((SCAN_CHUNKS, N_HEADS, HD, HD), lambda i: (npair - 1 - i, 0, 0, 0))],
        out_specs=[blk] * 4 + [pl.BlockSpec((8 * SCAN_CHUNKS, GW), lambda i: (npair - 1 - i, 0))],
        out_shape=[_sds((t, GW), F32)] * 4 + [_sds((n * 8, GW), F32)],
        scratch_shapes=[pltpu.VMEM((N_HEADS, HD, HD), F32)], name=name,
        compiler_params=_params(1))(do, w, p, qd, kd, gc_b, vn, s_hist)


def _delta_prep_bwd(name, qn, kn, vv, beta_b, gc_b, tinv, u, w, vn, do, dvn, dqd, dkd, dw, ddec):
    t = qn.shape[0]

    def body(q_ref, k_ref, v_ref, b_ref, g_ref, t_ref, u_ref, w_ref, vn_ref, do_ref, dvn_ref, dqd_ref,
             dkd_ref, dw_ref, ddec_ref, dq_ref, dk_ref, dv_ref, dbeta_ref, dg_ref):
        ii, jj, causal, strict = _pair_masks()
        suffix = ((ii // CHUNK) == (jj // CHUNK)) & (jj >= ii)
        first = ii < CHUNK
        rs = lambda a: jnp.sum(a, axis=1, keepdims=True)
        sls = [slice(hh * HD, (hh + 1) * HD) for hh in range(HEADS_PER_STEP)]
        xs = [_dot3(t_ref[:, sl], dvn_ref[:, sl], 0, 0) for sl in sls]
        ys = [_dot3(t_ref[:, sl], dw_ref[:, sl], 0, 0) for sl in sls]
        k16s = [_b16(k_ref[:, sl]) for sl in sls]
        kks = [_dot(k16, k16, 1, 1) for k16 in k16s]
        qks = [_dot(_b16(q_ref[:, sl]), k16, 1, 1) for sl, k16 in zip(sls, k16s)]
        dps = [jnp.where(causal, _dot(_b16(do_ref[:, sl]), vn_ref[:, sl], 1, 1), 0.0) for sl in sls]
        das = [-jnp.where(strict, _dot(_b16(x), _b16(u_ref[:, sl]), 1, 1) + _dot(_b16(y), w_ref[:, sl], 1, 1), 0.0)
               for sl, x, y in zip(sls, xs, ys)]
        for hh, sl in enumerate(sls):
            q, k, v, beta, gc = q_ref[:, sl], k_ref[:, sl], v_ref[:, sl], b_ref[:, sl], g_ref[:, sl]
            last_a, last_b = g_ref[CHUNK - 1:CHUNK, sl], g_ref[PAIR - 1:PAIR, sl]
            dmat, gam, e2 = _decay_parts(gc, last_a, last_b, ii, jj, causal)
            q16, k16 = _b16(q), k16s[hh]
            kk, qk, dp, x, y, da = kks[hh], qks[hh], dps[hh], xs[hh], ys[hh], das[hh]
            dqd, dkd = dqd_ref[:, sl], dkd_ref[:, sl]
            dpd16 = _b16(dp * dmat)
            dkk16 = _b16(da * beta * dmat)
            dq_ref[:, sl] = gam * dqd + _dot(dpd16, k16, 1, 0)
            dk_ref[:, sl] = (e2 * dkd + _dot(dpd16, q16, 0, 0) + beta * gam * y
                             + _dot(dkk16, k16, 1, 0) + _dot(dkk16, k16, 0, 0))
            dv_ref[:, sl] = beta * x
            dbeta = rs(v * x) + rs(k * gam * y) + rs(da * kk * dmat)
            dbeta_ref[:, sl] = jnp.broadcast_to(dbeta, (PAIR, HD))
            m = (dp * qk + da * beta * kk) * dmat
            dgam = rs(q * dqd) + rs(k * beta * y)
            de2 = rs(k * dkd)
            colsum = _to_col(jnp.sum(m, axis=0, keepdims=True), ii, jj)
            te2 = de2 * e2
            dgc = rs(m) - colsum + gam * dgam - te2
            tail_a = jnp.sum(jnp.where(first, te2, 0.0), axis=0, keepdims=True)
            tail_b = jnp.sum(jnp.where(first, 0.0, te2), axis=0, keepdims=True)
            dgc = dgc + jnp.where(ii == CHUNK - 1, tail_a + ddec_ref[0:1, sl] * jnp.exp(last_a), 0.0)
            dgc = dgc + jnp.where(ii == PAIR - 1, tail_b + ddec_ref[8:9, sl] * jnp.exp(last_b), 0.0)
            dgc_row = _to_row(dgc, ii, jj)
            dg = jnp.sum(jnp.where(suffix, jnp.broadcast_to(dgc_row, (PAIR, PAIR)), 0.0), axis=1, keepdims=True)
            dg_ref[:, sl] = jnp.broadcast_to(dg, (PAIR, HD))

    blk = pl.BlockSpec((PAIR, HEADS_PER_STEP * HD), lambda i, h: (i, h))
    return pl.pallas_call(
        body, grid=(t // PAIR, N_HEADS // HEADS_PER_STEP),
        in_specs=[blk] * 14 + [pl.BlockSpec((16, HEADS_PER_STEP * HD), lambda i, h: (i, h))], out_specs=[blk] * 5,
        out_shape=[_sds((t, GW), F32)] * 5, name=name,
        compiler_params=_params(2))(qn, kn, vv, beta_b, gc_b, tinv, u, w, vn, do, dvn, dqd, dkd, dw, ddec)


def _rope_tables(name, pos_col, inv_row):
    t = pos_col.shape[0]
    tm = min(1024, t)

    def body(pos_ref, inv_ref, cos_ref, sin_ref):
        ang = pos_ref[...].astype(F32) * inv_ref[...]
        lane = _iota2(ang.shape, 1)
        cos_ref[...] = jnp.cos(ang)
        sin_ref[...] = jnp.where(lane < HD // 2, -1.0, 1.0) * jnp.sin(ang)

    tab = pl.BlockSpec((tm, HD), lambda i: (i, 0))
    return pl.pallas_call(
        body, grid=(t // tm,), in_specs=[pl.BlockSpec((tm, 1), lambda i: (i, 0)), pl.BlockSpec((1, HD), lambda i: (0, 0))],
        out_specs=[tab, tab], out_shape=[_sds((t, HD), F32)] * 2, name=name,
        compiler_params=_params(1))(pos_col, inv_row)


def _head_rms(xh, wv):
    return xh * lax.rsqrt(jnp.mean(xh * xh, axis=-1, keepdims=True) + EPS) * wv


def _qk_fwd(name, proj, pair_blk, wq_row, wk_row, cos_t, sin_t):
    t = proj.shape[0]
    tm = min(512, t)

    def body(x_ref, wq_ref, wk_ref, cos_ref, sin_ref, q_ref, k_ref):
        cos, sin = cos_ref[...], sin_ref[...]
        for o_ref, w_ref, base in ((q_ref, wq_ref, 0), (k_ref, wk_ref, GW)):
            for h in range(N_HEADS):
                y = _head_rms(x_ref[:, base + h * HD:base + (h + 1) * HD], w_ref[...])
                o_ref[:, h * HD:(h + 1) * HD] = y * cos + pltpu.roll(y, HD // 2, 1) * sin

    vec = pl.BlockSpec((1, HD), lambda i: (0, 0))
    tab = pl.BlockSpec((tm, HD), lambda i: (i, 0))
    wide = pl.BlockSpec((tm, GW), lambda i: (i, 0))
    return pl.pallas_call(
        body, grid=(t // tm,),
        in_specs=[pl.BlockSpec((tm, 2 * GW), lambda i: (i, pair_blk)), vec, vec, tab, tab],
        out_specs=[wide, wide], out_shape=[_sds((t, GW), F32)] * 2, name=name,
        compiler_params=_params(1))(proj, wq_row, wk_row, cos_t, sin_t)


def _qk_bwd(name, proj, pair_blk, wq_row, wk_row, cos_t, sin_t, dq_full, dk_full, dproj):
    t = proj.shape[0]
    tm = min(512, t)

    def body(x_ref, wq_ref, wk_ref, cos_ref, sin_ref, dq_ref, dk_ref, dproj_ref, dx_ref, dwq_ref, dwk_ref):
        cos, sin = cos_ref[...], sin_ref[...]

        @pl.when(pl.program_id(0) == 0)
        def _():
            dwq_ref[...] = jnp.zeros_like(dwq_ref)
            dwk_ref[...] = jnp.zeros_like(dwk_ref)

        for dy_ref, w_ref, dw_ref, base in ((dq_ref, wq_ref, dwq_ref, 0), (dk_ref, wk_ref, dwk_ref, GW)):
            dw = jnp.zeros((1, HD), F32)
            for h in range(N_HEADS):
                dy = dy_ref[:, h * HD:(h + 1) * HD]
                dy = dy * cos - pltpu.roll(dy, HD // 2, 1) * sin
                _, vjp = jax.vjp(_head_rms, x_ref[:, base + h * HD:base + (h + 1) * HD], w_ref[...])
                dx, dwh = vjp(dy)
                dw = dw + dwh
                dx_ref[:, base + h * HD:base + (h + 1) * HD] = dx.astype(BF16)
            dw_ref[...] += dw

    vec = pl.BlockSpec((1, HD), lambda i: (0, 0))
    tab = pl.BlockSpec((tm, HD), lambda i: (i, 0))
    wide = pl.BlockSpec((tm, GW), lambda i: (i, 0))
    pair = pl.BlockSpec((tm, 2 * GW), lambda i: (i, pair_blk))
    return pl.pallas_call(
        body, grid=(t // tm,), in_specs=[pair, vec, vec, tab, tab, wide, wide, ANY],
        out_specs=[pair, vec, vec],
        out_shape=[_sds(dproj.shape, BF16), _sds((1, HD), F32), _sds((1, HD), F32)], input_output_aliases={7: 0},
        name=name, compiler_params=_params(1))(proj, wq_row, wk_row, cos_t, sin_t, dq_full, dk_full, dproj)


def _cast_into(name, x, dproj, blk_idx):
    t = x.shape[0]
    tm = min(512, t)

    def body(x_ref, dproj_ref, o_ref):
        o_ref[...] = x_ref[...].astype(BF16)

    return pl.pallas_call(
        body, grid=(t // tm,), in_specs=[pl.BlockSpec((tm, GW), lambda i: (i, 0)), ANY],
        out_specs=pl.BlockSpec((tm, GW), lambda i: (i, blk_idx)), out_shape=_sds(dproj.shape, BF16),
        input_output_aliases={1: 0}, name=name, compiler_params=_params(1))(x, dproj)


GROUP = SPAN * max(DILATIONS)
SCALE = HD ** -0.5
TILE_BATCH = 8


def _band_mask(lo):
    qi = _iota2((SPAN, 2 * SPAN), 0)
    ki = _iota2((SPAN, 2 * SPAN), 1)
    return (ki >= qi) & (ki <= qi + SPAN) & (ki >= lo)


def _tiles():
    return [(pi, r, u, rho) for pi, r in enumerate(DILATIONS) for rho in range(r) for u in range(GROUP // (SPAN * r))]


def _rows(r, u, rho):
    return pl.ds(u * SPAN * r + rho, SPAN, stride=r) if r > 1 else pl.ds(u * SPAN, SPAN)


def _attn_fwd(name, q, k, v, v_blk):
    t = q.shape[0]

    def body(qc_ref, kc_ref, vc_ref, kp_ref, vp_ref, ob_ref, lse_ref, o_scr, l_scr):
        mask_in = _band_mask(0)
        mask_edge = _band_mask(jnp.where(pl.program_id(0) == 0, SPAN, 0))
        tiles = _tiles()
        k_own = v_own = None
        for b0 in range(0, len(tiles), TILE_BATCH):
            work = []
            for pi, r, u, rho in tiles[b0:b0 + TILE_BATCH]:
                rows = _rows(r, u, rho)
                if u > 0:
                    k_prev, v_prev, mask = k_own, v_own, mask_in
                else:
                    prows = _rows(r, GROUP // (SPAN * r) - 1, rho)
                    k_prev, v_prev, mask = kp_ref[prows, :].astype(BF16), vp_ref[prows, :].astype(BF16), mask_edge
                k_own, v_own = kc_ref[rows, :].astype(BF16), vc_ref[rows, :].astype(BF16)
                work.append((pi, rows, mask, qc_ref[rows, :].astype(BF16), jnp.concatenate([k_prev, k_own], axis=0),
                             jnp.concatenate([v_prev, v_own], axis=0)))
            scores = [_dot(qt, kcat, 1, 1) for _, _, _, qt, kcat, _ in work]
            soft = []
            for (_, _, mask, _, _, _), s in zip(work, scores):
                s = jnp.where(mask, s * SCALE, NEG)
                m = jnp.max(s, axis=1, keepdims=True)
                p = jnp.exp(s - m)
                soft.append((m, _b16(p), jnp.sum(p, axis=1, keepdims=True)))
            outs = [_dot(p, vcat, 1, 0) for (_, p, _), (_, _, _, _, _, vcat) in zip(soft, work)]
            for (pi, rows, _, _, _, _), (m, _, den), o in zip(work, soft, outs):
                o_scr[pi, rows, :] = o / den
                l_scr[pi, rows, :] = jnp.broadcast_to(m + jnp.log(den), (SPAN, HD))
        step = 256
        for c in range(GROUP // step):
            sl = pl.ds(c * step, step)
            ob, lse = _merge([o_scr[i, sl, :] for i in range(3)], [l_scr[i, sl, :] for i in range(3)])
            ob_ref[sl, :] = ob
            lse_ref[sl, :] = lse

    cur = pl.BlockSpec((GROUP, HD), lambda g, h: (g, h))
    prev = pl.BlockSpec((GROUP, HD), lambda g, h: (jnp.maximum(g - 1, 0), h))
    vcur = pl.BlockSpec((GROUP, HD), lambda g, h: (g, v_blk * N_HEADS + h))
    vprev = pl.BlockSpec((GROUP, HD), lambda g, h: (jnp.maximum(g - 1, 0), v_blk * N_HEADS + h))
    return pl.pallas_call(
        body, grid=(t // GROUP, N_HEADS), in_specs=[cur, cur, vcur, prev, vprev], out_specs=[cur, cur],
        out_shape=[_sds((t, GW), F32), _sds((t, GW), F32)],
        scratch_shapes=[pltpu.VMEM((3, GROUP, HD), F32), pltpu.VMEM((3, GROUP, HD), F32)], name=name,
        compiler_params=_params(2))(q, k, v, k, v)


def _attn_bwd(name, q, k, v, v_blk, do, lse, delta):
    t = q.shape[0]
    ng = t // GROUP

    def probs(work):
        scores = [_dot(qt, kcat, 1, 1) for qt, _, _, _, kcat, _, _ in work]
        dps = [_dot(dot, vcat, 1, 1) for _, dot, _, _, _, vcat, _ in work]
        out = []
        for (_, _, lt, dlt, kcat, _, mask), s, dp in zip(work, scores, dps):
            wide = kcat.shape[0] // SPAN
            lw = jnp.concatenate([lt] * wide, axis=1) if wide > 1 else lt
            dw = jnp.concatenate([dlt] * wide, axis=1) if wide > 1 else dlt
            p = jnp.exp(jnp.where(mask, s * SCALE - lw, NEG))
            out.append((_b16(p * (dp - dw) * SCALE), _b16(p)))
        return out

    def body(qc_ref, kc_ref, vc_ref, doc_ref, lc_ref, dc_ref, kp_ref, vp_ref, qn_ref, don_ref, ln_ref, dn_ref,
             dq_ref, dk_ref, dv_ref):
        g = pl.program_id(0)
        mask_in = _band_mask(0)
        mask_edge = _band_mask(jnp.where(g == 0, SPAN, 0))
        dk_ref[...] = jnp.zeros_like(dk_ref)
        dv_ref[...] = jnp.zeros_like(dv_ref)
        tiles = _tiles()
        k_own = v_own = None
        for b0 in range(0, len(tiles), TILE_BATCH):
            where, work = [], []
            for pi, r, u, rho in tiles[b0:b0 + TILE_BATCH]:
                rows = _rows(r, u, rho)
                if u > 0:
                    prows, k_prev, v_prev, mask = _rows(r, u - 1, rho), k_own, v_own, mask_in
                else:
                    prows = _rows(r, GROUP // (SPAN * r) - 1, rho)
                    k_prev, v_prev, mask = kp_ref[prows, :].astype(BF16), vp_ref[prows, :].astype(BF16), mask_edge
                k_own, v_own = kc_ref[rows, :].astype(BF16), vc_ref[rows, :].astype(BF16)
                where.append((pi, u, rows, prows))
                work.append((qc_ref[rows, :].astype(BF16), doc_ref[rows, :].astype(BF16), lc_ref[rows, :], dc_ref[rows, :],
                             jnp.concatenate([k_prev, k_own], axis=0), jnp.concatenate([v_prev, v_own], axis=0), mask))
            dsp = probs(work)
            dqs = [_dot(ds, w[4], 1, 0) for (ds, _), w in zip(dsp, work)]
            dks = [_dot(ds, w[0], 0, 0) for (ds, _), w in zip(dsp, work)]
            dvs = [_dot(p, w[1], 0, 0) for (_, p), w in zip(dsp, work)]
            for (pi, u, rows, prows), dq_t, dk2, dv2 in zip(where, dqs, dks, dvs):
                if pi == 0:
                    dq_ref[rows, :] = dq_t
                else:
                    dq_ref[rows, :] += dq_t
                dk_ref[rows, :] += dk2[SPAN:, :]
                dv_ref[rows, :] += dv2[SPAN:, :]
                if u > 0:
                    dk_ref[prows, :] += dk2[:SPAN, :]
                    dv_ref[prows, :] += dv2[:SPAN, :]
        qi = _iota2((SPAN, SPAN), 0)
        ki = _iota2((SPAN, SPAN), 1)
        mask_next = (ki >= qi) & (ki < jnp.where(g == ng - 1, 0, SPAN))
        edge = [(r, rho) for r in DILATIONS for rho in range(r)]
        for b0 in range(0, len(edge), TILE_BATCH):
            where, work = [], []
            for r, rho in edge[b0:b0 + TILE_BATCH]:
                krows, qrows = _rows(r, GROUP // (SPAN * r) - 1, rho), _rows(r, 0, rho)
                where.append(krows)
                work.append((qn_ref[qrows, :].astype(BF16), don_ref[qrows, :].astype(BF16), ln_ref[qrows, :],
                             dn_ref[qrows, :], kc_ref[krows, :].astype(BF16), vc_ref[krows, :].astype(BF16), mask_next))
            dsp = probs(work)
            dks = [_dot(ds, w[0], 0, 0) for (ds, _), w in zip(dsp, work)]
            dvs = [_dot(p, w[1], 0, 0) for (_, p), w in zip(dsp, work)]
            for krows, dk1, dv1 in zip(where, dks, dvs):
                dk_ref[krows, :] += dk1
                dv_ref[krows, :] += dv1

    cur = pl.BlockSpec((GROUP, HD), lambda g, h: (g, h))
    prev = pl.BlockSpec((GROUP, HD), lambda g, h: (jnp.maximum(g - 1, 0), h))
    nxt = pl.BlockSpec((GROUP, HD), lambda g, h: (jnp.minimum(g + 1, ng - 1), h))
    vcur = pl.BlockSpec((GROUP, HD), lambda g, h: (g, v_blk * N_HEADS + h))
    vprev = pl.BlockSpec((GROUP, HD), lambda g, h: (jnp.maximum(g - 1, 0), v_blk * N_HEADS + h))
    return pl.pallas_call(
        body, grid=(ng, N_HEADS), in_specs=[cur, cur, vcur, cur, cur, cur, prev, vprev] + [nxt] * 4,
        out_specs=[cur] * 3,
        out_shape=[_sds((t, GW), F32)] * 3, name=name,
        compiler_params=_params(2))(q, k, v, do, lse, delta, k, v, q, do, lse, delta)


def _merge(os_, ls_):
    m = jnp.maximum(jnp.maximum(ls_[0], ls_[1]), ls_[2])
    ws = [jnp.exp(l - m) for l in ls_]
    tot = ws[0] + ws[1] + ws[2]
    ob = (ws[0] * os_[0] + ws[1] * os_[1] + ws[2] * os_[2]) / tot
    return ob, m + jnp.log(tot)


def _gated_norm(oa, z, wv):
    return _head_rms(oa, wv) * _silu(z)


def _mix_fwd(name, oa_raw, proj, z_blk, ob, w_dn, w_an):
    t = oa_raw.shape[0]
    tm = min(512, t)

    def body(oa_ref, z_ref, ob_ref, wd_ref, wa_ref, mix_ref):
        for h in range(N_HEADS):
            sl = slice(h * HD, (h + 1) * HD)
            mix_ref[:, sl] = _gated_norm(oa_ref[:, sl], z_ref[:, sl], wd_ref[...]).astype(BF16)
            mix_ref[:, GW + h * HD:GW + (h + 1) * HD] = _head_rms(ob_ref[:, sl], wa_ref[...]).astype(BF16)

    vec = pl.BlockSpec((1, HD), lambda i: (0, 0))
    wide = pl.BlockSpec((tm, GW), lambda i: (i, 0))
    return pl.pallas_call(
        body, grid=(t // tm,),
        in_specs=[wide, pl.BlockSpec((tm, GW), lambda i: (i, z_blk)), wide, vec, vec],
        out_specs=pl.BlockSpec((tm, 2 * GW), lambda i: (i, 0)),
        out_shape=_sds((t, 2 * GW), BF16), name=name,
        compiler_params=_params(1))(oa_raw, proj, ob, w_dn, w_an)


def _mix_bwd(name, dx1_16, w_out, oa_raw, proj, z_blk, ob, w_dn, w_an, dep):
    t, d = dx1_16.shape
    tm = min(512, t)

    def body(dx_ref, wo_ref, oa_ref, z_ref, ob_ref, wd_ref, wa_ref, dep_ref,
             doa_ref, dz_ref, dob_ref, dl_ref, dwd_ref, dwa_ref):
        dwd = jnp.zeros((1, HD), F32)
        dwa = jnp.zeros((1, HD), F32)
        dxv = dx_ref[...]
        pairs = [_dot(dxv, wo_ref[2 * p * HD:2 * (p + 1) * HD, :], 1, 1) for p in range(N_HEADS)]
        heads = [half for pr in pairs for half in (pr[:, :HD], pr[:, HD:])]
        dm_a, dm_b = heads[:N_HEADS], heads[N_HEADS:]
        for h in range(N_HEADS):
            sl = slice(h * HD, (h + 1) * HD)
            _, vjp = jax.vjp(_gated_norm, oa_ref[:, sl], z_ref[:, sl], wd_ref[...])
            doa, dz, dw1 = vjp(dm_a[h])
            doa_ref[:, sl] = doa
            dz_ref[:, sl] = dz.astype(BF16)
            dwd = dwd + dw1
            obh = ob_ref[:, sl]
            _, vjp2 = jax.vjp(_head_rms, obh, wa_ref[...])
            dob, dw2 = vjp2(dm_b[h])
            dwa = dwa + dw2
            dob_ref[:, sl] = dob
            dl_ref[:, sl] = jnp.broadcast_to(jnp.sum(dob * obh, axis=1, keepdims=True), (tm, HD))

        @pl.when(pl.program_id(0) == 0)
        def _():
            dwd_ref[...] = jnp.zeros_like(dwd_ref)
            dwa_ref[...] = jnp.zeros_like(dwa_ref)

        dwd_ref[...] += dwd
        dwa_ref[...] += dwa

    vec = pl.BlockSpec((1, HD), lambda i: (0, 0))
    wide = pl.BlockSpec((tm, GW), lambda i: (i, 0))
    return pl.pallas_call(
        body, grid=(t // tm,),
        in_specs=[pl.BlockSpec((tm, d), lambda i: (i, 0)), pl.BlockSpec((2 * GW, d), lambda i: (0, 0)), wide,
                  pl.BlockSpec((tm, GW), lambda i: (i, z_blk)), wide, vec, vec, ANY],
        out_specs=[wide, pl.BlockSpec((tm, GW), lambda i: (i, z_blk)), wide, wide, vec, vec],
        out_shape=[_sds((t, GW), F32), _sds(proj.shape, BF16), _sds((t, GW), F32), _sds((t, GW), F32),
                   _sds((1, HD), F32), _sds((1, HD), F32)], name=name,
        compiler_params=_params(1))(dx1_16, w_out, oa_raw, proj, ob, w_dn, w_an, dep)


def _halves(n):
    cut = (n // 256) * 128
    return [(0, cut), (cut, n)]


def _gate_up_swiglu(name, h2, w_gu_g):
    t, d = h2.shape
    n = w_gu_g.shape[2]
    per = N_DEV // 2
    tm = min(512, t)

    def body(a_ref, bg_ref, bu_ref, gu_ref, act_ref):
        a = a_ref[...]
        cuts = _halves(n)
        gs = [_dot(a, bg_ref[:, c0:c1], 1, 0) for c0, c1 in cuts]
        ups = [_dot(a, bu_ref[:, c0:c1], 1, 0) for c0, c1 in cuts]
        for (c0, c1), g, up in zip(cuts, gs, ups):
            gu_ref[0, :, c0:c1] = g.astype(BF16)
            gu_ref[1, :, c0:c1] = up.astype(BF16)
            act_ref[:, c0:c1] = (_silu(g) * up).astype(BF16)

    return pl.pallas_call(
        body, grid=(per, t // tm),
        in_specs=[pl.BlockSpec((tm, d), lambda j, i: (i, 0)), pl.BlockSpec((None, d, n), lambda j, i: (j, 0, 0)),
                  pl.BlockSpec((None, d, n), lambda j, i: (j + per, 0, 0))],
        out_specs=[pl.BlockSpec((2, tm, n), lambda j, i: (0, i, j)), pl.BlockSpec((tm, n), lambda j, i: (i, j))],
        out_shape=[_sds((2, t, per * n), BF16), _sds((t, per * n), BF16)], name=name,
        compiler_params=_params(2))(h2, w_gu_g, w_gu_g)


def _d_gate_up(name, dy16, w_down, gu3, dep):
    t, d = dy16.shape
    f = w_down.shape[0]
    tm, tn = min(1024, t), f // 4

    def body(a_ref, b_ref, g_ref, dep_ref, o_ref):
        a = a_ref[...]
        cuts = _halves(tn)
        dacts = [_dot(a, b_ref[c0:c1, :], 1, 1) for c0, c1 in cuts]
        for (c0, c1), dact in zip(cuts, dacts):
            g, up = g_ref[0, :, c0:c1].astype(F32), g_ref[1, :, c0:c1].astype(F32)
            sg = _sigmoid(g)
            o_ref[0, :, c0:c1] = (dact * up * sg * (1.0 + g * (1.0 - sg))).astype(BF16)
            o_ref[1, :, c0:c1] = (dact * g * sg).astype(BF16)

    return pl.pallas_call(
        body, grid=(f // tn, t // tm),
        in_specs=[pl.BlockSpec((tm, d), lambda j, i: (i, 0)), pl.BlockSpec((tn, d), lambda j, i: (j, 0)),
                  pl.BlockSpec((2, tm, tn), lambda j, i: (0, i, j)), ANY],
        out_specs=pl.BlockSpec((2, tm, tn), lambda j, i: (0, i, j)), out_shape=_sds((2, t, f), BF16), name=name,
        compiler_params=_params(2))(dy16, w_down, gu3, dep)


def _d_h2(name, dgu3, w_gu_g, dep):
    _, t, f = dgu3.shape
    n_dev, d, n = w_gu_g.shape
    per = n_dev // 2
    tm, tn = min(512, t), 512

    def body(g_ref, u_ref, b_ref, dep_ref, o_ref):
        acc = None
        for s in range(n_dev):
            a_ref = g_ref if s < per else u_ref
            part = _dot(a_ref[:, (s % per) * n:(s % per + 1) * n], b_ref[s], 1, 1)
            acc = part if acc is None else acc + part
        o_ref[...] = acc

    return pl.pallas_call(
        body, grid=(d // tn, t // tm),
        in_specs=[pl.BlockSpec((None, tm, f), lambda j, i: (0, i, 0)), pl.BlockSpec((None, tm, f), lambda j, i: (1, i, 0)),
                  pl.BlockSpec((n_dev, tn, n), lambda j, i: (0, j, 0)), ANY],
        out_specs=pl.BlockSpec((tm, tn), lambda j, i: (i, j)), out_shape=_sds((t, d), F32), name=name,
        compiler_params=_params(2))(dgu3, dgu3, w_gu_g, dep)


def _out_proj_norm(name, mixed, w_out, x, w_norm):
    t, d = x.shape
    kdim = mixed.shape[1]
    tm = min(512, t)

    def body(a_ref, b_ref, x_ref, w_ref, x1_ref, h_ref):
        x1 = x_ref[...] + _dot(a_ref[...], b_ref[...], 1, 0)
        x1_ref[...] = x1
        h_ref[...] = _rms_f(x1, w_ref[...]).astype(BF16)

    row = pl.BlockSpec((tm, d), lambda i: (i, 0))
    return pl.pallas_call(
        body, grid=(t // tm,),
        in_specs=[pl.BlockSpec((tm, kdim), lambda i: (i, 0)), pl.BlockSpec((kdim, d), lambda i: (0, 0)), row,
                  pl.BlockSpec((1, d), lambda i: (0, 0))],
        out_specs=[row, row], out_shape=[_sds((t, d), F32), _sds((t, d), BF16)], name=name,
        compiler_params=_params(1))(mixed, w_out, x, w_norm)


def _down_loss(name, act, w_down, x1, target):
    t, f = act.shape
    d = x1.shape[1]
    tm, tn = min(1024, t), 512

    def body(a_ref, b_ref, x_ref, t_ref, dy_ref, dy16_ref, l_ref):
        diff = _dot(a_ref[...], b_ref[...], 1, 0) + x_ref[...] - t_ref[...]
        dyv = diff * (1.0 / d)
        dy_ref[...] = dyv
        dy16_ref[...] = dyv.astype(BF16)
        tot = jnp.sum(jnp.sum(diff * diff, axis=1, keepdims=True), axis=0, keepdims=True) * (0.5 / d)

        @pl.when((pl.program_id(0) == 0) & (pl.program_id(1) == 0))
        def _():
            l_ref[...] = jnp.zeros_like(l_ref)

        l_ref[...] += jnp.broadcast_to(tot, (8, 128))

    tile = pl.BlockSpec((tm, tn), lambda i, j: (i, j))
    return pl.pallas_call(
        body, grid=(t // tm, d // tn),
        in_specs=[pl.BlockSpec((tm, f), lambda i, j: (i, 0)), pl.BlockSpec((f, tn), lambda i, j: (0, j)), tile, tile],
        out_specs=[tile, tile, pl.BlockSpec((8, 128), lambda i, j: (0, 0))],
        out_shape=[_sds((t, d), F32), _sds((t, d), BF16), _sds((8, 128), F32)], name=name,
        compiler_params=_params(2))(act, w_down, x1, target)


def _peer(me, k):
    pid = (me + k) % N_DEV
    return (pid // 4, (pid // 2) % 2, pid % 2)


def _my_id():
    return 4 * lax.axis_index("x") + 2 * lax.axis_index("y") + lax.axis_index("c")


def _exchange(name, arrays, scatter, dep):
    n = len(arrays)

    def body(*refs):
        ins, outs = refs[:n], refs[n + 1:2 * n + 1]
        send_sems, recv_sems, local_sems = refs[2 * n + 1:]
        me = _my_id()
        started = []
        for a in range(n):
            src = ins[a].at[me] if scatter[a] else ins[a]
            loc = pltpu.make_async_copy(src, outs[a].at[me], local_sems.at[a])
            loc.start()
            started.append(loc)
        remote = []
        for k in range(1, N_DEV):
            to = (me + k) % N_DEV
            for a in range(n):
                src = ins[a].at[to] if scatter[a] else ins[a]
                cp = pltpu.make_async_remote_copy(src_ref=src, dst_ref=outs[a].at[me],
                                                  send_sem=send_sems.at[a * (N_DEV - 1) + k - 1], recv_sem=recv_sems.at[a * (N_DEV - 1) + k - 1],
                                                  device_id=_peer(me, k), device_id_type=pl.DeviceIdType.MESH)
                cp.start()
                remote.append(cp)
        for k in range(1, N_DEV):
            frm = (me + N_DEV - k) % N_DEV
            for a in range(n):
                src = ins[a].at[frm] if scatter[a] else ins[a]
                pltpu.make_async_remote_copy(src_ref=src, dst_ref=outs[a].at[frm],
                                             send_sem=send_sems.at[a * (N_DEV - 1) + k - 1], recv_sem=recv_sems.at[a * (N_DEV - 1) + k - 1],
                                             device_id=_peer(me, k), device_id_type=pl.DeviceIdType.MESH).wait_recv()
        for cp in remote:
            cp.wait_send()
        for loc in started:
            loc.wait()

    out_shape = [_sds((N_DEV,) + (a.shape[1:] if sc else a.shape), a.dtype) for a, sc in zip(arrays, scatter)]
    return pl.pallas_call(
        body, in_specs=[ANY] * (n + 1), out_specs=[ANY] * n, out_shape=out_shape,
        scratch_shapes=[pltpu.SemaphoreType.DMA((n * (N_DEV - 1),)), pltpu.SemaphoreType.DMA((n * (N_DEV - 1),)),
                        pltpu.SemaphoreType.DMA((n,))],
        name=name)(*arrays, dep)


def _gather_two_level(name, arrays):
    n = len(arrays)
    per = N_DEV - 1
    units = []
    for a, arr in enumerate(arrays):
        cuts = 4 if arr.shape[0] % 64 == 0 and arr.shape[0] >= 1024 else 1
        units += [(a, p * (arr.shape[0] // cuts), arr.shape[0] // cuts) for p in range(cuts)]
    nu = len(units)

    def body(*refs):
        ins, outs = refs[:n], refs[n:2 * n]
        send_sems, recv_sems, local_sems = refs[2 * n:]
        x, y, c = lax.axis_index("x"), lax.axis_index("y"), lax.axis_index("c")
        me, sibling = (x, y, c), (x, y, 1 - c)
        flip = lambda v, on: v + on - 2 * v * on
        relayed = (flip(x, c), flip(y, 1 - c), c)
        other = (flip(x, 1 - c), flip(y, c), c)
        diagonal = (1 - x, 1 - y, c)
        k_relayed, k_other = 2 - c, 1 + c

        def copy(u, k, block, to, from_input=False):
            a, r0, nr = units[u]
            slot = outs[a].at[4 * block[0] + 2 * block[1] + block[2], pl.ds(r0, nr)]
            return pltpu.make_async_remote_copy(
                src_ref=ins[a].at[pl.ds(r0, nr)] if from_input else slot, dst_ref=slot,
                send_sem=send_sems.at[u * per + k], recv_sem=recv_sems.at[u * per + k], device_id=to,
                device_id_type=pl.DeviceIdType.MESH)

        mine = [pltpu.make_async_copy(ins[a], outs[a].at[4 * x + 2 * y + c], local_sems.at[a]) for a in range(n)]
        for cp in mine:
            cp.start()
        sent = [copy(u, 1, me, (1 - x, y, c), True) for u in range(nu)]
        sent += [copy(u, 2, me, (x, 1 - y, c), True) for u in range(nu)]
        sent += [copy(u, 0, me, sibling, True) for u in range(nu)]
        for cp in sent:
            cp.start()
        for u in range(nu):
            copy(u, k_relayed, relayed, me).wait_recv()
            sent.append(copy(u, 3, relayed, other))
            sent.append(copy(u, 3 + k_relayed, relayed, sibling))
            sent[-2].start()
            sent[-1].start()
        for u in range(nu):
            copy(u, k_other, other, me).wait_recv()
            sent.append(copy(u, 3 + k_other, other, sibling))
            sent[-1].start()
        for u in range(nu):
            copy(u, 3, diagonal, me).wait_recv()
            sent.append(copy(u, 6, diagonal, sibling))
            sent[-1].start()
        for u in range(nu):
            copy(u, 0, sibling, me).wait_recv()
            for j, chip in enumerate([(1 - x, y), (x, 1 - y), (1 - x, 1 - y)]):
                copy(u, 4 + j, (*chip, 1 - c), me).wait_recv()
        for cp in sent:
            cp.wait_send()
        for cp in mine:
            cp.wait()

    return pl.pallas_call(
        body, in_specs=[ANY] * n, out_specs=[ANY] * n,
        out_shape=[_sds((N_DEV,) + a.shape, a.dtype) for a in arrays],
        scratch_shapes=[pltpu.SemaphoreType.DMA((nu * per,)), pltpu.SemaphoreType.DMA((nu * per,)),
                        pltpu.SemaphoreType.DMA((n,))],
        name=name)(*arrays)


HBM = pl.BlockSpec(memory_space=pltpu.HBM)
SEM = pl.BlockSpec(memory_space=pltpu.SEMAPHORE)
EFFECT = pltpu.SideEffectType.DATAFLOW_SIDE_EFFECTING


def _remote_copies(srcs, lands, scatter, send_sems, recv_sems, me, incoming):
    out = []
    for k in range(1, N_DEV):
        other = (me + N_DEV - k) % N_DEV if incoming else (me + k) % N_DEV
        for a in range(len(srcs)):
            sem = a * (N_DEV - 1) + k - 1
            src = srcs[a].at[other] if scatter[a] else srcs[a]
            dst = lands[a].at[other if incoming else me]
            out.append(pltpu.make_async_remote_copy(src_ref=src, dst_ref=dst, send_sem=send_sems.at[sem],
                                                    recv_sem=recv_sems.at[sem], device_id=_peer(me, k),
                                                    device_id_type=pl.DeviceIdType.MESH))
    return out


def _exchange_start(name, arrays, scatter, dep):
    n = len(arrays)
    lands = [lax.empty((N_DEV,) + (a.shape[1:] if sc else a.shape), a.dtype) for a, sc in zip(arrays, scatter)]

    def body(*refs):
        srcs, land_refs = refs[:n], refs[n:2 * n]
        send_sems, recv_sems = refs[2 * n + 1], refs[2 * n + 2]
        token = refs[-1]
        for cp in _remote_copies(srcs, land_refs, scatter, send_sems, recv_sems, _my_id(), False):
            cp.start()
        token[...] = jnp.zeros_like(token)

    n_sem = n * (N_DEV - 1)
    out_shape = ([pltpu.SemaphoreType.DMA((n_sem,)), pltpu.SemaphoreType.DMA((n_sem,))]
                 + [pltpu.HBM(a.shape, a.dtype) for a in arrays] + [pltpu.HBM(l.shape, l.dtype) for l in lands]
                 + [_sds((8, 128), F32)])
    aliases = {i: 2 + i for i in range(2 * n)}
    args = [pltpu.with_memory_space_constraint(a, pltpu.HBM) for a in list(arrays) + lands] + [dep]
    res = pl.pallas_call(
        body, name=name, in_specs=[HBM] * (2 * n) + [ANY], out_shape=out_shape,
        out_specs=[SEM, SEM] + [HBM] * (2 * n) + [pl.BlockSpec(memory_space=pltpu.VMEM)],
        input_output_aliases=aliases, compiler_params=pltpu.CompilerParams(has_side_effects=EFFECT))(*args)
    return dict(send=res[0], recv=res[1], srcs=res[2:2 + n], lands=res[2 + n:2 + 2 * n], token=res[-1],
                scatter=scatter)


def _exchange_wait(name, started, after):
    n = len(started["srcs"])
    scatter = started["scatter"]

    def body(*refs):
        srcs, land_refs = refs[:n], refs[n:2 * n]
        send_sems, recv_sems = refs[2 * n], refs[2 * n + 1]
        me = _my_id()
        for cp in _remote_copies(srcs, land_refs, scatter, send_sems, recv_sems, me, False):
            cp.wait_send()
        for cp in _remote_copies(srcs, land_refs, scatter, send_sems, recv_sems, me, True):
            cp.wait_recv()

    arrs = list(started["srcs"]) + list(started["lands"])
    res = pl.pallas_call(
        body, name=name, in_specs=[HBM] * (2 * n) + [SEM, SEM, ANY],
        out_shape=[pltpu.HBM(a.shape, a.dtype) for a in arrs], out_specs=[HBM] * (2 * n),
        input_output_aliases={i: i for i in range(2 * n)},
        compiler_params=pltpu.CompilerParams(has_side_effects=EFFECT))(*arrs, started["send"], started["recv"], after)
    me = _my_id()
    out = []
    for src, land, sc in zip(res[:n], res[n:], scatter):
        own = lax.dynamic_index_in_dim(src, me, 0, keepdims=True) if sc else src[None]
        out.append(lax.dynamic_update_slice(land, own, (me,) + (0,) * (land.ndim - 1)))
    return out


def _adamw(name, parts, w, m, v):
    r, c = w.shape
    tr, tc = r, c
    if r % 8 == 0:
        tr = next(cand for cand in (128, 88, 64, 40, 8) if r % cand == 0)
    else:
        tc = 256
    c1 = 1.0 / (1.0 - ADAM_B1 ** ADAM_STEP)
    c2 = 1.0 / (1.0 - ADAM_B2 ** ADAM_STEP)

    def body(p_ref, w_ref, m_ref, v_ref, g_ref, d_ref, nm_ref, nv_ref):
        g = p_ref[0].astype(F32)
        for s in range(1, N_DEV):
            g = g + p_ref[s].astype(F32)
        mn = ADAM_B1 * m_ref[...] + (1.0 - ADAM_B1) * g
        vn = ADAM_B2 * v_ref[...] + (1.0 - ADAM_B2) * (g * g)
        g_ref[...] = g
        nm_ref[...] = mn
        nv_ref[...] = vn
        d_ref[...] = -ADAM_LR * ((mn * c1) / (jnp.sqrt(vn * c2) + ADAM_EPS) + ADAM_WD * w_ref[...])

    blk = pl.BlockSpec((tr, tc), lambda i, j: (i, j))
    return pl.pallas_call(
        body, grid=(r // tr, c // tc),
        in_specs=[pl.BlockSpec((N_DEV, tr, tc), lambda i, j: (0, i, j)), blk, blk, blk],
        out_specs=[blk] * 4, out_shape=[_sds((r, c), F32)] * 4, name=name,
        compiler_params=_params(2, VMEM_LIMIT))(parts, w, m, v)


def _pad_rows(a, rows):
    return jnp.pad(a, ((0, rows - a.shape[0]), (0, 0)))


def _lane_row(vec8, offset):
    return jnp.pad(vec8.reshape(1, 8), ((0, 0), (offset, HD - 8 - offset)))


def kernel(x, positions, attn_norm_w, w_in, conv_w, a_log, dt_bias, delta_out_norm_w, q_norm_w, k_norm_w, attn_out_norm_w, w_out, ffn_norm_w, w_gate_up, w_down, loss_target, m_attn_norm_w, m_w_in, m_conv_w, m_a_log, m_dt_bias, m_delta_out_norm_w, m_q_norm_w, m_k_norm_w, m_attn_out_norm_w, m_w_out, m_ffn_norm_w, m_w_gate_up, m_w_down, v_attn_norm_w, v_w_in, v_conv_w, v_a_log, v_dt_bias, v_delta_out_norm_w, v_q_norm_w, v_k_norm_w, v_attn_out_norm_w, v_w_out, v_ffn_norm_w, v_w_gate_up, v_w_down):
    x2 = x[0]
    t, d = x2.shape
    target = loss_target[0]
    pos_col = positions.reshape(t, 1)
    half = HD // 2
    inv = (ROPE_THETA ** (-np.arange(half, dtype=np.float32) / half)).astype(np.float32)
    inv_row = jnp.asarray(np.concatenate([inv, inv]).reshape(1, HD))

    n_in = w_in.shape[2]
    n_gu = w_gate_up.shape[2]
    w_in_g, conv_g = _gather_two_level("gather_in", [w_in[0].astype(BF16), _pad_rows(conv_w[0], 8)])
    out_fly = _exchange_start("gather_out_start", [w_out[0].astype(BF16)], [False], conv_g)
    gu_fly = _exchange_start("gather_gate_up_start", [w_gate_up[0].astype(BF16)], [False], out_fly["token"])
    down_fly = _exchange_start("gather_down_start", [w_down[0].astype(BF16)], [False], gu_fly["token"])
    n_main = 4 * GW
    n_small = 2 * N_HEADS
    segments = [(0, n_main, 0), (n_main + n_small, N_DEV * n_in, n_main), (n_main, n_main + n_small, 7 * GW)]
    pieces = []
    for lo, hi, _ in segments:
        f = lo
        while f < hi:
            j = f // n_in
            end = min(hi, (j + 1) * n_in)
            pieces.append(w_in_g[j][:, f - j * n_in:end - j * n_in])
            f = end
    w_cat = jnp.concatenate(pieces + [jnp.zeros((d, HD - n_small), BF16)], axis=1)
    n_cat = w_cat.shape[1]
    small_blk = (7 * GW) // HD
    conv_w8 =jnp.transpose(conv_g, (1, 0, 2)).reshape(8, 3 * GW)
    alog_row = _lane_row(a_log[0], 8)
    dtb_row = _lane_row(dt_bias[0], 8)

    tm = min(2048, t)
    h1 = _rms_fwd("norm1", x2, attn_norm_w, down_fly["token"])
    tmp, tnp = min(1024, t), n_cat // 3
    proj = _mm("in_proj", h1, w_cat, grid=(t // tmp, n_cat // tnp, 1),
               a_spec=pl.BlockSpec((tmp, d), lambda i, j, k: (i, 0)),
               b_spec=pl.BlockSpec((d, tnp), lambda i, j, k: (0, j)),
               o_spec=pl.BlockSpec((tmp, tnp), lambda i, j, k: (i, j)),
               out_shape=_sds((t, n_cat), F32), ca=1, cb=0, nk=1)
    qn = _conv_fwd("conv_q", proj, conv_w8, 0, True, HD ** -0.5)
    kn = _conv_fwd("conv_k", proj, conv_w8, 1, True, 1.0)
    vv = _conv_fwd("conv_v", proj, conv_w8, 2, False, 1.0)
    beta_b, gc_b = _gates_fwd("gates", proj, small_blk, alog_row, dtb_row)
    u, w, p, tinv, qd, kd = _delta_prep("delta_prep", qn, kn, vv, beta_b, gc_b)
    oa_raw, vn, s_hist = _delta_scan("delta_scan", u, w, p, qd, kd, gc_b)

    cos_t, sin_t = _rope_tables("rope_tables", pos_col, inv_row)
    aq, ak = _qk_fwd("attn_qk", proj, 2, q_norm_w, k_norm_w, cos_t, sin_t)
    ob, lse = _attn_fwd("attn_fwd", aq, ak, proj, 6)
    mixed = _mix_fwd("mix", oa_raw, proj, 3, ob, delta_out_norm_w, attn_out_norm_w)
    (w_out_g,) = _exchange_wait("gather_out_wait", out_fly, mixed)
    w_out_full = w_out_g.reshape(2 * GW, d)
    tn = 512
    x1, h2 = _out_proj_norm("out_proj", mixed, w_out_full, x2, ffn_norm_w)
    per = N_DEV // 2
    (w_gu_g,) = _exchange_wait("gather_gate_up_wait", gu_fly, h2)
    gu3, act = _gate_up_swiglu("gate_up", h2, w_gu_g)
    (w_down_g,) = _exchange_wait("gather_down_wait", down_fly, act)
    w_down_full = w_down_g.reshape(D_FF, d)
    tmd = min(1024, t)
    dy, dy16, loss_tile = _down_loss("down_proj", act, w_down_full, x1, target)

    tk, nkt = t, 1
    g_down = _mm("g_down", act, dy16, grid=(D_FF // 1408, d // 512, nkt),
                 a_spec=pl.BlockSpec((tk, 1408), lambda i, j, k: (k, i)),
                 b_spec=pl.BlockSpec((tk, 512), lambda i, j, k: (k, j)),
                 o_spec=pl.BlockSpec((1408, 512), lambda i, j, k: (i, j)),
                 out_shape=_sds((D_FF, d), F32), ca=0, cb=0, nk=nkt)
    down_g_fly = _exchange_start("reduce_down_start", [g_down.reshape(N_DEV, D_FF // N_DEV, d)], [True], dy16)
    dgu3 = _d_gate_up("d_gate_up", dy16, w_down_full, gu3, down_g_fly["token"])
    g_gu = _mm("g_gate_up", h2, dgu3, grid=(d // 512, N_DEV, nkt),
               a_spec=pl.BlockSpec((tk, 512), lambda i, j, k: (k, i)),
               b_spec=pl.BlockSpec((None, tk, n_gu), lambda i, j, k: (j // per, k, j % per)),
               o_spec=pl.BlockSpec((None, 512, n_gu), lambda i, j, k: (j, i, 0)),
               out_shape=_sds((N_DEV, d, n_gu), F32), ca=0, cb=0, nk=nkt)
    gu_g_fly = _exchange_start("reduce_gate_up_start", [g_gu], [True], dy16)
    dh2 = _d_h2("d_h2", dgu3, w_gu_g, gu_g_fly["token"])
    dx1, dx1_16, g_ffn_norm = _rms_bwd("norm2_bwd", x1, ffn_norm_w, dh2, dy)

    g_out = _mm("g_out", mixed, dx1_16, grid=((2 * GW) // 512, 1, nkt),
                a_spec=pl.BlockSpec((tk, 512), lambda i, j, k: (k, i)),
                b_spec=pl.BlockSpec((tk, d), lambda i, j, k: (k, 0)),
                o_spec=pl.BlockSpec((512, d), lambda i, j, k: (i, 0)),
                out_shape=_sds((2 * GW, d), F32), ca=0, cb=0, nk=nkt)
    out_g_fly = _exchange_start("reduce_out_start", [g_out.reshape(N_DEV, (2 * GW) // N_DEV, d)], [True], g_ffn_norm)
    doa, dproj, dob, delta, g_dn, g_an = _mix_bwd("mix_bwd", dx1_16, w_out_full, oa_raw, proj, 3, ob,
                                                  delta_out_norm_w, attn_out_norm_w, out_g_fly["token"])
    d_aq, d_ak, d_av = _attn_bwd("attn_bwd", aq, ak, proj, 6, dob, lse, delta)
    dproj, g_qn, g_kn = _qk_bwd("attn_qk_bwd", proj, 2, q_norm_w, k_norm_w, cos_t, sin_t, d_aq, d_ak, dproj)
    dproj = _cast_into("attn_v_bwd", d_av, dproj, 6)

    dvn, dqd, dkd, dw, ddec = _delta_scan_bwd("delta_scan_bwd", doa, w, p, qd, kd, gc_b, vn, s_hist)
    dqn, dkn, dvv, dbeta_b, dg_b = _delta_prep_bwd("delta_prep_bwd", qn, kn, vv, beta_b, gc_b, tinv, u, w, vn,
                                                   doa, dvn, dqd, dkd, dw, ddec)
    dproj, gcw_q = _conv_bwd("conv_q_bwd", proj, conv_w8, dqn, dproj, 0, True, HD ** -0.5)
    dproj, gcw_k = _conv_bwd("conv_k_bwd", proj, conv_w8, dkn, dproj, 1, True, 1.0)
    dproj, gcw_v = _conv_bwd("conv_v_bwd", proj, conv_w8, dvv, dproj, 2, False, 1.0)
    dproj, g_alog_row, g_dtb_row = _gates_bwd("gates_bwd", proj, small_blk, alog_row, dtb_row, dbeta_b, dg_b, dproj)
    tmc = 384
    g_cat = _mm("g_in", dproj, h1, grid=(n_cat // tmc, 1, nkt),
                a_spec=pl.BlockSpec((tk, tmc), lambda i, j, k: (k, i)),
                b_spec=pl.BlockSpec((tk, d), lambda i, j, k: (k, 0)),
                o_spec=pl.BlockSpec((tmc, d), lambda i, j, k: (i, 0)),
                out_shape=_sds((n_cat, d), BF16), ca=0, cb=0, nk=nkt)
    parts = []
    for j in range(N_DEV):
        cols = []
        for lo, hi, start in sorted(segments):
            a, b = max(lo, j * n_in), min(hi, (j + 1) * n_in)
            if a < b:
                cols.append(g_cat[start + a - lo:start + b - lo])
        parts.append(cols[0] if len(cols) == 1 else jnp.concatenate(cols, axis=0))
    g_in_parts = jnp.stack(parts)
    g_conv = jnp.concatenate([gcw_q, gcw_k, gcw_v], axis=1)
    n_cw = conv_w.shape[2]
    g_conv_parts = jnp.transpose(g_conv.reshape(8, N_DEV, n_cw), (1, 0, 2))
    in_g_fly = _exchange_start("reduce_in_start", [g_in_parts, g_conv_parts], [True] * 2, g_dtb_row)
    tmh1 = min(512, t)
    dh1 = _mm("d_h1", dproj, w_cat, dep=in_g_fly["token"], grid=(t // tmh1, d // 1024, 1),
              a_spec=pl.BlockSpec((tmh1, n_cat), lambda i, j, k: (i, 0)),
              b_spec=pl.BlockSpec((1024, n_cat), lambda i, j, k: (j, 0)),
              o_spec=pl.BlockSpec((tmh1, 1024), lambda i, j, k: (i, j)),
              out_shape=_sds((t, d), F32), ca=1, cb=1, nk=1)
    grad_x, _, g_attn_norm = _rms_bwd("norm1_bwd", x2, attn_norm_w, dh1, dx1)

    small_rows = [g_attn_norm.reshape(d // HD, HD), g_ffn_norm.reshape(d // HD, HD), g_dn, g_qn, g_kn, g_an,
                  g_alog_row, g_dtb_row, loss_tile[:1]]
    loss_row = sum(r.shape[0] for r in small_rows) - 1
    small_pack = _pad_rows(jnp.concatenate(small_rows, axis=0), 40)
    (r_down,) = _exchange_wait("reduce_down_wait", down_g_fly, grad_x)
    (r_gu,) = _exchange_wait("reduce_gate_up_wait", gu_g_fly, grad_x)
    (r_out,) = _exchange_wait("reduce_out_wait", out_g_fly, grad_x)
    res_gu = [a[None] for a in _adamw("adamw_gate_up", r_gu, w_gate_up[0], m_w_gate_up[0], v_w_gate_up[0])]
    res_down = [a[None] for a in _adamw("adamw_down", r_down, w_down[0], m_w_down[0], v_w_down[0])]
    res_out = [a[None] for a in _adamw("adamw_out", r_out, w_out[0], m_w_out[0], v_w_out[0])]
    done = (res_gu[3][0, :1, :1] + res_down[3][0, :1, :1] + res_out[3][0, :1, :1])
    r_in, r_conv = _exchange_wait("reduce_in_wait", in_g_fly, done)
    upd_in = _adamw("adamw_in", r_in, jnp.transpose(w_in[0]), jnp.transpose(m_w_in[0]), jnp.transpose(v_w_in[0]))
    res_in = [jnp.transpose(a)[None] for a in upd_in]
    (r_small,) = _exchange("gather_small_grads", [small_pack], [False], upd_in[0])

    def pack_small(an, fn, dn, qn_, kn_, aon, al, db):
        rows = [an.reshape(d // HD, HD), fn.reshape(d // HD, HD), dn, qn_, kn_, aon,
                _lane_row(al[0], 8), _lane_row(db[0], 8)]
        return _pad_rows(jnp.concatenate(rows, axis=0), 40)

    def unpack_small(pk):
        nr = d // HD
        return dict(attn_norm_w=pk[:nr].reshape(1, d), ffn_norm_w=pk[nr:2 * nr].reshape(1, d),
                    delta_out_norm_w=pk[2 * nr:2 * nr + 1], q_norm_w=pk[2 * nr + 1:2 * nr + 2],
                    k_norm_w=pk[2 * nr + 2:2 * nr + 3], attn_out_norm_w=pk[2 * nr + 3:2 * nr + 4],
                    a_log=pk[2 * nr + 4:2 * nr + 5, 8:16], dt_bias=pk[2 * nr + 5:2 * nr + 6, 8:16])

    res_small = _adamw("adamw_small", r_small,
                       pack_small(attn_norm_w, ffn_norm_w, delta_out_norm_w, q_norm_w, k_norm_w, attn_out_norm_w, a_log, dt_bias),
                       pack_small(m_attn_norm_w, m_ffn_norm_w, m_delta_out_norm_w, m_q_norm_w, m_k_norm_w, m_attn_out_norm_w, m_a_log, m_dt_bias),
                       pack_small(v_attn_norm_w, v_ffn_norm_w, v_delta_out_norm_w, v_q_norm_w, v_k_norm_w, v_attn_out_norm_w, v_a_log, v_dt_bias))
    small = [unpack_small(a) for a in res_small]
    res_conv =[a[None, :4] for a in _adamw("adamw_conv", r_conv, _pad_rows(conv_w[0], 8), _pad_rows(m_conv_w[0], 8),
                                            _pad_rows(v_conv_w[0], 8))]

    loss = jnp.sum(r_small[:, loss_row, 0])
    outs = [loss, grad_x[None]]
    for i in range(4):
        s = small[i]
        outs += [s["attn_norm_w"], res_in[i], res_conv[i], s["a_log"], s["dt_bias"], s["delta_out_norm_w"],
                 s["q_norm_w"], s["k_norm_w"], s["attn_out_norm_w"], res_out[i], s["ffn_norm_w"], res_gu[i],
                 res_down[i]]
    return tuple(outs)
```

```python
import numpy as np
import jax
import jax.numpy as jnp
from jax import lax
from jax.experimental import pallas as pl
from jax.experimental.pallas import tpu as pltpu

F32 = jnp.float32
BF16 = jnp.bfloat16

N_DEV = 8
N_HEADS = 8
HD = 128
GW = N_HEADS * HD
CHUNK = 64
PAIR = 2 * CHUNK
SCAN_CHUNKS = 4
SCAN_ROWS = SCAN_CHUNKS * CHUNK
SPAN = 128
DILATIONS = (1, 4, 16)
ROPE_THETA = 10000.0
EPS = 1e-6
D_FF = 5632
ADAM_LR, ADAM_B1, ADAM_B2, ADAM_EPS, ADAM_WD, ADAM_STEP = 0.001, 0.9, 0.999, 1e-8, 0.01, 10
NEG = -1e30
VMEM_LIMIT = 56 * 1024 * 1024
ANY = pl.BlockSpec(memory_space=pl.ANY)
HEADS_PER_STEP = 8


def _params(n_grid, vmem=VMEM_LIMIT):
    return pltpu.CompilerParams(dimension_semantics=("arbitrary",) * n_grid, vmem_limit_bytes=vmem)


def _sds(shape, dtype):
    return jax.ShapeDtypeStruct(tuple(shape), dtype)


def _sigmoid(x):
    return 1.0 / (1.0 + jnp.exp(-x))


def _silu(x):
    return x * _sigmoid(x)


def _softplus(x):
    return jnp.maximum(x, 0.0) + jnp.log(1.0 + jnp.exp(-jnp.abs(x)))


def _dot(a, b, ca, cb):
    return lax.dot_general(a, b, (((ca,), (cb,)), ((), ())), preferred_element_type=F32)


def _b16(x):
    return x if x.dtype == BF16 else x.astype(BF16)


def _split(x):
    hi = x.astype(BF16)
    return hi, (x - hi.astype(F32)).astype(BF16)


def _dot3(a, b, ca, cb):
    a_hi, a_lo = _split(a)
    b_hi, b_lo = _split(b)
    return _dot(a_hi, b_hi, ca, cb) + (_dot(a_hi, b_lo, ca, cb) + _dot(a_lo, b_hi, ca, cb))


def _iota2(shape, axis):
    return lax.broadcasted_iota(jnp.int32, shape, axis)


def _mm(name, a, b, *, grid, a_spec, b_spec, o_spec, out_shape, ca, cb, nk, dep=None):
    assert nk == 1 and grid[2] == 1

    def body(*refs):
        refs[-1][...] = _dot(_b16(refs[0][...]), _b16(refs[1][...]), ca, cb).astype(refs[-1].dtype)

    in_specs = [a_spec, b_spec] + ([ANY] if dep is not None else [])
    args = (a, b) + ((dep,) if dep is not None else ())
    return pl.pallas_call(body, grid=grid, in_specs=in_specs, out_specs=o_spec, out_shape=out_shape,
                          name=name, compiler_params=_params(3))(*args)


def _rms_f(xv, wv):
    return xv * lax.rsqrt(jnp.mean(xv * xv, axis=-1, keepdims=True) + EPS) * wv


def _rms_fwd(name, x, w, dep):
    t, d = x.shape
    tm = min(512, t)

    def body(x_ref, w_ref, dep_ref, o_ref):
        o_ref[...] = _rms_f(x_ref[...], w_ref[...]).astype(BF16)

    row = pl.BlockSpec((tm, d), lambda i: (i, 0))
    vec = pl.BlockSpec((1, d), lambda i: (0, 0))
    return pl.pallas_call(body, grid=(t // tm,), in_specs=[row, vec, ANY], out_specs=row,
                          out_shape=_sds((t, d), BF16), name=name, compiler_params=_params(1))(x, w, dep)


def _rms_bwd(name, x, w, dh, res):
    t, d = x.shape
    tm = min(256, t)

    def body(x_ref, w_ref, dh_ref, res_ref, dx_ref, dx16_ref, dw_ref):
        _, vjp = jax.vjp(_rms_f, x_ref[...], w_ref[...])
        dxv, dwv = vjp(dh_ref[...])
        dxv = dxv + res_ref[...]
        dx_ref[...] = dxv
        dx16_ref[...] = dxv.astype(BF16)

        @pl.when(pl.program_id(0) == 0)
        def _():
            dw_ref[...] = jnp.zeros_like(dw_ref)

        dw_ref[...] += dwv

    row = pl.BlockSpec((tm, d), lambda i: (i, 0))
    vec = pl.BlockSpec((1, d), lambda i: (0, 0))
    return pl.pallas_call(body, grid=(t // tm,), in_specs=[row, vec, row, row], out_specs=[row, row, vec],
                          out_shape=[_sds((t, d), F32), _sds((t, d), BF16), _sds((1, d), F32)], name=name,
                          compiler_params=_params(1))(x, w, dh, res)


def _shift_rows(x, s):
    t = x.shape[0]
    r = pltpu.roll(x, s % t, 0)
    row8 = _iota2((8, x.shape[1]), 0)
    if s > 0:
        return jnp.concatenate([jnp.where(row8 >= s, r[:8], 0.0), r[8:]], axis=0)
    return jnp.concatenate([r[:t - 8], jnp.where(row8 < 8 + s, r[t - 8:], 0.0)], axis=0)


def _conv_taps(xv, w_ref):
    c = w_ref[3:4, :] * xv
    for s in (1, 2, 3):
        c = c + w_ref[3 - s:4 - s, :] * _shift_rows(xv, s)
    return c


def _post_conv(c, l2, scale):
    y = _silu(c)
    if l2:
        y = y * lax.rsqrt(jnp.sum(y * y, axis=-1, keepdims=True) + EPS) * scale
    return y


def _conv_fwd(name, proj, conv_w8, group, l2, scale):
    t = proj.shape[0]

    def body(x_ref, w_ref, o_ref):
        o_ref[...] = _post_conv(_conv_taps(x_ref[...], w_ref), l2, scale)

    return pl.pallas_call(
        body, grid=(N_HEADS,),
        in_specs=[pl.BlockSpec((t, HD), lambda h: (0, h + group * N_HEADS)),
                  pl.BlockSpec((8, HD), lambda h: (0, h + group * N_HEADS))],
        out_specs=pl.BlockSpec((t, HD), lambda h: (0, h)),
        out_shape=_sds((t, GW), F32), name=name, compiler_params=_params(1, VMEM_LIMIT))(proj, conv_w8)


def _conv_bwd(name, proj, conv_w8, dn, dproj, group, l2, scale):
    t = proj.shape[0]

    def body(x_ref, w_ref, dn_ref, dproj_ref, dx_ref, dw_ref):
        xv = x_ref[...]
        c = _conv_taps(xv, w_ref)
        _, vjp = jax.vjp(lambda cc: _post_conv(cc, l2, scale), c)
        (dc,) = vjp(dn_ref[...])
        dx = w_ref[3:4, :] * dc
        dw = jnp.zeros((8, HD), F32)
        rid = _iota2((8, HD), 0)
        dw = dw + jnp.where(rid == 3, jnp.sum(dc * xv, axis=0, keepdims=True), 0.0)
        for s in (1, 2, 3):
            dx = dx + w_ref[3 - s:4 - s, :] * _shift_rows(dc, -s)
            dw = dw + jnp.where(rid == 3 - s, jnp.sum(dc * _shift_rows(xv, s), axis=0, keepdims=True), 0.0)
        dx_ref[...] = dx.astype(BF16)
        dw_ref[...] = dw

    return pl.pallas_call(
        body, grid=(N_HEADS,),
        in_specs=[pl.BlockSpec((t, HD), lambda h: (0, h + group * N_HEADS)),
                  pl.BlockSpec((8, HD), lambda h: (0, h + group * N_HEADS)),
                  pl.BlockSpec((t, HD), lambda h: (0, h)), ANY],
        out_specs=[pl.BlockSpec((t, HD), lambda h: (0, h + group * N_HEADS)), pl.BlockSpec((8, HD), lambda h: (0, h))],
        out_shape=[_sds(dproj.shape, BF16), _sds((8, GW), F32)], input_output_aliases={3: 0}, name=name,
        compiler_params=_params(1, VMEM_LIMIT))(proj, conv_w8, dn, dproj)


def _chunk_cumsum(g, rows):
    pos = rows % CHUNK
    s = 1
    while s < CHUNK:
        g = g + jnp.where(pos >= s, pltpu.roll(g, s, 0), 0.0)
        s *= 2
    return g


def _gates_fwd(name, proj, small_blk, alog_row, dtb_row):
    t = proj.shape[0]
    tm = min(512, t)

    def body(s_ref, a_ref, b_ref, beta_ref, gc_ref):
        sm = s_ref[...]
        beta = _sigmoid(sm)
        g = -jnp.exp(a_ref[...]) * _softplus(sm + b_ref[...])
        gc = _chunk_cumsum(g, _iota2((tm, HD), 0))
        lane = _iota2((tm, HD), 1)
        for h in range(N_HEADS):
            bcol = jnp.sum(jnp.where(lane == h, beta, 0.0), axis=1, keepdims=True)
            gcol = jnp.sum(jnp.where(lane == 8 + h, gc, 0.0), axis=1, keepdims=True)
            beta_ref[:, h * HD:(h + 1) * HD] = jnp.broadcast_to(bcol, (tm, HD))
            gc_ref[:, h * HD:(h + 1) * HD] = jnp.broadcast_to(gcol, (tm, HD))

    vec = pl.BlockSpec((1, HD), lambda i: (0, 0))
    wide = pl.BlockSpec((tm, GW), lambda i: (i, 0))
    return pl.pallas_call(
        body, grid=(t // tm,),
        in_specs=[pl.BlockSpec((tm, HD), lambda i: (i, small_blk)), vec, vec], out_specs=[wide, wide],
        out_shape=[_sds((t, GW), F32), _sds((t, GW), F32)], name=name,
        compiler_params=_params(1))(proj, alog_row, dtb_row)


def _gates_bwd(name, proj, small_blk, alog_row, dtb_row, dbeta_b, dg_b, dproj):
    t = proj.shape[0]
    tm = min(512, t)

    def body(s_ref, a_ref, b_ref, db_ref, dg_ref, dproj_ref, ds_ref, da_ref, dbias_ref):
        sm = s_ref[...]
        lane = _iota2((tm, HD), 1)
        db = jnp.zeros((tm, HD), F32)
        dg = jnp.zeros((tm, HD), F32)
        for h in range(N_HEADS):
            db = db + jnp.where(lane == h, db_ref[:, h * HD:(h + 1) * HD], 0.0)
            dg = dg + jnp.where(lane == 8 + h, dg_ref[:, h * HD:(h + 1) * HD], 0.0)
        beta = _sigmoid(sm)
        ea = jnp.exp(a_ref[...])
        pre = sm + b_ref[...]
        g = -ea * _softplus(pre)
        dpre = dg * (-ea) * _sigmoid(pre)
        ds_ref[...] = (db * beta * (1.0 - beta) + dpre).astype(BF16)

        @pl.when(pl.program_id(0) == 0)
        def _():
            da_ref[...] = jnp.zeros_like(da_ref)
            dbias_ref[...] = jnp.zeros_like(dbias_ref)

        da_ref[...] += jnp.sum(dg * g, axis=0, keepdims=True)
        dbias_ref[...] += jnp.sum(dpre, axis=0, keepdims=True)

    vec = pl.BlockSpec((1, HD), lambda i: (0, 0))
    wide = pl.BlockSpec((tm, GW), lambda i: (i, 0))
    return pl.pallas_call(
        body, grid=(t // tm,),
        in_specs=[pl.BlockSpec((tm, HD), lambda i: (i, small_blk)), vec, vec, wide, wide, ANY],
        out_specs=[pl.BlockSpec((tm, HD), lambda i: (i, small_blk)), vec, vec],
        out_shape=[_sds(dproj.shape, BF16), _sds((1, HD), F32), _sds((1, HD), F32)],
        input_output_aliases={5: 0}, name=name,
        compiler_params=_params(1))(proj, alog_row, dtb_row, dbeta_b, dg_b, dproj)


def _pair_masks():
    ii = _iota2((PAIR, PAIR), 0)
    jj = _iota2((PAIR, PAIR), 1)
    same = (ii // CHUNK) == (jj // CHUNK)
    return ii, jj, same & (ii >= jj), same & (ii > jj)


def _to_row(col_b, ii, jj):
    return jnp.sum(jnp.where(ii == jj, col_b, 0.0), axis=0, keepdims=True)


def _to_col(row, ii, jj):
    return jnp.sum(jnp.where(ii == jj, jnp.broadcast_to(row, (PAIR, PAIR)), 0.0), axis=1, keepdims=True)


def _decay_parts(gc, last_a, last_b, ii, jj, causal):
    diff = gc - _to_row(gc, ii, jj)
    dmat = jnp.where(causal, jnp.exp(jnp.where(causal, diff, 0.0)), 0.0)
    glast = jnp.where(ii < CHUNK, last_a, last_b)
    return dmat, jnp.exp(gc), jnp.exp(glast - gc)


def _unit_lower_inverse(lows, ii, jj):
    eye = jnp.where(ii == jj, 1.0, 0.0)
    mm = lambda xs, ys: [_dot3(a, b, 1, 0) for a, b in zip(xs, ys)]
    plus = lambda xs: [eye + a for a in xs]
    minus = lambda xs: [eye - a for a in xs]
    d1 = [jnp.where((ii // 16) == (jj // 16), low, 0.0) for low in lows]
    d2 = mm(d1, d1)
    a = mm(minus(d1), plus(d2))
    d4 = mm(d2, d2)
    a = mm(a, plus(d4))
    d8 = mm(d4, d4)
    td = mm(a, plus(d8))
    n1 = mm(td, [low - d for low, d in zip(lows, d1)])
    n2 = mm(n1, n1)
    return mm(mm(minus(n1), plus(n2)), td)


def _delta_prep(name, qn, kn, vv, beta_b, gc_b):
    t = qn.shape[0]

    def body(q_ref, k_ref, v_ref, b_ref, g_ref, u_ref, w_ref, p_ref, t_ref, qd_ref, kd_ref):
        ii, jj, causal, strict = _pair_masks()
        sls = [slice(hh * HD, (hh + 1) * HD) for hh in range(HEADS_PER_STEP)]
        lows = []
        for sl in sls:
            q, k, beta = q_ref[:, sl], k_ref[:, sl], b_ref[:, sl]
            dmat, gam, e2 = _decay_parts(g_ref[:, sl], g_ref[CHUNK - 1:CHUNK, sl], g_ref[PAIR - 1:PAIR, sl],
                                         ii, jj, causal)
            k16 = _b16(k)
            lows.append(jnp.where(strict, beta * _dot(k16, k16, 1, 1) * dmat, 0.0))
            p_ref[:, sl] = jnp.where(causal, _dot(_b16(q), k16, 1, 1) * dmat, 0.0).astype(BF16)
            qd_ref[:, sl] = (q * gam).astype(BF16)
            kd_ref[:, sl] = (k * e2).astype(BF16)
        for sl, tinv in zip(sls, _unit_lower_inverse(lows, ii, jj)):
            beta = b_ref[:, sl]
            t_ref[:, sl] = tinv
            u_ref[:, sl] = _dot3(tinv, v_ref[:, sl] * beta, 1, 0)
            w_ref[:, sl] = _dot3(tinv, k_ref[:, sl] * (beta * jnp.exp(g_ref[:, sl])), 1, 0).astype(BF16)

    blk = pl.BlockSpec((PAIR, HEADS_PER_STEP * HD), lambda i, h: (i, h))
    return pl.pallas_call(
        body, grid=(t // PAIR, N_HEADS // HEADS_PER_STEP), in_specs=[blk] * 5, out_specs=[blk] * 6,
        out_shape=[_sds((t, GW), F32), _sds((t, GW), BF16), _sds((t, GW), BF16), _sds((t, GW), F32),
                   _sds((t, GW), BF16), _sds((t, GW), BF16)],
        name=name, compiler_params=_params(2))(qn, kn, vv, beta_b, gc_b)


def _delta_scan(name, u, w, p, qd, kd, gc_b):
    t = u.shape[0]
    n = t // CHUNK

    def body(u_ref, w_ref, p_ref, qd_ref, kd_ref, g_ref, o_ref, vn_ref, sh_ref, state):
        @pl.when(pl.program_id(0) == 0)
        def _():
            state[...] = jnp.zeros_like(state)

        sls = [slice(h * HD, (h + 1) * HD) for h in range(N_HEADS)]
        heads = range(N_HEADS)
        s = [state[h] for h in heads]
        for c in range(SCAN_CHUNKS):
            rows = slice(c * CHUNK, (c + 1) * CHUNK)
            last = slice((c + 1) * CHUNK - 1, (c + 1) * CHUNK)
            for h in heads:
                sh_ref[c, h] = s[h]
            s16 = [_b16(a) for a in s]
            ws = [_dot(w_ref[rows, sls[h]], s16[h], 1, 0) for h in heads]
            qs = [_dot(qd_ref[rows, sls[h]], s16[h], 1, 0) for h in heads]
            vn16 = [_b16(u_ref[rows, sls[h]] - ws[h]) for h in heads]
            pv = [_dot(p_ref[rows, sls[h]], jnp.concatenate([vn16[h], vn16[h]], axis=0), 1, 0) for h in heads]
            kv = [_dot(kd_ref[rows, sls[h]], vn16[h], 0, 0) for h in heads]
            for h in heads:
                o_ref[rows, sls[h]] = qs[h] + pv[h]
                vn_ref[rows, sls[h]] = vn16[h]
            s = [s[h] * jnp.exp(g_ref[last, sls[h]]) + kv[h] for h in heads]
        for h in heads:
            state[h] = s[h]

    blk = pl.BlockSpec((SCAN_ROWS, GW), lambda i: (i, 0))
    return pl.pallas_call(
        body, grid=(t // SCAN_ROWS,), in_specs=[blk] * 6,
        out_specs=[blk, blk, pl.BlockSpec((SCAN_CHUNKS, N_HEADS, HD, HD), lambda i: (i, 0, 0, 0))],
        out_shape=[_sds((t, GW), F32), _sds((t, GW), BF16), _sds((n, N_HEADS, HD, HD), F32)],
        scratch_shapes=[pltpu.VMEM((N_HEADS, HD, HD), F32)], name=name,
        compiler_params=_params(1))(u, w, p, qd, kd, gc_b)


def _delta_scan_bwd(name, do, w, p, qd, kd, gc_b, vn, s_hist):
    t = do.shape[0]
    n = t // CHUNK

    def body(do_ref, w_ref, p_ref, qd_ref, kd_ref, g_ref, vn_ref, sh_ref,
             dvn_ref, dqd_ref, dkd_ref, dw_ref, ddec_ref, dstate):
        @pl.when(pl.program_id(0) == 0)
        def _():
            dstate[...] = jnp.zeros_like(dstate)

        sls = [slice(h * HD, (h + 1) * HD) for h in range(N_HEADS)]
        heads = range(N_HEADS)
        ds = [dstate[h] for h in heads]
        for c in reversed(range(SCAN_CHUNKS)):
            rows = slice(c * CHUNK, (c + 1) * CHUNK)
            last = slice((c + 1) * CHUNK - 1, (c + 1) * CHUNK)
            ds16 = [_b16(a) for a in ds]
            s16 = [_b16(sh_ref[c, h]) for h in heads]
            do16 = [_b16(do_ref[rows, sls[h]]) for h in heads]
            ptdo = [_dot(p_ref[rows, sls[h]], do16[h], 0, 0) for h in heads]
            kds = [_dot(kd_ref[rows, sls[h]], ds16[h], 1, 0) for h in heads]
            qdo = [_dot(qd_ref[rows, sls[h]], do16[h], 0, 0) for h in heads]
            for h in heads:
                dqd_ref[rows, sls[h]] = _dot(do16[h], s16[h], 1, 1)
                dkd_ref[rows, sls[h]] = _dot(vn_ref[rows, sls[h]], ds16[h], 1, 1)
            dvn = [ptdo[h][:CHUNK, :] + ptdo[h][CHUNK:, :] + kds[h] for h in heads]
            dvn16 = [_b16(a) for a in dvn]
            wdv = [_dot(w_ref[rows, sls[h]], dvn16[h], 0, 0) for h in heads]
            for h in heads:
                dvn_ref[rows, sls[h]] = dvn[h]
                dw_ref[rows, sls[h]] = -_dot(dvn16[h], s16[h], 1, 1)
                tot = jnp.sum(jnp.sum(sh_ref[c, h] * ds[h], axis=1, keepdims=True), axis=0, keepdims=True)
                ddec_ref[c * 8:(c + 1) * 8, sls[h]] = jnp.broadcast_to(tot, (8, HD))
            ds = [ds[h] * jnp.exp(g_ref[last, sls[h]]) + qdo[h] - wdv[h] for h in heads]
        for h in heads:
            dstate[h] = ds[h]

    npair = t // SCAN_ROWS
    blk = pl.BlockSpec((SCAN_ROWS, GW), lambda i: (npair - 1 - i, 0))
    return pl.pallas_call(
        body, grid=(npair,),
        in_specs=[blk] * 7 + [pl.BlockSpec((SCAN_CHUNKS, N_HEADS, HD, HD), lambda i: (npair - 1 - i, 0, 0, 0))],
        out_specs=[blk] * 4 + [pl.BlockSpec((8 * SCAN_CHUNKS, GW), lambda i: (npair - 1 - i, 0))],
        out_shape=[_sds((t, GW), F32)] * 4 + [_sds((n * 8, GW), F32)],
        scratch_shapes=[pltpu.VMEM((N_HEADS, HD, HD), F32)], name=name,
        compiler_params=_params(1))(do, w, p, qd, kd, gc_b, vn, s_hist)


def _delta_prep_bwd(name, qn, kn, vv, beta_b, gc_b, tinv, u, w, vn, do, dvn, dqd, dkd, dw, ddec):
    t = qn.shape[0]

    def body(q_ref, k_ref, v_ref, b_ref, g_ref, t_ref, u_ref, w_ref, vn_ref, do_ref, dvn_ref, dqd_ref,
             dkd_ref, dw_ref, ddec_ref, dq_ref, dk_ref, dv_ref, dbeta_ref, dg_ref):
        ii, jj, causal, strict = _pair_masks()
        suffix = ((ii // CHUNK) == (jj // CHUNK)) & (jj >= ii)
        first = ii < CHUNK
        rs = lambda a: jnp.sum(a, axis=1, keepdims=True)
        sls = [slice(hh * HD, (hh + 1) * HD) for hh in range(HEADS_PER_STEP)]
        xs = [_dot3(t_ref[:, sl], dvn_ref[:, sl], 0, 0) for sl in sls]
        ys = [_dot3(t_ref[:, sl], dw_ref[:, sl], 0, 0) for sl in sls]
        k16s = [_b16(k_ref[:, sl]) for sl in sls]
        kks = [_dot(k16, k16, 1, 1) for k16 in k16s]
        qks = [_dot(_b16(q_ref[:, sl]), k16, 1, 1) for sl, k16 in zip(sls, k16s)]
        dps = [jnp.where(causal, _dot(_b16(do_ref[:, sl]), vn_ref[:, sl], 1, 1), 0.0) for sl in sls]
        das = [-jnp.where(strict, _dot(_b16(x), _b16(u_ref[:, sl]), 1, 1) + _dot(_b16(y), w_ref[:, sl], 1, 1), 0.0)
               for sl, x, y in zip(sls, xs, ys)]
        for hh, sl in enumerate(sls):
            q, k, v, beta, gc = q_ref[:, sl], k_ref[:, sl], v_ref[:, sl], b_ref[:, sl], g_ref[:, sl]
            last_a, last_b = g_ref[CHUNK - 1:CHUNK, sl], g_ref[PAIR - 1:PAIR, sl]
            dmat, gam, e2 = _decay_parts(gc, last_a, last_b, ii, jj, causal)
            q16, k16 = _b16(q), k16s[hh]
            kk, qk, dp, x, y, da = kks[hh], qks[hh], dps[hh], xs[hh], ys[hh], das[hh]
            dqd, dkd = dqd_ref[:, sl], dkd_ref[:, sl]
            dpd16 = _b16(dp * dmat)
            dkk16 = _b16(da * beta * dmat)
            dq_ref[:, sl] = gam * dqd + _dot(dpd16, k16, 1, 0)
            dk_ref[:, sl] = (e2 * dkd + _dot(dpd16, q16, 0, 0) + beta * gam * y
                             + _dot(dkk16, k16, 1, 0) + _dot(dkk16, k16, 0, 0))
            dv_ref[:, sl] = beta * x
            dbeta = rs(v * x) + rs(k * gam * y) + rs(da * kk * dmat)
            dbeta_ref[:, sl] = jnp.broadcast_to(dbeta, (PAIR, HD))
            m = (dp * qk + da * beta * kk) * dmat
            dgam = rs(q * dqd) + rs(k * beta * y)
            de2 = rs(k * dkd)
            colsum = _to_col(jnp.sum(m, axis=0, keepdims=True), ii, jj)
            te2 = de2 * e2
            dgc = rs(m) - colsum + gam * dgam - te2
            tail_a = jnp.sum(jnp.where(first, te2, 0.0), axis=0, keepdims=True)
            tail_b = jnp.sum(jnp.where(first, 0.0, te2), axis=0, keepdims=True)
            dgc = dgc + jnp.where(ii == CHUNK - 1, tail_a + ddec_ref[0:1, sl] * jnp.exp(last_a), 0.0)
            dgc = dgc + jnp.where(ii == PAIR - 1, tail_b + ddec_ref[8:9, sl] * jnp.exp(last_b), 0.0)
            dgc_row = _to_row(dgc, ii, jj)
            dg = jnp.sum(jnp.where(suffix, jnp.broadcast_to(dgc_row, (PAIR, PAIR)), 0.0), axis=1, keepdims=True)
            dg_ref[:, sl] = jnp.broadcast_to(dg, (PAIR, HD))

    blk = pl.BlockSpec((PAIR, HEADS_PER_STEP * HD), lambda i, h: (i, h))
    return pl.pallas_call(
        body, grid=(t // PAIR, N_HEADS // HEADS_PER_STEP),
        in_specs=[blk] * 14 + [pl.BlockSpec((16, HEADS_PER_STEP * HD), lambda i, h: (i, h))], out_specs=[blk] * 5,
        out_shape=[_sds((t, GW), F32)] * 5, name=name,
        compiler_params=_params(2))(qn, kn, vv, beta_b, gc_b, tinv, u, w, vn, do, dvn, dqd, dkd, dw, ddec)


def _rope_tables(name, pos_col, inv_row):
    t = pos_col.shape[0]
    tm = min(1024, t)

    def body(pos_ref, inv_ref, cos_ref, sin_ref):
        ang = pos_ref[...].astype(F32) * inv_ref[...]
        lane = _iota2(ang.shape, 1)
        cos_ref[...] = jnp.cos(ang)
        sin_ref[...] = jnp.where(lane < HD // 2, -1.0, 1.0) * jnp.sin(ang)

    tab = pl.BlockSpec((tm, HD), lambda i: (i, 0))
    return pl.pallas_call(
        body, grid=(t // tm,), in_specs=[pl.BlockSpec((tm, 1), lambda i: (i, 0)), pl.BlockSpec((1, HD), lambda i: (0, 0))],
        out_specs=[tab, tab], out_shape=[_sds((t, HD), F32)] * 2, name=name,
        compiler_params=_params(1))(pos_col, inv_row)


def _head_rms(xh, wv):
    return xh * lax.rsqrt(jnp.mean(xh * xh, axis=-1, keepdims=True) + EPS) * wv


def _qk_fwd(name, proj, pair_blk, wq_row, wk_row, cos_t, sin_t):
    t = proj.shape[0]
    tm = min(512, t)

    def body(x_ref, wq_ref, wk_ref, cos_ref, sin_ref, q_ref, k_ref):
        cos, sin = cos_ref[...], sin_ref[...]
        for o_ref, w_ref, base in ((q_ref, wq_ref, 0), (k_ref, wk_ref, GW)):
            for h in range(N_HEADS):
                y = _head_rms(x_ref[:, base + h * HD:base + (h + 1) * HD], w_ref[...])
                o_ref[:, h * HD:(h + 1) * HD] = y * cos + pltpu.roll(y, HD // 2, 1) * sin

    vec = pl.BlockSpec((1, HD), lambda i: (0, 0))
    tab = pl.BlockSpec((tm, HD), lambda i: (i, 0))
    wide = pl.BlockSpec((tm, GW), lambda i: (i, 0))
    return pl.pallas_call(
        body, grid=(t // tm,),
        in_specs=[pl.BlockSpec((tm, 2 * GW), lambda i: (i, pair_blk)), vec, vec, tab, tab],
        out_specs=[wide, wide], out_shape=[_sds((t, GW), F32)] * 2, name=name,
        compiler_params=_params(1))(proj, wq_row, wk_row, cos_t, sin_t)


def _qk_bwd(name, proj, pair_blk, wq_row, wk_row, cos_t, sin_t, dq_full, dk_full, dproj):
    t = proj.shape[0]
    tm = min(512, t)

    def body(x_ref, wq_ref, wk_ref, cos_ref, sin_ref, dq_ref, dk_ref, dproj_ref, dx_ref, dwq_ref, dwk_ref):
        cos, sin = cos_ref[...], sin_ref[...]

        @pl.when(pl.program_id(0) == 0)
        def _():
            dwq_ref[...] = jnp.zeros_like(dwq_ref)
            dwk_ref[...] = jnp.zeros_like(dwk_ref)

        for dy_ref, w_ref, dw_ref, base in ((dq_ref, wq_ref, dwq_ref, 0), (dk_ref, wk_ref, dwk_ref, GW)):
            dw = jnp.zeros((1, HD), F32)
            for h in range(N_HEADS):
                dy = dy_ref[:, h * HD:(h + 1) * HD]
                dy = dy * cos - pltpu.roll(dy, HD // 2, 1) * sin
                _, vjp = jax.vjp(_head_rms, x_ref[:, base + h * HD:base + (h + 1) * HD], w_ref[...])
                dx, dwh = vjp(dy)
                dw = dw + dwh
                dx_ref[:, base + h * HD:base + (h + 1) * HD] = dx.astype(BF16)
            dw_ref[...] += dw

    vec = pl.BlockSpec((1, HD), lambda i: (0, 0))
    tab = pl.BlockSpec((tm, HD), lambda i: (i, 0))
    wide = pl.BlockSpec((tm, GW), lambda i: (i, 0))
    pair = pl.BlockSpec((tm, 2 * GW), lambda i: (i, pair_blk))
    return pl.pallas_call(
        body, grid=(t // tm,), in_specs=[pair, vec, vec, tab, tab, wide, wide, ANY],
        out_specs=[pair, vec, vec],
        out_shape=[_sds(dproj.shape, BF16), _sds((1, HD), F32), _sds((1, HD), F32)], input_output_aliases={7: 0},
        name=name, compiler_params=_params(1))(proj, wq_row, wk_row, cos_t, sin_t, dq_full, dk_full, dproj)


def _cast_into(name, x, dproj, blk_idx):
    t = x.shape[0]
    tm = min(512, t)

    def body(x_ref, dproj_ref, o_ref):
        o_ref[...] = x_ref[...].astype(BF16)

    return pl.pallas_call(
        body, grid=(t // tm,), in_specs=[pl.BlockSpec((tm, GW), lambda i: (i, 0)), ANY],
        out_specs=pl.BlockSpec((tm, GW), lambda i: (i, blk_idx)), out_shape=_sds(dproj.shape, BF16),
        input_output_aliases={1: 0}, name=name, compiler_params=_params(1))(x, dproj)


GROUP = SPAN * max(DILATIONS)
SCALE = HD ** -0.5
TILE_BATCH = 8


def _band_mask(lo):
    qi = _iota2((SPAN, 2 * SPAN), 0)
    ki = _iota2((SPAN, 2 * SPAN), 1)
    return (ki >= qi) & (ki <= qi + SPAN) & (ki >= lo)


def _tiles():
    return [(pi, r, u, rho) for pi, r in enumerate(DILATIONS) for rho in range(r) for u in range(GROUP // (SPAN * r))]


def _rows(r, u, rho):
    return pl.ds(u * SPAN * r + rho, SPAN, stride=r) if r > 1 else pl.ds(u * SPAN, SPAN)


def _attn_fwd(name, q, k, v, v_blk):
    t = q.shape[0]

    def body(qc_ref, kc_ref, vc_ref, kp_ref, vp_ref, ob_ref, lse_ref, o_scr, l_scr):
        mask_in = _band_mask(0)
        mask_edge = _band_mask(jnp.where(pl.program_id(0) == 0, SPAN, 0))
        tiles = _tiles()
        k_own = v_own = None
        for b0 in range(0, len(tiles), TILE_BATCH):
            work = []
            for pi, r, u, rho in tiles[b0:b0 + TILE_BATCH]:
                rows = _rows(r, u, rho)
                if u > 0:
                    k_prev, v_prev, mask = k_own, v_own, mask_in
                else:
                    prows = _rows(r, GROUP // (SPAN * r) - 1, rho)
                    k_prev, v_prev, mask = kp_ref[prows, :].astype(BF16), vp_ref[prows, :].astype(BF16), mask_edge
                k_own, v_own = kc_ref[rows, :].astype(BF16), vc_ref[rows, :].astype(BF16)
                work.append((pi, rows, mask, qc_ref[rows, :].astype(BF16), jnp.concatenate([k_prev, k_own], axis=0),
                             jnp.concatenate([v_prev, v_own], axis=0)))
            scores = [_dot(qt, kcat, 1, 1) for _, _, _, qt, kcat, _ in work]
            soft = []
            for (_, _, mask, _, _, _), s in zip(work, scores):
                s = jnp.where(mask, s * SCALE, NEG)
                m = jnp.max(s, axis=1, keepdims=True)
                p = jnp.exp(s - m)
                soft.append((m, _b16(p), jnp.sum(p, axis=1, keepdims=True)))
            outs = [_dot(p, vcat, 1, 0) for (_, p, _), (_, _, _, _, _, vcat) in zip(soft, work)]
            for (pi, rows, _, _, _, _), (m, _, den), o in zip(work, soft, outs):
                o_scr[pi, rows, :] = o / den
                l_scr[pi, rows, :] = jnp.broadcast_to(m + jnp.log(den), (SPAN, HD))
        step = 256
        for c in range(GROUP // step):
            sl = pl.ds(c * step, step)
            ob, lse = _merge([o_scr[i, sl, :] for i in range(3)], [l_scr[i, sl, :] for i in range(3)])
            ob_ref[sl, :] = ob
            lse_ref[sl, :] = lse

    cur = pl.BlockSpec((GROUP, HD), lambda g, h: (g, h))
    prev = pl.BlockSpec((GROUP, HD), lambda g, h: (jnp.maximum(g - 1, 0), h))
    vcur = pl.BlockSpec((GROUP, HD), lambda g, h: (g, v_blk * N_HEADS + h))
    vprev = pl.BlockSpec((GROUP, HD), lambda g, h: (jnp.maximum(g - 1, 0), v_blk * N_HEADS + h))
    return pl.pallas_call(
        body, grid=(t // GROUP, N_HEADS), in_specs=[cur, cur, vcur, prev, vprev], out_specs=[cur, cur],
        out_shape=[_sds((t, GW), F32), _sds((t, GW), F32)],
        scratch_shapes=[pltpu.VMEM((3, GROUP, HD), F32), pltpu.VMEM((3, GROUP, HD), F32)], name=name,
        compiler_params=_params(2))(q, k, v, k, v)


def _attn_bwd(name, q, k, v, v_blk, do, lse, delta):
    t = q.shape[0]
    ng = t // GROUP

    def probs(work):
        scores = [_dot(qt, kcat, 1, 1) for qt, _, _, _, kcat, _, _ in work]
        dps = [_dot(dot, vcat, 1, 1) for _, dot, _, _, _, vcat, _ in work]
        out = []
        for (_, _, lt, dlt, kcat, _, mask), s, dp in zip(work, scores, dps):
            wide = kcat.shape[0] // SPAN
            lw = jnp.concatenate([lt] * wide, axis=1) if wide > 1 else lt
            dw = jnp.concatenate([dlt] * wide, axis=1) if wide > 1 else dlt
            p = jnp.exp(jnp.where(mask, s * SCALE - lw, NEG))
            out.append((_b16(p * (dp - dw) * SCALE), _b16(p)))
        return out

    def body(qc_ref, kc_ref, vc_ref, doc_ref, lc_ref, dc_ref, kp_ref, vp_ref, qn_ref, don_ref, ln_ref, dn_ref,
             dq_ref, dk_ref, dv_ref):
        g = pl.program_id(0)
        mask_in = _band_mask(0)
        mask_edge = _band_mask(jnp.where(g == 0, SPAN, 0))
        dk_ref[...] = jnp.zeros_like(dk_ref)
        dv_ref[...] = jnp.zeros_like(dv_ref)
        tiles = _tiles()
        k_own = v_own = None
        for b0 in range(0, len(tiles), TILE_BATCH):
            where, work = [], []
            for pi, r, u, rho in tiles[b0:b0 + TILE_BATCH]:
                rows = _rows(r, u, rho)
                if u > 0:
                    prows, k_prev, v_prev, mask = _rows(r, u - 1, rho), k_own, v_own, mask_in
                else:
                    prows = _rows(r, GROUP // (SPAN * r) - 1, rho)
                    k_prev, v_prev, mask = kp_ref[prows, :].astype(BF16), vp_ref[prows, :].astype(BF16), mask_edge
                k_own, v_own = kc_ref[rows, :].astype(BF16), vc_ref[rows, :].astype(BF16)
                where.append((pi, u, rows, prows))
                work.append((qc_ref[rows, :].astype(BF16), doc_ref[rows, :].astype(BF16), lc_ref[rows, :], dc_ref[rows, :],
                             jnp.concatenate([k_prev, k_own], axis=0), jnp.concatenate([v_prev, v_own], axis=0), mask))
            dsp = probs(work)
            dqs = [_dot(ds, w[4], 1, 0) for (ds, _), w in zip(dsp, work)]
            dks = [_dot(ds, w[0], 0, 0) for (ds, _), w in zip(dsp, work)]
            dvs = [_dot(p, w[1], 0, 0) for (_, p), w in zip(dsp, work)]
            for (pi, u, rows, prows), dq_t, dk2, dv2 in zip(where, dqs, dks, dvs):
                if pi == 0:
                    dq_ref[rows, :] = dq_t
                else:
                    dq_ref[rows, :] += dq_t
                dk_ref[rows, :] += dk2[SPAN:, :]
                dv_ref[rows, :] += dv2[SPAN:, :]
                if u > 0:
                    dk_ref[prows, :] += dk2[:SPAN, :]
                    dv_ref[prows, :] += dv2[:SPAN, :]
        qi = _iota2((SPAN, SPAN), 0)
        ki = _iota2((SPAN, SPAN), 1)
        mask_next = (ki >= qi) & (ki < jnp.where(g == ng - 1, 0, SPAN))
        edge = [(r, rho) for r in DILATIONS for rho in range(r)]
        for b0 in range(0, len(edge), TILE_BATCH):
            where, work = [], []
            for r, rho in edge[b0:b0 + TILE_BATCH]:
                krows, qrows = _rows(r, GROUP // (SPAN * r) - 1, rho), _rows(r, 0, rho)
                where.append(krows)
                work.append((qn_ref[qrows, :].astype(BF16), don_ref[qrows, :].astype(BF16), ln_ref[qrows, :],
                             dn_ref[qrows, :], kc_ref[krows, :].astype(BF16), vc_ref[krows, :].astype(BF16), mask_next))
            dsp = probs(work)
            dks = [_dot(ds, w[0], 0, 0) for (ds, _), w in zip(dsp, work)]
            dvs = [_dot(p, w[1], 0, 0) for (_, p), w in zip(dsp, work)]
            for krows, dk1, dv1 in zip(where, dks, dvs):
                dk_ref[krows, :] += dk1
                dv_ref[krows, :] += dv1

    cur = pl.BlockSpec((GROUP, HD), lambda g, h: (g, h))
    prev = pl.BlockSpec((GROUP, HD), lambda g, h: (jnp.maximum(g - 1, 0), h))
    nxt = pl.BlockSpec((GROUP, HD), lambda g, h: (jnp.minimum(g + 1, ng - 1), h))
    vcur = pl.BlockSpec((GROUP, HD), lambda g, h: (g, v_blk * N_HEADS + h))
    vprev = pl.BlockSpec((GROUP, HD), lambda g, h: (jnp.maximum(g - 1, 0), v_blk * N_HEADS + h))
    return pl.pallas_call(
        body, grid=(ng, N_HEADS), in_specs=[cur, cur, vcur, cur, cur, cur, prev, vprev] + [nxt] * 4,
        out_specs=[cur] * 3,
        out_shape=[_sds((t, GW), F32)] * 3, name=name,
        compiler_params=_params(2))(q, k, v, do, lse, delta, k, v, q, do, lse, delta)


def _merge(os_, ls_):
    m = jnp.maximum(jnp.maximum(ls_[0], ls_[1]), ls_[2])
    ws = [jnp.exp(l - m) for l in ls_]
    tot = ws[0] + ws[1] + ws[2]
    ob = (ws[0] * os_[0] + ws[1] * os_[1] + ws[2] * os_[2]) / tot
    return ob, m + jnp.log(tot)


def _gated_norm(oa, z, wv):
    return _head_rms(oa, wv) * _silu(z)


def _mix_fwd(name, oa_raw, proj, z_blk, ob, w_dn, w_an):
    t = oa_raw.shape[0]
    tm = min(512, t)

    def body(oa_ref, z_ref, ob_ref, wd_ref, wa_ref, mix_ref):
        for h in range(N_HEADS):
            sl = slice(h * HD, (h + 1) * HD)
            mix_ref[:, sl] = _gated_norm(oa_ref[:, sl], z_ref[:, sl], wd_ref[...]).astype(BF16)
            mix_ref[:, GW + h * HD:GW + (h + 1) * HD] = _head_rms(ob_ref[:, sl], wa_ref[...]).astype(BF16)

    vec = pl.BlockSpec((1, HD), lambda i: (0, 0))
    wide = pl.BlockSpec((tm, GW), lambda i: (i, 0))
    return pl.pallas_call(
        body, grid=(t // tm,),
        in_specs=[wide, pl.BlockSpec((tm, GW), lambda i: (i, z_blk)), wide, vec, vec],
        out_specs=pl.BlockSpec((tm, 2 * GW), lambda i: (i, 0)),
        out_shape=_sds((t, 2 * GW), BF16), name=name,
        compiler_params=_params(1))(oa_raw, proj, ob, w_dn, w_an)


def _mix_bwd(name, dx1_16, w_out, oa_raw, proj, z_blk, ob, w_dn, w_an, dep):
    t, d = dx1_16.shape
    tm = min(512, t)

    def body(dx_ref, wo_ref, oa_ref, z_ref, ob_ref, wd_ref, wa_ref, dep_ref,
             doa_ref, dz_ref, dob_ref, dl_ref, dwd_ref, dwa_ref):
        dwd = jnp.zeros((1, HD), F32)
        dwa = jnp.zeros((1, HD), F32)
        dxv = dx_ref[...]
        pairs = [_dot(dxv, wo_ref[2 * p * HD:2 * (p + 1) * HD, :], 1, 1) for p in range(N_HEADS)]
        heads = [half for pr in pairs for half in (pr[:, :HD], pr[:, HD:])]
        dm_a, dm_b = heads[:N_HEADS], heads[N_HEADS:]
        for h in range(N_HEADS):
            sl = slice(h * HD, (h + 1) * HD)
            _, vjp = jax.vjp(_gated_norm, oa_ref[:, sl], z_ref[:, sl], wd_ref[...])
            doa, dz, dw1 = vjp(dm_a[h])
            doa_ref[:, sl] = doa
            dz_ref[:, sl] = dz.astype(BF16)
            dwd = dwd + dw1
            obh = ob_ref[:, sl]
            _, vjp2 = jax.vjp(_head_rms, obh, wa_ref[...])
            dob, dw2 = vjp2(dm_b[h])
            dwa = dwa + dw2
            dob_ref[:, sl] = dob
            dl_ref[:, sl] = jnp.broadcast_to(jnp.sum(dob * obh, axis=1, keepdims=True), (tm, HD))

        @pl.when(pl.program_id(0) == 0)
        def _():
            dwd_ref[...] = jnp.zeros_like(dwd_ref)
            dwa_ref[...] = jnp.zeros_like(dwa_ref)

        dwd_ref[...] += dwd
        dwa_ref[...] += dwa

    vec = pl.BlockSpec((1, HD), lambda i: (0, 0))
    wide = pl.BlockSpec((tm, GW), lambda i: (i, 0))
    return pl.pallas_call(
        body, grid=(t // tm,),
        in_specs=[pl.BlockSpec((tm, d), lambda i: (i, 0)), pl.BlockSpec((2 * GW, d), lambda i: (0, 0)), wide,
                  pl.BlockSpec((tm, GW), lambda i: (i, z_blk)), wide, vec, vec, ANY],
        out_specs=[wide, pl.BlockSpec((tm, GW), lambda i: (i, z_blk)), wide, wide, vec, vec],
        out_shape=[_sds((t, GW), F32), _sds(proj.shape, BF16), _sds((t, GW), F32), _sds((t, GW), F32),
                   _sds((1, HD), F32), _sds((1, HD), F32)], name=name,
        compiler_params=_params(1))(dx1_16, w_out, oa_raw, proj, ob, w_dn, w_an, dep)


def _halves(n):
    cut = (n // 256) * 128
    return [(0, cut), (cut, n)]


def _gate_up_swiglu(name, h2, w_gu_g):
    t, d = h2.shape
    n = w_gu_g.shape[2]
    per = N_DEV // 2
    tm = min(512, t)

    def body(a_ref, bg_ref, bu_ref, gu_ref, act_ref):
        a = a_ref[...]
        cuts = _halves(n)
        gs = [_dot(a, bg_ref[:, c0:c1], 1, 0) for c0, c1 in cuts]
        ups = [_dot(a, bu_ref[:, c0:c1], 1, 0) for c0, c1 in cuts]
        for (c0, c1), g, up in zip(cuts, gs, ups):
            gu_ref[0, :, c0:c1] = g.astype(BF16)
            gu_ref[1, :, c0:c1] = up.astype(BF16)
            act_ref[:, c0:c1] = (_silu(g) * up).astype(BF16)

    return pl.pallas_call(
        body, grid=(per, t // tm),
        in_specs=[pl.BlockSpec((tm, d), lambda j, i: (i, 0)), pl.BlockSpec((None, d, n), lambda j, i: (j, 0, 0)),
                  pl.BlockSpec((None, d, n), lambda j, i: (j + per, 0, 0))],
        out_specs=[pl.BlockSpec((2, tm, n), lambda j, i: (0, i, j)), pl.BlockSpec((tm, n), lambda j, i: (i, j))],
        out_shape=[_sds((2, t, per * n), BF16), _sds((t, per * n), BF16)], name=name,
        compiler_params=_params(2))(h2, w_gu_g, w_gu_g)


def _d_gate_up(name, dy16, w_down, gu3, dep):
    t, d = dy16.shape
    f = w_down.shape[0]
    tm, tn = min(1024, t), f // 4

    def body(a_ref, b_ref, g_ref, dep_ref, o_ref):
        a = a_ref[...]
        cuts = _halves(tn)
        dacts = [_dot(a, b_ref[c0:c1, :], 1, 1) for c0, c1 in cuts]
        for (c0, c1), dact in zip(cuts, dacts):
            g, up = g_ref[0, :, c0:c1].astype(F32), g_ref[1, :, c0:c1].astype(F32)
            sg = _sigmoid(g)
            o_ref[0, :, c0:c1] = (dact * up * sg * (1.0 + g * (1.0 - sg))).astype(BF16)
            o_ref[1, :, c0:c1] = (dact * g * sg).astype(BF16)

    return pl.pallas_call(
        body, grid=(f // tn, t // tm),
        in_specs=[pl.BlockSpec((tm, d), lambda j, i: (i, 0)), pl.BlockSpec((tn, d), lambda j, i: (j, 0)),
                  pl.BlockSpec((2, tm, tn), lambda j, i: (0, i, j)), ANY],
        out_specs=pl.BlockSpec((2, tm, tn), lambda j, i: (0, i, j)), out_shape=_sds((2, t, f), BF16), name=name,
        compiler_params=_params(2))(dy16, w_down, gu3, dep)


def _d_h2(name, dgu3, w_gu_g, dep):
    _, t, f = dgu3.shape
    n_dev, d, n = w_gu_g.shape
    per = n_dev // 2
    tm, tn = min(512, t), 512

    def body(g_ref, u_ref, b_ref, dep_ref, o_ref):
        acc = None
        for s in range(n_dev):
            a_ref = g_ref if s < per else u_ref
            part = _dot(a_ref[:, (s % per) * n:(s % per + 1) * n], b_ref[s], 1, 1)
            acc = part if acc is None else acc + part
        o_ref[...] = acc

    return pl.pallas_call(
        body, grid=(d // tn, t // tm),
        in_specs=[pl.BlockSpec((None, tm, f), lambda j, i: (0, i, 0)), pl.BlockSpec((None, tm, f), lambda j, i: (1, i, 0)),
                  pl.BlockSpec((n_dev, tn, n), lambda j, i: (0, j, 0)), ANY],
        out_specs=pl.BlockSpec((tm, tn), lambda j, i: (i, j)), out_shape=_sds((t, d), F32), name=name,
        compiler_params=_params(2))(dgu3, dgu3, w_gu_g, dep)


def _out_proj_norm(name, mixed, w_out, x, w_norm):
    t, d = x.shape
    kdim = mixed.shape[1]
    tm = min(512, t)

    def body(a_ref, b_ref, x_ref, w_ref, x1_ref, h_ref):
        x1 = x_ref[...] + _dot(a_ref[...], b_ref[...], 1, 0)
        x1_ref[...] = x1
        h_ref[...] = _rms_f(x1, w_ref[...]).astype(BF16)

    row = pl.BlockSpec((tm, d), lambda i: (i, 0))
    return pl.pallas_call(
        body, grid=(t // tm,),
        in_specs=[pl.BlockSpec((tm, kdim), lambda i: (i, 0)), pl.BlockSpec((kdim, d), lambda i: (0, 0)), row,
                  pl.BlockSpec((1, d), lambda i: (0, 0))],
        out_specs=[row, row], out_shape=[_sds((t, d), F32), _sds((t, d), BF16)], name=name,
        compiler_params=_params(1))(mixed, w_out, x, w_norm)


def _down_loss(name, act, w_down, x1, target):
    t, f = act.shape
    d = x1.shape[1]
    tm, tn = min(1024, t), 512

    def body(a_ref, b_ref, x_ref, t_ref, dy_ref, dy16_ref, l_ref):
        diff = _dot(a_ref[...], b_ref[...], 1, 0) + x_ref[...] - t_ref[...]
        dyv = diff * (1.0 / d)
        dy_ref[...] = dyv
        dy16_ref[...] = dyv.astype(BF16)
        tot = jnp.sum(jnp.sum(diff * diff, axis=1, keepdims=True), axis=0, keepdims=True) * (0.5 / d)

        @pl.when((pl.program_id(0) == 0) & (pl.program_id(1) == 0))
        def _():
            l_ref[...] = jnp.zeros_like(l_ref)

        l_ref[...] += jnp.broadcast_to(tot, (8, 128))

    tile = pl.BlockSpec((tm, tn), lambda i, j: (i, j))
    return pl.pallas_call(
        body, grid=(t // tm, d // tn),
        in_specs=[pl.BlockSpec((tm, f), lambda i, j: (i, 0)), pl.BlockSpec((f, tn), lambda i, j: (0, j)), tile, tile],
        out_specs=[tile, tile, pl.BlockSpec((8, 128), lambda i, j: (0, 0))],
        out_shape=[_sds((t, d), F32), _sds((t, d), BF16), _sds((8, 128), F32)], name=name,
        compiler_params=_params(2))(act, w_down, x1, target)


def _peer(me, k):
    pid = (me + k) % N_DEV
    return (pid // 4, (pid // 2) % 2, pid % 2)


def _my_id():
    return 4 * lax.axis_index("x") + 2 * lax.axis_index("y") + lax.axis_index("c")


def _exchange(name, arrays, scatter, dep):
    n = len(arrays)

    def body(*refs):
        ins, outs = refs[:n], refs[n + 1:2 * n + 1]
        send_sems, recv_sems, local_sems = refs[2 * n + 1:]
        me = _my_id()
        started = []
        for a in range(n):
            src = ins[a].at[me] if scatter[a] else ins[a]
            loc = pltpu.make_async_copy(src, outs[a].at[me], local_sems.at[a])
            loc.start()
            started.append(loc)
        remote = []
        for k in range(1, N_DEV):
            to = (me + k) % N_DEV
            for a in range(n):
                src = ins[a].at[to] if scatter[a] else ins[a]
                cp = pltpu.make_async_remote_copy(src_ref=src, dst_ref=outs[a].at[me],
                                                  send_sem=send_sems.at[a * (N_DEV - 1) + k - 1], recv_sem=recv_sems.at[a * (N_DEV - 1) + k - 1],
                                                  device_id=_peer(me, k), device_id_type=pl.DeviceIdType.MESH)
                cp.start()
                remote.append(cp)
        for k in range(1, N_DEV):
            frm = (me + N_DEV - k) % N_DEV
            for a in range(n):
                src = ins[a].at[frm] if scatter[a] else ins[a]
                pltpu.make_async_remote_copy(src_ref=src, dst_ref=outs[a].at[frm],
                                             send_sem=send_sems.at[a * (N_DEV - 1) + k - 1], recv_sem=recv_sems.at[a * (N_DEV - 1) + k - 1],
                                             device_id=_peer(me, k), device_id_type=pl.DeviceIdType.MESH).wait_recv()
        for cp in remote:
            cp.wait_send()
        for loc in started:
            loc.wait()

    out_shape = [_sds((N_DEV,) + (a.shape[1:] if sc else a.shape), a.dtype) for a, sc in zip(arrays, scatter)]
    return pl.pallas_call(
        body, in_specs=[ANY] * (n + 1), out_specs=[ANY] * n, out_shape=out_shape,
        scratch_shapes=[pltpu.SemaphoreType.DMA((n * (N_DEV - 1),)), pltpu.SemaphoreType.DMA((n * (N_DEV - 1),)),
                        pltpu.SemaphoreType.DMA((n,))],
        name=name)(*arrays, dep)


def _gather_two_level(name, arrays):
    n = len(arrays)
    per = N_DEV - 1
    units = []
    for a, arr in enumerate(arrays):
        cuts = 4 if arr.shape[0] % 64 == 0 and arr.shape[0] >= 1024 else 1
        units += [(a, p * (arr.shape[0] // cuts), arr.shape[0] // cuts) for p in range(cuts)]
    nu = len(units)

    def body(*refs):
        ins, outs = refs[:n], refs[n:2 * n]
        send_sems, recv_sems, local_sems = refs[2 * n:]
        x, y, c = lax.axis_index("x"), lax.axis_index("y"), lax.axis_index("c")
        me, sibling = (x, y, c), (x, y, 1 - c)
        flip = lambda v, on: v + on - 2 * v * on
        relayed = (flip(x, c), flip(y, 1 - c), c)
        other = (flip(x, 1 - c), flip(y, c), c)
        diagonal = (1 - x, 1 - y, c)
        k_relayed, k_other = 2 - c, 1 + c

        def copy(u, k, block, to, from_input=False):
            a, r0, nr = units[u]
            slot = outs[a].at[4 * block[0] + 2 * block[1] + block[2], pl.ds(r0, nr)]
            return pltpu.make_async_remote_copy(
                src_ref=ins[a].at[pl.ds(r0, nr)] if from_input else slot, dst_ref=slot,
                send_sem=send_sems.at[u * per + k], recv_sem=recv_sems.at[u * per + k], device_id=to,
                device_id_type=pl.DeviceIdType.MESH)

        mine = [pltpu.make_async_copy(ins[a], outs[a].at[4 * x + 2 * y + c], local_sems.at[a]) for a in range(n)]
        for cp in mine:
            cp.start()
        sent = [copy(u, 1, me, (1 - x, y, c), True) for u in range(nu)]
        sent += [copy(u, 2, me, (x, 1 - y, c), True) for u in range(nu)]
        sent += [copy(u, 0, me, sibling, True) for u in range(nu)]
        for cp in sent:
            cp.start()
        for u in range(nu):
            copy(u, k_relayed, relayed, me).wait_recv()
            sent.append(copy(u, 3, relayed, other))
            sent.append(copy(u, 3 + k_relayed, relayed, sibling))
            sent[-2].start()
            sent[-1].start()
        for u in range(nu):
            copy(u, k_other, other, me).wait_recv()
            sent.append(copy(u, 3 + k_other, other, sibling))
            sent[-1].start()
        for u in range(nu):
            copy(u, 3, diagonal, me).wait_recv()
            sent.append(copy(u, 6, diagonal, sibling))
            sent[-1].start()
        for u in range(nu):
            copy(u, 0, sibling, me).wait_recv()
            for j, chip in enumerate([(1 - x, y), (x, 1 - y), (1 - x, 1 - y)]):
                copy(u, 4 + j, (*chip, 1 - c), me).wait_recv()
        for cp in sent:
            cp.wait_send()
        for cp in mine:
            cp.wait()

    return pl.pallas_call(
        body, in_specs=[ANY] * n, out_specs=[ANY] * n,
        out_shape=[_sds((N_DEV,) + a.shape, a.dtype) for a in arrays],
        scratch_shapes=[pltpu.SemaphoreType.DMA((nu * per,)), pltpu.SemaphoreType.DMA((nu * per,)),
                        pltpu.SemaphoreType.DMA((n,))],
        name=name)(*arrays)


HBM = pl.BlockSpec(memory_space=pltpu.HBM)
SEM = pl.BlockSpec(memory_space=pltpu.SEMAPHORE)
EFFECT = pltpu.SideEffectType.DATAFLOW_SIDE_EFFECTING


def _remote_copies(srcs, lands, scatter, send_sems, recv_sems, me, incoming):
    out = []
    for k in range(1, N_DEV):
        other = (me + N_DEV - k) % N_DEV if incoming else (me + k) % N_DEV
        for a in range(len(srcs)):
            sem = a * (N_DEV - 1) + k - 1
            src = srcs[a].at[other] if scatter[a] else srcs[a]
            dst = lands[a].at[other if incoming else me]
            out.append(pltpu.make_async_remote_copy(src_ref=src, dst_ref=dst, send_sem=send_sems.at[sem],
                                                    recv_sem=recv_sems.at[sem], device_id=_peer(me, k),
                                                    device_id_type=pl.DeviceIdType.MESH))
    return out


def _exchange_start(name, arrays, scatter, dep):
    n = len(arrays)
    lands = [lax.empty((N_DEV,) + (a.shape[1:] if sc else a.shape), a.dtype) for a, sc in zip(arrays, scatter)]

    def body(*refs):
        srcs, land_refs = refs[:n], refs[n:2 * n]
        send_sems, recv_sems = refs[2 * n + 1], refs[2 * n + 2]
        token = refs[-1]
        for cp in _remote_copies(srcs, land_refs, scatter, send_sems, recv_sems, _my_id(), False):
            cp.start()
        token[...] = jnp.zeros_like(token)

    n_sem = n * (N_DEV - 1)
    out_shape = ([pltpu.SemaphoreType.DMA((n_sem,)), pltpu.SemaphoreType.DMA((n_sem,))]
                 + [pltpu.HBM(a.shape, a.dtype) for a in arrays] + [pltpu.HBM(l.shape, l.dtype) for l in lands]
                 + [_sds((8, 128), F32)])
    aliases = {i: 2 + i for i in range(2 * n)}
    args = [pltpu.with_memory_space_constraint(a, pltpu.HBM) for a in list(arrays) + lands] + [dep]
    res = pl.pallas_call(
        body, name=name, in_specs=[HBM] * (2 * n) + [ANY], out_shape=out_shape,
        out_specs=[SEM, SEM] + [HBM] * (2 * n) + [pl.BlockSpec(memory_space=pltpu.VMEM)],
        input_output_aliases=aliases, compiler_params=pltpu.CompilerParams(has_side_effects=EFFECT))(*args)
    return dict(send=res[0], recv=res[1], srcs=res[2:2 + n], lands=res[2 + n:2 + 2 * n], token=res[-1],
                scatter=scatter)


def _exchange_wait(name, started, after):
    n = len(started["srcs"])
    scatter = started["scatter"]

    def body(*refs):
        srcs, land_refs = refs[:n], refs[n:2 * n]
        send_sems, recv_sems = refs[2 * n], refs[2 * n + 1]
        me = _my_id()
        for cp in _remote_copies(srcs, land_refs, scatter, send_sems, recv_sems, me, False):
            cp.wait_send()
        for cp in _remote_copies(srcs, land_refs, scatter, send_sems, recv_sems, me, True):
            cp.wait_recv()

    arrs = list(started["srcs"]) + list(started["lands"])
    res = pl.pallas_call(
        body, name=name, in_specs=[HBM] * (2 * n) + [SEM, SEM, ANY],
        out_shape=[pltpu.HBM(a.shape, a.dtype) for a in arrs], out_specs=[HBM] * (2 * n),
        input_output_aliases={i: i for i in range(2 * n)},
        compiler_params=pltpu.CompilerParams(has_side_effects=EFFECT))(*arrs, started["send"], started["recv"], after)
    me = _my_id()
    out = []
    for src, land, sc in zip(res[:n], res[n:], scatter):
        own = lax.dynamic_index_in_dim(src, me, 0, keepdims=True) if sc else src[None]
        out.append(lax.dynamic_update_slice(land, own, (me,) + (0,) * (land.ndim - 1)))
    return out


def _adamw(name, parts, w, m, v):
    r, c = w.shape
    tr, tc = r, c
    if r % 8 == 0:
        tr = next(cand for cand in (128, 88, 64, 40, 8) if r % cand == 0)
    else:
        tc = 256
    c1 = 1.0 / (1.0 - ADAM_B1 ** ADAM_STEP)
    c2 = 1.0 / (1.0 - ADAM_B2 ** ADAM_STEP)

    def body(p_ref, w_ref, m_ref, v_ref, g_ref, d_ref, nm_ref, nv_ref):
        g = p_ref[0].astype(F32)
        for s in range(1, N_DEV):
            g = g + p_ref[s].astype(F32)
        mn = ADAM_B1 * m_ref[...] + (1.0 - ADAM_B1) * g
        vn = ADAM_B2 * v_ref[...] + (1.0 - ADAM_B2) * (g * g)
        g_ref[...] = g
        nm_ref[...] = mn
        nv_ref[...] = vn
        d_ref[...] = -ADAM_LR * ((mn * c1) / (jnp.sqrt(vn * c2) + ADAM_EPS) + ADAM_WD * w_ref[...])

    blk = pl.BlockSpec((tr, tc), lambda i, j: (i, j))
    return pl.pallas_call(
        body, grid=(r // tr, c // tc),
        in_specs=[pl.BlockSpec((N_DEV, tr, tc), lambda i, j: (0, i, j)), blk, blk, blk],
        out_specs=[blk] * 4, out_shape=[_sds((r, c), F32)] * 4, name=name,
        compiler_params=_params(2, VMEM_LIMIT))(parts, w, m, v)


def _pad_rows(a, rows):
    return jnp.pad(a, ((0, rows - a.shape[0]), (0, 0)))


def _lane_row(vec8, offset):
    return jnp.pad(vec8.reshape(1, 8), ((0, 0), (offset, HD - 8 - offset)))


def kernel(x, positions, attn_norm_w, w_in, conv_w, a_log, dt_bias, delta_out_norm_w, q_norm_w, k_norm_w, attn_out_norm_w, w_out, ffn_norm_w, w_gate_up, w_down, loss_target, m_attn_norm_w, m_w_in, m_conv_w, m_a_log, m_dt_bias, m_delta_out_norm_w, m_q_norm_w, m_k_norm_w, m_attn_out_norm_w, m_w_out, m_ffn_norm_w, m_w_gate_up, m_w_down, v_attn_norm_w, v_w_in, v_conv_w, v_a_log, v_dt_bias, v_delta_out_norm_w, v_q_norm_w, v_k_norm_w, v_attn_out_norm_w, v_w_out, v_ffn_norm_w, v_w_gate_up, v_w_down):
    x2 = x[0]
    t, d = x2.shape
    target = loss_target[0]
    pos_col = positions.reshape(t, 1)
    half = HD // 2
    inv = (ROPE_THETA ** (-np.arange(half, dtype=np.float32) / half)).astype(np.float32)
    inv_row = jnp.asarray(np.concatenate([inv, inv]).reshape(1, HD))

    n_in = w_in.shape[2]
    n_gu = w_gate_up.shape[2]
    w_in_g, conv_g = _gather_two_level("gather_in", [w_in[0].astype(BF16), _pad_rows(conv_w[0], 8)])
    out_fly = _exchange_start("gather_out_start", [w_out[0].astype(BF16)], [False], conv_g)
    gu_fly = _exchange_start("gather_gate_up_start", [w_gate_up[0].astype(BF16)], [False], out_fly["token"])
    down_fly = _exchange_start("gather_down_start", [w_down[0].astype(BF16)], [False], gu_fly["token"])
    n_main = 4 * GW
    n_small = 2 * N_HEADS
    segments = [(0, n_main, 0), (n_main + n_small, N_DEV * n_in, n_main), (n_main, n_main + n_small, 7 * GW)]
    pieces = []
    for lo, hi, _ in segments:
        f = lo
        while f < hi:
            j = f // n_in
            end = min(hi, (j + 1) * n_in)
            pieces.append(w_in_g[j][:, f - j * n_in:end - j * n_in])
            f = end
    w_cat = jnp.concatenate(pieces + [jnp.zeros((d, HD - n_small), BF16)], axis=1)
    n_cat = w_cat.shape[1]
    small_blk = (7 * GW) // HD
    conv_w8 =jnp.transpose(conv_g, (1, 0, 2)).reshape(8, 3 * GW)
    alog_row = _lane_row(a_log[0], 8)
    dtb_row = _lane_row(dt_bias[0], 8)

    tm = min(2048, t)
    h1 = _rms_fwd("norm1", x2, attn_norm_w, down_fly["token"])
    tmp, tnp = min(1024, t), n_cat // 3
    proj = _mm("in_proj", h1, w_cat, grid=(t // tmp, n_cat // tnp, 1),
               a_spec=pl.BlockSpec((tmp, d), lambda i, j, k: (i, 0)),
               b_spec=pl.BlockSpec((d, tnp), lambda i, j, k: (0, j)),
               o_spec=pl.BlockSpec((tmp, tnp), lambda i, j, k: (i, j)),
               out_shape=_sds((t, n_cat), F32), ca=1, cb=0, nk=1)
    qn = _conv_fwd("conv_q", proj, conv_w8, 0, True, HD ** -0.5)
    kn = _conv_fwd("conv_k", proj, conv_w8, 1, True, 1.0)
    vv = _conv_fwd("conv_v", proj, conv_w8, 2, False, 1.0)
    beta_b, gc_b = _gates_fwd("gates", proj, small_blk, alog_row, dtb_row)
    u, w, p, tinv, qd, kd = _delta_prep("delta_prep", qn, kn, vv, beta_b, gc_b)
    oa_raw, vn, s_hist = _delta_scan("delta_scan", u, w, p, qd, kd, gc_b)

    cos_t, sin_t = _rope_tables("rope_tables", pos_col, inv_row)
    aq, ak = _qk_fwd("attn_qk", proj, 2, q_norm_w, k_norm_w, cos_t, sin_t)
    ob, lse = _attn_fwd("attn_fwd", aq, ak, proj, 6)
    mixed = _mix_fwd("mix", oa_raw, proj, 3, ob, delta_out_norm_w, attn_out_norm_w)
    (w_out_g,) = _exchange_wait("gather_out_wait", out_fly, mixed)
    w_out_full = w_out_g.reshape(2 * GW, d)
    tn = 512
    x1, h2 = _out_proj_norm("out_proj", mixed, w_out_full, x2, ffn_norm_w)
    per = N_DEV // 2
    (w_gu_g,) = _exchange_wait("gather_gate_up_wait", gu_fly, h2)
    gu3, act = _gate_up_swiglu("gate_up", h2, w_gu_g)
    (w_down_g,) = _exchange_wait("gather_down_wait", down_fly, act)
    w_down_full = w_down_g.reshape(D_FF, d)
    tmd = min(1024, t)
    dy, dy16, loss_tile = _down_loss("down_proj", act, w_down_full, x1, target)

    tk, nkt = t, 1
    g_down = _mm("g_down", act, dy16, grid=(D_FF // 1408, d // 512, nkt),
                 a_spec=pl.BlockSpec((tk, 1408), lambda i, j, k: (k, i)),
                 b_spec=pl.BlockSpec((tk, 512), lambda i, j, k: (k, j)),
                 o_spec=pl.BlockSpec((1408, 512), lambda i, j, k: (i, j)),
                 out_shape=_sds((D_FF, d), F32), ca=0, cb=0, nk=nkt)
    down_g_fly = _exchange_start("reduce_down_start", [g_down.reshape(N_DEV, D_FF // N_DEV, d)], [True], dy16)
    dgu3 = _d_gate_up("d_gate_up", dy16, w_down_full, gu3, down_g_fly["token"])
    g_gu = _mm("g_gate_up", h2, dgu3, grid=(d // 512, N_DEV, nkt),
               a_spec=pl.BlockSpec((tk, 512), lambda i, j, k: (k, i)),
               b_spec=pl.BlockSpec((None, tk, n_gu), lambda i, j, k: (j // per, k, j % per)),
               o_spec=pl.BlockSpec((None, 512, n_gu), lambda i, j, k: (j, i, 0)),
               out_shape=_sds((N_DEV, d, n_gu), F32), ca=0, cb=0, nk=nkt)
    gu_g_fly = _exchange_start("reduce_gate_up_start", [g_gu], [True], dy16)
    dh2 = _d_h2("d_h2", dgu3, w_gu_g, gu_g_fly["token"])
    dx1, dx1_16, g_ffn_norm = _rms_bwd("norm2_bwd", x1, ffn_norm_w, dh2, dy)

    g_out = _mm("g_out", mixed, dx1_16, grid=((2 * GW) // 512, 1, nkt),
                a_spec=pl.BlockSpec((tk, 512), lambda i, j, k: (k, i)),
                b_spec=pl.BlockSpec((tk, d), lambda i, j, k: (k, 0)),
                o_spec=pl.BlockSpec((512, d), lambda i, j, k: (i, 0)),
                out_shape=_sds((2 * GW, d), F32), ca=0, cb=0, nk=nkt)
    out_g_fly = _exchange_start("reduce_out_start", [g_out.reshape(N_DEV, (2 * GW) // N_DEV, d)], [True], g_ffn_norm)
    doa, dproj, dob, delta, g_dn, g_an = _mix_bwd("mix_bwd", dx1_16, w_out_full, oa_raw, proj, 3, ob,
                                                  delta_out_norm_w, attn_out_norm_w, out_g_fly["token"])
    d_aq, d_ak, d_av = _attn_bwd("attn_bwd", aq, ak, proj, 6, dob, lse, delta)
    dproj, g_qn, g_kn = _qk_bwd("attn_qk_bwd", proj, 2, q_norm_w, k_norm_w, cos_t, sin_t, d_aq, d_ak, dproj)
    dproj = _cast_into("attn_v_bwd", d_av, dproj, 6)

    dvn, dqd, dkd, dw, ddec = _delta_scan_bwd("delta_scan_bwd", doa, w, p, qd, kd, gc_b, vn, s_hist)
    dqn, dkn, dvv, dbeta_b, dg_b = _delta_prep_bwd("delta_prep_bwd", qn, kn, vv, beta_b, gc_b, tinv, u, w, vn,
                                                   doa, dvn, dqd, dkd, dw, ddec)
    dproj, gcw_q = _conv_bwd("conv_q_bwd", proj, conv_w8, dqn, dproj, 0, True, HD ** -0.5)
    dproj, gcw_k = _conv_bwd("conv_k_bwd", proj, conv_w8, dkn, dproj, 1, True, 1.0)
    dproj, gcw_v = _conv_bwd("conv_v_bwd", proj, conv_w8, dvv, dproj, 2, False, 1.0)
    dproj, g_alog_row, g_dtb_row = _gates_bwd("gates_bwd", proj, small_blk, alog_row, dtb_row, dbeta_b, dg_b, dproj)
    tmc = 384
    g_cat = _mm("g_in", dproj, h1, grid=(n_cat // tmc, 1, nkt),
                a_spec=pl.BlockSpec((tk, tmc), lambda i, j, k: (k, i)),
                b_spec=pl.BlockSpec((tk, d), lambda i, j, k: (k, 0)),
                o_spec=pl.BlockSpec((tmc, d), lambda i, j, k: (i, 0)),
                out_shape=_sds((n_cat, d), BF16), ca=0, cb=0, nk=nkt)
    parts = []
    for j in range(N_DEV):
        cols = []
        for lo, hi, start in sorted(segments):
            a, b = max(lo, j * n_in), min(hi, (j + 1) * n_in)
            if a < b:
                cols.append(g_cat[start + a - lo:start + b - lo])
        parts.append(cols[0] if len(cols) == 1 else jnp.concatenate(cols, axis=0))
    g_in_parts = jnp.stack(parts)
    g_conv = jnp.concatenate([gcw_q, gcw_k, gcw_v], axis=1)
    n_cw = conv_w.shape[2]
    g_conv_parts = jnp.transpose(g_conv.reshape(8, N_DEV, n_cw), (1, 0, 2))
    in_g_fly = _exchange_start("reduce_in_start", [g_in_parts, g_conv_parts], [True] * 2, g_dtb_row)
    tmh1 = min(512, t)
    dh1 = _mm("d_h1", dproj, w_cat, dep=in_g_fly["token"], grid=(t // tmh1, d // 1024, 1),
              a_spec=pl.BlockSpec((tmh1, n_cat), lambda i, j, k: (i, 0)),
              b_spec=pl.BlockSpec((1024, n_cat), lambda i, j, k: (j, 0)),
              o_spec=pl.BlockSpec((tmh1, 1024), lambda i, j, k: (i, j)),
              out_shape=_sds((t, d), F32), ca=1, cb=1, nk=1)
    grad_x, _, g_attn_norm = _rms_bwd("norm1_bwd", x2, attn_norm_w, dh1, dx1)

    small_rows = [g_attn_norm.reshape(d // HD, HD), g_ffn_norm.reshape(d // HD, HD), g_dn, g_qn, g_kn, g_an,
                  g_alog_row, g_dtb_row, loss_tile[:1]]
    loss_row = sum(r.shape[0] for r in small_rows) - 1
    small_pack = _pad_rows(jnp.concatenate(small_rows, axis=0), 40)
    (r_down,) = _exchange_wait("reduce_down_wait", down_g_fly, grad_x)
    (r_gu,) = _exchange_wait("reduce_gate_up_wait", gu_g_fly, grad_x)
    (r_out,) = _exchange_wait("reduce_out_wait", out_g_fly, grad_x)
    res_gu = [a[None] for a in _adamw("adamw_gate_up", r_gu, w_gate_up[0], m_w_gate_up[0], v_w_gate_up[0])]
    res_down = [a[None] for a in _adamw("adamw_down", r_down, w_down[0], m_w_down[0], v_w_down[0])]
    res_out = [a[None] for a in _adamw("adamw_out", r_out, w_out[0], m_w_out[0], v_w_out[0])]
    done = (res_gu[3][0, :1, :1] + res_down[3][0, :1, :1] + res_out[3][0, :1, :1])
    r_in, r_conv = _exchange_wait("reduce_in_wait", in_g_fly, done)
    small_fly = _exchange_start("gather_small_start", [small_pack], [False], r_in)
    upd_in = _adamw("adamw_in", r_in, jnp.transpose(w_in[0]), jnp.transpose(m_w_in[0]), jnp.transpose(v_w_in[0]))
    res_in = [jnp.transpose(a)[None] for a in upd_in]
    (r_small,) = _exchange_wait("gather_small_wait", small_fly, upd_in[0])

    def pack_small(an, fn, dn, qn_, kn_, aon, al, db):
        rows = [an.reshape(d // HD, HD), fn.reshape(d // HD, HD), dn, qn_, kn_, aon,
                _lane_row(al[0], 8), _lane_row(db[0], 8)]
        return _pad_rows(jnp.concatenate(rows, axis=0), 40)

    def unpack_small(pk):
        nr = d // HD
        return dict(attn_norm_w=pk[:nr].reshape(1, d), ffn_norm_w=pk[nr:2 * nr].reshape(1, d),
                    delta_out_norm_w=pk[2 * nr:2 * nr + 1], q_norm_w=pk[2 * nr + 1:2 * nr + 2],
                    k_norm_w=pk[2 * nr + 2:2 * nr + 3], attn_out_norm_w=pk[2 * nr + 3:2 * nr + 4],
                    a_log=pk[2 * nr + 4:2 * nr + 5, 8:16], dt_bias=pk[2 * nr + 5:2 * nr + 6, 8:16])

    res_small = _adamw("adamw_small", r_small,
                       pack_small(attn_norm_w, ffn_norm_w, delta_out_norm_w, q_norm_w, k_norm_w, attn_out_norm_w, a_log, dt_bias),
                       pack_small(m_attn_norm_w, m_ffn_norm_w, m_delta_out_norm_w, m_q_norm_w, m_k_norm_w, m_attn_out_norm_w, m_a_log, m_dt_bias),
                       pack_small(v_attn_norm_w, v_ffn_norm_w, v_delta_out_norm_w, v_q_norm_w, v_k_norm_w, v_attn_out_norm_w, v_a_log, v_dt_bias))
    small = [unpack_small(a) for a in res_small]
    res_conv =[a[None, :4] for a in _adamw("adamw_conv", r_conv, _pad_rows(conv_w[0], 8), _pad_rows(m_conv_w[0], 8),
                                            _pad_rows(v_conv_w[0], 8))]

    loss = jnp.sum(r_small[:, loss_row, 0])
    outs = [loss, grad_x[None]]
    for i in range(4):
        s = small[i]
        outs += [s["attn_norm_w"], res_in[i], res_conv[i], s["a_log"], s["dt_bias"], s["delta_out_norm_w"],
                 s["q_norm_w"], s["k_norm_w"], s["attn_out_norm_w"], res_out[i], s["ffn_norm_w"], res_gu[i],
                 res_down[i]]
    return tuple(outs)
```

```python
import numpy as np
import jax
import jax.numpy as jnp
from jax import lax
from jax.experimental import pallas as pl
from jax.experimental.pallas import tpu as pltpu

F32 = jnp.float32
BF16 = jnp.bfloat16

N_DEV = 8
N_HEADS = 8
HD = 128
GW = N_HEADS * HD
CHUNK = 64
PAIR = 2 * CHUNK
SCAN_CHUNKS = 4
SCAN_ROWS = SCAN_CHUNKS * CHUNK
SPAN = 128
DILATIONS = (1, 4, 16)
ROPE_THETA = 10000.0
EPS = 1e-6
D_FF = 5632
ADAM_LR, ADAM_B1, ADAM_B2, ADAM_EPS, ADAM_WD, ADAM_STEP = 0.001, 0.9, 0.999, 1e-8, 0.01, 10
NEG = -1e30
VMEM_LIMIT = 56 * 1024 * 1024
ANY = pl.BlockSpec(memory_space=pl.ANY)
HEADS_PER_STEP = 8


def _params(n_grid, vmem=VMEM_LIMIT):
    return pltpu.CompilerParams(dimension_semantics=("arbitrary",) * n_grid, vmem_limit_bytes=vmem)


def _sds(shape, dtype):
    return jax.ShapeDtypeStruct(tuple(shape), dtype)


def _sigmoid(x):
    return 1.0 / (1.0 + jnp.exp(-x))


def _silu(x):
    return x * _sigmoid(x)


def _softplus(x):
    return jnp.maximum(x, 0.0) + jnp.log(1.0 + jnp.exp(-jnp.abs(x)))


def _dot(a, b, ca, cb):
    return lax.dot_general(a, b, (((ca,), (cb,)), ((), ())), preferred_element_type=F32)


def _b16(x):
    return x if x.dtype == BF16 else x.astype(BF16)


def _split(x):
    hi = x.astype(BF16)
    return hi, (x - hi.astype(F32)).astype(BF16)


def _dot3(a, b, ca, cb):
    a_hi, a_lo = _split(a)
    b_hi, b_lo = _split(b)
    return _dot(a_hi, b_hi, ca, cb) + (_dot(a_hi, b_lo, ca, cb) + _dot(a_lo, b_hi, ca, cb))


def _iota2(shape, axis):
    return lax.broadcasted_iota(jnp.int32, shape, axis)


def _mm(name, a, b, *, grid, a_spec, b_spec, o_spec, out_shape, ca, cb, nk, dep=None):
    assert nk == 1 and grid[2] == 1

    def body(*refs):
        refs[-1][...] = _dot(_b16(refs[0][...]), _b16(refs[1][...]), ca, cb).astype(refs[-1].dtype)

    in_specs = [a_spec, b_spec] + ([ANY] if dep is not None else [])
    args = (a, b) + ((dep,) if dep is not None else ())
    return pl.pallas_call(body, grid=grid, in_specs=in_specs, out_specs=o_spec, out_shape=out_shape,
                          name=name, compiler_params=_params(3))(*args)


def _rms_f(xv, wv):
    return xv * lax.rsqrt(jnp.mean(xv * xv, axis=-1, keepdims=True) + EPS) * wv


def _rms_fwd(name, x, w, dep):
    t, d = x.shape
    tm = min(512, t)

    def body(x_ref, w_ref, dep_ref, o_ref):
        o_ref[...] = _rms_f(x_ref[...], w_ref[...]).astype(BF16)

    row = pl.BlockSpec((tm, d), lambda i: (i, 0))
    vec = pl.BlockSpec((1, d), lambda i: (0, 0))
    return pl.pallas_call(body, grid=(t // tm,), in_specs=[row, vec, ANY], out_specs=row,
                          out_shape=_sds((t, d), BF16), name=name, compiler_params=_params(1))(x, w, dep)


def _rms_bwd(name, x, w, dh, res):
    t, d = x.shape
    tm = min(256, t)

    def body(x_ref, w_ref, dh_ref, res_ref, dx_ref, dx16_ref, dw_ref):
        _, vjp = jax.vjp(_rms_f, x_ref[...], w_ref[...])
        dxv, dwv = vjp(dh_ref[...])
        dxv = dxv + res_ref[...]
        dx_ref[...] = dxv
        dx16_ref[...] = dxv.astype(BF16)

        @pl.when(pl.program_id(0) == 0)
        def _():
            dw_ref[...] = jnp.zeros_like(dw_ref)

        dw_ref[...] += dwv

    row = pl.BlockSpec((tm, d), lambda i: (i, 0))
    vec = pl.BlockSpec((1, d), lambda i: (0, 0))
    return pl.pallas_call(body, grid=(t // tm,), in_specs=[row, vec, row, row], out_specs=[row, row, vec],
                          out_shape=[_sds((t, d), F32), _sds((t, d), BF16), _sds((1, d), F32)], name=name,
                          compiler_params=_params(1))(x, w, dh, res)


def _shift_rows(x, s):
    t = x.shape[0]
    r = pltpu.roll(x, s % t, 0)
    row8 = _iota2((8, x.shape[1]), 0)
    if s > 0:
        return jnp.concatenate([jnp.where(row8 >= s, r[:8], 0.0), r[8:]], axis=0)
    return jnp.concatenate([r[:t - 8], jnp.where(row8 < 8 + s, r[t - 8:], 0.0)], axis=0)


def _conv_taps(xv, w_ref):
    c = w_ref[3:4, :] * xv
    for s in (1, 2, 3):
        c = c + w_ref[3 - s:4 - s, :] * _shift_rows(xv, s)
    return c


def _post_conv(c, l2, scale):
    y = _silu(c)
    if l2:
        y = y * lax.rsqrt(jnp.sum(y * y, axis=-1, keepdims=True) + EPS) * scale
    return y


def _conv_fwd(name, proj, conv_w8, group, l2, scale):
    t = proj.shape[0]

    def body(x_ref, w_ref, o_ref):
        o_ref[...] = _post_conv(_conv_taps(x_ref[...], w_ref), l2, scale)

    return pl.pallas_call(
        body, grid=(N_HEADS,),
        in_specs=[pl.BlockSpec((t, HD), lambda h: (0, h + group * N_HEADS)),
                  pl.BlockSpec((8, HD), lambda h: (0, h + group * N_HEADS))],
        out_specs=pl.BlockSpec((t, HD), lambda h: (0, h)),
        out_shape=_sds((t, GW), F32), name=name, compiler_params=_params(1, VMEM_LIMIT))(proj, conv_w8)


def _conv_bwd(name, proj, conv_w8, dn, dproj, group, l2, scale):
    t = proj.shape[0]

    def body(x_ref, w_ref, dn_ref, dproj_ref, dx_ref, dw_ref):
        xv = x_ref[...]
        c = _conv_taps(xv, w_ref)
        _, vjp = jax.vjp(lambda cc: _post_conv(cc, l2, scale), c)
        (dc,) = vjp(dn_ref[...])
        dx = w_ref[3:4, :] * dc
        dw = jnp.zeros((8, HD), F32)
        rid = _iota2((8, HD), 0)
        dw = dw + jnp.where(rid == 3, jnp.sum(dc * xv, axis=0, keepdims=True), 0.0)
        for s in (1, 2, 3):
            dx = dx + w_ref[3 - s:4 - s, :] * _shift_rows(dc, -s)
            dw = dw + jnp.where(rid == 3 - s, jnp.sum(dc * _shift_rows(xv, s), axis=0, keepdims=True), 0.0)
        dx_ref[...] = dx.astype(BF16)
        dw_ref[...] = dw

    return pl.pallas_call(
        body, grid=(N_HEADS,),
        in_specs=[pl.BlockSpec((t, HD), lambda h: (0, h + group * N_HEADS)),
                  pl.BlockSpec((8, HD), lambda h: (0, h + group * N_HEADS)),
                  pl.BlockSpec((t, HD), lambda h: (0, h)), ANY],
        out_specs=[pl.BlockSpec((t, HD), lambda h: (0, h + group * N_HEADS)), pl.BlockSpec((8, HD), lambda h: (0, h))],
        out_shape=[_sds(dproj.shape, BF16), _sds((8, GW), F32)], input_output_aliases={3: 0}, name=name,
        compiler_params=_params(1, VMEM_LIMIT))(proj, conv_w8, dn, dproj)


def _chunk_cumsum(g, rows):
    pos = rows % CHUNK
    s = 1
    while s < CHUNK:
        g = g + jnp.where(pos >= s, pltpu.roll(g, s, 0), 0.0)
        s *= 2
    return g


def _gates_fwd(name, proj, small_blk, alog_row, dtb_row):
    t = proj.shape[0]
    tm = min(512, t)

    def body(s_ref, a_ref, b_ref, beta_ref, gc_ref):
        sm = s_ref[...]
        beta = _sigmoid(sm)
        g = -jnp.exp(a_ref[...]) * _softplus(sm + b_ref[...])
        gc = _chunk_cumsum(g, _iota2((tm, HD), 0))
        lane = _iota2((tm, HD), 1)
        for h in range(N_HEADS):
            bcol = jnp.sum(jnp.where(lane == h, beta, 0.0), axis=1, keepdims=True)
            gcol = jnp.sum(jnp.where(lane == 8 + h, gc, 0.0), axis=1, keepdims=True)
            beta_ref[:, h * HD:(h + 1) * HD] = jnp.broadcast_to(bcol, (tm, HD))
            gc_ref[:, h * HD:(h + 1) * HD] = jnp.broadcast_to(gcol, (tm, HD))

    vec = pl.BlockSpec((1, HD), lambda i: (0, 0))
    wide = pl.BlockSpec((tm, GW), lambda i: (i, 0))
    return pl.pallas_call(
        body, grid=(t // tm,),
        in_specs=[pl.BlockSpec((tm, HD), lambda i: (i, small_blk)), vec, vec], out_specs=[wide, wide],
        out_shape=[_sds((t, GW), F32), _sds((t, GW), F32)], name=name,
        compiler_params=_params(1))(proj, alog_row, dtb_row)


def _gates_bwd(name, proj, small_blk, alog_row, dtb_row, dbeta_b, dg_b, dproj):
    t = proj.shape[0]
    tm = min(512, t)

    def body(s_ref, a_ref, b_ref, db_ref, dg_ref, dproj_ref, ds_ref, da_ref, dbias_ref):
        sm = s_ref[...]
        lane = _iota2((tm, HD), 1)
        db = jnp.zeros((tm, HD), F32)
        dg = jnp.zeros((tm, HD), F32)
        for h in range(N_HEADS):
            db = db + jnp.where(lane == h, db_ref[:, h * HD:(h + 1) * HD], 0.0)
            dg = dg + jnp.where(lane == 8 + h, dg_ref[:, h * HD:(h + 1) * HD], 0.0)
        beta = _sigmoid(sm)
        ea = jnp.exp(a_ref[...])
        pre = sm + b_ref[...]
        g = -ea * _softplus(pre)
        dpre = dg * (-ea) * _sigmoid(pre)
        ds_ref[...] = (db * beta * (1.0 - beta) + dpre).astype(BF16)

        @pl.when(pl.program_id(0) == 0)
        def _():
            da_ref[...] = jnp.zeros_like(da_ref)
            dbias_ref[...] = jnp.zeros_like(dbias_ref)

        da_ref[...] += jnp.sum(dg * g, axis=0, keepdims=True)
        dbias_ref[...] += jnp.sum(dpre, axis=0, keepdims=True)

    vec = pl.BlockSpec((1, HD), lambda i: (0, 0))
    wide = pl.BlockSpec((tm, GW), lambda i: (i, 0))
    return pl.pallas_call(
        body, grid=(t // tm,),
        in_specs=[pl.BlockSpec((tm, HD), lambda i: (i, small_blk)), vec, vec, wide, wide, ANY],
        out_specs=[pl.BlockSpec((tm, HD), lambda i: (i, small_blk)), vec, vec],
        out_shape=[_sds(dproj.shape, BF16), _sds((1, HD), F32), _sds((1, HD), F32)],
        input_output_aliases={5: 0}, name=name,
        compiler_params=_params(1))(proj, alog_row, dtb_row, dbeta_b, dg_b, dproj)


def _pair_masks():
    ii = _iota2((PAIR, PAIR), 0)
    jj = _iota2((PAIR, PAIR), 1)
    same = (ii // CHUNK) == (jj // CHUNK)
    return ii, jj, same & (ii >= jj), same & (ii > jj)


def _to_row(col_b, ii, jj):
    return jnp.sum(jnp.where(ii == jj, col_b, 0.0), axis=0, keepdims=True)


def _to_col(row, ii, jj):
    return jnp.sum(jnp.where(ii == jj, jnp.broadcast_to(row, (PAIR, PAIR)), 0.0), axis=1, keepdims=True)


def _decay_parts(gc, last_a, last_b, ii, jj, causal):
    diff = gc - _to_row(gc, ii, jj)
    dmat = jnp.where(causal, jnp.exp(jnp.where(causal, diff, 0.0)), 0.0)
    glast = jnp.where(ii < CHUNK, last_a, last_b)
    return dmat, jnp.exp(gc), jnp.exp(glast - gc)


def _unit_lower_inverse(lows, ii, jj):
    eye = jnp.where(ii == jj, 1.0, 0.0)
    mm = lambda xs, ys: [_dot3(a, b, 1, 0) for a, b in zip(xs, ys)]
    plus = lambda xs: [eye + a for a in xs]
    minus = lambda xs: [eye - a for a in xs]
    d1 = [jnp.where((ii // 16) == (jj // 16), low, 0.0) for low in lows]
    d2 = mm(d1, d1)
    a = mm(minus(d1), plus(d2))
    d4 = mm(d2, d2)
    a = mm(a, plus(d4))
    d8 = mm(d4, d4)
    td = mm(a, plus(d8))
    n1 = mm(td, [low - d for low, d in zip(lows, d1)])
    n2 = mm(n1, n1)
    return mm(mm(minus(n1), plus(n2)), td)


def _delta_prep(name, qn, kn, vv, beta_b, gc_b):
    t = qn.shape[0]

    def body(q_ref, k_ref, v_ref, b_ref, g_ref, u_ref, w_ref, p_ref, t_ref, qd_ref, kd_ref):
        ii, jj, causal, strict = _pair_masks()
        sls = [slice(hh * HD, (hh + 1) * HD) for hh in range(HEADS_PER_STEP)]
        lows = []
        for sl in sls:
            q, k, beta = q_ref[:, sl], k_ref[:, sl], b_ref[:, sl]
            dmat, gam, e2 = _decay_parts(g_ref[:, sl], g_ref[CHUNK - 1:CHUNK, sl], g_ref[PAIR - 1:PAIR, sl],
                                         ii, jj, causal)
            k16 = _b16(k)
            lows.append(jnp.where(strict, beta * _dot(k16, k16, 1, 1) * dmat, 0.0))
            p_ref[:, sl] = jnp.where(causal, _dot(_b16(q), k16, 1, 1) * dmat, 0.0).astype(BF16)
            qd_ref[:, sl] = (q * gam).astype(BF16)
            kd_ref[:, sl] = (k * e2).astype(BF16)
        for sl, tinv in zip(sls, _unit_lower_inverse(lows, ii, jj)):
            beta = b_ref[:, sl]
            t_ref[:, sl] = tinv
            u_ref[:, sl] = _dot3(tinv, v_ref[:, sl] * beta, 1, 0)
            w_ref[:, sl] = _dot3(tinv, k_ref[:, sl] * (beta * jnp.exp(g_ref[:, sl])), 1, 0).astype(BF16)

    blk = pl.BlockSpec((PAIR, HEADS_PER_STEP * HD), lambda i, h: (i, h))
    return pl.pallas_call(
        body, grid=(t // PAIR, N_HEADS // HEADS_PER_STEP), in_specs=[blk] * 5, out_specs=[blk] * 6,
        out_shape=[_sds((t, GW), F32), _sds((t, GW), BF16), _sds((t, GW), BF16), _sds((t, GW), F32),
                   _sds((t, GW), BF16), _sds((t, GW), BF16)],
        name=name, compiler_params=_params(2))(qn, kn, vv, beta_b, gc_b)


def _delta_scan(name, u, w, p, qd, kd, gc_b):
    t = u.shape[0]
    n = t // CHUNK

    def body(u_ref, w_ref, p_ref, qd_ref, kd_ref, g_ref, o_ref, vn_ref, sh_ref, state):
        @pl.when(pl.program_id(0) == 0)
        def _():
            state[...] = jnp.zeros_like(state)

        sls = [slice(h * HD, (h + 1) * HD) for h in range(N_HEADS)]
        heads = range(N_HEADS)
        s = [state[h] for h in heads]
        for c in range(SCAN_CHUNKS):
            rows = slice(c * CHUNK, (c + 1) * CHUNK)
            last = slice((c + 1) * CHUNK - 1, (c + 1) * CHUNK)
            for h in heads:
                sh_ref[c, h] = s[h]
            s16 = [_b16(a) for a in s]
            ws = [_dot(w_ref[rows, sls[h]], s16[h], 1, 0) for h in heads]
            qs = [_dot(qd_ref[rows, sls[h]], s16[h], 1, 0) for h in heads]
            vn16 = [_b16(u_ref[rows, sls[h]] - ws[h]) for h in heads]
            pv = [_dot(p_ref[rows, sls[h]], jnp.concatenate([vn16[h], vn16[h]], axis=0), 1, 0) for h in heads]
            kv = [_dot(kd_ref[rows, sls[h]], vn16[h], 0, 0) for h in heads]
            for h in heads:
                o_ref[rows, sls[h]] = qs[h] + pv[h]
                vn_ref[rows, sls[h]] = vn16[h]
            s = [s[h] * jnp.exp(g_ref[last, sls[h]]) + kv[h] for h in heads]
        for h in heads:
            state[h] = s[h]

    blk = pl.BlockSpec((SCAN_ROWS, GW), lambda i: (i, 0))
    return pl.pallas_call(
        body, grid=(t // SCAN_ROWS,), in_specs=[blk] * 6,
        out_specs=[blk, blk, pl.BlockSpec((SCAN_CHUNKS, N_HEADS, HD, HD), lambda i: (i, 0, 0, 0))],
        out_shape=[_sds((t, GW), F32), _sds((t, GW), BF16), _sds((n, N_HEADS, HD, HD), F32)],
        scratch_shapes=[pltpu.VMEM((N_HEADS, HD, HD), F32)], name=name,
        compiler_params=_params(1))(u, w, p, qd, kd, gc_b)


def _delta_scan_bwd(name, do, w, p, qd, kd, gc_b, vn, s_hist):
    t = do.shape[0]
    n = t // CHUNK

    def body(do_ref, w_ref, p_ref, qd_ref, kd_ref, g_ref, vn_ref, sh_ref,
             dvn_ref, dqd_ref, dkd_ref, dw_ref, ddec_ref, dstate):
        @pl.when(pl.program_id(0) == 0)
        def _():
            dstate[...] = jnp.zeros_like(dstate)

        sls = [slice(h * HD, (h + 1) * HD) for h in range(N_HEADS)]
        heads = range(N_HEADS)
        ds = [dstate[h] for h in heads]
        for c in reversed(range(SCAN_CHUNKS)):
            rows = slice(c * CHUNK, (c + 1) * CHUNK)
            last = slice((c + 1) * CHUNK - 1, (c + 1) * CHUNK)
            ds16 = [_b16(a) for a in ds]
            s16 = [_b16(sh_ref[c, h]) for h in heads]
            do16 = [_b16(do_ref[rows, sls[h]]) for h in heads]
            ptdo = [_dot(p_ref[rows, sls[h]], do16[h], 0, 0) for h in heads]
            kds = [_dot(kd_ref[rows, sls[h]], ds16[h], 1, 0) for h in heads]
            qdo = [_dot(qd_ref[rows, sls[h]], do16[h], 0, 0) for h in heads]
            for h in heads:
                dqd_ref[rows, sls[h]] = _dot(do16[h], s16[h], 1, 1)
                dkd_ref[rows, sls[h]] = _dot(vn_ref[rows, sls[h]], ds16[h], 1, 1)
            dvn = [ptdo[h][:CHUNK, :] + ptdo[h][CHUNK:, :] + kds[h] for h in heads]
            dvn16 = [_b16(a) for a in dvn]
            wdv = [_dot(w_ref[rows, sls[h]], dvn16[h], 0, 0) for h in heads]
            for h in heads:
                dvn_ref[rows, sls[h]] = dvn[h]
                dw_ref[rows, sls[h]] = -_dot(dvn16[h], s16[h], 1, 1)
                tot = jnp.sum(jnp.sum(sh_ref[c, h] * ds[h], axis=1, keepdims=True), axis=0, keepdims=True)
                ddec_ref[c * 8:(c + 1) * 8, sls[h]] = jnp.broadcast_to(tot, (8, HD))
            ds = [ds[h] * jnp.exp(g_ref[last, sls[h]]) + qdo[h] - wdv[h] for h in heads]
        for h in heads:
            dstate[h] = ds[h]

    npair = t // SCAN_ROWS
    blk = pl.BlockSpec((SCAN_ROWS, GW), lambda i: (npair - 1 - i, 0))
    return pl.pallas_call(
        body, grid=(npair,),
        in_specs=[blk] * 7 + [pl.BlockSpec((SCAN_CHUNKS, N_HEADS, HD, HD), lambda i: (npair - 1 - i, 0, 0, 0))],
        out_specs=[blk] * 4 + [pl.BlockSpec((8 * SCAN_CHUNKS, GW), lambda i: (npair - 1 - i, 0))],
        out_shape=[_sds((t, GW), F32)] * 4 + [_sds((n * 8, GW), F32)],
        scratch_shapes=[pltpu.VMEM((N_HEADS, HD, HD), F32)], name=name,
        compiler_params=_params(1))(do, w, p, qd, kd, gc_b, vn, s_hist)


def _delta_prep_bwd(name, qn, kn, vv, beta_b, gc_b, tinv, u, w, vn, do, dvn, dqd, dkd, dw, ddec):
    t = qn.shape[0]

    def body(q_ref, k_ref, v_ref, b_ref, g_ref, t_ref, u_ref, w_ref, vn_ref, do_ref, dvn_ref, dqd_ref,
             dkd_ref, dw_ref, ddec_ref, dq_ref, dk_ref, dv_ref, dbeta_ref, dg_ref):
        ii, jj, causal, strict = _pair_masks()
        suffix = ((ii // CHUNK) == (jj // CHUNK)) & (jj >= ii)
        first = ii < CHUNK
        rs = lambda a: jnp.sum(a, axis=1, keepdims=True)
        sls = [slice(hh * HD, (hh + 1) * HD) for hh in range(HEADS_PER_STEP)]
        xs = [_dot3(t_ref[:, sl], dvn_ref[:, sl], 0, 0) for sl in sls]
        ys = [_dot3(t_ref[:, sl], dw_ref[:, sl], 0, 0) for sl in sls]
        k16s = [_b16(k_ref[:, sl]) for sl in sls]
        kks = [_dot(k16, k16, 1, 1) for k16 in k16s]
        qks = [_dot(_b16(q_ref[:, sl]), k16, 1, 1) for sl, k16 in zip(sls, k16s)]
        dps = [jnp.where(causal, _dot(_b16(do_ref[:, sl]), vn_ref[:, sl], 1, 1), 0.0) for sl in sls]
        das = [-jnp.where(strict, _dot(_b16(x), _b16(u_ref[:, sl]), 1, 1) + _dot(_b16(y), w_ref[:, sl], 1, 1), 0.0)
               for sl, x, y in zip(sls, xs, ys)]
        for hh, sl in enumerate(sls):
            q, k, v, beta, gc = q_ref[:, sl], k_ref[:, sl], v_ref[:, sl], b_ref[:, sl], g_ref[:, sl]
            last_a, last_b = g_ref[CHUNK - 1:CHUNK, sl], g_ref[PAIR - 1:PAIR, sl]
            dmat, gam, e2 = _decay_parts(gc, last_a, last_b, ii, jj, causal)
            q16, k16 = _b16(q), k16s[hh]
            kk, qk, dp, x, y, da = kks[hh], qks[hh], dps[hh], xs[hh], ys[hh], das[hh]
            dqd, dkd = dqd_ref[:, sl], dkd_ref[:, sl]
            dpd16 = _b16(dp * dmat)
            dkk16 = _b16(da * beta * dmat)
            dq_ref[:, sl] = gam * dqd + _dot(dpd16, k16, 1, 0)
            dk_ref[:, sl] = (e2 * dkd + _dot(dpd16, q16, 0, 0) + beta * gam * y
                             + _dot(dkk16, k16, 1, 0) + _dot(dkk16, k16, 0, 0))
            dv_ref[:, sl] = beta * x
            dbeta = rs(v * x) + rs(k * gam * y) + rs(da * kk * dmat)
            dbeta_ref[:, sl] = jnp.broadcast_to(dbeta, (PAIR, HD))
            m = (dp * qk + da * beta * kk) * dmat
            dgam = rs(q * dqd) + rs(k * beta * y)
            de2 = rs(k * dkd)
            colsum = _to_col(jnp.sum(m, axis=0, keepdims=True), ii, jj)
            te2 = de2 * e2
            dgc = rs(m) - colsum + gam * dgam - te2
            tail_a = jnp.sum(jnp.where(first, te2, 0.0), axis=0, keepdims=True)
            tail_b = jnp.sum(jnp.where(first, 0.0, te2), axis=0, keepdims=True)
            dgc = dgc + jnp.where(ii == CHUNK - 1, tail_a + ddec_ref[0:1, sl] * jnp.exp(last_a), 0.0)
            dgc = dgc + jnp.where(ii == PAIR - 1, tail_b + ddec_ref[8:9, sl] * jnp.exp(last_b), 0.0)
            dgc_row = _to_row(dgc, ii, jj)
            dg = jnp.sum(jnp.where(suffix, jnp.broadcast_to(dgc_row, (PAIR, PAIR)), 0.0), axis=1, keepdims=True)
            dg_ref[:, sl] = jnp.broadcast_to(dg, (PAIR, HD))

    blk = pl.BlockSpec((PAIR, HEADS_PER_STEP * HD), lambda i, h: (i, h))
    return pl.pallas_call(
        body, grid=(t // PAIR, N_HEADS // HEADS_PER_STEP),
        in_specs=[blk] * 14 + [pl.BlockSpec((16, HEADS_PER_STEP * HD), lambda i, h: (i, h))], out_specs=[blk] * 5,
        out_shape=[_sds((t, GW), F32)] * 5, name=name,
        compiler_params=_params(2))(qn, kn, vv, beta_b, gc_b, tinv, u, w, vn, do, dvn, dqd, dkd, dw, ddec)


def _rope_tables(name, pos_col, inv_row):
    t = pos_col.shape[0]
    tm = min(1024, t)

    def body(pos_ref, inv_ref, cos_ref, sin_ref):
        ang = pos_ref[...].astype(F32) * inv_ref[...]
        lane = _iota2(ang.shape, 1)
        cos_ref[...] = jnp.cos(ang)
        sin_ref[...] = jnp.where(lane < HD // 2, -1.0, 1.0) * jnp.sin(ang)

    tab = pl.BlockSpec((tm, HD), lambda i: (i, 0))
    return pl.pallas_call(
        body, grid=(t // tm,), in_specs=[pl.BlockSpec((tm, 1), lambda i: (i, 0)), pl.BlockSpec((1, HD), lambda i: (0, 0))],
        out_specs=[tab, tab], out_shape=[_sds((t, HD), F32)] * 2, name=name,
        compiler_params=_params(1))(pos_col, inv_row)


def _head_rms(xh, wv):
    return xh * lax.rsqrt(jnp.mean(xh * xh, axis=-1, keepdims=True) + EPS) * wv


def _qk_fwd(name, proj, pair_blk, wq_row, wk_row, cos_t, sin_t):
    t = proj.shape[0]
    tm = min(512, t)

    def body(x_ref, wq_ref, wk_ref, cos_ref, sin_ref, q_ref, k_ref):
        cos, sin = cos_ref[...], sin_ref[...]
        for o_ref, w_ref, base in ((q_ref, wq_ref, 0), (k_ref, wk_ref, GW)):
            for h in range(N_HEADS):
                y = _head_rms(x_ref[:, base + h * HD:base + (h + 1) * HD], w_ref[...])
                o_ref[:, h * HD:(h + 1) * HD] = y * cos + pltpu.roll(y, HD // 2, 1) * sin

    vec = pl.BlockSpec((1, HD), lambda i: (0, 0))
    tab = pl.BlockSpec((tm, HD), lambda i: (i, 0))
    wide = pl.BlockSpec((tm, GW), lambda i: (i, 0))
    return pl.pallas_call(
        body, grid=(t // tm,),
        in_specs=[pl.BlockSpec((tm, 2 * GW), lambda i: (i, pair_blk)), vec, vec, tab, tab],
        out_specs=[wide, wide], out_shape=[_sds((t, GW), F32)] * 2, name=name,
        compiler_params=_params(1))(proj, wq_row, wk_row, cos_t, sin_t)


def _qk_bwd(name, proj, pair_blk, wq_row, wk_row, cos_t, sin_t, dq_full, dk_full, dproj):
    t = proj.shape[0]
    tm = min(512, t)

    def body(x_ref, wq_ref, wk_ref, cos_ref, sin_ref, dq_ref, dk_ref, dproj_ref, dx_ref, dwq_ref, dwk_ref):
        cos, sin = cos_ref[...], sin_ref[...]

        @pl.when(pl.program_id(0) == 0)
        def _():
            dwq_ref[...] = jnp.zeros_like(dwq_ref)
            dwk_ref[...] = jnp.zeros_like(dwk_ref)

        for dy_ref, w_ref, dw_ref, base in ((dq_ref, wq_ref, dwq_ref, 0), (dk_ref, wk_ref, dwk_ref, GW)):
            dw = jnp.zeros((1, HD), F32)
            for h in range(N_HEADS):
                dy = dy_ref[:, h * HD:(h + 1) * HD]
                dy = dy * cos - pltpu.roll(dy, HD // 2, 1) * sin
                _, vjp = jax.vjp(_head_rms, x_ref[:, base + h * HD:base + (h + 1) * HD], w_ref[...])
                dx, dwh = vjp(dy)
                dw = dw + dwh
                dx_ref[:, base + h * HD:base + (h + 1) * HD] = dx.astype(BF16)
            dw_ref[...] += dw

    vec = pl.BlockSpec((1, HD), lambda i: (0, 0))
    tab = pl.BlockSpec((tm, HD), lambda i: (i, 0))
    wide = pl.BlockSpec((tm, GW), lambda i: (i, 0))
    pair = pl.BlockSpec((tm, 2 * GW), lambda i: (i, pair_blk))
    return pl.pallas_call(
        body, grid=(t // tm,), in_specs=[pair, vec, vec, tab, tab, wide, wide, ANY],
        out_specs=[pair, vec, vec],
        out_shape=[_sds(dproj.shape, BF16), _sds((1, HD), F32), _sds((1, HD), F32)], input_output_aliases={7: 0},
        name=name, compiler_params=_params(1))(proj, wq_row, wk_row, cos_t, sin_t, dq_full, dk_full, dproj)


def _cast_into(name, x, dproj, blk_idx):
    t = x.shape[0]
    tm = min(512, t)

    def body(x_ref, dproj_ref, o_ref):
        o_ref[...] = x_ref[...].astype(BF16)

    return pl.pallas_call(
        body, grid=(t // tm,), in_specs=[pl.BlockSpec((tm, GW), lambda i: (i, 0)), ANY],
        out_specs=pl.BlockSpec((tm, GW), lambda i: (i, blk_idx)), out_shape=_sds(dproj.shape, BF16),
        input_output_aliases={1: 0}, name=name, compiler_params=_params(1))(x, dproj)


GROUP = SPAN * max(DILATIONS)
SCALE = HD ** -0.5
TILE_BATCH = 8


def _band_mask(lo):
    qi = _iota2((SPAN, 2 * SPAN), 0)
    ki = _iota2((SPAN, 2 * SPAN), 1)
    return (ki >= qi) & (ki <= qi + SPAN) & (ki >= lo)


def _tiles():
    return [(pi, r, u, rho) for pi, r in enumerate(DILATIONS) for rho in range(r) for u in range(GROUP // (SPAN * r))]


def _rows(r, u, rho):
    return pl.ds(u * SPAN * r + rho, SPAN, stride=r) if r > 1 else pl.ds(u * SPAN, SPAN)


def _attn_fwd(name, q, k, v, v_blk):
    t = q.shape[0]

    def body(qc_ref, kc_ref, vc_ref, kp_ref, vp_ref, ob_ref, lse_ref, o_scr, l_scr):
        mask_in = _band_mask(0)
        mask_edge = _band_mask(jnp.where(pl.program_id(0) == 0, SPAN, 0))
        tiles = _tiles()
        k_own = v_own = None
        for b0 in range(0, len(tiles), TILE_BATCH):
            work = []
            for pi, r, u, rho in tiles[b0:b0 + TILE_BATCH]:
                rows = _rows(r, u, rho)
                if u > 0:
                    k_prev, v_prev, mask = k_own, v_own, mask_in
                else:
                    prows = _rows(r, GROUP // (SPAN * r) - 1, rho)
                    k_prev, v_prev, mask = kp_ref[prows, :].astype(BF16), vp_ref[prows, :].astype(BF16), mask_edge
                k_own, v_own = kc_ref[rows, :].astype(BF16), vc_ref[rows, :].astype(BF16)
                work.append((pi, rows, mask, qc_ref[rows, :].astype(BF16), jnp.concatenate([k_prev, k_own], axis=0),
                             jnp.concatenate([v_prev, v_own], axis=0)))
            scores = [_dot(qt, kcat, 1, 1) for _, _, _, qt, kcat, _ in work]
            soft = []
            for (_, _, mask, _, _, _), s in zip(work, scores):
                s = jnp.where(mask, s * SCALE, NEG)
                m = jnp.max(s, axis=1, keepdims=True)
                p = jnp.exp(s - m)
                soft.append((m, _b16(p), jnp.sum(p, axis=1, keepdims=True)))
            outs = [_dot(p, vcat, 1, 0) for (_, p, _), (_, _, _, _, _, vcat) in zip(soft, work)]
            for (pi, rows, _, _, _, _), (m, _, den), o in zip(work, soft, outs):
                o_scr[pi, rows, :] = o * (1.0 / den)
                l_scr[pi, rows, :] = jnp.broadcast_to(m + jnp.log(den), (SPAN, HD))
        step = 256
        for c in range(GROUP // step):
            sl = pl.ds(c * step, step)
            ob, lse = _merge([o_scr[i, sl, :] for i in range(3)], [l_scr[i, sl, :] for i in range(3)])
            ob_ref[sl, :] = ob
            lse_ref[sl, :] = lse

    cur = pl.BlockSpec((GROUP, HD), lambda g, h: (g, h))
    prev = pl.BlockSpec((GROUP, HD), lambda g, h: (jnp.maximum(g - 1, 0), h))
    vcur = pl.BlockSpec((GROUP, HD), lambda g, h: (g, v_blk * N_HEADS + h))
    vprev = pl.BlockSpec((GROUP, HD), lambda g, h: (jnp.maximum(g - 1, 0), v_blk * N_HEADS + h))
    return pl.pallas_call(
        body, grid=(t // GROUP, N_HEADS), in_specs=[cur, cur, vcur, prev, vprev], out_specs=[cur, cur],
        out_shape=[_sds((t, GW), F32), _sds((t, GW), F32)],
        scratch_shapes=[pltpu.VMEM((3, GROUP, HD), F32), pltpu.VMEM((3, GROUP, HD), F32)], name=name,
        compiler_params=_params(2))(q, k, v, k, v)


def _attn_bwd(name, q, k, v, v_blk, do, lse, delta):
    t = q.shape[0]
    ng = t // GROUP

    def probs(work):
        scores = [_dot(qt, kcat, 1, 1) for qt, _, _, _, kcat, _, _ in work]
        dps = [_dot(dot, vcat, 1, 1) for _, dot, _, _, _, vcat, _ in work]
        out = []
        for (_, _, lt, dlt, kcat, _, mask), s, dp in zip(work, scores, dps):
            wide = kcat.shape[0] // SPAN
            lw = jnp.concatenate([lt] * wide, axis=1) if wide > 1 else lt
            dw = jnp.concatenate([dlt] * wide, axis=1) if wide > 1 else dlt
            p = jnp.exp(jnp.where(mask, s * SCALE - lw, NEG))
            out.append((_b16(p * (dp - dw) * SCALE), _b16(p)))
        return out

    def body(qc_ref, kc_ref, vc_ref, doc_ref, lc_ref, dc_ref, kp_ref, vp_ref, qn_ref, don_ref, ln_ref, dn_ref,
             dq_ref, dk_ref, dv_ref):
        g = pl.program_id(0)
        mask_in = _band_mask(0)
        mask_edge = _band_mask(jnp.where(g == 0, SPAN, 0))
        dk_ref[...] = jnp.zeros_like(dk_ref)
        dv_ref[...] = jnp.zeros_like(dv_ref)
        tiles = _tiles()
        k_own = v_own = None
        for b0 in range(0, len(tiles), TILE_BATCH):
            where, work = [], []
            for pi, r, u, rho in tiles[b0:b0 + TILE_BATCH]:
                rows = _rows(r, u, rho)
                if u > 0:
                    prows, k_prev, v_prev, mask = _rows(r, u - 1, rho), k_own, v_own, mask_in
                else:
                    prows = _rows(r, GROUP // (SPAN * r) - 1, rho)
                    k_prev, v_prev, mask = kp_ref[prows, :].astype(BF16), vp_ref[prows, :].astype(BF16), mask_edge
                k_own, v_own = kc_ref[rows, :].astype(BF16), vc_ref[rows, :].astype(BF16)
                where.append((pi, u, rows, prows))
                work.append((qc_ref[rows, :].astype(BF16), doc_ref[rows, :].astype(BF16), lc_ref[rows, :], dc_ref[rows, :],
                             jnp.concatenate([k_prev, k_own], axis=0), jnp.concatenate([v_prev, v_own], axis=0), mask))
            dsp = probs(work)
            dqs = [_dot(ds, w[4], 1, 0) for (ds, _), w in zip(dsp, work)]
            dks = [_dot(ds, w[0], 0, 0) for (ds, _), w in zip(dsp, work)]
            dvs = [_dot(p, w[1], 0, 0) for (_, p), w in zip(dsp, work)]
            for (pi, u, rows, prows), dq_t, dk2, dv2 in zip(where, dqs, dks, dvs):
                if pi == 0:
                    dq_ref[rows, :] = dq_t
                else:
                    dq_ref[rows, :] += dq_t
                dk_ref[rows, :] += dk2[SPAN:, :]
                dv_ref[rows, :] += dv2[SPAN:, :]
                if u > 0:
                    dk_ref[prows, :] += dk2[:SPAN, :]
                    dv_ref[prows, :] += dv2[:SPAN, :]
        qi = _iota2((SPAN, SPAN), 0)
        ki = _iota2((SPAN, SPAN), 1)
        mask_next = (ki >= qi) & (ki < jnp.where(g == ng - 1, 0, SPAN))
        edge = [(r, rho) for r in DILATIONS for rho in range(r)]
        for b0 in range(0, len(edge), TILE_BATCH):
            where, work = [], []
            for r, rho in edge[b0:b0 + TILE_BATCH]:
                krows, qrows = _rows(r, GROUP // (SPAN * r) - 1, rho), _rows(r, 0, rho)
                where.append(krows)
                work.append((qn_ref[qrows, :].astype(BF16), don_ref[qrows, :].astype(BF16), ln_ref[qrows, :],
                             dn_ref[qrows, :], kc_ref[krows, :].astype(BF16), vc_ref[krows, :].astype(BF16), mask_next))
            dsp = probs(work)
            dks = [_dot(ds, w[0], 0, 0) for (ds, _), w in zip(dsp, work)]
            dvs = [_dot(p, w[1], 0, 0) for (_, p), w in zip(dsp, work)]
            for krows, dk1, dv1 in zip(where, dks, dvs):
                dk_ref[krows, :] += dk1
                dv_ref[krows, :] += dv1

    cur = pl.BlockSpec((GROUP, HD), lambda g, h: (g, h))
    prev = pl.BlockSpec((GROUP, HD), lambda g, h: (jnp.maximum(g - 1, 0), h))
    nxt = pl.BlockSpec((GROUP, HD), lambda g, h: (jnp.minimum(g + 1, ng - 1), h))
    vcur = pl.BlockSpec((GROUP, HD), lambda g, h: (g, v_blk * N_HEADS + h))
    vprev = pl.BlockSpec((GROUP, HD), lambda g, h: (jnp.maximum(g - 1, 0), v_blk * N_HEADS + h))
    return pl.pallas_call(
        body, grid=(ng, N_HEADS), in_specs=[cur, cur, vcur, cur, cur, cur, prev, vprev] + [nxt] * 4,
        out_specs=[cur] * 3,
        out_shape=[_sds((t, GW), F32)] * 3, name=name,
        compiler_params=_params(2))(q, k, v, do, lse, delta, k, v, q, do, lse, delta)


def _merge(os_, ls_):
    m = jnp.maximum(jnp.maximum(ls_[0], ls_[1]), ls_[2])
    ws = [jnp.exp(l - m) for l in ls_]
    tot = ws[0] + ws[1] + ws[2]
    ob = (ws[0] * os_[0] + ws[1] * os_[1] + ws[2] * os_[2]) / tot
    return ob, m + jnp.log(tot)


def _gated_norm(oa, z, wv):
    return _head_rms(oa, wv) * _silu(z)


def _mix_fwd(name, oa_raw, proj, z_blk, ob, w_dn, w_an):
    t = oa_raw.shape[0]
    tm = min(512, t)

    def body(oa_ref, z_ref, ob_ref, wd_ref, wa_ref, mix_ref):
        for h in range(N_HEADS):
            sl = slice(h * HD, (h + 1) * HD)
            mix_ref[:, sl] = _gated_norm(oa_ref[:, sl], z_ref[:, sl], wd_ref[...]).astype(BF16)
            mix_ref[:, GW + h * HD:GW + (h + 1) * HD] = _head_rms(ob_ref[:, sl], wa_ref[...]).astype(BF16)

    vec = pl.BlockSpec((1, HD), lambda i: (0, 0))
    wide = pl.BlockSpec((tm, GW), lambda i: (i, 0))
    return pl.pallas_call(
        body, grid=(t // tm,),
        in_specs=[wide, pl.BlockSpec((tm, GW), lambda i: (i, z_blk)), wide, vec, vec],
        out_specs=pl.BlockSpec((tm, 2 * GW), lambda i: (i, 0)),
        out_shape=_sds((t, 2 * GW), BF16), name=name,
        compiler_params=_params(1))(oa_raw, proj, ob, w_dn, w_an)


def _mix_bwd(name, dx1_16, w_out, oa_raw, proj, z_blk, ob, w_dn, w_an, dep):
    t, d = dx1_16.shape
    tm = min(512, t)

    def body(dx_ref, wo_ref, oa_ref, z_ref, ob_ref, wd_ref, wa_ref, dep_ref,
             doa_ref, dz_ref, dob_ref, dl_ref, dwd_ref, dwa_ref):
        dwd = jnp.zeros((1, HD), F32)
        dwa = jnp.zeros((1, HD), F32)
        dxv = dx_ref[...]
        pairs = [_dot(dxv, wo_ref[2 * p * HD:2 * (p + 1) * HD, :], 1, 1) for p in range(N_HEADS)]
        heads = [half for pr in pairs for half in (pr[:, :HD], pr[:, HD:])]
        dm_a, dm_b = heads[:N_HEADS], heads[N_HEADS:]
        for h in range(N_HEADS):
            sl = slice(h * HD, (h + 1) * HD)
            _, vjp = jax.vjp(_gated_norm, oa_ref[:, sl], z_ref[:, sl], wd_ref[...])
            doa, dz, dw1 = vjp(dm_a[h])
            doa_ref[:, sl] = doa
            dz_ref[:, sl] = dz.astype(BF16)
            dwd = dwd + dw1
            obh = ob_ref[:, sl]
            _, vjp2 = jax.vjp(_head_rms, obh, wa_ref[...])
            dob, dw2 = vjp2(dm_b[h])
            dwa = dwa + dw2
            dob_ref[:, sl] = dob
            dl_ref[:, sl] = jnp.broadcast_to(jnp.sum(dob * obh, axis=1, keepdims=True), (tm, HD))

        @pl.when(pl.program_id(0) == 0)
        def _():
            dwd_ref[...] = jnp.zeros_like(dwd_ref)
            dwa_ref[...] = jnp.zeros_like(dwa_ref)

        dwd_ref[...] += dwd
        dwa_ref[...] += dwa

    vec = pl.BlockSpec((1, HD), lambda i: (0, 0))
    wide = pl.BlockSpec((tm, GW), lambda i: (i, 0))
    return pl.pallas_call(
        body, grid=(t // tm,),
        in_specs=[pl.BlockSpec((tm, d), lambda i: (i, 0)), pl.BlockSpec((2 * GW, d), lambda i: (0, 0)), wide,
                  pl.BlockSpec((tm, GW), lambda i: (i, z_blk)), wide, vec, vec, ANY],
        out_specs=[wide, pl.BlockSpec((tm, GW), lambda i: (i, z_blk)), wide, wide, vec, vec],
        out_shape=[_sds((t, GW), F32), _sds(proj.shape, BF16), _sds((t, GW), F32), _sds((t, GW), F32),
                   _sds((1, HD), F32), _sds((1, HD), F32)], name=name,
        compiler_params=_params(1))(dx1_16, w_out, oa_raw, proj, ob, w_dn, w_an, dep)


def _halves(n):
    cut = (n // 256) * 128
    return [(0, cut), (cut, n)]


def _gate_up_swiglu(name, h2, w_gu_g):
    t, d = h2.shape
    n = w_gu_g.shape[2]
    per = N_DEV // 2
    tm = min(512, t)

    def body(a_ref, bg_ref, bu_ref, gu_ref, act_ref):
        a = a_ref[...]
        cuts = _halves(n)
        gs = [_dot(a, bg_ref[:, c0:c1], 1, 0) for c0, c1 in cuts]
        ups = [_dot(a, bu_ref[:, c0:c1], 1, 0) for c0, c1 in cuts]
        for (c0, c1), g, up in zip(cuts, gs, ups):
            gu_ref[0, :, c0:c1] = g.astype(BF16)
            gu_ref[1, :, c0:c1] = up.astype(BF16)
            act_ref[:, c0:c1] = (_silu(g) * up).astype(BF16)

    return pl.pallas_call(
        body, grid=(per, t // tm),
        in_specs=[pl.BlockSpec((tm, d), lambda j, i: (i, 0)), pl.BlockSpec((None, d, n), lambda j, i: (j, 0, 0)),
                  pl.BlockSpec((None, d, n), lambda j, i: (j + per, 0, 0))],
        out_specs=[pl.BlockSpec((2, tm, n), lambda j, i: (0, i, j)), pl.BlockSpec((tm, n), lambda j, i: (i, j))],
        out_shape=[_sds((2, t, per * n), BF16), _sds((t, per * n), BF16)], name=name,
        compiler_params=_params(2))(h2, w_gu_g, w_gu_g)


def _d_gate_up(name, dy16, w_down, gu3, dep):
    t, d = dy16.shape
    f = w_down.shape[0]
    tm, tn = min(1024, t), f // 4

    def body(a_ref, b_ref, g_ref, dep_ref, o_ref):
        a = a_ref[...]
        cuts = _halves(tn)
        dacts = [_dot(a, b_ref[c0:c1, :], 1, 1) for c0, c1 in cuts]
        for (c0, c1), dact in zip(cuts, dacts):
            g, up = g_ref[0, :, c0:c1].astype(F32), g_ref[1, :, c0:c1].astype(F32)
            sg = _sigmoid(g)
            o_ref[0, :, c0:c1] = (dact * up * sg * (1.0 + g * (1.0 - sg))).astype(BF16)
            o_ref[1, :, c0:c1] = (dact * g * sg).astype(BF16)

    return pl.pallas_call(
        body, grid=(f // tn, t // tm),
        in_specs=[pl.BlockSpec((tm, d), lambda j, i: (i, 0)), pl.BlockSpec((tn, d), lambda j, i: (j, 0)),
                  pl.BlockSpec((2, tm, tn), lambda j, i: (0, i, j)), ANY],
        out_specs=pl.BlockSpec((2, tm, tn), lambda j, i: (0, i, j)), out_shape=_sds((2, t, f), BF16), name=name,
        compiler_params=_params(2))(dy16, w_down, gu3, dep)


def _d_h2(name, dgu3, w_gu_g, dep):
    _, t, f = dgu3.shape
    n_dev, d, n = w_gu_g.shape
    per = n_dev // 2
    tm, tn = min(512, t), 512

    def body(g_ref, u_ref, b_ref, dep_ref, o_ref):
        acc = None
        for s in range(n_dev):
            a_ref = g_ref if s < per else u_ref
            part = _dot(a_ref[:, (s % per) * n:(s % per + 1) * n], b_ref[s], 1, 1)
            acc = part if acc is None else acc + part
        o_ref[...] = acc

    return pl.pallas_call(
        body, grid=(d // tn, t // tm),
        in_specs=[pl.BlockSpec((None, tm, f), lambda j, i: (0, i, 0)), pl.BlockSpec((None, tm, f), lambda j, i: (1, i, 0)),
                  pl.BlockSpec((n_dev, tn, n), lambda j, i: (0, j, 0)), ANY],
        out_specs=pl.BlockSpec((tm, tn), lambda j, i: (i, j)), out_shape=_sds((t, d), F32), name=name,
        compiler_params=_params(2))(dgu3, dgu3, w_gu_g, dep)


def _out_proj_norm(name, mixed, w_out, x, w_norm):
    t, d = x.shape
    kdim = mixed.shape[1]
    tm = min(512, t)

    def body(a_ref, b_ref, x_ref, w_ref, x1_ref, h_ref):
        x1 = x_ref[...] + _dot(a_ref[...], b_ref[...], 1, 0)
        x1_ref[...] = x1
        h_ref[...] = _rms_f(x1, w_ref[...]).astype(BF16)

    row = pl.BlockSpec((tm, d), lambda i: (i, 0))
    return pl.pallas_call(
        body, grid=(t // tm,),
        in_specs=[pl.BlockSpec((tm, kdim), lambda i: (i, 0)), pl.BlockSpec((kdim, d), lambda i: (0, 0)), row,
                  pl.BlockSpec((1, d), lambda i: (0, 0))],
        out_specs=[row, row], out_shape=[_sds((t, d), F32), _sds((t, d), BF16)], name=name,
        compiler_params=_params(1))(mixed, w_out, x, w_norm)


def _down_loss(name, act, w_down, x1, target):
    t, f = act.shape
    d = x1.shape[1]
    tm, tn = min(1024, t), 512

    def body(a_ref, b_ref, x_ref, t_ref, dy_ref, dy16_ref, l_ref):
        diff = _dot(a_ref[...], b_ref[...], 1, 0) + x_ref[...] - t_ref[...]
        dyv = diff * (1.0 / d)
        dy_ref[...] = dyv
        dy16_ref[...] = dyv.astype(BF16)
        tot = jnp.sum(jnp.sum(diff * diff, axis=1, keepdims=True), axis=0, keepdims=True) * (0.5 / d)

        @pl.when((pl.program_id(0) == 0) & (pl.program_id(1) == 0))
        def _():
            l_ref[...] = jnp.zeros_like(l_ref)

        l_ref[...] += jnp.broadcast_to(tot, (8, 128))

    tile = pl.BlockSpec((tm, tn), lambda i, j: (i, j))
    return pl.pallas_call(
        body, grid=(t // tm, d // tn),
        in_specs=[pl.BlockSpec((tm, f), lambda i, j: (i, 0)), pl.BlockSpec((f, tn), lambda i, j: (0, j)), tile, tile],
        out_specs=[tile, tile, pl.BlockSpec((8, 128), lambda i, j: (0, 0))],
        out_shape=[_sds((t, d), F32), _sds((t, d), BF16), _sds((8, 128), F32)], name=name,
        compiler_params=_params(2))(act, w_down, x1, target)


def _peer(me, k):
    pid = (me + k) % N_DEV
    return (pid // 4, (pid // 2) % 2, pid % 2)


def _my_id():
    return 4 * lax.axis_index("x") + 2 * lax.axis_index("y") + lax.axis_index("c")


def _exchange(name, arrays, scatter, dep):
    n = len(arrays)

    def body(*refs):
        ins, outs = refs[:n], refs[n + 1:2 * n + 1]
        send_sems, recv_sems, local_sems = refs[2 * n + 1:]
        me = _my_id()
        started = []
        for a in range(n):
            src = ins[a].at[me] if scatter[a] else ins[a]
            loc = pltpu.make_async_copy(src, outs[a].at[me], local_sems.at[a])
            loc.start()
            started.append(loc)
        remote = []
        for k in range(1, N_DEV):
            to = (me + k) % N_DEV
            for a in range(n):
                src = ins[a].at[to] if scatter[a] else ins[a]
                cp = pltpu.make_async_remote_copy(src_ref=src, dst_ref=outs[a].at[me],
                                                  send_sem=send_sems.at[a * (N_DEV - 1) + k - 1], recv_sem=recv_sems.at[a * (N_DEV - 1) + k - 1],
                                                  device_id=_peer(me, k), device_id_type=pl.DeviceIdType.MESH)
                cp.start()
                remote.append(cp)
        for k in range(1, N_DEV):
            frm = (me + N_DEV - k) % N_DEV
            for a in range(n):
                src = ins[a].at[frm] if scatter[a] else ins[a]
                pltpu.make_async_remote_copy(src_ref=src, dst_ref=outs[a].at[frm],
                                             send_sem=send_sems.at[a * (N_DEV - 1) + k - 1], recv_sem=recv_sems.at[a * (N_DEV - 1) + k - 1],
                                             device_id=_peer(me, k), device_id_type=pl.DeviceIdType.MESH).wait_recv()
        for cp in remote:
            cp.wait_send()
        for loc in started:
            loc.wait()

    out_shape = [_sds((N_DEV,) + (a.shape[1:] if sc else a.shape), a.dtype) for a, sc in zip(arrays, scatter)]
    return pl.pallas_call(
        body, in_specs=[ANY] * (n + 1), out_specs=[ANY] * n, out_shape=out_shape,
        scratch_shapes=[pltpu.SemaphoreType.DMA((n * (N_DEV - 1),)), pltpu.SemaphoreType.DMA((n * (N_DEV - 1),)),
                        pltpu.SemaphoreType.DMA((n,))],
        name=name)(*arrays, dep)


def _gather_two_level(name, arrays):
    n = len(arrays)
    per = N_DEV - 1
    units = []
    for a, arr in enumerate(arrays):
        cuts = 4 if arr.shape[0] % 64 == 0 and arr.shape[0] >= 1024 else 1
        units += [(a, p * (arr.shape[0] // cuts), arr.shape[0] // cuts) for p in range(cuts)]
    nu = len(units)

    def body(*refs):
        ins, outs = refs[:n], refs[n:2 * n]
        send_sems, recv_sems, local_sems = refs[2 * n:]
        x, y, c = lax.axis_index("x"), lax.axis_index("y"), lax.axis_index("c")
        me, sibling = (x, y, c), (x, y, 1 - c)
        flip = lambda v, on: v + on - 2 * v * on
        relayed = (flip(x, c), flip(y, 1 - c), c)
        other = (flip(x, 1 - c), flip(y, c), c)
        diagonal = (1 - x, 1 - y, c)
        k_relayed, k_other = 2 - c, 1 + c

        def copy(u, k, block, to, from_input=False):
            a, r0, nr = units[u]
            slot = outs[a].at[4 * block[0] + 2 * block[1] + block[2], pl.ds(r0, nr)]
            return pltpu.make_async_remote_copy(
                src_ref=ins[a].at[pl.ds(r0, nr)] if from_input else slot, dst_ref=slot,
                send_sem=send_sems.at[u * per + k], recv_sem=recv_sems.at[u * per + k], device_id=to,
                device_id_type=pl.DeviceIdType.MESH)

        mine = [pltpu.make_async_copy(ins[a], outs[a].at[4 * x + 2 * y + c], local_sems.at[a]) for a in range(n)]
        for cp in mine:
            cp.start()
        sent = [copy(u, 1, me, (1 - x, y, c), True) for u in range(nu)]
        sent += [copy(u, 2, me, (x, 1 - y, c), True) for u in range(nu)]
        sent += [copy(u, 0, me, sibling, True) for u in range(nu)]
        for cp in sent:
            cp.start()
        for u in range(nu):
            copy(u, k_relayed, relayed, me).wait_recv()
            sent.append(copy(u, 3, relayed, other))
            sent.append(copy(u, 3 + k_relayed, relayed, sibling))
            sent[-2].start()
            sent[-1].start()
        for u in range(nu):
            copy(u, k_other, other, me).wait_recv()
            sent.append(copy(u, 3 + k_other, other, sibling))
            sent[-1].start()
        for u in range(nu):
            copy(u, 3, diagonal, me).wait_recv()
            sent.append(copy(u, 6, diagonal, sibling))
            sent[-1].start()
        for u in range(nu):
            copy(u, 0, sibling, me).wait_recv()
            for j, chip in enumerate([(1 - x, y), (x, 1 - y), (1 - x, 1 - y)]):
                copy(u, 4 + j, (*chip, 1 - c), me).wait_recv()
        for cp in sent:
            cp.wait_send()
        for cp in mine:
            cp.wait()

    return pl.pallas_call(
        body, in_specs=[ANY] * n, out_specs=[ANY] * n,
        out_shape=[_sds((N_DEV,) + a.shape, a.dtype) for a in arrays],
        scratch_shapes=[pltpu.SemaphoreType.DMA((nu * per,)), pltpu.SemaphoreType.DMA((nu * per,)),
                        pltpu.SemaphoreType.DMA((n,))],
        name=name)(*arrays)


HBM = pl.BlockSpec(memory_space=pltpu.HBM)
SEM = pl.BlockSpec(memory_space=pltpu.SEMAPHORE)
EFFECT = pltpu.SideEffectType.DATAFLOW_SIDE_EFFECTING


def _remote_copies(srcs, lands, scatter, send_sems, recv_sems, me, incoming):
    out = []
    for k in range(1, N_DEV):
        other = (me + N_DEV - k) % N_DEV if incoming else (me + k) % N_DEV
        for a in range(len(srcs)):
            sem = a * (N_DEV - 1) + k - 1
            src = srcs[a].at[other] if scatter[a] else srcs[a]
            dst = lands[a].at[other if incoming else me]
            out.append(pltpu.make_async_remote_copy(src_ref=src, dst_ref=dst, send_sem=send_sems.at[sem],
                                                    recv_sem=recv_sems.at[sem], device_id=_peer(me, k),
                                                    device_id_type=pl.DeviceIdType.MESH))
    return out


def _exchange_start(name, arrays, scatter, dep):
    n = len(arrays)
    lands = [lax.empty((N_DEV,) + (a.shape[1:] if sc else a.shape), a.dtype) for a, sc in zip(arrays, scatter)]

    def body(*refs):
        srcs, land_refs = refs[:n], refs[n:2 * n]
        send_sems, recv_sems = refs[2 * n + 1], refs[2 * n + 2]
        token = refs[-1]
        for cp in _remote_copies(srcs, land_refs, scatter, send_sems, recv_sems, _my_id(), False):
            cp.start()
        token[...] = jnp.zeros_like(token)

    n_sem = n * (N_DEV - 1)
    out_shape = ([pltpu.SemaphoreType.DMA((n_sem,)), pltpu.SemaphoreType.DMA((n_sem,))]
                 + [pltpu.HBM(a.shape, a.dtype) for a in arrays] + [pltpu.HBM(l.shape, l.dtype) for l in lands]
                 + [_sds((8, 128), F32)])
    aliases = {i: 2 + i for i in range(2 * n)}
    args = [pltpu.with_memory_space_constraint(a, pltpu.HBM) for a in list(arrays) + lands] + [dep]
    res = pl.pallas_call(
        body, name=name, in_specs=[HBM] * (2 * n) + [ANY], out_shape=out_shape,
        out_specs=[SEM, SEM] + [HBM] * (2 * n) + [pl.BlockSpec(memory_space=pltpu.VMEM)],
        input_output_aliases=aliases, compiler_params=pltpu.CompilerParams(has_side_effects=EFFECT))(*args)
    return dict(send=res[0], recv=res[1], srcs=res[2:2 + n], lands=res[2 + n:2 + 2 * n], token=res[-1],
                scatter=scatter)


def _exchange_wait(name, started, after):
    n = len(started["srcs"])
    scatter = started["scatter"]

    def body(*refs):
        srcs, land_refs = refs[:n], refs[n:2 * n]
        send_sems, recv_sems = refs[2 * n], refs[2 * n + 1]
        me = _my_id()
        for cp in _remote_copies(srcs, land_refs, scatter, send_sems, recv_sems, me, False):
            cp.wait_send()
        for cp in _remote_copies(srcs, land_refs, scatter, send_sems, recv_sems, me, True):
            cp.wait_recv()

    arrs = list(started["srcs"]) + list(started["lands"])
    res = pl.pallas_call(
        body, name=name, in_specs=[HBM] * (2 * n) + [SEM, SEM, ANY],
        out_shape=[pltpu.HBM(a.shape, a.dtype) for a in arrs], out_specs=[HBM] * (2 * n),
        input_output_aliases={i: i for i in range(2 * n)},
        compiler_params=pltpu.CompilerParams(has_side_effects=EFFECT))(*arrs, started["send"], started["recv"], after)
    me = _my_id()
    out = []
    for src, land, sc in zip(res[:n], res[n:], scatter):
        own = lax.dynamic_index_in_dim(src, me, 0, keepdims=True) if sc else src[None]
        out.append(lax.dynamic_update_slice(land, own, (me,) + (0,) * (land.ndim - 1)))
    return out


def _adamw(name, parts, w, m, v):
    r, c = w.shape
    tr, tc = r, c
    if r % 8 == 0:
        tr = next(cand for cand in (128, 88, 64, 40, 8) if r % cand == 0)
    else:
        tc = 256
    c1 = 1.0 / (1.0 - ADAM_B1 ** ADAM_STEP)
    c2 = 1.0 / (1.0 - ADAM_B2 ** ADAM_STEP)

    def body(p_ref, w_ref, m_ref, v_ref, g_ref, d_ref, nm_ref, nv_ref):
        g = p_ref[0].astype(F32)
        for s in range(1, N_DEV):
            g = g + p_ref[s].astype(F32)
        mn = ADAM_B1 * m_ref[...] + (1.0 - ADAM_B1) * g
        vn = ADAM_B2 * v_ref[...] + (1.0 - ADAM_B2) * (g * g)
        g_ref[...] = g
        nm_ref[...] = mn
        nv_ref[...] = vn
        d_ref[...] = -ADAM_LR * ((mn * c1) / (jnp.sqrt(vn * c2) + ADAM_EPS) + ADAM_WD * w_ref[...])

    blk = pl.BlockSpec((tr, tc), lambda i, j: (i, j))
    return pl.pallas_call(
        body, grid=(r // tr, c // tc),
        in_specs=[pl.BlockSpec((N_DEV, tr, tc), lambda i, j: (0, i, j)), blk, blk, blk],
        out_specs=[blk] * 4, out_shape=[_sds((r, c), F32)] * 4, name=name,
        compiler_params=_params(2, VMEM_LIMIT))(parts, w, m, v)


def _pad_rows(a, rows):
    return jnp.pad(a, ((0, rows - a.shape[0]), (0, 0)))


def _lane_row(vec8, offset):
    return jnp.pad(vec8.reshape(1, 8), ((0, 0), (offset, HD - 8 - offset)))


def kernel(x, positions, attn_norm_w, w_in, conv_w, a_log, dt_bias, delta_out_norm_w, q_norm_w, k_norm_w, attn_out_norm_w, w_out, ffn_norm_w, w_gate_up, w_down, loss_target, m_attn_norm_w, m_w_in, m_conv_w, m_a_log, m_dt_bias, m_delta_out_norm_w, m_q_norm_w, m_k_norm_w, m_attn_out_norm_w, m_w_out, m_ffn_norm_w, m_w_gate_up, m_w_down, v_attn_norm_w, v_w_in, v_conv_w, v_a_log, v_dt_bias, v_delta_out_norm_w, v_q_norm_w, v_k_norm_w, v_attn_out_norm_w, v_w_out, v_ffn_norm_w, v_w_gate_up, v_w_down):
    x2 = x[0]
    t, d = x2.shape
    target = loss_target[0]
    pos_col = positions.reshape(t, 1)
    half = HD // 2
    inv = (ROPE_THETA ** (-np.arange(half, dtype=np.float32) / half)).astype(np.float32)
    inv_row = jnp.asarray(np.concatenate([inv, inv]).reshape(1, HD))

    n_in = w_in.shape[2]
    n_gu = w_gate_up.shape[2]
    w_in_g, conv_g = _gather_two_level("gather_in", [w_in[0].astype(BF16), _pad_rows(conv_w[0], 8)])
    out_fly = _exchange_start("gather_out_start", [w_out[0].astype(BF16)], [False], conv_g)
    gu_fly = _exchange_start("gather_gate_up_start", [w_gate_up[0].astype(BF16)], [False], out_fly["token"])
    down_fly = _exchange_start("gather_down_start", [w_down[0].astype(BF16)], [False], gu_fly["token"])
    n_main = 4 * GW
    n_small = 2 * N_HEADS
    segments = [(0, n_main, 0), (n_main + n_small, N_DEV * n_in, n_main), (n_main, n_main + n_small, 7 * GW)]
    pieces = []
    for lo, hi, _ in segments:
        f = lo
        while f < hi:
            j = f // n_in
            end = min(hi, (j + 1) * n_in)
            pieces.append(w_in_g[j][:, f - j * n_in:end - j * n_in])
            f = end
    w_cat = jnp.concatenate(pieces + [jnp.zeros((d, HD - n_small), BF16)], axis=1)
    n_cat = w_cat.shape[1]
    small_blk = (7 * GW) // HD
    conv_w8 =jnp.transpose(conv_g, (1, 0, 2)).reshape(8, 3 * GW)
    alog_row = _lane_row(a_log[0], 8)
    dtb_row = _lane_row(dt_bias[0], 8)

    tm = min(2048, t)
    h1 = _rms_fwd("norm1", x2, attn_norm_w, down_fly["token"])
    tmp, tnp = min(1024, t), n_cat // 3
    proj = _mm("in_proj", h1, w_cat, grid=(t // tmp, n_cat // tnp, 1),
               a_spec=pl.BlockSpec((tmp, d), lambda i, j, k: (i, 0)),
               b_spec=pl.BlockSpec((d, tnp), lambda i, j, k: (0, j)),
               o_spec=pl.BlockSpec((tmp, tnp), lambda i, j, k: (i, j)),
               out_shape=_sds((t, n_cat), F32), ca=1, cb=0, nk=1)
    qn = _conv_fwd("conv_q", proj, conv_w8, 0, True, HD ** -0.5)
    kn = _conv_fwd("conv_k", proj, conv_w8, 1, True, 1.0)
    vv = _conv_fwd("conv_v", proj, conv_w8, 2, False, 1.0)
    beta_b, gc_b = _gates_fwd("gates", proj, small_blk, alog_row, dtb_row)
    u, w, p, tinv, qd, kd = _delta_prep("delta_prep", qn, kn, vv, beta_b, gc_b)
    oa_raw, vn, s_hist = _delta_scan("delta_scan", u, w, p, qd, kd, gc_b)

    cos_t, sin_t = _rope_tables("rope_tables", pos_col, inv_row)
    aq, ak = _qk_fwd("attn_qk", proj, 2, q_norm_w, k_norm_w, cos_t, sin_t)
    ob, lse = _attn_fwd("attn_fwd", aq, ak, proj, 6)
    mixed = _mix_fwd("mix", oa_raw, proj, 3, ob, delta_out_norm_w, attn_out_norm_w)
    (w_out_g,) = _exchange_wait("gather_out_wait", out_fly, mixed)
    w_out_full = w_out_g.reshape(2 * GW, d)
    tn = 512
    x1, h2 = _out_proj_norm("out_proj", mixed, w_out_full, x2, ffn_norm_w)
    per = N_DEV // 2
    (w_gu_g,) = _exchange_wait("gather_gate_up_wait", gu_fly, h2)
    gu3, act = _gate_up_swiglu("gate_up", h2, w_gu_g)
    (w_down_g,) = _exchange_wait("gather_down_wait", down_fly, act)
    w_down_full = w_down_g.reshape(D_FF, d)
    tmd = min(1024, t)
    dy, dy16, loss_tile = _down_loss("down_proj", act, w_down_full, x1, target)

    tk, nkt = t, 1
    g_down = _mm("g_down", act, dy16, grid=(D_FF // 1408, d // 512, nkt),
                 a_spec=pl.BlockSpec((tk, 1408), lambda i, j, k: (k, i)),
                 b_spec=pl.BlockSpec((tk, 512), lambda i, j, k: (k, j)),
                 o_spec=pl.BlockSpec((1408, 512), lambda i, j, k: (i, j)),
                 out_shape=_sds((D_FF, d), F32), ca=0, cb=0, nk=nkt)
    down_g_fly = _exchange_start("reduce_down_start", [g_down.reshape(N_DEV, D_FF // N_DEV, d)], [True], dy16)
    dgu3 = _d_gate_up("d_gate_up", dy16, w_down_full, gu3, down_g_fly["token"])
    g_gu = _mm("g_gate_up", h2, dgu3, grid=(d // 512, N_DEV, nkt),
               a_spec=pl.BlockSpec((tk, 512), lambda i, j, k: (k, i)),
               b_spec=pl.BlockSpec((None, tk, n_gu), lambda i, j, k: (j // per, k, j % per)),
               o_spec=pl.BlockSpec((None, 512, n_gu), lambda i, j, k: (j, i, 0)),
               out_shape=_sds((N_DEV, d, n_gu), F32), ca=0, cb=0, nk=nkt)
    gu_g_fly = _exchange_start("reduce_gate_up_start", [g_gu], [True], dy16)
    dh2 = _d_h2("d_h2", dgu3, w_gu_g, gu_g_fly["token"])
    dx1, dx1_16, g_ffn_norm = _rms_bwd("norm2_bwd", x1, ffn_norm_w, dh2, dy)

    g_out = _mm("g_out", mixed, dx1_16, grid=((2 * GW) // 512, 1, nkt),
                a_spec=pl.BlockSpec((tk, 512), lambda i, j, k: (k, i)),
                b_spec=pl.BlockSpec((tk, d), lambda i, j, k: (k, 0)),
                o_spec=pl.BlockSpec((512, d), lambda i, j, k: (i, 0)),
                out_shape=_sds((2 * GW, d), F32), ca=0, cb=0, nk=nkt)
    out_g_fly = _exchange_start("reduce_out_start", [g_out.reshape(N_DEV, (2 * GW) // N_DEV, d)], [True], g_ffn_norm)
    doa, dproj, dob, delta, g_dn, g_an = _mix_bwd("mix_bwd", dx1_16, w_out_full, oa_raw, proj, 3, ob,
                                                  delta_out_norm_w, attn_out_norm_w, out_g_fly["token"])
    d_aq, d_ak, d_av = _attn_bwd("attn_bwd", aq, ak, proj, 6, dob, lse, delta)
    dproj, g_qn, g_kn = _qk_bwd("attn_qk_bwd", proj, 2, q_norm_w, k_norm_w, cos_t, sin_t, d_aq, d_ak, dproj)
    dproj = _cast_into("attn_v_bwd", d_av, dproj, 6)

    dvn, dqd, dkd, dw, ddec = _delta_scan_bwd("delta_scan_bwd", doa, w, p, qd, kd, gc_b, vn, s_hist)
    dqn, dkn, dvv, dbeta_b, dg_b = _delta_prep_bwd("delta_prep_bwd", qn, kn, vv, beta_b, gc_b, tinv, u, w, vn,
                                                   doa, dvn, dqd, dkd, dw, ddec)
    dproj, gcw_q = _conv_bwd("conv_q_bwd", proj, conv_w8, dqn, dproj, 0, True, HD ** -0.5)
    dproj, gcw_k = _conv_bwd("conv_k_bwd", proj, conv_w8, dkn, dproj, 1, True, 1.0)
    dproj, gcw_v = _conv_bwd("conv_v_bwd", proj, conv_w8, dvv, dproj, 2, False, 1.0)
    dproj, g_alog_row, g_dtb_row = _gates_bwd("gates_bwd", proj, small_blk, alog_row, dtb_row, dbeta_b, dg_b, dproj)
    tmc = 384
    g_cat = _mm("g_in", dproj, h1, grid=(n_cat // tmc, 1, nkt),
                a_spec=pl.BlockSpec((tk, tmc), lambda i, j, k: (k, i)),
                b_spec=pl.BlockSpec((tk, d), lambda i, j, k: (k, 0)),
                o_spec=pl.BlockSpec((tmc, d), lambda i, j, k: (i, 0)),
                out_shape=_sds((n_cat, d), BF16), ca=0, cb=0, nk=nkt)
    parts = []
    for j in range(N_DEV):
        cols = []
        for lo, hi, start in sorted(segments):
            a, b = max(lo, j * n_in), min(hi, (j + 1) * n_in)
            if a < b:
                cols.append(g_cat[start + a - lo:start + b - lo])
        parts.append(cols[0] if len(cols) == 1 else jnp.concatenate(cols, axis=0))
    g_in_parts = jnp.stack(parts)
    g_conv = jnp.concatenate([gcw_q, gcw_k, gcw_v], axis=1)
    n_cw = conv_w.shape[2]
    g_conv_parts = jnp.transpose(g_conv.reshape(8, N_DEV, n_cw), (1, 0, 2))
    in_g_fly = _exchange_start("reduce_in_start", [g_in_parts, g_conv_parts], [True] * 2, g_dtb_row)
    tmh1 = min(512, t)
    dh1 = _mm("d_h1", dproj, w_cat, dep=in_g_fly["token"], grid=(t // tmh1, d // 1024, 1),
              a_spec=pl.BlockSpec((tmh1, n_cat), lambda i, j, k: (i, 0)),
              b_spec=pl.BlockSpec((1024, n_cat), lambda i, j, k: (j, 0)),
              o_spec=pl.BlockSpec((tmh1, 1024), lambda i, j, k: (i, j)),
              out_shape=_sds((t, d), F32), ca=1, cb=1, nk=1)
    grad_x, _, g_attn_norm = _rms_bwd("norm1_bwd", x2, attn_norm_w, dh1, dx1)

    small_rows = [g_attn_norm.reshape(d // HD, HD), g_ffn_norm.reshape(d // HD, HD), g_dn, g_qn, g_kn, g_an,
                  g_alog_row, g_dtb_row, loss_tile[:1]]
    loss_row = sum(r.shape[0] for r in small_rows) - 1
    small_pack = _pad_rows(jnp.concatenate(small_rows, axis=0), 40)
    (r_down,) = _exchange_wait("reduce_down_wait", down_g_fly, grad_x)
    (r_gu,) = _exchange_wait("reduce_gate_up_wait", gu_g_fly, grad_x)
    (r_out,) = _exchange_wait("reduce_out_wait", out_g_fly, grad_x)
    res_gu = [a[None] for a in _adamw("adamw_gate_up", r_gu, w_gate_up[0], m_w_gate_up[0], v_w_gate_up[0])]
    res_down = [a[None] for a in _adamw("adamw_down", r_down, w_down[0], m_w_down[0], v_w_down[0])]
    res_out = [a[None] for a in _adamw("adamw_out", r_out, w_out[0], m_w_out[0], v_w_out[0])]
    done = (res_gu[3][0, :1, :1] + res_down[3][0, :1, :1] + res_out[3][0, :1, :1])
    r_in, r_conv = _exchange_wait("reduce_in_wait", in_g_fly, done)
    upd_in = _adamw("adamw_in", r_in, jnp.transpose(w_in[0]), jnp.transpose(m_w_in[0]), jnp.transpose(v_w_in[0]))
    res_in = [jnp.transpose(a)[None] for a in upd_in]
    (r_small,) = _exchange("gather_small_grads", [small_pack], [False], upd_in[0])

    def pack_small(an, fn, dn, qn_, kn_, aon, al, db):
        rows = [an.reshape(d // HD, HD), fn.reshape(d // HD, HD), dn, qn_, kn_, aon,
                _lane_row(al[0], 8), _lane_row(db[0], 8)]
        return _pad_rows(jnp.concatenate(rows, axis=0), 40)

    def unpack_small(pk):
        nr = d // HD
        return dict(attn_norm_w=pk[:nr].reshape(1, d), ffn_norm_w=pk[nr:2 * nr].reshape(1, d),
                    delta_out_norm_w=pk[2 * nr:2 * nr + 1], q_norm_w=pk[2 * nr + 1:2 * nr + 2],
                    k_norm_w=pk[2 * nr + 2:2 * nr + 3], attn_out_norm_w=pk[2 * nr + 3:2 * nr + 4],
                    a_log=pk[2 * nr + 4:2 * nr + 5, 8:16], dt_bias=pk[2 * nr + 5:2 * nr + 6, 8:16])

    res_small = _adamw("adamw_small", r_small,
                       pack_small(attn_norm_w, ffn_norm_w, delta_out_norm_w, q_norm_w, k_norm_w, attn_out_norm_w, a_log, dt_bias),
                       pack_small(m_attn_norm_w, m_ffn_norm_w, m_delta_out_norm_w, m_q_norm_w, m_k_norm_w, m_attn_out_norm_w, m_a_log, m_dt_bias),
                       pack_small(v_attn_norm_w, v_ffn_norm_w, v_delta_out_norm_w, v_q_norm_w, v_k_norm_w, v_attn_out_norm_w, v_a_log, v_dt_bias))
    small = [unpack_small(a) for a in res_small]
    res_conv =[a[None, :4] for a in _adamw("adamw_conv", r_conv, _pad_rows(conv_w[0], 8), _pad_rows(m_conv_w[0], 8),
                                            _pad_rows(v_conv_w[0], 8))]

    loss = jnp.sum(r_small[:, loss_row, 0])
    outs = [loss, grad_x[None]]
    for i in range(4):
        s = small[i]
        outs += [s["attn_norm_w"], res_in[i], res_conv[i], s["a_log"], s["dt_bias"], s["delta_out_norm_w"],
                 s["q_norm_w"], s["k_norm_w"], s["attn_out_norm_w"], res_out[i], s["ffn_norm_w"], res_gu[i],
                 res_down[i]]
    return tuple(outs)
```
